```python
import jax, jax.numpy as jnp
from jax import lax
import numpy as np

D_MODEL = 1024
BATCH = 8
SEQ = 4096
DEPTH = 4

EPS = 1e-6
N_BRANCH = 4
BRANCH_WIDTH = D_MODEL // 2
GM_HEADS = 4
GM_CHUNK = 128
POOL_WINDOWS = (2, 4, 8, 16)
POOL_GROUP = BRANCH_WIDTH // len(POOL_WINDOWS)
ATT_HEADS = 4
ATT_HEAD_DIM = BRANCH_WIDTH // ATT_HEADS
DIL_PATTERNS = ((128, 1), (512, 4), (2048, 16))
N_DIL = len(DIL_PATTERNS)
ATT_BLOCK = 128
MEM_LEN = 256
MEM_HEADS = 4
MEM_HEAD_DIM = BRANCH_WIDTH // MEM_HEADS
NEG = -1e30

IN_SIZES = (2 * BRANCH_WIDTH, BRANCH_WIDTH,
            BRANCH_WIDTH, BRANCH_WIDTH,
            N_DIL * BRANCH_WIDTH, BRANCH_WIDTH, BRANCH_WIDTH, BRANCH_WIDTH,
            BRANCH_WIDTH, BRANCH_WIDTH,
            N_BRANCH * D_MODEL)
D_IN = sum(IN_SIZES)

kernel_name = "hybrid_gmlp_pool_dilated_attn_mem"


def rms_norm(x, g):
    xf = x.astype(jnp.float32)
    y = xf * lax.rsqrt(jnp.mean(xf * xf, axis=-1, keepdims=True) + EPS)
    return (y * g.astype(jnp.float32)).astype(x.dtype)


def layer_norm(x, g, b):
    xf = x.astype(jnp.float32)
    mu = jnp.mean(xf, axis=-1, keepdims=True)
    var = jnp.mean(jnp.square(xf - mu), axis=-1, keepdims=True)
    y = (xf - mu) * lax.rsqrt(var + EPS)
    return (y * g.astype(jnp.float32) + b.astype(jnp.float32)).astype(x.dtype)


def split_cols(h):
    idx = np.cumsum(np.array(IN_SIZES))[:-1].tolist()
    return jnp.split(h, idx, axis=-1)


def gmlp_spatial_gating(uv, ln_g, ln_b, w_s, b_s):
    u, v = jnp.split(jax.nn.gelu(uv, approximate=False), 2, axis=-1)
    v = layer_norm(v, ln_g, ln_b)
    B_, S_, W = v.shape
    nc = S_ // GM_CHUNK
    vh = v.reshape(B_, nc, GM_CHUNK, GM_HEADS, W // GM_HEADS)
    causal = jnp.tril(jnp.ones((GM_CHUNK, GM_CHUNK), dtype=bool))
    w = jnp.where(causal[None], w_s, jnp.zeros_like(w_s)).astype(v.dtype)
    mixed = jnp.einsum('hts,bcshe->bcthe', w, vh) + b_s.T.astype(v.dtype)[None, None, :, :, None]
    return u * mixed.reshape(B_, S_, W)


def multiscale_pool(p, pool_w, pool_scale):
    B_, S_, W = p.shape
    pf = p.astype(jnp.float32)
    cs = jnp.cumsum(pf, axis=1)
    count = jnp.arange(1, S_ + 1, dtype=jnp.float32)
    outs = []
    for g, win in enumerate(POOL_WINDOWS):
        c = cs[..., g * POOL_GROUP:(g + 1) * POOL_GROUP]
        prev = jnp.pad(c, ((0, 0), (win, 0), (0, 0)))[:, :S_]
        mean = (c - prev) / jnp.minimum(count, float(win))[None, :, None]
        outs.append(mean - pf[..., g * POOL_GROUP:(g + 1) * POOL_GROUP])
    d = jnp.stack(outs, axis=2)
    y = jnp.einsum('bsgi,gio->bsgo', d, pool_w.astype(jnp.float32)).reshape(B_, S_, W)
    return (y * pool_scale.astype(jnp.float32)).astype(p.dtype)


def dilated_window_attention(q, k, v, window, dilation):
    B_, S_, H, E = q.shape
    n_back = window // dilation
    L = S_ // dilation
    nb = -(-L // ATT_BLOCK)
    Lp = nb * ATT_BLOCK

    def to_blocks(t):
        t = t.reshape(B_, L, dilation, H, E).transpose(0, 2, 1, 3, 4)
        t = jnp.pad(t, ((0, 0), (0, 0), (0, Lp - L), (0, 0), (0, 0)))
        return t.reshape(B_, dilation, nb, ATT_BLOCK, H, E)

    def with_prev(t):
        prev = jnp.pad(t, ((0, 0), (0, 0), (1, 0), (0, 0), (0, 0), (0, 0)))[:, :, :nb]
        return jnp.concatenate([prev, t], axis=3)

    qb = to_blocks(q).astype(jnp.float32)
    kk = with_prev(to_blocks(k)).astype(jnp.float32)
    vv = with_prev(to_blocks(v)).astype(jnp.float32)
    s = jnp.einsum('brnqhe,brnkhe->brnhqk', qb, kk) * (E ** -0.5)
    i = jnp.arange(ATT_BLOCK)[:, None]
    j = jnp.arange(2 * ATT_BLOCK)[None, :]
    dist = ATT_BLOCK + i - j
    band = (dist >= 0) & (dist <= n_back)
    key_exists = (jnp.arange(nb)[:, None] > 0) | (jnp.arange(2 * ATT_BLOCK)[None, :] >= ATT_BLOCK)
    valid = band[None] & key_exists[:, None, :]
    s = jnp.where(valid[None, None, :, None], s, NEG)
    m = jnp.max(s, axis=-1, keepdims=True)
    e = jnp.exp(s - m)
    den = jnp.sum(e, axis=-1, keepdims=True)
    o = jnp.einsum('brnhqk,brnkhe->brnqhe', e / den, vv)
    lse = (m + jnp.log(den))[..., 0]
    o = o.reshape(B_, dilation, Lp, H, E)[:, :, :L].transpose(0, 2, 1, 3, 4).reshape(B_, S_, H, E)
    lse = lse.transpose(0, 1, 2, 4, 3).reshape(B_, dilation, Lp, H)[:, :, :L]
    lse = lse.transpose(0, 2, 1, 3).reshape(B_, S_, H)
    return o, lse


def dilated_mixture(c_q, c_k, c_v):
    B_, S_, _ = c_q.shape
    qg = c_q.reshape(B_, S_, N_DIL, ATT_HEADS, ATT_HEAD_DIM)
    k = c_k.reshape(B_, S_, ATT_HEADS, ATT_HEAD_DIM)
    v = c_v.reshape(B_, S_, ATT_HEADS, ATT_HEAD_DIM)
    outs, lses = [], []
    for g, (win, dil) in enumerate(DIL_PATTERNS):
        o, l = dilated_window_attention(qg[:, :, g], k, v, win, dil)
        outs.append(o)
        lses.append(l)
    alpha = jax.nn.softmax(jnp.stack(lses, axis=0), axis=0)
    o = jnp.sum(alpha[..., None] * jnp.stack(outs, axis=0), axis=0)
    return o.reshape(B_, S_, BRANCH_WIDTH).astype(c_q.dtype)


def memory_attention(m_q, mem_n, w_kv):
    B_, S_, _ = m_q.shape
    q = m_q.reshape(B_, S_, MEM_HEADS, MEM_HEAD_DIM).astype(jnp.float32)
    k, v = jnp.split(mem_n @ w_kv, 2, axis=-1)
    k = k.reshape(B_, -1, MEM_HEADS, MEM_HEAD_DIM).astype(jnp.float32)
    v = v.reshape(B_, -1, MEM_HEADS, MEM_HEAD_DIM).astype(jnp.float32)
    s = jnp.einsum('bshe,bmhe->bhsm', q, k) * (MEM_HEAD_DIM ** -0.5)
    p = jax.nn.softmax(s, axis=-1)
    o = jnp.einsum('bhsm,bmhe->bshe', p, v)
    return o.reshape(B_, S_, BRANCH_WIDTH).astype(m_q.dtype)


def _fwd_setup_inputs(seed: int = 0) -> dict:
    key = jax.random.key(seed)
    ks = jax.random.split(key, 20)
    f32 = jnp.float32
    nrm = lambda k, shape, s: jax.random.normal(k, shape, f32) * s
    return {
        "x": nrm(ks[0], (BATCH, SEQ, D_MODEL), 1.0),
        "mem": nrm(ks[1], (BATCH, MEM_LEN, D_MODEL), 1.0),
        "norm_g": 1.0 + nrm(ks[2], (DEPTH, D_MODEL), 0.02),
        "w_in": nrm(ks[3], (DEPTH, D_MODEL, D_IN), D_MODEL ** -0.5),
        "gm_ln_g": 1.0 + nrm(ks[4], (DEPTH, BRANCH_WIDTH), 0.02),
        "gm_ln_b": nrm(ks[5], (DEPTH, BRANCH_WIDTH), 0.02),
        "gm_ws": nrm(ks[6], (DEPTH, GM_HEADS, GM_CHUNK, GM_CHUNK), GM_CHUNK ** -0.5),
        "gm_bs": 1.0 + nrm(ks[7], (DEPTH, GM_HEADS, GM_CHUNK), 0.1),
        "pool_w": nrm(ks[8], (DEPTH, len(POOL_WINDOWS), POOL_GROUP, POOL_GROUP), POOL_GROUP ** -0.5),
        "pool_scale": 1.0 + nrm(ks[9], (DEPTH, BRANCH_WIDTH), 0.1),
        "mem_norm_g": 1.0 + nrm(ks[10], (DEPTH, D_MODEL), 0.02),
        "w_mem_kv": nrm(ks[11], (DEPTH, D_MODEL, 2 * BRANCH_WIDTH), D_MODEL ** -0.5),
        "w_branch": nrm(ks[12], (DEPTH, N_BRANCH, BRANCH_WIDTH, D_MODEL), BRANCH_WIDTH ** -0.5),
        "w_out": nrm(ks[13], (DEPTH, D_MODEL, D_MODEL), 0.5 * D_MODEL ** -0.5),
        "final_norm_g": 1.0 + nrm(ks[14], (D_MODEL,), 0.02),
    }


def _fwd_reference(x, mem, norm_g, w_in, gm_ln_g, gm_ln_b, gm_ws, gm_bs, pool_w, pool_scale,
              mem_norm_g, w_mem_kv, w_branch, w_out, final_norm_g):
    B_, S_, D = x.shape
    for l in range(DEPTH):
        h = rms_norm(x, norm_g[l])
        proj = h @ w_in[l]
        (a_uv, a_gate, p_in, p_gate, c_q, c_k, c_v, c_gate,
         m_q, m_gate, g_merge) = split_cols(proj)
        y_a = gmlp_spatial_gating(a_uv, gm_ln_g[l], gm_ln_b[l], gm_ws[l], gm_bs[l]) * jax.nn.silu(a_gate)
        y_p = multiscale_pool(p_in, pool_w[l], pool_scale[l]) * jax.nn.silu(p_gate)
        y_c = dilated_mixture(c_q, c_k, c_v) * jax.nn.silu(c_gate)
        mem_n = rms_norm(mem, mem_norm_g[l])
        y_m = memory_attention(m_q, mem_n, w_mem_kv[l]) * jax.nn.silu(m_gate)
        gates = jax.nn.sigmoid(g_merge.reshape(B_, S_, N_BRANCH, D))
        z = (gates[:, :, 0] * (y_a @ w_branch[l, 0])
             + gates[:, :, 1] * (y_p @ w_branch[l, 1])
             + gates[:, :, 2] * (y_c @ w_branch[l, 2])
             + gates[:, :, 3] * (y_m @ w_branch[l, 3]))
        x = x + z @ w_out[l]
    return rms_norm(x, final_norm_g)


import jax as _jax
import jax.numpy as _jnp

TWIN_FORMAT = 'train_step'
FWD_PARAMS = ['x', 'mem', 'norm_g', 'w_in', 'gm_ln_g', 'gm_ln_b', 'gm_ws', 'gm_bs', 'pool_w', 'pool_scale', 'mem_norm_g', 'w_mem_kv', 'w_branch', 'w_out', 'final_norm_g']
TWIN_WEIGHTS = ['norm_g', 'w_in', 'gm_ln_g', 'gm_ln_b', 'gm_ws', 'gm_bs', 'pool_w', 'pool_scale', 'mem_norm_g', 'w_mem_kv', 'w_branch', 'w_out', 'final_norm_g']
TWIN_DIFF_INPUT = 'x'
TWIN_INPUTS = ['x', 'mem', 'norm_g', 'w_in', 'gm_ln_g', 'gm_ln_b', 'gm_ws', 'gm_bs', 'pool_w', 'pool_scale', 'mem_norm_g', 'w_mem_kv', 'w_branch', 'w_out', 'final_norm_g', 'loss_target', 'm_norm_g', 'm_w_in', 'm_gm_ln_g', 'm_gm_ln_b', 'm_gm_ws', 'm_gm_bs', 'm_pool_w', 'm_pool_scale', 'm_mem_norm_g', 'm_w_mem_kv', 'm_w_branch', 'm_w_out', 'm_final_norm_g', 'v_norm_g', 'v_w_in', 'v_gm_ln_g', 'v_gm_ln_b', 'v_gm_ws', 'v_gm_bs', 'v_pool_w', 'v_pool_scale', 'v_mem_norm_g', 'v_w_mem_kv', 'v_w_branch', 'v_w_out', 'v_final_norm_g']
TWIN_OUTPUTS = ['loss', 'grad_x', 'grad_norm_g', 'grad_w_in', 'grad_gm_ln_g', 'grad_gm_ln_b', 'grad_gm_ws', 'grad_gm_bs', 'grad_pool_w', 'grad_pool_scale', 'grad_mem_norm_g', 'grad_w_mem_kv', 'grad_w_branch', 'grad_w_out', 'grad_final_norm_g', 'delta_norm_g', 'delta_w_in', 'delta_gm_ln_g', 'delta_gm_ln_b', 'delta_gm_ws', 'delta_gm_bs', 'delta_pool_w', 'delta_pool_scale', 'delta_mem_norm_g', 'delta_w_mem_kv', 'delta_w_branch', 'delta_w_out', 'delta_final_norm_g', 'new_m_norm_g', 'new_m_w_in', 'new_m_gm_ln_g', 'new_m_gm_ln_b', 'new_m_gm_ws', 'new_m_gm_bs', 'new_m_pool_w', 'new_m_pool_scale', 'new_m_mem_norm_g', 'new_m_w_mem_kv', 'new_m_w_branch', 'new_m_w_out', 'new_m_final_norm_g', 'new_v_norm_g', 'new_v_w_in', 'new_v_gm_ln_g', 'new_v_gm_ln_b', 'new_v_gm_ws', 'new_v_gm_bs', 'new_v_pool_w', 'new_v_pool_scale', 'new_v_mem_norm_g', 'new_v_w_mem_kv', 'new_v_w_branch', 'new_v_w_out', 'new_v_final_norm_g']
TWIN_LEAF_KINDS = {'loss': 'loss', 'grad_x': 'grad_x', 'grad_norm_g': 'grad_w', 'grad_w_in': 'grad_w', 'grad_gm_ln_g': 'grad_w', 'grad_gm_ln_b': 'grad_w', 'grad_gm_ws': 'grad_w', 'grad_gm_bs': 'grad_w', 'grad_pool_w': 'grad_w', 'grad_pool_scale': 'grad_w', 'grad_mem_norm_g': 'grad_w', 'grad_w_mem_kv': 'grad_w', 'grad_w_branch': 'grad_w', 'grad_w_out': 'grad_w', 'grad_final_norm_g': 'grad_w', 'delta_norm_g': 'delta_w', 'delta_w_in': 'delta_w', 'delta_gm_ln_g': 'delta_w', 'delta_gm_ln_b': 'delta_w', 'delta_gm_ws': 'delta_w', 'delta_gm_bs': 'delta_w', 'delta_pool_w': 'delta_w', 'delta_pool_scale': 'delta_w', 'delta_mem_norm_g': 'delta_w', 'delta_w_mem_kv': 'delta_w', 'delta_w_branch': 'delta_w', 'delta_w_out': 'delta_w', 'delta_final_norm_g': 'delta_w', 'new_m_norm_g': 'new_m', 'new_m_w_in': 'new_m', 'new_m_gm_ln_g': 'new_m', 'new_m_gm_ln_b': 'new_m', 'new_m_gm_ws': 'new_m', 'new_m_gm_bs': 'new_m', 'new_m_pool_w': 'new_m', 'new_m_pool_scale': 'new_m', 'new_m_mem_norm_g': 'new_m', 'new_m_w_mem_kv': 'new_m', 'new_m_w_branch': 'new_m', 'new_m_w_out': 'new_m', 'new_m_final_norm_g': 'new_m', 'new_v_norm_g': 'new_v', 'new_v_w_in': 'new_v', 'new_v_gm_ln_g': 'new_v', 'new_v_gm_ln_b': 'new_v', 'new_v_gm_ws': 'new_v', 'new_v_gm_bs': 'new_v', 'new_v_pool_w': 'new_v', 'new_v_pool_scale': 'new_v', 'new_v_mem_norm_g': 'new_v', 'new_v_w_mem_kv': 'new_v', 'new_v_w_branch': 'new_v', 'new_v_w_out': 'new_v', 'new_v_final_norm_g': 'new_v'}


def _forward(args):
    return _fwd_reference(*[args[k] for k in FWD_PARAMS])


def _output_shape():
    def fwd():
        inp = _fwd_setup_inputs(0)
        return _fwd_reference(*[inp[k] for k in FWD_PARAMS])
    out = _jax.eval_shape(fwd)
    return out.shape, out.dtype

N_MICROBATCH = 1
ADAM_LR = 0.001
ADAM_B1 = 0.9
ADAM_B2 = 0.999
ADAM_EPS = 1e-08
ADAM_WD = 0.01
ADAM_STEP = 10
PER_EXAMPLE_BATCH_AXIS = {'x': 0, 'mem': 0, 'loss_target': 0}
SHARED_INPUTS = []
_WEIGHT_DTYPES = {'norm_g': _jnp.float32, 'w_in': _jnp.float32, 'gm_ln_g': _jnp.float32, 'gm_ln_b': _jnp.float32, 'gm_ws': _jnp.float32, 'gm_bs': _jnp.float32, 'pool_w': _jnp.float32, 'pool_scale': _jnp.float32, 'mem_norm_g': _jnp.float32, 'w_mem_kv': _jnp.float32, 'w_branch': _jnp.float32, 'w_out': _jnp.float32, 'final_norm_g': _jnp.float32}
MOMENT_SCALE = {'norm_g': 5.411046e-02, 'w_in': 1.672314e-02, 'gm_ln_g': 1.855134e-02, 'gm_ln_b': 1.894680e-02, 'gm_ws': 1.868963e-02, 'gm_bs': 2.639605e-02, 'pool_w': 3.532342e-02, 'pool_scale': 3.527986e-02, 'mem_norm_g': 4.392786e-03, 'w_mem_kv': 4.244768e-03, 'w_branch': 1.718844e-02, 'w_out': 6.867106e-02, 'final_norm_g': 3.200878e+01}


def _to_microbatches(a, axis):
    t = _jnp.moveaxis(a, axis, 0)
    t = t.reshape((N_MICROBATCH, t.shape[0] // N_MICROBATCH) + t.shape[1:])
    return _jnp.moveaxis(t, 1, axis + 1)


def setup_inputs(seed: int = 0) -> dict:
    inp = _fwd_setup_inputs(seed)
    key = _jax.random.fold_in(_jax.random.key(seed), 7919)
    shape, _ = _output_shape()
    out = dict(inp)
    out["loss_target"] = _jax.random.normal(_jax.random.fold_in(key, 0), shape, _jnp.float32)
    for i, name in enumerate(TWIN_WEIGHTS):
        w = inp[name].astype(_jnp.float32)
        if MOMENT_SCALE is None:
            s = _jnp.sqrt(_jnp.mean(_jnp.square(w)) + 1e-30)
        else:
            s = MOMENT_SCALE[name]
        km, kv = _jax.random.split(_jax.random.fold_in(key, i + 1))
        out[name] = w
        out["m_" + name] = s * _jax.random.normal(km, w.shape, _jnp.float32)
        out["v_" + name] = (s * s) * _jax.random.uniform(kv, w.shape, _jnp.float32, 0.5, 1.5)
    if N_MICROBATCH > 1:
        for name, axis in PER_EXAMPLE_BATCH_AXIS.items():
            out[name] = _to_microbatches(out[name], axis)
    return {'x': out['x'], 'mem': out['mem'], 'norm_g': out['norm_g'], 'w_in': out['w_in'], 'gm_ln_g': out['gm_ln_g'], 'gm_ln_b': out['gm_ln_b'], 'gm_ws': out['gm_ws'], 'gm_bs': out['gm_bs'], 'pool_w': out['pool_w'], 'pool_scale': out['pool_scale'], 'mem_norm_g': out['mem_norm_g'], 'w_mem_kv': out['w_mem_kv'], 'w_branch': out['w_branch'], 'w_out': out['w_out'], 'final_norm_g': out['final_norm_g'], 'loss_target': out['loss_target'], 'm_norm_g': out['m_norm_g'], 'm_w_in': out['m_w_in'], 'm_gm_ln_g': out['m_gm_ln_g'], 'm_gm_ln_b': out['m_gm_ln_b'], 'm_gm_ws': out['m_gm_ws'], 'm_gm_bs': out['m_gm_bs'], 'm_pool_w': out['m_pool_w'], 'm_pool_scale': out['m_pool_scale'], 'm_mem_norm_g': out['m_mem_norm_g'], 'm_w_mem_kv': out['m_w_mem_kv'], 'm_w_branch': out['m_w_branch'], 'm_w_out': out['m_w_out'], 'm_final_norm_g': out['m_final_norm_g'], 'v_norm_g': out['v_norm_g'], 'v_w_in': out['v_w_in'], 'v_gm_ln_g': out['v_gm_ln_g'], 'v_gm_ln_b': out['v_gm_ln_b'], 'v_gm_ws': out['v_gm_ws'], 'v_gm_bs': out['v_gm_bs'], 'v_pool_w': out['v_pool_w'], 'v_pool_scale': out['v_pool_scale'], 'v_mem_norm_g': out['v_mem_norm_g'], 'v_w_mem_kv': out['v_w_mem_kv'], 'v_w_branch': out['v_w_branch'], 'v_w_out': out['v_w_out'], 'v_final_norm_g': out['v_final_norm_g']}


def _loss(weights, diff, rest, loss_target):
    with _jax.named_scope("forward"):
        args = {**rest, TWIN_DIFF_INPUT: diff, **{k: w.astype(_WEIGHT_DTYPES[k]) for k, w in weights.items()}}
        y = _forward(args)
    with _jax.named_scope("loss_head"):
        err = _jnp.square(y.astype(_jnp.float32) - loss_target)
        return 0.5 * _jnp.sum(_jnp.mean(err, axis=-1)) if err.ndim else 0.5 * err


def _adamw(w, g, m, v):
    m = ADAM_B1 * m + (1.0 - ADAM_B1) * g
    v = ADAM_B2 * v + (1.0 - ADAM_B2) * _jnp.square(g)
    m_hat = m / (1.0 - ADAM_B1 ** ADAM_STEP)
    v_hat = v / (1.0 - ADAM_B2 ** ADAM_STEP)
    delta = -ADAM_LR * (m_hat / (_jnp.sqrt(v_hat) + ADAM_EPS) + ADAM_WD * w)
    return delta, m, v


def reference(x, mem, norm_g, w_in, gm_ln_g, gm_ln_b, gm_ws, gm_bs, pool_w, pool_scale, mem_norm_g, w_mem_kv, w_branch, w_out, final_norm_g, loss_target, m_norm_g, m_w_in, m_gm_ln_g, m_gm_ln_b, m_gm_ws, m_gm_bs, m_pool_w, m_pool_scale, m_mem_norm_g, m_w_mem_kv, m_w_branch, m_w_out, m_final_norm_g, v_norm_g, v_w_in, v_gm_ln_g, v_gm_ln_b, v_gm_ws, v_gm_bs, v_pool_w, v_pool_scale, v_mem_norm_g, v_w_mem_kv, v_w_branch, v_w_out, v_final_norm_g):
    given = dict(x=x, mem=mem, norm_g=norm_g, w_in=w_in, gm_ln_g=gm_ln_g, gm_ln_b=gm_ln_b, gm_ws=gm_ws, gm_bs=gm_bs, pool_w=pool_w, pool_scale=pool_scale, mem_norm_g=mem_norm_g, w_mem_kv=w_mem_kv, w_branch=w_branch, w_out=w_out, final_norm_g=final_norm_g, loss_target=loss_target, m_norm_g=m_norm_g, m_w_in=m_w_in, m_gm_ln_g=m_gm_ln_g, m_gm_ln_b=m_gm_ln_b, m_gm_ws=m_gm_ws, m_gm_bs=m_gm_bs, m_pool_w=m_pool_w, m_pool_scale=m_pool_scale, m_mem_norm_g=m_mem_norm_g, m_w_mem_kv=m_w_mem_kv, m_w_branch=m_w_branch, m_w_out=m_w_out, m_final_norm_g=m_final_norm_g, v_norm_g=v_norm_g, v_w_in=v_w_in, v_gm_ln_g=v_gm_ln_g, v_gm_ln_b=v_gm_ln_b, v_gm_ws=v_gm_ws, v_gm_bs=v_gm_bs, v_pool_w=v_pool_w, v_pool_scale=v_pool_scale, v_mem_norm_g=v_mem_norm_g, v_w_mem_kv=v_w_mem_kv, v_w_branch=v_w_branch, v_w_out=v_w_out, v_final_norm_g=v_final_norm_g)
    weights = {n: given[n] for n in TWIN_WEIGHTS}
    shared = {n: given[n] for n in SHARED_INPUTS}
    per_example = {n: given[n] for n in ['x', 'mem']}
    grad_fn = _jax.value_and_grad(_loss, argnums=(0, 1))

    def one_microbatch(ex, loss_target):
        ex = dict(ex)
        diff = ex.pop(TWIN_DIFF_INPUT)
        return grad_fn(weights, diff, {**shared, **ex}, loss_target)

    if N_MICROBATCH == 1:
        loss, (grad_w, grad_x) = one_microbatch(per_example, given["loss_target"])
    else:
        def body(carry, xs):
            loss_sum, grad_sum = carry
            l_k, (gw_k, gx_k) = one_microbatch(xs[0], xs[1])
            with _jax.named_scope("update"):
                return (loss_sum + l_k, _jax.tree.map(_jnp.add, grad_sum, gw_k)), gx_k

        init = (_jnp.zeros((), _jnp.float32), _jax.tree.map(_jnp.zeros_like, weights))
        (loss, grad_w), grad_x = _jax.lax.scan(body, init, (per_example, given["loss_target"]))
    with _jax.named_scope("update"):
        delta_w, new_m, new_v = {}, {}, {}
        for n in TWIN_WEIGHTS:
            delta_w[n], new_m[n], new_v[n] = _adamw(weights[n], grad_w[n], given["m_" + n], given["v_" + n])
    return (loss, grad_x, *[grad_w[n] for n in TWIN_WEIGHTS], *[delta_w[n] for n in TWIN_WEIGHTS],
            *[new_m[n] for n in TWIN_WEIGHTS], *[new_v[n] for n in TWIN_WEIGHTS])
```

```python
import functools
import math

import jax
import jax.numpy as jnp
from jax import lax
from jax.experimental import pallas as pl
from jax.experimental.pallas import tpu as pltpu

F32 = jnp.float32
BF16 = jnp.bfloat16

D_MODEL = 1024
DEPTH = 4
WIDTH = 512
D_IN = 10752
HEAD = 128
N_HEAD = 4
CHUNK = 128
MEM_LEN = 256
POOL_WINDOWS = (2, 4, 8, 16)
DILATIONS = (1, 4, 16)
EPS = 1e-6
NEG = -1e30
ATT_SCALE = HEAD ** -0.5
N_DEV = 8
N_CHIP = 4

CB_U, CB_V, CB_AGATE, CB_PIN, CB_PGATE = 0, 1, 2, 3, 4
CB_Q0, CB_K, CB_CV, CB_CGATE, CB_MQ, CB_MGATE, CB_MERGE = 5, 8, 9, 10, 11, 12, 13

ADAM_LR = 0.001
ADAM_B1 = 0.9
ADAM_B2 = 0.999
ADAM_EPS = 1e-08
ADAM_WD = 0.01
ADAM_STEP = 10

VMEM_LIMIT = 56 * 1024 * 1024
MESH = pl.DeviceIdType.MESH
ANY = pl.BlockSpec(memory_space=pl.ANY)

NT = (((1,), (1,)), ((), ()))
TN = (((0,), (0,)), ((), ()))


def _dot(a, b):
    return jnp.dot(a, b, preferred_element_type=F32)


def _dot_nt(a, b):
    return lax.dot_general(a, b, NT, preferred_element_type=F32)


def _dot_tn(a, b):
    return lax.dot_general(a, b, TN, preferred_element_type=F32)


def _sigmoid(x):
    return 1.0 / (1.0 + jnp.exp(-x))


def _silu(x):
    return x * _sigmoid(x)


def _dsilu(x):
    s = _sigmoid(x)
    return s * (1.0 + x * (1.0 - s))


def _gelu(x):
    return 0.5 * x * (1.0 + lax.erf(x * (2.0 ** -0.5)))


def _dgelu(x):
    return 0.5 * (1.0 + lax.erf(x * (2.0 ** -0.5))) + x * jnp.exp(-0.5 * x * x) * (1.0 / math.sqrt(2.0 * math.pi))


def _col(blk, h):
    lane = lax.broadcasted_iota(jnp.int32, blk.shape, 1)
    return jnp.sum(jnp.where(lane == h, blk, 0.0), axis=1, keepdims=True)


def _put_cols(cols):
    rows = cols[0].shape[0]
    lane = lax.broadcasted_iota(jnp.int32, (rows, 128), 1)
    out = jnp.zeros((rows, 128), F32)
    for h, cv in enumerate(cols):
        out = jnp.where(lane == h, cv, out)
    return out


def _params(sem, vmem=VMEM_LIMIT):
    return pltpu.CompilerParams(dimension_semantics=sem, vmem_limit_bytes=vmem)


def _full(shape):
    nd = len(shape)
    return pl.BlockSpec(shape, lambda *_: (0,) * nd)


def _rows(tm, width, cb=0):
    return pl.BlockSpec((tm, width), lambda i: (i, cb))


def _in_proj(x, g, w):
    S = x.shape[0]
    tm, tn = 1024, 1536

    def body(x_ref, g_ref, w_ref, proj_ref, h_ref, hs):
        @pl.when(pl.program_id(1) == 0)
        def _():
            xf = x_ref[...]
            r = lax.rsqrt(jnp.mean(xf * xf, axis=-1, keepdims=True) + EPS)
            h = (xf * r * g_ref[...]).astype(BF16)
            hs[...] = h
            h_ref[...] = h

        proj_ref[...] = _dot(hs[...], w_ref[...]).astype(BF16)

    return pl.pallas_call(
        body, name="in_proj",
        grid=(S // tm, D_IN // tn),
        in_specs=[pl.BlockSpec((tm, D_MODEL), lambda i, j: (i, 0)),
                  pl.BlockSpec((1, D_MODEL), lambda i, j: (0, 0)),
                  pl.BlockSpec((D_MODEL, tn), lambda i, j: (0, j))],
        out_specs=[pl.BlockSpec((tm, tn), lambda i, j: (i, j)),
                   pl.BlockSpec((tm, D_MODEL), lambda i, j: (i, 0))],
        out_shape=[jax.ShapeDtypeStruct((S, D_IN), BF16), jax.ShapeDtypeStruct((S, D_MODEL), BF16)],
        scratch_shapes=[pltpu.VMEM((tm, D_MODEL), BF16)],
        compiler_params=_params(("parallel", "arbitrary")),
    )(x, g, w)


def _mem_kv(mem, g, w):
    M = mem.shape[0]

    def body(m_ref, g_ref, w_ref, kv_ref, mn_ref):
        xf = m_ref[...]
        r = lax.rsqrt(jnp.mean(xf * xf, axis=-1, keepdims=True) + EPS)
        mn = (xf * r * g_ref[...]).astype(BF16)
        mn_ref[...] = mn
        kv_ref[...] = _dot(mn, w_ref[...]).astype(BF16)

    return pl.pallas_call(
        body, name="mem_kv",
        out_shape=[jax.ShapeDtypeStruct((M, 2 * WIDTH), BF16), jax.ShapeDtypeStruct((M, D_MODEL), BF16)],
        compiler_params=pltpu.CompilerParams(vmem_limit_bytes=VMEM_LIMIT),
    )(mem, g, w)


def _band_masks(win):
    t = lax.broadcasted_iota(jnp.int32, (CHUNK, CHUNK), 0)
    s = lax.broadcasted_iota(jnp.int32, (CHUNK, CHUNK), 1)
    cur = jnp.logical_and(t - s >= 0, t - s < win)
    prev = s > t + CHUNK - win
    return cur.astype(BF16), prev.astype(BF16)


def _inv_count(first_row, win):
    t = first_row + lax.broadcasted_iota(jnp.int32, (CHUNK, 1), 0)
    return 1.0 / jnp.minimum(t + 1, win).astype(F32)


def _layer_norm_fwd(v):
    mu = jnp.mean(v, axis=-1, keepdims=True)
    vc = v - mu
    var = jnp.mean(vc * vc, axis=-1, keepdims=True)
    rstd = lax.rsqrt(var + EPS)
    return vc * rstd, rstd


def _mem_softmax(q, kmem):
    s = _dot_nt(q, kmem) * ATT_SCALE
    m = jnp.max(s, axis=-1, keepdims=True)
    e = jnp.exp(s - m)
    return e * (1.0 / jnp.sum(e, axis=-1, keepdims=True))


def _abm_fwd(proj, ln_g, ln_b, wsm, bias_full, pool_w, pool_scale, kv):
    S = proj.shape[0]
    tm = 512
    nchunk = tm // CHUNK

    def body(u_ref, v_ref, ag_ref, p_ref, ph_ref, pg_ref, mq_ref, mg_ref, lng_ref, lnb_ref, wsm_ref, bias_ref,
             pw_ref, ps_ref, kv_ref, y_ref, mix):
        i = pl.program_id(0)
        u = _gelu(u_ref[...].astype(F32))
        v = _gelu(v_ref[...].astype(F32))
        vhat, _ = _layer_norm_fwd(v)
        vln = (vhat * lng_ref[...] + lnb_ref[...]).astype(BF16)
        for c in range(nchunk):
            for h in range(N_HEAD):
                rs, cs = slice(c * CHUNK, (c + 1) * CHUNK), slice(h * HEAD, (h + 1) * HEAD)
                mix[rs, cs] = _dot(wsm_ref[h], vln[rs, cs]) + bias_ref[:, cs]
        y_ref[0] = (u * mix[...] * _silu(ag_ref[...].astype(F32))).astype(BF16)
        halo_ok = (i > 0).astype(F32)
        for c in range(nchunk):
            rs = slice(c * CHUNK, (c + 1) * CHUNK)
            for g, win in enumerate(POOL_WINDOWS):
                cs = slice(g * HEAD, (g + 1) * HEAD)
                bcur, bprev = _band_masks(win)
                cur = p_ref[rs, cs]
                if c == 0:
                    prev = (ph_ref[:, cs].astype(F32) * halo_ok).astype(BF16)
                else:
                    prev = p_ref[(c - 1) * CHUNK:c * CHUNK, cs]
                sums = _dot(bcur, cur) + _dot(bprev, prev)
                dm = sums * _inv_count(i * tm + c * CHUNK, win) - cur.astype(F32)
                mix[rs, cs] = _dot(dm.astype(BF16), pw_ref[g])
        y_ref[1] = (mix[...] * ps_ref[...] * _silu(pg_ref[...].astype(F32))).astype(BF16)
        for h in range(N_HEAD):
            cs = slice(h * HEAD, (h + 1) * HEAD)
            p = _mem_softmax(mq_ref[:, cs], kv_ref[:, cs])
            mix[:, cs] = _dot(p.astype(BF16), kv_ref[:, WIDTH + h * HEAD:WIDTH + (h + 1) * HEAD])
        y_ref[2] = (mix[...] * _silu(mg_ref[...].astype(F32))).astype(BF16)

    blk = tm // CHUNK
    return pl.pallas_call(
        body, name="abm_fwd",
        grid=(S // tm,),
        in_specs=[_rows(tm, WIDTH, CB_U), _rows(tm, WIDTH, CB_V), _rows(tm, WIDTH, CB_AGATE),
                  _rows(tm, WIDTH, CB_PIN),
                  pl.BlockSpec((CHUNK, WIDTH), lambda i: (jnp.maximum(i * blk - 1, 0), CB_PIN)),
                  _rows(tm, WIDTH, CB_PGATE), _rows(tm, WIDTH, CB_MQ), _rows(tm, WIDTH, CB_MGATE),
                  _full((1, WIDTH)), _full((1, WIDTH)), _full((N_HEAD, CHUNK, CHUNK)), _full((CHUNK, WIDTH)),
                  _full((4, HEAD, HEAD)), _full((1, WIDTH)), _full((MEM_LEN, 2 * WIDTH))],
        out_specs=pl.BlockSpec((3, tm, WIDTH), lambda i: (0, i, 0)),
        out_shape=jax.ShapeDtypeStruct((3, S, WIDTH), BF16),
        scratch_shapes=[pltpu.VMEM((tm, WIDTH), F32)],
        compiler_params=_params(("parallel",)),
    )(proj, proj, proj, proj, proj, proj, proj, proj, ln_g, ln_b, wsm, bias_full, pool_w, pool_scale, kv)


ATT_TILE = 512


def _attn_fwd(q, qcb, k, kcb, v, vcb, bps):
    S = q.shape[0]
    tm = ATT_TILE
    nb = tm // CHUNK

    def body(q_ref, k_ref, v_ref, kh_ref, vh_ref, o_ref, l_ref):
        i = pl.program_id(0)
        row = lax.broadcasted_iota(jnp.int32, (CHUNK, CHUNK), 0)
        col = lax.broadcasted_iota(jnp.int32, (CHUNK, CHUNK), 1)
        m_cur = col <= row
        m_prev = col >= row
        for n in range(nb):
            rs = slice(n * CHUNK, (n + 1) * CHUNK)
            ps = slice((n - 1) * CHUNK, n * CHUNK)
            pen = jnp.where((i * nb + n) % bps != 0, 0.0, NEG)
            lcols = []
            for h in range(N_HEAD):
                cs = slice(h * HEAD, (h + 1) * HEAD)
                qh = q_ref[rs, cs]
                kp, vp = (kh_ref[:, cs], vh_ref[:, cs]) if n == 0 else (k_ref[ps, cs], v_ref[ps, cs])
                sc = jnp.where(m_cur, _dot_nt(qh, k_ref[rs, cs]) * ATT_SCALE, NEG)
                sp = jnp.where(m_prev, _dot_nt(qh, kp) * ATT_SCALE, NEG) + pen
                m = jnp.maximum(jnp.max(sc, axis=-1, keepdims=True), jnp.max(sp, axis=-1, keepdims=True))
                ec = jnp.exp(sc - m)
                ep = jnp.exp(sp - m)
                den = jnp.sum(ec, axis=-1, keepdims=True) + jnp.sum(ep, axis=-1, keepdims=True)
                inv = 1.0 / den
                o = _dot((ec * inv).astype(BF16), v_ref[rs, cs]) + _dot((ep * inv).astype(BF16), vp)
                o_ref[rs, cs] = o.astype(BF16)
                lcols.append(m + jnp.log(den))
            l_ref[rs, :] = _put_cols(lcols)

    def halo(cb):
        return pl.BlockSpec((CHUNK, WIDTH), lambda i: (jnp.maximum(i * nb - 1, 0), cb))

    return pl.pallas_call(
        body, name=f"attn_fwd_{bps}",
        grid=(S // tm,),
        in_specs=[_rows(tm, WIDTH, qcb), _rows(tm, WIDTH, kcb), _rows(tm, WIDTH, vcb), halo(kcb), halo(vcb)],
        out_specs=[_rows(tm, WIDTH), _rows(tm, 128)],
        out_shape=[jax.ShapeDtypeStruct((S, WIDTH), BF16), jax.ShapeDtypeStruct((S, 128), F32)],
        compiler_params=_params(("parallel",)),
    )(q, k, v, k, v)


def _merge_specs(tm):
    return [_rows(tm, WIDTH, CB_MERGE + j) for j in range(8)]


def _merge_fwd(x, y3, o_g, l_g, proj, wb, wout):
    S = x.shape[0]
    tm = 256

    def body(x_ref, y_ref, o0, o1, o2, l0, l1, l2, cg_ref, *rest):
        gm = rest[:8]
        wb_ref, wo_ref, xn_ref, yc_ref, oc_ref, lse_ref, z_ref, ocs = rest[8:]
        lcols = []
        for h in range(N_HEAD):
            cs = slice(h * HEAD, (h + 1) * HEAD)
            ls = [_col(l[...], h) for l in (l0, l1, l2)]
            m = jnp.maximum(jnp.maximum(ls[0], ls[1]), ls[2])
            tot = jnp.exp(ls[0] - m) + jnp.exp(ls[1] - m) + jnp.exp(ls[2] - m)
            lse = m + jnp.log(tot)
            ocs[:, cs] = sum(jnp.exp(lg - lse) * o[:, cs].astype(F32) for lg, o in zip(ls, (o0, o1, o2)))
            lcols.append(lse)
        lse_ref[...] = _put_cols(lcols)
        oc = ocs[...]
        oc_ref[...] = oc.astype(BF16)
        yc = (oc * _silu(cg_ref[...].astype(F32))).astype(BF16)
        yc_ref[...] = yc
        ys = (y_ref[0], y_ref[1], yc, y_ref[2])
        z = jnp.zeros((tm, D_MODEL), F32)
        for b in range(4):
            gate = _sigmoid(jnp.concatenate([gm[2 * b][...], gm[2 * b + 1][...]], axis=1).astype(F32))
            z = z + gate * _dot(ys[b], wb_ref[b])
        zb = z.astype(BF16)
        z_ref[...] = zb
        xn_ref[...] = x_ref[...] + _dot(zb, wo_ref[...])

    return pl.pallas_call(
        body, name="merge_fwd",
        grid=(S // tm,),
        in_specs=[_rows(tm, D_MODEL), pl.BlockSpec((3, tm, WIDTH), lambda i: (0, i, 0)),
                  _rows(tm, WIDTH), _rows(tm, WIDTH), _rows(tm, WIDTH),
                  _rows(tm, 128), _rows(tm, 128), _rows(tm, 128),
                  _rows(tm, WIDTH, CB_CGATE)] + _merge_specs(tm)
                 + [_full((4, WIDTH, D_MODEL)), _full((D_MODEL, D_MODEL))],
        out_specs=[_rows(tm, D_MODEL), _rows(tm, WIDTH), _rows(tm, WIDTH), _rows(tm, 128), _rows(tm, D_MODEL)],
        out_shape=[jax.ShapeDtypeStruct((S, D_MODEL), F32), jax.ShapeDtypeStruct((S, WIDTH), BF16),
                   jax.ShapeDtypeStruct((S, WIDTH), BF16), jax.ShapeDtypeStruct((S, 128), F32),
                   jax.ShapeDtypeStruct((S, D_MODEL), BF16)],
        scratch_shapes=[pltpu.VMEM((tm, WIDTH), F32)],
        compiler_params=_params(("parallel",)),
    )(x, y3, *o_g, *l_g, proj, *([proj] * 8), wb, wout)


def _loss_head(x, g, tgt):
    S = x.shape[0]
    tm = 512

    def body(x_ref, g_ref, t_ref, loss_ref, dx_ref, dg_ref):
        @pl.when(pl.program_id(0) == 0)
        def _():
            loss_ref[...] = jnp.zeros_like(loss_ref)
            dg_ref[...] = jnp.zeros_like(dg_ref)

        xf = x_ref[...]
        r = lax.rsqrt(jnp.mean(xf * xf, axis=-1, keepdims=True) + EPS)
        xhat = xf * r
        gv = g_ref[...]
        err = xhat * gv - t_ref[...]
        e2 = jnp.sum(err * err, axis=-1, keepdims=True)
        loss_ref[...] += (0.5 / D_MODEL) * jnp.sum(e2, axis=0, keepdims=True)
        dy = err * (1.0 / D_MODEL)
        dg_ref[...] += jnp.sum(dy * xhat, axis=0, keepdims=True)
        dxh = dy * gv
        dx_ref[...] = r * (dxh - xhat * jnp.mean(dxh * xhat, axis=-1, keepdims=True))

    return pl.pallas_call(
        body, name="loss_head",
        grid=(S // tm,),
        in_specs=[_rows(tm, D_MODEL), _full((1, D_MODEL)), _rows(tm, D_MODEL)],
        out_specs=[_full((1, 128)), _rows(tm, D_MODEL), _full((1, D_MODEL))],
        out_shape=[jax.ShapeDtypeStruct((1, 128), F32), jax.ShapeDtypeStruct((S, D_MODEL), F32),
                   jax.ShapeDtypeStruct((1, D_MODEL), F32)],
        compiler_params=_params(("arbitrary",)),
    )(x, g, tgt)


def _merge_bwd(dxo, y3, yc, oc, proj, wb, wout):
    S = dxo.shape[0]
    tm = 256

    def body(dx_ref, y_ref, yc_ref, oc_ref, cg_ref, *rest):
        gm = rest[:8]
        wb_ref, wo_ref, dy_ref, doc_ref, delta_ref, dcg_ref, dgm_ref, dt_ref = rest[8:]
        dz = _dot_nt(dx_ref[...].astype(BF16), wo_ref[...])
        ys = (y_ref[0], y_ref[1], yc_ref[...], y_ref[2])
        for b in range(4):
            gate = _sigmoid(jnp.concatenate([gm[2 * b][...], gm[2 * b + 1][...]], axis=1).astype(F32))
            t = _dot(ys[b], wb_ref[b])
            dgm_ref[:, b * D_MODEL:(b + 1) * D_MODEL] = (dz * t * gate * (1.0 - gate)).astype(BF16)
            dt = (dz * gate).astype(BF16)
            dt_ref[b] = dt
            dyb = _dot_nt(dt, wb_ref[b])
            if b == 2:
                cg = cg_ref[...].astype(F32)
                oc = oc_ref[...].astype(F32)
                doc = dyb * _silu(cg)
                dcg_ref[...] = (dyb * oc * _dsilu(cg)).astype(BF16)
                doc_ref[...] = doc.astype(BF16)
                prod = doc * oc
                delta_ref[...] = _put_cols([jnp.sum(prod[:, h * HEAD:(h + 1) * HEAD], axis=1, keepdims=True)
                                            for h in range(N_HEAD)])
            else:
                dy_ref[b if b < 2 else 2] = dyb.astype(BF16)

    return pl.pallas_call(
        body, name="merge_bwd",
        grid=(S // tm,),
        in_specs=[_rows(tm, D_MODEL), pl.BlockSpec((3, tm, WIDTH), lambda i: (0, i, 0)),
                  _rows(tm, WIDTH), _rows(tm, WIDTH), _rows(tm, WIDTH, CB_CGATE)] + _merge_specs(tm)
                 + [_full((4, WIDTH, D_MODEL)), _full((D_MODEL, D_MODEL))],
        out_specs=[pl.BlockSpec((3, tm, WIDTH), lambda i: (0, i, 0)), _rows(tm, WIDTH), _rows(tm, 128),
                   _rows(tm, WIDTH), _rows(tm, 4 * D_MODEL), pl.BlockSpec((4, tm, D_MODEL), lambda i: (0, i, 0))],
        out_shape=[jax.ShapeDtypeStruct((3, S, WIDTH), BF16), jax.ShapeDtypeStruct((S, WIDTH), BF16),
                   jax.ShapeDtypeStruct((S, 128), F32), jax.ShapeDtypeStruct((S, WIDTH), BF16),
                   jax.ShapeDtypeStruct((S, 4 * D_MODEL), BF16), jax.ShapeDtypeStruct((4, S, D_MODEL), BF16)],
        compiler_params=_params(("parallel",)),
    )(dxo, y3, yc, oc, proj, *([proj] * 8), wb, wout)


def _attn_bwd(q, qcb, k, kcb, v, vcb, do, lse, delta, bps):
    S = q.shape[0]
    tm = ATT_TILE
    nb = tm // CHUNK
    nblk = S // CHUNK

    def body(q_ref, k_ref, v_ref, do_ref, l_ref, d_ref, kh_ref, vh_ref, qn_ref, don_ref, ln_ref, dn_ref,
             dq_ref, dk_ref, dv_ref, dks, dvs):
        i = pl.program_id(0)
        row = lax.broadcasted_iota(jnp.int32, (CHUNK, CHUNK), 0)
        col = lax.broadcasted_iota(jnp.int32, (CHUNK, CHUNK), 1)
        m_cur = col <= row
        m_prev = col >= row
        dks[...] = jnp.zeros_like(dks)
        dvs[...] = jnp.zeros_like(dvs)

        def pair(qh, doh, lse_h, delta_h, kh, vh, mask, pen):
            s = jnp.where(mask, _dot_nt(qh, kh) * ATT_SCALE, NEG) + pen
            p = jnp.exp(s - lse_h)
            ds = p * (_dot_nt(doh, vh) - delta_h)
            return p.astype(BF16), ds.astype(BF16)

        for n in range(nb + 1):
            if n < nb:
                rs = slice(n * CHUNK, (n + 1) * CHUNK)
                qr, dor, lr, dr = q_ref, do_ref, l_ref, d_ref
            else:
                rs = slice(0, CHUNK)
                qr, dor, lr, dr = qn_ref, don_ref, ln_ref, dn_ref
            ps = slice((n - 1) * CHUNK, n * CHUNK)
            gb = i * nb + n
            pen = jnp.where(gb % bps != 0, 0.0, NEG)
            if n == nb:
                pen = pen + jnp.where(gb < nblk, 0.0, NEG)
            lblk = lr[rs, :]
            dblk = dr[rs, :]
            for h in range(N_HEAD):
                cs = slice(h * HEAD, (h + 1) * HEAD)
                qh, doh = qr[rs, cs], dor[rs, cs]
                lse_h, delta_h = _col(lblk, h), _col(dblk, h)
                kp, vp = (kh_ref[:, cs], vh_ref[:, cs]) if n == 0 else (k_ref[ps, cs], v_ref[ps, cs])
                pp, dsp = pair(qh, doh, lse_h, delta_h, kp, vp, m_prev, pen)
                if n > 0:
                    dks[ps, cs] += _dot_tn(dsp, qh) * ATT_SCALE
                    dvs[ps, cs] += _dot_tn(pp, doh)
                if n < nb:
                    pc, dsc = pair(qh, doh, lse_h, delta_h, k_ref[rs, cs], v_ref[rs, cs], m_cur, 0.0)
                    dks[rs, cs] += _dot_tn(dsc, qh) * ATT_SCALE
                    dvs[rs, cs] += _dot_tn(pc, doh)
                    dq_ref[rs, cs] = ((_dot(dsc, k_ref[rs, cs]) + _dot(dsp, kp)) * ATT_SCALE).astype(BF16)
        dk_ref[...] = dks[...].astype(BF16)
        dv_ref[...] = dvs[...].astype(BF16)

    def prev_halo(cb):
        return pl.BlockSpec((CHUNK, WIDTH), lambda i: (jnp.maximum(i * nb - 1, 0), cb))

    def next_halo(width, cb=0):
        return pl.BlockSpec((CHUNK, width), lambda i: (jnp.minimum(i * nb + nb, nblk - 1), cb))

    return pl.pallas_call(
        body, name=f"attn_bwd_{bps}",
        grid=(S // tm,),
        in_specs=[_rows(tm, WIDTH, qcb), _rows(tm, WIDTH, kcb), _rows(tm, WIDTH, vcb), _rows(tm, WIDTH),
                  _rows(tm, 128), _rows(tm, 128), prev_halo(kcb), prev_halo(vcb),
                  next_halo(WIDTH, qcb), next_halo(WIDTH), next_halo(128), next_halo(128)],
        out_specs=[_rows(tm, WIDTH), _rows(tm, WIDTH), _rows(tm, WIDTH)],
        out_shape=[jax.ShapeDtypeStruct((S, WIDTH), BF16)] * 3,
        scratch_shapes=[pltpu.VMEM((tm, WIDTH), F32), pltpu.VMEM((tm, WIDTH), F32)],
        compiler_params=_params(("parallel",)),
    )(q, k, v, do, lse, delta, k, v, q, do, lse, delta)


def _abm_bwd(proj, dy3, ln_g, ln_b, wsm, wsm_t, bias_full, pool_w, pool_wt, pool_scale, kv):
    S = proj.shape[0]
    tm = 512
    nchunk = tm // CHUNK
    nblk = S // CHUNK

    def body(u_ref, v_ref, ag_ref, p_ref, ph_ref, pg_ref, pgn_ref, mq_ref, mg_ref, dy_ref, dypn_ref,
             lng_ref, lnb_ref, wsm_ref, wsmt_ref, bias_ref, pw_ref, pwt_ref, ps_ref, kv_ref,
             dab_ref, dm_ref, dlng_ref, dlnb_ref, dws_ref, dbias_ref, dpw_ref, dps_ref, dkv_ref,
             mix, dvl, ddn):
        i = pl.program_id(0)

        @pl.when(i == 0)
        def _():
            for r in (dlng_ref, dlnb_ref, dws_ref, dbias_ref, dpw_ref, dps_ref, dkv_ref):
                r[...] = jnp.zeros_like(r)

        au = u_ref[...].astype(F32)
        av = v_ref[...].astype(F32)
        ag = ag_ref[...].astype(F32)
        u = _gelu(au)
        v = _gelu(av)
        vhat, rstd = _layer_norm_fwd(v)
        vln = (vhat * lng_ref[...] + lnb_ref[...]).astype(BF16)
        for c in range(nchunk):
            for h in range(N_HEAD):
                rs, cs = slice(c * CHUNK, (c + 1) * CHUNK), slice(h * HEAD, (h + 1) * HEAD)
                mix[rs, cs] = _dot(wsm_ref[h], vln[rs, cs]) + bias_ref[:, cs]
        dya = dy_ref[0].astype(F32)
        sg = _silu(ag)
        mixed = mix[...]
        dab_ref[:, 2 * WIDTH:3 * WIDTH] = (dya * u * mixed * _dsilu(ag)).astype(BF16)
        dab_ref[:, 0:WIDTH] = (dya * mixed * sg * _dgelu(au)).astype(BF16)
        dmixed = dya * u * sg
        dmb = dmixed.astype(BF16)
        tril = (lax.broadcasted_iota(jnp.int32, (CHUNK, CHUNK), 1)
                <= lax.broadcasted_iota(jnp.int32, (CHUNK, CHUNK), 0))
        for c in range(nchunk):
            rs = slice(c * CHUNK, (c + 1) * CHUNK)
            dbias_ref[...] += dmixed[rs, :]
            for h in range(N_HEAD):
                cs = slice(h * HEAD, (h + 1) * HEAD)
                dvl[rs, cs] = _dot(wsmt_ref[h], dmb[rs, cs])
                dws_ref[h] += jnp.where(tril, _dot_nt(dmb[rs, cs], vln[rs, cs]), 0.0)
        dvln = dvl[...]
        dlng_ref[...] += jnp.sum(dvln * vhat, axis=0, keepdims=True)
        dlnb_ref[...] += jnp.sum(dvln, axis=0, keepdims=True)
        dvh = dvln * lng_ref[...]
        dv = rstd * (dvh - jnp.mean(dvh, axis=-1, keepdims=True)
                     - vhat * jnp.mean(dvh * vhat, axis=-1, keepdims=True))
        dab_ref[:, WIDTH:2 * WIDTH] = (dv * _dgelu(av)).astype(BF16)

        halo_ok = (i > 0).astype(F32)
        for c in range(nchunk):
            rs = slice(c * CHUNK, (c + 1) * CHUNK)
            for g, win in enumerate(POOL_WINDOWS):
                cs = slice(g * HEAD, (g + 1) * HEAD)
                bcur, bprev = _band_masks(win)
                cur = p_ref[rs, cs]
                if c == 0:
                    prev = (ph_ref[:, cs].astype(F32) * halo_ok).astype(BF16)
                else:
                    prev = p_ref[(c - 1) * CHUNK:c * CHUNK, cs]
                sums = _dot(bcur, cur) + _dot(bprev, prev)
                dvl[rs, cs] = sums * _inv_count(i * tm + c * CHUNK, win) - cur.astype(F32)
        dmat = dvl[...].astype(BF16)
        for g in range(4):
            cs = slice(g * HEAD, (g + 1) * HEAD)
            mix[:, cs] = _dot(dmat[:, cs], pw_ref[g])
        yg = mix[...]
        pg = pg_ref[...].astype(F32)
        dyp = dy_ref[1].astype(F32)
        dyy = dyp * _silu(pg)
        scale = ps_ref[...]
        dab_ref[:, 4 * WIDTH:5 * WIDTH] = (dyp * yg * scale * _dsilu(pg)).astype(BF16)
        dps_ref[...] += jnp.sum(dyy * yg, axis=0, keepdims=True)
        dyg = (dyy * scale).astype(BF16)
        for g in range(4):
            cs = slice(g * HEAD, (g + 1) * HEAD)
            dpw_ref[g] += _dot_tn(dmat[:, cs], dyg[:, cs])
            mix[:, cs] = _dot(dyg[:, cs], pwt_ref[g])
        next_ok = (i + 1 < S // tm).astype(F32)
        dygn = (dypn_ref[...].astype(F32) * _silu(pgn_ref[...].astype(F32)) * scale * next_ok).astype(BF16)
        for c in range(nchunk + 1):
            for g, win in enumerate(POOL_WINDOWS):
                cs = slice(g * HEAD, (g + 1) * HEAD)
                if c < nchunk:
                    dd = mix[c * CHUNK:(c + 1) * CHUNK, cs]
                else:
                    dd = _dot(dygn[:, cs], pwt_ref[g])
                ddn[c * CHUNK:(c + 1) * CHUNK, cs] = dd * _inv_count(i * tm + c * CHUNK, win)
        ddnb = ddn[...].astype(BF16)
        for c in range(nchunk):
            rs = slice(c * CHUNK, (c + 1) * CHUNK)
            ns = slice((c + 1) * CHUNK, (c + 2) * CHUNK)
            for g, win in enumerate(POOL_WINDOWS):
                cs = slice(g * HEAD, (g + 1) * HEAD)
                bcur, bprev = _band_masks(win)
                dp = _dot_tn(bcur, ddnb[rs, cs]) + _dot_tn(bprev, ddnb[ns, cs]) - mix[rs, cs]
                dab_ref[rs, 3 * WIDTH + g * HEAD:3 * WIDTH + (g + 1) * HEAD] = dp.astype(BF16)

        mg = mg_ref[...].astype(F32)
        dym = dy_ref[2].astype(F32)
        dob = (dym * _silu(mg)).astype(BF16)
        for h in range(N_HEAD):
            cs = slice(h * HEAD, (h + 1) * HEAD)
            vs = slice(WIDTH + h * HEAD, WIDTH + (h + 1) * HEAD)
            qh = mq_ref[:, cs]
            p = _mem_softmax(qh, kv_ref[:, cs])
            pb = p.astype(BF16)
            mix[:, cs] = _dot(pb, kv_ref[:, vs])
            dp = _dot_nt(dob[:, cs], kv_ref[:, vs])
            ds = (p * (dp - jnp.sum(p * dp, axis=-1, keepdims=True))).astype(BF16)
            dm_ref[:, cs] = (_dot(ds, kv_ref[:, cs]) * ATT_SCALE).astype(BF16)
            dkv_ref[:, cs] += _dot_tn(ds, qh) * ATT_SCALE
            dkv_ref[:, vs] += _dot_tn(pb, dob[:, cs])
        dm_ref[:, WIDTH:2 * WIDTH] = (dym * mix[...] * _dsilu(mg)).astype(BF16)

    blk = tm // CHUNK
    small = [_full((1, WIDTH)), _full((1, WIDTH)), _full((N_HEAD, CHUNK, CHUNK)), _full((CHUNK, WIDTH)),
             _full((4, HEAD, HEAD)), _full((1, WIDTH)), _full((MEM_LEN, 2 * WIDTH))]
    return pl.pallas_call(
        body, name="abm_bwd",
        grid=(S // tm,),
        in_specs=[_rows(tm, WIDTH, CB_U), _rows(tm, WIDTH, CB_V), _rows(tm, WIDTH, CB_AGATE),
                  _rows(tm, WIDTH, CB_PIN),
                  pl.BlockSpec((CHUNK, WIDTH), lambda i: (jnp.maximum(i * blk - 1, 0), CB_PIN)),
                  _rows(tm, WIDTH, CB_PGATE),
                  pl.BlockSpec((CHUNK, WIDTH), lambda i: (jnp.minimum(i * blk + blk, nblk - 1), CB_PGATE)),
                  _rows(tm, WIDTH, CB_MQ), _rows(tm, WIDTH, CB_MGATE),
                  pl.BlockSpec((3, tm, WIDTH), lambda i: (0, i, 0)),
                  pl.BlockSpec((None, CHUNK, WIDTH), lambda i: (1, jnp.minimum(i * blk + blk, nblk - 1), 0)),
                  _full((1, WIDTH)), _full((1, WIDTH)), _full((N_HEAD, CHUNK, CHUNK)), _full((N_HEAD, CHUNK, CHUNK)),
                  _full((CHUNK, WIDTH)), _full((4, HEAD, HEAD)), _full((4, HEAD, HEAD)), _full((1, WIDTH)),
                  _full((MEM_LEN, 2 * WIDTH))],
        out_specs=[_rows(tm, 5 * WIDTH), _rows(tm, 2 * WIDTH)] + small,
        out_shape=[jax.ShapeDtypeStruct((S, 5 * WIDTH), BF16), jax.ShapeDtypeStruct((S, 2 * WIDTH), BF16),
                   jax.ShapeDtypeStruct((1, WIDTH), F32), jax.ShapeDtypeStruct((1, WIDTH), F32),
                   jax.ShapeDtypeStruct((N_HEAD, CHUNK, CHUNK), F32), jax.ShapeDtypeStruct((CHUNK, WIDTH), F32),
                   jax.ShapeDtypeStruct((4, HEAD, HEAD), F32), jax.ShapeDtypeStruct((1, WIDTH), F32),
                   jax.ShapeDtypeStruct((MEM_LEN, 2 * WIDTH), F32)],
        scratch_shapes=[pltpu.VMEM((tm, WIDTH), F32), pltpu.VMEM((tm, WIDTH), F32),
                        pltpu.VMEM((tm + CHUNK, WIDTH), F32)],
        compiler_params=_params(("arbitrary",)),
    )(proj, proj, proj, proj, proj, proj, proj, proj, proj, dy3, dy3,
      ln_g, ln_b, wsm, wsm_t, bias_full, pool_w, pool_wt, pool_scale, kv)


def _bias_reduce(dbias_full):
    def body(d_ref, o_ref):
        d = d_ref[...]
        o_ref[...] = _put_cols([jnp.sum(d[:, h * HEAD:(h + 1) * HEAD], axis=1, keepdims=True) for h in range(N_HEAD)])

    return pl.pallas_call(body, name="bias_reduce", out_shape=jax.ShapeDtypeStruct((CHUNK, 128), F32))(dbias_full)


def _mem_bwd(mem, g, mem_n, w, dkv):
    def body(m_ref, g_ref, mn_ref, w_ref, dkv_ref, dw_ref, dg_ref):
        dkvb = dkv_ref[...].astype(BF16)
        dw_ref[...] = _dot_tn(mn_ref[...], dkvb).astype(BF16)
        dmn = _dot_nt(dkvb, w_ref[...])
        xf = m_ref[...]
        r = lax.rsqrt(jnp.mean(xf * xf, axis=-1, keepdims=True) + EPS)
        dg_ref[...] = jnp.sum(dmn * xf * r, axis=0, keepdims=True)

    return pl.pallas_call(
        body, name="mem_bwd",
        out_shape=[jax.ShapeDtypeStruct((D_MODEL, 2 * WIDTH), BF16), jax.ShapeDtypeStruct((1, D_MODEL), F32)],
        compiler_params=pltpu.CompilerParams(vmem_limit_bytes=VMEM_LIMIT),
    )(mem, g, mem_n, w, dkv)


def _dh_bwd(dproj, w, x, g, dxo):
    S = x.shape[0]
    tm, tk = 512, 1536
    nk = D_IN // tk

    def body(dp_ref, w_ref, x_ref, g_ref, dxo_ref, dx_ref, dg_ref, acc):
        i, kk = pl.program_id(0), pl.program_id(1)

        @pl.when(jnp.logical_and(i == 0, kk == 0))
        def _():
            dg_ref[...] = jnp.zeros_like(dg_ref)

        @pl.when(kk == 0)
        def _():
            acc[...] = jnp.zeros_like(acc)

        acc[...] += _dot_nt(dp_ref[...], w_ref[...])

        @pl.when(kk == nk - 1)
        def _():
            xf = x_ref[...]
            r = lax.rsqrt(jnp.mean(xf * xf, axis=-1, keepdims=True) + EPS)
            xhat = xf * r
            dh = acc[...]
            dg_ref[...] += jnp.sum(dh * xhat, axis=0, keepdims=True)
            dxh = dh * g_ref[...]
            dx_ref[...] = dxo_ref[...] + r * (dxh - xhat * jnp.mean(dxh * xhat, axis=-1, keepdims=True))

    return pl.pallas_call(
        body, name="dh_bwd",
        grid=(S // tm, nk),
        in_specs=[pl.BlockSpec((tm, tk), lambda i, k: (i, k)), pl.BlockSpec((D_MODEL, tk), lambda i, k: (0, k)),
                  pl.BlockSpec((tm, D_MODEL), lambda i, k: (i, 0)), pl.BlockSpec((1, D_MODEL), lambda i, k: (0, 0)),
                  pl.BlockSpec((tm, D_MODEL), lambda i, k: (i, 0))],
        out_specs=[pl.BlockSpec((tm, D_MODEL), lambda i, k: (i, 0)), pl.BlockSpec((1, D_MODEL), lambda i, k: (0, 0))],
        out_shape=[jax.ShapeDtypeStruct((S, D_MODEL), F32), jax.ShapeDtypeStruct((1, D_MODEL), F32)],
        scratch_shapes=[pltpu.VMEM((tm, D_MODEL), F32)],
        compiler_params=_params(("arbitrary", "arbitrary")),
    )(dproj, w, x, g, dxo)


def _matmul_tn(a, b, tn, name):
    K, M = a.shape
    N = b.shape[1]
    tk = 1024
    nk = K // tk

    def body(a_ref, b_ref, o_ref, acc):
        kk = pl.program_id(1)

        @pl.when(kk == 0)
        def _():
            acc[...] = jnp.zeros_like(acc)

        acc[...] += _dot_tn(a_ref[...].astype(BF16), b_ref[...].astype(BF16))

        @pl.when(kk == nk - 1)
        def _():
            o_ref[...] = acc[...].astype(BF16)

    return pl.pallas_call(
        body, name=name,
        grid=(N // tn, nk),
        in_specs=[pl.BlockSpec((tk, M), lambda j, k: (k, 0)), pl.BlockSpec((tk, tn), lambda j, k: (k, j))],
        out_specs=pl.BlockSpec((M, tn), lambda j, k: (0, j)),
        out_shape=jax.ShapeDtypeStruct((M, N), BF16),
        scratch_shapes=[pltpu.VMEM((M, tn), F32)],
        compiler_params=_params(("parallel", "arbitrary")),
    )(a, b)


def _adamw(parts, w, m, v, name):
    P, R, C = parts.shape
    tr = R
    for cand in (512, 256, 128, 64, 32, 16, 8):
        if R % cand == 0 and cand * C * 4 <= (2 << 20):
            tr = cand
            break
    c1 = 1.0 / (1.0 - ADAM_B1 ** ADAM_STEP)
    c2 = 1.0 / (1.0 - ADAM_B2 ** ADAM_STEP)

    def body(p_ref, w_ref, m_ref, v_ref, g_ref, d_ref, nm_ref, nv_ref):
        g = p_ref[0].astype(F32)
        for k in range(1, P):
            g = g + p_ref[k].astype(F32)
        nm = ADAM_B1 * m_ref[...] + (1.0 - ADAM_B1) * g
        nv = ADAM_B2 * v_ref[...] + (1.0 - ADAM_B2) * (g * g)
        g_ref[...] = g
        nm_ref[...] = nm
        nv_ref[...] = nv
        d_ref[...] = -ADAM_LR * ((nm * c1) / (jnp.sqrt(nv * c2) + ADAM_EPS) + ADAM_WD * w_ref[...])

    spec = pl.BlockSpec((tr, C), lambda i: (i, 0))
    return pl.pallas_call(
        body, name=name,
        grid=(R // tr,),
        in_specs=[pl.BlockSpec((P, tr, C), lambda i: (0, i, 0)), spec, spec, spec],
        out_specs=[spec] * 4,
        out_shape=[jax.ShapeDtypeStruct((R, C), F32)] * 4,
        compiler_params=_params(("parallel",)),
    )(parts, w, m, v)


def _place():
    return lax.axis_index("x"), lax.axis_index("y"), lax.axis_index("c")


def _all_gather(shards):
    n = len(shards)

    def body(*refs):
        ins, outs = refs[:n], refs[n:2 * n]
        send_sems, recv_sems, local_sems = refs[2 * n:]
        x, y, c = _place()
        me, sibling = (x, y, c), (x, y, 1 - c)
        chips = [(1 - x, y), (x, 1 - y), (1 - x, 1 - y)]

        def copy(a, k, block, to, src=None):
            dst = outs[a].at[4 * block[0] + 2 * block[1] + block[2]]
            return pltpu.make_async_remote_copy(
                src_ref=dst if src is None else src, dst_ref=dst,
                send_sem=send_sems.at[7 * a + k], recv_sem=recv_sems.at[7 * a + k],
                device_id=to, device_id_type=MESH)

        started = []
        mine = []
        for a in range(n):
            cp = pltpu.make_async_copy(ins[a], outs[a].at[4 * x + 2 * y + c], local_sems.at[a])
            cp.start()
            mine.append(cp)
            started.append(copy(a, 0, me, sibling, src=ins[a]))
            started += [copy(a, 1 + j, me, (*chip, c), src=ins[a]) for j, chip in enumerate(chips)]
        for cp in started:
            cp.start()
        for j, chip in enumerate(chips):
            for a in range(n):
                copy(a, 1 + j, (*chip, c), me).wait_recv()
                fwd = copy(a, 4 + j, (*chip, c), sibling)
                fwd.start()
                started.append(fwd)
        for a in range(n):
            copy(a, 0, sibling, me).wait_recv()
            for j, chip in enumerate(chips):
                copy(a, 4 + j, (*chip, 1 - c), me).wait_recv()
        for cp in started:
            cp.wait_send()
        for cp in mine:
            cp.wait()

    return pl.pallas_call(
        body, name="weights_all_gather",
        in_specs=[ANY] * n, out_specs=[ANY] * n,
        out_shape=[jax.ShapeDtypeStruct((N_DEV,) + s.shape, s.dtype) for s in shards],
        scratch_shapes=[pltpu.SemaphoreType.DMA((7 * n,)), pltpu.SemaphoreType.DMA((7 * n,)),
                        pltpu.SemaphoreType.DMA((n,))],
        compiler_params=pltpu.CompilerParams(has_side_effects=True),
    )(*shards)


def _rs_sibling(grads):
    n = len(grads)

    def body(*refs):
        ins, outs = refs[:n], refs[n:2 * n]
        send_sems, recv_sems = refs[2 * n:]
        x, y, c = _place()
        copies = [pltpu.make_async_remote_copy(
            src_ref=ins[a].at[1 - c], dst_ref=outs[a], send_sem=send_sems.at[a], recv_sem=recv_sems.at[a],
            device_id=(x, y, 1 - c), device_id_type=MESH) for a in range(n)]
        for cp in copies:
            cp.start()
        for cp in copies:
            cp.wait()

    return pl.pallas_call(
        body, name="grads_to_sibling",
        in_specs=[ANY] * n, out_specs=[ANY] * n,
        out_shape=[jax.ShapeDtypeStruct(g.shape[1:], g.dtype) for g in grads],
        scratch_shapes=[pltpu.SemaphoreType.DMA((n,)), pltpu.SemaphoreType.DMA((n,))],
        compiler_params=pltpu.CompilerParams(has_side_effects=True),
    )(*grads)


def _pair_sum(grad, recv, name):
    _, nchip, R, C = grad.shape
    tr = R
    for cand in (512, 256, 128, 64, 32, 16):
        if R % cand == 0 and cand * C * 2 <= (2 << 20):
            tr = cand
            break

    def body(c_ref, g_ref, r_ref, o_ref):
        o_ref[...] = (g_ref[...].astype(F32) + r_ref[...].astype(F32)).astype(BF16)

    return pl.pallas_call(
        body, name=name,
        grid_spec=pltpu.PrefetchScalarGridSpec(
            num_scalar_prefetch=1, grid=(nchip, R // tr),
            in_specs=[pl.BlockSpec((None, None, tr, C), lambda j, i, c_ref: (c_ref[0], j, i, 0)),
                      pl.BlockSpec((None, tr, C), lambda j, i, c_ref: (j, i, 0))],
            out_specs=pl.BlockSpec((None, tr, C), lambda j, i, c_ref: (j, i, 0))),
        out_shape=jax.ShapeDtypeStruct((nchip, R, C), BF16),
        compiler_params=_params(("parallel", "parallel")),
    )(lax.axis_index("c").reshape(1).astype(jnp.int32), grad, recv)


def _rs_chips(parts):
    n = len(parts)

    def body(*refs):
        ins, outs = refs[:n], refs[n:2 * n]
        send_sems, recv_sems, local_sems = refs[2 * n:]
        x, y, c = _place()
        my_chip = 2 * x + y
        others = [(1 - x, y), (x, 1 - y), (1 - x, 1 - y)]
        copies, mine = [], []
        for a in range(n):
            cp = pltpu.make_async_copy(ins[a].at[my_chip], outs[a].at[my_chip], local_sems.at[a])
            cp.start()
            mine.append(cp)
            for k, (ox, oy) in enumerate(others):
                copies.append(pltpu.make_async_remote_copy(
                    src_ref=ins[a].at[2 * ox + oy], dst_ref=outs[a].at[my_chip],
                    send_sem=send_sems.at[3 * a + k], recv_sem=recv_sems.at[3 * a + k],
                    device_id=(ox, oy, c), device_id_type=MESH))
        for cp in copies:
            cp.start()
        for cp in copies:
            cp.wait()
        for cp in mine:
            cp.wait()

    return pl.pallas_call(
        body, name="grads_to_chips",
        in_specs=[ANY] * n, out_specs=[ANY] * n,
        out_shape=[jax.ShapeDtypeStruct(p.shape, p.dtype) for p in parts],
        scratch_shapes=[pltpu.SemaphoreType.DMA((3 * n,)), pltpu.SemaphoreType.DMA((3 * n,)),
                        pltpu.SemaphoreType.DMA((n,))],
        compiler_params=pltpu.CompilerParams(has_side_effects=True),
    )(*parts)


SMALL_ROWS = 544


def _all_reduce_small(buf):
    def body(in_ref, out_ref, recv, acc, send1, recv1, send2, recv2):
        x, y, c = _place()
        me = 4 * x + 2 * y + c
        peers = [(x ^ (r >> 2), y ^ ((r >> 1) & 1), c ^ (r & 1)) for r in range(1, N_DEV)]

        def idx(p):
            return 4 * p[0] + 2 * p[1] + p[2]

        first = [pltpu.make_async_remote_copy(
            src_ref=in_ref.at[idx(p)], dst_ref=recv.at[me], send_sem=send1.at[r], recv_sem=recv1.at[r],
            device_id=p, device_id_type=MESH) for r, p in enumerate(peers)]
        for cp in first:
            cp.start()
        recv[me] = in_ref[me]
        for r, p in enumerate(peers):
            pltpu.make_async_remote_copy(
                src_ref=in_ref.at[idx(p)], dst_ref=recv.at[idx(p)], send_sem=send1.at[r], recv_sem=recv1.at[r],
                device_id=p, device_id_type=MESH).wait_recv()
        total = recv[0]
        for k in range(1, N_DEV):
            total = total + recv[k]
        acc[...] = total
        out_ref[me] = total
        second = [pltpu.make_async_remote_copy(
            src_ref=acc, dst_ref=out_ref.at[me], send_sem=send2.at[r], recv_sem=recv2.at[r],
            device_id=p, device_id_type=MESH) for r, p in enumerate(peers)]
        for cp in second:
            cp.start()
        for r, p in enumerate(peers):
            pltpu.make_async_remote_copy(
                src_ref=acc, dst_ref=out_ref.at[idx(p)], send_sem=send2.at[r], recv_sem=recv2.at[r],
                device_id=p, device_id_type=MESH).wait_recv()
        for cp in first + second:
            cp.wait_send()

    vm = pl.BlockSpec(memory_space=pltpu.VMEM)
    return pl.pallas_call(
        body, name="small_grads_all_reduce",
        in_specs=[vm], out_specs=vm,
        out_shape=jax.ShapeDtypeStruct(buf.shape, F32),
        scratch_shapes=[pltpu.VMEM(buf.shape, F32), pltpu.VMEM(buf.shape[1:], F32),
                        pltpu.SemaphoreType.DMA((7,)), pltpu.SemaphoreType.DMA((7,)),
                        pltpu.SemaphoreType.DMA((7,)), pltpu.SemaphoreType.DMA((7,))],
        compiler_params=pltpu.CompilerParams(has_side_effects=True, vmem_limit_bytes=VMEM_LIMIT),
    )(buf)


def _dilate(a, d):
    if d == 1:
        return a
    S, C = a.shape
    return a.reshape(S // d, d, C).transpose(1, 0, 2).reshape(S, C)


def _undilate(a, d):
    if d == 1:
        return a
    S, C = a.shape
    return a.reshape(d, S // d, C).transpose(1, 0, 2).reshape(S, C)


def _cols(a, cb, n=1):
    return a[:, cb * WIDTH:(cb + n) * WIDTH]


def _to_blocks(g, kind):
    if kind == "cols":
        R = g.shape[0]
        return g.reshape(R, N_CHIP, 2, -1).transpose(2, 1, 0, 3)
    if kind == "rows":
        C = g.shape[1]
        return g.reshape(N_CHIP, 2, -1, C).transpose(1, 0, 2, 3)
    return g.reshape(4 * WIDTH, N_CHIP, 2, -1).transpose(2, 1, 0, 3)


SMALL = ("norm_g", "gm_ln_g", "gm_ln_b", "gm_ws", "gm_bs", "pool_w", "pool_scale", "mem_norm_g", "final_norm_g")


def _pack_small(tree):
    flat = jnp.concatenate([tree[k].reshape(-1, 128) for k in SMALL], axis=0)
    return jnp.pad(flat, ((0, N_DEV * SMALL_ROWS - flat.shape[0]), (0, 0)))


def _unpack_small(flat, like):
    out, at = {}, 0
    for k in SMALL:
        rows = like[k].size // 128
        out[k] = flat[at:at + rows].reshape(like[k].shape)
        at += rows
    return out


def _make_layer(win, wkv, wb, wout, norm_g, mem_norm_g, ln_g, ln_b, gm_ws, gm_bs, pool_w, pool_scale):
    tril = jnp.tril(jnp.ones((CHUNK, CHUNK), bool))
    wsm = jnp.where(tril, gm_ws, 0.0).astype(BF16)
    pw = pool_w.astype(BF16)
    return dict(win=win, wkv=wkv, wb=wb, wout=wout, g=norm_g[None], mg=mem_norm_g[None], ln_g=ln_g[None],
                ln_b=ln_b[None], wsm=wsm, wsm_t=wsm.transpose(0, 2, 1), pw=pw, pw_t=pw.transpose(0, 2, 1),
                ps=pool_scale[None], bias=jnp.repeat(gm_bs.T, HEAD, axis=1))


def _layer_fwd(xl, mem0, L):
    S = xl.shape[0]
    proj, h = _in_proj(xl, L["g"], L["win"])
    kv, mem_n = _mem_kv(mem0, L["mg"], L["wkv"])
    y3 = _abm_fwd(proj, L["ln_g"], L["ln_b"], L["wsm"], L["bias"], L["pw"], L["ps"], kv)
    o_g, l_g = [], []
    for gi, d in enumerate(DILATIONS):
        if d == 1:
            o, lse = _attn_fwd(proj, CB_Q0, proj, CB_K, proj, CB_CV, S // CHUNK)
        else:
            o, lse = _attn_fwd(_dilate(_cols(proj, CB_Q0 + gi), d), 0, _dilate(_cols(proj, CB_K), d), 0,
                               _dilate(_cols(proj, CB_CV), d), 0, S // d // CHUNK)
        o_g.append(_undilate(o, d))
        l_g.append(_undilate(lse, d))
    xn, yc, oc, lse, z = _merge_fwd(xl, y3, o_g, l_g, proj, L["wb"], L["wout"])
    return xn, dict(x=xl, proj=proj, h=h, kv=kv, mem_n=mem_n, y3=y3, yc=yc, oc=oc, lse=lse, z=z)


def _layer_bwd(dx, mem0, L, sv):
    S = dx.shape[0]
    proj = sv["proj"]
    dy3, doc, delta, dcg, dgm, dt = _merge_bwd(dx, sv["y3"], sv["yc"], sv["oc"], proj, L["wb"], L["wout"])
    dwout = _matmul_tn(sv["z"], dx, D_MODEL, "dw_out")
    ys = (sv["y3"][0], sv["y3"][1], sv["yc"], sv["y3"][2])
    dwb = jnp.stack([_matmul_tn(ys[b], dt[b], D_MODEL, "dw_branch") for b in range(4)])
    dab, dm, dlng, dlnb, dws, dbias, dpw, dps, dkv = _abm_bwd(
        proj, dy3, L["ln_g"], L["ln_b"], L["wsm"], L["wsm_t"], L["bias"], L["pw"], L["pw_t"], L["ps"], sv["kv"])
    dq, dk, dv = [], None, None
    for gi, d in enumerate(DILATIONS):
        if d == 1:
            r = _attn_bwd(proj, CB_Q0, proj, CB_K, proj, CB_CV, doc, sv["lse"], delta, S // CHUNK)
        else:
            r = _attn_bwd(_dilate(_cols(proj, CB_Q0 + gi), d), 0, _dilate(_cols(proj, CB_K), d), 0,
                          _dilate(_cols(proj, CB_CV), d), 0, _dilate(doc, d), _dilate(sv["lse"], d),
                          _dilate(delta, d), S // d // CHUNK)
        dq.append(_undilate(r[0], d))
        dkg, dvg = _undilate(r[1], d).astype(F32), _undilate(r[2], d).astype(F32)
        dk = dkg if dk is None else dk + dkg
        dv = dvg if dv is None else dv + dvg
    dproj = jnp.concatenate([dab, dq[0], dq[1], dq[2], dk.astype(BF16), dv.astype(BF16), dcg, dm, dgm], axis=1)
    dwkv, dmg = _mem_bwd(mem0, L["mg"], sv["mem_n"], L["wkv"], dkv)
    dwin = _matmul_tn(sv["h"], dproj, 1536, "dw_in")
    dxi, dng = _dh_bwd(dproj, L["win"], sv["x"], L["g"], dx)
    big = dict(w_in=dwin, w_mem_kv=dwkv, w_branch=dwb, w_out=dwout)
    small = dict(norm_g=dng[0], gm_ln_g=dlng[0], gm_ln_b=dlnb[0], gm_ws=dws,
                 gm_bs=_bias_reduce(dbias)[:, :N_HEAD].T, pool_w=dpw, pool_scale=dps[0], mem_norm_g=dmg[0])
    return dxi, big, small


def kernel(x, mem, norm_g, w_in, gm_ln_g, gm_ln_b, gm_ws, gm_bs, pool_w, pool_scale, mem_norm_g, w_mem_kv, w_branch, w_out, final_norm_g, loss_target, m_norm_g, m_w_in, m_gm_ln_g, m_gm_ln_b, m_gm_ws, m_gm_bs, m_pool_w, m_pool_scale, m_mem_norm_g, m_w_mem_kv, m_w_branch, m_w_out, m_final_norm_g, v_norm_g, v_w_in, v_gm_ln_g, v_gm_ln_b, v_gm_ws, v_gm_bs, v_pool_w, v_pool_scale, v_mem_norm_g, v_w_mem_kv, v_w_branch, v_w_out, v_final_norm_g):
    x0 = x[0]
    mem0 = mem[0]
    tgt = loss_target[0]
    S = x0.shape[0]

    win_g, wkv_g, wb_g, wout_g = _all_gather(
        [w_in.astype(BF16), w_mem_kv.astype(BF16), w_branch.astype(BF16), w_out.astype(BF16)])

    layers = [_make_layer(win_g[:, l].transpose(1, 0, 2).reshape(D_MODEL, D_IN),
                          wkv_g[:, l].reshape(D_MODEL, 2 * WIDTH),
                          wb_g[:, l].transpose(1, 2, 0, 3).reshape(4, WIDTH, D_MODEL),
                          wout_g[:, l].reshape(D_MODEL, D_MODEL),
                          norm_g[l], mem_norm_g[l], gm_ln_g[l], gm_ln_b[l], gm_ws[l], gm_bs[l], pool_w[l],
                          pool_scale[l]) for l in range(DEPTH)]

    saved = []
    xl = x0
    for l in range(DEPTH):
        xl, sv = _layer_fwd(xl, mem0, layers[l])
        saved.append(sv)

    loss_part, dx, d_final = _loss_head(xl, final_norm_g[None], tgt)
    loss = lax.psum(loss_part[0, 0], ("x", "y", "c"))

    big = {k: [None] * DEPTH for k in ("w_in", "w_mem_kv", "w_branch", "w_out")}
    small = {k: [None] * DEPTH for k in SMALL if k != "final_norm_g"}
    for l in reversed(range(DEPTH)):
        dx, gb, gs = _layer_bwd(dx, mem0, layers[l], saved[l])
        big["w_in"][l] = _to_blocks(gb["w_in"], "cols")
        big["w_mem_kv"][l] = _to_blocks(gb["w_mem_kv"], "rows")
        big["w_branch"][l] = _to_blocks(gb["w_branch"], "branch")
        big["w_out"][l] = _to_blocks(gb["w_out"], "rows")
        for k in gs:
            small[k][l] = gs[k]
    grad_x = dx[None]

    names = ("w_in", "w_mem_kv", "w_branch", "w_out")
    grads = [jnp.concatenate(big[k], axis=2) for k in names]
    from_sibling = _rs_sibling(grads)
    pair = [_pair_sum(g, r, "pair_sum_" + k) for g, r, k in zip(grads, from_sibling, names)]
    parts = _rs_chips(pair)

    small_tree = {k: jnp.stack(small[k]) for k in small}
    small_tree["final_norm_g"] = d_final[0]
    reduced = _all_reduce_small(_pack_small(small_tree).reshape(N_DEV, SMALL_ROWS, 128))

    weights = dict(norm_g=norm_g, w_in=w_in, gm_ln_g=gm_ln_g, gm_ln_b=gm_ln_b, gm_ws=gm_ws, gm_bs=gm_bs,
                   pool_w=pool_w, pool_scale=pool_scale, mem_norm_g=mem_norm_g, w_mem_kv=w_mem_kv,
                   w_branch=w_branch, w_out=w_out, final_norm_g=final_norm_g)
    m_in = dict(norm_g=m_norm_g, w_in=m_w_in, gm_ln_g=m_gm_ln_g, gm_ln_b=m_gm_ln_b, gm_ws=m_gm_ws, gm_bs=m_gm_bs,
                pool_w=m_pool_w, pool_scale=m_pool_scale, mem_norm_g=m_mem_norm_g, w_mem_kv=m_w_mem_kv,
                w_branch=m_w_branch, w_out=m_w_out, final_norm_g=m_final_norm_g)
    v_in = dict(norm_g=v_norm_g, w_in=v_w_in, gm_ln_g=v_gm_ln_g, gm_ln_b=v_gm_ln_b, gm_ws=v_gm_ws, gm_bs=v_gm_bs,
                pool_w=v_pool_w, pool_scale=v_pool_scale, mem_norm_g=v_mem_norm_g, w_mem_kv=v_w_mem_kv,
                w_branch=v_w_branch, w_out=v_w_out, final_norm_g=v_final_norm_g)
    res = {}
    for k, p in zip(names, parts):
        shape = weights[k].shape
        R, C = p.shape[1], p.shape[2]
        outs = _adamw(p, weights[k].reshape(R, C), m_in[k].reshape(R, C), v_in[k].reshape(R, C), "adamw_" + k)
        res[k] = [o.reshape(shape) for o in outs]
    outs = _adamw(reduced.reshape(1, N_DEV * SMALL_ROWS, 128), _pack_small(weights), _pack_small(m_in),
                  _pack_small(v_in), "adamw_small")
    unpacked = [_unpack_small(o, weights) for o in outs]
    for k in SMALL:
        res[k] = [u[k] for u in unpacked]

    order = ("norm_g", "w_in", "gm_ln_g", "gm_ln_b", "gm_ws", "gm_bs", "pool_w", "pool_scale", "mem_norm_g",
             "w_mem_kv", "w_branch", "w_out", "final_norm_g")
    return (loss, grad_x, *[res[k][0] for k in order], *[res[k][1] for k in order],
            *[res[k][2] for k in order], *[res[k][3] for k in order])
```

```python
import functools
import math

import jax
import jax.numpy as jnp
from jax import lax
from jax.experimental import pallas as pl
from jax.experimental.pallas import tpu as pltpu

F32 = jnp.float32
BF16 = jnp.bfloat16

D_MODEL = 1024
DEPTH = 4
WIDTH = 512
D_IN = 10752
HEAD = 128
N_HEAD = 4
CHUNK = 128
MEM_LEN = 256
POOL_WINDOWS = (2, 4, 8, 16)
DILATIONS = (1, 4, 16)
EPS = 1e-6
NEG = -1e30
ATT_SCALE = HEAD ** -0.5
N_DEV = 8
N_CHIP = 4

CB_U, CB_V, CB_AGATE, CB_PIN, CB_PGATE = 0, 1, 2, 3, 4
CB_Q0, CB_K, CB_CV, CB_CGATE, CB_MQ, CB_MGATE, CB_MERGE = 5, 8, 9, 10, 11, 12, 13

ADAM_LR = 0.001
ADAM_B1 = 0.9
ADAM_B2 = 0.999
ADAM_EPS = 1e-08
ADAM_WD = 0.01
ADAM_STEP = 10

VMEM_LIMIT = 56 * 1024 * 1024
MESH = pl.DeviceIdType.MESH
ANY = pl.BlockSpec(memory_space=pl.ANY)

NT = (((1,), (1,)), ((), ()))
TN = (((0,), (0,)), ((), ()))


def _dot(a, b):
    return jnp.dot(a, b, preferred_element_type=F32)


def _dot_nt(a, b):
    return lax.dot_general(a, b, NT, preferred_element_type=F32)


def _dot_tn(a, b):
    return lax.dot_general(a, b, TN, preferred_element_type=F32)


def _sigmoid(x):
    return 1.0 / (1.0 + jnp.exp(-x))


def _silu(x):
    return x * _sigmoid(x)


def _dsilu(x):
    s = _sigmoid(x)
    return s * (1.0 + x * (1.0 - s))


def _gelu(x):
    return 0.5 * x * (1.0 + lax.erf(x * (2.0 ** -0.5)))


def _dgelu(x):
    return 0.5 * (1.0 + lax.erf(x * (2.0 ** -0.5))) + x * jnp.exp(-0.5 * x * x) * (1.0 / math.sqrt(2.0 * math.pi))


def _col(blk, h):
    lane = lax.broadcasted_iota(jnp.int32, blk.shape, 1)
    return jnp.sum(jnp.where(lane == h, blk, 0.0), axis=1, keepdims=True)


def _put_cols(cols):
    rows = cols[0].shape[0]
    lane = lax.broadcasted_iota(jnp.int32, (rows, 128), 1)
    out = jnp.zeros((rows, 128), F32)
    for h, cv in enumerate(cols):
        out = jnp.where(lane == h, cv, out)
    return out


def _params(sem, vmem=VMEM_LIMIT):
    return pltpu.CompilerParams(dimension_semantics=sem, vmem_limit_bytes=vmem)


def _full(shape):
    nd = len(shape)
    return pl.BlockSpec(shape, lambda *_: (0,) * nd)


def _rows(tm, width, cb=0):
    return pl.BlockSpec((tm, width), lambda i: (i, cb))


def _in_proj(x, g, w):
    S = x.shape[0]
    tm, tn = 1024, 1536

    def body(x_ref, g_ref, w_ref, proj_ref, h_ref, hs):
        @pl.when(pl.program_id(1) == 0)
        def _():
            xf = x_ref[...]
            r = lax.rsqrt(jnp.mean(xf * xf, axis=-1, keepdims=True) + EPS)
            h = (xf * r * g_ref[...]).astype(BF16)
            hs[...] = h
            h_ref[...] = h

        proj_ref[...] = _dot(hs[...], w_ref[...]).astype(BF16)

    return pl.pallas_call(
        body, name="in_proj",
        grid=(S // tm, D_IN // tn),
        in_specs=[pl.BlockSpec((tm, D_MODEL), lambda i, j: (i, 0)),
                  pl.BlockSpec((1, D_MODEL), lambda i, j: (0, 0)),
                  pl.BlockSpec((D_MODEL, tn), lambda i, j: (0, j))],
        out_specs=[pl.BlockSpec((tm, tn), lambda i, j: (i, j)),
                   pl.BlockSpec((tm, D_MODEL), lambda i, j: (i, 0))],
        out_shape=[jax.ShapeDtypeStruct((S, D_IN), BF16), jax.ShapeDtypeStruct((S, D_MODEL), BF16)],
        scratch_shapes=[pltpu.VMEM((tm, D_MODEL), BF16)],
        compiler_params=_params(("parallel", "arbitrary")),
    )(x, g, w)


def _mem_kv(mem, g, w):
    M = mem.shape[0]

    def body(m_ref, g_ref, w_ref, kv_ref, mn_ref):
        xf = m_ref[...]
        r = lax.rsqrt(jnp.mean(xf * xf, axis=-1, keepdims=True) + EPS)
        mn = (xf * r * g_ref[...]).astype(BF16)
        mn_ref[...] = mn
        kv_ref[...] = _dot(mn, w_ref[...]).astype(BF16)

    return pl.pallas_call(
        body, name="mem_kv",
        out_shape=[jax.ShapeDtypeStruct((M, 2 * WIDTH), BF16), jax.ShapeDtypeStruct((M, D_MODEL), BF16)],
        compiler_params=pltpu.CompilerParams(vmem_limit_bytes=VMEM_LIMIT),
    )(mem, g, w)


def _band_masks(win):
    t = lax.broadcasted_iota(jnp.int32, (CHUNK, CHUNK), 0)
    s = lax.broadcasted_iota(jnp.int32, (CHUNK, CHUNK), 1)
    cur = jnp.logical_and(t - s >= 0, t - s < win)
    prev = s > t + CHUNK - win
    return cur.astype(BF16), prev.astype(BF16)


def _inv_count(first_row, win):
    t = first_row + lax.broadcasted_iota(jnp.int32, (CHUNK, 1), 0)
    return 1.0 / jnp.minimum(t + 1, win).astype(F32)


def _layer_norm_fwd(v):
    mu = jnp.mean(v, axis=-1, keepdims=True)
    vc = v - mu
    var = jnp.mean(vc * vc, axis=-1, keepdims=True)
    rstd = lax.rsqrt(var + EPS)
    return vc * rstd, rstd


def _mem_softmax(q, kmem):
    s = _dot_nt(q, kmem) * ATT_SCALE
    m = jnp.max(s, axis=-1, keepdims=True)
    e = jnp.exp(s - m)
    return e * (1.0 / jnp.sum(e, axis=-1, keepdims=True))


def _abm_fwd(proj, ln_g, ln_b, wsm, bias_full, pool_w, pool_scale, kv):
    S = proj.shape[0]
    tm = 512
    nchunk = tm // CHUNK

    def body(u_ref, v_ref, ag_ref, p_ref, ph_ref, pg_ref, mq_ref, mg_ref, lng_ref, lnb_ref, wsm_ref, bias_ref,
             pw_ref, ps_ref, kv_ref, y_ref, mix):
        i = pl.program_id(0)
        u = _gelu(u_ref[...].astype(F32))
        v = _gelu(v_ref[...].astype(F32))
        vhat, _ = _layer_norm_fwd(v)
        vln = (vhat * lng_ref[...] + lnb_ref[...]).astype(BF16)
        for c in range(nchunk):
            for h in range(N_HEAD):
                rs, cs = slice(c * CHUNK, (c + 1) * CHUNK), slice(h * HEAD, (h + 1) * HEAD)
                mix[rs, cs] = _dot(wsm_ref[h], vln[rs, cs]) + bias_ref[:, cs]
        y_ref[0] = (u * mix[...] * _silu(ag_ref[...].astype(F32))).astype(BF16)
        halo_ok = (i > 0).astype(F32)
        for c in range(nchunk):
            rs = slice(c * CHUNK, (c + 1) * CHUNK)
            for g, win in enumerate(POOL_WINDOWS):
                cs = slice(g * HEAD, (g + 1) * HEAD)
                bcur, bprev = _band_masks(win)
                cur = p_ref[rs, cs]
                if c == 0:
                    prev = (ph_ref[:, cs].astype(F32) * halo_ok).astype(BF16)
                else:
                    prev = p_ref[(c - 1) * CHUNK:c * CHUNK, cs]
                sums = _dot(bcur, cur) + _dot(bprev, prev)
                dm = sums * _inv_count(i * tm + c * CHUNK, win) - cur.astype(F32)
                mix[rs, cs] = _dot(dm.astype(BF16), pw_ref[g])
        y_ref[1] = (mix[...] * ps_ref[...] * _silu(pg_ref[...].astype(F32))).astype(BF16)
        for h in range(N_HEAD):
            cs = slice(h * HEAD, (h + 1) * HEAD)
            p = _mem_softmax(mq_ref[:, cs], kv_ref[:, cs])
            mix[:, cs] = _dot(p.astype(BF16), kv_ref[:, WIDTH + h * HEAD:WIDTH + (h + 1) * HEAD])
        y_ref[2] = (mix[...] * _silu(mg_ref[...].astype(F32))).astype(BF16)

    blk = tm // CHUNK
    return pl.pallas_call(
        body, name="abm_fwd",
        grid=(S // tm,),
        in_specs=[_rows(tm, WIDTH, CB_U), _rows(tm, WIDTH, CB_V), _rows(tm, WIDTH, CB_AGATE),
                  _rows(tm, WIDTH, CB_PIN),
                  pl.BlockSpec((CHUNK, WIDTH), lambda i: (jnp.maximum(i * blk - 1, 0), CB_PIN)),
                  _rows(tm, WIDTH, CB_PGATE), _rows(tm, WIDTH, CB_MQ), _rows(tm, WIDTH, CB_MGATE),
                  _full((1, WIDTH)), _full((1, WIDTH)), _full((N_HEAD, CHUNK, CHUNK)), _full((CHUNK, WIDTH)),
                  _full((4, HEAD, HEAD)), _full((1, WIDTH)), _full((MEM_LEN, 2 * WIDTH))],
        out_specs=pl.BlockSpec((3, tm, WIDTH), lambda i: (0, i, 0)),
        out_shape=jax.ShapeDtypeStruct((3, S, WIDTH), BF16),
        scratch_shapes=[pltpu.VMEM((tm, WIDTH), F32)],
        compiler_params=_params(("parallel",)),
    )(proj, proj, proj, proj, proj, proj, proj, proj, ln_g, ln_b, wsm, bias_full, pool_w, pool_scale, kv)


ATT_TILE = 512


def _attn_fwd(q, qcb, k, kcb, v, vcb, bps):
    S = q.shape[0]
    tm = ATT_TILE
    nb = tm // CHUNK

    nblocks = nb * N_HEAD

    def body(q_ref, k_ref, v_ref, kh_ref, vh_ref, o_ref, l_ref, sc_s, sp_s, pc_s, pp_s):
        i = pl.program_id(0)

        def prev_kv(n, cs):
            if n == 0:
                return kh_ref[:, cs], vh_ref[:, cs]
            ps = slice((n - 1) * CHUNK, n * CHUNK)
            return k_ref[ps, cs], v_ref[ps, cs]

        pens = []
        for n in range(nb):
            rs = slice(n * CHUNK, (n + 1) * CHUNK)
            pens.append(jnp.full((N_HEAD * CHUNK, 1), jnp.where((i * nb + n) % bps != 0, 0.0, NEG), F32))
            for h in range(N_HEAD):
                cs = slice(h * HEAD, (h + 1) * HEAD)
                bs = slice((n * N_HEAD + h) * CHUNK, (n * N_HEAD + h + 1) * CHUNK)
                qh = q_ref[rs, cs]
                sc_s[bs, :] = _dot_nt(qh, k_ref[rs, cs])
                sp_s[bs, :] = _dot_nt(qh, prev_kv(n, cs)[0])
        row = lax.broadcasted_iota(jnp.int32, (nblocks * CHUNK, CHUNK), 0) & (CHUNK - 1)
        col = lax.broadcasted_iota(jnp.int32, (nblocks * CHUNK, CHUNK), 1)
        sc = jnp.where(col <= row, sc_s[...] * ATT_SCALE, NEG)
        sp = jnp.where(col >= row, sp_s[...] * ATT_SCALE, NEG) + jnp.concatenate(pens, axis=0)
        m = jnp.maximum(jnp.max(sc, axis=-1, keepdims=True), jnp.max(sp, axis=-1, keepdims=True))
        ec = jnp.exp(sc - m)
        ep = jnp.exp(sp - m)
        den = jnp.sum(ec, axis=-1, keepdims=True) + jnp.sum(ep, axis=-1, keepdims=True)
        inv = 1.0 / den
        pc_s[...] = (ec * inv).astype(BF16)
        pp_s[...] = (ep * inv).astype(BF16)
        lse = m + jnp.log(den)
        for n in range(nb):
            rs = slice(n * CHUNK, (n + 1) * CHUNK)
            for h in range(N_HEAD):
                cs = slice(h * HEAD, (h + 1) * HEAD)
                bs = slice((n * N_HEAD + h) * CHUNK, (n * N_HEAD + h + 1) * CHUNK)
                o = _dot(pc_s[bs, :], v_ref[rs, cs]) + _dot(pp_s[bs, :], prev_kv(n, cs)[1])
                o_ref[rs, cs] = o.astype(BF16)
            l_ref[rs, :] = _put_cols([lse[(n * N_HEAD + h) * CHUNK:(n * N_HEAD + h + 1) * CHUNK]
                                      for h in range(N_HEAD)])

    def halo(cb):
        return pl.BlockSpec((CHUNK, WIDTH), lambda i: (jnp.maximum(i * nb - 1, 0), cb))

    return pl.pallas_call(
        body, name=f"attn_fwd_{bps}",
        grid=(S // tm,),
        in_specs=[_rows(tm, WIDTH, qcb), _rows(tm, WIDTH, kcb), _rows(tm, WIDTH, vcb), halo(kcb), halo(vcb)],
        out_specs=[_rows(tm, WIDTH), _rows(tm, 128)],
        out_shape=[jax.ShapeDtypeStruct((S, WIDTH), BF16), jax.ShapeDtypeStruct((S, 128), F32)],
        scratch_shapes=[pltpu.VMEM((nblocks * CHUNK, CHUNK), F32), pltpu.VMEM((nblocks * CHUNK, CHUNK), F32),
                        pltpu.VMEM((nblocks * CHUNK, CHUNK), BF16), pltpu.VMEM((nblocks * CHUNK, CHUNK), BF16)],
        compiler_params=_params(("parallel",)),
    )(q, k, v, k, v)


def _merge_specs(tm):
    return [_rows(tm, WIDTH, CB_MERGE + j) for j in range(8)]


def _merge_fwd(x, y3, o_g, l_g, proj, wb, wout):
    S = x.shape[0]
    tm = 256

    def body(x_ref, y_ref, o0, o1, o2, l0, l1, l2, cg_ref, *rest):
        gm = rest[:8]
        wb_ref, wo_ref, xn_ref, yc_ref, oc_ref, lse_ref, z_ref, ocs = rest[8:]
        lcols = []
        for h in range(N_HEAD):
            cs = slice(h * HEAD, (h + 1) * HEAD)
            ls = [_col(l[...], h) for l in (l0, l1, l2)]
            m = jnp.maximum(jnp.maximum(ls[0], ls[1]), ls[2])
            tot = jnp.exp(ls[0] - m) + jnp.exp(ls[1] - m) + jnp.exp(ls[2] - m)
            lse = m + jnp.log(tot)
            ocs[:, cs] = sum(jnp.exp(lg - lse) * o[:, cs].astype(F32) for lg, o in zip(ls, (o0, o1, o2)))
            lcols.append(lse)
        lse_ref[...] = _put_cols(lcols)
        oc = ocs[...]
        oc_ref[...] = oc.astype(BF16)
        yc = (oc * _silu(cg_ref[...].astype(F32))).astype(BF16)
        yc_ref[...] = yc
        ys = (y_ref[0], y_ref[1], yc, y_ref[2])
        z = jnp.zeros((tm, D_MODEL), F32)
        for b in range(4):
            gate = _sigmoid(jnp.concatenate([gm[2 * b][...], gm[2 * b + 1][...]], axis=1).astype(F32))
            z = z + gate * _dot(ys[b], wb_ref[b])
        zb = z.astype(BF16)
        z_ref[...] = zb
        xn_ref[...] = x_ref[...] + _dot(zb, wo_ref[...])

    return pl.pallas_call(
        body, name="merge_fwd",
        grid=(S // tm,),
        in_specs=[_rows(tm, D_MODEL), pl.BlockSpec((3, tm, WIDTH), lambda i: (0, i, 0)),
                  _rows(tm, WIDTH), _rows(tm, WIDTH), _rows(tm, WIDTH),
                  _rows(tm, 128), _rows(tm, 128), _rows(tm, 128),
                  _rows(tm, WIDTH, CB_CGATE)] + _merge_specs(tm)
                 + [_full((4, WIDTH, D_MODEL)), _full((D_MODEL, D_MODEL))],
        out_specs=[_rows(tm, D_MODEL), _rows(tm, WIDTH), _rows(tm, WIDTH), _rows(tm, 128), _rows(tm, D_MODEL)],
        out_shape=[jax.ShapeDtypeStruct((S, D_MODEL), F32), jax.ShapeDtypeStruct((S, WIDTH), BF16),
                   jax.ShapeDtypeStruct((S, WIDTH), BF16), jax.ShapeDtypeStruct((S, 128), F32),
                   jax.ShapeDtypeStruct((S, D_MODEL), BF16)],
        scratch_shapes=[pltpu.VMEM((tm, WIDTH), F32)],
        compiler_params=_params(("parallel",)),
    )(x, y3, *o_g, *l_g, proj, *([proj] * 8), wb, wout)


def _loss_head(x, g, tgt):
    S = x.shape[0]
    tm = 512

    def body(x_ref, g_ref, t_ref, loss_ref, dx_ref, dg_ref):
        @pl.when(pl.program_id(0) == 0)
        def _():
            loss_ref[...] = jnp.zeros_like(loss_ref)
            dg_ref[...] = jnp.zeros_like(dg_ref)

        xf = x_ref[...]
        r = lax.rsqrt(jnp.mean(xf * xf, axis=-1, keepdims=True) + EPS)
        xhat = xf * r
        gv = g_ref[...]
        err = xhat * gv - t_ref[...]
        e2 = jnp.sum(err * err, axis=-1, keepdims=True)
        loss_ref[...] += (0.5 / D_MODEL) * jnp.sum(e2, axis=0, keepdims=True)
        dy = err * (1.0 / D_MODEL)
        dg_ref[...] += jnp.sum(dy * xhat, axis=0, keepdims=True)
        dxh = dy * gv
        dx_ref[...] = r * (dxh - xhat * jnp.mean(dxh * xhat, axis=-1, keepdims=True))

    return pl.pallas_call(
        body, name="loss_head",
        grid=(S // tm,),
        in_specs=[_rows(tm, D_MODEL), _full((1, D_MODEL)), _rows(tm, D_MODEL)],
        out_specs=[_full((1, 128)), _rows(tm, D_MODEL), _full((1, D_MODEL))],
        out_shape=[jax.ShapeDtypeStruct((1, 128), F32), jax.ShapeDtypeStruct((S, D_MODEL), F32),
                   jax.ShapeDtypeStruct((1, D_MODEL), F32)],
        compiler_params=_params(("arbitrary",)),
    )(x, g, tgt)


def _merge_bwd(dxo, y3, yc, oc, proj, wb, wout):
    S = dxo.shape[0]
    tm = 256

    def body(dx_ref, y_ref, yc_ref, oc_ref, cg_ref, *rest):
        gm = rest[:8]
        wb_ref, wo_ref, dy_ref, doc_ref, delta_ref, dcg_ref, dgm_ref, dt_ref = rest[8:]
        dz = _dot_nt(dx_ref[...].astype(BF16), wo_ref[...])
        ys = (y_ref[0], y_ref[1], yc_ref[...], y_ref[2])
        for b in range(4):
            gate = _sigmoid(jnp.concatenate([gm[2 * b][...], gm[2 * b + 1][...]], axis=1).astype(F32))
            t = _dot(ys[b], wb_ref[b])
            dgm_ref[:, b * D_MODEL:(b + 1) * D_MODEL] = (dz * t * gate * (1.0 - gate)).astype(BF16)
            dt = (dz * gate).astype(BF16)
            dt_ref[b] = dt
            dyb = _dot_nt(dt, wb_ref[b])
            if b == 2:
                cg = cg_ref[...].astype(F32)
                oc = oc_ref[...].astype(F32)
                doc = dyb * _silu(cg)
                dcg_ref[...] = (dyb * oc * _dsilu(cg)).astype(BF16)
                doc_ref[...] = doc.astype(BF16)
                prod = doc * oc
                delta_ref[...] = _put_cols([jnp.sum(prod[:, h * HEAD:(h + 1) * HEAD], axis=1, keepdims=True)
                                            for h in range(N_HEAD)])
            else:
                dy_ref[b if b < 2 else 2] = dyb.astype(BF16)

    return pl.pallas_call(
        body, name="merge_bwd",
        grid=(S // tm,),
        in_specs=[_rows(tm, D_MODEL), pl.BlockSpec((3, tm, WIDTH), lambda i: (0, i, 0)),
                  _rows(tm, WIDTH), _rows(tm, WIDTH), _rows(tm, WIDTH, CB_CGATE)] + _merge_specs(tm)
                 + [_full((4, WIDTH, D_MODEL)), _full((D_MODEL, D_MODEL))],
        out_specs=[pl.BlockSpec((3, tm, WIDTH), lambda i: (0, i, 0)), _rows(tm, WIDTH), _rows(tm, 128),
                   _rows(tm, WIDTH), _rows(tm, 4 * D_MODEL), pl.BlockSpec((4, tm, D_MODEL), lambda i: (0, i, 0))],
        out_shape=[jax.ShapeDtypeStruct((3, S, WIDTH), BF16), jax.ShapeDtypeStruct((S, WIDTH), BF16),
                   jax.ShapeDtypeStruct((S, 128), F32), jax.ShapeDtypeStruct((S, WIDTH), BF16),
                   jax.ShapeDtypeStruct((S, 4 * D_MODEL), BF16), jax.ShapeDtypeStruct((4, S, D_MODEL), BF16)],
        compiler_params=_params(("parallel",)),
    )(dxo, y3, yc, oc, proj, *([proj] * 8), wb, wout)


def _attn_bwd(q, qcb, k, kcb, v, vcb, do, lse, delta, bps):
    S = q.shape[0]
    tm = ATT_TILE
    nb = tm // CHUNK
    nblk = S // CHUNK

    ncur = nb * N_HEAD
    nprev = (nb + 1) * N_HEAD

    def body(q_ref, k_ref, v_ref, do_ref, l_ref, d_ref, kh_ref, vh_ref, qn_ref, don_ref, ln_ref, dn_ref,
             dq_ref, dk_ref, dv_ref, sc_s, sp_s, dpc_s, dpp_s, pc_s, pp_s, dsc_s, dsp_s):
        i = pl.program_id(0)

        def rows_of(n):
            if n < nb:
                rs = slice(n * CHUNK, (n + 1) * CHUNK)
                return rs, q_ref, do_ref, l_ref, d_ref
            return slice(0, CHUNK), qn_ref, don_ref, ln_ref, dn_ref

        def prev_kv(n, cs):
            if n == 0:
                return kh_ref[:, cs], vh_ref[:, cs]
            ps = slice((n - 1) * CHUNK, n * CHUNK)
            return k_ref[ps, cs], v_ref[ps, cs]

        def blk(n, h):
            return slice((n * N_HEAD + h) * CHUNK, (n * N_HEAD + h + 1) * CHUNK)

        pens, lses, deltas = [], [], []
        for n in range(nb + 1):
            rs, qr, dor, lr, dr = rows_of(n)
            gb = i * nb + n
            pen = jnp.where(gb % bps != 0, 0.0, NEG)
            if n == nb:
                pen = pen + jnp.where(gb < nblk, 0.0, NEG)
            pens.append(jnp.full((N_HEAD * CHUNK, 1), pen, F32))
            lblk, dblk = lr[rs, :], dr[rs, :]
            for h in range(N_HEAD):
                cs = slice(h * HEAD, (h + 1) * HEAD)
                qh, doh = qr[rs, cs], dor[rs, cs]
                lses.append(_col(lblk, h))
                deltas.append(_col(dblk, h))
                kp, vp = prev_kv(n, cs)
                sp_s[blk(n, h), :] = _dot_nt(qh, kp)
                dpp_s[blk(n, h), :] = _dot_nt(doh, vp)
                if n < nb:
                    sc_s[blk(n, h), :] = _dot_nt(qh, k_ref[rs, cs])
                    dpc_s[blk(n, h), :] = _dot_nt(doh, v_ref[rs, cs])
        lse = jnp.concatenate(lses, axis=0)
        delta = jnp.concatenate(deltas, axis=0)
        row = lax.broadcasted_iota(jnp.int32, (nprev * CHUNK, CHUNK), 0) & (CHUNK - 1)
        col = lax.broadcasted_iota(jnp.int32, (nprev * CHUNK, CHUNK), 1)
        sp = jnp.where(col >= row, sp_s[...] * ATT_SCALE, NEG) + jnp.concatenate(pens, axis=0)
        pp = jnp.exp(sp - lse)
        pp_s[...] = pp.astype(BF16)
        dsp_s[...] = (pp * (dpp_s[...] - delta)).astype(BF16)
        nc = ncur * CHUNK
        sc = jnp.where(col[:nc] <= row[:nc], sc_s[...] * ATT_SCALE, NEG)
        pc = jnp.exp(sc - lse[:nc])
        pc_s[...] = pc.astype(BF16)
        dsc_s[...] = (pc * (dpc_s[...] - delta[:nc])).astype(BF16)
        for n in range(nb):
            rs, qr, dor, _, _ = rows_of(n)
            rn, qnr, donr, _, _ = rows_of(n + 1)
            for h in range(N_HEAD):
                cs = slice(h * HEAD, (h + 1) * HEAD)
                kp, _ = prev_kv(n, cs)
                dq = _dot(dsc_s[blk(n, h), :], k_ref[rs, cs]) + _dot(dsp_s[blk(n, h), :], kp)
                dq_ref[rs, cs] = (dq * ATT_SCALE).astype(BF16)
                dk = _dot_tn(dsc_s[blk(n, h), :], qr[rs, cs]) + _dot_tn(dsp_s[blk(n + 1, h), :], qnr[rn, cs])
                dk_ref[rs, cs] = (dk * ATT_SCALE).astype(BF16)
                dv = _dot_tn(pc_s[blk(n, h), :], dor[rs, cs]) + _dot_tn(pp_s[blk(n + 1, h), :], donr[rn, cs])
                dv_ref[rs, cs] = dv.astype(BF16)

    def prev_halo(cb):
        return pl.BlockSpec((CHUNK, WIDTH), lambda i: (jnp.maximum(i * nb - 1, 0), cb))

    def next_halo(width, cb=0):
        return pl.BlockSpec((CHUNK, width), lambda i: (jnp.minimum(i * nb + nb, nblk - 1), cb))

    return pl.pallas_call(
        body, name=f"attn_bwd_{bps}",
        grid=(S // tm,),
        in_specs=[_rows(tm, WIDTH, qcb), _rows(tm, WIDTH, kcb), _rows(tm, WIDTH, vcb), _rows(tm, WIDTH),
                  _rows(tm, 128), _rows(tm, 128), prev_halo(kcb), prev_halo(vcb),
                  next_halo(WIDTH, qcb), next_halo(WIDTH), next_halo(128), next_halo(128)],
        out_specs=[_rows(tm, WIDTH), _rows(tm, WIDTH), _rows(tm, WIDTH)],
        out_shape=[jax.ShapeDtypeStruct((S, WIDTH), BF16)] * 3,
        scratch_shapes=[pltpu.VMEM((ncur * CHUNK, CHUNK), F32), pltpu.VMEM((nprev * CHUNK, CHUNK), F32),
                        pltpu.VMEM((ncur * CHUNK, CHUNK), F32), pltpu.VMEM((nprev * CHUNK, CHUNK), F32),
                        pltpu.VMEM((ncur * CHUNK, CHUNK), BF16), pltpu.VMEM((nprev * CHUNK, CHUNK), BF16),
                        pltpu.VMEM((ncur * CHUNK, CHUNK), BF16), pltpu.VMEM((nprev * CHUNK, CHUNK), BF16)],
        compiler_params=_params(("parallel",)),
    )(q, k, v, do, lse, delta, k, v, q, do, lse, delta)


def _abm_bwd(proj, dy3, ln_g, ln_b, wsm, wsm_t, bias_full, pool_w, pool_wt, pool_scale, kv):
    S = proj.shape[0]
    tm = 512
    nchunk = tm // CHUNK
    nblk = S // CHUNK

    def body(u_ref, v_ref, ag_ref, p_ref, ph_ref, pg_ref, pgn_ref, mq_ref, mg_ref, dy_ref, dypn_ref,
             lng_ref, lnb_ref, wsm_ref, wsmt_ref, bias_ref, pw_ref, pwt_ref, ps_ref, kv_ref,
             dab_ref, dm_ref, dlng_ref, dlnb_ref, dws_ref, dbias_ref, dpw_ref, dps_ref, dkv_ref,
             mix, dvl, ddn):
        i = pl.program_id(0)

        @pl.when(i == 0)
        def _():
            for r in (dlng_ref, dlnb_ref, dws_ref, dbias_ref, dpw_ref, dps_ref, dkv_ref):
                r[...] = jnp.zeros_like(r)

        au = u_ref[...].astype(F32)
        av = v_ref[...].astype(F32)
        ag = ag_ref[...].astype(F32)
        u = _gelu(au)
        v = _gelu(av)
        vhat, rstd = _layer_norm_fwd(v)
        vln = (vhat * lng_ref[...] + lnb_ref[...]).astype(BF16)
        for c in range(nchunk):
            for h in range(N_HEAD):
                rs, cs = slice(c * CHUNK, (c + 1) * CHUNK), slice(h * HEAD, (h + 1) * HEAD)
                mix[rs, cs] = _dot(wsm_ref[h], vln[rs, cs]) + bias_ref[:, cs]
        dya = dy_ref[0].astype(F32)
        sg = _silu(ag)
        mixed = mix[...]
        dab_ref[:, 2 * WIDTH:3 * WIDTH] = (dya * u * mixed * _dsilu(ag)).astype(BF16)
        dab_ref[:, 0:WIDTH] = (dya * mixed * sg * _dgelu(au)).astype(BF16)
        dmixed = dya * u * sg
        dmb = dmixed.astype(BF16)
        tril = (lax.broadcasted_iota(jnp.int32, (CHUNK, CHUNK), 1)
                <= lax.broadcasted_iota(jnp.int32, (CHUNK, CHUNK), 0))
        for c in range(nchunk):
            rs = slice(c * CHUNK, (c + 1) * CHUNK)
            dbias_ref[...] += dmixed[rs, :]
            for h in range(N_HEAD):
                cs = slice(h * HEAD, (h + 1) * HEAD)
                dvl[rs, cs] = _dot(wsmt_ref[h], dmb[rs, cs])
                dws_ref[h] += jnp.where(tril, _dot_nt(dmb[rs, cs], vln[rs, cs]), 0.0)
        dvln = dvl[...]
        dlng_ref[...] += jnp.sum(dvln * vhat, axis=0, keepdims=True)
        dlnb_ref[...] += jnp.sum(dvln, axis=0, keepdims=True)
        dvh = dvln * lng_ref[...]
        dv = rstd * (dvh - jnp.mean(dvh, axis=-1, keepdims=True)
                     - vhat * jnp.mean(dvh * vhat, axis=-1, keepdims=True))
        dab_ref[:, WIDTH:2 * WIDTH] = (dv * _dgelu(av)).astype(BF16)

        halo_ok = (i > 0).astype(F32)
        for c in range(nchunk):
            rs = slice(c * CHUNK, (c + 1) * CHUNK)
            for g, win in enumerate(POOL_WINDOWS):
                cs = slice(g * HEAD, (g + 1) * HEAD)
                bcur, bprev = _band_masks(win)
                cur = p_ref[rs, cs]
                if c == 0:
                    prev = (ph_ref[:, cs].astype(F32) * halo_ok).astype(BF16)
                else:
                    prev = p_ref[(c - 1) * CHUNK:c * CHUNK, cs]
                sums = _dot(bcur, cur) + _dot(bprev, prev)
                dvl[rs, cs] = sums * _inv_count(i * tm + c * CHUNK, win) - cur.astype(F32)
        dmat = dvl[...].astype(BF16)
        for g in range(4):
            cs = slice(g * HEAD, (g + 1) * HEAD)
            mix[:, cs] = _dot(dmat[:, cs], pw_ref[g])
        yg = mix[...]
        pg = pg_ref[...].astype(F32)
        dyp = dy_ref[1].astype(F32)
        dyy = dyp * _silu(pg)
        scale = ps_ref[...]
        dab_ref[:, 4 * WIDTH:5 * WIDTH] = (dyp * yg * scale * _dsilu(pg)).astype(BF16)
        dps_ref[...] += jnp.sum(dyy * yg, axis=0, keepdims=True)
        dyg = (dyy * scale).astype(BF16)
        for g in range(4):
            cs = slice(g * HEAD, (g + 1) * HEAD)
            dpw_ref[g] += _dot_tn(dmat[:, cs], dyg[:, cs])
            mix[:, cs] = _dot(dyg[:, cs], pwt_ref[g])
        next_ok = (i + 1 < S // tm).astype(F32)
        dygn = (dypn_ref[...].astype(F32) * _silu(pgn_ref[...].astype(F32)) * scale * next_ok).astype(BF16)
        for c in range(nchunk + 1):
            for g, win in enumerate(POOL_WINDOWS):
                cs = slice(g * HEAD, (g + 1) * HEAD)
                if c < nchunk:
                    dd = mix[c * CHUNK:(c + 1) * CHUNK, cs]
                else:
                    dd = _dot(dygn[:, cs], pwt_ref[g])
                ddn[c * CHUNK:(c + 1) * CHUNK, cs] = dd * _inv_count(i * tm + c * CHUNK, win)
        ddnb = ddn[...].astype(BF16)
        for c in range(nchunk):
            rs = slice(c * CHUNK, (c + 1) * CHUNK)
            ns = slice((c + 1) * CHUNK, (c + 2) * CHUNK)
            for g, win in enumerate(POOL_WINDOWS):
                cs = slice(g * HEAD, (g + 1) * HEAD)
                bcur, bprev = _band_masks(win)
                dp = _dot_tn(bcur, ddnb[rs, cs]) + _dot_tn(bprev, ddnb[ns, cs]) - mix[rs, cs]
                dab_ref[rs, 3 * WIDTH + g * HEAD:3 * WIDTH + (g + 1) * HEAD] = dp.astype(BF16)

        mg = mg_ref[...].astype(F32)
        dym = dy_ref[2].astype(F32)
        dob = (dym * _silu(mg)).astype(BF16)
        for h in range(N_HEAD):
            cs = slice(h * HEAD, (h + 1) * HEAD)
            vs = slice(WIDTH + h * HEAD, WIDTH + (h + 1) * HEAD)
            qh = mq_ref[:, cs]
            p = _mem_softmax(qh, kv_ref[:, cs])
            pb = p.astype(BF16)
            mix[:, cs] = _dot(pb, kv_ref[:, vs])
            dp = _dot_nt(dob[:, cs], kv_ref[:, vs])
            ds = (p * (dp - jnp.sum(p * dp, axis=-1, keepdims=True))).astype(BF16)
            dm_ref[:, cs] = (_dot(ds, kv_ref[:, cs]) * ATT_SCALE).astype(BF16)
            dkv_ref[:, cs] += _dot_tn(ds, qh) * ATT_SCALE
            dkv_ref[:, vs] += _dot_tn(pb, dob[:, cs])
        dm_ref[:, WIDTH:2 * WIDTH] = (dym * mix[...] * _dsilu(mg)).astype(BF16)

    blk = tm // CHUNK
    small = [_full((1, WIDTH)), _full((1, WIDTH)), _full((N_HEAD, CHUNK, CHUNK)), _full((CHUNK, WIDTH)),
             _full((4, HEAD, HEAD)), _full((1, WIDTH)), _full((MEM_LEN, 2 * WIDTH))]
    return pl.pallas_call(
        body, name="abm_bwd",
        grid=(S // tm,),
        in_specs=[_rows(tm, WIDTH, CB_U), _rows(tm, WIDTH, CB_V), _rows(tm, WIDTH, CB_AGATE),
                  _rows(tm, WIDTH, CB_PIN),
                  pl.BlockSpec((CHUNK, WIDTH), lambda i: (jnp.maximum(i * blk - 1, 0), CB_PIN)),
                  _rows(tm, WIDTH, CB_PGATE),
                  pl.BlockSpec((CHUNK, WIDTH), lambda i: (jnp.minimum(i * blk + blk, nblk - 1), CB_PGATE)),
                  _rows(tm, WIDTH, CB_MQ), _rows(tm, WIDTH, CB_MGATE),
                  pl.BlockSpec((3, tm, WIDTH), lambda i: (0, i, 0)),
                  pl.BlockSpec((None, CHUNK, WIDTH), lambda i: (1, jnp.minimum(i * blk + blk, nblk - 1), 0)),
                  _full((1, WIDTH)), _full((1, WIDTH)), _full((N_HEAD, CHUNK, CHUNK)), _full((N_HEAD, CHUNK, CHUNK)),
                  _full((CHUNK, WIDTH)), _full((4, HEAD, HEAD)), _full((4, HEAD, HEAD)), _full((1, WIDTH)),
                  _full((MEM_LEN, 2 * WIDTH))],
        out_specs=[_rows(tm, 5 * WIDTH), _rows(tm, 2 * WIDTH)] + small,
        out_shape=[jax.ShapeDtypeStruct((S, 5 * WIDTH), BF16), jax.ShapeDtypeStruct((S, 2 * WIDTH), BF16),
                   jax.ShapeDtypeStruct((1, WIDTH), F32), jax.ShapeDtypeStruct((1, WIDTH), F32),
                   jax.ShapeDtypeStruct((N_HEAD, CHUNK, CHUNK), F32), jax.ShapeDtypeStruct((CHUNK, WIDTH), F32),
                   jax.ShapeDtypeStruct((4, HEAD, HEAD), F32), jax.ShapeDtypeStruct((1, WIDTH), F32),
                   jax.ShapeDtypeStruct((MEM_LEN, 2 * WIDTH), F32)],
        scratch_shapes=[pltpu.VMEM((tm, WIDTH), F32), pltpu.VMEM((tm, WIDTH), F32),
                        pltpu.VMEM((tm + CHUNK, WIDTH), F32)],
        compiler_params=_params(("arbitrary",)),
    )(proj, proj, proj, proj, proj, proj, proj, proj, proj, dy3, dy3,
      ln_g, ln_b, wsm, wsm_t, bias_full, pool_w, pool_wt, pool_scale, kv)


def _bias_reduce(dbias_full):
    def body(d_ref, o_ref):
        d = d_ref[...]
        o_ref[...] = _put_cols([jnp.sum(d[:, h * HEAD:(h + 1) * HEAD], axis=1, keepdims=True) for h in range(N_HEAD)])

    return pl.pallas_call(body, name="bias_reduce", out_shape=jax.ShapeDtypeStruct((CHUNK, 128), F32))(dbias_full)


def _mem_bwd(mem, g, mem_n, w, dkv):
    def body(m_ref, g_ref, mn_ref, w_ref, dkv_ref, dw_ref, dg_ref):
        dkvb = dkv_ref[...].astype(BF16)
        dw_ref[...] = _dot_tn(mn_ref[...], dkvb).astype(BF16)
        dmn = _dot_nt(dkvb, w_ref[...])
        xf = m_ref[...]
        r = lax.rsqrt(jnp.mean(xf * xf, axis=-1, keepdims=True) + EPS)
        dg_ref[...] = jnp.sum(dmn * xf * r, axis=0, keepdims=True)

    return pl.pallas_call(
        body, name="mem_bwd",
        out_shape=[jax.ShapeDtypeStruct((D_MODEL, 2 * WIDTH), BF16), jax.ShapeDtypeStruct((1, D_MODEL), F32)],
        compiler_params=pltpu.CompilerParams(vmem_limit_bytes=VMEM_LIMIT),
    )(mem, g, mem_n, w, dkv)


def _dh_bwd(dproj, w, x, g, dxo):
    S = x.shape[0]
    tm, tk = 512, 1536
    nk = D_IN // tk

    def body(dp_ref, w_ref, x_ref, g_ref, dxo_ref, dx_ref, dg_ref, acc):
        i, kk = pl.program_id(0), pl.program_id(1)

        @pl.when(jnp.logical_and(i == 0, kk == 0))
        def _():
            dg_ref[...] = jnp.zeros_like(dg_ref)

        @pl.when(kk == 0)
        def _():
            acc[...] = jnp.zeros_like(acc)

        acc[...] += _dot_nt(dp_ref[...], w_ref[...])

        @pl.when(kk == nk - 1)
        def _():
            xf = x_ref[...]
            r = lax.rsqrt(jnp.mean(xf * xf, axis=-1, keepdims=True) + EPS)
            xhat = xf * r
            dh = acc[...]
            dg_ref[...] += jnp.sum(dh * xhat, axis=0, keepdims=True)
            dxh = dh * g_ref[...]
            dx_ref[...] = dxo_ref[...] + r * (dxh - xhat * jnp.mean(dxh * xhat, axis=-1, keepdims=True))

    return pl.pallas_call(
        body, name="dh_bwd",
        grid=(S // tm, nk),
        in_specs=[pl.BlockSpec((tm, tk), lambda i, k: (i, k)), pl.BlockSpec((D_MODEL, tk), lambda i, k: (0, k)),
                  pl.BlockSpec((tm, D_MODEL), lambda i, k: (i, 0)), pl.BlockSpec((1, D_MODEL), lambda i, k: (0, 0)),
                  pl.BlockSpec((tm, D_MODEL), lambda i, k: (i, 0))],
        out_specs=[pl.BlockSpec((tm, D_MODEL), lambda i, k: (i, 0)), pl.BlockSpec((1, D_MODEL), lambda i, k: (0, 0))],
        out_shape=[jax.ShapeDtypeStruct((S, D_MODEL), F32), jax.ShapeDtypeStruct((1, D_MODEL), F32)],
        scratch_shapes=[pltpu.VMEM((tm, D_MODEL), F32)],
        compiler_params=_params(("arbitrary", "arbitrary")),
    )(dproj, w, x, g, dxo)


def _matmul_tn(a, b, tn, name):
    K, M = a.shape
    N = b.shape[1]
    tk = 1024
    nk = K // tk

    def body(a_ref, b_ref, o_ref, acc):
        kk = pl.program_id(1)

        @pl.when(kk == 0)
        def _():
            acc[...] = jnp.zeros_like(acc)

        acc[...] += _dot_tn(a_ref[...].astype(BF16), b_ref[...].astype(BF16))

        @pl.when(kk == nk - 1)
        def _():
            o_ref[...] = acc[...].astype(BF16)

    return pl.pallas_call(
        body, name=name,
        grid=(N // tn, nk),
        in_specs=[pl.BlockSpec((tk, M), lambda j, k: (k, 0)), pl.BlockSpec((tk, tn), lambda j, k: (k, j))],
        out_specs=pl.BlockSpec((M, tn), lambda j, k: (0, j)),
        out_shape=jax.ShapeDtypeStruct((M, N), BF16),
        scratch_shapes=[pltpu.VMEM((M, tn), F32)],
        compiler_params=_params(("parallel", "arbitrary")),
    )(a, b)


def _adamw(parts, w, m, v, name):
    P, R, C = parts.shape
    tr = R
    for cand in (512, 256, 128, 64, 32, 16, 8):
        if R % cand == 0 and cand * C * 4 <= (2 << 20):
            tr = cand
            break
    c1 = 1.0 / (1.0 - ADAM_B1 ** ADAM_STEP)
    c2 = 1.0 / (1.0 - ADAM_B2 ** ADAM_STEP)

    def body(p_ref, w_ref, m_ref, v_ref, g_ref, d_ref, nm_ref, nv_ref):
        g = p_ref[0].astype(F32)
        for k in range(1, P):
            g = g + p_ref[k].astype(F32)
        nm = ADAM_B1 * m_ref[...] + (1.0 - ADAM_B1) * g
        nv = ADAM_B2 * v_ref[...] + (1.0 - ADAM_B2) * (g * g)
        g_ref[...] = g
        nm_ref[...] = nm
        nv_ref[...] = nv
        d_ref[...] = -ADAM_LR * ((nm * c1) / (jnp.sqrt(nv * c2) + ADAM_EPS) + ADAM_WD * w_ref[...])

    spec = pl.BlockSpec((tr, C), lambda i: (i, 0))
    return pl.pallas_call(
        body, name=name,
        grid=(R // tr,),
        in_specs=[pl.BlockSpec((P, tr, C), lambda i: (0, i, 0)), spec, spec, spec],
        out_specs=[spec] * 4,
        out_shape=[jax.ShapeDtypeStruct((R, C), F32)] * 4,
        compiler_params=_params(("parallel",)),
    )(parts, w, m, v)


def _place():
    return lax.axis_index("x"), lax.axis_index("y"), lax.axis_index("c")


def _all_gather(shards):
    n = len(shards)

    def body(*refs):
        ins, outs = refs[:n], refs[n:2 * n]
        send_sems, recv_sems, local_sems = refs[2 * n:]
        x, y, c = _place()
        me, sibling = (x, y, c), (x, y, 1 - c)
        chips = [(1 - x, y), (x, 1 - y), (1 - x, 1 - y)]

        def copy(a, k, block, to, src=None):
            dst = outs[a].at[4 * block[0] + 2 * block[1] + block[2]]
            return pltpu.make_async_remote_copy(
                src_ref=dst if src is None else src, dst_ref=dst,
                send_sem=send_sems.at[7 * a + k], recv_sem=recv_sems.at[7 * a + k],
                device_id=to, device_id_type=MESH)

        started = []
        mine = []
        for a in range(n):
            cp = pltpu.make_async_copy(ins[a], outs[a].at[4 * x + 2 * y + c], local_sems.at[a])
            cp.start()
            mine.append(cp)
            started.append(copy(a, 0, me, sibling, src=ins[a]))
            started += [copy(a, 1 + j, me, (*chip, c), src=ins[a]) for j, chip in enumerate(chips)]
        for cp in started:
            cp.start()
        for j, chip in enumerate(chips):
            for a in range(n):
                copy(a, 1 + j, (*chip, c), me).wait_recv()
                fwd = copy(a, 4 + j, (*chip, c), sibling)
                fwd.start()
                started.append(fwd)
        for a in range(n):
            copy(a, 0, sibling, me).wait_recv()
            for j, chip in enumerate(chips):
                copy(a, 4 + j, (*chip, 1 - c), me).wait_recv()
        for cp in started:
            cp.wait_send()
        for cp in mine:
            cp.wait()

    return pl.pallas_call(
        body, name="weights_all_gather",
        in_specs=[ANY] * n, out_specs=[ANY] * n,
        out_shape=[jax.ShapeDtypeStruct((N_DEV,) + s.shape, s.dtype) for s in shards],
        scratch_shapes=[pltpu.SemaphoreType.DMA((7 * n,)), pltpu.SemaphoreType.DMA((7 * n,)),
                        pltpu.SemaphoreType.DMA((n,))],
        compiler_params=pltpu.CompilerParams(has_side_effects=True),
    )(*shards)


def _rs_sibling(grads):
    n = len(grads)

    def body(*refs):
        ins, outs = refs[:n], refs[n:2 * n]
        send_sems, recv_sems = refs[2 * n:]
        x, y, c = _place()
        copies = [pltpu.make_async_remote_copy(
            src_ref=ins[a].at[1 - c], dst_ref=outs[a], send_sem=send_sems.at[a], recv_sem=recv_sems.at[a],
            device_id=(x, y, 1 - c), device_id_type=MESH) for a in range(n)]
        for cp in copies:
            cp.start()
        for cp in copies:
            cp.wait()

    return pl.pallas_call(
        body, name="grads_to_sibling",
        in_specs=[ANY] * n, out_specs=[ANY] * n,
        out_shape=[jax.ShapeDtypeStruct(g.shape[1:], g.dtype) for g in grads],
        scratch_shapes=[pltpu.SemaphoreType.DMA((n,)), pltpu.SemaphoreType.DMA((n,))],
        compiler_params=pltpu.CompilerParams(has_side_effects=True),
    )(*grads)


def _pair_sum(grad, recv, name):
    _, nchip, R, C = grad.shape
    tr = R
    for cand in (512, 256, 128, 64, 32, 16):
        if R % cand == 0 and cand * C * 2 <= (2 << 20):
            tr = cand
            break

    def body(c_ref, g_ref, r_ref, o_ref):
        o_ref[...] = (g_ref[...].astype(F32) + r_ref[...].astype(F32)).astype(BF16)

    return pl.pallas_call(
        body, name=name,
        grid_spec=pltpu.PrefetchScalarGridSpec(
            num_scalar_prefetch=1, grid=(nchip, R // tr),
            in_specs=[pl.BlockSpec((None, None, tr, C), lambda j, i, c_ref: (c_ref[0], j, i, 0)),
                      pl.BlockSpec((None, tr, C), lambda j, i, c_ref: (j, i, 0))],
            out_specs=pl.BlockSpec((None, tr, C), lambda j, i, c_ref: (j, i, 0))),
        out_shape=jax.ShapeDtypeStruct((nchip, R, C), BF16),
        compiler_params=_params(("parallel", "parallel")),
    )(lax.axis_index("c").reshape(1).astype(jnp.int32), grad, recv)


def _rs_chips(parts):
    n = len(parts)

    def body(*refs):
        ins, outs = refs[:n], refs[n:2 * n]
        send_sems, recv_sems, local_sems = refs[2 * n:]
        x, y, c = _place()
        my_chip = 2 * x + y
        others = [(1 - x, y), (x, 1 - y), (1 - x, 1 - y)]
        copies, mine = [], []
        for a in range(n):
            cp = pltpu.make_async_copy(ins[a].at[my_chip], outs[a].at[my_chip], local_sems.at[a])
            cp.start()
            mine.append(cp)
            for k, (ox, oy) in enumerate(others):
                copies.append(pltpu.make_async_remote_copy(
                    src_ref=ins[a].at[2 * ox + oy], dst_ref=outs[a].at[my_chip],
                    send_sem=send_sems.at[3 * a + k], recv_sem=recv_sems.at[3 * a + k],
                    device_id=(ox, oy, c), device_id_type=MESH))
        for cp in copies:
            cp.start()
        for cp in copies:
            cp.wait()
        for cp in mine:
            cp.wait()

    return pl.pallas_call(
        body, name="grads_to_chips",
        in_specs=[ANY] * n, out_specs=[ANY] * n,
        out_shape=[jax.ShapeDtypeStruct(p.shape, p.dtype) for p in parts],
        scratch_shapes=[pltpu.SemaphoreType.DMA((3 * n,)), pltpu.SemaphoreType.DMA((3 * n,)),
                        pltpu.SemaphoreType.DMA((n,))],
        compiler_params=pltpu.CompilerParams(has_side_effects=True),
    )(*parts)


SMALL_ROWS = 544


def _all_reduce_small(buf):
    def body(in_ref, out_ref, recv, acc, send1, recv1, send2, recv2):
        x, y, c = _place()
        me = 4 * x + 2 * y + c
        peers = [(x ^ (r >> 2), y ^ ((r >> 1) & 1), c ^ (r & 1)) for r in range(1, N_DEV)]

        def idx(p):
            return 4 * p[0] + 2 * p[1] + p[2]

        first = [pltpu.make_async_remote_copy(
            src_ref=in_ref.at[idx(p)], dst_ref=recv.at[me], send_sem=send1.at[r], recv_sem=recv1.at[r],
            device_id=p, device_id_type=MESH) for r, p in enumerate(peers)]
        for cp in first:
            cp.start()
        recv[me] = in_ref[me]
        for r, p in enumerate(peers):
            pltpu.make_async_remote_copy(
                src_ref=in_ref.at[idx(p)], dst_ref=recv.at[idx(p)], send_sem=send1.at[r], recv_sem=recv1.at[r],
                device_id=p, device_id_type=MESH).wait_recv()
        total = recv[0]
        for k in range(1, N_DEV):
            total = total + recv[k]
        acc[...] = total
        out_ref[me] = total
        second = [pltpu.make_async_remote_copy(
            src_ref=acc, dst_ref=out_ref.at[me], send_sem=send2.at[r], recv_sem=recv2.at[r],
            device_id=p, device_id_type=MESH) for r, p in enumerate(peers)]
        for cp in second:
            cp.start()
        for r, p in enumerate(peers):
            pltpu.make_async_remote_copy(
                src_ref=acc, dst_ref=out_ref.at[idx(p)], send_sem=send2.at[r], recv_sem=recv2.at[r],
                device_id=p, device_id_type=MESH).wait_recv()
        for cp in first + second:
            cp.wait_send()

    vm = pl.BlockSpec(memory_space=pltpu.VMEM)
    return pl.pallas_call(
        body, name="small_grads_all_reduce",
        in_specs=[vm], out_specs=vm,
        out_shape=jax.ShapeDtypeStruct(buf.shape, F32),
        scratch_shapes=[pltpu.VMEM(buf.shape, F32), pltpu.VMEM(buf.shape[1:], F32),
                        pltpu.SemaphoreType.DMA((7,)), pltpu.SemaphoreType.DMA((7,)),
                        pltpu.SemaphoreType.DMA((7,)), pltpu.SemaphoreType.DMA((7,))],
        compiler_params=pltpu.CompilerParams(has_side_effects=True, vmem_limit_bytes=VMEM_LIMIT),
    )(buf)


def _dilate(a, d):
    if d == 1:
        return a
    S, C = a.shape
    return a.reshape(S // d, d, C).transpose(1, 0, 2).reshape(S, C)


def _undilate(a, d):
    if d == 1:
        return a
    S, C = a.shape
    return a.reshape(d, S // d, C).transpose(1, 0, 2).reshape(S, C)


def _cols(a, cb, n=1):
    return a[:, cb * WIDTH:(cb + n) * WIDTH]


def _to_blocks(g, kind):
    if kind == "cols":
        R = g.shape[0]
        return g.reshape(R, N_CHIP, 2, -1).transpose(2, 1, 0, 3)
    if kind == "rows":
        C = g.shape[1]
        return g.reshape(N_CHIP, 2, -1, C).transpose(1, 0, 2, 3)
    return g.reshape(4 * WIDTH, N_CHIP, 2, -1).transpose(2, 1, 0, 3)


SMALL = ("norm_g", "gm_ln_g", "gm_ln_b", "gm_ws", "gm_bs", "pool_w", "pool_scale", "mem_norm_g", "final_norm_g")


def _pack_small(tree):
    flat = jnp.concatenate([tree[k].reshape(-1, 128) for k in SMALL], axis=0)
    return jnp.pad(flat, ((0, N_DEV * SMALL_ROWS - flat.shape[0]), (0, 0)))


def _unpack_small(flat, like):
    out, at = {}, 0
    for k in SMALL:
        rows = like[k].size // 128
        out[k] = flat[at:at + rows].reshape(like[k].shape)
        at += rows
    return out


def _make_layer(win, wkv, wb, wout, norm_g, mem_norm_g, ln_g, ln_b, gm_ws, gm_bs, pool_w, pool_scale):
    tril = jnp.tril(jnp.ones((CHUNK, CHUNK), bool))
    wsm = jnp.where(tril, gm_ws, 0.0).astype(BF16)
    pw = pool_w.astype(BF16)
    return dict(win=win, wkv=wkv, wb=wb, wout=wout, g=norm_g[None], mg=mem_norm_g[None], ln_g=ln_g[None],
                ln_b=ln_b[None], wsm=wsm, wsm_t=wsm.transpose(0, 2, 1), pw=pw, pw_t=pw.transpose(0, 2, 1),
                ps=pool_scale[None], bias=jnp.repeat(gm_bs.T, HEAD, axis=1))


def _layer_fwd(xl, mem0, L):
    S = xl.shape[0]
    proj, h = _in_proj(xl, L["g"], L["win"])
    kv, mem_n = _mem_kv(mem0, L["mg"], L["wkv"])
    y3 = _abm_fwd(proj, L["ln_g"], L["ln_b"], L["wsm"], L["bias"], L["pw"], L["ps"], kv)
    o_g, l_g = [], []
    for gi, d in enumerate(DILATIONS):
        if d == 1:
            o, lse = _attn_fwd(proj, CB_Q0, proj, CB_K, proj, CB_CV, S // CHUNK)
        else:
            o, lse = _attn_fwd(_dilate(_cols(proj, CB_Q0 + gi), d), 0, _dilate(_cols(proj, CB_K), d), 0,
                               _dilate(_cols(proj, CB_CV), d), 0, S // d // CHUNK)
        o_g.append(_undilate(o, d))
        l_g.append(_undilate(lse, d))
    xn, yc, oc, lse, z = _merge_fwd(xl, y3, o_g, l_g, proj, L["wb"], L["wout"])
    return xn, dict(x=xl, proj=proj, h=h, kv=kv, mem_n=mem_n, y3=y3, yc=yc, oc=oc, lse=lse, z=z)


def _layer_bwd(dx, mem0, L, sv):
    S = dx.shape[0]
    proj = sv["proj"]
    dy3, doc, delta, dcg, dgm, dt = _merge_bwd(dx, sv["y3"], sv["yc"], sv["oc"], proj, L["wb"], L["wout"])
    dwout = _matmul_tn(sv["z"], dx, D_MODEL, "dw_out")
    ys = (sv["y3"][0], sv["y3"][1], sv["yc"], sv["y3"][2])
    dwb = jnp.stack([_matmul_tn(ys[b], dt[b], D_MODEL, "dw_branch") for b in range(4)])
    dab, dm, dlng, dlnb, dws, dbias, dpw, dps, dkv = _abm_bwd(
        proj, dy3, L["ln_g"], L["ln_b"], L["wsm"], L["wsm_t"], L["bias"], L["pw"], L["pw_t"], L["ps"], sv["kv"])
    dq, dk, dv = [], None, None
    for gi, d in enumerate(DILATIONS):
        if d == 1:
            r = _attn_bwd(proj, CB_Q0, proj, CB_K, proj, CB_CV, doc, sv["lse"], delta, S // CHUNK)
        else:
            r = _attn_bwd(_dilate(_cols(proj, CB_Q0 + gi), d), 0, _dilate(_cols(proj, CB_K), d), 0,
                          _dilate(_cols(proj, CB_CV), d), 0, _dilate(doc, d), _dilate(sv["lse"], d),
                          _dilate(delta, d), S // d // CHUNK)
        dq.append(_undilate(r[0], d))
        dkg, dvg = _undilate(r[1], d).astype(F32), _undilate(r[2], d).astype(F32)
        dk = dkg if dk is None else dk + dkg
        dv = dvg if dv is None else dv + dvg
    dproj = jnp.concatenate([dab, dq[0], dq[1], dq[2], dk.astype(BF16), dv.astype(BF16), dcg, dm, dgm], axis=1)
    dwkv, dmg = _mem_bwd(mem0, L["mg"], sv["mem_n"], L["wkv"], dkv)
    dwin = _matmul_tn(sv["h"], dproj, 1536, "dw_in")
    dxi, dng = _dh_bwd(dproj, L["win"], sv["x"], L["g"], dx)
    big = dict(w_in=dwin, w_mem_kv=dwkv, w_branch=dwb, w_out=dwout)
    small = dict(norm_g=dng[0], gm_ln_g=dlng[0], gm_ln_b=dlnb[0], gm_ws=dws,
                 gm_bs=_bias_reduce(dbias)[:, :N_HEAD].T, pool_w=dpw, pool_scale=dps[0], mem_norm_g=dmg[0])
    return dxi, big, small


def kernel(x, mem, norm_g, w_in, gm_ln_g, gm_ln_b, gm_ws, gm_bs, pool_w, pool_scale, mem_norm_g, w_mem_kv, w_branch, w_out, final_norm_g, loss_target, m_norm_g, m_w_in, m_gm_ln_g, m_gm_ln_b, m_gm_ws, m_gm_bs, m_pool_w, m_pool_scale, m_mem_norm_g, m_w_mem_kv, m_w_branch, m_w_out, m_final_norm_g, v_norm_g, v_w_in, v_gm_ln_g, v_gm_ln_b, v_gm_ws, v_gm_bs, v_pool_w, v_pool_scale, v_mem_norm_g, v_w_mem_kv, v_w_branch, v_w_out, v_final_norm_g):
    x0 = x[0]
    mem0 = mem[0]
    tgt = loss_target[0]
    S = x0.shape[0]

    win_g, wkv_g, wb_g, wout_g = _all_gather(
        [w_in.astype(BF16), w_mem_kv.astype(BF16), w_branch.astype(BF16), w_out.astype(BF16)])

    layers = [_make_layer(win_g[:, l].transpose(1, 0, 2).reshape(D_MODEL, D_IN),
                          wkv_g[:, l].reshape(D_MODEL, 2 * WIDTH),
                          wb_g[:, l].transpose(1, 2, 0, 3).reshape(4, WIDTH, D_MODEL),
                          wout_g[:, l].reshape(D_MODEL, D_MODEL),
                          norm_g[l], mem_norm_g[l], gm_ln_g[l], gm_ln_b[l], gm_ws[l], gm_bs[l], pool_w[l],
                          pool_scale[l]) for l in range(DEPTH)]

    saved = []
    xl = x0
    for l in range(DEPTH):
        xl, sv = _layer_fwd(xl, mem0, layers[l])
        saved.append(sv)

    loss_part, dx, d_final = _loss_head(xl, final_norm_g[None], tgt)
    loss = lax.psum(loss_part[0, 0], ("x", "y", "c"))

    big = {k: [None] * DEPTH for k in ("w_in", "w_mem_kv", "w_branch", "w_out")}
    small = {k: [None] * DEPTH for k in SMALL if k != "final_norm_g"}
    for l in reversed(range(DEPTH)):
        dx, gb, gs = _layer_bwd(dx, mem0, layers[l], saved[l])
        big["w_in"][l] = _to_blocks(gb["w_in"], "cols")
        big["w_mem_kv"][l] = _to_blocks(gb["w_mem_kv"], "rows")
        big["w_branch"][l] = _to_blocks(gb["w_branch"], "branch")
        big["w_out"][l] = _to_blocks(gb["w_out"], "rows")
        for k in gs:
            small[k][l] = gs[k]
    grad_x = dx[None]

    names = ("w_in", "w_mem_kv", "w_branch", "w_out")
    grads = [jnp.concatenate(big[k], axis=2) for k in names]
    from_sibling = _rs_sibling(grads)
    pair = [_pair_sum(g, r, "pair_sum_" + k) for g, r, k in zip(grads, from_sibling, names)]
    parts = _rs_chips(pair)

    small_tree = {k: jnp.stack(small[k]) for k in small}
    small_tree["final_norm_g"] = d_final[0]
    reduced = _all_reduce_small(_pack_small(small_tree).reshape(N_DEV, SMALL_ROWS, 128))

    weights = dict(norm_g=norm_g, w_in=w_in, gm_ln_g=gm_ln_g, gm_ln_b=gm_ln_b, gm_ws=gm_ws, gm_bs=gm_bs,
                   pool_w=pool_w, pool_scale=pool_scale, mem_norm_g=mem_norm_g, w_mem_kv=w_mem_kv,
                   w_branch=w_branch, w_out=w_out, final_norm_g=final_norm_g)
    m_in = dict(norm_g=m_norm_g, w_in=m_w_in, gm_ln_g=m_gm_ln_g, gm_ln_b=m_gm_ln_b, gm_ws=m_gm_ws, gm_bs=m_gm_bs,
                pool_w=m_pool_w, pool_scale=m_pool_scale, mem_norm_g=m_mem_norm_g, w_mem_kv=m_w_mem_kv,
                w_branch=m_w_branch, w_out=m_w_out, final_norm_g=m_final_norm_g)
    v_in = dict(norm_g=v_norm_g, w_in=v_w_in, gm_ln_g=v_gm_ln_g, gm_ln_b=v_gm_ln_b, gm_ws=v_gm_ws, gm_bs=v_gm_bs,
                pool_w=v_pool_w, pool_scale=v_pool_scale, mem_norm_g=v_mem_norm_g, w_mem_kv=v_w_mem_kv,
                w_branch=v_w_branch, w_out=v_w_out, final_norm_g=v_final_norm_g)
    res = {}
    for k, p in zip(names, parts):
        shape = weights[k].shape
        R, C = p.shape[1], p.shape[2]
        outs = _adamw(p, weights[k].reshape(R, C), m_in[k].reshape(R, C), v_in[k].reshape(R, C), "adamw_" + k)
        res[k] = [o.reshape(shape) for o in outs]
    outs = _adamw(reduced.reshape(1, N_DEV * SMALL_ROWS, 128), _pack_small(weights), _pack_small(m_in),
                  _pack_small(v_in), "adamw_small")
    unpacked = [_unpack_small(o, weights) for o in outs]
    for k in SMALL:
        res[k] = [u[k] for u in unpacked]

    order = ("norm_g", "w_in", "gm_ln_g", "gm_ln_b", "gm_ws", "gm_bs", "pool_w", "pool_scale", "mem_norm_g",
             "w_mem_kv", "w_branch", "w_out", "final_norm_g")
    return (loss, grad_x, *[res[k][0] for k in order], *[res[k][1] for k in order],
            *[res[k][2] for k in order], *[res[k][3] for k in order])
```

```python
import functools
import math

import jax
import jax.numpy as jnp
from jax import lax
from jax.experimental import pallas as pl
from jax.experimental.pallas import tpu as pltpu

F32 = jnp.float32
BF16 = jnp.bfloat16

D_MODEL = 1024
DEPTH = 4
WIDTH = 512
D_IN = 10752
HEAD = 128
N_HEAD = 4
CHUNK = 128
MEM_LEN = 256
POOL_WINDOWS = (2, 4, 8, 16)
DILATIONS = (1, 4, 16)
EPS = 1e-6
NEG = -1e30
ATT_SCALE = HEAD ** -0.5
N_DEV = 8
N_CHIP = 4

CB_U, CB_V, CB_AGATE, CB_PIN, CB_PGATE = 0, 1, 2, 3, 4
CB_Q0, CB_K, CB_CV, CB_CGATE, CB_MQ, CB_MGATE, CB_MERGE = 5, 8, 9, 10, 11, 12, 13

ADAM_LR = 0.001
ADAM_B1 = 0.9
ADAM_B2 = 0.999
ADAM_EPS = 1e-08
ADAM_WD = 0.01
ADAM_STEP = 10

VMEM_LIMIT = 56 * 1024 * 1024
MESH = pl.DeviceIdType.MESH
ANY = pl.BlockSpec(memory_space=pl.ANY)

NT = (((1,), (1,)), ((), ()))
TN = (((0,), (0,)), ((), ()))


def _dot(a, b):
    return jnp.dot(a, b, preferred_element_type=F32)


def _dot_nt(a, b):
    return lax.dot_general(a, b, NT, preferred_element_type=F32)


def _dot_tn(a, b):
    return lax.dot_general(a, b, TN, preferred_element_type=F32)


def _sigmoid(x):
    return 1.0 / (1.0 + jnp.exp(-x))


def _silu(x):
    return x * _sigmoid(x)


def _dsilu(x):
    s = _sigmoid(x)
    return s * (1.0 + x * (1.0 - s))


def _gelu(x):
    return 0.5 * x * (1.0 + lax.erf(x * (2.0 ** -0.5)))


def _dgelu(x):
    return 0.5 * (1.0 + lax.erf(x * (2.0 ** -0.5))) + x * jnp.exp(-0.5 * x * x) * (1.0 / math.sqrt(2.0 * math.pi))


def _col(blk, h):
    lane = lax.broadcasted_iota(jnp.int32, blk.shape, 1)
    return jnp.sum(jnp.where(lane == h, blk, 0.0), axis=1, keepdims=True)


def _put_cols(cols):
    rows = cols[0].shape[0]
    lane = lax.broadcasted_iota(jnp.int32, (rows, 128), 1)
    out = jnp.zeros((rows, 128), F32)
    for h, cv in enumerate(cols):
        out = jnp.where(lane == h, cv, out)
    return out


def _params(sem, vmem=VMEM_LIMIT):
    return pltpu.CompilerParams(dimension_semantics=sem, vmem_limit_bytes=vmem)


def _full(shape):
    nd = len(shape)
    return pl.BlockSpec(shape, lambda *_: (0,) * nd)


def _rows(tm, width, cb=0):
    return pl.BlockSpec((tm, width), lambda i: (i, cb))


def _in_proj(x, g, w, shards=()):
    S = x.shape[0]
    tm, tn = 1024, 1536
    n = len(shards)
    ni, nj = S // tm, D_IN // tn

    def body(x_ref, g_ref, w_ref, *rest):
        ins, (proj_ref, h_ref), outs = rest[:n], rest[n:n + 2], rest[n + 2:2 * n + 2]
        hs, sems = rest[2 * n + 2], rest[2 * n + 3:]
        i, j = pl.program_id(0), pl.program_id(1)

        if n:
            @pl.when(jnp.logical_and(i == 0, j == 0))
            def _():
                _comm_start(_ag_first(ins, outs, *sems))

        @pl.when(j == 0)
        def _():
            xf = x_ref[...]
            r = lax.rsqrt(jnp.mean(xf * xf, axis=-1, keepdims=True) + EPS)
            h = (xf * r * g_ref[...]).astype(BF16)
            hs[...] = h
            h_ref[...] = h

        proj_ref[...] = _dot(hs[...], w_ref[...]).astype(BF16)

        if n:
            @pl.when(jnp.logical_and(i == ni - 1, j == nj - 1))
            def _():
                _comm_wait(_ag_first(ins, outs, *sems))

    res = pl.pallas_call(
        body, name="in_proj_gather" if n else "in_proj",
        grid=(ni, nj),
        in_specs=[pl.BlockSpec((tm, D_MODEL), lambda i, j: (i, 0)),
                  pl.BlockSpec((1, D_MODEL), lambda i, j: (0, 0)),
                  pl.BlockSpec((D_MODEL, tn), lambda i, j: (0, j))] + [ANY] * n,
        out_specs=[pl.BlockSpec((tm, tn), lambda i, j: (i, j)),
                   pl.BlockSpec((tm, D_MODEL), lambda i, j: (i, 0))] + [ANY] * n,
        out_shape=[jax.ShapeDtypeStruct((S, D_IN), BF16), jax.ShapeDtypeStruct((S, D_MODEL), BF16)]
                  + [jax.ShapeDtypeStruct((N_DEV,) + s.shape, s.dtype) for s in shards],
        scratch_shapes=[pltpu.VMEM((tm, D_MODEL), BF16)] + (_dma_sems(4 * n, 4 * n, n) if n else []),
        compiler_params=_params(("arbitrary", "arbitrary")),
    )(x, g, w, *shards)
    return res[0], res[1], list(res[2:])


def _mem_kv(mem, g, w):
    M = mem.shape[0]

    def body(m_ref, g_ref, w_ref, kv_ref, mn_ref):
        xf = m_ref[...]
        r = lax.rsqrt(jnp.mean(xf * xf, axis=-1, keepdims=True) + EPS)
        mn = (xf * r * g_ref[...]).astype(BF16)
        mn_ref[...] = mn
        kv_ref[...] = _dot(mn, w_ref[...]).astype(BF16)

    return pl.pallas_call(
        body, name="mem_kv",
        out_shape=[jax.ShapeDtypeStruct((M, 2 * WIDTH), BF16), jax.ShapeDtypeStruct((M, D_MODEL), BF16)],
        compiler_params=pltpu.CompilerParams(vmem_limit_bytes=VMEM_LIMIT),
    )(mem, g, w)


def _band_masks(win):
    t = lax.broadcasted_iota(jnp.int32, (CHUNK, CHUNK), 0)
    s = lax.broadcasted_iota(jnp.int32, (CHUNK, CHUNK), 1)
    cur = jnp.logical_and(t - s >= 0, t - s < win)
    prev = s > t + CHUNK - win
    return cur.astype(BF16), prev.astype(BF16)


def _inv_count(first_row, win):
    t = first_row + lax.broadcasted_iota(jnp.int32, (CHUNK, 1), 0)
    return 1.0 / jnp.minimum(t + 1, win).astype(F32)


def _layer_norm_fwd(v):
    mu = jnp.mean(v, axis=-1, keepdims=True)
    vc = v - mu
    var = jnp.mean(vc * vc, axis=-1, keepdims=True)
    rstd = lax.rsqrt(var + EPS)
    return vc * rstd, rstd


def _mem_softmax(q, kmem):
    s = _dot_nt(q, kmem) * ATT_SCALE
    m = jnp.max(s, axis=-1, keepdims=True)
    e = jnp.exp(s - m)
    return e * (1.0 / jnp.sum(e, axis=-1, keepdims=True))


def _abm_fwd(proj, ln_g, ln_b, wsm, bias_full, pool_w, pool_scale, kv, gathered=()):
    S = proj.shape[0]
    tm = 512
    nchunk = tm // CHUNK
    n = len(gathered)
    nsteps = S // tm

    def body(u_ref, v_ref, ag_ref, p_ref, ph_ref, pg_ref, mq_ref, mg_ref, lng_ref, lnb_ref, wsm_ref, bias_ref,
             pw_ref, ps_ref, kv_ref, *rest):
        y_ref, bufs, mix, sems = rest[n], rest[n + 1:2 * n + 1], rest[2 * n + 1], rest[2 * n + 2:]
        i = pl.program_id(0)

        if n:
            @pl.when(i == 0)
            def _():
                _comm_start(_ag_second(bufs, *sems))

        u = _gelu(u_ref[...].astype(F32))
        v = _gelu(v_ref[...].astype(F32))
        vhat, _ = _layer_norm_fwd(v)
        vln = (vhat * lng_ref[...] + lnb_ref[...]).astype(BF16)
        for c in range(nchunk):
            for h in range(N_HEAD):
                rs, cs = slice(c * CHUNK, (c + 1) * CHUNK), slice(h * HEAD, (h + 1) * HEAD)
                mix[rs, cs] = _dot(wsm_ref[h], vln[rs, cs]) + bias_ref[:, cs]
        y_ref[0] = (u * mix[...] * _silu(ag_ref[...].astype(F32))).astype(BF16)
        halo_ok = (i > 0).astype(F32)
        for c in range(nchunk):
            rs = slice(c * CHUNK, (c + 1) * CHUNK)
            for g, win in enumerate(POOL_WINDOWS):
                cs = slice(g * HEAD, (g + 1) * HEAD)
                bcur, bprev = _band_masks(win)
                cur = p_ref[rs, cs]
                if c == 0:
                    prev = (ph_ref[:, cs].astype(F32) * halo_ok).astype(BF16)
                else:
                    prev = p_ref[(c - 1) * CHUNK:c * CHUNK, cs]
                sums = _dot(bcur, cur) + _dot(bprev, prev)
                dm = sums * _inv_count(i * tm + c * CHUNK, win) - cur.astype(F32)
                mix[rs, cs] = _dot(dm.astype(BF16), pw_ref[g])
        y_ref[1] = (mix[...] * ps_ref[...] * _silu(pg_ref[...].astype(F32))).astype(BF16)
        for h in range(N_HEAD):
            cs = slice(h * HEAD, (h + 1) * HEAD)
            p = _mem_softmax(mq_ref[:, cs], kv_ref[:, cs])
            mix[:, cs] = _dot(p.astype(BF16), kv_ref[:, WIDTH + h * HEAD:WIDTH + (h + 1) * HEAD])
        y_ref[2] = (mix[...] * _silu(mg_ref[...].astype(F32))).astype(BF16)

        if n:
            @pl.when(i == nsteps - 1)
            def _():
                _comm_wait(_ag_second(bufs, *sems))

    blk = tm // CHUNK
    res = pl.pallas_call(
        body, name="abm_fwd_gather" if n else "abm_fwd",
        grid=(nsteps,),
        in_specs=[_rows(tm, WIDTH, CB_U), _rows(tm, WIDTH, CB_V), _rows(tm, WIDTH, CB_AGATE),
                  _rows(tm, WIDTH, CB_PIN),
                  pl.BlockSpec((CHUNK, WIDTH), lambda i: (jnp.maximum(i * blk - 1, 0), CB_PIN)),
                  _rows(tm, WIDTH, CB_PGATE), _rows(tm, WIDTH, CB_MQ), _rows(tm, WIDTH, CB_MGATE),
                  _full((1, WIDTH)), _full((1, WIDTH)), _full((N_HEAD, CHUNK, CHUNK)), _full((CHUNK, WIDTH)),
                  _full((4, HEAD, HEAD)), _full((1, WIDTH)), _full((MEM_LEN, 2 * WIDTH))] + [ANY] * n,
        out_specs=[pl.BlockSpec((3, tm, WIDTH), lambda i: (0, i, 0))] + [ANY] * n,
        out_shape=[jax.ShapeDtypeStruct((3, S, WIDTH), BF16)]
                  + [jax.ShapeDtypeStruct(b.shape, b.dtype) for b in gathered],
        input_output_aliases={15 + a: 1 + a for a in range(n)},
        scratch_shapes=[pltpu.VMEM((tm, WIDTH), F32)] + (_dma_sems(3 * n, 3 * n) if n else []),
        compiler_params=_params(("arbitrary",)),
    )(proj, proj, proj, proj, proj, proj, proj, proj, ln_g, ln_b, wsm, bias_full, pool_w, pool_scale, kv, *gathered)
    return res[0], list(res[1:])


ATT_TILE = 512


def _attn_fwd(q, qcb, k, kcb, v, vcb, bps):
    S = q.shape[0]
    tm = ATT_TILE
    nb = tm // CHUNK

    nblocks = nb * N_HEAD

    def body(q_ref, k_ref, v_ref, kh_ref, vh_ref, o_ref, l_ref, sc_s, sp_s, pc_s, pp_s):
        i = pl.program_id(0)

        def prev_kv(n, cs):
            if n == 0:
                return kh_ref[:, cs], vh_ref[:, cs]
            ps = slice((n - 1) * CHUNK, n * CHUNK)
            return k_ref[ps, cs], v_ref[ps, cs]

        pens = []
        for n in range(nb):
            rs = slice(n * CHUNK, (n + 1) * CHUNK)
            pens.append(jnp.full((N_HEAD * CHUNK, 1), jnp.where((i * nb + n) % bps != 0, 0.0, NEG), F32))
            for h in range(N_HEAD):
                cs = slice(h * HEAD, (h + 1) * HEAD)
                bs = slice((n * N_HEAD + h) * CHUNK, (n * N_HEAD + h + 1) * CHUNK)
                qh = q_ref[rs, cs]
                sc_s[bs, :] = _dot_nt(qh, k_ref[rs, cs])
                sp_s[bs, :] = _dot_nt(qh, prev_kv(n, cs)[0])
        row = lax.broadcasted_iota(jnp.int32, (nblocks * CHUNK, CHUNK), 0) & (CHUNK - 1)
        col = lax.broadcasted_iota(jnp.int32, (nblocks * CHUNK, CHUNK), 1)
        sc = jnp.where(col <= row, sc_s[...] * ATT_SCALE, NEG)
        sp = jnp.where(col >= row, sp_s[...] * ATT_SCALE, NEG) + jnp.concatenate(pens, axis=0)
        m = jnp.maximum(jnp.max(sc, axis=-1, keepdims=True), jnp.max(sp, axis=-1, keepdims=True))
        ec = jnp.exp(sc - m)
        ep = jnp.exp(sp - m)
        den = jnp.sum(ec, axis=-1, keepdims=True) + jnp.sum(ep, axis=-1, keepdims=True)
        inv = 1.0 / den
        pc_s[...] = (ec * inv).astype(BF16)
        pp_s[...] = (ep * inv).astype(BF16)
        lse = m + jnp.log(den)
        for n in range(nb):
            rs = slice(n * CHUNK, (n + 1) * CHUNK)
            for h in range(N_HEAD):
                cs = slice(h * HEAD, (h + 1) * HEAD)
                bs = slice((n * N_HEAD + h) * CHUNK, (n * N_HEAD + h + 1) * CHUNK)
                o = _dot(pc_s[bs, :], v_ref[rs, cs]) + _dot(pp_s[bs, :], prev_kv(n, cs)[1])
                o_ref[rs, cs] = o.astype(BF16)
            l_ref[rs, :] = _put_cols([lse[(n * N_HEAD + h) * CHUNK:(n * N_HEAD + h + 1) * CHUNK]
                                      for h in range(N_HEAD)])

    def halo(cb):
        return pl.BlockSpec((CHUNK, WIDTH), lambda i: (jnp.maximum(i * nb - 1, 0), cb))

    return pl.pallas_call(
        body, name=f"attn_fwd_{bps}",
        grid=(S // tm,),
        in_specs=[_rows(tm, WIDTH, qcb), _rows(tm, WIDTH, kcb), _rows(tm, WIDTH, vcb), halo(kcb), halo(vcb)],
        out_specs=[_rows(tm, WIDTH), _rows(tm, 128)],
        out_shape=[jax.ShapeDtypeStruct((S, WIDTH), BF16), jax.ShapeDtypeStruct((S, 128), F32)],
        scratch_shapes=[pltpu.VMEM((nblocks * CHUNK, CHUNK), F32), pltpu.VMEM((nblocks * CHUNK, CHUNK), F32),
                        pltpu.VMEM((nblocks * CHUNK, CHUNK), BF16), pltpu.VMEM((nblocks * CHUNK, CHUNK), BF16)],
        compiler_params=_params(("parallel",)),
    )(q, k, v, k, v)


def _merge_specs(tm):
    return [_rows(tm, WIDTH, CB_MERGE + j) for j in range(8)]


def _merge_fwd(x, y3, o_g, l_g, proj, wb, wout):
    S = x.shape[0]
    tm = 256

    def body(x_ref, y_ref, o0, o1, o2, l0, l1, l2, cg_ref, *rest):
        gm = rest[:8]
        wb_ref, wo_ref, xn_ref, yc_ref, oc_ref, lse_ref, z_ref, ocs = rest[8:]
        lcols = []
        for h in range(N_HEAD):
            cs = slice(h * HEAD, (h + 1) * HEAD)
            ls = [_col(l[...], h) for l in (l0, l1, l2)]
            m = jnp.maximum(jnp.maximum(ls[0], ls[1]), ls[2])
            tot = jnp.exp(ls[0] - m) + jnp.exp(ls[1] - m) + jnp.exp(ls[2] - m)
            lse = m + jnp.log(tot)
            ocs[:, cs] = sum(jnp.exp(lg - lse) * o[:, cs].astype(F32) for lg, o in zip(ls, (o0, o1, o2)))
            lcols.append(lse)
        lse_ref[...] = _put_cols(lcols)
        oc = ocs[...]
        oc_ref[...] = oc.astype(BF16)
        yc = (oc * _silu(cg_ref[...].astype(F32))).astype(BF16)
        yc_ref[...] = yc
        ys = (y_ref[0], y_ref[1], yc, y_ref[2])
        z = jnp.zeros((tm, D_MODEL), F32)
        for b in range(4):
            gate = _sigmoid(jnp.concatenate([gm[2 * b][...], gm[2 * b + 1][...]], axis=1).astype(F32))
            z = z + gate * _dot(ys[b], wb_ref[b])
        zb = z.astype(BF16)
        z_ref[...] = zb
        xn_ref[...] = x_ref[...] + _dot(zb, wo_ref[...])

    return pl.pallas_call(
        body, name="merge_fwd",
        grid=(S // tm,),
        in_specs=[_rows(tm, D_MODEL), pl.BlockSpec((3, tm, WIDTH), lambda i: (0, i, 0)),
                  _rows(tm, WIDTH), _rows(tm, WIDTH), _rows(tm, WIDTH),
                  _rows(tm, 128), _rows(tm, 128), _rows(tm, 128),
                  _rows(tm, WIDTH, CB_CGATE)] + _merge_specs(tm)
                 + [_full((4, WIDTH, D_MODEL)), _full((D_MODEL, D_MODEL))],
        out_specs=[_rows(tm, D_MODEL), _rows(tm, WIDTH), _rows(tm, WIDTH), _rows(tm, 128), _rows(tm, D_MODEL)],
        out_shape=[jax.ShapeDtypeStruct((S, D_MODEL), F32), jax.ShapeDtypeStruct((S, WIDTH), BF16),
                   jax.ShapeDtypeStruct((S, WIDTH), BF16), jax.ShapeDtypeStruct((S, 128), F32),
                   jax.ShapeDtypeStruct((S, D_MODEL), BF16)],
        scratch_shapes=[pltpu.VMEM((tm, WIDTH), F32)],
        compiler_params=_params(("parallel",)),
    )(x, y3, *o_g, *l_g, proj, *([proj] * 8), wb, wout)


def _loss_head(x, g, tgt):
    S = x.shape[0]
    tm = 512

    def body(x_ref, g_ref, t_ref, loss_ref, dx_ref, dg_ref):
        @pl.when(pl.program_id(0) == 0)
        def _():
            loss_ref[...] = jnp.zeros_like(loss_ref)
            dg_ref[...] = jnp.zeros_like(dg_ref)

        xf = x_ref[...]
        r = lax.rsqrt(jnp.mean(xf * xf, axis=-1, keepdims=True) + EPS)
        xhat = xf * r
        gv = g_ref[...]
        err = xhat * gv - t_ref[...]
        e2 = jnp.sum(err * err, axis=-1, keepdims=True)
        loss_ref[...] += (0.5 / D_MODEL) * jnp.sum(e2, axis=0, keepdims=True)
        dy = err * (1.0 / D_MODEL)
        dg_ref[...] += jnp.sum(dy * xhat, axis=0, keepdims=True)
        dxh = dy * gv
        dx_ref[...] = r * (dxh - xhat * jnp.mean(dxh * xhat, axis=-1, keepdims=True))

    return pl.pallas_call(
        body, name="loss_head",
        grid=(S // tm,),
        in_specs=[_rows(tm, D_MODEL), _full((1, D_MODEL)), _rows(tm, D_MODEL)],
        out_specs=[_full((1, 128)), _rows(tm, D_MODEL), _full((1, D_MODEL))],
        out_shape=[jax.ShapeDtypeStruct((1, 128), F32), jax.ShapeDtypeStruct((S, D_MODEL), F32),
                   jax.ShapeDtypeStruct((1, D_MODEL), F32)],
        compiler_params=_params(("arbitrary",)),
    )(x, g, tgt)


def _merge_bwd(dxo, y3, yc, oc, proj, wb, wout, grads=()):
    S = dxo.shape[0]
    tm = 256
    n = len(grads)
    nsteps = S // tm

    def body(dx_ref, y_ref, yc_ref, oc_ref, cg_ref, *rest):
        gm = rest[:8]
        wb_ref, wo_ref = rest[8:10]
        g_in = rest[10:10 + n]
        dy_ref, doc_ref, delta_ref, dcg_ref, dgm_ref, dt_ref = rest[10 + n:16 + n]
        g_out, sems = rest[16 + n:16 + 2 * n], rest[16 + 2 * n:]
        i = pl.program_id(0)

        if n:
            @pl.when(i == 0)
            def _():
                _comm_start(_rs_first(g_in, g_out, *sems))

        dz = _dot_nt(dx_ref[...].astype(BF16), wo_ref[...])
        ys = (y_ref[0], y_ref[1], yc_ref[...], y_ref[2])
        for b in range(4):
            gate = _sigmoid(jnp.concatenate([gm[2 * b][...], gm[2 * b + 1][...]], axis=1).astype(F32))
            t = _dot(ys[b], wb_ref[b])
            dgm_ref[:, b * D_MODEL:(b + 1) * D_MODEL] = (dz * t * gate * (1.0 - gate)).astype(BF16)
            dt = (dz * gate).astype(BF16)
            dt_ref[b] = dt
            dyb = _dot_nt(dt, wb_ref[b])
            if b == 2:
                cg = cg_ref[...].astype(F32)
                oc = oc_ref[...].astype(F32)
                doc = dyb * _silu(cg)
                dcg_ref[...] = (dyb * oc * _dsilu(cg)).astype(BF16)
                doc_ref[...] = doc.astype(BF16)
                prod = doc * oc
                delta_ref[...] = _put_cols([jnp.sum(prod[:, h * HEAD:(h + 1) * HEAD], axis=1, keepdims=True)
                                            for h in range(N_HEAD)])
            else:
                dy_ref[b if b < 2 else 2] = dyb.astype(BF16)

        if n:
            @pl.when(i == nsteps - 1)
            def _():
                _comm_wait(_rs_first(g_in, g_out, *sems))

    res = pl.pallas_call(
        body, name="merge_bwd_scatter" if n else "merge_bwd",
        grid=(nsteps,),
        in_specs=[_rows(tm, D_MODEL), pl.BlockSpec((3, tm, WIDTH), lambda i: (0, i, 0)),
                  _rows(tm, WIDTH), _rows(tm, WIDTH), _rows(tm, WIDTH, CB_CGATE)] + _merge_specs(tm)
                 + [_full((4, WIDTH, D_MODEL)), _full((D_MODEL, D_MODEL))] + [ANY] * n,
        out_specs=[pl.BlockSpec((3, tm, WIDTH), lambda i: (0, i, 0)), _rows(tm, WIDTH), _rows(tm, 128),
                   _rows(tm, WIDTH), _rows(tm, 4 * D_MODEL), pl.BlockSpec((4, tm, D_MODEL), lambda i: (0, i, 0))]
                  + [ANY] * n,
        out_shape=[jax.ShapeDtypeStruct((3, S, WIDTH), BF16), jax.ShapeDtypeStruct((S, WIDTH), BF16),
                   jax.ShapeDtypeStruct((S, 128), F32), jax.ShapeDtypeStruct((S, WIDTH), BF16),
                   jax.ShapeDtypeStruct((S, 4 * D_MODEL), BF16), jax.ShapeDtypeStruct((4, S, D_MODEL), BF16)]
                  + [jax.ShapeDtypeStruct(g.shape[1:], g.dtype) for g in grads],
        scratch_shapes=_dma_sems(n, n) if n else [],
        compiler_params=_params(("arbitrary",)),
    )(dxo, y3, yc, oc, proj, *([proj] * 8), wb, wout, *grads)
    return res[:6], list(res[6:])


def _attn_bwd(q, qcb, k, kcb, v, vcb, do, lse, delta, bps):
    S = q.shape[0]
    tm = ATT_TILE
    nb = tm // CHUNK
    nblk = S // CHUNK

    ncur = nb * N_HEAD
    nprev = (nb + 1) * N_HEAD

    def body(q_ref, k_ref, v_ref, do_ref, l_ref, d_ref, kh_ref, vh_ref, qn_ref, don_ref, ln_ref, dn_ref,
             dq_ref, dk_ref, dv_ref, sc_s, sp_s, dpc_s, dpp_s, pc_s, pp_s, dsc_s, dsp_s):
        i = pl.program_id(0)

        def rows_of(n):
            if n < nb:
                rs = slice(n * CHUNK, (n + 1) * CHUNK)
                return rs, q_ref, do_ref, l_ref, d_ref
            return slice(0, CHUNK), qn_ref, don_ref, ln_ref, dn_ref

        def prev_kv(n, cs):
            if n == 0:
                return kh_ref[:, cs], vh_ref[:, cs]
            ps = slice((n - 1) * CHUNK, n * CHUNK)
            return k_ref[ps, cs], v_ref[ps, cs]

        def blk(n, h):
            return slice((n * N_HEAD + h) * CHUNK, (n * N_HEAD + h + 1) * CHUNK)

        pens, lses, deltas = [], [], []
        for n in range(nb + 1):
            rs, qr, dor, lr, dr = rows_of(n)
            gb = i * nb + n
            pen = jnp.where(gb % bps != 0, 0.0, NEG)
            if n == nb:
                pen = pen + jnp.where(gb < nblk, 0.0, NEG)
            pens.append(jnp.full((N_HEAD * CHUNK, 1), pen, F32))
            lblk, dblk = lr[rs, :], dr[rs, :]
            for h in range(N_HEAD):
                cs = slice(h * HEAD, (h + 1) * HEAD)
                qh, doh = qr[rs, cs], dor[rs, cs]
                lses.append(_col(lblk, h))
                deltas.append(_col(dblk, h))
                kp, vp = prev_kv(n, cs)
                sp_s[blk(n, h), :] = _dot_nt(qh, kp)
                dpp_s[blk(n, h), :] = _dot_nt(doh, vp)
                if n < nb:
                    sc_s[blk(n, h), :] = _dot_nt(qh, k_ref[rs, cs])
                    dpc_s[blk(n, h), :] = _dot_nt(doh, v_ref[rs, cs])
        lse = jnp.concatenate(lses, axis=0)
        delta = jnp.concatenate(deltas, axis=0)
        row = lax.broadcasted_iota(jnp.int32, (nprev * CHUNK, CHUNK), 0) & (CHUNK - 1)
        col = lax.broadcasted_iota(jnp.int32, (nprev * CHUNK, CHUNK), 1)
        sp = jnp.where(col >= row, sp_s[...] * ATT_SCALE, NEG) + jnp.concatenate(pens, axis=0)
        pp = jnp.exp(sp - lse)
        pp_s[...] = pp.astype(BF16)
        dsp_s[...] = (pp * (dpp_s[...] - delta)).astype(BF16)
        nc = ncur * CHUNK
        sc = jnp.where(col[:nc] <= row[:nc], sc_s[...] * ATT_SCALE, NEG)
        pc = jnp.exp(sc - lse[:nc])
        pc_s[...] = pc.astype(BF16)
        dsc_s[...] = (pc * (dpc_s[...] - delta[:nc])).astype(BF16)
        for n in range(nb):
            rs, qr, dor, _, _ = rows_of(n)
            rn, qnr, donr, _, _ = rows_of(n + 1)
            for h in range(N_HEAD):
                cs = slice(h * HEAD, (h + 1) * HEAD)
                kp, _ = prev_kv(n, cs)
                dq = _dot(dsc_s[blk(n, h), :], k_ref[rs, cs]) + _dot(dsp_s[blk(n, h), :], kp)
                dq_ref[rs, cs] = (dq * ATT_SCALE).astype(BF16)
                dk = _dot_tn(dsc_s[blk(n, h), :], qr[rs, cs]) + _dot_tn(dsp_s[blk(n + 1, h), :], qnr[rn, cs])
                dk_ref[rs, cs] = (dk * ATT_SCALE).astype(BF16)
                dv = _dot_tn(pc_s[blk(n, h), :], dor[rs, cs]) + _dot_tn(pp_s[blk(n + 1, h), :], donr[rn, cs])
                dv_ref[rs, cs] = dv.astype(BF16)

    def prev_halo(cb):
        return pl.BlockSpec((CHUNK, WIDTH), lambda i: (jnp.maximum(i * nb - 1, 0), cb))

    def next_halo(width, cb=0):
        return pl.BlockSpec((CHUNK, width), lambda i: (jnp.minimum(i * nb + nb, nblk - 1), cb))

    return pl.pallas_call(
        body, name=f"attn_bwd_{bps}",
        grid=(S // tm,),
        in_specs=[_rows(tm, WIDTH, qcb), _rows(tm, WIDTH, kcb), _rows(tm, WIDTH, vcb), _rows(tm, WIDTH),
                  _rows(tm, 128), _rows(tm, 128), prev_halo(kcb), prev_halo(vcb),
                  next_halo(WIDTH, qcb), next_halo(WIDTH), next_halo(128), next_halo(128)],
        out_specs=[_rows(tm, WIDTH), _rows(tm, WIDTH), _rows(tm, WIDTH)],
        out_shape=[jax.ShapeDtypeStruct((S, WIDTH), BF16)] * 3,
        scratch_shapes=[pltpu.VMEM((ncur * CHUNK, CHUNK), F32), pltpu.VMEM((nprev * CHUNK, CHUNK), F32),
                        pltpu.VMEM((ncur * CHUNK, CHUNK), F32), pltpu.VMEM((nprev * CHUNK, CHUNK), F32),
                        pltpu.VMEM((ncur * CHUNK, CHUNK), BF16), pltpu.VMEM((nprev * CHUNK, CHUNK), BF16),
                        pltpu.VMEM((ncur * CHUNK, CHUNK), BF16), pltpu.VMEM((nprev * CHUNK, CHUNK), BF16)],
        compiler_params=_params(("parallel",)),
    )(q, k, v, do, lse, delta, k, v, q, do, lse, delta)


def _abm_bwd(proj, dy3, ln_g, ln_b, wsm, wsm_t, bias_full, pool_w, pool_wt, pool_scale, kv):
    S = proj.shape[0]
    tm = 512
    nchunk = tm // CHUNK
    nblk = S // CHUNK

    def body(u_ref, v_ref, ag_ref, p_ref, ph_ref, pg_ref, pgn_ref, mq_ref, mg_ref, dy_ref, dypn_ref,
             lng_ref, lnb_ref, wsm_ref, wsmt_ref, bias_ref, pw_ref, pwt_ref, ps_ref, kv_ref,
             dab_ref, dm_ref, dlng_ref, dlnb_ref, dws_ref, dbias_ref, dpw_ref, dps_ref, dkv_ref,
             mix, dvl, ddn):
        i = pl.program_id(0)

        @pl.when(i == 0)
        def _():
            for r in (dlng_ref, dlnb_ref, dws_ref, dbias_ref, dpw_ref, dps_ref, dkv_ref):
                r[...] = jnp.zeros_like(r)

        au = u_ref[...].astype(F32)
        av = v_ref[...].astype(F32)
        ag = ag_ref[...].astype(F32)
        u = _gelu(au)
        v = _gelu(av)
        vhat, rstd = _layer_norm_fwd(v)
        vln = (vhat * lng_ref[...] + lnb_ref[...]).astype(BF16)
        for c in range(nchunk):
            for h in range(N_HEAD):
                rs, cs = slice(c * CHUNK, (c + 1) * CHUNK), slice(h * HEAD, (h + 1) * HEAD)
                mix[rs, cs] = _dot(wsm_ref[h], vln[rs, cs]) + bias_ref[:, cs]
        dya = dy_ref[0].astype(F32)
        sg = _silu(ag)
        mixed = mix[...]
        dab_ref[:, 2 * WIDTH:3 * WIDTH] = (dya * u * mixed * _dsilu(ag)).astype(BF16)
        dab_ref[:, 0:WIDTH] = (dya * mixed * sg * _dgelu(au)).astype(BF16)
        dmixed = dya * u * sg
        dmb = dmixed.astype(BF16)
        tril = (lax.broadcasted_iota(jnp.int32, (CHUNK, CHUNK), 1)
                <= lax.broadcasted_iota(jnp.int32, (CHUNK, CHUNK), 0))
        for c in range(nchunk):
            rs = slice(c * CHUNK, (c + 1) * CHUNK)
            dbias_ref[...] += dmixed[rs, :]
            for h in range(N_HEAD):
                cs = slice(h * HEAD, (h + 1) * HEAD)
                dvl[rs, cs] = _dot(wsmt_ref[h], dmb[rs, cs])
                dws_ref[h] += jnp.where(tril, _dot_nt(dmb[rs, cs], vln[rs, cs]), 0.0)
        dvln = dvl[...]
        dlng_ref[...] += jnp.sum(dvln * vhat, axis=0, keepdims=True)
        dlnb_ref[...] += jnp.sum(dvln, axis=0, keepdims=True)
        dvh = dvln * lng_ref[...]
        dv = rstd * (dvh - jnp.mean(dvh, axis=-1, keepdims=True)
                     - vhat * jnp.mean(dvh * vhat, axis=-1, keepdims=True))
        dab_ref[:, WIDTH:2 * WIDTH] = (dv * _dgelu(av)).astype(BF16)

        halo_ok = (i > 0).astype(F32)
        for c in range(nchunk):
            rs = slice(c * CHUNK, (c + 1) * CHUNK)
            for g, win in enumerate(POOL_WINDOWS):
                cs = slice(g * HEAD, (g + 1) * HEAD)
                bcur, bprev = _band_masks(win)
                cur = p_ref[rs, cs]
                if c == 0:
                    prev = (ph_ref[:, cs].astype(F32) * halo_ok).astype(BF16)
                else:
                    prev = p_ref[(c - 1) * CHUNK:c * CHUNK, cs]
                sums = _dot(bcur, cur) + _dot(bprev, prev)
                dvl[rs, cs] = sums * _inv_count(i * tm + c * CHUNK, win) - cur.astype(F32)
        dmat = dvl[...].astype(BF16)
        for g in range(4):
            cs = slice(g * HEAD, (g + 1) * HEAD)
            mix[:, cs] = _dot(dmat[:, cs], pw_ref[g])
        yg = mix[...]
        pg = pg_ref[...].astype(F32)
        dyp = dy_ref[1].astype(F32)
        dyy = dyp * _silu(pg)
        scale = ps_ref[...]
        dab_ref[:, 4 * WIDTH:5 * WIDTH] = (dyp * yg * scale * _dsilu(pg)).astype(BF16)
        dps_ref[...] += jnp.sum(dyy * yg, axis=0, keepdims=True)
        dyg = (dyy * scale).astype(BF16)
        for g in range(4):
            cs = slice(g * HEAD, (g + 1) * HEAD)
            dpw_ref[g] += _dot_tn(dmat[:, cs], dyg[:, cs])
            mix[:, cs] = _dot(dyg[:, cs], pwt_ref[g])
        next_ok = (i + 1 < S // tm).astype(F32)
        dygn = (dypn_ref[...].astype(F32) * _silu(pgn_ref[...].astype(F32)) * scale * next_ok).astype(BF16)
        for c in range(nchunk + 1):
            for g, win in enumerate(POOL_WINDOWS):
                cs = slice(g * HEAD, (g + 1) * HEAD)
                if c < nchunk:
                    dd = mix[c * CHUNK:(c + 1) * CHUNK, cs]
                else:
                    dd = _dot(dygn[:, cs], pwt_ref[g])
                ddn[c * CHUNK:(c + 1) * CHUNK, cs] = dd * _inv_count(i * tm + c * CHUNK, win)
        ddnb = ddn[...].astype(BF16)
        for c in range(nchunk):
            rs = slice(c * CHUNK, (c + 1) * CHUNK)
            ns = slice((c + 1) * CHUNK, (c + 2) * CHUNK)
            for g, win in enumerate(POOL_WINDOWS):
                cs = slice(g * HEAD, (g + 1) * HEAD)
                bcur, bprev = _band_masks(win)
                dp = _dot_tn(bcur, ddnb[rs, cs]) + _dot_tn(bprev, ddnb[ns, cs]) - mix[rs, cs]
                dab_ref[rs, 3 * WIDTH + g * HEAD:3 * WIDTH + (g + 1) * HEAD] = dp.astype(BF16)

        mg = mg_ref[...].astype(F32)
        dym = dy_ref[2].astype(F32)
        dob = (dym * _silu(mg)).astype(BF16)
        for h in range(N_HEAD):
            cs = slice(h * HEAD, (h + 1) * HEAD)
            vs = slice(WIDTH + h * HEAD, WIDTH + (h + 1) * HEAD)
            qh = mq_ref[:, cs]
            p = _mem_softmax(qh, kv_ref[:, cs])
            pb = p.astype(BF16)
            mix[:, cs] = _dot(pb, kv_ref[:, vs])
            dp = _dot_nt(dob[:, cs], kv_ref[:, vs])
            ds = (p * (dp - jnp.sum(p * dp, axis=-1, keepdims=True))).astype(BF16)
            dm_ref[:, cs] = (_dot(ds, kv_ref[:, cs]) * ATT_SCALE).astype(BF16)
            dkv_ref[:, cs] += _dot_tn(ds, qh) * ATT_SCALE
            dkv_ref[:, vs] += _dot_tn(pb, dob[:, cs])
        dm_ref[:, WIDTH:2 * WIDTH] = (dym * mix[...] * _dsilu(mg)).astype(BF16)

    blk = tm // CHUNK
    small = [_full((1, WIDTH)), _full((1, WIDTH)), _full((N_HEAD, CHUNK, CHUNK)), _full((CHUNK, WIDTH)),
             _full((4, HEAD, HEAD)), _full((1, WIDTH)), _full((MEM_LEN, 2 * WIDTH))]
    return pl.pallas_call(
        body, name="abm_bwd",
        grid=(S // tm,),
        in_specs=[_rows(tm, WIDTH, CB_U), _rows(tm, WIDTH, CB_V), _rows(tm, WIDTH, CB_AGATE),
                  _rows(tm, WIDTH, CB_PIN),
                  pl.BlockSpec((CHUNK, WIDTH), lambda i: (jnp.maximum(i * blk - 1, 0), CB_PIN)),
                  _rows(tm, WIDTH, CB_PGATE),
                  pl.BlockSpec((CHUNK, WIDTH), lambda i: (jnp.minimum(i * blk + blk, nblk - 1), CB_PGATE)),
                  _rows(tm, WIDTH, CB_MQ), _rows(tm, WIDTH, CB_MGATE),
                  pl.BlockSpec((3, tm, WIDTH), lambda i: (0, i, 0)),
                  pl.BlockSpec((None, CHUNK, WIDTH), lambda i: (1, jnp.minimum(i * blk + blk, nblk - 1), 0)),
                  _full((1, WIDTH)), _full((1, WIDTH)), _full((N_HEAD, CHUNK, CHUNK)), _full((N_HEAD, CHUNK, CHUNK)),
                  _full((CHUNK, WIDTH)), _full((4, HEAD, HEAD)), _full((4, HEAD, HEAD)), _full((1, WIDTH)),
                  _full((MEM_LEN, 2 * WIDTH))],
        out_specs=[_rows(tm, 5 * WIDTH), _rows(tm, 2 * WIDTH)] + small,
        out_shape=[jax.ShapeDtypeStruct((S, 5 * WIDTH), BF16), jax.ShapeDtypeStruct((S, 2 * WIDTH), BF16),
                   jax.ShapeDtypeStruct((1, WIDTH), F32), jax.ShapeDtypeStruct((1, WIDTH), F32),
                   jax.ShapeDtypeStruct((N_HEAD, CHUNK, CHUNK), F32), jax.ShapeDtypeStruct((CHUNK, WIDTH), F32),
                   jax.ShapeDtypeStruct((4, HEAD, HEAD), F32), jax.ShapeDtypeStruct((1, WIDTH), F32),
                   jax.ShapeDtypeStruct((MEM_LEN, 2 * WIDTH), F32)],
        scratch_shapes=[pltpu.VMEM((tm, WIDTH), F32), pltpu.VMEM((tm, WIDTH), F32),
                        pltpu.VMEM((tm + CHUNK, WIDTH), F32)],
        compiler_params=_params(("arbitrary",)),
    )(proj, proj, proj, proj, proj, proj, proj, proj, proj, dy3, dy3,
      ln_g, ln_b, wsm, wsm_t, bias_full, pool_w, pool_wt, pool_scale, kv)


def _bias_reduce(dbias_full):
    def body(d_ref, o_ref):
        d = d_ref[...]
        o_ref[...] = _put_cols([jnp.sum(d[:, h * HEAD:(h + 1) * HEAD], axis=1, keepdims=True) for h in range(N_HEAD)])

    return pl.pallas_call(body, name="bias_reduce", out_shape=jax.ShapeDtypeStruct((CHUNK, 128), F32))(dbias_full)


def _mem_bwd(mem, g, mem_n, w, dkv):
    def body(m_ref, g_ref, mn_ref, w_ref, dkv_ref, dw_ref, dg_ref):
        dkvb = dkv_ref[...].astype(BF16)
        dw_ref[...] = _dot_tn(mn_ref[...], dkvb).astype(BF16)
        dmn = _dot_nt(dkvb, w_ref[...])
        xf = m_ref[...]
        r = lax.rsqrt(jnp.mean(xf * xf, axis=-1, keepdims=True) + EPS)
        dg_ref[...] = jnp.sum(dmn * xf * r, axis=0, keepdims=True)

    return pl.pallas_call(
        body, name="mem_bwd",
        out_shape=[jax.ShapeDtypeStruct((D_MODEL, 2 * WIDTH), BF16), jax.ShapeDtypeStruct((1, D_MODEL), F32)],
        compiler_params=pltpu.CompilerParams(vmem_limit_bytes=VMEM_LIMIT),
    )(mem, g, mem_n, w, dkv)


def _dh_bwd(dproj, w, x, g, dxo, parts=()):
    S = x.shape[0]
    tm, tk = 1024, 1536
    nk = D_IN // tk
    ni = S // tm
    n = len(parts)

    def body(dp_ref, w_ref, x_ref, g_ref, dxo_ref, *rest):
        p_in = rest[:n]
        dx_ref, dg_ref = rest[n:n + 2]
        p_out, acc, sems = rest[n + 2:2 * n + 2], rest[2 * n + 2], rest[2 * n + 3:]
        i, kk = pl.program_id(0), pl.program_id(1)

        @pl.when(jnp.logical_and(i == 0, kk == 0))
        def _():
            dg_ref[...] = jnp.zeros_like(dg_ref)
            if n:
                _comm_start(_rs_second(p_in, p_out, *sems))

        @pl.when(kk == 0)
        def _():
            acc[...] = jnp.zeros_like(acc)

        acc[...] += _dot_nt(dp_ref[...], w_ref[...])

        @pl.when(kk == nk - 1)
        def _():
            xf = x_ref[...]
            r = lax.rsqrt(jnp.mean(xf * xf, axis=-1, keepdims=True) + EPS)
            xhat = xf * r
            dh = acc[...]
            dg_ref[...] += jnp.sum(dh * xhat, axis=0, keepdims=True)
            dxh = dh * g_ref[...]
            dx_ref[...] = dxo_ref[...] + r * (dxh - xhat * jnp.mean(dxh * xhat, axis=-1, keepdims=True))

        if n:
            @pl.when(jnp.logical_and(i == ni - 1, kk == nk - 1))
            def _():
                _comm_wait(_rs_second(p_in, p_out, *sems))

    res = pl.pallas_call(
        body, name="dh_bwd_scatter" if n else "dh_bwd",
        grid=(ni, nk),
        in_specs=[pl.BlockSpec((tm, tk), lambda i, k: (i, k)), pl.BlockSpec((D_MODEL, tk), lambda i, k: (0, k)),
                  pl.BlockSpec((tm, D_MODEL), lambda i, k: (i, 0)), pl.BlockSpec((1, D_MODEL), lambda i, k: (0, 0)),
                  pl.BlockSpec((tm, D_MODEL), lambda i, k: (i, 0))] + [ANY] * n,
        out_specs=[pl.BlockSpec((tm, D_MODEL), lambda i, k: (i, 0)), pl.BlockSpec((1, D_MODEL), lambda i, k: (0, 0))]
                  + [ANY] * n,
        out_shape=[jax.ShapeDtypeStruct((S, D_MODEL), F32), jax.ShapeDtypeStruct((1, D_MODEL), F32)]
                  + [jax.ShapeDtypeStruct(p.shape, p.dtype) for p in parts],
        scratch_shapes=[pltpu.VMEM((tm, D_MODEL), F32)] + (_dma_sems(3 * n, 3 * n, n) if n else []),
        compiler_params=_params(("arbitrary", "arbitrary")),
    )(dproj, w, x, g, dxo, *parts)
    return res[0], res[1], list(res[2:])


def _matmul_tn(a, b, tn, name):
    K, M = a.shape
    N = b.shape[1]
    tk = 1024
    nk = K // tk

    def body(a_ref, b_ref, o_ref, acc):
        kk = pl.program_id(1)

        @pl.when(kk == 0)
        def _():
            acc[...] = jnp.zeros_like(acc)

        acc[...] += _dot_tn(a_ref[...].astype(BF16), b_ref[...].astype(BF16))

        @pl.when(kk == nk - 1)
        def _():
            o_ref[...] = acc[...].astype(BF16)

    return pl.pallas_call(
        body, name=name,
        grid=(N // tn, nk),
        in_specs=[pl.BlockSpec((tk, M), lambda j, k: (k, 0)), pl.BlockSpec((tk, tn), lambda j, k: (k, j))],
        out_specs=pl.BlockSpec((M, tn), lambda j, k: (0, j)),
        out_shape=jax.ShapeDtypeStruct((M, N), BF16),
        scratch_shapes=[pltpu.VMEM((M, tn), F32)],
        compiler_params=_params(("parallel", "arbitrary")),
    )(a, b)


def _row_tile(R, C):
    for cand in (512, 256, 128, 64, 32, 16, 8):
        if R % cand == 0 and cand * C * 4 <= (2 << 20):
            return cand
    return R


def _adamw_update(p_ref, w_ref, m_ref, v_ref, g_ref, d_ref, nm_ref, nv_ref):
    c1 = 1.0 / (1.0 - ADAM_B1 ** ADAM_STEP)
    c2 = 1.0 / (1.0 - ADAM_B2 ** ADAM_STEP)
    g = p_ref[0].astype(F32)
    for k in range(1, p_ref.shape[0]):
        g = g + p_ref[k].astype(F32)
    nm = ADAM_B1 * m_ref[...] + (1.0 - ADAM_B1) * g
    nv = ADAM_B2 * v_ref[...] + (1.0 - ADAM_B2) * (g * g)
    g_ref[...] = g
    nm_ref[...] = nm
    nv_ref[...] = nv
    d_ref[...] = -ADAM_LR * ((nm * c1) / (jnp.sqrt(nv * c2) + ADAM_EPS) + ADAM_WD * w_ref[...])


def _adamw(parts, w, m, v, name):
    P, R, C = parts.shape
    tr = _row_tile(R, C)

    def body(*refs):
        _adamw_update(*refs)

    spec = pl.BlockSpec((tr, C), lambda i: (i, 0))
    return pl.pallas_call(
        body, name=name,
        grid=(R // tr,),
        in_specs=[pl.BlockSpec((P, tr, C), lambda i: (0, i, 0)), spec, spec, spec],
        out_specs=[spec] * 4,
        out_shape=[jax.ShapeDtypeStruct((R, C), F32)] * 4,
        compiler_params=_params(("parallel",)),
    )(parts, w, m, v)


def _adamw_layers(parts, w, m, v, name):
    depth = len(parts)
    P, R, C = parts[0].shape
    tr = _row_tile(R, C)

    def body(*refs):
        layer = pl.program_id(0)
        for k in range(depth):
            @pl.when(layer == k)
            def _(k=k):
                _adamw_update(refs[k], *refs[depth:])

    def part_spec(k):
        return pl.BlockSpec((P, tr, C), lambda l, i: (0, jnp.where(l == k, i, 0), 0))

    spec = pl.BlockSpec((None, tr, C), lambda l, i: (l, i, 0))
    return pl.pallas_call(
        body, name=name,
        grid=(depth, R // tr),
        in_specs=[part_spec(k) for k in range(depth)] + [spec] * 3,
        out_specs=[spec] * 4,
        out_shape=[jax.ShapeDtypeStruct((depth, R, C), F32)] * 4,
        compiler_params=_params(("arbitrary", "arbitrary")),
    )(*parts, w, m, v)


def _place():
    return lax.axis_index("x"), lax.axis_index("y"), lax.axis_index("c")


def _all_gather(shards):
    n = len(shards)

    def body(*refs):
        ins, outs = refs[:n], refs[n:2 * n]
        send1, recv1, local_sems, send2, recv2 = refs[2 * n:]
        first = _ag_first(ins, outs, send1, recv1, local_sems)
        second = _ag_second(outs, send2, recv2)
        _comm_start(first)
        for j in range(3):
            for a in range(n):
                first[2][4 * a + 1 + j].wait_recv()
            for a in range(n):
                second[1][3 * a + j].start()
        for a in range(n):
            first[2][4 * a].wait_recv()
        for cp in second[2]:
            cp.wait_recv()
        for cp in first[1] + second[1]:
            cp.wait_send()
        for cp in first[0]:
            cp.wait()

    return pl.pallas_call(
        body, name="weights_all_gather",
        in_specs=[ANY] * n, out_specs=[ANY] * n,
        out_shape=[jax.ShapeDtypeStruct((N_DEV,) + s.shape, s.dtype) for s in shards],
        scratch_shapes=_dma_sems(4 * n, 4 * n, n, 3 * n, 3 * n),
        compiler_params=pltpu.CompilerParams(has_side_effects=True),
    )(*shards)


N_BIG = 4


def _dev(p):
    return 4 * p[0] + 2 * p[1] + p[2]


def _other_chips(x, y):
    return [(1 - x, y), (x, 1 - y), (1 - x, 1 - y)]


def _remote(src, dst, send_sems, recv_sems, k, to):
    return pltpu.make_async_remote_copy(src_ref=src, dst_ref=dst, send_sem=send_sems.at[k], recv_sem=recv_sems.at[k],
                                        device_id=to, device_id_type=MESH)


def _ag_first(ins, outs, send_sems, recv_sems, local_sems):
    x, y, c = _place()
    me = (x, y, c)
    targets = [(x, y, 1 - c)] + [(*chip, c) for chip in _other_chips(x, y)]
    local, out, inc = [], [], []
    for a in range(len(ins)):
        local.append(pltpu.make_async_copy(ins[a], outs[a].at[_dev(me)], local_sems.at[a]))
        for k, to in enumerate(targets):
            out.append(_remote(ins[a], outs[a].at[_dev(me)], send_sems, recv_sems, 4 * a + k, to))
            inc.append(_remote(ins[a], outs[a].at[_dev(to)], send_sems, recv_sems, 4 * a + k, to))
    return local, out, inc


def _ag_second(bufs, send_sems, recv_sems):
    x, y, c = _place()
    out, inc = [], []
    for a in range(len(bufs)):
        for j, chip in enumerate(_other_chips(x, y)):
            mine, theirs = bufs[a].at[_dev((*chip, c))], bufs[a].at[_dev((*chip, 1 - c))]
            out.append(_remote(mine, mine, send_sems, recv_sems, 3 * a + j, (x, y, 1 - c)))
            inc.append(_remote(theirs, theirs, send_sems, recv_sems, 3 * a + j, (x, y, 1 - c)))
    return [], out, inc


def _rs_first(ins, outs, send_sems, recv_sems):
    x, y, c = _place()
    out = [_remote(ins[a].at[1 - c], outs[a], send_sems, recv_sems, a, (x, y, 1 - c)) for a in range(len(ins))]
    return [], out, out


def _rs_second(ins, outs, send_sems, recv_sems, local_sems):
    x, y, c = _place()
    my_chip = 2 * x + y
    local, out, inc = [], [], []
    for a in range(len(ins)):
        local.append(pltpu.make_async_copy(ins[a].at[my_chip], outs[a].at[my_chip], local_sems.at[a]))
        for k, (ox, oy) in enumerate(_other_chips(x, y)):
            out.append(_remote(ins[a].at[2 * ox + oy], outs[a].at[my_chip], send_sems, recv_sems, 3 * a + k, (ox, oy, c)))
            inc.append(_remote(ins[a].at[2 * ox + oy], outs[a].at[2 * ox + oy], send_sems, recv_sems, 3 * a + k,
                               (ox, oy, c)))
    return local, out, inc


def _comm_start(exchange):
    local, out, _ = exchange
    for cp in local + out:
        cp.start()


def _comm_wait(exchange):
    local, out, inc = exchange
    for cp in inc:
        cp.wait_recv()
    for cp in out:
        cp.wait_send()
    for cp in local:
        cp.wait()


def _dma_sems(*counts):
    return [pltpu.SemaphoreType.DMA((n,)) for n in counts]


def _rs_sibling(grads):
    n = len(grads)

    def body(*refs):
        ex = _rs_first(refs[:n], refs[n:2 * n], *refs[2 * n:])
        _comm_start(ex)
        _comm_wait(ex)

    return pl.pallas_call(
        body, name="grads_to_sibling",
        in_specs=[ANY] * n, out_specs=[ANY] * n,
        out_shape=[jax.ShapeDtypeStruct(g.shape[1:], g.dtype) for g in grads],
        scratch_shapes=_dma_sems(n, n),
        compiler_params=pltpu.CompilerParams(has_side_effects=True),
    )(*grads)


def _pair_sum(grads, recvs):
    n = len(grads)

    def body(c_ref, *refs):
        for a in range(n):
            refs[2 * n + a][...] = (refs[a][...].astype(F32) + refs[n + a][...].astype(F32)).astype(BF16)

    def g_spec(g):
        return pl.BlockSpec((None, None) + g.shape[2:], lambda j, c_ref: (c_ref[0], j, 0, 0))

    def r_spec(r):
        return pl.BlockSpec((None,) + r.shape[1:], lambda j, c_ref: (j, 0, 0))

    return pl.pallas_call(
        body, name="pair_sum",
        grid_spec=pltpu.PrefetchScalarGridSpec(
            num_scalar_prefetch=1, grid=(N_CHIP,),
            in_specs=[g_spec(g) for g in grads] + [r_spec(r) for r in recvs],
            out_specs=[r_spec(r) for r in recvs]),
        out_shape=[jax.ShapeDtypeStruct(r.shape, BF16) for r in recvs],
        compiler_params=_params(("parallel",)),
    )(lax.axis_index("c").reshape(1).astype(jnp.int32), *grads, *recvs)


def _rs_chips(parts):
    n = len(parts)

    def body(*refs):
        ex = _rs_second(refs[:n], refs[n:2 * n], *refs[2 * n:])
        _comm_start(ex)
        _comm_wait(ex)

    return pl.pallas_call(
        body, name="grads_to_chips",
        in_specs=[ANY] * n, out_specs=[ANY] * n,
        out_shape=[jax.ShapeDtypeStruct(p.shape, p.dtype) for p in parts],
        scratch_shapes=_dma_sems(3 * n, 3 * n, n),
        compiler_params=pltpu.CompilerParams(has_side_effects=True),
    )(*parts)


SMALL_ROWS = 544


def _all_reduce_small(buf):
    def body(in_ref, out_ref, recv, acc, send1, recv1, send2, recv2):
        x, y, c = _place()
        me = 4 * x + 2 * y + c
        peers = [(x ^ (r >> 2), y ^ ((r >> 1) & 1), c ^ (r & 1)) for r in range(1, N_DEV)]

        def idx(p):
            return 4 * p[0] + 2 * p[1] + p[2]

        first = [pltpu.make_async_remote_copy(
            src_ref=in_ref.at[idx(p)], dst_ref=recv.at[me], send_sem=send1.at[r], recv_sem=recv1.at[r],
            device_id=p, device_id_type=MESH) for r, p in enumerate(peers)]
        for cp in first:
            cp.start()
        recv[me] = in_ref[me]
        for r, p in enumerate(peers):
            pltpu.make_async_remote_copy(
                src_ref=in_ref.at[idx(p)], dst_ref=recv.at[idx(p)], send_sem=send1.at[r], recv_sem=recv1.at[r],
                device_id=p, device_id_type=MESH).wait_recv()
        total = recv[0]
        for k in range(1, N_DEV):
            total = total + recv[k]
        acc[...] = total
        out_ref[me] = total
        second = [pltpu.make_async_remote_copy(
            src_ref=acc, dst_ref=out_ref.at[me], send_sem=send2.at[r], recv_sem=recv2.at[r],
            device_id=p, device_id_type=MESH) for r, p in enumerate(peers)]
        for cp in second:
            cp.start()
        for r, p in enumerate(peers):
            pltpu.make_async_remote_copy(
                src_ref=acc, dst_ref=out_ref.at[idx(p)], send_sem=send2.at[r], recv_sem=recv2.at[r],
                device_id=p, device_id_type=MESH).wait_recv()
        for cp in first + second:
            cp.wait_send()

    vm = pl.BlockSpec(memory_space=pltpu.VMEM)
    return pl.pallas_call(
        body, name="small_grads_all_reduce",
        in_specs=[vm], out_specs=vm,
        out_shape=jax.ShapeDtypeStruct(buf.shape, F32),
        scratch_shapes=[pltpu.VMEM(buf.shape, F32), pltpu.VMEM(buf.shape[1:], F32),
                        pltpu.SemaphoreType.DMA((7,)), pltpu.SemaphoreType.DMA((7,)),
                        pltpu.SemaphoreType.DMA((7,)), pltpu.SemaphoreType.DMA((7,))],
        compiler_params=pltpu.CompilerParams(has_side_effects=True, vmem_limit_bytes=VMEM_LIMIT),
    )(buf)


def _dilate(a, d):
    if d == 1:
        return a
    S, C = a.shape
    return a.reshape(S // d, d, C).transpose(1, 0, 2).reshape(S, C)


def _undilate(a, d):
    if d == 1:
        return a
    S, C = a.shape
    return a.reshape(d, S // d, C).transpose(1, 0, 2).reshape(S, C)


def _cols(a, cb, n=1):
    return a[:, cb * WIDTH:(cb + n) * WIDTH]


def _to_blocks(g, kind):
    if kind == "cols":
        R = g.shape[0]
        return g.reshape(R, N_CHIP, 2, -1).transpose(2, 1, 0, 3)
    if kind == "rows":
        C = g.shape[1]
        return g.reshape(N_CHIP, 2, -1, C).transpose(1, 0, 2, 3)
    return g.reshape(4 * WIDTH, N_CHIP, 2, -1).transpose(2, 1, 0, 3)


SMALL = ("norm_g", "gm_ln_g", "gm_ln_b", "gm_ws", "gm_bs", "pool_w", "pool_scale", "mem_norm_g", "final_norm_g")


def _pack_small(tree):
    flat = jnp.concatenate([tree[k].reshape(-1, 128) for k in SMALL], axis=0)
    return jnp.pad(flat, ((0, N_DEV * SMALL_ROWS - flat.shape[0]), (0, 0)))


def _unpack_small(flat, like):
    out, at = {}, 0
    for k in SMALL:
        rows = like[k].size // 128
        out[k] = flat[at:at + rows].reshape(like[k].shape)
        at += rows
    return out


def _make_layer(win, wkv, wb, wout, norm_g, mem_norm_g, ln_g, ln_b, gm_ws, gm_bs, pool_w, pool_scale):
    tril = jnp.tril(jnp.ones((CHUNK, CHUNK), bool))
    wsm = jnp.where(tril, gm_ws, 0.0).astype(BF16)
    pw = pool_w.astype(BF16)
    return dict(win=win, wkv=wkv, wb=wb, wout=wout, g=norm_g[None], mg=mem_norm_g[None], ln_g=ln_g[None],
                ln_b=ln_b[None], wsm=wsm, wsm_t=wsm.transpose(0, 2, 1), pw=pw, pw_t=pw.transpose(0, 2, 1),
                ps=pool_scale[None], bias=jnp.repeat(gm_bs.T, HEAD, axis=1))


def _layer_fwd(xl, mem0, L, next_shards=()):
    S = xl.shape[0]
    proj, h, half_gathered = _in_proj(xl, L["g"], L["win"], next_shards)
    kv, mem_n = _mem_kv(mem0, L["mg"], L["wkv"])
    y3, gathered = _abm_fwd(proj, L["ln_g"], L["ln_b"], L["wsm"], L["bias"], L["pw"], L["ps"], kv, half_gathered)
    o_g, l_g = [], []
    for gi, d in enumerate(DILATIONS):
        if d == 1:
            o, lse = _attn_fwd(proj, CB_Q0, proj, CB_K, proj, CB_CV, S // CHUNK)
        else:
            o, lse = _attn_fwd(_dilate(_cols(proj, CB_Q0 + gi), d), 0, _dilate(_cols(proj, CB_K), d), 0,
                               _dilate(_cols(proj, CB_CV), d), 0, S // d // CHUNK)
        o_g.append(_undilate(o, d))
        l_g.append(_undilate(lse, d))
    xn, yc, oc, lse, z = _merge_fwd(xl, y3, o_g, l_g, proj, L["wb"], L["wout"])
    return xn, dict(x=xl, proj=proj, h=h, kv=kv, mem_n=mem_n, y3=y3, yc=yc, oc=oc, lse=lse, z=z), gathered


def _layer_bwd(dx, mem0, L, sv, later=()):
    S = dx.shape[0]
    proj = sv["proj"]
    (dy3, doc, delta, dcg, dgm, dt), from_sibling = _merge_bwd(dx, sv["y3"], sv["yc"], sv["oc"], proj, L["wb"],
                                                              L["wout"], later)
    pair = _pair_sum(later, from_sibling) if later else ()
    dwout = _matmul_tn(sv["z"], dx, D_MODEL, "dw_out")
    ys = (sv["y3"][0], sv["y3"][1], sv["yc"], sv["y3"][2])
    dwb = jnp.stack([_matmul_tn(ys[b], dt[b], D_MODEL, "dw_branch") for b in range(4)])
    dab, dm, dlng, dlnb, dws, dbias, dpw, dps, dkv = _abm_bwd(
        proj, dy3, L["ln_g"], L["ln_b"], L["wsm"], L["wsm_t"], L["bias"], L["pw"], L["pw_t"], L["ps"], sv["kv"])
    dq, dk, dv = [], None, None
    for gi, d in enumerate(DILATIONS):
        if d == 1:
            r = _attn_bwd(proj, CB_Q0, proj, CB_K, proj, CB_CV, doc, sv["lse"], delta, S // CHUNK)
        else:
            r = _attn_bwd(_dilate(_cols(proj, CB_Q0 + gi), d), 0, _dilate(_cols(proj, CB_K), d), 0,
                          _dilate(_cols(proj, CB_CV), d), 0, _dilate(doc, d), _dilate(sv["lse"], d),
                          _dilate(delta, d), S // d // CHUNK)
        dq.append(_undilate(r[0], d))
        dkg, dvg = _undilate(r[1], d).astype(F32), _undilate(r[2], d).astype(F32)
        dk = dkg if dk is None else dk + dkg
        dv = dvg if dv is None else dv + dvg
    dproj = jnp.concatenate([dab, dq[0], dq[1], dq[2], dk.astype(BF16), dv.astype(BF16), dcg, dm, dgm], axis=1)
    dwkv, dmg = _mem_bwd(mem0, L["mg"], sv["mem_n"], L["wkv"], dkv)
    dwin = _matmul_tn(sv["h"], dproj, 1536, "dw_in")
    dxi, dng, parts = _dh_bwd(dproj, L["win"], sv["x"], L["g"], dx, pair)
    big = dict(w_in=dwin, w_mem_kv=dwkv, w_branch=dwb, w_out=dwout)
    small = dict(norm_g=dng[0], gm_ln_g=dlng[0], gm_ln_b=dlnb[0], gm_ws=dws,
                 gm_bs=_bias_reduce(dbias)[:, :N_HEAD].T, pool_w=dpw, pool_scale=dps[0], mem_norm_g=dmg[0])
    return dxi, big, small, parts


BIG = ("w_in", "w_mem_kv", "w_branch", "w_out")


def _blocked(big):
    return [_to_blocks(big["w_in"], "cols"), _to_blocks(big["w_mem_kv"], "rows"),
            _to_blocks(big["w_branch"], "branch"), _to_blocks(big["w_out"], "rows")]


def _full_weights(gathered):
    win, wkv, wb, wout = gathered
    return (win.transpose(1, 0, 2).reshape(D_MODEL, D_IN), wkv.reshape(D_MODEL, 2 * WIDTH),
            wb.reshape(N_DEV, 4, WIDTH, -1).transpose(1, 2, 0, 3).reshape(4, WIDTH, D_MODEL),
            wout.reshape(D_MODEL, D_MODEL))


def kernel(x, mem, norm_g, w_in, gm_ln_g, gm_ln_b, gm_ws, gm_bs, pool_w, pool_scale, mem_norm_g, w_mem_kv, w_branch, w_out, final_norm_g, loss_target, m_norm_g, m_w_in, m_gm_ln_g, m_gm_ln_b, m_gm_ws, m_gm_bs, m_pool_w, m_pool_scale, m_mem_norm_g, m_w_mem_kv, m_w_branch, m_w_out, m_final_norm_g, v_norm_g, v_w_in, v_gm_ln_g, v_gm_ln_b, v_gm_ws, v_gm_bs, v_pool_w, v_pool_scale, v_mem_norm_g, v_w_mem_kv, v_w_branch, v_w_out, v_final_norm_g):
    x0 = x[0]
    mem0 = mem[0]
    tgt = loss_target[0]
    S = x0.shape[0]

    shards = [[w_in[l].astype(BF16), w_mem_kv[l].astype(BF16), w_branch[l].astype(BF16).reshape(4 * WIDTH, -1),
               w_out[l].astype(BF16)] for l in range(DEPTH)]
    gathered = _all_gather(shards[0])
    layers, saved = [], []
    xl = x0
    for l in range(DEPTH):
        layers.append(_make_layer(*_full_weights(gathered), norm_g[l], mem_norm_g[l], gm_ln_g[l], gm_ln_b[l],
                                  gm_ws[l], gm_bs[l], pool_w[l], pool_scale[l]))
        xl, sv, gathered = _layer_fwd(xl, mem0, layers[l], shards[l + 1] if l + 1 < DEPTH else ())
        saved.append(sv)

    loss_part, dx, d_final = _loss_head(xl, final_norm_g[None], tgt)
    loss = lax.psum(loss_part[0, 0], ("x", "y", "c"))

    small = {k: [None] * DEPTH for k in SMALL if k != "final_norm_g"}
    parts = [None] * DEPTH
    later = ()
    for l in reversed(range(DEPTH)):
        dx, gb, gs, done = _layer_bwd(dx, mem0, layers[l], saved[l], later)
        if later:
            parts[l + 1] = done
        later = _blocked(gb)
        for k in gs:
            small[k][l] = gs[k]
    grad_x = dx[None]
    parts[0] = _rs_chips(_pair_sum(later, _rs_sibling(later)))

    small_tree = {k: jnp.stack(small[k]) for k in small}
    small_tree["final_norm_g"] = d_final[0]
    reduced = _all_reduce_small(_pack_small(small_tree).reshape(N_DEV, SMALL_ROWS, 128))

    weights = dict(norm_g=norm_g, w_in=w_in, gm_ln_g=gm_ln_g, gm_ln_b=gm_ln_b, gm_ws=gm_ws, gm_bs=gm_bs,
                   pool_w=pool_w, pool_scale=pool_scale, mem_norm_g=mem_norm_g, w_mem_kv=w_mem_kv,
                   w_branch=w_branch, w_out=w_out, final_norm_g=final_norm_g)
    m_in = dict(norm_g=m_norm_g, w_in=m_w_in, gm_ln_g=m_gm_ln_g, gm_ln_b=m_gm_ln_b, gm_ws=m_gm_ws, gm_bs=m_gm_bs,
                pool_w=m_pool_w, pool_scale=m_pool_scale, mem_norm_g=m_mem_norm_g, w_mem_kv=m_w_mem_kv,
                w_branch=m_w_branch, w_out=m_w_out, final_norm_g=m_final_norm_g)
    v_in = dict(norm_g=v_norm_g, w_in=v_w_in, gm_ln_g=v_gm_ln_g, gm_ln_b=v_gm_ln_b, gm_ws=v_gm_ws, gm_bs=v_gm_bs,
                pool_w=v_pool_w, pool_scale=v_pool_scale, mem_norm_g=v_mem_norm_g, w_mem_kv=v_w_mem_kv,
                w_branch=v_w_branch, w_out=v_w_out, final_norm_g=v_final_norm_g)
    res = {}
    for a, k in enumerate(BIG):
        shape = weights[k].shape
        by_layer = [parts[l][a] for l in range(DEPTH)]
        lrc = (DEPTH,) + by_layer[0].shape[1:]
        outs = _adamw_layers(by_layer, weights[k].reshape(lrc), m_in[k].reshape(lrc), v_in[k].reshape(lrc),
                             "adamw_" + k)
        res[k] = [o.reshape(shape) for o in outs]
    outs = _adamw(reduced.reshape(1, N_DEV * SMALL_ROWS, 128), _pack_small(weights), _pack_small(m_in),
                  _pack_small(v_in), "adamw_small")
    unpacked = [_unpack_small(o, weights) for o in outs]
    for k in SMALL:
        res[k] = [u[k] for u in unpacked]

    order = ("norm_g", "w_in", "gm_ln_g", "gm_ln_b", "gm_ws", "gm_bs", "pool_w", "pool_scale", "mem_norm_g",
             "w_mem_kv", "w_branch", "w_out", "final_norm_g")
    return (loss, grad_x, *[res[k][0] for k in order], *[res[k][1] for k in order],
            *[res[k][2] for k in order], *[res[k][3] for k in order])
```

```python
import functools
import math

import jax
import jax.numpy as jnp
from jax import lax
from jax.experimental import pallas as pl
from jax.experimental.pallas import tpu as pltpu

F32 = jnp.float32
BF16 = jnp.bfloat16

D_MODEL = 1024
DEPTH = 4
WIDTH = 512
D_IN = 10752
HEAD = 128
N_HEAD = 4
CHUNK = 128
MEM_LEN = 256
POOL_WINDOWS = (2, 4, 8, 16)
DILATIONS = (1, 4, 16)
EPS = 1e-6
NEG = -1e30
ATT_SCALE = HEAD ** -0.5
N_DEV = 8
N_CHIP = 4

CB_U, CB_V, CB_AGATE, CB_PIN, CB_PGATE = 0, 1, 2, 3, 4
CB_Q0, CB_K, CB_CV, CB_CGATE, CB_MQ, CB_MGATE, CB_MERGE = 5, 8, 9, 10, 11, 12, 13

ADAM_LR = 0.001
ADAM_B1 = 0.9
ADAM_B2 = 0.999
ADAM_EPS = 1e-08
ADAM_WD = 0.01
ADAM_STEP = 10

VMEM_LIMIT = 56 * 1024 * 1024
MESH = pl.DeviceIdType.MESH
ANY = pl.BlockSpec(memory_space=pl.ANY)

NT = (((1,), (1,)), ((), ()))
TN = (((0,), (0,)), ((), ()))


def _dot(a, b):
    return jnp.dot(a, b, preferred_element_type=F32)


def _dot_nt(a, b):
    return lax.dot_general(a, b, NT, preferred_element_type=F32)


def _dot_tn(a, b):
    return lax.dot_general(a, b, TN, preferred_element_type=F32)


def _sigmoid(x):
    return 1.0 / (1.0 + jnp.exp(-x))


def _silu(x):
    return x * _sigmoid(x)


def _dsilu(x):
    s = _sigmoid(x)
    return s * (1.0 + x * (1.0 - s))


def _gelu(x):
    return 0.5 * x * (1.0 + lax.erf(x * (2.0 ** -0.5)))


def _dgelu(x):
    return 0.5 * (1.0 + lax.erf(x * (2.0 ** -0.5))) + x * jnp.exp(-0.5 * x * x) * (1.0 / math.sqrt(2.0 * math.pi))


def _col(blk, h):
    lane = lax.broadcasted_iota(jnp.int32, blk.shape, 1)
    return jnp.sum(jnp.where(lane == h, blk, 0.0), axis=1, keepdims=True)


def _put_cols(cols):
    rows = cols[0].shape[0]
    lane = lax.broadcasted_iota(jnp.int32, (rows, 128), 1)
    out = jnp.zeros((rows, 128), F32)
    for h, cv in enumerate(cols):
        out = jnp.where(lane == h, cv, out)
    return out


def _params(sem, vmem=VMEM_LIMIT):
    return pltpu.CompilerParams(dimension_semantics=sem, vmem_limit_bytes=vmem)


def _full(shape):
    nd = len(shape)
    return pl.BlockSpec(shape, lambda *_: (0,) * nd)


def _rows(tm, width, cb=0):
    return pl.BlockSpec((tm, width), lambda i: (i, cb))


def _in_proj(x, g, w, shards=()):
    S = x.shape[0]
    tm, tn = 1024, 1536
    n = len(shards)
    ni, nj = S // tm, D_IN // tn

    def body(x_ref, g_ref, w_ref, *rest):
        ins, (proj_ref, h_ref), outs = rest[:n], rest[n:n + 2], rest[n + 2:2 * n + 2]
        hs, sems = rest[2 * n + 2], rest[2 * n + 3:]
        i, j = pl.program_id(0), pl.program_id(1)

        if n:
            @pl.when(jnp.logical_and(i == 0, j == 0))
            def _():
                _comm_start(_ag_first(ins, outs, *sems))

        @pl.when(j == 0)
        def _():
            xf = x_ref[...]
            r = lax.rsqrt(jnp.mean(xf * xf, axis=-1, keepdims=True) + EPS)
            h = (xf * r * g_ref[...]).astype(BF16)
            hs[...] = h
            h_ref[...] = h

        proj_ref[...] = _dot(hs[...], w_ref[...]).astype(BF16)

        if n:
            @pl.when(jnp.logical_and(i == ni - 1, j == nj - 1))
            def _():
                _comm_wait(_ag_first(ins, outs, *sems))

    res = pl.pallas_call(
        body, name="in_proj_gather" if n else "in_proj",
        grid=(ni, nj),
        in_specs=[pl.BlockSpec((tm, D_MODEL), lambda i, j: (i, 0)),
                  pl.BlockSpec((1, D_MODEL), lambda i, j: (0, 0)),
                  pl.BlockSpec((D_MODEL, tn), lambda i, j: (0, j))] + [ANY] * n,
        out_specs=[pl.BlockSpec((tm, tn), lambda i, j: (i, j)),
                   pl.BlockSpec((tm, D_MODEL), lambda i, j: (i, 0))] + [ANY] * n,
        out_shape=[jax.ShapeDtypeStruct((S, D_IN), BF16), jax.ShapeDtypeStruct((S, D_MODEL), BF16)]
                  + [jax.ShapeDtypeStruct((N_DEV,) + s.shape, s.dtype) for s in shards],
        scratch_shapes=[pltpu.VMEM((tm, D_MODEL), BF16)] + (_dma_sems(4 * n, 4 * n, n) if n else []),
        compiler_params=_params(("arbitrary", "arbitrary")),
    )(x, g, w, *shards)
    return res[0], res[1], list(res[2:])


def _mem_kv(mem, g, w):
    M = mem.shape[0]

    def body(m_ref, g_ref, w_ref, kv_ref, mn_ref):
        xf = m_ref[...]
        r = lax.rsqrt(jnp.mean(xf * xf, axis=-1, keepdims=True) + EPS)
        mn = (xf * r * g_ref[...]).astype(BF16)
        mn_ref[...] = mn
        kv_ref[...] = _dot(mn, w_ref[...]).astype(BF16)

    return pl.pallas_call(
        body, name="mem_kv",
        out_shape=[jax.ShapeDtypeStruct((M, 2 * WIDTH), BF16), jax.ShapeDtypeStruct((M, D_MODEL), BF16)],
        compiler_params=pltpu.CompilerParams(vmem_limit_bytes=VMEM_LIMIT),
    )(mem, g, w)


def _band_masks(win):
    t = lax.broadcasted_iota(jnp.int32, (CHUNK, CHUNK), 0)
    s = lax.broadcasted_iota(jnp.int32, (CHUNK, CHUNK), 1)
    cur = jnp.logical_and(t - s >= 0, t - s < win)
    prev = s > t + CHUNK - win
    return cur.astype(BF16), prev.astype(BF16)


def _inv_count(first_row, win):
    t = first_row + lax.broadcasted_iota(jnp.int32, (CHUNK, 1), 0)
    return 1.0 / jnp.minimum(t + 1, win).astype(F32)


def _layer_norm_fwd(v):
    mu = jnp.mean(v, axis=-1, keepdims=True)
    vc = v - mu
    var = jnp.mean(vc * vc, axis=-1, keepdims=True)
    rstd = lax.rsqrt(var + EPS)
    return vc * rstd, rstd


def _mem_softmax(q, kmem):
    s = _dot_nt(q, kmem) * ATT_SCALE
    m = jnp.max(s, axis=-1, keepdims=True)
    e = jnp.exp(s - m)
    return e * (1.0 / jnp.sum(e, axis=-1, keepdims=True))


def _abm_fwd(proj, ln_g, ln_b, wsm, bias_full, pool_w, pool_scale, kv, gathered=()):
    S = proj.shape[0]
    tm = 512
    nchunk = tm // CHUNK
    n = len(gathered)
    nsteps = S // tm

    def body(u_ref, v_ref, ag_ref, p_ref, ph_ref, pg_ref, mq_ref, mg_ref, lng_ref, lnb_ref, wsm_ref, bias_ref,
             pw_ref, ps_ref, kv_ref, *rest):
        y_ref, bufs, mix, sems = rest[n], rest[n + 1:2 * n + 1], rest[2 * n + 1], rest[2 * n + 2:]
        i = pl.program_id(0)

        if n:
            @pl.when(i == 0)
            def _():
                _comm_start(_ag_second(bufs, *sems))

        u = _gelu(u_ref[...].astype(F32))
        v = _gelu(v_ref[...].astype(F32))
        vhat, _ = _layer_norm_fwd(v)
        vln = (vhat * lng_ref[...] + lnb_ref[...]).astype(BF16)
        for c in range(nchunk):
            for h in range(N_HEAD):
                rs, cs = slice(c * CHUNK, (c + 1) * CHUNK), slice(h * HEAD, (h + 1) * HEAD)
                mix[rs, cs] = _dot(wsm_ref[h], vln[rs, cs]) + bias_ref[:, cs]
        y_ref[0] = (u * mix[...] * _silu(ag_ref[...].astype(F32))).astype(BF16)
        halo_ok = (i > 0).astype(F32)
        for c in range(nchunk):
            rs = slice(c * CHUNK, (c + 1) * CHUNK)
            for g, win in enumerate(POOL_WINDOWS):
                cs = slice(g * HEAD, (g + 1) * HEAD)
                bcur, bprev = _band_masks(win)
                cur = p_ref[rs, cs]
                if c == 0:
                    prev = (ph_ref[:, cs].astype(F32) * halo_ok).astype(BF16)
                else:
                    prev = p_ref[(c - 1) * CHUNK:c * CHUNK, cs]
                sums = _dot(bcur, cur) + _dot(bprev, prev)
                dm = sums * _inv_count(i * tm + c * CHUNK, win) - cur.astype(F32)
                mix[rs, cs] = _dot(dm.astype(BF16), pw_ref[g])
        y_ref[1] = (mix[...] * ps_ref[...] * _silu(pg_ref[...].astype(F32))).astype(BF16)
        for h in range(N_HEAD):
            cs = slice(h * HEAD, (h + 1) * HEAD)
            p = _mem_softmax(mq_ref[:, cs], kv_ref[:, cs])
            mix[:, cs] = _dot(p.astype(BF16), kv_ref[:, WIDTH + h * HEAD:WIDTH + (h + 1) * HEAD])
        y_ref[2] = (mix[...] * _silu(mg_ref[...].astype(F32))).astype(BF16)

        if n:
            @pl.when(i == nsteps - 1)
            def _():
                _comm_wait(_ag_second(bufs, *sems))

    blk = tm // CHUNK
    res = pl.pallas_call(
        body, name="abm_fwd_gather" if n else "abm_fwd",
        grid=(nsteps,),
        in_specs=[_rows(tm, WIDTH, CB_U), _rows(tm, WIDTH, CB_V), _rows(tm, WIDTH, CB_AGATE),
                  _rows(tm, WIDTH, CB_PIN),
                  pl.BlockSpec((CHUNK, WIDTH), lambda i: (jnp.maximum(i * blk - 1, 0), CB_PIN)),
                  _rows(tm, WIDTH, CB_PGATE), _rows(tm, WIDTH, CB_MQ), _rows(tm, WIDTH, CB_MGATE),
                  _full((1, WIDTH)), _full((1, WIDTH)), _full((N_HEAD, CHUNK, CHUNK)), _full((CHUNK, WIDTH)),
                  _full((4, HEAD, HEAD)), _full((1, WIDTH)), _full((MEM_LEN, 2 * WIDTH))] + [ANY] * n,
        out_specs=[pl.BlockSpec((3, tm, WIDTH), lambda i: (0, i, 0))] + [ANY] * n,
        out_shape=[jax.ShapeDtypeStruct((3, S, WIDTH), BF16)]
                  + [jax.ShapeDtypeStruct(b.shape, b.dtype) for b in gathered],
        input_output_aliases={15 + a: 1 + a for a in range(n)},
        scratch_shapes=[pltpu.VMEM((tm, WIDTH), F32)] + (_dma_sems(3 * n, 3 * n) if n else []),
        compiler_params=_params(("arbitrary",)),
    )(proj, proj, proj, proj, proj, proj, proj, proj, ln_g, ln_b, wsm, bias_full, pool_w, pool_scale, kv, *gathered)
    return res[0], list(res[1:])


ATT_TILE = 512


def _attn_fwd(q, qcb, k, kcb, v, vcb, bps):
    S = q.shape[0]
    tm = ATT_TILE
    nb = tm // CHUNK

    nblocks = nb * N_HEAD

    def body(q_ref, k_ref, v_ref, kh_ref, vh_ref, o_ref, l_ref, sc_s, sp_s, pc_s, pp_s):
        i = pl.program_id(0)

        def prev_kv(n, cs):
            if n == 0:
                return kh_ref[:, cs], vh_ref[:, cs]
            ps = slice((n - 1) * CHUNK, n * CHUNK)
            return k_ref[ps, cs], v_ref[ps, cs]

        pens = []
        for n in range(nb):
            rs = slice(n * CHUNK, (n + 1) * CHUNK)
            pens.append(jnp.full((N_HEAD * CHUNK, 1), jnp.where((i * nb + n) % bps != 0, 0.0, NEG), F32))
            for h in range(N_HEAD):
                cs = slice(h * HEAD, (h + 1) * HEAD)
                bs = slice((n * N_HEAD + h) * CHUNK, (n * N_HEAD + h + 1) * CHUNK)
                qh = q_ref[rs, cs]
                sc_s[bs, :] = _dot_nt(qh, k_ref[rs, cs])
                sp_s[bs, :] = _dot_nt(qh, prev_kv(n, cs)[0])
        row = lax.broadcasted_iota(jnp.int32, (nblocks * CHUNK, CHUNK), 0) & (CHUNK - 1)
        col = lax.broadcasted_iota(jnp.int32, (nblocks * CHUNK, CHUNK), 1)
        sc = jnp.where(col <= row, sc_s[...] * ATT_SCALE, NEG)
        sp = jnp.where(col >= row, sp_s[...] * ATT_SCALE, NEG) + jnp.concatenate(pens, axis=0)
        m = jnp.maximum(jnp.max(sc, axis=-1, keepdims=True), jnp.max(sp, axis=-1, keepdims=True))
        ec = jnp.exp(sc - m)
        ep = jnp.exp(sp - m)
        den = jnp.sum(ec, axis=-1, keepdims=True) + jnp.sum(ep, axis=-1, keepdims=True)
        inv = 1.0 / den
        pc_s[...] = (ec * inv).astype(BF16)
        pp_s[...] = (ep * inv).astype(BF16)
        lse = m + jnp.log(den)
        for n in range(nb):
            rs = slice(n * CHUNK, (n + 1) * CHUNK)
            for h in range(N_HEAD):
                cs = slice(h * HEAD, (h + 1) * HEAD)
                bs = slice((n * N_HEAD + h) * CHUNK, (n * N_HEAD + h + 1) * CHUNK)
                o = _dot(pc_s[bs, :], v_ref[rs, cs]) + _dot(pp_s[bs, :], prev_kv(n, cs)[1])
                o_ref[rs, cs] = o.astype(BF16)
            l_ref[rs, :] = _put_cols([lse[(n * N_HEAD + h) * CHUNK:(n * N_HEAD + h + 1) * CHUNK]
                                      for h in range(N_HEAD)])

    def halo(cb):
        return pl.BlockSpec((CHUNK, WIDTH), lambda i: (jnp.maximum(i * nb - 1, 0), cb))

    return pl.pallas_call(
        body, name=f"attn_fwd_{bps}",
        grid=(S // tm,),
        in_specs=[_rows(tm, WIDTH, qcb), _rows(tm, WIDTH, kcb), _rows(tm, WIDTH, vcb), halo(kcb), halo(vcb)],
        out_specs=[_rows(tm, WIDTH), _rows(tm, 128)],
        out_shape=[jax.ShapeDtypeStruct((S, WIDTH), BF16), jax.ShapeDtypeStruct((S, 128), F32)],
        scratch_shapes=[pltpu.VMEM((nblocks * CHUNK, CHUNK), F32), pltpu.VMEM((nblocks * CHUNK, CHUNK), F32),
                        pltpu.VMEM((nblocks * CHUNK, CHUNK), BF16), pltpu.VMEM((nblocks * CHUNK, CHUNK), BF16)],
        compiler_params=_params(("parallel",)),
    )(q, k, v, k, v)


def _merge_specs(tm):
    return [_rows(tm, WIDTH, CB_MERGE + j) for j in range(8)]


def _merge_fwd(x, y3, o_g, l_g, proj, wb, wout):
    S = x.shape[0]
    tm = 256

    def body(x_ref, y_ref, o0, o1, o2, l0, l1, l2, cg_ref, *rest):
        gm = rest[:8]
        wb_ref, wo_ref, xn_ref, yc_ref, oc_ref, lse_ref, z_ref, ocs = rest[8:]
        lcols = []
        for h in range(N_HEAD):
            cs = slice(h * HEAD, (h + 1) * HEAD)
            ls = [_col(l[...], h) for l in (l0, l1, l2)]
            m = jnp.maximum(jnp.maximum(ls[0], ls[1]), ls[2])
            tot = jnp.exp(ls[0] - m) + jnp.exp(ls[1] - m) + jnp.exp(ls[2] - m)
            lse = m + jnp.log(tot)
            ocs[:, cs] = sum(jnp.exp(lg - lse) * o[:, cs].astype(F32) for lg, o in zip(ls, (o0, o1, o2)))
            lcols.append(lse)
        lse_ref[...] = _put_cols(lcols)
        oc = ocs[...]
        oc_ref[...] = oc.astype(BF16)
        yc = (oc * _silu(cg_ref[...].astype(F32))).astype(BF16)
        yc_ref[...] = yc
        ys = (y_ref[0], y_ref[1], yc, y_ref[2])
        z = jnp.zeros((tm, D_MODEL), F32)
        for b in range(4):
            gate = _sigmoid(jnp.concatenate([gm[2 * b][...], gm[2 * b + 1][...]], axis=1).astype(F32))
            z = z + gate * _dot(ys[b], wb_ref[b])
        zb = z.astype(BF16)
        z_ref[...] = zb
        xn_ref[...] = x_ref[...] + _dot(zb, wo_ref[...])

    return pl.pallas_call(
        body, name="merge_fwd",
        grid=(S // tm,),
        in_specs=[_rows(tm, D_MODEL), pl.BlockSpec((3, tm, WIDTH), lambda i: (0, i, 0)),
                  _rows(tm, WIDTH), _rows(tm, WIDTH), _rows(tm, WIDTH),
                  _rows(tm, 128), _rows(tm, 128), _rows(tm, 128),
                  _rows(tm, WIDTH, CB_CGATE)] + _merge_specs(tm)
                 + [_full((4, WIDTH, D_MODEL)), _full((D_MODEL, D_MODEL))],
        out_specs=[_rows(tm, D_MODEL), _rows(tm, WIDTH), _rows(tm, WIDTH), _rows(tm, 128), _rows(tm, D_MODEL)],
        out_shape=[jax.ShapeDtypeStruct((S, D_MODEL), F32), jax.ShapeDtypeStruct((S, WIDTH), BF16),
                   jax.ShapeDtypeStruct((S, WIDTH), BF16), jax.ShapeDtypeStruct((S, 128), F32),
                   jax.ShapeDtypeStruct((S, D_MODEL), BF16)],
        scratch_shapes=[pltpu.VMEM((tm, WIDTH), F32)],
        compiler_params=_params(("parallel",)),
    )(x, y3, *o_g, *l_g, proj, *([proj] * 8), wb, wout)


def _loss_head(x, g, tgt):
    S = x.shape[0]
    tm = 512

    def body(x_ref, g_ref, t_ref, loss_ref, dx_ref, dg_ref):
        @pl.when(pl.program_id(0) == 0)
        def _():
            loss_ref[...] = jnp.zeros_like(loss_ref)
            dg_ref[...] = jnp.zeros_like(dg_ref)

        xf = x_ref[...]
        r = lax.rsqrt(jnp.mean(xf * xf, axis=-1, keepdims=True) + EPS)
        xhat = xf * r
        gv = g_ref[...]
        err = xhat * gv - t_ref[...]
        e2 = jnp.sum(err * err, axis=-1, keepdims=True)
        loss_ref[...] += (0.5 / D_MODEL) * jnp.sum(e2, axis=0, keepdims=True)
        dy = err * (1.0 / D_MODEL)
        dg_ref[...] += jnp.sum(dy * xhat, axis=0, keepdims=True)
        dxh = dy * gv
        dx_ref[...] = r * (dxh - xhat * jnp.mean(dxh * xhat, axis=-1, keepdims=True))

    return pl.pallas_call(
        body, name="loss_head",
        grid=(S // tm,),
        in_specs=[_rows(tm, D_MODEL), _full((1, D_MODEL)), _rows(tm, D_MODEL)],
        out_specs=[_full((1, 128)), _rows(tm, D_MODEL), _full((1, D_MODEL))],
        out_shape=[jax.ShapeDtypeStruct((1, 128), F32), jax.ShapeDtypeStruct((S, D_MODEL), F32),
                   jax.ShapeDtypeStruct((1, D_MODEL), F32)],
        compiler_params=_params(("arbitrary",)),
    )(x, g, tgt)


def _merge_bwd(dxo, y3, yc, oc, proj, wb, wout, grads=()):
    S = dxo.shape[0]
    tm = 256
    n = len(grads)
    nsteps = S // tm

    def body(dx_ref, y_ref, yc_ref, oc_ref, cg_ref, *rest):
        gm = rest[:8]
        wb_ref, wo_ref = rest[8:10]
        g_in = rest[10:10 + n]
        dy_ref, doc_ref, delta_ref, dcg_ref, dgm_ref, dt_ref = rest[10 + n:16 + n]
        g_out, sems = rest[16 + n:16 + 2 * n], rest[16 + 2 * n:]
        i = pl.program_id(0)

        if n:
            @pl.when(i == 0)
            def _():
                _comm_start(_rs_first(g_in, g_out, *sems))

        dz = _dot_nt(dx_ref[...].astype(BF16), wo_ref[...])
        ys = (y_ref[0], y_ref[1], yc_ref[...], y_ref[2])
        for b in range(4):
            gate = _sigmoid(jnp.concatenate([gm[2 * b][...], gm[2 * b + 1][...]], axis=1).astype(F32))
            t = _dot(ys[b], wb_ref[b])
            dgm_ref[:, b * D_MODEL:(b + 1) * D_MODEL] = (dz * t * gate * (1.0 - gate)).astype(BF16)
            dt = (dz * gate).astype(BF16)
            dt_ref[b] = dt
            dyb = _dot_nt(dt, wb_ref[b])
            if b == 2:
                cg = cg_ref[...].astype(F32)
                oc = oc_ref[...].astype(F32)
                doc = dyb * _silu(cg)
                dcg_ref[...] = (dyb * oc * _dsilu(cg)).astype(BF16)
                doc_ref[...] = doc.astype(BF16)
                prod = doc * oc
                delta_ref[...] = _put_cols([jnp.sum(prod[:, h * HEAD:(h + 1) * HEAD], axis=1, keepdims=True)
                                            for h in range(N_HEAD)])
            else:
                dy_ref[b if b < 2 else 2] = dyb.astype(BF16)

        if n:
            @pl.when(i == nsteps - 1)
            def _():
                _comm_wait(_rs_first(g_in, g_out, *sems))

    res = pl.pallas_call(
        body, name="merge_bwd_scatter" if n else "merge_bwd",
        grid=(nsteps,),
        in_specs=[_rows(tm, D_MODEL), pl.BlockSpec((3, tm, WIDTH), lambda i: (0, i, 0)),
                  _rows(tm, WIDTH), _rows(tm, WIDTH), _rows(tm, WIDTH, CB_CGATE)] + _merge_specs(tm)
                 + [_full((4, WIDTH, D_MODEL)), _full((D_MODEL, D_MODEL))] + [ANY] * n,
        out_specs=[pl.BlockSpec((3, tm, WIDTH), lambda i: (0, i, 0)), _rows(tm, WIDTH), _rows(tm, 128),
                   _rows(tm, WIDTH), _rows(tm, 4 * D_MODEL), pl.BlockSpec((4, tm, D_MODEL), lambda i: (0, i, 0))]
                  + [ANY] * n,
        out_shape=[jax.ShapeDtypeStruct((3, S, WIDTH), BF16), jax.ShapeDtypeStruct((S, WIDTH), BF16),
                   jax.ShapeDtypeStruct((S, 128), F32), jax.ShapeDtypeStruct((S, WIDTH), BF16),
                   jax.ShapeDtypeStruct((S, 4 * D_MODEL), BF16), jax.ShapeDtypeStruct((4, S, D_MODEL), BF16)]
                  + [jax.ShapeDtypeStruct(g.shape[1:], g.dtype) for g in grads],
        scratch_shapes=_dma_sems(n, n) if n else [],
        compiler_params=_params(("arbitrary",)),
    )(dxo, y3, yc, oc, proj, *([proj] * 8), wb, wout, *grads)
    return res[:6], list(res[6:])


def _attn_bwd(q, qcb, k, kcb, v, vcb, do, lse, delta, bps):
    S = q.shape[0]
    tm = ATT_TILE
    nb = tm // CHUNK
    nblk = S // CHUNK

    ncur = nb * N_HEAD
    nprev = (nb + 1) * N_HEAD

    def body(q_ref, k_ref, v_ref, do_ref, l_ref, d_ref, kh_ref, vh_ref, qn_ref, don_ref, ln_ref, dn_ref,
             dq_ref, dk_ref, dv_ref, sc_s, sp_s, dpc_s, dpp_s, pc_s, pp_s, dsc_s, dsp_s):
        i = pl.program_id(0)

        def rows_of(n):
            if n < nb:
                rs = slice(n * CHUNK, (n + 1) * CHUNK)
                return rs, q_ref, do_ref, l_ref, d_ref
            return slice(0, CHUNK), qn_ref, don_ref, ln_ref, dn_ref

        def prev_kv(n, cs):
            if n == 0:
                return kh_ref[:, cs], vh_ref[:, cs]
            ps = slice((n - 1) * CHUNK, n * CHUNK)
            return k_ref[ps, cs], v_ref[ps, cs]

        def blk(n, h):
            return slice((n * N_HEAD + h) * CHUNK, (n * N_HEAD + h + 1) * CHUNK)

        pens, lses, deltas = [], [], []
        for n in range(nb + 1):
            rs, qr, dor, lr, dr = rows_of(n)
            gb = i * nb + n
            pen = jnp.where(gb % bps != 0, 0.0, NEG)
            if n == nb:
                pen = pen + jnp.where(gb < nblk, 0.0, NEG)
            pens.append(jnp.full((N_HEAD * CHUNK, 1), pen, F32))
            lblk, dblk = lr[rs, :], dr[rs, :]
            for h in range(N_HEAD):
                cs = slice(h * HEAD, (h + 1) * HEAD)
                qh, doh = qr[rs, cs], dor[rs, cs]
                lses.append(_col(lblk, h))
                deltas.append(_col(dblk, h))
                kp, vp = prev_kv(n, cs)
                sp_s[blk(n, h), :] = _dot_nt(qh, kp)
                dpp_s[blk(n, h), :] = _dot_nt(doh, vp)
                if n < nb:
                    sc_s[blk(n, h), :] = _dot_nt(qh, k_ref[rs, cs])
                    dpc_s[blk(n, h), :] = _dot_nt(doh, v_ref[rs, cs])
        lse = jnp.concatenate(lses, axis=0)
        delta = jnp.concatenate(deltas, axis=0)
        row = lax.broadcasted_iota(jnp.int32, (nprev * CHUNK, CHUNK), 0) & (CHUNK - 1)
        col = lax.broadcasted_iota(jnp.int32, (nprev * CHUNK, CHUNK), 1)
        sp = jnp.where(col >= row, sp_s[...] * ATT_SCALE, NEG) + jnp.concatenate(pens, axis=0)
        pp = jnp.exp(sp - lse)
        pp_s[...] = pp.astype(BF16)
        dsp_s[...] = (pp * (dpp_s[...] - delta)).astype(BF16)
        nc = ncur * CHUNK
        sc = jnp.where(col[:nc] <= row[:nc], sc_s[...] * ATT_SCALE, NEG)
        pc = jnp.exp(sc - lse[:nc])
        pc_s[...] = pc.astype(BF16)
        dsc_s[...] = (pc * (dpc_s[...] - delta[:nc])).astype(BF16)
        for n in range(nb):
            rs, qr, dor, _, _ = rows_of(n)
            rn, qnr, donr, _, _ = rows_of(n + 1)
            for h in range(N_HEAD):
                cs = slice(h * HEAD, (h + 1) * HEAD)
                kp, _ = prev_kv(n, cs)
                dq = _dot(dsc_s[blk(n, h), :], k_ref[rs, cs]) + _dot(dsp_s[blk(n, h), :], kp)
                dq_ref[rs, cs] = (dq * ATT_SCALE).astype(BF16)
                dk = _dot_tn(dsc_s[blk(n, h), :], qr[rs, cs]) + _dot_tn(dsp_s[blk(n + 1, h), :], qnr[rn, cs])
                dk_ref[rs, cs] = (dk * ATT_SCALE).astype(BF16)
                dv = _dot_tn(pc_s[blk(n, h), :], dor[rs, cs]) + _dot_tn(pp_s[blk(n + 1, h), :], donr[rn, cs])
                dv_ref[rs, cs] = dv.astype(BF16)

    def prev_halo(cb):
        return pl.BlockSpec((CHUNK, WIDTH), lambda i: (jnp.maximum(i * nb - 1, 0), cb))

    def next_halo(width, cb=0):
        return pl.BlockSpec((CHUNK, width), lambda i: (jnp.minimum(i * nb + nb, nblk - 1), cb))

    return pl.pallas_call(
        body, name=f"attn_bwd_{bps}",
        grid=(S // tm,),
        in_specs=[_rows(tm, WIDTH, qcb), _rows(tm, WIDTH, kcb), _rows(tm, WIDTH, vcb), _rows(tm, WIDTH),
                  _rows(tm, 128), _rows(tm, 128), prev_halo(kcb), prev_halo(vcb),
                  next_halo(WIDTH, qcb), next_halo(WIDTH), next_halo(128), next_halo(128)],
        out_specs=[_rows(tm, WIDTH), _rows(tm, WIDTH), _rows(tm, WIDTH)],
        out_shape=[jax.ShapeDtypeStruct((S, WIDTH), BF16)] * 3,
        scratch_shapes=[pltpu.VMEM((ncur * CHUNK, CHUNK), F32), pltpu.VMEM((nprev * CHUNK, CHUNK), F32),
                        pltpu.VMEM((ncur * CHUNK, CHUNK), F32), pltpu.VMEM((nprev * CHUNK, CHUNK), F32),
                        pltpu.VMEM((ncur * CHUNK, CHUNK), BF16), pltpu.VMEM((nprev * CHUNK, CHUNK), BF16),
                        pltpu.VMEM((ncur * CHUNK, CHUNK), BF16), pltpu.VMEM((nprev * CHUNK, CHUNK), BF16)],
        compiler_params=_params(("parallel",)),
    )(q, k, v, do, lse, delta, k, v, q, do, lse, delta)


def _dilated_split(d):
    hp = min(N_HEAD, 16 // d)
    return hp, N_HEAD // hp, HEAD * hp


def _attn_fwd_dilated(proj, qcb, kcb, vcb, d):
    S = proj.shape[0]
    T = CHUNK * d
    hp, nh, cw = _dilated_split(d)
    nblocks = d * hp

    def body(q_ref, k_ref, v_ref, o_ref, l_ref, qf, kf, vf, kpf, vpf, kst, vst, of, lf, sc_s, sp_s, pc_s, pp_s):
        i, hh = pl.program_id(0), pl.program_id(1)
        for j in range(hp):
            cs = slice(j * HEAD, (j + 1) * HEAD)
            qf[j] = q_ref[:, cs].astype(F32)
            kf[j] = k_ref[:, cs].astype(F32)
            vf[j] = v_ref[:, cs].astype(F32)

        @pl.when(i == 0)
        def _():
            kpf[...] = jnp.zeros_like(kpf)
            vpf[...] = jnp.zeros_like(vpf)

        @pl.when(i > 0)
        def _():
            kpf[...] = kst[hh]
            vpf[...] = vst[hh]

        def blk(ref, r, j):
            return ref.at[j][pl.ds(r, CHUNK, stride=d), :].astype(BF16)

        def bs(r, j):
            return slice((r * hp + j) * CHUNK, (r * hp + j + 1) * CHUNK)

        for r in range(d):
            for j in range(hp):
                qb = blk(qf, r, j)
                sc_s[bs(r, j), :] = _dot_nt(qb, blk(kf, r, j))
                sp_s[bs(r, j), :] = _dot_nt(qb, blk(kpf, r, j))
        row = lax.broadcasted_iota(jnp.int32, (nblocks * CHUNK, CHUNK), 0) & (CHUNK - 1)
        col = lax.broadcasted_iota(jnp.int32, (nblocks * CHUNK, CHUNK), 1)
        sc = jnp.where(col <= row, sc_s[...] * ATT_SCALE, NEG)
        sp = jnp.where(col >= row, sp_s[...] * ATT_SCALE, NEG) + jnp.where(i > 0, 0.0, NEG)
        m = jnp.maximum(jnp.max(sc, axis=-1, keepdims=True), jnp.max(sp, axis=-1, keepdims=True))
        ec = jnp.exp(sc - m)
        ep = jnp.exp(sp - m)
        den = jnp.sum(ec, axis=-1, keepdims=True) + jnp.sum(ep, axis=-1, keepdims=True)
        inv = 1.0 / den
        pc_s[...] = (ec * inv).astype(BF16)
        pp_s[...] = (ep * inv).astype(BF16)
        lse = m + jnp.log(den)
        lane = lax.broadcasted_iota(jnp.int32, (CHUNK, 128), 1)
        for r in range(d):
            lblk = jnp.zeros((CHUNK, 128), F32)
            for j in range(hp):
                o = _dot(pc_s[bs(r, j), :], blk(vf, r, j)) + _dot(pp_s[bs(r, j), :], blk(vpf, r, j))
                of.at[j][pl.ds(r, CHUNK, stride=d), :] = o
                lblk = jnp.where(lane == hh * hp + j, lse[bs(r, j)], lblk)
            lf[pl.ds(r, CHUNK, stride=d), :] = lblk
        for j in range(hp):
            o_ref[:, j * HEAD:(j + 1) * HEAD] = of[j].astype(BF16)

        @pl.when(hh == 0)
        def _():
            l_ref[...] = lf[...]

        @pl.when(hh > 0)
        def _():
            l_ref[...] += lf[...]

        kst[hh] = kf[...]
        vst[hh] = vf[...]

    def cols(cb):
        return pl.BlockSpec((T, cw), lambda i, hh: (i, cb * nh + hh))

    tile = pltpu.VMEM((hp, T, HEAD), F32)
    return pl.pallas_call(
        body, name=f"attn_fwd_dilated_{d}",
        grid=(S // T, nh),
        in_specs=[cols(qcb), cols(kcb), cols(vcb)],
        out_specs=[cols(0), pl.BlockSpec((T, 128), lambda i, hh: (i, 0))],
        out_shape=[jax.ShapeDtypeStruct((S, WIDTH), BF16), jax.ShapeDtypeStruct((S, 128), F32)],
        scratch_shapes=[tile, tile, tile, tile, tile,
                        pltpu.VMEM((nh, hp, T, HEAD), F32), pltpu.VMEM((nh, hp, T, HEAD), F32),
                        tile, pltpu.VMEM((T, 128), F32),
                        pltpu.VMEM((nblocks * CHUNK, CHUNK), F32), pltpu.VMEM((nblocks * CHUNK, CHUNK), F32),
                        pltpu.VMEM((nblocks * CHUNK, CHUNK), BF16), pltpu.VMEM((nblocks * CHUNK, CHUNK), BF16)],
        compiler_params=_params(("arbitrary", "arbitrary")),
    )(proj, proj, proj)


def _attn_bwd_dilated(proj, qcb, kcb, vcb, do, lse, delta, d):
    S = proj.shape[0]
    T = CHUNK * d
    nt = S // T
    hp, nh, cw = _dilated_split(d)
    nblocks = d * hp

    def body(q_ref, k_ref, v_ref, do_ref, l_ref, d_ref, dq_ref, dk_ref, dv_ref,
             qf, dof, kf, vf, kpf, vpf, dqf, acck, accv, newk, newv,
             sc_s, sp_s, dpc_s, dpp_s, pc_s, pp_s, dsc_s, dsp_s):
        hh, i = pl.program_id(0), pl.program_id(1)

        @pl.when(i == 0)
        def _():
            for ref in (kpf, vpf, acck, accv):
                ref[...] = jnp.zeros_like(ref)
            dk_ref[...] = jnp.zeros_like(dk_ref)
            dv_ref[...] = jnp.zeros_like(dv_ref)

        def blk(ref, r, j):
            return ref.at[j][pl.ds(r, CHUNK, stride=d), :].astype(BF16)

        def bs(r, j):
            return slice((r * hp + j) * CHUNK, (r * hp + j + 1) * CHUNK)

        @pl.when(i < nt)
        def _():
            for j in range(hp):
                cs = slice(j * HEAD, (j + 1) * HEAD)
                qf[j] = q_ref[:, cs].astype(F32)
                dof[j] = do_ref[:, cs].astype(F32)
                kf[j] = k_ref[:, cs].astype(F32)
                vf[j] = v_ref[:, cs].astype(F32)
            lses, deltas = [], []
            for r in range(d):
                lblk = l_ref[pl.ds(r, CHUNK, stride=d), :]
                dblk = d_ref[pl.ds(r, CHUNK, stride=d), :]
                for j in range(hp):
                    lses.append(_col(lblk, hh * hp + j))
                    deltas.append(_col(dblk, hh * hp + j))
                    qb, dob = blk(qf, r, j), blk(dof, r, j)
                    sc_s[bs(r, j), :] = _dot_nt(qb, blk(kf, r, j))
                    dpc_s[bs(r, j), :] = _dot_nt(dob, blk(vf, r, j))
                    sp_s[bs(r, j), :] = _dot_nt(qb, blk(kpf, r, j))
                    dpp_s[bs(r, j), :] = _dot_nt(dob, blk(vpf, r, j))
            lse = jnp.concatenate(lses, axis=0)
            delta = jnp.concatenate(deltas, axis=0)
            row = lax.broadcasted_iota(jnp.int32, (nblocks * CHUNK, CHUNK), 0) & (CHUNK - 1)
            col = lax.broadcasted_iota(jnp.int32, (nblocks * CHUNK, CHUNK), 1)
            sp = jnp.where(col >= row, sp_s[...] * ATT_SCALE, NEG) + jnp.where(i > 0, 0.0, NEG)
            pp = jnp.exp(sp - lse)
            pp_s[...] = pp.astype(BF16)
            dsp_s[...] = (pp * (dpp_s[...] - delta)).astype(BF16)
            sc = jnp.where(col <= row, sc_s[...] * ATT_SCALE, NEG)
            pc = jnp.exp(sc - lse)
            pc_s[...] = pc.astype(BF16)
            dsc_s[...] = (pc * (dpc_s[...] - delta)).astype(BF16)
            for r in range(d):
                rows = pl.ds(r, CHUNK, stride=d)
                for j in range(hp):
                    qb, dob = blk(qf, r, j), blk(dof, r, j)
                    dsc, dsp = dsc_s[bs(r, j), :], dsp_s[bs(r, j), :]
                    dqf.at[j][rows, :] = (_dot(dsc, blk(kf, r, j)) + _dot(dsp, blk(kpf, r, j))) * ATT_SCALE
                    newk.at[j][rows, :] = _dot_tn(dsc, qb) * ATT_SCALE
                    newv.at[j][rows, :] = _dot_tn(pc_s[bs(r, j), :], dob)
                    acck.at[j][rows, :] += _dot_tn(dsp, qb) * ATT_SCALE
                    accv.at[j][rows, :] += _dot_tn(pp_s[bs(r, j), :], dob)
            for j in range(hp):
                dq_ref[:, j * HEAD:(j + 1) * HEAD] = dqf[j].astype(BF16)

        @pl.when(i > 0)
        def _():
            for j in range(hp):
                dk_ref[:, j * HEAD:(j + 1) * HEAD] = acck[j].astype(BF16)
                dv_ref[:, j * HEAD:(j + 1) * HEAD] = accv[j].astype(BF16)

        @pl.when(i < nt)
        def _():
            acck[...] = newk[...]
            accv[...] = newv[...]
            kpf[...] = kf[...]
            vpf[...] = vf[...]

    def cur(width, cb, nsplit):
        return pl.BlockSpec((T, width), lambda hh, i: (jnp.minimum(i, nt - 1), cb * nsplit + hh * (nsplit > 1)))

    def lag():
        return pl.BlockSpec((T, cw), lambda hh, i: (jnp.maximum(i - 1, 0), hh))

    tile = pltpu.VMEM((hp, T, HEAD), F32)
    f32s = pltpu.VMEM((nblocks * CHUNK, CHUNK), F32)
    b16s = pltpu.VMEM((nblocks * CHUNK, CHUNK), BF16)
    return pl.pallas_call(
        body, name=f"attn_bwd_dilated_{d}",
        grid=(nh, nt + 1),
        in_specs=[cur(cw, qcb, nh), cur(cw, kcb, nh), cur(cw, vcb, nh), cur(cw, 0, nh), cur(128, 0, 1), cur(128, 0, 1)],
        out_specs=[cur(cw, 0, nh), lag(), lag()],
        out_shape=[jax.ShapeDtypeStruct((S, WIDTH), BF16)] * 3,
        scratch_shapes=[tile] * 11 + [f32s] * 4 + [b16s] * 4,
        compiler_params=_params(("arbitrary", "arbitrary")),
    )(proj, proj, proj, do, lse, delta)


def _abm_bwd(proj, dy3, ln_g, ln_b, wsm, wsm_t, bias_full, pool_w, pool_wt, pool_scale, kv):
    S = proj.shape[0]
    tm = 512
    nchunk = tm // CHUNK
    nblk = S // CHUNK

    def body(u_ref, v_ref, ag_ref, p_ref, ph_ref, pg_ref, pgn_ref, mq_ref, mg_ref, dy_ref, dypn_ref,
             lng_ref, lnb_ref, wsm_ref, wsmt_ref, bias_ref, pw_ref, pwt_ref, ps_ref, kv_ref,
             dab_ref, dm_ref, dlng_ref, dlnb_ref, dws_ref, dbias_ref, dpw_ref, dps_ref, dkv_ref,
             mix, dvl, ddn):
        i = pl.program_id(0)

        @pl.when(i == 0)
        def _():
            for r in (dlng_ref, dlnb_ref, dws_ref, dbias_ref, dpw_ref, dps_ref, dkv_ref):
                r[...] = jnp.zeros_like(r)

        au = u_ref[...].astype(F32)
        av = v_ref[...].astype(F32)
        ag = ag_ref[...].astype(F32)
        u = _gelu(au)
        v = _gelu(av)
        vhat, rstd = _layer_norm_fwd(v)
        vln = (vhat * lng_ref[...] + lnb_ref[...]).astype(BF16)
        for c in range(nchunk):
            for h in range(N_HEAD):
                rs, cs = slice(c * CHUNK, (c + 1) * CHUNK), slice(h * HEAD, (h + 1) * HEAD)
                mix[rs, cs] = _dot(wsm_ref[h], vln[rs, cs]) + bias_ref[:, cs]
        dya = dy_ref[0].astype(F32)
        sg = _silu(ag)
        mixed = mix[...]
        dab_ref[:, 2 * WIDTH:3 * WIDTH] = (dya * u * mixed * _dsilu(ag)).astype(BF16)
        dab_ref[:, 0:WIDTH] = (dya * mixed * sg * _dgelu(au)).astype(BF16)
        dmixed = dya * u * sg
        dmb = dmixed.astype(BF16)
        tril = (lax.broadcasted_iota(jnp.int32, (CHUNK, CHUNK), 1)
                <= lax.broadcasted_iota(jnp.int32, (CHUNK, CHUNK), 0))
        for c in range(nchunk):
            rs = slice(c * CHUNK, (c + 1) * CHUNK)
            dbias_ref[...] += dmixed[rs, :]
            for h in range(N_HEAD):
                cs = slice(h * HEAD, (h + 1) * HEAD)
                dvl[rs, cs] = _dot(wsmt_ref[h], dmb[rs, cs])
                dws_ref[h] += jnp.where(tril, _dot_nt(dmb[rs, cs], vln[rs, cs]), 0.0)
        dvln = dvl[...]
        dlng_ref[...] += jnp.sum(dvln * vhat, axis=0, keepdims=True)
        dlnb_ref[...] += jnp.sum(dvln, axis=0, keepdims=True)
        dvh = dvln * lng_ref[...]
        dv = rstd * (dvh - jnp.mean(dvh, axis=-1, keepdims=True)
                     - vhat * jnp.mean(dvh * vhat, axis=-1, keepdims=True))
        dab_ref[:, WIDTH:2 * WIDTH] = (dv * _dgelu(av)).astype(BF16)

        halo_ok = (i > 0).astype(F32)
        for c in range(nchunk):
            rs = slice(c * CHUNK, (c + 1) * CHUNK)
            for g, win in enumerate(POOL_WINDOWS):
                cs = slice(g * HEAD, (g + 1) * HEAD)
                bcur, bprev = _band_masks(win)
                cur = p_ref[rs, cs]
                if c == 0:
                    prev = (ph_ref[:, cs].astype(F32) * halo_ok).astype(BF16)
                else:
                    prev = p_ref[(c - 1) * CHUNK:c * CHUNK, cs]
                sums = _dot(bcur, cur) + _dot(bprev, prev)
                dvl[rs, cs] = sums * _inv_count(i * tm + c * CHUNK, win) - cur.astype(F32)
        dmat = dvl[...].astype(BF16)
        for g in range(4):
            cs = slice(g * HEAD, (g + 1) * HEAD)
            mix[:, cs] = _dot(dmat[:, cs], pw_ref[g])
        yg = mix[...]
        pg = pg_ref[...].astype(F32)
        dyp = dy_ref[1].astype(F32)
        dyy = dyp * _silu(pg)
        scale = ps_ref[...]
        dab_ref[:, 4 * WIDTH:5 * WIDTH] = (dyp * yg * scale * _dsilu(pg)).astype(BF16)
        dps_ref[...] += jnp.sum(dyy * yg, axis=0, keepdims=True)
        dyg = (dyy * scale).astype(BF16)
        for g in range(4):
            cs = slice(g * HEAD, (g + 1) * HEAD)
            dpw_ref[g] += _dot_tn(dmat[:, cs], dyg[:, cs])
            mix[:, cs] = _dot(dyg[:, cs], pwt_ref[g])
        next_ok = (i + 1 < S // tm).astype(F32)
        dygn = (dypn_ref[...].astype(F32) * _silu(pgn_ref[...].astype(F32)) * scale * next_ok).astype(BF16)
        for c in range(nchunk + 1):
            for g, win in enumerate(POOL_WINDOWS):
                cs = slice(g * HEAD, (g + 1) * HEAD)
                if c < nchunk:
                    dd = mix[c * CHUNK:(c + 1) * CHUNK, cs]
                else:
                    dd = _dot(dygn[:, cs], pwt_ref[g])
                ddn[c * CHUNK:(c + 1) * CHUNK, cs] = dd * _inv_count(i * tm + c * CHUNK, win)
        ddnb = ddn[...].astype(BF16)
        for c in range(nchunk):
            rs = slice(c * CHUNK, (c + 1) * CHUNK)
            ns = slice((c + 1) * CHUNK, (c + 2) * CHUNK)
            for g, win in enumerate(POOL_WINDOWS):
                cs = slice(g * HEAD, (g + 1) * HEAD)
                bcur, bprev = _band_masks(win)
                dp = _dot_tn(bcur, ddnb[rs, cs]) + _dot_tn(bprev, ddnb[ns, cs]) - mix[rs, cs]
                dab_ref[rs, 3 * WIDTH + g * HEAD:3 * WIDTH + (g + 1) * HEAD] = dp.astype(BF16)

        mg = mg_ref[...].astype(F32)
        dym = dy_ref[2].astype(F32)
        dob = (dym * _silu(mg)).astype(BF16)
        for h in range(N_HEAD):
            cs = slice(h * HEAD, (h + 1) * HEAD)
            vs = slice(WIDTH + h * HEAD, WIDTH + (h + 1) * HEAD)
            qh = mq_ref[:, cs]
            p = _mem_softmax(qh, kv_ref[:, cs])
            pb = p.astype(BF16)
            mix[:, cs] = _dot(pb, kv_ref[:, vs])
            dp = _dot_nt(dob[:, cs], kv_ref[:, vs])
            ds = (p * (dp - jnp.sum(p * dp, axis=-1, keepdims=True))).astype(BF16)
            dm_ref[:, cs] = (_dot(ds, kv_ref[:, cs]) * ATT_SCALE).astype(BF16)
            dkv_ref[:, cs] += _dot_tn(ds, qh) * ATT_SCALE
            dkv_ref[:, vs] += _dot_tn(pb, dob[:, cs])
        dm_ref[:, WIDTH:2 * WIDTH] = (dym * mix[...] * _dsilu(mg)).astype(BF16)

    blk = tm // CHUNK
    small = [_full((1, WIDTH)), _full((1, WIDTH)), _full((N_HEAD, CHUNK, CHUNK)), _full((CHUNK, WIDTH)),
             _full((4, HEAD, HEAD)), _full((1, WIDTH)), _full((MEM_LEN, 2 * WIDTH))]
    return pl.pallas_call(
        body, name="abm_bwd",
        grid=(S // tm,),
        in_specs=[_rows(tm, WIDTH, CB_U), _rows(tm, WIDTH, CB_V), _rows(tm, WIDTH, CB_AGATE),
                  _rows(tm, WIDTH, CB_PIN),
                  pl.BlockSpec((CHUNK, WIDTH), lambda i: (jnp.maximum(i * blk - 1, 0), CB_PIN)),
                  _rows(tm, WIDTH, CB_PGATE),
                  pl.BlockSpec((CHUNK, WIDTH), lambda i: (jnp.minimum(i * blk + blk, nblk - 1), CB_PGATE)),
                  _rows(tm, WIDTH, CB_MQ), _rows(tm, WIDTH, CB_MGATE),
                  pl.BlockSpec((3, tm, WIDTH), lambda i: (0, i, 0)),
                  pl.BlockSpec((None, CHUNK, WIDTH), lambda i: (1, jnp.minimum(i * blk + blk, nblk - 1), 0)),
                  _full((1, WIDTH)), _full((1, WIDTH)), _full((N_HEAD, CHUNK, CHUNK)), _full((N_HEAD, CHUNK, CHUNK)),
                  _full((CHUNK, WIDTH)), _full((4, HEAD, HEAD)), _full((4, HEAD, HEAD)), _full((1, WIDTH)),
                  _full((MEM_LEN, 2 * WIDTH))],
        out_specs=[_rows(tm, 5 * WIDTH), _rows(tm, 2 * WIDTH)] + small,
        out_shape=[jax.ShapeDtypeStruct((S, 5 * WIDTH), BF16), jax.ShapeDtypeStruct((S, 2 * WIDTH), BF16),
                   jax.ShapeDtypeStruct((1, WIDTH), F32), jax.ShapeDtypeStruct((1, WIDTH), F32),
                   jax.ShapeDtypeStruct((N_HEAD, CHUNK, CHUNK), F32), jax.ShapeDtypeStruct((CHUNK, WIDTH), F32),
                   jax.ShapeDtypeStruct((4, HEAD, HEAD), F32), jax.ShapeDtypeStruct((1, WIDTH), F32),
                   jax.ShapeDtypeStruct((MEM_LEN, 2 * WIDTH), F32)],
        scratch_shapes=[pltpu.VMEM((tm, WIDTH), F32), pltpu.VMEM((tm, WIDTH), F32),
                        pltpu.VMEM((tm + CHUNK, WIDTH), F32)],
        compiler_params=_params(("arbitrary",)),
    )(proj, proj, proj, proj, proj, proj, proj, proj, proj, dy3, dy3,
      ln_g, ln_b, wsm, wsm_t, bias_full, pool_w, pool_wt, pool_scale, kv)


def _bias_reduce(dbias_full):
    def body(d_ref, o_ref):
        d = d_ref[...]
        o_ref[...] = _put_cols([jnp.sum(d[:, h * HEAD:(h + 1) * HEAD], axis=1, keepdims=True) for h in range(N_HEAD)])

    return pl.pallas_call(body, name="bias_reduce", out_shape=jax.ShapeDtypeStruct((CHUNK, 128), F32))(dbias_full)


def _mem_bwd(mem, g, mem_n, w, dkv):
    def body(m_ref, g_ref, mn_ref, w_ref, dkv_ref, dw_ref, dg_ref):
        dkvb = dkv_ref[...].astype(BF16)
        dw_ref[...] = _dot_tn(mn_ref[...], dkvb).astype(BF16)
        dmn = _dot_nt(dkvb, w_ref[...])
        xf = m_ref[...]
        r = lax.rsqrt(jnp.mean(xf * xf, axis=-1, keepdims=True) + EPS)
        dg_ref[...] = jnp.sum(dmn * xf * r, axis=0, keepdims=True)

    return pl.pallas_call(
        body, name="mem_bwd",
        out_shape=[jax.ShapeDtypeStruct((D_MODEL, 2 * WIDTH), BF16), jax.ShapeDtypeStruct((1, D_MODEL), F32)],
        compiler_params=pltpu.CompilerParams(vmem_limit_bytes=VMEM_LIMIT),
    )(mem, g, mem_n, w, dkv)


def _dh_bwd(dproj, w, x, g, dxo, parts=()):
    S = x.shape[0]
    tm, tk = 1024, 1536
    nk = D_IN // tk
    ni = S // tm
    n = len(parts)

    def body(dp_ref, w_ref, x_ref, g_ref, dxo_ref, *rest):
        p_in = rest[:n]
        dx_ref, dg_ref = rest[n:n + 2]
        p_out, acc, sems = rest[n + 2:2 * n + 2], rest[2 * n + 2], rest[2 * n + 3:]
        i, kk = pl.program_id(0), pl.program_id(1)

        @pl.when(jnp.logical_and(i == 0, kk == 0))
        def _():
            dg_ref[...] = jnp.zeros_like(dg_ref)
            if n:
                _comm_start(_rs_second(p_in, p_out, *sems))

        @pl.when(kk == 0)
        def _():
            acc[...] = jnp.zeros_like(acc)

        acc[...] += _dot_nt(dp_ref[...], w_ref[...])

        @pl.when(kk == nk - 1)
        def _():
            xf = x_ref[...]
            r = lax.rsqrt(jnp.mean(xf * xf, axis=-1, keepdims=True) + EPS)
            xhat = xf * r
            dh = acc[...]
            dg_ref[...] += jnp.sum(dh * xhat, axis=0, keepdims=True)
            dxh = dh * g_ref[...]
            dx_ref[...] = dxo_ref[...] + r * (dxh - xhat * jnp.mean(dxh * xhat, axis=-1, keepdims=True))

        if n:
            @pl.when(jnp.logical_and(i == ni - 1, kk == nk - 1))
            def _():
                _comm_wait(_rs_second(p_in, p_out, *sems))

    res = pl.pallas_call(
        body, name="dh_bwd_scatter" if n else "dh_bwd",
        grid=(ni, nk),
        in_specs=[pl.BlockSpec((tm, tk), lambda i, k: (i, k)), pl.BlockSpec((D_MODEL, tk), lambda i, k: (0, k)),
                  pl.BlockSpec((tm, D_MODEL), lambda i, k: (i, 0)), pl.BlockSpec((1, D_MODEL), lambda i, k: (0, 0)),
                  pl.BlockSpec((tm, D_MODEL), lambda i, k: (i, 0))] + [ANY] * n,
        out_specs=[pl.BlockSpec((tm, D_MODEL), lambda i, k: (i, 0)), pl.BlockSpec((1, D_MODEL), lambda i, k: (0, 0))]
                  + [ANY] * n,
        out_shape=[jax.ShapeDtypeStruct((S, D_MODEL), F32), jax.ShapeDtypeStruct((1, D_MODEL), F32)]
                  + [jax.ShapeDtypeStruct(p.shape, p.dtype) for p in parts],
        scratch_shapes=[pltpu.VMEM((tm, D_MODEL), F32)] + (_dma_sems(3 * n, 3 * n, n) if n else []),
        compiler_params=_params(("arbitrary", "arbitrary")),
    )(dproj, w, x, g, dxo, *parts)
    return res[0], res[1], list(res[2:])


def _matmul_tn(a, b, tn, name):
    K, M = a.shape
    N = b.shape[1]
    tk = 1024
    nk = K // tk

    def body(a_ref, b_ref, o_ref, acc):
        kk = pl.program_id(1)

        @pl.when(kk == 0)
        def _():
            acc[...] = jnp.zeros_like(acc)

        acc[...] += _dot_tn(a_ref[...].astype(BF16), b_ref[...].astype(BF16))

        @pl.when(kk == nk - 1)
        def _():
            o_ref[...] = acc[...].astype(BF16)

    return pl.pallas_call(
        body, name=name,
        grid=(N // tn, nk),
        in_specs=[pl.BlockSpec((tk, M), lambda j, k: (k, 0)), pl.BlockSpec((tk, tn), lambda j, k: (k, j))],
        out_specs=pl.BlockSpec((M, tn), lambda j, k: (0, j)),
        out_shape=jax.ShapeDtypeStruct((M, N), BF16),
        scratch_shapes=[pltpu.VMEM((M, tn), F32)],
        compiler_params=_params(("parallel", "arbitrary")),
    )(a, b)


def _row_tile(R, C):
    for cand in (512, 256, 128, 64, 32, 16, 8):
        if R % cand == 0 and cand * C * 4 <= (2 << 20):
            return cand
    return R


def _adamw_update(p_ref, w_ref, m_ref, v_ref, g_ref, d_ref, nm_ref, nv_ref):
    c1 = 1.0 / (1.0 - ADAM_B1 ** ADAM_STEP)
    c2 = 1.0 / (1.0 - ADAM_B2 ** ADAM_STEP)
    g = p_ref[0].astype(F32)
    for k in range(1, p_ref.shape[0]):
        g = g + p_ref[k].astype(F32)
    nm = ADAM_B1 * m_ref[...] + (1.0 - ADAM_B1) * g
    nv = ADAM_B2 * v_ref[...] + (1.0 - ADAM_B2) * (g * g)
    g_ref[...] = g
    nm_ref[...] = nm
    nv_ref[...] = nv
    d_ref[...] = -ADAM_LR * ((nm * c1) / (jnp.sqrt(nv * c2) + ADAM_EPS) + ADAM_WD * w_ref[...])


def _adamw(parts, w, m, v, name):
    P, R, C = parts.shape
    tr = _row_tile(R, C)

    def body(*refs):
        _adamw_update(*refs)

    spec = pl.BlockSpec((tr, C), lambda i: (i, 0))
    return pl.pallas_call(
        body, name=name,
        grid=(R // tr,),
        in_specs=[pl.BlockSpec((P, tr, C), lambda i: (0, i, 0)), spec, spec, spec],
        out_specs=[spec] * 4,
        out_shape=[jax.ShapeDtypeStruct((R, C), F32)] * 4,
        compiler_params=_params(("parallel",)),
    )(parts, w, m, v)


def _adamw_layers(parts, w, m, v, name):
    depth = len(parts)
    P, R, C = parts[0].shape
    tr = _row_tile(R, C)

    def body(*refs):
        layer = pl.program_id(0)
        for k in range(depth):
            @pl.when(layer == k)
            def _(k=k):
                _adamw_update(refs[k], *refs[depth:])

    def part_spec(k):
        return pl.BlockSpec((P, tr, C), lambda l, i: (0, jnp.where(l == k, i, 0), 0))

    spec = pl.BlockSpec((None, tr, C), lambda l, i: (l, i, 0))
    return pl.pallas_call(
        body, name=name,
        grid=(depth, R // tr),
        in_specs=[part_spec(k) for k in range(depth)] + [spec] * 3,
        out_specs=[spec] * 4,
        out_shape=[jax.ShapeDtypeStruct((depth, R, C), F32)] * 4,
        compiler_params=_params(("arbitrary", "arbitrary")),
    )(*parts, w, m, v)


def _place():
    return lax.axis_index("x"), lax.axis_index("y"), lax.axis_index("c")


def _all_gather(shards):
    n = len(shards)

    def body(*refs):
        ins, outs = refs[:n], refs[n:2 * n]
        send1, recv1, local_sems, send2, recv2 = refs[2 * n:]
        first = _ag_first(ins, outs, send1, recv1, local_sems)
        second = _ag_second(outs, send2, recv2)
        _comm_start(first)
        for j in range(3):
            for a in range(n):
                first[2][4 * a + 1 + j].wait_recv()
            for a in range(n):
                second[1][3 * a + j].start()
        for a in range(n):
            first[2][4 * a].wait_recv()
        for cp in second[2]:
            cp.wait_recv()
        for cp in first[1] + second[1]:
            cp.wait_send()
        for cp in first[0]:
            cp.wait()

    return pl.pallas_call(
        body, name="weights_all_gather",
        in_specs=[ANY] * n, out_specs=[ANY] * n,
        out_shape=[jax.ShapeDtypeStruct((N_DEV,) + s.shape, s.dtype) for s in shards],
        scratch_shapes=_dma_sems(4 * n, 4 * n, n, 3 * n, 3 * n),
        compiler_params=pltpu.CompilerParams(has_side_effects=True),
    )(*shards)


N_BIG = 4


def _dev(p):
    return 4 * p[0] + 2 * p[1] + p[2]


def _other_chips(x, y):
    return [(1 - x, y), (x, 1 - y), (1 - x, 1 - y)]


def _remote(src, dst, send_sems, recv_sems, k, to):
    return pltpu.make_async_remote_copy(src_ref=src, dst_ref=dst, send_sem=send_sems.at[k], recv_sem=recv_sems.at[k],
                                        device_id=to, device_id_type=MESH)


def _ag_first(ins, outs, send_sems, recv_sems, local_sems):
    x, y, c = _place()
    me = (x, y, c)
    targets = [(x, y, 1 - c)] + [(*chip, c) for chip in _other_chips(x, y)]
    local, out, inc = [], [], []
    for a in range(len(ins)):
        local.append(pltpu.make_async_copy(ins[a], outs[a].at[_dev(me)], local_sems.at[a]))
        for k, to in enumerate(targets):
            out.append(_remote(ins[a], outs[a].at[_dev(me)], send_sems, recv_sems, 4 * a + k, to))
            inc.append(_remote(ins[a], outs[a].at[_dev(to)], send_sems, recv_sems, 4 * a + k, to))
    return local, out, inc


def _ag_second(bufs, send_sems, recv_sems):
    x, y, c = _place()
    out, inc = [], []
    for a in range(len(bufs)):
        for j, chip in enumerate(_other_chips(x, y)):
            mine, theirs = bufs[a].at[_dev((*chip, c))], bufs[a].at[_dev((*chip, 1 - c))]
            out.append(_remote(mine, mine, send_sems, recv_sems, 3 * a + j, (x, y, 1 - c)))
            inc.append(_remote(theirs, theirs, send_sems, recv_sems, 3 * a + j, (x, y, 1 - c)))
    return [], out, inc


def _rs_first(ins, outs, send_sems, recv_sems):
    x, y, c = _place()
    out = [_remote(ins[a].at[1 - c], outs[a], send_sems, recv_sems, a, (x, y, 1 - c)) for a in range(len(ins))]
    return [], out, out


def _rs_second(ins, outs, send_sems, recv_sems, local_sems):
    x, y, c = _place()
    my_chip = 2 * x + y
    local, out, inc = [], [], []
    for a in range(len(ins)):
        local.append(pltpu.make_async_copy(ins[a].at[my_chip], outs[a].at[my_chip], local_sems.at[a]))
        for k, (ox, oy) in enumerate(_other_chips(x, y)):
            out.append(_remote(ins[a].at[2 * ox + oy], outs[a].at[my_chip], send_sems, recv_sems, 3 * a + k, (ox, oy, c)))
            inc.append(_remote(ins[a].at[2 * ox + oy], outs[a].at[2 * ox + oy], send_sems, recv_sems, 3 * a + k,
                               (ox, oy, c)))
    return local, out, inc


def _comm_start(exchange):
    local, out, _ = exchange
    for cp in local + out:
        cp.start()


def _comm_wait(exchange):
    local, out, inc = exchange
    for cp in inc:
        cp.wait_recv()
    for cp in out:
        cp.wait_send()
    for cp in local:
        cp.wait()


def _dma_sems(*counts):
    return [pltpu.SemaphoreType.DMA((n,)) for n in counts]


def _rs_sibling(grads):
    n = len(grads)

    def body(*refs):
        ex = _rs_first(refs[:n], refs[n:2 * n], *refs[2 * n:])
        _comm_start(ex)
        _comm_wait(ex)

    return pl.pallas_call(
        body, name="grads_to_sibling",
        in_specs=[ANY] * n, out_specs=[ANY] * n,
        out_shape=[jax.ShapeDtypeStruct(g.shape[1:], g.dtype) for g in grads],
        scratch_shapes=_dma_sems(n, n),
        compiler_params=pltpu.CompilerParams(has_side_effects=True),
    )(*grads)


def _pair_sum(grads, recvs):
    n = len(grads)

    def body(c_ref, *refs):
        for a in range(n):
            refs[2 * n + a][...] = (refs[a][...].astype(F32) + refs[n + a][...].astype(F32)).astype(BF16)

    def g_spec(g):
        return pl.BlockSpec((None, None) + g.shape[2:], lambda j, c_ref: (c_ref[0], j, 0, 0))

    def r_spec(r):
        return pl.BlockSpec((None,) + r.shape[1:], lambda j, c_ref: (j, 0, 0))

    return pl.pallas_call(
        body, name="pair_sum",
        grid_spec=pltpu.PrefetchScalarGridSpec(
            num_scalar_prefetch=1, grid=(N_CHIP,),
            in_specs=[g_spec(g) for g in grads] + [r_spec(r) for r in recvs],
            out_specs=[r_spec(r) for r in recvs]),
        out_shape=[jax.ShapeDtypeStruct(r.shape, BF16) for r in recvs],
        compiler_params=_params(("parallel",)),
    )(lax.axis_index("c").reshape(1).astype(jnp.int32), *grads, *recvs)


def _rs_chips(parts):
    n = len(parts)

    def body(*refs):
        ex = _rs_second(refs[:n], refs[n:2 * n], *refs[2 * n:])
        _comm_start(ex)
        _comm_wait(ex)

    return pl.pallas_call(
        body, name="grads_to_chips",
        in_specs=[ANY] * n, out_specs=[ANY] * n,
        out_shape=[jax.ShapeDtypeStruct(p.shape, p.dtype) for p in parts],
        scratch_shapes=_dma_sems(3 * n, 3 * n, n),
        compiler_params=pltpu.CompilerParams(has_side_effects=True),
    )(*parts)


SMALL_ROWS = 544


def _all_reduce_small(buf):
    def body(in_ref, out_ref, recv, acc, send1, recv1, send2, recv2):
        x, y, c = _place()
        me = 4 * x + 2 * y + c
        peers = [(x ^ (r >> 2), y ^ ((r >> 1) & 1), c ^ (r & 1)) for r in range(1, N_DEV)]

        def idx(p):
            return 4 * p[0] + 2 * p[1] + p[2]

        first = [pltpu.make_async_remote_copy(
            src_ref=in_ref.at[idx(p)], dst_ref=recv.at[me], send_sem=send1.at[r], recv_sem=recv1.at[r],
            device_id=p, device_id_type=MESH) for r, p in enumerate(peers)]
        for cp in first:
            cp.start()
        recv[me] = in_ref[me]
        for r, p in enumerate(peers):
            pltpu.make_async_remote_copy(
                src_ref=in_ref.at[idx(p)], dst_ref=recv.at[idx(p)], send_sem=send1.at[r], recv_sem=recv1.at[r],
                device_id=p, device_id_type=MESH).wait_recv()
        total = recv[0]
        for k in range(1, N_DEV):
            total = total + recv[k]
        acc[...] = total
        out_ref[me] = total
        second = [pltpu.make_async_remote_copy(
            src_ref=acc, dst_ref=out_ref.at[me], send_sem=send2.at[r], recv_sem=recv2.at[r],
            device_id=p, device_id_type=MESH) for r, p in enumerate(peers)]
        for cp in second:
            cp.start()
        for r, p in enumerate(peers):
            pltpu.make_async_remote_copy(
                src_ref=acc, dst_ref=out_ref.at[idx(p)], send_sem=send2.at[r], recv_sem=recv2.at[r],
                device_id=p, device_id_type=MESH).wait_recv()
        for cp in first + second:
            cp.wait_send()

    vm = pl.BlockSpec(memory_space=pltpu.VMEM)
    return pl.pallas_call(
        body, name="small_grads_all_reduce",
        in_specs=[vm], out_specs=vm,
        out_shape=jax.ShapeDtypeStruct(buf.shape, F32),
        scratch_shapes=[pltpu.VMEM(buf.shape, F32), pltpu.VMEM(buf.shape[1:], F32),
                        pltpu.SemaphoreType.DMA((7,)), pltpu.SemaphoreType.DMA((7,)),
                        pltpu.SemaphoreType.DMA((7,)), pltpu.SemaphoreType.DMA((7,))],
        compiler_params=pltpu.CompilerParams(has_side_effects=True, vmem_limit_bytes=VMEM_LIMIT),
    )(buf)


def _dilate(a, d):
    if d == 1:
        return a
    S, C = a.shape
    return a.reshape(S // d, d, C).transpose(1, 0, 2).reshape(S, C)


def _undilate(a, d):
    if d == 1:
        return a
    S, C = a.shape
    return a.reshape(d, S // d, C).transpose(1, 0, 2).reshape(S, C)


def _cols(a, cb, n=1):
    return a[:, cb * WIDTH:(cb + n) * WIDTH]


def _to_blocks(g, kind):
    if kind == "cols":
        R = g.shape[0]
        return g.reshape(R, N_CHIP, 2, -1).transpose(2, 1, 0, 3)
    if kind == "rows":
        C = g.shape[1]
        return g.reshape(N_CHIP, 2, -1, C).transpose(1, 0, 2, 3)
    return g.reshape(4 * WIDTH, N_CHIP, 2, -1).transpose(2, 1, 0, 3)


SMALL = ("norm_g", "gm_ln_g", "gm_ln_b", "gm_ws", "gm_bs", "pool_w", "pool_scale", "mem_norm_g", "final_norm_g")


def _pack_small(tree):
    flat = jnp.concatenate([tree[k].reshape(-1, 128) for k in SMALL], axis=0)
    return jnp.pad(flat, ((0, N_DEV * SMALL_ROWS - flat.shape[0]), (0, 0)))


def _unpack_small(flat, like):
    out, at = {}, 0
    for k in SMALL:
        rows = like[k].size // 128
        out[k] = flat[at:at + rows].reshape(like[k].shape)
        at += rows
    return out


def _make_layer(win, wkv, wb, wout, norm_g, mem_norm_g, ln_g, ln_b, gm_ws, gm_bs, pool_w, pool_scale):
    tril = jnp.tril(jnp.ones((CHUNK, CHUNK), bool))
    wsm = jnp.where(tril, gm_ws, 0.0).astype(BF16)
    pw = pool_w.astype(BF16)
    return dict(win=win, wkv=wkv, wb=wb, wout=wout, g=norm_g[None], mg=mem_norm_g[None], ln_g=ln_g[None],
                ln_b=ln_b[None], wsm=wsm, wsm_t=wsm.transpose(0, 2, 1), pw=pw, pw_t=pw.transpose(0, 2, 1),
                ps=pool_scale[None], bias=jnp.repeat(gm_bs.T, HEAD, axis=1))


def _layer_fwd(xl, mem0, L, next_shards=()):
    S = xl.shape[0]
    proj, h, half_gathered = _in_proj(xl, L["g"], L["win"], next_shards)
    kv, mem_n = _mem_kv(mem0, L["mg"], L["wkv"])
    y3, gathered = _abm_fwd(proj, L["ln_g"], L["ln_b"], L["wsm"], L["bias"], L["pw"], L["ps"], kv, half_gathered)
    o_g, l_g = [], []
    for gi, d in enumerate(DILATIONS):
        if d == 1:
            o, lse = _attn_fwd(proj, CB_Q0, proj, CB_K, proj, CB_CV, S // CHUNK)
        else:
            o, lse = _attn_fwd_dilated(proj, CB_Q0 + gi, CB_K, CB_CV, d)
        o_g.append(o)
        l_g.append(lse)
    xn, yc, oc, lse, z = _merge_fwd(xl, y3, o_g, l_g, proj, L["wb"], L["wout"])
    return xn, dict(x=xl, proj=proj, h=h, kv=kv, mem_n=mem_n, y3=y3, yc=yc, oc=oc, lse=lse, z=z), gathered


def _layer_bwd(dx, mem0, L, sv, later=()):
    S = dx.shape[0]
    proj = sv["proj"]
    (dy3, doc, delta, dcg, dgm, dt), from_sibling = _merge_bwd(dx, sv["y3"], sv["yc"], sv["oc"], proj, L["wb"],
                                                              L["wout"], later)
    pair = _pair_sum(later, from_sibling) if later else ()
    dwout = _matmul_tn(sv["z"], dx, D_MODEL, "dw_out")
    ys = (sv["y3"][0], sv["y3"][1], sv["yc"], sv["y3"][2])
    dwb = jnp.stack([_matmul_tn(ys[b], dt[b], D_MODEL, "dw_branch") for b in range(4)])
    dab, dm, dlng, dlnb, dws, dbias, dpw, dps, dkv = _abm_bwd(
        proj, dy3, L["ln_g"], L["ln_b"], L["wsm"], L["wsm_t"], L["bias"], L["pw"], L["pw_t"], L["ps"], sv["kv"])
    dq, dk, dv = [], None, None
    for gi, d in enumerate(DILATIONS):
        if d == 1:
            r = _attn_bwd(proj, CB_Q0, proj, CB_K, proj, CB_CV, doc, sv["lse"], delta, S // CHUNK)
        else:
            r = _attn_bwd_dilated(proj, CB_Q0 + gi, CB_K, CB_CV, doc, sv["lse"], delta, d)
        dq.append(r[0])
        dkg, dvg = r[1].astype(F32), r[2].astype(F32)
        dk = dkg if dk is None else dk + dkg
        dv = dvg if dv is None else dv + dvg
    dproj = jnp.concatenate([dab, dq[0], dq[1], dq[2], dk.astype(BF16), dv.astype(BF16), dcg, dm, dgm], axis=1)
    dwkv, dmg = _mem_bwd(mem0, L["mg"], sv["mem_n"], L["wkv"], dkv)
    dwin = _matmul_tn(sv["h"], dproj, 1536, "dw_in")
    dxi, dng, parts = _dh_bwd(dproj, L["win"], sv["x"], L["g"], dx, pair)
    big = dict(w_in=dwin, w_mem_kv=dwkv, w_branch=dwb, w_out=dwout)
    small = dict(norm_g=dng[0], gm_ln_g=dlng[0], gm_ln_b=dlnb[0], gm_ws=dws,
                 gm_bs=_bias_reduce(dbias)[:, :N_HEAD].T, pool_w=dpw, pool_scale=dps[0], mem_norm_g=dmg[0])
    return dxi, big, small, parts


BIG = ("w_in", "w_mem_kv", "w_branch", "w_out")


def _blocked(big):
    return [_to_blocks(big["w_in"], "cols"), _to_blocks(big["w_mem_kv"], "rows"),
            _to_blocks(big["w_branch"], "branch"), _to_blocks(big["w_out"], "rows")]


def _full_weights(gathered):
    win, wkv, wb, wout = gathered
    return (win.transpose(1, 0, 2).reshape(D_MODEL, D_IN), wkv.reshape(D_MODEL, 2 * WIDTH),
            wb.reshape(N_DEV, 4, WIDTH, -1).transpose(1, 2, 0, 3).reshape(4, WIDTH, D_MODEL),
            wout.reshape(D_MODEL, D_MODEL))


def kernel(x, mem, norm_g, w_in, gm_ln_g, gm_ln_b, gm_ws, gm_bs, pool_w, pool_scale, mem_norm_g, w_mem_kv, w_branch, w_out, final_norm_g, loss_target, m_norm_g, m_w_in, m_gm_ln_g, m_gm_ln_b, m_gm_ws, m_gm_bs, m_pool_w, m_pool_scale, m_mem_norm_g, m_w_mem_kv, m_w_branch, m_w_out, m_final_norm_g, v_norm_g, v_w_in, v_gm_ln_g, v_gm_ln_b, v_gm_ws, v_gm_bs, v_pool_w, v_pool_scale, v_mem_norm_g, v_w_mem_kv, v_w_branch, v_w_out, v_final_norm_g):
    x0 = x[0]
    mem0 = mem[0]
    tgt = loss_target[0]
    S = x0.shape[0]

    shards = [[w_in[l].astype(BF16), w_mem_kv[l].astype(BF16), w_branch[l].astype(BF16).reshape(4 * WIDTH, -1),
               w_out[l].astype(BF16)] for l in range(DEPTH)]
    gathered = _all_gather(shards[0])
    layers, saved = [], []
    xl = x0
    for l in range(DEPTH):
        layers.append(_make_layer(*_full_weights(gathered), norm_g[l], mem_norm_g[l], gm_ln_g[l], gm_ln_b[l],
                                  gm_ws[l], gm_bs[l], pool_w[l], pool_scale[l]))
        xl, sv, gathered = _layer_fwd(xl, mem0, layers[l], shards[l + 1] if l + 1 < DEPTH else ())
        saved.append(sv)

    loss_part, dx, d_final = _loss_head(xl, final_norm_g[None], tgt)
    loss = lax.psum(loss_part[0, 0], ("x", "y", "c"))

    small = {k: [None] * DEPTH for k in SMALL if k != "final_norm_g"}
    parts = [None] * DEPTH
    later = ()
    for l in reversed(range(DEPTH)):
        dx, gb, gs, done = _layer_bwd(dx, mem0, layers[l], saved[l], later)
        if later:
            parts[l + 1] = done
        later = _blocked(gb)
        for k in gs:
            small[k][l] = gs[k]
    grad_x = dx[None]
    parts[0] = _rs_chips(_pair_sum(later, _rs_sibling(later)))

    small_tree = {k: jnp.stack(small[k]) for k in small}
    small_tree["final_norm_g"] = d_final[0]
    reduced = _all_reduce_small(_pack_small(small_tree).reshape(N_DEV, SMALL_ROWS, 128))

    weights = dict(norm_g=norm_g, w_in=w_in, gm_ln_g=gm_ln_g, gm_ln_b=gm_ln_b, gm_ws=gm_ws, gm_bs=gm_bs,
                   pool_w=pool_w, pool_scale=pool_scale, mem_norm_g=mem_norm_g, w_mem_kv=w_mem_kv,
                   w_branch=w_branch, w_out=w_out, final_norm_g=final_norm_g)
    m_in = dict(norm_g=m_norm_g, w_in=m_w_in, gm_ln_g=m_gm_ln_g, gm_ln_b=m_gm_ln_b, gm_ws=m_gm_ws, gm_bs=m_gm_bs,
                pool_w=m_pool_w, pool_scale=m_pool_scale, mem_norm_g=m_mem_norm_g, w_mem_kv=m_w_mem_kv,
                w_branch=m_w_branch, w_out=m_w_out, final_norm_g=m_final_norm_g)
    v_in = dict(norm_g=v_norm_g, w_in=v_w_in, gm_ln_g=v_gm_ln_g, gm_ln_b=v_gm_ln_b, gm_ws=v_gm_ws, gm_bs=v_gm_bs,
                pool_w=v_pool_w, pool_scale=v_pool_scale, mem_norm_g=v_mem_norm_g, w_mem_kv=v_w_mem_kv,
                w_branch=v_w_branch, w_out=v_w_out, final_norm_g=v_final_norm_g)
    res = {}
    for a, k in enumerate(BIG):
        shape = weights[k].shape
        by_layer = [parts[l][a] for l in range(DEPTH)]
        lrc = (DEPTH,) + by_layer[0].shape[1:]
        outs = _adamw_layers(by_layer, weights[k].reshape(lrc), m_in[k].reshape(lrc), v_in[k].reshape(lrc),
                             "adamw_" + k)
        res[k] = [o.reshape(shape) for o in outs]
    outs = _adamw(reduced.reshape(1, N_DEV * SMALL_ROWS, 128), _pack_small(weights), _pack_small(m_in),
                  _pack_small(v_in), "adamw_small")
    unpacked = [_unpack_small(o, weights) for o in outs]
    for k in SMALL:
        res[k] = [u[k] for u in unpacked]

    order = ("norm_g", "w_in", "gm_ln_g", "gm_ln_b", "gm_ws", "gm_bs", "pool_w", "pool_scale", "mem_norm_g",
             "w_mem_kv", "w_branch", "w_out", "final_norm_g")
    return (loss, grad_x, *[res[k][0] for k in order], *[res[k][1] for k in order],
            *[res[k][2] for k in order], *[res[k][3] for k in order])
```

```python
import functools
import math

import jax
import jax.numpy as jnp
from jax import lax
from jax.experimental import pallas as pl
from jax.experimental.pallas import tpu as pltpu

F32 = jnp.float32
BF16 = jnp.bfloat16

D_MODEL = 1024
DEPTH = 4
WIDTH = 512
D_IN = 10752
HEAD = 128
N_HEAD = 4
CHUNK = 128
MEM_LEN = 256
POOL_WINDOWS = (2, 4, 8, 16)
DILATIONS = (1, 4, 16)
EPS = 1e-6
NEG = -1e30
ATT_SCALE = HEAD ** -0.5
N_DEV = 8
N_CHIP = 4

D_BRANCHES = 6656
D_GATES = D_IN - D_BRANCHES
CB_U, CB_V, CB_AGATE, CB_PIN, CB_PGATE = 0, 1, 2, 3, 4
CB_Q0, CB_K, CB_CV, CB_CGATE, CB_MQ, CB_MGATE = 5, 8, 9, 10, 11, 12

ADAM_LR = 0.001
ADAM_B1 = 0.9
ADAM_B2 = 0.999
ADAM_EPS = 1e-08
ADAM_WD = 0.01
ADAM_STEP = 10

VMEM_LIMIT = 56 * 1024 * 1024
MESH = pl.DeviceIdType.MESH
ANY = pl.BlockSpec(memory_space=pl.ANY)

NT = (((1,), (1,)), ((), ()))
TN = (((0,), (0,)), ((), ()))


def _dot(a, b):
    return jnp.dot(a, b, preferred_element_type=F32)


def _dot_nt(a, b):
    return lax.dot_general(a, b, NT, preferred_element_type=F32)


def _dot_tn(a, b):
    return lax.dot_general(a, b, TN, preferred_element_type=F32)


def _sigmoid(x):
    return 1.0 / (1.0 + jnp.exp(-x))


def _silu(x):
    return x * _sigmoid(x)


def _dsilu(x):
    s = _sigmoid(x)
    return s * (1.0 + x * (1.0 - s))


def _gelu(x):
    return 0.5 * x * (1.0 + lax.erf(x * (2.0 ** -0.5)))


def _dgelu(x):
    return 0.5 * (1.0 + lax.erf(x * (2.0 ** -0.5))) + x * jnp.exp(-0.5 * x * x) * (1.0 / math.sqrt(2.0 * math.pi))


def _col(blk, h):
    lane = lax.broadcasted_iota(jnp.int32, blk.shape, 1)
    return jnp.sum(jnp.where(lane == h, blk, 0.0), axis=1, keepdims=True)


def _put_cols(cols):
    rows = cols[0].shape[0]
    lane = lax.broadcasted_iota(jnp.int32, (rows, 128), 1)
    out = jnp.zeros((rows, 128), F32)
    for h, cv in enumerate(cols):
        out = jnp.where(lane == h, cv, out)
    return out


def _params(sem, vmem=VMEM_LIMIT):
    return pltpu.CompilerParams(dimension_semantics=sem, vmem_limit_bytes=vmem)


def _full(shape):
    nd = len(shape)
    return pl.BlockSpec(shape, lambda *_: (0,) * nd)


def _rows(tm, width, cb=0):
    return pl.BlockSpec((tm, width), lambda i: (i, cb))


def _in_proj(x, g, w, shards=()):
    S = x.shape[0]
    tm, tn = 1024, D_BRANCHES // 4
    n = len(shards)
    ni, nj = S // tm, D_BRANCHES // tn

    def body(x_ref, g_ref, w_ref, *rest):
        ins, (proj_ref, h_ref), outs = rest[:n], rest[n:n + 2], rest[n + 2:2 * n + 2]
        hs, sems = rest[2 * n + 2], rest[2 * n + 3:]
        i, j = pl.program_id(0), pl.program_id(1)

        if n:
            @pl.when(jnp.logical_and(i == 0, j == 0))
            def _():
                _comm_start(_ag_first(ins, outs, *sems))

        @pl.when(j == 0)
        def _():
            xf = x_ref[...]
            r = lax.rsqrt(jnp.mean(xf * xf, axis=-1, keepdims=True) + EPS)
            h = (xf * r * g_ref[...]).astype(BF16)
            hs[...] = h
            h_ref[...] = h

        proj_ref[...] = _dot(hs[...], w_ref[...]).astype(BF16)

        if n:
            @pl.when(jnp.logical_and(i == ni - 1, j == nj - 1))
            def _():
                _comm_wait(_ag_first(ins, outs, *sems))

    res = pl.pallas_call(
        body, name="in_proj_gather" if n else "in_proj",
        grid=(ni, nj),
        in_specs=[pl.BlockSpec((tm, D_MODEL), lambda i, j: (i, 0)),
                  pl.BlockSpec((1, D_MODEL), lambda i, j: (0, 0)),
                  pl.BlockSpec((D_MODEL, tn), lambda i, j: (0, j))] + [ANY] * n,
        out_specs=[pl.BlockSpec((tm, tn), lambda i, j: (i, j)),
                   pl.BlockSpec((tm, D_MODEL), lambda i, j: (i, 0))] + [ANY] * n,
        out_shape=[jax.ShapeDtypeStruct((S, D_BRANCHES), BF16), jax.ShapeDtypeStruct((S, D_MODEL), BF16)]
                  + [jax.ShapeDtypeStruct((N_DEV,) + s.shape, s.dtype) for s in shards],
        scratch_shapes=[pltpu.VMEM((tm, D_MODEL), BF16)] + (_dma_sems(4 * n, 4 * n, n) if n else []),
        compiler_params=_params(("arbitrary", "arbitrary")),
    )(x, g, w, *shards)
    return res[0], res[1], list(res[2:])


def _matmul(a, b, name):
    M, K = a.shape
    N = b.shape[1]
    tm, tn = 1024, 1024

    def body(a_ref, b_ref, o_ref):
        o_ref[...] = _dot(a_ref[...], b_ref[...]).astype(BF16)

    return pl.pallas_call(
        body, name=name,
        grid=(M // tm, N // tn),
        in_specs=[pl.BlockSpec((tm, K), lambda i, j: (i, 0)), pl.BlockSpec((K, tn), lambda i, j: (0, j))],
        out_specs=pl.BlockSpec((tm, tn), lambda i, j: (i, j)),
        out_shape=jax.ShapeDtypeStruct((M, N), BF16),
        compiler_params=_params(("parallel", "parallel")),
    )(a, b)


def _mem_kv(mem, g, w):
    M = mem.shape[0]

    def body(m_ref, g_ref, w_ref, kv_ref, mn_ref):
        xf = m_ref[...]
        r = lax.rsqrt(jnp.mean(xf * xf, axis=-1, keepdims=True) + EPS)
        mn = (xf * r * g_ref[...]).astype(BF16)
        mn_ref[...] = mn
        kv_ref[...] = _dot(mn, w_ref[...]).astype(BF16)

    return pl.pallas_call(
        body, name="mem_kv",
        out_shape=[jax.ShapeDtypeStruct((M, 2 * WIDTH), BF16), jax.ShapeDtypeStruct((M, D_MODEL), BF16)],
        compiler_params=pltpu.CompilerParams(vmem_limit_bytes=VMEM_LIMIT),
    )(mem, g, w)


def _band_masks(win):
    t = lax.broadcasted_iota(jnp.int32, (CHUNK, CHUNK), 0)
    s = lax.broadcasted_iota(jnp.int32, (CHUNK, CHUNK), 1)
    cur = jnp.logical_and(t - s >= 0, t - s < win)
    prev = s > t + CHUNK - win
    return cur.astype(BF16), prev.astype(BF16)


def _inv_count(first_row, win):
    t = first_row + lax.broadcasted_iota(jnp.int32, (CHUNK, 1), 0)
    return 1.0 / jnp.minimum(t + 1, win).astype(F32)


def _layer_norm_fwd(v):
    mu = jnp.mean(v, axis=-1, keepdims=True)
    vc = v - mu
    var = jnp.mean(vc * vc, axis=-1, keepdims=True)
    rstd = lax.rsqrt(var + EPS)
    return vc * rstd, rstd


def _mem_softmax(q, kmem):
    s = _dot_nt(q, kmem) * ATT_SCALE
    m = jnp.max(s, axis=-1, keepdims=True)
    e = jnp.exp(s - m)
    return e * (1.0 / jnp.sum(e, axis=-1, keepdims=True))


def _abm_fwd(proj, ln_g, ln_b, wsm, bias_full, pool_w, pool_scale, kv, gathered=()):
    S = proj.shape[0]
    tm = 512
    nchunk = tm // CHUNK
    n = len(gathered)
    nsteps = S // tm

    def body(u_ref, v_ref, ag_ref, p_ref, ph_ref, pg_ref, mq_ref, mg_ref, lng_ref, lnb_ref, wsm_ref, bias_ref,
             pw_ref, ps_ref, kv_ref, *rest):
        y_ref, bufs, mix, sems = rest[n], rest[n + 1:2 * n + 1], rest[2 * n + 1], rest[2 * n + 2:]
        i = pl.program_id(0)

        if n:
            @pl.when(i == 0)
            def _():
                _comm_start(_ag_second(bufs, *sems))

        u = _gelu(u_ref[...].astype(F32))
        v = _gelu(v_ref[...].astype(F32))
        vhat, _ = _layer_norm_fwd(v)
        vln = (vhat * lng_ref[...] + lnb_ref[...]).astype(BF16)
        for c in range(nchunk):
            for h in range(N_HEAD):
                rs, cs = slice(c * CHUNK, (c + 1) * CHUNK), slice(h * HEAD, (h + 1) * HEAD)
                mix[rs, cs] = _dot(wsm_ref[h], vln[rs, cs]) + bias_ref[:, cs]
        y_ref[0] = (u * mix[...] * _silu(ag_ref[...].astype(F32))).astype(BF16)
        halo_ok = (i > 0).astype(F32)
        for c in range(nchunk):
            rs = slice(c * CHUNK, (c + 1) * CHUNK)
            for g, win in enumerate(POOL_WINDOWS):
                cs = slice(g * HEAD, (g + 1) * HEAD)
                bcur, bprev = _band_masks(win)
                cur = p_ref[rs, cs]
                if c == 0:
                    prev = (ph_ref[:, cs].astype(F32) * halo_ok).astype(BF16)
                else:
                    prev = p_ref[(c - 1) * CHUNK:c * CHUNK, cs]
                sums = _dot(bcur, cur) + _dot(bprev, prev)
                dm = sums * _inv_count(i * tm + c * CHUNK, win) - cur.astype(F32)
                mix[rs, cs] = _dot(dm.astype(BF16), pw_ref[g])
        y_ref[1] = (mix[...] * ps_ref[...] * _silu(pg_ref[...].astype(F32))).astype(BF16)
        for h in range(N_HEAD):
            cs = slice(h * HEAD, (h + 1) * HEAD)
            p = _mem_softmax(mq_ref[:, cs], kv_ref[:, cs])
            mix[:, cs] = _dot(p.astype(BF16), kv_ref[:, WIDTH + h * HEAD:WIDTH + (h + 1) * HEAD])
        y_ref[2] = (mix[...] * _silu(mg_ref[...].astype(F32))).astype(BF16)

        if n:
            @pl.when(i == nsteps - 1)
            def _():
                _comm_wait(_ag_second(bufs, *sems))

    blk = tm // CHUNK
    res = pl.pallas_call(
        body, name="abm_fwd_gather" if n else "abm_fwd",
        grid=(nsteps,),
        in_specs=[_rows(tm, WIDTH, CB_U), _rows(tm, WIDTH, CB_V), _rows(tm, WIDTH, CB_AGATE),
                  _rows(tm, WIDTH, CB_PIN),
                  pl.BlockSpec((CHUNK, WIDTH), lambda i: (jnp.maximum(i * blk - 1, 0), CB_PIN)),
                  _rows(tm, WIDTH, CB_PGATE), _rows(tm, WIDTH, CB_MQ), _rows(tm, WIDTH, CB_MGATE),
                  _full((1, WIDTH)), _full((1, WIDTH)), _full((N_HEAD, CHUNK, CHUNK)), _full((CHUNK, WIDTH)),
                  _full((4, HEAD, HEAD)), _full((1, WIDTH)), _full((MEM_LEN, 2 * WIDTH))] + [ANY] * n,
        out_specs=[pl.BlockSpec((3, tm, WIDTH), lambda i: (0, i, 0))] + [ANY] * n,
        out_shape=[jax.ShapeDtypeStruct((4, S, WIDTH), BF16)]
                  + [jax.ShapeDtypeStruct(b.shape, b.dtype) for b in gathered],
        input_output_aliases={15 + a: 1 + a for a in range(n)},
        scratch_shapes=[pltpu.VMEM((tm, WIDTH), F32)] + (_dma_sems(3 * n, 3 * n) if n else []),
        compiler_params=_params(("arbitrary",)),
    )(proj, proj, proj, proj, proj, proj, proj, proj, ln_g, ln_b, wsm, bias_full, pool_w, pool_scale, kv, *gathered)
    return res[0], list(res[1:])


ATT_TILE = 512


def _attn_fwd(q, qcb, k, kcb, v, vcb, bps):
    S = q.shape[0]
    tm = ATT_TILE
    nb = tm // CHUNK

    nblocks = nb * N_HEAD

    def body(q_ref, k_ref, v_ref, kh_ref, vh_ref, o_ref, l_ref, sc_s, sp_s, pc_s, pp_s):
        i = pl.program_id(0)

        def prev_kv(n, cs):
            if n == 0:
                return kh_ref[:, cs], vh_ref[:, cs]
            ps = slice((n - 1) * CHUNK, n * CHUNK)
            return k_ref[ps, cs], v_ref[ps, cs]

        pens = []
        for n in range(nb):
            rs = slice(n * CHUNK, (n + 1) * CHUNK)
            pens.append(jnp.full((N_HEAD * CHUNK, 1), jnp.where((i * nb + n) % bps != 0, 0.0, NEG), F32))
            for h in range(N_HEAD):
                cs = slice(h * HEAD, (h + 1) * HEAD)
                bs = slice((n * N_HEAD + h) * CHUNK, (n * N_HEAD + h + 1) * CHUNK)
                qh = q_ref[rs, cs]
                sc_s[bs, :] = _dot_nt(qh, k_ref[rs, cs])
                sp_s[bs, :] = _dot_nt(qh, prev_kv(n, cs)[0])
        row = lax.broadcasted_iota(jnp.int32, (nblocks * CHUNK, CHUNK), 0) & (CHUNK - 1)
        col = lax.broadcasted_iota(jnp.int32, (nblocks * CHUNK, CHUNK), 1)
        sc = jnp.where(col <= row, sc_s[...] * ATT_SCALE, NEG)
        sp = jnp.where(col >= row, sp_s[...] * ATT_SCALE, NEG) + jnp.concatenate(pens, axis=0)
        m = jnp.maximum(jnp.max(sc, axis=-1, keepdims=True), jnp.max(sp, axis=-1, keepdims=True))
        ec = jnp.exp(sc - m)
        ep = jnp.exp(sp - m)
        den = jnp.sum(ec, axis=-1, keepdims=True) + jnp.sum(ep, axis=-1, keepdims=True)
        inv = 1.0 / den
        pc_s[...] = (ec * inv).astype(BF16)
        pp_s[...] = (ep * inv).astype(BF16)
        lse = m + jnp.log(den)
        for n in range(nb):
            rs = slice(n * CHUNK, (n + 1) * CHUNK)
            for h in range(N_HEAD):
                cs = slice(h * HEAD, (h + 1) * HEAD)
                bs = slice((n * N_HEAD + h) * CHUNK, (n * N_HEAD + h + 1) * CHUNK)
                o = _dot(pc_s[bs, :], v_ref[rs, cs]) + _dot(pp_s[bs, :], prev_kv(n, cs)[1])
                o_ref[rs, cs] = o.astype(BF16)
            l_ref[rs, :] = _put_cols([lse[(n * N_HEAD + h) * CHUNK:(n * N_HEAD + h + 1) * CHUNK]
                                      for h in range(N_HEAD)])

    def halo(cb):
        return pl.BlockSpec((CHUNK, WIDTH), lambda i: (jnp.maximum(i * nb - 1, 0), cb))

    return pl.pallas_call(
        body, name=f"attn_fwd_{bps}",
        grid=(S // tm,),
        in_specs=[_rows(tm, WIDTH, qcb), _rows(tm, WIDTH, kcb), _rows(tm, WIDTH, vcb), halo(kcb), halo(vcb)],
        out_specs=[_rows(tm, WIDTH), _rows(tm, 128)],
        out_shape=[jax.ShapeDtypeStruct((S, WIDTH), BF16), jax.ShapeDtypeStruct((S, 128), F32)],
        scratch_shapes=[pltpu.VMEM((nblocks * CHUNK, CHUNK), F32), pltpu.VMEM((nblocks * CHUNK, CHUNK), F32),
                        pltpu.VMEM((nblocks * CHUNK, CHUNK), BF16), pltpu.VMEM((nblocks * CHUNK, CHUNK), BF16)],
        compiler_params=_params(("parallel",)),
    )(q, k, v, k, v)


def _gate_specs(tm):
    return [pl.BlockSpec((tm, D_MODEL), lambda i, b=b: (i, b)) for b in range(4)]


Y_SLOT = (0, 1, 3, 2)


def _merge_fwd(x, y4, o_g, l_g, proj, gates, wb, wout):
    S = x.shape[0]
    tm = 256

    def body(x_ref, y_ref, o0, o1, o2, l0, l1, l2, cg_ref, *rest):
        gm = rest[:4]
        wb_ref, wo_ref, xn_ref, yc_ref, oc_ref, lse_ref, z_ref, ocs = rest[4:]
        lcols = []
        for h in range(N_HEAD):
            cs = slice(h * HEAD, (h + 1) * HEAD)
            ls = [_col(l[...], h) for l in (l0, l1, l2)]
            m = jnp.maximum(jnp.maximum(ls[0], ls[1]), ls[2])
            tot = jnp.exp(ls[0] - m) + jnp.exp(ls[1] - m) + jnp.exp(ls[2] - m)
            lse = m + jnp.log(tot)
            ocs[:, cs] = sum(jnp.exp(lg - lse) * o[:, cs].astype(F32) for lg, o in zip(ls, (o0, o1, o2)))
            lcols.append(lse)
        lse_ref[...] = _put_cols(lcols)
        oc = ocs[...]
        oc_ref[...] = oc.astype(BF16)
        yc = (oc * _silu(cg_ref[...].astype(F32))).astype(BF16)
        yc_ref[...] = yc
        ys = (y_ref[0], y_ref[1], yc, y_ref[2])
        z = jnp.zeros((tm, D_MODEL), F32)
        for b in range(4):
            z = z + _sigmoid(gm[b][...].astype(F32)) * _dot(ys[b], wb_ref[b])
        zb = z.astype(BF16)
        z_ref[...] = zb
        xn_ref[...] = x_ref[...] + _dot(zb, wo_ref[...])

    return pl.pallas_call(
        body, name="merge_fwd",
        grid=(S // tm,),
        in_specs=[_rows(tm, D_MODEL), pl.BlockSpec((3, tm, WIDTH), lambda i: (0, i, 0)),
                  _rows(tm, WIDTH), _rows(tm, WIDTH), _rows(tm, WIDTH),
                  _rows(tm, 128), _rows(tm, 128), _rows(tm, 128),
                  _rows(tm, WIDTH, CB_CGATE)] + _gate_specs(tm)
                 + [_full((4, WIDTH, D_MODEL)), _full((D_MODEL, D_MODEL))],
        out_specs=[_rows(tm, D_MODEL), pl.BlockSpec((None, tm, WIDTH), lambda i: (Y_SLOT[2], i, 0)),
                   _rows(tm, WIDTH), _rows(tm, 128), _rows(tm, D_MODEL)],
        out_shape=[jax.ShapeDtypeStruct((S, D_MODEL), F32), jax.ShapeDtypeStruct(y4.shape, BF16),
                   jax.ShapeDtypeStruct((S, WIDTH), BF16), jax.ShapeDtypeStruct((S, 128), F32),
                   jax.ShapeDtypeStruct((S, D_MODEL), BF16)],
        input_output_aliases={1: 1},
        scratch_shapes=[pltpu.VMEM((tm, WIDTH), F32)],
        compiler_params=_params(("parallel",)),
    )(x, y4, *o_g, *l_g, proj, *([gates] * 4), wb, wout)


def _loss_head(x, g, tgt):
    S = x.shape[0]
    tm = 512

    def body(x_ref, g_ref, t_ref, loss_ref, dx_ref, dg_ref):
        @pl.when(pl.program_id(0) == 0)
        def _():
            loss_ref[...] = jnp.zeros_like(loss_ref)
            dg_ref[...] = jnp.zeros_like(dg_ref)

        xf = x_ref[...]
        r = lax.rsqrt(jnp.mean(xf * xf, axis=-1, keepdims=True) + EPS)
        xhat = xf * r
        gv = g_ref[...]
        err = xhat * gv - t_ref[...]
        e2 = jnp.sum(err * err, axis=-1, keepdims=True)
        loss_ref[...] += (0.5 / D_MODEL) * jnp.sum(e2, axis=0, keepdims=True)
        dy = err * (1.0 / D_MODEL)
        dg_ref[...] += jnp.sum(dy * xhat, axis=0, keepdims=True)
        dxh = dy * gv
        dx_ref[...] = r * (dxh - xhat * jnp.mean(dxh * xhat, axis=-1, keepdims=True))

    return pl.pallas_call(
        body, name="loss_head",
        grid=(S // tm,),
        in_specs=[_rows(tm, D_MODEL), _full((1, D_MODEL)), _rows(tm, D_MODEL)],
        out_specs=[_full((1, 128)), _rows(tm, D_MODEL), _full((1, D_MODEL))],
        out_shape=[jax.ShapeDtypeStruct((1, 128), F32), jax.ShapeDtypeStruct((S, D_MODEL), F32),
                   jax.ShapeDtypeStruct((1, D_MODEL), F32)],
        compiler_params=_params(("arbitrary",)),
    )(x, g, tgt)


def _merge_bwd(dxo, y4, oc, proj, gates, wb, wout, grads=()):
    S = dxo.shape[0]
    tm = 256
    n = len(grads)
    nsteps = S // tm

    def body(dx_ref, y_ref, oc_ref, cg_ref, *rest):
        gm = rest[:4]
        wb_ref, wo_ref = rest[4:6]
        g_in = rest[6:6 + n]
        dy_ref, doc_ref, delta_ref, dcg_ref, dgm_ref, dt_ref = rest[6 + n:12 + n]
        g_out, sems = rest[12 + n:12 + 2 * n], rest[12 + 2 * n:]
        i = pl.program_id(0)

        if n:
            @pl.when(i == 0)
            def _():
                _comm_start(_rs_first(g_in, g_out, *sems))

        dz = _dot_nt(dx_ref[...].astype(BF16), wo_ref[...])
        for b in range(4):
            gate = _sigmoid(gm[b][...].astype(F32))
            t = _dot(y_ref[Y_SLOT[b]], wb_ref[b])
            dgm_ref[:, b * D_MODEL:(b + 1) * D_MODEL] = (dz * t * gate * (1.0 - gate)).astype(BF16)
            dt = (dz * gate).astype(BF16)
            dt_ref[b] = dt
            dyb = _dot_nt(dt, wb_ref[b])
            if b == 2:
                cg = cg_ref[...].astype(F32)
                oc = oc_ref[...].astype(F32)
                doc = dyb * _silu(cg)
                dcg_ref[...] = (dyb * oc * _dsilu(cg)).astype(BF16)
                doc_ref[...] = doc.astype(BF16)
                prod = doc * oc
                delta_ref[...] = _put_cols([jnp.sum(prod[:, h * HEAD:(h + 1) * HEAD], axis=1, keepdims=True)
                                            for h in range(N_HEAD)])
            else:
                dy_ref[b if b < 2 else 2] = dyb.astype(BF16)

        if n:
            @pl.when(i == nsteps - 1)
            def _():
                _comm_wait(_rs_first(g_in, g_out, *sems))

    res = pl.pallas_call(
        body, name="merge_bwd_scatter" if n else "merge_bwd",
        grid=(nsteps,),
        in_specs=[_rows(tm, D_MODEL), pl.BlockSpec((4, tm, WIDTH), lambda i: (0, i, 0)),
                  _rows(tm, WIDTH), _rows(tm, WIDTH, CB_CGATE)] + _gate_specs(tm)
                 + [_full((4, WIDTH, D_MODEL)), _full((D_MODEL, D_MODEL))] + [ANY] * n,
        out_specs=[pl.BlockSpec((3, tm, WIDTH), lambda i: (0, i, 0)), _rows(tm, WIDTH), _rows(tm, 128),
                   _rows(tm, WIDTH), _rows(tm, 4 * D_MODEL), pl.BlockSpec((4, tm, D_MODEL), lambda i: (0, i, 0))]
                  + [ANY] * n,
        out_shape=[jax.ShapeDtypeStruct((3, S, WIDTH), BF16), jax.ShapeDtypeStruct((S, WIDTH), BF16),
                   jax.ShapeDtypeStruct((S, 128), F32), jax.ShapeDtypeStruct((S, WIDTH), BF16),
                   jax.ShapeDtypeStruct((S, 4 * D_MODEL), BF16), jax.ShapeDtypeStruct((4, S, D_MODEL), BF16)]
                  + [jax.ShapeDtypeStruct(g.shape[1:], g.dtype) for g in grads],
        scratch_shapes=_dma_sems(n, n) if n else [],
        compiler_params=_params(("arbitrary",)),
    )(dxo, y4, oc, proj, *([gates] * 4), wb, wout, *grads)
    return res[:6], list(res[6:])


def _dw_branch(y4, dt):
    S = y4.shape[1]
    tk = 1024
    nk = S // tk

    def body(y_ref, dt_ref, o_ref, acc):
        kk = pl.program_id(1)

        @pl.when(kk == 0)
        def _():
            acc[...] = jnp.zeros_like(acc)

        acc[...] += _dot_tn(y_ref[...], dt_ref[...])

        @pl.when(kk == nk - 1)
        def _():
            o_ref[...] = acc[...].astype(BF16)

    def slot(b):
        return jnp.where(b == 2, Y_SLOT[2], jnp.where(b == 3, Y_SLOT[3], b))

    return pl.pallas_call(
        body, name="dw_branch",
        grid=(4, nk),
        in_specs=[pl.BlockSpec((None, tk, WIDTH), lambda b, k: (slot(b), k, 0)),
                  pl.BlockSpec((None, tk, D_MODEL), lambda b, k: (b, k, 0))],
        out_specs=pl.BlockSpec((None, WIDTH, D_MODEL), lambda b, k: (b, 0, 0)),
        out_shape=jax.ShapeDtypeStruct((4, WIDTH, D_MODEL), BF16),
        scratch_shapes=[pltpu.VMEM((WIDTH, D_MODEL), F32)],
        compiler_params=_params(("parallel", "arbitrary")),
    )(y4, dt)


def _attn_bwd(q, qcb, k, kcb, v, vcb, do, lse, delta, bps):
    S = q.shape[0]
    tm = ATT_TILE
    nb = tm // CHUNK
    nblk = S // CHUNK

    ncur = nb * N_HEAD
    nprev = (nb + 1) * N_HEAD

    def body(q_ref, k_ref, v_ref, do_ref, l_ref, d_ref, kh_ref, vh_ref, qn_ref, don_ref, ln_ref, dn_ref,
             dq_ref, dk_ref, dv_ref, sc_s, sp_s, dpc_s, dpp_s, pc_s, pp_s, dsc_s, dsp_s):
        i = pl.program_id(0)

        def rows_of(n):
            if n < nb:
                rs = slice(n * CHUNK, (n + 1) * CHUNK)
                return rs, q_ref, do_ref, l_ref, d_ref
            return slice(0, CHUNK), qn_ref, don_ref, ln_ref, dn_ref

        def prev_kv(n, cs):
            if n == 0:
                return kh_ref[:, cs], vh_ref[:, cs]
            ps = slice((n - 1) * CHUNK, n * CHUNK)
            return k_ref[ps, cs], v_ref[ps, cs]

        def blk(n, h):
            return slice((n * N_HEAD + h) * CHUNK, (n * N_HEAD + h + 1) * CHUNK)

        pens, lses, deltas = [], [], []
        for n in range(nb + 1):
            rs, qr, dor, lr, dr = rows_of(n)
            gb = i * nb + n
            pen = jnp.where(gb % bps != 0, 0.0, NEG)
            if n == nb:
                pen = pen + jnp.where(gb < nblk, 0.0, NEG)
            pens.append(jnp.full((N_HEAD * CHUNK, 1), pen, F32))
            lblk, dblk = lr[rs, :], dr[rs, :]
            for h in range(N_HEAD):
                cs = slice(h * HEAD, (h + 1) * HEAD)
                qh, doh = qr[rs, cs], dor[rs, cs]
                lses.append(_col(lblk, h))
                deltas.append(_col(dblk, h))
                kp, vp = prev_kv(n, cs)
                sp_s[blk(n, h), :] = _dot_nt(qh, kp)
                dpp_s[blk(n, h), :] = _dot_nt(doh, vp)
                if n < nb:
                    sc_s[blk(n, h), :] = _dot_nt(qh, k_ref[rs, cs])
                    dpc_s[blk(n, h), :] = _dot_nt(doh, v_ref[rs, cs])
        lse = jnp.concatenate(lses, axis=0)
        delta = jnp.concatenate(deltas, axis=0)
        row = lax.broadcasted_iota(jnp.int32, (nprev * CHUNK, CHUNK), 0) & (CHUNK - 1)
        col = lax.broadcasted_iota(jnp.int32, (nprev * CHUNK, CHUNK), 1)
        sp = jnp.where(col >= row, sp_s[...] * ATT_SCALE, NEG) + jnp.concatenate(pens, axis=0)
        pp = jnp.exp(sp - lse)
        pp_s[...] = pp.astype(BF16)
        dsp_s[...] = (pp * (dpp_s[...] - delta)).astype(BF16)
        nc = ncur * CHUNK
        sc = jnp.where(col[:nc] <= row[:nc], sc_s[...] * ATT_SCALE, NEG)
        pc = jnp.exp(sc - lse[:nc])
        pc_s[...] = pc.astype(BF16)
        dsc_s[...] = (pc * (dpc_s[...] - delta[:nc])).astype(BF16)
        for n in range(nb):
            rs, qr, dor, _, _ = rows_of(n)
            rn, qnr, donr, _, _ = rows_of(n + 1)
            for h in range(N_HEAD):
                cs = slice(h * HEAD, (h + 1) * HEAD)
                kp, _ = prev_kv(n, cs)
                dq = _dot(dsc_s[blk(n, h), :], k_ref[rs, cs]) + _dot(dsp_s[blk(n, h), :], kp)
                dq_ref[rs, cs] = (dq * ATT_SCALE).astype(BF16)
                dk = _dot_tn(dsc_s[blk(n, h), :], qr[rs, cs]) + _dot_tn(dsp_s[blk(n + 1, h), :], qnr[rn, cs])
                dk_ref[rs, cs] = (dk * ATT_SCALE).astype(BF16)
                dv = _dot_tn(pc_s[blk(n, h), :], dor[rs, cs]) + _dot_tn(pp_s[blk(n + 1, h), :], donr[rn, cs])
                dv_ref[rs, cs] = dv.astype(BF16)

    def prev_halo(cb):
        return pl.BlockSpec((CHUNK, WIDTH), lambda i: (jnp.maximum(i * nb - 1, 0), cb))

    def next_halo(width, cb=0):
        return pl.BlockSpec((CHUNK, width), lambda i: (jnp.minimum(i * nb + nb, nblk - 1), cb))

    return pl.pallas_call(
        body, name=f"attn_bwd_{bps}",
        grid=(S // tm,),
        in_specs=[_rows(tm, WIDTH, qcb), _rows(tm, WIDTH, kcb), _rows(tm, WIDTH, vcb), _rows(tm, WIDTH),
                  _rows(tm, 128), _rows(tm, 128), prev_halo(kcb), prev_halo(vcb),
                  next_halo(WIDTH, qcb), next_halo(WIDTH), next_halo(128), next_halo(128)],
        out_specs=[_rows(tm, WIDTH), _rows(tm, WIDTH), _rows(tm, WIDTH)],
        out_shape=[jax.ShapeDtypeStruct((S, WIDTH), BF16)] * 3,
        scratch_shapes=[pltpu.VMEM((ncur * CHUNK, CHUNK), F32), pltpu.VMEM((nprev * CHUNK, CHUNK), F32),
                        pltpu.VMEM((ncur * CHUNK, CHUNK), F32), pltpu.VMEM((nprev * CHUNK, CHUNK), F32),
                        pltpu.VMEM((ncur * CHUNK, CHUNK), BF16), pltpu.VMEM((nprev * CHUNK, CHUNK), BF16),
                        pltpu.VMEM((ncur * CHUNK, CHUNK), BF16), pltpu.VMEM((nprev * CHUNK, CHUNK), BF16)],
        compiler_params=_params(("parallel",)),
    )(q, k, v, do, lse, delta, k, v, q, do, lse, delta)


def _dilated_split(d):
    hp = min(N_HEAD, 16 // d)
    return hp, N_HEAD // hp, HEAD * hp


def _attn_fwd_dilated(proj, qcb, kcb, vcb, d):
    S = proj.shape[0]
    T = CHUNK * d
    hp, nh, cw = _dilated_split(d)
    nblocks = d * hp

    def body(q_ref, k_ref, v_ref, o_ref, l_ref, qf, kf, vf, kpf, vpf, kst, vst, of, lf, sc_s, sp_s, pc_s, pp_s):
        i, hh = pl.program_id(0), pl.program_id(1)
        for j in range(hp):
            cs = slice(j * HEAD, (j + 1) * HEAD)
            qf[j] = q_ref[:, cs].astype(F32)
            kf[j] = k_ref[:, cs].astype(F32)
            vf[j] = v_ref[:, cs].astype(F32)

        @pl.when(i == 0)
        def _():
            kpf[...] = jnp.zeros_like(kpf)
            vpf[...] = jnp.zeros_like(vpf)

        @pl.when(i > 0)
        def _():
            kpf[...] = kst[hh]
            vpf[...] = vst[hh]

        def blk(ref, r, j):
            return ref.at[j][pl.ds(r, CHUNK, stride=d), :].astype(BF16)

        def bs(r, j):
            return slice((r * hp + j) * CHUNK, (r * hp + j + 1) * CHUNK)

        for r in range(d):
            for j in range(hp):
                qb = blk(qf, r, j)
                sc_s[bs(r, j), :] = _dot_nt(qb, blk(kf, r, j))
                sp_s[bs(r, j), :] = _dot_nt(qb, blk(kpf, r, j))
        row = lax.broadcasted_iota(jnp.int32, (nblocks * CHUNK, CHUNK), 0) & (CHUNK - 1)
        col = lax.broadcasted_iota(jnp.int32, (nblocks * CHUNK, CHUNK), 1)
        sc = jnp.where(col <= row, sc_s[...] * ATT_SCALE, NEG)
        sp = jnp.where(col >= row, sp_s[...] * ATT_SCALE, NEG) + jnp.where(i > 0, 0.0, NEG)
        m = jnp.maximum(jnp.max(sc, axis=-1, keepdims=True), jnp.max(sp, axis=-1, keepdims=True))
        ec = jnp.exp(sc - m)
        ep = jnp.exp(sp - m)
        den = jnp.sum(ec, axis=-1, keepdims=True) + jnp.sum(ep, axis=-1, keepdims=True)
        inv = 1.0 / den
        pc_s[...] = (ec * inv).astype(BF16)
        pp_s[...] = (ep * inv).astype(BF16)
        lse = m + jnp.log(den)
        lane = lax.broadcasted_iota(jnp.int32, (CHUNK, 128), 1)
        for r in range(d):
            lblk = jnp.zeros((CHUNK, 128), F32)
            for j in range(hp):
                o = _dot(pc_s[bs(r, j), :], blk(vf, r, j)) + _dot(pp_s[bs(r, j), :], blk(vpf, r, j))
                of.at[j][pl.ds(r, CHUNK, stride=d), :] = o
                lblk = jnp.where(lane == hh * hp + j, lse[bs(r, j)], lblk)
            lf[pl.ds(r, CHUNK, stride=d), :] = lblk
        for j in range(hp):
            o_ref[:, j * HEAD:(j + 1) * HEAD] = of[j].astype(BF16)

        @pl.when(hh == 0)
        def _():
            l_ref[...] = lf[...]

        @pl.when(hh > 0)
        def _():
            l_ref[...] += lf[...]

        kst[hh] = kf[...]
        vst[hh] = vf[...]

    def cols(cb):
        return pl.BlockSpec((T, cw), lambda i, hh: (i, cb * nh + hh))

    tile = pltpu.VMEM((hp, T, HEAD), F32)
    return pl.pallas_call(
        body, name=f"attn_fwd_dilated_{d}",
        grid=(S // T, nh),
        in_specs=[cols(qcb), cols(kcb), cols(vcb)],
        out_specs=[cols(0), pl.BlockSpec((T, 128), lambda i, hh: (i, 0))],
        out_shape=[jax.ShapeDtypeStruct((S, WIDTH), BF16), jax.ShapeDtypeStruct((S, 128), F32)],
        scratch_shapes=[tile, tile, tile, tile, tile,
                        pltpu.VMEM((nh, hp, T, HEAD), F32), pltpu.VMEM((nh, hp, T, HEAD), F32),
                        tile, pltpu.VMEM((T, 128), F32),
                        pltpu.VMEM((nblocks * CHUNK, CHUNK), F32), pltpu.VMEM((nblocks * CHUNK, CHUNK), F32),
                        pltpu.VMEM((nblocks * CHUNK, CHUNK), BF16), pltpu.VMEM((nblocks * CHUNK, CHUNK), BF16)],
        compiler_params=_params(("arbitrary", "arbitrary")),
    )(proj, proj, proj)


def _attn_bwd_dilated(proj, qcb, kcb, vcb, do, lse, delta, d):
    S = proj.shape[0]
    T = CHUNK * d
    nt = S // T
    hp, nh, cw = _dilated_split(d)
    nblocks = d * hp

    def body(q_ref, k_ref, v_ref, do_ref, l_ref, d_ref, dq_ref, dk_ref, dv_ref,
             qf, dof, kf, vf, kpf, vpf, dqf, acck, accv, newk, newv,
             sc_s, sp_s, dpc_s, dpp_s, pc_s, pp_s, dsc_s, dsp_s):
        hh, i = pl.program_id(0), pl.program_id(1)

        @pl.when(i == 0)
        def _():
            for ref in (kpf, vpf, acck, accv):
                ref[...] = jnp.zeros_like(ref)
            dk_ref[...] = jnp.zeros_like(dk_ref)
            dv_ref[...] = jnp.zeros_like(dv_ref)

        def blk(ref, r, j):
            return ref.at[j][pl.ds(r, CHUNK, stride=d), :].astype(BF16)

        def bs(r, j):
            return slice((r * hp + j) * CHUNK, (r * hp + j + 1) * CHUNK)

        @pl.when(i < nt)
        def _():
            for j in range(hp):
                cs = slice(j * HEAD, (j + 1) * HEAD)
                qf[j] = q_ref[:, cs].astype(F32)
                dof[j] = do_ref[:, cs].astype(F32)
                kf[j] = k_ref[:, cs].astype(F32)
                vf[j] = v_ref[:, cs].astype(F32)
            lses, deltas = [], []
            for r in range(d):
                lblk = l_ref[pl.ds(r, CHUNK, stride=d), :]
                dblk = d_ref[pl.ds(r, CHUNK, stride=d), :]
                for j in range(hp):
                    lses.append(_col(lblk, hh * hp + j))
                    deltas.append(_col(dblk, hh * hp + j))
                    qb, dob = blk(qf, r, j), blk(dof, r, j)
                    sc_s[bs(r, j), :] = _dot_nt(qb, blk(kf, r, j))
                    dpc_s[bs(r, j), :] = _dot_nt(dob, blk(vf, r, j))
                    sp_s[bs(r, j), :] = _dot_nt(qb, blk(kpf, r, j))
                    dpp_s[bs(r, j), :] = _dot_nt(dob, blk(vpf, r, j))
            lse = jnp.concatenate(lses, axis=0)
            delta = jnp.concatenate(deltas, axis=0)
            row = lax.broadcasted_iota(jnp.int32, (nblocks * CHUNK, CHUNK), 0) & (CHUNK - 1)
            col = lax.broadcasted_iota(jnp.int32, (nblocks * CHUNK, CHUNK), 1)
            sp = jnp.where(col >= row, sp_s[...] * ATT_SCALE, NEG) + jnp.where(i > 0, 0.0, NEG)
            pp = jnp.exp(sp - lse)
            pp_s[...] = pp.astype(BF16)
            dsp_s[...] = (pp * (dpp_s[...] - delta)).astype(BF16)
            sc = jnp.where(col <= row, sc_s[...] * ATT_SCALE, NEG)
            pc = jnp.exp(sc - lse)
            pc_s[...] = pc.astype(BF16)
            dsc_s[...] = (pc * (dpc_s[...] - delta)).astype(BF16)
            for r in range(d):
                rows = pl.ds(r, CHUNK, stride=d)
                for j in range(hp):
                    qb, dob = blk(qf, r, j), blk(dof, r, j)
                    dsc, dsp = dsc_s[bs(r, j), :], dsp_s[bs(r, j), :]
                    dqf.at[j][rows, :] = (_dot(dsc, blk(kf, r, j)) + _dot(dsp, blk(kpf, r, j))) * ATT_SCALE
                    newk.at[j][rows, :] = _dot_tn(dsc, qb) * ATT_SCALE
                    newv.at[j][rows, :] = _dot_tn(pc_s[bs(r, j), :], dob)
                    acck.at[j][rows, :] += _dot_tn(dsp, qb) * ATT_SCALE
                    accv.at[j][rows, :] += _dot_tn(pp_s[bs(r, j), :], dob)
            for j in range(hp):
                dq_ref[:, j * HEAD:(j + 1) * HEAD] = dqf[j].astype(BF16)

        @pl.when(i > 0)
        def _():
            for j in range(hp):
                dk_ref[:, j * HEAD:(j + 1) * HEAD] = acck[j].astype(BF16)
                dv_ref[:, j * HEAD:(j + 1) * HEAD] = accv[j].astype(BF16)

        @pl.when(i < nt)
        def _():
            acck[...] = newk[...]
            accv[...] = newv[...]
            kpf[...] = kf[...]
            vpf[...] = vf[...]

    def cur(width, cb, nsplit):
        return pl.BlockSpec((T, width), lambda hh, i: (jnp.minimum(i, nt - 1), cb * nsplit + hh * (nsplit > 1)))

    def lag():
        return pl.BlockSpec((T, cw), lambda hh, i: (jnp.maximum(i - 1, 0), hh))

    tile = pltpu.VMEM((hp, T, HEAD), F32)
    f32s = pltpu.VMEM((nblocks * CHUNK, CHUNK), F32)
    b16s = pltpu.VMEM((nblocks * CHUNK, CHUNK), BF16)
    return pl.pallas_call(
        body, name=f"attn_bwd_dilated_{d}",
        grid=(nh, nt + 1),
        in_specs=[cur(cw, qcb, nh), cur(cw, kcb, nh), cur(cw, vcb, nh), cur(cw, 0, nh), cur(128, 0, 1), cur(128, 0, 1)],
        out_specs=[cur(cw, 0, nh), lag(), lag()],
        out_shape=[jax.ShapeDtypeStruct((S, WIDTH), BF16)] * 3,
        scratch_shapes=[tile] * 11 + [f32s] * 4 + [b16s] * 4,
        compiler_params=_params(("arbitrary", "arbitrary")),
    )(proj, proj, proj, do, lse, delta)


def _abm_bwd(proj, dy3, ln_g, ln_b, wsm, wsm_t, bias_full, pool_w, pool_wt, pool_scale, kv):
    S = proj.shape[0]
    tm = 512
    nchunk = tm // CHUNK
    nblk = S // CHUNK

    def body(u_ref, v_ref, ag_ref, p_ref, ph_ref, pg_ref, pgn_ref, mq_ref, mg_ref, dy_ref, dypn_ref,
             lng_ref, lnb_ref, wsm_ref, wsmt_ref, bias_ref, pw_ref, pwt_ref, ps_ref, kv_ref,
             dab_ref, dm_ref, dlng_ref, dlnb_ref, dws_ref, dbias_ref, dpw_ref, dps_ref, dkv_ref,
             mix, dvl, ddn):
        i = pl.program_id(0)

        @pl.when(i == 0)
        def _():
            for r in (dlng_ref, dlnb_ref, dws_ref, dbias_ref, dpw_ref, dps_ref, dkv_ref):
                r[...] = jnp.zeros_like(r)

        au = u_ref[...].astype(F32)
        av = v_ref[...].astype(F32)
        ag = ag_ref[...].astype(F32)
        u = _gelu(au)
        v = _gelu(av)
        vhat, rstd = _layer_norm_fwd(v)
        vln = (vhat * lng_ref[...] + lnb_ref[...]).astype(BF16)
        for c in range(nchunk):
            for h in range(N_HEAD):
                rs, cs = slice(c * CHUNK, (c + 1) * CHUNK), slice(h * HEAD, (h + 1) * HEAD)
                mix[rs, cs] = _dot(wsm_ref[h], vln[rs, cs]) + bias_ref[:, cs]
        dya = dy_ref[0].astype(F32)
        sg = _silu(ag)
        mixed = mix[...]
        dab_ref[:, 2 * WIDTH:3 * WIDTH] = (dya * u * mixed * _dsilu(ag)).astype(BF16)
        dab_ref[:, 0:WIDTH] = (dya * mixed * sg * _dgelu(au)).astype(BF16)
        dmixed = dya * u * sg
        dmb = dmixed.astype(BF16)
        tril = (lax.broadcasted_iota(jnp.int32, (CHUNK, CHUNK), 1)
                <= lax.broadcasted_iota(jnp.int32, (CHUNK, CHUNK), 0))
        for c in range(nchunk):
            rs = slice(c * CHUNK, (c + 1) * CHUNK)
            dbias_ref[...] += dmixed[rs, :]
            for h in range(N_HEAD):
                cs = slice(h * HEAD, (h + 1) * HEAD)
                dvl[rs, cs] = _dot(wsmt_ref[h], dmb[rs, cs])
                dws_ref[h] += jnp.where(tril, _dot_nt(dmb[rs, cs], vln[rs, cs]), 0.0)
        dvln = dvl[...]
        dlng_ref[...] += jnp.sum(dvln * vhat, axis=0, keepdims=True)
        dlnb_ref[...] += jnp.sum(dvln, axis=0, keepdims=True)
        dvh = dvln * lng_ref[...]
        dv = rstd * (dvh - jnp.mean(dvh, axis=-1, keepdims=True)
                     - vhat * jnp.mean(dvh * vhat, axis=-1, keepdims=True))
        dab_ref[:, WIDTH:2 * WIDTH] = (dv * _dgelu(av)).astype(BF16)

        halo_ok = (i > 0).astype(F32)
        for c in range(nchunk):
            rs = slice(c * CHUNK, (c + 1) * CHUNK)
            for g, win in enumerate(POOL_WINDOWS):
                cs = slice(g * HEAD, (g + 1) * HEAD)
                bcur, bprev = _band_masks(win)
                cur = p_ref[rs, cs]
                if c == 0:
                    prev = (ph_ref[:, cs].astype(F32) * halo_ok).astype(BF16)
                else:
                    prev = p_ref[(c - 1) * CHUNK:c * CHUNK, cs]
                sums = _dot(bcur, cur) + _dot(bprev, prev)
                dvl[rs, cs] = sums * _inv_count(i * tm + c * CHUNK, win) - cur.astype(F32)
        dmat = dvl[...].astype(BF16)
        for g in range(4):
            cs = slice(g * HEAD, (g + 1) * HEAD)
            mix[:, cs] = _dot(dmat[:, cs], pw_ref[g])
        yg = mix[...]
        pg = pg_ref[...].astype(F32)
        dyp = dy_ref[1].astype(F32)
        dyy = dyp * _silu(pg)
        scale = ps_ref[...]
        dab_ref[:, 4 * WIDTH:5 * WIDTH] = (dyp * yg * scale * _dsilu(pg)).astype(BF16)
        dps_ref[...] += jnp.sum(dyy * yg, axis=0, keepdims=True)
        dyg = (dyy * scale).astype(BF16)
        for g in range(4):
            cs = slice(g * HEAD, (g + 1) * HEAD)
            dpw_ref[g] += _dot_tn(dmat[:, cs], dyg[:, cs])
            mix[:, cs] = _dot(dyg[:, cs], pwt_ref[g])
        next_ok = (i + 1 < S // tm).astype(F32)
        dygn = (dypn_ref[...].astype(F32) * _silu(pgn_ref[...].astype(F32)) * scale * next_ok).astype(BF16)
        for c in range(nchunk + 1):
            for g, win in enumerate(POOL_WINDOWS):
                cs = slice(g * HEAD, (g + 1) * HEAD)
                if c < nchunk:
                    dd = mix[c * CHUNK:(c + 1) * CHUNK, cs]
                else:
                    dd = _dot(dygn[:, cs], pwt_ref[g])
                ddn[c * CHUNK:(c + 1) * CHUNK, cs] = dd * _inv_count(i * tm + c * CHUNK, win)
        ddnb = ddn[...].astype(BF16)
        for c in range(nchunk):
            rs = slice(c * CHUNK, (c + 1) * CHUNK)
            ns = slice((c + 1) * CHUNK, (c + 2) * CHUNK)
            for g, win in enumerate(POOL_WINDOWS):
                cs = slice(g * HEAD, (g + 1) * HEAD)
                bcur, bprev = _band_masks(win)
                dp = _dot_tn(bcur, ddnb[rs, cs]) + _dot_tn(bprev, ddnb[ns, cs]) - mix[rs, cs]
                dab_ref[rs, 3 * WIDTH + g * HEAD:3 * WIDTH + (g + 1) * HEAD] = dp.astype(BF16)

        mg = mg_ref[...].astype(F32)
        dym = dy_ref[2].astype(F32)
        dob = (dym * _silu(mg)).astype(BF16)
        for h in range(N_HEAD):
            cs = slice(h * HEAD, (h + 1) * HEAD)
            vs = slice(WIDTH + h * HEAD, WIDTH + (h + 1) * HEAD)
            qh = mq_ref[:, cs]
            p = _mem_softmax(qh, kv_ref[:, cs])
            pb = p.astype(BF16)
            mix[:, cs] = _dot(pb, kv_ref[:, vs])
            dp = _dot_nt(dob[:, cs], kv_ref[:, vs])
            ds = (p * (dp - jnp.sum(p * dp, axis=-1, keepdims=True))).astype(BF16)
            dm_ref[:, cs] = (_dot(ds, kv_ref[:, cs]) * ATT_SCALE).astype(BF16)
            dkv_ref[:, cs] += _dot_tn(ds, qh) * ATT_SCALE
            dkv_ref[:, vs] += _dot_tn(pb, dob[:, cs])
        dm_ref[:, WIDTH:2 * WIDTH] = (dym * mix[...] * _dsilu(mg)).astype(BF16)

    blk = tm // CHUNK
    small = [_full((1, WIDTH)), _full((1, WIDTH)), _full((N_HEAD, CHUNK, CHUNK)), _full((CHUNK, WIDTH)),
             _full((4, HEAD, HEAD)), _full((1, WIDTH)), _full((MEM_LEN, 2 * WIDTH))]
    return pl.pallas_call(
        body, name="abm_bwd",
        grid=(S // tm,),
        in_specs=[_rows(tm, WIDTH, CB_U), _rows(tm, WIDTH, CB_V), _rows(tm, WIDTH, CB_AGATE),
                  _rows(tm, WIDTH, CB_PIN),
                  pl.BlockSpec((CHUNK, WIDTH), lambda i: (jnp.maximum(i * blk - 1, 0), CB_PIN)),
                  _rows(tm, WIDTH, CB_PGATE),
                  pl.BlockSpec((CHUNK, WIDTH), lambda i: (jnp.minimum(i * blk + blk, nblk - 1), CB_PGATE)),
                  _rows(tm, WIDTH, CB_MQ), _rows(tm, WIDTH, CB_MGATE),
                  pl.BlockSpec((3, tm, WIDTH), lambda i: (0, i, 0)),
                  pl.BlockSpec((None, CHUNK, WIDTH), lambda i: (1, jnp.minimum(i * blk + blk, nblk - 1), 0)),
                  _full((1, WIDTH)), _full((1, WIDTH)), _full((N_HEAD, CHUNK, CHUNK)), _full((N_HEAD, CHUNK, CHUNK)),
                  _full((CHUNK, WIDTH)), _full((4, HEAD, HEAD)), _full((4, HEAD, HEAD)), _full((1, WIDTH)),
                  _full((MEM_LEN, 2 * WIDTH))],
        out_specs=[_rows(tm, 5 * WIDTH), _rows(tm, 2 * WIDTH)] + small,
        out_shape=[jax.ShapeDtypeStruct((S, D_BRANCHES), BF16), jax.ShapeDtypeStruct((S, 2 * WIDTH), BF16),
                   jax.ShapeDtypeStruct((1, WIDTH), F32), jax.ShapeDtypeStruct((1, WIDTH), F32),
                   jax.ShapeDtypeStruct((N_HEAD, CHUNK, CHUNK), F32), jax.ShapeDtypeStruct((CHUNK, WIDTH), F32),
                   jax.ShapeDtypeStruct((4, HEAD, HEAD), F32), jax.ShapeDtypeStruct((1, WIDTH), F32),
                   jax.ShapeDtypeStruct((MEM_LEN, 2 * WIDTH), F32)],
        scratch_shapes=[pltpu.VMEM((tm, WIDTH), F32), pltpu.VMEM((tm, WIDTH), F32),
                        pltpu.VMEM((tm + CHUNK, WIDTH), F32)],
        compiler_params=_params(("arbitrary",)),
    )(proj, proj, proj, proj, proj, proj, proj, proj, proj, dy3, dy3,
      ln_g, ln_b, wsm, wsm_t, bias_full, pool_w, pool_wt, pool_scale, kv)


def _bias_reduce(dbias_full):
    def body(d_ref, o_ref):
        d = d_ref[...]
        o_ref[...] = _put_cols([jnp.sum(d[:, h * HEAD:(h + 1) * HEAD], axis=1, keepdims=True) for h in range(N_HEAD)])

    return pl.pallas_call(body, name="bias_reduce", out_shape=jax.ShapeDtypeStruct((CHUNK, 128), F32))(dbias_full)


def _mem_bwd(mem, g, mem_n, w, dkv):
    def body(m_ref, g_ref, mn_ref, w_ref, dkv_ref, dw_ref, dg_ref):
        dkvb = dkv_ref[...].astype(BF16)
        dw_ref[...] = _dot_tn(mn_ref[...], dkvb).astype(BF16)
        dmn = _dot_nt(dkvb, w_ref[...])
        xf = m_ref[...]
        r = lax.rsqrt(jnp.mean(xf * xf, axis=-1, keepdims=True) + EPS)
        dg_ref[...] = jnp.sum(dmn * xf * r, axis=0, keepdims=True)

    return pl.pallas_call(
        body, name="mem_bwd",
        out_shape=[jax.ShapeDtypeStruct((D_MODEL, 2 * WIDTH), BF16), jax.ShapeDtypeStruct((1, D_MODEL), F32)],
        compiler_params=pltpu.CompilerParams(vmem_limit_bytes=VMEM_LIMIT),
    )(mem, g, mem_n, w, dkv)


def _dh_bwd(dpb, wbr, dpg, wg, x, g, dxo, parts=()):
    S = x.shape[0]
    tm = 1024
    tkb, tkg = D_BRANCHES // 4, D_GATES // 4
    nkb = 4
    nk = 8
    ni = S // tm
    n = len(parts)

    def body(dpb_ref, wbr_ref, dpg_ref, wg_ref, x_ref, g_ref, dxo_ref, *rest):
        p_in = rest[:n]
        dx_ref, dg_ref = rest[n:n + 2]
        p_out, acc, sems = rest[n + 2:2 * n + 2], rest[2 * n + 2], rest[2 * n + 3:]
        i, kk = pl.program_id(0), pl.program_id(1)

        @pl.when(jnp.logical_and(i == 0, kk == 0))
        def _():
            dg_ref[...] = jnp.zeros_like(dg_ref)
            if n:
                _comm_start(_rs_second(p_in, p_out, *sems))

        @pl.when(kk == 0)
        def _():
            acc[...] = jnp.zeros_like(acc)

        @pl.when(kk < nkb)
        def _():
            acc[...] += _dot_nt(dpb_ref[...], wbr_ref[...])

        @pl.when(kk >= nkb)
        def _():
            acc[...] += _dot_nt(dpg_ref[...], wg_ref[...])

        @pl.when(kk == nk - 1)
        def _():
            xf = x_ref[...]
            r = lax.rsqrt(jnp.mean(xf * xf, axis=-1, keepdims=True) + EPS)
            xhat = xf * r
            dh = acc[...]
            dg_ref[...] += jnp.sum(dh * xhat, axis=0, keepdims=True)
            dxh = dh * g_ref[...]
            dx_ref[...] = dxo_ref[...] + r * (dxh - xhat * jnp.mean(dxh * xhat, axis=-1, keepdims=True))

        if n:
            @pl.when(jnp.logical_and(i == ni - 1, kk == nk - 1))
            def _():
                _comm_wait(_rs_second(p_in, p_out, *sems))

    res = pl.pallas_call(
        body, name="dh_bwd_scatter" if n else "dh_bwd",
        grid=(ni, nk),
        in_specs=[pl.BlockSpec((tm, tkb), lambda i, k: (i, jnp.minimum(k, nkb - 1))),
                  pl.BlockSpec((D_MODEL, tkb), lambda i, k: (0, jnp.minimum(k, nkb - 1))),
                  pl.BlockSpec((tm, tkg), lambda i, k: (i, jnp.maximum(k - nkb, 0))),
                  pl.BlockSpec((D_MODEL, tkg), lambda i, k: (0, jnp.maximum(k - nkb, 0))),
                  pl.BlockSpec((tm, D_MODEL), lambda i, k: (i, 0)), pl.BlockSpec((1, D_MODEL), lambda i, k: (0, 0)),
                  pl.BlockSpec((tm, D_MODEL), lambda i, k: (i, 0))] + [ANY] * n,
        out_specs=[pl.BlockSpec((tm, D_MODEL), lambda i, k: (i, 0)), pl.BlockSpec((1, D_MODEL), lambda i, k: (0, 0))]
                  + [ANY] * n,
        out_shape=[jax.ShapeDtypeStruct((S, D_MODEL), F32), jax.ShapeDtypeStruct((1, D_MODEL), F32)]
                  + [jax.ShapeDtypeStruct(p.shape, p.dtype) for p in parts],
        scratch_shapes=[pltpu.VMEM((tm, D_MODEL), F32)] + (_dma_sems(3 * n, 3 * n, n) if n else []),
        compiler_params=_params(("arbitrary", "arbitrary")),
    )(dpb, wbr, dpg, wg, x, g, dxo, *parts)
    return res[0], res[1], list(res[2:])


def _matmul_tn(a, b, tn, name):
    K, M = a.shape
    N = b.shape[1]
    tk = 1024
    nk = K // tk

    def body(a_ref, b_ref, o_ref, acc):
        kk = pl.program_id(1)

        @pl.when(kk == 0)
        def _():
            acc[...] = jnp.zeros_like(acc)

        acc[...] += _dot_tn(a_ref[...].astype(BF16), b_ref[...].astype(BF16))

        @pl.when(kk == nk - 1)
        def _():
            o_ref[...] = acc[...].astype(BF16)

    return pl.pallas_call(
        body, name=name,
        grid=(N // tn, nk),
        in_specs=[pl.BlockSpec((tk, M), lambda j, k: (k, 0)), pl.BlockSpec((tk, tn), lambda j, k: (k, j))],
        out_specs=pl.BlockSpec((M, tn), lambda j, k: (0, j)),
        out_shape=jax.ShapeDtypeStruct((M, N), BF16),
        scratch_shapes=[pltpu.VMEM((M, tn), F32)],
        compiler_params=_params(("parallel", "arbitrary")),
    )(a, b)


def _row_tile(R, C):
    for cand in (512, 256, 128, 64, 32, 16, 8):
        if R % cand == 0 and cand * C * 4 <= (2 << 20):
            return cand
    return R


def _adamw_update(p_ref, w_ref, m_ref, v_ref, g_ref, d_ref, nm_ref, nv_ref):
    c1 = 1.0 / (1.0 - ADAM_B1 ** ADAM_STEP)
    c2 = 1.0 / (1.0 - ADAM_B2 ** ADAM_STEP)
    g = p_ref[0].astype(F32)
    for k in range(1, p_ref.shape[0]):
        g = g + p_ref[k].astype(F32)
    nm = ADAM_B1 * m_ref[...] + (1.0 - ADAM_B1) * g
    nv = ADAM_B2 * v_ref[...] + (1.0 - ADAM_B2) * (g * g)
    g_ref[...] = g
    nm_ref[...] = nm
    nv_ref[...] = nv
    d_ref[...] = -ADAM_LR * ((nm * c1) / (jnp.sqrt(nv * c2) + ADAM_EPS) + ADAM_WD * w_ref[...])


def _adamw(parts, w, m, v, name):
    P, R, C = parts.shape
    tr = _row_tile(R, C)

    def body(*refs):
        _adamw_update(*refs)

    spec = pl.BlockSpec((tr, C), lambda i: (i, 0))
    return pl.pallas_call(
        body, name=name,
        grid=(R // tr,),
        in_specs=[pl.BlockSpec((P, tr, C), lambda i: (0, i, 0)), spec, spec, spec],
        out_specs=[spec] * 4,
        out_shape=[jax.ShapeDtypeStruct((R, C), F32)] * 4,
        compiler_params=_params(("parallel",)),
    )(parts, w, m, v)


def _adamw_layers(parts, w, m, v, name):
    depth = len(parts)
    P, R, C = parts[0].shape
    tr = _row_tile(R, C)

    def body(*refs):
        layer = pl.program_id(0)
        for k in range(depth):
            @pl.when(layer == k)
            def _(k=k):
                _adamw_update(refs[k], *refs[depth:])

    def part_spec(k):
        return pl.BlockSpec((P, tr, C), lambda l, i: (0, jnp.where(l == k, i, 0), 0))

    spec = pl.BlockSpec((None, tr, C), lambda l, i: (l, i, 0))
    return pl.pallas_call(
        body, name=name,
        grid=(depth, R // tr),
        in_specs=[part_spec(k) for k in range(depth)] + [spec] * 3,
        out_specs=[spec] * 4,
        out_shape=[jax.ShapeDtypeStruct((depth, R, C), F32)] * 4,
        compiler_params=_params(("arbitrary", "arbitrary")),
    )(*parts, w, m, v)


def _place():
    return lax.axis_index("x"), lax.axis_index("y"), lax.axis_index("c")


def _all_gather(shards):
    n = len(shards)

    def body(*refs):
        ins, outs = refs[:n], refs[n:2 * n]
        send1, recv1, local_sems, send2, recv2 = refs[2 * n:]
        first = _ag_first(ins, outs, send1, recv1, local_sems)
        second = _ag_second(outs, send2, recv2)
        _comm_start(first)
        for j in range(3):
            for a in range(n):
                first[2][4 * a + 1 + j].wait_recv()
            for a in range(n):
                second[1][3 * a + j].start()
        for a in range(n):
            first[2][4 * a].wait_recv()
        for cp in second[2]:
            cp.wait_recv()
        for cp in first[1] + second[1]:
            cp.wait_send()
        for cp in first[0]:
            cp.wait()

    return pl.pallas_call(
        body, name="weights_all_gather",
        in_specs=[ANY] * n, out_specs=[ANY] * n,
        out_shape=[jax.ShapeDtypeStruct((N_DEV,) + s.shape, s.dtype) for s in shards],
        scratch_shapes=_dma_sems(4 * n, 4 * n, n, 3 * n, 3 * n),
        compiler_params=pltpu.CompilerParams(has_side_effects=True),
    )(*shards)


N_BIG = 4


def _dev(p):
    return 4 * p[0] + 2 * p[1] + p[2]


def _other_chips(x, y):
    return [(1 - x, y), (x, 1 - y), (1 - x, 1 - y)]


def _remote(src, dst, send_sems, recv_sems, k, to):
    return pltpu.make_async_remote_copy(src_ref=src, dst_ref=dst, send_sem=send_sems.at[k], recv_sem=recv_sems.at[k],
                                        device_id=to, device_id_type=MESH)


def _ag_first(ins, outs, send_sems, recv_sems, local_sems):
    x, y, c = _place()
    me = (x, y, c)
    targets = [(x, y, 1 - c)] + [(*chip, c) for chip in _other_chips(x, y)]
    local, out, inc = [], [], []
    for a in range(len(ins)):
        local.append(pltpu.make_async_copy(ins[a], outs[a].at[_dev(me)], local_sems.at[a]))
        for k, to in enumerate(targets):
            out.append(_remote(ins[a], outs[a].at[_dev(me)], send_sems, recv_sems, 4 * a + k, to))
            inc.append(_remote(ins[a], outs[a].at[_dev(to)], send_sems, recv_sems, 4 * a + k, to))
    return local, out, inc


def _ag_second(bufs, send_sems, recv_sems):
    x, y, c = _place()
    out, inc = [], []
    for a in range(len(bufs)):
        for j, chip in enumerate(_other_chips(x, y)):
            mine, theirs = bufs[a].at[_dev((*chip, c))], bufs[a].at[_dev((*chip, 1 - c))]
            out.append(_remote(mine, mine, send_sems, recv_sems, 3 * a + j, (x, y, 1 - c)))
            inc.append(_remote(theirs, theirs, send_sems, recv_sems, 3 * a + j, (x, y, 1 - c)))
    return [], out, inc


def _rs_first(ins, outs, send_sems, recv_sems):
    x, y, c = _place()
    out = [_remote(ins[a].at[1 - c], outs[a], send_sems, recv_sems, a, (x, y, 1 - c)) for a in range(len(ins))]
    return [], out, out


def _rs_second(ins, outs, send_sems, recv_sems, local_sems):
    x, y, c = _place()
    my_chip = 2 * x + y
    local, out, inc = [], [], []
    for a in range(len(ins)):
        local.append(pltpu.make_async_copy(ins[a].at[my_chip], outs[a].at[my_chip], local_sems.at[a]))
        for k, (ox, oy) in enumerate(_other_chips(x, y)):
            out.append(_remote(ins[a].at[2 * ox + oy], outs[a].at[my_chip], send_sems, recv_sems, 3 * a + k, (ox, oy, c)))
            inc.append(_remote(ins[a].at[2 * ox + oy], outs[a].at[2 * ox + oy], send_sems, recv_sems, 3 * a + k,
                               (ox, oy, c)))
    return local, out, inc


def _comm_start(exchange):
    local, out, _ = exchange
    for cp in local + out:
        cp.start()


def _comm_wait(exchange):
    local, out, inc = exchange
    for cp in inc:
        cp.wait_recv()
    for cp in out:
        cp.wait_send()
    for cp in local:
        cp.wait()


def _dma_sems(*counts):
    return [pltpu.SemaphoreType.DMA((n,)) for n in counts]


def _rs_sibling(grads):
    n = len(grads)

    def body(*refs):
        ex = _rs_first(refs[:n], refs[n:2 * n], *refs[2 * n:])
        _comm_start(ex)
        _comm_wait(ex)

    return pl.pallas_call(
        body, name="grads_to_sibling",
        in_specs=[ANY] * n, out_specs=[ANY] * n,
        out_shape=[jax.ShapeDtypeStruct(g.shape[1:], g.dtype) for g in grads],
        scratch_shapes=_dma_sems(n, n),
        compiler_params=pltpu.CompilerParams(has_side_effects=True),
    )(*grads)


def _pair_sum(grads, recvs):
    n = len(grads)

    def body(c_ref, *refs):
        for a in range(n):
            refs[2 * n + a][...] = (refs[a][...].astype(F32) + refs[n + a][...].astype(F32)).astype(BF16)

    def g_spec(g):
        return pl.BlockSpec((None, None) + g.shape[2:], lambda j, c_ref: (c_ref[0], j, 0, 0))

    def r_spec(r):
        return pl.BlockSpec((None,) + r.shape[1:], lambda j, c_ref: (j, 0, 0))

    return pl.pallas_call(
        body, name="pair_sum",
        grid_spec=pltpu.PrefetchScalarGridSpec(
            num_scalar_prefetch=1, grid=(N_CHIP,),
            in_specs=[g_spec(g) for g in grads] + [r_spec(r) for r in recvs],
            out_specs=[r_spec(r) for r in recvs]),
        out_shape=[jax.ShapeDtypeStruct(r.shape, BF16) for r in recvs],
        compiler_params=_params(("parallel",)),
    )(lax.axis_index("c").reshape(1).astype(jnp.int32), *grads, *recvs)


def _rs_chips(parts):
    n = len(parts)

    def body(*refs):
        ex = _rs_second(refs[:n], refs[n:2 * n], *refs[2 * n:])
        _comm_start(ex)
        _comm_wait(ex)

    return pl.pallas_call(
        body, name="grads_to_chips",
        in_specs=[ANY] * n, out_specs=[ANY] * n,
        out_shape=[jax.ShapeDtypeStruct(p.shape, p.dtype) for p in parts],
        scratch_shapes=_dma_sems(3 * n, 3 * n, n),
        compiler_params=pltpu.CompilerParams(has_side_effects=True),
    )(*parts)


SMALL_ROWS = 544


def _all_reduce_small(buf):
    def body(in_ref, out_ref, recv, acc, send1, recv1, send2, recv2):
        x, y, c = _place()
        me = 4 * x + 2 * y + c
        peers = [(x ^ (r >> 2), y ^ ((r >> 1) & 1), c ^ (r & 1)) for r in range(1, N_DEV)]

        def idx(p):
            return 4 * p[0] + 2 * p[1] + p[2]

        first = [pltpu.make_async_remote_copy(
            src_ref=in_ref.at[idx(p)], dst_ref=recv.at[me], send_sem=send1.at[r], recv_sem=recv1.at[r],
            device_id=p, device_id_type=MESH) for r, p in enumerate(peers)]
        for cp in first:
            cp.start()
        recv[me] = in_ref[me]
        for r, p in enumerate(peers):
            pltpu.make_async_remote_copy(
                src_ref=in_ref.at[idx(p)], dst_ref=recv.at[idx(p)], send_sem=send1.at[r], recv_sem=recv1.at[r],
                device_id=p, device_id_type=MESH).wait_recv()
        total = recv[0]
        for k in range(1, N_DEV):
            total = total + recv[k]
        acc[...] = total
        out_ref[me] = total
        second = [pltpu.make_async_remote_copy(
            src_ref=acc, dst_ref=out_ref.at[me], send_sem=send2.at[r], recv_sem=recv2.at[r],
            device_id=p, device_id_type=MESH) for r, p in enumerate(peers)]
        for cp in second:
            cp.start()
        for r, p in enumerate(peers):
            pltpu.make_async_remote_copy(
                src_ref=acc, dst_ref=out_ref.at[idx(p)], send_sem=send2.at[r], recv_sem=recv2.at[r],
                device_id=p, device_id_type=MESH).wait_recv()
        for cp in first + second:
            cp.wait_send()

    vm = pl.BlockSpec(memory_space=pltpu.VMEM)
    return pl.pallas_call(
        body, name="small_grads_all_reduce",
        in_specs=[vm], out_specs=vm,
        out_shape=jax.ShapeDtypeStruct(buf.shape, F32),
        scratch_shapes=[pltpu.VMEM(buf.shape, F32), pltpu.VMEM(buf.shape[1:], F32),
                        pltpu.SemaphoreType.DMA((7,)), pltpu.SemaphoreType.DMA((7,)),
                        pltpu.SemaphoreType.DMA((7,)), pltpu.SemaphoreType.DMA((7,))],
        compiler_params=pltpu.CompilerParams(has_side_effects=True, vmem_limit_bytes=VMEM_LIMIT),
    )(buf)


def _dilate(a, d):
    if d == 1:
        return a
    S, C = a.shape
    return a.reshape(S // d, d, C).transpose(1, 0, 2).reshape(S, C)


def _undilate(a, d):
    if d == 1:
        return a
    S, C = a.shape
    return a.reshape(d, S // d, C).transpose(1, 0, 2).reshape(S, C)


def _cols(a, cb, n=1):
    return a[:, cb * WIDTH:(cb + n) * WIDTH]


def _to_blocks(g, kind):
    if kind == "cols":
        R = g.shape[0]
        return g.reshape(R, N_CHIP, 2, -1).transpose(2, 1, 0, 3)
    if kind == "rows":
        C = g.shape[1]
        return g.reshape(N_CHIP, 2, -1, C).transpose(1, 0, 2, 3)
    return g.reshape(4 * WIDTH, N_CHIP, 2, -1).transpose(2, 1, 0, 3)


SMALL = ("norm_g", "gm_ln_g", "gm_ln_b", "gm_ws", "gm_bs", "pool_w", "pool_scale", "mem_norm_g", "final_norm_g")


def _pack_small(tree):
    flat = jnp.concatenate([tree[k].reshape(-1, 128) for k in SMALL], axis=0)
    return jnp.pad(flat, ((0, N_DEV * SMALL_ROWS - flat.shape[0]), (0, 0)))


def _unpack_small(flat, like):
    out, at = {}, 0
    for k in SMALL:
        rows = like[k].size // 128
        out[k] = flat[at:at + rows].reshape(like[k].shape)
        at += rows
    return out


def _make_layer(wbr, wg, wkv, wb, wout, norm_g, mem_norm_g, ln_g, ln_b, gm_ws, gm_bs, pool_w, pool_scale):
    tril = jnp.tril(jnp.ones((CHUNK, CHUNK), bool))
    wsm = jnp.where(tril, gm_ws, 0.0).astype(BF16)
    pw = pool_w.astype(BF16)
    return dict(wbr=wbr, wg=wg, wkv=wkv, wb=wb, wout=wout, g=norm_g[None], mg=mem_norm_g[None], ln_g=ln_g[None],
                ln_b=ln_b[None], wsm=wsm, wsm_t=wsm.transpose(0, 2, 1), pw=pw, pw_t=pw.transpose(0, 2, 1),
                ps=pool_scale[None], bias=jnp.repeat(gm_bs.T, HEAD, axis=1))


def _layer_fwd(xl, mem0, L, next_shards=()):
    S = xl.shape[0]
    proj, h, half_gathered = _in_proj(xl, L["g"], L["wbr"], next_shards)
    gates = _matmul(h, L["wg"], "gate_proj")
    kv, mem_n = _mem_kv(mem0, L["mg"], L["wkv"])
    y4, gathered = _abm_fwd(proj, L["ln_g"], L["ln_b"], L["wsm"], L["bias"], L["pw"], L["ps"], kv, half_gathered)
    o_g, l_g = [], []
    for gi, d in enumerate(DILATIONS):
        if d == 1:
            o, lse = _attn_fwd(proj, CB_Q0, proj, CB_K, proj, CB_CV, S // CHUNK)
        else:
            o, lse = _attn_fwd_dilated(proj, CB_Q0 + gi, CB_K, CB_CV, d)
        o_g.append(o)
        l_g.append(lse)
    xn, y4, oc, lse, z = _merge_fwd(xl, y4, o_g, l_g, proj, gates, L["wb"], L["wout"])
    return xn, dict(x=xl, proj=proj, gates=gates, h=h, kv=kv, mem_n=mem_n, y4=y4, oc=oc, lse=lse, z=z), gathered


def _place_cols(dst, piece, cb):
    return lax.dynamic_update_slice(dst, piece, (0, cb * WIDTH))


def _layer_bwd(dx, mem0, L, sv, later=()):
    S = dx.shape[0]
    proj = sv["proj"]
    (dy3, doc, delta, dcg, dgm, dt), from_sibling = _merge_bwd(dx, sv["y4"], sv["oc"], proj, sv["gates"], L["wb"],
                                                              L["wout"], later)
    pair = _pair_sum(later, from_sibling) if later else ()
    dwout = _matmul_tn(sv["z"], dx, D_MODEL, "dw_out")
    dwb = _dw_branch(sv["y4"], dt)
    dpb, dm, dlng, dlnb, dws, dbias, dpw, dps, dkv = _abm_bwd(
        proj, dy3, L["ln_g"], L["ln_b"], L["wsm"], L["wsm_t"], L["bias"], L["pw"], L["pw_t"], L["ps"], sv["kv"])
    dk, dv = None, None
    for gi, d in enumerate(DILATIONS):
        if d == 1:
            r = _attn_bwd(proj, CB_Q0, proj, CB_K, proj, CB_CV, doc, sv["lse"], delta, S // CHUNK)
        else:
            r = _attn_bwd_dilated(proj, CB_Q0 + gi, CB_K, CB_CV, doc, sv["lse"], delta, d)
        dpb = _place_cols(dpb, r[0], CB_Q0 + gi)
        dkg, dvg = r[1].astype(F32), r[2].astype(F32)
        dk = dkg if dk is None else dk + dkg
        dv = dvg if dv is None else dv + dvg
    dpb = _place_cols(dpb, dk.astype(BF16), CB_K)
    dpb = _place_cols(dpb, dv.astype(BF16), CB_CV)
    dpb = _place_cols(dpb, dcg, CB_CGATE)
    dpb = _place_cols(dpb, dm, CB_MQ)
    dwkv, dmg = _mem_bwd(mem0, L["mg"], sv["mem_n"], L["wkv"], dkv)
    dwin = jnp.concatenate([_matmul_tn(sv["h"], dpb, D_BRANCHES // 4, "dw_in_branches"),
                            _matmul_tn(sv["h"], dgm, D_GATES // 4, "dw_in_gates")], axis=1)
    dxi, dng, parts = _dh_bwd(dpb, L["wbr"], dgm, L["wg"], sv["x"], L["g"], dx, pair)
    big = dict(w_in=dwin, w_mem_kv=dwkv, w_branch=dwb, w_out=dwout)
    small = dict(norm_g=dng[0], gm_ln_g=dlng[0], gm_ln_b=dlnb[0], gm_ws=dws,
                 gm_bs=_bias_reduce(dbias)[:, :N_HEAD].T, pool_w=dpw, pool_scale=dps[0], mem_norm_g=dmg[0])
    return dxi, big, small, parts


BIG = ("w_in", "w_mem_kv", "w_branch", "w_out")


def _blocked(big):
    return [_to_blocks(big["w_in"], "cols"), _to_blocks(big["w_mem_kv"], "rows"),
            _to_blocks(big["w_branch"], "branch"), _to_blocks(big["w_out"], "rows")]


def _full_weights(gathered):
    win, wkv, wb, wout = gathered
    cut = D_BRANCHES - 4 * win.shape[2]
    wbr = jnp.concatenate([win[0], win[1], win[2], win[3], win[4][:, :cut]], axis=1)
    wg = jnp.concatenate([win[4][:, cut:], win[5], win[6], win[7]], axis=1)
    return (wbr, wg, wkv.reshape(D_MODEL, 2 * WIDTH),
            wb.reshape(N_DEV, 4, WIDTH, -1).transpose(1, 2, 0, 3).reshape(4, WIDTH, D_MODEL),
            wout.reshape(D_MODEL, D_MODEL))


def kernel(x, mem, norm_g, w_in, gm_ln_g, gm_ln_b, gm_ws, gm_bs, pool_w, pool_scale, mem_norm_g, w_mem_kv, w_branch, w_out, final_norm_g, loss_target, m_norm_g, m_w_in, m_gm_ln_g, m_gm_ln_b, m_gm_ws, m_gm_bs, m_pool_w, m_pool_scale, m_mem_norm_g, m_w_mem_kv, m_w_branch, m_w_out, m_final_norm_g, v_norm_g, v_w_in, v_gm_ln_g, v_gm_ln_b, v_gm_ws, v_gm_bs, v_pool_w, v_pool_scale, v_mem_norm_g, v_w_mem_kv, v_w_branch, v_w_out, v_final_norm_g):
    x0 = x[0]
    mem0 = mem[0]
    tgt = loss_target[0]
    S = x0.shape[0]

    shards = [[w_in[l].astype(BF16), w_mem_kv[l].astype(BF16), w_branch[l].astype(BF16).reshape(4 * WIDTH, -1),
               w_out[l].astype(BF16)] for l in range(DEPTH)]
    gathered = _all_gather(shards[0])
    layers, saved = [], []
    xl = x0
    for l in range(DEPTH):
        layers.append(_make_layer(*_full_weights(gathered), norm_g[l], mem_norm_g[l], gm_ln_g[l], gm_ln_b[l],
                                  gm_ws[l], gm_bs[l], pool_w[l], pool_scale[l]))
        xl, sv, gathered = _layer_fwd(xl, mem0, layers[l], shards[l + 1] if l + 1 < DEPTH else ())
        saved.append(sv)

    loss_part, dx, d_final = _loss_head(xl, final_norm_g[None], tgt)
    loss = lax.psum(loss_part[0, 0], ("x", "y", "c"))

    small = {k: [None] * DEPTH for k in SMALL if k != "final_norm_g"}
    parts = [None] * DEPTH
    later = ()
    for l in reversed(range(DEPTH)):
        dx, gb, gs, done = _layer_bwd(dx, mem0, layers[l], saved[l], later)
        if later:
            parts[l + 1] = done
        later = _blocked(gb)
        for k in gs:
            small[k][l] = gs[k]
    grad_x = dx[None]
    parts[0] = _rs_chips(_pair_sum(later, _rs_sibling(later)))

    small_tree = {k: jnp.stack(small[k]) for k in small}
    small_tree["final_norm_g"] = d_final[0]
    reduced = _all_reduce_small(_pack_small(small_tree).reshape(N_DEV, SMALL_ROWS, 128))

    weights = dict(norm_g=norm_g, w_in=w_in, gm_ln_g=gm_ln_g, gm_ln_b=gm_ln_b, gm_ws=gm_ws, gm_bs=gm_bs,
                   pool_w=pool_w, pool_scale=pool_scale, mem_norm_g=mem_norm_g, w_mem_kv=w_mem_kv,
                   w_branch=w_branch, w_out=w_out, final_norm_g=final_norm_g)
    m_in = dict(norm_g=m_norm_g, w_in=m_w_in, gm_ln_g=m_gm_ln_g, gm_ln_b=m_gm_ln_b, gm_ws=m_gm_ws, gm_bs=m_gm_bs,
                pool_w=m_pool_w, pool_scale=m_pool_scale, mem_norm_g=m_mem_norm_g, w_mem_kv=m_w_mem_kv,
                w_branch=m_w_branch, w_out=m_w_out, final_norm_g=m_final_norm_g)
    v_in = dict(norm_g=v_norm_g, w_in=v_w_in, gm_ln_g=v_gm_ln_g, gm_ln_b=v_gm_ln_b, gm_ws=v_gm_ws, gm_bs=v_gm_bs,
                pool_w=v_pool_w, pool_scale=v_pool_scale, mem_norm_g=v_mem_norm_g, w_mem_kv=v_w_mem_kv,
                w_branch=v_w_branch, w_out=v_w_out, final_norm_g=v_final_norm_g)
    res = {}
    for a, k in enumerate(BIG):
        shape = weights[k].shape
        by_layer = [parts[l][a] for l in range(DEPTH)]
        lrc = (DEPTH,) + by_layer[0].shape[1:]
        outs = _adamw_layers(by_layer, weights[k].reshape(lrc), m_in[k].reshape(lrc), v_in[k].reshape(lrc),
                             "adamw_" + k)
        res[k] = [o.reshape(shape) for o in outs]
    outs = _adamw(reduced.reshape(1, N_DEV * SMALL_ROWS, 128), _pack_small(weights), _pack_small(m_in),
                  _pack_small(v_in), "adamw_small")
    unpacked = [_unpack_small(o, weights) for o in outs]
    for k in SMALL:
        res[k] = [u[k] for u in unpacked]

    order = ("norm_g", "w_in", "gm_ln_g", "gm_ln_b", "gm_ws", "gm_bs", "pool_w", "pool_scale", "mem_norm_g",
             "w_mem_kv", "w_branch", "w_out", "final_norm_g")
    return (loss, grad_x, *[res[k][0] for k in order], *[res[k][1] for k in order],
            *[res[k][2] for k in order], *[res[k][3] for k in order])
```

```python
import functools
import math

import jax
import jax.numpy as jnp
from jax import lax
from jax.experimental import pallas as pl
from jax.experimental.pallas import tpu as pltpu

F32 = jnp.float32
BF16 = jnp.bfloat16

D_MODEL = 1024
DEPTH = 4
WIDTH = 512
D_IN = 10752
HEAD = 128
N_HEAD = 4
CHUNK = 128
MEM_LEN = 256
POOL_WINDOWS = (2, 4, 8, 16)
DILATIONS = (1, 4, 16)
EPS = 1e-6
NEG = -1e30
ATT_SCALE = HEAD ** -0.5
N_DEV = 8
N_CHIP = 4

D_BRANCHES = 6656
D_GATES = D_IN - D_BRANCHES
CB_U, CB_V, CB_AGATE, CB_PIN, CB_PGATE = 0, 1, 2, 3, 4
CB_Q0, CB_K, CB_CV, CB_CGATE, CB_MQ, CB_MGATE = 5, 8, 9, 10, 11, 12

ADAM_LR = 0.001
ADAM_B1 = 0.9
ADAM_B2 = 0.999
ADAM_EPS = 1e-08
ADAM_WD = 0.01
ADAM_STEP = 10

VMEM_LIMIT = 56 * 1024 * 1024
MESH = pl.DeviceIdType.MESH
ANY = pl.BlockSpec(memory_space=pl.ANY)

NT = (((1,), (1,)), ((), ()))
TN = (((0,), (0,)), ((), ()))


def _dot(a, b):
    return jnp.dot(a, b, preferred_element_type=F32)


def _dot_nt(a, b):
    return lax.dot_general(a, b, NT, preferred_element_type=F32)


def _dot_tn(a, b):
    return lax.dot_general(a, b, TN, preferred_element_type=F32)


def _sigmoid(x):
    return 1.0 / (1.0 + jnp.exp(-x))


def _silu(x):
    return x * _sigmoid(x)


def _dsilu(x):
    s = _sigmoid(x)
    return s * (1.0 + x * (1.0 - s))


def _gelu(x):
    return 0.5 * x * (1.0 + lax.erf(x * (2.0 ** -0.5)))


def _dgelu(x):
    return 0.5 * (1.0 + lax.erf(x * (2.0 ** -0.5))) + x * jnp.exp(-0.5 * x * x) * (1.0 / math.sqrt(2.0 * math.pi))


def _col(blk, h):
    lane = lax.broadcasted_iota(jnp.int32, blk.shape, 1)
    return jnp.sum(jnp.where(lane == h, blk, 0.0), axis=1, keepdims=True)


def _put_cols(cols):
    rows = cols[0].shape[0]
    lane = lax.broadcasted_iota(jnp.int32, (rows, 128), 1)
    out = jnp.zeros((rows, 128), F32)
    for h, cv in enumerate(cols):
        out = jnp.where(lane == h, cv, out)
    return out


def _params(sem, vmem=VMEM_LIMIT):
    return pltpu.CompilerParams(dimension_semantics=sem, vmem_limit_bytes=vmem)


def _full(shape):
    nd = len(shape)
    return pl.BlockSpec(shape, lambda *_: (0,) * nd)


def _rows(tm, width, cb=0):
    return pl.BlockSpec((tm, width), lambda i: (i, cb))


def _in_proj(x, g, wbr, wg, shards=()):
    S = x.shape[0]
    tm, tnb, tng = 1024, D_BRANCHES // 4, D_GATES // 4
    njb = 4
    n = len(shards)
    ni, nj = S // tm, 2 * njb

    def body(x_ref, g_ref, wbr_ref, wg_ref, *rest):
        ins, (proj_ref, gates_ref, h_ref), outs = rest[:n], rest[n:n + 3], rest[n + 3:2 * n + 3]
        hs, sems = rest[2 * n + 3], rest[2 * n + 4:]
        i, j = pl.program_id(0), pl.program_id(1)

        if n:
            @pl.when(jnp.logical_and(i == 0, j == 0))
            def _():
                _comm_start(_ag_first(ins, outs, *sems))

        @pl.when(j == 0)
        def _():
            xf = x_ref[...]
            r = lax.rsqrt(jnp.mean(xf * xf, axis=-1, keepdims=True) + EPS)
            h = (xf * r * g_ref[...]).astype(BF16)
            hs[...] = h
            h_ref[...] = h

        @pl.when(j < njb)
        def _():
            proj_ref[...] = _dot(hs[...], wbr_ref[...]).astype(BF16)

        @pl.when(j >= njb)
        def _():
            gates_ref[...] = _dot(hs[...], wg_ref[...]).astype(BF16)

        if n:
            @pl.when(jnp.logical_and(i == ni - 1, j == nj - 1))
            def _():
                _comm_wait(_ag_first(ins, outs, *sems))

    def first(j):
        return jnp.minimum(j, njb - 1)

    def second(j):
        return jnp.maximum(j - njb, 0)

    res = pl.pallas_call(
        body, name="in_proj_gather" if n else "in_proj",
        grid=(ni, nj),
        in_specs=[pl.BlockSpec((tm, D_MODEL), lambda i, j: (i, 0)),
                  pl.BlockSpec((1, D_MODEL), lambda i, j: (0, 0)),
                  pl.BlockSpec((D_MODEL, tnb), lambda i, j: (0, first(j))),
                  pl.BlockSpec((D_MODEL, tng), lambda i, j: (0, second(j)))] + [ANY] * n,
        out_specs=[pl.BlockSpec((tm, tnb), lambda i, j: (i, first(j))),
                   pl.BlockSpec((tm, tng), lambda i, j: (i, second(j))),
                   pl.BlockSpec((tm, D_MODEL), lambda i, j: (i, 0))] + [ANY] * n,
        out_shape=[jax.ShapeDtypeStruct((S, D_BRANCHES), BF16), jax.ShapeDtypeStruct((S, D_GATES), BF16),
                   jax.ShapeDtypeStruct((S, D_MODEL), BF16)]
                  + [jax.ShapeDtypeStruct((N_DEV,) + s.shape, s.dtype) for s in shards],
        scratch_shapes=[pltpu.VMEM((tm, D_MODEL), BF16)] + (_dma_sems(4 * n, 4 * n, n) if n else []),
        compiler_params=_params(("arbitrary", "arbitrary")),
    )(x, g, wbr, wg, *shards)
    return res[0], res[1], res[2], list(res[3:])


def _mem_kv(mem, g, w):
    M = mem.shape[0]

    def body(m_ref, g_ref, w_ref, kv_ref, mn_ref):
        xf = m_ref[...]
        r = lax.rsqrt(jnp.mean(xf * xf, axis=-1, keepdims=True) + EPS)
        mn = (xf * r * g_ref[...]).astype(BF16)
        mn_ref[...] = mn
        kv_ref[...] = _dot(mn, w_ref[...]).astype(BF16)

    return pl.pallas_call(
        body, name="mem_kv",
        out_shape=[jax.ShapeDtypeStruct((M, 2 * WIDTH), BF16), jax.ShapeDtypeStruct((M, D_MODEL), BF16)],
        compiler_params=pltpu.CompilerParams(vmem_limit_bytes=VMEM_LIMIT),
    )(mem, g, w)


def _band_masks(win):
    t = lax.broadcasted_iota(jnp.int32, (CHUNK, CHUNK), 0)
    s = lax.broadcasted_iota(jnp.int32, (CHUNK, CHUNK), 1)
    cur = jnp.logical_and(t - s >= 0, t - s < win)
    prev = s > t + CHUNK - win
    return cur.astype(BF16), prev.astype(BF16)


def _inv_count(first_row, win):
    t = first_row + lax.broadcasted_iota(jnp.int32, (CHUNK, 1), 0)
    return 1.0 / jnp.minimum(t + 1, win).astype(F32)


def _layer_norm_fwd(v):
    mu = jnp.mean(v, axis=-1, keepdims=True)
    vc = v - mu
    var = jnp.mean(vc * vc, axis=-1, keepdims=True)
    rstd = lax.rsqrt(var + EPS)
    return vc * rstd, rstd


def _mem_softmax(q, kmem):
    s = _dot_nt(q, kmem) * ATT_SCALE
    m = jnp.max(s, axis=-1, keepdims=True)
    e = jnp.exp(s - m)
    return e * (1.0 / jnp.sum(e, axis=-1, keepdims=True))


def _abm_fwd(proj, ln_g, ln_b, wsm, bias_full, pool_w, pool_scale, kv, gathered=()):
    S = proj.shape[0]
    tm = 512
    nchunk = tm // CHUNK
    n = len(gathered)
    nsteps = S // tm

    def body(u_ref, v_ref, ag_ref, p_ref, ph_ref, pg_ref, mq_ref, mg_ref, lng_ref, lnb_ref, wsm_ref, bias_ref,
             pw_ref, ps_ref, kv_ref, *rest):
        y_ref, bufs, mix, sems = rest[n], rest[n + 1:2 * n + 1], rest[2 * n + 1], rest[2 * n + 2:]
        i = pl.program_id(0)

        if n:
            @pl.when(i == 0)
            def _():
                _comm_start(_ag_second(bufs, *sems))

        u = _gelu(u_ref[...].astype(F32))
        v = _gelu(v_ref[...].astype(F32))
        vhat, _ = _layer_norm_fwd(v)
        vln = (vhat * lng_ref[...] + lnb_ref[...]).astype(BF16)
        for c in range(nchunk):
            for h in range(N_HEAD):
                rs, cs = slice(c * CHUNK, (c + 1) * CHUNK), slice(h * HEAD, (h + 1) * HEAD)
                mix[rs, cs] = _dot(wsm_ref[h], vln[rs, cs]) + bias_ref[:, cs]
        y_ref[0] = (u * mix[...] * _silu(ag_ref[...].astype(F32))).astype(BF16)
        halo_ok = (i > 0).astype(F32)
        for c in range(nchunk):
            rs = slice(c * CHUNK, (c + 1) * CHUNK)
            for g, win in enumerate(POOL_WINDOWS):
                cs = slice(g * HEAD, (g + 1) * HEAD)
                bcur, bprev = _band_masks(win)
                cur = p_ref[rs, cs]
                if c == 0:
                    prev = (ph_ref[:, cs].astype(F32) * halo_ok).astype(BF16)
                else:
                    prev = p_ref[(c - 1) * CHUNK:c * CHUNK, cs]
                sums = _dot(bcur, cur) + _dot(bprev, prev)
                dm = sums * _inv_count(i * tm + c * CHUNK, win) - cur.astype(F32)
                mix[rs, cs] = _dot(dm.astype(BF16), pw_ref[g])
        y_ref[1] = (mix[...] * ps_ref[...] * _silu(pg_ref[...].astype(F32))).astype(BF16)
        for h in range(N_HEAD):
            cs = slice(h * HEAD, (h + 1) * HEAD)
            p = _mem_softmax(mq_ref[:, cs], kv_ref[:, cs])
            mix[:, cs] = _dot(p.astype(BF16), kv_ref[:, WIDTH + h * HEAD:WIDTH + (h + 1) * HEAD])
        y_ref[2] = (mix[...] * _silu(mg_ref[...].astype(F32))).astype(BF16)

        if n:
            @pl.when(i == nsteps - 1)
            def _():
                _comm_wait(_ag_second(bufs, *sems))

    blk = tm // CHUNK
    res = pl.pallas_call(
        body, name="abm_fwd_gather" if n else "abm_fwd",
        grid=(nsteps,),
        in_specs=[_rows(tm, WIDTH, CB_U), _rows(tm, WIDTH, CB_V), _rows(tm, WIDTH, CB_AGATE),
                  _rows(tm, WIDTH, CB_PIN),
                  pl.BlockSpec((CHUNK, WIDTH), lambda i: (jnp.maximum(i * blk - 1, 0), CB_PIN)),
                  _rows(tm, WIDTH, CB_PGATE), _rows(tm, WIDTH, CB_MQ), _rows(tm, WIDTH, CB_MGATE),
                  _full((1, WIDTH)), _full((1, WIDTH)), _full((N_HEAD, CHUNK, CHUNK)), _full((CHUNK, WIDTH)),
                  _full((4, HEAD, HEAD)), _full((1, WIDTH)), _full((MEM_LEN, 2 * WIDTH))] + [ANY] * n,
        out_specs=[pl.BlockSpec((3, tm, WIDTH), lambda i: (0, i, 0))] + [ANY] * n,
        out_shape=[jax.ShapeDtypeStruct((4, S, WIDTH), BF16)]
                  + [jax.ShapeDtypeStruct(b.shape, b.dtype) for b in gathered],
        input_output_aliases={15 + a: 1 + a for a in range(n)},
        scratch_shapes=[pltpu.VMEM((tm, WIDTH), F32)] + (_dma_sems(3 * n, 3 * n) if n else []),
        compiler_params=_params(("arbitrary",)),
    )(proj, proj, proj, proj, proj, proj, proj, proj, ln_g, ln_b, wsm, bias_full, pool_w, pool_scale, kv, *gathered)
    return res[0], list(res[1:])


ATT_TILE = 512


def _attn_fwd(q, qcb, k, kcb, v, vcb, bps):
    S = q.shape[0]
    tm = ATT_TILE
    nb = tm // CHUNK

    nblocks = nb * N_HEAD

    def body(q_ref, k_ref, v_ref, kh_ref, vh_ref, o_ref, l_ref, sc_s, sp_s, pc_s, pp_s):
        i = pl.program_id(0)

        def prev_kv(n, cs):
            if n == 0:
                return kh_ref[:, cs], vh_ref[:, cs]
            ps = slice((n - 1) * CHUNK, n * CHUNK)
            return k_ref[ps, cs], v_ref[ps, cs]

        pens = []
        for n in range(nb):
            rs = slice(n * CHUNK, (n + 1) * CHUNK)
            pens.append(jnp.full((N_HEAD * CHUNK, 1), jnp.where((i * nb + n) % bps != 0, 0.0, NEG), F32))
            for h in range(N_HEAD):
                cs = slice(h * HEAD, (h + 1) * HEAD)
                bs = slice((n * N_HEAD + h) * CHUNK, (n * N_HEAD + h + 1) * CHUNK)
                qh = q_ref[rs, cs]
                sc_s[bs, :] = _dot_nt(qh, k_ref[rs, cs])
                sp_s[bs, :] = _dot_nt(qh, prev_kv(n, cs)[0])
        row = lax.broadcasted_iota(jnp.int32, (nblocks * CHUNK, CHUNK), 0) & (CHUNK - 1)
        col = lax.broadcasted_iota(jnp.int32, (nblocks * CHUNK, CHUNK), 1)
        sc = jnp.where(col <= row, sc_s[...] * ATT_SCALE, NEG)
        sp = jnp.where(col >= row, sp_s[...] * ATT_SCALE, NEG) + jnp.concatenate(pens, axis=0)
        m = jnp.maximum(jnp.max(sc, axis=-1, keepdims=True), jnp.max(sp, axis=-1, keepdims=True))
        ec = jnp.exp(sc - m)
        ep = jnp.exp(sp - m)
        den = jnp.sum(ec, axis=-1, keepdims=True) + jnp.sum(ep, axis=-1, keepdims=True)
        inv = 1.0 / den
        pc_s[...] = (ec * inv).astype(BF16)
        pp_s[...] = (ep * inv).astype(BF16)
        lse = m + jnp.log(den)
        for n in range(nb):
            rs = slice(n * CHUNK, (n + 1) * CHUNK)
            for h in range(N_HEAD):
                cs = slice(h * HEAD, (h + 1) * HEAD)
                bs = slice((n * N_HEAD + h) * CHUNK, (n * N_HEAD + h + 1) * CHUNK)
                o = _dot(pc_s[bs, :], v_ref[rs, cs]) + _dot(pp_s[bs, :], prev_kv(n, cs)[1])
                o_ref[rs, cs] = o.astype(BF16)
            l_ref[rs, :] = _put_cols([lse[(n * N_HEAD + h) * CHUNK:(n * N_HEAD + h + 1) * CHUNK]
                                      for h in range(N_HEAD)])

    def halo(cb):
        return pl.BlockSpec((CHUNK, WIDTH), lambda i: (jnp.maximum(i * nb - 1, 0), cb))

    return pl.pallas_call(
        body, name=f"attn_fwd_{bps}",
        grid=(S // tm,),
        in_specs=[_rows(tm, WIDTH, qcb), _rows(tm, WIDTH, kcb), _rows(tm, WIDTH, vcb), halo(kcb), halo(vcb)],
        out_specs=[_rows(tm, WIDTH), _rows(tm, 128)],
        out_shape=[jax.ShapeDtypeStruct((S, WIDTH), BF16), jax.ShapeDtypeStruct((S, 128), F32)],
        scratch_shapes=[pltpu.VMEM((nblocks * CHUNK, CHUNK), F32), pltpu.VMEM((nblocks * CHUNK, CHUNK), F32),
                        pltpu.VMEM((nblocks * CHUNK, CHUNK), BF16), pltpu.VMEM((nblocks * CHUNK, CHUNK), BF16)],
        compiler_params=_params(("parallel",)),
    )(q, k, v, k, v)


def _gate_specs(tm):
    return [pl.BlockSpec((tm, D_MODEL), lambda i, b=b: (i, b)) for b in range(4)]


Y_SLOT = (0, 1, 3, 2)


def _merge_fwd(x, y4, o_g, l_g, proj, gates, wb, wout, shards=()):
    S = x.shape[0]
    tm = 256
    n = len(shards)
    nsteps = S // tm
    forward_at = nsteps - 4

    def body(x_ref, y_ref, o0, o1, o2, l0, l1, l2, cg_ref, *rest):
        gm = rest[:4]
        wb_ref, wo_ref = rest[4:6]
        s_in = rest[6:6 + n]
        xn_ref, yc_ref, oc_ref, lse_ref, z_ref = rest[6 + n:11 + n]
        s_out, ocs, sems = rest[11 + n:11 + 2 * n], rest[11 + 2 * n], rest[12 + 2 * n:]
        i = pl.program_id(0)

        if n:
            @pl.when(i == 0)
            def _():
                _comm_start(_ag_first(s_in, s_out, *sems[:3]))

            @pl.when(i == forward_at)
            def _():
                incoming = _ag_first(s_in, s_out, *sems[:3])[2]
                for a in range(n):
                    for k in range(1, 4):
                        incoming[4 * a + k].wait_recv()
                _comm_start(_ag_second(s_out, *sems[3:]))

        lcols = []
        for h in range(N_HEAD):
            cs = slice(h * HEAD, (h + 1) * HEAD)
            ls = [_col(l[...], h) for l in (l0, l1, l2)]
            m = jnp.maximum(jnp.maximum(ls[0], ls[1]), ls[2])
            tot = jnp.exp(ls[0] - m) + jnp.exp(ls[1] - m) + jnp.exp(ls[2] - m)
            lse = m + jnp.log(tot)
            ocs[:, cs] = sum(jnp.exp(lg - lse) * o[:, cs].astype(F32) for lg, o in zip(ls, (o0, o1, o2)))
            lcols.append(lse)
        lse_ref[...] = _put_cols(lcols)
        oc = ocs[...]
        oc_ref[...] = oc.astype(BF16)
        yc = (oc * _silu(cg_ref[...].astype(F32))).astype(BF16)
        yc_ref[...] = yc
        ys = (y_ref[0], y_ref[1], yc, y_ref[2])
        z = jnp.zeros((tm, D_MODEL), F32)
        for b in range(4):
            z = z + _sigmoid(gm[b][...].astype(F32)) * _dot(ys[b], wb_ref[b])
        zb = z.astype(BF16)
        z_ref[...] = zb
        xn_ref[...] = x_ref[...] + _dot(zb, wo_ref[...])

        if n:
            @pl.when(i == nsteps - 1)
            def _():
                local, out, incoming = _ag_first(s_in, s_out, *sems[:3])
                for a in range(n):
                    incoming[4 * a].wait_recv()
                _comm_wait(_ag_second(s_out, *sems[3:]))
                for cp in out:
                    cp.wait_send()
                for cp in local:
                    cp.wait()

    res = pl.pallas_call(
        body, name="merge_fwd_gather" if n else "merge_fwd",
        grid=(nsteps,),
        in_specs=[_rows(tm, D_MODEL), pl.BlockSpec((3, tm, WIDTH), lambda i: (0, i, 0)),
                  _rows(tm, WIDTH), _rows(tm, WIDTH), _rows(tm, WIDTH),
                  _rows(tm, 128), _rows(tm, 128), _rows(tm, 128),
                  _rows(tm, WIDTH, CB_CGATE)] + _gate_specs(tm)
                 + [_full((4, WIDTH, D_MODEL)), _full((D_MODEL, D_MODEL))] + [ANY] * n,
        out_specs=[_rows(tm, D_MODEL), pl.BlockSpec((None, tm, WIDTH), lambda i: (Y_SLOT[2], i, 0)),
                   _rows(tm, WIDTH), _rows(tm, 128), _rows(tm, D_MODEL)] + [ANY] * n,
        out_shape=[jax.ShapeDtypeStruct((S, D_MODEL), F32), jax.ShapeDtypeStruct(y4.shape, BF16),
                   jax.ShapeDtypeStruct((S, WIDTH), BF16), jax.ShapeDtypeStruct((S, 128), F32),
                   jax.ShapeDtypeStruct((S, D_MODEL), BF16)]
                  + [jax.ShapeDtypeStruct((N_DEV,) + s.shape, s.dtype) for s in shards],
        input_output_aliases={1: 1},
        scratch_shapes=[pltpu.VMEM((tm, WIDTH), F32)] + (_dma_sems(4 * n, 4 * n, n, 3 * n, 3 * n) if n else []),
        compiler_params=_params(("arbitrary",)),
    )(x, y4, *o_g, *l_g, proj, *([gates] * 4), wb, wout, *shards)
    return res[:5], list(res[5:])


def _loss_head(x, g, tgt):
    S = x.shape[0]
    tm = 512

    def body(x_ref, g_ref, t_ref, loss_ref, dx_ref, dg_ref):
        @pl.when(pl.program_id(0) == 0)
        def _():
            loss_ref[...] = jnp.zeros_like(loss_ref)
            dg_ref[...] = jnp.zeros_like(dg_ref)

        xf = x_ref[...]
        r = lax.rsqrt(jnp.mean(xf * xf, axis=-1, keepdims=True) + EPS)
        xhat = xf * r
        gv = g_ref[...]
        err = xhat * gv - t_ref[...]
        e2 = jnp.sum(err * err, axis=-1, keepdims=True)
        loss_ref[...] += (0.5 / D_MODEL) * jnp.sum(e2, axis=0, keepdims=True)
        dy = err * (1.0 / D_MODEL)
        dg_ref[...] += jnp.sum(dy * xhat, axis=0, keepdims=True)
        dxh = dy * gv
        dx_ref[...] = r * (dxh - xhat * jnp.mean(dxh * xhat, axis=-1, keepdims=True))

    return pl.pallas_call(
        body, name="loss_head",
        grid=(S // tm,),
        in_specs=[_rows(tm, D_MODEL), _full((1, D_MODEL)), _rows(tm, D_MODEL)],
        out_specs=[_full((1, 128)), _rows(tm, D_MODEL), _full((1, D_MODEL))],
        out_shape=[jax.ShapeDtypeStruct((1, 128), F32), jax.ShapeDtypeStruct((S, D_MODEL), F32),
                   jax.ShapeDtypeStruct((1, D_MODEL), F32)],
        compiler_params=_params(("arbitrary",)),
    )(x, g, tgt)


def _merge_bwd(dxo, y4, oc, proj, gates, wb, wout, grads=()):
    S = dxo.shape[0]
    tm = 256
    n = len(grads)
    nsteps = S // tm

    def body(dx_ref, y_ref, oc_ref, cg_ref, *rest):
        gm = rest[:4]
        wb_ref, wo_ref = rest[4:6]
        g_in = rest[6:6 + n]
        dy_ref, doc_ref, delta_ref, dcg_ref, dgm_ref, dt_ref = rest[6 + n:12 + n]
        g_out, sems = rest[12 + n:12 + 2 * n], rest[12 + 2 * n:]
        i = pl.program_id(0)

        if n:
            @pl.when(i == 0)
            def _():
                _comm_start(_rs_first(g_in, g_out, *sems))

        dz = _dot_nt(dx_ref[...].astype(BF16), wo_ref[...])
        for b in range(4):
            gate = _sigmoid(gm[b][...].astype(F32))
            t = _dot(y_ref[Y_SLOT[b]], wb_ref[b])
            dgm_ref[:, b * D_MODEL:(b + 1) * D_MODEL] = (dz * t * gate * (1.0 - gate)).astype(BF16)
            dt = (dz * gate).astype(BF16)
            dt_ref[b] = dt
            dyb = _dot_nt(dt, wb_ref[b])
            if b == 2:
                cg = cg_ref[...].astype(F32)
                oc = oc_ref[...].astype(F32)
                doc = dyb * _silu(cg)
                dcg_ref[...] = (dyb * oc * _dsilu(cg)).astype(BF16)
                doc_ref[...] = doc.astype(BF16)
                prod = doc * oc
                delta_ref[...] = _put_cols([jnp.sum(prod[:, h * HEAD:(h + 1) * HEAD], axis=1, keepdims=True)
                                            for h in range(N_HEAD)])
            else:
                dy_ref[b if b < 2 else 2] = dyb.astype(BF16)

        if n:
            @pl.when(i == nsteps - 1)
            def _():
                _comm_wait(_rs_first(g_in, g_out, *sems))

    res = pl.pallas_call(
        body, name="merge_bwd_scatter" if n else "merge_bwd",
        grid=(nsteps,),
        in_specs=[_rows(tm, D_MODEL), pl.BlockSpec((4, tm, WIDTH), lambda i: (0, i, 0)),
                  _rows(tm, WIDTH), _rows(tm, WIDTH, CB_CGATE)] + _gate_specs(tm)
                 + [_full((4, WIDTH, D_MODEL)), _full((D_MODEL, D_MODEL))] + [ANY] * n,
        out_specs=[pl.BlockSpec((3, tm, WIDTH), lambda i: (0, i, 0)), _rows(tm, WIDTH), _rows(tm, 128),
                   _rows(tm, WIDTH), _rows(tm, 4 * D_MODEL), pl.BlockSpec((4, tm, D_MODEL), lambda i: (0, i, 0))]
                  + [ANY] * n,
        out_shape=[jax.ShapeDtypeStruct((3, S, WIDTH), BF16), jax.ShapeDtypeStruct((S, WIDTH), BF16),
                   jax.ShapeDtypeStruct((S, 128), F32), jax.ShapeDtypeStruct((S, WIDTH), BF16),
                   jax.ShapeDtypeStruct((S, 4 * D_MODEL), BF16), jax.ShapeDtypeStruct((4, S, D_MODEL), BF16)]
                  + [jax.ShapeDtypeStruct(g.shape[1:], g.dtype) for g in grads],
        scratch_shapes=_dma_sems(n, n) if n else [],
        compiler_params=_params(("arbitrary",)),
    )(dxo, y4, oc, proj, *([gates] * 4), wb, wout, *grads)
    return res[:6], list(res[6:])


def _dw_branch(y4, dt):
    S = y4.shape[1]
    tk = 1024
    nk = S // tk

    def body(y_ref, dt_ref, o_ref, acc):
        kk = pl.program_id(1)

        @pl.when(kk == 0)
        def _():
            acc[...] = jnp.zeros_like(acc)

        acc[...] += _dot_tn(y_ref[...], dt_ref[...])

        @pl.when(kk == nk - 1)
        def _():
            o_ref[...] = acc[...].astype(BF16)

    def slot(b):
        return jnp.where(b == 2, Y_SLOT[2], jnp.where(b == 3, Y_SLOT[3], b))

    return pl.pallas_call(
        body, name="dw_branch",
        grid=(4, nk),
        in_specs=[pl.BlockSpec((None, tk, WIDTH), lambda b, k: (slot(b), k, 0)),
                  pl.BlockSpec((None, tk, D_MODEL), lambda b, k: (b, k, 0))],
        out_specs=pl.BlockSpec((None, WIDTH, D_MODEL), lambda b, k: (b, 0, 0)),
        out_shape=jax.ShapeDtypeStruct((4, WIDTH, D_MODEL), BF16),
        scratch_shapes=[pltpu.VMEM((WIDTH, D_MODEL), F32)],
        compiler_params=_params(("parallel", "arbitrary")),
    )(y4, dt)


def _attn_bwd(q, qcb, k, kcb, v, vcb, do, lse, delta, bps):
    S = q.shape[0]
    tm = ATT_TILE
    nb = tm // CHUNK
    nblk = S // CHUNK

    ncur = nb * N_HEAD
    nprev = (nb + 1) * N_HEAD

    def body(q_ref, k_ref, v_ref, do_ref, l_ref, d_ref, kh_ref, vh_ref, qn_ref, don_ref, ln_ref, dn_ref,
             dq_ref, dk_ref, dv_ref, sc_s, sp_s, dpc_s, dpp_s, pc_s, pp_s, dsc_s, dsp_s):
        i = pl.program_id(0)

        def rows_of(n):
            if n < nb:
                rs = slice(n * CHUNK, (n + 1) * CHUNK)
                return rs, q_ref, do_ref, l_ref, d_ref
            return slice(0, CHUNK), qn_ref, don_ref, ln_ref, dn_ref

        def prev_kv(n, cs):
            if n == 0:
                return kh_ref[:, cs], vh_ref[:, cs]
            ps = slice((n - 1) * CHUNK, n * CHUNK)
            return k_ref[ps, cs], v_ref[ps, cs]

        def blk(n, h):
            return slice((n * N_HEAD + h) * CHUNK, (n * N_HEAD + h + 1) * CHUNK)

        pens, lses, deltas = [], [], []
        for n in range(nb + 1):
            rs, qr, dor, lr, dr = rows_of(n)
            gb = i * nb + n
            pen = jnp.where(gb % bps != 0, 0.0, NEG)
            if n == nb:
                pen = pen + jnp.where(gb < nblk, 0.0, NEG)
            pens.append(jnp.full((N_HEAD * CHUNK, 1), pen, F32))
            lblk, dblk = lr[rs, :], dr[rs, :]
            for h in range(N_HEAD):
                cs = slice(h * HEAD, (h + 1) * HEAD)
                qh, doh = qr[rs, cs], dor[rs, cs]
                lses.append(_col(lblk, h))
                deltas.append(_col(dblk, h))
                kp, vp = prev_kv(n, cs)
                sp_s[blk(n, h), :] = _dot_nt(qh, kp)
                dpp_s[blk(n, h), :] = _dot_nt(doh, vp)
                if n < nb:
                    sc_s[blk(n, h), :] = _dot_nt(qh, k_ref[rs, cs])
                    dpc_s[blk(n, h), :] = _dot_nt(doh, v_ref[rs, cs])
        lse = jnp.concatenate(lses, axis=0)
        delta = jnp.concatenate(deltas, axis=0)
        row = lax.broadcasted_iota(jnp.int32, (nprev * CHUNK, CHUNK), 0) & (CHUNK - 1)
        col = lax.broadcasted_iota(jnp.int32, (nprev * CHUNK, CHUNK), 1)
        sp = jnp.where(col >= row, sp_s[...] * ATT_SCALE, NEG) + jnp.concatenate(pens, axis=0)
        pp = jnp.exp(sp - lse)
        pp_s[...] = pp.astype(BF16)
        dsp_s[...] = (pp * (dpp_s[...] - delta)).astype(BF16)
        nc = ncur * CHUNK
        sc = jnp.where(col[:nc] <= row[:nc], sc_s[...] * ATT_SCALE, NEG)
        pc = jnp.exp(sc - lse[:nc])
        pc_s[...] = pc.astype(BF16)
        dsc_s[...] = (pc * (dpc_s[...] - delta[:nc])).astype(BF16)
        for n in range(nb):
            rs, qr, dor, _, _ = rows_of(n)
            rn, qnr, donr, _, _ = rows_of(n + 1)
            for h in range(N_HEAD):
                cs = slice(h * HEAD, (h + 1) * HEAD)
                kp, _ = prev_kv(n, cs)
                dq = _dot(dsc_s[blk(n, h), :], k_ref[rs, cs]) + _dot(dsp_s[blk(n, h), :], kp)
                dq_ref[rs, cs] = (dq * ATT_SCALE).astype(BF16)
                dk = _dot_tn(dsc_s[blk(n, h), :], qr[rs, cs]) + _dot_tn(dsp_s[blk(n + 1, h), :], qnr[rn, cs])
                dk_ref[rs, cs] = (dk * ATT_SCALE).astype(BF16)
                dv = _dot_tn(pc_s[blk(n, h), :], dor[rs, cs]) + _dot_tn(pp_s[blk(n + 1, h), :], donr[rn, cs])
                dv_ref[rs, cs] = dv.astype(BF16)

    def prev_halo(cb):
        return pl.BlockSpec((CHUNK, WIDTH), lambda i: (jnp.maximum(i * nb - 1, 0), cb))

    def next_halo(width, cb=0):
        return pl.BlockSpec((CHUNK, width), lambda i: (jnp.minimum(i * nb + nb, nblk - 1), cb))

    return pl.pallas_call(
        body, name=f"attn_bwd_{bps}",
        grid=(S // tm,),
        in_specs=[_rows(tm, WIDTH, qcb), _rows(tm, WIDTH, kcb), _rows(tm, WIDTH, vcb), _rows(tm, WIDTH),
                  _rows(tm, 128), _rows(tm, 128), prev_halo(kcb), prev_halo(vcb),
                  next_halo(WIDTH, qcb), next_halo(WIDTH), next_halo(128), next_halo(128)],
        out_specs=[_rows(tm, WIDTH), _rows(tm, WIDTH), _rows(tm, WIDTH)],
        out_shape=[jax.ShapeDtypeStruct((S, WIDTH), BF16)] * 3,
        scratch_shapes=[pltpu.VMEM((ncur * CHUNK, CHUNK), F32), pltpu.VMEM((nprev * CHUNK, CHUNK), F32),
                        pltpu.VMEM((ncur * CHUNK, CHUNK), F32), pltpu.VMEM((nprev * CHUNK, CHUNK), F32),
                        pltpu.VMEM((ncur * CHUNK, CHUNK), BF16), pltpu.VMEM((nprev * CHUNK, CHUNK), BF16),
                        pltpu.VMEM((ncur * CHUNK, CHUNK), BF16), pltpu.VMEM((nprev * CHUNK, CHUNK), BF16)],
        compiler_params=_params(("parallel",)),
    )(q, k, v, do, lse, delta, k, v, q, do, lse, delta)


def _dilated_split(d):
    hp = min(N_HEAD, 16 // d)
    return hp, N_HEAD // hp, HEAD * hp


def _attn_fwd_dilated(proj, qcb, kcb, vcb, d):
    S = proj.shape[0]
    T = CHUNK * d
    hp, nh, cw = _dilated_split(d)
    nblocks = d * hp

    def body(q_ref, k_ref, v_ref, o_ref, l_ref, qf, kf, vf, kpf, vpf, kst, vst, of, lf, sc_s, sp_s, pc_s, pp_s):
        i, hh = pl.program_id(0), pl.program_id(1)
        for j in range(hp):
            cs = slice(j * HEAD, (j + 1) * HEAD)
            qf[j] = q_ref[:, cs].astype(F32)
            kf[j] = k_ref[:, cs].astype(F32)
            vf[j] = v_ref[:, cs].astype(F32)

        @pl.when(i == 0)
        def _():
            kpf[...] = jnp.zeros_like(kpf)
            vpf[...] = jnp.zeros_like(vpf)

        @pl.when(i > 0)
        def _():
            kpf[...] = kst[hh]
            vpf[...] = vst[hh]

        def blk(ref, r, j):
            return ref.at[j][pl.ds(r, CHUNK, stride=d), :].astype(BF16)

        def bs(r, j):
            return slice((r * hp + j) * CHUNK, (r * hp + j + 1) * CHUNK)

        for r in range(d):
            for j in range(hp):
                qb = blk(qf, r, j)
                sc_s[bs(r, j), :] = _dot_nt(qb, blk(kf, r, j))
                sp_s[bs(r, j), :] = _dot_nt(qb, blk(kpf, r, j))
        row = lax.broadcasted_iota(jnp.int32, (nblocks * CHUNK, CHUNK), 0) & (CHUNK - 1)
        col = lax.broadcasted_iota(jnp.int32, (nblocks * CHUNK, CHUNK), 1)
        sc = jnp.where(col <= row, sc_s[...] * ATT_SCALE, NEG)
        sp = jnp.where(col >= row, sp_s[...] * ATT_SCALE, NEG) + jnp.where(i > 0, 0.0, NEG)
        m = jnp.maximum(jnp.max(sc, axis=-1, keepdims=True), jnp.max(sp, axis=-1, keepdims=True))
        ec = jnp.exp(sc - m)
        ep = jnp.exp(sp - m)
        den = jnp.sum(ec, axis=-1, keepdims=True) + jnp.sum(ep, axis=-1, keepdims=True)
        inv = 1.0 / den
        pc_s[...] = (ec * inv).astype(BF16)
        pp_s[...] = (ep * inv).astype(BF16)
        lse = m + jnp.log(den)
        lane = lax.broadcasted_iota(jnp.int32, (CHUNK, 128), 1)
        for r in range(d):
            lblk = jnp.zeros((CHUNK, 128), F32)
            for j in range(hp):
                o = _dot(pc_s[bs(r, j), :], blk(vf, r, j)) + _dot(pp_s[bs(r, j), :], blk(vpf, r, j))
                of.at[j][pl.ds(r, CHUNK, stride=d), :] = o
                lblk = jnp.where(lane == hh * hp + j, lse[bs(r, j)], lblk)
            lf[pl.ds(r, CHUNK, stride=d), :] = lblk
        for j in range(hp):
            o_ref[:, j * HEAD:(j + 1) * HEAD] = of[j].astype(BF16)

        @pl.when(hh == 0)
        def _():
            l_ref[...] = lf[...]

        @pl.when(hh > 0)
        def _():
            l_ref[...] += lf[...]

        kst[hh] = kf[...]
        vst[hh] = vf[...]

    def cols(cb):
        return pl.BlockSpec((T, cw), lambda i, hh: (i, cb * nh + hh))

    tile = pltpu.VMEM((hp, T, HEAD), F32)
    return pl.pallas_call(
        body, name=f"attn_fwd_dilated_{d}",
        grid=(S // T, nh),
        in_specs=[cols(qcb), cols(kcb), cols(vcb)],
        out_specs=[cols(0), pl.BlockSpec((T, 128), lambda i, hh: (i, 0))],
        out_shape=[jax.ShapeDtypeStruct((S, WIDTH), BF16), jax.ShapeDtypeStruct((S, 128), F32)],
        scratch_shapes=[tile, tile, tile, tile, tile,
                        pltpu.VMEM((nh, hp, T, HEAD), F32), pltpu.VMEM((nh, hp, T, HEAD), F32),
                        tile, pltpu.VMEM((T, 128), F32),
                        pltpu.VMEM((nblocks * CHUNK, CHUNK), F32), pltpu.VMEM((nblocks * CHUNK, CHUNK), F32),
                        pltpu.VMEM((nblocks * CHUNK, CHUNK), BF16), pltpu.VMEM((nblocks * CHUNK, CHUNK), BF16)],
        compiler_params=_params(("arbitrary", "arbitrary")),
    )(proj, proj, proj)


def _attn_bwd_dilated(proj, qcb, kcb, vcb, do, lse, delta, d):
    S = proj.shape[0]
    T = CHUNK * d
    nt = S // T
    hp, nh, cw = _dilated_split(d)
    nblocks = d * hp

    def body(q_ref, k_ref, v_ref, do_ref, l_ref, d_ref, dq_ref, dk_ref, dv_ref,
             qf, dof, kf, vf, kpf, vpf, dqf, acck, accv, newk, newv,
             sc_s, sp_s, dpc_s, dpp_s, pc_s, pp_s, dsc_s, dsp_s):
        hh, i = pl.program_id(0), pl.program_id(1)

        @pl.when(i == 0)
        def _():
            for ref in (kpf, vpf, acck, accv):
                ref[...] = jnp.zeros_like(ref)
            dk_ref[...] = jnp.zeros_like(dk_ref)
            dv_ref[...] = jnp.zeros_like(dv_ref)

        def blk(ref, r, j):
            return ref.at[j][pl.ds(r, CHUNK, stride=d), :].astype(BF16)

        def bs(r, j):
            return slice((r * hp + j) * CHUNK, (r * hp + j + 1) * CHUNK)

        @pl.when(i < nt)
        def _():
            for j in range(hp):
                cs = slice(j * HEAD, (j + 1) * HEAD)
                qf[j] = q_ref[:, cs].astype(F32)
                dof[j] = do_ref[:, cs].astype(F32)
                kf[j] = k_ref[:, cs].astype(F32)
                vf[j] = v_ref[:, cs].astype(F32)
            lses, deltas = [], []
            for r in range(d):
                lblk = l_ref[pl.ds(r, CHUNK, stride=d), :]
                dblk = d_ref[pl.ds(r, CHUNK, stride=d), :]
                for j in range(hp):
                    lses.append(_col(lblk, hh * hp + j))
                    deltas.append(_col(dblk, hh * hp + j))
                    qb, dob = blk(qf, r, j), blk(dof, r, j)
                    sc_s[bs(r, j), :] = _dot_nt(qb, blk(kf, r, j))
                    dpc_s[bs(r, j), :] = _dot_nt(dob, blk(vf, r, j))
                    sp_s[bs(r, j), :] = _dot_nt(qb, blk(kpf, r, j))
                    dpp_s[bs(r, j), :] = _dot_nt(dob, blk(vpf, r, j))
            lse = jnp.concatenate(lses, axis=0)
            delta = jnp.concatenate(deltas, axis=0)
            row = lax.broadcasted_iota(jnp.int32, (nblocks * CHUNK, CHUNK), 0) & (CHUNK - 1)
            col = lax.broadcasted_iota(jnp.int32, (nblocks * CHUNK, CHUNK), 1)
            sp = jnp.where(col >= row, sp_s[...] * ATT_SCALE, NEG) + jnp.where(i > 0, 0.0, NEG)
            pp = jnp.exp(sp - lse)
            pp_s[...] = pp.astype(BF16)
            dsp_s[...] = (pp * (dpp_s[...] - delta)).astype(BF16)
            sc = jnp.where(col <= row, sc_s[...] * ATT_SCALE, NEG)
            pc = jnp.exp(sc - lse)
            pc_s[...] = pc.astype(BF16)
            dsc_s[...] = (pc * (dpc_s[...] - delta)).astype(BF16)
            for r in range(d):
                rows = pl.ds(r, CHUNK, stride=d)
                for j in range(hp):
                    qb, dob = blk(qf, r, j), blk(dof, r, j)
                    dsc, dsp = dsc_s[bs(r, j), :], dsp_s[bs(r, j), :]
                    dqf.at[j][rows, :] = (_dot(dsc, blk(kf, r, j)) + _dot(dsp, blk(kpf, r, j))) * ATT_SCALE
                    newk.at[j][rows, :] = _dot_tn(dsc, qb) * ATT_SCALE
                    newv.at[j][rows, :] = _dot_tn(pc_s[bs(r, j), :], dob)
                    acck.at[j][rows, :] += _dot_tn(dsp, qb) * ATT_SCALE
                    accv.at[j][rows, :] += _dot_tn(pp_s[bs(r, j), :], dob)
            for j in range(hp):
                dq_ref[:, j * HEAD:(j + 1) * HEAD] = dqf[j].astype(BF16)

        @pl.when(i > 0)
        def _():
            for j in range(hp):
                dk_ref[:, j * HEAD:(j + 1) * HEAD] = acck[j].astype(BF16)
                dv_ref[:, j * HEAD:(j + 1) * HEAD] = accv[j].astype(BF16)

        @pl.when(i < nt)
        def _():
            acck[...] = newk[...]
            accv[...] = newv[...]
            kpf[...] = kf[...]
            vpf[...] = vf[...]

    def cur(width, cb, nsplit):
        return pl.BlockSpec((T, width), lambda hh, i: (jnp.minimum(i, nt - 1), cb * nsplit + hh * (nsplit > 1)))

    def lag():
        return pl.BlockSpec((T, cw), lambda hh, i: (jnp.maximum(i - 1, 0), hh))

    tile = pltpu.VMEM((hp, T, HEAD), F32)
    f32s = pltpu.VMEM((nblocks * CHUNK, CHUNK), F32)
    b16s = pltpu.VMEM((nblocks * CHUNK, CHUNK), BF16)
    return pl.pallas_call(
        body, name=f"attn_bwd_dilated_{d}",
        grid=(nh, nt + 1),
        in_specs=[cur(cw, qcb, nh), cur(cw, kcb, nh), cur(cw, vcb, nh), cur(cw, 0, nh), cur(128, 0, 1), cur(128, 0, 1)],
        out_specs=[cur(cw, 0, nh), lag(), lag()],
        out_shape=[jax.ShapeDtypeStruct((S, WIDTH), BF16)] * 3,
        scratch_shapes=[tile] * 11 + [f32s] * 4 + [b16s] * 4,
        compiler_params=_params(("arbitrary", "arbitrary")),
    )(proj, proj, proj, do, lse, delta)


def _abm_bwd(proj, dy3, ln_g, ln_b, wsm, wsm_t, bias_full, pool_w, pool_wt, pool_scale, kv):
    S = proj.shape[0]
    tm = 512
    nchunk = tm // CHUNK
    nblk = S // CHUNK

    def body(u_ref, v_ref, ag_ref, p_ref, ph_ref, pg_ref, pgn_ref, mq_ref, mg_ref, dy_ref, dypn_ref,
             lng_ref, lnb_ref, wsm_ref, wsmt_ref, bias_ref, pw_ref, pwt_ref, ps_ref, kv_ref,
             dab_ref, dm_ref, dlng_ref, dlnb_ref, dws_ref, dbias_ref, dpw_ref, dps_ref, dkv_ref,
             mix, dvl, ddn):
        i = pl.program_id(0)

        @pl.when(i == 0)
        def _():
            for r in (dlng_ref, dlnb_ref, dws_ref, dbias_ref, dpw_ref, dps_ref, dkv_ref):
                r[...] = jnp.zeros_like(r)

        au = u_ref[...].astype(F32)
        av = v_ref[...].astype(F32)
        ag = ag_ref[...].astype(F32)
        u = _gelu(au)
        v = _gelu(av)
        vhat, rstd = _layer_norm_fwd(v)
        vln = (vhat * lng_ref[...] + lnb_ref[...]).astype(BF16)
        for c in range(nchunk):
            for h in range(N_HEAD):
                rs, cs = slice(c * CHUNK, (c + 1) * CHUNK), slice(h * HEAD, (h + 1) * HEAD)
                mix[rs, cs] = _dot(wsm_ref[h], vln[rs, cs]) + bias_ref[:, cs]
        dya = dy_ref[0].astype(F32)
        sg = _silu(ag)
        mixed = mix[...]
        dab_ref[:, 2 * WIDTH:3 * WIDTH] = (dya * u * mixed * _dsilu(ag)).astype(BF16)
        dab_ref[:, 0:WIDTH] = (dya * mixed * sg * _dgelu(au)).astype(BF16)
        dmixed = dya * u * sg
        dmb = dmixed.astype(BF16)
        tril = (lax.broadcasted_iota(jnp.int32, (CHUNK, CHUNK), 1)
                <= lax.broadcasted_iota(jnp.int32, (CHUNK, CHUNK), 0))
        for c in range(nchunk):
            rs = slice(c * CHUNK, (c + 1) * CHUNK)
            dbias_ref[...] += dmixed[rs, :]
            for h in range(N_HEAD):
                cs = slice(h * HEAD, (h + 1) * HEAD)
                dvl[rs, cs] = _dot(wsmt_ref[h], dmb[rs, cs])
                dws_ref[h] += jnp.where(tril, _dot_nt(dmb[rs, cs], vln[rs, cs]), 0.0)
        dvln = dvl[...]
        dlng_ref[...] += jnp.sum(dvln * vhat, axis=0, keepdims=True)
        dlnb_ref[...] += jnp.sum(dvln, axis=0, keepdims=True)
        dvh = dvln * lng_ref[...]
        dv = rstd * (dvh - jnp.mean(dvh, axis=-1, keepdims=True)
                     - vhat * jnp.mean(dvh * vhat, axis=-1, keepdims=True))
        dab_ref[:, WIDTH:2 * WIDTH] = (dv * _dgelu(av)).astype(BF16)

        halo_ok = (i > 0).astype(F32)
        for c in range(nchunk):
            rs = slice(c * CHUNK, (c + 1) * CHUNK)
            for g, win in enumerate(POOL_WINDOWS):
                cs = slice(g * HEAD, (g + 1) * HEAD)
                bcur, bprev = _band_masks(win)
                cur = p_ref[rs, cs]
                if c == 0:
                    prev = (ph_ref[:, cs].astype(F32) * halo_ok).astype(BF16)
                else:
                    prev = p_ref[(c - 1) * CHUNK:c * CHUNK, cs]
                sums = _dot(bcur, cur) + _dot(bprev, prev)
                dvl[rs, cs] = sums * _inv_count(i * tm + c * CHUNK, win) - cur.astype(F32)
        dmat = dvl[...].astype(BF16)
        for g in range(4):
            cs = slice(g * HEAD, (g + 1) * HEAD)
            mix[:, cs] = _dot(dmat[:, cs], pw_ref[g])
        yg = mix[...]
        pg = pg_ref[...].astype(F32)
        dyp = dy_ref[1].astype(F32)
        dyy = dyp * _silu(pg)
        scale = ps_ref[...]
        dab_ref[:, 4 * WIDTH:5 * WIDTH] = (dyp * yg * scale * _dsilu(pg)).astype(BF16)
        dps_ref[...] += jnp.sum(dyy * yg, axis=0, keepdims=True)
        dyg = (dyy * scale).astype(BF16)
        for g in range(4):
            cs = slice(g * HEAD, (g + 1) * HEAD)
            dpw_ref[g] += _dot_tn(dmat[:, cs], dyg[:, cs])
            mix[:, cs] = _dot(dyg[:, cs], pwt_ref[g])
        next_ok = (i + 1 < S // tm).astype(F32)
        dygn = (dypn_ref[...].astype(F32) * _silu(pgn_ref[...].astype(F32)) * scale * next_ok).astype(BF16)
        for c in range(nchunk + 1):
            for g, win in enumerate(POOL_WINDOWS):
                cs = slice(g * HEAD, (g + 1) * HEAD)
                if c < nchunk:
                    dd = mix[c * CHUNK:(c + 1) * CHUNK, cs]
                else:
                    dd = _dot(dygn[:, cs], pwt_ref[g])
                ddn[c * CHUNK:(c + 1) * CHUNK, cs] = dd * _inv_count(i * tm + c * CHUNK, win)
        ddnb = ddn[...].astype(BF16)
        for c in range(nchunk):
            rs = slice(c * CHUNK, (c + 1) * CHUNK)
            ns = slice((c + 1) * CHUNK, (c + 2) * CHUNK)
            for g, win in enumerate(POOL_WINDOWS):
                cs = slice(g * HEAD, (g + 1) * HEAD)
                bcur, bprev = _band_masks(win)
                dp = _dot_tn(bcur, ddnb[rs, cs]) + _dot_tn(bprev, ddnb[ns, cs]) - mix[rs, cs]
                dab_ref[rs, 3 * WIDTH + g * HEAD:3 * WIDTH + (g + 1) * HEAD] = dp.astype(BF16)

        mg = mg_ref[...].astype(F32)
        dym = dy_ref[2].astype(F32)
        dob = (dym * _silu(mg)).astype(BF16)
        for h in range(N_HEAD):
            cs = slice(h * HEAD, (h + 1) * HEAD)
            vs = slice(WIDTH + h * HEAD, WIDTH + (h + 1) * HEAD)
            qh = mq_ref[:, cs]
            p = _mem_softmax(qh, kv_ref[:, cs])
            pb = p.astype(BF16)
            mix[:, cs] = _dot(pb, kv_ref[:, vs])
            dp = _dot_nt(dob[:, cs], kv_ref[:, vs])
            ds = (p * (dp - jnp.sum(p * dp, axis=-1, keepdims=True))).astype(BF16)
            dm_ref[:, cs] = (_dot(ds, kv_ref[:, cs]) * ATT_SCALE).astype(BF16)
            dkv_ref[:, cs] += _dot_tn(ds, qh) * ATT_SCALE
            dkv_ref[:, vs] += _dot_tn(pb, dob[:, cs])
        dm_ref[:, WIDTH:2 * WIDTH] = (dym * mix[...] * _dsilu(mg)).astype(BF16)

    blk = tm // CHUNK
    small = [_full((1, WIDTH)), _full((1, WIDTH)), _full((N_HEAD, CHUNK, CHUNK)), _full((CHUNK, WIDTH)),
             _full((4, HEAD, HEAD)), _full((1, WIDTH)), _full((MEM_LEN, 2 * WIDTH))]
    return pl.pallas_call(
        body, name="abm_bwd",
        grid=(S // tm,),
        in_specs=[_rows(tm, WIDTH, CB_U), _rows(tm, WIDTH, CB_V), _rows(tm, WIDTH, CB_AGATE),
                  _rows(tm, WIDTH, CB_PIN),
                  pl.BlockSpec((CHUNK, WIDTH), lambda i: (jnp.maximum(i * blk - 1, 0), CB_PIN)),
                  _rows(tm, WIDTH, CB_PGATE),
                  pl.BlockSpec((CHUNK, WIDTH), lambda i: (jnp.minimum(i * blk + blk, nblk - 1), CB_PGATE)),
                  _rows(tm, WIDTH, CB_MQ), _rows(tm, WIDTH, CB_MGATE),
                  pl.BlockSpec((3, tm, WIDTH), lambda i: (0, i, 0)),
                  pl.BlockSpec((None, CHUNK, WIDTH), lambda i: (1, jnp.minimum(i * blk + blk, nblk - 1), 0)),
                  _full((1, WIDTH)), _full((1, WIDTH)), _full((N_HEAD, CHUNK, CHUNK)), _full((N_HEAD, CHUNK, CHUNK)),
                  _full((CHUNK, WIDTH)), _full((4, HEAD, HEAD)), _full((4, HEAD, HEAD)), _full((1, WIDTH)),
                  _full((MEM_LEN, 2 * WIDTH))],
        out_specs=[_rows(tm, 5 * WIDTH), _rows(tm, 2 * WIDTH)] + small,
        out_shape=[jax.ShapeDtypeStruct((S, D_BRANCHES), BF16), jax.ShapeDtypeStruct((S, 2 * WIDTH), BF16),
                   jax.ShapeDtypeStruct((1, WIDTH), F32), jax.ShapeDtypeStruct((1, WIDTH), F32),
                   jax.ShapeDtypeStruct((N_HEAD, CHUNK, CHUNK), F32), jax.ShapeDtypeStruct((CHUNK, WIDTH), F32),
                   jax.ShapeDtypeStruct((4, HEAD, HEAD), F32), jax.ShapeDtypeStruct((1, WIDTH), F32),
                   jax.ShapeDtypeStruct((MEM_LEN, 2 * WIDTH), F32)],
        scratch_shapes=[pltpu.VMEM((tm, WIDTH), F32), pltpu.VMEM((tm, WIDTH), F32),
                        pltpu.VMEM((tm + CHUNK, WIDTH), F32)],
        compiler_params=_params(("arbitrary",)),
    )(proj, proj, proj, proj, proj, proj, proj, proj, proj, dy3, dy3,
      ln_g, ln_b, wsm, wsm_t, bias_full, pool_w, pool_wt, pool_scale, kv)


def _bias_reduce(dbias_full):
    def body(d_ref, o_ref):
        d = d_ref[...]
        o_ref[...] = _put_cols([jnp.sum(d[:, h * HEAD:(h + 1) * HEAD], axis=1, keepdims=True) for h in range(N_HEAD)])

    return pl.pallas_call(body, name="bias_reduce", out_shape=jax.ShapeDtypeStruct((CHUNK, 128), F32))(dbias_full)


def _mem_bwd(mem, g, mem_n, w, dkv):
    def body(m_ref, g_ref, mn_ref, w_ref, dkv_ref, dw_ref, dg_ref):
        dkvb = dkv_ref[...].astype(BF16)
        dw_ref[...] = _dot_tn(mn_ref[...], dkvb).astype(BF16)
        dmn = _dot_nt(dkvb, w_ref[...])
        xf = m_ref[...]
        r = lax.rsqrt(jnp.mean(xf * xf, axis=-1, keepdims=True) + EPS)
        dg_ref[...] = jnp.sum(dmn * xf * r, axis=0, keepdims=True)

    return pl.pallas_call(
        body, name="mem_bwd",
        out_shape=[jax.ShapeDtypeStruct((D_MODEL, 2 * WIDTH), BF16), jax.ShapeDtypeStruct((1, D_MODEL), F32)],
        compiler_params=pltpu.CompilerParams(vmem_limit_bytes=VMEM_LIMIT),
    )(mem, g, mem_n, w, dkv)


def _dh_bwd(dpb, wbr, dpg, wg, x, g, dxo, parts=()):
    S = x.shape[0]
    tm = 1024
    tkb, tkg = D_BRANCHES // 4, D_GATES // 4
    nkb = 4
    nk = 8
    ni = S // tm
    n = len(parts)

    def body(dpb_ref, wbr_ref, dpg_ref, wg_ref, x_ref, g_ref, dxo_ref, *rest):
        p_in = rest[:n]
        dx_ref, dg_ref = rest[n:n + 2]
        p_out, acc, sems = rest[n + 2:2 * n + 2], rest[2 * n + 2], rest[2 * n + 3:]
        i, kk = pl.program_id(0), pl.program_id(1)

        @pl.when(jnp.logical_and(i == 0, kk == 0))
        def _():
            dg_ref[...] = jnp.zeros_like(dg_ref)
            if n:
                _comm_start(_rs_second(p_in, p_out, *sems))

        @pl.when(kk == 0)
        def _():
            acc[...] = jnp.zeros_like(acc)

        @pl.when(kk < nkb)
        def _():
            acc[...] += _dot_nt(dpb_ref[...], wbr_ref[...])

        @pl.when(kk >= nkb)
        def _():
            acc[...] += _dot_nt(dpg_ref[...], wg_ref[...])

        @pl.when(kk == nk - 1)
        def _():
            xf = x_ref[...]
            r = lax.rsqrt(jnp.mean(xf * xf, axis=-1, keepdims=True) + EPS)
            xhat = xf * r
            dh = acc[...]
            dg_ref[...] += jnp.sum(dh * xhat, axis=0, keepdims=True)
            dxh = dh * g_ref[...]
            dx_ref[...] = dxo_ref[...] + r * (dxh - xhat * jnp.mean(dxh * xhat, axis=-1, keepdims=True))

        if n:
            @pl.when(jnp.logical_and(i == ni - 1, kk == nk - 1))
            def _():
                _comm_wait(_rs_second(p_in, p_out, *sems))

    res = pl.pallas_call(
        body, name="dh_bwd_scatter" if n else "dh_bwd",
        grid=(ni, nk),
        in_specs=[pl.BlockSpec((tm, tkb), lambda i, k: (i, jnp.minimum(k, nkb - 1))),
                  pl.BlockSpec((D_MODEL, tkb), lambda i, k: (0, jnp.minimum(k, nkb - 1))),
                  pl.BlockSpec((tm, tkg), lambda i, k: (i, jnp.maximum(k - nkb, 0))),
                  pl.BlockSpec((D_MODEL, tkg), lambda i, k: (0, jnp.maximum(k - nkb, 0))),
                  pl.BlockSpec((tm, D_MODEL), lambda i, k: (i, 0)), pl.BlockSpec((1, D_MODEL), lambda i, k: (0, 0)),
                  pl.BlockSpec((tm, D_MODEL), lambda i, k: (i, 0))] + [ANY] * n,
        out_specs=[pl.BlockSpec((tm, D_MODEL), lambda i, k: (i, 0)), pl.BlockSpec((1, D_MODEL), lambda i, k: (0, 0))]
                  + [ANY] * n,
        out_shape=[jax.ShapeDtypeStruct((S, D_MODEL), F32), jax.ShapeDtypeStruct((1, D_MODEL), F32)]
                  + [jax.ShapeDtypeStruct(p.shape, p.dtype) for p in parts],
        scratch_shapes=[pltpu.VMEM((tm, D_MODEL), F32)] + (_dma_sems(3 * n, 3 * n, n) if n else []),
        compiler_params=_params(("arbitrary", "arbitrary")),
    )(dpb, wbr, dpg, wg, x, g, dxo, *parts)
    return res[0], res[1], list(res[2:])


def _matmul_tn(a, b, tn, name, parts=()):
    K, M = a.shape
    N = b.shape[1]
    tk = 1024
    nk = K // tk
    nj = N // tn
    n = len(parts)

    def body(a_ref, b_ref, *rest):
        p_in, o_ref, p_out = rest[:n], rest[n], rest[n + 1:2 * n + 1]
        acc, sems = rest[2 * n + 1], rest[2 * n + 2:]
        j, kk = pl.program_id(0), pl.program_id(1)

        if n:
            @pl.when(jnp.logical_and(j == 0, kk == 0))
            def _():
                _comm_start(_rs_second(p_in, p_out, *sems))

        @pl.when(kk == 0)
        def _():
            acc[...] = jnp.zeros_like(acc)

        acc[...] += _dot_tn(a_ref[...].astype(BF16), b_ref[...].astype(BF16))

        @pl.when(kk == nk - 1)
        def _():
            o_ref[...] = acc[...].astype(BF16)

        if n:
            @pl.when(jnp.logical_and(j == nj - 1, kk == nk - 1))
            def _():
                _comm_wait(_rs_second(p_in, p_out, *sems))

    res = pl.pallas_call(
        body, name=name,
        grid=(nj, nk),
        in_specs=[pl.BlockSpec((tk, M), lambda j, k: (k, 0)), pl.BlockSpec((tk, tn), lambda j, k: (k, j))] + [ANY] * n,
        out_specs=[pl.BlockSpec((M, tn), lambda j, k: (0, j))] + [ANY] * n,
        out_shape=[jax.ShapeDtypeStruct((M, N), BF16)] + [jax.ShapeDtypeStruct(p.shape, p.dtype) for p in parts],
        scratch_shapes=[pltpu.VMEM((M, tn), F32)] + (_dma_sems(3 * n, 3 * n, n) if n else []),
        compiler_params=_params(("arbitrary", "arbitrary")),
    )(a, b, *parts)
    return (res[0], list(res[1:])) if n else res[0]


def _row_tile(R, C):
    for cand in (512, 256, 128, 64, 32, 16, 8):
        if R % cand == 0 and cand * C * 4 <= (2 << 20):
            return cand
    return R


def _adamw_update(p_ref, w_ref, m_ref, v_ref, g_ref, d_ref, nm_ref, nv_ref):
    c1 = 1.0 / (1.0 - ADAM_B1 ** ADAM_STEP)
    c2 = 1.0 / (1.0 - ADAM_B2 ** ADAM_STEP)
    g = p_ref[0].astype(F32)
    for k in range(1, p_ref.shape[0]):
        g = g + p_ref[k].astype(F32)
    nm = ADAM_B1 * m_ref[...] + (1.0 - ADAM_B1) * g
    nv = ADAM_B2 * v_ref[...] + (1.0 - ADAM_B2) * (g * g)
    g_ref[...] = g
    nm_ref[...] = nm
    nv_ref[...] = nv
    d_ref[...] = -ADAM_LR * ((nm * c1) / (jnp.sqrt(nv * c2) + ADAM_EPS) + ADAM_WD * w_ref[...])


def _adamw(parts, w, m, v, name):
    P, R, C = parts.shape
    tr = _row_tile(R, C)

    def body(*refs):
        _adamw_update(*refs)

    spec = pl.BlockSpec((tr, C), lambda i: (i, 0))
    return pl.pallas_call(
        body, name=name,
        grid=(R // tr,),
        in_specs=[pl.BlockSpec((P, tr, C), lambda i: (0, i, 0)), spec, spec, spec],
        out_specs=[spec] * 4,
        out_shape=[jax.ShapeDtypeStruct((R, C), F32)] * 4,
        compiler_params=_params(("parallel",)),
    )(parts, w, m, v)


def _adamw_layers(parts, w, m, v, name):
    depth = len(parts)
    P, R, C = parts[0].shape
    tr = _row_tile(R, C)

    def body(*refs):
        layer = pl.program_id(0)
        for k in range(depth):
            @pl.when(layer == k)
            def _(k=k):
                _adamw_update(refs[k], *refs[depth:])

    def part_spec(k):
        return pl.BlockSpec((P, tr, C), lambda l, i: (0, jnp.where(l == k, i, 0), 0))

    spec = pl.BlockSpec((None, tr, C), lambda l, i: (l, i, 0))
    return pl.pallas_call(
        body, name=name,
        grid=(depth, R // tr),
        in_specs=[part_spec(k) for k in range(depth)] + [spec] * 3,
        out_specs=[spec] * 4,
        out_shape=[jax.ShapeDtypeStruct((depth, R, C), F32)] * 4,
        compiler_params=_params(("arbitrary", "arbitrary")),
    )(*parts, w, m, v)


def _place():
    return lax.axis_index("x"), lax.axis_index("y"), lax.axis_index("c")


def _all_gather(shards):
    n = len(shards)

    def body(*refs):
        ins, outs = refs[:n], refs[n:2 * n]
        send1, recv1, local_sems, send2, recv2 = refs[2 * n:]
        first = _ag_first(ins, outs, send1, recv1, local_sems)
        second = _ag_second(outs, send2, recv2)
        _comm_start(first)
        for j in range(3):
            for a in range(n):
                first[2][4 * a + 1 + j].wait_recv()
            for a in range(n):
                second[1][3 * a + j].start()
        for a in range(n):
            first[2][4 * a].wait_recv()
        for cp in second[2]:
            cp.wait_recv()
        for cp in first[1] + second[1]:
            cp.wait_send()
        for cp in first[0]:
            cp.wait()

    return pl.pallas_call(
        body, name="weights_all_gather",
        in_specs=[ANY] * n, out_specs=[ANY] * n,
        out_shape=[jax.ShapeDtypeStruct((N_DEV,) + s.shape, s.dtype) for s in shards],
        scratch_shapes=_dma_sems(4 * n, 4 * n, n, 3 * n, 3 * n),
        compiler_params=pltpu.CompilerParams(has_side_effects=True),
    )(*shards)


N_BIG = 4


def _dev(p):
    return 4 * p[0] + 2 * p[1] + p[2]


def _other_chips(x, y):
    return [(1 - x, y), (x, 1 - y), (1 - x, 1 - y)]


def _remote(src, dst, send_sems, recv_sems, k, to):
    return pltpu.make_async_remote_copy(src_ref=src, dst_ref=dst, send_sem=send_sems.at[k], recv_sem=recv_sems.at[k],
                                        device_id=to, device_id_type=MESH)


def _ag_first(ins, outs, send_sems, recv_sems, local_sems):
    x, y, c = _place()
    me = (x, y, c)
    targets = [(x, y, 1 - c)] + [(*chip, c) for chip in _other_chips(x, y)]
    local, out, inc = [], [], []
    for a in range(len(ins)):
        local.append(pltpu.make_async_copy(ins[a], outs[a].at[_dev(me)], local_sems.at[a]))
        for k, to in enumerate(targets):
            out.append(_remote(ins[a], outs[a].at[_dev(me)], send_sems, recv_sems, 4 * a + k, to))
            inc.append(_remote(ins[a], outs[a].at[_dev(to)], send_sems, recv_sems, 4 * a + k, to))
    return local, out, inc


def _ag_second(bufs, send_sems, recv_sems):
    x, y, c = _place()
    out, inc = [], []
    for a in range(len(bufs)):
        for j, chip in enumerate(_other_chips(x, y)):
            mine, theirs = bufs[a].at[_dev((*chip, c))], bufs[a].at[_dev((*chip, 1 - c))]
            out.append(_remote(mine, mine, send_sems, recv_sems, 3 * a + j, (x, y, 1 - c)))
            inc.append(_remote(theirs, theirs, send_sems, recv_sems, 3 * a + j, (x, y, 1 - c)))
    return [], out, inc


def _rs_first(ins, outs, send_sems, recv_sems):
    x, y, c = _place()
    out = [_remote(ins[a].at[1 - c], outs[a], send_sems, recv_sems, a, (x, y, 1 - c)) for a in range(len(ins))]
    return [], out, out


def _rs_second(ins, outs, send_sems, recv_sems, local_sems):
    x, y, c = _place()
    my_chip = 2 * x + y
    local, out, inc = [], [], []
    for a in range(len(ins)):
        local.append(pltpu.make_async_copy(ins[a].at[my_chip], outs[a].at[my_chip], local_sems.at[a]))
        for k, (ox, oy) in enumerate(_other_chips(x, y)):
            out.append(_remote(ins[a].at[2 * ox + oy], outs[a].at[my_chip], send_sems, recv_sems, 3 * a + k, (ox, oy, c)))
            inc.append(_remote(ins[a].at[2 * ox + oy], outs[a].at[2 * ox + oy], send_sems, recv_sems, 3 * a + k,
                               (ox, oy, c)))
    return local, out, inc


def _comm_start(exchange):
    local, out, _ = exchange
    for cp in local + out:
        cp.start()


def _comm_wait(exchange):
    local, out, inc = exchange
    for cp in inc:
        cp.wait_recv()
    for cp in out:
        cp.wait_send()
    for cp in local:
        cp.wait()


def _dma_sems(*counts):
    return [pltpu.SemaphoreType.DMA((n,)) for n in counts]


def _rs_sibling(grads):
    n = len(grads)

    def body(*refs):
        ex = _rs_first(refs[:n], refs[n:2 * n], *refs[2 * n:])
        _comm_start(ex)
        _comm_wait(ex)

    return pl.pallas_call(
        body, name="grads_to_sibling",
        in_specs=[ANY] * n, out_specs=[ANY] * n,
        out_shape=[jax.ShapeDtypeStruct(g.shape[1:], g.dtype) for g in grads],
        scratch_shapes=_dma_sems(n, n),
        compiler_params=pltpu.CompilerParams(has_side_effects=True),
    )(*grads)


def _pair_sum(grads, recvs):
    n = len(grads)

    def body(c_ref, *refs):
        for a in range(n):
            refs[2 * n + a][...] = (refs[a][...].astype(F32) + refs[n + a][...].astype(F32)).astype(BF16)

    def g_spec(g):
        return pl.BlockSpec((None, None) + g.shape[2:], lambda j, c_ref: (c_ref[0], j, 0, 0))

    def r_spec(r):
        return pl.BlockSpec((None,) + r.shape[1:], lambda j, c_ref: (j, 0, 0))

    return pl.pallas_call(
        body, name="pair_sum",
        grid_spec=pltpu.PrefetchScalarGridSpec(
            num_scalar_prefetch=1, grid=(N_CHIP,),
            in_specs=[g_spec(g) for g in grads] + [r_spec(r) for r in recvs],
            out_specs=[r_spec(r) for r in recvs]),
        out_shape=[jax.ShapeDtypeStruct(r.shape, BF16) for r in recvs],
        compiler_params=_params(("parallel",)),
    )(lax.axis_index("c").reshape(1).astype(jnp.int32), *grads, *recvs)


def _rs_chips(parts):
    n = len(parts)

    def body(*refs):
        ex = _rs_second(refs[:n], refs[n:2 * n], *refs[2 * n:])
        _comm_start(ex)
        _comm_wait(ex)

    return pl.pallas_call(
        body, name="grads_to_chips",
        in_specs=[ANY] * n, out_specs=[ANY] * n,
        out_shape=[jax.ShapeDtypeStruct(p.shape, p.dtype) for p in parts],
        scratch_shapes=_dma_sems(3 * n, 3 * n, n),
        compiler_params=pltpu.CompilerParams(has_side_effects=True),
    )(*parts)


SMALL_ROWS = 544


def _all_reduce_small(buf):
    def body(in_ref, out_ref, recv, acc, send1, recv1, send2, recv2):
        x, y, c = _place()
        me = 4 * x + 2 * y + c
        peers = [(x ^ (r >> 2), y ^ ((r >> 1) & 1), c ^ (r & 1)) for r in range(1, N_DEV)]

        def idx(p):
            return 4 * p[0] + 2 * p[1] + p[2]

        first = [pltpu.make_async_remote_copy(
            src_ref=in_ref.at[idx(p)], dst_ref=recv.at[me], send_sem=send1.at[r], recv_sem=recv1.at[r],
            device_id=p, device_id_type=MESH) for r, p in enumerate(peers)]
        for cp in first:
            cp.start()
        recv[me] = in_ref[me]
        for r, p in enumerate(peers):
            pltpu.make_async_remote_copy(
                src_ref=in_ref.at[idx(p)], dst_ref=recv.at[idx(p)], send_sem=send1.at[r], recv_sem=recv1.at[r],
                device_id=p, device_id_type=MESH).wait_recv()
        total = recv[0]
        for k in range(1, N_DEV):
            total = total + recv[k]
        acc[...] = total
        out_ref[me] = total
        second = [pltpu.make_async_remote_copy(
            src_ref=acc, dst_ref=out_ref.at[me], send_sem=send2.at[r], recv_sem=recv2.at[r],
            device_id=p, device_id_type=MESH) for r, p in enumerate(peers)]
        for cp in second:
            cp.start()
        for r, p in enumerate(peers):
            pltpu.make_async_remote_copy(
                src_ref=acc, dst_ref=out_ref.at[idx(p)], send_sem=send2.at[r], recv_sem=recv2.at[r],
                device_id=p, device_id_type=MESH).wait_recv()
        for cp in first + second:
            cp.wait_send()

    vm = pl.BlockSpec(memory_space=pltpu.VMEM)
    return pl.pallas_call(
        body, name="small_grads_all_reduce",
        in_specs=[vm], out_specs=vm,
        out_shape=jax.ShapeDtypeStruct(buf.shape, F32),
        scratch_shapes=[pltpu.VMEM(buf.shape, F32), pltpu.VMEM(buf.shape[1:], F32),
                        pltpu.SemaphoreType.DMA((7,)), pltpu.SemaphoreType.DMA((7,)),
                        pltpu.SemaphoreType.DMA((7,)), pltpu.SemaphoreType.DMA((7,))],
        compiler_params=pltpu.CompilerParams(has_side_effects=True, vmem_limit_bytes=VMEM_LIMIT),
    )(buf)


def _dilate(a, d):
    if d == 1:
        return a
    S, C = a.shape
    return a.reshape(S // d, d, C).transpose(1, 0, 2).reshape(S, C)


def _undilate(a, d):
    if d == 1:
        return a
    S, C = a.shape
    return a.reshape(d, S // d, C).transpose(1, 0, 2).reshape(S, C)


def _cols(a, cb, n=1):
    return a[:, cb * WIDTH:(cb + n) * WIDTH]


def _to_blocks(g, kind):
    if kind == "cols":
        R = g.shape[0]
        return g.reshape(R, N_CHIP, 2, -1).transpose(2, 1, 0, 3)
    if kind == "rows":
        C = g.shape[1]
        return g.reshape(N_CHIP, 2, -1, C).transpose(1, 0, 2, 3)
    return g.reshape(4 * WIDTH, N_CHIP, 2, -1).transpose(2, 1, 0, 3)


SMALL = ("norm_g", "gm_ln_g", "gm_ln_b", "gm_ws", "gm_bs", "pool_w", "pool_scale", "mem_norm_g", "final_norm_g")


def _pack_small(tree):
    flat = jnp.concatenate([tree[k].reshape(-1, 128) for k in SMALL], axis=0)
    return jnp.pad(flat, ((0, N_DEV * SMALL_ROWS - flat.shape[0]), (0, 0)))


def _unpack_small(flat, like):
    out, at = {}, 0
    for k in SMALL:
        rows = like[k].size // 128
        out[k] = flat[at:at + rows].reshape(like[k].shape)
        at += rows
    return out


def _make_layer(wbr, wg, wkv, wb, wout, norm_g, mem_norm_g, ln_g, ln_b, gm_ws, gm_bs, pool_w, pool_scale):
    tril = jnp.tril(jnp.ones((CHUNK, CHUNK), bool))
    wsm = jnp.where(tril, gm_ws, 0.0).astype(BF16)
    pw = pool_w.astype(BF16)
    return dict(wbr=wbr, wg=wg, wkv=wkv, wb=wb, wout=wout, g=norm_g[None], mg=mem_norm_g[None], ln_g=ln_g[None],
                ln_b=ln_b[None], wsm=wsm, wsm_t=wsm.transpose(0, 2, 1), pw=pw, pw_t=pw.transpose(0, 2, 1),
                ps=pool_scale[None], bias=jnp.repeat(gm_bs.T, HEAD, axis=1))


def _layer_fwd(xl, mem0, L, next_shards=()):
    S = xl.shape[0]
    proj, gates, h, half_gathered = _in_proj(xl, L["g"], L["wbr"], L["wg"], next_shards[:1])
    kv, mem_n = _mem_kv(mem0, L["mg"], L["wkv"])
    y4, gathered = _abm_fwd(proj, L["ln_g"], L["ln_b"], L["wsm"], L["bias"], L["pw"], L["ps"], kv, half_gathered)
    o_g, l_g = [], []
    for gi, d in enumerate(DILATIONS):
        if d == 1:
            o, lse = _attn_fwd(proj, CB_Q0, proj, CB_K, proj, CB_CV, S // CHUNK)
        else:
            o, lse = _attn_fwd_dilated(proj, CB_Q0 + gi, CB_K, CB_CV, d)
        o_g.append(o)
        l_g.append(lse)
    (xn, y4, oc, lse, z), rest = _merge_fwd(xl, y4, o_g, l_g, proj, gates, L["wb"], L["wout"], next_shards[1:])
    saved = dict(x=xl, proj=proj, gates=gates, h=h, kv=kv, mem_n=mem_n, y4=y4, oc=oc, lse=lse, z=z)
    return xn, saved, gathered + rest


def _place_cols(dst, piece, cb):
    return lax.dynamic_update_slice(dst, piece, (0, cb * WIDTH))


def _layer_bwd(dx, mem0, L, sv, later=()):
    S = dx.shape[0]
    proj = sv["proj"]
    (dy3, doc, delta, dcg, dgm, dt), from_sibling = _merge_bwd(dx, sv["y4"], sv["oc"], proj, sv["gates"], L["wb"],
                                                              L["wout"], later)
    pair = _pair_sum(later, from_sibling) if later else ()
    dwout = _matmul_tn(sv["z"], dx, D_MODEL, "dw_out")
    dwb = _dw_branch(sv["y4"], dt)
    dpb, dm, dlng, dlnb, dws, dbias, dpw, dps, dkv = _abm_bwd(
        proj, dy3, L["ln_g"], L["ln_b"], L["wsm"], L["wsm_t"], L["bias"], L["pw"], L["pw_t"], L["ps"], sv["kv"])
    dk, dv = None, None
    for gi, d in enumerate(DILATIONS):
        if d == 1:
            r = _attn_bwd(proj, CB_Q0, proj, CB_K, proj, CB_CV, doc, sv["lse"], delta, S // CHUNK)
        else:
            r = _attn_bwd_dilated(proj, CB_Q0 + gi, CB_K, CB_CV, doc, sv["lse"], delta, d)
        dpb = _place_cols(dpb, r[0], CB_Q0 + gi)
        dkg, dvg = r[1].astype(F32), r[2].astype(F32)
        dk = dkg if dk is None else dk + dkg
        dv = dvg if dv is None else dv + dvg
    dpb = _place_cols(dpb, dk.astype(BF16), CB_K)
    dpb = _place_cols(dpb, dv.astype(BF16), CB_CV)
    dpb = _place_cols(dpb, dcg, CB_CGATE)
    dpb = _place_cols(dpb, dm, CB_MQ)
    dwkv, dmg = _mem_bwd(mem0, L["mg"], sv["mem_n"], L["wkv"], dkv)
    if later:
        dwin_b, parts_rest = _matmul_tn(sv["h"], dpb, D_BRANCHES // 4, "dw_in_branches_scatter", pair[1:])
    else:
        dwin_b, parts_rest = _matmul_tn(sv["h"], dpb, D_BRANCHES // 4, "dw_in_branches"), []
    dwin = jnp.concatenate([dwin_b, _matmul_tn(sv["h"], dgm, D_GATES // 4, "dw_in_gates")], axis=1)
    dxi, dng, parts = _dh_bwd(dpb, L["wbr"], dgm, L["wg"], sv["x"], L["g"], dx, pair[:1])
    parts = parts + parts_rest
    big = dict(w_in=dwin, w_mem_kv=dwkv, w_branch=dwb, w_out=dwout)
    small = dict(norm_g=dng[0], gm_ln_g=dlng[0], gm_ln_b=dlnb[0], gm_ws=dws,
                 gm_bs=_bias_reduce(dbias)[:, :N_HEAD].T, pool_w=dpw, pool_scale=dps[0], mem_norm_g=dmg[0])
    return dxi, big, small, parts


BIG = ("w_in", "w_mem_kv", "w_branch", "w_out")


def _blocked(big):
    return [_to_blocks(big["w_in"], "cols"), _to_blocks(big["w_mem_kv"], "rows"),
            _to_blocks(big["w_branch"], "branch"), _to_blocks(big["w_out"], "rows")]


def _full_weights(gathered):
    win, wkv, wb, wout = gathered
    cut = D_BRANCHES - 4 * win.shape[2]
    wbr = jnp.concatenate([win[0], win[1], win[2], win[3], win[4][:, :cut]], axis=1)
    wg = jnp.concatenate([win[4][:, cut:], win[5], win[6], win[7]], axis=1)
    return (wbr, wg, wkv.reshape(D_MODEL, 2 * WIDTH),
            wb.reshape(N_DEV, 4, WIDTH, -1).transpose(1, 2, 0, 3).reshape(4, WIDTH, D_MODEL),
            wout.reshape(D_MODEL, D_MODEL))


def kernel(x, mem, norm_g, w_in, gm_ln_g, gm_ln_b, gm_ws, gm_bs, pool_w, pool_scale, mem_norm_g, w_mem_kv, w_branch, w_out, final_norm_g, loss_target, m_norm_g, m_w_in, m_gm_ln_g, m_gm_ln_b, m_gm_ws, m_gm_bs, m_pool_w, m_pool_scale, m_mem_norm_g, m_w_mem_kv, m_w_branch, m_w_out, m_final_norm_g, v_norm_g, v_w_in, v_gm_ln_g, v_gm_ln_b, v_gm_ws, v_gm_bs, v_pool_w, v_pool_scale, v_mem_norm_g, v_w_mem_kv, v_w_branch, v_w_out, v_final_norm_g):
    x0 = x[0]
    mem0 = mem[0]
    tgt = loss_target[0]
    S = x0.shape[0]

    shards = [[w_in[l].astype(BF16), w_mem_kv[l].astype(BF16), w_branch[l].astype(BF16).reshape(4 * WIDTH, -1),
               w_out[l].astype(BF16)] for l in range(DEPTH)]
    gathered = _all_gather(shards[0])
    layers, saved = [], []
    xl = x0
    for l in range(DEPTH):
        layers.append(_make_layer(*_full_weights(gathered), norm_g[l], mem_norm_g[l], gm_ln_g[l], gm_ln_b[l],
                                  gm_ws[l], gm_bs[l], pool_w[l], pool_scale[l]))
        xl, sv, gathered = _layer_fwd(xl, mem0, layers[l], shards[l + 1] if l + 1 < DEPTH else ())
        saved.append(sv)

    loss_part, dx, d_final = _loss_head(xl, final_norm_g[None], tgt)
    loss = lax.psum(loss_part[0, 0], ("x", "y", "c"))

    small = {k: [None] * DEPTH for k in SMALL if k != "final_norm_g"}
    parts = [None] * DEPTH
    later = ()
    for l in reversed(range(DEPTH)):
        dx, gb, gs, done = _layer_bwd(dx, mem0, layers[l], saved[l], later)
        if later:
            parts[l + 1] = done
        later = _blocked(gb)
        for k in gs:
            small[k][l] = gs[k]
    grad_x = dx[None]
    parts[0] = _rs_chips(_pair_sum(later, _rs_sibling(later)))

    small_tree = {k: jnp.stack(small[k]) for k in small}
    small_tree["final_norm_g"] = d_final[0]
    reduced = _all_reduce_small(_pack_small(small_tree).reshape(N_DEV, SMALL_ROWS, 128))

    weights = dict(norm_g=norm_g, w_in=w_in, gm_ln_g=gm_ln_g, gm_ln_b=gm_ln_b, gm_ws=gm_ws, gm_bs=gm_bs,
                   pool_w=pool_w, pool_scale=pool_scale, mem_norm_g=mem_norm_g, w_mem_kv=w_mem_kv,
                   w_branch=w_branch, w_out=w_out, final_norm_g=final_norm_g)
    m_in = dict(norm_g=m_norm_g, w_in=m_w_in, gm_ln_g=m_gm_ln_g, gm_ln_b=m_gm_ln_b, gm_ws=m_gm_ws, gm_bs=m_gm_bs,
                pool_w=m_pool_w, pool_scale=m_pool_scale, mem_norm_g=m_mem_norm_g, w_mem_kv=m_w_mem_kv,
                w_branch=m_w_branch, w_out=m_w_out, final_norm_g=m_final_norm_g)
    v_in = dict(norm_g=v_norm_g, w_in=v_w_in, gm_ln_g=v_gm_ln_g, gm_ln_b=v_gm_ln_b, gm_ws=v_gm_ws, gm_bs=v_gm_bs,
                pool_w=v_pool_w, pool_scale=v_pool_scale, mem_norm_g=v_mem_norm_g, w_mem_kv=v_w_mem_kv,
                w_branch=v_w_branch, w_out=v_w_out, final_norm_g=v_final_norm_g)
    res = {}
    for a, k in enumerate(BIG):
        shape = weights[k].shape
        by_layer = [parts[l][a] for l in range(DEPTH)]
        lrc = (DEPTH,) + by_layer[0].shape[1:]
        outs = _adamw_layers(by_layer, weights[k].reshape(lrc), m_in[k].reshape(lrc), v_in[k].reshape(lrc),
                             "adamw_" + k)
        res[k] = [o.reshape(shape) for o in outs]
    outs = _adamw(reduced.reshape(1, N_DEV * SMALL_ROWS, 128), _pack_small(weights), _pack_small(m_in),
                  _pack_small(v_in), "adamw_small")
    unpacked = [_unpack_small(o, weights) for o in outs]
    for k in SMALL:
        res[k] = [u[k] for u in unpacked]

    order = ("norm_g", "w_in", "gm_ln_g", "gm_ln_b", "gm_ws", "gm_bs", "pool_w", "pool_scale", "mem_norm_g",
             "w_mem_kv", "w_branch", "w_out", "final_norm_g")
    return (loss, grad_x, *[res[k][0] for k in order], *[res[k][1] for k in order],
            *[res[k][2] for k in order], *[res[k][3] for k in order])
```

```python
import functools
import math

import jax
import jax.numpy as jnp
from jax import lax
from jax.experimental import pallas as pl
from jax.experimental.pallas import tpu as pltpu

F32 = jnp.float32
BF16 = jnp.bfloat16

D_MODEL = 1024
DEPTH = 4
WIDTH = 512
D_IN = 10752
HEAD = 128
N_HEAD = 4
CHUNK = 128
MEM_LEN = 256
POOL_WINDOWS = (2, 4, 8, 16)
DILATIONS = (1, 4, 16)
EPS = 1e-6
NEG = -1e30
ATT_SCALE = HEAD ** -0.5
N_DEV = 8
N_CHIP = 4

D_BRANCHES = 6656
D_GATES = D_IN - D_BRANCHES
CB_U, CB_V, CB_AGATE, CB_PIN, CB_PGATE = 0, 1, 2, 3, 4
CB_Q0, CB_K, CB_CV, CB_CGATE, CB_MQ, CB_MGATE = 5, 8, 9, 10, 11, 12

ADAM_LR = 0.001
ADAM_B1 = 0.9
ADAM_B2 = 0.999
ADAM_EPS = 1e-08
ADAM_WD = 0.01
ADAM_STEP = 10

VMEM_LIMIT = 56 * 1024 * 1024
MESH = pl.DeviceIdType.MESH
ANY = pl.BlockSpec(memory_space=pl.ANY)

NT = (((1,), (1,)), ((), ()))
TN = (((0,), (0,)), ((), ()))


def _dot(a, b):
    return jnp.dot(a, b, preferred_element_type=F32)


def _dot_nt(a, b):
    return lax.dot_general(a, b, NT, preferred_element_type=F32)


def _dot_tn(a, b):
    return lax.dot_general(a, b, TN, preferred_element_type=F32)


def _sigmoid(x):
    return 1.0 / (1.0 + jnp.exp(-x))


def _silu(x):
    return x * _sigmoid(x)


def _dsilu(x):
    s = _sigmoid(x)
    return s * (1.0 + x * (1.0 - s))


def _gelu(x):
    return 0.5 * x * (1.0 + lax.erf(x * (2.0 ** -0.5)))


def _dgelu(x):
    return 0.5 * (1.0 + lax.erf(x * (2.0 ** -0.5))) + x * jnp.exp(-0.5 * x * x) * (1.0 / math.sqrt(2.0 * math.pi))


def _col(blk, h):
    lane = lax.broadcasted_iota(jnp.int32, blk.shape, 1)
    return jnp.sum(jnp.where(lane == h, blk, 0.0), axis=1, keepdims=True)


def _put_cols(cols):
    rows = cols[0].shape[0]
    lane = lax.broadcasted_iota(jnp.int32, (rows, 128), 1)
    out = jnp.zeros((rows, 128), F32)
    for h, cv in enumerate(cols):
        out = jnp.where(lane == h, cv, out)
    return out


def _params(sem, vmem=VMEM_LIMIT):
    return pltpu.CompilerParams(dimension_semantics=sem, vmem_limit_bytes=vmem)


def _full(shape):
    nd = len(shape)
    return pl.BlockSpec(shape, lambda *_: (0,) * nd)


def _rows(tm, width, cb=0):
    return pl.BlockSpec((tm, width), lambda i: (i, cb))


def _in_proj(x, g, wbr, wg, shards=()):
    S = x.shape[0]
    tm, tnb, tng = 1024, D_BRANCHES // 4, D_GATES // 4
    njb = 4
    n = len(shards)
    ni, nj = S // tm, 2 * njb

    def body(x_ref, g_ref, wbr_ref, wg_ref, *rest):
        ins, (proj_ref, gates_ref, h_ref), outs = rest[:n], rest[n:n + 3], rest[n + 3:2 * n + 3]
        hs, sems = rest[2 * n + 3], rest[2 * n + 4:]
        i, j = pl.program_id(0), pl.program_id(1)

        if n:
            @pl.when(jnp.logical_and(i == 0, j == 0))
            def _():
                _comm_start(_ag_first(ins, outs, *sems))

        @pl.when(j == 0)
        def _():
            xf = x_ref[...]
            r = lax.rsqrt(jnp.mean(xf * xf, axis=-1, keepdims=True) + EPS)
            h = (xf * r * g_ref[...]).astype(BF16)
            hs[...] = h
            h_ref[...] = h

        @pl.when(j < njb)
        def _():
            proj_ref[...] = _dot(hs[...], wbr_ref[...]).astype(BF16)

        @pl.when(j >= njb)
        def _():
            gates_ref[...] = _dot(hs[...], wg_ref[...]).astype(BF16)

        if n:
            @pl.when(jnp.logical_and(i == ni - 1, j == nj - 1))
            def _():
                _comm_wait(_ag_first(ins, outs, *sems))

    def first(j):
        return jnp.minimum(j, njb - 1)

    def second(j):
        return jnp.maximum(j - njb, 0)

    res = pl.pallas_call(
        body, name="in_proj_gather" if n else "in_proj",
        grid=(ni, nj),
        in_specs=[pl.BlockSpec((tm, D_MODEL), lambda i, j: (i, 0)),
                  pl.BlockSpec((1, D_MODEL), lambda i, j: (0, 0)),
                  pl.BlockSpec((D_MODEL, tnb), lambda i, j: (0, first(j))),
                  pl.BlockSpec((D_MODEL, tng), lambda i, j: (0, second(j)))] + [ANY] * n,
        out_specs=[pl.BlockSpec((tm, tnb), lambda i, j: (i, first(j))),
                   pl.BlockSpec((tm, tng), lambda i, j: (i, second(j))),
                   pl.BlockSpec((tm, D_MODEL), lambda i, j: (i, 0))] + [ANY] * n,
        out_shape=[jax.ShapeDtypeStruct((S, D_BRANCHES), BF16), jax.ShapeDtypeStruct((S, D_GATES), BF16),
                   jax.ShapeDtypeStruct((S, D_MODEL), BF16)]
                  + [jax.ShapeDtypeStruct((N_DEV,) + s.shape, s.dtype) for s in shards],
        scratch_shapes=[pltpu.VMEM((tm, D_MODEL), BF16)] + (_dma_sems(4 * n, 4 * n, n) if n else []),
        compiler_params=_params(("arbitrary", "arbitrary")),
    )(x, g, wbr, wg, *shards)
    return res[0], res[1], res[2], list(res[3:])


def _mem_kv(mem, g, w):
    M = mem.shape[0]

    def body(m_ref, g_ref, w_ref, kv_ref, mn_ref):
        xf = m_ref[...]
        r = lax.rsqrt(jnp.mean(xf * xf, axis=-1, keepdims=True) + EPS)
        mn = (xf * r * g_ref[...]).astype(BF16)
        mn_ref[...] = mn
        kv_ref[...] = _dot(mn, w_ref[...]).astype(BF16)

    return pl.pallas_call(
        body, name="mem_kv",
        out_shape=[jax.ShapeDtypeStruct((M, 2 * WIDTH), BF16), jax.ShapeDtypeStruct((M, D_MODEL), BF16)],
        compiler_params=pltpu.CompilerParams(vmem_limit_bytes=VMEM_LIMIT),
    )(mem, g, w)


def _band_masks(win):
    t = lax.broadcasted_iota(jnp.int32, (CHUNK, CHUNK), 0)
    s = lax.broadcasted_iota(jnp.int32, (CHUNK, CHUNK), 1)
    cur = jnp.logical_and(t - s >= 0, t - s < win)
    prev = s > t + CHUNK - win
    return cur.astype(BF16), prev.astype(BF16)


def _inv_count(first_row, win):
    t = first_row + lax.broadcasted_iota(jnp.int32, (CHUNK, 1), 0)
    return 1.0 / jnp.minimum(t + 1, win).astype(F32)


def _layer_norm_fwd(v):
    mu = jnp.mean(v, axis=-1, keepdims=True)
    vc = v - mu
    var = jnp.mean(vc * vc, axis=-1, keepdims=True)
    rstd = lax.rsqrt(var + EPS)
    return vc * rstd, rstd


def _mem_softmax(q, kmem):
    s = _dot_nt(q, kmem) * ATT_SCALE
    m = jnp.max(s, axis=-1, keepdims=True)
    e = jnp.exp(s - m)
    return e * (1.0 / jnp.sum(e, axis=-1, keepdims=True))


def _abm_fwd(proj, ln_g, ln_b, wsm, bias_full, pool_w, pool_scale, kv, gathered=()):
    S = proj.shape[0]
    tm = 512
    nchunk = tm // CHUNK
    n = len(gathered)
    nsteps = S // tm

    def body(u_ref, v_ref, ag_ref, p_ref, ph_ref, pg_ref, mq_ref, mg_ref, lng_ref, lnb_ref, wsm_ref, bias_ref,
             pw_ref, ps_ref, kv_ref, *rest):
        y_ref, bufs, mix, sems = rest[n], rest[n + 1:2 * n + 1], rest[2 * n + 1], rest[2 * n + 2:]
        i = pl.program_id(0)

        if n:
            @pl.when(i == 0)
            def _():
                _comm_start(_ag_second(bufs, *sems))

        u = _gelu(u_ref[...].astype(F32))
        v = _gelu(v_ref[...].astype(F32))
        vhat, _ = _layer_norm_fwd(v)
        vln = (vhat * lng_ref[...] + lnb_ref[...]).astype(BF16)
        for c in range(nchunk):
            for h in range(N_HEAD):
                rs, cs = slice(c * CHUNK, (c + 1) * CHUNK), slice(h * HEAD, (h + 1) * HEAD)
                mix[rs, cs] = _dot(wsm_ref[h], vln[rs, cs]) + bias_ref[:, cs]
        y_ref[0] = (u * mix[...] * _silu(ag_ref[...].astype(F32))).astype(BF16)
        halo_ok = (i > 0).astype(F32)
        for c in range(nchunk):
            rs = slice(c * CHUNK, (c + 1) * CHUNK)
            for g, win in enumerate(POOL_WINDOWS):
                cs = slice(g * HEAD, (g + 1) * HEAD)
                bcur, bprev = _band_masks(win)
                cur = p_ref[rs, cs]
                if c == 0:
                    prev = (ph_ref[:, cs].astype(F32) * halo_ok).astype(BF16)
                else:
                    prev = p_ref[(c - 1) * CHUNK:c * CHUNK, cs]
                sums = _dot(bcur, cur) + _dot(bprev, prev)
                dm = sums * _inv_count(i * tm + c * CHUNK, win) - cur.astype(F32)
                mix[rs, cs] = _dot(dm.astype(BF16), pw_ref[g])
        y_ref[1] = (mix[...] * ps_ref[...] * _silu(pg_ref[...].astype(F32))).astype(BF16)
        for h in range(N_HEAD):
            cs = slice(h * HEAD, (h + 1) * HEAD)
            p = _mem_softmax(mq_ref[:, cs], kv_ref[:, cs])
            mix[:, cs] = _dot(p.astype(BF16), kv_ref[:, WIDTH + h * HEAD:WIDTH + (h + 1) * HEAD])
        y_ref[2] = (mix[...] * _silu(mg_ref[...].astype(F32))).astype(BF16)

        if n:
            @pl.when(i == nsteps - 1)
            def _():
                _comm_wait(_ag_second(bufs, *sems))

    blk = tm // CHUNK
    res = pl.pallas_call(
        body, name="abm_fwd_gather" if n else "abm_fwd",
        grid=(nsteps,),
        in_specs=[_rows(tm, WIDTH, CB_U), _rows(tm, WIDTH, CB_V), _rows(tm, WIDTH, CB_AGATE),
                  _rows(tm, WIDTH, CB_PIN),
                  pl.BlockSpec((CHUNK, WIDTH), lambda i: (jnp.maximum(i * blk - 1, 0), CB_PIN)),
                  _rows(tm, WIDTH, CB_PGATE), _rows(tm, WIDTH, CB_MQ), _rows(tm, WIDTH, CB_MGATE),
                  _full((1, WIDTH)), _full((1, WIDTH)), _full((N_HEAD, CHUNK, CHUNK)), _full((CHUNK, WIDTH)),
                  _full((4, HEAD, HEAD)), _full((1, WIDTH)), _full((MEM_LEN, 2 * WIDTH))] + [ANY] * n,
        out_specs=[pl.BlockSpec((3, tm, WIDTH), lambda i: (0, i, 0))] + [ANY] * n,
        out_shape=[jax.ShapeDtypeStruct((4, S, WIDTH), BF16)]
                  + [jax.ShapeDtypeStruct(b.shape, b.dtype) for b in gathered],
        input_output_aliases={15 + a: 1 + a for a in range(n)},
        scratch_shapes=[pltpu.VMEM((tm, WIDTH), F32)] + (_dma_sems(3 * n, 3 * n) if n else []),
        compiler_params=_params(("arbitrary",)),
    )(proj, proj, proj, proj, proj, proj, proj, proj, ln_g, ln_b, wsm, bias_full, pool_w, pool_scale, kv, *gathered)
    return res[0], list(res[1:])


ATT_TILE = 512


def _attn_fwd(q, qcb, k, kcb, v, vcb, bps):
    S = q.shape[0]
    tm = ATT_TILE
    nb = tm // CHUNK

    nblocks = nb * N_HEAD

    def body(q_ref, k_ref, v_ref, kh_ref, vh_ref, o_ref, l_ref, sc_s, sp_s, pc_s, pp_s):
        i = pl.program_id(0)

        def prev_kv(n, cs):
            if n == 0:
                return kh_ref[:, cs], vh_ref[:, cs]
            ps = slice((n - 1) * CHUNK, n * CHUNK)
            return k_ref[ps, cs], v_ref[ps, cs]

        pens = []
        for n in range(nb):
            rs = slice(n * CHUNK, (n + 1) * CHUNK)
            pens.append(jnp.full((N_HEAD * CHUNK, 1), jnp.where((i * nb + n) % bps != 0, 0.0, NEG), F32))
            for h in range(N_HEAD):
                cs = slice(h * HEAD, (h + 1) * HEAD)
                bs = slice((n * N_HEAD + h) * CHUNK, (n * N_HEAD + h + 1) * CHUNK)
                qh = q_ref[rs, cs]
                sc_s[bs, :] = _dot_nt(qh, k_ref[rs, cs])
                sp_s[bs, :] = _dot_nt(qh, prev_kv(n, cs)[0])
        row = lax.broadcasted_iota(jnp.int32, (nblocks * CHUNK, CHUNK), 0) & (CHUNK - 1)
        col = lax.broadcasted_iota(jnp.int32, (nblocks * CHUNK, CHUNK), 1)
        sc = jnp.where(col <= row, sc_s[...] * ATT_SCALE, NEG)
        sp = jnp.where(col >= row, sp_s[...] * ATT_SCALE, NEG) + jnp.concatenate(pens, axis=0)
        m = jnp.maximum(jnp.max(sc, axis=-1, keepdims=True), jnp.max(sp, axis=-1, keepdims=True))
        ec = jnp.exp(sc - m)
        ep = jnp.exp(sp - m)
        den = jnp.sum(ec, axis=-1, keepdims=True) + jnp.sum(ep, axis=-1, keepdims=True)
        inv = 1.0 / den
        pc_s[...] = (ec * inv).astype(BF16)
        pp_s[...] = (ep * inv).astype(BF16)
        lse = m + jnp.log(den)
        for n in range(nb):
            rs = slice(n * CHUNK, (n + 1) * CHUNK)
            for h in range(N_HEAD):
                cs = slice(h * HEAD, (h + 1) * HEAD)
                bs = slice((n * N_HEAD + h) * CHUNK, (n * N_HEAD + h + 1) * CHUNK)
                o = _dot(pc_s[bs, :], v_ref[rs, cs]) + _dot(pp_s[bs, :], prev_kv(n, cs)[1])
                o_ref[rs, cs] = o.astype(BF16)
            l_ref[rs, :] = _put_cols([lse[(n * N_HEAD + h) * CHUNK:(n * N_HEAD + h + 1) * CHUNK]
                                      for h in range(N_HEAD)])

    def halo(cb):
        return pl.BlockSpec((CHUNK, WIDTH), lambda i: (jnp.maximum(i * nb - 1, 0), cb))

    return pl.pallas_call(
        body, name=f"attn_fwd_{bps}",
        grid=(S // tm,),
        in_specs=[_rows(tm, WIDTH, qcb), _rows(tm, WIDTH, kcb), _rows(tm, WIDTH, vcb), halo(kcb), halo(vcb)],
        out_specs=[_rows(tm, WIDTH), _rows(tm, 128)],
        out_shape=[jax.ShapeDtypeStruct((S, WIDTH), BF16), jax.ShapeDtypeStruct((S, 128), F32)],
        scratch_shapes=[pltpu.VMEM((nblocks * CHUNK, CHUNK), F32), pltpu.VMEM((nblocks * CHUNK, CHUNK), F32),
                        pltpu.VMEM((nblocks * CHUNK, CHUNK), BF16), pltpu.VMEM((nblocks * CHUNK, CHUNK), BF16)],
        compiler_params=_params(("parallel",)),
    )(q, k, v, k, v)


def _gate_specs(tm):
    return [pl.BlockSpec((tm, D_MODEL), lambda i, b=b: (i, b)) for b in range(4)]


Y_SLOT = (0, 1, 3, 2)


def _merge_fwd(x, y4, o_g, l_g, proj, gates, wb, wout, shards=()):
    S = x.shape[0]
    tm = 256
    n = len(shards)
    nsteps = S // tm
    forward_at = nsteps - 4

    def body(x_ref, y_ref, o0, o1, o2, l0, l1, l2, cg_ref, *rest):
        gm = rest[:4]
        wb_ref, wo_ref = rest[4:6]
        s_in = rest[6:6 + n]
        xn_ref, yc_ref, oc_ref, lse_ref, z_ref = rest[6 + n:11 + n]
        s_out, ocs, sems = rest[11 + n:11 + 2 * n], rest[11 + 2 * n], rest[12 + 2 * n:]
        i = pl.program_id(0)

        if n:
            @pl.when(i == 0)
            def _():
                _comm_start(_ag_first(s_in, s_out, *sems[:3]))

            @pl.when(i == forward_at)
            def _():
                incoming = _ag_first(s_in, s_out, *sems[:3])[2]
                for a in range(n):
                    for k in range(1, 4):
                        incoming[4 * a + k].wait_recv()
                _comm_start(_ag_second(s_out, *sems[3:]))

        lcols = []
        for h in range(N_HEAD):
            cs = slice(h * HEAD, (h + 1) * HEAD)
            ls = [_col(l[...], h) for l in (l0, l1, l2)]
            m = jnp.maximum(jnp.maximum(ls[0], ls[1]), ls[2])
            tot = jnp.exp(ls[0] - m) + jnp.exp(ls[1] - m) + jnp.exp(ls[2] - m)
            lse = m + jnp.log(tot)
            ocs[:, cs] = sum(jnp.exp(lg - lse) * o[:, cs].astype(F32) for lg, o in zip(ls, (o0, o1, o2)))
            lcols.append(lse)
        lse_ref[...] = _put_cols(lcols)
        oc = ocs[...]
        oc_ref[...] = oc.astype(BF16)
        yc = (oc * _silu(cg_ref[...].astype(F32))).astype(BF16)
        yc_ref[...] = yc
        ys = (y_ref[0], y_ref[1], yc, y_ref[2])
        z = jnp.zeros((tm, D_MODEL), F32)
        for b in range(4):
            z = z + _sigmoid(gm[b][...].astype(F32)) * _dot(ys[b], wb_ref[b])
        zb = z.astype(BF16)
        z_ref[...] = zb
        xn_ref[...] = x_ref[...] + _dot(zb, wo_ref[...])

        if n:
            @pl.when(i == nsteps - 1)
            def _():
                local, out, incoming = _ag_first(s_in, s_out, *sems[:3])
                for a in range(n):
                    incoming[4 * a].wait_recv()
                _comm_wait(_ag_second(s_out, *sems[3:]))
                for cp in out:
                    cp.wait_send()
                for cp in local:
                    cp.wait()

    res = pl.pallas_call(
        body, name="merge_fwd_gather" if n else "merge_fwd",
        grid=(nsteps,),
        in_specs=[_rows(tm, D_MODEL), pl.BlockSpec((3, tm, WIDTH), lambda i: (0, i, 0)),
                  _rows(tm, WIDTH), _rows(tm, WIDTH), _rows(tm, WIDTH),
                  _rows(tm, 128), _rows(tm, 128), _rows(tm, 128),
                  _rows(tm, WIDTH, CB_CGATE)] + _gate_specs(tm)
                 + [_full((4, WIDTH, D_MODEL)), _full((D_MODEL, D_MODEL))] + [ANY] * n,
        out_specs=[_rows(tm, D_MODEL), pl.BlockSpec((None, tm, WIDTH), lambda i: (Y_SLOT[2], i, 0)),
                   _rows(tm, WIDTH), _rows(tm, 128), _rows(tm, D_MODEL)] + [ANY] * n,
        out_shape=[jax.ShapeDtypeStruct((S, D_MODEL), F32), jax.ShapeDtypeStruct(y4.shape, BF16),
                   jax.ShapeDtypeStruct((S, WIDTH), BF16), jax.ShapeDtypeStruct((S, 128), F32),
                   jax.ShapeDtypeStruct((S, D_MODEL), BF16)]
                  + [jax.ShapeDtypeStruct((N_DEV,) + s.shape, s.dtype) for s in shards],
        input_output_aliases={1: 1},
        scratch_shapes=[pltpu.VMEM((tm, WIDTH), F32)] + (_dma_sems(4 * n, 4 * n, n, 3 * n, 3 * n) if n else []),
        compiler_params=_params(("arbitrary",)),
    )(x, y4, *o_g, *l_g, proj, *([gates] * 4), wb, wout, *shards)
    return res[:5], list(res[5:])


def _loss_head(x, g, tgt):
    S = x.shape[0]
    tm = 512

    def body(x_ref, g_ref, t_ref, loss_ref, dx_ref, dg_ref):
        @pl.when(pl.program_id(0) == 0)
        def _():
            loss_ref[...] = jnp.zeros_like(loss_ref)
            dg_ref[...] = jnp.zeros_like(dg_ref)

        xf = x_ref[...]
        r = lax.rsqrt(jnp.mean(xf * xf, axis=-1, keepdims=True) + EPS)
        xhat = xf * r
        gv = g_ref[...]
        err = xhat * gv - t_ref[...]
        e2 = jnp.sum(err * err, axis=-1, keepdims=True)
        loss_ref[...] += (0.5 / D_MODEL) * jnp.sum(e2, axis=0, keepdims=True)
        dy = err * (1.0 / D_MODEL)
        dg_ref[...] += jnp.sum(dy * xhat, axis=0, keepdims=True)
        dxh = dy * gv
        dx_ref[...] = r * (dxh - xhat * jnp.mean(dxh * xhat, axis=-1, keepdims=True))

    return pl.pallas_call(
        body, name="loss_head",
        grid=(S // tm,),
        in_specs=[_rows(tm, D_MODEL), _full((1, D_MODEL)), _rows(tm, D_MODEL)],
        out_specs=[_full((1, 128)), _rows(tm, D_MODEL), _full((1, D_MODEL))],
        out_shape=[jax.ShapeDtypeStruct((1, 128), F32), jax.ShapeDtypeStruct((S, D_MODEL), F32),
                   jax.ShapeDtypeStruct((1, D_MODEL), F32)],
        compiler_params=_params(("arbitrary",)),
    )(x, g, tgt)


def _merge_bwd(dxo, y4, oc, proj, gates, wb, wout, grads=()):
    S = dxo.shape[0]
    tm = 256
    n = len(grads)
    nsteps = S // tm

    def body(dx_ref, y_ref, oc_ref, cg_ref, *rest):
        gm = rest[:4]
        wb_ref, wo_ref = rest[4:6]
        g_in = rest[6:6 + n]
        dy_ref, doc_ref, delta_ref, dcg_ref, dgm_ref, dt_ref = rest[6 + n:12 + n]
        g_out, sems = rest[12 + n:12 + 2 * n], rest[12 + 2 * n:]
        i = pl.program_id(0)

        if n:
            @pl.when(i == 0)
            def _():
                _comm_start(_rs_first(g_in, g_out, *sems))

        dz = _dot_nt(dx_ref[...].astype(BF16), wo_ref[...])
        for b in range(4):
            gate = _sigmoid(gm[b][...].astype(F32))
            t = _dot(y_ref[Y_SLOT[b]], wb_ref[b])
            dgm_ref[:, b * D_MODEL:(b + 1) * D_MODEL] = (dz * t * gate * (1.0 - gate)).astype(BF16)
            dt = (dz * gate).astype(BF16)
            dt_ref[b] = dt
            dyb = _dot_nt(dt, wb_ref[b])
            if b == 2:
                cg = cg_ref[...].astype(F32)
                oc = oc_ref[...].astype(F32)
                doc = dyb * _silu(cg)
                dcg_ref[...] = (dyb * oc * _dsilu(cg)).astype(BF16)
                doc_ref[...] = doc.astype(BF16)
                prod = doc * oc
                delta_ref[...] = _put_cols([jnp.sum(prod[:, h * HEAD:(h + 1) * HEAD], axis=1, keepdims=True)
                                            for h in range(N_HEAD)])
            else:
                dy_ref[b if b < 2 else 2] = dyb.astype(BF16)

        if n:
            @pl.when(i == nsteps - 1)
            def _():
                _comm_wait(_rs_first(g_in, g_out, *sems))

    res = pl.pallas_call(
        body, name="merge_bwd_scatter" if n else "merge_bwd",
        grid=(nsteps,),
        in_specs=[_rows(tm, D_MODEL), pl.BlockSpec((4, tm, WIDTH), lambda i: (0, i, 0)),
                  _rows(tm, WIDTH), _rows(tm, WIDTH, CB_CGATE)] + _gate_specs(tm)
                 + [_full((4, WIDTH, D_MODEL)), _full((D_MODEL, D_MODEL))] + [ANY] * n,
        out_specs=[pl.BlockSpec((3, tm, WIDTH), lambda i: (0, i, 0)), _rows(tm, WIDTH), _rows(tm, 128),
                   _rows(tm, WIDTH), _rows(tm, 4 * D_MODEL), pl.BlockSpec((4, tm, D_MODEL), lambda i: (0, i, 0))]
                  + [ANY] * n,
        out_shape=[jax.ShapeDtypeStruct((3, S, WIDTH), BF16), jax.ShapeDtypeStruct((S, WIDTH), BF16),
                   jax.ShapeDtypeStruct((S, 128), F32), jax.ShapeDtypeStruct((S, WIDTH), BF16),
                   jax.ShapeDtypeStruct((S, 4 * D_MODEL), BF16), jax.ShapeDtypeStruct((4, S, D_MODEL), BF16)]
                  + [jax.ShapeDtypeStruct(g.shape[1:], g.dtype) for g in grads],
        scratch_shapes=_dma_sems(n, n) if n else [],
        compiler_params=_params(("arbitrary",)),
    )(dxo, y4, oc, proj, *([gates] * 4), wb, wout, *grads)
    return res[:6], list(res[6:])


def _dw_branch(y4, dt):
    S = y4.shape[1]
    tk = 1024
    nk = S // tk

    def body(y_ref, dt_ref, o_ref, acc):
        kk = pl.program_id(1)

        @pl.when(kk == 0)
        def _():
            acc[...] = jnp.zeros_like(acc)

        acc[...] += _dot_tn(y_ref[...], dt_ref[...])

        @pl.when(kk == nk - 1)
        def _():
            o_ref[...] = acc[...].astype(BF16)

    def slot(b):
        return jnp.where(b == 2, Y_SLOT[2], jnp.where(b == 3, Y_SLOT[3], b))

    return pl.pallas_call(
        body, name="dw_branch",
        grid=(4, nk),
        in_specs=[pl.BlockSpec((None, tk, WIDTH), lambda b, k: (slot(b), k, 0)),
                  pl.BlockSpec((None, tk, D_MODEL), lambda b, k: (b, k, 0))],
        out_specs=pl.BlockSpec((None, WIDTH, D_MODEL), lambda b, k: (b, 0, 0)),
        out_shape=jax.ShapeDtypeStruct((4, WIDTH, D_MODEL), BF16),
        scratch_shapes=[pltpu.VMEM((WIDTH, D_MODEL), F32)],
        compiler_params=_params(("parallel", "arbitrary")),
    )(y4, dt)


def _attn_bwd(q, qcb, k, kcb, v, vcb, do, lse, delta, bps):
    S = q.shape[0]
    tm = ATT_TILE
    nb = tm // CHUNK
    nblk = S // CHUNK

    ncur = nb * N_HEAD
    nprev = (nb + 1) * N_HEAD

    def body(q_ref, k_ref, v_ref, do_ref, l_ref, d_ref, kh_ref, vh_ref, qn_ref, don_ref, ln_ref, dn_ref,
             dq_ref, dk_ref, dv_ref, sc_s, sp_s, dpc_s, dpp_s, pc_s, pp_s, dsc_s, dsp_s):
        i = pl.program_id(0)

        def rows_of(n):
            if n < nb:
                rs = slice(n * CHUNK, (n + 1) * CHUNK)
                return rs, q_ref, do_ref, l_ref, d_ref
            return slice(0, CHUNK), qn_ref, don_ref, ln_ref, dn_ref

        def prev_kv(n, cs):
            if n == 0:
                return kh_ref[:, cs], vh_ref[:, cs]
            ps = slice((n - 1) * CHUNK, n * CHUNK)
            return k_ref[ps, cs], v_ref[ps, cs]

        def blk(n, h):
            return slice((n * N_HEAD + h) * CHUNK, (n * N_HEAD + h + 1) * CHUNK)

        pens, lses, deltas = [], [], []
        for n in range(nb + 1):
            rs, qr, dor, lr, dr = rows_of(n)
            gb = i * nb + n
            pen = jnp.where(gb % bps != 0, 0.0, NEG)
            if n == nb:
                pen = pen + jnp.where(gb < nblk, 0.0, NEG)
            pens.append(jnp.full((N_HEAD * CHUNK, 1), pen, F32))
            lblk, dblk = lr[rs, :], dr[rs, :]
            for h in range(N_HEAD):
                cs = slice(h * HEAD, (h + 1) * HEAD)
                qh, doh = qr[rs, cs], dor[rs, cs]
                lses.append(_col(lblk, h))
                deltas.append(_col(dblk, h))
                kp, vp = prev_kv(n, cs)
                sp_s[blk(n, h), :] = _dot_nt(qh, kp)
                dpp_s[blk(n, h), :] = _dot_nt(doh, vp)
                if n < nb:
                    sc_s[blk(n, h), :] = _dot_nt(qh, k_ref[rs, cs])
                    dpc_s[blk(n, h), :] = _dot_nt(doh, v_ref[rs, cs])
        lse = jnp.concatenate(lses, axis=0)
        delta = jnp.concatenate(deltas, axis=0)
        row = lax.broadcasted_iota(jnp.int32, (nprev * CHUNK, CHUNK), 0) & (CHUNK - 1)
        col = lax.broadcasted_iota(jnp.int32, (nprev * CHUNK, CHUNK), 1)
        sp = jnp.where(col >= row, sp_s[...] * ATT_SCALE, NEG) + jnp.concatenate(pens, axis=0)
        pp = jnp.exp(sp - lse)
        pp_s[...] = pp.astype(BF16)
        dsp_s[...] = (pp * (dpp_s[...] - delta)).astype(BF16)
        nc = ncur * CHUNK
        sc = jnp.where(col[:nc] <= row[:nc], sc_s[...] * ATT_SCALE, NEG)
        pc = jnp.exp(sc - lse[:nc])
        pc_s[...] = pc.astype(BF16)
        dsc_s[...] = (pc * (dpc_s[...] - delta[:nc])).astype(BF16)
        for n in range(nb):
            rs, qr, dor, _, _ = rows_of(n)
            rn, qnr, donr, _, _ = rows_of(n + 1)
            for h in range(N_HEAD):
                cs = slice(h * HEAD, (h + 1) * HEAD)
                kp, _ = prev_kv(n, cs)
                dq = _dot(dsc_s[blk(n, h), :], k_ref[rs, cs]) + _dot(dsp_s[blk(n, h), :], kp)
                dq_ref[rs, cs] = (dq * ATT_SCALE).astype(BF16)
                dk = _dot_tn(dsc_s[blk(n, h), :], qr[rs, cs]) + _dot_tn(dsp_s[blk(n + 1, h), :], qnr[rn, cs])
                dk_ref[rs, cs] = (dk * ATT_SCALE).astype(BF16)
                dv = _dot_tn(pc_s[blk(n, h), :], dor[rs, cs]) + _dot_tn(pp_s[blk(n + 1, h), :], donr[rn, cs])
                dv_ref[rs, cs] = dv.astype(BF16)

    def prev_halo(cb):
        return pl.BlockSpec((CHUNK, WIDTH), lambda i: (jnp.maximum(i * nb - 1, 0), cb))

    def next_halo(width, cb=0):
        return pl.BlockSpec((CHUNK, width), lambda i: (jnp.minimum(i * nb + nb, nblk - 1), cb))

    return pl.pallas_call(
        body, name=f"attn_bwd_{bps}",
        grid=(S // tm,),
        in_specs=[_rows(tm, WIDTH, qcb), _rows(tm, WIDTH, kcb), _rows(tm, WIDTH, vcb), _rows(tm, WIDTH),
                  _rows(tm, 128), _rows(tm, 128), prev_halo(kcb), prev_halo(vcb),
                  next_halo(WIDTH, qcb), next_halo(WIDTH), next_halo(128), next_halo(128)],
        out_specs=[_rows(tm, WIDTH), _rows(tm, WIDTH), _rows(tm, WIDTH)],
        out_shape=[jax.ShapeDtypeStruct((S, WIDTH), BF16)] * 3,
        scratch_shapes=[pltpu.VMEM((ncur * CHUNK, CHUNK), F32), pltpu.VMEM((nprev * CHUNK, CHUNK), F32),
                        pltpu.VMEM((ncur * CHUNK, CHUNK), F32), pltpu.VMEM((nprev * CHUNK, CHUNK), F32),
                        pltpu.VMEM((ncur * CHUNK, CHUNK), BF16), pltpu.VMEM((nprev * CHUNK, CHUNK), BF16),
                        pltpu.VMEM((ncur * CHUNK, CHUNK), BF16), pltpu.VMEM((nprev * CHUNK, CHUNK), BF16)],
        compiler_params=_params(("parallel",)),
    )(q, k, v, do, lse, delta, k, v, q, do, lse, delta)


def _dilated_split(d):
    hp = min(N_HEAD, 16 // d)
    return hp, N_HEAD // hp, HEAD * hp


def _by_class(src_ref, nat, dst, d, hp):
    for j in range(hp):
        nat[j] = src_ref[:, j * HEAD:(j + 1) * HEAD].astype(F32)
    for r in range(d):
        for j in range(hp):
            dst[j, r * CHUNK:(r + 1) * CHUNK, :] = nat.at[j][pl.ds(r, CHUNK, stride=d), :]


def _from_class(src, nat, dst_ref, d, hp):
    for r in range(d):
        for j in range(hp):
            nat.at[j][pl.ds(r, CHUNK, stride=d), :] = src[j, r * CHUNK:(r + 1) * CHUNK, :]
    for j in range(hp):
        dst_ref[:, j * HEAD:(j + 1) * HEAD] = nat[j].astype(BF16)


def _attn_fwd_dilated(proj, qcb, kcb, vcb, d):
    S = proj.shape[0]
    T = CHUNK * d
    hp, nh, cw = _dilated_split(d)
    nblocks = d * hp

    def body(q_ref, k_ref, v_ref, o_ref, l_ref, nat, qf, kf, vf, kpf, vpf, kst, vst, lf, sc_s, sp_s, pc_s, pp_s):
        i, hh = pl.program_id(0), pl.program_id(1)
        _by_class(q_ref, nat, qf, d, hp)
        _by_class(k_ref, nat, kf, d, hp)
        _by_class(v_ref, nat, vf, d, hp)

        @pl.when(i == 0)
        def _():
            kpf[...] = jnp.zeros_like(kpf)
            vpf[...] = jnp.zeros_like(vpf)

        @pl.when(i > 0)
        def _():
            kpf[...] = kst[hh]
            vpf[...] = vst[hh]

        def blk(ref, r, j):
            return ref[j, r * CHUNK:(r + 1) * CHUNK, :].astype(BF16)

        def bs(r, j):
            return slice((r * hp + j) * CHUNK, (r * hp + j + 1) * CHUNK)

        for r in range(d):
            for j in range(hp):
                qb = blk(qf, r, j)
                sc_s[bs(r, j), :] = _dot_nt(qb, blk(kf, r, j))
                sp_s[bs(r, j), :] = _dot_nt(qb, blk(kpf, r, j))
        row = lax.broadcasted_iota(jnp.int32, (nblocks * CHUNK, CHUNK), 0) & (CHUNK - 1)
        col = lax.broadcasted_iota(jnp.int32, (nblocks * CHUNK, CHUNK), 1)
        sc = jnp.where(col <= row, sc_s[...] * ATT_SCALE, NEG)
        sp = jnp.where(col >= row, sp_s[...] * ATT_SCALE, NEG) + jnp.where(i > 0, 0.0, NEG)
        m = jnp.maximum(jnp.max(sc, axis=-1, keepdims=True), jnp.max(sp, axis=-1, keepdims=True))
        ec = jnp.exp(sc - m)
        ep = jnp.exp(sp - m)
        den = jnp.sum(ec, axis=-1, keepdims=True) + jnp.sum(ep, axis=-1, keepdims=True)
        inv = 1.0 / den
        pc_s[...] = (ec * inv).astype(BF16)
        pp_s[...] = (ep * inv).astype(BF16)
        lse = m + jnp.log(den)
        lane = lax.broadcasted_iota(jnp.int32, (CHUNK, 128), 1)
        for r in range(d):
            lblk = jnp.zeros((CHUNK, 128), F32)
            for j in range(hp):
                o = _dot(pc_s[bs(r, j), :], blk(vf, r, j)) + _dot(pp_s[bs(r, j), :], blk(vpf, r, j))
                nat.at[j][pl.ds(r, CHUNK, stride=d), :] = o
                lblk = jnp.where(lane == hh * hp + j, lse[bs(r, j)], lblk)
            lf[pl.ds(r, CHUNK, stride=d), :] = lblk
        for j in range(hp):
            o_ref[:, j * HEAD:(j + 1) * HEAD] = nat[j].astype(BF16)

        @pl.when(hh == 0)
        def _():
            l_ref[...] = lf[...]

        @pl.when(hh > 0)
        def _():
            l_ref[...] += lf[...]

        kst[hh] = kf[...]
        vst[hh] = vf[...]

    def cols(cb):
        return pl.BlockSpec((T, cw), lambda i, hh: (i, cb * nh + hh))

    tile = pltpu.VMEM((hp, T, HEAD), F32)
    return pl.pallas_call(
        body, name=f"attn_fwd_dilated_{d}",
        grid=(S // T, nh),
        in_specs=[cols(qcb), cols(kcb), cols(vcb)],
        out_specs=[cols(0), pl.BlockSpec((T, 128), lambda i, hh: (i, 0))],
        out_shape=[jax.ShapeDtypeStruct((S, WIDTH), BF16), jax.ShapeDtypeStruct((S, 128), F32)],
        scratch_shapes=[tile, tile, tile, tile, tile, tile,
                        pltpu.VMEM((nh, hp, T, HEAD), F32), pltpu.VMEM((nh, hp, T, HEAD), F32),
                        pltpu.VMEM((T, 128), F32),
                        pltpu.VMEM((nblocks * CHUNK, CHUNK), F32), pltpu.VMEM((nblocks * CHUNK, CHUNK), F32),
                        pltpu.VMEM((nblocks * CHUNK, CHUNK), BF16), pltpu.VMEM((nblocks * CHUNK, CHUNK), BF16)],
        compiler_params=_params(("arbitrary", "arbitrary")),
    )(proj, proj, proj)


def _attn_bwd_dilated(proj, qcb, kcb, vcb, do, lse, delta, d):
    S = proj.shape[0]
    T = CHUNK * d
    nt = S // T
    hp, nh, cw = _dilated_split(d)
    nblocks = d * hp

    def body(q_ref, k_ref, v_ref, do_ref, l_ref, d_ref, dq_ref, dk_ref, dv_ref,
             nat, qf, dof, kf, vf, kpf, vpf, dqf, acck, accv, newk, newv,
             sc_s, sp_s, dpc_s, dpp_s, pc_s, pp_s, dsc_s, dsp_s):
        hh, i = pl.program_id(0), pl.program_id(1)

        @pl.when(i == 0)
        def _():
            for ref in (kpf, vpf, acck, accv):
                ref[...] = jnp.zeros_like(ref)
            dk_ref[...] = jnp.zeros_like(dk_ref)
            dv_ref[...] = jnp.zeros_like(dv_ref)

        def blk(ref, r, j):
            return ref[j, r * CHUNK:(r + 1) * CHUNK, :].astype(BF16)

        def bs(r, j):
            return slice((r * hp + j) * CHUNK, (r * hp + j + 1) * CHUNK)

        @pl.when(i < nt)
        def _():
            _by_class(q_ref, nat, qf, d, hp)
            _by_class(do_ref, nat, dof, d, hp)
            _by_class(k_ref, nat, kf, d, hp)
            _by_class(v_ref, nat, vf, d, hp)
            lses, deltas = [], []
            for r in range(d):
                lblk = l_ref[pl.ds(r, CHUNK, stride=d), :]
                dblk = d_ref[pl.ds(r, CHUNK, stride=d), :]
                for j in range(hp):
                    lses.append(_col(lblk, hh * hp + j))
                    deltas.append(_col(dblk, hh * hp + j))
                    qb, dob = blk(qf, r, j), blk(dof, r, j)
                    sc_s[bs(r, j), :] = _dot_nt(qb, blk(kf, r, j))
                    dpc_s[bs(r, j), :] = _dot_nt(dob, blk(vf, r, j))
                    sp_s[bs(r, j), :] = _dot_nt(qb, blk(kpf, r, j))
                    dpp_s[bs(r, j), :] = _dot_nt(dob, blk(vpf, r, j))
            lse = jnp.concatenate(lses, axis=0)
            delta = jnp.concatenate(deltas, axis=0)
            row = lax.broadcasted_iota(jnp.int32, (nblocks * CHUNK, CHUNK), 0) & (CHUNK - 1)
            col = lax.broadcasted_iota(jnp.int32, (nblocks * CHUNK, CHUNK), 1)
            sp = jnp.where(col >= row, sp_s[...] * ATT_SCALE, NEG) + jnp.where(i > 0, 0.0, NEG)
            pp = jnp.exp(sp - lse)
            pp_s[...] = pp.astype(BF16)
            dsp_s[...] = (pp * (dpp_s[...] - delta)).astype(BF16)
            sc = jnp.where(col <= row, sc_s[...] * ATT_SCALE, NEG)
            pc = jnp.exp(sc - lse)
            pc_s[...] = pc.astype(BF16)
            dsc_s[...] = (pc * (dpc_s[...] - delta)).astype(BF16)
            for r in range(d):
                rows = slice(r * CHUNK, (r + 1) * CHUNK)
                for j in range(hp):
                    qb, dob = blk(qf, r, j), blk(dof, r, j)
                    dsc, dsp = dsc_s[bs(r, j), :], dsp_s[bs(r, j), :]
                    dqf[j, rows, :] = (_dot(dsc, blk(kf, r, j)) + _dot(dsp, blk(kpf, r, j))) * ATT_SCALE
                    newk[j, rows, :] = _dot_tn(dsc, qb) * ATT_SCALE
                    newv[j, rows, :] = _dot_tn(pc_s[bs(r, j), :], dob)
                    acck[j, rows, :] += _dot_tn(dsp, qb) * ATT_SCALE
                    accv[j, rows, :] += _dot_tn(pp_s[bs(r, j), :], dob)
            _from_class(dqf, nat, dq_ref, d, hp)

        @pl.when(i > 0)
        def _():
            _from_class(acck, nat, dk_ref, d, hp)
            _from_class(accv, nat, dv_ref, d, hp)

        @pl.when(i < nt)
        def _():
            acck[...] = newk[...]
            accv[...] = newv[...]
            kpf[...] = kf[...]
            vpf[...] = vf[...]

    def cur(width, cb, nsplit):
        return pl.BlockSpec((T, width), lambda hh, i: (jnp.minimum(i, nt - 1), cb * nsplit + hh * (nsplit > 1)))

    def lag():
        return pl.BlockSpec((T, cw), lambda hh, i: (jnp.maximum(i - 1, 0), hh))

    tile = pltpu.VMEM((hp, T, HEAD), F32)
    f32s = pltpu.VMEM((nblocks * CHUNK, CHUNK), F32)
    b16s = pltpu.VMEM((nblocks * CHUNK, CHUNK), BF16)
    return pl.pallas_call(
        body, name=f"attn_bwd_dilated_{d}",
        grid=(nh, nt + 1),
        in_specs=[cur(cw, qcb, nh), cur(cw, kcb, nh), cur(cw, vcb, nh), cur(cw, 0, nh), cur(128, 0, 1), cur(128, 0, 1)],
        out_specs=[cur(cw, 0, nh), lag(), lag()],
        out_shape=[jax.ShapeDtypeStruct((S, WIDTH), BF16)] * 3,
        scratch_shapes=[tile] * 12 + [f32s] * 4 + [b16s] * 4,
        compiler_params=_params(("arbitrary", "arbitrary")),
    )(proj, proj, proj, do, lse, delta)


def _abm_bwd(proj, dy3, ln_g, ln_b, wsm, wsm_t, bias_full, pool_w, pool_wt, pool_scale, kv):
    S = proj.shape[0]
    tm = 512
    nchunk = tm // CHUNK
    nblk = S // CHUNK

    def body(u_ref, v_ref, ag_ref, p_ref, ph_ref, pg_ref, pgn_ref, mq_ref, mg_ref, dy_ref, dypn_ref,
             lng_ref, lnb_ref, wsm_ref, wsmt_ref, bias_ref, pw_ref, pwt_ref, ps_ref, kv_ref,
             dab_ref, dm_ref, dlng_ref, dlnb_ref, dws_ref, dbias_ref, dpw_ref, dps_ref, dkv_ref,
             mix, dvl, ddn):
        i = pl.program_id(0)

        @pl.when(i == 0)
        def _():
            for r in (dlng_ref, dlnb_ref, dws_ref, dbias_ref, dpw_ref, dps_ref, dkv_ref):
                r[...] = jnp.zeros_like(r)

        au = u_ref[...].astype(F32)
        av = v_ref[...].astype(F32)
        ag = ag_ref[...].astype(F32)
        u = _gelu(au)
        v = _gelu(av)
        vhat, rstd = _layer_norm_fwd(v)
        vln = (vhat * lng_ref[...] + lnb_ref[...]).astype(BF16)
        for c in range(nchunk):
            for h in range(N_HEAD):
                rs, cs = slice(c * CHUNK, (c + 1) * CHUNK), slice(h * HEAD, (h + 1) * HEAD)
                mix[rs, cs] = _dot(wsm_ref[h], vln[rs, cs]) + bias_ref[:, cs]
        dya = dy_ref[0].astype(F32)
        sg = _silu(ag)
        mixed = mix[...]
        dab_ref[:, 2 * WIDTH:3 * WIDTH] = (dya * u * mixed * _dsilu(ag)).astype(BF16)
        dab_ref[:, 0:WIDTH] = (dya * mixed * sg * _dgelu(au)).astype(BF16)
        dmixed = dya * u * sg
        dmb = dmixed.astype(BF16)
        tril = (lax.broadcasted_iota(jnp.int32, (CHUNK, CHUNK), 1)
                <= lax.broadcasted_iota(jnp.int32, (CHUNK, CHUNK), 0))
        for c in range(nchunk):
            rs = slice(c * CHUNK, (c + 1) * CHUNK)
            dbias_ref[...] += dmixed[rs, :]
            for h in range(N_HEAD):
                cs = slice(h * HEAD, (h + 1) * HEAD)
                dvl[rs, cs] = _dot(wsmt_ref[h], dmb[rs, cs])
                dws_ref[h] += jnp.where(tril, _dot_nt(dmb[rs, cs], vln[rs, cs]), 0.0)
        dvln = dvl[...]
        dlng_ref[...] += jnp.sum(dvln * vhat, axis=0, keepdims=True)
        dlnb_ref[...] += jnp.sum(dvln, axis=0, keepdims=True)
        dvh = dvln * lng_ref[...]
        dv = rstd * (dvh - jnp.mean(dvh, axis=-1, keepdims=True)
                     - vhat * jnp.mean(dvh * vhat, axis=-1, keepdims=True))
        dab_ref[:, WIDTH:2 * WIDTH] = (dv * _dgelu(av)).astype(BF16)

        halo_ok = (i > 0).astype(F32)
        for c in range(nchunk):
            rs = slice(c * CHUNK, (c + 1) * CHUNK)
            for g, win in enumerate(POOL_WINDOWS):
                cs = slice(g * HEAD, (g + 1) * HEAD)
                bcur, bprev = _band_masks(win)
                cur = p_ref[rs, cs]
                if c == 0:
                    prev = (ph_ref[:, cs].astype(F32) * halo_ok).astype(BF16)
                else:
                    prev = p_ref[(c - 1) * CHUNK:c * CHUNK, cs]
                sums = _dot(bcur, cur) + _dot(bprev, prev)
                dvl[rs, cs] = sums * _inv_count(i * tm + c * CHUNK, win) - cur.astype(F32)
        dmat = dvl[...].astype(BF16)
        for g in range(4):
            cs = slice(g * HEAD, (g + 1) * HEAD)
            mix[:, cs] = _dot(dmat[:, cs], pw_ref[g])
        yg = mix[...]
        pg = pg_ref[...].astype(F32)
        dyp = dy_ref[1].astype(F32)
        dyy = dyp * _silu(pg)
        scale = ps_ref[...]
        dab_ref[:, 4 * WIDTH:5 * WIDTH] = (dyp * yg * scale * _dsilu(pg)).astype(BF16)
        dps_ref[...] += jnp.sum(dyy * yg, axis=0, keepdims=True)
        dyg = (dyy * scale).astype(BF16)
        for g in range(4):
            cs = slice(g * HEAD, (g + 1) * HEAD)
            dpw_ref[g] += _dot_tn(dmat[:, cs], dyg[:, cs])
            mix[:, cs] = _dot(dyg[:, cs], pwt_ref[g])
        next_ok = (i + 1 < S // tm).astype(F32)
        dygn = (dypn_ref[...].astype(F32) * _silu(pgn_ref[...].astype(F32)) * scale * next_ok).astype(BF16)
        for c in range(nchunk + 1):
            for g, win in enumerate(POOL_WINDOWS):
                cs = slice(g * HEAD, (g + 1) * HEAD)
                if c < nchunk:
                    dd = mix[c * CHUNK:(c + 1) * CHUNK, cs]
                else:
                    dd = _dot(dygn[:, cs], pwt_ref[g])
                ddn[c * CHUNK:(c + 1) * CHUNK, cs] = dd * _inv_count(i * tm + c * CHUNK, win)
        ddnb = ddn[...].astype(BF16)
        for c in range(nchunk):
            rs = slice(c * CHUNK, (c + 1) * CHUNK)
            ns = slice((c + 1) * CHUNK, (c + 2) * CHUNK)
            for g, win in enumerate(POOL_WINDOWS):
                cs = slice(g * HEAD, (g + 1) * HEAD)
                bcur, bprev = _band_masks(win)
                dp = _dot_tn(bcur, ddnb[rs, cs]) + _dot_tn(bprev, ddnb[ns, cs]) - mix[rs, cs]
                dab_ref[rs, 3 * WIDTH + g * HEAD:3 * WIDTH + (g + 1) * HEAD] = dp.astype(BF16)

        mg = mg_ref[...].astype(F32)
        dym = dy_ref[2].astype(F32)
        dob = (dym * _silu(mg)).astype(BF16)
        for h in range(N_HEAD):
            cs = slice(h * HEAD, (h + 1) * HEAD)
            vs = slice(WIDTH + h * HEAD, WIDTH + (h + 1) * HEAD)
            qh = mq_ref[:, cs]
            p = _mem_softmax(qh, kv_ref[:, cs])
            pb = p.astype(BF16)
            mix[:, cs] = _dot(pb, kv_ref[:, vs])
            dp = _dot_nt(dob[:, cs], kv_ref[:, vs])
            ds = (p * (dp - jnp.sum(p * dp, axis=-1, keepdims=True))).astype(BF16)
            dm_ref[:, cs] = (_dot(ds, kv_ref[:, cs]) * ATT_SCALE).astype(BF16)
            dkv_ref[:, cs] += _dot_tn(ds, qh) * ATT_SCALE
            dkv_ref[:, vs] += _dot_tn(pb, dob[:, cs])
        dm_ref[:, WIDTH:2 * WIDTH] = (dym * mix[...] * _dsilu(mg)).astype(BF16)

    blk = tm // CHUNK
    small = [_full((1, WIDTH)), _full((1, WIDTH)), _full((N_HEAD, CHUNK, CHUNK)), _full((CHUNK, WIDTH)),
             _full((4, HEAD, HEAD)), _full((1, WIDTH)), _full((MEM_LEN, 2 * WIDTH))]
    return pl.pallas_call(
        body, name="abm_bwd",
        grid=(S // tm,),
        in_specs=[_rows(tm, WIDTH, CB_U), _rows(tm, WIDTH, CB_V), _rows(tm, WIDTH, CB_AGATE),
                  _rows(tm, WIDTH, CB_PIN),
                  pl.BlockSpec((CHUNK, WIDTH), lambda i: (jnp.maximum(i * blk - 1, 0), CB_PIN)),
                  _rows(tm, WIDTH, CB_PGATE),
                  pl.BlockSpec((CHUNK, WIDTH), lambda i: (jnp.minimum(i * blk + blk, nblk - 1), CB_PGATE)),
                  _rows(tm, WIDTH, CB_MQ), _rows(tm, WIDTH, CB_MGATE),
                  pl.BlockSpec((3, tm, WIDTH), lambda i: (0, i, 0)),
                  pl.BlockSpec((None, CHUNK, WIDTH), lambda i: (1, jnp.minimum(i * blk + blk, nblk - 1), 0)),
                  _full((1, WIDTH)), _full((1, WIDTH)), _full((N_HEAD, CHUNK, CHUNK)), _full((N_HEAD, CHUNK, CHUNK)),
                  _full((CHUNK, WIDTH)), _full((4, HEAD, HEAD)), _full((4, HEAD, HEAD)), _full((1, WIDTH)),
                  _full((MEM_LEN, 2 * WIDTH))],
        out_specs=[_rows(tm, 5 * WIDTH), _rows(tm, 2 * WIDTH)] + small,
        out_shape=[jax.ShapeDtypeStruct((S, D_BRANCHES), BF16), jax.ShapeDtypeStruct((S, 2 * WIDTH), BF16),
                   jax.ShapeDtypeStruct((1, WIDTH), F32), jax.ShapeDtypeStruct((1, WIDTH), F32),
                   jax.ShapeDtypeStruct((N_HEAD, CHUNK, CHUNK), F32), jax.ShapeDtypeStruct((CHUNK, WIDTH), F32),
                   jax.ShapeDtypeStruct((4, HEAD, HEAD), F32), jax.ShapeDtypeStruct((1, WIDTH), F32),
                   jax.ShapeDtypeStruct((MEM_LEN, 2 * WIDTH), F32)],
        scratch_shapes=[pltpu.VMEM((tm, WIDTH), F32), pltpu.VMEM((tm, WIDTH), F32),
                        pltpu.VMEM((tm + CHUNK, WIDTH), F32)],
        compiler_params=_params(("arbitrary",)),
    )(proj, proj, proj, proj, proj, proj, proj, proj, proj, dy3, dy3,
      ln_g, ln_b, wsm, wsm_t, bias_full, pool_w, pool_wt, pool_scale, kv)


def _bias_reduce(dbias_full):
    def body(d_ref, o_ref):
        d = d_ref[...]
        o_ref[...] = _put_cols([jnp.sum(d[:, h * HEAD:(h + 1) * HEAD], axis=1, keepdims=True) for h in range(N_HEAD)])

    return pl.pallas_call(body, name="bias_reduce", out_shape=jax.ShapeDtypeStruct((CHUNK, 128), F32))(dbias_full)


def _mem_bwd(mem, g, mem_n, w, dkv):
    def body(m_ref, g_ref, mn_ref, w_ref, dkv_ref, dw_ref, dg_ref):
        dkvb = dkv_ref[...].astype(BF16)
        dw_ref[...] = _dot_tn(mn_ref[...], dkvb).astype(BF16)
        dmn = _dot_nt(dkvb, w_ref[...])
        xf = m_ref[...]
        r = lax.rsqrt(jnp.mean(xf * xf, axis=-1, keepdims=True) + EPS)
        dg_ref[...] = jnp.sum(dmn * xf * r, axis=0, keepdims=True)

    return pl.pallas_call(
        body, name="mem_bwd",
        out_shape=[jax.ShapeDtypeStruct((D_MODEL, 2 * WIDTH), BF16), jax.ShapeDtypeStruct((1, D_MODEL), F32)],
        compiler_params=pltpu.CompilerParams(vmem_limit_bytes=VMEM_LIMIT),
    )(mem, g, mem_n, w, dkv)


def _dh_bwd(dpb, wbr, dpg, wg, x, g, dxo, parts=()):
    S = x.shape[0]
    tm = 1024
    tkb, tkg = D_BRANCHES // 4, D_GATES // 4
    nkb = 4
    nk = 8
    ni = S // tm
    n = len(parts)

    def body(dpb_ref, wbr_ref, dpg_ref, wg_ref, x_ref, g_ref, dxo_ref, *rest):
        p_in = rest[:n]
        dx_ref, dg_ref = rest[n:n + 2]
        p_out, acc, sems = rest[n + 2:2 * n + 2], rest[2 * n + 2], rest[2 * n + 3:]
        i, kk = pl.program_id(0), pl.program_id(1)

        @pl.when(jnp.logical_and(i == 0, kk == 0))
        def _():
            dg_ref[...] = jnp.zeros_like(dg_ref)
            if n:
                _comm_start(_rs_second(p_in, p_out, *sems))

        @pl.when(kk == 0)
        def _():
            acc[...] = jnp.zeros_like(acc)

        @pl.when(kk < nkb)
        def _():
            acc[...] += _dot_nt(dpb_ref[...], wbr_ref[...])

        @pl.when(kk >= nkb)
        def _():
            acc[...] += _dot_nt(dpg_ref[...], wg_ref[...])

        @pl.when(kk == nk - 1)
        def _():
            xf = x_ref[...]
            r = lax.rsqrt(jnp.mean(xf * xf, axis=-1, keepdims=True) + EPS)
            xhat = xf * r
            dh = acc[...]
            dg_ref[...] += jnp.sum(dh * xhat, axis=0, keepdims=True)
            dxh = dh * g_ref[...]
            dx_ref[...] = dxo_ref[...] + r * (dxh - xhat * jnp.mean(dxh * xhat, axis=-1, keepdims=True))

        if n:
            @pl.when(jnp.logical_and(i == ni - 1, kk == nk - 1))
            def _():
                _comm_wait(_rs_second(p_in, p_out, *sems))

    res = pl.pallas_call(
        body, name="dh_bwd_scatter" if n else "dh_bwd",
        grid=(ni, nk),
        in_specs=[pl.BlockSpec((tm, tkb), lambda i, k: (i, jnp.minimum(k, nkb - 1))),
                  pl.BlockSpec((D_MODEL, tkb), lambda i, k: (0, jnp.minimum(k, nkb - 1))),
                  pl.BlockSpec((tm, tkg), lambda i, k: (i, jnp.maximum(k - nkb, 0))),
                  pl.BlockSpec((D_MODEL, tkg), lambda i, k: (0, jnp.maximum(k - nkb, 0))),
                  pl.BlockSpec((tm, D_MODEL), lambda i, k: (i, 0)), pl.BlockSpec((1, D_MODEL), lambda i, k: (0, 0)),
                  pl.BlockSpec((tm, D_MODEL), lambda i, k: (i, 0))] + [ANY] * n,
        out_specs=[pl.BlockSpec((tm, D_MODEL), lambda i, k: (i, 0)), pl.BlockSpec((1, D_MODEL), lambda i, k: (0, 0))]
                  + [ANY] * n,
        out_shape=[jax.ShapeDtypeStruct((S, D_MODEL), F32), jax.ShapeDtypeStruct((1, D_MODEL), F32)]
                  + [jax.ShapeDtypeStruct(p.shape, p.dtype) for p in parts],
        scratch_shapes=[pltpu.VMEM((tm, D_MODEL), F32)] + (_dma_sems(3 * n, 3 * n, n) if n else []),
        compiler_params=_params(("arbitrary", "arbitrary")),
    )(dpb, wbr, dpg, wg, x, g, dxo, *parts)
    return res[0], res[1], list(res[2:])


def _matmul_tn(a, b, tn, name, parts=()):
    K, M = a.shape
    N = b.shape[1]
    tk = 1024
    nk = K // tk
    nj = N // tn
    n = len(parts)

    def body(a_ref, b_ref, *rest):
        p_in, o_ref, p_out = rest[:n], rest[n], rest[n + 1:2 * n + 1]
        acc, sems = rest[2 * n + 1], rest[2 * n + 2:]
        j, kk = pl.program_id(0), pl.program_id(1)

        if n:
            @pl.when(jnp.logical_and(j == 0, kk == 0))
            def _():
                _comm_start(_rs_second(p_in, p_out, *sems))

        @pl.when(kk == 0)
        def _():
            acc[...] = jnp.zeros_like(acc)

        acc[...] += _dot_tn(a_ref[...].astype(BF16), b_ref[...].astype(BF16))

        @pl.when(kk == nk - 1)
        def _():
            o_ref[...] = acc[...].astype(BF16)

        if n:
            @pl.when(jnp.logical_and(j == nj - 1, kk == nk - 1))
            def _():
                _comm_wait(_rs_second(p_in, p_out, *sems))

    res = pl.pallas_call(
        body, name=name,
        grid=(nj, nk),
        in_specs=[pl.BlockSpec((tk, M), lambda j, k: (k, 0)), pl.BlockSpec((tk, tn), lambda j, k: (k, j))] + [ANY] * n,
        out_specs=[pl.BlockSpec((M, tn), lambda j, k: (0, j))] + [ANY] * n,
        out_shape=[jax.ShapeDtypeStruct((M, N), BF16)] + [jax.ShapeDtypeStruct(p.shape, p.dtype) for p in parts],
        scratch_shapes=[pltpu.VMEM((M, tn), F32)] + (_dma_sems(3 * n, 3 * n, n) if n else []),
        compiler_params=_params(("arbitrary", "arbitrary")),
    )(a, b, *parts)
    return (res[0], list(res[1:])) if n else res[0]


def _row_tile(R, C):
    for cand in (512, 256, 128, 64, 32, 16, 8):
        if R % cand == 0 and cand * C * 4 <= (2 << 20):
            return cand
    return R


def _adamw_update(p_ref, w_ref, m_ref, v_ref, g_ref, d_ref, nm_ref, nv_ref):
    c1 = 1.0 / (1.0 - ADAM_B1 ** ADAM_STEP)
    c2 = 1.0 / (1.0 - ADAM_B2 ** ADAM_STEP)
    g = p_ref[0].astype(F32)
    for k in range(1, p_ref.shape[0]):
        g = g + p_ref[k].astype(F32)
    nm = ADAM_B1 * m_ref[...] + (1.0 - ADAM_B1) * g
    nv = ADAM_B2 * v_ref[...] + (1.0 - ADAM_B2) * (g * g)
    g_ref[...] = g
    nm_ref[...] = nm
    nv_ref[...] = nv
    d_ref[...] = -ADAM_LR * ((nm * c1) / (jnp.sqrt(nv * c2) + ADAM_EPS) + ADAM_WD * w_ref[...])


def _adamw(parts, w, m, v, name):
    P, R, C = parts.shape
    tr = _row_tile(R, C)

    def body(*refs):
        _adamw_update(*refs)

    spec = pl.BlockSpec((tr, C), lambda i: (i, 0))
    return pl.pallas_call(
        body, name=name,
        grid=(R // tr,),
        in_specs=[pl.BlockSpec((P, tr, C), lambda i: (0, i, 0)), spec, spec, spec],
        out_specs=[spec] * 4,
        out_shape=[jax.ShapeDtypeStruct((R, C), F32)] * 4,
        compiler_params=_params(("parallel",)),
    )(parts, w, m, v)


def _adamw_layers(parts, w, m, v, name):
    depth = len(parts)
    P, R, C = parts[0].shape
    tr = _row_tile(R, C)

    def body(*refs):
        layer = pl.program_id(0)
        for k in range(depth):
            @pl.when(layer == k)
            def _(k=k):
                _adamw_update(refs[k], *refs[depth:])

    def part_spec(k):
        return pl.BlockSpec((P, tr, C), lambda l, i: (0, jnp.where(l == k, i, 0), 0))

    spec = pl.BlockSpec((None, tr, C), lambda l, i: (l, i, 0))
    return pl.pallas_call(
        body, name=name,
        grid=(depth, R // tr),
        in_specs=[part_spec(k) for k in range(depth)] + [spec] * 3,
        out_specs=[spec] * 4,
        out_shape=[jax.ShapeDtypeStruct((depth, R, C), F32)] * 4,
        compiler_params=_params(("arbitrary", "arbitrary")),
    )(*parts, w, m, v)


def _place():
    return lax.axis_index("x"), lax.axis_index("y"), lax.axis_index("c")


def _all_gather(shards):
    n = len(shards)

    def body(*refs):
        ins, outs = refs[:n], refs[n:2 * n]
        send1, recv1, local_sems, send2, recv2 = refs[2 * n:]
        first = _ag_first(ins, outs, send1, recv1, local_sems)
        second = _ag_second(outs, send2, recv2)
        _comm_start(first)
        for j in range(3):
            for a in range(n):
                first[2][4 * a + 1 + j].wait_recv()
            for a in range(n):
                second[1][3 * a + j].start()
        for a in range(n):
            first[2][4 * a].wait_recv()
        for cp in second[2]:
            cp.wait_recv()
        for cp in first[1] + second[1]:
            cp.wait_send()
        for cp in first[0]:
            cp.wait()

    return pl.pallas_call(
        body, name="weights_all_gather",
        in_specs=[ANY] * n, out_specs=[ANY] * n,
        out_shape=[jax.ShapeDtypeStruct((N_DEV,) + s.shape, s.dtype) for s in shards],
        scratch_shapes=_dma_sems(4 * n, 4 * n, n, 3 * n, 3 * n),
        compiler_params=pltpu.CompilerParams(has_side_effects=True),
    )(*shards)


N_BIG = 4


def _dev(p):
    return 4 * p[0] + 2 * p[1] + p[2]


def _other_chips(x, y):
    return [(1 - x, y), (x, 1 - y), (1 - x, 1 - y)]


def _remote(src, dst, send_sems, recv_sems, k, to):
    return pltpu.make_async_remote_copy(src_ref=src, dst_ref=dst, send_sem=send_sems.at[k], recv_sem=recv_sems.at[k],
                                        device_id=to, device_id_type=MESH)


def _ag_first(ins, outs, send_sems, recv_sems, local_sems):
    x, y, c = _place()
    me = (x, y, c)
    targets = [(x, y, 1 - c)] + [(*chip, c) for chip in _other_chips(x, y)]
    local, out, inc = [], [], []
    for a in range(len(ins)):
        local.append(pltpu.make_async_copy(ins[a], outs[a].at[_dev(me)], local_sems.at[a]))
        for k, to in enumerate(targets):
            out.append(_remote(ins[a], outs[a].at[_dev(me)], send_sems, recv_sems, 4 * a + k, to))
            inc.append(_remote(ins[a], outs[a].at[_dev(to)], send_sems, recv_sems, 4 * a + k, to))
    return local, out, inc


def _ag_second(bufs, send_sems, recv_sems):
    x, y, c = _place()
    out, inc = [], []
    for a in range(len(bufs)):
        for j, chip in enumerate(_other_chips(x, y)):
            mine, theirs = bufs[a].at[_dev((*chip, c))], bufs[a].at[_dev((*chip, 1 - c))]
            out.append(_remote(mine, mine, send_sems, recv_sems, 3 * a + j, (x, y, 1 - c)))
            inc.append(_remote(theirs, theirs, send_sems, recv_sems, 3 * a + j, (x, y, 1 - c)))
    return [], out, inc


def _rs_first(ins, outs, send_sems, recv_sems):
    x, y, c = _place()
    out = [_remote(ins[a].at[1 - c], outs[a], send_sems, recv_sems, a, (x, y, 1 - c)) for a in range(len(ins))]
    return [], out, out


def _rs_second(ins, outs, send_sems, recv_sems, local_sems):
    x, y, c = _place()
    my_chip = 2 * x + y
    local, out, inc = [], [], []
    for a in range(len(ins)):
        local.append(pltpu.make_async_copy(ins[a].at[my_chip], outs[a].at[my_chip], local_sems.at[a]))
        for k, (ox, oy) in enumerate(_other_chips(x, y)):
            out.append(_remote(ins[a].at[2 * ox + oy], outs[a].at[my_chip], send_sems, recv_sems, 3 * a + k, (ox, oy, c)))
            inc.append(_remote(ins[a].at[2 * ox + oy], outs[a].at[2 * ox + oy], send_sems, recv_sems, 3 * a + k,
                               (ox, oy, c)))
    return local, out, inc


def _comm_start(exchange):
    local, out, _ = exchange
    for cp in local + out:
        cp.start()


def _comm_wait(exchange):
    local, out, inc = exchange
    for cp in inc:
        cp.wait_recv()
    for cp in out:
        cp.wait_send()
    for cp in local:
        cp.wait()


def _dma_sems(*counts):
    return [pltpu.SemaphoreType.DMA((n,)) for n in counts]


def _rs_sibling(grads):
    n = len(grads)

    def body(*refs):
        ex = _rs_first(refs[:n], refs[n:2 * n], *refs[2 * n:])
        _comm_start(ex)
        _comm_wait(ex)

    return pl.pallas_call(
        body, name="grads_to_sibling",
        in_specs=[ANY] * n, out_specs=[ANY] * n,
        out_shape=[jax.ShapeDtypeStruct(g.shape[1:], g.dtype) for g in grads],
        scratch_shapes=_dma_sems(n, n),
        compiler_params=pltpu.CompilerParams(has_side_effects=True),
    )(*grads)


def _pair_sum(grads, recvs):
    n = len(grads)

    def body(c_ref, *refs):
        for a in range(n):
            refs[2 * n + a][...] = (refs[a][...].astype(F32) + refs[n + a][...].astype(F32)).astype(BF16)

    def g_spec(g):
        return pl.BlockSpec((None, None) + g.shape[2:], lambda j, c_ref: (c_ref[0], j, 0, 0))

    def r_spec(r):
        return pl.BlockSpec((None,) + r.shape[1:], lambda j, c_ref: (j, 0, 0))

    return pl.pallas_call(
        body, name="pair_sum",
        grid_spec=pltpu.PrefetchScalarGridSpec(
            num_scalar_prefetch=1, grid=(N_CHIP,),
            in_specs=[g_spec(g) for g in grads] + [r_spec(r) for r in recvs],
            out_specs=[r_spec(r) for r in recvs]),
        out_shape=[jax.ShapeDtypeStruct(r.shape, BF16) for r in recvs],
        compiler_params=_params(("parallel",)),
    )(lax.axis_index("c").reshape(1).astype(jnp.int32), *grads, *recvs)


def _rs_chips(parts):
    n = len(parts)

    def body(*refs):
        ex = _rs_second(refs[:n], refs[n:2 * n], *refs[2 * n:])
        _comm_start(ex)
        _comm_wait(ex)

    return pl.pallas_call(
        body, name="grads_to_chips",
        in_specs=[ANY] * n, out_specs=[ANY] * n,
        out_shape=[jax.ShapeDtypeStruct(p.shape, p.dtype) for p in parts],
        scratch_shapes=_dma_sems(3 * n, 3 * n, n),
        compiler_params=pltpu.CompilerParams(has_side_effects=True),
    )(*parts)


SMALL_ROWS = 544


def _all_reduce_small(buf):
    def body(in_ref, out_ref, recv, acc, send1, recv1, send2, recv2):
        x, y, c = _place()
        me = 4 * x + 2 * y + c
        peers = [(x ^ (r >> 2), y ^ ((r >> 1) & 1), c ^ (r & 1)) for r in range(1, N_DEV)]

        def idx(p):
            return 4 * p[0] + 2 * p[1] + p[2]

        first = [pltpu.make_async_remote_copy(
            src_ref=in_ref.at[idx(p)], dst_ref=recv.at[me], send_sem=send1.at[r], recv_sem=recv1.at[r],
            device_id=p, device_id_type=MESH) for r, p in enumerate(peers)]
        for cp in first:
            cp.start()
        recv[me] = in_ref[me]
        for r, p in enumerate(peers):
            pltpu.make_async_remote_copy(
                src_ref=in_ref.at[idx(p)], dst_ref=recv.at[idx(p)], send_sem=send1.at[r], recv_sem=recv1.at[r],
                device_id=p, device_id_type=MESH).wait_recv()
        total = recv[0]
        for k in range(1, N_DEV):
            total = total + recv[k]
        acc[...] = total
        out_ref[me] = total
        second = [pltpu.make_async_remote_copy(
            src_ref=acc, dst_ref=out_ref.at[me], send_sem=send2.at[r], recv_sem=recv2.at[r],
            device_id=p, device_id_type=MESH) for r, p in enumerate(peers)]
        for cp in second:
            cp.start()
        for r, p in enumerate(peers):
            pltpu.make_async_remote_copy(
                src_ref=acc, dst_ref=out_ref.at[idx(p)], send_sem=send2.at[r], recv_sem=recv2.at[r],
                device_id=p, device_id_type=MESH).wait_recv()
        for cp in first + second:
            cp.wait_send()

    vm = pl.BlockSpec(memory_space=pltpu.VMEM)
    return pl.pallas_call(
        body, name="small_grads_all_reduce",
        in_specs=[vm], out_specs=vm,
        out_shape=jax.ShapeDtypeStruct(buf.shape, F32),
        scratch_shapes=[pltpu.VMEM(buf.shape, F32), pltpu.VMEM(buf.shape[1:], F32),
                        pltpu.SemaphoreType.DMA((7,)), pltpu.SemaphoreType.DMA((7,)),
                        pltpu.SemaphoreType.DMA((7,)), pltpu.SemaphoreType.DMA((7,))],
        compiler_params=pltpu.CompilerParams(has_side_effects=True, vmem_limit_bytes=VMEM_LIMIT),
    )(buf)


def _dilate(a, d):
    if d == 1:
        return a
    S, C = a.shape
    return a.reshape(S // d, d, C).transpose(1, 0, 2).reshape(S, C)


def _undilate(a, d):
    if d == 1:
        return a
    S, C = a.shape
    return a.reshape(d, S // d, C).transpose(1, 0, 2).reshape(S, C)


def _cols(a, cb, n=1):
    return a[:, cb * WIDTH:(cb + n) * WIDTH]


def _to_blocks(g, kind):
    if kind == "cols":
        R = g.shape[0]
        return g.reshape(R, N_CHIP, 2, -1).transpose(2, 1, 0, 3)
    if kind == "rows":
        C = g.shape[1]
        return g.reshape(N_CHIP, 2, -1, C).transpose(1, 0, 2, 3)
    return g.reshape(4 * WIDTH, N_CHIP, 2, -1).transpose(2, 1, 0, 3)


SMALL = ("norm_g", "gm_ln_g", "gm_ln_b", "gm_ws", "gm_bs", "pool_w", "pool_scale", "mem_norm_g", "final_norm_g")


def _pack_small(tree):
    flat = jnp.concatenate([tree[k].reshape(-1, 128) for k in SMALL], axis=0)
    return jnp.pad(flat, ((0, N_DEV * SMALL_ROWS - flat.shape[0]), (0, 0)))


def _unpack_small(flat, like):
    out, at = {}, 0
    for k in SMALL:
        rows = like[k].size // 128
        out[k] = flat[at:at + rows].reshape(like[k].shape)
        at += rows
    return out


def _make_layer(wbr, wg, wkv, wb, wout, norm_g, mem_norm_g, ln_g, ln_b, gm_ws, gm_bs, pool_w, pool_scale):
    tril = jnp.tril(jnp.ones((CHUNK, CHUNK), bool))
    wsm = jnp.where(tril, gm_ws, 0.0).astype(BF16)
    pw = pool_w.astype(BF16)
    return dict(wbr=wbr, wg=wg, wkv=wkv, wb=wb, wout=wout, g=norm_g[None], mg=mem_norm_g[None], ln_g=ln_g[None],
                ln_b=ln_b[None], wsm=wsm, wsm_t=wsm.transpose(0, 2, 1), pw=pw, pw_t=pw.transpose(0, 2, 1),
                ps=pool_scale[None], bias=jnp.repeat(gm_bs.T, HEAD, axis=1))


def _layer_fwd(xl, mem0, L, next_shards=()):
    S = xl.shape[0]
    proj, gates, h, half_gathered = _in_proj(xl, L["g"], L["wbr"], L["wg"], next_shards[:1])
    kv, mem_n = _mem_kv(mem0, L["mg"], L["wkv"])
    y4, gathered = _abm_fwd(proj, L["ln_g"], L["ln_b"], L["wsm"], L["bias"], L["pw"], L["ps"], kv, half_gathered)
    o_g, l_g = [], []
    for gi, d in enumerate(DILATIONS):
        if d == 1:
            o, lse = _attn_fwd(proj, CB_Q0, proj, CB_K, proj, CB_CV, S // CHUNK)
        else:
            o, lse = _attn_fwd_dilated(proj, CB_Q0 + gi, CB_K, CB_CV, d)
        o_g.append(o)
        l_g.append(lse)
    (xn, y4, oc, lse, z), rest = _merge_fwd(xl, y4, o_g, l_g, proj, gates, L["wb"], L["wout"], next_shards[1:])
    saved = dict(x=xl, proj=proj, gates=gates, h=h, kv=kv, mem_n=mem_n, y4=y4, oc=oc, lse=lse, z=z)
    return xn, saved, gathered + rest


def _place_cols(dst, piece, cb):
    return lax.dynamic_update_slice(dst, piece, (0, cb * WIDTH))


def _layer_bwd(dx, mem0, L, sv, later=()):
    S = dx.shape[0]
    proj = sv["proj"]
    (dy3, doc, delta, dcg, dgm, dt), from_sibling = _merge_bwd(dx, sv["y4"], sv["oc"], proj, sv["gates"], L["wb"],
                                                              L["wout"], later)
    pair = _pair_sum(later, from_sibling) if later else ()
    dwout = _matmul_tn(sv["z"], dx, D_MODEL, "dw_out")
    dwb = _dw_branch(sv["y4"], dt)
    dpb, dm, dlng, dlnb, dws, dbias, dpw, dps, dkv = _abm_bwd(
        proj, dy3, L["ln_g"], L["ln_b"], L["wsm"], L["wsm_t"], L["bias"], L["pw"], L["pw_t"], L["ps"], sv["kv"])
    dk, dv = None, None
    for gi, d in enumerate(DILATIONS):
        if d == 1:
            r = _attn_bwd(proj, CB_Q0, proj, CB_K, proj, CB_CV, doc, sv["lse"], delta, S // CHUNK)
        else:
            r = _attn_bwd_dilated(proj, CB_Q0 + gi, CB_K, CB_CV, doc, sv["lse"], delta, d)
        dpb = _place_cols(dpb, r[0], CB_Q0 + gi)
        dkg, dvg = r[1].astype(F32), r[2].astype(F32)
        dk = dkg if dk is None else dk + dkg
        dv = dvg if dv is None else dv + dvg
    dpb = _place_cols(dpb, dk.astype(BF16), CB_K)
    dpb = _place_cols(dpb, dv.astype(BF16), CB_CV)
    dpb = _place_cols(dpb, dcg, CB_CGATE)
    dpb = _place_cols(dpb, dm, CB_MQ)
    dwkv, dmg = _mem_bwd(mem0, L["mg"], sv["mem_n"], L["wkv"], dkv)
    if later:
        dwin_b, parts_rest = _matmul_tn(sv["h"], dpb, D_BRANCHES // 4, "dw_in_branches_scatter", pair[1:])
    else:
        dwin_b, parts_rest = _matmul_tn(sv["h"], dpb, D_BRANCHES // 4, "dw_in_branches"), []
    dwin = jnp.concatenate([dwin_b, _matmul_tn(sv["h"], dgm, D_GATES // 4, "dw_in_gates")], axis=1)
    dxi, dng, parts = _dh_bwd(dpb, L["wbr"], dgm, L["wg"], sv["x"], L["g"], dx, pair[:1])
    parts = parts + parts_rest
    big = dict(w_in=dwin, w_mem_kv=dwkv, w_branch=dwb, w_out=dwout)
    small = dict(norm_g=dng[0], gm_ln_g=dlng[0], gm_ln_b=dlnb[0], gm_ws=dws,
                 gm_bs=_bias_reduce(dbias)[:, :N_HEAD].T, pool_w=dpw, pool_scale=dps[0], mem_norm_g=dmg[0])
    return dxi, big, small, parts


BIG = ("w_in", "w_mem_kv", "w_branch", "w_out")


def _blocked(big):
    return [_to_blocks(big["w_in"], "cols"), _to_blocks(big["w_mem_kv"], "rows"),
            _to_blocks(big["w_branch"], "branch"), _to_blocks(big["w_out"], "rows")]


def _full_weights(gathered):
    win, wkv, wb, wout = gathered
    cut = D_BRANCHES - 4 * win.shape[2]
    wbr = jnp.concatenate([win[0], win[1], win[2], win[3], win[4][:, :cut]], axis=1)
    wg = jnp.concatenate([win[4][:, cut:], win[5], win[6], win[7]], axis=1)
    return (wbr, wg, wkv.reshape(D_MODEL, 2 * WIDTH),
            wb.reshape(N_DEV, 4, WIDTH, -1).transpose(1, 2, 0, 3).reshape(4, WIDTH, D_MODEL),
            wout.reshape(D_MODEL, D_MODEL))


def kernel(x, mem, norm_g, w_in, gm_ln_g, gm_ln_b, gm_ws, gm_bs, pool_w, pool_scale, mem_norm_g, w_mem_kv, w_branch, w_out, final_norm_g, loss_target, m_norm_g, m_w_in, m_gm_ln_g, m_gm_ln_b, m_gm_ws, m_gm_bs, m_pool_w, m_pool_scale, m_mem_norm_g, m_w_mem_kv, m_w_branch, m_w_out, m_final_norm_g, v_norm_g, v_w_in, v_gm_ln_g, v_gm_ln_b, v_gm_ws, v_gm_bs, v_pool_w, v_pool_scale, v_mem_norm_g, v_w_mem_kv, v_w_branch, v_w_out, v_final_norm_g):
    x0 = x[0]
    mem0 = mem[0]
    tgt = loss_target[0]
    S = x0.shape[0]

    shards = [[w_in[l].astype(BF16), w_mem_kv[l].astype(BF16), w_branch[l].astype(BF16).reshape(4 * WIDTH, -1),
               w_out[l].astype(BF16)] for l in range(DEPTH)]
    gathered = _all_gather(shards[0])
    layers, saved = [], []
    xl = x0
    for l in range(DEPTH):
        layers.append(_make_layer(*_full_weights(gathered), norm_g[l], mem_norm_g[l], gm_ln_g[l], gm_ln_b[l],
                                  gm_ws[l], gm_bs[l], pool_w[l], pool_scale[l]))
        xl, sv, gathered = _layer_fwd(xl, mem0, layers[l], shards[l + 1] if l + 1 < DEPTH else ())
        saved.append(sv)

    loss_part, dx, d_final = _loss_head(xl, final_norm_g[None], tgt)
    loss = lax.psum(loss_part[0, 0], ("x", "y", "c"))

    small = {k: [None] * DEPTH for k in SMALL if k != "final_norm_g"}
    parts = [None] * DEPTH
    later = ()
    for l in reversed(range(DEPTH)):
        dx, gb, gs, done = _layer_bwd(dx, mem0, layers[l], saved[l], later)
        if later:
            parts[l + 1] = done
        later = _blocked(gb)
        for k in gs:
            small[k][l] = gs[k]
    grad_x = dx[None]
    parts[0] = _rs_chips(_pair_sum(later, _rs_sibling(later)))

    small_tree = {k: jnp.stack(small[k]) for k in small}
    small_tree["final_norm_g"] = d_final[0]
    reduced = _all_reduce_small(_pack_small(small_tree).reshape(N_DEV, SMALL_ROWS, 128))

    weights = dict(norm_g=norm_g, w_in=w_in, gm_ln_g=gm_ln_g, gm_ln_b=gm_ln_b, gm_ws=gm_ws, gm_bs=gm_bs,
                   pool_w=pool_w, pool_scale=pool_scale, mem_norm_g=mem_norm_g, w_mem_kv=w_mem_kv,
                   w_branch=w_branch, w_out=w_out, final_norm_g=final_norm_g)
    m_in = dict(norm_g=m_norm_g, w_in=m_w_in, gm_ln_g=m_gm_ln_g, gm_ln_b=m_gm_ln_b, gm_ws=m_gm_ws, gm_bs=m_gm_bs,
                pool_w=m_pool_w, pool_scale=m_pool_scale, mem_norm_g=m_mem_norm_g, w_mem_kv=m_w_mem_kv,
                w_branch=m_w_branch, w_out=m_w_out, final_norm_g=m_final_norm_g)
    v_in = dict(norm_g=v_norm_g, w_in=v_w_in, gm_ln_g=v_gm_ln_g, gm_ln_b=v_gm_ln_b, gm_ws=v_gm_ws, gm_bs=v_gm_bs,
                pool_w=v_pool_w, pool_scale=v_pool_scale, mem_norm_g=v_mem_norm_g, w_mem_kv=v_w_mem_kv,
                w_branch=v_w_branch, w_out=v_w_out, final_norm_g=v_final_norm_g)
    res = {}
    for a, k in enumerate(BIG):
        shape = weights[k].shape
        by_layer = [parts[l][a] for l in range(DEPTH)]
        lrc = (DEPTH,) + by_layer[0].shape[1:]
        outs = _adamw_layers(by_layer, weights[k].reshape(lrc), m_in[k].reshape(lrc), v_in[k].reshape(lrc),
                             "adamw_" + k)
        res[k] = [o.reshape(shape) for o in outs]
    outs = _adamw(reduced.reshape(1, N_DEV * SMALL_ROWS, 128), _pack_small(weights), _pack_small(m_in),
                  _pack_small(v_in), "adamw_small")
    unpacked = [_unpack_small(o, weights) for o in outs]
    for k in SMALL:
        res[k] = [u[k] for u in unpacked]

    order = ("norm_g", "w_in", "gm_ln_g", "gm_ln_b", "gm_ws", "gm_bs", "pool_w", "pool_scale", "mem_norm_g",
             "w_mem_kv", "w_branch", "w_out", "final_norm_g")
    return (loss, grad_x, *[res[k][0] for k in order], *[res[k][1] for k in order],
            *[res[k][2] for k in order], *[res[k][3] for k in order])
```

```python
import functools
import math

import jax
import jax.numpy as jnp
from jax import lax
from jax.experimental import pallas as pl
from jax.experimental.pallas import tpu as pltpu

F32 = jnp.float32
BF16 = jnp.bfloat16

D_MODEL = 1024
DEPTH = 4
WIDTH = 512
D_IN = 10752
HEAD = 128
N_HEAD = 4
CHUNK = 128
MEM_LEN = 256
POOL_WINDOWS = (2, 4, 8, 16)
DILATIONS = (1, 4, 16)
EPS = 1e-6
NEG = -1e30
ATT_SCALE = HEAD ** -0.5
N_DEV = 8
N_CHIP = 4

D_BRANCHES = 6656
D_GATES = D_IN - D_BRANCHES
CB_U, CB_V, CB_AGATE, CB_PIN, CB_PGATE = 0, 1, 2, 3, 4
CB_Q0, CB_K, CB_CV, CB_CGATE, CB_MQ, CB_MGATE = 5, 8, 9, 10, 11, 12

ADAM_LR = 0.001
ADAM_B1 = 0.9
ADAM_B2 = 0.999
ADAM_EPS = 1e-08
ADAM_WD = 0.01
ADAM_STEP = 10

VMEM_LIMIT = 56 * 1024 * 1024
MESH = pl.DeviceIdType.MESH
ANY = pl.BlockSpec(memory_space=pl.ANY)

NT = (((1,), (1,)), ((), ()))
TN = (((0,), (0,)), ((), ()))


def _dot(a, b):
    return jnp.dot(a, b, preferred_element_type=F32)


def _dot_nt(a, b):
    return lax.dot_general(a, b, NT, preferred_element_type=F32)


def _dot_tn(a, b):
    return lax.dot_general(a, b, TN, preferred_element_type=F32)


def _sigmoid(x):
    return 1.0 / (1.0 + jnp.exp(-x))


def _silu(x):
    return x * _sigmoid(x)


def _dsilu(x):
    s = _sigmoid(x)
    return s * (1.0 + x * (1.0 - s))


def _gelu(x):
    return 0.5 * x * (1.0 + lax.erf(x * (2.0 ** -0.5)))


def _dgelu(x):
    return 0.5 * (1.0 + lax.erf(x * (2.0 ** -0.5))) + x * jnp.exp(-0.5 * x * x) * (1.0 / math.sqrt(2.0 * math.pi))


def _col(blk, h):
    lane = lax.broadcasted_iota(jnp.int32, blk.shape, 1)
    return jnp.sum(jnp.where(lane == h, blk, 0.0), axis=1, keepdims=True)


def _put_cols(cols):
    rows = cols[0].shape[0]
    lane = lax.broadcasted_iota(jnp.int32, (rows, 128), 1)
    out = jnp.zeros((rows, 128), F32)
    for h, cv in enumerate(cols):
        out = jnp.where(lane == h, cv, out)
    return out


def _params(sem, vmem=VMEM_LIMIT):
    return pltpu.CompilerParams(dimension_semantics=sem, vmem_limit_bytes=vmem)


def _full(shape):
    nd = len(shape)
    return pl.BlockSpec(shape, lambda *_: (0,) * nd)


def _rows(tm, width, cb=0):
    return pl.BlockSpec((tm, width), lambda i: (i, cb))


def _in_proj(x, g, wbr, wg, shards=()):
    S = x.shape[0]
    tm, tnb, tng = 1024, D_BRANCHES // 4, D_GATES // 4
    njb = 4
    n = len(shards)
    ni, nj = S // tm, 2 * njb

    def body(x_ref, g_ref, wbr_ref, wg_ref, *rest):
        ins, (proj_ref, gates_ref, h_ref), outs = rest[:n], rest[n:n + 3], rest[n + 3:2 * n + 3]
        hs, sems = rest[2 * n + 3], rest[2 * n + 4:]
        i, j = pl.program_id(0), pl.program_id(1)

        if n:
            @pl.when(jnp.logical_and(i == 0, j == 0))
            def _():
                _comm_start(_ag_first(ins, outs, *sems))

        @pl.when(j == 0)
        def _():
            xf = x_ref[...]
            r = lax.rsqrt(jnp.mean(xf * xf, axis=-1, keepdims=True) + EPS)
            h = (xf * r * g_ref[...]).astype(BF16)
            hs[...] = h
            h_ref[...] = h

        @pl.when(j < njb)
        def _():
            proj_ref[...] = _dot(hs[...], wbr_ref[...]).astype(BF16)

        @pl.when(j >= njb)
        def _():
            gates_ref[...] = _dot(hs[...], wg_ref[...]).astype(BF16)

        if n:
            @pl.when(jnp.logical_and(i == ni - 1, j == nj - 1))
            def _():
                _comm_wait(_ag_first(ins, outs, *sems))

    def first(j):
        return jnp.minimum(j, njb - 1)

    def second(j):
        return jnp.maximum(j - njb, 0)

    res = pl.pallas_call(
        body, name="in_proj_gather" if n else "in_proj",
        grid=(ni, nj),
        in_specs=[pl.BlockSpec((tm, D_MODEL), lambda i, j: (i, 0)),
                  pl.BlockSpec((1, D_MODEL), lambda i, j: (0, 0)),
                  pl.BlockSpec((D_MODEL, tnb), lambda i, j: (0, first(j))),
                  pl.BlockSpec((D_MODEL, tng), lambda i, j: (0, second(j)))] + [ANY] * n,
        out_specs=[pl.BlockSpec((tm, tnb), lambda i, j: (i, first(j))),
                   pl.BlockSpec((tm, tng), lambda i, j: (i, second(j))),
                   pl.BlockSpec((tm, D_MODEL), lambda i, j: (i, 0))] + [ANY] * n,
        out_shape=[jax.ShapeDtypeStruct((S, D_BRANCHES), BF16), jax.ShapeDtypeStruct((S, D_GATES), BF16),
                   jax.ShapeDtypeStruct((S, D_MODEL), BF16)]
                  + [jax.ShapeDtypeStruct((N_DEV,) + s.shape, s.dtype) for s in shards],
        scratch_shapes=[pltpu.VMEM((tm, D_MODEL), BF16)] + (_dma_sems(4 * n, 4 * n, n) if n else []),
        compiler_params=_params(("arbitrary", "arbitrary")),
    )(x, g, wbr, wg, *shards)
    return res[0], res[1], res[2], list(res[3:])


def _mem_kv(mem, g, w):
    M = mem.shape[0]

    def body(m_ref, g_ref, w_ref, kv_ref, mn_ref):
        xf = m_ref[...]
        r = lax.rsqrt(jnp.mean(xf * xf, axis=-1, keepdims=True) + EPS)
        mn = (xf * r * g_ref[...]).astype(BF16)
        mn_ref[...] = mn
        kv_ref[...] = _dot(mn, w_ref[...]).astype(BF16)

    return pl.pallas_call(
        body, name="mem_kv",
        out_shape=[jax.ShapeDtypeStruct((M, 2 * WIDTH), BF16), jax.ShapeDtypeStruct((M, D_MODEL), BF16)],
        compiler_params=pltpu.CompilerParams(vmem_limit_bytes=VMEM_LIMIT),
    )(mem, g, w)


def _band_masks(win):
    t = lax.broadcasted_iota(jnp.int32, (CHUNK, CHUNK), 0)
    s = lax.broadcasted_iota(jnp.int32, (CHUNK, CHUNK), 1)
    cur = jnp.logical_and(t - s >= 0, t - s < win)
    prev = s > t + CHUNK - win
    return cur.astype(BF16), prev.astype(BF16)


def _inv_count(first_row, win):
    t = first_row + lax.broadcasted_iota(jnp.int32, (CHUNK, 1), 0)
    return 1.0 / jnp.minimum(t + 1, win).astype(F32)


def _layer_norm_fwd(v):
    mu = jnp.mean(v, axis=-1, keepdims=True)
    vc = v - mu
    var = jnp.mean(vc * vc, axis=-1, keepdims=True)
    rstd = lax.rsqrt(var + EPS)
    return vc * rstd, rstd


def _mem_softmax(q, kmem):
    s = _dot_nt(q, kmem) * ATT_SCALE
    m = jnp.max(s, axis=-1, keepdims=True)
    e = jnp.exp(s - m)
    return e * (1.0 / jnp.sum(e, axis=-1, keepdims=True))


def _abm_fwd(proj, ln_g, ln_b, wsm, bias_full, pool_w, pool_scale, kv, gathered=()):
    S = proj.shape[0]
    tm = 512
    nchunk = tm // CHUNK
    n = len(gathered)
    nsteps = S // tm

    def body(u_ref, v_ref, ag_ref, p_ref, ph_ref, pg_ref, mq_ref, mg_ref, lng_ref, lnb_ref, wsm_ref, bias_ref,
             pw_ref, ps_ref, kv_ref, *rest):
        y_ref, bufs, mix, sems = rest[n], rest[n + 1:2 * n + 1], rest[2 * n + 1], rest[2 * n + 2:]
        i = pl.program_id(0)

        if n:
            @pl.when(i == 0)
            def _():
                _comm_start(_ag_second(bufs, *sems))

        u = _gelu(u_ref[...].astype(F32))
        v = _gelu(v_ref[...].astype(F32))
        vhat, _ = _layer_norm_fwd(v)
        vln = (vhat * lng_ref[...] + lnb_ref[...]).astype(BF16)
        for c in range(nchunk):
            for h in range(N_HEAD):
                rs, cs = slice(c * CHUNK, (c + 1) * CHUNK), slice(h * HEAD, (h + 1) * HEAD)
                mix[rs, cs] = _dot(wsm_ref[h], vln[rs, cs]) + bias_ref[:, cs]
        y_ref[0] = (u * mix[...] * _silu(ag_ref[...].astype(F32))).astype(BF16)
        halo_ok = (i > 0).astype(F32)
        for c in range(nchunk):
            rs = slice(c * CHUNK, (c + 1) * CHUNK)
            for g, win in enumerate(POOL_WINDOWS):
                cs = slice(g * HEAD, (g + 1) * HEAD)
                bcur, bprev = _band_masks(win)
                cur = p_ref[rs, cs]
                if c == 0:
                    prev = (ph_ref[:, cs].astype(F32) * halo_ok).astype(BF16)
                else:
                    prev = p_ref[(c - 1) * CHUNK:c * CHUNK, cs]
                sums = _dot(bcur, cur) + _dot(bprev, prev)
                dm = sums * _inv_count(i * tm + c * CHUNK, win) - cur.astype(F32)
                mix[rs, cs] = _dot(dm.astype(BF16), pw_ref[g])
        y_ref[1] = (mix[...] * ps_ref[...] * _silu(pg_ref[...].astype(F32))).astype(BF16)
        for h in range(N_HEAD):
            cs = slice(h * HEAD, (h + 1) * HEAD)
            p = _mem_softmax(mq_ref[:, cs], kv_ref[:, cs])
            mix[:, cs] = _dot(p.astype(BF16), kv_ref[:, WIDTH + h * HEAD:WIDTH + (h + 1) * HEAD])
        y_ref[2] = (mix[...] * _silu(mg_ref[...].astype(F32))).astype(BF16)

        if n:
            @pl.when(i == nsteps - 1)
            def _():
                _comm_wait(_ag_second(bufs, *sems))

    blk = tm // CHUNK
    res = pl.pallas_call(
        body, name="abm_fwd_gather" if n else "abm_fwd",
        grid=(nsteps,),
        in_specs=[_rows(tm, WIDTH, CB_U), _rows(tm, WIDTH, CB_V), _rows(tm, WIDTH, CB_AGATE),
                  _rows(tm, WIDTH, CB_PIN),
                  pl.BlockSpec((CHUNK, WIDTH), lambda i: (jnp.maximum(i * blk - 1, 0), CB_PIN)),
                  _rows(tm, WIDTH, CB_PGATE), _rows(tm, WIDTH, CB_MQ), _rows(tm, WIDTH, CB_MGATE),
                  _full((1, WIDTH)), _full((1, WIDTH)), _full((N_HEAD, CHUNK, CHUNK)), _full((CHUNK, WIDTH)),
                  _full((4, HEAD, HEAD)), _full((1, WIDTH)), _full((MEM_LEN, 2 * WIDTH))] + [ANY] * n,
        out_specs=[pl.BlockSpec((3, tm, WIDTH), lambda i: (0, i, 0))] + [ANY] * n,
        out_shape=[jax.ShapeDtypeStruct((4, S, WIDTH), BF16)]
                  + [jax.ShapeDtypeStruct(b.shape, b.dtype) for b in gathered],
        input_output_aliases={15 + a: 1 + a for a in range(n)},
        scratch_shapes=[pltpu.VMEM((tm, WIDTH), F32)] + (_dma_sems(3 * n, 3 * n) if n else []),
        compiler_params=_params(("arbitrary",)),
    )(proj, proj, proj, proj, proj, proj, proj, proj, ln_g, ln_b, wsm, bias_full, pool_w, pool_scale, kv, *gathered)
    return res[0], list(res[1:])


ATT_TILE = 512


def _attn_fwd(q, qcb, k, kcb, v, vcb, bps):
    S = q.shape[0]
    tm = ATT_TILE
    nb = tm // CHUNK

    nblocks = nb * N_HEAD

    def body(q_ref, k_ref, v_ref, kh_ref, vh_ref, o_ref, l_ref, sc_s, sp_s, pc_s, pp_s):
        i = pl.program_id(0)

        def prev_kv(n, cs):
            if n == 0:
                return kh_ref[:, cs], vh_ref[:, cs]
            ps = slice((n - 1) * CHUNK, n * CHUNK)
            return k_ref[ps, cs], v_ref[ps, cs]

        pens = []
        for n in range(nb):
            rs = slice(n * CHUNK, (n + 1) * CHUNK)
            pens.append(jnp.full((N_HEAD * CHUNK, 1), jnp.where((i * nb + n) % bps != 0, 0.0, NEG), F32))
            for h in range(N_HEAD):
                cs = slice(h * HEAD, (h + 1) * HEAD)
                bs = slice((n * N_HEAD + h) * CHUNK, (n * N_HEAD + h + 1) * CHUNK)
                qh = q_ref[rs, cs]
                sc_s[bs, :] = _dot_nt(qh, k_ref[rs, cs])
                sp_s[bs, :] = _dot_nt(qh, prev_kv(n, cs)[0])
        row = lax.broadcasted_iota(jnp.int32, (nblocks * CHUNK, CHUNK), 0) & (CHUNK - 1)
        col = lax.broadcasted_iota(jnp.int32, (nblocks * CHUNK, CHUNK), 1)
        sc = jnp.where(col <= row, sc_s[...] * ATT_SCALE, NEG)
        sp = jnp.where(col >= row, sp_s[...] * ATT_SCALE, NEG) + jnp.concatenate(pens, axis=0)
        m = jnp.maximum(jnp.max(sc, axis=-1, keepdims=True), jnp.max(sp, axis=-1, keepdims=True))
        ec = jnp.exp(sc - m)
        ep = jnp.exp(sp - m)
        den = jnp.sum(ec, axis=-1, keepdims=True) + jnp.sum(ep, axis=-1, keepdims=True)
        inv = 1.0 / den
        pc_s[...] = (ec * inv).astype(BF16)
        pp_s[...] = (ep * inv).astype(BF16)
        lse = m + jnp.log(den)
        for n in range(nb):
            rs = slice(n * CHUNK, (n + 1) * CHUNK)
            for h in range(N_HEAD):
                cs = slice(h * HEAD, (h + 1) * HEAD)
                bs = slice((n * N_HEAD + h) * CHUNK, (n * N_HEAD + h + 1) * CHUNK)
                o = _dot(pc_s[bs, :], v_ref[rs, cs]) + _dot(pp_s[bs, :], prev_kv(n, cs)[1])
                o_ref[rs, cs] = o.astype(BF16)
            l_ref[rs, :] = _put_cols([lse[(n * N_HEAD + h) * CHUNK:(n * N_HEAD + h + 1) * CHUNK]
                                      for h in range(N_HEAD)])

    def halo(cb):
        return pl.BlockSpec((CHUNK, WIDTH), lambda i: (jnp.maximum(i * nb - 1, 0), cb))

    return pl.pallas_call(
        body, name=f"attn_fwd_{bps}",
        grid=(S // tm,),
        in_specs=[_rows(tm, WIDTH, qcb), _rows(tm, WIDTH, kcb), _rows(tm, WIDTH, vcb), halo(kcb), halo(vcb)],
        out_specs=[_rows(tm, WIDTH), _rows(tm, 128)],
        out_shape=[jax.ShapeDtypeStruct((S, WIDTH), BF16), jax.ShapeDtypeStruct((S, 128), F32)],
        scratch_shapes=[pltpu.VMEM((nblocks * CHUNK, CHUNK), F32), pltpu.VMEM((nblocks * CHUNK, CHUNK), F32),
                        pltpu.VMEM((nblocks * CHUNK, CHUNK), BF16), pltpu.VMEM((nblocks * CHUNK, CHUNK), BF16)],
        compiler_params=_params(("parallel",)),
    )(q, k, v, k, v)


def _gate_specs(tm):
    return [pl.BlockSpec((tm, D_MODEL), lambda i, b=b: (i, b)) for b in range(4)]


Y_SLOT = (0, 1, 3, 2)


def _merge_fwd(x, y4, o_g, l_g, proj, gates, wb, wout, shards=()):
    S = x.shape[0]
    tm = 256
    n = len(shards)
    nsteps = S // tm
    forward_at = nsteps - 4

    def body(x_ref, y_ref, o0, o1, o2, l0, l1, l2, cg_ref, *rest):
        gm = rest[:4]
        wb_ref, wo_ref = rest[4:6]
        s_in = rest[6:6 + n]
        xn_ref, yc_ref, oc_ref, lse_ref, z_ref = rest[6 + n:11 + n]
        s_out, ocs, sems = rest[11 + n:11 + 2 * n], rest[11 + 2 * n], rest[12 + 2 * n:]
        i = pl.program_id(0)

        if n:
            @pl.when(i == 0)
            def _():
                _comm_start(_ag_first(s_in, s_out, *sems[:3]))

            @pl.when(i == forward_at)
            def _():
                incoming = _ag_first(s_in, s_out, *sems[:3])[2]
                for a in range(n):
                    for k in range(1, 4):
                        incoming[4 * a + k].wait_recv()
                _comm_start(_ag_second(s_out, *sems[3:]))

        lcols = []
        for h in range(N_HEAD):
            cs = slice(h * HEAD, (h + 1) * HEAD)
            ls = [_col(l[...], h) for l in (l0, l1, l2)]
            m = jnp.maximum(jnp.maximum(ls[0], ls[1]), ls[2])
            tot = jnp.exp(ls[0] - m) + jnp.exp(ls[1] - m) + jnp.exp(ls[2] - m)
            lse = m + jnp.log(tot)
            ocs[:, cs] = sum(jnp.exp(lg - lse) * o[:, cs].astype(F32) for lg, o in zip(ls, (o0, o1, o2)))
            lcols.append(lse)
        lse_ref[...] = _put_cols(lcols)
        oc = ocs[...]
        oc_ref[...] = oc.astype(BF16)
        yc = (oc * _silu(cg_ref[...].astype(F32))).astype(BF16)
        yc_ref[...] = yc
        ys = (y_ref[0], y_ref[1], yc, y_ref[2])
        z = jnp.zeros((tm, D_MODEL), F32)
        for b in range(4):
            z = z + _sigmoid(gm[b][...].astype(F32)) * _dot(ys[b], wb_ref[b])
        zb = z.astype(BF16)
        z_ref[...] = zb
        xn_ref[...] = x_ref[...] + _dot(zb, wo_ref[...])

        if n:
            @pl.when(i == nsteps - 1)
            def _():
                local, out, incoming = _ag_first(s_in, s_out, *sems[:3])
                for a in range(n):
                    incoming[4 * a].wait_recv()
                _comm_wait(_ag_second(s_out, *sems[3:]))
                for cp in out:
                    cp.wait_send()
                for cp in local:
                    cp.wait()

    res = pl.pallas_call(
        body, name="merge_fwd_gather" if n else "merge_fwd",
        grid=(nsteps,),
        in_specs=[_rows(tm, D_MODEL), pl.BlockSpec((3, tm, WIDTH), lambda i: (0, i, 0)),
                  _rows(tm, WIDTH), _rows(tm, WIDTH), _rows(tm, WIDTH),
                  _rows(tm, 128), _rows(tm, 128), _rows(tm, 128),
                  _rows(tm, WIDTH, CB_CGATE)] + _gate_specs(tm)
                 + [_full((4, WIDTH, D_MODEL)), _full((D_MODEL, D_MODEL))] + [ANY] * n,
        out_specs=[_rows(tm, D_MODEL), pl.BlockSpec((None, tm, WIDTH), lambda i: (Y_SLOT[2], i, 0)),
                   _rows(tm, WIDTH), _rows(tm, 128), _rows(tm, D_MODEL)] + [ANY] * n,
        out_shape=[jax.ShapeDtypeStruct((S, D_MODEL), F32), jax.ShapeDtypeStruct(y4.shape, BF16),
                   jax.ShapeDtypeStruct((S, WIDTH), BF16), jax.ShapeDtypeStruct((S, 128), F32),
                   jax.ShapeDtypeStruct((S, D_MODEL), BF16)]
                  + [jax.ShapeDtypeStruct((N_DEV,) + s.shape, s.dtype) for s in shards],
        input_output_aliases={1: 1},
        scratch_shapes=[pltpu.VMEM((tm, WIDTH), F32)] + (_dma_sems(4 * n, 4 * n, n, 3 * n, 3 * n) if n else []),
        compiler_params=_params(("arbitrary",)),
    )(x, y4, *o_g, *l_g, proj, *([gates] * 4), wb, wout, *shards)
    return res[:5], list(res[5:])


def _loss_head(x, g, tgt):
    S = x.shape[0]
    tm = 512

    def body(x_ref, g_ref, t_ref, loss_ref, dx_ref, dg_ref):
        @pl.when(pl.program_id(0) == 0)
        def _():
            loss_ref[...] = jnp.zeros_like(loss_ref)
            dg_ref[...] = jnp.zeros_like(dg_ref)

        xf = x_ref[...]
        r = lax.rsqrt(jnp.mean(xf * xf, axis=-1, keepdims=True) + EPS)
        xhat = xf * r
        gv = g_ref[...]
        err = xhat * gv - t_ref[...]
        e2 = jnp.sum(err * err, axis=-1, keepdims=True)
        loss_ref[...] += (0.5 / D_MODEL) * jnp.sum(e2, axis=0, keepdims=True)
        dy = err * (1.0 / D_MODEL)
        dg_ref[...] += jnp.sum(dy * xhat, axis=0, keepdims=True)
        dxh = dy * gv
        dx_ref[...] = r * (dxh - xhat * jnp.mean(dxh * xhat, axis=-1, keepdims=True))

    return pl.pallas_call(
        body, name="loss_head",
        grid=(S // tm,),
        in_specs=[_rows(tm, D_MODEL), _full((1, D_MODEL)), _rows(tm, D_MODEL)],
        out_specs=[_full((1, 128)), _rows(tm, D_MODEL), _full((1, D_MODEL))],
        out_shape=[jax.ShapeDtypeStruct((1, 128), F32), jax.ShapeDtypeStruct((S, D_MODEL), F32),
                   jax.ShapeDtypeStruct((1, D_MODEL), F32)],
        compiler_params=_params(("arbitrary",)),
    )(x, g, tgt)


def _merge_bwd(dxo, y4, oc, proj, gates, wb, wout, grads=()):
    S = dxo.shape[0]
    tm = 256
    n = len(grads)
    nsteps = S // tm

    def body(dx_ref, y_ref, oc_ref, cg_ref, *rest):
        gm = rest[:4]
        wb_ref, wo_ref = rest[4:6]
        g_in = rest[6:6 + n]
        dy_ref, doc_ref, delta_ref, dcg_ref, dgm_ref, dt_ref = rest[6 + n:12 + n]
        g_out, sems = rest[12 + n:12 + 2 * n], rest[12 + 2 * n:]
        i = pl.program_id(0)

        if n:
            @pl.when(i == 0)
            def _():
                _comm_start(_rs_first(g_in, g_out, *sems))

        dz = _dot_nt(dx_ref[...].astype(BF16), wo_ref[...])
        for b in range(4):
            gate = _sigmoid(gm[b][...].astype(F32))
            t = _dot(y_ref[Y_SLOT[b]], wb_ref[b])
            dgm_ref[:, b * D_MODEL:(b + 1) * D_MODEL] = (dz * t * gate * (1.0 - gate)).astype(BF16)
            dt = (dz * gate).astype(BF16)
            dt_ref[b] = dt
            dyb = _dot_nt(dt, wb_ref[b])
            if b == 2:
                cg = cg_ref[...].astype(F32)
                oc = oc_ref[...].astype(F32)
                doc = dyb * _silu(cg)
                dcg_ref[...] = (dyb * oc * _dsilu(cg)).astype(BF16)
                doc_ref[...] = doc.astype(BF16)
                prod = doc * oc
                delta_ref[...] = _put_cols([jnp.sum(prod[:, h * HEAD:(h + 1) * HEAD], axis=1, keepdims=True)
                                            for h in range(N_HEAD)])
            else:
                dy_ref[b if b < 2 else 2] = dyb.astype(BF16)

        if n:
            @pl.when(i == nsteps - 1)
            def _():
                _comm_wait(_rs_first(g_in, g_out, *sems))

    res = pl.pallas_call(
        body, name="merge_bwd_scatter" if n else "merge_bwd",
        grid=(nsteps,),
        in_specs=[_rows(tm, D_MODEL), pl.BlockSpec((4, tm, WIDTH), lambda i: (0, i, 0)),
                  _rows(tm, WIDTH), _rows(tm, WIDTH, CB_CGATE)] + _gate_specs(tm)
                 + [_full((4, WIDTH, D_MODEL)), _full((D_MODEL, D_MODEL))] + [ANY] * n,
        out_specs=[pl.BlockSpec((3, tm, WIDTH), lambda i: (0, i, 0)), _rows(tm, WIDTH), _rows(tm, 128),
                   _rows(tm, WIDTH), _rows(tm, 4 * D_MODEL), pl.BlockSpec((4, tm, D_MODEL), lambda i: (0, i, 0))]
                  + [ANY] * n,
        out_shape=[jax.ShapeDtypeStruct((3, S, WIDTH), BF16), jax.ShapeDtypeStruct((S, WIDTH), BF16),
                   jax.ShapeDtypeStruct((S, 128), F32), jax.ShapeDtypeStruct((S, WIDTH), BF16),
                   jax.ShapeDtypeStruct((S, 4 * D_MODEL), BF16), jax.ShapeDtypeStruct((4, S, D_MODEL), BF16)]
                  + [jax.ShapeDtypeStruct(g.shape[1:], g.dtype) for g in grads],
        scratch_shapes=_dma_sems(n, n) if n else [],
        compiler_params=_params(("arbitrary",)),
    )(dxo, y4, oc, proj, *([gates] * 4), wb, wout, *grads)
    return res[:6], list(res[6:])


def _dw_branch(y4, dt):
    S = y4.shape[1]
    tk = 1024
    nk = S // tk

    def body(y_ref, dt_ref, o_ref, acc):
        kk = pl.program_id(1)

        @pl.when(kk == 0)
        def _():
            acc[...] = jnp.zeros_like(acc)

        acc[...] += _dot_tn(y_ref[...], dt_ref[...])

        @pl.when(kk == nk - 1)
        def _():
            o_ref[...] = acc[...].astype(BF16)

    def slot(b):
        return jnp.where(b == 2, Y_SLOT[2], jnp.where(b == 3, Y_SLOT[3], b))

    return pl.pallas_call(
        body, name="dw_branch",
        grid=(4, nk),
        in_specs=[pl.BlockSpec((None, tk, WIDTH), lambda b, k: (slot(b), k, 0)),
                  pl.BlockSpec((None, tk, D_MODEL), lambda b, k: (b, k, 0))],
        out_specs=pl.BlockSpec((None, WIDTH, D_MODEL), lambda b, k: (b, 0, 0)),
        out_shape=jax.ShapeDtypeStruct((4, WIDTH, D_MODEL), BF16),
        scratch_shapes=[pltpu.VMEM((WIDTH, D_MODEL), F32)],
        compiler_params=_params(("parallel", "arbitrary")),
    )(y4, dt)


def _attn_bwd(q, qcb, k, kcb, v, vcb, do, lse, delta, bps):
    S = q.shape[0]
    tm = ATT_TILE
    nb = tm // CHUNK
    nblk = S // CHUNK

    ncur = nb * N_HEAD
    nprev = (nb + 1) * N_HEAD

    def body(q_ref, k_ref, v_ref, do_ref, l_ref, d_ref, kh_ref, vh_ref, qn_ref, don_ref, ln_ref, dn_ref,
             dq_ref, dk_ref, dv_ref, sc_s, sp_s, dpc_s, dpp_s, pc_s, pp_s, dsc_s, dsp_s):
        i = pl.program_id(0)

        def rows_of(n):
            if n < nb:
                rs = slice(n * CHUNK, (n + 1) * CHUNK)
                return rs, q_ref, do_ref, l_ref, d_ref
            return slice(0, CHUNK), qn_ref, don_ref, ln_ref, dn_ref

        def prev_kv(n, cs):
            if n == 0:
                return kh_ref[:, cs], vh_ref[:, cs]
            ps = slice((n - 1) * CHUNK, n * CHUNK)
            return k_ref[ps, cs], v_ref[ps, cs]

        def blk(n, h):
            return slice((n * N_HEAD + h) * CHUNK, (n * N_HEAD + h + 1) * CHUNK)

        pens, lses, deltas = [], [], []
        for n in range(nb + 1):
            rs, qr, dor, lr, dr = rows_of(n)
            gb = i * nb + n
            pen = jnp.where(gb % bps != 0, 0.0, NEG)
            if n == nb:
                pen = pen + jnp.where(gb < nblk, 0.0, NEG)
            pens.append(jnp.full((N_HEAD * CHUNK, 1), pen, F32))
            lblk, dblk = lr[rs, :], dr[rs, :]
            for h in range(N_HEAD):
                cs = slice(h * HEAD, (h + 1) * HEAD)
                qh, doh = qr[rs, cs], dor[rs, cs]
                lses.append(_col(lblk, h))
                deltas.append(_col(dblk, h))
                kp, vp = prev_kv(n, cs)
                sp_s[blk(n, h), :] = _dot_nt(qh, kp)
                dpp_s[blk(n, h), :] = _dot_nt(doh, vp)
                if n < nb:
                    sc_s[blk(n, h), :] = _dot_nt(qh, k_ref[rs, cs])
                    dpc_s[blk(n, h), :] = _dot_nt(doh, v_ref[rs, cs])
        lse = jnp.concatenate(lses, axis=0)
        delta = jnp.concatenate(deltas, axis=0)
        row = lax.broadcasted_iota(jnp.int32, (nprev * CHUNK, CHUNK), 0) & (CHUNK - 1)
        col = lax.broadcasted_iota(jnp.int32, (nprev * CHUNK, CHUNK), 1)
        sp = jnp.where(col >= row, sp_s[...] * ATT_SCALE, NEG) + jnp.concatenate(pens, axis=0)
        pp = jnp.exp(sp - lse)
        pp_s[...] = pp.astype(BF16)
        dsp_s[...] = (pp * (dpp_s[...] - delta)).astype(BF16)
        nc = ncur * CHUNK
        sc = jnp.where(col[:nc] <= row[:nc], sc_s[...] * ATT_SCALE, NEG)
        pc = jnp.exp(sc - lse[:nc])
        pc_s[...] = pc.astype(BF16)
        dsc_s[...] = (pc * (dpc_s[...] - delta[:nc])).astype(BF16)
        for n in range(nb):
            rs, qr, dor, _, _ = rows_of(n)
            rn, qnr, donr, _, _ = rows_of(n + 1)
            for h in range(N_HEAD):
                cs = slice(h * HEAD, (h + 1) * HEAD)
                kp, _ = prev_kv(n, cs)
                dq = _dot(dsc_s[blk(n, h), :], k_ref[rs, cs]) + _dot(dsp_s[blk(n, h), :], kp)
                dq_ref[rs, cs] = (dq * ATT_SCALE).astype(BF16)
                dk = _dot_tn(dsc_s[blk(n, h), :], qr[rs, cs]) + _dot_tn(dsp_s[blk(n + 1, h), :], qnr[rn, cs])
                dk_ref[rs, cs] = (dk * ATT_SCALE).astype(BF16)
                dv = _dot_tn(pc_s[blk(n, h), :], dor[rs, cs]) + _dot_tn(pp_s[blk(n + 1, h), :], donr[rn, cs])
                dv_ref[rs, cs] = dv.astype(BF16)

    def prev_halo(cb):
        return pl.BlockSpec((CHUNK, WIDTH), lambda i: (jnp.maximum(i * nb - 1, 0), cb))

    def next_halo(width, cb=0):
        return pl.BlockSpec((CHUNK, width), lambda i: (jnp.minimum(i * nb + nb, nblk - 1), cb))

    return pl.pallas_call(
        body, name=f"attn_bwd_{bps}",
        grid=(S // tm,),
        in_specs=[_rows(tm, WIDTH, qcb), _rows(tm, WIDTH, kcb), _rows(tm, WIDTH, vcb), _rows(tm, WIDTH),
                  _rows(tm, 128), _rows(tm, 128), prev_halo(kcb), prev_halo(vcb),
                  next_halo(WIDTH, qcb), next_halo(WIDTH), next_halo(128), next_halo(128)],
        out_specs=[_rows(tm, WIDTH), _rows(tm, WIDTH), _rows(tm, WIDTH)],
        out_shape=[jax.ShapeDtypeStruct((S, WIDTH), BF16)] * 3,
        scratch_shapes=[pltpu.VMEM((ncur * CHUNK, CHUNK), F32), pltpu.VMEM((nprev * CHUNK, CHUNK), F32),
                        pltpu.VMEM((ncur * CHUNK, CHUNK), F32), pltpu.VMEM((nprev * CHUNK, CHUNK), F32),
                        pltpu.VMEM((ncur * CHUNK, CHUNK), BF16), pltpu.VMEM((nprev * CHUNK, CHUNK), BF16),
                        pltpu.VMEM((ncur * CHUNK, CHUNK), BF16), pltpu.VMEM((nprev * CHUNK, CHUNK), BF16)],
        compiler_params=_params(("parallel",)),
    )(q, k, v, do, lse, delta, k, v, q, do, lse, delta)


def _dilated_split(d):
    hp = min(N_HEAD, 16 // d)
    return hp, N_HEAD // hp, HEAD * hp


def _by_class(src_ref, dst, d, hp):
    for j in range(hp):
        dst[j] = pltpu.einshape("(tr)l->(rt)l", src_ref[:, j * HEAD:(j + 1) * HEAD], r=d)


def _from_class(src, dst_ref, d, hp):
    for j in range(hp):
        dst_ref[:, j * HEAD:(j + 1) * HEAD] = pltpu.einshape("(rt)l->(tr)l", src[j].astype(BF16), r=d)


def _attn_fwd_dilated(proj, qcb, kcb, vcb, d):
    S = proj.shape[0]
    T = CHUNK * d
    hp, nh, cw = _dilated_split(d)
    nblocks = d * hp

    def body(q_ref, k_ref, v_ref, o_ref, l_ref, qf, kf, vf, kpf, vpf, kst, vst, of, lf, sc_s, sp_s, pc_s, pp_s):
        i, hh = pl.program_id(0), pl.program_id(1)
        _by_class(q_ref, qf, d, hp)
        _by_class(k_ref, kf, d, hp)
        _by_class(v_ref, vf, d, hp)

        @pl.when(i == 0)
        def _():
            kpf[...] = jnp.zeros_like(kpf)
            vpf[...] = jnp.zeros_like(vpf)

        @pl.when(i > 0)
        def _():
            kpf[...] = kst[hh]
            vpf[...] = vst[hh]

        def blk(ref, r, j):
            return ref[j, r * CHUNK:(r + 1) * CHUNK, :]

        def bs(r, j):
            return slice((r * hp + j) * CHUNK, (r * hp + j + 1) * CHUNK)

        for r in range(d):
            for j in range(hp):
                qb = blk(qf, r, j)
                sc_s[bs(r, j), :] = _dot_nt(qb, blk(kf, r, j))
                sp_s[bs(r, j), :] = _dot_nt(qb, blk(kpf, r, j))
        row = lax.broadcasted_iota(jnp.int32, (nblocks * CHUNK, CHUNK), 0) & (CHUNK - 1)
        col = lax.broadcasted_iota(jnp.int32, (nblocks * CHUNK, CHUNK), 1)
        sc = jnp.where(col <= row, sc_s[...] * ATT_SCALE, NEG)
        sp = jnp.where(col >= row, sp_s[...] * ATT_SCALE, NEG) + jnp.where(i > 0, 0.0, NEG)
        m = jnp.maximum(jnp.max(sc, axis=-1, keepdims=True), jnp.max(sp, axis=-1, keepdims=True))
        ec = jnp.exp(sc - m)
        ep = jnp.exp(sp - m)
        den = jnp.sum(ec, axis=-1, keepdims=True) + jnp.sum(ep, axis=-1, keepdims=True)
        inv = 1.0 / den
        pc_s[...] = (ec * inv).astype(BF16)
        pp_s[...] = (ep * inv).astype(BF16)
        lse = m + jnp.log(den)
        lane = lax.broadcasted_iota(jnp.int32, (CHUNK, 128), 1)
        for r in range(d):
            lblk = jnp.zeros((CHUNK, 128), F32)
            for j in range(hp):
                o = _dot(pc_s[bs(r, j), :], blk(vf, r, j)) + _dot(pp_s[bs(r, j), :], blk(vpf, r, j))
                of[j, r * CHUNK:(r + 1) * CHUNK, :] = o
                lblk = jnp.where(lane == hh * hp + j, lse[bs(r, j)], lblk)
            lf[r * CHUNK:(r + 1) * CHUNK, :] = lblk
        _from_class(of, o_ref, d, hp)
        lnat = pltpu.einshape("(rt)l->(tr)l", lf[...], r=d)

        @pl.when(hh == 0)
        def _():
            l_ref[...] = lnat

        @pl.when(hh > 0)
        def _():
            l_ref[...] += lnat

        kst[hh] = kf[...]
        vst[hh] = vf[...]

    def cols(cb):
        return pl.BlockSpec((T, cw), lambda i, hh: (i, cb * nh + hh))

    tile = pltpu.VMEM((hp, T, HEAD), BF16)
    return pl.pallas_call(
        body, name=f"attn_fwd_dilated_{d}",
        grid=(S // T, nh),
        in_specs=[cols(qcb), cols(kcb), cols(vcb)],
        out_specs=[cols(0), pl.BlockSpec((T, 128), lambda i, hh: (i, 0))],
        out_shape=[jax.ShapeDtypeStruct((S, WIDTH), BF16), jax.ShapeDtypeStruct((S, 128), F32)],
        scratch_shapes=[tile, tile, tile, tile, tile,
                        pltpu.VMEM((nh, hp, T, HEAD), BF16), pltpu.VMEM((nh, hp, T, HEAD), BF16),
                        pltpu.VMEM((hp, T, HEAD), F32), pltpu.VMEM((T, 128), F32),
                        pltpu.VMEM((nblocks * CHUNK, CHUNK), F32), pltpu.VMEM((nblocks * CHUNK, CHUNK), F32),
                        pltpu.VMEM((nblocks * CHUNK, CHUNK), BF16), pltpu.VMEM((nblocks * CHUNK, CHUNK), BF16)],
        compiler_params=_params(("arbitrary", "arbitrary")),
    )(proj, proj, proj)


def _attn_bwd_dilated(proj, qcb, kcb, vcb, do, lse, delta, d):
    S = proj.shape[0]
    T = CHUNK * d
    nt = S // T
    hp, nh, cw = _dilated_split(d)
    nblocks = d * hp

    def body(q_ref, k_ref, v_ref, do_ref, l_ref, d_ref, dq_ref, dk_ref, dv_ref,
             qf, dof, kf, vf, kpf, vpf, dqf, acck, accv, newk, newv,
             sc_s, sp_s, dpc_s, dpp_s, pc_s, pp_s, dsc_s, dsp_s):
        hh, i = pl.program_id(0), pl.program_id(1)

        @pl.when(i == 0)
        def _():
            for ref in (kpf, vpf, acck, accv):
                ref[...] = jnp.zeros_like(ref)
            dk_ref[...] = jnp.zeros_like(dk_ref)
            dv_ref[...] = jnp.zeros_like(dv_ref)

        def blk(ref, r, j):
            return ref[j, r * CHUNK:(r + 1) * CHUNK, :]

        def bs(r, j):
            return slice((r * hp + j) * CHUNK, (r * hp + j + 1) * CHUNK)

        @pl.when(i < nt)
        def _():
            _by_class(q_ref, qf, d, hp)
            _by_class(do_ref, dof, d, hp)
            _by_class(k_ref, kf, d, hp)
            _by_class(v_ref, vf, d, hp)
            lses, deltas = [], []
            lcls = pltpu.einshape("(tr)l->(rt)l", l_ref[...], r=d)
            dcls = pltpu.einshape("(tr)l->(rt)l", d_ref[...], r=d)
            for r in range(d):
                lblk = lcls[r * CHUNK:(r + 1) * CHUNK]
                dblk = dcls[r * CHUNK:(r + 1) * CHUNK]
                for j in range(hp):
                    lses.append(_col(lblk, hh * hp + j))
                    deltas.append(_col(dblk, hh * hp + j))
                    qb, dob = blk(qf, r, j), blk(dof, r, j)
                    sc_s[bs(r, j), :] = _dot_nt(qb, blk(kf, r, j))
                    dpc_s[bs(r, j), :] = _dot_nt(dob, blk(vf, r, j))
                    sp_s[bs(r, j), :] = _dot_nt(qb, blk(kpf, r, j))
                    dpp_s[bs(r, j), :] = _dot_nt(dob, blk(vpf, r, j))
            lse = jnp.concatenate(lses, axis=0)
            delta = jnp.concatenate(deltas, axis=0)
            row = lax.broadcasted_iota(jnp.int32, (nblocks * CHUNK, CHUNK), 0) & (CHUNK - 1)
            col = lax.broadcasted_iota(jnp.int32, (nblocks * CHUNK, CHUNK), 1)
            sp = jnp.where(col >= row, sp_s[...] * ATT_SCALE, NEG) + jnp.where(i > 0, 0.0, NEG)
            pp = jnp.exp(sp - lse)
            pp_s[...] = pp.astype(BF16)
            dsp_s[...] = (pp * (dpp_s[...] - delta)).astype(BF16)
            sc = jnp.where(col <= row, sc_s[...] * ATT_SCALE, NEG)
            pc = jnp.exp(sc - lse)
            pc_s[...] = pc.astype(BF16)
            dsc_s[...] = (pc * (dpc_s[...] - delta)).astype(BF16)
            for r in range(d):
                rows = slice(r * CHUNK, (r + 1) * CHUNK)
                for j in range(hp):
                    qb, dob = blk(qf, r, j), blk(dof, r, j)
                    dsc, dsp = dsc_s[bs(r, j), :], dsp_s[bs(r, j), :]
                    dqf[j, rows, :] = (_dot(dsc, blk(kf, r, j)) + _dot(dsp, blk(kpf, r, j))) * ATT_SCALE
                    newk[j, rows, :] = _dot_tn(dsc, qb) * ATT_SCALE
                    newv[j, rows, :] = _dot_tn(pc_s[bs(r, j), :], dob)
                    acck[j, rows, :] += _dot_tn(dsp, qb) * ATT_SCALE
                    accv[j, rows, :] += _dot_tn(pp_s[bs(r, j), :], dob)
            _from_class(dqf, dq_ref, d, hp)

        @pl.when(i > 0)
        def _():
            _from_class(acck, dk_ref, d, hp)
            _from_class(accv, dv_ref, d, hp)

        @pl.when(i < nt)
        def _():
            acck[...] = newk[...]
            accv[...] = newv[...]
            kpf[...] = kf[...]
            vpf[...] = vf[...]

    def cur(width, cb, nsplit):
        return pl.BlockSpec((T, width), lambda hh, i: (jnp.minimum(i, nt - 1), cb * nsplit + hh * (nsplit > 1)))

    def lag():
        return pl.BlockSpec((T, cw), lambda hh, i: (jnp.maximum(i - 1, 0), hh))

    tile = pltpu.VMEM((hp, T, HEAD), BF16)
    acc = pltpu.VMEM((hp, T, HEAD), F32)
    f32s = pltpu.VMEM((nblocks * CHUNK, CHUNK), F32)
    b16s = pltpu.VMEM((nblocks * CHUNK, CHUNK), BF16)
    return pl.pallas_call(
        body, name=f"attn_bwd_dilated_{d}",
        grid=(nh, nt + 1),
        in_specs=[cur(cw, qcb, nh), cur(cw, kcb, nh), cur(cw, vcb, nh), cur(cw, 0, nh), cur(128, 0, 1), cur(128, 0, 1)],
        out_specs=[cur(cw, 0, nh), lag(), lag()],
        out_shape=[jax.ShapeDtypeStruct((S, WIDTH), BF16)] * 3,
        scratch_shapes=[tile] * 6 + [acc] * 5 + [f32s] * 4 + [b16s] * 4,
        compiler_params=_params(("arbitrary", "arbitrary")),
    )(proj, proj, proj, do, lse, delta)


def _abm_bwd(proj, dy3, ln_g, ln_b, wsm, wsm_t, bias_full, pool_w, pool_wt, pool_scale, kv):
    S = proj.shape[0]
    tm = 512
    nchunk = tm // CHUNK
    nblk = S // CHUNK

    def body(u_ref, v_ref, ag_ref, p_ref, ph_ref, pg_ref, pgn_ref, mq_ref, mg_ref, dy_ref, dypn_ref,
             lng_ref, lnb_ref, wsm_ref, wsmt_ref, bias_ref, pw_ref, pwt_ref, ps_ref, kv_ref,
             dab_ref, dm_ref, dlng_ref, dlnb_ref, dws_ref, dbias_ref, dpw_ref, dps_ref, dkv_ref,
             mix, dvl, ddn):
        i = pl.program_id(0)

        @pl.when(i == 0)
        def _():
            for r in (dlng_ref, dlnb_ref, dws_ref, dbias_ref, dpw_ref, dps_ref, dkv_ref):
                r[...] = jnp.zeros_like(r)

        au = u_ref[...].astype(F32)
        av = v_ref[...].astype(F32)
        ag = ag_ref[...].astype(F32)
        u = _gelu(au)
        v = _gelu(av)
        vhat, rstd = _layer_norm_fwd(v)
        vln = (vhat * lng_ref[...] + lnb_ref[...]).astype(BF16)
        for c in range(nchunk):
            for h in range(N_HEAD):
                rs, cs = slice(c * CHUNK, (c + 1) * CHUNK), slice(h * HEAD, (h + 1) * HEAD)
                mix[rs, cs] = _dot(wsm_ref[h], vln[rs, cs]) + bias_ref[:, cs]
        dya = dy_ref[0].astype(F32)
        sg = _silu(ag)
        mixed = mix[...]
        dab_ref[:, 2 * WIDTH:3 * WIDTH] = (dya * u * mixed * _dsilu(ag)).astype(BF16)
        dab_ref[:, 0:WIDTH] = (dya * mixed * sg * _dgelu(au)).astype(BF16)
        dmixed = dya * u * sg
        dmb = dmixed.astype(BF16)
        tril = (lax.broadcasted_iota(jnp.int32, (CHUNK, CHUNK), 1)
                <= lax.broadcasted_iota(jnp.int32, (CHUNK, CHUNK), 0))
        for c in range(nchunk):
            rs = slice(c * CHUNK, (c + 1) * CHUNK)
            dbias_ref[...] += dmixed[rs, :]
            for h in range(N_HEAD):
                cs = slice(h * HEAD, (h + 1) * HEAD)
                dvl[rs, cs] = _dot(wsmt_ref[h], dmb[rs, cs])
                dws_ref[h] += jnp.where(tril, _dot_nt(dmb[rs, cs], vln[rs, cs]), 0.0)
        dvln = dvl[...]
        dlng_ref[...] += jnp.sum(dvln * vhat, axis=0, keepdims=True)
        dlnb_ref[...] += jnp.sum(dvln, axis=0, keepdims=True)
        dvh = dvln * lng_ref[...]
        dv = rstd * (dvh - jnp.mean(dvh, axis=-1, keepdims=True)
                     - vhat * jnp.mean(dvh * vhat, axis=-1, keepdims=True))
        dab_ref[:, WIDTH:2 * WIDTH] = (dv * _dgelu(av)).astype(BF16)

        halo_ok = (i > 0).astype(F32)
        for c in range(nchunk):
            rs = slice(c * CHUNK, (c + 1) * CHUNK)
            for g, win in enumerate(POOL_WINDOWS):
                cs = slice(g * HEAD, (g + 1) * HEAD)
                bcur, bprev = _band_masks(win)
                cur = p_ref[rs, cs]
                if c == 0:
                    prev = (ph_ref[:, cs].astype(F32) * halo_ok).astype(BF16)
                else:
                    prev = p_ref[(c - 1) * CHUNK:c * CHUNK, cs]
                sums = _dot(bcur, cur) + _dot(bprev, prev)
                dvl[rs, cs] = sums * _inv_count(i * tm + c * CHUNK, win) - cur.astype(F32)
        dmat = dvl[...].astype(BF16)
        for g in range(4):
            cs = slice(g * HEAD, (g + 1) * HEAD)
            mix[:, cs] = _dot(dmat[:, cs], pw_ref[g])
        yg = mix[...]
        pg = pg_ref[...].astype(F32)
        dyp = dy_ref[1].astype(F32)
        dyy = dyp * _silu(pg)
        scale = ps_ref[...]
        dab_ref[:, 4 * WIDTH:5 * WIDTH] = (dyp * yg * scale * _dsilu(pg)).astype(BF16)
        dps_ref[...] += jnp.sum(dyy * yg, axis=0, keepdims=True)
        dyg = (dyy * scale).astype(BF16)
        for g in range(4):
            cs = slice(g * HEAD, (g + 1) * HEAD)
            dpw_ref[g] += _dot_tn(dmat[:, cs], dyg[:, cs])
            mix[:, cs] = _dot(dyg[:, cs], pwt_ref[g])
        next_ok = (i + 1 < S // tm).astype(F32)
        dygn = (dypn_ref[...].astype(F32) * _silu(pgn_ref[...].astype(F32)) * scale * next_ok).astype(BF16)
        for c in range(nchunk + 1):
            for g, win in enumerate(POOL_WINDOWS):
                cs = slice(g * HEAD, (g + 1) * HEAD)
                if c < nchunk:
                    dd = mix[c * CHUNK:(c + 1) * CHUNK, cs]
                else:
                    dd = _dot(dygn[:, cs], pwt_ref[g])
                ddn[c * CHUNK:(c + 1) * CHUNK, cs] = dd * _inv_count(i * tm + c * CHUNK, win)
        ddnb = ddn[...].astype(BF16)
        for c in range(nchunk):
            rs = slice(c * CHUNK, (c + 1) * CHUNK)
            ns = slice((c + 1) * CHUNK, (c + 2) * CHUNK)
            for g, win in enumerate(POOL_WINDOWS):
                cs = slice(g * HEAD, (g + 1) * HEAD)
                bcur, bprev = _band_masks(win)
                dp = _dot_tn(bcur, ddnb[rs, cs]) + _dot_tn(bprev, ddnb[ns, cs]) - mix[rs, cs]
                dab_ref[rs, 3 * WIDTH + g * HEAD:3 * WIDTH + (g + 1) * HEAD] = dp.astype(BF16)

        mg = mg_ref[...].astype(F32)
        dym = dy_ref[2].astype(F32)
        dob = (dym * _silu(mg)).astype(BF16)
        for h in range(N_HEAD):
            cs = slice(h * HEAD, (h + 1) * HEAD)
            vs = slice(WIDTH + h * HEAD, WIDTH + (h + 1) * HEAD)
            qh = mq_ref[:, cs]
            p = _mem_softmax(qh, kv_ref[:, cs])
            pb = p.astype(BF16)
            mix[:, cs] = _dot(pb, kv_ref[:, vs])
            dp = _dot_nt(dob[:, cs], kv_ref[:, vs])
            ds = (p * (dp - jnp.sum(p * dp, axis=-1, keepdims=True))).astype(BF16)
            dm_ref[:, cs] = (_dot(ds, kv_ref[:, cs]) * ATT_SCALE).astype(BF16)
            dkv_ref[:, cs] += _dot_tn(ds, qh) * ATT_SCALE
            dkv_ref[:, vs] += _dot_tn(pb, dob[:, cs])
        dm_ref[:, WIDTH:2 * WIDTH] = (dym * mix[...] * _dsilu(mg)).astype(BF16)

    blk = tm // CHUNK
    small = [_full((1, WIDTH)), _full((1, WIDTH)), _full((N_HEAD, CHUNK, CHUNK)), _full((CHUNK, WIDTH)),
             _full((4, HEAD, HEAD)), _full((1, WIDTH)), _full((MEM_LEN, 2 * WIDTH))]
    return pl.pallas_call(
        body, name="abm_bwd",
        grid=(S // tm,),
        in_specs=[_rows(tm, WIDTH, CB_U), _rows(tm, WIDTH, CB_V), _rows(tm, WIDTH, CB_AGATE),
                  _rows(tm, WIDTH, CB_PIN),
                  pl.BlockSpec((CHUNK, WIDTH), lambda i: (jnp.maximum(i * blk - 1, 0), CB_PIN)),
                  _rows(tm, WIDTH, CB_PGATE),
                  pl.BlockSpec((CHUNK, WIDTH), lambda i: (jnp.minimum(i * blk + blk, nblk - 1), CB_PGATE)),
                  _rows(tm, WIDTH, CB_MQ), _rows(tm, WIDTH, CB_MGATE),
                  pl.BlockSpec((3, tm, WIDTH), lambda i: (0, i, 0)),
                  pl.BlockSpec((None, CHUNK, WIDTH), lambda i: (1, jnp.minimum(i * blk + blk, nblk - 1), 0)),
                  _full((1, WIDTH)), _full((1, WIDTH)), _full((N_HEAD, CHUNK, CHUNK)), _full((N_HEAD, CHUNK, CHUNK)),
                  _full((CHUNK, WIDTH)), _full((4, HEAD, HEAD)), _full((4, HEAD, HEAD)), _full((1, WIDTH)),
                  _full((MEM_LEN, 2 * WIDTH))],
        out_specs=[_rows(tm, 5 * WIDTH), _rows(tm, 2 * WIDTH)] + small,
        out_shape=[jax.ShapeDtypeStruct((S, D_BRANCHES), BF16), jax.ShapeDtypeStruct((S, 2 * WIDTH), BF16),
                   jax.ShapeDtypeStruct((1, WIDTH), F32), jax.ShapeDtypeStruct((1, WIDTH), F32),
                   jax.ShapeDtypeStruct((N_HEAD, CHUNK, CHUNK), F32), jax.ShapeDtypeStruct((CHUNK, WIDTH), F32),
                   jax.ShapeDtypeStruct((4, HEAD, HEAD), F32), jax.ShapeDtypeStruct((1, WIDTH), F32),
                   jax.ShapeDtypeStruct((MEM_LEN, 2 * WIDTH), F32)],
        scratch_shapes=[pltpu.VMEM((tm, WIDTH), F32), pltpu.VMEM((tm, WIDTH), F32),
                        pltpu.VMEM((tm + CHUNK, WIDTH), F32)],
        compiler_params=_params(("arbitrary",)),
    )(proj, proj, proj, proj, proj, proj, proj, proj, proj, dy3, dy3,
      ln_g, ln_b, wsm, wsm_t, bias_full, pool_w, pool_wt, pool_scale, kv)


def _bias_reduce(dbias_full):
    def body(d_ref, o_ref):
        d = d_ref[...]
        o_ref[...] = _put_cols([jnp.sum(d[:, h * HEAD:(h + 1) * HEAD], axis=1, keepdims=True) for h in range(N_HEAD)])

    return pl.pallas_call(body, name="bias_reduce", out_shape=jax.ShapeDtypeStruct((CHUNK, 128), F32))(dbias_full)


def _mem_bwd(mem, g, mem_n, w, dkv):
    def body(m_ref, g_ref, mn_ref, w_ref, dkv_ref, dw_ref, dg_ref):
        dkvb = dkv_ref[...].astype(BF16)
        dw_ref[...] = _dot_tn(mn_ref[...], dkvb).astype(BF16)
        dmn = _dot_nt(dkvb, w_ref[...])
        xf = m_ref[...]
        r = lax.rsqrt(jnp.mean(xf * xf, axis=-1, keepdims=True) + EPS)
        dg_ref[...] = jnp.sum(dmn * xf * r, axis=0, keepdims=True)

    return pl.pallas_call(
        body, name="mem_bwd",
        out_shape=[jax.ShapeDtypeStruct((D_MODEL, 2 * WIDTH), BF16), jax.ShapeDtypeStruct((1, D_MODEL), F32)],
        compiler_params=pltpu.CompilerParams(vmem_limit_bytes=VMEM_LIMIT),
    )(mem, g, mem_n, w, dkv)


def _dh_bwd(dpb, wbr, dpg, wg, x, g, dxo, parts=()):
    S = x.shape[0]
    tm = 1024
    tkb, tkg = D_BRANCHES // 4, D_GATES // 4
    nkb = 4
    nk = 8
    ni = S // tm
    n = len(parts)

    def body(dpb_ref, wbr_ref, dpg_ref, wg_ref, x_ref, g_ref, dxo_ref, *rest):
        p_in = rest[:n]
        dx_ref, dg_ref = rest[n:n + 2]
        p_out, acc, sems = rest[n + 2:2 * n + 2], rest[2 * n + 2], rest[2 * n + 3:]
        i, kk = pl.program_id(0), pl.program_id(1)

        @pl.when(jnp.logical_and(i == 0, kk == 0))
        def _():
            dg_ref[...] = jnp.zeros_like(dg_ref)
            if n:
                _comm_start(_rs_second(p_in, p_out, *sems))

        @pl.when(kk == 0)
        def _():
            acc[...] = jnp.zeros_like(acc)

        @pl.when(kk < nkb)
        def _():
            acc[...] += _dot_nt(dpb_ref[...], wbr_ref[...])

        @pl.when(kk >= nkb)
        def _():
            acc[...] += _dot_nt(dpg_ref[...], wg_ref[...])

        @pl.when(kk == nk - 1)
        def _():
            xf = x_ref[...]
            r = lax.rsqrt(jnp.mean(xf * xf, axis=-1, keepdims=True) + EPS)
            xhat = xf * r
            dh = acc[...]
            dg_ref[...] += jnp.sum(dh * xhat, axis=0, keepdims=True)
            dxh = dh * g_ref[...]
            dx_ref[...] = dxo_ref[...] + r * (dxh - xhat * jnp.mean(dxh * xhat, axis=-1, keepdims=True))

        if n:
            @pl.when(jnp.logical_and(i == ni - 1, kk == nk - 1))
            def _():
                _comm_wait(_rs_second(p_in, p_out, *sems))

    res = pl.pallas_call(
        body, name="dh_bwd_scatter" if n else "dh_bwd",
        grid=(ni, nk),
        in_specs=[pl.BlockSpec((tm, tkb), lambda i, k: (i, jnp.minimum(k, nkb - 1))),
                  pl.BlockSpec((D_MODEL, tkb), lambda i, k: (0, jnp.minimum(k, nkb - 1))),
                  pl.BlockSpec((tm, tkg), lambda i, k: (i, jnp.maximum(k - nkb, 0))),
                  pl.BlockSpec((D_MODEL, tkg), lambda i, k: (0, jnp.maximum(k - nkb, 0))),
                  pl.BlockSpec((tm, D_MODEL), lambda i, k: (i, 0)), pl.BlockSpec((1, D_MODEL), lambda i, k: (0, 0)),
                  pl.BlockSpec((tm, D_MODEL), lambda i, k: (i, 0))] + [ANY] * n,
        out_specs=[pl.BlockSpec((tm, D_MODEL), lambda i, k: (i, 0)), pl.BlockSpec((1, D_MODEL), lambda i, k: (0, 0))]
                  + [ANY] * n,
        out_shape=[jax.ShapeDtypeStruct((S, D_MODEL), F32), jax.ShapeDtypeStruct((1, D_MODEL), F32)]
                  + [jax.ShapeDtypeStruct(p.shape, p.dtype) for p in parts],
        scratch_shapes=[pltpu.VMEM((tm, D_MODEL), F32)] + (_dma_sems(3 * n, 3 * n, n) if n else []),
        compiler_params=_params(("arbitrary", "arbitrary")),
    )(dpb, wbr, dpg, wg, x, g, dxo, *parts)
    return res[0], res[1], list(res[2:])


def _matmul_tn(a, b, tn, name, parts=()):
    K, M = a.shape
    N = b.shape[1]
    tk = 1024
    nk = K // tk
    nj = N // tn
    n = len(parts)

    def body(a_ref, b_ref, *rest):
        p_in, o_ref, p_out = rest[:n], rest[n], rest[n + 1:2 * n + 1]
        acc, sems = rest[2 * n + 1], rest[2 * n + 2:]
        j, kk = pl.program_id(0), pl.program_id(1)

        if n:
            @pl.when(jnp.logical_and(j == 0, kk == 0))
            def _():
                _comm_start(_rs_second(p_in, p_out, *sems))

        @pl.when(kk == 0)
        def _():
            acc[...] = jnp.zeros_like(acc)

        acc[...] += _dot_tn(a_ref[...].astype(BF16), b_ref[...].astype(BF16))

        @pl.when(kk == nk - 1)
        def _():
            o_ref[...] = acc[...].astype(BF16)

        if n:
            @pl.when(jnp.logical_and(j == nj - 1, kk == nk - 1))
            def _():
                _comm_wait(_rs_second(p_in, p_out, *sems))

    res = pl.pallas_call(
        body, name=name,
        grid=(nj, nk),
        in_specs=[pl.BlockSpec((tk, M), lambda j, k: (k, 0)), pl.BlockSpec((tk, tn), lambda j, k: (k, j))] + [ANY] * n,
        out_specs=[pl.BlockSpec((M, tn), lambda j, k: (0, j))] + [ANY] * n,
        out_shape=[jax.ShapeDtypeStruct((M, N), BF16)] + [jax.ShapeDtypeStruct(p.shape, p.dtype) for p in parts],
        scratch_shapes=[pltpu.VMEM((M, tn), F32)] + (_dma_sems(3 * n, 3 * n, n) if n else []),
        compiler_params=_params(("arbitrary", "arbitrary")),
    )(a, b, *parts)
    return (res[0], list(res[1:])) if n else res[0]


def _row_tile(R, C):
    for cand in (512, 256, 128, 64, 32, 16, 8):
        if R % cand == 0 and cand * C * 4 <= (2 << 20):
            return cand
    return R


def _adamw_update(p_ref, w_ref, m_ref, v_ref, g_ref, d_ref, nm_ref, nv_ref):
    c1 = 1.0 / (1.0 - ADAM_B1 ** ADAM_STEP)
    c2 = 1.0 / (1.0 - ADAM_B2 ** ADAM_STEP)
    g = p_ref[0].astype(F32)
    for k in range(1, p_ref.shape[0]):
        g = g + p_ref[k].astype(F32)
    nm = ADAM_B1 * m_ref[...] + (1.0 - ADAM_B1) * g
    nv = ADAM_B2 * v_ref[...] + (1.0 - ADAM_B2) * (g * g)
    g_ref[...] = g
    nm_ref[...] = nm
    nv_ref[...] = nv
    d_ref[...] = -ADAM_LR * ((nm * c1) / (jnp.sqrt(nv * c2) + ADAM_EPS) + ADAM_WD * w_ref[...])


def _adamw(parts, w, m, v, name):
    P, R, C = parts.shape
    tr = _row_tile(R, C)

    def body(*refs):
        _adamw_update(*refs)

    spec = pl.BlockSpec((tr, C), lambda i: (i, 0))
    return pl.pallas_call(
        body, name=name,
        grid=(R // tr,),
        in_specs=[pl.BlockSpec((P, tr, C), lambda i: (0, i, 0)), spec, spec, spec],
        out_specs=[spec] * 4,
        out_shape=[jax.ShapeDtypeStruct((R, C), F32)] * 4,
        compiler_params=_params(("parallel",)),
    )(parts, w, m, v)


def _adamw_layers(parts, w, m, v, name):
    depth = len(parts)
    P, R, C = parts[0].shape
    tr = _row_tile(R, C)

    def body(*refs):
        layer = pl.program_id(0)
        for k in range(depth):
            @pl.when(layer == k)
            def _(k=k):
                _adamw_update(refs[k], *refs[depth:])

    def part_spec(k):
        return pl.BlockSpec((P, tr, C), lambda l, i: (0, jnp.where(l == k, i, 0), 0))

    spec = pl.BlockSpec((None, tr, C), lambda l, i: (l, i, 0))
    return pl.pallas_call(
        body, name=name,
        grid=(depth, R // tr),
        in_specs=[part_spec(k) for k in range(depth)] + [spec] * 3,
        out_specs=[spec] * 4,
        out_shape=[jax.ShapeDtypeStruct((depth, R, C), F32)] * 4,
        compiler_params=_params(("arbitrary", "arbitrary")),
    )(*parts, w, m, v)


def _place():
    return lax.axis_index("x"), lax.axis_index("y"), lax.axis_index("c")


def _all_gather(shards):
    n = len(shards)

    def body(*refs):
        ins, outs = refs[:n], refs[n:2 * n]
        send1, recv1, local_sems, send2, recv2 = refs[2 * n:]
        first = _ag_first(ins, outs, send1, recv1, local_sems)
        second = _ag_second(outs, send2, recv2)
        _comm_start(first)
        for j in range(3):
            for a in range(n):
                first[2][4 * a + 1 + j].wait_recv()
            for a in range(n):
                second[1][3 * a + j].start()
        for a in range(n):
            first[2][4 * a].wait_recv()
        for cp in second[2]:
            cp.wait_recv()
        for cp in first[1] + second[1]:
            cp.wait_send()
        for cp in first[0]:
            cp.wait()

    return pl.pallas_call(
        body, name="weights_all_gather",
        in_specs=[ANY] * n, out_specs=[ANY] * n,
        out_shape=[jax.ShapeDtypeStruct((N_DEV,) + s.shape, s.dtype) for s in shards],
        scratch_shapes=_dma_sems(4 * n, 4 * n, n, 3 * n, 3 * n),
        compiler_params=pltpu.CompilerParams(has_side_effects=True),
    )(*shards)


N_BIG = 4


def _dev(p):
    return 4 * p[0] + 2 * p[1] + p[2]


def _other_chips(x, y):
    return [(1 - x, y), (x, 1 - y), (1 - x, 1 - y)]


def _remote(src, dst, send_sems, recv_sems, k, to):
    return pltpu.make_async_remote_copy(src_ref=src, dst_ref=dst, send_sem=send_sems.at[k], recv_sem=recv_sems.at[k],
                                        device_id=to, device_id_type=MESH)


def _ag_first(ins, outs, send_sems, recv_sems, local_sems):
    x, y, c = _place()
    me = (x, y, c)
    targets = [(x, y, 1 - c)] + [(*chip, c) for chip in _other_chips(x, y)]
    local, out, inc = [], [], []
    for a in range(len(ins)):
        local.append(pltpu.make_async_copy(ins[a], outs[a].at[_dev(me)], local_sems.at[a]))
        for k, to in enumerate(targets):
            out.append(_remote(ins[a], outs[a].at[_dev(me)], send_sems, recv_sems, 4 * a + k, to))
            inc.append(_remote(ins[a], outs[a].at[_dev(to)], send_sems, recv_sems, 4 * a + k, to))
    return local, out, inc


def _ag_second(bufs, send_sems, recv_sems):
    x, y, c = _place()
    out, inc = [], []
    for a in range(len(bufs)):
        for j, chip in enumerate(_other_chips(x, y)):
            mine, theirs = bufs[a].at[_dev((*chip, c))], bufs[a].at[_dev((*chip, 1 - c))]
            out.append(_remote(mine, mine, send_sems, recv_sems, 3 * a + j, (x, y, 1 - c)))
            inc.append(_remote(theirs, theirs, send_sems, recv_sems, 3 * a + j, (x, y, 1 - c)))
    return [], out, inc


def _rs_first(ins, outs, send_sems, recv_sems):
    x, y, c = _place()
    out = [_remote(ins[a].at[1 - c], outs[a], send_sems, recv_sems, a, (x, y, 1 - c)) for a in range(len(ins))]
    return [], out, out


def _rs_second(ins, outs, send_sems, recv_sems, local_sems):
    x, y, c = _place()
    my_chip = 2 * x + y
    local, out, inc = [], [], []
    for a in range(len(ins)):
        local.append(pltpu.make_async_copy(ins[a].at[my_chip], outs[a].at[my_chip], local_sems.at[a]))
        for k, (ox, oy) in enumerate(_other_chips(x, y)):
            out.append(_remote(ins[a].at[2 * ox + oy], outs[a].at[my_chip], send_sems, recv_sems, 3 * a + k, (ox, oy, c)))
            inc.append(_remote(ins[a].at[2 * ox + oy], outs[a].at[2 * ox + oy], send_sems, recv_sems, 3 * a + k,
                               (ox, oy, c)))
    return local, out, inc


def _comm_start(exchange):
    local, out, _ = exchange
    for cp in local + out:
        cp.start()


def _comm_wait(exchange):
    local, out, inc = exchange
    for cp in inc:
        cp.wait_recv()
    for cp in out:
        cp.wait_send()
    for cp in local:
        cp.wait()


def _dma_sems(*counts):
    return [pltpu.SemaphoreType.DMA((n,)) for n in counts]


def _rs_sibling(grads):
    n = len(grads)

    def body(*refs):
        ex = _rs_first(refs[:n], refs[n:2 * n], *refs[2 * n:])
        _comm_start(ex)
        _comm_wait(ex)

    return pl.pallas_call(
        body, name="grads_to_sibling",
        in_specs=[ANY] * n, out_specs=[ANY] * n,
        out_shape=[jax.ShapeDtypeStruct(g.shape[1:], g.dtype) for g in grads],
        scratch_shapes=_dma_sems(n, n),
        compiler_params=pltpu.CompilerParams(has_side_effects=True),
    )(*grads)


def _pair_sum(grads, recvs):
    n = len(grads)

    def body(c_ref, *refs):
        for a in range(n):
            refs[2 * n + a][...] = (refs[a][...].astype(F32) + refs[n + a][...].astype(F32)).astype(BF16)

    def g_spec(g):
        return pl.BlockSpec((None, None) + g.shape[2:], lambda j, c_ref: (c_ref[0], j, 0, 0))

    def r_spec(r):
        return pl.BlockSpec((None,) + r.shape[1:], lambda j, c_ref: (j, 0, 0))

    return pl.pallas_call(
        body, name="pair_sum",
        grid_spec=pltpu.PrefetchScalarGridSpec(
            num_scalar_prefetch=1, grid=(N_CHIP,),
            in_specs=[g_spec(g) for g in grads] + [r_spec(r) for r in recvs],
            out_specs=[r_spec(r) for r in recvs]),
        out_shape=[jax.ShapeDtypeStruct(r.shape, BF16) for r in recvs],
        compiler_params=_params(("parallel",)),
    )(lax.axis_index("c").reshape(1).astype(jnp.int32), *grads, *recvs)


def _rs_chips(parts):
    n = len(parts)

    def body(*refs):
        ex = _rs_second(refs[:n], refs[n:2 * n], *refs[2 * n:])
        _comm_start(ex)
        _comm_wait(ex)

    return pl.pallas_call(
        body, name="grads_to_chips",
        in_specs=[ANY] * n, out_specs=[ANY] * n,
        out_shape=[jax.ShapeDtypeStruct(p.shape, p.dtype) for p in parts],
        scratch_shapes=_dma_sems(3 * n, 3 * n, n),
        compiler_params=pltpu.CompilerParams(has_side_effects=True),
    )(*parts)


SMALL_ROWS = 544


def _all_reduce_small(buf):
    def body(in_ref, out_ref, recv, acc, send1, recv1, send2, recv2):
        x, y, c = _place()
        me = 4 * x + 2 * y + c
        peers = [(x ^ (r >> 2), y ^ ((r >> 1) & 1), c ^ (r & 1)) for r in range(1, N_DEV)]

        def idx(p):
            return 4 * p[0] + 2 * p[1] + p[2]

        first = [pltpu.make_async_remote_copy(
            src_ref=in_ref.at[idx(p)], dst_ref=recv.at[me], send_sem=send1.at[r], recv_sem=recv1.at[r],
            device_id=p, device_id_type=MESH) for r, p in enumerate(peers)]
        for cp in first:
            cp.start()
        recv[me] = in_ref[me]
        for r, p in enumerate(peers):
            pltpu.make_async_remote_copy(
                src_ref=in_ref.at[idx(p)], dst_ref=recv.at[idx(p)], send_sem=send1.at[r], recv_sem=recv1.at[r],
                device_id=p, device_id_type=MESH).wait_recv()
        total = recv[0]
        for k in range(1, N_DEV):
            total = total + recv[k]
        acc[...] = total
        out_ref[me] = total
        second = [pltpu.make_async_remote_copy(
            src_ref=acc, dst_ref=out_ref.at[me], send_sem=send2.at[r], recv_sem=recv2.at[r],
            device_id=p, device_id_type=MESH) for r, p in enumerate(peers)]
        for cp in second:
            cp.start()
        for r, p in enumerate(peers):
            pltpu.make_async_remote_copy(
                src_ref=acc, dst_ref=out_ref.at[idx(p)], send_sem=send2.at[r], recv_sem=recv2.at[r],
                device_id=p, device_id_type=MESH).wait_recv()
        for cp in first + second:
            cp.wait_send()

    vm = pl.BlockSpec(memory_space=pltpu.VMEM)
    return pl.pallas_call(
        body, name="small_grads_all_reduce",
        in_specs=[vm], out_specs=vm,
        out_shape=jax.ShapeDtypeStruct(buf.shape, F32),
        scratch_shapes=[pltpu.VMEM(buf.shape, F32), pltpu.VMEM(buf.shape[1:], F32),
                        pltpu.SemaphoreType.DMA((7,)), pltpu.SemaphoreType.DMA((7,)),
                        pltpu.SemaphoreType.DMA((7,)), pltpu.SemaphoreType.DMA((7,))],
        compiler_params=pltpu.CompilerParams(has_side_effects=True, vmem_limit_bytes=VMEM_LIMIT),
    )(buf)


def _dilate(a, d):
    if d == 1:
        return a
    S, C = a.shape
    return a.reshape(S // d, d, C).transpose(1, 0, 2).reshape(S, C)


def _undilate(a, d):
    if d == 1:
        return a
    S, C = a.shape
    return a.reshape(d, S // d, C).transpose(1, 0, 2).reshape(S, C)


def _cols(a, cb, n=1):
    return a[:, cb * WIDTH:(cb + n) * WIDTH]


def _to_blocks(g, kind):
    if kind == "cols":
        R = g.shape[0]
        return g.reshape(R, N_CHIP, 2, -1).transpose(2, 1, 0, 3)
    if kind == "rows":
        C = g.shape[1]
        return g.reshape(N_CHIP, 2, -1, C).transpose(1, 0, 2, 3)
    return g.reshape(4 * WIDTH, N_CHIP, 2, -1).transpose(2, 1, 0, 3)


SMALL = ("norm_g", "gm_ln_g", "gm_ln_b", "gm_ws", "gm_bs", "pool_w", "pool_scale", "mem_norm_g", "final_norm_g")


def _pack_small(tree):
    flat = jnp.concatenate([tree[k].reshape(-1, 128) for k in SMALL], axis=0)
    return jnp.pad(flat, ((0, N_DEV * SMALL_ROWS - flat.shape[0]), (0, 0)))


def _unpack_small(flat, like):
    out, at = {}, 0
    for k in SMALL:
        rows = like[k].size // 128
        out[k] = flat[at:at + rows].reshape(like[k].shape)
        at += rows
    return out


def _make_layer(wbr, wg, wkv, wb, wout, norm_g, mem_norm_g, ln_g, ln_b, gm_ws, gm_bs, pool_w, pool_scale):
    tril = jnp.tril(jnp.ones((CHUNK, CHUNK), bool))
    wsm = jnp.where(tril, gm_ws, 0.0).astype(BF16)
    pw = pool_w.astype(BF16)
    return dict(wbr=wbr, wg=wg, wkv=wkv, wb=wb, wout=wout, g=norm_g[None], mg=mem_norm_g[None], ln_g=ln_g[None],
                ln_b=ln_b[None], wsm=wsm, wsm_t=wsm.transpose(0, 2, 1), pw=pw, pw_t=pw.transpose(0, 2, 1),
                ps=pool_scale[None], bias=jnp.repeat(gm_bs.T, HEAD, axis=1))


def _layer_fwd(xl, mem0, L, next_shards=()):
    S = xl.shape[0]
    proj, gates, h, half_gathered = _in_proj(xl, L["g"], L["wbr"], L["wg"], next_shards[:1])
    kv, mem_n = _mem_kv(mem0, L["mg"], L["wkv"])
    y4, gathered = _abm_fwd(proj, L["ln_g"], L["ln_b"], L["wsm"], L["bias"], L["pw"], L["ps"], kv, half_gathered)
    o_g, l_g = [], []
    for gi, d in enumerate(DILATIONS):
        if d == 1:
            o, lse = _attn_fwd(proj, CB_Q0, proj, CB_K, proj, CB_CV, S // CHUNK)
        else:
            o, lse = _attn_fwd_dilated(proj, CB_Q0 + gi, CB_K, CB_CV, d)
        o_g.append(o)
        l_g.append(lse)
    (xn, y4, oc, lse, z), rest = _merge_fwd(xl, y4, o_g, l_g, proj, gates, L["wb"], L["wout"], next_shards[1:])
    saved = dict(x=xl, proj=proj, gates=gates, h=h, kv=kv, mem_n=mem_n, y4=y4, oc=oc, lse=lse, z=z)
    return xn, saved, gathered + rest


def _place_cols(dst, piece, cb):
    return lax.dynamic_update_slice(dst, piece, (0, cb * WIDTH))


def _layer_bwd(dx, mem0, L, sv, later=()):
    S = dx.shape[0]
    proj = sv["proj"]
    (dy3, doc, delta, dcg, dgm, dt), from_sibling = _merge_bwd(dx, sv["y4"], sv["oc"], proj, sv["gates"], L["wb"],
                                                              L["wout"], later)
    pair = _pair_sum(later, from_sibling) if later else ()
    dwout = _matmul_tn(sv["z"], dx, D_MODEL, "dw_out")
    dwb = _dw_branch(sv["y4"], dt)
    dpb, dm, dlng, dlnb, dws, dbias, dpw, dps, dkv = _abm_bwd(
        proj, dy3, L["ln_g"], L["ln_b"], L["wsm"], L["wsm_t"], L["bias"], L["pw"], L["pw_t"], L["ps"], sv["kv"])
    dk, dv = None, None
    for gi, d in enumerate(DILATIONS):
        if d == 1:
            r = _attn_bwd(proj, CB_Q0, proj, CB_K, proj, CB_CV, doc, sv["lse"], delta, S // CHUNK)
        else:
            r = _attn_bwd_dilated(proj, CB_Q0 + gi, CB_K, CB_CV, doc, sv["lse"], delta, d)
        dpb = _place_cols(dpb, r[0], CB_Q0 + gi)
        dkg, dvg = r[1].astype(F32), r[2].astype(F32)
        dk = dkg if dk is None else dk + dkg
        dv = dvg if dv is None else dv + dvg
    dpb = _place_cols(dpb, dk.astype(BF16), CB_K)
    dpb = _place_cols(dpb, dv.astype(BF16), CB_CV)
    dpb = _place_cols(dpb, dcg, CB_CGATE)
    dpb = _place_cols(dpb, dm, CB_MQ)
    dwkv, dmg = _mem_bwd(mem0, L["mg"], sv["mem_n"], L["wkv"], dkv)
    if later:
        dwin_b, parts_rest = _matmul_tn(sv["h"], dpb, D_BRANCHES // 4, "dw_in_branches_scatter", pair[1:])
    else:
        dwin_b, parts_rest = _matmul_tn(sv["h"], dpb, D_BRANCHES // 4, "dw_in_branches"), []
    dwin = jnp.concatenate([dwin_b, _matmul_tn(sv["h"], dgm, D_GATES // 4, "dw_in_gates")], axis=1)
    dxi, dng, parts = _dh_bwd(dpb, L["wbr"], dgm, L["wg"], sv["x"], L["g"], dx, pair[:1])
    parts = parts + parts_rest
    big = dict(w_in=dwin, w_mem_kv=dwkv, w_branch=dwb, w_out=dwout)
    small = dict(norm_g=dng[0], gm_ln_g=dlng[0], gm_ln_b=dlnb[0], gm_ws=dws,
                 gm_bs=_bias_reduce(dbias)[:, :N_HEAD].T, pool_w=dpw, pool_scale=dps[0], mem_norm_g=dmg[0])
    return dxi, big, small, parts


BIG = ("w_in", "w_mem_kv", "w_branch", "w_out")


def _blocked(big):
    return [_to_blocks(big["w_in"], "cols"), _to_blocks(big["w_mem_kv"], "rows"),
            _to_blocks(big["w_branch"], "branch"), _to_blocks(big["w_out"], "rows")]


def _full_weights(gathered):
    win, wkv, wb, wout = gathered
    cut = D_BRANCHES - 4 * win.shape[2]
    wbr = jnp.concatenate([win[0], win[1], win[2], win[3], win[4][:, :cut]], axis=1)
    wg = jnp.concatenate([win[4][:, cut:], win[5], win[6], win[7]], axis=1)
    return (wbr, wg, wkv.reshape(D_MODEL, 2 * WIDTH),
            wb.reshape(N_DEV, 4, WIDTH, -1).transpose(1, 2, 0, 3).reshape(4, WIDTH, D_MODEL),
            wout.reshape(D_MODEL, D_MODEL))


def kernel(x, mem, norm_g, w_in, gm_ln_g, gm_ln_b, gm_ws, gm_bs, pool_w, pool_scale, mem_norm_g, w_mem_kv, w_branch, w_out, final_norm_g, loss_target, m_norm_g, m_w_in, m_gm_ln_g, m_gm_ln_b, m_gm_ws, m_gm_bs, m_pool_w, m_pool_scale, m_mem_norm_g, m_w_mem_kv, m_w_branch, m_w_out, m_final_norm_g, v_norm_g, v_w_in, v_gm_ln_g, v_gm_ln_b, v_gm_ws, v_gm_bs, v_pool_w, v_pool_scale, v_mem_norm_g, v_w_mem_kv, v_w_branch, v_w_out, v_final_norm_g):
    x0 = x[0]
    mem0 = mem[0]
    tgt = loss_target[0]
    S = x0.shape[0]

    shards = [[w_in[l].astype(BF16), w_mem_kv[l].astype(BF16), w_branch[l].astype(BF16).reshape(4 * WIDTH, -1),
               w_out[l].astype(BF16)] for l in range(DEPTH)]
    gathered = _all_gather(shards[0])
    layers, saved = [], []
    xl = x0
    for l in range(DEPTH):
        layers.append(_make_layer(*_full_weights(gathered), norm_g[l], mem_norm_g[l], gm_ln_g[l], gm_ln_b[l],
                                  gm_ws[l], gm_bs[l], pool_w[l], pool_scale[l]))
        xl, sv, gathered = _layer_fwd(xl, mem0, layers[l], shards[l + 1] if l + 1 < DEPTH else ())
        saved.append(sv)

    loss_part, dx, d_final = _loss_head(xl, final_norm_g[None], tgt)
    loss = lax.psum(loss_part[0, 0], ("x", "y", "c"))

    small = {k: [None] * DEPTH for k in SMALL if k != "final_norm_g"}
    parts = [None] * DEPTH
    later = ()
    for l in reversed(range(DEPTH)):
        dx, gb, gs, done = _layer_bwd(dx, mem0, layers[l], saved[l], later)
        if later:
            parts[l + 1] = done
        later = _blocked(gb)
        for k in gs:
            small[k][l] = gs[k]
    grad_x = dx[None]
    parts[0] = _rs_chips(_pair_sum(later, _rs_sibling(later)))

    small_tree = {k: jnp.stack(small[k]) for k in small}
    small_tree["final_norm_g"] = d_final[0]
    reduced = _all_reduce_small(_pack_small(small_tree).reshape(N_DEV, SMALL_ROWS, 128))

    weights = dict(norm_g=norm_g, w_in=w_in, gm_ln_g=gm_ln_g, gm_ln_b=gm_ln_b, gm_ws=gm_ws, gm_bs=gm_bs,
                   pool_w=pool_w, pool_scale=pool_scale, mem_norm_g=mem_norm_g, w_mem_kv=w_mem_kv,
                   w_branch=w_branch, w_out=w_out, final_norm_g=final_norm_g)
    m_in = dict(norm_g=m_norm_g, w_in=m_w_in, gm_ln_g=m_gm_ln_g, gm_ln_b=m_gm_ln_b, gm_ws=m_gm_ws, gm_bs=m_gm_bs,
                pool_w=m_pool_w, pool_scale=m_pool_scale, mem_norm_g=m_mem_norm_g, w_mem_kv=m_w_mem_kv,
                w_branch=m_w_branch, w_out=m_w_out, final_norm_g=m_final_norm_g)
    v_in = dict(norm_g=v_norm_g, w_in=v_w_in, gm_ln_g=v_gm_ln_g, gm_ln_b=v_gm_ln_b, gm_ws=v_gm_ws, gm_bs=v_gm_bs,
                pool_w=v_pool_w, pool_scale=v_pool_scale, mem_norm_g=v_mem_norm_g, w_mem_kv=v_w_mem_kv,
                w_branch=v_w_branch, w_out=v_w_out, final_norm_g=v_final_norm_g)
    res = {}
    for a, k in enumerate(BIG):
        shape = weights[k].shape
        by_layer = [parts[l][a] for l in range(DEPTH)]
        lrc = (DEPTH,) + by_layer[0].shape[1:]
        outs = _adamw_layers(by_layer, weights[k].reshape(lrc), m_in[k].reshape(lrc), v_in[k].reshape(lrc),
                             "adamw_" + k)
        res[k] = [o.reshape(shape) for o in outs]
    outs = _adamw(reduced.reshape(1, N_DEV * SMALL_ROWS, 128), _pack_small(weights), _pack_small(m_in),
                  _pack_small(v_in), "adamw_small")
    unpacked = [_unpack_small(o, weights) for o in outs]
    for k in SMALL:
        res[k] = [u[k] for u in unpacked]

    order = ("norm_g", "w_in", "gm_ln_g", "gm_ln_b", "gm_ws", "gm_bs", "pool_w", "pool_scale", "mem_norm_g",
             "w_mem_kv", "w_branch", "w_out", "final_norm_g")
    return (loss, grad_x, *[res[k][0] for k in order], *[res[k][1] for k in order],
            *[res[k][2] for k in order], *[res[k][3] for k in order])
```

```python
import functools
import math

import jax
import jax.numpy as jnp
from jax import lax
from jax.experimental import pallas as pl
from jax.experimental.pallas import tpu as pltpu

F32 = jnp.float32
BF16 = jnp.bfloat16

D_MODEL = 1024
DEPTH = 4
WIDTH = 512
D_IN = 10752
HEAD = 128
N_HEAD = 4
CHUNK = 128
MEM_LEN = 256
POOL_WINDOWS = (2, 4, 8, 16)
DILATIONS = (1, 4, 16)
EPS = 1e-6
NEG = -1e30
ATT_SCALE = HEAD ** -0.5
N_DEV = 8
N_CHIP = 4

D_BRANCHES = 6656
D_GATES = D_IN - D_BRANCHES
CB_U, CB_V, CB_AGATE, CB_PIN, CB_PGATE = 0, 1, 2, 3, 4
CB_Q0, CB_K, CB_CV, CB_CGATE, CB_MQ, CB_MGATE = 5, 8, 9, 10, 11, 12

ADAM_LR = 0.001
ADAM_B1 = 0.9
ADAM_B2 = 0.999
ADAM_EPS = 1e-08
ADAM_WD = 0.01
ADAM_STEP = 10

VMEM_LIMIT = 56 * 1024 * 1024
MESH = pl.DeviceIdType.MESH
ANY = pl.BlockSpec(memory_space=pl.ANY)

NT = (((1,), (1,)), ((), ()))
TN = (((0,), (0,)), ((), ()))


def _dot(a, b):
    return jnp.dot(a, b, preferred_element_type=F32)


def _dot_nt(a, b):
    return lax.dot_general(a, b, NT, preferred_element_type=F32)


def _dot_tn(a, b):
    return lax.dot_general(a, b, TN, preferred_element_type=F32)


def _sigmoid(x):
    return 1.0 / (1.0 + jnp.exp(-x))


def _silu(x):
    return x * _sigmoid(x)


def _dsilu(x):
    s = _sigmoid(x)
    return s * (1.0 + x * (1.0 - s))


def _gelu(x):
    return 0.5 * x * (1.0 + lax.erf(x * (2.0 ** -0.5)))


def _dgelu(x):
    return 0.5 * (1.0 + lax.erf(x * (2.0 ** -0.5))) + x * jnp.exp(-0.5 * x * x) * (1.0 / math.sqrt(2.0 * math.pi))


def _col(blk, h):
    lane = lax.broadcasted_iota(jnp.int32, blk.shape, 1)
    return jnp.sum(jnp.where(lane == h, blk, 0.0), axis=1, keepdims=True)


def _put_cols(cols):
    rows = cols[0].shape[0]
    lane = lax.broadcasted_iota(jnp.int32, (rows, 128), 1)
    out = jnp.zeros((rows, 128), F32)
    for h, cv in enumerate(cols):
        out = jnp.where(lane == h, cv, out)
    return out


def _params(sem, vmem=VMEM_LIMIT):
    return pltpu.CompilerParams(dimension_semantics=sem, vmem_limit_bytes=vmem)


def _full(shape):
    nd = len(shape)
    return pl.BlockSpec(shape, lambda *_: (0,) * nd)


def _rows(tm, width, cb=0):
    return pl.BlockSpec((tm, width), lambda i: (i, cb))


def _in_proj(x, g, wbr, wg, shards=()):
    S = x.shape[0]
    tm, tnb, tng = 1024, D_BRANCHES // 4, D_GATES // 4
    njb = 4
    n = len(shards)
    ni, nj = S // tm, 2 * njb

    def body(x_ref, g_ref, wbr_ref, wg_ref, *rest):
        ins, (proj_ref, gates_ref, h_ref), outs = rest[:n], rest[n:n + 3], rest[n + 3:2 * n + 3]
        hs, sems = rest[2 * n + 3], rest[2 * n + 4:]
        i, j = pl.program_id(0), pl.program_id(1)

        if n:
            @pl.when(jnp.logical_and(i == 0, j == 0))
            def _():
                _comm_start(_ag_first(ins, outs, *sems))

        @pl.when(j == 0)
        def _():
            xf = x_ref[...]
            r = lax.rsqrt(jnp.mean(xf * xf, axis=-1, keepdims=True) + EPS)
            h = (xf * r * g_ref[...]).astype(BF16)
            hs[...] = h
            h_ref[...] = h

        @pl.when(j < njb)
        def _():
            proj_ref[...] = _dot(hs[...], wbr_ref[...]).astype(BF16)

        @pl.when(j >= njb)
        def _():
            gates_ref[...] = _dot(hs[...], wg_ref[...]).astype(BF16)

        if n:
            @pl.when(jnp.logical_and(i == ni - 1, j == nj - 1))
            def _():
                _comm_wait(_ag_first(ins, outs, *sems))

    def first(j):
        return jnp.minimum(j, njb - 1)

    def second(j):
        return jnp.maximum(j - njb, 0)

    res = pl.pallas_call(
        body, name="in_proj_gather" if n else "in_proj",
        grid=(ni, nj),
        in_specs=[pl.BlockSpec((tm, D_MODEL), lambda i, j: (i, 0)),
                  pl.BlockSpec((1, D_MODEL), lambda i, j: (0, 0)),
                  pl.BlockSpec((D_MODEL, tnb), lambda i, j: (0, first(j))),
                  pl.BlockSpec((D_MODEL, tng), lambda i, j: (0, second(j)))] + [ANY] * n,
        out_specs=[pl.BlockSpec((tm, tnb), lambda i, j: (i, first(j))),
                   pl.BlockSpec((tm, tng), lambda i, j: (i, second(j))),
                   pl.BlockSpec((tm, D_MODEL), lambda i, j: (i, 0))] + [ANY] * n,
        out_shape=[jax.ShapeDtypeStruct((S, D_BRANCHES), BF16), jax.ShapeDtypeStruct((S, D_GATES), BF16),
                   jax.ShapeDtypeStruct((S, D_MODEL), BF16)]
                  + [jax.ShapeDtypeStruct((N_DEV,) + s.shape, s.dtype) for s in shards],
        scratch_shapes=[pltpu.VMEM((tm, D_MODEL), BF16)] + (_dma_sems(4 * n, 4 * n, n) if n else []),
        compiler_params=_params(("arbitrary", "arbitrary")),
    )(x, g, wbr, wg, *shards)
    return res[0], res[1], res[2], list(res[3:])


def _mem_kv(mem, g, w):
    M = mem.shape[0]

    def body(m_ref, g_ref, w_ref, kv_ref, mn_ref):
        xf = m_ref[...]
        r = lax.rsqrt(jnp.mean(xf * xf, axis=-1, keepdims=True) + EPS)
        mn = (xf * r * g_ref[...]).astype(BF16)
        mn_ref[...] = mn
        kv_ref[...] = _dot(mn, w_ref[...]).astype(BF16)

    return pl.pallas_call(
        body, name="mem_kv",
        out_shape=[jax.ShapeDtypeStruct((M, 2 * WIDTH), BF16), jax.ShapeDtypeStruct((M, D_MODEL), BF16)],
        compiler_params=pltpu.CompilerParams(vmem_limit_bytes=VMEM_LIMIT),
    )(mem, g, w)


def _band_masks(win):
    t = lax.broadcasted_iota(jnp.int32, (CHUNK, CHUNK), 0)
    s = lax.broadcasted_iota(jnp.int32, (CHUNK, CHUNK), 1)
    cur = jnp.logical_and(t - s >= 0, t - s < win)
    prev = s > t + CHUNK - win
    return cur.astype(BF16), prev.astype(BF16)


def _inv_count(first_row, win):
    t = first_row + lax.broadcasted_iota(jnp.int32, (CHUNK, 1), 0)
    return 1.0 / jnp.minimum(t + 1, win).astype(F32)


def _layer_norm_fwd(v):
    mu = jnp.mean(v, axis=-1, keepdims=True)
    vc = v - mu
    var = jnp.mean(vc * vc, axis=-1, keepdims=True)
    rstd = lax.rsqrt(var + EPS)
    return vc * rstd, rstd


def _mem_softmax(q, kmem):
    s = _dot_nt(q, kmem) * ATT_SCALE
    m = jnp.max(s, axis=-1, keepdims=True)
    e = jnp.exp(s - m)
    return e * (1.0 / jnp.sum(e, axis=-1, keepdims=True))


def _abm_fwd(proj, ln_g, ln_b, wsm, bias_full, pool_w, pool_scale, kv, gathered=()):
    S = proj.shape[0]
    tm = 512
    nchunk = tm // CHUNK
    n = len(gathered)
    nsteps = S // tm

    def body(u_ref, v_ref, ag_ref, p_ref, ph_ref, pg_ref, mq_ref, mg_ref, lng_ref, lnb_ref, wsm_ref, bias_ref,
             pw_ref, ps_ref, kv_ref, *rest):
        y_ref, bufs, mix, sems = rest[n], rest[n + 1:2 * n + 1], rest[2 * n + 1], rest[2 * n + 2:]
        i = pl.program_id(0)

        if n:
            @pl.when(i == 0)
            def _():
                _comm_start(_ag_second(bufs, *sems))

        u = _gelu(u_ref[...].astype(F32))
        v = _gelu(v_ref[...].astype(F32))
        vhat, _ = _layer_norm_fwd(v)
        vln = (vhat * lng_ref[...] + lnb_ref[...]).astype(BF16)
        for c in range(nchunk):
            for h in range(N_HEAD):
                rs, cs = slice(c * CHUNK, (c + 1) * CHUNK), slice(h * HEAD, (h + 1) * HEAD)
                mix[rs, cs] = _dot(wsm_ref[h], vln[rs, cs]) + bias_ref[:, cs]
        y_ref[0] = (u * mix[...] * _silu(ag_ref[...].astype(F32))).astype(BF16)
        halo_ok = (i > 0).astype(F32)
        for c in range(nchunk):
            rs = slice(c * CHUNK, (c + 1) * CHUNK)
            for g, win in enumerate(POOL_WINDOWS):
                cs = slice(g * HEAD, (g + 1) * HEAD)
                bcur, bprev = _band_masks(win)
                cur = p_ref[rs, cs]
                if c == 0:
                    prev = (ph_ref[:, cs].astype(F32) * halo_ok).astype(BF16)
                else:
                    prev = p_ref[(c - 1) * CHUNK:c * CHUNK, cs]
                sums = _dot(bcur, cur) + _dot(bprev, prev)
                dm = sums * _inv_count(i * tm + c * CHUNK, win) - cur.astype(F32)
                mix[rs, cs] = _dot(dm.astype(BF16), pw_ref[g])
        y_ref[1] = (mix[...] * ps_ref[...] * _silu(pg_ref[...].astype(F32))).astype(BF16)
        for h in range(N_HEAD):
            cs = slice(h * HEAD, (h + 1) * HEAD)
            p = _mem_softmax(mq_ref[:, cs], kv_ref[:, cs])
            mix[:, cs] = _dot(p.astype(BF16), kv_ref[:, WIDTH + h * HEAD:WIDTH + (h + 1) * HEAD])
        y_ref[2] = (mix[...] * _silu(mg_ref[...].astype(F32))).astype(BF16)

        if n:
            @pl.when(i == nsteps - 1)
            def _():
                _comm_wait(_ag_second(bufs, *sems))

    blk = tm // CHUNK
    res = pl.pallas_call(
        body, name="abm_fwd_gather" if n else "abm_fwd",
        grid=(nsteps,),
        in_specs=[_rows(tm, WIDTH, CB_U), _rows(tm, WIDTH, CB_V), _rows(tm, WIDTH, CB_AGATE),
                  _rows(tm, WIDTH, CB_PIN),
                  pl.BlockSpec((CHUNK, WIDTH), lambda i: (jnp.maximum(i * blk - 1, 0), CB_PIN)),
                  _rows(tm, WIDTH, CB_PGATE), _rows(tm, WIDTH, CB_MQ), _rows(tm, WIDTH, CB_MGATE),
                  _full((1, WIDTH)), _full((1, WIDTH)), _full((N_HEAD, CHUNK, CHUNK)), _full((CHUNK, WIDTH)),
                  _full((4, HEAD, HEAD)), _full((1, WIDTH)), _full((MEM_LEN, 2 * WIDTH))] + [ANY] * n,
        out_specs=[pl.BlockSpec((3, tm, WIDTH), lambda i: (0, i, 0))] + [ANY] * n,
        out_shape=[jax.ShapeDtypeStruct((4, S, WIDTH), BF16)]
                  + [jax.ShapeDtypeStruct(b.shape, b.dtype) for b in gathered],
        input_output_aliases={15 + a: 1 + a for a in range(n)},
        scratch_shapes=[pltpu.VMEM((tm, WIDTH), F32)] + (_dma_sems(3 * n, 3 * n) if n else []),
        compiler_params=_params(("arbitrary",)),
    )(proj, proj, proj, proj, proj, proj, proj, proj, ln_g, ln_b, wsm, bias_full, pool_w, pool_scale, kv, *gathered)
    return res[0], list(res[1:])


ATT_TILE = 512


def _attn_fwd(q, qcb, k, kcb, v, vcb, bps):
    S = q.shape[0]
    tm = ATT_TILE
    nb = tm // CHUNK

    nblocks = nb * N_HEAD

    def body(q_ref, k_ref, v_ref, kh_ref, vh_ref, o_ref, l_ref, sc_s, sp_s, pc_s, pp_s):
        i = pl.program_id(0)

        def prev_kv(n, cs):
            if n == 0:
                return kh_ref[:, cs], vh_ref[:, cs]
            ps = slice((n - 1) * CHUNK, n * CHUNK)
            return k_ref[ps, cs], v_ref[ps, cs]

        pens = []
        for n in range(nb):
            rs = slice(n * CHUNK, (n + 1) * CHUNK)
            pens.append(jnp.full((N_HEAD * CHUNK, 1), jnp.where((i * nb + n) % bps != 0, 0.0, NEG), F32))
            for h in range(N_HEAD):
                cs = slice(h * HEAD, (h + 1) * HEAD)
                bs = slice((n * N_HEAD + h) * CHUNK, (n * N_HEAD + h + 1) * CHUNK)
                qh = q_ref[rs, cs]
                sc_s[bs, :] = _dot_nt(qh, k_ref[rs, cs])
                sp_s[bs, :] = _dot_nt(qh, prev_kv(n, cs)[0])
        row = lax.broadcasted_iota(jnp.int32, (nblocks * CHUNK, CHUNK), 0) & (CHUNK - 1)
        col = lax.broadcasted_iota(jnp.int32, (nblocks * CHUNK, CHUNK), 1)
        sc = jnp.where(col <= row, sc_s[...] * ATT_SCALE, NEG)
        sp = jnp.where(col >= row, sp_s[...] * ATT_SCALE, NEG) + jnp.concatenate(pens, axis=0)
        m = jnp.maximum(jnp.max(sc, axis=-1, keepdims=True), jnp.max(sp, axis=-1, keepdims=True))
        ec = jnp.exp(sc - m)
        ep = jnp.exp(sp - m)
        den = jnp.sum(ec, axis=-1, keepdims=True) + jnp.sum(ep, axis=-1, keepdims=True)
        inv = 1.0 / den
        pc_s[...] = (ec * inv).astype(BF16)
        pp_s[...] = (ep * inv).astype(BF16)
        lse = m + jnp.log(den)
        for n in range(nb):
            rs = slice(n * CHUNK, (n + 1) * CHUNK)
            for h in range(N_HEAD):
                cs = slice(h * HEAD, (h + 1) * HEAD)
                bs = slice((n * N_HEAD + h) * CHUNK, (n * N_HEAD + h + 1) * CHUNK)
                o = _dot(pc_s[bs, :], v_ref[rs, cs]) + _dot(pp_s[bs, :], prev_kv(n, cs)[1])
                o_ref[rs, cs] = o.astype(BF16)
            l_ref[rs, :] = _put_cols([lse[(n * N_HEAD + h) * CHUNK:(n * N_HEAD + h + 1) * CHUNK]
                                      for h in range(N_HEAD)])

    def halo(cb):
        return pl.BlockSpec((CHUNK, WIDTH), lambda i: (jnp.maximum(i * nb - 1, 0), cb))

    return pl.pallas_call(
        body, name=f"attn_fwd_{bps}",
        grid=(S // tm,),
        in_specs=[_rows(tm, WIDTH, qcb), _rows(tm, WIDTH, kcb), _rows(tm, WIDTH, vcb), halo(kcb), halo(vcb)],
        out_specs=[_rows(tm, WIDTH), _rows(tm, 128)],
        out_shape=[jax.ShapeDtypeStruct((S, WIDTH), BF16), jax.ShapeDtypeStruct((S, 128), F32)],
        scratch_shapes=[pltpu.VMEM((nblocks * CHUNK, CHUNK), F32), pltpu.VMEM((nblocks * CHUNK, CHUNK), F32),
                        pltpu.VMEM((nblocks * CHUNK, CHUNK), BF16), pltpu.VMEM((nblocks * CHUNK, CHUNK), BF16)],
        compiler_params=_params(("parallel",)),
    )(q, k, v, k, v)


def _gate_specs(tm):
    return [pl.BlockSpec((tm, D_MODEL), lambda i, b=b: (i, b)) for b in range(4)]


Y_SLOT = (0, 1, 3, 2)


def _merge_fwd(x, y4, o_g, l_g, proj, gates, wb, wout, shards=()):
    S = x.shape[0]
    tm = 256
    n = len(shards)
    nsteps = S // tm
    forward_at = nsteps - 4

    def body(x_ref, y_ref, o0, o1, o2, l0, l1, l2, cg_ref, *rest):
        gm = rest[:4]
        wb_ref, wo_ref = rest[4:6]
        s_in = rest[6:6 + n]
        xn_ref, yc_ref, oc_ref, lse_ref, z_ref = rest[6 + n:11 + n]
        s_out, ocs, sems = rest[11 + n:11 + 2 * n], rest[11 + 2 * n], rest[12 + 2 * n:]
        i = pl.program_id(0)

        if n:
            @pl.when(i == 0)
            def _():
                _comm_start(_ag_first(s_in, s_out, *sems[:3]))

            @pl.when(i == forward_at)
            def _():
                incoming = _ag_first(s_in, s_out, *sems[:3])[2]
                for a in range(n):
                    for k in range(1, 4):
                        incoming[4 * a + k].wait_recv()
                _comm_start(_ag_second(s_out, *sems[3:]))

        lcols = []
        for h in range(N_HEAD):
            cs = slice(h * HEAD, (h + 1) * HEAD)
            ls = [_col(l[...], h) for l in (l0, l1, l2)]
            m = jnp.maximum(jnp.maximum(ls[0], ls[1]), ls[2])
            tot = jnp.exp(ls[0] - m) + jnp.exp(ls[1] - m) + jnp.exp(ls[2] - m)
            lse = m + jnp.log(tot)
            ocs[:, cs] = sum(jnp.exp(lg - lse) * o[:, cs].astype(F32) for lg, o in zip(ls, (o0, o1, o2)))
            lcols.append(lse)
        lse_ref[...] = _put_cols(lcols)
        oc = ocs[...]
        oc_ref[...] = oc.astype(BF16)
        yc = (oc * _silu(cg_ref[...].astype(F32))).astype(BF16)
        yc_ref[...] = yc
        ys = (y_ref[0], y_ref[1], yc, y_ref[2])
        z = jnp.zeros((tm, D_MODEL), F32)
        for b in range(4):
            z = z + _sigmoid(gm[b][...].astype(F32)) * _dot(ys[b], wb_ref[b])
        zb = z.astype(BF16)
        z_ref[...] = zb
        xn_ref[...] = x_ref[...] + _dot(zb, wo_ref[...])

        if n:
            @pl.when(i == nsteps - 1)
            def _():
                local, out, incoming = _ag_first(s_in, s_out, *sems[:3])
                for a in range(n):
                    incoming[4 * a].wait_recv()
                _comm_wait(_ag_second(s_out, *sems[3:]))
                for cp in out:
                    cp.wait_send()
                for cp in local:
                    cp.wait()

    res = pl.pallas_call(
        body, name="merge_fwd_gather" if n else "merge_fwd",
        grid=(nsteps,),
        in_specs=[_rows(tm, D_MODEL), pl.BlockSpec((3, tm, WIDTH), lambda i: (0, i, 0)),
                  _rows(tm, WIDTH), _rows(tm, WIDTH), _rows(tm, WIDTH),
                  _rows(tm, 128), _rows(tm, 128), _rows(tm, 128),
                  _rows(tm, WIDTH, CB_CGATE)] + _gate_specs(tm)
                 + [_full((4, WIDTH, D_MODEL)), _full((D_MODEL, D_MODEL))] + [ANY] * n,
        out_specs=[_rows(tm, D_MODEL), pl.BlockSpec((None, tm, WIDTH), lambda i: (Y_SLOT[2], i, 0)),
                   _rows(tm, WIDTH), _rows(tm, 128), _rows(tm, D_MODEL)] + [ANY] * n,
        out_shape=[jax.ShapeDtypeStruct((S, D_MODEL), F32), jax.ShapeDtypeStruct(y4.shape, BF16),
                   jax.ShapeDtypeStruct((S, WIDTH), BF16), jax.ShapeDtypeStruct((S, 128), F32),
                   jax.ShapeDtypeStruct((S, D_MODEL), BF16)]
                  + [jax.ShapeDtypeStruct((N_DEV,) + s.shape, s.dtype) for s in shards],
        input_output_aliases={1: 1},
        scratch_shapes=[pltpu.VMEM((tm, WIDTH), F32)] + (_dma_sems(4 * n, 4 * n, n, 3 * n, 3 * n) if n else []),
        compiler_params=_params(("arbitrary",)),
    )(x, y4, *o_g, *l_g, proj, *([gates] * 4), wb, wout, *shards)
    return res[:5], list(res[5:])


def _loss_head(x, g, tgt):
    S = x.shape[0]
    tm = 512

    def body(x_ref, g_ref, t_ref, loss_ref, dx_ref, dg_ref):
        @pl.when(pl.program_id(0) == 0)
        def _():
            loss_ref[...] = jnp.zeros_like(loss_ref)
            dg_ref[...] = jnp.zeros_like(dg_ref)

        xf = x_ref[...]
        r = lax.rsqrt(jnp.mean(xf * xf, axis=-1, keepdims=True) + EPS)
        xhat = xf * r
        gv = g_ref[...]
        err = xhat * gv - t_ref[...]
        e2 = jnp.sum(err * err, axis=-1, keepdims=True)
        loss_ref[...] += (0.5 / D_MODEL) * jnp.sum(e2, axis=0, keepdims=True)
        dy = err * (1.0 / D_MODEL)
        dg_ref[...] += jnp.sum(dy * xhat, axis=0, keepdims=True)
        dxh = dy * gv
        dx_ref[...] = r * (dxh - xhat * jnp.mean(dxh * xhat, axis=-1, keepdims=True))

    return pl.pallas_call(
        body, name="loss_head",
        grid=(S // tm,),
        in_specs=[_rows(tm, D_MODEL), _full((1, D_MODEL)), _rows(tm, D_MODEL)],
        out_specs=[_full((1, 128)), _rows(tm, D_MODEL), _full((1, D_MODEL))],
        out_shape=[jax.ShapeDtypeStruct((1, 128), F32), jax.ShapeDtypeStruct((S, D_MODEL), F32),
                   jax.ShapeDtypeStruct((1, D_MODEL), F32)],
        compiler_params=_params(("arbitrary",)),
    )(x, g, tgt)


def _merge_bwd(dxo, y4, oc, proj, gates, wb, wout, grads=()):
    S = dxo.shape[0]
    tm = 256
    n = len(grads)
    nsteps = S // tm

    def body(dx_ref, y_ref, oc_ref, cg_ref, *rest):
        gm = rest[:4]
        wb_ref, wo_ref = rest[4:6]
        g_in = rest[6:6 + n]
        dy_ref, doc_ref, delta_ref, dcg_ref, dgm_ref, dt_ref = rest[6 + n:12 + n]
        g_out, sems = rest[12 + n:12 + 2 * n], rest[12 + 2 * n:]
        i = pl.program_id(0)

        if n:
            @pl.when(i == 0)
            def _():
                _comm_start(_rs_first(g_in, g_out, *sems))

        dz = _dot_nt(dx_ref[...].astype(BF16), wo_ref[...])
        for b in range(4):
            gate = _sigmoid(gm[b][...].astype(F32))
            t = _dot(y_ref[Y_SLOT[b]], wb_ref[b])
            dgm_ref[:, b * D_MODEL:(b + 1) * D_MODEL] = (dz * t * gate * (1.0 - gate)).astype(BF16)
            dt = (dz * gate).astype(BF16)
            dt_ref[b] = dt
            dyb = _dot_nt(dt, wb_ref[b])
            if b == 2:
                cg = cg_ref[...].astype(F32)
                oc = oc_ref[...].astype(F32)
                doc = dyb * _silu(cg)
                dcg_ref[...] = (dyb * oc * _dsilu(cg)).astype(BF16)
                doc_ref[...] = doc.astype(BF16)
                prod = doc * oc
                delta_ref[...] = _put_cols([jnp.sum(prod[:, h * HEAD:(h + 1) * HEAD], axis=1, keepdims=True)
                                            for h in range(N_HEAD)])
            else:
                dy_ref[b if b < 2 else 2] = dyb.astype(BF16)

        if n:
            @pl.when(i == nsteps - 1)
            def _():
                _comm_wait(_rs_first(g_in, g_out, *sems))

    res = pl.pallas_call(
        body, name="merge_bwd_scatter" if n else "merge_bwd",
        grid=(nsteps,),
        in_specs=[_rows(tm, D_MODEL), pl.BlockSpec((4, tm, WIDTH), lambda i: (0, i, 0)),
                  _rows(tm, WIDTH), _rows(tm, WIDTH, CB_CGATE)] + _gate_specs(tm)
                 + [_full((4, WIDTH, D_MODEL)), _full((D_MODEL, D_MODEL))] + [ANY] * n,
        out_specs=[pl.BlockSpec((3, tm, WIDTH), lambda i: (0, i, 0)), _rows(tm, WIDTH), _rows(tm, 128),
                   _rows(tm, WIDTH), _rows(tm, 4 * D_MODEL), pl.BlockSpec((4, tm, D_MODEL), lambda i: (0, i, 0))]
                  + [ANY] * n,
        out_shape=[jax.ShapeDtypeStruct((3, S, WIDTH), BF16), jax.ShapeDtypeStruct((S, WIDTH), BF16),
                   jax.ShapeDtypeStruct((S, 128), F32), jax.ShapeDtypeStruct((S, WIDTH), BF16),
                   jax.ShapeDtypeStruct((S, 4 * D_MODEL), BF16), jax.ShapeDtypeStruct((4, S, D_MODEL), BF16)]
                  + [jax.ShapeDtypeStruct(g.shape[1:], g.dtype) for g in grads],
        scratch_shapes=_dma_sems(n, n) if n else [],
        compiler_params=_params(("arbitrary",)),
    )(dxo, y4, oc, proj, *([gates] * 4), wb, wout, *grads)
    return res[:6], list(res[6:])


def _dw_branch(y4, dt):
    S = y4.shape[1]
    tk = 1024
    nk = S // tk

    def body(y_ref, dt_ref, o_ref, acc):
        kk = pl.program_id(1)

        @pl.when(kk == 0)
        def _():
            acc[...] = jnp.zeros_like(acc)

        acc[...] += _dot_tn(y_ref[...], dt_ref[...])

        @pl.when(kk == nk - 1)
        def _():
            o_ref[...] = acc[...].astype(BF16)

    def slot(b):
        return jnp.where(b == 2, Y_SLOT[2], jnp.where(b == 3, Y_SLOT[3], b))

    return pl.pallas_call(
        body, name="dw_branch",
        grid=(4, nk),
        in_specs=[pl.BlockSpec((None, tk, WIDTH), lambda b, k: (slot(b), k, 0)),
                  pl.BlockSpec((None, tk, D_MODEL), lambda b, k: (b, k, 0))],
        out_specs=pl.BlockSpec((None, WIDTH, D_MODEL), lambda b, k: (b, 0, 0)),
        out_shape=jax.ShapeDtypeStruct((4, WIDTH, D_MODEL), BF16),
        scratch_shapes=[pltpu.VMEM((WIDTH, D_MODEL), F32)],
        compiler_params=_params(("parallel", "arbitrary")),
    )(y4, dt)


def _attn_bwd(q, qcb, k, kcb, v, vcb, do, lse, delta, bps):
    S = q.shape[0]
    tm = ATT_TILE
    nb = tm // CHUNK
    nblk = S // CHUNK

    ncur = nb * N_HEAD
    nprev = (nb + 1) * N_HEAD

    def body(q_ref, k_ref, v_ref, do_ref, l_ref, d_ref, kh_ref, vh_ref, qn_ref, don_ref, ln_ref, dn_ref,
             dq_ref, dk_ref, dv_ref, sc_s, sp_s, dpc_s, dpp_s, pc_s, pp_s, dsc_s, dsp_s):
        i = pl.program_id(0)

        def rows_of(n):
            if n < nb:
                rs = slice(n * CHUNK, (n + 1) * CHUNK)
                return rs, q_ref, do_ref, l_ref, d_ref
            return slice(0, CHUNK), qn_ref, don_ref, ln_ref, dn_ref

        def prev_kv(n, cs):
            if n == 0:
                return kh_ref[:, cs], vh_ref[:, cs]
            ps = slice((n - 1) * CHUNK, n * CHUNK)
            return k_ref[ps, cs], v_ref[ps, cs]

        def blk(n, h):
            return slice((n * N_HEAD + h) * CHUNK, (n * N_HEAD + h + 1) * CHUNK)

        pens, lses, deltas = [], [], []
        for n in range(nb + 1):
            rs, qr, dor, lr, dr = rows_of(n)
            gb = i * nb + n
            pen = jnp.where(gb % bps != 0, 0.0, NEG)
            if n == nb:
                pen = pen + jnp.where(gb < nblk, 0.0, NEG)
            pens.append(jnp.full((N_HEAD * CHUNK, 1), pen, F32))
            lblk, dblk = lr[rs, :], dr[rs, :]
            for h in range(N_HEAD):
                cs = slice(h * HEAD, (h + 1) * HEAD)
                qh, doh = qr[rs, cs], dor[rs, cs]
                lses.append(_col(lblk, h))
                deltas.append(_col(dblk, h))
                kp, vp = prev_kv(n, cs)
                sp_s[blk(n, h), :] = _dot_nt(qh, kp)
                dpp_s[blk(n, h), :] = _dot_nt(doh, vp)
                if n < nb:
                    sc_s[blk(n, h), :] = _dot_nt(qh, k_ref[rs, cs])
                    dpc_s[blk(n, h), :] = _dot_nt(doh, v_ref[rs, cs])
        lse = jnp.concatenate(lses, axis=0)
        delta = jnp.concatenate(deltas, axis=0)
        row = lax.broadcasted_iota(jnp.int32, (nprev * CHUNK, CHUNK), 0) & (CHUNK - 1)
        col = lax.broadcasted_iota(jnp.int32, (nprev * CHUNK, CHUNK), 1)
        sp = jnp.where(col >= row, sp_s[...] * ATT_SCALE, NEG) + jnp.concatenate(pens, axis=0)
        pp = jnp.exp(sp - lse)
        pp_s[...] = pp.astype(BF16)
        dsp_s[...] = (pp * (dpp_s[...] - delta)).astype(BF16)
        nc = ncur * CHUNK
        sc = jnp.where(col[:nc] <= row[:nc], sc_s[...] * ATT_SCALE, NEG)
        pc = jnp.exp(sc - lse[:nc])
        pc_s[...] = pc.astype(BF16)
        dsc_s[...] = (pc * (dpc_s[...] - delta[:nc])).astype(BF16)
        for n in range(nb):
            rs, qr, dor, _, _ = rows_of(n)
            rn, qnr, donr, _, _ = rows_of(n + 1)
            for h in range(N_HEAD):
                cs = slice(h * HEAD, (h + 1) * HEAD)
                kp, _ = prev_kv(n, cs)
                dq = _dot(dsc_s[blk(n, h), :], k_ref[rs, cs]) + _dot(dsp_s[blk(n, h), :], kp)
                dq_ref[rs, cs] = (dq * ATT_SCALE).astype(BF16)
                dk = _dot_tn(dsc_s[blk(n, h), :], qr[rs, cs]) + _dot_tn(dsp_s[blk(n + 1, h), :], qnr[rn, cs])
                dk_ref[rs, cs] = (dk * ATT_SCALE).astype(BF16)
                dv = _dot_tn(pc_s[blk(n, h), :], dor[rs, cs]) + _dot_tn(pp_s[blk(n + 1, h), :], donr[rn, cs])
                dv_ref[rs, cs] = dv.astype(BF16)

    def prev_halo(cb):
        return pl.BlockSpec((CHUNK, WIDTH), lambda i: (jnp.maximum(i * nb - 1, 0), cb))

    def next_halo(width, cb=0):
        return pl.BlockSpec((CHUNK, width), lambda i: (jnp.minimum(i * nb + nb, nblk - 1), cb))

    return pl.pallas_call(
        body, name=f"attn_bwd_{bps}",
        grid=(S // tm,),
        in_specs=[_rows(tm, WIDTH, qcb), _rows(tm, WIDTH, kcb), _rows(tm, WIDTH, vcb), _rows(tm, WIDTH),
                  _rows(tm, 128), _rows(tm, 128), prev_halo(kcb), prev_halo(vcb),
                  next_halo(WIDTH, qcb), next_halo(WIDTH), next_halo(128), next_halo(128)],
        out_specs=[_rows(tm, WIDTH), _rows(tm, WIDTH), _rows(tm, WIDTH)],
        out_shape=[jax.ShapeDtypeStruct((S, WIDTH), BF16)] * 3,
        scratch_shapes=[pltpu.VMEM((ncur * CHUNK, CHUNK), F32), pltpu.VMEM((nprev * CHUNK, CHUNK), F32),
                        pltpu.VMEM((ncur * CHUNK, CHUNK), F32), pltpu.VMEM((nprev * CHUNK, CHUNK), F32),
                        pltpu.VMEM((ncur * CHUNK, CHUNK), BF16), pltpu.VMEM((nprev * CHUNK, CHUNK), BF16),
                        pltpu.VMEM((ncur * CHUNK, CHUNK), BF16), pltpu.VMEM((nprev * CHUNK, CHUNK), BF16)],
        compiler_params=_params(("parallel",)),
    )(q, k, v, do, lse, delta, k, v, q, do, lse, delta)


def _dilated_split(d):
    hp = min(N_HEAD, 16 // d)
    return hp, N_HEAD // hp, HEAD * hp


def _by_class(src_ref, dst, d, hp):
    for j in range(hp):
        dst[j] = pltpu.einshape("(tr)l->(rt)l", src_ref[:, j * HEAD:(j + 1) * HEAD], r=d)


def _from_class(src, dst_ref, d, hp):
    for j in range(hp):
        dst_ref[:, j * HEAD:(j + 1) * HEAD] = pltpu.einshape("(rt)l->(tr)l", src[j].astype(BF16), r=d)


def _attn_fwd_dilated(proj, qcb, kcb, vcb, d):
    S = proj.shape[0]
    T = CHUNK * d
    hp, nh, cw = _dilated_split(d)
    nblocks = d * hp

    def body(q_ref, k_ref, v_ref, o_ref, l_ref, qf, kf, vf, kpf, vpf, kst, vst, of, lf, sc_s, sp_s, pc_s, pp_s):
        i, hh = pl.program_id(0), pl.program_id(1)
        _by_class(q_ref, qf, d, hp)
        _by_class(k_ref, kf, d, hp)
        _by_class(v_ref, vf, d, hp)

        @pl.when(i == 0)
        def _():
            kpf[...] = jnp.zeros_like(kpf)
            vpf[...] = jnp.zeros_like(vpf)

        @pl.when(i > 0)
        def _():
            kpf[...] = kst[hh]
            vpf[...] = vst[hh]

        def blk(ref, r, j):
            return ref[j, r * CHUNK:(r + 1) * CHUNK, :]

        def bs(r, j):
            return slice((r * hp + j) * CHUNK, (r * hp + j + 1) * CHUNK)

        for r in range(d):
            for j in range(hp):
                qb = blk(qf, r, j)
                sc_s[bs(r, j), :] = _dot_nt(qb, blk(kf, r, j))
                sp_s[bs(r, j), :] = _dot_nt(qb, blk(kpf, r, j))
        row = lax.broadcasted_iota(jnp.int32, (nblocks * CHUNK, CHUNK), 0) & (CHUNK - 1)
        col = lax.broadcasted_iota(jnp.int32, (nblocks * CHUNK, CHUNK), 1)
        sc = jnp.where(col <= row, sc_s[...] * ATT_SCALE, NEG)
        sp = jnp.where(col >= row, sp_s[...] * ATT_SCALE, NEG) + jnp.where(i > 0, 0.0, NEG)
        m = jnp.maximum(jnp.max(sc, axis=-1, keepdims=True), jnp.max(sp, axis=-1, keepdims=True))
        ec = jnp.exp(sc - m)
        ep = jnp.exp(sp - m)
        den = jnp.sum(ec, axis=-1, keepdims=True) + jnp.sum(ep, axis=-1, keepdims=True)
        inv = 1.0 / den
        pc_s[...] = (ec * inv).astype(BF16)
        pp_s[...] = (ep * inv).astype(BF16)
        lse = m + jnp.log(den)
        lane = lax.broadcasted_iota(jnp.int32, (CHUNK, 128), 1)
        for r in range(d):
            lblk = jnp.zeros((CHUNK, 128), F32)
            for j in range(hp):
                o = _dot(pc_s[bs(r, j), :], blk(vf, r, j)) + _dot(pp_s[bs(r, j), :], blk(vpf, r, j))
                of[j, r * CHUNK:(r + 1) * CHUNK, :] = o
                lblk = jnp.where(lane == hh * hp + j, lse[bs(r, j)], lblk)
            lf[r * CHUNK:(r + 1) * CHUNK, :] = lblk
        _from_class(of, o_ref, d, hp)
        lnat = pltpu.einshape("(rt)l->(tr)l", lf[...], r=d)

        @pl.when(hh == 0)
        def _():
            l_ref[...] = lnat

        @pl.when(hh > 0)
        def _():
            l_ref[...] += lnat

        kst[hh] = kf[...]
        vst[hh] = vf[...]

    def cols(cb):
        return pl.BlockSpec((T, cw), lambda i, hh: (i, cb * nh + hh))

    tile = pltpu.VMEM((hp, T, HEAD), BF16)
    return pl.pallas_call(
        body, name=f"attn_fwd_dilated_{d}",
        grid=(S // T, nh),
        in_specs=[cols(qcb), cols(kcb), cols(vcb)],
        out_specs=[cols(0), pl.BlockSpec((T, 128), lambda i, hh: (i, 0))],
        out_shape=[jax.ShapeDtypeStruct((S, WIDTH), BF16), jax.ShapeDtypeStruct((S, 128), F32)],
        scratch_shapes=[tile, tile, tile, tile, tile,
                        pltpu.VMEM((nh, hp, T, HEAD), BF16), pltpu.VMEM((nh, hp, T, HEAD), BF16),
                        pltpu.VMEM((hp, T, HEAD), F32), pltpu.VMEM((T, 128), F32),
                        pltpu.VMEM((nblocks * CHUNK, CHUNK), F32), pltpu.VMEM((nblocks * CHUNK, CHUNK), F32),
                        pltpu.VMEM((nblocks * CHUNK, CHUNK), BF16), pltpu.VMEM((nblocks * CHUNK, CHUNK), BF16)],
        compiler_params=_params(("arbitrary", "arbitrary")),
    )(proj, proj, proj)


def _attn_bwd_dilated(proj, qcb, kcb, vcb, do, lse, delta, d):
    S = proj.shape[0]
    T = CHUNK * d
    nt = S // T
    hp, nh, cw = _dilated_split(d)
    nblocks = d * hp

    def body(q_ref, k_ref, v_ref, do_ref, l_ref, d_ref, dq_ref, dk_ref, dv_ref,
             qf, dof, kf, vf, kpf, vpf, dqf, acck, accv, newk, newv,
             sc_s, sp_s, dpc_s, dpp_s, pc_s, pp_s, dsc_s, dsp_s):
        hh, i = pl.program_id(0), pl.program_id(1)

        @pl.when(i == 0)
        def _():
            for ref in (kpf, vpf, acck, accv):
                ref[...] = jnp.zeros_like(ref)
            dk_ref[...] = jnp.zeros_like(dk_ref)
            dv_ref[...] = jnp.zeros_like(dv_ref)

        def blk(ref, r, j):
            return ref[j, r * CHUNK:(r + 1) * CHUNK, :]

        def bs(r, j):
            return slice((r * hp + j) * CHUNK, (r * hp + j + 1) * CHUNK)

        @pl.when(i < nt)
        def _():
            _by_class(q_ref, qf, d, hp)
            _by_class(do_ref, dof, d, hp)
            _by_class(k_ref, kf, d, hp)
            _by_class(v_ref, vf, d, hp)
            lses, deltas = [], []
            lcls = pltpu.einshape("(tr)l->(rt)l", l_ref[...], r=d)
            dcls = pltpu.einshape("(tr)l->(rt)l", d_ref[...], r=d)
            for r in range(d):
                lblk = lcls[r * CHUNK:(r + 1) * CHUNK]
                dblk = dcls[r * CHUNK:(r + 1) * CHUNK]
                for j in range(hp):
                    lses.append(_col(lblk, hh * hp + j))
                    deltas.append(_col(dblk, hh * hp + j))
                    qb, dob = blk(qf, r, j), blk(dof, r, j)
                    sc_s[bs(r, j), :] = _dot_nt(qb, blk(kf, r, j))
                    dpc_s[bs(r, j), :] = _dot_nt(dob, blk(vf, r, j))
                    sp_s[bs(r, j), :] = _dot_nt(qb, blk(kpf, r, j))
                    dpp_s[bs(r, j), :] = _dot_nt(dob, blk(vpf, r, j))
            lse = jnp.concatenate(lses, axis=0)
            delta = jnp.concatenate(deltas, axis=0)
            row = lax.broadcasted_iota(jnp.int32, (nblocks * CHUNK, CHUNK), 0) & (CHUNK - 1)
            col = lax.broadcasted_iota(jnp.int32, (nblocks * CHUNK, CHUNK), 1)
            sp = jnp.where(col >= row, sp_s[...] * ATT_SCALE, NEG) + jnp.where(i > 0, 0.0, NEG)
            pp = jnp.exp(sp - lse)
            pp_s[...] = pp.astype(BF16)
            dsp_s[...] = (pp * (dpp_s[...] - delta)).astype(BF16)
            sc = jnp.where(col <= row, sc_s[...] * ATT_SCALE, NEG)
            pc = jnp.exp(sc - lse)
            pc_s[...] = pc.astype(BF16)
            dsc_s[...] = (pc * (dpc_s[...] - delta)).astype(BF16)
            for r in range(d):
                rows = slice(r * CHUNK, (r + 1) * CHUNK)
                for j in range(hp):
                    qb, dob = blk(qf, r, j), blk(dof, r, j)
                    dsc, dsp = dsc_s[bs(r, j), :], dsp_s[bs(r, j), :]
                    dqf[j, rows, :] = (_dot(dsc, blk(kf, r, j)) + _dot(dsp, blk(kpf, r, j))) * ATT_SCALE
                    newk[j, rows, :] = _dot_tn(dsc, qb) * ATT_SCALE
                    newv[j, rows, :] = _dot_tn(pc_s[bs(r, j), :], dob)
                    acck[j, rows, :] += _dot_tn(dsp, qb) * ATT_SCALE
                    accv[j, rows, :] += _dot_tn(pp_s[bs(r, j), :], dob)
            _from_class(dqf, dq_ref, d, hp)

        @pl.when(i > 0)
        def _():
            _from_class(acck, dk_ref, d, hp)
            _from_class(accv, dv_ref, d, hp)

        @pl.when(i < nt)
        def _():
            acck[...] = newk[...]
            accv[...] = newv[...]
            kpf[...] = kf[...]
            vpf[...] = vf[...]

    def cur(width, cb, nsplit):
        return pl.BlockSpec((T, width), lambda hh, i: (jnp.minimum(i, nt - 1), cb * nsplit + hh * (nsplit > 1)))

    def lag():
        return pl.BlockSpec((T, cw), lambda hh, i: (jnp.maximum(i - 1, 0), hh))

    tile = pltpu.VMEM((hp, T, HEAD), BF16)
    acc = pltpu.VMEM((hp, T, HEAD), F32)
    f32s = pltpu.VMEM((nblocks * CHUNK, CHUNK), F32)
    b16s = pltpu.VMEM((nblocks * CHUNK, CHUNK), BF16)
    return pl.pallas_call(
        body, name=f"attn_bwd_dilated_{d}",
        grid=(nh, nt + 1),
        in_specs=[cur(cw, qcb, nh), cur(cw, kcb, nh), cur(cw, vcb, nh), cur(cw, 0, nh), cur(128, 0, 1), cur(128, 0, 1)],
        out_specs=[cur(cw, 0, nh), lag(), lag()],
        out_shape=[jax.ShapeDtypeStruct((S, WIDTH), BF16)] * 3,
        scratch_shapes=[tile] * 6 + [acc] * 5 + [f32s] * 4 + [b16s] * 4,
        compiler_params=_params(("arbitrary", "arbitrary")),
    )(proj, proj, proj, do, lse, delta)


def _abm_bwd(proj, dy3, ln_g, ln_b, wsm, wsm_t, bias_full, pool_w, pool_wt, pool_scale, kv):
    S = proj.shape[0]
    tm = 512
    nchunk = tm // CHUNK
    nblk = S // CHUNK

    def body(u_ref, v_ref, ag_ref, p_ref, ph_ref, pg_ref, pgn_ref, mq_ref, mg_ref, dy_ref, dypn_ref,
             lng_ref, lnb_ref, wsm_ref, wsmt_ref, bias_ref, pw_ref, pwt_ref, ps_ref, kv_ref,
             dab_ref, dm_ref, dlng_ref, dlnb_ref, dws_ref, dbias_ref, dpw_ref, dps_ref, dkv_ref,
             mix, dvl, ddn):
        i = pl.program_id(0)

        @pl.when(i == 0)
        def _():
            for r in (dlng_ref, dlnb_ref, dws_ref, dbias_ref, dpw_ref, dps_ref, dkv_ref):
                r[...] = jnp.zeros_like(r)

        au = u_ref[...].astype(F32)
        av = v_ref[...].astype(F32)
        ag = ag_ref[...].astype(F32)
        u = _gelu(au)
        v = _gelu(av)
        vhat, rstd = _layer_norm_fwd(v)
        vln = (vhat * lng_ref[...] + lnb_ref[...]).astype(BF16)
        for c in range(nchunk):
            for h in range(N_HEAD):
                rs, cs = slice(c * CHUNK, (c + 1) * CHUNK), slice(h * HEAD, (h + 1) * HEAD)
                mix[rs, cs] = _dot(wsm_ref[h], vln[rs, cs]) + bias_ref[:, cs]
        dya = dy_ref[0].astype(F32)
        sg = _silu(ag)
        mixed = mix[...]
        dab_ref[:, 2 * WIDTH:3 * WIDTH] = (dya * u * mixed * _dsilu(ag)).astype(BF16)
        dab_ref[:, 0:WIDTH] = (dya * mixed * sg * _dgelu(au)).astype(BF16)
        dmixed = dya * u * sg
        dmb = dmixed.astype(BF16)
        tril = (lax.broadcasted_iota(jnp.int32, (CHUNK, CHUNK), 1)
                <= lax.broadcasted_iota(jnp.int32, (CHUNK, CHUNK), 0))
        for c in range(nchunk):
            rs = slice(c * CHUNK, (c + 1) * CHUNK)
            dbias_ref[...] += dmixed[rs, :]
            for h in range(N_HEAD):
                cs = slice(h * HEAD, (h + 1) * HEAD)
                dvl[rs, cs] = _dot(wsmt_ref[h], dmb[rs, cs])
                dws_ref[h] += jnp.where(tril, _dot_nt(dmb[rs, cs], vln[rs, cs]), 0.0)
        dvln = dvl[...]
        dlng_ref[...] += jnp.sum(dvln * vhat, axis=0, keepdims=True)
        dlnb_ref[...] += jnp.sum(dvln, axis=0, keepdims=True)
        dvh = dvln * lng_ref[...]
        dv = rstd * (dvh - jnp.mean(dvh, axis=-1, keepdims=True)
                     - vhat * jnp.mean(dvh * vhat, axis=-1, keepdims=True))
        dab_ref[:, WIDTH:2 * WIDTH] = (dv * _dgelu(av)).astype(BF16)

        halo_ok = (i > 0).astype(F32)
        for c in range(nchunk):
            rs = slice(c * CHUNK, (c + 1) * CHUNK)
            for g, win in enumerate(POOL_WINDOWS):
                cs = slice(g * HEAD, (g + 1) * HEAD)
                bcur, bprev = _band_masks(win)
                cur = p_ref[rs, cs]
                if c == 0:
                    prev = (ph_ref[:, cs].astype(F32) * halo_ok).astype(BF16)
                else:
                    prev = p_ref[(c - 1) * CHUNK:c * CHUNK, cs]
                sums = _dot(bcur, cur) + _dot(bprev, prev)
                dvl[rs, cs] = sums * _inv_count(i * tm + c * CHUNK, win) - cur.astype(F32)
        dmat = dvl[...].astype(BF16)
        for g in range(4):
            cs = slice(g * HEAD, (g + 1) * HEAD)
            mix[:, cs] = _dot(dmat[:, cs], pw_ref[g])
        yg = mix[...]
        pg = pg_ref[...].astype(F32)
        dyp = dy_ref[1].astype(F32)
        dyy = dyp * _silu(pg)
        scale = ps_ref[...]
        dab_ref[:, 4 * WIDTH:5 * WIDTH] = (dyp * yg * scale * _dsilu(pg)).astype(BF16)
        dps_ref[...] += jnp.sum(dyy * yg, axis=0, keepdims=True)
        dyg = (dyy * scale).astype(BF16)
        for g in range(4):
            cs = slice(g * HEAD, (g + 1) * HEAD)
            dpw_ref[g] += _dot_tn(dmat[:, cs], dyg[:, cs])
            mix[:, cs] = _dot(dyg[:, cs], pwt_ref[g])
        next_ok = (i + 1 < S // tm).astype(F32)
        dygn = (dypn_ref[...].astype(F32) * _silu(pgn_ref[...].astype(F32)) * scale * next_ok).astype(BF16)
        for c in range(nchunk + 1):
            for g, win in enumerate(POOL_WINDOWS):
                cs = slice(g * HEAD, (g + 1) * HEAD)
                if c < nchunk:
                    dd = mix[c * CHUNK:(c + 1) * CHUNK, cs]
                else:
                    dd = _dot(dygn[:, cs], pwt_ref[g])
                ddn[c * CHUNK:(c + 1) * CHUNK, cs] = dd * _inv_count(i * tm + c * CHUNK, win)
        ddnb = ddn[...].astype(BF16)
        for c in range(nchunk):
            rs = slice(c * CHUNK, (c + 1) * CHUNK)
            ns = slice((c + 1) * CHUNK, (c + 2) * CHUNK)
            for g, win in enumerate(POOL_WINDOWS):
                cs = slice(g * HEAD, (g + 1) * HEAD)
                bcur, bprev = _band_masks(win)
                dp = _dot_tn(bcur, ddnb[rs, cs]) + _dot_tn(bprev, ddnb[ns, cs]) - mix[rs, cs]
                dab_ref[rs, 3 * WIDTH + g * HEAD:3 * WIDTH + (g + 1) * HEAD] = dp.astype(BF16)

        mg = mg_ref[...].astype(F32)
        dym = dy_ref[2].astype(F32)
        dob = (dym * _silu(mg)).astype(BF16)
        for h in range(N_HEAD):
            cs = slice(h * HEAD, (h + 1) * HEAD)
            vs = slice(WIDTH + h * HEAD, WIDTH + (h + 1) * HEAD)
            qh = mq_ref[:, cs]
            p = _mem_softmax(qh, kv_ref[:, cs])
            pb = p.astype(BF16)
            mix[:, cs] = _dot(pb, kv_ref[:, vs])
            dp = _dot_nt(dob[:, cs], kv_ref[:, vs])
            ds = (p * (dp - jnp.sum(p * dp, axis=-1, keepdims=True))).astype(BF16)
            dm_ref[:, cs] = (_dot(ds, kv_ref[:, cs]) * ATT_SCALE).astype(BF16)
            dkv_ref[:, cs] += _dot_tn(ds, qh) * ATT_SCALE
            dkv_ref[:, vs] += _dot_tn(pb, dob[:, cs])
        dm_ref[:, WIDTH:2 * WIDTH] = (dym * mix[...] * _dsilu(mg)).astype(BF16)

    blk = tm // CHUNK
    small = [_full((1, WIDTH)), _full((1, WIDTH)), _full((N_HEAD, CHUNK, CHUNK)), _full((CHUNK, WIDTH)),
             _full((4, HEAD, HEAD)), _full((1, WIDTH)), _full((MEM_LEN, 2 * WIDTH))]
    return pl.pallas_call(
        body, name="abm_bwd",
        grid=(S // tm,),
        in_specs=[_rows(tm, WIDTH, CB_U), _rows(tm, WIDTH, CB_V), _rows(tm, WIDTH, CB_AGATE),
                  _rows(tm, WIDTH, CB_PIN),
                  pl.BlockSpec((CHUNK, WIDTH), lambda i: (jnp.maximum(i * blk - 1, 0), CB_PIN)),
                  _rows(tm, WIDTH, CB_PGATE),
                  pl.BlockSpec((CHUNK, WIDTH), lambda i: (jnp.minimum(i * blk + blk, nblk - 1), CB_PGATE)),
                  _rows(tm, WIDTH, CB_MQ), _rows(tm, WIDTH, CB_MGATE),
                  pl.BlockSpec((3, tm, WIDTH), lambda i: (0, i, 0)),
                  pl.BlockSpec((None, CHUNK, WIDTH), lambda i: (1, jnp.minimum(i * blk + blk, nblk - 1), 0)),
                  _full((1, WIDTH)), _full((1, WIDTH)), _full((N_HEAD, CHUNK, CHUNK)), _full((N_HEAD, CHUNK, CHUNK)),
                  _full((CHUNK, WIDTH)), _full((4, HEAD, HEAD)), _full((4, HEAD, HEAD)), _full((1, WIDTH)),
                  _full((MEM_LEN, 2 * WIDTH))],
        out_specs=[_rows(tm, 5 * WIDTH), _rows(tm, 2 * WIDTH)] + small,
        out_shape=[jax.ShapeDtypeStruct((S, D_BRANCHES), BF16), jax.ShapeDtypeStruct((S, 2 * WIDTH), BF16),
                   jax.ShapeDtypeStruct((1, WIDTH), F32), jax.ShapeDtypeStruct((1, WIDTH), F32),
                   jax.ShapeDtypeStruct((N_HEAD, CHUNK, CHUNK), F32), jax.ShapeDtypeStruct((CHUNK, WIDTH), F32),
                   jax.ShapeDtypeStruct((4, HEAD, HEAD), F32), jax.ShapeDtypeStruct((1, WIDTH), F32),
                   jax.ShapeDtypeStruct((MEM_LEN, 2 * WIDTH), F32)],
        scratch_shapes=[pltpu.VMEM((tm, WIDTH), F32), pltpu.VMEM((tm, WIDTH), F32),
                        pltpu.VMEM((tm + CHUNK, WIDTH), F32)],
        compiler_params=_params(("arbitrary",)),
    )(proj, proj, proj, proj, proj, proj, proj, proj, proj, dy3, dy3,
      ln_g, ln_b, wsm, wsm_t, bias_full, pool_w, pool_wt, pool_scale, kv)


def _bias_reduce(dbias_full):
    def body(d_ref, o_ref):
        d = d_ref[...]
        o_ref[...] = _put_cols([jnp.sum(d[:, h * HEAD:(h + 1) * HEAD], axis=1, keepdims=True) for h in range(N_HEAD)])

    return pl.pallas_call(body, name="bias_reduce", out_shape=jax.ShapeDtypeStruct((CHUNK, 128), F32))(dbias_full)


def _mem_bwd(mem, g, mem_n, w, dkv):
    def body(m_ref, g_ref, mn_ref, w_ref, dkv_ref, dw_ref, dg_ref):
        dkvb = dkv_ref[...].astype(BF16)
        dw_ref[...] = _dot_tn(mn_ref[...], dkvb).astype(BF16)
        dmn = _dot_nt(dkvb, w_ref[...])
        xf = m_ref[...]
        r = lax.rsqrt(jnp.mean(xf * xf, axis=-1, keepdims=True) + EPS)
        dg_ref[...] = jnp.sum(dmn * xf * r, axis=0, keepdims=True)

    return pl.pallas_call(
        body, name="mem_bwd",
        out_shape=[jax.ShapeDtypeStruct((D_MODEL, 2 * WIDTH), BF16), jax.ShapeDtypeStruct((1, D_MODEL), F32)],
        compiler_params=pltpu.CompilerParams(vmem_limit_bytes=VMEM_LIMIT),
    )(mem, g, mem_n, w, dkv)


def _dh_bwd(dpb, wbr, dpg, wg, x, g, dxo, parts=()):
    S = x.shape[0]
    tm = 1024
    tkb, tkg = D_BRANCHES // 4, D_GATES // 4
    nkb = 4
    nk = 8
    ni = S // tm
    n = len(parts)

    def body(dpb_ref, wbr_ref, dpg_ref, wg_ref, x_ref, g_ref, dxo_ref, *rest):
        p_in = rest[:n]
        dx_ref, dg_ref = rest[n:n + 2]
        p_out, acc, sems = rest[n + 2:2 * n + 2], rest[2 * n + 2], rest[2 * n + 3:]
        i, kk = pl.program_id(0), pl.program_id(1)

        @pl.when(jnp.logical_and(i == 0, kk == 0))
        def _():
            dg_ref[...] = jnp.zeros_like(dg_ref)
            if n:
                _comm_start(_rs_second(p_in, p_out, *sems))

        @pl.when(kk == 0)
        def _():
            acc[...] = jnp.zeros_like(acc)

        @pl.when(kk < nkb)
        def _():
            acc[...] += _dot_nt(dpb_ref[...], wbr_ref[...])

        @pl.when(kk >= nkb)
        def _():
            acc[...] += _dot_nt(dpg_ref[...], wg_ref[...])

        @pl.when(kk == nk - 1)
        def _():
            xf = x_ref[...]
            r = lax.rsqrt(jnp.mean(xf * xf, axis=-1, keepdims=True) + EPS)
            xhat = xf * r
            dh = acc[...]
            dg_ref[...] += jnp.sum(dh * xhat, axis=0, keepdims=True)
            dxh = dh * g_ref[...]
            dx_ref[...] = dxo_ref[...] + r * (dxh - xhat * jnp.mean(dxh * xhat, axis=-1, keepdims=True))

        if n:
            @pl.when(jnp.logical_and(i == ni - 1, kk == nk - 1))
            def _():
                _comm_wait(_rs_second(p_in, p_out, *sems))

    res = pl.pallas_call(
        body, name="dh_bwd_scatter" if n else "dh_bwd",
        grid=(ni, nk),
        in_specs=[pl.BlockSpec((tm, tkb), lambda i, k: (i, jnp.minimum(k, nkb - 1))),
                  pl.BlockSpec((D_MODEL, tkb), lambda i, k: (0, jnp.minimum(k, nkb - 1))),
                  pl.BlockSpec((tm, tkg), lambda i, k: (i, jnp.maximum(k - nkb, 0))),
                  pl.BlockSpec((D_MODEL, tkg), lambda i, k: (0, jnp.maximum(k - nkb, 0))),
                  pl.BlockSpec((tm, D_MODEL), lambda i, k: (i, 0)), pl.BlockSpec((1, D_MODEL), lambda i, k: (0, 0)),
                  pl.BlockSpec((tm, D_MODEL), lambda i, k: (i, 0))] + [ANY] * n,
        out_specs=[pl.BlockSpec((tm, D_MODEL), lambda i, k: (i, 0)), pl.BlockSpec((1, D_MODEL), lambda i, k: (0, 0))]
                  + [ANY] * n,
        out_shape=[jax.ShapeDtypeStruct((S, D_MODEL), F32), jax.ShapeDtypeStruct((1, D_MODEL), F32)]
                  + [jax.ShapeDtypeStruct(p.shape, p.dtype) for p in parts],
        scratch_shapes=[pltpu.VMEM((tm, D_MODEL), F32)] + (_dma_sems(3 * n, 3 * n, n) if n else []),
        compiler_params=_params(("arbitrary", "arbitrary")),
    )(dpb, wbr, dpg, wg, x, g, dxo, *parts)
    return res[0], res[1], list(res[2:])


def _matmul_tn(a, b, tn, name, parts=()):
    K, M = a.shape
    N = b.shape[1]
    tk = 1024
    nk = K // tk
    nj = N // tn
    n = len(parts)

    def body(a_ref, b_ref, *rest):
        p_in, o_ref, p_out = rest[:n], rest[n], rest[n + 1:2 * n + 1]
        acc, sems = rest[2 * n + 1], rest[2 * n + 2:]
        j, kk = pl.program_id(0), pl.program_id(1)

        if n:
            @pl.when(jnp.logical_and(j == 0, kk == 0))
            def _():
                _comm_start(_rs_second(p_in, p_out, *sems))

        @pl.when(kk == 0)
        def _():
            acc[...] = jnp.zeros_like(acc)

        acc[...] += _dot_tn(a_ref[...].astype(BF16), b_ref[...].astype(BF16))

        @pl.when(kk == nk - 1)
        def _():
            o_ref[...] = acc[...].astype(BF16)

        if n:
            @pl.when(jnp.logical_and(j == nj - 1, kk == nk - 1))
            def _():
                _comm_wait(_rs_second(p_in, p_out, *sems))

    res = pl.pallas_call(
        body, name=name,
        grid=(nj, nk),
        in_specs=[pl.BlockSpec((tk, M), lambda j, k: (k, 0)), pl.BlockSpec((tk, tn), lambda j, k: (k, j))] + [ANY] * n,
        out_specs=[pl.BlockSpec((M, tn), lambda j, k: (0, j))] + [ANY] * n,
        out_shape=[jax.ShapeDtypeStruct((M, N), BF16)] + [jax.ShapeDtypeStruct(p.shape, p.dtype) for p in parts],
        scratch_shapes=[pltpu.VMEM((M, tn), F32)] + (_dma_sems(3 * n, 3 * n, n) if n else []),
        compiler_params=_params(("arbitrary", "arbitrary")),
    )(a, b, *parts)
    return (res[0], list(res[1:])) if n else res[0]


def _row_tile(R, C, block_bytes=2 << 20):
    for cand in range(min(R, block_bytes // (C * 4)) // 8 * 8, 0, -8):
        if R % cand == 0:
            return cand
    return R


def _adamw_update(p_ref, w_ref, m_ref, v_ref, g_ref, d_ref, nm_ref, nv_ref):
    c1 = 1.0 / (1.0 - ADAM_B1 ** ADAM_STEP)
    c2 = 1.0 / (1.0 - ADAM_B2 ** ADAM_STEP)
    g = p_ref[0].astype(F32)
    for k in range(1, p_ref.shape[0]):
        g = g + p_ref[k].astype(F32)
    nm = ADAM_B1 * m_ref[...] + (1.0 - ADAM_B1) * g
    nv = ADAM_B2 * v_ref[...] + (1.0 - ADAM_B2) * (g * g)
    g_ref[...] = g
    nm_ref[...] = nm
    nv_ref[...] = nv
    d_ref[...] = -ADAM_LR * ((nm * c1) / (jnp.sqrt(nv * c2) + ADAM_EPS) + ADAM_WD * w_ref[...])


def _adamw(parts, w, m, v, name):
    P, R, C = parts.shape
    tr = _row_tile(R, C)

    def body(*refs):
        _adamw_update(*refs)

    spec = pl.BlockSpec((tr, C), lambda i: (i, 0))
    return pl.pallas_call(
        body, name=name,
        grid=(R // tr,),
        in_specs=[pl.BlockSpec((P, tr, C), lambda i: (0, i, 0)), spec, spec, spec],
        out_specs=[spec] * 4,
        out_shape=[jax.ShapeDtypeStruct((R, C), F32)] * 4,
        compiler_params=_params(("parallel",)),
    )(parts, w, m, v)


def _adamw_layers(parts, w, m, v, name):
    depth = len(parts)
    P, R, C = parts[0].shape
    tr = _row_tile(R, C, 1 << 20)

    def body(*refs):
        layer = pl.program_id(0)
        for k in range(depth):
            @pl.when(layer == k)
            def _(k=k):
                _adamw_update(refs[k], *refs[depth:])

    def part_spec(k):
        return pl.BlockSpec((P, tr, C), lambda l, i: (0, jnp.where(l == k, i, 0), 0))

    spec = pl.BlockSpec((None, tr, C), lambda l, i: (l, i, 0))
    return pl.pallas_call(
        body, name=name,
        grid=(depth, R // tr),
        in_specs=[part_spec(k) for k in range(depth)] + [spec] * 3,
        out_specs=[spec] * 4,
        out_shape=[jax.ShapeDtypeStruct((depth, R, C), F32)] * 4,
        compiler_params=_params(("arbitrary", "arbitrary")),
    )(*parts, w, m, v)


def _place():
    return lax.axis_index("x"), lax.axis_index("y"), lax.axis_index("c")


def _all_gather(shards):
    n = len(shards)

    def body(*refs):
        ins, outs = refs[:n], refs[n:2 * n]
        send1, recv1, local_sems, send2, recv2 = refs[2 * n:]
        first = _ag_first(ins, outs, send1, recv1, local_sems)
        second = _ag_second(outs, send2, recv2)
        _comm_start(first)
        for j in range(3):
            for a in range(n):
                first[2][4 * a + 1 + j].wait_recv()
            for a in range(n):
                second[1][3 * a + j].start()
        for a in range(n):
            first[2][4 * a].wait_recv()
        for cp in second[2]:
            cp.wait_recv()
        for cp in first[1] + second[1]:
            cp.wait_send()
        for cp in first[0]:
            cp.wait()

    return pl.pallas_call(
        body, name="weights_all_gather",
        in_specs=[ANY] * n, out_specs=[ANY] * n,
        out_shape=[jax.ShapeDtypeStruct((N_DEV,) + s.shape, s.dtype) for s in shards],
        scratch_shapes=_dma_sems(4 * n, 4 * n, n, 3 * n, 3 * n),
        compiler_params=pltpu.CompilerParams(has_side_effects=True),
    )(*shards)


N_BIG = 4


def _dev(p):
    return 4 * p[0] + 2 * p[1] + p[2]


def _other_chips(x, y):
    return [(1 - x, y), (x, 1 - y), (1 - x, 1 - y)]


def _remote(src, dst, send_sems, recv_sems, k, to):
    return pltpu.make_async_remote_copy(src_ref=src, dst_ref=dst, send_sem=send_sems.at[k], recv_sem=recv_sems.at[k],
                                        device_id=to, device_id_type=MESH)


def _ag_first(ins, outs, send_sems, recv_sems, local_sems):
    x, y, c = _place()
    me = (x, y, c)
    targets = [(x, y, 1 - c)] + [(*chip, c) for chip in _other_chips(x, y)]
    local, out, inc = [], [], []
    for a in range(len(ins)):
        local.append(pltpu.make_async_copy(ins[a], outs[a].at[_dev(me)], local_sems.at[a]))
        for k, to in enumerate(targets):
            out.append(_remote(ins[a], outs[a].at[_dev(me)], send_sems, recv_sems, 4 * a + k, to))
            inc.append(_remote(ins[a], outs[a].at[_dev(to)], send_sems, recv_sems, 4 * a + k, to))
    return local, out, inc


def _ag_second(bufs, send_sems, recv_sems):
    x, y, c = _place()
    out, inc = [], []
    for a in range(len(bufs)):
        for j, chip in enumerate(_other_chips(x, y)):
            mine, theirs = bufs[a].at[_dev((*chip, c))], bufs[a].at[_dev((*chip, 1 - c))]
            out.append(_remote(mine, mine, send_sems, recv_sems, 3 * a + j, (x, y, 1 - c)))
            inc.append(_remote(theirs, theirs, send_sems, recv_sems, 3 * a + j, (x, y, 1 - c)))
    return [], out, inc


def _rs_first(ins, outs, send_sems, recv_sems):
    x, y, c = _place()
    out = [_remote(ins[a].at[1 - c], outs[a], send_sems, recv_sems, a, (x, y, 1 - c)) for a in range(len(ins))]
    return [], out, out


def _rs_second(ins, outs, send_sems, recv_sems, local_sems):
    x, y, c = _place()
    my_chip = 2 * x + y
    local, out, inc = [], [], []
    for a in range(len(ins)):
        local.append(pltpu.make_async_copy(ins[a].at[my_chip], outs[a].at[my_chip], local_sems.at[a]))
        for k, (ox, oy) in enumerate(_other_chips(x, y)):
            out.append(_remote(ins[a].at[2 * ox + oy], outs[a].at[my_chip], send_sems, recv_sems, 3 * a + k, (ox, oy, c)))
            inc.append(_remote(ins[a].at[2 * ox + oy], outs[a].at[2 * ox + oy], send_sems, recv_sems, 3 * a + k,
                               (ox, oy, c)))
    return local, out, inc


def _comm_start(exchange):
    local, out, _ = exchange
    for cp in local + out:
        cp.start()


def _comm_wait(exchange):
    local, out, inc = exchange
    for cp in inc:
        cp.wait_recv()
    for cp in out:
        cp.wait_send()
    for cp in local:
        cp.wait()


def _dma_sems(*counts):
    return [pltpu.SemaphoreType.DMA((n,)) for n in counts]


def _rs_sibling(grads):
    n = len(grads)

    def body(*refs):
        ex = _rs_first(refs[:n], refs[n:2 * n], *refs[2 * n:])
        _comm_start(ex)
        _comm_wait(ex)

    return pl.pallas_call(
        body, name="grads_to_sibling",
        in_specs=[ANY] * n, out_specs=[ANY] * n,
        out_shape=[jax.ShapeDtypeStruct(g.shape[1:], g.dtype) for g in grads],
        scratch_shapes=_dma_sems(n, n),
        compiler_params=pltpu.CompilerParams(has_side_effects=True),
    )(*grads)


def _pair_sum(grads, recvs):
    n = len(grads)

    def body(c_ref, *refs):
        for a in range(n):
            refs[2 * n + a][...] = (refs[a][...].astype(F32) + refs[n + a][...].astype(F32)).astype(BF16)

    def g_spec(g):
        return pl.BlockSpec((None, None) + g.shape[2:], lambda j, c_ref: (c_ref[0], j, 0, 0))

    def r_spec(r):
        return pl.BlockSpec((None,) + r.shape[1:], lambda j, c_ref: (j, 0, 0))

    return pl.pallas_call(
        body, name="pair_sum",
        grid_spec=pltpu.PrefetchScalarGridSpec(
            num_scalar_prefetch=1, grid=(N_CHIP,),
            in_specs=[g_spec(g) for g in grads] + [r_spec(r) for r in recvs],
            out_specs=[r_spec(r) for r in recvs]),
        out_shape=[jax.ShapeDtypeStruct(r.shape, BF16) for r in recvs],
        compiler_params=_params(("parallel",)),
    )(lax.axis_index("c").reshape(1).astype(jnp.int32), *grads, *recvs)


def _rs_chips(parts):
    n = len(parts)

    def body(*refs):
        ex = _rs_second(refs[:n], refs[n:2 * n], *refs[2 * n:])
        _comm_start(ex)
        _comm_wait(ex)

    return pl.pallas_call(
        body, name="grads_to_chips",
        in_specs=[ANY] * n, out_specs=[ANY] * n,
        out_shape=[jax.ShapeDtypeStruct(p.shape, p.dtype) for p in parts],
        scratch_shapes=_dma_sems(3 * n, 3 * n, n),
        compiler_params=pltpu.CompilerParams(has_side_effects=True),
    )(*parts)


SMALL_ROWS = 544


def _all_reduce_small(buf):
    def body(in_ref, out_ref, recv, acc, send1, recv1, send2, recv2):
        x, y, c = _place()
        me = 4 * x + 2 * y + c
        peers = [(x ^ (r >> 2), y ^ ((r >> 1) & 1), c ^ (r & 1)) for r in range(1, N_DEV)]

        def idx(p):
            return 4 * p[0] + 2 * p[1] + p[2]

        first = [pltpu.make_async_remote_copy(
            src_ref=in_ref.at[idx(p)], dst_ref=recv.at[me], send_sem=send1.at[r], recv_sem=recv1.at[r],
            device_id=p, device_id_type=MESH) for r, p in enumerate(peers)]
        for cp in first:
            cp.start()
        recv[me] = in_ref[me]
        for r, p in enumerate(peers):
            pltpu.make_async_remote_copy(
                src_ref=in_ref.at[idx(p)], dst_ref=recv.at[idx(p)], send_sem=send1.at[r], recv_sem=recv1.at[r],
                device_id=p, device_id_type=MESH).wait_recv()
        total = recv[0]
        for k in range(1, N_DEV):
            total = total + recv[k]
        acc[...] = total
        out_ref[me] = total
        second = [pltpu.make_async_remote_copy(
            src_ref=acc, dst_ref=out_ref.at[me], send_sem=send2.at[r], recv_sem=recv2.at[r],
            device_id=p, device_id_type=MESH) for r, p in enumerate(peers)]
        for cp in second:
            cp.start()
        for r, p in enumerate(peers):
            pltpu.make_async_remote_copy(
                src_ref=acc, dst_ref=out_ref.at[idx(p)], send_sem=send2.at[r], recv_sem=recv2.at[r],
                device_id=p, device_id_type=MESH).wait_recv()
        for cp in first + second:
            cp.wait_send()

    vm = pl.BlockSpec(memory_space=pltpu.VMEM)
    return pl.pallas_call(
        body, name="small_grads_all_reduce",
        in_specs=[vm], out_specs=vm,
        out_shape=jax.ShapeDtypeStruct(buf.shape, F32),
        scratch_shapes=[pltpu.VMEM(buf.shape, F32), pltpu.VMEM(buf.shape[1:], F32),
                        pltpu.SemaphoreType.DMA((7,)), pltpu.SemaphoreType.DMA((7,)),
                        pltpu.SemaphoreType.DMA((7,)), pltpu.SemaphoreType.DMA((7,))],
        compiler_params=pltpu.CompilerParams(has_side_effects=True, vmem_limit_bytes=VMEM_LIMIT),
    )(buf)


def _dilate(a, d):
    if d == 1:
        return a
    S, C = a.shape
    return a.reshape(S // d, d, C).transpose(1, 0, 2).reshape(S, C)


def _undilate(a, d):
    if d == 1:
        return a
    S, C = a.shape
    return a.reshape(d, S // d, C).transpose(1, 0, 2).reshape(S, C)


def _cols(a, cb, n=1):
    return a[:, cb * WIDTH:(cb + n) * WIDTH]


def _to_blocks(g, kind):
    if kind == "cols":
        R = g.shape[0]
        return g.reshape(R, N_CHIP, 2, -1).transpose(2, 1, 3, 0)
    if kind == "rows":
        C = g.shape[1]
        return g.reshape(N_CHIP, 2, -1, C).transpose(1, 0, 2, 3)
    return g.reshape(4 * WIDTH, N_CHIP, 2, -1).transpose(2, 1, 0, 3)


SMALL = ("norm_g", "gm_ln_g", "gm_ln_b", "gm_ws", "gm_bs", "pool_w", "pool_scale", "mem_norm_g", "final_norm_g")


def _pack_small(tree):
    flat = jnp.concatenate([tree[k].reshape(-1, 128) for k in SMALL], axis=0)
    return jnp.pad(flat, ((0, N_DEV * SMALL_ROWS - flat.shape[0]), (0, 0)))


def _unpack_small(flat, like):
    out, at = {}, 0
    for k in SMALL:
        rows = like[k].size // 128
        out[k] = flat[at:at + rows].reshape(like[k].shape)
        at += rows
    return out


def _make_layer(wbr, wg, wkv, wb, wout, norm_g, mem_norm_g, ln_g, ln_b, gm_ws, gm_bs, pool_w, pool_scale):
    tril = jnp.tril(jnp.ones((CHUNK, CHUNK), bool))
    wsm = jnp.where(tril, gm_ws, 0.0).astype(BF16)
    pw = pool_w.astype(BF16)
    return dict(wbr=wbr, wg=wg, wkv=wkv, wb=wb, wout=wout, g=norm_g[None], mg=mem_norm_g[None], ln_g=ln_g[None],
                ln_b=ln_b[None], wsm=wsm, wsm_t=wsm.transpose(0, 2, 1), pw=pw, pw_t=pw.transpose(0, 2, 1),
                ps=pool_scale[None], bias=jnp.repeat(gm_bs.T, HEAD, axis=1))


def _layer_fwd(xl, mem0, L, next_shards=()):
    S = xl.shape[0]
    proj, gates, h, half_gathered = _in_proj(xl, L["g"], L["wbr"], L["wg"], next_shards[:1])
    kv, mem_n = _mem_kv(mem0, L["mg"], L["wkv"])
    y4, gathered = _abm_fwd(proj, L["ln_g"], L["ln_b"], L["wsm"], L["bias"], L["pw"], L["ps"], kv, half_gathered)
    o_g, l_g = [], []
    for gi, d in enumerate(DILATIONS):
        if d == 1:
            o, lse = _attn_fwd(proj, CB_Q0, proj, CB_K, proj, CB_CV, S // CHUNK)
        else:
            o, lse = _attn_fwd_dilated(proj, CB_Q0 + gi, CB_K, CB_CV, d)
        o_g.append(o)
        l_g.append(lse)
    (xn, y4, oc, lse, z), rest = _merge_fwd(xl, y4, o_g, l_g, proj, gates, L["wb"], L["wout"], next_shards[1:])
    saved = dict(x=xl, proj=proj, gates=gates, h=h, kv=kv, mem_n=mem_n, y4=y4, oc=oc, lse=lse, z=z)
    return xn, saved, gathered + rest


def _place_cols(dst, piece, cb):
    return lax.dynamic_update_slice(dst, piece, (0, cb * WIDTH))


def _layer_bwd(dx, mem0, L, sv, later=()):
    S = dx.shape[0]
    proj = sv["proj"]
    (dy3, doc, delta, dcg, dgm, dt), from_sibling = _merge_bwd(dx, sv["y4"], sv["oc"], proj, sv["gates"], L["wb"],
                                                              L["wout"], later)
    pair = _pair_sum(later, from_sibling) if later else ()
    dwout = _matmul_tn(sv["z"], dx, D_MODEL, "dw_out")
    dwb = _dw_branch(sv["y4"], dt)
    dpb, dm, dlng, dlnb, dws, dbias, dpw, dps, dkv = _abm_bwd(
        proj, dy3, L["ln_g"], L["ln_b"], L["wsm"], L["wsm_t"], L["bias"], L["pw"], L["pw_t"], L["ps"], sv["kv"])
    dk, dv = None, None
    for gi, d in enumerate(DILATIONS):
        if d == 1:
            r = _attn_bwd(proj, CB_Q0, proj, CB_K, proj, CB_CV, doc, sv["lse"], delta, S // CHUNK)
        else:
            r = _attn_bwd_dilated(proj, CB_Q0 + gi, CB_K, CB_CV, doc, sv["lse"], delta, d)
        dpb = _place_cols(dpb, r[0], CB_Q0 + gi)
        dkg, dvg = r[1].astype(F32), r[2].astype(F32)
        dk = dkg if dk is None else dk + dkg
        dv = dvg if dv is None else dv + dvg
    dpb = _place_cols(dpb, dk.astype(BF16), CB_K)
    dpb = _place_cols(dpb, dv.astype(BF16), CB_CV)
    dpb = _place_cols(dpb, dcg, CB_CGATE)
    dpb = _place_cols(dpb, dm, CB_MQ)
    dwkv, dmg = _mem_bwd(mem0, L["mg"], sv["mem_n"], L["wkv"], dkv)
    if later:
        dwin_b, parts_rest = _matmul_tn(sv["h"], dpb, D_BRANCHES // 4, "dw_in_branches_scatter", pair[1:])
    else:
        dwin_b, parts_rest = _matmul_tn(sv["h"], dpb, D_BRANCHES // 4, "dw_in_branches"), []
    dwin = jnp.concatenate([dwin_b, _matmul_tn(sv["h"], dgm, D_GATES // 4, "dw_in_gates")], axis=1)
    dxi, dng, parts = _dh_bwd(dpb, L["wbr"], dgm, L["wg"], sv["x"], L["g"], dx, pair[:1])
    parts = parts + parts_rest
    big = dict(w_in=dwin, w_mem_kv=dwkv, w_branch=dwb, w_out=dwout)
    small = dict(norm_g=dng[0], gm_ln_g=dlng[0], gm_ln_b=dlnb[0], gm_ws=dws,
                 gm_bs=_bias_reduce(dbias)[:, :N_HEAD].T, pool_w=dpw, pool_scale=dps[0], mem_norm_g=dmg[0])
    return dxi, big, small, parts


BIG = ("w_in", "w_mem_kv", "w_branch", "w_out")


def _blocked(big):
    return [_to_blocks(big["w_in"], "cols"), _to_blocks(big["w_mem_kv"], "rows"),
            _to_blocks(big["w_branch"], "branch"), _to_blocks(big["w_out"], "rows")]


def _full_weights(gathered):
    win, wkv, wb, wout = gathered
    cut = D_BRANCHES - 4 * win.shape[2]
    wbr = jnp.concatenate([win[0], win[1], win[2], win[3], win[4][:, :cut]], axis=1)
    wg = jnp.concatenate([win[4][:, cut:], win[5], win[6], win[7]], axis=1)
    return (wbr, wg, wkv.reshape(D_MODEL, 2 * WIDTH),
            wb.reshape(N_DEV, 4, WIDTH, -1).transpose(1, 2, 0, 3).reshape(4, WIDTH, D_MODEL),
            wout.reshape(D_MODEL, D_MODEL))


def kernel(x, mem, norm_g, w_in, gm_ln_g, gm_ln_b, gm_ws, gm_bs, pool_w, pool_scale, mem_norm_g, w_mem_kv, w_branch, w_out, final_norm_g, loss_target, m_norm_g, m_w_in, m_gm_ln_g, m_gm_ln_b, m_gm_ws, m_gm_bs, m_pool_w, m_pool_scale, m_mem_norm_g, m_w_mem_kv, m_w_branch, m_w_out, m_final_norm_g, v_norm_g, v_w_in, v_gm_ln_g, v_gm_ln_b, v_gm_ws, v_gm_bs, v_pool_w, v_pool_scale, v_mem_norm_g, v_w_mem_kv, v_w_branch, v_w_out, v_final_norm_g):
    x0 = x[0]
    mem0 = mem[0]
    tgt = loss_target[0]
    S = x0.shape[0]

    shards = [[w_in[l].astype(BF16), w_mem_kv[l].astype(BF16), w_branch[l].astype(BF16).reshape(4 * WIDTH, -1),
               w_out[l].astype(BF16)] for l in range(DEPTH)]
    gathered = _all_gather(shards[0])
    layers, saved = [], []
    xl = x0
    for l in range(DEPTH):
        layers.append(_make_layer(*_full_weights(gathered), norm_g[l], mem_norm_g[l], gm_ln_g[l], gm_ln_b[l],
                                  gm_ws[l], gm_bs[l], pool_w[l], pool_scale[l]))
        xl, sv, gathered = _layer_fwd(xl, mem0, layers[l], shards[l + 1] if l + 1 < DEPTH else ())
        saved.append(sv)

    loss_part, dx, d_final = _loss_head(xl, final_norm_g[None], tgt)
    loss = lax.psum(loss_part[0, 0], ("x", "y", "c"))

    small = {k: [None] * DEPTH for k in SMALL if k != "final_norm_g"}
    parts = [None] * DEPTH
    later = ()
    for l in reversed(range(DEPTH)):
        dx, gb, gs, done = _layer_bwd(dx, mem0, layers[l], saved[l], later)
        if later:
            parts[l + 1] = done
        later = _blocked(gb)
        for k in gs:
            small[k][l] = gs[k]
    grad_x = dx[None]
    parts[0] = _rs_chips(_pair_sum(later, _rs_sibling(later)))

    small_tree = {k: jnp.stack(small[k]) for k in small}
    small_tree["final_norm_g"] = d_final[0]
    reduced = _all_reduce_small(_pack_small(small_tree).reshape(N_DEV, SMALL_ROWS, 128))

    weights = dict(norm_g=norm_g, w_in=w_in, gm_ln_g=gm_ln_g, gm_ln_b=gm_ln_b, gm_ws=gm_ws, gm_bs=gm_bs,
                   pool_w=pool_w, pool_scale=pool_scale, mem_norm_g=mem_norm_g, w_mem_kv=w_mem_kv,
                   w_branch=w_branch, w_out=w_out, final_norm_g=final_norm_g)
    m_in = dict(norm_g=m_norm_g, w_in=m_w_in, gm_ln_g=m_gm_ln_g, gm_ln_b=m_gm_ln_b, gm_ws=m_gm_ws, gm_bs=m_gm_bs,
                pool_w=m_pool_w, pool_scale=m_pool_scale, mem_norm_g=m_mem_norm_g, w_mem_kv=m_w_mem_kv,
                w_branch=m_w_branch, w_out=m_w_out, final_norm_g=m_final_norm_g)
    v_in = dict(norm_g=v_norm_g, w_in=v_w_in, gm_ln_g=v_gm_ln_g, gm_ln_b=v_gm_ln_b, gm_ws=v_gm_ws, gm_bs=v_gm_bs,
                pool_w=v_pool_w, pool_scale=v_pool_scale, mem_norm_g=v_mem_norm_g, w_mem_kv=v_w_mem_kv,
                w_branch=v_w_branch, w_out=v_w_out, final_norm_g=v_final_norm_g)
    res = {}
    def view(k, arr):
        return arr.transpose(0, 2, 1) if k == "w_in" else arr

    for a, k in enumerate(BIG):
        shape = view(k, weights[k]).shape
        by_layer = [parts[l][a] for l in range(DEPTH)]
        lrc = (DEPTH,) + by_layer[0].shape[1:]
        outs = _adamw_layers(by_layer, view(k, weights[k]).reshape(lrc), view(k, m_in[k]).reshape(lrc),
                             view(k, v_in[k]).reshape(lrc), "adamw_" + k)
        res[k] = [view(k, o.reshape(shape)) for o in outs]
    outs = _adamw(reduced.reshape(1, N_DEV * SMALL_ROWS, 128), _pack_small(weights), _pack_small(m_in),
                  _pack_small(v_in), "adamw_small")
    unpacked = [_unpack_small(o, weights) for o in outs]
    for k in SMALL:
        res[k] = [u[k] for u in unpacked]

    order = ("norm_g", "w_in", "gm_ln_g", "gm_ln_b", "gm_ws", "gm_bs", "pool_w", "pool_scale", "mem_norm_g",
             "w_mem_kv", "w_branch", "w_out", "final_norm_g")
    return (loss, grad_x, *[res[k][0] for k in order], *[res[k][1] for k in order],
            *[res[k][2] for k in order], *[res[k][3] for k in order])
```

```python
import functools
import math

import jax
import jax.numpy as jnp
from jax import lax
from jax.experimental import pallas as pl
from jax.experimental.pallas import tpu as pltpu

F32 = jnp.float32
BF16 = jnp.bfloat16

D_MODEL = 1024
DEPTH = 4
WIDTH = 512
D_IN = 10752
HEAD = 128
N_HEAD = 4
CHUNK = 128
MEM_LEN = 256
POOL_WINDOWS = (2, 4, 8, 16)
DILATIONS = (1, 4, 16)
EPS = 1e-6
NEG = -1e30
ATT_SCALE = HEAD ** -0.5
N_DEV = 8
N_CHIP = 4

D_BRANCHES = 6656
D_GATES = D_IN - D_BRANCHES
CB_U, CB_V, CB_AGATE, CB_PIN, CB_PGATE = 0, 1, 2, 3, 4
CB_Q0, CB_K, CB_CV, CB_CGATE, CB_MQ, CB_MGATE = 5, 8, 9, 10, 11, 12

ADAM_LR = 0.001
ADAM_B1 = 0.9
ADAM_B2 = 0.999
ADAM_EPS = 1e-08
ADAM_WD = 0.01
ADAM_STEP = 10

VMEM_LIMIT = 56 * 1024 * 1024
MESH = pl.DeviceIdType.MESH
ANY = pl.BlockSpec(memory_space=pl.ANY)

NT = (((1,), (1,)), ((), ()))
TN = (((0,), (0,)), ((), ()))


def _dot(a, b):
    return jnp.dot(a, b, preferred_element_type=F32)


def _dot_nt(a, b):
    return lax.dot_general(a, b, NT, preferred_element_type=F32)


def _dot_tn(a, b):
    return lax.dot_general(a, b, TN, preferred_element_type=F32)


def _sigmoid(x):
    return 1.0 / (1.0 + jnp.exp(-x))


def _silu(x):
    return x * _sigmoid(x)


def _dsilu(x):
    s = _sigmoid(x)
    return s * (1.0 + x * (1.0 - s))


def _gelu(x):
    return 0.5 * x * (1.0 + lax.erf(x * (2.0 ** -0.5)))


def _dgelu(x):
    return 0.5 * (1.0 + lax.erf(x * (2.0 ** -0.5))) + x * jnp.exp(-0.5 * x * x) * (1.0 / math.sqrt(2.0 * math.pi))


def _col(blk, h):
    lane = lax.broadcasted_iota(jnp.int32, blk.shape, 1)
    return jnp.sum(jnp.where(lane == h, blk, 0.0), axis=1, keepdims=True)


def _put_cols(cols):
    rows = cols[0].shape[0]
    lane = lax.broadcasted_iota(jnp.int32, (rows, 128), 1)
    out = jnp.zeros((rows, 128), F32)
    for h, cv in enumerate(cols):
        out = jnp.where(lane == h, cv, out)
    return out


def _params(sem, vmem=VMEM_LIMIT):
    return pltpu.CompilerParams(dimension_semantics=sem, vmem_limit_bytes=vmem)


def _full(shape):
    nd = len(shape)
    return pl.BlockSpec(shape, lambda *_: (0,) * nd)


def _rows(tm, width, cb=0):
    return pl.BlockSpec((tm, width), lambda i: (i, cb))


GATE_TILE = 512


def _in_proj(x, g, wt, shards=()):
    S = x.shape[0]
    tm, tnb = 1024, D_BRANCHES // 4
    njb, njg = 4, D_GATES // GATE_TILE
    n = len(shards)
    ni, nj = S // tm, njb + njg

    def body(x_ref, g_ref, wbr_ref, wg_ref, *rest):
        ins, (proj_ref, gates_ref, h_ref), outs = rest[:n], rest[n:n + 3], rest[n + 3:2 * n + 3]
        hs, sems = rest[2 * n + 3], rest[2 * n + 4:]
        i, j = pl.program_id(0), pl.program_id(1)

        if n:
            @pl.when(jnp.logical_and(i == 0, j == 0))
            def _():
                _comm_start(_ag_first(ins, outs, *sems))

        @pl.when(j == 0)
        def _():
            xf = x_ref[...]
            r = lax.rsqrt(jnp.mean(xf * xf, axis=-1, keepdims=True) + EPS)
            h = (xf * r * g_ref[...]).astype(BF16)
            hs[...] = h
            h_ref[...] = h

        @pl.when(j < njb)
        def _():
            proj_ref[...] = _dot_nt(hs[...], wbr_ref[...]).astype(BF16)

        @pl.when(j >= njb)
        def _():
            gates_ref[...] = _dot_nt(hs[...], wg_ref[...]).astype(BF16)

        if n:
            @pl.when(jnp.logical_and(i == ni - 1, j == nj - 1))
            def _():
                _comm_wait(_ag_first(ins, outs, *sems))

    def first(j):
        return jnp.minimum(j, njb - 1)

    def second(j):
        return jnp.maximum(j - njb, 0)

    res = pl.pallas_call(
        body, name="in_proj_gather" if n else "in_proj",
        grid=(ni, nj),
        in_specs=[pl.BlockSpec((tm, D_MODEL), lambda i, j: (i, 0)),
                  pl.BlockSpec((1, D_MODEL), lambda i, j: (0, 0)),
                  pl.BlockSpec((tnb, D_MODEL), lambda i, j: (first(j), 0)),
                  pl.BlockSpec((GATE_TILE, D_MODEL), lambda i, j: (D_BRANCHES // GATE_TILE + second(j), 0))]
                 + [ANY] * n,
        out_specs=[pl.BlockSpec((tm, tnb), lambda i, j: (i, first(j))),
                   pl.BlockSpec((tm, GATE_TILE), lambda i, j: (i, second(j))),
                   pl.BlockSpec((tm, D_MODEL), lambda i, j: (i, 0))] + [ANY] * n,
        out_shape=[jax.ShapeDtypeStruct((S, D_BRANCHES), BF16), jax.ShapeDtypeStruct((S, D_GATES), BF16),
                   jax.ShapeDtypeStruct((S, D_MODEL), BF16)]
                  + [jax.ShapeDtypeStruct((N_DEV,) + s.shape, s.dtype) for s in shards],
        scratch_shapes=[pltpu.VMEM((tm, D_MODEL), BF16)] + (_dma_sems(4 * n, 4 * n, n) if n else []),
        compiler_params=_params(("arbitrary", "arbitrary")),
    )(x, g, wt, wt, *shards)
    return res[0], res[1], res[2], list(res[3:])


def _mem_kv(mem, g, w):
    M = mem.shape[0]

    def body(m_ref, g_ref, w_ref, kv_ref, mn_ref):
        xf = m_ref[...]
        r = lax.rsqrt(jnp.mean(xf * xf, axis=-1, keepdims=True) + EPS)
        mn = (xf * r * g_ref[...]).astype(BF16)
        mn_ref[...] = mn
        kv_ref[...] = _dot(mn, w_ref[...]).astype(BF16)

    return pl.pallas_call(
        body, name="mem_kv",
        out_shape=[jax.ShapeDtypeStruct((M, 2 * WIDTH), BF16), jax.ShapeDtypeStruct((M, D_MODEL), BF16)],
        compiler_params=pltpu.CompilerParams(vmem_limit_bytes=VMEM_LIMIT),
    )(mem, g, w)


def _band_masks(win):
    t = lax.broadcasted_iota(jnp.int32, (CHUNK, CHUNK), 0)
    s = lax.broadcasted_iota(jnp.int32, (CHUNK, CHUNK), 1)
    cur = jnp.logical_and(t - s >= 0, t - s < win)
    prev = s > t + CHUNK - win
    return cur.astype(BF16), prev.astype(BF16)


def _inv_count(first_row, win):
    t = first_row + lax.broadcasted_iota(jnp.int32, (CHUNK, 1), 0)
    return 1.0 / jnp.minimum(t + 1, win).astype(F32)


def _layer_norm_fwd(v):
    mu = jnp.mean(v, axis=-1, keepdims=True)
    vc = v - mu
    var = jnp.mean(vc * vc, axis=-1, keepdims=True)
    rstd = lax.rsqrt(var + EPS)
    return vc * rstd, rstd


def _mem_softmax(q, kmem):
    s = _dot_nt(q, kmem) * ATT_SCALE
    m = jnp.max(s, axis=-1, keepdims=True)
    e = jnp.exp(s - m)
    return e * (1.0 / jnp.sum(e, axis=-1, keepdims=True))


def _abm_fwd(proj, ln_g, ln_b, wsm, bias_full, pool_w, pool_scale, kv, gathered=()):
    S = proj.shape[0]
    tm = 512
    nchunk = tm // CHUNK
    n = len(gathered)
    nsteps = S // tm

    def body(u_ref, v_ref, ag_ref, p_ref, ph_ref, pg_ref, mq_ref, mg_ref, lng_ref, lnb_ref, wsm_ref, bias_ref,
             pw_ref, ps_ref, kv_ref, *rest):
        y_ref, bufs, mix, sems = rest[n], rest[n + 1:2 * n + 1], rest[2 * n + 1], rest[2 * n + 2:]
        i = pl.program_id(0)

        if n:
            @pl.when(i == 0)
            def _():
                _comm_start(_ag_second(bufs, *sems))

        u = _gelu(u_ref[...].astype(F32))
        v = _gelu(v_ref[...].astype(F32))
        vhat, _ = _layer_norm_fwd(v)
        vln = (vhat * lng_ref[...] + lnb_ref[...]).astype(BF16)
        for c in range(nchunk):
            for h in range(N_HEAD):
                rs, cs = slice(c * CHUNK, (c + 1) * CHUNK), slice(h * HEAD, (h + 1) * HEAD)
                mix[rs, cs] = _dot(wsm_ref[h], vln[rs, cs]) + bias_ref[:, cs]
        y_ref[0] = (u * mix[...] * _silu(ag_ref[...].astype(F32))).astype(BF16)
        halo_ok = (i > 0).astype(F32)
        for c in range(nchunk):
            rs = slice(c * CHUNK, (c + 1) * CHUNK)
            for g, win in enumerate(POOL_WINDOWS):
                cs = slice(g * HEAD, (g + 1) * HEAD)
                bcur, bprev = _band_masks(win)
                cur = p_ref[rs, cs]
                if c == 0:
                    prev = (ph_ref[:, cs].astype(F32) * halo_ok).astype(BF16)
                else:
                    prev = p_ref[(c - 1) * CHUNK:c * CHUNK, cs]
                sums = _dot(bcur, cur) + _dot(bprev, prev)
                dm = sums * _inv_count(i * tm + c * CHUNK, win) - cur.astype(F32)
                mix[rs, cs] = _dot(dm.astype(BF16), pw_ref[g])
        y_ref[1] = (mix[...] * ps_ref[...] * _silu(pg_ref[...].astype(F32))).astype(BF16)
        for h in range(N_HEAD):
            cs = slice(h * HEAD, (h + 1) * HEAD)
            p = _mem_softmax(mq_ref[:, cs], kv_ref[:, cs])
            mix[:, cs] = _dot(p.astype(BF16), kv_ref[:, WIDTH + h * HEAD:WIDTH + (h + 1) * HEAD])
        y_ref[2] = (mix[...] * _silu(mg_ref[...].astype(F32))).astype(BF16)

        if n:
            @pl.when(i == nsteps - 1)
            def _():
                _comm_wait(_ag_second(bufs, *sems))

    blk = tm // CHUNK
    res = pl.pallas_call(
        body, name="abm_fwd_gather" if n else "abm_fwd",
        grid=(nsteps,),
        in_specs=[_rows(tm, WIDTH, CB_U), _rows(tm, WIDTH, CB_V), _rows(tm, WIDTH, CB_AGATE),
                  _rows(tm, WIDTH, CB_PIN),
                  pl.BlockSpec((CHUNK, WIDTH), lambda i: (jnp.maximum(i * blk - 1, 0), CB_PIN)),
                  _rows(tm, WIDTH, CB_PGATE), _rows(tm, WIDTH, CB_MQ), _rows(tm, WIDTH, CB_MGATE),
                  _full((1, WIDTH)), _full((1, WIDTH)), _full((N_HEAD, CHUNK, CHUNK)), _full((CHUNK, WIDTH)),
                  _full((4, HEAD, HEAD)), _full((1, WIDTH)), _full((MEM_LEN, 2 * WIDTH))] + [ANY] * n,
        out_specs=[pl.BlockSpec((3, tm, WIDTH), lambda i: (0, i, 0))] + [ANY] * n,
        out_shape=[jax.ShapeDtypeStruct((4, S, WIDTH), BF16)]
                  + [jax.ShapeDtypeStruct(b.shape, b.dtype) for b in gathered],
        input_output_aliases={15 + a: 1 + a for a in range(n)},
        scratch_shapes=[pltpu.VMEM((tm, WIDTH), F32)] + (_dma_sems(3 * n, 3 * n) if n else []),
        compiler_params=_params(("arbitrary",)),
    )(proj, proj, proj, proj, proj, proj, proj, proj, ln_g, ln_b, wsm, bias_full, pool_w, pool_scale, kv, *gathered)
    return res[0], list(res[1:])


ATT_TILE = 512


def _attn_fwd(q, qcb, k, kcb, v, vcb, bps):
    S = q.shape[0]
    tm = ATT_TILE
    nb = tm // CHUNK

    nblocks = nb * N_HEAD

    def body(q_ref, k_ref, v_ref, kh_ref, vh_ref, o_ref, l_ref, sc_s, sp_s, pc_s, pp_s):
        i = pl.program_id(0)

        def prev_kv(n, cs):
            if n == 0:
                return kh_ref[:, cs], vh_ref[:, cs]
            ps = slice((n - 1) * CHUNK, n * CHUNK)
            return k_ref[ps, cs], v_ref[ps, cs]

        pens = []
        for n in range(nb):
            rs = slice(n * CHUNK, (n + 1) * CHUNK)
            pens.append(jnp.full((N_HEAD * CHUNK, 1), jnp.where((i * nb + n) % bps != 0, 0.0, NEG), F32))
            for h in range(N_HEAD):
                cs = slice(h * HEAD, (h + 1) * HEAD)
                bs = slice((n * N_HEAD + h) * CHUNK, (n * N_HEAD + h + 1) * CHUNK)
                qh = q_ref[rs, cs]
                sc_s[bs, :] = _dot_nt(qh, k_ref[rs, cs])
                sp_s[bs, :] = _dot_nt(qh, prev_kv(n, cs)[0])
        row = lax.broadcasted_iota(jnp.int32, (nblocks * CHUNK, CHUNK), 0) & (CHUNK - 1)
        col = lax.broadcasted_iota(jnp.int32, (nblocks * CHUNK, CHUNK), 1)
        sc = jnp.where(col <= row, sc_s[...] * ATT_SCALE, NEG)
        sp = jnp.where(col >= row, sp_s[...] * ATT_SCALE, NEG) + jnp.concatenate(pens, axis=0)
        m = jnp.maximum(jnp.max(sc, axis=-1, keepdims=True), jnp.max(sp, axis=-1, keepdims=True))
        ec = jnp.exp(sc - m)
        ep = jnp.exp(sp - m)
        den = jnp.sum(ec, axis=-1, keepdims=True) + jnp.sum(ep, axis=-1, keepdims=True)
        inv = 1.0 / den
        pc_s[...] = (ec * inv).astype(BF16)
        pp_s[...] = (ep * inv).astype(BF16)
        lse = m + jnp.log(den)
        for n in range(nb):
            rs = slice(n * CHUNK, (n + 1) * CHUNK)
            for h in range(N_HEAD):
                cs = slice(h * HEAD, (h + 1) * HEAD)
                bs = slice((n * N_HEAD + h) * CHUNK, (n * N_HEAD + h + 1) * CHUNK)
                o = _dot(pc_s[bs, :], v_ref[rs, cs]) + _dot(pp_s[bs, :], prev_kv(n, cs)[1])
                o_ref[rs, cs] = o.astype(BF16)
            l_ref[rs, :] = _put_cols([lse[(n * N_HEAD + h) * CHUNK:(n * N_HEAD + h + 1) * CHUNK]
                                      for h in range(N_HEAD)])

    def halo(cb):
        return pl.BlockSpec((CHUNK, WIDTH), lambda i: (jnp.maximum(i * nb - 1, 0), cb))

    return pl.pallas_call(
        body, name=f"attn_fwd_{bps}",
        grid=(S // tm,),
        in_specs=[_rows(tm, WIDTH, qcb), _rows(tm, WIDTH, kcb), _rows(tm, WIDTH, vcb), halo(kcb), halo(vcb)],
        out_specs=[_rows(tm, WIDTH), _rows(tm, 128)],
        out_shape=[jax.ShapeDtypeStruct((S, WIDTH), BF16), jax.ShapeDtypeStruct((S, 128), F32)],
        scratch_shapes=[pltpu.VMEM((nblocks * CHUNK, CHUNK), F32), pltpu.VMEM((nblocks * CHUNK, CHUNK), F32),
                        pltpu.VMEM((nblocks * CHUNK, CHUNK), BF16), pltpu.VMEM((nblocks * CHUNK, CHUNK), BF16)],
        compiler_params=_params(("parallel",)),
    )(q, k, v, k, v)


def _gate_specs(tm):
    return [pl.BlockSpec((tm, D_MODEL), lambda i, b=b: (i, b)) for b in range(4)]


Y_SLOT = (0, 1, 3, 2)


def _merge_fwd(x, y4, o_g, l_g, proj, gates, wb, wout, shards=()):
    S = x.shape[0]
    tm = 256
    n = len(shards)
    nsteps = S // tm
    forward_at = nsteps - 4

    def body(x_ref, y_ref, o0, o1, o2, l0, l1, l2, cg_ref, *rest):
        gm = rest[:4]
        wb_ref, wo_ref = rest[4:6]
        s_in = rest[6:6 + n]
        xn_ref, yc_ref, oc_ref, lse_ref, z_ref = rest[6 + n:11 + n]
        s_out, ocs, sems = rest[11 + n:11 + 2 * n], rest[11 + 2 * n], rest[12 + 2 * n:]
        i = pl.program_id(0)

        if n:
            @pl.when(i == 0)
            def _():
                _comm_start(_ag_first(s_in, s_out, *sems[:3]))

            @pl.when(i == forward_at)
            def _():
                incoming = _ag_first(s_in, s_out, *sems[:3])[2]
                for a in range(n):
                    for k in range(1, 4):
                        incoming[4 * a + k].wait_recv()
                _comm_start(_ag_second(s_out, *sems[3:]))

        lcols = []
        for h in range(N_HEAD):
            cs = slice(h * HEAD, (h + 1) * HEAD)
            ls = [_col(l[...], h) for l in (l0, l1, l2)]
            m = jnp.maximum(jnp.maximum(ls[0], ls[1]), ls[2])
            tot = jnp.exp(ls[0] - m) + jnp.exp(ls[1] - m) + jnp.exp(ls[2] - m)
            lse = m + jnp.log(tot)
            ocs[:, cs] = sum(jnp.exp(lg - lse) * o[:, cs].astype(F32) for lg, o in zip(ls, (o0, o1, o2)))
            lcols.append(lse)
        lse_ref[...] = _put_cols(lcols)
        oc = ocs[...]
        oc_ref[...] = oc.astype(BF16)
        yc = (oc * _silu(cg_ref[...].astype(F32))).astype(BF16)
        yc_ref[...] = yc
        ys = (y_ref[0], y_ref[1], yc, y_ref[2])
        z = jnp.zeros((tm, D_MODEL), F32)
        for b in range(4):
            z = z + _sigmoid(gm[b][...].astype(F32)) * _dot(ys[b], wb_ref[b])
        zb = z.astype(BF16)
        z_ref[...] = zb
        xn_ref[...] = x_ref[...] + _dot(zb, wo_ref[...])

        if n:
            @pl.when(i == nsteps - 1)
            def _():
                local, out, incoming = _ag_first(s_in, s_out, *sems[:3])
                for a in range(n):
                    incoming[4 * a].wait_recv()
                _comm_wait(_ag_second(s_out, *sems[3:]))
                for cp in out:
                    cp.wait_send()
                for cp in local:
                    cp.wait()

    res = pl.pallas_call(
        body, name="merge_fwd_gather" if n else "merge_fwd",
        grid=(nsteps,),
        in_specs=[_rows(tm, D_MODEL), pl.BlockSpec((3, tm, WIDTH), lambda i: (0, i, 0)),
                  _rows(tm, WIDTH), _rows(tm, WIDTH), _rows(tm, WIDTH),
                  _rows(tm, 128), _rows(tm, 128), _rows(tm, 128),
                  _rows(tm, WIDTH, CB_CGATE)] + _gate_specs(tm)
                 + [_full((4, WIDTH, D_MODEL)), _full((D_MODEL, D_MODEL))] + [ANY] * n,
        out_specs=[_rows(tm, D_MODEL), pl.BlockSpec((None, tm, WIDTH), lambda i: (Y_SLOT[2], i, 0)),
                   _rows(tm, WIDTH), _rows(tm, 128), _rows(tm, D_MODEL)] + [ANY] * n,
        out_shape=[jax.ShapeDtypeStruct((S, D_MODEL), F32), jax.ShapeDtypeStruct(y4.shape, BF16),
                   jax.ShapeDtypeStruct((S, WIDTH), BF16), jax.ShapeDtypeStruct((S, 128), F32),
                   jax.ShapeDtypeStruct((S, D_MODEL), BF16)]
                  + [jax.ShapeDtypeStruct((N_DEV,) + s.shape, s.dtype) for s in shards],
        input_output_aliases={1: 1},
        scratch_shapes=[pltpu.VMEM((tm, WIDTH), F32)] + (_dma_sems(4 * n, 4 * n, n, 3 * n, 3 * n) if n else []),
        compiler_params=_params(("arbitrary",)),
    )(x, y4, *o_g, *l_g, proj, *([gates] * 4), wb, wout, *shards)
    return res[:5], list(res[5:])


def _loss_head(x, g, tgt):
    S = x.shape[0]
    tm = 512

    def body(x_ref, g_ref, t_ref, loss_ref, dx_ref, dg_ref):
        @pl.when(pl.program_id(0) == 0)
        def _():
            loss_ref[...] = jnp.zeros_like(loss_ref)
            dg_ref[...] = jnp.zeros_like(dg_ref)

        xf = x_ref[...]
        r = lax.rsqrt(jnp.mean(xf * xf, axis=-1, keepdims=True) + EPS)
        xhat = xf * r
        gv = g_ref[...]
        err = xhat * gv - t_ref[...]
        e2 = jnp.sum(err * err, axis=-1, keepdims=True)
        loss_ref[...] += (0.5 / D_MODEL) * jnp.sum(e2, axis=0, keepdims=True)
        dy = err * (1.0 / D_MODEL)
        dg_ref[...] += jnp.sum(dy * xhat, axis=0, keepdims=True)
        dxh = dy * gv
        dx_ref[...] = r * (dxh - xhat * jnp.mean(dxh * xhat, axis=-1, keepdims=True))

    return pl.pallas_call(
        body, name="loss_head",
        grid=(S // tm,),
        in_specs=[_rows(tm, D_MODEL), _full((1, D_MODEL)), _rows(tm, D_MODEL)],
        out_specs=[_full((1, 128)), _rows(tm, D_MODEL), _full((1, D_MODEL))],
        out_shape=[jax.ShapeDtypeStruct((1, 128), F32), jax.ShapeDtypeStruct((S, D_MODEL), F32),
                   jax.ShapeDtypeStruct((1, D_MODEL), F32)],
        compiler_params=_params(("arbitrary",)),
    )(x, g, tgt)


def _merge_bwd(dxo, y4, oc, proj, gates, wb, wout, grads=()):
    S = dxo.shape[0]
    tm = 256
    n = len(grads)
    nsteps = S // tm

    def body(dx_ref, y_ref, oc_ref, cg_ref, *rest):
        gm = rest[:4]
        wb_ref, wo_ref = rest[4:6]
        g_in = rest[6:6 + n]
        dy_ref, doc_ref, delta_ref, dcg_ref, dgm_ref, dt_ref = rest[6 + n:12 + n]
        g_out, sems = rest[12 + n:12 + 2 * n], rest[12 + 2 * n:]
        i = pl.program_id(0)

        if n:
            @pl.when(i == 0)
            def _():
                _comm_start(_rs_first(g_in, g_out, *sems))

        dz = _dot_nt(dx_ref[...].astype(BF16), wo_ref[...])
        for b in range(4):
            gate = _sigmoid(gm[b][...].astype(F32))
            t = _dot(y_ref[Y_SLOT[b]], wb_ref[b])
            dgm_ref[:, b * D_MODEL:(b + 1) * D_MODEL] = (dz * t * gate * (1.0 - gate)).astype(BF16)
            dt = (dz * gate).astype(BF16)
            dt_ref[b] = dt
            dyb = _dot_nt(dt, wb_ref[b])
            if b == 2:
                cg = cg_ref[...].astype(F32)
                oc = oc_ref[...].astype(F32)
                doc = dyb * _silu(cg)
                dcg_ref[...] = (dyb * oc * _dsilu(cg)).astype(BF16)
                doc_ref[...] = doc.astype(BF16)
                prod = doc * oc
                delta_ref[...] = _put_cols([jnp.sum(prod[:, h * HEAD:(h + 1) * HEAD], axis=1, keepdims=True)
                                            for h in range(N_HEAD)])
            else:
                dy_ref[b if b < 2 else 2] = dyb.astype(BF16)

        if n:
            @pl.when(i == nsteps - 1)
            def _():
                _comm_wait(_rs_first(g_in, g_out, *sems))

    res = pl.pallas_call(
        body, name="merge_bwd_scatter" if n else "merge_bwd",
        grid=(nsteps,),
        in_specs=[_rows(tm, D_MODEL), pl.BlockSpec((4, tm, WIDTH), lambda i: (0, i, 0)),
                  _rows(tm, WIDTH), _rows(tm, WIDTH, CB_CGATE)] + _gate_specs(tm)
                 + [_full((4, WIDTH, D_MODEL)), _full((D_MODEL, D_MODEL))] + [ANY] * n,
        out_specs=[pl.BlockSpec((3, tm, WIDTH), lambda i: (0, i, 0)), _rows(tm, WIDTH), _rows(tm, 128),
                   _rows(tm, WIDTH), _rows(tm, 4 * D_MODEL), pl.BlockSpec((4, tm, D_MODEL), lambda i: (0, i, 0))]
                  + [ANY] * n,
        out_shape=[jax.ShapeDtypeStruct((3, S, WIDTH), BF16), jax.ShapeDtypeStruct((S, WIDTH), BF16),
                   jax.ShapeDtypeStruct((S, 128), F32), jax.ShapeDtypeStruct((S, WIDTH), BF16),
                   jax.ShapeDtypeStruct((S, 4 * D_MODEL), BF16), jax.ShapeDtypeStruct((4, S, D_MODEL), BF16)]
                  + [jax.ShapeDtypeStruct(g.shape[:1] + g.shape[2:], g.dtype) for g in grads],
        scratch_shapes=_dma_sems(N_CHIP * n, N_CHIP * n) if n else [],
        compiler_params=_params(("arbitrary",)),
    )(dxo, y4, oc, proj, *([gates] * 4), wb, wout, *grads)
    return res[:6], list(res[6:])


def _dw_branch(y4, dt):
    S = y4.shape[1]
    tk = 1024
    nk = S // tk

    def body(y_ref, dt_ref, o_ref, acc):
        kk = pl.program_id(1)

        @pl.when(kk == 0)
        def _():
            acc[...] = jnp.zeros_like(acc)

        acc[...] += _dot_tn(y_ref[...], dt_ref[...])

        @pl.when(kk == nk - 1)
        def _():
            o_ref[...] = acc[...].astype(BF16)

    def slot(b):
        return jnp.where(b == 2, Y_SLOT[2], jnp.where(b == 3, Y_SLOT[3], b))

    return pl.pallas_call(
        body, name="dw_branch",
        grid=(4, nk),
        in_specs=[pl.BlockSpec((None, tk, WIDTH), lambda b, k: (slot(b), k, 0)),
                  pl.BlockSpec((None, tk, D_MODEL), lambda b, k: (b, k, 0))],
        out_specs=pl.BlockSpec((None, WIDTH, D_MODEL), lambda b, k: (b, 0, 0)),
        out_shape=jax.ShapeDtypeStruct((4, WIDTH, D_MODEL), BF16),
        scratch_shapes=[pltpu.VMEM((WIDTH, D_MODEL), F32)],
        compiler_params=_params(("parallel", "arbitrary")),
    )(y4, dt)


def _attn_bwd(q, qcb, k, kcb, v, vcb, do, lse, delta, bps):
    S = q.shape[0]
    tm = ATT_TILE
    nb = tm // CHUNK
    nblk = S // CHUNK

    ncur = nb * N_HEAD
    nprev = (nb + 1) * N_HEAD

    def body(q_ref, k_ref, v_ref, do_ref, l_ref, d_ref, kh_ref, vh_ref, qn_ref, don_ref, ln_ref, dn_ref,
             dq_ref, dk_ref, dv_ref, sc_s, sp_s, dpc_s, dpp_s, pc_s, pp_s, dsc_s, dsp_s):
        i = pl.program_id(0)

        def rows_of(n):
            if n < nb:
                rs = slice(n * CHUNK, (n + 1) * CHUNK)
                return rs, q_ref, do_ref, l_ref, d_ref
            return slice(0, CHUNK), qn_ref, don_ref, ln_ref, dn_ref

        def prev_kv(n, cs):
            if n == 0:
                return kh_ref[:, cs], vh_ref[:, cs]
            ps = slice((n - 1) * CHUNK, n * CHUNK)
            return k_ref[ps, cs], v_ref[ps, cs]

        def blk(n, h):
            return slice((n * N_HEAD + h) * CHUNK, (n * N_HEAD + h + 1) * CHUNK)

        pens, lses, deltas = [], [], []
        for n in range(nb + 1):
            rs, qr, dor, lr, dr = rows_of(n)
            gb = i * nb + n
            pen = jnp.where(gb % bps != 0, 0.0, NEG)
            if n == nb:
                pen = pen + jnp.where(gb < nblk, 0.0, NEG)
            pens.append(jnp.full((N_HEAD * CHUNK, 1), pen, F32))
            lblk, dblk = lr[rs, :], dr[rs, :]
            for h in range(N_HEAD):
                cs = slice(h * HEAD, (h + 1) * HEAD)
                qh, doh = qr[rs, cs], dor[rs, cs]
                lses.append(_col(lblk, h))
                deltas.append(_col(dblk, h))
                kp, vp = prev_kv(n, cs)
                sp_s[blk(n, h), :] = _dot_nt(qh, kp)
                dpp_s[blk(n, h), :] = _dot_nt(doh, vp)
                if n < nb:
                    sc_s[blk(n, h), :] = _dot_nt(qh, k_ref[rs, cs])
                    dpc_s[blk(n, h), :] = _dot_nt(doh, v_ref[rs, cs])
        lse = jnp.concatenate(lses, axis=0)
        delta = jnp.concatenate(deltas, axis=0)
        row = lax.broadcasted_iota(jnp.int32, (nprev * CHUNK, CHUNK), 0) & (CHUNK - 1)
        col = lax.broadcasted_iota(jnp.int32, (nprev * CHUNK, CHUNK), 1)
        sp = jnp.where(col >= row, sp_s[...] * ATT_SCALE, NEG) + jnp.concatenate(pens, axis=0)
        pp = jnp.exp(sp - lse)
        pp_s[...] = pp.astype(BF16)
        dsp_s[...] = (pp * (dpp_s[...] - delta)).astype(BF16)
        nc = ncur * CHUNK
        sc = jnp.where(col[:nc] <= row[:nc], sc_s[...] * ATT_SCALE, NEG)
        pc = jnp.exp(sc - lse[:nc])
        pc_s[...] = pc.astype(BF16)
        dsc_s[...] = (pc * (dpc_s[...] - delta[:nc])).astype(BF16)
        for n in range(nb):
            rs, qr, dor, _, _ = rows_of(n)
            rn, qnr, donr, _, _ = rows_of(n + 1)
            for h in range(N_HEAD):
                cs = slice(h * HEAD, (h + 1) * HEAD)
                kp, _ = prev_kv(n, cs)
                dq = _dot(dsc_s[blk(n, h), :], k_ref[rs, cs]) + _dot(dsp_s[blk(n, h), :], kp)
                dq_ref[rs, cs] = (dq * ATT_SCALE).astype(BF16)
                dk = _dot_tn(dsc_s[blk(n, h), :], qr[rs, cs]) + _dot_tn(dsp_s[blk(n + 1, h), :], qnr[rn, cs])
                dk_ref[rs, cs] = (dk * ATT_SCALE).astype(BF16)
                dv = _dot_tn(pc_s[blk(n, h), :], dor[rs, cs]) + _dot_tn(pp_s[blk(n + 1, h), :], donr[rn, cs])
                dv_ref[rs, cs] = dv.astype(BF16)

    def prev_halo(cb):
        return pl.BlockSpec((CHUNK, WIDTH), lambda i: (jnp.maximum(i * nb - 1, 0), cb))

    def next_halo(width, cb=0):
        return pl.BlockSpec((CHUNK, width), lambda i: (jnp.minimum(i * nb + nb, nblk - 1), cb))

    return pl.pallas_call(
        body, name=f"attn_bwd_{bps}",
        grid=(S // tm,),
        in_specs=[_rows(tm, WIDTH, qcb), _rows(tm, WIDTH, kcb), _rows(tm, WIDTH, vcb), _rows(tm, WIDTH),
                  _rows(tm, 128), _rows(tm, 128), prev_halo(kcb), prev_halo(vcb),
                  next_halo(WIDTH, qcb), next_halo(WIDTH), next_halo(128), next_halo(128)],
        out_specs=[_rows(tm, WIDTH), _rows(tm, WIDTH), _rows(tm, WIDTH)],
        out_shape=[jax.ShapeDtypeStruct((S, WIDTH), BF16)] * 3,
        scratch_shapes=[pltpu.VMEM((ncur * CHUNK, CHUNK), F32), pltpu.VMEM((nprev * CHUNK, CHUNK), F32),
                        pltpu.VMEM((ncur * CHUNK, CHUNK), F32), pltpu.VMEM((nprev * CHUNK, CHUNK), F32),
                        pltpu.VMEM((ncur * CHUNK, CHUNK), BF16), pltpu.VMEM((nprev * CHUNK, CHUNK), BF16),
                        pltpu.VMEM((ncur * CHUNK, CHUNK), BF16), pltpu.VMEM((nprev * CHUNK, CHUNK), BF16)],
        compiler_params=_params(("parallel",)),
    )(q, k, v, do, lse, delta, k, v, q, do, lse, delta)


def _dilated_split(d):
    hp = min(N_HEAD, 16 // d)
    return hp, N_HEAD // hp, HEAD * hp


def _by_class(src_ref, dst, d, hp):
    for j in range(hp):
        dst[j] = pltpu.einshape("(tr)l->(rt)l", src_ref[:, j * HEAD:(j + 1) * HEAD], r=d)


def _from_class(src, dst_ref, d, hp):
    for j in range(hp):
        dst_ref[:, j * HEAD:(j + 1) * HEAD] = pltpu.einshape("(rt)l->(tr)l", src[j].astype(BF16), r=d)


def _attn_fwd_dilated(proj, qcb, kcb, vcb, d):
    S = proj.shape[0]
    T = CHUNK * d
    hp, nh, cw = _dilated_split(d)
    nblocks = d * hp

    def body(q_ref, k_ref, v_ref, o_ref, l_ref, qf, kf, vf, kpf, vpf, kst, vst, of, lf, sc_s, sp_s, pc_s, pp_s):
        i, hh = pl.program_id(0), pl.program_id(1)
        _by_class(q_ref, qf, d, hp)
        _by_class(k_ref, kf, d, hp)
        _by_class(v_ref, vf, d, hp)

        @pl.when(i == 0)
        def _():
            kpf[...] = jnp.zeros_like(kpf)
            vpf[...] = jnp.zeros_like(vpf)

        @pl.when(i > 0)
        def _():
            kpf[...] = kst[hh]
            vpf[...] = vst[hh]

        def blk(ref, r, j):
            return ref[j, r * CHUNK:(r + 1) * CHUNK, :]

        def bs(r, j):
            return slice((r * hp + j) * CHUNK, (r * hp + j + 1) * CHUNK)

        for r in range(d):
            for j in range(hp):
                qb = blk(qf, r, j)
                sc_s[bs(r, j), :] = _dot_nt(qb, blk(kf, r, j))
                sp_s[bs(r, j), :] = _dot_nt(qb, blk(kpf, r, j))
        row = lax.broadcasted_iota(jnp.int32, (nblocks * CHUNK, CHUNK), 0) & (CHUNK - 1)
        col = lax.broadcasted_iota(jnp.int32, (nblocks * CHUNK, CHUNK), 1)
        sc = jnp.where(col <= row, sc_s[...] * ATT_SCALE, NEG)
        sp = jnp.where(col >= row, sp_s[...] * ATT_SCALE, NEG) + jnp.where(i > 0, 0.0, NEG)
        m = jnp.maximum(jnp.max(sc, axis=-1, keepdims=True), jnp.max(sp, axis=-1, keepdims=True))
        ec = jnp.exp(sc - m)
        ep = jnp.exp(sp - m)
        den = jnp.sum(ec, axis=-1, keepdims=True) + jnp.sum(ep, axis=-1, keepdims=True)
        inv = 1.0 / den
        pc_s[...] = (ec * inv).astype(BF16)
        pp_s[...] = (ep * inv).astype(BF16)
        lse = m + jnp.log(den)
        lane = lax.broadcasted_iota(jnp.int32, (CHUNK, 128), 1)
        for r in range(d):
            lblk = jnp.zeros((CHUNK, 128), F32)
            for j in range(hp):
                o = _dot(pc_s[bs(r, j), :], blk(vf, r, j)) + _dot(pp_s[bs(r, j), :], blk(vpf, r, j))
                of[j, r * CHUNK:(r + 1) * CHUNK, :] = o
                lblk = jnp.where(lane == hh * hp + j, lse[bs(r, j)], lblk)
            lf[r * CHUNK:(r + 1) * CHUNK, :] = lblk
        _from_class(of, o_ref, d, hp)
        lnat = pltpu.einshape("(rt)l->(tr)l", lf[...], r=d)

        @pl.when(hh == 0)
        def _():
            l_ref[...] = lnat

        @pl.when(hh > 0)
        def _():
            l_ref[...] += lnat

        kst[hh] = kf[...]
        vst[hh] = vf[...]

    def cols(cb):
        return pl.BlockSpec((T, cw), lambda i, hh: (i, cb * nh + hh))

    tile = pltpu.VMEM((hp, T, HEAD), BF16)
    return pl.pallas_call(
        body, name=f"attn_fwd_dilated_{d}",
        grid=(S // T, nh),
        in_specs=[cols(qcb), cols(kcb), cols(vcb)],
        out_specs=[cols(0), pl.BlockSpec((T, 128), lambda i, hh: (i, 0))],
        out_shape=[jax.ShapeDtypeStruct((S, WIDTH), BF16), jax.ShapeDtypeStruct((S, 128), F32)],
        scratch_shapes=[tile, tile, tile, tile, tile,
                        pltpu.VMEM((nh, hp, T, HEAD), BF16), pltpu.VMEM((nh, hp, T, HEAD), BF16),
                        pltpu.VMEM((hp, T, HEAD), F32), pltpu.VMEM((T, 128), F32),
                        pltpu.VMEM((nblocks * CHUNK, CHUNK), F32), pltpu.VMEM((nblocks * CHUNK, CHUNK), F32),
                        pltpu.VMEM((nblocks * CHUNK, CHUNK), BF16), pltpu.VMEM((nblocks * CHUNK, CHUNK), BF16)],
        compiler_params=_params(("arbitrary", "arbitrary")),
    )(proj, proj, proj)


def _attn_bwd_dilated(proj, qcb, kcb, vcb, do, lse, delta, d):
    S = proj.shape[0]
    T = CHUNK * d
    nt = S // T
    hp, nh, cw = _dilated_split(d)
    nblocks = d * hp

    def body(q_ref, k_ref, v_ref, do_ref, l_ref, d_ref, dq_ref, dk_ref, dv_ref,
             qf, dof, kf, vf, kpf, vpf, dqf, acck, accv, newk, newv,
             sc_s, sp_s, dpc_s, dpp_s, pc_s, pp_s, dsc_s, dsp_s):
        hh, i = pl.program_id(0), pl.program_id(1)

        @pl.when(i == 0)
        def _():
            for ref in (kpf, vpf, acck, accv):
                ref[...] = jnp.zeros_like(ref)
            dk_ref[...] = jnp.zeros_like(dk_ref)
            dv_ref[...] = jnp.zeros_like(dv_ref)

        def blk(ref, r, j):
            return ref[j, r * CHUNK:(r + 1) * CHUNK, :]

        def bs(r, j):
            return slice((r * hp + j) * CHUNK, (r * hp + j + 1) * CHUNK)

        @pl.when(i < nt)
        def _():
            _by_class(q_ref, qf, d, hp)
            _by_class(do_ref, dof, d, hp)
            _by_class(k_ref, kf, d, hp)
            _by_class(v_ref, vf, d, hp)
            lses, deltas = [], []
            lcls = pltpu.einshape("(tr)l->(rt)l", l_ref[...], r=d)
            dcls = pltpu.einshape("(tr)l->(rt)l", d_ref[...], r=d)
            for r in range(d):
                lblk = lcls[r * CHUNK:(r + 1) * CHUNK]
                dblk = dcls[r * CHUNK:(r + 1) * CHUNK]
                for j in range(hp):
                    lses.append(_col(lblk, hh * hp + j))
                    deltas.append(_col(dblk, hh * hp + j))
                    qb, dob = blk(qf, r, j), blk(dof, r, j)
                    sc_s[bs(r, j), :] = _dot_nt(qb, blk(kf, r, j))
                    dpc_s[bs(r, j), :] = _dot_nt(dob, blk(vf, r, j))
                    sp_s[bs(r, j), :] = _dot_nt(qb, blk(kpf, r, j))
                    dpp_s[bs(r, j), :] = _dot_nt(dob, blk(vpf, r, j))
            lse = jnp.concatenate(lses, axis=0)
            delta = jnp.concatenate(deltas, axis=0)
            row = lax.broadcasted_iota(jnp.int32, (nblocks * CHUNK, CHUNK), 0) & (CHUNK - 1)
            col = lax.broadcasted_iota(jnp.int32, (nblocks * CHUNK, CHUNK), 1)
            sp = jnp.where(col >= row, sp_s[...] * ATT_SCALE, NEG) + jnp.where(i > 0, 0.0, NEG)
            pp = jnp.exp(sp - lse)
            pp_s[...] = pp.astype(BF16)
            dsp_s[...] = (pp * (dpp_s[...] - delta)).astype(BF16)
            sc = jnp.where(col <= row, sc_s[...] * ATT_SCALE, NEG)
            pc = jnp.exp(sc - lse)
            pc_s[...] = pc.astype(BF16)
            dsc_s[...] = (pc * (dpc_s[...] - delta)).astype(BF16)
            for r in range(d):
                rows = slice(r * CHUNK, (r + 1) * CHUNK)
                for j in range(hp):
                    qb, dob = blk(qf, r, j), blk(dof, r, j)
                    dsc, dsp = dsc_s[bs(r, j), :], dsp_s[bs(r, j), :]
                    dqf[j, rows, :] = (_dot(dsc, blk(kf, r, j)) + _dot(dsp, blk(kpf, r, j))) * ATT_SCALE
                    newk[j, rows, :] = _dot_tn(dsc, qb) * ATT_SCALE
                    newv[j, rows, :] = _dot_tn(pc_s[bs(r, j), :], dob)
                    acck[j, rows, :] += _dot_tn(dsp, qb) * ATT_SCALE
                    accv[j, rows, :] += _dot_tn(pp_s[bs(r, j), :], dob)
            _from_class(dqf, dq_ref, d, hp)

        @pl.when(i > 0)
        def _():
            _from_class(acck, dk_ref, d, hp)
            _from_class(accv, dv_ref, d, hp)

        @pl.when(i < nt)
        def _():
            acck[...] = newk[...]
            accv[...] = newv[...]
            kpf[...] = kf[...]
            vpf[...] = vf[...]

    def cur(width, cb, nsplit):
        return pl.BlockSpec((T, width), lambda hh, i: (jnp.minimum(i, nt - 1), cb * nsplit + hh * (nsplit > 1)))

    def lag():
        return pl.BlockSpec((T, cw), lambda hh, i: (jnp.maximum(i - 1, 0), hh))

    tile = pltpu.VMEM((hp, T, HEAD), BF16)
    acc = pltpu.VMEM((hp, T, HEAD), F32)
    f32s = pltpu.VMEM((nblocks * CHUNK, CHUNK), F32)
    b16s = pltpu.VMEM((nblocks * CHUNK, CHUNK), BF16)
    return pl.pallas_call(
        body, name=f"attn_bwd_dilated_{d}",
        grid=(nh, nt + 1),
        in_specs=[cur(cw, qcb, nh), cur(cw, kcb, nh), cur(cw, vcb, nh), cur(cw, 0, nh), cur(128, 0, 1), cur(128, 0, 1)],
        out_specs=[cur(cw, 0, nh), lag(), lag()],
        out_shape=[jax.ShapeDtypeStruct((S, WIDTH), BF16)] * 3,
        scratch_shapes=[tile] * 6 + [acc] * 5 + [f32s] * 4 + [b16s] * 4,
        compiler_params=_params(("arbitrary", "arbitrary")),
    )(proj, proj, proj, do, lse, delta)


def _abm_bwd(proj, dy3, ln_g, ln_b, wsm, wsm_t, bias_full, pool_w, pool_wt, pool_scale, kv):
    S = proj.shape[0]
    tm = 512
    nchunk = tm // CHUNK
    nblk = S // CHUNK

    def body(u_ref, v_ref, ag_ref, p_ref, ph_ref, pg_ref, pgn_ref, mq_ref, mg_ref, dy_ref, dypn_ref,
             lng_ref, lnb_ref, wsm_ref, wsmt_ref, bias_ref, pw_ref, pwt_ref, ps_ref, kv_ref,
             dab_ref, dm_ref, dlng_ref, dlnb_ref, dws_ref, dbias_ref, dpw_ref, dps_ref, dkv_ref,
             mix, dvl, ddn):
        i = pl.program_id(0)

        @pl.when(i == 0)
        def _():
            for r in (dlng_ref, dlnb_ref, dws_ref, dbias_ref, dpw_ref, dps_ref, dkv_ref):
                r[...] = jnp.zeros_like(r)

        au = u_ref[...].astype(F32)
        av = v_ref[...].astype(F32)
        ag = ag_ref[...].astype(F32)
        u = _gelu(au)
        v = _gelu(av)
        vhat, rstd = _layer_norm_fwd(v)
        vln = (vhat * lng_ref[...] + lnb_ref[...]).astype(BF16)
        for c in range(nchunk):
            for h in range(N_HEAD):
                rs, cs = slice(c * CHUNK, (c + 1) * CHUNK), slice(h * HEAD, (h + 1) * HEAD)
                mix[rs, cs] = _dot(wsm_ref[h], vln[rs, cs]) + bias_ref[:, cs]
        dya = dy_ref[0].astype(F32)
        sg = _silu(ag)
        mixed = mix[...]
        dab_ref[:, 2 * WIDTH:3 * WIDTH] = (dya * u * mixed * _dsilu(ag)).astype(BF16)
        dab_ref[:, 0:WIDTH] = (dya * mixed * sg * _dgelu(au)).astype(BF16)
        dmixed = dya * u * sg
        dmb = dmixed.astype(BF16)
        tril = (lax.broadcasted_iota(jnp.int32, (CHUNK, CHUNK), 1)
                <= lax.broadcasted_iota(jnp.int32, (CHUNK, CHUNK), 0))
        for c in range(nchunk):
            rs = slice(c * CHUNK, (c + 1) * CHUNK)
            dbias_ref[...] += dmixed[rs, :]
            for h in range(N_HEAD):
                cs = slice(h * HEAD, (h + 1) * HEAD)
                dvl[rs, cs] = _dot(wsmt_ref[h], dmb[rs, cs])
                dws_ref[h] += jnp.where(tril, _dot_nt(dmb[rs, cs], vln[rs, cs]), 0.0)
        dvln = dvl[...]
        dlng_ref[...] += jnp.sum(dvln * vhat, axis=0, keepdims=True)
        dlnb_ref[...] += jnp.sum(dvln, axis=0, keepdims=True)
        dvh = dvln * lng_ref[...]
        dv = rstd * (dvh - jnp.mean(dvh, axis=-1, keepdims=True)
                     - vhat * jnp.mean(dvh * vhat, axis=-1, keepdims=True))
        dab_ref[:, WIDTH:2 * WIDTH] = (dv * _dgelu(av)).astype(BF16)

        halo_ok = (i > 0).astype(F32)
        for c in range(nchunk):
            rs = slice(c * CHUNK, (c + 1) * CHUNK)
            for g, win in enumerate(POOL_WINDOWS):
                cs = slice(g * HEAD, (g + 1) * HEAD)
                bcur, bprev = _band_masks(win)
                cur = p_ref[rs, cs]
                if c == 0:
                    prev = (ph_ref[:, cs].astype(F32) * halo_ok).astype(BF16)
                else:
                    prev = p_ref[(c - 1) * CHUNK:c * CHUNK, cs]
                sums = _dot(bcur, cur) + _dot(bprev, prev)
                dvl[rs, cs] = sums * _inv_count(i * tm + c * CHUNK, win) - cur.astype(F32)
        dmat = dvl[...].astype(BF16)
        for g in range(4):
            cs = slice(g * HEAD, (g + 1) * HEAD)
            mix[:, cs] = _dot(dmat[:, cs], pw_ref[g])
        yg = mix[...]
        pg = pg_ref[...].astype(F32)
        dyp = dy_ref[1].astype(F32)
        dyy = dyp * _silu(pg)
        scale = ps_ref[...]
        dab_ref[:, 4 * WIDTH:5 * WIDTH] = (dyp * yg * scale * _dsilu(pg)).astype(BF16)
        dps_ref[...] += jnp.sum(dyy * yg, axis=0, keepdims=True)
        dyg = (dyy * scale).astype(BF16)
        for g in range(4):
            cs = slice(g * HEAD, (g + 1) * HEAD)
            dpw_ref[g] += _dot_tn(dmat[:, cs], dyg[:, cs])
            mix[:, cs] = _dot(dyg[:, cs], pwt_ref[g])
        next_ok = (i + 1 < S // tm).astype(F32)
        dygn = (dypn_ref[...].astype(F32) * _silu(pgn_ref[...].astype(F32)) * scale * next_ok).astype(BF16)
        for c in range(nchunk + 1):
            for g, win in enumerate(POOL_WINDOWS):
                cs = slice(g * HEAD, (g + 1) * HEAD)
                if c < nchunk:
                    dd = mix[c * CHUNK:(c + 1) * CHUNK, cs]
                else:
                    dd = _dot(dygn[:, cs], pwt_ref[g])
                ddn[c * CHUNK:(c + 1) * CHUNK, cs] = dd * _inv_count(i * tm + c * CHUNK, win)
        ddnb = ddn[...].astype(BF16)
        for c in range(nchunk):
            rs = slice(c * CHUNK, (c + 1) * CHUNK)
            ns = slice((c + 1) * CHUNK, (c + 2) * CHUNK)
            for g, win in enumerate(POOL_WINDOWS):
                cs = slice(g * HEAD, (g + 1) * HEAD)
                bcur, bprev = _band_masks(win)
                dp = _dot_tn(bcur, ddnb[rs, cs]) + _dot_tn(bprev, ddnb[ns, cs]) - mix[rs, cs]
                dab_ref[rs, 3 * WIDTH + g * HEAD:3 * WIDTH + (g + 1) * HEAD] = dp.astype(BF16)

        mg = mg_ref[...].astype(F32)
        dym = dy_ref[2].astype(F32)
        dob = (dym * _silu(mg)).astype(BF16)
        for h in range(N_HEAD):
            cs = slice(h * HEAD, (h + 1) * HEAD)
            vs = slice(WIDTH + h * HEAD, WIDTH + (h + 1) * HEAD)
            qh = mq_ref[:, cs]
            p = _mem_softmax(qh, kv_ref[:, cs])
            pb = p.astype(BF16)
            mix[:, cs] = _dot(pb, kv_ref[:, vs])
            dp = _dot_nt(dob[:, cs], kv_ref[:, vs])
            ds = (p * (dp - jnp.sum(p * dp, axis=-1, keepdims=True))).astype(BF16)
            dm_ref[:, cs] = (_dot(ds, kv_ref[:, cs]) * ATT_SCALE).astype(BF16)
            dkv_ref[:, cs] += _dot_tn(ds, qh) * ATT_SCALE
            dkv_ref[:, vs] += _dot_tn(pb, dob[:, cs])
        dm_ref[:, WIDTH:2 * WIDTH] = (dym * mix[...] * _dsilu(mg)).astype(BF16)

    blk = tm // CHUNK
    small = [_full((1, WIDTH)), _full((1, WIDTH)), _full((N_HEAD, CHUNK, CHUNK)), _full((CHUNK, WIDTH)),
             _full((4, HEAD, HEAD)), _full((1, WIDTH)), _full((MEM_LEN, 2 * WIDTH))]
    return pl.pallas_call(
        body, name="abm_bwd",
        grid=(S // tm,),
        in_specs=[_rows(tm, WIDTH, CB_U), _rows(tm, WIDTH, CB_V), _rows(tm, WIDTH, CB_AGATE),
                  _rows(tm, WIDTH, CB_PIN),
                  pl.BlockSpec((CHUNK, WIDTH), lambda i: (jnp.maximum(i * blk - 1, 0), CB_PIN)),
                  _rows(tm, WIDTH, CB_PGATE),
                  pl.BlockSpec((CHUNK, WIDTH), lambda i: (jnp.minimum(i * blk + blk, nblk - 1), CB_PGATE)),
                  _rows(tm, WIDTH, CB_MQ), _rows(tm, WIDTH, CB_MGATE),
                  pl.BlockSpec((3, tm, WIDTH), lambda i: (0, i, 0)),
                  pl.BlockSpec((None, CHUNK, WIDTH), lambda i: (1, jnp.minimum(i * blk + blk, nblk - 1), 0)),
                  _full((1, WIDTH)), _full((1, WIDTH)), _full((N_HEAD, CHUNK, CHUNK)), _full((N_HEAD, CHUNK, CHUNK)),
                  _full((CHUNK, WIDTH)), _full((4, HEAD, HEAD)), _full((4, HEAD, HEAD)), _full((1, WIDTH)),
                  _full((MEM_LEN, 2 * WIDTH))],
        out_specs=[_rows(tm, 5 * WIDTH), _rows(tm, 2 * WIDTH)] + small,
        out_shape=[jax.ShapeDtypeStruct((S, D_BRANCHES), BF16), jax.ShapeDtypeStruct((S, 2 * WIDTH), BF16),
                   jax.ShapeDtypeStruct((1, WIDTH), F32), jax.ShapeDtypeStruct((1, WIDTH), F32),
                   jax.ShapeDtypeStruct((N_HEAD, CHUNK, CHUNK), F32), jax.ShapeDtypeStruct((CHUNK, WIDTH), F32),
                   jax.ShapeDtypeStruct((4, HEAD, HEAD), F32), jax.ShapeDtypeStruct((1, WIDTH), F32),
                   jax.ShapeDtypeStruct((MEM_LEN, 2 * WIDTH), F32)],
        scratch_shapes=[pltpu.VMEM((tm, WIDTH), F32), pltpu.VMEM((tm, WIDTH), F32),
                        pltpu.VMEM((tm + CHUNK, WIDTH), F32)],
        compiler_params=_params(("arbitrary",)),
    )(proj, proj, proj, proj, proj, proj, proj, proj, proj, dy3, dy3,
      ln_g, ln_b, wsm, wsm_t, bias_full, pool_w, pool_wt, pool_scale, kv)


def _bias_reduce(dbias_full):
    def body(d_ref, o_ref):
        d = d_ref[...]
        o_ref[...] = _put_cols([jnp.sum(d[:, h * HEAD:(h + 1) * HEAD], axis=1, keepdims=True) for h in range(N_HEAD)])

    return pl.pallas_call(body, name="bias_reduce", out_shape=jax.ShapeDtypeStruct((CHUNK, 128), F32))(dbias_full)


def _mem_bwd(mem, g, mem_n, w, dkv):
    def body(m_ref, g_ref, mn_ref, w_ref, dkv_ref, dw_ref, dg_ref):
        dkvb = dkv_ref[...].astype(BF16)
        dw_ref[...] = _dot_tn(mn_ref[...], dkvb).astype(BF16)
        dmn = _dot_nt(dkvb, w_ref[...])
        xf = m_ref[...]
        r = lax.rsqrt(jnp.mean(xf * xf, axis=-1, keepdims=True) + EPS)
        dg_ref[...] = jnp.sum(dmn * xf * r, axis=0, keepdims=True)

    return pl.pallas_call(
        body, name="mem_bwd",
        out_shape=[jax.ShapeDtypeStruct((D_MODEL, 2 * WIDTH), BF16), jax.ShapeDtypeStruct((1, D_MODEL), F32)],
        compiler_params=pltpu.CompilerParams(vmem_limit_bytes=VMEM_LIMIT),
    )(mem, g, mem_n, w, dkv)


def _dh_bwd(dpb, dpg, wt, x, g, dxo, parts=()):
    S = x.shape[0]
    tm, tkb = 1024, D_BRANCHES // 4
    nkb, nkg = 4, D_GATES // GATE_TILE
    nk = nkb + nkg
    ni = S // tm
    n = len(parts)

    def body(dpb_ref, wbr_ref, dpg_ref, wg_ref, x_ref, g_ref, dxo_ref, *rest):
        p_in = rest[:n]
        dx_ref, dg_ref = rest[n:n + 2]
        p_out, acc, sems = rest[n + 2:2 * n + 2], rest[2 * n + 2], rest[2 * n + 3:]
        i, kk = pl.program_id(0), pl.program_id(1)

        @pl.when(jnp.logical_and(i == 0, kk == 0))
        def _():
            dg_ref[...] = jnp.zeros_like(dg_ref)
            if n:
                _comm_start(_rs_second(p_in, p_out, *sems))

        @pl.when(kk == 0)
        def _():
            acc[...] = jnp.zeros_like(acc)

        @pl.when(kk < nkb)
        def _():
            acc[...] += _dot(dpb_ref[...], wbr_ref[...])

        @pl.when(kk >= nkb)
        def _():
            acc[...] += _dot(dpg_ref[...], wg_ref[...])

        @pl.when(kk == nk - 1)
        def _():
            xf = x_ref[...]
            r = lax.rsqrt(jnp.mean(xf * xf, axis=-1, keepdims=True) + EPS)
            xhat = xf * r
            dh = acc[...]
            dg_ref[...] += jnp.sum(dh * xhat, axis=0, keepdims=True)
            dxh = dh * g_ref[...]
            dx_ref[...] = dxo_ref[...] + r * (dxh - xhat * jnp.mean(dxh * xhat, axis=-1, keepdims=True))

        if n:
            @pl.when(jnp.logical_and(i == ni - 1, kk == nk - 1))
            def _():
                _comm_wait(_rs_second(p_in, p_out, *sems))

    res = pl.pallas_call(
        body, name="dh_bwd_scatter" if n else "dh_bwd",
        grid=(ni, nk),
        in_specs=[pl.BlockSpec((tm, tkb), lambda i, k: (i, jnp.minimum(k, nkb - 1))),
                  pl.BlockSpec((tkb, D_MODEL), lambda i, k: (jnp.minimum(k, nkb - 1), 0)),
                  pl.BlockSpec((tm, GATE_TILE), lambda i, k: (i, jnp.maximum(k - nkb, 0))),
                  pl.BlockSpec((GATE_TILE, D_MODEL),
                               lambda i, k: (D_BRANCHES // GATE_TILE + jnp.maximum(k - nkb, 0), 0)),
                  pl.BlockSpec((tm, D_MODEL), lambda i, k: (i, 0)), pl.BlockSpec((1, D_MODEL), lambda i, k: (0, 0)),
                  pl.BlockSpec((tm, D_MODEL), lambda i, k: (i, 0))] + [ANY] * n,
        out_specs=[pl.BlockSpec((tm, D_MODEL), lambda i, k: (i, 0)), pl.BlockSpec((1, D_MODEL), lambda i, k: (0, 0))]
                  + [ANY] * n,
        out_shape=[jax.ShapeDtypeStruct((S, D_MODEL), F32), jax.ShapeDtypeStruct((1, D_MODEL), F32)]
                  + [jax.ShapeDtypeStruct(p.shape, p.dtype) for p in parts],
        scratch_shapes=[pltpu.VMEM((tm, D_MODEL), F32)] + (_dma_sems(3 * n, 3 * n, n) if n else []),
        compiler_params=_params(("arbitrary", "arbitrary")),
    )(dpb, wt, dpg, wt, x, g, dxo, *parts)
    return res[0], res[1], list(res[2:])


def _dw_in(h, dpb, dpg, parts=()):
    S = h.shape[0]
    tk = 1024
    nk = S // tk
    n = len(parts)
    tmb = D_BRANCHES // 4
    ng = D_GATES // GATE_TILE

    def accumulate(a_ref, h_ref, o_ref, acc):
        kk = pl.program_id(1)

        @pl.when(kk == 0)
        def _():
            acc[...] = jnp.zeros_like(acc)

        acc[...] += _dot_tn(a_ref[...], h_ref[...])

        @pl.when(kk == nk - 1)
        def _():
            o_ref[...] = acc[...].astype(BF16)

    def branches(a_ref, h_ref, *rest):
        p_in, o_ref, p_out = rest[:n], rest[n], rest[n + 1:2 * n + 1]
        acc, sems = rest[2 * n + 1], rest[2 * n + 2:]
        i, kk = pl.program_id(0), pl.program_id(1)

        if n:
            @pl.when(jnp.logical_and(i == 0, kk == 0))
            def _():
                _comm_start(_rs_second(p_in, p_out, *sems))

        accumulate(a_ref, h_ref, o_ref, acc)

        if n:
            @pl.when(jnp.logical_and(i == 3, kk == nk - 1))
            def _():
                _comm_wait(_rs_second(p_in, p_out, *sems))

    def gates(a_ref, h_ref, dst_ref, o_ref, acc):
        accumulate(a_ref, h_ref, o_ref, acc)

    res = pl.pallas_call(
        branches, name="dw_in_branches_scatter" if n else "dw_in_branches",
        grid=(4, nk),
        in_specs=[pl.BlockSpec((tk, tmb), lambda i, k: (k, i)), pl.BlockSpec((tk, D_MODEL), lambda i, k: (k, 0))]
                 + [ANY] * n,
        out_specs=[pl.BlockSpec((tmb, D_MODEL), lambda i, k: (i, 0))] + [ANY] * n,
        out_shape=[jax.ShapeDtypeStruct((D_IN, D_MODEL), BF16)]
                  + [jax.ShapeDtypeStruct(p.shape, p.dtype) for p in parts],
        scratch_shapes=[pltpu.VMEM((tmb, D_MODEL), F32)] + (_dma_sems(3 * n, 3 * n, n) if n else []),
        compiler_params=_params(("arbitrary", "arbitrary")),
    )(dpb, h, *parts)
    dwt = pl.pallas_call(
        gates, name="dw_in_gates",
        grid=(ng, nk),
        in_specs=[pl.BlockSpec((tk, GATE_TILE), lambda i, k: (k, i)), pl.BlockSpec((tk, D_MODEL), lambda i, k: (k, 0)),
                  ANY],
        out_specs=pl.BlockSpec((GATE_TILE, D_MODEL), lambda i, k: (D_BRANCHES // GATE_TILE + i, 0)),
        out_shape=jax.ShapeDtypeStruct((D_IN, D_MODEL), BF16),
        input_output_aliases={2: 0},
        scratch_shapes=[pltpu.VMEM((GATE_TILE, D_MODEL), F32)],
        compiler_params=_params(("parallel", "arbitrary")),
    )(dpg, h, res[0])
    return dwt, list(res[1:])


def _matmul_tn(a, b, tn, name, parts=()):
    K, M = a.shape
    N = b.shape[1]
    tk = 1024
    nk = K // tk
    nj = N // tn
    n = len(parts)

    def body(a_ref, b_ref, *rest):
        p_in, o_ref, p_out = rest[:n], rest[n], rest[n + 1:2 * n + 1]
        acc, sems = rest[2 * n + 1], rest[2 * n + 2:]
        j, kk = pl.program_id(0), pl.program_id(1)

        if n:
            @pl.when(jnp.logical_and(j == 0, kk == 0))
            def _():
                _comm_start(_rs_second(p_in, p_out, *sems))

        @pl.when(kk == 0)
        def _():
            acc[...] = jnp.zeros_like(acc)

        acc[...] += _dot_tn(a_ref[...].astype(BF16), b_ref[...].astype(BF16))

        @pl.when(kk == nk - 1)
        def _():
            o_ref[...] = acc[...].astype(BF16)

        if n:
            @pl.when(jnp.logical_and(j == nj - 1, kk == nk - 1))
            def _():
                _comm_wait(_rs_second(p_in, p_out, *sems))

    res = pl.pallas_call(
        body, name=name,
        grid=(nj, nk),
        in_specs=[pl.BlockSpec((tk, M), lambda j, k: (k, 0)), pl.BlockSpec((tk, tn), lambda j, k: (k, j))] + [ANY] * n,
        out_specs=[pl.BlockSpec((M, tn), lambda j, k: (0, j))] + [ANY] * n,
        out_shape=[jax.ShapeDtypeStruct((M, N), BF16)] + [jax.ShapeDtypeStruct(p.shape, p.dtype) for p in parts],
        scratch_shapes=[pltpu.VMEM((M, tn), F32)] + (_dma_sems(3 * n, 3 * n, n) if n else []),
        compiler_params=_params(("arbitrary", "arbitrary")),
    )(a, b, *parts)
    return (res[0], list(res[1:])) if n else res[0]


def _row_tile(R, C, block_bytes=2 << 20):
    for cand in range(min(R, block_bytes // (C * 4)) // 8 * 8, 0, -8):
        if R % cand == 0:
            return cand
    return R


def _adamw_update(p_ref, w_ref, m_ref, v_ref, g_ref, d_ref, nm_ref, nv_ref):
    c1 = 1.0 / (1.0 - ADAM_B1 ** ADAM_STEP)
    c2 = 1.0 / (1.0 - ADAM_B2 ** ADAM_STEP)
    g = p_ref[0].astype(F32)
    for k in range(1, p_ref.shape[0]):
        g = g + p_ref[k].astype(F32)
    nm = ADAM_B1 * m_ref[...] + (1.0 - ADAM_B1) * g
    nv = ADAM_B2 * v_ref[...] + (1.0 - ADAM_B2) * (g * g)
    g_ref[...] = g
    nm_ref[...] = nm
    nv_ref[...] = nv
    d_ref[...] = -ADAM_LR * ((nm * c1) / (jnp.sqrt(nv * c2) + ADAM_EPS) + ADAM_WD * w_ref[...])


def _adamw(parts, w, m, v, name):
    P, R, C = parts.shape
    tr = _row_tile(R, C)

    def body(*refs):
        _adamw_update(*refs)

    spec = pl.BlockSpec((tr, C), lambda i: (i, 0))
    return pl.pallas_call(
        body, name=name,
        grid=(R // tr,),
        in_specs=[pl.BlockSpec((P, tr, C), lambda i: (0, i, 0)), spec, spec, spec],
        out_specs=[spec] * 4,
        out_shape=[jax.ShapeDtypeStruct((R, C), F32)] * 4,
        compiler_params=_params(("parallel",)),
    )(parts, w, m, v)


def _adamw_layers(parts, w, m, v, name):
    depth = len(parts)
    P, R, C = parts[0].shape
    tr = _row_tile(R, C, 1 << 20)

    def body(*refs):
        layer = pl.program_id(0)
        for k in range(depth):
            @pl.when(layer == k)
            def _(k=k):
                _adamw_update(refs[k], *refs[depth:])

    def part_spec(k):
        return pl.BlockSpec((P, tr, C), lambda l, i: (0, jnp.where(l == k, i, 0), 0))

    spec = pl.BlockSpec((None, tr, C), lambda l, i: (l, i, 0))
    return pl.pallas_call(
        body, name=name,
        grid=(depth, R // tr),
        in_specs=[part_spec(k) for k in range(depth)] + [spec] * 3,
        out_specs=[spec] * 4,
        out_shape=[jax.ShapeDtypeStruct((depth, R, C), F32)] * 4,
        compiler_params=_params(("arbitrary", "arbitrary")),
    )(*parts, w, m, v)


def _place():
    return lax.axis_index("x"), lax.axis_index("y"), lax.axis_index("c")


def _all_gather(shards):
    n = len(shards)

    def body(*refs):
        ins, outs = refs[:n], refs[n:2 * n]
        send1, recv1, local_sems, send2, recv2 = refs[2 * n:]
        first = _ag_first(ins, outs, send1, recv1, local_sems)
        second = _ag_second(outs, send2, recv2)
        _comm_start(first)
        for j in range(3):
            for a in range(n):
                first[2][4 * a + 1 + j].wait_recv()
            for a in range(n):
                second[1][3 * a + j].start()
        for a in range(n):
            first[2][4 * a].wait_recv()
        for cp in second[2]:
            cp.wait_recv()
        for cp in first[1] + second[1]:
            cp.wait_send()
        for cp in first[0]:
            cp.wait()

    return pl.pallas_call(
        body, name="weights_all_gather",
        in_specs=[ANY] * n, out_specs=[ANY] * n,
        out_shape=[jax.ShapeDtypeStruct((N_DEV,) + s.shape, s.dtype) for s in shards],
        scratch_shapes=_dma_sems(4 * n, 4 * n, n, 3 * n, 3 * n),
        compiler_params=pltpu.CompilerParams(has_side_effects=True),
    )(*shards)


N_BIG = 4


def _dev(p):
    return 4 * p[0] + 2 * p[1] + p[2]


def _other_chips(x, y):
    return [(1 - x, y), (x, 1 - y), (1 - x, 1 - y)]


def _remote(src, dst, send_sems, recv_sems, k, to):
    return pltpu.make_async_remote_copy(src_ref=src, dst_ref=dst, send_sem=send_sems.at[k], recv_sem=recv_sems.at[k],
                                        device_id=to, device_id_type=MESH)


def _ag_first(ins, outs, send_sems, recv_sems, local_sems):
    x, y, c = _place()
    me = (x, y, c)
    targets = [(x, y, 1 - c)] + [(*chip, c) for chip in _other_chips(x, y)]
    local, out, inc = [], [], []
    for a in range(len(ins)):
        local.append(pltpu.make_async_copy(ins[a], outs[a].at[_dev(me)], local_sems.at[a]))
        for k, to in enumerate(targets):
            out.append(_remote(ins[a], outs[a].at[_dev(me)], send_sems, recv_sems, 4 * a + k, to))
            inc.append(_remote(ins[a], outs[a].at[_dev(to)], send_sems, recv_sems, 4 * a + k, to))
    return local, out, inc


def _ag_second(bufs, send_sems, recv_sems):
    x, y, c = _place()
    out, inc = [], []
    for a in range(len(bufs)):
        for j, chip in enumerate(_other_chips(x, y)):
            mine, theirs = bufs[a].at[_dev((*chip, c))], bufs[a].at[_dev((*chip, 1 - c))]
            out.append(_remote(mine, mine, send_sems, recv_sems, 3 * a + j, (x, y, 1 - c)))
            inc.append(_remote(theirs, theirs, send_sems, recv_sems, 3 * a + j, (x, y, 1 - c)))
    return [], out, inc


def _rs_first(ins, outs, send_sems, recv_sems):
    x, y, c = _place()
    out = [_remote(ins[a].at[j, 1 - c], outs[a].at[j], send_sems, recv_sems, N_CHIP * a + j, (x, y, 1 - c))
           for a in range(len(ins)) for j in range(N_CHIP)]
    return [], out, out


def _rs_second(ins, outs, send_sems, recv_sems, local_sems):
    x, y, c = _place()
    my_chip = 2 * x + y
    local, out, inc = [], [], []
    for a in range(len(ins)):
        local.append(pltpu.make_async_copy(ins[a].at[my_chip], outs[a].at[my_chip], local_sems.at[a]))
        for k, (ox, oy) in enumerate(_other_chips(x, y)):
            out.append(_remote(ins[a].at[2 * ox + oy], outs[a].at[my_chip], send_sems, recv_sems, 3 * a + k, (ox, oy, c)))
            inc.append(_remote(ins[a].at[2 * ox + oy], outs[a].at[2 * ox + oy], send_sems, recv_sems, 3 * a + k,
                               (ox, oy, c)))
    return local, out, inc


def _comm_start(exchange):
    local, out, _ = exchange
    for cp in local + out:
        cp.start()


def _comm_wait(exchange):
    local, out, inc = exchange
    for cp in inc:
        cp.wait_recv()
    for cp in out:
        cp.wait_send()
    for cp in local:
        cp.wait()


def _dma_sems(*counts):
    return [pltpu.SemaphoreType.DMA((n,)) for n in counts]


def _rs_sibling(grads):
    n = len(grads)

    def body(*refs):
        ex = _rs_first(refs[:n], refs[n:2 * n], *refs[2 * n:])
        _comm_start(ex)
        _comm_wait(ex)

    return pl.pallas_call(
        body, name="grads_to_sibling",
        in_specs=[ANY] * n, out_specs=[ANY] * n,
        out_shape=[jax.ShapeDtypeStruct(g.shape[:1] + g.shape[2:], g.dtype) for g in grads],
        scratch_shapes=_dma_sems(N_CHIP * n, N_CHIP * n),
        compiler_params=pltpu.CompilerParams(has_side_effects=True),
    )(*grads)


def _pair_sum(grads, recvs):
    n = len(grads)

    def body(c_ref, *refs):
        for a in range(n):
            refs[2 * n + a][...] = (refs[a][...].astype(F32) + refs[n + a][...].astype(F32)).astype(BF16)

    def g_spec(g):
        return pl.BlockSpec((None, None) + g.shape[2:], lambda j, c_ref: (j, c_ref[0], 0, 0))

    def r_spec(r):
        return pl.BlockSpec((None,) + r.shape[1:], lambda j, c_ref: (j, 0, 0))

    return pl.pallas_call(
        body, name="pair_sum",
        grid_spec=pltpu.PrefetchScalarGridSpec(
            num_scalar_prefetch=1, grid=(N_CHIP,),
            in_specs=[g_spec(g) for g in grads] + [r_spec(r) for r in recvs],
            out_specs=[r_spec(r) for r in recvs]),
        out_shape=[jax.ShapeDtypeStruct(r.shape, BF16) for r in recvs],
        compiler_params=_params(("parallel",)),
    )(lax.axis_index("c").reshape(1).astype(jnp.int32), *grads, *recvs)


def _rs_chips(parts):
    n = len(parts)

    def body(*refs):
        ex = _rs_second(refs[:n], refs[n:2 * n], *refs[2 * n:])
        _comm_start(ex)
        _comm_wait(ex)

    return pl.pallas_call(
        body, name="grads_to_chips",
        in_specs=[ANY] * n, out_specs=[ANY] * n,
        out_shape=[jax.ShapeDtypeStruct(p.shape, p.dtype) for p in parts],
        scratch_shapes=_dma_sems(3 * n, 3 * n, n),
        compiler_params=pltpu.CompilerParams(has_side_effects=True),
    )(*parts)


SMALL_ROWS = 544


def _all_reduce_small(buf):
    def body(in_ref, out_ref, recv, acc, send1, recv1, send2, recv2):
        x, y, c = _place()
        me = 4 * x + 2 * y + c
        peers = [(x ^ (r >> 2), y ^ ((r >> 1) & 1), c ^ (r & 1)) for r in range(1, N_DEV)]

        def idx(p):
            return 4 * p[0] + 2 * p[1] + p[2]

        first = [pltpu.make_async_remote_copy(
            src_ref=in_ref.at[idx(p)], dst_ref=recv.at[me], send_sem=send1.at[r], recv_sem=recv1.at[r],
            device_id=p, device_id_type=MESH) for r, p in enumerate(peers)]
        for cp in first:
            cp.start()
        recv[me] = in_ref[me]
        for r, p in enumerate(peers):
            pltpu.make_async_remote_copy(
                src_ref=in_ref.at[idx(p)], dst_ref=recv.at[idx(p)], send_sem=send1.at[r], recv_sem=recv1.at[r],
                device_id=p, device_id_type=MESH).wait_recv()
        total = recv[0]
        for k in range(1, N_DEV):
            total = total + recv[k]
        acc[...] = total
        out_ref[me] = total
        second = [pltpu.make_async_remote_copy(
            src_ref=acc, dst_ref=out_ref.at[me], send_sem=send2.at[r], recv_sem=recv2.at[r],
            device_id=p, device_id_type=MESH) for r, p in enumerate(peers)]
        for cp in second:
            cp.start()
        for r, p in enumerate(peers):
            pltpu.make_async_remote_copy(
                src_ref=acc, dst_ref=out_ref.at[idx(p)], send_sem=send2.at[r], recv_sem=recv2.at[r],
                device_id=p, device_id_type=MESH).wait_recv()
        for cp in first + second:
            cp.wait_send()

    vm = pl.BlockSpec(memory_space=pltpu.VMEM)
    return pl.pallas_call(
        body, name="small_grads_all_reduce",
        in_specs=[vm], out_specs=vm,
        out_shape=jax.ShapeDtypeStruct(buf.shape, F32),
        scratch_shapes=[pltpu.VMEM(buf.shape, F32), pltpu.VMEM(buf.shape[1:], F32),
                        pltpu.SemaphoreType.DMA((7,)), pltpu.SemaphoreType.DMA((7,)),
                        pltpu.SemaphoreType.DMA((7,)), pltpu.SemaphoreType.DMA((7,))],
        compiler_params=pltpu.CompilerParams(has_side_effects=True, vmem_limit_bytes=VMEM_LIMIT),
    )(buf)


def _dilate(a, d):
    if d == 1:
        return a
    S, C = a.shape
    return a.reshape(S // d, d, C).transpose(1, 0, 2).reshape(S, C)


def _undilate(a, d):
    if d == 1:
        return a
    S, C = a.shape
    return a.reshape(d, S // d, C).transpose(1, 0, 2).reshape(S, C)


def _cols(a, cb, n=1):
    return a[:, cb * WIDTH:(cb + n) * WIDTH]


def _to_blocks(g, kind):
    if kind == "rows":
        C = g.shape[1]
        return g.reshape(N_CHIP, 2, -1, C)
    return g.reshape(4 * WIDTH, N_CHIP, 2, -1).transpose(1, 2, 0, 3)


SMALL = ("norm_g", "gm_ln_g", "gm_ln_b", "gm_ws", "gm_bs", "pool_w", "pool_scale", "mem_norm_g", "final_norm_g")


def _pack_small(tree):
    flat = jnp.concatenate([tree[k].reshape(-1, 128) for k in SMALL], axis=0)
    return jnp.pad(flat, ((0, N_DEV * SMALL_ROWS - flat.shape[0]), (0, 0)))


def _unpack_small(flat, like):
    out, at = {}, 0
    for k in SMALL:
        rows = like[k].size // 128
        out[k] = flat[at:at + rows].reshape(like[k].shape)
        at += rows
    return out


def _make_layer(wt, wkv, wb, wout, norm_g, mem_norm_g, ln_g, ln_b, gm_ws, gm_bs, pool_w, pool_scale):
    tril = jnp.tril(jnp.ones((CHUNK, CHUNK), bool))
    wsm = jnp.where(tril, gm_ws, 0.0).astype(BF16)
    pw = pool_w.astype(BF16)
    return dict(wt=wt, wkv=wkv, wb=wb, wout=wout, g=norm_g[None], mg=mem_norm_g[None], ln_g=ln_g[None],
                ln_b=ln_b[None], wsm=wsm, wsm_t=wsm.transpose(0, 2, 1), pw=pw, pw_t=pw.transpose(0, 2, 1),
                ps=pool_scale[None], bias=jnp.repeat(gm_bs.T, HEAD, axis=1))


def _layer_fwd(xl, mem0, L, next_shards=()):
    S = xl.shape[0]
    proj, gates, h, half_gathered = _in_proj(xl, L["g"], L["wt"], next_shards[:1])
    kv, mem_n = _mem_kv(mem0, L["mg"], L["wkv"])
    y4, gathered = _abm_fwd(proj, L["ln_g"], L["ln_b"], L["wsm"], L["bias"], L["pw"], L["ps"], kv, half_gathered)
    o_g, l_g = [], []
    for gi, d in enumerate(DILATIONS):
        if d == 1:
            o, lse = _attn_fwd(proj, CB_Q0, proj, CB_K, proj, CB_CV, S // CHUNK)
        else:
            o, lse = _attn_fwd_dilated(proj, CB_Q0 + gi, CB_K, CB_CV, d)
        o_g.append(o)
        l_g.append(lse)
    (xn, y4, oc, lse, z), rest = _merge_fwd(xl, y4, o_g, l_g, proj, gates, L["wb"], L["wout"], next_shards[1:])
    saved = dict(x=xl, proj=proj, gates=gates, h=h, kv=kv, mem_n=mem_n, y4=y4, oc=oc, lse=lse, z=z)
    return xn, saved, gathered + rest


def _place_cols(dst, piece, cb):
    return lax.dynamic_update_slice(dst, piece, (0, cb * WIDTH))


def _layer_bwd(dx, mem0, L, sv, later=()):
    S = dx.shape[0]
    proj = sv["proj"]
    (dy3, doc, delta, dcg, dgm, dt), from_sibling = _merge_bwd(dx, sv["y4"], sv["oc"], proj, sv["gates"], L["wb"],
                                                              L["wout"], later)
    pair = _pair_sum(later, from_sibling) if later else ()
    dwout = _matmul_tn(sv["z"], dx, D_MODEL, "dw_out")
    dwb = _dw_branch(sv["y4"], dt)
    dpb, dm, dlng, dlnb, dws, dbias, dpw, dps, dkv = _abm_bwd(
        proj, dy3, L["ln_g"], L["ln_b"], L["wsm"], L["wsm_t"], L["bias"], L["pw"], L["pw_t"], L["ps"], sv["kv"])
    dk, dv = None, None
    for gi, d in enumerate(DILATIONS):
        if d == 1:
            r = _attn_bwd(proj, CB_Q0, proj, CB_K, proj, CB_CV, doc, sv["lse"], delta, S // CHUNK)
        else:
            r = _attn_bwd_dilated(proj, CB_Q0 + gi, CB_K, CB_CV, doc, sv["lse"], delta, d)
        dpb = _place_cols(dpb, r[0], CB_Q0 + gi)
        dkg, dvg = r[1].astype(F32), r[2].astype(F32)
        dk = dkg if dk is None else dk + dkg
        dv = dvg if dv is None else dv + dvg
    dpb = _place_cols(dpb, dk.astype(BF16), CB_K)
    dpb = _place_cols(dpb, dv.astype(BF16), CB_CV)
    dpb = _place_cols(dpb, dcg, CB_CGATE)
    dpb = _place_cols(dpb, dm, CB_MQ)
    dwkv, dmg = _mem_bwd(mem0, L["mg"], sv["mem_n"], L["wkv"], dkv)
    dwin_t, parts_rest = _dw_in(sv["h"], dpb, dgm, pair[1:])
    dxi, dng, parts = _dh_bwd(dpb, dgm, L["wt"], sv["x"], L["g"], dx, pair[:1])
    parts = parts + parts_rest
    big = dict(w_in=dwin_t, w_mem_kv=dwkv, w_branch=dwb, w_out=dwout)
    small = dict(norm_g=dng[0], gm_ln_g=dlng[0], gm_ln_b=dlnb[0], gm_ws=dws,
                 gm_bs=_bias_reduce(dbias)[:, :N_HEAD].T, pool_w=dpw, pool_scale=dps[0], mem_norm_g=dmg[0])
    return dxi, big, small, parts


BIG = ("w_in", "w_mem_kv", "w_branch", "w_out")


def _blocked(big):
    return [_to_blocks(big["w_in"], "rows"), _to_blocks(big["w_mem_kv"], "rows"),
            _to_blocks(big["w_branch"], "branch"), _to_blocks(big["w_out"], "rows")]


def _full_weights(gathered):
    win_t, wkv, wb, wout = gathered
    return (win_t.reshape(D_IN, D_MODEL), wkv.reshape(D_MODEL, 2 * WIDTH),
            wb.reshape(N_DEV, 4, WIDTH, -1).transpose(1, 2, 0, 3).reshape(4, WIDTH, D_MODEL),
            wout.reshape(D_MODEL, D_MODEL))


def kernel(x, mem, norm_g, w_in, gm_ln_g, gm_ln_b, gm_ws, gm_bs, pool_w, pool_scale, mem_norm_g, w_mem_kv, w_branch, w_out, final_norm_g, loss_target, m_norm_g, m_w_in, m_gm_ln_g, m_gm_ln_b, m_gm_ws, m_gm_bs, m_pool_w, m_pool_scale, m_mem_norm_g, m_w_mem_kv, m_w_branch, m_w_out, m_final_norm_g, v_norm_g, v_w_in, v_gm_ln_g, v_gm_ln_b, v_gm_ws, v_gm_bs, v_pool_w, v_pool_scale, v_mem_norm_g, v_w_mem_kv, v_w_branch, v_w_out, v_final_norm_g):
    x0 = x[0]
    mem0 = mem[0]
    tgt = loss_target[0]
    S = x0.shape[0]

    shards = [[w_in[l].T.astype(BF16), w_mem_kv[l].astype(BF16), w_branch[l].astype(BF16).reshape(4 * WIDTH, -1),
               w_out[l].astype(BF16)] for l in range(DEPTH)]
    gathered = _all_gather(shards[0])
    layers, saved = [], []
    xl = x0
    for l in range(DEPTH):
        layers.append(_make_layer(*_full_weights(gathered), norm_g[l], mem_norm_g[l], gm_ln_g[l], gm_ln_b[l],
                                  gm_ws[l], gm_bs[l], pool_w[l], pool_scale[l]))
        xl, sv, gathered = _layer_fwd(xl, mem0, layers[l], shards[l + 1] if l + 1 < DEPTH else ())
        saved.append(sv)

    loss_part, dx, d_final = _loss_head(xl, final_norm_g[None], tgt)
    loss = lax.psum(loss_part[0, 0], ("x", "y", "c"))

    small = {k: [None] * DEPTH for k in SMALL if k != "final_norm_g"}
    parts = [None] * DEPTH
    later = ()
    for l in reversed(range(DEPTH)):
        dx, gb, gs, done = _layer_bwd(dx, mem0, layers[l], saved[l], later)
        if later:
            parts[l + 1] = done
        later = _blocked(gb)
        for k in gs:
            small[k][l] = gs[k]
    grad_x = dx[None]
    parts[0] = _rs_chips(_pair_sum(later, _rs_sibling(later)))

    small_tree = {k: jnp.stack(small[k]) for k in small}
    small_tree["final_norm_g"] = d_final[0]
    reduced = _all_reduce_small(_pack_small(small_tree).reshape(N_DEV, SMALL_ROWS, 128))

    weights = dict(norm_g=norm_g, w_in=w_in, gm_ln_g=gm_ln_g, gm_ln_b=gm_ln_b, gm_ws=gm_ws, gm_bs=gm_bs,
                   pool_w=pool_w, pool_scale=pool_scale, mem_norm_g=mem_norm_g, w_mem_kv=w_mem_kv,
                   w_branch=w_branch, w_out=w_out, final_norm_g=final_norm_g)
    m_in = dict(norm_g=m_norm_g, w_in=m_w_in, gm_ln_g=m_gm_ln_g, gm_ln_b=m_gm_ln_b, gm_ws=m_gm_ws, gm_bs=m_gm_bs,
                pool_w=m_pool_w, pool_scale=m_pool_scale, mem_norm_g=m_mem_norm_g, w_mem_kv=m_w_mem_kv,
                w_branch=m_w_branch, w_out=m_w_out, final_norm_g=m_final_norm_g)
    v_in = dict(norm_g=v_norm_g, w_in=v_w_in, gm_ln_g=v_gm_ln_g, gm_ln_b=v_gm_ln_b, gm_ws=v_gm_ws, gm_bs=v_gm_bs,
                pool_w=v_pool_w, pool_scale=v_pool_scale, mem_norm_g=v_mem_norm_g, w_mem_kv=v_w_mem_kv,
                w_branch=v_w_branch, w_out=v_w_out, final_norm_g=v_final_norm_g)
    res = {}
    def view(k, arr):
        return arr.transpose(0, 2, 1) if k == "w_in" else arr

    for a, k in enumerate(BIG):
        shape = view(k, weights[k]).shape
        by_layer = [parts[l][a] for l in range(DEPTH)]
        lrc = (DEPTH,) + by_layer[0].shape[1:]
        outs = _adamw_layers(by_layer, view(k, weights[k]).reshape(lrc), view(k, m_in[k]).reshape(lrc),
                             view(k, v_in[k]).reshape(lrc), "adamw_" + k)
        res[k] = [view(k, o.reshape(shape)) for o in outs]
    outs = _adamw(reduced.reshape(1, N_DEV * SMALL_ROWS, 128), _pack_small(weights), _pack_small(m_in),
                  _pack_small(v_in), "adamw_small")
    unpacked = [_unpack_small(o, weights) for o in outs]
    for k in SMALL:
        res[k] = [u[k] for u in unpacked]

    order = ("norm_g", "w_in", "gm_ln_g", "gm_ln_b", "gm_ws", "gm_bs", "pool_w", "pool_scale", "mem_norm_g",
             "w_mem_kv", "w_branch", "w_out", "final_norm_g")
    return (loss, grad_x, *[res[k][0] for k in order], *[res[k][1] for k in order],
            *[res[k][2] for k in order], *[res[k][3] for k in order])
```

```python
import functools
import math

import jax
import jax.numpy as jnp
from jax import lax
from jax.experimental import pallas as pl
from jax.experimental.pallas import tpu as pltpu

F32 = jnp.float32
BF16 = jnp.bfloat16

D_MODEL = 1024
DEPTH = 4
WIDTH = 512
D_IN = 10752
HEAD = 128
N_HEAD = 4
CHUNK = 128
MEM_LEN = 256
POOL_WINDOWS = (2, 4, 8, 16)
DILATIONS = (1, 4, 16)
EPS = 1e-6
NEG = -1e30
ATT_SCALE = HEAD ** -0.5
N_DEV = 8
N_CHIP = 4

D_BRANCHES = 6656
D_GATES = D_IN - D_BRANCHES
CB_U, CB_V, CB_AGATE, CB_PIN, CB_PGATE = 0, 1, 2, 3, 4
CB_Q0, CB_K, CB_CV, CB_CGATE, CB_MQ, CB_MGATE = 5, 8, 9, 10, 11, 12

ADAM_LR = 0.001
ADAM_B1 = 0.9
ADAM_B2 = 0.999
ADAM_EPS = 1e-08
ADAM_WD = 0.01
ADAM_STEP = 10

VMEM_LIMIT = 56 * 1024 * 1024
MESH = pl.DeviceIdType.MESH
ANY = pl.BlockSpec(memory_space=pl.ANY)

NT = (((1,), (1,)), ((), ()))
TN = (((0,), (0,)), ((), ()))


def _dot(a, b):
    return jnp.dot(a, b, preferred_element_type=F32)


def _dot_nt(a, b):
    return lax.dot_general(a, b, NT, preferred_element_type=F32)


def _dot_tn(a, b):
    return lax.dot_general(a, b, TN, preferred_element_type=F32)


def _sigmoid(x):
    return 0.5 * jnp.tanh(0.5 * x) + 0.5


def _silu(x):
    return x * _sigmoid(x)


def _silu_and_grad(x):
    s = _sigmoid(x)
    return x * s, s * (1.0 + x * (1.0 - s))


def _gelu(x):
    return 0.5 * x * (1.0 + lax.erf(x * (2.0 ** -0.5)))


def _gelu_and_grad(x):
    cdf = 0.5 * (1.0 + lax.erf(x * (2.0 ** -0.5)))
    return x * cdf, cdf + x * jnp.exp(-0.5 * x * x) * (1.0 / math.sqrt(2.0 * math.pi))


def _col(blk, h):
    lane = lax.broadcasted_iota(jnp.int32, blk.shape, 1)
    return jnp.sum(jnp.where(lane == h, blk, 0.0), axis=1, keepdims=True)


def _put_cols(cols):
    rows = cols[0].shape[0]
    lane = lax.broadcasted_iota(jnp.int32, (rows, 128), 1)
    out = jnp.zeros((rows, 128), F32)
    for h, cv in enumerate(cols):
        out = jnp.where(lane == h, cv, out)
    return out


def _params(sem, vmem=VMEM_LIMIT):
    return pltpu.CompilerParams(dimension_semantics=sem, vmem_limit_bytes=vmem)


def _full(shape):
    nd = len(shape)
    return pl.BlockSpec(shape, lambda *_: (0,) * nd)


def _rows(tm, width, cb=0):
    return pl.BlockSpec((tm, width), lambda i: (i, cb))


GATE_TILE = 512


def _in_proj(x, g, wt, shards=()):
    S = x.shape[0]
    tm, tnb = 1024, D_BRANCHES // 4
    njb, njg = 4, D_GATES // GATE_TILE
    n = len(shards)
    ni, nj = S // tm, njb + njg

    def body(x_ref, g_ref, wbr_ref, wg_ref, *rest):
        ins, (proj_ref, gates_ref, h_ref), outs = rest[:n], rest[n:n + 3], rest[n + 3:2 * n + 3]
        hs, sems = rest[2 * n + 3], rest[2 * n + 4:]
        i, j = pl.program_id(0), pl.program_id(1)

        if n:
            @pl.when(jnp.logical_and(i == 0, j == 0))
            def _():
                _comm_start(_ag_first(ins, outs, *sems))

        @pl.when(j == 0)
        def _():
            xf = x_ref[...]
            r = lax.rsqrt(jnp.mean(xf * xf, axis=-1, keepdims=True) + EPS)
            h = (xf * r * g_ref[...]).astype(BF16)
            hs[...] = h
            h_ref[...] = h

        @pl.when(j < njb)
        def _():
            proj_ref[...] = _dot_nt(hs[...], wbr_ref[...]).astype(BF16)

        @pl.when(j >= njb)
        def _():
            gates_ref[...] = _dot_nt(hs[...], wg_ref[...]).astype(BF16)

        if n:
            @pl.when(jnp.logical_and(i == ni - 1, j == nj - 1))
            def _():
                _comm_wait(_ag_first(ins, outs, *sems))

    def first(j):
        return jnp.minimum(j, njb - 1)

    def second(j):
        return jnp.maximum(j - njb, 0)

    res = pl.pallas_call(
        body, name="in_proj_gather" if n else "in_proj",
        grid=(ni, nj),
        in_specs=[pl.BlockSpec((tm, D_MODEL), lambda i, j: (i, 0)),
                  pl.BlockSpec((1, D_MODEL), lambda i, j: (0, 0)),
                  pl.BlockSpec((tnb, D_MODEL), lambda i, j: (first(j), 0)),
                  pl.BlockSpec((GATE_TILE, D_MODEL), lambda i, j: (D_BRANCHES // GATE_TILE + second(j), 0))]
                 + [ANY] * n,
        out_specs=[pl.BlockSpec((tm, tnb), lambda i, j: (i, first(j))),
                   pl.BlockSpec((tm, GATE_TILE), lambda i, j: (i, second(j))),
                   pl.BlockSpec((tm, D_MODEL), lambda i, j: (i, 0))] + [ANY] * n,
        out_shape=[jax.ShapeDtypeStruct((S, D_BRANCHES), BF16), jax.ShapeDtypeStruct((S, D_GATES), BF16),
                   jax.ShapeDtypeStruct((S, D_MODEL), BF16)]
                  + [jax.ShapeDtypeStruct((N_DEV,) + s.shape, s.dtype) for s in shards],
        scratch_shapes=[pltpu.VMEM((tm, D_MODEL), BF16)] + (_dma_sems(4 * n, 4 * n, n) if n else []),
        compiler_params=_params(("arbitrary", "arbitrary")),
    )(x, g, wt, wt, *shards)
    return res[0], res[1], res[2], list(res[3:])


def _mem_kv(mem, g, w):
    M = mem.shape[0]

    def body(m_ref, g_ref, w_ref, kv_ref, mn_ref):
        xf = m_ref[...]
        r = lax.rsqrt(jnp.mean(xf * xf, axis=-1, keepdims=True) + EPS)
        mn = (xf * r * g_ref[...]).astype(BF16)
        mn_ref[...] = mn
        kv_ref[...] = _dot(mn, w_ref[...]).astype(BF16)

    return pl.pallas_call(
        body, name="mem_kv",
        out_shape=[jax.ShapeDtypeStruct((M, 2 * WIDTH), BF16), jax.ShapeDtypeStruct((M, D_MODEL), BF16)],
        compiler_params=pltpu.CompilerParams(vmem_limit_bytes=VMEM_LIMIT),
    )(mem, g, w)


def _band_masks(win):
    t = lax.broadcasted_iota(jnp.int32, (CHUNK, CHUNK), 0)
    s = lax.broadcasted_iota(jnp.int32, (CHUNK, CHUNK), 1)
    cur = jnp.logical_and(t - s >= 0, t - s < win)
    prev = s > t + CHUNK - win
    return cur.astype(BF16), prev.astype(BF16)


def _inv_count(first_row, win):
    t = first_row + lax.broadcasted_iota(jnp.int32, (CHUNK, 1), 0)
    return 1.0 / jnp.minimum(t + 1, win).astype(F32)


def _layer_norm_fwd(v):
    mu = jnp.mean(v, axis=-1, keepdims=True)
    vc = v - mu
    var = jnp.mean(vc * vc, axis=-1, keepdims=True)
    rstd = lax.rsqrt(var + EPS)
    return vc * rstd, rstd


def _mem_softmax(q, kmem):
    s = _dot_nt(q, kmem) * ATT_SCALE
    m = jnp.max(s, axis=-1, keepdims=True)
    e = jnp.exp(s - m)
    return e * (1.0 / jnp.sum(e, axis=-1, keepdims=True))


def _abm_fwd(proj, ln_g, ln_b, wsm, bias_full, pool_w, pool_scale, kv, gathered=()):
    S = proj.shape[0]
    tm = 512
    nchunk = tm // CHUNK
    n = len(gathered)
    nsteps = S // tm

    def body(u_ref, v_ref, ag_ref, p_ref, ph_ref, pg_ref, mq_ref, mg_ref, lng_ref, lnb_ref, wsm_ref, bias_ref,
             pw_ref, ps_ref, kv_ref, *rest):
        y_ref, bufs, mix, sems = rest[n], rest[n + 1:2 * n + 1], rest[2 * n + 1], rest[2 * n + 2:]
        i = pl.program_id(0)

        if n:
            @pl.when(i == 0)
            def _():
                _comm_start(_ag_second(bufs, *sems))

        u = _gelu(u_ref[...].astype(F32))
        v = _gelu(v_ref[...].astype(F32))
        vhat, _ = _layer_norm_fwd(v)
        vln = (vhat * lng_ref[...] + lnb_ref[...]).astype(BF16)
        for c in range(nchunk):
            for h in range(N_HEAD):
                rs, cs = slice(c * CHUNK, (c + 1) * CHUNK), slice(h * HEAD, (h + 1) * HEAD)
                mix[rs, cs] = _dot(wsm_ref[h], vln[rs, cs]) + bias_ref[:, cs]
        y_ref[0] = (u * mix[...] * _silu(ag_ref[...].astype(F32))).astype(BF16)
        halo_ok = (i > 0).astype(F32)
        for c in range(nchunk):
            rs = slice(c * CHUNK, (c + 1) * CHUNK)
            for g, win in enumerate(POOL_WINDOWS):
                cs = slice(g * HEAD, (g + 1) * HEAD)
                bcur, bprev = _band_masks(win)
                cur = p_ref[rs, cs]
                if c == 0:
                    prev = (ph_ref[:, cs].astype(F32) * halo_ok).astype(BF16)
                else:
                    prev = p_ref[(c - 1) * CHUNK:c * CHUNK, cs]
                sums = _dot(bcur, cur) + _dot(bprev, prev)
                dm = sums * _inv_count(i * tm + c * CHUNK, win) - cur.astype(F32)
                mix[rs, cs] = _dot(dm.astype(BF16), pw_ref[g])
        y_ref[1] = (mix[...] * ps_ref[...] * _silu(pg_ref[...].astype(F32))).astype(BF16)
        for h in range(N_HEAD):
            cs = slice(h * HEAD, (h + 1) * HEAD)
            p = _mem_softmax(mq_ref[:, cs], kv_ref[:, cs])
            mix[:, cs] = _dot(p.astype(BF16), kv_ref[:, WIDTH + h * HEAD:WIDTH + (h + 1) * HEAD])
        y_ref[2] = (mix[...] * _silu(mg_ref[...].astype(F32))).astype(BF16)

        if n:
            @pl.when(i == nsteps - 1)
            def _():
                _comm_wait(_ag_second(bufs, *sems))

    blk = tm // CHUNK
    res = pl.pallas_call(
        body, name="abm_fwd_gather" if n else "abm_fwd",
        grid=(nsteps,),
        in_specs=[_rows(tm, WIDTH, CB_U), _rows(tm, WIDTH, CB_V), _rows(tm, WIDTH, CB_AGATE),
                  _rows(tm, WIDTH, CB_PIN),
                  pl.BlockSpec((CHUNK, WIDTH), lambda i: (jnp.maximum(i * blk - 1, 0), CB_PIN)),
                  _rows(tm, WIDTH, CB_PGATE), _rows(tm, WIDTH, CB_MQ), _rows(tm, WIDTH, CB_MGATE),
                  _full((1, WIDTH)), _full((1, WIDTH)), _full((N_HEAD, CHUNK, CHUNK)), _full((CHUNK, WIDTH)),
                  _full((4, HEAD, HEAD)), _full((1, WIDTH)), _full((MEM_LEN, 2 * WIDTH))] + [ANY] * n,
        out_specs=[pl.BlockSpec((3, tm, WIDTH), lambda i: (0, i, 0))] + [ANY] * n,
        out_shape=[jax.ShapeDtypeStruct((4, S, WIDTH), BF16)]
                  + [jax.ShapeDtypeStruct(b.shape, b.dtype) for b in gathered],
        input_output_aliases={15 + a: 1 + a for a in range(n)},
        scratch_shapes=[pltpu.VMEM((tm, WIDTH), F32)] + (_dma_sems(3 * n, 3 * n) if n else []),
        compiler_params=_params(("arbitrary",)),
    )(proj, proj, proj, proj, proj, proj, proj, proj, ln_g, ln_b, wsm, bias_full, pool_w, pool_scale, kv, *gathered)
    return res[0], list(res[1:])


ATT_TILE = 512


def _attn_fwd(q, qcb, k, kcb, v, vcb, bps):
    S = q.shape[0]
    tm = ATT_TILE
    nb = tm // CHUNK

    nblocks = nb * N_HEAD

    def body(q_ref, k_ref, v_ref, kh_ref, vh_ref, o_ref, l_ref, sc_s, sp_s, pc_s, pp_s):
        i = pl.program_id(0)

        def prev_kv(n, cs):
            if n == 0:
                return kh_ref[:, cs], vh_ref[:, cs]
            ps = slice((n - 1) * CHUNK, n * CHUNK)
            return k_ref[ps, cs], v_ref[ps, cs]

        pens = []
        for n in range(nb):
            rs = slice(n * CHUNK, (n + 1) * CHUNK)
            pens.append(jnp.full((N_HEAD * CHUNK, 1), jnp.where((i * nb + n) % bps != 0, 0.0, NEG), F32))
            for h in range(N_HEAD):
                cs = slice(h * HEAD, (h + 1) * HEAD)
                bs = slice((n * N_HEAD + h) * CHUNK, (n * N_HEAD + h + 1) * CHUNK)
                qh = q_ref[rs, cs]
                sc_s[bs, :] = _dot_nt(qh, k_ref[rs, cs])
                sp_s[bs, :] = _dot_nt(qh, prev_kv(n, cs)[0])
        row = lax.broadcasted_iota(jnp.int32, (nblocks * CHUNK, CHUNK), 0) & (CHUNK - 1)
        col = lax.broadcasted_iota(jnp.int32, (nblocks * CHUNK, CHUNK), 1)
        sc = jnp.where(col <= row, sc_s[...] * ATT_SCALE, NEG)
        sp = jnp.where(col >= row, sp_s[...] * ATT_SCALE, NEG) + jnp.concatenate(pens, axis=0)
        m = jnp.maximum(jnp.max(sc, axis=-1, keepdims=True), jnp.max(sp, axis=-1, keepdims=True))
        ec = jnp.exp(sc - m)
        ep = jnp.exp(sp - m)
        den = jnp.sum(ec, axis=-1, keepdims=True) + jnp.sum(ep, axis=-1, keepdims=True)
        inv = 1.0 / den
        pc_s[...] = (ec * inv).astype(BF16)
        pp_s[...] = (ep * inv).astype(BF16)
        lse = m + jnp.log(den)
        for n in range(nb):
            rs = slice(n * CHUNK, (n + 1) * CHUNK)
            for h in range(N_HEAD):
                cs = slice(h * HEAD, (h + 1) * HEAD)
                bs = slice((n * N_HEAD + h) * CHUNK, (n * N_HEAD + h + 1) * CHUNK)
                o = _dot(pc_s[bs, :], v_ref[rs, cs]) + _dot(pp_s[bs, :], prev_kv(n, cs)[1])
                o_ref[rs, cs] = o.astype(BF16)
            l_ref[rs, :] = _put_cols([lse[(n * N_HEAD + h) * CHUNK:(n * N_HEAD + h + 1) * CHUNK]
                                      for h in range(N_HEAD)])

    def halo(cb):
        return pl.BlockSpec((CHUNK, WIDTH), lambda i: (jnp.maximum(i * nb - 1, 0), cb))

    return pl.pallas_call(
        body, name=f"attn_fwd_{bps}",
        grid=(S // tm,),
        in_specs=[_rows(tm, WIDTH, qcb), _rows(tm, WIDTH, kcb), _rows(tm, WIDTH, vcb), halo(kcb), halo(vcb)],
        out_specs=[_rows(tm, WIDTH), _rows(tm, 128)],
        out_shape=[jax.ShapeDtypeStruct((S, WIDTH), BF16), jax.ShapeDtypeStruct((S, 128), F32)],
        scratch_shapes=[pltpu.VMEM((nblocks * CHUNK, CHUNK), F32), pltpu.VMEM((nblocks * CHUNK, CHUNK), F32),
                        pltpu.VMEM((nblocks * CHUNK, CHUNK), BF16), pltpu.VMEM((nblocks * CHUNK, CHUNK), BF16)],
        compiler_params=_params(("parallel",)),
    )(q, k, v, k, v)


def _gate_specs(tm):
    return [pl.BlockSpec((tm, D_MODEL), lambda i, b=b: (i, b)) for b in range(4)]


Y_SLOT = (0, 1, 3, 2)


def _merge_fwd(x, y4, o_g, l_g, proj, gates, wb, wout, shards=()):
    S = x.shape[0]
    tm = 256
    n = len(shards)
    nsteps = S // tm
    forward_at = nsteps - 4

    def body(x_ref, y_ref, o0, o1, o2, l0, l1, l2, cg_ref, *rest):
        gm = rest[:4]
        wb_ref, wo_ref = rest[4:6]
        s_in = rest[6:6 + n]
        xn_ref, yc_ref, oc_ref, lse_ref, z_ref = rest[6 + n:11 + n]
        s_out, ocs, sems = rest[11 + n:11 + 2 * n], rest[11 + 2 * n], rest[12 + 2 * n:]
        i = pl.program_id(0)

        if n:
            @pl.when(i == 0)
            def _():
                _comm_start(_ag_first(s_in, s_out, *sems[:3]))

            @pl.when(i == forward_at)
            def _():
                incoming = _ag_first(s_in, s_out, *sems[:3])[2]
                for a in range(n):
                    for k in range(1, 4):
                        incoming[4 * a + k].wait_recv()
                _comm_start(_ag_second(s_out, *sems[3:]))

        lcols = []
        for h in range(N_HEAD):
            cs = slice(h * HEAD, (h + 1) * HEAD)
            ls = [_col(l[...], h) for l in (l0, l1, l2)]
            m = jnp.maximum(jnp.maximum(ls[0], ls[1]), ls[2])
            tot = jnp.exp(ls[0] - m) + jnp.exp(ls[1] - m) + jnp.exp(ls[2] - m)
            lse = m + jnp.log(tot)
            ocs[:, cs] = sum(jnp.exp(lg - lse) * o[:, cs].astype(F32) for lg, o in zip(ls, (o0, o1, o2)))
            lcols.append(lse)
        lse_ref[...] = _put_cols(lcols)
        oc = ocs[...]
        oc_ref[...] = oc.astype(BF16)
        yc = (oc * _silu(cg_ref[...].astype(F32))).astype(BF16)
        yc_ref[...] = yc
        ys = (y_ref[0], y_ref[1], yc, y_ref[2])
        z = jnp.zeros((tm, D_MODEL), F32)
        for b in range(4):
            z = z + _sigmoid(gm[b][...].astype(F32)) * _dot(ys[b], wb_ref[b])
        zb = z.astype(BF16)
        z_ref[...] = zb
        xn_ref[...] = x_ref[...] + _dot(zb, wo_ref[...])

        if n:
            @pl.when(i == nsteps - 1)
            def _():
                local, out, incoming = _ag_first(s_in, s_out, *sems[:3])
                for a in range(n):
                    incoming[4 * a].wait_recv()
                _comm_wait(_ag_second(s_out, *sems[3:]))
                for cp in out:
                    cp.wait_send()
                for cp in local:
                    cp.wait()

    res = pl.pallas_call(
        body, name="merge_fwd_gather" if n else "merge_fwd",
        grid=(nsteps,),
        in_specs=[_rows(tm, D_MODEL), pl.BlockSpec((3, tm, WIDTH), lambda i: (0, i, 0)),
                  _rows(tm, WIDTH), _rows(tm, WIDTH), _rows(tm, WIDTH),
                  _rows(tm, 128), _rows(tm, 128), _rows(tm, 128),
                  _rows(tm, WIDTH, CB_CGATE)] + _gate_specs(tm)
                 + [_full((4, WIDTH, D_MODEL)), _full((D_MODEL, D_MODEL))] + [ANY] * n,
        out_specs=[_rows(tm, D_MODEL), pl.BlockSpec((None, tm, WIDTH), lambda i: (Y_SLOT[2], i, 0)),
                   _rows(tm, WIDTH), _rows(tm, 128), _rows(tm, D_MODEL)] + [ANY] * n,
        out_shape=[jax.ShapeDtypeStruct((S, D_MODEL), F32), jax.ShapeDtypeStruct(y4.shape, BF16),
                   jax.ShapeDtypeStruct((S, WIDTH), BF16), jax.ShapeDtypeStruct((S, 128), F32),
                   jax.ShapeDtypeStruct((S, D_MODEL), BF16)]
                  + [jax.ShapeDtypeStruct((N_DEV,) + s.shape, s.dtype) for s in shards],
        input_output_aliases={1: 1},
        scratch_shapes=[pltpu.VMEM((tm, WIDTH), F32)] + (_dma_sems(4 * n, 4 * n, n, 3 * n, 3 * n) if n else []),
        compiler_params=_params(("arbitrary",)),
    )(x, y4, *o_g, *l_g, proj, *([gates] * 4), wb, wout, *shards)
    return res[:5], list(res[5:])


def _loss_head(x, g, tgt):
    S = x.shape[0]
    tm = 512

    def body(x_ref, g_ref, t_ref, loss_ref, dx_ref, dg_ref):
        @pl.when(pl.program_id(0) == 0)
        def _():
            loss_ref[...] = jnp.zeros_like(loss_ref)
            dg_ref[...] = jnp.zeros_like(dg_ref)

        xf = x_ref[...]
        r = lax.rsqrt(jnp.mean(xf * xf, axis=-1, keepdims=True) + EPS)
        xhat = xf * r
        gv = g_ref[...]
        err = xhat * gv - t_ref[...]
        e2 = jnp.sum(err * err, axis=-1, keepdims=True)
        loss_ref[...] += (0.5 / D_MODEL) * jnp.sum(e2, axis=0, keepdims=True)
        dy = err * (1.0 / D_MODEL)
        dg_ref[...] += jnp.sum(dy * xhat, axis=0, keepdims=True)
        dxh = dy * gv
        dx_ref[...] = r * (dxh - xhat * jnp.mean(dxh * xhat, axis=-1, keepdims=True))

    return pl.pallas_call(
        body, name="loss_head",
        grid=(S // tm,),
        in_specs=[_rows(tm, D_MODEL), _full((1, D_MODEL)), _rows(tm, D_MODEL)],
        out_specs=[_full((1, 128)), _rows(tm, D_MODEL), _full((1, D_MODEL))],
        out_shape=[jax.ShapeDtypeStruct((1, 128), F32), jax.ShapeDtypeStruct((S, D_MODEL), F32),
                   jax.ShapeDtypeStruct((1, D_MODEL), F32)],
        compiler_params=_params(("arbitrary",)),
    )(x, g, tgt)


def _merge_bwd(dxo, y4, oc, proj, gates, wb, wout, grads=()):
    S = dxo.shape[0]
    tm = 256
    n = len(grads)
    nsteps = S // tm

    def body(dx_ref, y_ref, oc_ref, cg_ref, *rest):
        gm = rest[:4]
        wb_ref, wo_ref = rest[4:6]
        g_in = rest[6:6 + n]
        dy_ref, doc_ref, delta_ref, dcg_ref, dgm_ref, dt_ref = rest[6 + n:12 + n]
        g_out, sems = rest[12 + n:12 + 2 * n], rest[12 + 2 * n:]
        i = pl.program_id(0)

        if n:
            @pl.when(i == 0)
            def _():
                _comm_start(_rs_first(g_in, g_out, *sems))

        dz = _dot_nt(dx_ref[...].astype(BF16), wo_ref[...])
        for b in range(4):
            gate = _sigmoid(gm[b][...].astype(F32))
            t = _dot(y_ref[Y_SLOT[b]], wb_ref[b])
            dgm_ref[:, b * D_MODEL:(b + 1) * D_MODEL] = (dz * t * gate * (1.0 - gate)).astype(BF16)
            dt = (dz * gate).astype(BF16)
            dt_ref[b] = dt
            dyb = _dot_nt(dt, wb_ref[b])
            if b == 2:
                cg = cg_ref[...].astype(F32)
                oc = oc_ref[...].astype(F32)
                scg, dscg = _silu_and_grad(cg)
                doc = dyb * scg
                dcg_ref[...] = (dyb * oc * dscg).astype(BF16)
                doc_ref[...] = doc.astype(BF16)
                prod = doc * oc
                delta_ref[...] = _put_cols([jnp.sum(prod[:, h * HEAD:(h + 1) * HEAD], axis=1, keepdims=True)
                                            for h in range(N_HEAD)])
            else:
                dy_ref[b if b < 2 else 2] = dyb.astype(BF16)

        if n:
            @pl.when(i == nsteps - 1)
            def _():
                _comm_wait(_rs_first(g_in, g_out, *sems))

    res = pl.pallas_call(
        body, name="merge_bwd_scatter" if n else "merge_bwd",
        grid=(nsteps,),
        in_specs=[_rows(tm, D_MODEL), pl.BlockSpec((4, tm, WIDTH), lambda i: (0, i, 0)),
                  _rows(tm, WIDTH), _rows(tm, WIDTH, CB_CGATE)] + _gate_specs(tm)
                 + [_full((4, WIDTH, D_MODEL)), _full((D_MODEL, D_MODEL))] + [ANY] * n,
        out_specs=[pl.BlockSpec((3, tm, WIDTH), lambda i: (0, i, 0)), _rows(tm, WIDTH), _rows(tm, 128),
                   _rows(tm, WIDTH), _rows(tm, 4 * D_MODEL), pl.BlockSpec((4, tm, D_MODEL), lambda i: (0, i, 0))]
                  + [ANY] * n,
        out_shape=[jax.ShapeDtypeStruct((3, S, WIDTH), BF16), jax.ShapeDtypeStruct((S, WIDTH), BF16),
                   jax.ShapeDtypeStruct((S, 128), F32), jax.ShapeDtypeStruct((S, WIDTH), BF16),
                   jax.ShapeDtypeStruct((S, 4 * D_MODEL), BF16), jax.ShapeDtypeStruct((4, S, D_MODEL), BF16)]
                  + [jax.ShapeDtypeStruct(g.shape[:1] + g.shape[2:], g.dtype) for g in grads],
        scratch_shapes=_dma_sems(N_CHIP * n, N_CHIP * n) if n else [],
        compiler_params=_params(("arbitrary",)),
    )(dxo, y4, oc, proj, *([gates] * 4), wb, wout, *grads)
    return res[:6], list(res[6:])


def _dw_branch(y4, dt):
    S = y4.shape[1]
    tk = 1024
    nk = S // tk

    def body(y_ref, dt_ref, o_ref, acc):
        kk = pl.program_id(1)

        @pl.when(kk == 0)
        def _():
            acc[...] = jnp.zeros_like(acc)

        acc[...] += _dot_tn(y_ref[...], dt_ref[...])

        @pl.when(kk == nk - 1)
        def _():
            o_ref[...] = acc[...].astype(BF16)

    def slot(b):
        return jnp.where(b == 2, Y_SLOT[2], jnp.where(b == 3, Y_SLOT[3], b))

    return pl.pallas_call(
        body, name="dw_branch",
        grid=(4, nk),
        in_specs=[pl.BlockSpec((None, tk, WIDTH), lambda b, k: (slot(b), k, 0)),
                  pl.BlockSpec((None, tk, D_MODEL), lambda b, k: (b, k, 0))],
        out_specs=pl.BlockSpec((None, WIDTH, D_MODEL), lambda b, k: (b, 0, 0)),
        out_shape=jax.ShapeDtypeStruct((4, WIDTH, D_MODEL), BF16),
        scratch_shapes=[pltpu.VMEM((WIDTH, D_MODEL), F32)],
        compiler_params=_params(("parallel", "arbitrary")),
    )(y4, dt)


def _attn_bwd(q, qcb, k, kcb, v, vcb, do, lse, delta, bps):
    S = q.shape[0]
    tm = ATT_TILE
    nb = tm // CHUNK
    nblk = S // CHUNK

    ncur = nb * N_HEAD
    nprev = (nb + 1) * N_HEAD

    def body(q_ref, k_ref, v_ref, do_ref, l_ref, d_ref, kh_ref, vh_ref, qn_ref, don_ref, ln_ref, dn_ref,
             dq_ref, dk_ref, dv_ref, sc_s, sp_s, dpc_s, dpp_s, pc_s, pp_s, dsc_s, dsp_s):
        i = pl.program_id(0)

        def rows_of(n):
            if n < nb:
                rs = slice(n * CHUNK, (n + 1) * CHUNK)
                return rs, q_ref, do_ref, l_ref, d_ref
            return slice(0, CHUNK), qn_ref, don_ref, ln_ref, dn_ref

        def prev_kv(n, cs):
            if n == 0:
                return kh_ref[:, cs], vh_ref[:, cs]
            ps = slice((n - 1) * CHUNK, n * CHUNK)
            return k_ref[ps, cs], v_ref[ps, cs]

        def blk(n, h):
            return slice((n * N_HEAD + h) * CHUNK, (n * N_HEAD + h + 1) * CHUNK)

        pens, lses, deltas = [], [], []
        for n in range(nb + 1):
            rs, qr, dor, lr, dr = rows_of(n)
            gb = i * nb + n
            pen = jnp.where(gb % bps != 0, 0.0, NEG)
            if n == nb:
                pen = pen + jnp.where(gb < nblk, 0.0, NEG)
            pens.append(jnp.full((N_HEAD * CHUNK, 1), pen, F32))
            lblk, dblk = lr[rs, :], dr[rs, :]
            for h in range(N_HEAD):
                cs = slice(h * HEAD, (h + 1) * HEAD)
                qh, doh = qr[rs, cs], dor[rs, cs]
                lses.append(_col(lblk, h))
                deltas.append(_col(dblk, h))
                kp, vp = prev_kv(n, cs)
                sp_s[blk(n, h), :] = _dot_nt(qh, kp)
                dpp_s[blk(n, h), :] = _dot_nt(doh, vp)
                if n < nb:
                    sc_s[blk(n, h), :] = _dot_nt(qh, k_ref[rs, cs])
                    dpc_s[blk(n, h), :] = _dot_nt(doh, v_ref[rs, cs])
        lse = jnp.concatenate(lses, axis=0)
        delta = jnp.concatenate(deltas, axis=0)
        row = lax.broadcasted_iota(jnp.int32, (nprev * CHUNK, CHUNK), 0) & (CHUNK - 1)
        col = lax.broadcasted_iota(jnp.int32, (nprev * CHUNK, CHUNK), 1)
        sp = jnp.where(col >= row, sp_s[...] * ATT_SCALE, NEG) + jnp.concatenate(pens, axis=0)
        pp = jnp.exp(sp - lse)
        pp_s[...] = pp.astype(BF16)
        dsp_s[...] = (pp * (dpp_s[...] - delta)).astype(BF16)
        nc = ncur * CHUNK
        sc = jnp.where(col[:nc] <= row[:nc], sc_s[...] * ATT_SCALE, NEG)
        pc = jnp.exp(sc - lse[:nc])
        pc_s[...] = pc.astype(BF16)
        dsc_s[...] = (pc * (dpc_s[...] - delta[:nc])).astype(BF16)
        for n in range(nb):
            rs, qr, dor, _, _ = rows_of(n)
            rn, qnr, donr, _, _ = rows_of(n + 1)
            for h in range(N_HEAD):
                cs = slice(h * HEAD, (h + 1) * HEAD)
                kp, _ = prev_kv(n, cs)
                dq = _dot(dsc_s[blk(n, h), :], k_ref[rs, cs]) + _dot(dsp_s[blk(n, h), :], kp)
                dq_ref[rs, cs] = (dq * ATT_SCALE).astype(BF16)
                dk = _dot_tn(dsc_s[blk(n, h), :], qr[rs, cs]) + _dot_tn(dsp_s[blk(n + 1, h), :], qnr[rn, cs])
                dk_ref[rs, cs] = (dk * ATT_SCALE).astype(BF16)
                dv = _dot_tn(pc_s[blk(n, h), :], dor[rs, cs]) + _dot_tn(pp_s[blk(n + 1, h), :], donr[rn, cs])
                dv_ref[rs, cs] = dv.astype(BF16)

    def prev_halo(cb):
        return pl.BlockSpec((CHUNK, WIDTH), lambda i: (jnp.maximum(i * nb - 1, 0), cb))

    def next_halo(width, cb=0):
        return pl.BlockSpec((CHUNK, width), lambda i: (jnp.minimum(i * nb + nb, nblk - 1), cb))

    return pl.pallas_call(
        body, name=f"attn_bwd_{bps}",
        grid=(S // tm,),
        in_specs=[_rows(tm, WIDTH, qcb), _rows(tm, WIDTH, kcb), _rows(tm, WIDTH, vcb), _rows(tm, WIDTH),
                  _rows(tm, 128), _rows(tm, 128), prev_halo(kcb), prev_halo(vcb),
                  next_halo(WIDTH, qcb), next_halo(WIDTH), next_halo(128), next_halo(128)],
        out_specs=[_rows(tm, WIDTH), _rows(tm, WIDTH), _rows(tm, WIDTH)],
        out_shape=[jax.ShapeDtypeStruct((S, WIDTH), BF16)] * 3,
        scratch_shapes=[pltpu.VMEM((ncur * CHUNK, CHUNK), F32), pltpu.VMEM((nprev * CHUNK, CHUNK), F32),
                        pltpu.VMEM((ncur * CHUNK, CHUNK), F32), pltpu.VMEM((nprev * CHUNK, CHUNK), F32),
                        pltpu.VMEM((ncur * CHUNK, CHUNK), BF16), pltpu.VMEM((nprev * CHUNK, CHUNK), BF16),
                        pltpu.VMEM((ncur * CHUNK, CHUNK), BF16), pltpu.VMEM((nprev * CHUNK, CHUNK), BF16)],
        compiler_params=_params(("parallel",)),
    )(q, k, v, do, lse, delta, k, v, q, do, lse, delta)


def _dilated_split(d):
    hp = min(N_HEAD, 16 // d)
    return hp, N_HEAD // hp, HEAD * hp


def _by_class(src_ref, dst, d, hp):
    for j in range(hp):
        dst[j] = pltpu.einshape("(tr)l->(rt)l", src_ref[:, j * HEAD:(j + 1) * HEAD], r=d)


def _from_class(src, dst_ref, d, hp):
    for j in range(hp):
        dst_ref[:, j * HEAD:(j + 1) * HEAD] = pltpu.einshape("(rt)l->(tr)l", src[j].astype(BF16), r=d)


def _attn_fwd_dilated(proj, qcb, kcb, vcb, d):
    S = proj.shape[0]
    T = CHUNK * d
    hp, nh, cw = _dilated_split(d)
    nblocks = d * hp

    def body(q_ref, k_ref, v_ref, o_ref, l_ref, qf, kf, vf, kpf, vpf, kst, vst, of, lf, sc_s, sp_s, pc_s, pp_s):
        i, hh = pl.program_id(0), pl.program_id(1)
        _by_class(q_ref, qf, d, hp)
        _by_class(k_ref, kf, d, hp)
        _by_class(v_ref, vf, d, hp)

        @pl.when(i == 0)
        def _():
            kpf[...] = jnp.zeros_like(kpf)
            vpf[...] = jnp.zeros_like(vpf)

        @pl.when(i > 0)
        def _():
            kpf[...] = kst[hh]
            vpf[...] = vst[hh]

        def blk(ref, r, j):
            return ref[j, r * CHUNK:(r + 1) * CHUNK, :]

        def bs(r, j):
            return slice((r * hp + j) * CHUNK, (r * hp + j + 1) * CHUNK)

        for r in range(d):
            for j in range(hp):
                qb = blk(qf, r, j)
                sc_s[bs(r, j), :] = _dot_nt(qb, blk(kf, r, j))
                sp_s[bs(r, j), :] = _dot_nt(qb, blk(kpf, r, j))
        row = lax.broadcasted_iota(jnp.int32, (nblocks * CHUNK, CHUNK), 0) & (CHUNK - 1)
        col = lax.broadcasted_iota(jnp.int32, (nblocks * CHUNK, CHUNK), 1)
        sc = jnp.where(col <= row, sc_s[...] * ATT_SCALE, NEG)
        sp = jnp.where(col >= row, sp_s[...] * ATT_SCALE, NEG) + jnp.where(i > 0, 0.0, NEG)
        m = jnp.maximum(jnp.max(sc, axis=-1, keepdims=True), jnp.max(sp, axis=-1, keepdims=True))
        ec = jnp.exp(sc - m)
        ep = jnp.exp(sp - m)
        den = jnp.sum(ec, axis=-1, keepdims=True) + jnp.sum(ep, axis=-1, keepdims=True)
        inv = 1.0 / den
        pc_s[...] = (ec * inv).astype(BF16)
        pp_s[...] = (ep * inv).astype(BF16)
        lse = m + jnp.log(den)
        lane = lax.broadcasted_iota(jnp.int32, (CHUNK, 128), 1)
        for r in range(d):
            lblk = jnp.zeros((CHUNK, 128), F32)
            for j in range(hp):
                o = _dot(pc_s[bs(r, j), :], blk(vf, r, j)) + _dot(pp_s[bs(r, j), :], blk(vpf, r, j))
                of[j, r * CHUNK:(r + 1) * CHUNK, :] = o
                lblk = jnp.where(lane == hh * hp + j, lse[bs(r, j)], lblk)
            lf[r * CHUNK:(r + 1) * CHUNK, :] = lblk
        _from_class(of, o_ref, d, hp)
        lnat = pltpu.einshape("(rt)l->(tr)l", lf[...], r=d)

        @pl.when(hh == 0)
        def _():
            l_ref[...] = lnat

        @pl.when(hh > 0)
        def _():
            l_ref[...] += lnat

        kst[hh] = kf[...]
        vst[hh] = vf[...]

    def cols(cb):
        return pl.BlockSpec((T, cw), lambda i, hh: (i, cb * nh + hh))

    tile = pltpu.VMEM((hp, T, HEAD), BF16)
    return pl.pallas_call(
        body, name=f"attn_fwd_dilated_{d}",
        grid=(S // T, nh),
        in_specs=[cols(qcb), cols(kcb), cols(vcb)],
        out_specs=[cols(0), pl.BlockSpec((T, 128), lambda i, hh: (i, 0))],
        out_shape=[jax.ShapeDtypeStruct((S, WIDTH), BF16), jax.ShapeDtypeStruct((S, 128), F32)],
        scratch_shapes=[tile, tile, tile, tile, tile,
                        pltpu.VMEM((nh, hp, T, HEAD), BF16), pltpu.VMEM((nh, hp, T, HEAD), BF16),
                        pltpu.VMEM((hp, T, HEAD), F32), pltpu.VMEM((T, 128), F32),
                        pltpu.VMEM((nblocks * CHUNK, CHUNK), F32), pltpu.VMEM((nblocks * CHUNK, CHUNK), F32),
                        pltpu.VMEM((nblocks * CHUNK, CHUNK), BF16), pltpu.VMEM((nblocks * CHUNK, CHUNK), BF16)],
        compiler_params=_params(("arbitrary", "arbitrary")),
    )(proj, proj, proj)


def _attn_bwd_dilated(proj, qcb, kcb, vcb, do, lse, delta, d):
    S = proj.shape[0]
    T = CHUNK * d
    nt = S // T
    hp, nh, cw = _dilated_split(d)
    nblocks = d * hp

    def body(q_ref, k_ref, v_ref, do_ref, l_ref, d_ref, dq_ref, dk_ref, dv_ref,
             qf, dof, kf, vf, kpf, vpf, dqf, acck, accv, newk, newv,
             sc_s, sp_s, dpc_s, dpp_s, pc_s, pp_s, dsc_s, dsp_s):
        hh, i = pl.program_id(0), pl.program_id(1)

        @pl.when(i == 0)
        def _():
            for ref in (kpf, vpf, acck, accv):
                ref[...] = jnp.zeros_like(ref)
            dk_ref[...] = jnp.zeros_like(dk_ref)
            dv_ref[...] = jnp.zeros_like(dv_ref)

        def blk(ref, r, j):
            return ref[j, r * CHUNK:(r + 1) * CHUNK, :]

        def bs(r, j):
            return slice((r * hp + j) * CHUNK, (r * hp + j + 1) * CHUNK)

        @pl.when(i < nt)
        def _():
            _by_class(q_ref, qf, d, hp)
            _by_class(do_ref, dof, d, hp)
            _by_class(k_ref, kf, d, hp)
            _by_class(v_ref, vf, d, hp)
            lses, deltas = [], []
            lcls = pltpu.einshape("(tr)l->(rt)l", l_ref[...], r=d)
            dcls = pltpu.einshape("(tr)l->(rt)l", d_ref[...], r=d)
            for r in range(d):
                lblk = lcls[r * CHUNK:(r + 1) * CHUNK]
                dblk = dcls[r * CHUNK:(r + 1) * CHUNK]
                for j in range(hp):
                    lses.append(_col(lblk, hh * hp + j))
                    deltas.append(_col(dblk, hh * hp + j))
                    qb, dob = blk(qf, r, j), blk(dof, r, j)
                    sc_s[bs(r, j), :] = _dot_nt(qb, blk(kf, r, j))
                    dpc_s[bs(r, j), :] = _dot_nt(dob, blk(vf, r, j))
                    sp_s[bs(r, j), :] = _dot_nt(qb, blk(kpf, r, j))
                    dpp_s[bs(r, j), :] = _dot_nt(dob, blk(vpf, r, j))
            lse = jnp.concatenate(lses, axis=0)
            delta = jnp.concatenate(deltas, axis=0)
            row = lax.broadcasted_iota(jnp.int32, (nblocks * CHUNK, CHUNK), 0) & (CHUNK - 1)
            col = lax.broadcasted_iota(jnp.int32, (nblocks * CHUNK, CHUNK), 1)
            sp = jnp.where(col >= row, sp_s[...] * ATT_SCALE, NEG) + jnp.where(i > 0, 0.0, NEG)
            pp = jnp.exp(sp - lse)
            pp_s[...] = pp.astype(BF16)
            dsp_s[...] = (pp * (dpp_s[...] - delta)).astype(BF16)
            sc = jnp.where(col <= row, sc_s[...] * ATT_SCALE, NEG)
            pc = jnp.exp(sc - lse)
            pc_s[...] = pc.astype(BF16)
            dsc_s[...] = (pc * (dpc_s[...] - delta)).astype(BF16)
            for r in range(d):
                rows = slice(r * CHUNK, (r + 1) * CHUNK)
                for j in range(hp):
                    qb, dob = blk(qf, r, j), blk(dof, r, j)
                    dsc, dsp = dsc_s[bs(r, j), :], dsp_s[bs(r, j), :]
                    dqf[j, rows, :] = (_dot(dsc, blk(kf, r, j)) + _dot(dsp, blk(kpf, r, j))) * ATT_SCALE
                    newk[j, rows, :] = _dot_tn(dsc, qb) * ATT_SCALE
                    newv[j, rows, :] = _dot_tn(pc_s[bs(r, j), :], dob)
                    acck[j, rows, :] += _dot_tn(dsp, qb) * ATT_SCALE
                    accv[j, rows, :] += _dot_tn(pp_s[bs(r, j), :], dob)
            _from_class(dqf, dq_ref, d, hp)

        @pl.when(i > 0)
        def _():
            _from_class(acck, dk_ref, d, hp)
            _from_class(accv, dv_ref, d, hp)

        @pl.when(i < nt)
        def _():
            acck[...] = newk[...]
            accv[...] = newv[...]
            kpf[...] = kf[...]
            vpf[...] = vf[...]

    def cur(width, cb, nsplit):
        return pl.BlockSpec((T, width), lambda hh, i: (jnp.minimum(i, nt - 1), cb * nsplit + hh * (nsplit > 1)))

    def lag():
        return pl.BlockSpec((T, cw), lambda hh, i: (jnp.maximum(i - 1, 0), hh))

    tile = pltpu.VMEM((hp, T, HEAD), BF16)
    acc = pltpu.VMEM((hp, T, HEAD), F32)
    f32s = pltpu.VMEM((nblocks * CHUNK, CHUNK), F32)
    b16s = pltpu.VMEM((nblocks * CHUNK, CHUNK), BF16)
    return pl.pallas_call(
        body, name=f"attn_bwd_dilated_{d}",
        grid=(nh, nt + 1),
        in_specs=[cur(cw, qcb, nh), cur(cw, kcb, nh), cur(cw, vcb, nh), cur(cw, 0, nh), cur(128, 0, 1), cur(128, 0, 1)],
        out_specs=[cur(cw, 0, nh), lag(), lag()],
        out_shape=[jax.ShapeDtypeStruct((S, WIDTH), BF16)] * 3,
        scratch_shapes=[tile] * 6 + [acc] * 5 + [f32s] * 4 + [b16s] * 4,
        compiler_params=_params(("arbitrary", "arbitrary")),
    )(proj, proj, proj, do, lse, delta)


def _abm_bwd(proj, dy3, ln_g, ln_b, wsm, wsm_t, bias_full, pool_w, pool_wt, pool_scale, kv):
    S = proj.shape[0]
    tm = 512
    nchunk = tm // CHUNK
    nblk = S // CHUNK

    def body(u_ref, v_ref, ag_ref, p_ref, ph_ref, pg_ref, pgn_ref, mq_ref, mg_ref, dy_ref, dypn_ref,
             lng_ref, lnb_ref, wsm_ref, wsmt_ref, bias_ref, pw_ref, pwt_ref, ps_ref, kv_ref,
             dab_ref, dm_ref, dlng_ref, dlnb_ref, dws_ref, dbias_ref, dpw_ref, dps_ref, dkv_ref,
             mix, dvl, ddn):
        i = pl.program_id(0)

        @pl.when(i == 0)
        def _():
            for r in (dlng_ref, dlnb_ref, dws_ref, dbias_ref, dpw_ref, dps_ref, dkv_ref):
                r[...] = jnp.zeros_like(r)

        au = u_ref[...].astype(F32)
        av = v_ref[...].astype(F32)
        ag = ag_ref[...].astype(F32)
        u, du = _gelu_and_grad(au)
        v, dgelu_v = _gelu_and_grad(av)
        vhat, rstd = _layer_norm_fwd(v)
        vln = (vhat * lng_ref[...] + lnb_ref[...]).astype(BF16)
        for c in range(nchunk):
            for h in range(N_HEAD):
                rs, cs = slice(c * CHUNK, (c + 1) * CHUNK), slice(h * HEAD, (h + 1) * HEAD)
                mix[rs, cs] = _dot(wsm_ref[h], vln[rs, cs]) + bias_ref[:, cs]
        dya = dy_ref[0].astype(F32)
        sg, dsg = _silu_and_grad(ag)
        mixed = mix[...]
        dab_ref[:, 2 * WIDTH:3 * WIDTH] = (dya * u * mixed * dsg).astype(BF16)
        dab_ref[:, 0:WIDTH] = (dya * mixed * sg * du).astype(BF16)
        dmixed = dya * u * sg
        dmb = dmixed.astype(BF16)
        tril = (lax.broadcasted_iota(jnp.int32, (CHUNK, CHUNK), 1)
                <= lax.broadcasted_iota(jnp.int32, (CHUNK, CHUNK), 0))
        for c in range(nchunk):
            rs = slice(c * CHUNK, (c + 1) * CHUNK)
            dbias_ref[...] += dmixed[rs, :]
            for h in range(N_HEAD):
                cs = slice(h * HEAD, (h + 1) * HEAD)
                dvl[rs, cs] = _dot(wsmt_ref[h], dmb[rs, cs])
                dws_ref[h] += jnp.where(tril, _dot_nt(dmb[rs, cs], vln[rs, cs]), 0.0)
        dvln = dvl[...]
        dlng_ref[...] += jnp.sum(dvln * vhat, axis=0, keepdims=True)
        dlnb_ref[...] += jnp.sum(dvln, axis=0, keepdims=True)
        dvh = dvln * lng_ref[...]
        dv = rstd * (dvh - jnp.mean(dvh, axis=-1, keepdims=True)
                     - vhat * jnp.mean(dvh * vhat, axis=-1, keepdims=True))
        dab_ref[:, WIDTH:2 * WIDTH] = (dv * dgelu_v).astype(BF16)

        halo_ok = (i > 0).astype(F32)
        for c in range(nchunk):
            rs = slice(c * CHUNK, (c + 1) * CHUNK)
            for g, win in enumerate(POOL_WINDOWS):
                cs = slice(g * HEAD, (g + 1) * HEAD)
                bcur, bprev = _band_masks(win)
                cur = p_ref[rs, cs]
                if c == 0:
                    prev = (ph_ref[:, cs].astype(F32) * halo_ok).astype(BF16)
                else:
                    prev = p_ref[(c - 1) * CHUNK:c * CHUNK, cs]
                sums = _dot(bcur, cur) + _dot(bprev, prev)
                dvl[rs, cs] = sums * _inv_count(i * tm + c * CHUNK, win) - cur.astype(F32)
        dmat = dvl[...].astype(BF16)
        for g in range(4):
            cs = slice(g * HEAD, (g + 1) * HEAD)
            mix[:, cs] = _dot(dmat[:, cs], pw_ref[g])
        yg = mix[...]
        pg = pg_ref[...].astype(F32)
        dyp = dy_ref[1].astype(F32)
        spg, dspg = _silu_and_grad(pg)
        dyy = dyp * spg
        scale = ps_ref[...]
        dab_ref[:, 4 * WIDTH:5 * WIDTH] = (dyp * yg * scale * dspg).astype(BF16)
        dps_ref[...] += jnp.sum(dyy * yg, axis=0, keepdims=True)
        dyg = (dyy * scale).astype(BF16)
        for g in range(4):
            cs = slice(g * HEAD, (g + 1) * HEAD)
            dpw_ref[g] += _dot_tn(dmat[:, cs], dyg[:, cs])
            mix[:, cs] = _dot(dyg[:, cs], pwt_ref[g])
        next_ok = (i + 1 < S // tm).astype(F32)
        dygn = (dypn_ref[...].astype(F32) * _silu(pgn_ref[...].astype(F32)) * scale * next_ok).astype(BF16)
        for c in range(nchunk + 1):
            for g, win in enumerate(POOL_WINDOWS):
                cs = slice(g * HEAD, (g + 1) * HEAD)
                if c < nchunk:
                    dd = mix[c * CHUNK:(c + 1) * CHUNK, cs]
                else:
                    dd = _dot(dygn[:, cs], pwt_ref[g])
                ddn[c * CHUNK:(c + 1) * CHUNK, cs] = dd * _inv_count(i * tm + c * CHUNK, win)
        ddnb = ddn[...].astype(BF16)
        for c in range(nchunk):
            rs = slice(c * CHUNK, (c + 1) * CHUNK)
            ns = slice((c + 1) * CHUNK, (c + 2) * CHUNK)
            for g, win in enumerate(POOL_WINDOWS):
                cs = slice(g * HEAD, (g + 1) * HEAD)
                bcur, bprev = _band_masks(win)
                dp = _dot_tn(bcur, ddnb[rs, cs]) + _dot_tn(bprev, ddnb[ns, cs]) - mix[rs, cs]
                dab_ref[rs, 3 * WIDTH + g * HEAD:3 * WIDTH + (g + 1) * HEAD] = dp.astype(BF16)

        mg = mg_ref[...].astype(F32)
        dym = dy_ref[2].astype(F32)
        smg, dsmg = _silu_and_grad(mg)
        dob = (dym * smg).astype(BF16)
        for h in range(N_HEAD):
            cs = slice(h * HEAD, (h + 1) * HEAD)
            vs = slice(WIDTH + h * HEAD, WIDTH + (h + 1) * HEAD)
            qh = mq_ref[:, cs]
            p = _mem_softmax(qh, kv_ref[:, cs])
            pb = p.astype(BF16)
            mix[:, cs] = _dot(pb, kv_ref[:, vs])
            dp = _dot_nt(dob[:, cs], kv_ref[:, vs])
            ds = (p * (dp - jnp.sum(p * dp, axis=-1, keepdims=True))).astype(BF16)
            dm_ref[:, cs] = (_dot(ds, kv_ref[:, cs]) * ATT_SCALE).astype(BF16)
            dkv_ref[:, cs] += _dot_tn(ds, qh) * ATT_SCALE
            dkv_ref[:, vs] += _dot_tn(pb, dob[:, cs])
        dm_ref[:, WIDTH:2 * WIDTH] = (dym * mix[...] * dsmg).astype(BF16)

    blk = tm // CHUNK
    small = [_full((1, WIDTH)), _full((1, WIDTH)), _full((N_HEAD, CHUNK, CHUNK)), _full((CHUNK, WIDTH)),
             _full((4, HEAD, HEAD)), _full((1, WIDTH)), _full((MEM_LEN, 2 * WIDTH))]
    return pl.pallas_call(
        body, name="abm_bwd",
        grid=(S // tm,),
        in_specs=[_rows(tm, WIDTH, CB_U), _rows(tm, WIDTH, CB_V), _rows(tm, WIDTH, CB_AGATE),
                  _rows(tm, WIDTH, CB_PIN),
                  pl.BlockSpec((CHUNK, WIDTH), lambda i: (jnp.maximum(i * blk - 1, 0), CB_PIN)),
                  _rows(tm, WIDTH, CB_PGATE),
                  pl.BlockSpec((CHUNK, WIDTH), lambda i: (jnp.minimum(i * blk + blk, nblk - 1), CB_PGATE)),
                  _rows(tm, WIDTH, CB_MQ), _rows(tm, WIDTH, CB_MGATE),
                  pl.BlockSpec((3, tm, WIDTH), lambda i: (0, i, 0)),
                  pl.BlockSpec((None, CHUNK, WIDTH), lambda i: (1, jnp.minimum(i * blk + blk, nblk - 1), 0)),
                  _full((1, WIDTH)), _full((1, WIDTH)), _full((N_HEAD, CHUNK, CHUNK)), _full((N_HEAD, CHUNK, CHUNK)),
                  _full((CHUNK, WIDTH)), _full((4, HEAD, HEAD)), _full((4, HEAD, HEAD)), _full((1, WIDTH)),
                  _full((MEM_LEN, 2 * WIDTH))],
        out_specs=[_rows(tm, 5 * WIDTH), _rows(tm, 2 * WIDTH)] + small,
        out_shape=[jax.ShapeDtypeStruct((S, D_BRANCHES), BF16), jax.ShapeDtypeStruct((S, 2 * WIDTH), BF16),
                   jax.ShapeDtypeStruct((1, WIDTH), F32), jax.ShapeDtypeStruct((1, WIDTH), F32),
                   jax.ShapeDtypeStruct((N_HEAD, CHUNK, CHUNK), F32), jax.ShapeDtypeStruct((CHUNK, WIDTH), F32),
                   jax.ShapeDtypeStruct((4, HEAD, HEAD), F32), jax.ShapeDtypeStruct((1, WIDTH), F32),
                   jax.ShapeDtypeStruct((MEM_LEN, 2 * WIDTH), F32)],
        scratch_shapes=[pltpu.VMEM((tm, WIDTH), F32), pltpu.VMEM((tm, WIDTH), F32),
                        pltpu.VMEM((tm + CHUNK, WIDTH), F32)],
        compiler_params=_params(("arbitrary",)),
    )(proj, proj, proj, proj, proj, proj, proj, proj, proj, dy3, dy3,
      ln_g, ln_b, wsm, wsm_t, bias_full, pool_w, pool_wt, pool_scale, kv)


def _bias_reduce(dbias_full):
    def body(d_ref, o_ref):
        d = d_ref[...]
        o_ref[...] = _put_cols([jnp.sum(d[:, h * HEAD:(h + 1) * HEAD], axis=1, keepdims=True) for h in range(N_HEAD)])

    return pl.pallas_call(body, name="bias_reduce", out_shape=jax.ShapeDtypeStruct((CHUNK, 128), F32))(dbias_full)


def _mem_bwd(mem, g, mem_n, w, dkv):
    def body(m_ref, g_ref, mn_ref, w_ref, dkv_ref, dw_ref, dg_ref):
        dkvb = dkv_ref[...].astype(BF16)
        dw_ref[...] = _dot_tn(mn_ref[...], dkvb).astype(BF16)
        dmn = _dot_nt(dkvb, w_ref[...])
        xf = m_ref[...]
        r = lax.rsqrt(jnp.mean(xf * xf, axis=-1, keepdims=True) + EPS)
        dg_ref[...] = jnp.sum(dmn * xf * r, axis=0, keepdims=True)

    return pl.pallas_call(
        body, name="mem_bwd",
        out_shape=[jax.ShapeDtypeStruct((D_MODEL, 2 * WIDTH), BF16), jax.ShapeDtypeStruct((1, D_MODEL), F32)],
        compiler_params=pltpu.CompilerParams(vmem_limit_bytes=VMEM_LIMIT),
    )(mem, g, mem_n, w, dkv)


def _dh_bwd(dpb, dpg, wt, x, g, dxo, parts=()):
    S = x.shape[0]
    tm, tkb = 1024, D_BRANCHES // 4
    nkb, nkg = 4, D_GATES // GATE_TILE
    nk = nkb + nkg
    ni = S // tm
    n = len(parts)

    def body(dpb_ref, wbr_ref, dpg_ref, wg_ref, x_ref, g_ref, dxo_ref, *rest):
        p_in = rest[:n]
        dx_ref, dg_ref = rest[n:n + 2]
        p_out, acc, sems = rest[n + 2:2 * n + 2], rest[2 * n + 2], rest[2 * n + 3:]
        i, kk = pl.program_id(0), pl.program_id(1)

        @pl.when(jnp.logical_and(i == 0, kk == 0))
        def _():
            dg_ref[...] = jnp.zeros_like(dg_ref)
            if n:
                _comm_start(_rs_second(p_in, p_out, *sems))

        @pl.when(kk == 0)
        def _():
            acc[...] = jnp.zeros_like(acc)

        @pl.when(kk < nkb)
        def _():
            acc[...] += _dot(dpb_ref[...], wbr_ref[...])

        @pl.when(kk >= nkb)
        def _():
            acc[...] += _dot(dpg_ref[...], wg_ref[...])

        @pl.when(kk == nk - 1)
        def _():
            xf = x_ref[...]
            r = lax.rsqrt(jnp.mean(xf * xf, axis=-1, keepdims=True) + EPS)
            xhat = xf * r
            dh = acc[...]
            dg_ref[...] += jnp.sum(dh * xhat, axis=0, keepdims=True)
            dxh = dh * g_ref[...]
            dx_ref[...] = dxo_ref[...] + r * (dxh - xhat * jnp.mean(dxh * xhat, axis=-1, keepdims=True))

        if n:
            @pl.when(jnp.logical_and(i == ni - 1, kk == nk - 1))
            def _():
                _comm_wait(_rs_second(p_in, p_out, *sems))

    res = pl.pallas_call(
        body, name="dh_bwd_scatter" if n else "dh_bwd",
        grid=(ni, nk),
        in_specs=[pl.BlockSpec((tm, tkb), lambda i, k: (i, jnp.minimum(k, nkb - 1))),
                  pl.BlockSpec((tkb, D_MODEL), lambda i, k: (jnp.minimum(k, nkb - 1), 0)),
                  pl.BlockSpec((tm, GATE_TILE), lambda i, k: (i, jnp.maximum(k - nkb, 0))),
                  pl.BlockSpec((GATE_TILE, D_MODEL),
                               lambda i, k: (D_BRANCHES // GATE_TILE + jnp.maximum(k - nkb, 0), 0)),
                  pl.BlockSpec((tm, D_MODEL), lambda i, k: (i, 0)), pl.BlockSpec((1, D_MODEL), lambda i, k: (0, 0)),
                  pl.BlockSpec((tm, D_MODEL), lambda i, k: (i, 0))] + [ANY] * n,
        out_specs=[pl.BlockSpec((tm, D_MODEL), lambda i, k: (i, 0)), pl.BlockSpec((1, D_MODEL), lambda i, k: (0, 0))]
                  + [ANY] * n,
        out_shape=[jax.ShapeDtypeStruct((S, D_MODEL), F32), jax.ShapeDtypeStruct((1, D_MODEL), F32)]
                  + [jax.ShapeDtypeStruct(p.shape, p.dtype) for p in parts],
        scratch_shapes=[pltpu.VMEM((tm, D_MODEL), F32)] + (_dma_sems(3 * n, 3 * n, n) if n else []),
        compiler_params=_params(("arbitrary", "arbitrary")),
    )(dpb, wt, dpg, wt, x, g, dxo, *parts)
    return res[0], res[1], list(res[2:])


def _dw_in(h, dpb, dpg, parts=()):
    S = h.shape[0]
    tk = 1024
    nk = S // tk
    n = len(parts)
    tmb = D_BRANCHES // 4
    ng = D_GATES // GATE_TILE

    def accumulate(a_ref, h_ref, o_ref, acc):
        kk = pl.program_id(1)

        @pl.when(kk == 0)
        def _():
            acc[...] = jnp.zeros_like(acc)

        acc[...] += _dot_tn(a_ref[...], h_ref[...])

        @pl.when(kk == nk - 1)
        def _():
            o_ref[...] = acc[...].astype(BF16)

    def branches(a_ref, h_ref, *rest):
        p_in, o_ref, p_out = rest[:n], rest[n], rest[n + 1:2 * n + 1]
        acc, sems = rest[2 * n + 1], rest[2 * n + 2:]
        i, kk = pl.program_id(0), pl.program_id(1)

        if n:
            @pl.when(jnp.logical_and(i == 0, kk == 0))
            def _():
                _comm_start(_rs_second(p_in, p_out, *sems))

        accumulate(a_ref, h_ref, o_ref, acc)

        if n:
            @pl.when(jnp.logical_and(i == 3, kk == nk - 1))
            def _():
                _comm_wait(_rs_second(p_in, p_out, *sems))

    def gates(a_ref, h_ref, dst_ref, o_ref, acc):
        accumulate(a_ref, h_ref, o_ref, acc)

    res = pl.pallas_call(
        branches, name="dw_in_branches_scatter" if n else "dw_in_branches",
        grid=(4, nk),
        in_specs=[pl.BlockSpec((tk, tmb), lambda i, k: (k, i)), pl.BlockSpec((tk, D_MODEL), lambda i, k: (k, 0))]
                 + [ANY] * n,
        out_specs=[pl.BlockSpec((tmb, D_MODEL), lambda i, k: (i, 0))] + [ANY] * n,
        out_shape=[jax.ShapeDtypeStruct((D_IN, D_MODEL), BF16)]
                  + [jax.ShapeDtypeStruct(p.shape, p.dtype) for p in parts],
        scratch_shapes=[pltpu.VMEM((tmb, D_MODEL), F32)] + (_dma_sems(3 * n, 3 * n, n) if n else []),
        compiler_params=_params(("arbitrary", "arbitrary")),
    )(dpb, h, *parts)
    dwt = pl.pallas_call(
        gates, name="dw_in_gates",
        grid=(ng, nk),
        in_specs=[pl.BlockSpec((tk, GATE_TILE), lambda i, k: (k, i)), pl.BlockSpec((tk, D_MODEL), lambda i, k: (k, 0)),
                  ANY],
        out_specs=pl.BlockSpec((GATE_TILE, D_MODEL), lambda i, k: (D_BRANCHES // GATE_TILE + i, 0)),
        out_shape=jax.ShapeDtypeStruct((D_IN, D_MODEL), BF16),
        input_output_aliases={2: 0},
        scratch_shapes=[pltpu.VMEM((GATE_TILE, D_MODEL), F32)],
        compiler_params=_params(("parallel", "arbitrary")),
    )(dpg, h, res[0])
    return dwt, list(res[1:])


def _matmul_tn(a, b, tn, name, parts=()):
    K, M = a.shape
    N = b.shape[1]
    tk = 1024
    nk = K // tk
    nj = N // tn
    n = len(parts)

    def body(a_ref, b_ref, *rest):
        p_in, o_ref, p_out = rest[:n], rest[n], rest[n + 1:2 * n + 1]
        acc, sems = rest[2 * n + 1], rest[2 * n + 2:]
        j, kk = pl.program_id(0), pl.program_id(1)

        if n:
            @pl.when(jnp.logical_and(j == 0, kk == 0))
            def _():
                _comm_start(_rs_second(p_in, p_out, *sems))

        @pl.when(kk == 0)
        def _():
            acc[...] = jnp.zeros_like(acc)

        acc[...] += _dot_tn(a_ref[...].astype(BF16), b_ref[...].astype(BF16))

        @pl.when(kk == nk - 1)
        def _():
            o_ref[...] = acc[...].astype(BF16)

        if n:
            @pl.when(jnp.logical_and(j == nj - 1, kk == nk - 1))
            def _():
                _comm_wait(_rs_second(p_in, p_out, *sems))

    res = pl.pallas_call(
        body, name=name,
        grid=(nj, nk),
        in_specs=[pl.BlockSpec((tk, M), lambda j, k: (k, 0)), pl.BlockSpec((tk, tn), lambda j, k: (k, j))] + [ANY] * n,
        out_specs=[pl.BlockSpec((M, tn), lambda j, k: (0, j))] + [ANY] * n,
        out_shape=[jax.ShapeDtypeStruct((M, N), BF16)] + [jax.ShapeDtypeStruct(p.shape, p.dtype) for p in parts],
        scratch_shapes=[pltpu.VMEM((M, tn), F32)] + (_dma_sems(3 * n, 3 * n, n) if n else []),
        compiler_params=_params(("arbitrary", "arbitrary")),
    )(a, b, *parts)
    return (res[0], list(res[1:])) if n else res[0]


def _row_tile(R, C, block_bytes=2 << 20):
    for cand in range(min(R, block_bytes // (C * 4)) // 8 * 8, 0, -8):
        if R % cand == 0:
            return cand
    return R


def _adamw_update(p_ref, w_ref, m_ref, v_ref, g_ref, d_ref, nm_ref, nv_ref):
    c1 = 1.0 / (1.0 - ADAM_B1 ** ADAM_STEP)
    c2 = 1.0 / (1.0 - ADAM_B2 ** ADAM_STEP)
    g = p_ref[0].astype(F32)
    for k in range(1, p_ref.shape[0]):
        g = g + p_ref[k].astype(F32)
    nm = ADAM_B1 * m_ref[...] + (1.0 - ADAM_B1) * g
    nv = ADAM_B2 * v_ref[...] + (1.0 - ADAM_B2) * (g * g)
    g_ref[...] = g
    nm_ref[...] = nm
    nv_ref[...] = nv
    d_ref[...] = -ADAM_LR * ((nm * c1) / (jnp.sqrt(nv * c2) + ADAM_EPS) + ADAM_WD * w_ref[...])


def _adamw(parts, w, m, v, name):
    P, R, C = parts.shape
    tr = _row_tile(R, C)

    def body(*refs):
        _adamw_update(*refs)

    spec = pl.BlockSpec((tr, C), lambda i: (i, 0))
    return pl.pallas_call(
        body, name=name,
        grid=(R // tr,),
        in_specs=[pl.BlockSpec((P, tr, C), lambda i: (0, i, 0)), spec, spec, spec],
        out_specs=[spec] * 4,
        out_shape=[jax.ShapeDtypeStruct((R, C), F32)] * 4,
        compiler_params=_params(("parallel",)),
    )(parts, w, m, v)


def _adamw_layers(parts, w, m, v, name):
    depth = len(parts)
    P, R, C = parts[0].shape
    tr = _row_tile(R, C, 1 << 20)

    def body(*refs):
        layer = pl.program_id(0)
        for k in range(depth):
            @pl.when(layer == k)
            def _(k=k):
                _adamw_update(refs[k], *refs[depth:])

    def part_spec(k):
        return pl.BlockSpec((P, tr, C), lambda l, i: (0, jnp.where(l == k, i, 0), 0))

    spec = pl.BlockSpec((None, tr, C), lambda l, i: (l, i, 0))
    return pl.pallas_call(
        body, name=name,
        grid=(depth, R // tr),
        in_specs=[part_spec(k) for k in range(depth)] + [spec] * 3,
        out_specs=[spec] * 4,
        out_shape=[jax.ShapeDtypeStruct((depth, R, C), F32)] * 4,
        compiler_params=_params(("arbitrary", "arbitrary")),
    )(*parts, w, m, v)


def _place():
    return lax.axis_index("x"), lax.axis_index("y"), lax.axis_index("c")


def _all_gather(shards):
    n = len(shards)

    def body(*refs):
        ins, outs = refs[:n], refs[n:2 * n]
        send1, recv1, local_sems, send2, recv2 = refs[2 * n:]
        first = _ag_first(ins, outs, send1, recv1, local_sems)
        second = _ag_second(outs, send2, recv2)
        _comm_start(first)
        for j in range(3):
            for a in range(n):
                first[2][4 * a + 1 + j].wait_recv()
            for a in range(n):
                second[1][3 * a + j].start()
        for a in range(n):
            first[2][4 * a].wait_recv()
        for cp in second[2]:
            cp.wait_recv()
        for cp in first[1] + second[1]:
            cp.wait_send()
        for cp in first[0]:
            cp.wait()

    return pl.pallas_call(
        body, name="weights_all_gather",
        in_specs=[ANY] * n, out_specs=[ANY] * n,
        out_shape=[jax.ShapeDtypeStruct((N_DEV,) + s.shape, s.dtype) for s in shards],
        scratch_shapes=_dma_sems(4 * n, 4 * n, n, 3 * n, 3 * n),
        compiler_params=pltpu.CompilerParams(has_side_effects=True),
    )(*shards)


N_BIG = 4


def _dev(p):
    return 4 * p[0] + 2 * p[1] + p[2]


def _other_chips(x, y):
    return [(1 - x, y), (x, 1 - y), (1 - x, 1 - y)]


def _remote(src, dst, send_sems, recv_sems, k, to):
    return pltpu.make_async_remote_copy(src_ref=src, dst_ref=dst, send_sem=send_sems.at[k], recv_sem=recv_sems.at[k],
                                        device_id=to, device_id_type=MESH)


def _ag_first(ins, outs, send_sems, recv_sems, local_sems):
    x, y, c = _place()
    me = (x, y, c)
    targets = [(x, y, 1 - c)] + [(*chip, c) for chip in _other_chips(x, y)]
    local, out, inc = [], [], []
    for a in range(len(ins)):
        local.append(pltpu.make_async_copy(ins[a], outs[a].at[_dev(me)], local_sems.at[a]))
        for k, to in enumerate(targets):
            out.append(_remote(ins[a], outs[a].at[_dev(me)], send_sems, recv_sems, 4 * a + k, to))
            inc.append(_remote(ins[a], outs[a].at[_dev(to)], send_sems, recv_sems, 4 * a + k, to))
    return local, out, inc


def _ag_second(bufs, send_sems, recv_sems):
    x, y, c = _place()
    out, inc = [], []
    for a in range(len(bufs)):
        for j, chip in enumerate(_other_chips(x, y)):
            mine, theirs = bufs[a].at[_dev((*chip, c))], bufs[a].at[_dev((*chip, 1 - c))]
            out.append(_remote(mine, mine, send_sems, recv_sems, 3 * a + j, (x, y, 1 - c)))
            inc.append(_remote(theirs, theirs, send_sems, recv_sems, 3 * a + j, (x, y, 1 - c)))
    return [], out, inc


def _rs_first(ins, outs, send_sems, recv_sems):
    x, y, c = _place()
    out = [_remote(ins[a].at[j, 1 - c], outs[a].at[j], send_sems, recv_sems, N_CHIP * a + j, (x, y, 1 - c))
           for a in range(len(ins)) for j in range(N_CHIP)]
    return [], out, out


def _rs_second(ins, outs, send_sems, recv_sems, local_sems):
    x, y, c = _place()
    my_chip = 2 * x + y
    local, out, inc = [], [], []
    for a in range(len(ins)):
        local.append(pltpu.make_async_copy(ins[a].at[my_chip], outs[a].at[my_chip], local_sems.at[a]))
        for k, (ox, oy) in enumerate(_other_chips(x, y)):
            out.append(_remote(ins[a].at[2 * ox + oy], outs[a].at[my_chip], send_sems, recv_sems, 3 * a + k, (ox, oy, c)))
            inc.append(_remote(ins[a].at[2 * ox + oy], outs[a].at[2 * ox + oy], send_sems, recv_sems, 3 * a + k,
                               (ox, oy, c)))
    return local, out, inc


def _comm_start(exchange):
    local, out, _ = exchange
    for cp in local + out:
        cp.start()


def _comm_wait(exchange):
    local, out, inc = exchange
    for cp in inc:
        cp.wait_recv()
    for cp in out:
        cp.wait_send()
    for cp in local:
        cp.wait()


def _dma_sems(*counts):
    return [pltpu.SemaphoreType.DMA((n,)) for n in counts]


def _rs_sibling(grads):
    n = len(grads)

    def body(*refs):
        ex = _rs_first(refs[:n], refs[n:2 * n], *refs[2 * n:])
        _comm_start(ex)
        _comm_wait(ex)

    return pl.pallas_call(
        body, name="grads_to_sibling",
        in_specs=[ANY] * n, out_specs=[ANY] * n,
        out_shape=[jax.ShapeDtypeStruct(g.shape[:1] + g.shape[2:], g.dtype) for g in grads],
        scratch_shapes=_dma_sems(N_CHIP * n, N_CHIP * n),
        compiler_params=pltpu.CompilerParams(has_side_effects=True),
    )(*grads)


def _pair_sum(grads, recvs):
    n = len(grads)

    def body(c_ref, *refs):
        for a in range(n):
            refs[2 * n + a][...] = (refs[a][...].astype(F32) + refs[n + a][...].astype(F32)).astype(BF16)

    def g_spec(g):
        return pl.BlockSpec((None, None) + g.shape[2:], lambda j, c_ref: (j, c_ref[0], 0, 0))

    def r_spec(r):
        return pl.BlockSpec((None,) + r.shape[1:], lambda j, c_ref: (j, 0, 0))

    return pl.pallas_call(
        body, name="pair_sum",
        grid_spec=pltpu.PrefetchScalarGridSpec(
            num_scalar_prefetch=1, grid=(N_CHIP,),
            in_specs=[g_spec(g) for g in grads] + [r_spec(r) for r in recvs],
            out_specs=[r_spec(r) for r in recvs]),
        out_shape=[jax.ShapeDtypeStruct(r.shape, BF16) for r in recvs],
        compiler_params=_params(("parallel",)),
    )(lax.axis_index("c").reshape(1).astype(jnp.int32), *grads, *recvs)


def _rs_chips(parts):
    n = len(parts)

    def body(*refs):
        ex = _rs_second(refs[:n], refs[n:2 * n], *refs[2 * n:])
        _comm_start(ex)
        _comm_wait(ex)

    return pl.pallas_call(
        body, name="grads_to_chips",
        in_specs=[ANY] * n, out_specs=[ANY] * n,
        out_shape=[jax.ShapeDtypeStruct(p.shape, p.dtype) for p in parts],
        scratch_shapes=_dma_sems(3 * n, 3 * n, n),
        compiler_params=pltpu.CompilerParams(has_side_effects=True),
    )(*parts)


SMALL_ROWS = 544


def _all_reduce_small(buf):
    def body(in_ref, out_ref, recv, acc, send1, recv1, send2, recv2):
        x, y, c = _place()
        me = 4 * x + 2 * y + c
        peers = [(x ^ (r >> 2), y ^ ((r >> 1) & 1), c ^ (r & 1)) for r in range(1, N_DEV)]

        def idx(p):
            return 4 * p[0] + 2 * p[1] + p[2]

        first = [pltpu.make_async_remote_copy(
            src_ref=in_ref.at[idx(p)], dst_ref=recv.at[me], send_sem=send1.at[r], recv_sem=recv1.at[r],
            device_id=p, device_id_type=MESH) for r, p in enumerate(peers)]
        for cp in first:
            cp.start()
        recv[me] = in_ref[me]
        for r, p in enumerate(peers):
            pltpu.make_async_remote_copy(
                src_ref=in_ref.at[idx(p)], dst_ref=recv.at[idx(p)], send_sem=send1.at[r], recv_sem=recv1.at[r],
                device_id=p, device_id_type=MESH).wait_recv()
        total = recv[0]
        for k in range(1, N_DEV):
            total = total + recv[k]
        acc[...] = total
        out_ref[me] = total
        second = [pltpu.make_async_remote_copy(
            src_ref=acc, dst_ref=out_ref.at[me], send_sem=send2.at[r], recv_sem=recv2.at[r],
            device_id=p, device_id_type=MESH) for r, p in enumerate(peers)]
        for cp in second:
            cp.start()
        for r, p in enumerate(peers):
            pltpu.make_async_remote_copy(
                src_ref=acc, dst_ref=out_ref.at[idx(p)], send_sem=send2.at[r], recv_sem=recv2.at[r],
                device_id=p, device_id_type=MESH).wait_recv()
        for cp in first + second:
            cp.wait_send()

    vm = pl.BlockSpec(memory_space=pltpu.VMEM)
    return pl.pallas_call(
        body, name="small_grads_all_reduce",
        in_specs=[vm], out_specs=vm,
        out_shape=jax.ShapeDtypeStruct(buf.shape, F32),
        scratch_shapes=[pltpu.VMEM(buf.shape, F32), pltpu.VMEM(buf.shape[1:], F32),
                        pltpu.SemaphoreType.DMA((7,)), pltpu.SemaphoreType.DMA((7,)),
                        pltpu.SemaphoreType.DMA((7,)), pltpu.SemaphoreType.DMA((7,))],
        compiler_params=pltpu.CompilerParams(has_side_effects=True, vmem_limit_bytes=VMEM_LIMIT),
    )(buf)


def _dilate(a, d):
    if d == 1:
        return a
    S, C = a.shape
    return a.reshape(S // d, d, C).transpose(1, 0, 2).reshape(S, C)


def _undilate(a, d):
    if d == 1:
        return a
    S, C = a.shape
    return a.reshape(d, S // d, C).transpose(1, 0, 2).reshape(S, C)


def _cols(a, cb, n=1):
    return a[:, cb * WIDTH:(cb + n) * WIDTH]


def _to_blocks(g, kind):
    if kind == "rows":
        C = g.shape[1]
        return g.reshape(N_CHIP, 2, -1, C)
    return g.reshape(4 * WIDTH, N_CHIP, 2, -1).transpose(1, 2, 0, 3)


SMALL = ("norm_g", "gm_ln_g", "gm_ln_b", "gm_ws", "gm_bs", "pool_w", "pool_scale", "mem_norm_g", "final_norm_g")


def _pack_small(tree):
    flat = jnp.concatenate([tree[k].reshape(-1, 128) for k in SMALL], axis=0)
    return jnp.pad(flat, ((0, N_DEV * SMALL_ROWS - flat.shape[0]), (0, 0)))


def _unpack_small(flat, like):
    out, at = {}, 0
    for k in SMALL:
        rows = like[k].size // 128
        out[k] = flat[at:at + rows].reshape(like[k].shape)
        at += rows
    return out


def _make_layer(wt, wkv, wb, wout, norm_g, mem_norm_g, ln_g, ln_b, gm_ws, gm_bs, pool_w, pool_scale):
    tril = jnp.tril(jnp.ones((CHUNK, CHUNK), bool))
    wsm = jnp.where(tril, gm_ws, 0.0).astype(BF16)
    pw = pool_w.astype(BF16)
    return dict(wt=wt, wkv=wkv, wb=wb, wout=wout, g=norm_g[None], mg=mem_norm_g[None], ln_g=ln_g[None],
                ln_b=ln_b[None], wsm=wsm, wsm_t=wsm.transpose(0, 2, 1), pw=pw, pw_t=pw.transpose(0, 2, 1),
                ps=pool_scale[None], bias=jnp.repeat(gm_bs.T, HEAD, axis=1))


def _layer_fwd(xl, mem0, L, next_shards=()):
    S = xl.shape[0]
    proj, gates, h, half_gathered = _in_proj(xl, L["g"], L["wt"], next_shards[:1])
    kv, mem_n = _mem_kv(mem0, L["mg"], L["wkv"])
    y4, gathered = _abm_fwd(proj, L["ln_g"], L["ln_b"], L["wsm"], L["bias"], L["pw"], L["ps"], kv, half_gathered)
    o_g, l_g = [], []
    for gi, d in enumerate(DILATIONS):
        if d == 1:
            o, lse = _attn_fwd(proj, CB_Q0, proj, CB_K, proj, CB_CV, S // CHUNK)
        else:
            o, lse = _attn_fwd_dilated(proj, CB_Q0 + gi, CB_K, CB_CV, d)
        o_g.append(o)
        l_g.append(lse)
    (xn, y4, oc, lse, z), rest = _merge_fwd(xl, y4, o_g, l_g, proj, gates, L["wb"], L["wout"], next_shards[1:])
    saved = dict(x=xl, proj=proj, gates=gates, h=h, kv=kv, mem_n=mem_n, y4=y4, oc=oc, lse=lse, z=z)
    return xn, saved, gathered + rest


def _place_cols(dst, piece, cb):
    return lax.dynamic_update_slice(dst, piece, (0, cb * WIDTH))


def _layer_bwd(dx, mem0, L, sv, later=()):
    S = dx.shape[0]
    proj = sv["proj"]
    (dy3, doc, delta, dcg, dgm, dt), from_sibling = _merge_bwd(dx, sv["y4"], sv["oc"], proj, sv["gates"], L["wb"],
                                                              L["wout"], later)
    pair = _pair_sum(later, from_sibling) if later else ()
    dwout = _matmul_tn(sv["z"], dx, D_MODEL, "dw_out")
    dwb = _dw_branch(sv["y4"], dt)
    dpb, dm, dlng, dlnb, dws, dbias, dpw, dps, dkv = _abm_bwd(
        proj, dy3, L["ln_g"], L["ln_b"], L["wsm"], L["wsm_t"], L["bias"], L["pw"], L["pw_t"], L["ps"], sv["kv"])
    dk, dv = None, None
    for gi, d in enumerate(DILATIONS):
        if d == 1:
            r = _attn_bwd(proj, CB_Q0, proj, CB_K, proj, CB_CV, doc, sv["lse"], delta, S // CHUNK)
        else:
            r = _attn_bwd_dilated(proj, CB_Q0 + gi, CB_K, CB_CV, doc, sv["lse"], delta, d)
        dpb = _place_cols(dpb, r[0], CB_Q0 + gi)
        dkg, dvg = r[1].astype(F32), r[2].astype(F32)
        dk = dkg if dk is None else dk + dkg
        dv = dvg if dv is None else dv + dvg
    dpb = _place_cols(dpb, dk.astype(BF16), CB_K)
    dpb = _place_cols(dpb, dv.astype(BF16), CB_CV)
    dpb = _place_cols(dpb, dcg, CB_CGATE)
    dpb = _place_cols(dpb, dm, CB_MQ)
    dwkv, dmg = _mem_bwd(mem0, L["mg"], sv["mem_n"], L["wkv"], dkv)
    dwin_t, parts_rest = _dw_in(sv["h"], dpb, dgm, pair[1:])
    dxi, dng, parts = _dh_bwd(dpb, dgm, L["wt"], sv["x"], L["g"], dx, pair[:1])
    parts = parts + parts_rest
    big = dict(w_in=dwin_t, w_mem_kv=dwkv, w_branch=dwb, w_out=dwout)
    small = dict(norm_g=dng[0], gm_ln_g=dlng[0], gm_ln_b=dlnb[0], gm_ws=dws,
                 gm_bs=_bias_reduce(dbias)[:, :N_HEAD].T, pool_w=dpw, pool_scale=dps[0], mem_norm_g=dmg[0])
    return dxi, big, small, parts


BIG = ("w_in", "w_mem_kv", "w_branch", "w_out")


def _blocked(big):
    return [_to_blocks(big["w_in"], "rows"), _to_blocks(big["w_mem_kv"], "rows"),
            _to_blocks(big["w_branch"], "branch"), _to_blocks(big["w_out"], "rows")]


def _full_weights(gathered):
    win_t, wkv, wb, wout = gathered
    return (win_t.reshape(D_IN, D_MODEL), wkv.reshape(D_MODEL, 2 * WIDTH),
            wb.reshape(N_DEV, 4, WIDTH, -1).transpose(1, 2, 0, 3).reshape(4, WIDTH, D_MODEL),
            wout.reshape(D_MODEL, D_MODEL))


def kernel(x, mem, norm_g, w_in, gm_ln_g, gm_ln_b, gm_ws, gm_bs, pool_w, pool_scale, mem_norm_g, w_mem_kv, w_branch, w_out, final_norm_g, loss_target, m_norm_g, m_w_in, m_gm_ln_g, m_gm_ln_b, m_gm_ws, m_gm_bs, m_pool_w, m_pool_scale, m_mem_norm_g, m_w_mem_kv, m_w_branch, m_w_out, m_final_norm_g, v_norm_g, v_w_in, v_gm_ln_g, v_gm_ln_b, v_gm_ws, v_gm_bs, v_pool_w, v_pool_scale, v_mem_norm_g, v_w_mem_kv, v_w_branch, v_w_out, v_final_norm_g):
    x0 = x[0]
    mem0 = mem[0]
    tgt = loss_target[0]
    S = x0.shape[0]

    shards = [[w_in[l].T.astype(BF16), w_mem_kv[l].astype(BF16), w_branch[l].astype(BF16).reshape(4 * WIDTH, -1),
               w_out[l].astype(BF16)] for l in range(DEPTH)]
    gathered = _all_gather(shards[0])
    layers, saved = [], []
    xl = x0
    for l in range(DEPTH):
        layers.append(_make_layer(*_full_weights(gathered), norm_g[l], mem_norm_g[l], gm_ln_g[l], gm_ln_b[l],
                                  gm_ws[l], gm_bs[l], pool_w[l], pool_scale[l]))
        xl, sv, gathered = _layer_fwd(xl, mem0, layers[l], shards[l + 1] if l + 1 < DEPTH else ())
        saved.append(sv)

    loss_part, dx, d_final = _loss_head(xl, final_norm_g[None], tgt)
    loss = lax.psum(loss_part[0, 0], ("x", "y", "c"))

    small = {k: [None] * DEPTH for k in SMALL if k != "final_norm_g"}
    parts = [None] * DEPTH
    later = ()
    for l in reversed(range(DEPTH)):
        dx, gb, gs, done = _layer_bwd(dx, mem0, layers[l], saved[l], later)
        if later:
            parts[l + 1] = done
        later = _blocked(gb)
        for k in gs:
            small[k][l] = gs[k]
    grad_x = dx[None]
    parts[0] = _rs_chips(_pair_sum(later, _rs_sibling(later)))

    small_tree = {k: jnp.stack(small[k]) for k in small}
    small_tree["final_norm_g"] = d_final[0]
    reduced = _all_reduce_small(_pack_small(small_tree).reshape(N_DEV, SMALL_ROWS, 128))

    weights = dict(norm_g=norm_g, w_in=w_in, gm_ln_g=gm_ln_g, gm_ln_b=gm_ln_b, gm_ws=gm_ws, gm_bs=gm_bs,
                   pool_w=pool_w, pool_scale=pool_scale, mem_norm_g=mem_norm_g, w_mem_kv=w_mem_kv,
                   w_branch=w_branch, w_out=w_out, final_norm_g=final_norm_g)
    m_in = dict(norm_g=m_norm_g, w_in=m_w_in, gm_ln_g=m_gm_ln_g, gm_ln_b=m_gm_ln_b, gm_ws=m_gm_ws, gm_bs=m_gm_bs,
                pool_w=m_pool_w, pool_scale=m_pool_scale, mem_norm_g=m_mem_norm_g, w_mem_kv=m_w_mem_kv,
                w_branch=m_w_branch, w_out=m_w_out, final_norm_g=m_final_norm_g)
    v_in = dict(norm_g=v_norm_g, w_in=v_w_in, gm_ln_g=v_gm_ln_g, gm_ln_b=v_gm_ln_b, gm_ws=v_gm_ws, gm_bs=v_gm_bs,
                pool_w=v_pool_w, pool_scale=v_pool_scale, mem_norm_g=v_mem_norm_g, w_mem_kv=v_w_mem_kv,
                w_branch=v_w_branch, w_out=v_w_out, final_norm_g=v_final_norm_g)
    res = {}
    def view(k, arr):
        return arr.transpose(0, 2, 1) if k == "w_in" else arr

    for a, k in enumerate(BIG):
        shape = view(k, weights[k]).shape
        by_layer = [parts[l][a] for l in range(DEPTH)]
        lrc = (DEPTH,) + by_layer[0].shape[1:]
        outs = _adamw_layers(by_layer, view(k, weights[k]).reshape(lrc), view(k, m_in[k]).reshape(lrc),
                             view(k, v_in[k]).reshape(lrc), "adamw_" + k)
        res[k] = [view(k, o.reshape(shape)) for o in outs]
    outs = _adamw(reduced.reshape(1, N_DEV * SMALL_ROWS, 128), _pack_small(weights), _pack_small(m_in),
                  _pack_small(v_in), "adamw_small")
    unpacked = [_unpack_small(o, weights) for o in outs]
    for k in SMALL:
        res[k] = [u[k] for u in unpacked]

    order = ("norm_g", "w_in", "gm_ln_g", "gm_ln_b", "gm_ws", "gm_bs", "pool_w", "pool_scale", "mem_norm_g",
             "w_mem_kv", "w_branch", "w_out", "final_norm_g")
    return (loss, grad_x, *[res[k][0] for k in order], *[res[k][1] for k in order],
            *[res[k][2] for k in order], *[res[k][3] for k in order])
```

```python
import functools
import math

import jax
import jax.numpy as jnp
from jax import lax
from jax.experimental import pallas as pl
from jax.experimental.pallas import tpu as pltpu

F32 = jnp.float32
BF16 = jnp.bfloat16

D_MODEL = 1024
DEPTH = 4
WIDTH = 512
D_IN = 10752
HEAD = 128
N_HEAD = 4
CHUNK = 128
MEM_LEN = 256
POOL_WINDOWS = (2, 4, 8, 16)
DILATIONS = (1, 4, 16)
EPS = 1e-6
NEG = -1e30
ATT_SCALE = HEAD ** -0.5
N_DEV = 8
N_CHIP = 4

D_BRANCHES = 6656
D_GATES = D_IN - D_BRANCHES
CB_U, CB_V, CB_AGATE, CB_PIN, CB_PGATE = 0, 1, 2, 3, 4
CB_Q0, CB_K, CB_CV, CB_CGATE, CB_MQ, CB_MGATE = 5, 8, 9, 10, 11, 12

ADAM_LR = 0.001
ADAM_B1 = 0.9
ADAM_B2 = 0.999
ADAM_EPS = 1e-08
ADAM_WD = 0.01
ADAM_STEP = 10

VMEM_LIMIT = 56 * 1024 * 1024
MESH = pl.DeviceIdType.MESH
ANY = pl.BlockSpec(memory_space=pl.ANY)

NT = (((1,), (1,)), ((), ()))
TN = (((0,), (0,)), ((), ()))


def _dot(a, b):
    return jnp.dot(a, b, preferred_element_type=F32)


def _dot_nt(a, b):
    return lax.dot_general(a, b, NT, preferred_element_type=F32)


def _dot_tn(a, b):
    return lax.dot_general(a, b, TN, preferred_element_type=F32)


def _sigmoid(x):
    return 0.5 * jnp.tanh(0.5 * x) + 0.5


def _silu(x):
    return x * _sigmoid(x)


def _silu_and_grad(x):
    s = _sigmoid(x)
    return x * s, s * (1.0 + x * (1.0 - s))


def _gelu(x):
    return 0.5 * x * (1.0 + lax.erf(x * (2.0 ** -0.5)))


def _gelu_and_grad(x):
    cdf = 0.5 * (1.0 + lax.erf(x * (2.0 ** -0.5)))
    return x * cdf, cdf + x * jnp.exp(-0.5 * x * x) * (1.0 / math.sqrt(2.0 * math.pi))


def _col(blk, h):
    lane = lax.broadcasted_iota(jnp.int32, blk.shape, 1)
    return jnp.sum(jnp.where(lane == h, blk, 0.0), axis=1, keepdims=True)


def _put_cols(cols):
    rows = cols[0].shape[0]
    lane = lax.broadcasted_iota(jnp.int32, (rows, 128), 1)
    out = jnp.zeros((rows, 128), F32)
    for h, cv in enumerate(cols):
        out = jnp.where(lane == h, cv, out)
    return out


def _params(sem, vmem=VMEM_LIMIT):
    return pltpu.CompilerParams(dimension_semantics=sem, vmem_limit_bytes=vmem)


def _full(shape):
    nd = len(shape)
    return pl.BlockSpec(shape, lambda *_: (0,) * nd)


def _rows(tm, width, cb=0):
    return pl.BlockSpec((tm, width), lambda i: (i, cb))


GATE_TILE = 512


def _in_proj(x, g, wt, shards=()):
    S = x.shape[0]
    tm, tnb = 1024, D_BRANCHES // 4
    njb, njg = 4, D_GATES // GATE_TILE
    n = len(shards)
    ni, nj = S // tm, njb + njg

    def body(x_ref, g_ref, wbr_ref, wg_ref, *rest):
        ins, (proj_ref, gates_ref, h_ref), outs = rest[:n], rest[n:n + 3], rest[n + 3:2 * n + 3]
        hs, sems = rest[2 * n + 3], rest[2 * n + 4:]
        i, j = pl.program_id(0), pl.program_id(1)

        if n:
            @pl.when(jnp.logical_and(i == 0, j == 0))
            def _():
                _comm_start(_ag_first(ins, outs, *sems))

        @pl.when(j == 0)
        def _():
            xf = x_ref[...]
            r = lax.rsqrt(jnp.mean(xf * xf, axis=-1, keepdims=True) + EPS)
            h = (xf * r * g_ref[...]).astype(BF16)
            hs[...] = h
            h_ref[...] = h

        @pl.when(j < njb)
        def _():
            proj_ref[...] = _dot_nt(hs[...], wbr_ref[...]).astype(BF16)

        @pl.when(j >= njb)
        def _():
            gates_ref[...] = _dot_nt(hs[...], wg_ref[...]).astype(BF16)

        if n:
            @pl.when(jnp.logical_and(i == ni - 1, j == nj - 1))
            def _():
                _comm_wait(_ag_first(ins, outs, *sems))

    def first(j):
        return jnp.minimum(j, njb - 1)

    def second(j):
        return jnp.maximum(j - njb, 0)

    res = pl.pallas_call(
        body, name="in_proj_gather" if n else "in_proj",
        grid=(ni, nj),
        in_specs=[pl.BlockSpec((tm, D_MODEL), lambda i, j: (i, 0)),
                  pl.BlockSpec((1, D_MODEL), lambda i, j: (0, 0)),
                  pl.BlockSpec((tnb, D_MODEL), lambda i, j: (first(j), 0)),
                  pl.BlockSpec((GATE_TILE, D_MODEL), lambda i, j: (D_BRANCHES // GATE_TILE + second(j), 0))]
                 + [ANY] * n,
        out_specs=[pl.BlockSpec((tm, tnb), lambda i, j: (i, first(j))),
                   pl.BlockSpec((tm, GATE_TILE), lambda i, j: (i, second(j))),
                   pl.BlockSpec((tm, D_MODEL), lambda i, j: (i, 0))] + [ANY] * n,
        out_shape=[jax.ShapeDtypeStruct((S, D_BRANCHES), BF16), jax.ShapeDtypeStruct((S, D_GATES), BF16),
                   jax.ShapeDtypeStruct((S, D_MODEL), BF16)]
                  + [jax.ShapeDtypeStruct((N_DEV,) + s.shape, s.dtype) for s in shards],
        scratch_shapes=[pltpu.VMEM((tm, D_MODEL), BF16)] + (_dma_sems(4 * n, 4 * n, n) if n else []),
        compiler_params=_params(("arbitrary", "arbitrary")),
    )(x, g, wt, wt, *shards)
    return res[0], res[1], res[2], list(res[3:])


def _mem_kv(mem, g, w):
    M = mem.shape[0]

    def body(m_ref, g_ref, w_ref, kv_ref, mn_ref):
        xf = m_ref[...]
        r = lax.rsqrt(jnp.mean(xf * xf, axis=-1, keepdims=True) + EPS)
        mn = (xf * r * g_ref[...]).astype(BF16)
        mn_ref[...] = mn
        kv_ref[...] = _dot(mn, w_ref[...]).astype(BF16)

    return pl.pallas_call(
        body, name="mem_kv",
        out_shape=[jax.ShapeDtypeStruct((M, 2 * WIDTH), BF16), jax.ShapeDtypeStruct((M, D_MODEL), BF16)],
        compiler_params=pltpu.CompilerParams(vmem_limit_bytes=VMEM_LIMIT),
    )(mem, g, w)


def _band_masks(win):
    t = lax.broadcasted_iota(jnp.int32, (CHUNK, CHUNK), 0)
    s = lax.broadcasted_iota(jnp.int32, (CHUNK, CHUNK), 1)
    cur = jnp.logical_and(t - s >= 0, t - s < win)
    prev = s > t + CHUNK - win
    return cur.astype(BF16), prev.astype(BF16)


def _inv_count(first_row, win):
    t = first_row + lax.broadcasted_iota(jnp.int32, (CHUNK, 1), 0)
    return 1.0 / jnp.minimum(t + 1, win).astype(F32)


def _layer_norm_fwd(v):
    mu = jnp.mean(v, axis=-1, keepdims=True)
    vc = v - mu
    var = jnp.mean(vc * vc, axis=-1, keepdims=True)
    rstd = lax.rsqrt(var + EPS)
    return vc * rstd, rstd


def _mem_softmax(q, kmem):
    s = _dot_nt(q, kmem) * ATT_SCALE
    m = jnp.max(s, axis=-1, keepdims=True)
    e = jnp.exp(s - m)
    return e * (1.0 / jnp.sum(e, axis=-1, keepdims=True))


def _abm_fwd(proj, ln_g, ln_b, wsm, bias_full, pool_w, pool_scale, kv, gathered=()):
    S = proj.shape[0]
    tm = 512
    nchunk = tm // CHUNK
    n = len(gathered)
    nsteps = S // tm

    def body(u_ref, v_ref, ag_ref, p_ref, ph_ref, pg_ref, mq_ref, mg_ref, lng_ref, lnb_ref, wsm_ref, bias_ref,
             pw_ref, ps_ref, kv_ref, *rest):
        y_ref, bufs, mix, sems = rest[n], rest[n + 1:2 * n + 1], rest[2 * n + 1], rest[2 * n + 2:]
        i = pl.program_id(0)

        if n:
            @pl.when(i == 0)
            def _():
                _comm_start(_ag_second(bufs, *sems))

        u = _gelu(u_ref[...].astype(F32))
        v = _gelu(v_ref[...].astype(F32))
        vhat, _ = _layer_norm_fwd(v)
        vln = (vhat * lng_ref[...] + lnb_ref[...]).astype(BF16)
        for c in range(nchunk):
            for h in range(N_HEAD):
                rs, cs = slice(c * CHUNK, (c + 1) * CHUNK), slice(h * HEAD, (h + 1) * HEAD)
                mix[rs, cs] = _dot(wsm_ref[h], vln[rs, cs]) + bias_ref[:, cs]
        y_ref[0] = (u * mix[...] * _silu(ag_ref[...].astype(F32))).astype(BF16)
        halo_ok = (i > 0).astype(F32)
        for c in range(nchunk):
            rs = slice(c * CHUNK, (c + 1) * CHUNK)
            for g, win in enumerate(POOL_WINDOWS):
                cs = slice(g * HEAD, (g + 1) * HEAD)
                bcur, bprev = _band_masks(win)
                cur = p_ref[rs, cs]
                if c == 0:
                    prev = (ph_ref[:, cs].astype(F32) * halo_ok).astype(BF16)
                else:
                    prev = p_ref[(c - 1) * CHUNK:c * CHUNK, cs]
                sums = _dot(bcur, cur) + _dot(bprev, prev)
                dm = sums * _inv_count(i * tm + c * CHUNK, win) - cur.astype(F32)
                mix[rs, cs] = _dot(dm.astype(BF16), pw_ref[g])
        y_ref[1] = (mix[...] * ps_ref[...] * _silu(pg_ref[...].astype(F32))).astype(BF16)
        for h in range(N_HEAD):
            cs = slice(h * HEAD, (h + 1) * HEAD)
            p = _mem_softmax(mq_ref[:, cs], kv_ref[:, cs])
            mix[:, cs] = _dot(p.astype(BF16), kv_ref[:, WIDTH + h * HEAD:WIDTH + (h + 1) * HEAD])
        y_ref[2] = (mix[...] * _silu(mg_ref[...].astype(F32))).astype(BF16)

        if n:
            @pl.when(i == nsteps - 1)
            def _():
                _comm_wait(_ag_second(bufs, *sems))

    blk = tm // CHUNK
    res = pl.pallas_call(
        body, name="abm_fwd_gather" if n else "abm_fwd",
        grid=(nsteps,),
        in_specs=[_rows(tm, WIDTH, CB_U), _rows(tm, WIDTH, CB_V), _rows(tm, WIDTH, CB_AGATE),
                  _rows(tm, WIDTH, CB_PIN),
                  pl.BlockSpec((CHUNK, WIDTH), lambda i: (jnp.maximum(i * blk - 1, 0), CB_PIN)),
                  _rows(tm, WIDTH, CB_PGATE), _rows(tm, WIDTH, CB_MQ), _rows(tm, WIDTH, CB_MGATE),
                  _full((1, WIDTH)), _full((1, WIDTH)), _full((N_HEAD, CHUNK, CHUNK)), _full((CHUNK, WIDTH)),
                  _full((4, HEAD, HEAD)), _full((1, WIDTH)), _full((MEM_LEN, 2 * WIDTH))] + [ANY] * n,
        out_specs=[pl.BlockSpec((3, tm, WIDTH), lambda i: (0, i, 0))] + [ANY] * n,
        out_shape=[jax.ShapeDtypeStruct((4, S, WIDTH), BF16)]
                  + [jax.ShapeDtypeStruct(b.shape, b.dtype) for b in gathered],
        input_output_aliases={15 + a: 1 + a for a in range(n)},
        scratch_shapes=[pltpu.VMEM((tm, WIDTH), F32)] + (_dma_sems(3 * n, 3 * n) if n else []),
        compiler_params=_params(("arbitrary",)),
    )(proj, proj, proj, proj, proj, proj, proj, proj, ln_g, ln_b, wsm, bias_full, pool_w, pool_scale, kv, *gathered)
    return res[0], list(res[1:])


ATT_TILE = 512


def _attn_fwd(q, qcb, k, kcb, v, vcb, bps):
    S = q.shape[0]
    tm = ATT_TILE
    nb = tm // CHUNK

    nblocks = nb * N_HEAD

    def body(q_ref, k_ref, v_ref, kh_ref, vh_ref, o_ref, l_ref, sc_s, sp_s, pc_s, pp_s):
        i = pl.program_id(0)

        def prev_kv(n, cs):
            if n == 0:
                return kh_ref[:, cs], vh_ref[:, cs]
            ps = slice((n - 1) * CHUNK, n * CHUNK)
            return k_ref[ps, cs], v_ref[ps, cs]

        pens = []
        for n in range(nb):
            rs = slice(n * CHUNK, (n + 1) * CHUNK)
            pens.append(jnp.full((N_HEAD * CHUNK, 1), jnp.where((i * nb + n) % bps != 0, 0.0, NEG), F32))
            for h in range(N_HEAD):
                cs = slice(h * HEAD, (h + 1) * HEAD)
                bs = slice((n * N_HEAD + h) * CHUNK, (n * N_HEAD + h + 1) * CHUNK)
                qh = q_ref[rs, cs]
                sc_s[bs, :] = _dot_nt(qh, k_ref[rs, cs])
                sp_s[bs, :] = _dot_nt(qh, prev_kv(n, cs)[0])
        row = lax.broadcasted_iota(jnp.int32, (nblocks * CHUNK, CHUNK), 0) & (CHUNK - 1)
        col = lax.broadcasted_iota(jnp.int32, (nblocks * CHUNK, CHUNK), 1)
        sc = jnp.where(col <= row, sc_s[...] * ATT_SCALE, NEG)
        sp = jnp.where(col >= row, sp_s[...] * ATT_SCALE, NEG) + jnp.concatenate(pens, axis=0)
        m = jnp.maximum(jnp.max(sc, axis=-1, keepdims=True), jnp.max(sp, axis=-1, keepdims=True))
        ec = jnp.exp(sc - m)
        ep = jnp.exp(sp - m)
        den = jnp.sum(ec, axis=-1, keepdims=True) + jnp.sum(ep, axis=-1, keepdims=True)
        inv = 1.0 / den
        pc_s[...] = (ec * inv).astype(BF16)
        pp_s[...] = (ep * inv).astype(BF16)
        lse = m + jnp.log(den)
        for n in range(nb):
            rs = slice(n * CHUNK, (n + 1) * CHUNK)
            for h in range(N_HEAD):
                cs = slice(h * HEAD, (h + 1) * HEAD)
                bs = slice((n * N_HEAD + h) * CHUNK, (n * N_HEAD + h + 1) * CHUNK)
                o = _dot(pc_s[bs, :], v_ref[rs, cs]) + _dot(pp_s[bs, :], prev_kv(n, cs)[1])
                o_ref[rs, cs] = o.astype(BF16)
            l_ref[rs, :] = _put_cols([lse[(n * N_HEAD + h) * CHUNK:(n * N_HEAD + h + 1) * CHUNK]
                                      for h in range(N_HEAD)])

    def halo(cb):
        return pl.BlockSpec((CHUNK, WIDTH), lambda i: (jnp.maximum(i * nb - 1, 0), cb))

    return pl.pallas_call(
        body, name=f"attn_fwd_{bps}",
        grid=(S // tm,),
        in_specs=[_rows(tm, WIDTH, qcb), _rows(tm, WIDTH, kcb), _rows(tm, WIDTH, vcb), halo(kcb), halo(vcb)],
        out_specs=[_rows(tm, WIDTH), _rows(tm, 128)],
        out_shape=[jax.ShapeDtypeStruct((S, WIDTH), BF16), jax.ShapeDtypeStruct((S, 128), F32)],
        scratch_shapes=[pltpu.VMEM((nblocks * CHUNK, CHUNK), F32), pltpu.VMEM((nblocks * CHUNK, CHUNK), F32),
                        pltpu.VMEM((nblocks * CHUNK, CHUNK), BF16), pltpu.VMEM((nblocks * CHUNK, CHUNK), BF16)],
        compiler_params=_params(("parallel",)),
    )(q, k, v, k, v)


def _gate_specs(tm):
    return [pl.BlockSpec((tm, D_MODEL), lambda i, b=b: (i, b)) for b in range(4)]


Y_SLOT = (0, 1, 3, 2)


def _merge_fwd(x, y4, o_g, l_g, proj, gates, wb, wout, shards=()):
    S = x.shape[0]
    tm = 256
    n = len(shards)
    nsteps = S // tm
    forward_at = nsteps - 4

    def body(x_ref, y_ref, o0, o1, o2, l0, l1, l2, cg_ref, *rest):
        gm = rest[:4]
        wb_ref, wo_ref = rest[4:6]
        s_in = rest[6:6 + n]
        xn_ref, yc_ref, oc_ref, lse_ref, z_ref = rest[6 + n:11 + n]
        s_out, ocs, sems = rest[11 + n:11 + 2 * n], rest[11 + 2 * n], rest[12 + 2 * n:]
        i = pl.program_id(0)

        if n:
            @pl.when(i == 0)
            def _():
                _comm_start(_ag_first(s_in, s_out, *sems[:3]))

            @pl.when(i == forward_at)
            def _():
                incoming = _ag_first(s_in, s_out, *sems[:3])[2]
                for a in range(n):
                    for k in range(1, 4):
                        incoming[4 * a + k].wait_recv()
                _comm_start(_ag_second(s_out, *sems[3:]))

        lcols = []
        for h in range(N_HEAD):
            cs = slice(h * HEAD, (h + 1) * HEAD)
            ls = [_col(l[...], h) for l in (l0, l1, l2)]
            m = jnp.maximum(jnp.maximum(ls[0], ls[1]), ls[2])
            tot = jnp.exp(ls[0] - m) + jnp.exp(ls[1] - m) + jnp.exp(ls[2] - m)
            lse = m + jnp.log(tot)
            ocs[:, cs] = sum(jnp.exp(lg - lse) * o[:, cs].astype(F32) for lg, o in zip(ls, (o0, o1, o2)))
            lcols.append(lse)
        lse_ref[...] = _put_cols(lcols)
        oc = ocs[...]
        oc_ref[...] = oc.astype(BF16)
        yc = (oc * _silu(cg_ref[...].astype(F32))).astype(BF16)
        yc_ref[...] = yc
        ys = (y_ref[0], y_ref[1], yc, y_ref[2])
        z = jnp.zeros((tm, D_MODEL), F32)
        for b in range(4):
            z = z + _sigmoid(gm[b][...].astype(F32)) * _dot(ys[b], wb_ref[b])
        zb = z.astype(BF16)
        z_ref[...] = zb
        xn_ref[...] = x_ref[...] + _dot(zb, wo_ref[...])

        if n:
            @pl.when(i == nsteps - 1)
            def _():
                local, out, incoming = _ag_first(s_in, s_out, *sems[:3])
                for a in range(n):
                    incoming[4 * a].wait_recv()
                _comm_wait(_ag_second(s_out, *sems[3:]))
                for cp in out:
                    cp.wait_send()
                for cp in local:
                    cp.wait()

    res = pl.pallas_call(
        body, name="merge_fwd_gather" if n else "merge_fwd",
        grid=(nsteps,),
        in_specs=[_rows(tm, D_MODEL), pl.BlockSpec((3, tm, WIDTH), lambda i: (0, i, 0)),
                  _rows(tm, WIDTH), _rows(tm, WIDTH), _rows(tm, WIDTH),
                  _rows(tm, 128), _rows(tm, 128), _rows(tm, 128),
                  _rows(tm, WIDTH, CB_CGATE)] + _gate_specs(tm)
                 + [_full((4, WIDTH, D_MODEL)), _full((D_MODEL, D_MODEL))] + [ANY] * n,
        out_specs=[_rows(tm, D_MODEL), pl.BlockSpec((None, tm, WIDTH), lambda i: (Y_SLOT[2], i, 0)),
                   _rows(tm, WIDTH), _rows(tm, 128), _rows(tm, D_MODEL)] + [ANY] * n,
        out_shape=[jax.ShapeDtypeStruct((S, D_MODEL), F32), jax.ShapeDtypeStruct(y4.shape, BF16),
                   jax.ShapeDtypeStruct((S, WIDTH), BF16), jax.ShapeDtypeStruct((S, 128), F32),
                   jax.ShapeDtypeStruct((S, D_MODEL), BF16)]
                  + [jax.ShapeDtypeStruct((N_DEV,) + s.shape, s.dtype) for s in shards],
        input_output_aliases={1: 1},
        scratch_shapes=[pltpu.VMEM((tm, WIDTH), F32)] + (_dma_sems(4 * n, 4 * n, n, 3 * n, 3 * n) if n else []),
        compiler_params=_params(("arbitrary",)),
    )(x, y4, *o_g, *l_g, proj, *([gates] * 4), wb, wout, *shards)
    return res[:5], list(res[5:])


def _loss_head(x, g, tgt):
    S = x.shape[0]
    tm = 512

    def body(x_ref, g_ref, t_ref, loss_ref, dx_ref, dg_ref):
        @pl.when(pl.program_id(0) == 0)
        def _():
            loss_ref[...] = jnp.zeros_like(loss_ref)
            dg_ref[...] = jnp.zeros_like(dg_ref)

        xf = x_ref[...]
        r = lax.rsqrt(jnp.mean(xf * xf, axis=-1, keepdims=True) + EPS)
        xhat = xf * r
        gv = g_ref[...]
        err = xhat * gv - t_ref[...]
        e2 = jnp.sum(err * err, axis=-1, keepdims=True)
        loss_ref[...] += (0.5 / D_MODEL) * jnp.sum(e2, axis=0, keepdims=True)
        dy = err * (1.0 / D_MODEL)
        dg_ref[...] += jnp.sum(dy * xhat, axis=0, keepdims=True)
        dxh = dy * gv
        dx_ref[...] = r * (dxh - xhat * jnp.mean(dxh * xhat, axis=-1, keepdims=True))

    return pl.pallas_call(
        body, name="loss_head",
        grid=(S // tm,),
        in_specs=[_rows(tm, D_MODEL), _full((1, D_MODEL)), _rows(tm, D_MODEL)],
        out_specs=[_full((1, 128)), _rows(tm, D_MODEL), _full((1, D_MODEL))],
        out_shape=[jax.ShapeDtypeStruct((1, 128), F32), jax.ShapeDtypeStruct((S, D_MODEL), F32),
                   jax.ShapeDtypeStruct((1, D_MODEL), F32)],
        compiler_params=_params(("arbitrary",)),
    )(x, g, tgt)


def _merge_bwd(dxo, y4, oc, z, proj, gates, wb, wout, grads=()):
    S = dxo.shape[0]
    tm = 256
    n = len(grads)
    nsteps = S // tm

    def body(dx_ref, y_ref, oc_ref, z_ref, cg_ref, *rest):
        gm = rest[:4]
        wb_ref, wo_ref = rest[4:6]
        g_in = rest[6:6 + n]
        dy_ref, doc_ref, delta_ref, dcg_ref, dgm_ref, dwb_ref, dwo_ref = rest[6 + n:13 + n]
        g_out = rest[13 + n:13 + 2 * n]
        acc_b, acc_o = rest[13 + 2 * n:15 + 2 * n]
        sems = rest[15 + 2 * n:]
        i = pl.program_id(0)

        @pl.when(i == 0)
        def _():
            acc_b[...] = jnp.zeros_like(acc_b)
            acc_o[...] = jnp.zeros_like(acc_o)
            if n:
                _comm_start(_rs_first(g_in, g_out, *sems))

        dxb = dx_ref[...].astype(BF16)
        acc_o[...] += _dot_tn(z_ref[...], dxb)
        dz = _dot_nt(dxb, wo_ref[...])
        for b in range(4):
            gate = _sigmoid(gm[b][...].astype(F32))
            yb = y_ref[Y_SLOT[b]]
            t = _dot(yb, wb_ref[b])
            dgm_ref[:, b * D_MODEL:(b + 1) * D_MODEL] = (dz * t * gate * (1.0 - gate)).astype(BF16)
            dt = (dz * gate).astype(BF16)
            acc_b[b] += _dot_tn(yb, dt)
            dyb = _dot_nt(dt, wb_ref[b])
            if b == 2:
                cg = cg_ref[...].astype(F32)
                oc = oc_ref[...].astype(F32)
                scg, dscg = _silu_and_grad(cg)
                doc = dyb * scg
                dcg_ref[...] = (dyb * oc * dscg).astype(BF16)
                doc_ref[...] = doc.astype(BF16)
                prod = doc * oc
                delta_ref[...] = _put_cols([jnp.sum(prod[:, h * HEAD:(h + 1) * HEAD], axis=1, keepdims=True)
                                            for h in range(N_HEAD)])
            else:
                dy_ref[b if b < 2 else 2] = dyb.astype(BF16)

        @pl.when(i == nsteps - 1)
        def _():
            dwb_ref[...] = acc_b[...].astype(BF16)
            dwo_ref[...] = acc_o[...].astype(BF16)
            if n:
                _comm_wait(_rs_first(g_in, g_out, *sems))

    def resident(shape):
        nd = len(shape)
        return pl.BlockSpec(shape, lambda i: (0,) * nd, pipeline_mode=pl.Buffered(1))

    res = pl.pallas_call(
        body, name="merge_bwd_scatter" if n else "merge_bwd",
        grid=(nsteps,),
        in_specs=[_rows(tm, D_MODEL), pl.BlockSpec((4, tm, WIDTH), lambda i: (0, i, 0)),
                  _rows(tm, WIDTH), _rows(tm, D_MODEL), _rows(tm, WIDTH, CB_CGATE)] + _gate_specs(tm)
                 + [resident((4, WIDTH, D_MODEL)), resident((D_MODEL, D_MODEL))] + [ANY] * n,
        out_specs=[pl.BlockSpec((3, tm, WIDTH), lambda i: (0, i, 0)), _rows(tm, WIDTH), _rows(tm, 128),
                   _rows(tm, WIDTH), _rows(tm, 4 * D_MODEL), _full((4, WIDTH, D_MODEL)), _full((D_MODEL, D_MODEL))]
                  + [ANY] * n,
        out_shape=[jax.ShapeDtypeStruct((3, S, WIDTH), BF16), jax.ShapeDtypeStruct((S, WIDTH), BF16),
                   jax.ShapeDtypeStruct((S, 128), F32), jax.ShapeDtypeStruct((S, WIDTH), BF16),
                   jax.ShapeDtypeStruct((S, 4 * D_MODEL), BF16), jax.ShapeDtypeStruct((4, WIDTH, D_MODEL), BF16),
                   jax.ShapeDtypeStruct((D_MODEL, D_MODEL), BF16)]
                  + [jax.ShapeDtypeStruct(g.shape[:1] + g.shape[2:], g.dtype) for g in grads],
        scratch_shapes=[pltpu.VMEM((4, WIDTH, D_MODEL), F32), pltpu.VMEM((D_MODEL, D_MODEL), F32)]
                       + (_dma_sems(N_CHIP * n, N_CHIP * n) if n else []),
        compiler_params=_params(("arbitrary",)),
    )(dxo, y4, oc, z, proj, *([gates] * 4), wb, wout, *grads)
    return res[:7], list(res[7:])


def _attn_bwd(q, qcb, k, kcb, v, vcb, do, lse, delta, bps):
    S = q.shape[0]
    tm = ATT_TILE
    nb = tm // CHUNK
    nblk = S // CHUNK

    ncur = nb * N_HEAD
    nprev = (nb + 1) * N_HEAD

    def body(q_ref, k_ref, v_ref, do_ref, l_ref, d_ref, kh_ref, vh_ref, qn_ref, don_ref, ln_ref, dn_ref,
             dq_ref, dk_ref, dv_ref, sc_s, sp_s, dpc_s, dpp_s, pc_s, pp_s, dsc_s, dsp_s):
        i = pl.program_id(0)

        def rows_of(n):
            if n < nb:
                rs = slice(n * CHUNK, (n + 1) * CHUNK)
                return rs, q_ref, do_ref, l_ref, d_ref
            return slice(0, CHUNK), qn_ref, don_ref, ln_ref, dn_ref

        def prev_kv(n, cs):
            if n == 0:
                return kh_ref[:, cs], vh_ref[:, cs]
            ps = slice((n - 1) * CHUNK, n * CHUNK)
            return k_ref[ps, cs], v_ref[ps, cs]

        def blk(n, h):
            return slice((n * N_HEAD + h) * CHUNK, (n * N_HEAD + h + 1) * CHUNK)

        pens, lses, deltas = [], [], []
        for n in range(nb + 1):
            rs, qr, dor, lr, dr = rows_of(n)
            gb = i * nb + n
            pen = jnp.where(gb % bps != 0, 0.0, NEG)
            if n == nb:
                pen = pen + jnp.where(gb < nblk, 0.0, NEG)
            pens.append(jnp.full((N_HEAD * CHUNK, 1), pen, F32))
            lblk, dblk = lr[rs, :], dr[rs, :]
            for h in range(N_HEAD):
                cs = slice(h * HEAD, (h + 1) * HEAD)
                qh, doh = qr[rs, cs], dor[rs, cs]
                lses.append(_col(lblk, h))
                deltas.append(_col(dblk, h))
                kp, vp = prev_kv(n, cs)
                sp_s[blk(n, h), :] = _dot_nt(qh, kp)
                dpp_s[blk(n, h), :] = _dot_nt(doh, vp)
                if n < nb:
                    sc_s[blk(n, h), :] = _dot_nt(qh, k_ref[rs, cs])
                    dpc_s[blk(n, h), :] = _dot_nt(doh, v_ref[rs, cs])
        lse = jnp.concatenate(lses, axis=0)
        delta = jnp.concatenate(deltas, axis=0)
        row = lax.broadcasted_iota(jnp.int32, (nprev * CHUNK, CHUNK), 0) & (CHUNK - 1)
        col = lax.broadcasted_iota(jnp.int32, (nprev * CHUNK, CHUNK), 1)
        sp = jnp.where(col >= row, sp_s[...] * ATT_SCALE, NEG) + jnp.concatenate(pens, axis=0)
        pp = jnp.exp(sp - lse)
        pp_s[...] = pp.astype(BF16)
        dsp_s[...] = (pp * (dpp_s[...] - delta)).astype(BF16)
        nc = ncur * CHUNK
        sc = jnp.where(col[:nc] <= row[:nc], sc_s[...] * ATT_SCALE, NEG)
        pc = jnp.exp(sc - lse[:nc])
        pc_s[...] = pc.astype(BF16)
        dsc_s[...] = (pc * (dpc_s[...] - delta[:nc])).astype(BF16)
        for n in range(nb):
            rs, qr, dor, _, _ = rows_of(n)
            rn, qnr, donr, _, _ = rows_of(n + 1)
            for h in range(N_HEAD):
                cs = slice(h * HEAD, (h + 1) * HEAD)
                kp, _ = prev_kv(n, cs)
                dq = _dot(dsc_s[blk(n, h), :], k_ref[rs, cs]) + _dot(dsp_s[blk(n, h), :], kp)
                dq_ref[rs, cs] = (dq * ATT_SCALE).astype(BF16)
                dk = _dot_tn(dsc_s[blk(n, h), :], qr[rs, cs]) + _dot_tn(dsp_s[blk(n + 1, h), :], qnr[rn, cs])
                dk_ref[rs, cs] = (dk * ATT_SCALE).astype(BF16)
                dv = _dot_tn(pc_s[blk(n, h), :], dor[rs, cs]) + _dot_tn(pp_s[blk(n + 1, h), :], donr[rn, cs])
                dv_ref[rs, cs] = dv.astype(BF16)

    def prev_halo(cb):
        return pl.BlockSpec((CHUNK, WIDTH), lambda i: (jnp.maximum(i * nb - 1, 0), cb))

    def next_halo(width, cb=0):
        return pl.BlockSpec((CHUNK, width), lambda i: (jnp.minimum(i * nb + nb, nblk - 1), cb))

    return pl.pallas_call(
        body, name=f"attn_bwd_{bps}",
        grid=(S // tm,),
        in_specs=[_rows(tm, WIDTH, qcb), _rows(tm, WIDTH, kcb), _rows(tm, WIDTH, vcb), _rows(tm, WIDTH),
                  _rows(tm, 128), _rows(tm, 128), prev_halo(kcb), prev_halo(vcb),
                  next_halo(WIDTH, qcb), next_halo(WIDTH), next_halo(128), next_halo(128)],
        out_specs=[_rows(tm, WIDTH), _rows(tm, WIDTH), _rows(tm, WIDTH)],
        out_shape=[jax.ShapeDtypeStruct((S, WIDTH), BF16)] * 3,
        scratch_shapes=[pltpu.VMEM((ncur * CHUNK, CHUNK), F32), pltpu.VMEM((nprev * CHUNK, CHUNK), F32),
                        pltpu.VMEM((ncur * CHUNK, CHUNK), F32), pltpu.VMEM((nprev * CHUNK, CHUNK), F32),
                        pltpu.VMEM((ncur * CHUNK, CHUNK), BF16), pltpu.VMEM((nprev * CHUNK, CHUNK), BF16),
                        pltpu.VMEM((ncur * CHUNK, CHUNK), BF16), pltpu.VMEM((nprev * CHUNK, CHUNK), BF16)],
        compiler_params=_params(("parallel",)),
    )(q, k, v, do, lse, delta, k, v, q, do, lse, delta)


def _dilated_split(d):
    hp = min(N_HEAD, 16 // d)
    return hp, N_HEAD // hp, HEAD * hp


def _by_class(src_ref, dst, d, hp):
    for j in range(hp):
        dst[j] = pltpu.einshape("(tr)l->(rt)l", src_ref[:, j * HEAD:(j + 1) * HEAD], r=d)


def _from_class(src, dst_ref, d, hp):
    for j in range(hp):
        dst_ref[:, j * HEAD:(j + 1) * HEAD] = pltpu.einshape("(rt)l->(tr)l", src[j].astype(BF16), r=d)


def _attn_fwd_dilated(proj, qcb, kcb, vcb, d):
    S = proj.shape[0]
    T = CHUNK * d
    hp, nh, cw = _dilated_split(d)
    nblocks = d * hp

    def body(q_ref, k_ref, v_ref, o_ref, l_ref, qf, kf, vf, kpf, vpf, kst, vst, of, lf, sc_s, sp_s, pc_s, pp_s):
        i, hh = pl.program_id(0), pl.program_id(1)
        _by_class(q_ref, qf, d, hp)
        _by_class(k_ref, kf, d, hp)
        _by_class(v_ref, vf, d, hp)

        @pl.when(i == 0)
        def _():
            kpf[...] = jnp.zeros_like(kpf)
            vpf[...] = jnp.zeros_like(vpf)

        @pl.when(i > 0)
        def _():
            kpf[...] = kst[hh]
            vpf[...] = vst[hh]

        def blk(ref, r, j):
            return ref[j, r * CHUNK:(r + 1) * CHUNK, :]

        def bs(r, j):
            return slice((r * hp + j) * CHUNK, (r * hp + j + 1) * CHUNK)

        for r in range(d):
            for j in range(hp):
                qb = blk(qf, r, j)
                sc_s[bs(r, j), :] = _dot_nt(qb, blk(kf, r, j))
                sp_s[bs(r, j), :] = _dot_nt(qb, blk(kpf, r, j))
        row = lax.broadcasted_iota(jnp.int32, (nblocks * CHUNK, CHUNK), 0) & (CHUNK - 1)
        col = lax.broadcasted_iota(jnp.int32, (nblocks * CHUNK, CHUNK), 1)
        sc = jnp.where(col <= row, sc_s[...] * ATT_SCALE, NEG)
        sp = jnp.where(col >= row, sp_s[...] * ATT_SCALE, NEG) + jnp.where(i > 0, 0.0, NEG)
        m = jnp.maximum(jnp.max(sc, axis=-1, keepdims=True), jnp.max(sp, axis=-1, keepdims=True))
        ec = jnp.exp(sc - m)
        ep = jnp.exp(sp - m)
        den = jnp.sum(ec, axis=-1, keepdims=True) + jnp.sum(ep, axis=-1, keepdims=True)
        inv = 1.0 / den
        pc_s[...] = (ec * inv).astype(BF16)
        pp_s[...] = (ep * inv).astype(BF16)
        lse = m + jnp.log(den)
        lane = lax.broadcasted_iota(jnp.int32, (CHUNK, 128), 1)
        for r in range(d):
            lblk = jnp.zeros((CHUNK, 128), F32)
            for j in range(hp):
                o = _dot(pc_s[bs(r, j), :], blk(vf, r, j)) + _dot(pp_s[bs(r, j), :], blk(vpf, r, j))
                of[j, r * CHUNK:(r + 1) * CHUNK, :] = o
                lblk = jnp.where(lane == hh * hp + j, lse[bs(r, j)], lblk)
            lf[r * CHUNK:(r + 1) * CHUNK, :] = lblk
        _from_class(of, o_ref, d, hp)
        lnat = pltpu.einshape("(rt)l->(tr)l", lf[...], r=d)

        @pl.when(hh == 0)
        def _():
            l_ref[...] = lnat

        @pl.when(hh > 0)
        def _():
            l_ref[...] += lnat

        kst[hh] = kf[...]
        vst[hh] = vf[...]

    def cols(cb):
        return pl.BlockSpec((T, cw), lambda i, hh: (i, cb * nh + hh))

    tile = pltpu.VMEM((hp, T, HEAD), BF16)
    return pl.pallas_call(
        body, name=f"attn_fwd_dilated_{d}",
        grid=(S // T, nh),
        in_specs=[cols(qcb), cols(kcb), cols(vcb)],
        out_specs=[cols(0), pl.BlockSpec((T, 128), lambda i, hh: (i, 0))],
        out_shape=[jax.ShapeDtypeStruct((S, WIDTH), BF16), jax.ShapeDtypeStruct((S, 128), F32)],
        scratch_shapes=[tile, tile, tile, tile, tile,
                        pltpu.VMEM((nh, hp, T, HEAD), BF16), pltpu.VMEM((nh, hp, T, HEAD), BF16),
                        pltpu.VMEM((hp, T, HEAD), F32), pltpu.VMEM((T, 128), F32),
                        pltpu.VMEM((nblocks * CHUNK, CHUNK), F32), pltpu.VMEM((nblocks * CHUNK, CHUNK), F32),
                        pltpu.VMEM((nblocks * CHUNK, CHUNK), BF16), pltpu.VMEM((nblocks * CHUNK, CHUNK), BF16)],
        compiler_params=_params(("arbitrary", "arbitrary")),
    )(proj, proj, proj)


def _attn_bwd_dilated(proj, qcb, kcb, vcb, do, lse, delta, d):
    S = proj.shape[0]
    T = CHUNK * d
    nt = S // T
    hp, nh, cw = _dilated_split(d)
    nblocks = d * hp

    def body(q_ref, k_ref, v_ref, do_ref, l_ref, d_ref, dq_ref, dk_ref, dv_ref,
             qf, dof, kf, vf, kpf, vpf, dqf, acck, accv, newk, newv,
             sc_s, sp_s, dpc_s, dpp_s, pc_s, pp_s, dsc_s, dsp_s):
        hh, i = pl.program_id(0), pl.program_id(1)

        @pl.when(i == 0)
        def _():
            for ref in (kpf, vpf, acck, accv):
                ref[...] = jnp.zeros_like(ref)
            dk_ref[...] = jnp.zeros_like(dk_ref)
            dv_ref[...] = jnp.zeros_like(dv_ref)

        def blk(ref, r, j):
            return ref[j, r * CHUNK:(r + 1) * CHUNK, :]

        def bs(r, j):
            return slice((r * hp + j) * CHUNK, (r * hp + j + 1) * CHUNK)

        @pl.when(i < nt)
        def _():
            _by_class(q_ref, qf, d, hp)
            _by_class(do_ref, dof, d, hp)
            _by_class(k_ref, kf, d, hp)
            _by_class(v_ref, vf, d, hp)
            lses, deltas = [], []
            lcls = pltpu.einshape("(tr)l->(rt)l", l_ref[...], r=d)
            dcls = pltpu.einshape("(tr)l->(rt)l", d_ref[...], r=d)
            for r in range(d):
                lblk = lcls[r * CHUNK:(r + 1) * CHUNK]
                dblk = dcls[r * CHUNK:(r + 1) * CHUNK]
                for j in range(hp):
                    lses.append(_col(lblk, hh * hp + j))
                    deltas.append(_col(dblk, hh * hp + j))
                    qb, dob = blk(qf, r, j), blk(dof, r, j)
                    sc_s[bs(r, j), :] = _dot_nt(qb, blk(kf, r, j))
                    dpc_s[bs(r, j), :] = _dot_nt(dob, blk(vf, r, j))
                    sp_s[bs(r, j), :] = _dot_nt(qb, blk(kpf, r, j))
                    dpp_s[bs(r, j), :] = _dot_nt(dob, blk(vpf, r, j))
            lse = jnp.concatenate(lses, axis=0)
            delta = jnp.concatenate(deltas, axis=0)
            row = lax.broadcasted_iota(jnp.int32, (nblocks * CHUNK, CHUNK), 0) & (CHUNK - 1)
            col = lax.broadcasted_iota(jnp.int32, (nblocks * CHUNK, CHUNK), 1)
            sp = jnp.where(col >= row, sp_s[...] * ATT_SCALE, NEG) + jnp.where(i > 0, 0.0, NEG)
            pp = jnp.exp(sp - lse)
            pp_s[...] = pp.astype(BF16)
            dsp_s[...] = (pp * (dpp_s[...] - delta)).astype(BF16)
            sc = jnp.where(col <= row, sc_s[...] * ATT_SCALE, NEG)
            pc = jnp.exp(sc - lse)
            pc_s[...] = pc.astype(BF16)
            dsc_s[...] = (pc * (dpc_s[...] - delta)).astype(BF16)
            for r in range(d):
                rows = slice(r * CHUNK, (r + 1) * CHUNK)
                for j in range(hp):
                    qb, dob = blk(qf, r, j), blk(dof, r, j)
                    dsc, dsp = dsc_s[bs(r, j), :], dsp_s[bs(r, j), :]
                    dqf[j, rows, :] = (_dot(dsc, blk(kf, r, j)) + _dot(dsp, blk(kpf, r, j))) * ATT_SCALE
                    newk[j, rows, :] = _dot_tn(dsc, qb) * ATT_SCALE
                    newv[j, rows, :] = _dot_tn(pc_s[bs(r, j), :], dob)
                    acck[j, rows, :] += _dot_tn(dsp, qb) * ATT_SCALE
                    accv[j, rows, :] += _dot_tn(pp_s[bs(r, j), :], dob)
            _from_class(dqf, dq_ref, d, hp)

        @pl.when(i > 0)
        def _():
            _from_class(acck, dk_ref, d, hp)
            _from_class(accv, dv_ref, d, hp)

        @pl.when(i < nt)
        def _():
            acck[...] = newk[...]
            accv[...] = newv[...]
            kpf[...] = kf[...]
            vpf[...] = vf[...]

    def cur(width, cb, nsplit):
        return pl.BlockSpec((T, width), lambda hh, i: (jnp.minimum(i, nt - 1), cb * nsplit + hh * (nsplit > 1)))

    def lag():
        return pl.BlockSpec((T, cw), lambda hh, i: (jnp.maximum(i - 1, 0), hh))

    tile = pltpu.VMEM((hp, T, HEAD), BF16)
    acc = pltpu.VMEM((hp, T, HEAD), F32)
    f32s = pltpu.VMEM((nblocks * CHUNK, CHUNK), F32)
    b16s = pltpu.VMEM((nblocks * CHUNK, CHUNK), BF16)
    return pl.pallas_call(
        body, name=f"attn_bwd_dilated_{d}",
        grid=(nh, nt + 1),
        in_specs=[cur(cw, qcb, nh), cur(cw, kcb, nh), cur(cw, vcb, nh), cur(cw, 0, nh), cur(128, 0, 1), cur(128, 0, 1)],
        out_specs=[cur(cw, 0, nh), lag(), lag()],
        out_shape=[jax.ShapeDtypeStruct((S, WIDTH), BF16)] * 3,
        scratch_shapes=[tile] * 6 + [acc] * 5 + [f32s] * 4 + [b16s] * 4,
        compiler_params=_params(("arbitrary", "arbitrary")),
    )(proj, proj, proj, do, lse, delta)


def _abm_bwd(proj, dy3, ln_g, ln_b, wsm, wsm_t, bias_full, pool_w, pool_wt, pool_scale, kv):
    S = proj.shape[0]
    tm = 512
    nchunk = tm // CHUNK
    nblk = S // CHUNK

    def body(u_ref, v_ref, ag_ref, p_ref, ph_ref, pg_ref, pgn_ref, mq_ref, mg_ref, dy_ref, dypn_ref,
             lng_ref, lnb_ref, wsm_ref, wsmt_ref, bias_ref, pw_ref, pwt_ref, ps_ref, kv_ref,
             dab_ref, dm_ref, dlng_ref, dlnb_ref, dws_ref, dbias_ref, dpw_ref, dps_ref, dkv_ref,
             mix, dvl, ddn):
        i = pl.program_id(0)

        @pl.when(i == 0)
        def _():
            for r in (dlng_ref, dlnb_ref, dws_ref, dbias_ref, dpw_ref, dps_ref, dkv_ref):
                r[...] = jnp.zeros_like(r)

        au = u_ref[...].astype(F32)
        av = v_ref[...].astype(F32)
        ag = ag_ref[...].astype(F32)
        u, du = _gelu_and_grad(au)
        v, dgelu_v = _gelu_and_grad(av)
        vhat, rstd = _layer_norm_fwd(v)
        vln = (vhat * lng_ref[...] + lnb_ref[...]).astype(BF16)
        for c in range(nchunk):
            for h in range(N_HEAD):
                rs, cs = slice(c * CHUNK, (c + 1) * CHUNK), slice(h * HEAD, (h + 1) * HEAD)
                mix[rs, cs] = _dot(wsm_ref[h], vln[rs, cs]) + bias_ref[:, cs]
        dya = dy_ref[0].astype(F32)
        sg, dsg = _silu_and_grad(ag)
        mixed = mix[...]
        dab_ref[:, 2 * WIDTH:3 * WIDTH] = (dya * u * mixed * dsg).astype(BF16)
        dab_ref[:, 0:WIDTH] = (dya * mixed * sg * du).astype(BF16)
        dmixed = dya * u * sg
        dmb = dmixed.astype(BF16)
        tril = (lax.broadcasted_iota(jnp.int32, (CHUNK, CHUNK), 1)
                <= lax.broadcasted_iota(jnp.int32, (CHUNK, CHUNK), 0))
        for c in range(nchunk):
            rs = slice(c * CHUNK, (c + 1) * CHUNK)
            dbias_ref[...] += dmixed[rs, :]
            for h in range(N_HEAD):
                cs = slice(h * HEAD, (h + 1) * HEAD)
                dvl[rs, cs] = _dot(wsmt_ref[h], dmb[rs, cs])
                dws_ref[h] += jnp.where(tril, _dot_nt(dmb[rs, cs], vln[rs, cs]), 0.0)
        dvln = dvl[...]
        dlng_ref[...] += jnp.sum(dvln * vhat, axis=0, keepdims=True)
        dlnb_ref[...] += jnp.sum(dvln, axis=0, keepdims=True)
        dvh = dvln * lng_ref[...]
        dv = rstd * (dvh - jnp.mean(dvh, axis=-1, keepdims=True)
                     - vhat * jnp.mean(dvh * vhat, axis=-1, keepdims=True))
        dab_ref[:, WIDTH:2 * WIDTH] = (dv * dgelu_v).astype(BF16)

        halo_ok = (i > 0).astype(F32)
        for c in range(nchunk):
            rs = slice(c * CHUNK, (c + 1) * CHUNK)
            for g, win in enumerate(POOL_WINDOWS):
                cs = slice(g * HEAD, (g + 1) * HEAD)
                bcur, bprev = _band_masks(win)
                cur = p_ref[rs, cs]
                if c == 0:
                    prev = (ph_ref[:, cs].astype(F32) * halo_ok).astype(BF16)
                else:
                    prev = p_ref[(c - 1) * CHUNK:c * CHUNK, cs]
                sums = _dot(bcur, cur) + _dot(bprev, prev)
                dvl[rs, cs] = sums * _inv_count(i * tm + c * CHUNK, win) - cur.astype(F32)
        dmat = dvl[...].astype(BF16)
        for g in range(4):
            cs = slice(g * HEAD, (g + 1) * HEAD)
            mix[:, cs] = _dot(dmat[:, cs], pw_ref[g])
        yg = mix[...]
        pg = pg_ref[...].astype(F32)
        dyp = dy_ref[1].astype(F32)
        spg, dspg = _silu_and_grad(pg)
        dyy = dyp * spg
        scale = ps_ref[...]
        dab_ref[:, 4 * WIDTH:5 * WIDTH] = (dyp * yg * scale * dspg).astype(BF16)
        dps_ref[...] += jnp.sum(dyy * yg, axis=0, keepdims=True)
        dyg = (dyy * scale).astype(BF16)
        for g in range(4):
            cs = slice(g * HEAD, (g + 1) * HEAD)
            dpw_ref[g] += _dot_tn(dmat[:, cs], dyg[:, cs])
            mix[:, cs] = _dot(dyg[:, cs], pwt_ref[g])
        next_ok = (i + 1 < S // tm).astype(F32)
        dygn = (dypn_ref[...].astype(F32) * _silu(pgn_ref[...].astype(F32)) * scale * next_ok).astype(BF16)
        for c in range(nchunk + 1):
            for g, win in enumerate(POOL_WINDOWS):
                cs = slice(g * HEAD, (g + 1) * HEAD)
                if c < nchunk:
                    dd = mix[c * CHUNK:(c + 1) * CHUNK, cs]
                else:
                    dd = _dot(dygn[:, cs], pwt_ref[g])
                ddn[c * CHUNK:(c + 1) * CHUNK, cs] = dd * _inv_count(i * tm + c * CHUNK, win)
        ddnb = ddn[...].astype(BF16)
        for c in range(nchunk):
            rs = slice(c * CHUNK, (c + 1) * CHUNK)
            ns = slice((c + 1) * CHUNK, (c + 2) * CHUNK)
            for g, win in enumerate(POOL_WINDOWS):
                cs = slice(g * HEAD, (g + 1) * HEAD)
                bcur, bprev = _band_masks(win)
                dp = _dot_tn(bcur, ddnb[rs, cs]) + _dot_tn(bprev, ddnb[ns, cs]) - mix[rs, cs]
                dab_ref[rs, 3 * WIDTH + g * HEAD:3 * WIDTH + (g + 1) * HEAD] = dp.astype(BF16)

        mg = mg_ref[...].astype(F32)
        dym = dy_ref[2].astype(F32)
        smg, dsmg = _silu_and_grad(mg)
        dob = (dym * smg).astype(BF16)
        for h in range(N_HEAD):
            cs = slice(h * HEAD, (h + 1) * HEAD)
            vs = slice(WIDTH + h * HEAD, WIDTH + (h + 1) * HEAD)
            qh = mq_ref[:, cs]
            p = _mem_softmax(qh, kv_ref[:, cs])
            pb = p.astype(BF16)
            mix[:, cs] = _dot(pb, kv_ref[:, vs])
            dp = _dot_nt(dob[:, cs], kv_ref[:, vs])
            ds = (p * (dp - jnp.sum(p * dp, axis=-1, keepdims=True))).astype(BF16)
            dm_ref[:, cs] = (_dot(ds, kv_ref[:, cs]) * ATT_SCALE).astype(BF16)
            dkv_ref[:, cs] += _dot_tn(ds, qh) * ATT_SCALE
            dkv_ref[:, vs] += _dot_tn(pb, dob[:, cs])
        dm_ref[:, WIDTH:2 * WIDTH] = (dym * mix[...] * dsmg).astype(BF16)

    blk = tm // CHUNK
    small = [_full((1, WIDTH)), _full((1, WIDTH)), _full((N_HEAD, CHUNK, CHUNK)), _full((CHUNK, WIDTH)),
             _full((4, HEAD, HEAD)), _full((1, WIDTH)), _full((MEM_LEN, 2 * WIDTH))]
    return pl.pallas_call(
        body, name="abm_bwd",
        grid=(S // tm,),
        in_specs=[_rows(tm, WIDTH, CB_U), _rows(tm, WIDTH, CB_V), _rows(tm, WIDTH, CB_AGATE),
                  _rows(tm, WIDTH, CB_PIN),
                  pl.BlockSpec((CHUNK, WIDTH), lambda i: (jnp.maximum(i * blk - 1, 0), CB_PIN)),
                  _rows(tm, WIDTH, CB_PGATE),
                  pl.BlockSpec((CHUNK, WIDTH), lambda i: (jnp.minimum(i * blk + blk, nblk - 1), CB_PGATE)),
                  _rows(tm, WIDTH, CB_MQ), _rows(tm, WIDTH, CB_MGATE),
                  pl.BlockSpec((3, tm, WIDTH), lambda i: (0, i, 0)),
                  pl.BlockSpec((None, CHUNK, WIDTH), lambda i: (1, jnp.minimum(i * blk + blk, nblk - 1), 0)),
                  _full((1, WIDTH)), _full((1, WIDTH)), _full((N_HEAD, CHUNK, CHUNK)), _full((N_HEAD, CHUNK, CHUNK)),
                  _full((CHUNK, WIDTH)), _full((4, HEAD, HEAD)), _full((4, HEAD, HEAD)), _full((1, WIDTH)),
                  _full((MEM_LEN, 2 * WIDTH))],
        out_specs=[_rows(tm, 5 * WIDTH), _rows(tm, 2 * WIDTH)] + small,
        out_shape=[jax.ShapeDtypeStruct((S, D_BRANCHES), BF16), jax.ShapeDtypeStruct((S, 2 * WIDTH), BF16),
                   jax.ShapeDtypeStruct((1, WIDTH), F32), jax.ShapeDtypeStruct((1, WIDTH), F32),
                   jax.ShapeDtypeStruct((N_HEAD, CHUNK, CHUNK), F32), jax.ShapeDtypeStruct((CHUNK, WIDTH), F32),
                   jax.ShapeDtypeStruct((4, HEAD, HEAD), F32), jax.ShapeDtypeStruct((1, WIDTH), F32),
                   jax.ShapeDtypeStruct((MEM_LEN, 2 * WIDTH), F32)],
        scratch_shapes=[pltpu.VMEM((tm, WIDTH), F32), pltpu.VMEM((tm, WIDTH), F32),
                        pltpu.VMEM((tm + CHUNK, WIDTH), F32)],
        compiler_params=_params(("arbitrary",)),
    )(proj, proj, proj, proj, proj, proj, proj, proj, proj, dy3, dy3,
      ln_g, ln_b, wsm, wsm_t, bias_full, pool_w, pool_wt, pool_scale, kv)


def _bias_reduce(dbias_full):
    def body(d_ref, o_ref):
        d = d_ref[...]
        o_ref[...] = _put_cols([jnp.sum(d[:, h * HEAD:(h + 1) * HEAD], axis=1, keepdims=True) for h in range(N_HEAD)])

    return pl.pallas_call(body, name="bias_reduce", out_shape=jax.ShapeDtypeStruct((CHUNK, 128), F32))(dbias_full)


def _mem_bwd(mem, g, mem_n, w, dkv):
    def body(m_ref, g_ref, mn_ref, w_ref, dkv_ref, dw_ref, dg_ref):
        dkvb = dkv_ref[...].astype(BF16)
        dw_ref[...] = _dot_tn(mn_ref[...], dkvb).astype(BF16)
        dmn = _dot_nt(dkvb, w_ref[...])
        xf = m_ref[...]
        r = lax.rsqrt(jnp.mean(xf * xf, axis=-1, keepdims=True) + EPS)
        dg_ref[...] = jnp.sum(dmn * xf * r, axis=0, keepdims=True)

    return pl.pallas_call(
        body, name="mem_bwd",
        out_shape=[jax.ShapeDtypeStruct((D_MODEL, 2 * WIDTH), BF16), jax.ShapeDtypeStruct((1, D_MODEL), F32)],
        compiler_params=pltpu.CompilerParams(vmem_limit_bytes=VMEM_LIMIT),
    )(mem, g, mem_n, w, dkv)


def _dh_bwd(dpb, dpg, wt, x, g, dxo, parts=()):
    S = x.shape[0]
    tm, tkb = 1024, D_BRANCHES // 4
    nkb, nkg = 4, D_GATES // GATE_TILE
    nk = nkb + nkg
    ni = S // tm
    n = len(parts)

    def body(dpb_ref, wbr_ref, dpg_ref, wg_ref, x_ref, g_ref, dxo_ref, *rest):
        p_in = rest[:n]
        dx_ref, dg_ref = rest[n:n + 2]
        p_out, acc, sems = rest[n + 2:2 * n + 2], rest[2 * n + 2], rest[2 * n + 3:]
        i, kk = pl.program_id(0), pl.program_id(1)

        @pl.when(jnp.logical_and(i == 0, kk == 0))
        def _():
            dg_ref[...] = jnp.zeros_like(dg_ref)
            if n:
                _comm_start(_rs_second(p_in, p_out, *sems))

        @pl.when(kk == 0)
        def _():
            acc[...] = jnp.zeros_like(acc)

        @pl.when(kk < nkb)
        def _():
            acc[...] += _dot(dpb_ref[...], wbr_ref[...])

        @pl.when(kk >= nkb)
        def _():
            acc[...] += _dot(dpg_ref[...], wg_ref[...])

        @pl.when(kk == nk - 1)
        def _():
            xf = x_ref[...]
            r = lax.rsqrt(jnp.mean(xf * xf, axis=-1, keepdims=True) + EPS)
            xhat = xf * r
            dh = acc[...]
            dg_ref[...] += jnp.sum(dh * xhat, axis=0, keepdims=True)
            dxh = dh * g_ref[...]
            dx_ref[...] = dxo_ref[...] + r * (dxh - xhat * jnp.mean(dxh * xhat, axis=-1, keepdims=True))

        if n:
            @pl.when(jnp.logical_and(i == ni - 1, kk == nk - 1))
            def _():
                _comm_wait(_rs_second(p_in, p_out, *sems))

    res = pl.pallas_call(
        body, name="dh_bwd_scatter" if n else "dh_bwd",
        grid=(ni, nk),
        in_specs=[pl.BlockSpec((tm, tkb), lambda i, k: (i, jnp.minimum(k, nkb - 1))),
                  pl.BlockSpec((tkb, D_MODEL), lambda i, k: (jnp.minimum(k, nkb - 1), 0)),
                  pl.BlockSpec((tm, GATE_TILE), lambda i, k: (i, jnp.maximum(k - nkb, 0))),
                  pl.BlockSpec((GATE_TILE, D_MODEL),
                               lambda i, k: (D_BRANCHES // GATE_TILE + jnp.maximum(k - nkb, 0), 0)),
                  pl.BlockSpec((tm, D_MODEL), lambda i, k: (i, 0)), pl.BlockSpec((1, D_MODEL), lambda i, k: (0, 0)),
                  pl.BlockSpec((tm, D_MODEL), lambda i, k: (i, 0))] + [ANY] * n,
        out_specs=[pl.BlockSpec((tm, D_MODEL), lambda i, k: (i, 0)), pl.BlockSpec((1, D_MODEL), lambda i, k: (0, 0))]
                  + [ANY] * n,
        out_shape=[jax.ShapeDtypeStruct((S, D_MODEL), F32), jax.ShapeDtypeStruct((1, D_MODEL), F32)]
                  + [jax.ShapeDtypeStruct(p.shape, p.dtype) for p in parts],
        scratch_shapes=[pltpu.VMEM((tm, D_MODEL), F32)] + (_dma_sems(3 * n, 3 * n, n) if n else []),
        compiler_params=_params(("arbitrary", "arbitrary")),
    )(dpb, wt, dpg, wt, x, g, dxo, *parts)
    return res[0], res[1], list(res[2:])


def _dw_in(h, dpb, dpg, parts=()):
    S = h.shape[0]
    tk = 1024
    nk = S // tk
    n = len(parts)
    tmb = D_BRANCHES // 4
    ng = D_GATES // GATE_TILE

    def accumulate(a_ref, h_ref, o_ref, acc):
        kk = pl.program_id(1)

        @pl.when(kk == 0)
        def _():
            acc[...] = jnp.zeros_like(acc)

        acc[...] += _dot_tn(a_ref[...], h_ref[...])

        @pl.when(kk == nk - 1)
        def _():
            o_ref[...] = acc[...].astype(BF16)

    def branches(a_ref, h_ref, *rest):
        p_in, o_ref, p_out = rest[:n], rest[n], rest[n + 1:2 * n + 1]
        acc, sems = rest[2 * n + 1], rest[2 * n + 2:]
        i, kk = pl.program_id(0), pl.program_id(1)

        if n:
            @pl.when(jnp.logical_and(i == 0, kk == 0))
            def _():
                _comm_start(_rs_second(p_in, p_out, *sems))

        accumulate(a_ref, h_ref, o_ref, acc)

        if n:
            @pl.when(jnp.logical_and(i == 3, kk == nk - 1))
            def _():
                _comm_wait(_rs_second(p_in, p_out, *sems))

    def gates(a_ref, h_ref, dst_ref, o_ref, acc):
        accumulate(a_ref, h_ref, o_ref, acc)

    res = pl.pallas_call(
        branches, name="dw_in_branches_scatter" if n else "dw_in_branches",
        grid=(4, nk),
        in_specs=[pl.BlockSpec((tk, tmb), lambda i, k: (k, i)), pl.BlockSpec((tk, D_MODEL), lambda i, k: (k, 0))]
                 + [ANY] * n,
        out_specs=[pl.BlockSpec((tmb, D_MODEL), lambda i, k: (i, 0))] + [ANY] * n,
        out_shape=[jax.ShapeDtypeStruct((D_IN, D_MODEL), BF16)]
                  + [jax.ShapeDtypeStruct(p.shape, p.dtype) for p in parts],
        scratch_shapes=[pltpu.VMEM((tmb, D_MODEL), F32)] + (_dma_sems(3 * n, 3 * n, n) if n else []),
        compiler_params=_params(("arbitrary", "arbitrary")),
    )(dpb, h, *parts)
    dwt = pl.pallas_call(
        gates, name="dw_in_gates",
        grid=(ng, nk),
        in_specs=[pl.BlockSpec((tk, GATE_TILE), lambda i, k: (k, i)), pl.BlockSpec((tk, D_MODEL), lambda i, k: (k, 0)),
                  ANY],
        out_specs=pl.BlockSpec((GATE_TILE, D_MODEL), lambda i, k: (D_BRANCHES // GATE_TILE + i, 0)),
        out_shape=jax.ShapeDtypeStruct((D_IN, D_MODEL), BF16),
        input_output_aliases={2: 0},
        scratch_shapes=[pltpu.VMEM((GATE_TILE, D_MODEL), F32)],
        compiler_params=_params(("parallel", "arbitrary")),
    )(dpg, h, res[0])
    return dwt, list(res[1:])


def _row_tile(R, C, block_bytes=2 << 20):
    for cand in range(min(R, block_bytes // (C * 4)) // 8 * 8, 0, -8):
        if R % cand == 0:
            return cand
    return R


def _adamw_update(p_ref, w_ref, m_ref, v_ref, g_ref, d_ref, nm_ref, nv_ref):
    c1 = 1.0 / (1.0 - ADAM_B1 ** ADAM_STEP)
    c2 = 1.0 / (1.0 - ADAM_B2 ** ADAM_STEP)
    g = p_ref[0].astype(F32)
    for k in range(1, p_ref.shape[0]):
        g = g + p_ref[k].astype(F32)
    nm = ADAM_B1 * m_ref[...] + (1.0 - ADAM_B1) * g
    nv = ADAM_B2 * v_ref[...] + (1.0 - ADAM_B2) * (g * g)
    g_ref[...] = g
    nm_ref[...] = nm
    nv_ref[...] = nv
    d_ref[...] = -ADAM_LR * ((nm * c1) / (jnp.sqrt(nv * c2) + ADAM_EPS) + ADAM_WD * w_ref[...])


def _adamw(parts, w, m, v, name):
    P, R, C = parts.shape
    tr = _row_tile(R, C)

    def body(*refs):
        _adamw_update(*refs)

    spec = pl.BlockSpec((tr, C), lambda i: (i, 0))
    return pl.pallas_call(
        body, name=name,
        grid=(R // tr,),
        in_specs=[pl.BlockSpec((P, tr, C), lambda i: (0, i, 0)), spec, spec, spec],
        out_specs=[spec] * 4,
        out_shape=[jax.ShapeDtypeStruct((R, C), F32)] * 4,
        compiler_params=_params(("parallel",)),
    )(parts, w, m, v)


def _adamw_layers(parts, w, m, v, name):
    depth = len(parts)
    P, R, C = parts[0].shape
    tr = _row_tile(R, C, 1 << 20)

    def body(*refs):
        layer = pl.program_id(0)
        for k in range(depth):
            @pl.when(layer == k)
            def _(k=k):
                _adamw_update(refs[k], *refs[depth:])

    def part_spec(k):
        return pl.BlockSpec((P, tr, C), lambda l, i: (0, jnp.where(l == k, i, 0), 0))

    spec = pl.BlockSpec((None, tr, C), lambda l, i: (l, i, 0))
    return pl.pallas_call(
        body, name=name,
        grid=(depth, R // tr),
        in_specs=[part_spec(k) for k in range(depth)] + [spec] * 3,
        out_specs=[spec] * 4,
        out_shape=[jax.ShapeDtypeStruct((depth, R, C), F32)] * 4,
        compiler_params=_params(("arbitrary", "arbitrary")),
    )(*parts, w, m, v)


def _place():
    return lax.axis_index("x"), lax.axis_index("y"), lax.axis_index("c")


def _all_gather(shards):
    n = len(shards)

    def body(*refs):
        ins, outs = refs[:n], refs[n:2 * n]
        send1, recv1, local_sems, send2, recv2 = refs[2 * n:]
        first = _ag_first(ins, outs, send1, recv1, local_sems)
        second = _ag_second(outs, send2, recv2)
        _comm_start(first)
        for j in range(3):
            for a in range(n):
                first[2][4 * a + 1 + j].wait_recv()
            for a in range(n):
                second[1][3 * a + j].start()
        for a in range(n):
            first[2][4 * a].wait_recv()
        for cp in second[2]:
            cp.wait_recv()
        for cp in first[1] + second[1]:
            cp.wait_send()
        for cp in first[0]:
            cp.wait()

    return pl.pallas_call(
        body, name="weights_all_gather",
        in_specs=[ANY] * n, out_specs=[ANY] * n,
        out_shape=[jax.ShapeDtypeStruct((N_DEV,) + s.shape, s.dtype) for s in shards],
        scratch_shapes=_dma_sems(4 * n, 4 * n, n, 3 * n, 3 * n),
        compiler_params=pltpu.CompilerParams(has_side_effects=True),
    )(*shards)


N_BIG = 4


def _dev(p):
    return 4 * p[0] + 2 * p[1] + p[2]


def _other_chips(x, y):
    return [(1 - x, y), (x, 1 - y), (1 - x, 1 - y)]


def _remote(src, dst, send_sems, recv_sems, k, to):
    return pltpu.make_async_remote_copy(src_ref=src, dst_ref=dst, send_sem=send_sems.at[k], recv_sem=recv_sems.at[k],
                                        device_id=to, device_id_type=MESH)


def _ag_first(ins, outs, send_sems, recv_sems, local_sems):
    x, y, c = _place()
    me = (x, y, c)
    targets = [(x, y, 1 - c)] + [(*chip, c) for chip in _other_chips(x, y)]
    local, out, inc = [], [], []
    for a in range(len(ins)):
        local.append(pltpu.make_async_copy(ins[a], outs[a].at[_dev(me)], local_sems.at[a]))
        for k, to in enumerate(targets):
            out.append(_remote(ins[a], outs[a].at[_dev(me)], send_sems, recv_sems, 4 * a + k, to))
            inc.append(_remote(ins[a], outs[a].at[_dev(to)], send_sems, recv_sems, 4 * a + k, to))
    return local, out, inc


def _ag_second(bufs, send_sems, recv_sems):
    x, y, c = _place()
    out, inc = [], []
    for a in range(len(bufs)):
        for j, chip in enumerate(_other_chips(x, y)):
            mine, theirs = bufs[a].at[_dev((*chip, c))], bufs[a].at[_dev((*chip, 1 - c))]
            out.append(_remote(mine, mine, send_sems, recv_sems, 3 * a + j, (x, y, 1 - c)))
            inc.append(_remote(theirs, theirs, send_sems, recv_sems, 3 * a + j, (x, y, 1 - c)))
    return [], out, inc


def _rs_first(ins, outs, send_sems, recv_sems):
    x, y, c = _place()
    out = [_remote(ins[a].at[j, 1 - c], outs[a].at[j], send_sems, recv_sems, N_CHIP * a + j, (x, y, 1 - c))
           for a in range(len(ins)) for j in range(N_CHIP)]
    return [], out, out


def _rs_second(ins, outs, send_sems, recv_sems, local_sems):
    x, y, c = _place()
    my_chip = 2 * x + y
    local, out, inc = [], [], []
    for a in range(len(ins)):
        local.append(pltpu.make_async_copy(ins[a].at[my_chip], outs[a].at[my_chip], local_sems.at[a]))
        for k, (ox, oy) in enumerate(_other_chips(x, y)):
            out.append(_remote(ins[a].at[2 * ox + oy], outs[a].at[my_chip], send_sems, recv_sems, 3 * a + k, (ox, oy, c)))
            inc.append(_remote(ins[a].at[2 * ox + oy], outs[a].at[2 * ox + oy], send_sems, recv_sems, 3 * a + k,
                               (ox, oy, c)))
    return local, out, inc


def _comm_start(exchange):
    local, out, _ = exchange
    for cp in local + out:
        cp.start()


def _comm_wait(exchange):
    local, out, inc = exchange
    for cp in inc:
        cp.wait_recv()
    for cp in out:
        cp.wait_send()
    for cp in local:
        cp.wait()


def _dma_sems(*counts):
    return [pltpu.SemaphoreType.DMA((n,)) for n in counts]


def _rs_sibling(grads):
    n = len(grads)

    def body(*refs):
        ex = _rs_first(refs[:n], refs[n:2 * n], *refs[2 * n:])
        _comm_start(ex)
        _comm_wait(ex)

    return pl.pallas_call(
        body, name="grads_to_sibling",
        in_specs=[ANY] * n, out_specs=[ANY] * n,
        out_shape=[jax.ShapeDtypeStruct(g.shape[:1] + g.shape[2:], g.dtype) for g in grads],
        scratch_shapes=_dma_sems(N_CHIP * n, N_CHIP * n),
        compiler_params=pltpu.CompilerParams(has_side_effects=True),
    )(*grads)


def _pair_sum(grads, recvs):
    n = len(grads)

    def body(c_ref, *refs):
        for a in range(n):
            refs[2 * n + a][...] = (refs[a][...].astype(F32) + refs[n + a][...].astype(F32)).astype(BF16)

    def g_spec(g):
        return pl.BlockSpec((None, None) + g.shape[2:], lambda j, c_ref: (j, c_ref[0], 0, 0))

    def r_spec(r):
        return pl.BlockSpec((None,) + r.shape[1:], lambda j, c_ref: (j, 0, 0))

    return pl.pallas_call(
        body, name="pair_sum",
        grid_spec=pltpu.PrefetchScalarGridSpec(
            num_scalar_prefetch=1, grid=(N_CHIP,),
            in_specs=[g_spec(g) for g in grads] + [r_spec(r) for r in recvs],
            out_specs=[r_spec(r) for r in recvs]),
        out_shape=[jax.ShapeDtypeStruct(r.shape, BF16) for r in recvs],
        compiler_params=_params(("parallel",)),
    )(lax.axis_index("c").reshape(1).astype(jnp.int32), *grads, *recvs)


SMALL_ROWS = 544


def _all_reduce_small(buf, parts=()):
    n = len(parts)

    def body(in_ref, *rest):
        p_in, out_ref, p_out = rest[:n], rest[n], rest[n + 1:2 * n + 1]
        recv, acc, send1, recv1, send2, recv2 = rest[2 * n + 1:2 * n + 7]
        scatter_sems = rest[2 * n + 7:]
        x, y, c = _place()
        me = 4 * x + 2 * y + c
        peers = [(x ^ (r >> 2), y ^ ((r >> 1) & 1), c ^ (r & 1)) for r in range(1, N_DEV)]

        def idx(p):
            return 4 * p[0] + 2 * p[1] + p[2]

        if n:
            _comm_start(_rs_second(p_in, p_out, *scatter_sems))
        first = [pltpu.make_async_remote_copy(
            src_ref=in_ref.at[idx(p)], dst_ref=recv.at[me], send_sem=send1.at[r], recv_sem=recv1.at[r],
            device_id=p, device_id_type=MESH) for r, p in enumerate(peers)]
        for cp in first:
            cp.start()
        recv[me] = in_ref[me]
        for r, p in enumerate(peers):
            pltpu.make_async_remote_copy(
                src_ref=in_ref.at[idx(p)], dst_ref=recv.at[idx(p)], send_sem=send1.at[r], recv_sem=recv1.at[r],
                device_id=p, device_id_type=MESH).wait_recv()
        total = recv[0]
        for k in range(1, N_DEV):
            total = total + recv[k]
        acc[...] = total
        out_ref[me] = total
        second = [pltpu.make_async_remote_copy(
            src_ref=acc, dst_ref=out_ref.at[me], send_sem=send2.at[r], recv_sem=recv2.at[r],
            device_id=p, device_id_type=MESH) for r, p in enumerate(peers)]
        for cp in second:
            cp.start()
        for r, p in enumerate(peers):
            pltpu.make_async_remote_copy(
                src_ref=acc, dst_ref=out_ref.at[idx(p)], send_sem=send2.at[r], recv_sem=recv2.at[r],
                device_id=p, device_id_type=MESH).wait_recv()
        for cp in first + second:
            cp.wait_send()
        if n:
            _comm_wait(_rs_second(p_in, p_out, *scatter_sems))

    vm = pl.BlockSpec(memory_space=pltpu.VMEM)
    res = pl.pallas_call(
        body, name="small_grads_all_reduce",
        in_specs=[vm] + [ANY] * n, out_specs=[vm] + [ANY] * n,
        out_shape=[jax.ShapeDtypeStruct(buf.shape, F32)] + [jax.ShapeDtypeStruct(p.shape, p.dtype) for p in parts],
        scratch_shapes=[pltpu.VMEM(buf.shape, F32), pltpu.VMEM(buf.shape[1:], F32)] + _dma_sems(7, 7, 7, 7)
                       + (_dma_sems(3 * n, 3 * n, n) if n else []),
        compiler_params=pltpu.CompilerParams(has_side_effects=True, vmem_limit_bytes=VMEM_LIMIT),
    )(buf, *parts)
    return res[0], list(res[1:])


def _dilate(a, d):
    if d == 1:
        return a
    S, C = a.shape
    return a.reshape(S // d, d, C).transpose(1, 0, 2).reshape(S, C)


def _undilate(a, d):
    if d == 1:
        return a
    S, C = a.shape
    return a.reshape(d, S // d, C).transpose(1, 0, 2).reshape(S, C)


def _cols(a, cb, n=1):
    return a[:, cb * WIDTH:(cb + n) * WIDTH]


def _to_blocks(g, kind):
    if kind == "rows":
        C = g.shape[1]
        return g.reshape(N_CHIP, 2, -1, C)
    return g.reshape(4 * WIDTH, N_CHIP, 2, -1).transpose(1, 2, 0, 3)


SMALL = ("norm_g", "gm_ln_g", "gm_ln_b", "gm_ws", "gm_bs", "pool_w", "pool_scale", "mem_norm_g", "final_norm_g")


def _pack_small(tree):
    flat = jnp.concatenate([tree[k].reshape(-1, 128) for k in SMALL], axis=0)
    return jnp.pad(flat, ((0, N_DEV * SMALL_ROWS - flat.shape[0]), (0, 0)))


def _unpack_small(flat, like):
    out, at = {}, 0
    for k in SMALL:
        rows = like[k].size // 128
        out[k] = flat[at:at + rows].reshape(like[k].shape)
        at += rows
    return out


def _make_layer(wt, wkv, wb, wout, norm_g, mem_norm_g, ln_g, ln_b, gm_ws, gm_bs, pool_w, pool_scale):
    tril = jnp.tril(jnp.ones((CHUNK, CHUNK), bool))
    wsm = jnp.where(tril, gm_ws, 0.0).astype(BF16)
    pw = pool_w.astype(BF16)
    return dict(wt=wt, wkv=wkv, wb=wb, wout=wout, g=norm_g[None], mg=mem_norm_g[None], ln_g=ln_g[None],
                ln_b=ln_b[None], wsm=wsm, wsm_t=wsm.transpose(0, 2, 1), pw=pw, pw_t=pw.transpose(0, 2, 1),
                ps=pool_scale[None], bias=jnp.repeat(gm_bs.T, HEAD, axis=1))


def _layer_fwd(xl, mem0, L, next_shards=()):
    S = xl.shape[0]
    proj, gates, h, half_gathered = _in_proj(xl, L["g"], L["wt"], next_shards[:1])
    kv, mem_n = _mem_kv(mem0, L["mg"], L["wkv"])
    y4, gathered = _abm_fwd(proj, L["ln_g"], L["ln_b"], L["wsm"], L["bias"], L["pw"], L["ps"], kv, half_gathered)
    o_g, l_g = [], []
    for gi, d in enumerate(DILATIONS):
        if d == 1:
            o, lse = _attn_fwd(proj, CB_Q0, proj, CB_K, proj, CB_CV, S // CHUNK)
        else:
            o, lse = _attn_fwd_dilated(proj, CB_Q0 + gi, CB_K, CB_CV, d)
        o_g.append(o)
        l_g.append(lse)
    (xn, y4, oc, lse, z), rest = _merge_fwd(xl, y4, o_g, l_g, proj, gates, L["wb"], L["wout"], next_shards[1:])
    saved = dict(x=xl, proj=proj, gates=gates, h=h, kv=kv, mem_n=mem_n, y4=y4, oc=oc, lse=lse, z=z)
    return xn, saved, gathered + rest


def _place_cols(dst, piece, cb):
    return lax.dynamic_update_slice(dst, piece, (0, cb * WIDTH))


def _layer_bwd(dx, mem0, L, sv, later=()):
    S = dx.shape[0]
    proj = sv["proj"]
    (dy3, doc, delta, dcg, dgm, dwb, dwout), from_sibling = _merge_bwd(
        dx, sv["y4"], sv["oc"], sv["z"], proj, sv["gates"], L["wb"], L["wout"], later)
    pair = _pair_sum(later, from_sibling) if later else ()
    dpb, dm, dlng, dlnb, dws, dbias, dpw, dps, dkv = _abm_bwd(
        proj, dy3, L["ln_g"], L["ln_b"], L["wsm"], L["wsm_t"], L["bias"], L["pw"], L["pw_t"], L["ps"], sv["kv"])
    dk, dv = None, None
    for gi, d in enumerate(DILATIONS):
        if d == 1:
            r = _attn_bwd(proj, CB_Q0, proj, CB_K, proj, CB_CV, doc, sv["lse"], delta, S // CHUNK)
        else:
            r = _attn_bwd_dilated(proj, CB_Q0 + gi, CB_K, CB_CV, doc, sv["lse"], delta, d)
        dpb = _place_cols(dpb, r[0], CB_Q0 + gi)
        dkg, dvg = r[1].astype(F32), r[2].astype(F32)
        dk = dkg if dk is None else dk + dkg
        dv = dvg if dv is None else dv + dvg
    dpb = _place_cols(dpb, dk.astype(BF16), CB_K)
    dpb = _place_cols(dpb, dv.astype(BF16), CB_CV)
    dpb = _place_cols(dpb, dcg, CB_CGATE)
    dpb = _place_cols(dpb, dm, CB_MQ)
    dwkv, dmg = _mem_bwd(mem0, L["mg"], sv["mem_n"], L["wkv"], dkv)
    dwin_t, parts_rest = _dw_in(sv["h"], dpb, dgm, pair[1:])
    dxi, dng, parts = _dh_bwd(dpb, dgm, L["wt"], sv["x"], L["g"], dx, pair[:1])
    parts = parts + parts_rest
    big = dict(w_in=dwin_t, w_mem_kv=dwkv, w_branch=dwb, w_out=dwout)
    small = dict(norm_g=dng[0], gm_ln_g=dlng[0], gm_ln_b=dlnb[0], gm_ws=dws,
                 gm_bs=_bias_reduce(dbias)[:, :N_HEAD].T, pool_w=dpw, pool_scale=dps[0], mem_norm_g=dmg[0])
    return dxi, big, small, parts


BIG = ("w_in", "w_mem_kv", "w_branch", "w_out")


def _blocked(big):
    return [_to_blocks(big["w_in"], "rows"), _to_blocks(big["w_mem_kv"], "rows"),
            _to_blocks(big["w_branch"], "branch"), _to_blocks(big["w_out"], "rows")]


def _full_weights(gathered):
    win_t, wkv, wb, wout = gathered
    return (win_t.reshape(D_IN, D_MODEL), wkv.reshape(D_MODEL, 2 * WIDTH),
            wb.reshape(N_DEV, 4, WIDTH, -1).transpose(1, 2, 0, 3).reshape(4, WIDTH, D_MODEL),
            wout.reshape(D_MODEL, D_MODEL))


def kernel(x, mem, norm_g, w_in, gm_ln_g, gm_ln_b, gm_ws, gm_bs, pool_w, pool_scale, mem_norm_g, w_mem_kv, w_branch, w_out, final_norm_g, loss_target, m_norm_g, m_w_in, m_gm_ln_g, m_gm_ln_b, m_gm_ws, m_gm_bs, m_pool_w, m_pool_scale, m_mem_norm_g, m_w_mem_kv, m_w_branch, m_w_out, m_final_norm_g, v_norm_g, v_w_in, v_gm_ln_g, v_gm_ln_b, v_gm_ws, v_gm_bs, v_pool_w, v_pool_scale, v_mem_norm_g, v_w_mem_kv, v_w_branch, v_w_out, v_final_norm_g):
    x0 = x[0]
    mem0 = mem[0]
    tgt = loss_target[0]
    S = x0.shape[0]

    shards = [[w_in[l].T.astype(BF16), w_mem_kv[l].astype(BF16), w_branch[l].astype(BF16).reshape(4 * WIDTH, -1),
               w_out[l].astype(BF16)] for l in range(DEPTH)]
    gathered = _all_gather(shards[0])
    layers, saved = [], []
    xl = x0
    for l in range(DEPTH):
        layers.append(_make_layer(*_full_weights(gathered), norm_g[l], mem_norm_g[l], gm_ln_g[l], gm_ln_b[l],
                                  gm_ws[l], gm_bs[l], pool_w[l], pool_scale[l]))
        xl, sv, gathered = _layer_fwd(xl, mem0, layers[l], shards[l + 1] if l + 1 < DEPTH else ())
        saved.append(sv)

    loss_part, dx, d_final = _loss_head(xl, final_norm_g[None], tgt)
    loss = lax.psum(loss_part[0, 0], ("x", "y", "c"))

    small = {k: [None] * DEPTH for k in SMALL if k != "final_norm_g"}
    parts = [None] * DEPTH
    later = ()
    for l in reversed(range(DEPTH)):
        dx, gb, gs, done = _layer_bwd(dx, mem0, layers[l], saved[l], later)
        if later:
            parts[l + 1] = done
        later = _blocked(gb)
        for k in gs:
            small[k][l] = gs[k]
    grad_x = dx[None]
    small_tree = {k: jnp.stack(small[k]) for k in small}
    small_tree["final_norm_g"] = d_final[0]
    reduced, parts[0] = _all_reduce_small(_pack_small(small_tree).reshape(N_DEV, SMALL_ROWS, 128),
                                          _pair_sum(later, _rs_sibling(later)))

    weights = dict(norm_g=norm_g, w_in=w_in, gm_ln_g=gm_ln_g, gm_ln_b=gm_ln_b, gm_ws=gm_ws, gm_bs=gm_bs,
                   pool_w=pool_w, pool_scale=pool_scale, mem_norm_g=mem_norm_g, w_mem_kv=w_mem_kv,
                   w_branch=w_branch, w_out=w_out, final_norm_g=final_norm_g)
    m_in = dict(norm_g=m_norm_g, w_in=m_w_in, gm_ln_g=m_gm_ln_g, gm_ln_b=m_gm_ln_b, gm_ws=m_gm_ws, gm_bs=m_gm_bs,
                pool_w=m_pool_w, pool_scale=m_pool_scale, mem_norm_g=m_mem_norm_g, w_mem_kv=m_w_mem_kv,
                w_branch=m_w_branch, w_out=m_w_out, final_norm_g=m_final_norm_g)
    v_in = dict(norm_g=v_norm_g, w_in=v_w_in, gm_ln_g=v_gm_ln_g, gm_ln_b=v_gm_ln_b, gm_ws=v_gm_ws, gm_bs=v_gm_bs,
                pool_w=v_pool_w, pool_scale=v_pool_scale, mem_norm_g=v_mem_norm_g, w_mem_kv=v_w_mem_kv,
                w_branch=v_w_branch, w_out=v_w_out, final_norm_g=v_final_norm_g)
    res = {}
    def view(k, arr):
        return arr.transpose(0, 2, 1) if k == "w_in" else arr

    for a, k in enumerate(BIG):
        shape = view(k, weights[k]).shape
        by_layer = [parts[l][a] for l in range(DEPTH)]
        lrc = (DEPTH,) + by_layer[0].shape[1:]
        outs = _adamw_layers(by_layer, view(k, weights[k]).reshape(lrc), view(k, m_in[k]).reshape(lrc),
                             view(k, v_in[k]).reshape(lrc), "adamw_" + k)
        res[k] = [view(k, o.reshape(shape)) for o in outs]
    outs = _adamw(reduced.reshape(1, N_DEV * SMALL_ROWS, 128), _pack_small(weights), _pack_small(m_in),
                  _pack_small(v_in), "adamw_small")
    unpacked = [_unpack_small(o, weights) for o in outs]
    for k in SMALL:
        res[k] = [u[k] for u in unpacked]

    order = ("norm_g", "w_in", "gm_ln_g", "gm_ln_b", "gm_ws", "gm_bs", "pool_w", "pool_scale", "mem_norm_g",
             "w_mem_kv", "w_branch", "w_out", "final_norm_g")
    return (loss, grad_x, *[res[k][0] for k in order], *[res[k][1] for k in order],
            *[res[k][2] for k in order], *[res[k][3] for k in order])
```

```python
import functools
import math

import jax
import jax.numpy as jnp
from jax import lax
from jax.experimental import pallas as pl
from jax.experimental.pallas import tpu as pltpu

F32 = jnp.float32
BF16 = jnp.bfloat16

D_MODEL = 1024
DEPTH = 4
WIDTH = 512
D_IN = 10752
HEAD = 128
N_HEAD = 4
CHUNK = 128
MEM_LEN = 256
POOL_WINDOWS = (2, 4, 8, 16)
DILATIONS = (1, 4, 16)
EPS = 1e-6
NEG = -1e30
ATT_SCALE = HEAD ** -0.5
N_DEV = 8
N_CHIP = 4

D_BRANCHES = 6656
D_GATES = D_IN - D_BRANCHES
CB_U, CB_V, CB_AGATE, CB_PIN, CB_PGATE = 0, 1, 2, 3, 4
CB_Q0, CB_K, CB_CV, CB_CGATE, CB_MQ, CB_MGATE = 5, 8, 9, 10, 11, 12

ADAM_LR = 0.001
ADAM_B1 = 0.9
ADAM_B2 = 0.999
ADAM_EPS = 1e-08
ADAM_WD = 0.01
ADAM_STEP = 10

VMEM_LIMIT = 56 * 1024 * 1024
MESH = pl.DeviceIdType.MESH
ANY = pl.BlockSpec(memory_space=pl.ANY)

NT = (((1,), (1,)), ((), ()))
TN = (((0,), (0,)), ((), ()))


def _dot(a, b):
    return jnp.dot(a, b, preferred_element_type=F32)


def _dot_nt(a, b):
    return lax.dot_general(a, b, NT, preferred_element_type=F32)


def _dot_tn(a, b):
    return lax.dot_general(a, b, TN, preferred_element_type=F32)


def _sigmoid(x):
    return 0.5 * jnp.tanh(0.5 * x) + 0.5


def _silu(x):
    return x * _sigmoid(x)


def _silu_and_grad(x):
    s = _sigmoid(x)
    return x * s, s * (1.0 + x * (1.0 - s))


def _gelu(x):
    return 0.5 * x * (1.0 + lax.erf(x * (2.0 ** -0.5)))


def _gelu_and_grad(x):
    cdf = 0.5 * (1.0 + lax.erf(x * (2.0 ** -0.5)))
    return x * cdf, cdf + x * jnp.exp(-0.5 * x * x) * (1.0 / math.sqrt(2.0 * math.pi))


def _col(blk, h):
    lane = lax.broadcasted_iota(jnp.int32, blk.shape, 1)
    return jnp.sum(jnp.where(lane == h, blk, 0.0), axis=1, keepdims=True)


def _put_cols(cols):
    rows = cols[0].shape[0]
    lane = lax.broadcasted_iota(jnp.int32, (rows, 128), 1)
    out = jnp.zeros((rows, 128), F32)
    for h, cv in enumerate(cols):
        out = jnp.where(lane == h, cv, out)
    return out


def _params(sem, vmem=VMEM_LIMIT):
    return pltpu.CompilerParams(dimension_semantics=sem, vmem_limit_bytes=vmem)


def _full(shape):
    nd = len(shape)
    return pl.BlockSpec(shape, lambda *_: (0,) * nd)


def _rows(tm, width, cb=0):
    return pl.BlockSpec((tm, width), lambda i: (i, cb))


GATE_TILE = 512


def _in_proj(x, g, wt, wgt, shards=()):
    S = x.shape[0]
    tm, tnb, tng = 1024, D_BRANCHES // 4, D_GATES // 4
    njb, njg = 4, 4
    n = len(shards)
    ni, nj = S // tm, njb + njg

    def body(x_ref, g_ref, wbr_ref, wg_ref, *rest):
        ins, (proj_ref, gates_ref, h_ref), outs = rest[:n], rest[n:n + 3], rest[n + 3:2 * n + 3]
        hs, sems = rest[2 * n + 3], rest[2 * n + 4:]
        i, j = pl.program_id(0), pl.program_id(1)

        if n:
            @pl.when(jnp.logical_and(i == 0, j == 0))
            def _():
                _comm_start(_ag_first(ins, outs, *sems))

        @pl.when(j == 0)
        def _():
            xf = x_ref[...]
            r = lax.rsqrt(jnp.mean(xf * xf, axis=-1, keepdims=True) + EPS)
            h = (xf * r * g_ref[...]).astype(BF16)
            hs[...] = h
            h_ref[...] = h

        @pl.when(j < njb)
        def _():
            proj_ref[...] = _dot_nt(hs[...], wbr_ref[...]).astype(BF16)

        @pl.when(j >= njb)
        def _():
            gates_ref[...] = _dot_nt(hs[...], wg_ref[...]).astype(BF16)

        if n:
            @pl.when(jnp.logical_and(i == ni - 1, j == nj - 1))
            def _():
                _comm_wait(_ag_first(ins, outs, *sems))

    def first(j):
        return jnp.minimum(j, njb - 1)

    def second(j):
        return jnp.maximum(j - njb, 0)

    res = pl.pallas_call(
        body, name="in_proj_gather" if n else "in_proj",
        grid=(ni, nj),
        in_specs=[pl.BlockSpec((tm, D_MODEL), lambda i, j: (i, 0)),
                  pl.BlockSpec((1, D_MODEL), lambda i, j: (0, 0)),
                  pl.BlockSpec((tnb, D_MODEL), lambda i, j: (first(j), 0)),
                  pl.BlockSpec((tng, D_MODEL), lambda i, j: (second(j), 0))]
                 + [ANY] * n,
        out_specs=[pl.BlockSpec((tm, tnb), lambda i, j: (i, first(j))),
                   pl.BlockSpec((tm, tng), lambda i, j: (i, second(j))),
                   pl.BlockSpec((tm, D_MODEL), lambda i, j: (i, 0))] + [ANY] * n,
        out_shape=[jax.ShapeDtypeStruct((S, D_BRANCHES), BF16), jax.ShapeDtypeStruct((S, D_GATES), BF16),
                   jax.ShapeDtypeStruct((S, D_MODEL), BF16)]
                  + [jax.ShapeDtypeStruct((N_DEV,) + s.shape, s.dtype) for s in shards],
        scratch_shapes=[pltpu.VMEM((tm, D_MODEL), BF16)] + (_dma_sems(4 * n, 4 * n, n) if n else []),
        compiler_params=_params(("arbitrary", "arbitrary")),
    )(x, g, wt, wgt, *shards)
    return res[0], res[1], res[2], list(res[3:])


def _mem_kv(mem, g, w):
    M = mem.shape[0]

    def body(m_ref, g_ref, w_ref, kv_ref, mn_ref):
        xf = m_ref[...]
        r = lax.rsqrt(jnp.mean(xf * xf, axis=-1, keepdims=True) + EPS)
        mn = (xf * r * g_ref[...]).astype(BF16)
        mn_ref[...] = mn
        kv_ref[...] = _dot(mn, w_ref[...]).astype(BF16)

    return pl.pallas_call(
        body, name="mem_kv",
        out_shape=[jax.ShapeDtypeStruct((M, 2 * WIDTH), BF16), jax.ShapeDtypeStruct((M, D_MODEL), BF16)],
        compiler_params=pltpu.CompilerParams(vmem_limit_bytes=VMEM_LIMIT),
    )(mem, g, w)


def _band_masks(win):
    t = lax.broadcasted_iota(jnp.int32, (CHUNK, CHUNK), 0)
    s = lax.broadcasted_iota(jnp.int32, (CHUNK, CHUNK), 1)
    cur = jnp.logical_and(t - s >= 0, t - s < win)
    prev = s > t + CHUNK - win
    return cur.astype(BF16), prev.astype(BF16)


def _inv_count(first_row, win):
    t = first_row + lax.broadcasted_iota(jnp.int32, (CHUNK, 1), 0)
    return 1.0 / jnp.minimum(t + 1, win).astype(F32)


def _layer_norm_fwd(v):
    mu = jnp.mean(v, axis=-1, keepdims=True)
    vc = v - mu
    var = jnp.mean(vc * vc, axis=-1, keepdims=True)
    rstd = lax.rsqrt(var + EPS)
    return vc * rstd, rstd


def _mem_softmax(q, kmem):
    s = _dot_nt(q, kmem) * ATT_SCALE
    m = jnp.max(s, axis=-1, keepdims=True)
    e = jnp.exp(s - m)
    return e * (1.0 / jnp.sum(e, axis=-1, keepdims=True))


def _abm_fwd(proj, ln_g, ln_b, wsm, bias_full, pool_w, pool_scale, kv, gathered=()):
    S = proj.shape[0]
    tm = 512
    nchunk = tm // CHUNK
    n = len(gathered)
    nsteps = S // tm

    def body(u_ref, v_ref, ag_ref, p_ref, ph_ref, pg_ref, mq_ref, mg_ref, lng_ref, lnb_ref, wsm_ref, bias_ref,
             pw_ref, ps_ref, kv_ref, *rest):
        y_ref, bufs, mix, sems = rest[n], rest[n + 1:2 * n + 1], rest[2 * n + 1], rest[2 * n + 2:]
        i = pl.program_id(0)

        if n:
            @pl.when(i == 0)
            def _():
                _comm_start(_ag_second(bufs, *sems))

        u = _gelu(u_ref[...].astype(F32))
        v = _gelu(v_ref[...].astype(F32))
        vhat, _ = _layer_norm_fwd(v)
        vln = (vhat * lng_ref[...] + lnb_ref[...]).astype(BF16)
        for c in range(nchunk):
            for h in range(N_HEAD):
                rs, cs = slice(c * CHUNK, (c + 1) * CHUNK), slice(h * HEAD, (h + 1) * HEAD)
                mix[rs, cs] = _dot(wsm_ref[h], vln[rs, cs]) + bias_ref[:, cs]
        y_ref[0] = (u * mix[...] * _silu(ag_ref[...].astype(F32))).astype(BF16)
        halo_ok = (i > 0).astype(F32)
        for c in range(nchunk):
            rs = slice(c * CHUNK, (c + 1) * CHUNK)
            for g, win in enumerate(POOL_WINDOWS):
                cs = slice(g * HEAD, (g + 1) * HEAD)
                bcur, bprev = _band_masks(win)
                cur = p_ref[rs, cs]
                if c == 0:
                    prev = (ph_ref[:, cs].astype(F32) * halo_ok).astype(BF16)
                else:
                    prev = p_ref[(c - 1) * CHUNK:c * CHUNK, cs]
                sums = _dot(bcur, cur) + _dot(bprev, prev)
                dm = sums * _inv_count(i * tm + c * CHUNK, win) - cur.astype(F32)
                mix[rs, cs] = _dot(dm.astype(BF16), pw_ref[g])
        y_ref[1] = (mix[...] * ps_ref[...] * _silu(pg_ref[...].astype(F32))).astype(BF16)
        for h in range(N_HEAD):
            cs = slice(h * HEAD, (h + 1) * HEAD)
            p = _mem_softmax(mq_ref[:, cs], kv_ref[:, cs])
            mix[:, cs] = _dot(p.astype(BF16), kv_ref[:, WIDTH + h * HEAD:WIDTH + (h + 1) * HEAD])
        y_ref[2] = (mix[...] * _silu(mg_ref[...].astype(F32))).astype(BF16)

        if n:
            @pl.when(i == nsteps - 1)
            def _():
                _comm_wait(_ag_second(bufs, *sems))

    blk = tm // CHUNK
    res = pl.pallas_call(
        body, name="abm_fwd_gather" if n else "abm_fwd",
        grid=(nsteps,),
        in_specs=[_rows(tm, WIDTH, CB_U), _rows(tm, WIDTH, CB_V), _rows(tm, WIDTH, CB_AGATE),
                  _rows(tm, WIDTH, CB_PIN),
                  pl.BlockSpec((CHUNK, WIDTH), lambda i: (jnp.maximum(i * blk - 1, 0), CB_PIN)),
                  _rows(tm, WIDTH, CB_PGATE), _rows(tm, WIDTH, CB_MQ), _rows(tm, WIDTH, CB_MGATE),
                  _full((1, WIDTH)), _full((1, WIDTH)), _full((N_HEAD, CHUNK, CHUNK)), _full((CHUNK, WIDTH)),
                  _full((4, HEAD, HEAD)), _full((1, WIDTH)), _full((MEM_LEN, 2 * WIDTH))] + [ANY] * n,
        out_specs=[pl.BlockSpec((3, tm, WIDTH), lambda i: (0, i, 0))] + [ANY] * n,
        out_shape=[jax.ShapeDtypeStruct((4, S, WIDTH), BF16)]
                  + [jax.ShapeDtypeStruct(b.shape, b.dtype) for b in gathered],
        input_output_aliases={15 + a: 1 + a for a in range(n)},
        scratch_shapes=[pltpu.VMEM((tm, WIDTH), F32)] + (_dma_sems(3 * n, 3 * n) if n else []),
        compiler_params=_params(("arbitrary",)),
    )(proj, proj, proj, proj, proj, proj, proj, proj, ln_g, ln_b, wsm, bias_full, pool_w, pool_scale, kv, *gathered)
    return res[0], list(res[1:])


ATT_TILE = 512


def _attn_fwd(q, qcb, k, kcb, v, vcb, bps):
    S = q.shape[0]
    tm = ATT_TILE
    nb = tm // CHUNK

    nblocks = nb * N_HEAD

    def body(q_ref, k_ref, v_ref, kh_ref, vh_ref, o_ref, l_ref, sc_s, sp_s, pc_s, pp_s):
        i = pl.program_id(0)

        def prev_kv(n, cs):
            if n == 0:
                return kh_ref[:, cs], vh_ref[:, cs]
            ps = slice((n - 1) * CHUNK, n * CHUNK)
            return k_ref[ps, cs], v_ref[ps, cs]

        pens = []
        for n in range(nb):
            rs = slice(n * CHUNK, (n + 1) * CHUNK)
            pens.append(jnp.full((N_HEAD * CHUNK, 1), jnp.where((i * nb + n) % bps != 0, 0.0, NEG), F32))
            for h in range(N_HEAD):
                cs = slice(h * HEAD, (h + 1) * HEAD)
                bs = slice((n * N_HEAD + h) * CHUNK, (n * N_HEAD + h + 1) * CHUNK)
                qh = q_ref[rs, cs]
                sc_s[bs, :] = _dot_nt(qh, k_ref[rs, cs])
                sp_s[bs, :] = _dot_nt(qh, prev_kv(n, cs)[0])
        row = lax.broadcasted_iota(jnp.int32, (nblocks * CHUNK, CHUNK), 0) & (CHUNK - 1)
        col = lax.broadcasted_iota(jnp.int32, (nblocks * CHUNK, CHUNK), 1)
        sc = jnp.where(col <= row, sc_s[...] * ATT_SCALE, NEG)
        sp = jnp.where(col >= row, sp_s[...] * ATT_SCALE, NEG) + jnp.concatenate(pens, axis=0)
        m = jnp.maximum(jnp.max(sc, axis=-1, keepdims=True), jnp.max(sp, axis=-1, keepdims=True))
        ec = jnp.exp(sc - m)
        ep = jnp.exp(sp - m)
        den = jnp.sum(ec, axis=-1, keepdims=True) + jnp.sum(ep, axis=-1, keepdims=True)
        inv = 1.0 / den
        pc_s[...] = (ec * inv).astype(BF16)
        pp_s[...] = (ep * inv).astype(BF16)
        lse = m + jnp.log(den)
        for n in range(nb):
            rs = slice(n * CHUNK, (n + 1) * CHUNK)
            for h in range(N_HEAD):
                cs = slice(h * HEAD, (h + 1) * HEAD)
                bs = slice((n * N_HEAD + h) * CHUNK, (n * N_HEAD + h + 1) * CHUNK)
                o = _dot(pc_s[bs, :], v_ref[rs, cs]) + _dot(pp_s[bs, :], prev_kv(n, cs)[1])
                o_ref[rs, cs] = o.astype(BF16)
            l_ref[rs, :] = _put_cols([lse[(n * N_HEAD + h) * CHUNK:(n * N_HEAD + h + 1) * CHUNK]
                                      for h in range(N_HEAD)])

    def halo(cb):
        return pl.BlockSpec((CHUNK, WIDTH), lambda i: (jnp.maximum(i * nb - 1, 0), cb))

    return pl.pallas_call(
        body, name=f"attn_fwd_{bps}",
        grid=(S // tm,),
        in_specs=[_rows(tm, WIDTH, qcb), _rows(tm, WIDTH, kcb), _rows(tm, WIDTH, vcb), halo(kcb), halo(vcb)],
        out_specs=[_rows(tm, WIDTH), _rows(tm, 128)],
        out_shape=[jax.ShapeDtypeStruct((S, WIDTH), BF16), jax.ShapeDtypeStruct((S, 128), F32)],
        scratch_shapes=[pltpu.VMEM((nblocks * CHUNK, CHUNK), F32), pltpu.VMEM((nblocks * CHUNK, CHUNK), F32),
                        pltpu.VMEM((nblocks * CHUNK, CHUNK), BF16), pltpu.VMEM((nblocks * CHUNK, CHUNK), BF16)],
        compiler_params=_params(("parallel",)),
    )(q, k, v, k, v)


def _gate_specs(tm):
    return [pl.BlockSpec((tm, D_MODEL), lambda i, b=b: (i, b)) for b in range(4)]


Y_SLOT = (0, 1, 3, 2)


def _merge_fwd(x, y4, o_g, l_g, proj, gates, wb, wout, shards=()):
    S = x.shape[0]
    tm = 256
    n = len(shards)
    nsteps = S // tm
    forward_at = nsteps - 4

    def body(x_ref, y_ref, o0, o1, o2, l0, l1, l2, cg_ref, *rest):
        gm = rest[:4]
        wb_ref, wo_ref = rest[4:6]
        s_in = rest[6:6 + n]
        xn_ref, yc_ref, oc_ref, lse_ref, z_ref = rest[6 + n:11 + n]
        s_out, ocs, sems = rest[11 + n:11 + 2 * n], rest[11 + 2 * n], rest[12 + 2 * n:]
        i = pl.program_id(0)

        if n:
            @pl.when(i == 0)
            def _():
                _comm_start(_ag_first(s_in, s_out, *sems[:3]))

            @pl.when(i == forward_at)
            def _():
                incoming = _ag_first(s_in, s_out, *sems[:3])[2]
                for a in range(n):
                    for k in range(1, 4):
                        incoming[4 * a + k].wait_recv()
                _comm_start(_ag_second(s_out, *sems[3:]))

        lcols = []
        for h in range(N_HEAD):
            cs = slice(h * HEAD, (h + 1) * HEAD)
            ls = [_col(l[...], h) for l in (l0, l1, l2)]
            m = jnp.maximum(jnp.maximum(ls[0], ls[1]), ls[2])
            tot = jnp.exp(ls[0] - m) + jnp.exp(ls[1] - m) + jnp.exp(ls[2] - m)
            lse = m + jnp.log(tot)
            ocs[:, cs] = sum(jnp.exp(lg - lse) * o[:, cs].astype(F32) for lg, o in zip(ls, (o0, o1, o2)))
            lcols.append(lse)
        lse_ref[...] = _put_cols(lcols)
        oc = ocs[...]
        oc_ref[...] = oc.astype(BF16)
        yc = (oc * _silu(cg_ref[...].astype(F32))).astype(BF16)
        yc_ref[...] = yc
        ys = (y_ref[0], y_ref[1], yc, y_ref[2])
        z = jnp.zeros((tm, D_MODEL), F32)
        for b in range(4):
            z = z + _sigmoid(gm[b][...].astype(F32)) * _dot(ys[b], wb_ref[b])
        zb = z.astype(BF16)
        z_ref[...] = zb
        xn_ref[...] = x_ref[...] + _dot(zb, wo_ref[...])

        if n:
            @pl.when(i == nsteps - 1)
            def _():
                local, out, incoming = _ag_first(s_in, s_out, *sems[:3])
                for a in range(n):
                    incoming[4 * a].wait_recv()
                _comm_wait(_ag_second(s_out, *sems[3:]))
                for cp in out:
                    cp.wait_send()
                for cp in local:
                    cp.wait()

    res = pl.pallas_call(
        body, name="merge_fwd_gather" if n else "merge_fwd",
        grid=(nsteps,),
        in_specs=[_rows(tm, D_MODEL), pl.BlockSpec((3, tm, WIDTH), lambda i: (0, i, 0)),
                  _rows(tm, WIDTH), _rows(tm, WIDTH), _rows(tm, WIDTH),
                  _rows(tm, 128), _rows(tm, 128), _rows(tm, 128),
                  _rows(tm, WIDTH, CB_CGATE)] + _gate_specs(tm)
                 + [_full((4, WIDTH, D_MODEL)), _full((D_MODEL, D_MODEL))] + [ANY] * n,
        out_specs=[_rows(tm, D_MODEL), pl.BlockSpec((None, tm, WIDTH), lambda i: (Y_SLOT[2], i, 0)),
                   _rows(tm, WIDTH), _rows(tm, 128), _rows(tm, D_MODEL)] + [ANY] * n,
        out_shape=[jax.ShapeDtypeStruct((S, D_MODEL), F32), jax.ShapeDtypeStruct(y4.shape, BF16),
                   jax.ShapeDtypeStruct((S, WIDTH), BF16), jax.ShapeDtypeStruct((S, 128), F32),
                   jax.ShapeDtypeStruct((S, D_MODEL), BF16)]
                  + [jax.ShapeDtypeStruct((N_DEV,) + s.shape, s.dtype) for s in shards],
        input_output_aliases={1: 1},
        scratch_shapes=[pltpu.VMEM((tm, WIDTH), F32)] + (_dma_sems(4 * n, 4 * n, n, 3 * n, 3 * n) if n else []),
        compiler_params=_params(("arbitrary",)),
    )(x, y4, *o_g, *l_g, proj, *([gates] * 4), wb, wout, *shards)
    return res[:5], list(res[5:])


def _loss_head(x, g, tgt):
    S = x.shape[0]
    tm = 512

    def body(x_ref, g_ref, t_ref, loss_ref, dx_ref, dg_ref):
        @pl.when(pl.program_id(0) == 0)
        def _():
            loss_ref[...] = jnp.zeros_like(loss_ref)
            dg_ref[...] = jnp.zeros_like(dg_ref)

        xf = x_ref[...]
        r = lax.rsqrt(jnp.mean(xf * xf, axis=-1, keepdims=True) + EPS)
        xhat = xf * r
        gv = g_ref[...]
        err = xhat * gv - t_ref[...]
        e2 = jnp.sum(err * err, axis=-1, keepdims=True)
        loss_ref[...] += (0.5 / D_MODEL) * jnp.sum(e2, axis=0, keepdims=True)
        dy = err * (1.0 / D_MODEL)
        dg_ref[...] += jnp.sum(dy * xhat, axis=0, keepdims=True)
        dxh = dy * gv
        dx_ref[...] = r * (dxh - xhat * jnp.mean(dxh * xhat, axis=-1, keepdims=True))

    return pl.pallas_call(
        body, name="loss_head",
        grid=(S // tm,),
        in_specs=[_rows(tm, D_MODEL), _full((1, D_MODEL)), _rows(tm, D_MODEL)],
        out_specs=[_full((1, 128)), _rows(tm, D_MODEL), _full((1, D_MODEL))],
        out_shape=[jax.ShapeDtypeStruct((1, 128), F32), jax.ShapeDtypeStruct((S, D_MODEL), F32),
                   jax.ShapeDtypeStruct((1, D_MODEL), F32)],
        compiler_params=_params(("arbitrary",)),
    )(x, g, tgt)


def _merge_bwd(dxo, y4, oc, z, proj, gates, wb, wout, grads=()):
    S = dxo.shape[0]
    tm = 256
    n = len(grads)
    nsteps = S // tm

    def body(dx_ref, y_ref, oc_ref, z_ref, cg_ref, *rest):
        gm = rest[:4]
        wb_ref, wo_ref = rest[4:6]
        g_in = rest[6:6 + n]
        dy_ref, doc_ref, delta_ref, dcg_ref, dgm_ref, dwb_ref, dwo_ref = rest[6 + n:13 + n]
        g_out = rest[13 + n:13 + 2 * n]
        acc_b, acc_o = rest[13 + 2 * n:15 + 2 * n]
        sems = rest[15 + 2 * n:]
        i = pl.program_id(0)

        @pl.when(i == 0)
        def _():
            acc_b[...] = jnp.zeros_like(acc_b)
            acc_o[...] = jnp.zeros_like(acc_o)
            if n:
                _comm_start(_rs_first(g_in, g_out, *sems))

        dxb = dx_ref[...].astype(BF16)
        acc_o[...] += _dot_tn(z_ref[...], dxb)
        dz = _dot_nt(dxb, wo_ref[...])
        for b in range(4):
            gate = _sigmoid(gm[b][...].astype(F32))
            yb = y_ref[Y_SLOT[b]]
            t = _dot(yb, wb_ref[b])
            dgm_ref[:, b * D_MODEL:(b + 1) * D_MODEL] = (dz * t * gate * (1.0 - gate)).astype(BF16)
            dt = (dz * gate).astype(BF16)
            acc_b[b] += _dot_tn(yb, dt)
            dyb = _dot_nt(dt, wb_ref[b])
            if b == 2:
                cg = cg_ref[...].astype(F32)
                oc = oc_ref[...].astype(F32)
                scg, dscg = _silu_and_grad(cg)
                doc = dyb * scg
                dcg_ref[...] = (dyb * oc * dscg).astype(BF16)
                doc_ref[...] = doc.astype(BF16)
                prod = doc * oc
                delta_ref[...] = _put_cols([jnp.sum(prod[:, h * HEAD:(h + 1) * HEAD], axis=1, keepdims=True)
                                            for h in range(N_HEAD)])
            else:
                dy_ref[b if b < 2 else 2] = dyb.astype(BF16)

        @pl.when(i == nsteps - 1)
        def _():
            dwb_ref[...] = acc_b[...].astype(BF16)
            dwo_ref[...] = acc_o[...].astype(BF16)
            if n:
                _comm_wait(_rs_first(g_in, g_out, *sems))

    def resident(shape):
        nd = len(shape)
        return pl.BlockSpec(shape, lambda i: (0,) * nd, pipeline_mode=pl.Buffered(1))

    res = pl.pallas_call(
        body, name="merge_bwd_scatter" if n else "merge_bwd",
        grid=(nsteps,),
        in_specs=[_rows(tm, D_MODEL), pl.BlockSpec((4, tm, WIDTH), lambda i: (0, i, 0)),
                  _rows(tm, WIDTH), _rows(tm, D_MODEL), _rows(tm, WIDTH, CB_CGATE)] + _gate_specs(tm)
                 + [resident((4, WIDTH, D_MODEL)), resident((D_MODEL, D_MODEL))] + [ANY] * n,
        out_specs=[pl.BlockSpec((3, tm, WIDTH), lambda i: (0, i, 0)), _rows(tm, WIDTH), _rows(tm, 128),
                   _rows(tm, WIDTH), _rows(tm, 4 * D_MODEL), _full((4, WIDTH, D_MODEL)), _full((D_MODEL, D_MODEL))]
                  + [ANY] * n,
        out_shape=[jax.ShapeDtypeStruct((3, S, WIDTH), BF16), jax.ShapeDtypeStruct((S, WIDTH), BF16),
                   jax.ShapeDtypeStruct((S, 128), F32), jax.ShapeDtypeStruct((S, WIDTH), BF16),
                   jax.ShapeDtypeStruct((S, 4 * D_MODEL), BF16), jax.ShapeDtypeStruct((4, WIDTH, D_MODEL), BF16),
                   jax.ShapeDtypeStruct((D_MODEL, D_MODEL), BF16)]
                  + [jax.ShapeDtypeStruct(g.shape[:1] + g.shape[2:], g.dtype) for g in grads],
        scratch_shapes=[pltpu.VMEM((4, WIDTH, D_MODEL), F32), pltpu.VMEM((D_MODEL, D_MODEL), F32)]
                       + (_dma_sems(N_CHIP * n, N_CHIP * n) if n else []),
        compiler_params=_params(("arbitrary",)),
    )(dxo, y4, oc, z, proj, *([gates] * 4), wb, wout, *grads)
    return res[:7], list(res[7:])


def _attn_bwd(q, qcb, k, kcb, v, vcb, do, lse, delta, bps):
    S = q.shape[0]
    tm = ATT_TILE
    nb = tm // CHUNK
    nblk = S // CHUNK

    ncur = nb * N_HEAD
    nprev = (nb + 1) * N_HEAD

    def body(q_ref, k_ref, v_ref, do_ref, l_ref, d_ref, kh_ref, vh_ref, qn_ref, don_ref, ln_ref, dn_ref,
             dq_ref, dk_ref, dv_ref, sc_s, sp_s, dpc_s, dpp_s, pc_s, pp_s, dsc_s, dsp_s):
        i = pl.program_id(0)

        def rows_of(n):
            if n < nb:
                rs = slice(n * CHUNK, (n + 1) * CHUNK)
                return rs, q_ref, do_ref, l_ref, d_ref
            return slice(0, CHUNK), qn_ref, don_ref, ln_ref, dn_ref

        def prev_kv(n, cs):
            if n == 0:
                return kh_ref[:, cs], vh_ref[:, cs]
            ps = slice((n - 1) * CHUNK, n * CHUNK)
            return k_ref[ps, cs], v_ref[ps, cs]

        def blk(n, h):
            return slice((n * N_HEAD + h) * CHUNK, (n * N_HEAD + h + 1) * CHUNK)

        pens, lses, deltas = [], [], []
        for n in range(nb + 1):
            rs, qr, dor, lr, dr = rows_of(n)
            gb = i * nb + n
            pen = jnp.where(gb % bps != 0, 0.0, NEG)
            if n == nb:
                pen = pen + jnp.where(gb < nblk, 0.0, NEG)
            pens.append(jnp.full((N_HEAD * CHUNK, 1), pen, F32))
            lblk, dblk = lr[rs, :], dr[rs, :]
            for h in range(N_HEAD):
                cs = slice(h * HEAD, (h + 1) * HEAD)
                qh, doh = qr[rs, cs], dor[rs, cs]
                lses.append(_col(lblk, h))
                deltas.append(_col(dblk, h))
                kp, vp = prev_kv(n, cs)
                sp_s[blk(n, h), :] = _dot_nt(qh, kp)
                dpp_s[blk(n, h), :] = _dot_nt(doh, vp)
                if n < nb:
                    sc_s[blk(n, h), :] = _dot_nt(qh, k_ref[rs, cs])
                    dpc_s[blk(n, h), :] = _dot_nt(doh, v_ref[rs, cs])
        lse = jnp.concatenate(lses, axis=0)
        delta = jnp.concatenate(deltas, axis=0)
        row = lax.broadcasted_iota(jnp.int32, (nprev * CHUNK, CHUNK), 0) & (CHUNK - 1)
        col = lax.broadcasted_iota(jnp.int32, (nprev * CHUNK, CHUNK), 1)
        sp = jnp.where(col >= row, sp_s[...] * ATT_SCALE, NEG) + jnp.concatenate(pens, axis=0)
        pp = jnp.exp(sp - lse)
        pp_s[...] = pp.astype(BF16)
        dsp_s[...] = (pp * (dpp_s[...] - delta)).astype(BF16)
        nc = ncur * CHUNK
        sc = jnp.where(col[:nc] <= row[:nc], sc_s[...] * ATT_SCALE, NEG)
        pc = jnp.exp(sc - lse[:nc])
        pc_s[...] = pc.astype(BF16)
        dsc_s[...] = (pc * (dpc_s[...] - delta[:nc])).astype(BF16)
        for n in range(nb):
            rs, qr, dor, _, _ = rows_of(n)
            rn, qnr, donr, _, _ = rows_of(n + 1)
            for h in range(N_HEAD):
                cs = slice(h * HEAD, (h + 1) * HEAD)
                kp, _ = prev_kv(n, cs)
                dq = _dot(dsc_s[blk(n, h), :], k_ref[rs, cs]) + _dot(dsp_s[blk(n, h), :], kp)
                dq_ref[rs, cs] = (dq * ATT_SCALE).astype(BF16)
                dk = _dot_tn(dsc_s[blk(n, h), :], qr[rs, cs]) + _dot_tn(dsp_s[blk(n + 1, h), :], qnr[rn, cs])
                dk_ref[rs, cs] = (dk * ATT_SCALE).astype(BF16)
                dv = _dot_tn(pc_s[blk(n, h), :], dor[rs, cs]) + _dot_tn(pp_s[blk(n + 1, h), :], donr[rn, cs])
                dv_ref[rs, cs] = dv.astype(BF16)

    def prev_halo(cb):
        return pl.BlockSpec((CHUNK, WIDTH), lambda i: (jnp.maximum(i * nb - 1, 0), cb))

    def next_halo(width, cb=0):
        return pl.BlockSpec((CHUNK, width), lambda i: (jnp.minimum(i * nb + nb, nblk - 1), cb))

    return pl.pallas_call(
        body, name=f"attn_bwd_{bps}",
        grid=(S // tm,),
        in_specs=[_rows(tm, WIDTH, qcb), _rows(tm, WIDTH, kcb), _rows(tm, WIDTH, vcb), _rows(tm, WIDTH),
                  _rows(tm, 128), _rows(tm, 128), prev_halo(kcb), prev_halo(vcb),
                  next_halo(WIDTH, qcb), next_halo(WIDTH), next_halo(128), next_halo(128)],
        out_specs=[_rows(tm, WIDTH), _rows(tm, WIDTH), _rows(tm, WIDTH)],
        out_shape=[jax.ShapeDtypeStruct((S, WIDTH), BF16)] * 3,
        scratch_shapes=[pltpu.VMEM((ncur * CHUNK, CHUNK), F32), pltpu.VMEM((nprev * CHUNK, CHUNK), F32),
                        pltpu.VMEM((ncur * CHUNK, CHUNK), F32), pltpu.VMEM((nprev * CHUNK, CHUNK), F32),
                        pltpu.VMEM((ncur * CHUNK, CHUNK), BF16), pltpu.VMEM((nprev * CHUNK, CHUNK), BF16),
                        pltpu.VMEM((ncur * CHUNK, CHUNK), BF16), pltpu.VMEM((nprev * CHUNK, CHUNK), BF16)],
        compiler_params=_params(("parallel",)),
    )(q, k, v, do, lse, delta, k, v, q, do, lse, delta)


def _dilated_split(d):
    hp = min(N_HEAD, 16 // d)
    return hp, N_HEAD // hp, HEAD * hp


def _by_class(src_ref, dst, d, hp):
    for j in range(hp):
        dst[j] = pltpu.einshape("(tr)l->(rt)l", src_ref[:, j * HEAD:(j + 1) * HEAD], r=d)


def _from_class(src, dst_ref, d, hp):
    for j in range(hp):
        dst_ref[:, j * HEAD:(j + 1) * HEAD] = pltpu.einshape("(rt)l->(tr)l", src[j].astype(BF16), r=d)


def _attn_fwd_dilated(proj, qcb, kcb, vcb, d):
    S = proj.shape[0]
    T = CHUNK * d
    hp, nh, cw = _dilated_split(d)
    nblocks = d * hp

    def body(q_ref, k_ref, v_ref, o_ref, l_ref, qf, kf, vf, kpf, vpf, kst, vst, of, lf, sc_s, sp_s, pc_s, pp_s):
        i, hh = pl.program_id(0), pl.program_id(1)
        _by_class(q_ref, qf, d, hp)
        _by_class(k_ref, kf, d, hp)
        _by_class(v_ref, vf, d, hp)

        @pl.when(i == 0)
        def _():
            kpf[...] = jnp.zeros_like(kpf)
            vpf[...] = jnp.zeros_like(vpf)

        @pl.when(i > 0)
        def _():
            kpf[...] = kst[hh]
            vpf[...] = vst[hh]

        def blk(ref, r, j):
            return ref[j, r * CHUNK:(r + 1) * CHUNK, :]

        def bs(r, j):
            return slice((r * hp + j) * CHUNK, (r * hp + j + 1) * CHUNK)

        for r in range(d):
            for j in range(hp):
                qb = blk(qf, r, j)
                sc_s[bs(r, j), :] = _dot_nt(qb, blk(kf, r, j))
                sp_s[bs(r, j), :] = _dot_nt(qb, blk(kpf, r, j))
        row = lax.broadcasted_iota(jnp.int32, (nblocks * CHUNK, CHUNK), 0) & (CHUNK - 1)
        col = lax.broadcasted_iota(jnp.int32, (nblocks * CHUNK, CHUNK), 1)
        sc = jnp.where(col <= row, sc_s[...] * ATT_SCALE, NEG)
        sp = jnp.where(col >= row, sp_s[...] * ATT_SCALE, NEG) + jnp.where(i > 0, 0.0, NEG)
        m = jnp.maximum(jnp.max(sc, axis=-1, keepdims=True), jnp.max(sp, axis=-1, keepdims=True))
        ec = jnp.exp(sc - m)
        ep = jnp.exp(sp - m)
        den = jnp.sum(ec, axis=-1, keepdims=True) + jnp.sum(ep, axis=-1, keepdims=True)
        inv = 1.0 / den
        pc_s[...] = (ec * inv).astype(BF16)
        pp_s[...] = (ep * inv).astype(BF16)
        lse = m + jnp.log(den)
        lane = lax.broadcasted_iota(jnp.int32, (CHUNK, 128), 1)
        for r in range(d):
            lblk = jnp.zeros((CHUNK, 128), F32)
            for j in range(hp):
                o = _dot(pc_s[bs(r, j), :], blk(vf, r, j)) + _dot(pp_s[bs(r, j), :], blk(vpf, r, j))
                of[j, r * CHUNK:(r + 1) * CHUNK, :] = o
                lblk = jnp.where(lane == hh * hp + j, lse[bs(r, j)], lblk)
            lf[r * CHUNK:(r + 1) * CHUNK, :] = lblk
        _from_class(of, o_ref, d, hp)
        lnat = pltpu.einshape("(rt)l->(tr)l", lf[...], r=d)

        @pl.when(hh == 0)
        def _():
            l_ref[...] = lnat

        @pl.when(hh > 0)
        def _():
            l_ref[...] += lnat

        kst[hh] = kf[...]
        vst[hh] = vf[...]

    def cols(cb):
        return pl.BlockSpec((T, cw), lambda i, hh: (i, cb * nh + hh))

    tile = pltpu.VMEM((hp, T, HEAD), BF16)
    return pl.pallas_call(
        body, name=f"attn_fwd_dilated_{d}",
        grid=(S // T, nh),
        in_specs=[cols(qcb), cols(kcb), cols(vcb)],
        out_specs=[cols(0), pl.BlockSpec((T, 128), lambda i, hh: (i, 0))],
        out_shape=[jax.ShapeDtypeStruct((S, WIDTH), BF16), jax.ShapeDtypeStruct((S, 128), F32)],
        scratch_shapes=[tile, tile, tile, tile, tile,
                        pltpu.VMEM((nh, hp, T, HEAD), BF16), pltpu.VMEM((nh, hp, T, HEAD), BF16),
                        pltpu.VMEM((hp, T, HEAD), F32), pltpu.VMEM((T, 128), F32),
                        pltpu.VMEM((nblocks * CHUNK, CHUNK), F32), pltpu.VMEM((nblocks * CHUNK, CHUNK), F32),
                        pltpu.VMEM((nblocks * CHUNK, CHUNK), BF16), pltpu.VMEM((nblocks * CHUNK, CHUNK), BF16)],
        compiler_params=_params(("arbitrary", "arbitrary")),
    )(proj, proj, proj)


def _attn_bwd_dilated(proj, qcb, kcb, vcb, do, lse, delta, d):
    S = proj.shape[0]
    T = CHUNK * d
    nt = S // T
    hp, nh, cw = _dilated_split(d)
    nblocks = d * hp

    def body(q_ref, k_ref, v_ref, do_ref, l_ref, d_ref, dq_ref, dk_ref, dv_ref,
             qf, dof, kf, vf, kpf, vpf, dqf, acck, accv, newk, newv,
             sc_s, sp_s, dpc_s, dpp_s, pc_s, pp_s, dsc_s, dsp_s):
        hh, i = pl.program_id(0), pl.program_id(1)

        @pl.when(i == 0)
        def _():
            for ref in (kpf, vpf, acck, accv):
                ref[...] = jnp.zeros_like(ref)
            dk_ref[...] = jnp.zeros_like(dk_ref)
            dv_ref[...] = jnp.zeros_like(dv_ref)

        def blk(ref, r, j):
            return ref[j, r * CHUNK:(r + 1) * CHUNK, :]

        def bs(r, j):
            return slice((r * hp + j) * CHUNK, (r * hp + j + 1) * CHUNK)

        @pl.when(i < nt)
        def _():
            _by_class(q_ref, qf, d, hp)
            _by_class(do_ref, dof, d, hp)
            _by_class(k_ref, kf, d, hp)
            _by_class(v_ref, vf, d, hp)
            lses, deltas = [], []
            lcls = pltpu.einshape("(tr)l->(rt)l", l_ref[...], r=d)
            dcls = pltpu.einshape("(tr)l->(rt)l", d_ref[...], r=d)
            for r in range(d):
                lblk = lcls[r * CHUNK:(r + 1) * CHUNK]
                dblk = dcls[r * CHUNK:(r + 1) * CHUNK]
                for j in range(hp):
                    lses.append(_col(lblk, hh * hp + j))
                    deltas.append(_col(dblk, hh * hp + j))
                    qb, dob = blk(qf, r, j), blk(dof, r, j)
                    sc_s[bs(r, j), :] = _dot_nt(qb, blk(kf, r, j))
                    dpc_s[bs(r, j), :] = _dot_nt(dob, blk(vf, r, j))
                    sp_s[bs(r, j), :] = _dot_nt(qb, blk(kpf, r, j))
                    dpp_s[bs(r, j), :] = _dot_nt(dob, blk(vpf, r, j))
            lse = jnp.concatenate(lses, axis=0)
            delta = jnp.concatenate(deltas, axis=0)
            row = lax.broadcasted_iota(jnp.int32, (nblocks * CHUNK, CHUNK), 0) & (CHUNK - 1)
            col = lax.broadcasted_iota(jnp.int32, (nblocks * CHUNK, CHUNK), 1)
            sp = jnp.where(col >= row, sp_s[...] * ATT_SCALE, NEG) + jnp.where(i > 0, 0.0, NEG)
            pp = jnp.exp(sp - lse)
            pp_s[...] = pp.astype(BF16)
            dsp_s[...] = (pp * (dpp_s[...] - delta)).astype(BF16)
            sc = jnp.where(col <= row, sc_s[...] * ATT_SCALE, NEG)
            pc = jnp.exp(sc - lse)
            pc_s[...] = pc.astype(BF16)
            dsc_s[...] = (pc * (dpc_s[...] - delta)).astype(BF16)
            for r in range(d):
                rows = slice(r * CHUNK, (r + 1) * CHUNK)
                for j in range(hp):
                    qb, dob = blk(qf, r, j), blk(dof, r, j)
                    dsc, dsp = dsc_s[bs(r, j), :], dsp_s[bs(r, j), :]
                    dqf[j, rows, :] = (_dot(dsc, blk(kf, r, j)) + _dot(dsp, blk(kpf, r, j))) * ATT_SCALE
                    newk[j, rows, :] = _dot_tn(dsc, qb) * ATT_SCALE
                    newv[j, rows, :] = _dot_tn(pc_s[bs(r, j), :], dob)
                    acck[j, rows, :] += _dot_tn(dsp, qb) * ATT_SCALE
                    accv[j, rows, :] += _dot_tn(pp_s[bs(r, j), :], dob)
            _from_class(dqf, dq_ref, d, hp)

        @pl.when(i > 0)
        def _():
            _from_class(acck, dk_ref, d, hp)
            _from_class(accv, dv_ref, d, hp)

        @pl.when(i < nt)
        def _():
            acck[...] = newk[...]
            accv[...] = newv[...]
            kpf[...] = kf[...]
            vpf[...] = vf[...]

    def cur(width, cb, nsplit):
        return pl.BlockSpec((T, width), lambda hh, i: (jnp.minimum(i, nt - 1), cb * nsplit + hh * (nsplit > 1)))

    def lag():
        return pl.BlockSpec((T, cw), lambda hh, i: (jnp.maximum(i - 1, 0), hh))

    tile = pltpu.VMEM((hp, T, HEAD), BF16)
    acc = pltpu.VMEM((hp, T, HEAD), F32)
    f32s = pltpu.VMEM((nblocks * CHUNK, CHUNK), F32)
    b16s = pltpu.VMEM((nblocks * CHUNK, CHUNK), BF16)
    return pl.pallas_call(
        body, name=f"attn_bwd_dilated_{d}",
        grid=(nh, nt + 1),
        in_specs=[cur(cw, qcb, nh), cur(cw, kcb, nh), cur(cw, vcb, nh), cur(cw, 0, nh), cur(128, 0, 1), cur(128, 0, 1)],
        out_specs=[cur(cw, 0, nh), lag(), lag()],
        out_shape=[jax.ShapeDtypeStruct((S, WIDTH), BF16)] * 3,
        scratch_shapes=[tile] * 6 + [acc] * 5 + [f32s] * 4 + [b16s] * 4,
        compiler_params=_params(("arbitrary", "arbitrary")),
    )(proj, proj, proj, do, lse, delta)


def _abm_bwd(proj, dy3, ln_g, ln_b, wsm, wsm_t, bias_full, pool_w, pool_wt, pool_scale, kv):
    S = proj.shape[0]
    tm = 512
    nchunk = tm // CHUNK
    nblk = S // CHUNK

    def body(u_ref, v_ref, ag_ref, p_ref, ph_ref, pg_ref, pgn_ref, mq_ref, mg_ref, dy_ref, dypn_ref,
             lng_ref, lnb_ref, wsm_ref, wsmt_ref, bias_ref, pw_ref, pwt_ref, ps_ref, kv_ref,
             dab_ref, dm_ref, dlng_ref, dlnb_ref, dws_ref, dbias_ref, dpw_ref, dps_ref, dkv_ref,
             mix, dvl, ddn):
        i = pl.program_id(0)

        @pl.when(i == 0)
        def _():
            for r in (dlng_ref, dlnb_ref, dws_ref, dbias_ref, dpw_ref, dps_ref, dkv_ref):
                r[...] = jnp.zeros_like(r)

        au = u_ref[...].astype(F32)
        av = v_ref[...].astype(F32)
        ag = ag_ref[...].astype(F32)
        u, du = _gelu_and_grad(au)
        v, dgelu_v = _gelu_and_grad(av)
        vhat, rstd = _layer_norm_fwd(v)
        vln = (vhat * lng_ref[...] + lnb_ref[...]).astype(BF16)
        for c in range(nchunk):
            for h in range(N_HEAD):
                rs, cs = slice(c * CHUNK, (c + 1) * CHUNK), slice(h * HEAD, (h + 1) * HEAD)
                mix[rs, cs] = _dot(wsm_ref[h], vln[rs, cs]) + bias_ref[:, cs]
        dya = dy_ref[0].astype(F32)
        sg, dsg = _silu_and_grad(ag)
        mixed = mix[...]
        dab_ref[:, 2 * WIDTH:3 * WIDTH] = (dya * u * mixed * dsg).astype(BF16)
        dab_ref[:, 0:WIDTH] = (dya * mixed * sg * du).astype(BF16)
        dmixed = dya * u * sg
        dmb = dmixed.astype(BF16)
        tril = (lax.broadcasted_iota(jnp.int32, (CHUNK, CHUNK), 1)
                <= lax.broadcasted_iota(jnp.int32, (CHUNK, CHUNK), 0))
        for c in range(nchunk):
            rs = slice(c * CHUNK, (c + 1) * CHUNK)
            dbias_ref[...] += dmixed[rs, :]
            for h in range(N_HEAD):
                cs = slice(h * HEAD, (h + 1) * HEAD)
                dvl[rs, cs] = _dot(wsmt_ref[h], dmb[rs, cs])
                dws_ref[h] += jnp.where(tril, _dot_nt(dmb[rs, cs], vln[rs, cs]), 0.0)
        dvln = dvl[...]
        dlng_ref[...] += jnp.sum(dvln * vhat, axis=0, keepdims=True)
        dlnb_ref[...] += jnp.sum(dvln, axis=0, keepdims=True)
        dvh = dvln * lng_ref[...]
        dv = rstd * (dvh - jnp.mean(dvh, axis=-1, keepdims=True)
                     - vhat * jnp.mean(dvh * vhat, axis=-1, keepdims=True))
        dab_ref[:, WIDTH:2 * WIDTH] = (dv * dgelu_v).astype(BF16)

        halo_ok = (i > 0).astype(F32)
        for c in range(nchunk):
            rs = slice(c * CHUNK, (c + 1) * CHUNK)
            for g, win in enumerate(POOL_WINDOWS):
                cs = slice(g * HEAD, (g + 1) * HEAD)
                bcur, bprev = _band_masks(win)
                cur = p_ref[rs, cs]
                if c == 0:
                    prev = (ph_ref[:, cs].astype(F32) * halo_ok).astype(BF16)
                else:
                    prev = p_ref[(c - 1) * CHUNK:c * CHUNK, cs]
                sums = _dot(bcur, cur) + _dot(bprev, prev)
                dvl[rs, cs] = sums * _inv_count(i * tm + c * CHUNK, win) - cur.astype(F32)
        dmat = dvl[...].astype(BF16)
        for g in range(4):
            cs = slice(g * HEAD, (g + 1) * HEAD)
            mix[:, cs] = _dot(dmat[:, cs], pw_ref[g])
        yg = mix[...]
        pg = pg_ref[...].astype(F32)
        dyp = dy_ref[1].astype(F32)
        spg, dspg = _silu_and_grad(pg)
        dyy = dyp * spg
        scale = ps_ref[...]
        dab_ref[:, 4 * WIDTH:5 * WIDTH] = (dyp * yg * scale * dspg).astype(BF16)
        dps_ref[...] += jnp.sum(dyy * yg, axis=0, keepdims=True)
        dyg = (dyy * scale).astype(BF16)
        for g in range(4):
            cs = slice(g * HEAD, (g + 1) * HEAD)
            dpw_ref[g] += _dot_tn(dmat[:, cs], dyg[:, cs])
            mix[:, cs] = _dot(dyg[:, cs], pwt_ref[g])
        next_ok = (i + 1 < S // tm).astype(F32)
        dygn = (dypn_ref[...].astype(F32) * _silu(pgn_ref[...].astype(F32)) * scale * next_ok).astype(BF16)
        for c in range(nchunk + 1):
            for g, win in enumerate(POOL_WINDOWS):
                cs = slice(g * HEAD, (g + 1) * HEAD)
                if c < nchunk:
                    dd = mix[c * CHUNK:(c + 1) * CHUNK, cs]
                else:
                    dd = _dot(dygn[:, cs], pwt_ref[g])
                ddn[c * CHUNK:(c + 1) * CHUNK, cs] = dd * _inv_count(i * tm + c * CHUNK, win)
        ddnb = ddn[...].astype(BF16)
        for c in range(nchunk):
            rs = slice(c * CHUNK, (c + 1) * CHUNK)
            ns = slice((c + 1) * CHUNK, (c + 2) * CHUNK)
            for g, win in enumerate(POOL_WINDOWS):
                cs = slice(g * HEAD, (g + 1) * HEAD)
                bcur, bprev = _band_masks(win)
                dp = _dot_tn(bcur, ddnb[rs, cs]) + _dot_tn(bprev, ddnb[ns, cs]) - mix[rs, cs]
                dab_ref[rs, 3 * WIDTH + g * HEAD:3 * WIDTH + (g + 1) * HEAD] = dp.astype(BF16)

        mg = mg_ref[...].astype(F32)
        dym = dy_ref[2].astype(F32)
        smg, dsmg = _silu_and_grad(mg)
        dob = (dym * smg).astype(BF16)
        for h in range(N_HEAD):
            cs = slice(h * HEAD, (h + 1) * HEAD)
            vs = slice(WIDTH + h * HEAD, WIDTH + (h + 1) * HEAD)
            qh = mq_ref[:, cs]
            p = _mem_softmax(qh, kv_ref[:, cs])
            pb = p.astype(BF16)
            mix[:, cs] = _dot(pb, kv_ref[:, vs])
            dp = _dot_nt(dob[:, cs], kv_ref[:, vs])
            ds = (p * (dp - jnp.sum(p * dp, axis=-1, keepdims=True))).astype(BF16)
            dm_ref[:, cs] = (_dot(ds, kv_ref[:, cs]) * ATT_SCALE).astype(BF16)
            dkv_ref[:, cs] += _dot_tn(ds, qh) * ATT_SCALE
            dkv_ref[:, vs] += _dot_tn(pb, dob[:, cs])
        dm_ref[:, WIDTH:2 * WIDTH] = (dym * mix[...] * dsmg).astype(BF16)

    blk = tm // CHUNK
    small = [_full((1, WIDTH)), _full((1, WIDTH)), _full((N_HEAD, CHUNK, CHUNK)), _full((CHUNK, WIDTH)),
             _full((4, HEAD, HEAD)), _full((1, WIDTH)), _full((MEM_LEN, 2 * WIDTH))]
    return pl.pallas_call(
        body, name="abm_bwd",
        grid=(S // tm,),
        in_specs=[_rows(tm, WIDTH, CB_U), _rows(tm, WIDTH, CB_V), _rows(tm, WIDTH, CB_AGATE),
                  _rows(tm, WIDTH, CB_PIN),
                  pl.BlockSpec((CHUNK, WIDTH), lambda i: (jnp.maximum(i * blk - 1, 0), CB_PIN)),
                  _rows(tm, WIDTH, CB_PGATE),
                  pl.BlockSpec((CHUNK, WIDTH), lambda i: (jnp.minimum(i * blk + blk, nblk - 1), CB_PGATE)),
                  _rows(tm, WIDTH, CB_MQ), _rows(tm, WIDTH, CB_MGATE),
                  pl.BlockSpec((3, tm, WIDTH), lambda i: (0, i, 0)),
                  pl.BlockSpec((None, CHUNK, WIDTH), lambda i: (1, jnp.minimum(i * blk + blk, nblk - 1), 0)),
                  _full((1, WIDTH)), _full((1, WIDTH)), _full((N_HEAD, CHUNK, CHUNK)), _full((N_HEAD, CHUNK, CHUNK)),
                  _full((CHUNK, WIDTH)), _full((4, HEAD, HEAD)), _full((4, HEAD, HEAD)), _full((1, WIDTH)),
                  _full((MEM_LEN, 2 * WIDTH))],
        out_specs=[_rows(tm, 5 * WIDTH), _rows(tm, 2 * WIDTH)] + small,
        out_shape=[jax.ShapeDtypeStruct((S, D_BRANCHES), BF16), jax.ShapeDtypeStruct((S, 2 * WIDTH), BF16),
                   jax.ShapeDtypeStruct((1, WIDTH), F32), jax.ShapeDtypeStruct((1, WIDTH), F32),
                   jax.ShapeDtypeStruct((N_HEAD, CHUNK, CHUNK), F32), jax.ShapeDtypeStruct((CHUNK, WIDTH), F32),
                   jax.ShapeDtypeStruct((4, HEAD, HEAD), F32), jax.ShapeDtypeStruct((1, WIDTH), F32),
                   jax.ShapeDtypeStruct((MEM_LEN, 2 * WIDTH), F32)],
        scratch_shapes=[pltpu.VMEM((tm, WIDTH), F32), pltpu.VMEM((tm, WIDTH), F32),
                        pltpu.VMEM((tm + CHUNK, WIDTH), F32)],
        compiler_params=_params(("arbitrary",)),
    )(proj, proj, proj, proj, proj, proj, proj, proj, proj, dy3, dy3,
      ln_g, ln_b, wsm, wsm_t, bias_full, pool_w, pool_wt, pool_scale, kv)


def _bias_reduce(dbias_full):
    def body(d_ref, o_ref):
        d = d_ref[...]
        o_ref[...] = _put_cols([jnp.sum(d[:, h * HEAD:(h + 1) * HEAD], axis=1, keepdims=True) for h in range(N_HEAD)])

    return pl.pallas_call(body, name="bias_reduce", out_shape=jax.ShapeDtypeStruct((CHUNK, 128), F32))(dbias_full)


def _mem_bwd(mem, g, mem_n, w, dkv):
    def body(m_ref, g_ref, mn_ref, w_ref, dkv_ref, dw_ref, dg_ref):
        dkvb = dkv_ref[...].astype(BF16)
        dw_ref[...] = _dot_tn(mn_ref[...], dkvb).astype(BF16)
        dmn = _dot_nt(dkvb, w_ref[...])
        xf = m_ref[...]
        r = lax.rsqrt(jnp.mean(xf * xf, axis=-1, keepdims=True) + EPS)
        dg_ref[...] = jnp.sum(dmn * xf * r, axis=0, keepdims=True)

    return pl.pallas_call(
        body, name="mem_bwd",
        out_shape=[jax.ShapeDtypeStruct((D_MODEL, 2 * WIDTH), BF16), jax.ShapeDtypeStruct((1, D_MODEL), F32)],
        compiler_params=pltpu.CompilerParams(vmem_limit_bytes=VMEM_LIMIT),
    )(mem, g, mem_n, w, dkv)


def _dh_bwd(dpb, dpg, wt, wgt, x, g, dxo, parts=()):
    S = x.shape[0]
    tm, tkb, tkg = 1024, D_BRANCHES // 4, D_GATES // 4
    nkb, nkg = 4, 4
    nk = nkb + nkg
    ni = S // tm
    n = len(parts)

    def body(dpb_ref, wbr_ref, dpg_ref, wg_ref, x_ref, g_ref, dxo_ref, *rest):
        p_in = rest[:n]
        dx_ref, dg_ref = rest[n:n + 2]
        p_out, acc, sems = rest[n + 2:2 * n + 2], rest[2 * n + 2], rest[2 * n + 3:]
        i, kk = pl.program_id(0), pl.program_id(1)

        @pl.when(jnp.logical_and(i == 0, kk == 0))
        def _():
            dg_ref[...] = jnp.zeros_like(dg_ref)
            if n:
                _comm_start(_rs_second(p_in, p_out, *sems))

        @pl.when(kk == 0)
        def _():
            acc[...] = jnp.zeros_like(acc)

        @pl.when(kk < nkb)
        def _():
            acc[...] += _dot(dpb_ref[...], wbr_ref[...])

        @pl.when(kk >= nkb)
        def _():
            acc[...] += _dot(dpg_ref[...], wg_ref[...])

        @pl.when(kk == nk - 1)
        def _():
            xf = x_ref[...]
            r = lax.rsqrt(jnp.mean(xf * xf, axis=-1, keepdims=True) + EPS)
            xhat = xf * r
            dh = acc[...]
            dg_ref[...] += jnp.sum(dh * xhat, axis=0, keepdims=True)
            dxh = dh * g_ref[...]
            dx_ref[...] = dxo_ref[...] + r * (dxh - xhat * jnp.mean(dxh * xhat, axis=-1, keepdims=True))

        if n:
            @pl.when(jnp.logical_and(i == ni - 1, kk == nk - 1))
            def _():
                _comm_wait(_rs_second(p_in, p_out, *sems))

    res = pl.pallas_call(
        body, name="dh_bwd_scatter" if n else "dh_bwd",
        grid=(ni, nk),
        in_specs=[pl.BlockSpec((tm, tkb), lambda i, k: (i, jnp.minimum(k, nkb - 1))),
                  pl.BlockSpec((tkb, D_MODEL), lambda i, k: (jnp.minimum(k, nkb - 1), 0)),
                  pl.BlockSpec((tm, tkg), lambda i, k: (i, jnp.maximum(k - nkb, 0))),
                  pl.BlockSpec((tkg, D_MODEL), lambda i, k: (jnp.maximum(k - nkb, 0), 0)),
                  pl.BlockSpec((tm, D_MODEL), lambda i, k: (i, 0)), pl.BlockSpec((1, D_MODEL), lambda i, k: (0, 0)),
                  pl.BlockSpec((tm, D_MODEL), lambda i, k: (i, 0))] + [ANY] * n,
        out_specs=[pl.BlockSpec((tm, D_MODEL), lambda i, k: (i, 0)), pl.BlockSpec((1, D_MODEL), lambda i, k: (0, 0))]
                  + [ANY] * n,
        out_shape=[jax.ShapeDtypeStruct((S, D_MODEL), F32), jax.ShapeDtypeStruct((1, D_MODEL), F32)]
                  + [jax.ShapeDtypeStruct(p.shape, p.dtype) for p in parts],
        scratch_shapes=[pltpu.VMEM((tm, D_MODEL), F32)] + (_dma_sems(3 * n, 3 * n, n) if n else []),
        compiler_params=_params(("arbitrary", "arbitrary")),
    )(dpb, wt, dpg, wgt, x, g, dxo, *parts)
    return res[0], res[1], list(res[2:])


def _dw_in(h, dpb, dpg, parts=()):
    S = h.shape[0]
    tk = 1024
    nk = S // tk
    n = len(parts)
    tmb = D_BRANCHES // 4
    ng = D_GATES // GATE_TILE

    def accumulate(a_ref, h_ref, o_ref, acc):
        kk = pl.program_id(1)

        @pl.when(kk == 0)
        def _():
            acc[...] = jnp.zeros_like(acc)

        acc[...] += _dot_tn(a_ref[...], h_ref[...])

        @pl.when(kk == nk - 1)
        def _():
            o_ref[...] = acc[...].astype(BF16)

    def branches(a_ref, h_ref, *rest):
        p_in, o_ref, p_out = rest[:n], rest[n], rest[n + 1:2 * n + 1]
        acc, sems = rest[2 * n + 1], rest[2 * n + 2:]
        i, kk = pl.program_id(0), pl.program_id(1)

        if n:
            @pl.when(jnp.logical_and(i == 0, kk == 0))
            def _():
                _comm_start(_rs_second(p_in, p_out, *sems))

        accumulate(a_ref, h_ref, o_ref, acc)

        if n:
            @pl.when(jnp.logical_and(i == 3, kk == nk - 1))
            def _():
                _comm_wait(_rs_second(p_in, p_out, *sems))

    def gates(a_ref, h_ref, dst_ref, o_ref, acc):
        accumulate(a_ref, h_ref, o_ref, acc)

    res = pl.pallas_call(
        branches, name="dw_in_branches_scatter" if n else "dw_in_branches",
        grid=(4, nk),
        in_specs=[pl.BlockSpec((tk, tmb), lambda i, k: (k, i)), pl.BlockSpec((tk, D_MODEL), lambda i, k: (k, 0))]
                 + [ANY] * n,
        out_specs=[pl.BlockSpec((tmb, D_MODEL), lambda i, k: (i, 0))] + [ANY] * n,
        out_shape=[jax.ShapeDtypeStruct((D_IN, D_MODEL), BF16)]
                  + [jax.ShapeDtypeStruct(p.shape, p.dtype) for p in parts],
        scratch_shapes=[pltpu.VMEM((tmb, D_MODEL), F32)] + (_dma_sems(3 * n, 3 * n, n) if n else []),
        compiler_params=_params(("arbitrary", "arbitrary")),
    )(dpb, h, *parts)
    dwt = pl.pallas_call(
        gates, name="dw_in_gates",
        grid=(ng, nk),
        in_specs=[pl.BlockSpec((tk, GATE_TILE), lambda i, k: (k, i)), pl.BlockSpec((tk, D_MODEL), lambda i, k: (k, 0)),
                  ANY],
        out_specs=pl.BlockSpec((GATE_TILE, D_MODEL), lambda i, k: (D_BRANCHES // GATE_TILE + i, 0)),
        out_shape=jax.ShapeDtypeStruct((D_IN, D_MODEL), BF16),
        input_output_aliases={2: 0},
        scratch_shapes=[pltpu.VMEM((GATE_TILE, D_MODEL), F32)],
        compiler_params=_params(("parallel", "arbitrary")),
    )(dpg, h, res[0])
    return dwt, list(res[1:])


def _row_tile(R, C, block_bytes=2 << 20):
    for cand in range(min(R, block_bytes // (C * 4)) // 8 * 8, 0, -8):
        if R % cand == 0:
            return cand
    return R


def _adamw_update(p_ref, w_ref, m_ref, v_ref, g_ref, d_ref, nm_ref, nv_ref):
    c1 = 1.0 / (1.0 - ADAM_B1 ** ADAM_STEP)
    c2 = 1.0 / (1.0 - ADAM_B2 ** ADAM_STEP)
    g = p_ref[0].astype(F32)
    for k in range(1, p_ref.shape[0]):
        g = g + p_ref[k].astype(F32)
    nm = ADAM_B1 * m_ref[...] + (1.0 - ADAM_B1) * g
    nv = ADAM_B2 * v_ref[...] + (1.0 - ADAM_B2) * (g * g)
    g_ref[...] = g
    nm_ref[...] = nm
    nv_ref[...] = nv
    d_ref[...] = -ADAM_LR * ((nm * c1) / (jnp.sqrt(nv * c2) + ADAM_EPS) + ADAM_WD * w_ref[...])


def _adamw(parts, w, m, v, name):
    P, R, C = parts.shape
    tr = _row_tile(R, C)

    def body(*refs):
        _adamw_update(*refs)

    spec = pl.BlockSpec((tr, C), lambda i: (i, 0))
    return pl.pallas_call(
        body, name=name,
        grid=(R // tr,),
        in_specs=[pl.BlockSpec((P, tr, C), lambda i: (0, i, 0)), spec, spec, spec],
        out_specs=[spec] * 4,
        out_shape=[jax.ShapeDtypeStruct((R, C), F32)] * 4,
        compiler_params=_params(("parallel",)),
    )(parts, w, m, v)


def _adamw_layers(parts, w, m, v, name):
    depth = len(parts)
    P, R, C = parts[0].shape
    tr = _row_tile(R, C, 1 << 20)

    def body(*refs):
        layer = pl.program_id(0)
        for k in range(depth):
            @pl.when(layer == k)
            def _(k=k):
                _adamw_update(refs[k], *refs[depth:])

    def part_spec(k):
        return pl.BlockSpec((P, tr, C), lambda l, i: (0, jnp.where(l == k, i, 0), 0))

    spec = pl.BlockSpec((None, tr, C), lambda l, i: (l, i, 0))
    return pl.pallas_call(
        body, name=name,
        grid=(depth, R // tr),
        in_specs=[part_spec(k) for k in range(depth)] + [spec] * 3,
        out_specs=[spec] * 4,
        out_shape=[jax.ShapeDtypeStruct((depth, R, C), F32)] * 4,
        compiler_params=_params(("arbitrary", "arbitrary")),
    )(*parts, w, m, v)


def _place():
    return lax.axis_index("x"), lax.axis_index("y"), lax.axis_index("c")


def _all_gather(shards):
    n = len(shards)

    def body(*refs):
        ins, outs = refs[:n], refs[n:2 * n]
        send1, recv1, local_sems, send2, recv2 = refs[2 * n:]
        first = _ag_first(ins, outs, send1, recv1, local_sems)
        second = _ag_second(outs, send2, recv2)
        _comm_start(first)
        for j in range(3):
            for a in range(n):
                first[2][4 * a + 1 + j].wait_recv()
            for a in range(n):
                second[1][3 * a + j].start()
        for a in range(n):
            first[2][4 * a].wait_recv()
        for cp in second[2]:
            cp.wait_recv()
        for cp in first[1] + second[1]:
            cp.wait_send()
        for cp in first[0]:
            cp.wait()

    return pl.pallas_call(
        body, name="weights_all_gather",
        in_specs=[ANY] * n, out_specs=[ANY] * n,
        out_shape=[jax.ShapeDtypeStruct((N_DEV,) + s.shape, s.dtype) for s in shards],
        scratch_shapes=_dma_sems(4 * n, 4 * n, n, 3 * n, 3 * n),
        compiler_params=pltpu.CompilerParams(has_side_effects=True),
    )(*shards)


N_BIG = 4


def _dev(p):
    return 4 * p[0] + 2 * p[1] + p[2]


def _other_chips(x, y):
    return [(1 - x, y), (x, 1 - y), (1 - x, 1 - y)]


def _remote(src, dst, send_sems, recv_sems, k, to):
    return pltpu.make_async_remote_copy(src_ref=src, dst_ref=dst, send_sem=send_sems.at[k], recv_sem=recv_sems.at[k],
                                        device_id=to, device_id_type=MESH)


def _ag_first(ins, outs, send_sems, recv_sems, local_sems):
    x, y, c = _place()
    me = (x, y, c)
    targets = [(x, y, 1 - c)] + [(*chip, c) for chip in _other_chips(x, y)]
    local, out, inc = [], [], []
    for a in range(len(ins)):
        local.append(pltpu.make_async_copy(ins[a], outs[a].at[_dev(me)], local_sems.at[a]))
        for k, to in enumerate(targets):
            out.append(_remote(ins[a], outs[a].at[_dev(me)], send_sems, recv_sems, 4 * a + k, to))
            inc.append(_remote(ins[a], outs[a].at[_dev(to)], send_sems, recv_sems, 4 * a + k, to))
    return local, out, inc


def _ag_second(bufs, send_sems, recv_sems):
    x, y, c = _place()
    out, inc = [], []
    for a in range(len(bufs)):
        for j, chip in enumerate(_other_chips(x, y)):
            mine, theirs = bufs[a].at[_dev((*chip, c))], bufs[a].at[_dev((*chip, 1 - c))]
            out.append(_remote(mine, mine, send_sems, recv_sems, 3 * a + j, (x, y, 1 - c)))
            inc.append(_remote(theirs, theirs, send_sems, recv_sems, 3 * a + j, (x, y, 1 - c)))
    return [], out, inc


def _rs_first(ins, outs, send_sems, recv_sems):
    x, y, c = _place()
    out = [_remote(ins[a].at[j, 1 - c], outs[a].at[j], send_sems, recv_sems, N_CHIP * a + j, (x, y, 1 - c))
           for a in range(len(ins)) for j in range(N_CHIP)]
    return [], out, out


def _rs_second(ins, outs, send_sems, recv_sems, local_sems):
    x, y, c = _place()
    my_chip = 2 * x + y
    local, out, inc = [], [], []
    for a in range(len(ins)):
        local.append(pltpu.make_async_copy(ins[a].at[my_chip], outs[a].at[my_chip], local_sems.at[a]))
        for k, (ox, oy) in enumerate(_other_chips(x, y)):
            out.append(_remote(ins[a].at[2 * ox + oy], outs[a].at[my_chip], send_sems, recv_sems, 3 * a + k, (ox, oy, c)))
            inc.append(_remote(ins[a].at[2 * ox + oy], outs[a].at[2 * ox + oy], send_sems, recv_sems, 3 * a + k,
                               (ox, oy, c)))
    return local, out, inc


def _comm_start(exchange):
    local, out, _ = exchange
    for cp in local + out:
        cp.start()


def _comm_wait(exchange):
    local, out, inc = exchange
    for cp in inc:
        cp.wait_recv()
    for cp in out:
        cp.wait_send()
    for cp in local:
        cp.wait()


def _dma_sems(*counts):
    return [pltpu.SemaphoreType.DMA((n,)) for n in counts]


def _rs_sibling(grads):
    n = len(grads)

    def body(*refs):
        ex = _rs_first(refs[:n], refs[n:2 * n], *refs[2 * n:])
        _comm_start(ex)
        _comm_wait(ex)

    return pl.pallas_call(
        body, name="grads_to_sibling",
        in_specs=[ANY] * n, out_specs=[ANY] * n,
        out_shape=[jax.ShapeDtypeStruct(g.shape[:1] + g.shape[2:], g.dtype) for g in grads],
        scratch_shapes=_dma_sems(N_CHIP * n, N_CHIP * n),
        compiler_params=pltpu.CompilerParams(has_side_effects=True),
    )(*grads)


def _pair_sum(grads, recvs):
    n = len(grads)

    def body(c_ref, *refs):
        for a in range(n):
            refs[2 * n + a][...] = (refs[a][...].astype(F32) + refs[n + a][...].astype(F32)).astype(BF16)

    def g_spec(g):
        return pl.BlockSpec((None, None) + g.shape[2:], lambda j, c_ref: (j, c_ref[0], 0, 0))

    def r_spec(r):
        return pl.BlockSpec((None,) + r.shape[1:], lambda j, c_ref: (j, 0, 0))

    return pl.pallas_call(
        body, name="pair_sum",
        grid_spec=pltpu.PrefetchScalarGridSpec(
            num_scalar_prefetch=1, grid=(N_CHIP,),
            in_specs=[g_spec(g) for g in grads] + [r_spec(r) for r in recvs],
            out_specs=[r_spec(r) for r in recvs]),
        out_shape=[jax.ShapeDtypeStruct(r.shape, BF16) for r in recvs],
        compiler_params=_params(("parallel",)),
    )(lax.axis_index("c").reshape(1).astype(jnp.int32), *grads, *recvs)


SMALL_ROWS = 544


def _all_reduce_small(buf, parts=()):
    n = len(parts)

    def body(in_ref, *rest):
        p_in, out_ref, p_out = rest[:n], rest[n], rest[n + 1:2 * n + 1]
        recv, acc, send1, recv1, send2, recv2 = rest[2 * n + 1:2 * n + 7]
        scatter_sems = rest[2 * n + 7:]
        x, y, c = _place()
        me = 4 * x + 2 * y + c
        peers = [(x ^ (r >> 2), y ^ ((r >> 1) & 1), c ^ (r & 1)) for r in range(1, N_DEV)]

        def idx(p):
            return 4 * p[0] + 2 * p[1] + p[2]

        if n:
            _comm_start(_rs_second(p_in, p_out, *scatter_sems))
        first = [pltpu.make_async_remote_copy(
            src_ref=in_ref.at[idx(p)], dst_ref=recv.at[me], send_sem=send1.at[r], recv_sem=recv1.at[r],
            device_id=p, device_id_type=MESH) for r, p in enumerate(peers)]
        for cp in first:
            cp.start()
        recv[me] = in_ref[me]
        for r, p in enumerate(peers):
            pltpu.make_async_remote_copy(
                src_ref=in_ref.at[idx(p)], dst_ref=recv.at[idx(p)], send_sem=send1.at[r], recv_sem=recv1.at[r],
                device_id=p, device_id_type=MESH).wait_recv()
        total = recv[0]
        for k in range(1, N_DEV):
            total = total + recv[k]
        acc[...] = total
        out_ref[me] = total
        second = [pltpu.make_async_remote_copy(
            src_ref=acc, dst_ref=out_ref.at[me], send_sem=send2.at[r], recv_sem=recv2.at[r],
            device_id=p, device_id_type=MESH) for r, p in enumerate(peers)]
        for cp in second:
            cp.start()
        for r, p in enumerate(peers):
            pltpu.make_async_remote_copy(
                src_ref=acc, dst_ref=out_ref.at[idx(p)], send_sem=send2.at[r], recv_sem=recv2.at[r],
                device_id=p, device_id_type=MESH).wait_recv()
        for cp in first + second:
            cp.wait_send()
        if n:
            _comm_wait(_rs_second(p_in, p_out, *scatter_sems))

    vm = pl.BlockSpec(memory_space=pltpu.VMEM)
    res = pl.pallas_call(
        body, name="small_grads_all_reduce",
        in_specs=[vm] + [ANY] * n, out_specs=[vm] + [ANY] * n,
        out_shape=[jax.ShapeDtypeStruct(buf.shape, F32)] + [jax.ShapeDtypeStruct(p.shape, p.dtype) for p in parts],
        scratch_shapes=[pltpu.VMEM(buf.shape, F32), pltpu.VMEM(buf.shape[1:], F32)] + _dma_sems(7, 7, 7, 7)
                       + (_dma_sems(3 * n, 3 * n, n) if n else []),
        compiler_params=pltpu.CompilerParams(has_side_effects=True, vmem_limit_bytes=VMEM_LIMIT),
    )(buf, *parts)
    return res[0], list(res[1:])


def _dilate(a, d):
    if d == 1:
        return a
    S, C = a.shape
    return a.reshape(S // d, d, C).transpose(1, 0, 2).reshape(S, C)


def _undilate(a, d):
    if d == 1:
        return a
    S, C = a.shape
    return a.reshape(d, S // d, C).transpose(1, 0, 2).reshape(S, C)


def _cols(a, cb, n=1):
    return a[:, cb * WIDTH:(cb + n) * WIDTH]


def _to_blocks(g, kind):
    if kind == "rows":
        C = g.shape[1]
        return g.reshape(N_CHIP, 2, -1, C)
    return g.reshape(4 * WIDTH, N_CHIP, 2, -1).transpose(1, 2, 0, 3)


SMALL = ("norm_g", "gm_ln_g", "gm_ln_b", "gm_ws", "gm_bs", "pool_w", "pool_scale", "mem_norm_g", "final_norm_g")


def _pack_small(tree):
    flat = jnp.concatenate([tree[k].reshape(-1, 128) for k in SMALL], axis=0)
    return jnp.pad(flat, ((0, N_DEV * SMALL_ROWS - flat.shape[0]), (0, 0)))


def _unpack_small(flat, like):
    out, at = {}, 0
    for k in SMALL:
        rows = like[k].size // 128
        out[k] = flat[at:at + rows].reshape(like[k].shape)
        at += rows
    return out


def _make_layer(wt, wkv, wb, wout, norm_g, mem_norm_g, ln_g, ln_b, gm_ws, gm_bs, pool_w, pool_scale):
    tril = jnp.tril(jnp.ones((CHUNK, CHUNK), bool))
    wsm = jnp.where(tril, gm_ws, 0.0).astype(BF16)
    pw = pool_w.astype(BF16)
    return dict(wt=wt, wgt=wt[D_BRANCHES:], wkv=wkv, wb=wb, wout=wout, g=norm_g[None], mg=mem_norm_g[None],
                ln_g=ln_g[None],
                ln_b=ln_b[None], wsm=wsm, wsm_t=wsm.transpose(0, 2, 1), pw=pw, pw_t=pw.transpose(0, 2, 1),
                ps=pool_scale[None], bias=jnp.repeat(gm_bs.T, HEAD, axis=1))


def _layer_fwd(xl, mem0, L, next_shards=()):
    S = xl.shape[0]
    proj, gates, h, half_gathered = _in_proj(xl, L["g"], L["wt"], L["wgt"], next_shards[:1])
    kv, mem_n = _mem_kv(mem0, L["mg"], L["wkv"])
    y4, gathered = _abm_fwd(proj, L["ln_g"], L["ln_b"], L["wsm"], L["bias"], L["pw"], L["ps"], kv, half_gathered)
    o_g, l_g = [], []
    for gi, d in enumerate(DILATIONS):
        if d == 1:
            o, lse = _attn_fwd(proj, CB_Q0, proj, CB_K, proj, CB_CV, S // CHUNK)
        else:
            o, lse = _attn_fwd_dilated(proj, CB_Q0 + gi, CB_K, CB_CV, d)
        o_g.append(o)
        l_g.append(lse)
    (xn, y4, oc, lse, z), rest = _merge_fwd(xl, y4, o_g, l_g, proj, gates, L["wb"], L["wout"], next_shards[1:])
    saved = dict(x=xl, proj=proj, gates=gates, h=h, kv=kv, mem_n=mem_n, y4=y4, oc=oc, lse=lse, z=z)
    return xn, saved, gathered + rest


def _place_cols(dst, piece, cb):
    return lax.dynamic_update_slice(dst, piece, (0, cb * WIDTH))


def _layer_bwd(dx, mem0, L, sv, later=()):
    S = dx.shape[0]
    proj = sv["proj"]
    (dy3, doc, delta, dcg, dgm, dwb, dwout), from_sibling = _merge_bwd(
        dx, sv["y4"], sv["oc"], sv["z"], proj, sv["gates"], L["wb"], L["wout"], later)
    pair = _pair_sum(later, from_sibling) if later else ()
    dpb, dm, dlng, dlnb, dws, dbias, dpw, dps, dkv = _abm_bwd(
        proj, dy3, L["ln_g"], L["ln_b"], L["wsm"], L["wsm_t"], L["bias"], L["pw"], L["pw_t"], L["ps"], sv["kv"])
    dk, dv = None, None
    for gi, d in enumerate(DILATIONS):
        if d == 1:
            r = _attn_bwd(proj, CB_Q0, proj, CB_K, proj, CB_CV, doc, sv["lse"], delta, S // CHUNK)
        else:
            r = _attn_bwd_dilated(proj, CB_Q0 + gi, CB_K, CB_CV, doc, sv["lse"], delta, d)
        dpb = _place_cols(dpb, r[0], CB_Q0 + gi)
        dkg, dvg = r[1].astype(F32), r[2].astype(F32)
        dk = dkg if dk is None else dk + dkg
        dv = dvg if dv is None else dv + dvg
    dpb = _place_cols(dpb, dk.astype(BF16), CB_K)
    dpb = _place_cols(dpb, dv.astype(BF16), CB_CV)
    dpb = _place_cols(dpb, dcg, CB_CGATE)
    dpb = _place_cols(dpb, dm, CB_MQ)
    dwkv, dmg = _mem_bwd(mem0, L["mg"], sv["mem_n"], L["wkv"], dkv)
    dwin_t, parts_rest = _dw_in(sv["h"], dpb, dgm, pair[1:])
    dxi, dng, parts = _dh_bwd(dpb, dgm, L["wt"], L["wgt"], sv["x"], L["g"], dx, pair[:1])
    parts = parts + parts_rest
    big = dict(w_in=dwin_t, w_mem_kv=dwkv, w_branch=dwb, w_out=dwout)
    small = dict(norm_g=dng[0], gm_ln_g=dlng[0], gm_ln_b=dlnb[0], gm_ws=dws,
                 gm_bs=_bias_reduce(dbias)[:, :N_HEAD].T, pool_w=dpw, pool_scale=dps[0], mem_norm_g=dmg[0])
    return dxi, big, small, parts


BIG = ("w_in", "w_mem_kv", "w_branch", "w_out")


def _blocked(big):
    return [_to_blocks(big["w_in"], "rows"), _to_blocks(big["w_mem_kv"], "rows"),
            _to_blocks(big["w_branch"], "branch"), _to_blocks(big["w_out"], "rows")]


def _full_weights(gathered):
    win_t, wkv, wb, wout = gathered
    return (win_t.reshape(D_IN, D_MODEL), wkv.reshape(D_MODEL, 2 * WIDTH),
            wb.reshape(N_DEV, 4, WIDTH, -1).transpose(1, 2, 0, 3).reshape(4, WIDTH, D_MODEL),
            wout.reshape(D_MODEL, D_MODEL))


def kernel(x, mem, norm_g, w_in, gm_ln_g, gm_ln_b, gm_ws, gm_bs, pool_w, pool_scale, mem_norm_g, w_mem_kv, w_branch, w_out, final_norm_g, loss_target, m_norm_g, m_w_in, m_gm_ln_g, m_gm_ln_b, m_gm_ws, m_gm_bs, m_pool_w, m_pool_scale, m_mem_norm_g, m_w_mem_kv, m_w_branch, m_w_out, m_final_norm_g, v_norm_g, v_w_in, v_gm_ln_g, v_gm_ln_b, v_gm_ws, v_gm_bs, v_pool_w, v_pool_scale, v_mem_norm_g, v_w_mem_kv, v_w_branch, v_w_out, v_final_norm_g):
    x0 = x[0]
    mem0 = mem[0]
    tgt = loss_target[0]
    S = x0.shape[0]

    shards = [[w_in[l].T.astype(BF16), w_mem_kv[l].astype(BF16), w_branch[l].astype(BF16).reshape(4 * WIDTH, -1),
               w_out[l].astype(BF16)] for l in range(DEPTH)]
    gathered = _all_gather(shards[0])
    layers, saved = [], []
    xl = x0
    for l in range(DEPTH):
        layers.append(_make_layer(*_full_weights(gathered), norm_g[l], mem_norm_g[l], gm_ln_g[l], gm_ln_b[l],
                                  gm_ws[l], gm_bs[l], pool_w[l], pool_scale[l]))
        xl, sv, gathered = _layer_fwd(xl, mem0, layers[l], shards[l + 1] if l + 1 < DEPTH else ())
        saved.append(sv)

    loss_part, dx, d_final = _loss_head(xl, final_norm_g[None], tgt)
    loss = lax.psum(loss_part[0, 0], ("x", "y", "c"))

    small = {k: [None] * DEPTH for k in SMALL if k != "final_norm_g"}
    parts = [None] * DEPTH
    later = ()
    for l in reversed(range(DEPTH)):
        dx, gb, gs, done = _layer_bwd(dx, mem0, layers[l], saved[l], later)
        if later:
            parts[l + 1] = done
        later = _blocked(gb)
        for k in gs:
            small[k][l] = gs[k]
    grad_x = dx[None]
    small_tree = {k: jnp.stack(small[k]) for k in small}
    small_tree["final_norm_g"] = d_final[0]
    reduced, parts[0] = _all_reduce_small(_pack_small(small_tree).reshape(N_DEV, SMALL_ROWS, 128),
                                          _pair_sum(later, _rs_sibling(later)))

    weights = dict(norm_g=norm_g, w_in=w_in, gm_ln_g=gm_ln_g, gm_ln_b=gm_ln_b, gm_ws=gm_ws, gm_bs=gm_bs,
                   pool_w=pool_w, pool_scale=pool_scale, mem_norm_g=mem_norm_g, w_mem_kv=w_mem_kv,
                   w_branch=w_branch, w_out=w_out, final_norm_g=final_norm_g)
    m_in = dict(norm_g=m_norm_g, w_in=m_w_in, gm_ln_g=m_gm_ln_g, gm_ln_b=m_gm_ln_b, gm_ws=m_gm_ws, gm_bs=m_gm_bs,
                pool_w=m_pool_w, pool_scale=m_pool_scale, mem_norm_g=m_mem_norm_g, w_mem_kv=m_w_mem_kv,
                w_branch=m_w_branch, w_out=m_w_out, final_norm_g=m_final_norm_g)
    v_in = dict(norm_g=v_norm_g, w_in=v_w_in, gm_ln_g=v_gm_ln_g, gm_ln_b=v_gm_ln_b, gm_ws=v_gm_ws, gm_bs=v_gm_bs,
                pool_w=v_pool_w, pool_scale=v_pool_scale, mem_norm_g=v_mem_norm_g, w_mem_kv=v_w_mem_kv,
                w_branch=v_w_branch, w_out=v_w_out, final_norm_g=v_final_norm_g)
    res = {}
    def view(k, arr):
        return arr.transpose(0, 2, 1) if k == "w_in" else arr

    for a, k in enumerate(BIG):
        shape = view(k, weights[k]).shape
        by_layer = [parts[l][a] for l in range(DEPTH)]
        lrc = (DEPTH,) + by_layer[0].shape[1:]
        outs = _adamw_layers(by_layer, view(k, weights[k]).reshape(lrc), view(k, m_in[k]).reshape(lrc),
                             view(k, v_in[k]).reshape(lrc), "adamw_" + k)
        res[k] = [view(k, o.reshape(shape)) for o in outs]
    outs = _adamw(reduced.reshape(1, N_DEV * SMALL_ROWS, 128), _pack_small(weights), _pack_small(m_in),
                  _pack_small(v_in), "adamw_small")
    unpacked = [_unpack_small(o, weights) for o in outs]
    for k in SMALL:
        res[k] = [u[k] for u in unpacked]

    order = ("norm_g", "w_in", "gm_ln_g", "gm_ln_b", "gm_ws", "gm_bs", "pool_w", "pool_scale", "mem_norm_g",
             "w_mem_kv", "w_branch", "w_out", "final_norm_g")
    return (loss, grad_x, *[res[k][0] for k in order], *[res[k][1] for k in order],
            *[res[k][2] for k in order], *[res[k][3] for k in order])
```

```python
import functools
import math

import jax
import jax.numpy as jnp
from jax import lax
from jax.experimental import pallas as pl
from jax.experimental.pallas import tpu as pltpu

F32 = jnp.float32
BF16 = jnp.bfloat16

D_MODEL = 1024
DEPTH = 4
WIDTH = 512
D_IN = 10752
HEAD = 128
N_HEAD = 4
CHUNK = 128
MEM_LEN = 256
POOL_WINDOWS = (2, 4, 8, 16)
DILATIONS = (1, 4, 16)
EPS = 1e-6
NEG = -1e30
ATT_SCALE = HEAD ** -0.5
N_DEV = 8
N_CHIP = 4

D_BRANCHES = 6656
D_GATES = D_IN - D_BRANCHES
CB_U, CB_V, CB_AGATE, CB_PIN, CB_PGATE = 0, 1, 2, 3, 4
CB_Q0, CB_K, CB_CV, CB_CGATE, CB_MQ, CB_MGATE = 5, 8, 9, 10, 11, 12

ADAM_LR = 0.001
ADAM_B1 = 0.9
ADAM_B2 = 0.999
ADAM_EPS = 1e-08
ADAM_WD = 0.01
ADAM_STEP = 10

VMEM_LIMIT = 56 * 1024 * 1024
MESH = pl.DeviceIdType.MESH
ANY = pl.BlockSpec(memory_space=pl.ANY)

NT = (((1,), (1,)), ((), ()))
TN = (((0,), (0,)), ((), ()))


def _dot(a, b):
    return jnp.dot(a, b, preferred_element_type=F32)


def _dot_nt(a, b):
    return lax.dot_general(a, b, NT, preferred_element_type=F32)


def _dot_tn(a, b):
    return lax.dot_general(a, b, TN, preferred_element_type=F32)


def _sigmoid(x):
    return 0.5 * jnp.tanh(0.5 * x) + 0.5


def _silu(x):
    return x * _sigmoid(x)


def _silu_and_grad(x):
    s = _sigmoid(x)
    return x * s, s * (1.0 + x * (1.0 - s))


def _normal_cdf(x):
    return 0.5 * (1.0 + lax.erf(x * (2.0 ** -0.5)))


def _gelu_and_grad(x, cdf):
    return x * cdf, cdf + x * jnp.exp(-0.5 * x * x) * (1.0 / math.sqrt(2.0 * math.pi))


def _col(blk, h):
    lane = lax.broadcasted_iota(jnp.int32, blk.shape, 1)
    return jnp.sum(jnp.where(lane == h, blk, 0.0), axis=1, keepdims=True)


def _put_cols(cols):
    rows = cols[0].shape[0]
    lane = lax.broadcasted_iota(jnp.int32, (rows, 128), 1)
    out = jnp.zeros((rows, 128), F32)
    for h, cv in enumerate(cols):
        out = jnp.where(lane == h, cv, out)
    return out


def _params(sem, vmem=VMEM_LIMIT):
    return pltpu.CompilerParams(dimension_semantics=sem, vmem_limit_bytes=vmem)


def _full(shape):
    nd = len(shape)
    return pl.BlockSpec(shape, lambda *_: (0,) * nd)


def _rows(tm, width, cb=0):
    return pl.BlockSpec((tm, width), lambda i: (i, cb))


GATE_TILE = 512


def _in_proj(x, g, wt, wgt, shards=()):
    S = x.shape[0]
    tm, tnb, tng = 1024, D_BRANCHES // 4, D_GATES // 4
    njb, njg = 4, 4
    n = len(shards)
    ni, nj = S // tm, njb + njg

    def body(x_ref, g_ref, wbr_ref, wg_ref, *rest):
        ins, (proj_ref, gates_ref, h_ref), outs = rest[:n], rest[n:n + 3], rest[n + 3:2 * n + 3]
        hs, sems = rest[2 * n + 3], rest[2 * n + 4:]
        i, j = pl.program_id(0), pl.program_id(1)

        if n:
            @pl.when(jnp.logical_and(i == 0, j == 0))
            def _():
                _comm_start(_ag_first(ins, outs, *sems))

        @pl.when(j == 0)
        def _():
            xf = x_ref[...]
            r = lax.rsqrt(jnp.mean(xf * xf, axis=-1, keepdims=True) + EPS)
            h = (xf * r * g_ref[...]).astype(BF16)
            hs[...] = h
            h_ref[...] = h

        @pl.when(j < njb)
        def _():
            proj_ref[...] = _dot_nt(hs[...], wbr_ref[...]).astype(BF16)

        @pl.when(j >= njb)
        def _():
            gates_ref[...] = _dot_nt(hs[...], wg_ref[...]).astype(BF16)

        if n:
            @pl.when(jnp.logical_and(i == ni - 1, j == nj - 1))
            def _():
                _comm_wait(_ag_first(ins, outs, *sems))

    def first(j):
        return jnp.minimum(j, njb - 1)

    def second(j):
        return jnp.maximum(j - njb, 0)

    res = pl.pallas_call(
        body, name="in_proj_gather" if n else "in_proj",
        grid=(ni, nj),
        in_specs=[pl.BlockSpec((tm, D_MODEL), lambda i, j: (i, 0)),
                  pl.BlockSpec((1, D_MODEL), lambda i, j: (0, 0)),
                  pl.BlockSpec((tnb, D_MODEL), lambda i, j: (first(j), 0)),
                  pl.BlockSpec((tng, D_MODEL), lambda i, j: (second(j), 0))]
                 + [ANY] * n,
        out_specs=[pl.BlockSpec((tm, tnb), lambda i, j: (i, first(j))),
                   pl.BlockSpec((tm, tng), lambda i, j: (i, second(j))),
                   pl.BlockSpec((tm, D_MODEL), lambda i, j: (i, 0))] + [ANY] * n,
        out_shape=[jax.ShapeDtypeStruct((S, D_BRANCHES), BF16), jax.ShapeDtypeStruct((S, D_GATES), BF16),
                   jax.ShapeDtypeStruct((S, D_MODEL), BF16)]
                  + [jax.ShapeDtypeStruct((N_DEV,) + s.shape, s.dtype) for s in shards],
        scratch_shapes=[pltpu.VMEM((tm, D_MODEL), BF16)] + (_dma_sems(4 * n, 4 * n, n) if n else []),
        compiler_params=_params(("arbitrary", "arbitrary")),
    )(x, g, wt, wgt, *shards)
    return res[0], res[1], res[2], list(res[3:])


def _mem_kv(mem, g, w):
    M = mem.shape[0]

    def body(m_ref, g_ref, w_ref, kv_ref, mn_ref):
        xf = m_ref[...]
        r = lax.rsqrt(jnp.mean(xf * xf, axis=-1, keepdims=True) + EPS)
        mn = (xf * r * g_ref[...]).astype(BF16)
        mn_ref[...] = mn
        kv_ref[...] = _dot(mn, w_ref[...]).astype(BF16)

    return pl.pallas_call(
        body, name="mem_kv",
        out_shape=[jax.ShapeDtypeStruct((M, 2 * WIDTH), BF16), jax.ShapeDtypeStruct((M, D_MODEL), BF16)],
        compiler_params=pltpu.CompilerParams(vmem_limit_bytes=VMEM_LIMIT),
    )(mem, g, w)


def _band_masks(win):
    t = lax.broadcasted_iota(jnp.int32, (CHUNK, CHUNK), 0)
    s = lax.broadcasted_iota(jnp.int32, (CHUNK, CHUNK), 1)
    cur = jnp.logical_and(t - s >= 0, t - s < win)
    prev = s > t + CHUNK - win
    return cur.astype(BF16), prev.astype(BF16)


def _inv_count(first_row, win):
    t = first_row + lax.broadcasted_iota(jnp.int32, (CHUNK, 1), 0)
    return 1.0 / jnp.minimum(t + 1, win).astype(F32)


def _layer_norm_fwd(v):
    mu = jnp.mean(v, axis=-1, keepdims=True)
    vc = v - mu
    var = jnp.mean(vc * vc, axis=-1, keepdims=True)
    rstd = lax.rsqrt(var + EPS)
    return vc * rstd, rstd


def _mem_softmax(q, kmem):
    s = _dot_nt(q, kmem) * ATT_SCALE
    m = jnp.max(s, axis=-1, keepdims=True)
    e = jnp.exp(s - m)
    return e * (1.0 / jnp.sum(e, axis=-1, keepdims=True))


def _abm_fwd(proj, ln_g, ln_b, wsm, bias_full, pool_w, pool_scale, kv, gathered=()):
    S = proj.shape[0]
    tm = 512
    nchunk = tm // CHUNK
    n = len(gathered)
    nsteps = S // tm

    def body(u_ref, v_ref, ag_ref, p_ref, ph_ref, pg_ref, mq_ref, mg_ref, lng_ref, lnb_ref, wsm_ref, bias_ref,
             pw_ref, ps_ref, kv_ref, *rest):
        y_ref, cdf_ref, bufs = rest[n], rest[n + 1], rest[n + 2:2 * n + 2]
        mix, sems = rest[2 * n + 2], rest[2 * n + 3:]
        i = pl.program_id(0)

        if n:
            @pl.when(i == 0)
            def _():
                _comm_start(_ag_second(bufs, *sems))

        au, av = u_ref[...].astype(F32), v_ref[...].astype(F32)
        cdf_u, cdf_v = _normal_cdf(au), _normal_cdf(av)
        cdf_ref[0] = cdf_u.astype(BF16)
        cdf_ref[1] = cdf_v.astype(BF16)
        u, v = au * cdf_u, av * cdf_v
        vhat, _ = _layer_norm_fwd(v)
        vln = (vhat * lng_ref[...] + lnb_ref[...]).astype(BF16)
        for c in range(nchunk):
            for h in range(N_HEAD):
                rs, cs = slice(c * CHUNK, (c + 1) * CHUNK), slice(h * HEAD, (h + 1) * HEAD)
                mix[rs, cs] = _dot(wsm_ref[h], vln[rs, cs]) + bias_ref[:, cs]
        y_ref[0] = (u * mix[...] * _silu(ag_ref[...].astype(F32))).astype(BF16)
        halo_ok = (i > 0).astype(F32)
        for c in range(nchunk):
            rs = slice(c * CHUNK, (c + 1) * CHUNK)
            for g, win in enumerate(POOL_WINDOWS):
                cs = slice(g * HEAD, (g + 1) * HEAD)
                bcur, bprev = _band_masks(win)
                cur = p_ref[rs, cs]
                if c == 0:
                    prev = (ph_ref[:, cs].astype(F32) * halo_ok).astype(BF16)
                else:
                    prev = p_ref[(c - 1) * CHUNK:c * CHUNK, cs]
                sums = _dot(bcur, cur) + _dot(bprev, prev)
                dm = sums * _inv_count(i * tm + c * CHUNK, win) - cur.astype(F32)
                mix[rs, cs] = _dot(dm.astype(BF16), pw_ref[g])
        y_ref[1] = (mix[...] * ps_ref[...] * _silu(pg_ref[...].astype(F32))).astype(BF16)
        for h in range(N_HEAD):
            cs = slice(h * HEAD, (h + 1) * HEAD)
            p = _mem_softmax(mq_ref[:, cs], kv_ref[:, cs])
            mix[:, cs] = _dot(p.astype(BF16), kv_ref[:, WIDTH + h * HEAD:WIDTH + (h + 1) * HEAD])
        y_ref[2] = (mix[...] * _silu(mg_ref[...].astype(F32))).astype(BF16)

        if n:
            @pl.when(i == nsteps - 1)
            def _():
                _comm_wait(_ag_second(bufs, *sems))

    blk = tm // CHUNK
    res = pl.pallas_call(
        body, name="abm_fwd_gather" if n else "abm_fwd",
        grid=(nsteps,),
        in_specs=[_rows(tm, WIDTH, CB_U), _rows(tm, WIDTH, CB_V), _rows(tm, WIDTH, CB_AGATE),
                  _rows(tm, WIDTH, CB_PIN),
                  pl.BlockSpec((CHUNK, WIDTH), lambda i: (jnp.maximum(i * blk - 1, 0), CB_PIN)),
                  _rows(tm, WIDTH, CB_PGATE), _rows(tm, WIDTH, CB_MQ), _rows(tm, WIDTH, CB_MGATE),
                  _full((1, WIDTH)), _full((1, WIDTH)), _full((N_HEAD, CHUNK, CHUNK)), _full((CHUNK, WIDTH)),
                  _full((4, HEAD, HEAD)), _full((1, WIDTH)), _full((MEM_LEN, 2 * WIDTH))] + [ANY] * n,
        out_specs=[pl.BlockSpec((3, tm, WIDTH), lambda i: (0, i, 0)), pl.BlockSpec((2, tm, WIDTH), lambda i: (0, i, 0))]
                  + [ANY] * n,
        out_shape=[jax.ShapeDtypeStruct((4, S, WIDTH), BF16),
                   jax.ShapeDtypeStruct((2, S, WIDTH), BF16)]
                  + [jax.ShapeDtypeStruct(b.shape, b.dtype) for b in gathered],
        input_output_aliases={15 + a: 2 + a for a in range(n)},
        scratch_shapes=[pltpu.VMEM((tm, WIDTH), F32)] + (_dma_sems(3 * n, 3 * n) if n else []),
        compiler_params=_params(("arbitrary",)),
    )(proj, proj, proj, proj, proj, proj, proj, proj, ln_g, ln_b, wsm, bias_full, pool_w, pool_scale, kv, *gathered)
    return res[0], res[1], list(res[2:])


ATT_TILE = 512


def _attn_fwd(q, qcb, k, kcb, v, vcb, bps):
    S = q.shape[0]
    tm = ATT_TILE
    nb = tm // CHUNK

    nblocks = nb * N_HEAD

    def body(q_ref, k_ref, v_ref, kh_ref, vh_ref, o_ref, l_ref, sc_s, sp_s, pc_s, pp_s):
        i = pl.program_id(0)

        def prev_kv(n, cs):
            if n == 0:
                return kh_ref[:, cs], vh_ref[:, cs]
            ps = slice((n - 1) * CHUNK, n * CHUNK)
            return k_ref[ps, cs], v_ref[ps, cs]

        pens = []
        for n in range(nb):
            rs = slice(n * CHUNK, (n + 1) * CHUNK)
            pens.append(jnp.full((N_HEAD * CHUNK, 1), jnp.where((i * nb + n) % bps != 0, 0.0, NEG), F32))
            for h in range(N_HEAD):
                cs = slice(h * HEAD, (h + 1) * HEAD)
                bs = slice((n * N_HEAD + h) * CHUNK, (n * N_HEAD + h + 1) * CHUNK)
                qh = q_ref[rs, cs]
                sc_s[bs, :] = _dot_nt(qh, k_ref[rs, cs])
                sp_s[bs, :] = _dot_nt(qh, prev_kv(n, cs)[0])
        row = lax.broadcasted_iota(jnp.int32, (nblocks * CHUNK, CHUNK), 0) & (CHUNK - 1)
        col = lax.broadcasted_iota(jnp.int32, (nblocks * CHUNK, CHUNK), 1)
        sc = jnp.where(col <= row, sc_s[...] * ATT_SCALE, NEG)
        sp = jnp.where(col >= row, sp_s[...] * ATT_SCALE, NEG) + jnp.concatenate(pens, axis=0)
        m = jnp.maximum(jnp.max(sc, axis=-1, keepdims=True), jnp.max(sp, axis=-1, keepdims=True))
        ec = jnp.exp(sc - m)
        ep = jnp.exp(sp - m)
        den = jnp.sum(ec, axis=-1, keepdims=True) + jnp.sum(ep, axis=-1, keepdims=True)
        inv = 1.0 / den
        pc_s[...] = (ec * inv).astype(BF16)
        pp_s[...] = (ep * inv).astype(BF16)
        lse = m + jnp.log(den)
        for n in range(nb):
            rs = slice(n * CHUNK, (n + 1) * CHUNK)
            for h in range(N_HEAD):
                cs = slice(h * HEAD, (h + 1) * HEAD)
                bs = slice((n * N_HEAD + h) * CHUNK, (n * N_HEAD + h + 1) * CHUNK)
                o = _dot(pc_s[bs, :], v_ref[rs, cs]) + _dot(pp_s[bs, :], prev_kv(n, cs)[1])
                o_ref[rs, cs] = o.astype(BF16)
            l_ref[rs, :] = _put_cols([lse[(n * N_HEAD + h) * CHUNK:(n * N_HEAD + h + 1) * CHUNK]
                                      for h in range(N_HEAD)])

    def halo(cb):
        return pl.BlockSpec((CHUNK, WIDTH), lambda i: (jnp.maximum(i * nb - 1, 0), cb))

    return pl.pallas_call(
        body, name=f"attn_fwd_{bps}",
        grid=(S // tm,),
        in_specs=[_rows(tm, WIDTH, qcb), _rows(tm, WIDTH, kcb), _rows(tm, WIDTH, vcb), halo(kcb), halo(vcb)],
        out_specs=[_rows(tm, WIDTH), _rows(tm, 128)],
        out_shape=[jax.ShapeDtypeStruct((S, WIDTH), BF16), jax.ShapeDtypeStruct((S, 128), F32)],
        scratch_shapes=[pltpu.VMEM((nblocks * CHUNK, CHUNK), F32), pltpu.VMEM((nblocks * CHUNK, CHUNK), F32),
                        pltpu.VMEM((nblocks * CHUNK, CHUNK), BF16), pltpu.VMEM((nblocks * CHUNK, CHUNK), BF16)],
        compiler_params=_params(("parallel",)),
    )(q, k, v, k, v)


def _gate_specs(tm):
    return [pl.BlockSpec((tm, D_MODEL), lambda i, b=b: (i, b)) for b in range(4)]


Y_SLOT = (0, 1, 3, 2)


def _merge_fwd(x, y4, o_g, l_g, proj, gates, wb, wout, shards=()):
    S = x.shape[0]
    tm = 256
    n = len(shards)
    nsteps = S // tm
    forward_at = nsteps - 4

    def body(x_ref, y_ref, o0, o1, o2, l0, l1, l2, cg_ref, *rest):
        gm = rest[:4]
        wb_ref, wo_ref = rest[4:6]
        s_in = rest[6:6 + n]
        xn_ref, yc_ref, oc_ref, lse_ref, z_ref = rest[6 + n:11 + n]
        s_out, ocs, sems = rest[11 + n:11 + 2 * n], rest[11 + 2 * n], rest[12 + 2 * n:]
        i = pl.program_id(0)

        if n:
            @pl.when(i == 0)
            def _():
                _comm_start(_ag_first(s_in, s_out, *sems[:3]))

            @pl.when(i == forward_at)
            def _():
                incoming = _ag_first(s_in, s_out, *sems[:3])[2]
                for a in range(n):
                    for k in range(1, 4):
                        incoming[4 * a + k].wait_recv()
                _comm_start(_ag_second(s_out, *sems[3:]))

        lcols = []
        for h in range(N_HEAD):
            cs = slice(h * HEAD, (h + 1) * HEAD)
            ls = [_col(l[...], h) for l in (l0, l1, l2)]
            m = jnp.maximum(jnp.maximum(ls[0], ls[1]), ls[2])
            tot = jnp.exp(ls[0] - m) + jnp.exp(ls[1] - m) + jnp.exp(ls[2] - m)
            lse = m + jnp.log(tot)
            ocs[:, cs] = sum(jnp.exp(lg - lse) * o[:, cs].astype(F32) for lg, o in zip(ls, (o0, o1, o2)))
            lcols.append(lse)
        lse_ref[...] = _put_cols(lcols)
        oc = ocs[...]
        oc_ref[...] = oc.astype(BF16)
        yc = (oc * _silu(cg_ref[...].astype(F32))).astype(BF16)
        yc_ref[...] = yc
        ys = (y_ref[0], y_ref[1], yc, y_ref[2])
        z = jnp.zeros((tm, D_MODEL), F32)
        for b in range(4):
            z = z + _sigmoid(gm[b][...].astype(F32)) * _dot(ys[b], wb_ref[b])
        zb = z.astype(BF16)
        z_ref[...] = zb
        xn_ref[...] = x_ref[...] + _dot(zb, wo_ref[...])

        if n:
            @pl.when(i == nsteps - 1)
            def _():
                local, out, incoming = _ag_first(s_in, s_out, *sems[:3])
                for a in range(n):
                    incoming[4 * a].wait_recv()
                _comm_wait(_ag_second(s_out, *sems[3:]))
                for cp in out:
                    cp.wait_send()
                for cp in local:
                    cp.wait()

    res = pl.pallas_call(
        body, name="merge_fwd_gather" if n else "merge_fwd",
        grid=(nsteps,),
        in_specs=[_rows(tm, D_MODEL), pl.BlockSpec((3, tm, WIDTH), lambda i: (0, i, 0)),
                  _rows(tm, WIDTH), _rows(tm, WIDTH), _rows(tm, WIDTH),
                  _rows(tm, 128), _rows(tm, 128), _rows(tm, 128),
                  _rows(tm, WIDTH, CB_CGATE)] + _gate_specs(tm)
                 + [_full((4, WIDTH, D_MODEL)), _full((D_MODEL, D_MODEL))] + [ANY] * n,
        out_specs=[_rows(tm, D_MODEL), pl.BlockSpec((None, tm, WIDTH), lambda i: (Y_SLOT[2], i, 0)),
                   _rows(tm, WIDTH), _rows(tm, 128), _rows(tm, D_MODEL)] + [ANY] * n,
        out_shape=[jax.ShapeDtypeStruct((S, D_MODEL), F32), jax.ShapeDtypeStruct(y4.shape, BF16),
                   jax.ShapeDtypeStruct((S, WIDTH), BF16), jax.ShapeDtypeStruct((S, 128), F32),
                   jax.ShapeDtypeStruct((S, D_MODEL), BF16)]
                  + [jax.ShapeDtypeStruct((N_DEV,) + s.shape, s.dtype) for s in shards],
        input_output_aliases={1: 1},
        scratch_shapes=[pltpu.VMEM((tm, WIDTH), F32)] + (_dma_sems(4 * n, 4 * n, n, 3 * n, 3 * n) if n else []),
        compiler_params=_params(("arbitrary",)),
    )(x, y4, *o_g, *l_g, proj, *([gates] * 4), wb, wout, *shards)
    return res[:5], list(res[5:])


def _loss_head(x, g, tgt):
    S = x.shape[0]
    tm = 512

    def body(x_ref, g_ref, t_ref, loss_ref, dx_ref, dg_ref):
        @pl.when(pl.program_id(0) == 0)
        def _():
            loss_ref[...] = jnp.zeros_like(loss_ref)
            dg_ref[...] = jnp.zeros_like(dg_ref)

        xf = x_ref[...]
        r = lax.rsqrt(jnp.mean(xf * xf, axis=-1, keepdims=True) + EPS)
        xhat = xf * r
        gv = g_ref[...]
        err = xhat * gv - t_ref[...]
        e2 = jnp.sum(err * err, axis=-1, keepdims=True)
        loss_ref[...] += (0.5 / D_MODEL) * jnp.sum(e2, axis=0, keepdims=True)
        dy = err * (1.0 / D_MODEL)
        dg_ref[...] += jnp.sum(dy * xhat, axis=0, keepdims=True)
        dxh = dy * gv
        dx_ref[...] = r * (dxh - xhat * jnp.mean(dxh * xhat, axis=-1, keepdims=True))

    return pl.pallas_call(
        body, name="loss_head",
        grid=(S // tm,),
        in_specs=[_rows(tm, D_MODEL), _full((1, D_MODEL)), _rows(tm, D_MODEL)],
        out_specs=[_full((1, 128)), _rows(tm, D_MODEL), _full((1, D_MODEL))],
        out_shape=[jax.ShapeDtypeStruct((1, 128), F32), jax.ShapeDtypeStruct((S, D_MODEL), F32),
                   jax.ShapeDtypeStruct((1, D_MODEL), F32)],
        compiler_params=_params(("arbitrary",)),
    )(x, g, tgt)


def _merge_bwd(dxo, y4, oc, z, proj, gates, wb, wout, grads=()):
    S = dxo.shape[0]
    tm = 256
    n = len(grads)
    nsteps = S // tm

    def body(dx_ref, y_ref, oc_ref, z_ref, cg_ref, *rest):
        gm = rest[:4]
        wb_ref, wo_ref = rest[4:6]
        g_in = rest[6:6 + n]
        dy_ref, doc_ref, delta_ref, dcg_ref, dgm_ref, dwb_ref, dwo_ref = rest[6 + n:13 + n]
        g_out = rest[13 + n:13 + 2 * n]
        acc_b, acc_o = rest[13 + 2 * n:15 + 2 * n]
        sems = rest[15 + 2 * n:]
        i = pl.program_id(0)

        @pl.when(i == 0)
        def _():
            acc_b[...] = jnp.zeros_like(acc_b)
            acc_o[...] = jnp.zeros_like(acc_o)
            if n:
                _comm_start(_rs_first(g_in, g_out, *sems))

        dxb = dx_ref[...].astype(BF16)
        acc_o[...] += _dot_tn(z_ref[...], dxb)
        dz = _dot_nt(dxb, wo_ref[...])
        for b in range(4):
            gate = _sigmoid(gm[b][...].astype(F32))
            yb = y_ref[Y_SLOT[b]]
            t = _dot(yb, wb_ref[b])
            dgm_ref[:, b * D_MODEL:(b + 1) * D_MODEL] = (dz * t * gate * (1.0 - gate)).astype(BF16)
            dt = (dz * gate).astype(BF16)
            acc_b[b] += _dot_tn(yb, dt)
            dyb = _dot_nt(dt, wb_ref[b])
            if b == 2:
                cg = cg_ref[...].astype(F32)
                oc = oc_ref[...].astype(F32)
                scg, dscg = _silu_and_grad(cg)
                doc = dyb * scg
                dcg_ref[...] = (dyb * oc * dscg).astype(BF16)
                doc_ref[...] = doc.astype(BF16)
                prod = doc * oc
                delta_ref[...] = _put_cols([jnp.sum(prod[:, h * HEAD:(h + 1) * HEAD], axis=1, keepdims=True)
                                            for h in range(N_HEAD)])
            else:
                dy_ref[b if b < 2 else 2] = dyb.astype(BF16)

        @pl.when(i == nsteps - 1)
        def _():
            dwb_ref[...] = acc_b[...].astype(BF16)
            dwo_ref[...] = acc_o[...].astype(BF16)
            if n:
                _comm_wait(_rs_first(g_in, g_out, *sems))

    def resident(shape):
        nd = len(shape)
        return pl.BlockSpec(shape, lambda i: (0,) * nd, pipeline_mode=pl.Buffered(1))

    res = pl.pallas_call(
        body, name="merge_bwd_scatter" if n else "merge_bwd",
        grid=(nsteps,),
        in_specs=[_rows(tm, D_MODEL), pl.BlockSpec((4, tm, WIDTH), lambda i: (0, i, 0)),
                  _rows(tm, WIDTH), _rows(tm, D_MODEL), _rows(tm, WIDTH, CB_CGATE)] + _gate_specs(tm)
                 + [resident((4, WIDTH, D_MODEL)), resident((D_MODEL, D_MODEL))] + [ANY] * n,
        out_specs=[pl.BlockSpec((3, tm, WIDTH), lambda i: (0, i, 0)), _rows(tm, WIDTH), _rows(tm, 128),
                   _rows(tm, WIDTH), _rows(tm, 4 * D_MODEL), _full((4, WIDTH, D_MODEL)), _full((D_MODEL, D_MODEL))]
                  + [ANY] * n,
        out_shape=[jax.ShapeDtypeStruct((3, S, WIDTH), BF16), jax.ShapeDtypeStruct((S, WIDTH), BF16),
                   jax.ShapeDtypeStruct((S, 128), F32), jax.ShapeDtypeStruct((S, WIDTH), BF16),
                   jax.ShapeDtypeStruct((S, 4 * D_MODEL), BF16), jax.ShapeDtypeStruct((4, WIDTH, D_MODEL), BF16),
                   jax.ShapeDtypeStruct((D_MODEL, D_MODEL), BF16)]
                  + [jax.ShapeDtypeStruct(g.shape[:1] + g.shape[2:], g.dtype) for g in grads],
        scratch_shapes=[pltpu.VMEM((4, WIDTH, D_MODEL), F32), pltpu.VMEM((D_MODEL, D_MODEL), F32)]
                       + (_dma_sems(N_CHIP * n, N_CHIP * n) if n else []),
        compiler_params=_params(("arbitrary",)),
    )(dxo, y4, oc, z, proj, *([gates] * 4), wb, wout, *grads)
    return res[:7], list(res[7:])


def _attn_bwd(q, qcb, k, kcb, v, vcb, do, lse, delta, bps):
    S = q.shape[0]
    tm = ATT_TILE
    nb = tm // CHUNK
    nblk = S // CHUNK

    ncur = nb * N_HEAD
    nprev = (nb + 1) * N_HEAD

    def body(q_ref, k_ref, v_ref, do_ref, l_ref, d_ref, kh_ref, vh_ref, qn_ref, don_ref, ln_ref, dn_ref,
             dq_ref, dk_ref, dv_ref, sc_s, sp_s, dpc_s, dpp_s, pc_s, pp_s, dsc_s, dsp_s):
        i = pl.program_id(0)

        def rows_of(n):
            if n < nb:
                rs = slice(n * CHUNK, (n + 1) * CHUNK)
                return rs, q_ref, do_ref, l_ref, d_ref
            return slice(0, CHUNK), qn_ref, don_ref, ln_ref, dn_ref

        def prev_kv(n, cs):
            if n == 0:
                return kh_ref[:, cs], vh_ref[:, cs]
            ps = slice((n - 1) * CHUNK, n * CHUNK)
            return k_ref[ps, cs], v_ref[ps, cs]

        def blk(n, h):
            return slice((n * N_HEAD + h) * CHUNK, (n * N_HEAD + h + 1) * CHUNK)

        pens, lses, deltas = [], [], []
        for n in range(nb + 1):
            rs, qr, dor, lr, dr = rows_of(n)
            gb = i * nb + n
            pen = jnp.where(gb % bps != 0, 0.0, NEG)
            if n == nb:
                pen = pen + jnp.where(gb < nblk, 0.0, NEG)
            pens.append(jnp.full((N_HEAD * CHUNK, 1), pen, F32))
            lblk, dblk = lr[rs, :], dr[rs, :]
            for h in range(N_HEAD):
                cs = slice(h * HEAD, (h + 1) * HEAD)
                qh, doh = qr[rs, cs], dor[rs, cs]
                lses.append(_col(lblk, h))
                deltas.append(_col(dblk, h))
                kp, vp = prev_kv(n, cs)
                sp_s[blk(n, h), :] = _dot_nt(qh, kp)
                dpp_s[blk(n, h), :] = _dot_nt(doh, vp)
                if n < nb:
                    sc_s[blk(n, h), :] = _dot_nt(qh, k_ref[rs, cs])
                    dpc_s[blk(n, h), :] = _dot_nt(doh, v_ref[rs, cs])
        lse = jnp.concatenate(lses, axis=0)
        delta = jnp.concatenate(deltas, axis=0)
        row = lax.broadcasted_iota(jnp.int32, (nprev * CHUNK, CHUNK), 0) & (CHUNK - 1)
        col = lax.broadcasted_iota(jnp.int32, (nprev * CHUNK, CHUNK), 1)
        sp = jnp.where(col >= row, sp_s[...] * ATT_SCALE, NEG) + jnp.concatenate(pens, axis=0)
        pp = jnp.exp(sp - lse)
        pp_s[...] = pp.astype(BF16)
        dsp_s[...] = (pp * (dpp_s[...] - delta)).astype(BF16)
        nc = ncur * CHUNK
        sc = jnp.where(col[:nc] <= row[:nc], sc_s[...] * ATT_SCALE, NEG)
        pc = jnp.exp(sc - lse[:nc])
        pc_s[...] = pc.astype(BF16)
        dsc_s[...] = (pc * (dpc_s[...] - delta[:nc])).astype(BF16)
        for n in range(nb):
            rs, qr, dor, _, _ = rows_of(n)
            rn, qnr, donr, _, _ = rows_of(n + 1)
            for h in range(N_HEAD):
                cs = slice(h * HEAD, (h + 1) * HEAD)
                kp, _ = prev_kv(n, cs)
                dq = _dot(dsc_s[blk(n, h), :], k_ref[rs, cs]) + _dot(dsp_s[blk(n, h), :], kp)
                dq_ref[rs, cs] = (dq * ATT_SCALE).astype(BF16)
                dk = _dot_tn(dsc_s[blk(n, h), :], qr[rs, cs]) + _dot_tn(dsp_s[blk(n + 1, h), :], qnr[rn, cs])
                dk_ref[rs, cs] = (dk * ATT_SCALE).astype(BF16)
                dv = _dot_tn(pc_s[blk(n, h), :], dor[rs, cs]) + _dot_tn(pp_s[blk(n + 1, h), :], donr[rn, cs])
                dv_ref[rs, cs] = dv.astype(BF16)

    def prev_halo(cb):
        return pl.BlockSpec((CHUNK, WIDTH), lambda i: (jnp.maximum(i * nb - 1, 0), cb))

    def next_halo(width, cb=0):
        return pl.BlockSpec((CHUNK, width), lambda i: (jnp.minimum(i * nb + nb, nblk - 1), cb))

    return pl.pallas_call(
        body, name=f"attn_bwd_{bps}",
        grid=(S // tm,),
        in_specs=[_rows(tm, WIDTH, qcb), _rows(tm, WIDTH, kcb), _rows(tm, WIDTH, vcb), _rows(tm, WIDTH),
                  _rows(tm, 128), _rows(tm, 128), prev_halo(kcb), prev_halo(vcb),
                  next_halo(WIDTH, qcb), next_halo(WIDTH), next_halo(128), next_halo(128)],
        out_specs=[_rows(tm, WIDTH), _rows(tm, WIDTH), _rows(tm, WIDTH)],
        out_shape=[jax.ShapeDtypeStruct((S, WIDTH), BF16)] * 3,
        scratch_shapes=[pltpu.VMEM((ncur * CHUNK, CHUNK), F32), pltpu.VMEM((nprev * CHUNK, CHUNK), F32),
                        pltpu.VMEM((ncur * CHUNK, CHUNK), F32), pltpu.VMEM((nprev * CHUNK, CHUNK), F32),
                        pltpu.VMEM((ncur * CHUNK, CHUNK), BF16), pltpu.VMEM((nprev * CHUNK, CHUNK), BF16),
                        pltpu.VMEM((ncur * CHUNK, CHUNK), BF16), pltpu.VMEM((nprev * CHUNK, CHUNK), BF16)],
        compiler_params=_params(("parallel",)),
    )(q, k, v, do, lse, delta, k, v, q, do, lse, delta)


def _dilated_split(d):
    hp = min(N_HEAD, 16 // d)
    return hp, N_HEAD // hp, HEAD * hp


def _by_class(src_ref, dst, d, hp):
    for j in range(hp):
        dst[j] = pltpu.einshape("(tr)l->(rt)l", src_ref[:, j * HEAD:(j + 1) * HEAD], r=d)


def _from_class(src, dst_ref, d, hp):
    for j in range(hp):
        dst_ref[:, j * HEAD:(j + 1) * HEAD] = pltpu.einshape("(rt)l->(tr)l", src[j].astype(BF16), r=d)


def _attn_fwd_dilated(proj, qcb, kcb, vcb, d):
    S = proj.shape[0]
    T = CHUNK * d
    hp, nh, cw = _dilated_split(d)
    nblocks = d * hp

    def body(q_ref, k_ref, v_ref, o_ref, l_ref, qf, kf, vf, kpf, vpf, kst, vst, of, lf, sc_s, sp_s, pc_s, pp_s):
        i, hh = pl.program_id(0), pl.program_id(1)
        _by_class(q_ref, qf, d, hp)
        _by_class(k_ref, kf, d, hp)
        _by_class(v_ref, vf, d, hp)

        @pl.when(i == 0)
        def _():
            kpf[...] = jnp.zeros_like(kpf)
            vpf[...] = jnp.zeros_like(vpf)

        @pl.when(i > 0)
        def _():
            kpf[...] = kst[hh]
            vpf[...] = vst[hh]

        def blk(ref, r, j):
            return ref[j, r * CHUNK:(r + 1) * CHUNK, :]

        def bs(r, j):
            return slice((r * hp + j) * CHUNK, (r * hp + j + 1) * CHUNK)

        for r in range(d):
            for j in range(hp):
                qb = blk(qf, r, j)
                sc_s[bs(r, j), :] = _dot_nt(qb, blk(kf, r, j))
                sp_s[bs(r, j), :] = _dot_nt(qb, blk(kpf, r, j))
        row = lax.broadcasted_iota(jnp.int32, (nblocks * CHUNK, CHUNK), 0) & (CHUNK - 1)
        col = lax.broadcasted_iota(jnp.int32, (nblocks * CHUNK, CHUNK), 1)
        sc = jnp.where(col <= row, sc_s[...] * ATT_SCALE, NEG)
        sp = jnp.where(col >= row, sp_s[...] * ATT_SCALE, NEG) + jnp.where(i > 0, 0.0, NEG)
        m = jnp.maximum(jnp.max(sc, axis=-1, keepdims=True), jnp.max(sp, axis=-1, keepdims=True))
        ec = jnp.exp(sc - m)
        ep = jnp.exp(sp - m)
        den = jnp.sum(ec, axis=-1, keepdims=True) + jnp.sum(ep, axis=-1, keepdims=True)
        inv = 1.0 / den
        pc_s[...] = (ec * inv).astype(BF16)
        pp_s[...] = (ep * inv).astype(BF16)
        lse = m + jnp.log(den)
        lane = lax.broadcasted_iota(jnp.int32, (CHUNK, 128), 1)
        for r in range(d):
            lblk = jnp.zeros((CHUNK, 128), F32)
            for j in range(hp):
                o = _dot(pc_s[bs(r, j), :], blk(vf, r, j)) + _dot(pp_s[bs(r, j), :], blk(vpf, r, j))
                of[j, r * CHUNK:(r + 1) * CHUNK, :] = o
                lblk = jnp.where(lane == hh * hp + j, lse[bs(r, j)], lblk)
            lf[r * CHUNK:(r + 1) * CHUNK, :] = lblk
        _from_class(of, o_ref, d, hp)
        lnat = pltpu.einshape("(rt)l->(tr)l", lf[...], r=d)

        @pl.when(hh == 0)
        def _():
            l_ref[...] = lnat

        @pl.when(hh > 0)
        def _():
            l_ref[...] += lnat

        kst[hh] = kf[...]
        vst[hh] = vf[...]

    def cols(cb):
        return pl.BlockSpec((T, cw), lambda i, hh: (i, cb * nh + hh))

    tile = pltpu.VMEM((hp, T, HEAD), BF16)
    return pl.pallas_call(
        body, name=f"attn_fwd_dilated_{d}",
        grid=(S // T, nh),
        in_specs=[cols(qcb), cols(kcb), cols(vcb)],
        out_specs=[cols(0), pl.BlockSpec((T, 128), lambda i, hh: (i, 0))],
        out_shape=[jax.ShapeDtypeStruct((S, WIDTH), BF16), jax.ShapeDtypeStruct((S, 128), F32)],
        scratch_shapes=[tile, tile, tile, tile, tile,
                        pltpu.VMEM((nh, hp, T, HEAD), BF16), pltpu.VMEM((nh, hp, T, HEAD), BF16),
                        pltpu.VMEM((hp, T, HEAD), F32), pltpu.VMEM((T, 128), F32),
                        pltpu.VMEM((nblocks * CHUNK, CHUNK), F32), pltpu.VMEM((nblocks * CHUNK, CHUNK), F32),
                        pltpu.VMEM((nblocks * CHUNK, CHUNK), BF16), pltpu.VMEM((nblocks * CHUNK, CHUNK), BF16)],
        compiler_params=_params(("arbitrary", "arbitrary")),
    )(proj, proj, proj)


def _attn_bwd_dilated(proj, qcb, kcb, vcb, do, lse, delta, d):
    S = proj.shape[0]
    T = CHUNK * d
    nt = S // T
    hp, nh, cw = _dilated_split(d)
    nblocks = d * hp

    def body(q_ref, k_ref, v_ref, do_ref, l_ref, d_ref, dq_ref, dk_ref, dv_ref,
             qf, dof, kf, vf, kpf, vpf, dqf, acck, accv, newk, newv,
             sc_s, sp_s, dpc_s, dpp_s, pc_s, pp_s, dsc_s, dsp_s):
        hh, i = pl.program_id(0), pl.program_id(1)

        @pl.when(i == 0)
        def _():
            for ref in (kpf, vpf, acck, accv):
                ref[...] = jnp.zeros_like(ref)
            dk_ref[...] = jnp.zeros_like(dk_ref)
            dv_ref[...] = jnp.zeros_like(dv_ref)

        def blk(ref, r, j):
            return ref[j, r * CHUNK:(r + 1) * CHUNK, :]

        def bs(r, j):
            return slice((r * hp + j) * CHUNK, (r * hp + j + 1) * CHUNK)

        @pl.when(i < nt)
        def _():
            _by_class(q_ref, qf, d, hp)
            _by_class(do_ref, dof, d, hp)
            _by_class(k_ref, kf, d, hp)
            _by_class(v_ref, vf, d, hp)
            lses, deltas = [], []
            lcls = pltpu.einshape("(tr)l->(rt)l", l_ref[...], r=d)
            dcls = pltpu.einshape("(tr)l->(rt)l", d_ref[...], r=d)
            for r in range(d):
                lblk = lcls[r * CHUNK:(r + 1) * CHUNK]
                dblk = dcls[r * CHUNK:(r + 1) * CHUNK]
                for j in range(hp):
                    lses.append(_col(lblk, hh * hp + j))
                    deltas.append(_col(dblk, hh * hp + j))
                    qb, dob = blk(qf, r, j), blk(dof, r, j)
                    sc_s[bs(r, j), :] = _dot_nt(qb, blk(kf, r, j))
                    dpc_s[bs(r, j), :] = _dot_nt(dob, blk(vf, r, j))
                    sp_s[bs(r, j), :] = _dot_nt(qb, blk(kpf, r, j))
                    dpp_s[bs(r, j), :] = _dot_nt(dob, blk(vpf, r, j))
            lse = jnp.concatenate(lses, axis=0)
            delta = jnp.concatenate(deltas, axis=0)
            row = lax.broadcasted_iota(jnp.int32, (nblocks * CHUNK, CHUNK), 0) & (CHUNK - 1)
            col = lax.broadcasted_iota(jnp.int32, (nblocks * CHUNK, CHUNK), 1)
            sp = jnp.where(col >= row, sp_s[...] * ATT_SCALE, NEG) + jnp.where(i > 0, 0.0, NEG)
            pp = jnp.exp(sp - lse)
            pp_s[...] = pp.astype(BF16)
            dsp_s[...] = (pp * (dpp_s[...] - delta)).astype(BF16)
            sc = jnp.where(col <= row, sc_s[...] * ATT_SCALE, NEG)
            pc = jnp.exp(sc - lse)
            pc_s[...] = pc.astype(BF16)
            dsc_s[...] = (pc * (dpc_s[...] - delta)).astype(BF16)
            for r in range(d):
                rows = slice(r * CHUNK, (r + 1) * CHUNK)
                for j in range(hp):
                    qb, dob = blk(qf, r, j), blk(dof, r, j)
                    dsc, dsp = dsc_s[bs(r, j), :], dsp_s[bs(r, j), :]
                    dqf[j, rows, :] = (_dot(dsc, blk(kf, r, j)) + _dot(dsp, blk(kpf, r, j))) * ATT_SCALE
                    newk[j, rows, :] = _dot_tn(dsc, qb) * ATT_SCALE
                    newv[j, rows, :] = _dot_tn(pc_s[bs(r, j), :], dob)
                    acck[j, rows, :] += _dot_tn(dsp, qb) * ATT_SCALE
                    accv[j, rows, :] += _dot_tn(pp_s[bs(r, j), :], dob)
            _from_class(dqf, dq_ref, d, hp)

        @pl.when(i > 0)
        def _():
            _from_class(acck, dk_ref, d, hp)
            _from_class(accv, dv_ref, d, hp)

        @pl.when(i < nt)
        def _():
            acck[...] = newk[...]
            accv[...] = newv[...]
            kpf[...] = kf[...]
            vpf[...] = vf[...]

    def cur(width, cb, nsplit):
        return pl.BlockSpec((T, width), lambda hh, i: (jnp.minimum(i, nt - 1), cb * nsplit + hh * (nsplit > 1)))

    def lag():
        return pl.BlockSpec((T, cw), lambda hh, i: (jnp.maximum(i - 1, 0), hh))

    tile = pltpu.VMEM((hp, T, HEAD), BF16)
    acc = pltpu.VMEM((hp, T, HEAD), F32)
    f32s = pltpu.VMEM((nblocks * CHUNK, CHUNK), F32)
    b16s = pltpu.VMEM((nblocks * CHUNK, CHUNK), BF16)
    return pl.pallas_call(
        body, name=f"attn_bwd_dilated_{d}",
        grid=(nh, nt + 1),
        in_specs=[cur(cw, qcb, nh), cur(cw, kcb, nh), cur(cw, vcb, nh), cur(cw, 0, nh), cur(128, 0, 1), cur(128, 0, 1)],
        out_specs=[cur(cw, 0, nh), lag(), lag()],
        out_shape=[jax.ShapeDtypeStruct((S, WIDTH), BF16)] * 3,
        scratch_shapes=[tile] * 6 + [acc] * 5 + [f32s] * 4 + [b16s] * 4,
        compiler_params=_params(("arbitrary", "arbitrary")),
    )(proj, proj, proj, do, lse, delta)


def _abm_bwd(proj, cdf, dy3, ln_g, ln_b, wsm, wsm_t, bias_full, pool_w, pool_wt, pool_scale, kv):
    S = proj.shape[0]
    tm = 512
    nchunk = tm // CHUNK
    nblk = S // CHUNK

    def body(u_ref, v_ref, ag_ref, p_ref, ph_ref, pg_ref, pgn_ref, mq_ref, mg_ref, cdf_ref, dy_ref, dypn_ref,
             lng_ref, lnb_ref, wsm_ref, wsmt_ref, bias_ref, pw_ref, pwt_ref, ps_ref, kv_ref,
             dab_ref, dm_ref, dlng_ref, dlnb_ref, dws_ref, dbias_ref, dpw_ref, dps_ref, dkv_ref,
             mix, dvl, ddn):
        i = pl.program_id(0)

        @pl.when(i == 0)
        def _():
            for r in (dlng_ref, dlnb_ref, dws_ref, dbias_ref, dpw_ref, dps_ref, dkv_ref):
                r[...] = jnp.zeros_like(r)

        au = u_ref[...].astype(F32)
        av = v_ref[...].astype(F32)
        ag = ag_ref[...].astype(F32)
        u, du = _gelu_and_grad(au, cdf_ref[0].astype(F32))
        v, dgelu_v = _gelu_and_grad(av, cdf_ref[1].astype(F32))
        vhat, rstd = _layer_norm_fwd(v)
        vln = (vhat * lng_ref[...] + lnb_ref[...]).astype(BF16)
        for c in range(nchunk):
            for h in range(N_HEAD):
                rs, cs = slice(c * CHUNK, (c + 1) * CHUNK), slice(h * HEAD, (h + 1) * HEAD)
                mix[rs, cs] = _dot(wsm_ref[h], vln[rs, cs]) + bias_ref[:, cs]
        dya = dy_ref[0].astype(F32)
        sg, dsg = _silu_and_grad(ag)
        mixed = mix[...]
        dab_ref[:, 2 * WIDTH:3 * WIDTH] = (dya * u * mixed * dsg).astype(BF16)
        dab_ref[:, 0:WIDTH] = (dya * mixed * sg * du).astype(BF16)
        dmixed = dya * u * sg
        dmb = dmixed.astype(BF16)
        tril = (lax.broadcasted_iota(jnp.int32, (CHUNK, CHUNK), 1)
                <= lax.broadcasted_iota(jnp.int32, (CHUNK, CHUNK), 0))
        for c in range(nchunk):
            rs = slice(c * CHUNK, (c + 1) * CHUNK)
            dbias_ref[...] += dmixed[rs, :]
            for h in range(N_HEAD):
                cs = slice(h * HEAD, (h + 1) * HEAD)
                dvl[rs, cs] = _dot(wsmt_ref[h], dmb[rs, cs])
                dws_ref[h] += jnp.where(tril, _dot_nt(dmb[rs, cs], vln[rs, cs]), 0.0)
        dvln = dvl[...]
        dlng_ref[...] += jnp.sum(dvln * vhat, axis=0, keepdims=True)
        dlnb_ref[...] += jnp.sum(dvln, axis=0, keepdims=True)
        dvh = dvln * lng_ref[...]
        dv = rstd * (dvh - jnp.mean(dvh, axis=-1, keepdims=True)
                     - vhat * jnp.mean(dvh * vhat, axis=-1, keepdims=True))
        dab_ref[:, WIDTH:2 * WIDTH] = (dv * dgelu_v).astype(BF16)

        halo_ok = (i > 0).astype(F32)
        for c in range(nchunk):
            rs = slice(c * CHUNK, (c + 1) * CHUNK)
            for g, win in enumerate(POOL_WINDOWS):
                cs = slice(g * HEAD, (g + 1) * HEAD)
                bcur, bprev = _band_masks(win)
                cur = p_ref[rs, cs]
                if c == 0:
                    prev = (ph_ref[:, cs].astype(F32) * halo_ok).astype(BF16)
                else:
                    prev = p_ref[(c - 1) * CHUNK:c * CHUNK, cs]
                sums = _dot(bcur, cur) + _dot(bprev, prev)
                dvl[rs, cs] = sums * _inv_count(i * tm + c * CHUNK, win) - cur.astype(F32)
        dmat = dvl[...].astype(BF16)
        for g in range(4):
            cs = slice(g * HEAD, (g + 1) * HEAD)
            mix[:, cs] = _dot(dmat[:, cs], pw_ref[g])
        yg = mix[...]
        pg = pg_ref[...].astype(F32)
        dyp = dy_ref[1].astype(F32)
        spg, dspg = _silu_and_grad(pg)
        dyy = dyp * spg
        scale = ps_ref[...]
        dab_ref[:, 4 * WIDTH:5 * WIDTH] = (dyp * yg * scale * dspg).astype(BF16)
        dps_ref[...] += jnp.sum(dyy * yg, axis=0, keepdims=True)
        dyg = (dyy * scale).astype(BF16)
        for g in range(4):
            cs = slice(g * HEAD, (g + 1) * HEAD)
            dpw_ref[g] += _dot_tn(dmat[:, cs], dyg[:, cs])
            mix[:, cs] = _dot(dyg[:, cs], pwt_ref[g])
        next_ok = (i + 1 < S // tm).astype(F32)
        dygn = (dypn_ref[...].astype(F32) * _silu(pgn_ref[...].astype(F32)) * scale * next_ok).astype(BF16)
        for c in range(nchunk + 1):
            for g, win in enumerate(POOL_WINDOWS):
                cs = slice(g * HEAD, (g + 1) * HEAD)
                if c < nchunk:
                    dd = mix[c * CHUNK:(c + 1) * CHUNK, cs]
                else:
                    dd = _dot(dygn[:, cs], pwt_ref[g])
                ddn[c * CHUNK:(c + 1) * CHUNK, cs] = dd * _inv_count(i * tm + c * CHUNK, win)
        ddnb = ddn[...].astype(BF16)
        for c in range(nchunk):
            rs = slice(c * CHUNK, (c + 1) * CHUNK)
            ns = slice((c + 1) * CHUNK, (c + 2) * CHUNK)
            for g, win in enumerate(POOL_WINDOWS):
                cs = slice(g * HEAD, (g + 1) * HEAD)
                bcur, bprev = _band_masks(win)
                dp = _dot_tn(bcur, ddnb[rs, cs]) + _dot_tn(bprev, ddnb[ns, cs]) - mix[rs, cs]
                dab_ref[rs, 3 * WIDTH + g * HEAD:3 * WIDTH + (g + 1) * HEAD] = dp.astype(BF16)

        mg = mg_ref[...].astype(F32)
        dym = dy_ref[2].astype(F32)
        smg, dsmg = _silu_and_grad(mg)
        dob = (dym * smg).astype(BF16)
        for h in range(N_HEAD):
            cs = slice(h * HEAD, (h + 1) * HEAD)
            vs = slice(WIDTH + h * HEAD, WIDTH + (h + 1) * HEAD)
            qh = mq_ref[:, cs]
            p = _mem_softmax(qh, kv_ref[:, cs])
            pb = p.astype(BF16)
            mix[:, cs] = _dot(pb, kv_ref[:, vs])
            dp = _dot_nt(dob[:, cs], kv_ref[:, vs])
            ds = (p * (dp - jnp.sum(p * dp, axis=-1, keepdims=True))).astype(BF16)
            dm_ref[:, cs] = (_dot(ds, kv_ref[:, cs]) * ATT_SCALE).astype(BF16)
            dkv_ref[:, cs] += _dot_tn(ds, qh) * ATT_SCALE
            dkv_ref[:, vs] += _dot_tn(pb, dob[:, cs])
        dm_ref[:, WIDTH:2 * WIDTH] = (dym * mix[...] * dsmg).astype(BF16)

    blk = tm // CHUNK
    small = [_full((1, WIDTH)), _full((1, WIDTH)), _full((N_HEAD, CHUNK, CHUNK)), _full((CHUNK, WIDTH)),
             _full((4, HEAD, HEAD)), _full((1, WIDTH)), _full((MEM_LEN, 2 * WIDTH))]
    return pl.pallas_call(
        body, name="abm_bwd",
        grid=(S // tm,),
        in_specs=[_rows(tm, WIDTH, CB_U), _rows(tm, WIDTH, CB_V), _rows(tm, WIDTH, CB_AGATE),
                  _rows(tm, WIDTH, CB_PIN),
                  pl.BlockSpec((CHUNK, WIDTH), lambda i: (jnp.maximum(i * blk - 1, 0), CB_PIN)),
                  _rows(tm, WIDTH, CB_PGATE),
                  pl.BlockSpec((CHUNK, WIDTH), lambda i: (jnp.minimum(i * blk + blk, nblk - 1), CB_PGATE)),
                  _rows(tm, WIDTH, CB_MQ), _rows(tm, WIDTH, CB_MGATE),
                  pl.BlockSpec((2, tm, WIDTH), lambda i: (0, i, 0)),
                  pl.BlockSpec((3, tm, WIDTH), lambda i: (0, i, 0)),
                  pl.BlockSpec((None, CHUNK, WIDTH), lambda i: (1, jnp.minimum(i * blk + blk, nblk - 1), 0)),
                  _full((1, WIDTH)), _full((1, WIDTH)), _full((N_HEAD, CHUNK, CHUNK)), _full((N_HEAD, CHUNK, CHUNK)),
                  _full((CHUNK, WIDTH)), _full((4, HEAD, HEAD)), _full((4, HEAD, HEAD)), _full((1, WIDTH)),
                  _full((MEM_LEN, 2 * WIDTH))],
        out_specs=[_rows(tm, 5 * WIDTH), _rows(tm, 2 * WIDTH)] + small,
        out_shape=[jax.ShapeDtypeStruct((S, D_BRANCHES), BF16), jax.ShapeDtypeStruct((S, 2 * WIDTH), BF16),
                   jax.ShapeDtypeStruct((1, WIDTH), F32), jax.ShapeDtypeStruct((1, WIDTH), F32),
                   jax.ShapeDtypeStruct((N_HEAD, CHUNK, CHUNK), F32), jax.ShapeDtypeStruct((CHUNK, WIDTH), F32),
                   jax.ShapeDtypeStruct((4, HEAD, HEAD), F32), jax.ShapeDtypeStruct((1, WIDTH), F32),
                   jax.ShapeDtypeStruct((MEM_LEN, 2 * WIDTH), F32)],
        scratch_shapes=[pltpu.VMEM((tm, WIDTH), F32), pltpu.VMEM((tm, WIDTH), F32),
                        pltpu.VMEM((tm + CHUNK, WIDTH), F32)],
        compiler_params=_params(("arbitrary",)),
    )(proj, proj, proj, proj, proj, proj, proj, proj, proj, cdf, dy3, dy3,
      ln_g, ln_b, wsm, wsm_t, bias_full, pool_w, pool_wt, pool_scale, kv)


def _bias_reduce(dbias_full):
    def body(d_ref, o_ref):
        d = d_ref[...]
        o_ref[...] = _put_cols([jnp.sum(d[:, h * HEAD:(h + 1) * HEAD], axis=1, keepdims=True) for h in range(N_HEAD)])

    return pl.pallas_call(body, name="bias_reduce", out_shape=jax.ShapeDtypeStruct((CHUNK, 128), F32))(dbias_full)


def _mem_bwd(mem, g, mem_n, w, dkv):
    def body(m_ref, g_ref, mn_ref, w_ref, dkv_ref, dw_ref, dg_ref):
        dkvb = dkv_ref[...].astype(BF16)
        dw_ref[...] = _dot_tn(mn_ref[...], dkvb).astype(BF16)
        dmn = _dot_nt(dkvb, w_ref[...])
        xf = m_ref[...]
        r = lax.rsqrt(jnp.mean(xf * xf, axis=-1, keepdims=True) + EPS)
        dg_ref[...] = jnp.sum(dmn * xf * r, axis=0, keepdims=True)

    return pl.pallas_call(
        body, name="mem_bwd",
        out_shape=[jax.ShapeDtypeStruct((D_MODEL, 2 * WIDTH), BF16), jax.ShapeDtypeStruct((1, D_MODEL), F32)],
        compiler_params=pltpu.CompilerParams(vmem_limit_bytes=VMEM_LIMIT),
    )(mem, g, mem_n, w, dkv)


def _dh_bwd(dpb, dpg, wt, wgt, x, g, dxo, parts=()):
    S = x.shape[0]
    tm, tkb, tkg = 1024, D_BRANCHES // 4, D_GATES // 4
    nkb, nkg = 4, 4
    nk = nkb + nkg
    ni = S // tm
    n = len(parts)

    def body(dpb_ref, wbr_ref, dpg_ref, wg_ref, x_ref, g_ref, dxo_ref, *rest):
        p_in = rest[:n]
        dx_ref, dg_ref = rest[n:n + 2]
        p_out, acc, sems = rest[n + 2:2 * n + 2], rest[2 * n + 2], rest[2 * n + 3:]
        i, kk = pl.program_id(0), pl.program_id(1)

        @pl.when(jnp.logical_and(i == 0, kk == 0))
        def _():
            dg_ref[...] = jnp.zeros_like(dg_ref)
            if n:
                _comm_start(_rs_second(p_in, p_out, *sems))

        @pl.when(kk == 0)
        def _():
            acc[...] = jnp.zeros_like(acc)

        @pl.when(kk < nkb)
        def _():
            acc[...] += _dot(dpb_ref[...], wbr_ref[...])

        @pl.when(kk >= nkb)
        def _():
            acc[...] += _dot(dpg_ref[...], wg_ref[...])

        @pl.when(kk == nk - 1)
        def _():
            xf = x_ref[...]
            r = lax.rsqrt(jnp.mean(xf * xf, axis=-1, keepdims=True) + EPS)
            xhat = xf * r
            dh = acc[...]
            dg_ref[...] += jnp.sum(dh * xhat, axis=0, keepdims=True)
            dxh = dh * g_ref[...]
            dx_ref[...] = dxo_ref[...] + r * (dxh - xhat * jnp.mean(dxh * xhat, axis=-1, keepdims=True))

        if n:
            @pl.when(jnp.logical_and(i == ni - 1, kk == nk - 1))
            def _():
                _comm_wait(_rs_second(p_in, p_out, *sems))

    res = pl.pallas_call(
        body, name="dh_bwd_scatter" if n else "dh_bwd",
        grid=(ni, nk),
        in_specs=[pl.BlockSpec((tm, tkb), lambda i, k: (i, jnp.minimum(k, nkb - 1))),
                  pl.BlockSpec((tkb, D_MODEL), lambda i, k: (jnp.minimum(k, nkb - 1), 0)),
                  pl.BlockSpec((tm, tkg), lambda i, k: (i, jnp.maximum(k - nkb, 0))),
                  pl.BlockSpec((tkg, D_MODEL), lambda i, k: (jnp.maximum(k - nkb, 0), 0)),
                  pl.BlockSpec((tm, D_MODEL), lambda i, k: (i, 0)), pl.BlockSpec((1, D_MODEL), lambda i, k: (0, 0)),
                  pl.BlockSpec((tm, D_MODEL), lambda i, k: (i, 0))] + [ANY] * n,
        out_specs=[pl.BlockSpec((tm, D_MODEL), lambda i, k: (i, 0)), pl.BlockSpec((1, D_MODEL), lambda i, k: (0, 0))]
                  + [ANY] * n,
        out_shape=[jax.ShapeDtypeStruct((S, D_MODEL), F32), jax.ShapeDtypeStruct((1, D_MODEL), F32)]
                  + [jax.ShapeDtypeStruct(p.shape, p.dtype) for p in parts],
        scratch_shapes=[pltpu.VMEM((tm, D_MODEL), F32)] + (_dma_sems(3 * n, 3 * n, n) if n else []),
        compiler_params=_params(("arbitrary", "arbitrary")),
    )(dpb, wt, dpg, wgt, x, g, dxo, *parts)
    return res[0], res[1], list(res[2:])


def _dw_in(h, dpb, dpg, parts=()):
    S = h.shape[0]
    tk = 2048
    nk = S // tk
    n = len(parts)
    tmb = D_BRANCHES // 4
    ng = D_GATES // GATE_TILE

    def accumulate(a_ref, h_ref, o_ref, acc):
        kk = pl.program_id(1)

        @pl.when(kk == 0)
        def _():
            acc[...] = jnp.zeros_like(acc)

        acc[...] += _dot_tn(a_ref[...], h_ref[...])

        @pl.when(kk == nk - 1)
        def _():
            o_ref[...] = acc[...].astype(BF16)

    def branches(a_ref, h_ref, *rest):
        p_in, o_ref, p_out = rest[:n], rest[n], rest[n + 1:2 * n + 1]
        acc, sems = rest[2 * n + 1], rest[2 * n + 2:]
        i, kk = pl.program_id(0), pl.program_id(1)

        if n:
            @pl.when(jnp.logical_and(i == 0, kk == 0))
            def _():
                _comm_start(_rs_second(p_in, p_out, *sems))

        accumulate(a_ref, h_ref, o_ref, acc)

        if n:
            @pl.when(jnp.logical_and(i == 3, kk == nk - 1))
            def _():
                _comm_wait(_rs_second(p_in, p_out, *sems))

    def gates(a_ref, h_ref, dst_ref, o_ref, acc):
        accumulate(a_ref, h_ref, o_ref, acc)

    res = pl.pallas_call(
        branches, name="dw_in_branches_scatter" if n else "dw_in_branches",
        grid=(4, nk),
        in_specs=[pl.BlockSpec((tk, tmb), lambda i, k: (k, i)), pl.BlockSpec((tk, D_MODEL), lambda i, k: (k, 0))]
                 + [ANY] * n,
        out_specs=[pl.BlockSpec((tmb, D_MODEL), lambda i, k: (i, 0))] + [ANY] * n,
        out_shape=[jax.ShapeDtypeStruct((D_IN, D_MODEL), BF16)]
                  + [jax.ShapeDtypeStruct(p.shape, p.dtype) for p in parts],
        scratch_shapes=[pltpu.VMEM((tmb, D_MODEL), F32)] + (_dma_sems(3 * n, 3 * n, n) if n else []),
        compiler_params=_params(("arbitrary", "arbitrary")),
    )(dpb, h, *parts)
    dwt = pl.pallas_call(
        gates, name="dw_in_gates",
        grid=(ng, nk),
        in_specs=[pl.BlockSpec((tk, GATE_TILE), lambda i, k: (k, i)), pl.BlockSpec((tk, D_MODEL), lambda i, k: (k, 0)),
                  ANY],
        out_specs=pl.BlockSpec((GATE_TILE, D_MODEL), lambda i, k: (D_BRANCHES // GATE_TILE + i, 0)),
        out_shape=jax.ShapeDtypeStruct((D_IN, D_MODEL), BF16),
        input_output_aliases={2: 0},
        scratch_shapes=[pltpu.VMEM((GATE_TILE, D_MODEL), F32)],
        compiler_params=_params(("parallel", "arbitrary")),
    )(dpg, h, res[0])
    return dwt, list(res[1:])


def _row_tile(R, C, block_bytes=2 << 20):
    for cand in range(min(R, block_bytes // (C * 4)) // 8 * 8, 0, -8):
        if R % cand == 0:
            return cand
    return R


def _adamw_update(p_ref, w_ref, m_ref, v_ref, g_ref, d_ref, nm_ref, nv_ref):
    c1 = 1.0 / (1.0 - ADAM_B1 ** ADAM_STEP)
    c2 = 1.0 / (1.0 - ADAM_B2 ** ADAM_STEP)
    g = p_ref[0].astype(F32)
    for k in range(1, p_ref.shape[0]):
        g = g + p_ref[k].astype(F32)
    nm = ADAM_B1 * m_ref[...] + (1.0 - ADAM_B1) * g
    nv = ADAM_B2 * v_ref[...] + (1.0 - ADAM_B2) * (g * g)
    g_ref[...] = g
    nm_ref[...] = nm
    nv_ref[...] = nv
    d_ref[...] = -ADAM_LR * ((nm * c1) / (jnp.sqrt(nv * c2) + ADAM_EPS) + ADAM_WD * w_ref[...])


def _adamw(parts, w, m, v, name):
    P, R, C = parts.shape
    tr = _row_tile(R, C)

    def body(*refs):
        _adamw_update(*refs)

    spec = pl.BlockSpec((tr, C), lambda i: (i, 0))
    return pl.pallas_call(
        body, name=name,
        grid=(R // tr,),
        in_specs=[pl.BlockSpec((P, tr, C), lambda i: (0, i, 0)), spec, spec, spec],
        out_specs=[spec] * 4,
        out_shape=[jax.ShapeDtypeStruct((R, C), F32)] * 4,
        compiler_params=_params(("parallel",)),
    )(parts, w, m, v)


def _adamw_layers(parts, w, m, v, name):
    depth = len(parts)
    P, R, C = parts[0].shape
    tr = _row_tile(R, C, 1 << 20)

    def body(*refs):
        layer = pl.program_id(0)
        for k in range(depth):
            @pl.when(layer == k)
            def _(k=k):
                _adamw_update(refs[k], *refs[depth:])

    def part_spec(k):
        return pl.BlockSpec((P, tr, C), lambda l, i: (0, jnp.where(l == k, i, 0), 0))

    spec = pl.BlockSpec((None, tr, C), lambda l, i: (l, i, 0))
    return pl.pallas_call(
        body, name=name,
        grid=(depth, R // tr),
        in_specs=[part_spec(k) for k in range(depth)] + [spec] * 3,
        out_specs=[spec] * 4,
        out_shape=[jax.ShapeDtypeStruct((depth, R, C), F32)] * 4,
        compiler_params=_params(("arbitrary", "arbitrary")),
    )(*parts, w, m, v)


def _place():
    return lax.axis_index("x"), lax.axis_index("y"), lax.axis_index("c")


def _all_gather(shards):
    n = len(shards)

    def body(*refs):
        ins, outs = refs[:n], refs[n:2 * n]
        send1, recv1, local_sems, send2, recv2 = refs[2 * n:]
        first = _ag_first(ins, outs, send1, recv1, local_sems)
        second = _ag_second(outs, send2, recv2)
        _comm_start(first)
        for j in range(3):
            for a in range(n):
                first[2][4 * a + 1 + j].wait_recv()
            for a in range(n):
                second[1][3 * a + j].start()
        for a in range(n):
            first[2][4 * a].wait_recv()
        for cp in second[2]:
            cp.wait_recv()
        for cp in first[1] + second[1]:
            cp.wait_send()
        for cp in first[0]:
            cp.wait()

    return pl.pallas_call(
        body, name="weights_all_gather",
        in_specs=[ANY] * n, out_specs=[ANY] * n,
        out_shape=[jax.ShapeDtypeStruct((N_DEV,) + s.shape, s.dtype) for s in shards],
        scratch_shapes=_dma_sems(4 * n, 4 * n, n, 3 * n, 3 * n),
        compiler_params=pltpu.CompilerParams(has_side_effects=True),
    )(*shards)


N_BIG = 4


def _dev(p):
    return 4 * p[0] + 2 * p[1] + p[2]


def _other_chips(x, y):
    return [(1 - x, y), (x, 1 - y), (1 - x, 1 - y)]


def _remote(src, dst, send_sems, recv_sems, k, to):
    return pltpu.make_async_remote_copy(src_ref=src, dst_ref=dst, send_sem=send_sems.at[k], recv_sem=recv_sems.at[k],
                                        device_id=to, device_id_type=MESH)


def _ag_first(ins, outs, send_sems, recv_sems, local_sems):
    x, y, c = _place()
    me = (x, y, c)
    targets = [(x, y, 1 - c)] + [(*chip, c) for chip in _other_chips(x, y)]
    local, out, inc = [], [], []
    for a in range(len(ins)):
        local.append(pltpu.make_async_copy(ins[a], outs[a].at[_dev(me)], local_sems.at[a]))
        for k, to in enumerate(targets):
            out.append(_remote(ins[a], outs[a].at[_dev(me)], send_sems, recv_sems, 4 * a + k, to))
            inc.append(_remote(ins[a], outs[a].at[_dev(to)], send_sems, recv_sems, 4 * a + k, to))
    return local, out, inc


def _ag_second(bufs, send_sems, recv_sems):
    x, y, c = _place()
    out, inc = [], []
    for a in range(len(bufs)):
        for j, chip in enumerate(_other_chips(x, y)):
            mine, theirs = bufs[a].at[_dev((*chip, c))], bufs[a].at[_dev((*chip, 1 - c))]
            out.append(_remote(mine, mine, send_sems, recv_sems, 3 * a + j, (x, y, 1 - c)))
            inc.append(_remote(theirs, theirs, send_sems, recv_sems, 3 * a + j, (x, y, 1 - c)))
    return [], out, inc


def _rs_first(ins, outs, send_sems, recv_sems):
    x, y, c = _place()
    out = [_remote(ins[a].at[j, 1 - c], outs[a].at[j], send_sems, recv_sems, N_CHIP * a + j, (x, y, 1 - c))
           for a in range(len(ins)) for j in range(N_CHIP)]
    return [], out, out


def _rs_second(ins, outs, send_sems, recv_sems, local_sems):
    x, y, c = _place()
    my_chip = 2 * x + y
    local, out, inc = [], [], []
    for a in range(len(ins)):
        local.append(pltpu.make_async_copy(ins[a].at[my_chip], outs[a].at[my_chip], local_sems.at[a]))
        for k, (ox, oy) in enumerate(_other_chips(x, y)):
            out.append(_remote(ins[a].at[2 * ox + oy], outs[a].at[my_chip], send_sems, recv_sems, 3 * a + k, (ox, oy, c)))
            inc.append(_remote(ins[a].at[2 * ox + oy], outs[a].at[2 * ox + oy], send_sems, recv_sems, 3 * a + k,
                               (ox, oy, c)))
    return local, out, inc


def _comm_start(exchange):
    local, out, _ = exchange
    for cp in local + out:
        cp.start()


def _comm_wait(exchange):
    local, out, inc = exchange
    for cp in inc:
        cp.wait_recv()
    for cp in out:
        cp.wait_send()
    for cp in local:
        cp.wait()


def _dma_sems(*counts):
    return [pltpu.SemaphoreType.DMA((n,)) for n in counts]


def _rs_sibling(grads):
    n = len(grads)

    def body(*refs):
        ex = _rs_first(refs[:n], refs[n:2 * n], *refs[2 * n:])
        _comm_start(ex)
        _comm_wait(ex)

    return pl.pallas_call(
        body, name="grads_to_sibling",
        in_specs=[ANY] * n, out_specs=[ANY] * n,
        out_shape=[jax.ShapeDtypeStruct(g.shape[:1] + g.shape[2:], g.dtype) for g in grads],
        scratch_shapes=_dma_sems(N_CHIP * n, N_CHIP * n),
        compiler_params=pltpu.CompilerParams(has_side_effects=True),
    )(*grads)


def _pair_sum(grads, recvs):
    n = len(grads)

    def body(c_ref, *refs):
        for a in range(n):
            refs[2 * n + a][...] = (refs[a][...].astype(F32) + refs[n + a][...].astype(F32)).astype(BF16)

    def g_spec(g):
        return pl.BlockSpec((None, None) + g.shape[2:], lambda j, c_ref: (j, c_ref[0], 0, 0))

    def r_spec(r):
        return pl.BlockSpec((None,) + r.shape[1:], lambda j, c_ref: (j, 0, 0))

    return pl.pallas_call(
        body, name="pair_sum",
        grid_spec=pltpu.PrefetchScalarGridSpec(
            num_scalar_prefetch=1, grid=(N_CHIP,),
            in_specs=[g_spec(g) for g in grads] + [r_spec(r) for r in recvs],
            out_specs=[r_spec(r) for r in recvs]),
        out_shape=[jax.ShapeDtypeStruct(r.shape, BF16) for r in recvs],
        compiler_params=_params(("parallel",)),
    )(lax.axis_index("c").reshape(1).astype(jnp.int32), *grads, *recvs)


SMALL_ROWS = 544


def _all_reduce_small(buf, parts=()):
    n = len(parts)

    def body(in_ref, *rest):
        p_in, out_ref, p_out = rest[:n], rest[n], rest[n + 1:2 * n + 1]
        recv, acc, send1, recv1, send2, recv2 = rest[2 * n + 1:2 * n + 7]
        scatter_sems = rest[2 * n + 7:]
        x, y, c = _place()
        me = 4 * x + 2 * y + c
        peers = [(x ^ (r >> 2), y ^ ((r >> 1) & 1), c ^ (r & 1)) for r in range(1, N_DEV)]

        def idx(p):
            return 4 * p[0] + 2 * p[1] + p[2]

        if n:
            _comm_start(_rs_second(p_in, p_out, *scatter_sems))
        first = [pltpu.make_async_remote_copy(
            src_ref=in_ref.at[idx(p)], dst_ref=recv.at[me], send_sem=send1.at[r], recv_sem=recv1.at[r],
            device_id=p, device_id_type=MESH) for r, p in enumerate(peers)]
        for cp in first:
            cp.start()
        recv[me] = in_ref[me]
        for r, p in enumerate(peers):
            pltpu.make_async_remote_copy(
                src_ref=in_ref.at[idx(p)], dst_ref=recv.at[idx(p)], send_sem=send1.at[r], recv_sem=recv1.at[r],
                device_id=p, device_id_type=MESH).wait_recv()
        total = recv[0]
        for k in range(1, N_DEV):
            total = total + recv[k]
        acc[...] = total
        out_ref[me] = total
        second = [pltpu.make_async_remote_copy(
            src_ref=acc, dst_ref=out_ref.at[me], send_sem=send2.at[r], recv_sem=recv2.at[r],
            device_id=p, device_id_type=MESH) for r, p in enumerate(peers)]
        for cp in second:
            cp.start()
        for r, p in enumerate(peers):
            pltpu.make_async_remote_copy(
                src_ref=acc, dst_ref=out_ref.at[idx(p)], send_sem=send2.at[r], recv_sem=recv2.at[r],
                device_id=p, device_id_type=MESH).wait_recv()
        for cp in first + second:
            cp.wait_send()
        if n:
            _comm_wait(_rs_second(p_in, p_out, *scatter_sems))

    vm = pl.BlockSpec(memory_space=pltpu.VMEM)
    res = pl.pallas_call(
        body, name="small_grads_all_reduce",
        in_specs=[vm] + [ANY] * n, out_specs=[vm] + [ANY] * n,
        out_shape=[jax.ShapeDtypeStruct(buf.shape, F32)] + [jax.ShapeDtypeStruct(p.shape, p.dtype) for p in parts],
        scratch_shapes=[pltpu.VMEM(buf.shape, F32), pltpu.VMEM(buf.shape[1:], F32)] + _dma_sems(7, 7, 7, 7)
                       + (_dma_sems(3 * n, 3 * n, n) if n else []),
        compiler_params=pltpu.CompilerParams(has_side_effects=True, vmem_limit_bytes=VMEM_LIMIT),
    )(buf, *parts)
    return res[0], list(res[1:])


def _dilate(a, d):
    if d == 1:
        return a
    S, C = a.shape
    return a.reshape(S // d, d, C).transpose(1, 0, 2).reshape(S, C)


def _undilate(a, d):
    if d == 1:
        return a
    S, C = a.shape
    return a.reshape(d, S // d, C).transpose(1, 0, 2).reshape(S, C)


def _cols(a, cb, n=1):
    return a[:, cb * WIDTH:(cb + n) * WIDTH]


def _to_blocks(g, kind):
    if kind == "rows":
        C = g.shape[1]
        return g.reshape(N_CHIP, 2, -1, C)
    return g.reshape(4 * WIDTH, N_CHIP, 2, -1).transpose(1, 2, 0, 3)


SMALL = ("norm_g", "gm_ln_g", "gm_ln_b", "gm_ws", "gm_bs", "pool_w", "pool_scale", "mem_norm_g", "final_norm_g")


def _pack_small(tree):
    flat = jnp.concatenate([tree[k].reshape(-1, 128) for k in SMALL], axis=0)
    return jnp.pad(flat, ((0, N_DEV * SMALL_ROWS - flat.shape[0]), (0, 0)))


def _unpack_small(flat, like):
    out, at = {}, 0
    for k in SMALL:
        rows = like[k].size // 128
        out[k] = flat[at:at + rows].reshape(like[k].shape)
        at += rows
    return out


def _make_layer(wt, wkv, wb, wout, norm_g, mem_norm_g, ln_g, ln_b, gm_ws, gm_bs, pool_w, pool_scale):
    tril = jnp.tril(jnp.ones((CHUNK, CHUNK), bool))
    wsm = jnp.where(tril, gm_ws, 0.0).astype(BF16)
    pw = pool_w.astype(BF16)
    return dict(wt=wt, wgt=wt[D_BRANCHES:], wkv=wkv, wb=wb, wout=wout, g=norm_g[None], mg=mem_norm_g[None],
                ln_g=ln_g[None],
                ln_b=ln_b[None], wsm=wsm, wsm_t=wsm.transpose(0, 2, 1), pw=pw, pw_t=pw.transpose(0, 2, 1),
                ps=pool_scale[None], bias=jnp.repeat(gm_bs.T, HEAD, axis=1))


def _layer_fwd(xl, mem0, L, next_shards=()):
    S = xl.shape[0]
    proj, gates, h, half_gathered = _in_proj(xl, L["g"], L["wt"], L["wgt"], next_shards[:1])
    kv, mem_n = _mem_kv(mem0, L["mg"], L["wkv"])
    y4, cdf, gathered = _abm_fwd(proj, L["ln_g"], L["ln_b"], L["wsm"], L["bias"], L["pw"], L["ps"], kv, half_gathered)
    o_g, l_g = [], []
    for gi, d in enumerate(DILATIONS):
        if d == 1:
            o, lse = _attn_fwd(proj, CB_Q0, proj, CB_K, proj, CB_CV, S // CHUNK)
        else:
            o, lse = _attn_fwd_dilated(proj, CB_Q0 + gi, CB_K, CB_CV, d)
        o_g.append(o)
        l_g.append(lse)
    (xn, y4, oc, lse, z), rest = _merge_fwd(xl, y4, o_g, l_g, proj, gates, L["wb"], L["wout"], next_shards[1:])
    saved = dict(x=xl, proj=proj, gates=gates, h=h, kv=kv, mem_n=mem_n, y4=y4, cdf=cdf, oc=oc, lse=lse, z=z)
    return xn, saved, gathered + rest


def _place_cols(dst, piece, cb):
    return lax.dynamic_update_slice(dst, piece, (0, cb * WIDTH))


def _layer_bwd(dx, mem0, L, sv, later=()):
    S = dx.shape[0]
    proj = sv["proj"]
    (dy3, doc, delta, dcg, dgm, dwb, dwout), from_sibling = _merge_bwd(
        dx, sv["y4"], sv["oc"], sv["z"], proj, sv["gates"], L["wb"], L["wout"], later)
    pair = _pair_sum(later, from_sibling) if later else ()
    dpb, dm, dlng, dlnb, dws, dbias, dpw, dps, dkv = _abm_bwd(
        proj, sv["cdf"], dy3, L["ln_g"], L["ln_b"], L["wsm"], L["wsm_t"], L["bias"], L["pw"], L["pw_t"], L["ps"], sv["kv"])
    dk, dv = None, None
    for gi, d in enumerate(DILATIONS):
        if d == 1:
            r = _attn_bwd(proj, CB_Q0, proj, CB_K, proj, CB_CV, doc, sv["lse"], delta, S // CHUNK)
        else:
            r = _attn_bwd_dilated(proj, CB_Q0 + gi, CB_K, CB_CV, doc, sv["lse"], delta, d)
        dpb = _place_cols(dpb, r[0], CB_Q0 + gi)
        dkg, dvg = r[1].astype(F32), r[2].astype(F32)
        dk = dkg if dk is None else dk + dkg
        dv = dvg if dv is None else dv + dvg
    dpb = _place_cols(dpb, dk.astype(BF16), CB_K)
    dpb = _place_cols(dpb, dv.astype(BF16), CB_CV)
    dpb = _place_cols(dpb, dcg, CB_CGATE)
    dpb = _place_cols(dpb, dm, CB_MQ)
    dwkv, dmg = _mem_bwd(mem0, L["mg"], sv["mem_n"], L["wkv"], dkv)
    dwin_t, parts_rest = _dw_in(sv["h"], dpb, dgm, pair[1:])
    dxi, dng, parts = _dh_bwd(dpb, dgm, L["wt"], L["wgt"], sv["x"], L["g"], dx, pair[:1])
    parts = parts + parts_rest
    big = dict(w_in=dwin_t, w_mem_kv=dwkv, w_branch=dwb, w_out=dwout)
    small = dict(norm_g=dng[0], gm_ln_g=dlng[0], gm_ln_b=dlnb[0], gm_ws=dws,
                 gm_bs=_bias_reduce(dbias)[:, :N_HEAD].T, pool_w=dpw, pool_scale=dps[0], mem_norm_g=dmg[0])
    return dxi, big, small, parts


BIG = ("w_in", "w_mem_kv", "w_branch", "w_out")


def _blocked(big):
    return [_to_blocks(big["w_in"], "rows"), _to_blocks(big["w_mem_kv"], "rows"),
            _to_blocks(big["w_branch"], "branch"), _to_blocks(big["w_out"], "rows")]


def _full_weights(gathered):
    win_t, wkv, wb, wout = gathered
    return (win_t.reshape(D_IN, D_MODEL), wkv.reshape(D_MODEL, 2 * WIDTH),
            wb.reshape(N_DEV, 4, WIDTH, -1).transpose(1, 2, 0, 3).reshape(4, WIDTH, D_MODEL),
            wout.reshape(D_MODEL, D_MODEL))


def kernel(x, mem, norm_g, w_in, gm_ln_g, gm_ln_b, gm_ws, gm_bs, pool_w, pool_scale, mem_norm_g, w_mem_kv, w_branch, w_out, final_norm_g, loss_target, m_norm_g, m_w_in, m_gm_ln_g, m_gm_ln_b, m_gm_ws, m_gm_bs, m_pool_w, m_pool_scale, m_mem_norm_g, m_w_mem_kv, m_w_branch, m_w_out, m_final_norm_g, v_norm_g, v_w_in, v_gm_ln_g, v_gm_ln_b, v_gm_ws, v_gm_bs, v_pool_w, v_pool_scale, v_mem_norm_g, v_w_mem_kv, v_w_branch, v_w_out, v_final_norm_g):
    x0 = x[0]
    mem0 = mem[0]
    tgt = loss_target[0]
    S = x0.shape[0]

    shards = [[w_in[l].T.astype(BF16), w_mem_kv[l].astype(BF16), w_branch[l].astype(BF16).reshape(4 * WIDTH, -1),
               w_out[l].astype(BF16)] for l in range(DEPTH)]
    gathered = _all_gather(shards[0])
    layers, saved = [], []
    xl = x0
    for l in range(DEPTH):
        layers.append(_make_layer(*_full_weights(gathered), norm_g[l], mem_norm_g[l], gm_ln_g[l], gm_ln_b[l],
                                  gm_ws[l], gm_bs[l], pool_w[l], pool_scale[l]))
        xl, sv, gathered = _layer_fwd(xl, mem0, layers[l], shards[l + 1] if l + 1 < DEPTH else ())
        saved.append(sv)

    loss_part, dx, d_final = _loss_head(xl, final_norm_g[None], tgt)
    loss = lax.psum(loss_part[0, 0], ("x", "y", "c"))

    small = {k: [None] * DEPTH for k in SMALL if k != "final_norm_g"}
    parts = [None] * DEPTH
    later = ()
    for l in reversed(range(DEPTH)):
        dx, gb, gs, done = _layer_bwd(dx, mem0, layers[l], saved[l], later)
        if later:
            parts[l + 1] = done
        later = _blocked(gb)
        for k in gs:
            small[k][l] = gs[k]
    grad_x = dx[None]
    small_tree = {k: jnp.stack(small[k]) for k in small}
    small_tree["final_norm_g"] = d_final[0]
    reduced, parts[0] = _all_reduce_small(_pack_small(small_tree).reshape(N_DEV, SMALL_ROWS, 128),
                                          _pair_sum(later, _rs_sibling(later)))

    weights = dict(norm_g=norm_g, w_in=w_in, gm_ln_g=gm_ln_g, gm_ln_b=gm_ln_b, gm_ws=gm_ws, gm_bs=gm_bs,
                   pool_w=pool_w, pool_scale=pool_scale, mem_norm_g=mem_norm_g, w_mem_kv=w_mem_kv,
                   w_branch=w_branch, w_out=w_out, final_norm_g=final_norm_g)
    m_in = dict(norm_g=m_norm_g, w_in=m_w_in, gm_ln_g=m_gm_ln_g, gm_ln_b=m_gm_ln_b, gm_ws=m_gm_ws, gm_bs=m_gm_bs,
                pool_w=m_pool_w, pool_scale=m_pool_scale, mem_norm_g=m_mem_norm_g, w_mem_kv=m_w_mem_kv,
                w_branch=m_w_branch, w_out=m_w_out, final_norm_g=m_final_norm_g)
    v_in = dict(norm_g=v_norm_g, w_in=v_w_in, gm_ln_g=v_gm_ln_g, gm_ln_b=v_gm_ln_b, gm_ws=v_gm_ws, gm_bs=v_gm_bs,
                pool_w=v_pool_w, pool_scale=v_pool_scale, mem_norm_g=v_mem_norm_g, w_mem_kv=v_w_mem_kv,
                w_branch=v_w_branch, w_out=v_w_out, final_norm_g=v_final_norm_g)
    res = {}
    def view(k, arr):
        return arr.transpose(0, 2, 1) if k == "w_in" else arr

    for a, k in enumerate(BIG):
        shape = view(k, weights[k]).shape
        by_layer = [parts[l][a] for l in range(DEPTH)]
        lrc = (DEPTH,) + by_layer[0].shape[1:]
        outs = _adamw_layers(by_layer, view(k, weights[k]).reshape(lrc), view(k, m_in[k]).reshape(lrc),
                             view(k, v_in[k]).reshape(lrc), "adamw_" + k)
        res[k] = [view(k, o.reshape(shape)) for o in outs]
    outs = _adamw(reduced.reshape(1, N_DEV * SMALL_ROWS, 128), _pack_small(weights), _pack_small(m_in),
                  _pack_small(v_in), "adamw_small")
    unpacked = [_unpack_small(o, weights) for o in outs]
    for k in SMALL:
        res[k] = [u[k] for u in unpacked]

    order = ("norm_g", "w_in", "gm_ln_g", "gm_ln_b", "gm_ws", "gm_bs", "pool_w", "pool_scale", "mem_norm_g",
             "w_mem_kv", "w_branch", "w_out", "final_norm_g")
    return (loss, grad_x, *[res[k][0] for k in order], *[res[k][1] for k in order],
            *[res[k][2] for k in order], *[res[k][3] for k in order])
```

```python
import functools
import math

import jax
import jax.numpy as jnp
from jax import lax
from jax.experimental import pallas as pl
from jax.experimental.pallas import tpu as pltpu

F32 = jnp.float32
BF16 = jnp.bfloat16

D_MODEL = 1024
DEPTH = 4
WIDTH = 512
D_IN = 10752
HEAD = 128
N_HEAD = 4
CHUNK = 128
MEM_LEN = 256
POOL_WINDOWS = (2, 4, 8, 16)
DILATIONS = (1, 4, 16)
EPS = 1e-6
NEG = -1e30
ATT_SCALE = HEAD ** -0.5
N_DEV = 8
N_CHIP = 4

D_BRANCHES = 6656
D_GATES = D_IN - D_BRANCHES
CB_U, CB_V, CB_AGATE, CB_PIN, CB_PGATE = 0, 1, 2, 3, 4
CB_Q0, CB_K, CB_CV, CB_CGATE, CB_MQ, CB_MGATE = 5, 8, 9, 10, 11, 12

ADAM_LR = 0.001
ADAM_B1 = 0.9
ADAM_B2 = 0.999
ADAM_EPS = 1e-08
ADAM_WD = 0.01
ADAM_STEP = 10

VMEM_LIMIT = 56 * 1024 * 1024
MESH = pl.DeviceIdType.MESH
ANY = pl.BlockSpec(memory_space=pl.ANY)

NT = (((1,), (1,)), ((), ()))
TN = (((0,), (0,)), ((), ()))


def _dot(a, b):
    return jnp.dot(a, b, preferred_element_type=F32)


def _dot_nt(a, b):
    return lax.dot_general(a, b, NT, preferred_element_type=F32)


def _dot_tn(a, b):
    return lax.dot_general(a, b, TN, preferred_element_type=F32)


def _sigmoid(x):
    return 0.5 * jnp.tanh(0.5 * x) + 0.5


def _silu(x):
    return x * _sigmoid(x)


def _silu_and_grad(x):
    s = _sigmoid(x)
    return x * s, s * (1.0 + x * (1.0 - s))


def _normal_cdf(x):
    return 0.5 * (1.0 + lax.erf(x * (2.0 ** -0.5)))


def _gelu_and_grad(x, cdf):
    return x * cdf, cdf + x * jnp.exp(-0.5 * x * x) * (1.0 / math.sqrt(2.0 * math.pi))


def _col(blk, h):
    lane = lax.broadcasted_iota(jnp.int32, blk.shape, 1)
    return jnp.sum(jnp.where(lane == h, blk, 0.0), axis=1, keepdims=True)


def _put_cols(cols):
    rows = cols[0].shape[0]
    lane = lax.broadcasted_iota(jnp.int32, (rows, 128), 1)
    out = jnp.zeros((rows, 128), F32)
    for h, cv in enumerate(cols):
        out = jnp.where(lane == h, cv, out)
    return out


def _params(sem, vmem=VMEM_LIMIT):
    return pltpu.CompilerParams(dimension_semantics=sem, vmem_limit_bytes=vmem)


def _full(shape):
    nd = len(shape)
    return pl.BlockSpec(shape, lambda *_: (0,) * nd)


def _rows(tm, width, cb=0):
    return pl.BlockSpec((tm, width), lambda i: (i, cb))


GATE_TILE = 512


def _in_proj(x, g, wt, wgt, shards=()):
    S = x.shape[0]
    tm, tnb, tng = 1024, D_BRANCHES // 4, D_GATES // 4
    njb, njg = 4, 4
    n = len(shards)
    ni, nj = S // tm, njb + njg

    def body(x_ref, g_ref, wbr_ref, wg_ref, *rest):
        ins, (proj_ref, gates_ref, h_ref), outs = rest[:n], rest[n:n + 3], rest[n + 3:2 * n + 3]
        hs, sems = rest[2 * n + 3], rest[2 * n + 4:]
        i, j = pl.program_id(0), pl.program_id(1)

        if n:
            @pl.when(jnp.logical_and(i == 0, j == 0))
            def _():
                _comm_start(_ag_first(ins, outs, *sems))

        @pl.when(j == 0)
        def _():
            xf = x_ref[...]
            r = lax.rsqrt(jnp.mean(xf * xf, axis=-1, keepdims=True) + EPS)
            h = (xf * r * g_ref[...]).astype(BF16)
            hs[...] = h
            h_ref[...] = h

        @pl.when(j < njb)
        def _():
            proj_ref[...] = _dot_nt(hs[...], wbr_ref[...]).astype(BF16)

        @pl.when(j >= njb)
        def _():
            gates_ref[...] = _dot_nt(hs[...], wg_ref[...]).astype(BF16)

        if n:
            @pl.when(jnp.logical_and(i == ni - 1, j == nj - 1))
            def _():
                _comm_wait(_ag_first(ins, outs, *sems))

    def first(j):
        return jnp.minimum(j, njb - 1)

    def second(j):
        return jnp.maximum(j - njb, 0)

    res = pl.pallas_call(
        body, name="in_proj_gather" if n else "in_proj",
        grid=(ni, nj),
        in_specs=[pl.BlockSpec((tm, D_MODEL), lambda i, j: (i, 0)),
                  pl.BlockSpec((1, D_MODEL), lambda i, j: (0, 0)),
                  pl.BlockSpec((tnb, D_MODEL), lambda i, j: (first(j), 0)),
                  pl.BlockSpec((tng, D_MODEL), lambda i, j: (second(j), 0))]
                 + [ANY] * n,
        out_specs=[pl.BlockSpec((tm, tnb), lambda i, j: (i, first(j))),
                   pl.BlockSpec((tm, tng), lambda i, j: (i, second(j))),
                   pl.BlockSpec((tm, D_MODEL), lambda i, j: (i, 0))] + [ANY] * n,
        out_shape=[jax.ShapeDtypeStruct((S, D_BRANCHES), BF16), jax.ShapeDtypeStruct((S, D_GATES), BF16),
                   jax.ShapeDtypeStruct((S, D_MODEL), BF16)]
                  + [jax.ShapeDtypeStruct((N_DEV,) + s.shape, s.dtype) for s in shards],
        scratch_shapes=[pltpu.VMEM((tm, D_MODEL), BF16)] + (_dma_sems(4 * n, 4 * n, n) if n else []),
        compiler_params=_params(("arbitrary", "arbitrary")),
    )(x, g, wt, wgt, *shards)
    return res[0], res[1], res[2], list(res[3:])


def _mem_kv(mem, g, w):
    M = mem.shape[0]

    def body(m_ref, g_ref, w_ref, kv_ref, mn_ref):
        xf = m_ref[...]
        r = lax.rsqrt(jnp.mean(xf * xf, axis=-1, keepdims=True) + EPS)
        mn = (xf * r * g_ref[...]).astype(BF16)
        mn_ref[...] = mn
        kv_ref[...] = _dot(mn, w_ref[...]).astype(BF16)

    return pl.pallas_call(
        body, name="mem_kv",
        out_shape=[jax.ShapeDtypeStruct((M, 2 * WIDTH), BF16), jax.ShapeDtypeStruct((M, D_MODEL), BF16)],
        compiler_params=pltpu.CompilerParams(vmem_limit_bytes=VMEM_LIMIT),
    )(mem, g, w)


def _band_masks(win):
    t = lax.broadcasted_iota(jnp.int32, (CHUNK, CHUNK), 0)
    s = lax.broadcasted_iota(jnp.int32, (CHUNK, CHUNK), 1)
    cur = jnp.logical_and(t - s >= 0, t - s < win)
    prev = s > t + CHUNK - win
    return cur.astype(BF16), prev.astype(BF16)


def _inv_count(first_row, win):
    t = first_row + lax.broadcasted_iota(jnp.int32, (CHUNK, 1), 0)
    return 1.0 / jnp.minimum(t + 1, win).astype(F32)


def _layer_norm_fwd(v):
    mu = jnp.mean(v, axis=-1, keepdims=True)
    vc = v - mu
    var = jnp.mean(vc * vc, axis=-1, keepdims=True)
    rstd = lax.rsqrt(var + EPS)
    return vc * rstd, rstd


def _mem_softmax(q, kmem):
    s = _dot_nt(q, kmem) * ATT_SCALE
    m = jnp.max(s, axis=-1, keepdims=True)
    e = jnp.exp(s - m)
    return e * (1.0 / jnp.sum(e, axis=-1, keepdims=True))


def _abm_fwd(proj, ln_g, ln_b, wsm, bias_full, pool_w, pool_scale, kv, gathered=()):
    S = proj.shape[0]
    tm = 512
    nchunk = tm // CHUNK
    n = len(gathered)
    nsteps = S // tm

    def body(u_ref, v_ref, ag_ref, p_ref, ph_ref, pg_ref, mq_ref, mg_ref, lng_ref, lnb_ref, wsm_ref, bias_ref,
             pw_ref, ps_ref, kv_ref, *rest):
        y_ref, cdf_ref, bufs = rest[n], rest[n + 1], rest[n + 2:2 * n + 2]
        mix, sems = rest[2 * n + 2], rest[2 * n + 3:]
        i = pl.program_id(0)

        if n:
            @pl.when(i == 0)
            def _():
                _comm_start(_ag_second(bufs, *sems))

        au, av = u_ref[...].astype(F32), v_ref[...].astype(F32)
        cdf_u, cdf_v = _normal_cdf(au), _normal_cdf(av)
        cdf_ref[0] = cdf_u.astype(BF16)
        cdf_ref[1] = cdf_v.astype(BF16)
        u, v = au * cdf_u, av * cdf_v
        vhat, _ = _layer_norm_fwd(v)
        vln = (vhat * lng_ref[...] + lnb_ref[...]).astype(BF16)
        for c in range(nchunk):
            for h in range(N_HEAD):
                rs, cs = slice(c * CHUNK, (c + 1) * CHUNK), slice(h * HEAD, (h + 1) * HEAD)
                mix[rs, cs] = _dot(wsm_ref[h], vln[rs, cs]) + bias_ref[:, cs]
        y_ref[0] = (u * mix[...] * _silu(ag_ref[...].astype(F32))).astype(BF16)
        halo_ok = (i > 0).astype(F32)
        for c in range(nchunk):
            rs = slice(c * CHUNK, (c + 1) * CHUNK)
            for g, win in enumerate(POOL_WINDOWS):
                cs = slice(g * HEAD, (g + 1) * HEAD)
                bcur, bprev = _band_masks(win)
                cur = p_ref[rs, cs]
                if c == 0:
                    prev = (ph_ref[:, cs].astype(F32) * halo_ok).astype(BF16)
                else:
                    prev = p_ref[(c - 1) * CHUNK:c * CHUNK, cs]
                sums = _dot(bcur, cur) + _dot(bprev, prev)
                dm = sums * _inv_count(i * tm + c * CHUNK, win) - cur.astype(F32)
                mix[rs, cs] = _dot(dm.astype(BF16), pw_ref[g])
        y_ref[1] = (mix[...] * ps_ref[...] * _silu(pg_ref[...].astype(F32))).astype(BF16)
        for h in range(N_HEAD):
            cs = slice(h * HEAD, (h + 1) * HEAD)
            p = _mem_softmax(mq_ref[:, cs], kv_ref[:, cs])
            mix[:, cs] = _dot(p.astype(BF16), kv_ref[:, WIDTH + h * HEAD:WIDTH + (h + 1) * HEAD])
        y_ref[2] = (mix[...] * _silu(mg_ref[...].astype(F32))).astype(BF16)

        if n:
            @pl.when(i == nsteps - 1)
            def _():
                _comm_wait(_ag_second(bufs, *sems))

    blk = tm // CHUNK
    res = pl.pallas_call(
        body, name="abm_fwd_gather" if n else "abm_fwd",
        grid=(nsteps,),
        in_specs=[_rows(tm, WIDTH, CB_U), _rows(tm, WIDTH, CB_V), _rows(tm, WIDTH, CB_AGATE),
                  _rows(tm, WIDTH, CB_PIN),
                  pl.BlockSpec((CHUNK, WIDTH), lambda i: (jnp.maximum(i * blk - 1, 0), CB_PIN)),
                  _rows(tm, WIDTH, CB_PGATE), _rows(tm, WIDTH, CB_MQ), _rows(tm, WIDTH, CB_MGATE),
                  _full((1, WIDTH)), _full((1, WIDTH)), _full((N_HEAD, CHUNK, CHUNK)), _full((CHUNK, WIDTH)),
                  _full((4, HEAD, HEAD)), _full((1, WIDTH)), _full((MEM_LEN, 2 * WIDTH))] + [ANY] * n,
        out_specs=[pl.BlockSpec((3, tm, WIDTH), lambda i: (0, i, 0)), pl.BlockSpec((2, tm, WIDTH), lambda i: (0, i, 0))]
                  + [ANY] * n,
        out_shape=[jax.ShapeDtypeStruct((4, S, WIDTH), BF16),
                   jax.ShapeDtypeStruct((2, S, WIDTH), BF16)]
                  + [jax.ShapeDtypeStruct(b.shape, b.dtype) for b in gathered],
        input_output_aliases={15 + a: 2 + a for a in range(n)},
        scratch_shapes=[pltpu.VMEM((tm, WIDTH), F32)] + (_dma_sems(3 * n, 3 * n) if n else []),
        compiler_params=_params(("arbitrary",)),
    )(proj, proj, proj, proj, proj, proj, proj, proj, ln_g, ln_b, wsm, bias_full, pool_w, pool_scale, kv, *gathered)
    return res[0], res[1], list(res[2:])


ATT_TILE = 512


def _attn_fwd(q, qcb, k, kcb, v, vcb, bps):
    S = q.shape[0]
    tm = ATT_TILE
    nb = tm // CHUNK

    nblocks = nb * N_HEAD

    def body(q_ref, k_ref, v_ref, kh_ref, vh_ref, o_ref, l_ref, sc_s, sp_s, pc_s, pp_s):
        i = pl.program_id(0)

        def prev_kv(n, cs):
            if n == 0:
                return kh_ref[:, cs], vh_ref[:, cs]
            ps = slice((n - 1) * CHUNK, n * CHUNK)
            return k_ref[ps, cs], v_ref[ps, cs]

        pens = []
        for n in range(nb):
            rs = slice(n * CHUNK, (n + 1) * CHUNK)
            pens.append(jnp.full((N_HEAD * CHUNK, 1), jnp.where((i * nb + n) % bps != 0, 0.0, NEG), F32))
            for h in range(N_HEAD):
                cs = slice(h * HEAD, (h + 1) * HEAD)
                bs = slice((n * N_HEAD + h) * CHUNK, (n * N_HEAD + h + 1) * CHUNK)
                qh = q_ref[rs, cs]
                sc_s[bs, :] = _dot_nt(qh, k_ref[rs, cs])
                sp_s[bs, :] = _dot_nt(qh, prev_kv(n, cs)[0])
        row = lax.broadcasted_iota(jnp.int32, (nblocks * CHUNK, CHUNK), 0) & (CHUNK - 1)
        col = lax.broadcasted_iota(jnp.int32, (nblocks * CHUNK, CHUNK), 1)
        sc = jnp.where(col <= row, sc_s[...] * ATT_SCALE, NEG)
        sp = jnp.where(col >= row, sp_s[...] * ATT_SCALE, NEG) + jnp.concatenate(pens, axis=0)
        m = jnp.maximum(jnp.max(sc, axis=-1, keepdims=True), jnp.max(sp, axis=-1, keepdims=True))
        ec = jnp.exp(sc - m)
        ep = jnp.exp(sp - m)
        den = jnp.sum(ec, axis=-1, keepdims=True) + jnp.sum(ep, axis=-1, keepdims=True)
        inv = 1.0 / den
        pc_s[...] = (ec * inv).astype(BF16)
        pp_s[...] = (ep * inv).astype(BF16)
        lse = m + jnp.log(den)
        for n in range(nb):
            rs = slice(n * CHUNK, (n + 1) * CHUNK)
            for h in range(N_HEAD):
                cs = slice(h * HEAD, (h + 1) * HEAD)
                bs = slice((n * N_HEAD + h) * CHUNK, (n * N_HEAD + h + 1) * CHUNK)
                o = _dot(pc_s[bs, :], v_ref[rs, cs]) + _dot(pp_s[bs, :], prev_kv(n, cs)[1])
                o_ref[rs, cs] = o.astype(BF16)
            l_ref[rs, :] = _put_cols([lse[(n * N_HEAD + h) * CHUNK:(n * N_HEAD + h + 1) * CHUNK]
                                      for h in range(N_HEAD)])

    def halo(cb):
        return pl.BlockSpec((CHUNK, WIDTH), lambda i: (jnp.maximum(i * nb - 1, 0), cb))

    return pl.pallas_call(
        body, name=f"attn_fwd_{bps}",
        grid=(S // tm,),
        in_specs=[_rows(tm, WIDTH, qcb), _rows(tm, WIDTH, kcb), _rows(tm, WIDTH, vcb), halo(kcb), halo(vcb)],
        out_specs=[_rows(tm, WIDTH), _rows(tm, 128)],
        out_shape=[jax.ShapeDtypeStruct((S, WIDTH), BF16), jax.ShapeDtypeStruct((S, 128), F32)],
        scratch_shapes=[pltpu.VMEM((nblocks * CHUNK, CHUNK), F32), pltpu.VMEM((nblocks * CHUNK, CHUNK), F32),
                        pltpu.VMEM((nblocks * CHUNK, CHUNK), BF16), pltpu.VMEM((nblocks * CHUNK, CHUNK), BF16)],
        compiler_params=_params(("parallel",)),
    )(q, k, v, k, v)


def _gate_specs(tm):
    return [pl.BlockSpec((tm, D_MODEL), lambda i, b=b: (i, b)) for b in range(4)]


Y_SLOT = (0, 1, 3, 2)


def _merge_fwd(x, y4, o_g, l_g, proj, gates, wb, wout, shards=()):
    S = x.shape[0]
    tm = 256
    n = len(shards)
    nsteps = S // tm
    forward_at = nsteps - 4

    def body(x_ref, y_ref, o0, o1, o2, l0, l1, l2, cg_ref, *rest):
        gm = rest[:4]
        wb_ref, wo_ref = rest[4:6]
        s_in = rest[6:6 + n]
        xn_ref, yc_ref, oc_ref, lse_ref, z_ref = rest[6 + n:11 + n]
        s_out, ocs, sems = rest[11 + n:11 + 2 * n], rest[11 + 2 * n], rest[12 + 2 * n:]
        i = pl.program_id(0)

        if n:
            @pl.when(i == 0)
            def _():
                _comm_start(_ag_first(s_in, s_out, *sems[:3]))

            @pl.when(i == forward_at)
            def _():
                incoming = _ag_first(s_in, s_out, *sems[:3])[2]
                for a in range(n):
                    for k in range(1, 4):
                        incoming[4 * a + k].wait_recv()
                _comm_start(_ag_second(s_out, *sems[3:]))

        lcols = []
        for h in range(N_HEAD):
            cs = slice(h * HEAD, (h + 1) * HEAD)
            ls = [_col(l[...], h) for l in (l0, l1, l2)]
            m = jnp.maximum(jnp.maximum(ls[0], ls[1]), ls[2])
            tot = jnp.exp(ls[0] - m) + jnp.exp(ls[1] - m) + jnp.exp(ls[2] - m)
            lse = m + jnp.log(tot)
            ocs[:, cs] = sum(jnp.exp(lg - lse) * o[:, cs].astype(F32) for lg, o in zip(ls, (o0, o1, o2)))
            lcols.append(lse)
        lse_ref[...] = _put_cols(lcols)
        oc = ocs[...]
        oc_ref[...] = oc.astype(BF16)
        yc = (oc * _silu(cg_ref[...].astype(F32))).astype(BF16)
        yc_ref[...] = yc
        ys = (y_ref[0], y_ref[1], yc, y_ref[2])
        z = jnp.zeros((tm, D_MODEL), F32)
        for b in range(4):
            z = z + _sigmoid(gm[b][...].astype(F32)) * _dot(ys[b], wb_ref[b])
        zb = z.astype(BF16)
        z_ref[...] = zb
        xn_ref[...] = x_ref[...] + _dot(zb, wo_ref[...])

        if n:
            @pl.when(i == nsteps - 1)
            def _():
                local, out, incoming = _ag_first(s_in, s_out, *sems[:3])
                for a in range(n):
                    incoming[4 * a].wait_recv()
                _comm_wait(_ag_second(s_out, *sems[3:]))
                for cp in out:
                    cp.wait_send()
                for cp in local:
                    cp.wait()

    res = pl.pallas_call(
        body, name="merge_fwd_gather" if n else "merge_fwd",
        grid=(nsteps,),
        in_specs=[_rows(tm, D_MODEL), pl.BlockSpec((3, tm, WIDTH), lambda i: (0, i, 0)),
                  _rows(tm, WIDTH), _rows(tm, WIDTH), _rows(tm, WIDTH),
                  _rows(tm, 128), _rows(tm, 128), _rows(tm, 128),
                  _rows(tm, WIDTH, CB_CGATE)] + _gate_specs(tm)
                 + [_full((4, WIDTH, D_MODEL)), _full((D_MODEL, D_MODEL))] + [ANY] * n,
        out_specs=[_rows(tm, D_MODEL), pl.BlockSpec((None, tm, WIDTH), lambda i: (Y_SLOT[2], i, 0)),
                   _rows(tm, WIDTH), _rows(tm, 128), _rows(tm, D_MODEL)] + [ANY] * n,
        out_shape=[jax.ShapeDtypeStruct((S, D_MODEL), F32), jax.ShapeDtypeStruct(y4.shape, BF16),
                   jax.ShapeDtypeStruct((S, WIDTH), BF16), jax.ShapeDtypeStruct((S, 128), F32),
                   jax.ShapeDtypeStruct((S, D_MODEL), BF16)]
                  + [jax.ShapeDtypeStruct((N_DEV,) + s.shape, s.dtype) for s in shards],
        input_output_aliases={1: 1},
        scratch_shapes=[pltpu.VMEM((tm, WIDTH), F32)] + (_dma_sems(4 * n, 4 * n, n, 3 * n, 3 * n) if n else []),
        compiler_params=_params(("arbitrary",)),
    )(x, y4, *o_g, *l_g, proj, *([gates] * 4), wb, wout, *shards)
    return res[:5], list(res[5:])


def _loss_head(x, g, tgt):
    S = x.shape[0]
    tm = 512

    def body(x_ref, g_ref, t_ref, loss_ref, dx_ref, dg_ref):
        @pl.when(pl.program_id(0) == 0)
        def _():
            loss_ref[...] = jnp.zeros_like(loss_ref)
            dg_ref[...] = jnp.zeros_like(dg_ref)

        xf = x_ref[...]
        r = lax.rsqrt(jnp.mean(xf * xf, axis=-1, keepdims=True) + EPS)
        xhat = xf * r
        gv = g_ref[...]
        err = xhat * gv - t_ref[...]
        e2 = jnp.sum(err * err, axis=-1, keepdims=True)
        loss_ref[...] += (0.5 / D_MODEL) * jnp.sum(e2, axis=0, keepdims=True)
        dy = err * (1.0 / D_MODEL)
        dg_ref[...] += jnp.sum(dy * xhat, axis=0, keepdims=True)
        dxh = dy * gv
        dx_ref[...] = r * (dxh - xhat * jnp.mean(dxh * xhat, axis=-1, keepdims=True))

    return pl.pallas_call(
        body, name="loss_head",
        grid=(S // tm,),
        in_specs=[_rows(tm, D_MODEL), _full((1, D_MODEL)), _rows(tm, D_MODEL)],
        out_specs=[_full((1, 128)), _rows(tm, D_MODEL), _full((1, D_MODEL))],
        out_shape=[jax.ShapeDtypeStruct((1, 128), F32), jax.ShapeDtypeStruct((S, D_MODEL), F32),
                   jax.ShapeDtypeStruct((1, D_MODEL), F32)],
        compiler_params=_params(("arbitrary",)),
    )(x, g, tgt)


def _merge_bwd(dxo, y4, oc, z, proj, gates, wb, wout, grads=()):
    S = dxo.shape[0]
    tm = 256
    n = len(grads)
    nsteps = S // tm

    def body(dx_ref, y_ref, oc_ref, z_ref, cg_ref, *rest):
        gm = rest[:4]
        wb_ref, wo_ref = rest[4:6]
        g_in = rest[6:6 + n]
        dy_ref, doc_ref, delta_ref, dcg_ref, dgm_ref, dwb_ref, dwo_ref = rest[6 + n:13 + n]
        g_out = rest[13 + n:13 + 2 * n]
        acc_b, acc_o = rest[13 + 2 * n:15 + 2 * n]
        sems = rest[15 + 2 * n:]
        i = pl.program_id(0)

        @pl.when(i == 0)
        def _():
            acc_b[...] = jnp.zeros_like(acc_b)
            acc_o[...] = jnp.zeros_like(acc_o)
            if n:
                _comm_start(_rs_first(g_in, g_out, *sems))

        dxb = dx_ref[...].astype(BF16)
        acc_o[...] += _dot_tn(z_ref[...], dxb)
        dz = _dot_nt(dxb, wo_ref[...])
        for b in range(4):
            gate = _sigmoid(gm[b][...].astype(F32))
            yb = y_ref[Y_SLOT[b]]
            t = _dot(yb, wb_ref[b])
            dgm_ref[:, b * D_MODEL:(b + 1) * D_MODEL] = (dz * t * gate * (1.0 - gate)).astype(BF16)
            dt = (dz * gate).astype(BF16)
            acc_b[b] += _dot_tn(yb, dt)
            dyb = _dot_nt(dt, wb_ref[b])
            if b == 2:
                cg = cg_ref[...].astype(F32)
                oc = oc_ref[...].astype(F32)
                scg, dscg = _silu_and_grad(cg)
                doc = dyb * scg
                dcg_ref[...] = (dyb * oc * dscg).astype(BF16)
                doc_ref[...] = doc.astype(BF16)
                prod = doc * oc
                delta_ref[...] = _put_cols([jnp.sum(prod[:, h * HEAD:(h + 1) * HEAD], axis=1, keepdims=True)
                                            for h in range(N_HEAD)])
            else:
                dy_ref[b if b < 2 else 2] = dyb.astype(BF16)

        @pl.when(i == nsteps - 1)
        def _():
            dwb_ref[...] = acc_b[...].astype(BF16)
            dwo_ref[...] = acc_o[...].astype(BF16)
            if n:
                _comm_wait(_rs_first(g_in, g_out, *sems))

    def resident(shape):
        nd = len(shape)
        return pl.BlockSpec(shape, lambda i: (0,) * nd, pipeline_mode=pl.Buffered(1))

    res = pl.pallas_call(
        body, name="merge_bwd_scatter" if n else "merge_bwd",
        grid=(nsteps,),
        in_specs=[_rows(tm, D_MODEL), pl.BlockSpec((4, tm, WIDTH), lambda i: (0, i, 0)),
                  _rows(tm, WIDTH), _rows(tm, D_MODEL), _rows(tm, WIDTH, CB_CGATE)] + _gate_specs(tm)
                 + [resident((4, WIDTH, D_MODEL)), resident((D_MODEL, D_MODEL))] + [ANY] * n,
        out_specs=[pl.BlockSpec((3, tm, WIDTH), lambda i: (0, i, 0)), _rows(tm, WIDTH), _rows(tm, 128),
                   _rows(tm, WIDTH), _rows(tm, 4 * D_MODEL), _full((4, WIDTH, D_MODEL)), _full((D_MODEL, D_MODEL))]
                  + [ANY] * n,
        out_shape=[jax.ShapeDtypeStruct((3, S, WIDTH), BF16), jax.ShapeDtypeStruct((S, WIDTH), BF16),
                   jax.ShapeDtypeStruct((S, 128), F32), jax.ShapeDtypeStruct((S, WIDTH), BF16),
                   jax.ShapeDtypeStruct((S, 4 * D_MODEL), BF16), jax.ShapeDtypeStruct((4, WIDTH, D_MODEL), BF16),
                   jax.ShapeDtypeStruct((D_MODEL, D_MODEL), BF16)]
                  + [jax.ShapeDtypeStruct(g.shape[:1] + g.shape[2:], g.dtype) for g in grads],
        scratch_shapes=[pltpu.VMEM((4, WIDTH, D_MODEL), F32), pltpu.VMEM((D_MODEL, D_MODEL), F32)]
                       + (_dma_sems(N_CHIP * n, N_CHIP * n) if n else []),
        compiler_params=_params(("arbitrary",)),
    )(dxo, y4, oc, z, proj, *([gates] * 4), wb, wout, *grads)
    return res[:7], list(res[7:])


def _attn_bwd(q, qcb, k, kcb, v, vcb, do, lse, delta, bps):
    S = q.shape[0]
    tm = ATT_TILE
    nb = tm // CHUNK
    nblk = S // CHUNK

    ncur = nb * N_HEAD
    nprev = (nb + 1) * N_HEAD

    def body(q_ref, k_ref, v_ref, do_ref, l_ref, d_ref, kh_ref, vh_ref, qn_ref, don_ref, ln_ref, dn_ref,
             dq_ref, dk_ref, dv_ref, sc_s, sp_s, dpc_s, dpp_s, pc_s, pp_s, dsc_s, dsp_s):
        i = pl.program_id(0)

        def rows_of(n):
            if n < nb:
                rs = slice(n * CHUNK, (n + 1) * CHUNK)
                return rs, q_ref, do_ref, l_ref, d_ref
            return slice(0, CHUNK), qn_ref, don_ref, ln_ref, dn_ref

        def prev_kv(n, cs):
            if n == 0:
                return kh_ref[:, cs], vh_ref[:, cs]
            ps = slice((n - 1) * CHUNK, n * CHUNK)
            return k_ref[ps, cs], v_ref[ps, cs]

        def blk(n, h):
            return slice((n * N_HEAD + h) * CHUNK, (n * N_HEAD + h + 1) * CHUNK)

        pens, lses, deltas = [], [], []
        for n in range(nb + 1):
            rs, qr, dor, lr, dr = rows_of(n)
            gb = i * nb + n
            pen = jnp.where(gb % bps != 0, 0.0, NEG)
            if n == nb:
                pen = pen + jnp.where(gb < nblk, 0.0, NEG)
            pens.append(jnp.full((N_HEAD * CHUNK, 1), pen, F32))
            lblk, dblk = lr[rs, :], dr[rs, :]
            for h in range(N_HEAD):
                cs = slice(h * HEAD, (h + 1) * HEAD)
                qh, doh = qr[rs, cs], dor[rs, cs]
                lses.append(_col(lblk, h))
                deltas.append(_col(dblk, h))
                kp, vp = prev_kv(n, cs)
                sp_s[blk(n, h), :] = _dot_nt(qh, kp)
                dpp_s[blk(n, h), :] = _dot_nt(doh, vp)
                if n < nb:
                    sc_s[blk(n, h), :] = _dot_nt(qh, k_ref[rs, cs])
                    dpc_s[blk(n, h), :] = _dot_nt(doh, v_ref[rs, cs])
        lse = jnp.concatenate(lses, axis=0)
        delta = jnp.concatenate(deltas, axis=0)
        row = lax.broadcasted_iota(jnp.int32, (nprev * CHUNK, CHUNK), 0) & (CHUNK - 1)
        col = lax.broadcasted_iota(jnp.int32, (nprev * CHUNK, CHUNK), 1)
        sp = jnp.where(col >= row, sp_s[...] * ATT_SCALE, NEG) + jnp.concatenate(pens, axis=0)
        pp = jnp.exp(sp - lse)
        pp_s[...] = pp.astype(BF16)
        dsp_s[...] = (pp * (dpp_s[...] - delta)).astype(BF16)
        nc = ncur * CHUNK
        sc = jnp.where(col[:nc] <= row[:nc], sc_s[...] * ATT_SCALE, NEG)
        pc = jnp.exp(sc - lse[:nc])
        pc_s[...] = pc.astype(BF16)
        dsc_s[...] = (pc * (dpc_s[...] - delta[:nc])).astype(BF16)
        for n in range(nb):
            rs, qr, dor, _, _ = rows_of(n)
            rn, qnr, donr, _, _ = rows_of(n + 1)
            for h in range(N_HEAD):
                cs = slice(h * HEAD, (h + 1) * HEAD)
                kp, _ = prev_kv(n, cs)
                dq = _dot(dsc_s[blk(n, h), :], k_ref[rs, cs]) + _dot(dsp_s[blk(n, h), :], kp)
                dq_ref[rs, cs] = (dq * ATT_SCALE).astype(BF16)
                dk = _dot_tn(dsc_s[blk(n, h), :], qr[rs, cs]) + _dot_tn(dsp_s[blk(n + 1, h), :], qnr[rn, cs])
                dk_ref[rs, cs] = (dk * ATT_SCALE).astype(BF16)
                dv = _dot_tn(pc_s[blk(n, h), :], dor[rs, cs]) + _dot_tn(pp_s[blk(n + 1, h), :], donr[rn, cs])
                dv_ref[rs, cs] = dv.astype(BF16)

    def prev_halo(cb):
        return pl.BlockSpec((CHUNK, WIDTH), lambda i: (jnp.maximum(i * nb - 1, 0), cb))

    def next_halo(width, cb=0):
        return pl.BlockSpec((CHUNK, width), lambda i: (jnp.minimum(i * nb + nb, nblk - 1), cb))

    return pl.pallas_call(
        body, name=f"attn_bwd_{bps}",
        grid=(S // tm,),
        in_specs=[_rows(tm, WIDTH, qcb), _rows(tm, WIDTH, kcb), _rows(tm, WIDTH, vcb), _rows(tm, WIDTH),
                  _rows(tm, 128), _rows(tm, 128), prev_halo(kcb), prev_halo(vcb),
                  next_halo(WIDTH, qcb), next_halo(WIDTH), next_halo(128), next_halo(128)],
        out_specs=[_rows(tm, WIDTH), _rows(tm, WIDTH), _rows(tm, WIDTH)],
        out_shape=[jax.ShapeDtypeStruct((S, WIDTH), BF16)] * 3,
        scratch_shapes=[pltpu.VMEM((ncur * CHUNK, CHUNK), F32), pltpu.VMEM((nprev * CHUNK, CHUNK), F32),
                        pltpu.VMEM((ncur * CHUNK, CHUNK), F32), pltpu.VMEM((nprev * CHUNK, CHUNK), F32),
                        pltpu.VMEM((ncur * CHUNK, CHUNK), BF16), pltpu.VMEM((nprev * CHUNK, CHUNK), BF16),
                        pltpu.VMEM((ncur * CHUNK, CHUNK), BF16), pltpu.VMEM((nprev * CHUNK, CHUNK), BF16)],
        compiler_params=_params(("parallel",)),
    )(q, k, v, do, lse, delta, k, v, q, do, lse, delta)


def _dilated_split(d):
    hp = min(N_HEAD, 16 // d)
    return hp, N_HEAD // hp, HEAD * hp


def _by_class(src_ref, dst, d, hp):
    for j in range(hp):
        dst[j] = pltpu.einshape("(tr)l->(rt)l", src_ref[:, j * HEAD:(j + 1) * HEAD], r=d)


def _from_class(src, dst_ref, d, hp):
    for j in range(hp):
        dst_ref[:, j * HEAD:(j + 1) * HEAD] = pltpu.einshape("(rt)l->(tr)l", src[j].astype(BF16), r=d)


def _attn_fwd_dilated(proj, qcb, kcb, vcb, d):
    S = proj.shape[0]
    T = CHUNK * d
    hp, nh, cw = _dilated_split(d)
    nblocks = d * hp

    def body(q_ref, k_ref, v_ref, o_ref, l_ref, qf, kst, vst, of, lf, sc_s, sp_s, pc_s, pp_s):
        i, hh = pl.program_id(0), pl.program_id(1)
        kf, vf = kst.at[i % 2, hh], vst.at[i % 2, hh]
        kpf, vpf = kst.at[1 - i % 2, hh], vst.at[1 - i % 2, hh]

        @pl.when(i == 0)
        def _():
            kpf[...] = jnp.zeros_like(kpf)
            vpf[...] = jnp.zeros_like(vpf)

        _by_class(q_ref, qf, d, hp)
        _by_class(k_ref, kf, d, hp)
        _by_class(v_ref, vf, d, hp)

        def blk(ref, r, j):
            return ref[j, r * CHUNK:(r + 1) * CHUNK, :]

        def bs(r, j):
            return slice((r * hp + j) * CHUNK, (r * hp + j + 1) * CHUNK)

        for r in range(d):
            for j in range(hp):
                qb = blk(qf, r, j)
                sc_s[bs(r, j), :] = _dot_nt(qb, blk(kf, r, j))
                sp_s[bs(r, j), :] = _dot_nt(qb, blk(kpf, r, j))
        row = lax.broadcasted_iota(jnp.int32, (nblocks * CHUNK, CHUNK), 0) & (CHUNK - 1)
        col = lax.broadcasted_iota(jnp.int32, (nblocks * CHUNK, CHUNK), 1)
        sc = jnp.where(col <= row, sc_s[...] * ATT_SCALE, NEG)
        sp = jnp.where(col >= row, sp_s[...] * ATT_SCALE, NEG) + jnp.where(i > 0, 0.0, NEG)
        m = jnp.maximum(jnp.max(sc, axis=-1, keepdims=True), jnp.max(sp, axis=-1, keepdims=True))
        ec = jnp.exp(sc - m)
        ep = jnp.exp(sp - m)
        den = jnp.sum(ec, axis=-1, keepdims=True) + jnp.sum(ep, axis=-1, keepdims=True)
        inv = 1.0 / den
        pc_s[...] = (ec * inv).astype(BF16)
        pp_s[...] = (ep * inv).astype(BF16)
        lse = m + jnp.log(den)
        lane = lax.broadcasted_iota(jnp.int32, (CHUNK, 128), 1)
        for r in range(d):
            lblk = jnp.zeros((CHUNK, 128), F32)
            for j in range(hp):
                o = _dot(pc_s[bs(r, j), :], blk(vf, r, j)) + _dot(pp_s[bs(r, j), :], blk(vpf, r, j))
                of[j, r * CHUNK:(r + 1) * CHUNK, :] = o
                lblk = jnp.where(lane == hh * hp + j, lse[bs(r, j)], lblk)
            lf[r * CHUNK:(r + 1) * CHUNK, :] = lblk
        _from_class(of, o_ref, d, hp)
        lnat = pltpu.einshape("(rt)l->(tr)l", lf[...], r=d)

        @pl.when(hh == 0)
        def _():
            l_ref[...] = lnat

        @pl.when(hh > 0)
        def _():
            l_ref[...] += lnat

    def cols(cb):
        return pl.BlockSpec((T, cw), lambda i, hh: (i, cb * nh + hh))

    tile = pltpu.VMEM((hp, T, HEAD), BF16)
    return pl.pallas_call(
        body, name=f"attn_fwd_dilated_{d}",
        grid=(S // T, nh),
        in_specs=[cols(qcb), cols(kcb), cols(vcb)],
        out_specs=[cols(0), pl.BlockSpec((T, 128), lambda i, hh: (i, 0))],
        out_shape=[jax.ShapeDtypeStruct((S, WIDTH), BF16), jax.ShapeDtypeStruct((S, 128), F32)],
        scratch_shapes=[tile, pltpu.VMEM((2, nh, hp, T, HEAD), BF16), pltpu.VMEM((2, nh, hp, T, HEAD), BF16),
                        pltpu.VMEM((hp, T, HEAD), F32), pltpu.VMEM((T, 128), F32),
                        pltpu.VMEM((nblocks * CHUNK, CHUNK), F32), pltpu.VMEM((nblocks * CHUNK, CHUNK), F32),
                        pltpu.VMEM((nblocks * CHUNK, CHUNK), BF16), pltpu.VMEM((nblocks * CHUNK, CHUNK), BF16)],
        compiler_params=_params(("arbitrary", "arbitrary")),
    )(proj, proj, proj)


def _attn_bwd_dilated(proj, qcb, kcb, vcb, do, lse, delta, d):
    S = proj.shape[0]
    T = CHUNK * d
    nt = S // T
    hp, nh, cw = _dilated_split(d)
    nblocks = d * hp

    def body(q_ref, k_ref, v_ref, do_ref, l_ref, d_ref, dq_ref, dk_ref, dv_ref,
             qf, dof, kbuf, vbuf, dqf, gk, gv,
             sc_s, sp_s, dpc_s, dpp_s, pc_s, pp_s, dsc_s, dsp_s):
        hh, i = pl.program_id(0), pl.program_id(1)
        kf, vf, newk, newv = kbuf.at[i % 2], vbuf.at[i % 2], gk.at[i % 2], gv.at[i % 2]
        kpf, vpf, acck, accv = kbuf.at[1 - i % 2], vbuf.at[1 - i % 2], gk.at[1 - i % 2], gv.at[1 - i % 2]

        @pl.when(i == 0)
        def _():
            for ref in (kbuf, vbuf, gk, gv):
                ref[...] = jnp.zeros_like(ref)
            dk_ref[...] = jnp.zeros_like(dk_ref)
            dv_ref[...] = jnp.zeros_like(dv_ref)

        def blk(ref, r, j):
            return ref[j, r * CHUNK:(r + 1) * CHUNK, :]

        def bs(r, j):
            return slice((r * hp + j) * CHUNK, (r * hp + j + 1) * CHUNK)

        @pl.when(i < nt)
        def _():
            _by_class(q_ref, qf, d, hp)
            _by_class(do_ref, dof, d, hp)
            _by_class(k_ref, kf, d, hp)
            _by_class(v_ref, vf, d, hp)
            lses, deltas = [], []
            lcls = pltpu.einshape("(tr)l->(rt)l", l_ref[...], r=d)
            dcls = pltpu.einshape("(tr)l->(rt)l", d_ref[...], r=d)
            for r in range(d):
                lblk = lcls[r * CHUNK:(r + 1) * CHUNK]
                dblk = dcls[r * CHUNK:(r + 1) * CHUNK]
                for j in range(hp):
                    lses.append(_col(lblk, hh * hp + j))
                    deltas.append(_col(dblk, hh * hp + j))
                    qb, dob = blk(qf, r, j), blk(dof, r, j)
                    sc_s[bs(r, j), :] = _dot_nt(qb, blk(kf, r, j))
                    dpc_s[bs(r, j), :] = _dot_nt(dob, blk(vf, r, j))
                    sp_s[bs(r, j), :] = _dot_nt(qb, blk(kpf, r, j))
                    dpp_s[bs(r, j), :] = _dot_nt(dob, blk(vpf, r, j))
            lse = jnp.concatenate(lses, axis=0)
            delta = jnp.concatenate(deltas, axis=0)
            row = lax.broadcasted_iota(jnp.int32, (nblocks * CHUNK, CHUNK), 0) & (CHUNK - 1)
            col = lax.broadcasted_iota(jnp.int32, (nblocks * CHUNK, CHUNK), 1)
            sp = jnp.where(col >= row, sp_s[...] * ATT_SCALE, NEG) + jnp.where(i > 0, 0.0, NEG)
            pp = jnp.exp(sp - lse)
            pp_s[...] = pp.astype(BF16)
            dsp_s[...] = (pp * (dpp_s[...] - delta)).astype(BF16)
            sc = jnp.where(col <= row, sc_s[...] * ATT_SCALE, NEG)
            pc = jnp.exp(sc - lse)
            pc_s[...] = pc.astype(BF16)
            dsc_s[...] = (pc * (dpc_s[...] - delta)).astype(BF16)
            for r in range(d):
                rows = slice(r * CHUNK, (r + 1) * CHUNK)
                for j in range(hp):
                    qb, dob = blk(qf, r, j), blk(dof, r, j)
                    dsc, dsp = dsc_s[bs(r, j), :], dsp_s[bs(r, j), :]
                    dqf[j, rows, :] = (_dot(dsc, blk(kf, r, j)) + _dot(dsp, blk(kpf, r, j))) * ATT_SCALE
                    newk[j, rows, :] = _dot_tn(dsc, qb) * ATT_SCALE
                    newv[j, rows, :] = _dot_tn(pc_s[bs(r, j), :], dob)
                    acck[j, rows, :] += _dot_tn(dsp, qb) * ATT_SCALE
                    accv[j, rows, :] += _dot_tn(pp_s[bs(r, j), :], dob)
            _from_class(dqf, dq_ref, d, hp)

        @pl.when(i > 0)
        def _():
            _from_class(acck, dk_ref, d, hp)
            _from_class(accv, dv_ref, d, hp)

    def cur(width, cb, nsplit):
        return pl.BlockSpec((T, width), lambda hh, i: (jnp.minimum(i, nt - 1), cb * nsplit + hh * (nsplit > 1)))

    def lag():
        return pl.BlockSpec((T, cw), lambda hh, i: (jnp.maximum(i - 1, 0), hh))

    tile = pltpu.VMEM((hp, T, HEAD), BF16)
    acc = pltpu.VMEM((hp, T, HEAD), F32)
    f32s = pltpu.VMEM((nblocks * CHUNK, CHUNK), F32)
    b16s = pltpu.VMEM((nblocks * CHUNK, CHUNK), BF16)
    return pl.pallas_call(
        body, name=f"attn_bwd_dilated_{d}",
        grid=(nh, nt + 1),
        in_specs=[cur(cw, qcb, nh), cur(cw, kcb, nh), cur(cw, vcb, nh), cur(cw, 0, nh), cur(128, 0, 1), cur(128, 0, 1)],
        out_specs=[cur(cw, 0, nh), lag(), lag()],
        out_shape=[jax.ShapeDtypeStruct((S, WIDTH), BF16)] * 3,
        scratch_shapes=[tile, tile, pltpu.VMEM((2, hp, T, HEAD), BF16), pltpu.VMEM((2, hp, T, HEAD), BF16), acc,
                        pltpu.VMEM((2, hp, T, HEAD), F32), pltpu.VMEM((2, hp, T, HEAD), F32)]
                       + [f32s] * 4 + [b16s] * 4,
        compiler_params=_params(("arbitrary", "arbitrary")),
    )(proj, proj, proj, do, lse, delta)


def _abm_bwd(proj, cdf, dy3, ln_g, ln_b, wsm, wsm_t, bias_full, pool_w, pool_wt, pool_scale, kv):
    S = proj.shape[0]
    tm = 512
    nchunk = tm // CHUNK
    nblk = S // CHUNK

    def body(u_ref, v_ref, ag_ref, p_ref, ph_ref, pg_ref, pgn_ref, mq_ref, mg_ref, cdf_ref, dy_ref, dypn_ref,
             lng_ref, lnb_ref, wsm_ref, wsmt_ref, bias_ref, pw_ref, pwt_ref, ps_ref, kv_ref,
             dab_ref, dm_ref, dlng_ref, dlnb_ref, dws_ref, dbias_ref, dpw_ref, dps_ref, dkv_ref,
             mix, dvl, ddn):
        i = pl.program_id(0)

        @pl.when(i == 0)
        def _():
            for r in (dlng_ref, dlnb_ref, dws_ref, dbias_ref, dpw_ref, dps_ref, dkv_ref):
                r[...] = jnp.zeros_like(r)

        au = u_ref[...].astype(F32)
        av = v_ref[...].astype(F32)
        ag = ag_ref[...].astype(F32)
        u, du = _gelu_and_grad(au, cdf_ref[0].astype(F32))
        v, dgelu_v = _gelu_and_grad(av, cdf_ref[1].astype(F32))
        vhat, rstd = _layer_norm_fwd(v)
        vln = (vhat * lng_ref[...] + lnb_ref[...]).astype(BF16)
        for c in range(nchunk):
            for h in range(N_HEAD):
                rs, cs = slice(c * CHUNK, (c + 1) * CHUNK), slice(h * HEAD, (h + 1) * HEAD)
                mix[rs, cs] = _dot(wsm_ref[h], vln[rs, cs]) + bias_ref[:, cs]
        dya = dy_ref[0].astype(F32)
        sg, dsg = _silu_and_grad(ag)
        mixed = mix[...]
        dab_ref[:, 2 * WIDTH:3 * WIDTH] = (dya * u * mixed * dsg).astype(BF16)
        dab_ref[:, 0:WIDTH] = (dya * mixed * sg * du).astype(BF16)
        dmixed = dya * u * sg
        dmb = dmixed.astype(BF16)
        tril = (lax.broadcasted_iota(jnp.int32, (CHUNK, CHUNK), 1)
                <= lax.broadcasted_iota(jnp.int32, (CHUNK, CHUNK), 0))
        for c in range(nchunk):
            rs = slice(c * CHUNK, (c + 1) * CHUNK)
            dbias_ref[...] += dmixed[rs, :]
            for h in range(N_HEAD):
                cs = slice(h * HEAD, (h + 1) * HEAD)
                dvl[rs, cs] = _dot(wsmt_ref[h], dmb[rs, cs])
                dws_ref[h] += jnp.where(tril, _dot_nt(dmb[rs, cs], vln[rs, cs]), 0.0)
        dvln = dvl[...]
        dlng_ref[...] += jnp.sum(dvln * vhat, axis=0, keepdims=True)
        dlnb_ref[...] += jnp.sum(dvln, axis=0, keepdims=True)
        dvh = dvln * lng_ref[...]
        dv = rstd * (dvh - jnp.mean(dvh, axis=-1, keepdims=True)
                     - vhat * jnp.mean(dvh * vhat, axis=-1, keepdims=True))
        dab_ref[:, WIDTH:2 * WIDTH] = (dv * dgelu_v).astype(BF16)

        halo_ok = (i > 0).astype(F32)
        for c in range(nchunk):
            rs = slice(c * CHUNK, (c + 1) * CHUNK)
            for g, win in enumerate(POOL_WINDOWS):
                cs = slice(g * HEAD, (g + 1) * HEAD)
                bcur, bprev = _band_masks(win)
                cur = p_ref[rs, cs]
                if c == 0:
                    prev = (ph_ref[:, cs].astype(F32) * halo_ok).astype(BF16)
                else:
                    prev = p_ref[(c - 1) * CHUNK:c * CHUNK, cs]
                sums = _dot(bcur, cur) + _dot(bprev, prev)
                dvl[rs, cs] = sums * _inv_count(i * tm + c * CHUNK, win) - cur.astype(F32)
        dmat = dvl[...].astype(BF16)
        for g in range(4):
            cs = slice(g * HEAD, (g + 1) * HEAD)
            mix[:, cs] = _dot(dmat[:, cs], pw_ref[g])
        yg = mix[...]
        pg = pg_ref[...].astype(F32)
        dyp = dy_ref[1].astype(F32)
        spg, dspg = _silu_and_grad(pg)
        dyy = dyp * spg
        scale = ps_ref[...]
        dab_ref[:, 4 * WIDTH:5 * WIDTH] = (dyp * yg * scale * dspg).astype(BF16)
        dps_ref[...] += jnp.sum(dyy * yg, axis=0, keepdims=True)
        dyg = (dyy * scale).astype(BF16)
        for g in range(4):
            cs = slice(g * HEAD, (g + 1) * HEAD)
            dpw_ref[g] += _dot_tn(dmat[:, cs], dyg[:, cs])
            mix[:, cs] = _dot(dyg[:, cs], pwt_ref[g])
        next_ok = (i + 1 < S // tm).astype(F32)
        dygn = (dypn_ref[...].astype(F32) * _silu(pgn_ref[...].astype(F32)) * scale * next_ok).astype(BF16)
        for c in range(nchunk + 1):
            for g, win in enumerate(POOL_WINDOWS):
                cs = slice(g * HEAD, (g + 1) * HEAD)
                if c < nchunk:
                    dd = mix[c * CHUNK:(c + 1) * CHUNK, cs]
                else:
                    dd = _dot(dygn[:, cs], pwt_ref[g])
                ddn[c * CHUNK:(c + 1) * CHUNK, cs] = dd * _inv_count(i * tm + c * CHUNK, win)
        ddnb = ddn[...].astype(BF16)
        for c in range(nchunk):
            rs = slice(c * CHUNK, (c + 1) * CHUNK)
            ns = slice((c + 1) * CHUNK, (c + 2) * CHUNK)
            for g, win in enumerate(POOL_WINDOWS):
                cs = slice(g * HEAD, (g + 1) * HEAD)
                bcur, bprev = _band_masks(win)
                dp = _dot_tn(bcur, ddnb[rs, cs]) + _dot_tn(bprev, ddnb[ns, cs]) - mix[rs, cs]
                dab_ref[rs, 3 * WIDTH + g * HEAD:3 * WIDTH + (g + 1) * HEAD] = dp.astype(BF16)

        mg = mg_ref[...].astype(F32)
        dym = dy_ref[2].astype(F32)
        smg, dsmg = _silu_and_grad(mg)
        dob = (dym * smg).astype(BF16)
        for h in range(N_HEAD):
            cs = slice(h * HEAD, (h + 1) * HEAD)
            vs = slice(WIDTH + h * HEAD, WIDTH + (h + 1) * HEAD)
            qh = mq_ref[:, cs]
            p = _mem_softmax(qh, kv_ref[:, cs])
            pb = p.astype(BF16)
            mix[:, cs] = _dot(pb, kv_ref[:, vs])
            dp = _dot_nt(dob[:, cs], kv_ref[:, vs])
            ds = (p * (dp - jnp.sum(p * dp, axis=-1, keepdims=True))).astype(BF16)
            dm_ref[:, cs] = (_dot(ds, kv_ref[:, cs]) * ATT_SCALE).astype(BF16)
            dkv_ref[:, cs] += _dot_tn(ds, qh) * ATT_SCALE
            dkv_ref[:, vs] += _dot_tn(pb, dob[:, cs])
        dm_ref[:, WIDTH:2 * WIDTH] = (dym * mix[...] * dsmg).astype(BF16)

    blk = tm // CHUNK
    small = [_full((1, WIDTH)), _full((1, WIDTH)), _full((N_HEAD, CHUNK, CHUNK)), _full((CHUNK, WIDTH)),
             _full((4, HEAD, HEAD)), _full((1, WIDTH)), _full((MEM_LEN, 2 * WIDTH))]
    return pl.pallas_call(
        body, name="abm_bwd",
        grid=(S // tm,),
        in_specs=[_rows(tm, WIDTH, CB_U), _rows(tm, WIDTH, CB_V), _rows(tm, WIDTH, CB_AGATE),
                  _rows(tm, WIDTH, CB_PIN),
                  pl.BlockSpec((CHUNK, WIDTH), lambda i: (jnp.maximum(i * blk - 1, 0), CB_PIN)),
                  _rows(tm, WIDTH, CB_PGATE),
                  pl.BlockSpec((CHUNK, WIDTH), lambda i: (jnp.minimum(i * blk + blk, nblk - 1), CB_PGATE)),
                  _rows(tm, WIDTH, CB_MQ), _rows(tm, WIDTH, CB_MGATE),
                  pl.BlockSpec((2, tm, WIDTH), lambda i: (0, i, 0)),
                  pl.BlockSpec((3, tm, WIDTH), lambda i: (0, i, 0)),
                  pl.BlockSpec((None, CHUNK, WIDTH), lambda i: (1, jnp.minimum(i * blk + blk, nblk - 1), 0)),
                  _full((1, WIDTH)), _full((1, WIDTH)), _full((N_HEAD, CHUNK, CHUNK)), _full((N_HEAD, CHUNK, CHUNK)),
                  _full((CHUNK, WIDTH)), _full((4, HEAD, HEAD)), _full((4, HEAD, HEAD)), _full((1, WIDTH)),
                  _full((MEM_LEN, 2 * WIDTH))],
        out_specs=[_rows(tm, 5 * WIDTH), _rows(tm, 2 * WIDTH)] + small,
        out_shape=[jax.ShapeDtypeStruct((S, D_BRANCHES), BF16), jax.ShapeDtypeStruct((S, 2 * WIDTH), BF16),
                   jax.ShapeDtypeStruct((1, WIDTH), F32), jax.ShapeDtypeStruct((1, WIDTH), F32),
                   jax.ShapeDtypeStruct((N_HEAD, CHUNK, CHUNK), F32), jax.ShapeDtypeStruct((CHUNK, WIDTH), F32),
                   jax.ShapeDtypeStruct((4, HEAD, HEAD), F32), jax.ShapeDtypeStruct((1, WIDTH), F32),
                   jax.ShapeDtypeStruct((MEM_LEN, 2 * WIDTH), F32)],
        scratch_shapes=[pltpu.VMEM((tm, WIDTH), F32), pltpu.VMEM((tm, WIDTH), F32),
                        pltpu.VMEM((tm + CHUNK, WIDTH), F32)],
        compiler_params=_params(("arbitrary",)),
    )(proj, proj, proj, proj, proj, proj, proj, proj, proj, cdf, dy3, dy3,
      ln_g, ln_b, wsm, wsm_t, bias_full, pool_w, pool_wt, pool_scale, kv)


def _bias_reduce(dbias_full):
    def body(d_ref, o_ref):
        d = d_ref[...]
        o_ref[...] = _put_cols([jnp.sum(d[:, h * HEAD:(h + 1) * HEAD], axis=1, keepdims=True) for h in range(N_HEAD)])

    return pl.pallas_call(body, name="bias_reduce", out_shape=jax.ShapeDtypeStruct((CHUNK, 128), F32))(dbias_full)


def _mem_bwd(mem, g, mem_n, w, dkv):
    def body(m_ref, g_ref, mn_ref, w_ref, dkv_ref, dw_ref, dg_ref):
        dkvb = dkv_ref[...].astype(BF16)
        dw_ref[...] = _dot_tn(mn_ref[...], dkvb).astype(BF16)
        dmn = _dot_nt(dkvb, w_ref[...])
        xf = m_ref[...]
        r = lax.rsqrt(jnp.mean(xf * xf, axis=-1, keepdims=True) + EPS)
        dg_ref[...] = jnp.sum(dmn * xf * r, axis=0, keepdims=True)

    return pl.pallas_call(
        body, name="mem_bwd",
        out_shape=[jax.ShapeDtypeStruct((D_MODEL, 2 * WIDTH), BF16), jax.ShapeDtypeStruct((1, D_MODEL), F32)],
        compiler_params=pltpu.CompilerParams(vmem_limit_bytes=VMEM_LIMIT),
    )(mem, g, mem_n, w, dkv)


def _dh_bwd(dpb, dpg, wt, wgt, x, g, dxo, parts=()):
    S = x.shape[0]
    tm, tkb, tkg = 1024, D_BRANCHES // 4, D_GATES // 4
    nkb, nkg = 4, 4
    nk = nkb + nkg
    ni = S // tm
    n = len(parts)

    def body(dpb_ref, wbr_ref, dpg_ref, wg_ref, x_ref, g_ref, dxo_ref, *rest):
        p_in = rest[:n]
        dx_ref, dg_ref = rest[n:n + 2]
        p_out, acc, sems = rest[n + 2:2 * n + 2], rest[2 * n + 2], rest[2 * n + 3:]
        i, kk = pl.program_id(0), pl.program_id(1)

        @pl.when(jnp.logical_and(i == 0, kk == 0))
        def _():
            dg_ref[...] = jnp.zeros_like(dg_ref)
            if n:
                _comm_start(_rs_second(p_in, p_out, *sems))

        @pl.when(kk == 0)
        def _():
            acc[...] = jnp.zeros_like(acc)

        @pl.when(kk < nkb)
        def _():
            acc[...] += _dot(dpb_ref[...], wbr_ref[...])

        @pl.when(kk >= nkb)
        def _():
            acc[...] += _dot(dpg_ref[...], wg_ref[...])

        @pl.when(kk == nk - 1)
        def _():
            xf = x_ref[...]
            r = lax.rsqrt(jnp.mean(xf * xf, axis=-1, keepdims=True) + EPS)
            xhat = xf * r
            dh = acc[...]
            dg_ref[...] += jnp.sum(dh * xhat, axis=0, keepdims=True)
            dxh = dh * g_ref[...]
            dx_ref[...] = dxo_ref[...] + r * (dxh - xhat * jnp.mean(dxh * xhat, axis=-1, keepdims=True))

        if n:
            @pl.when(jnp.logical_and(i == ni - 1, kk == nk - 1))
            def _():
                _comm_wait(_rs_second(p_in, p_out, *sems))

    res = pl.pallas_call(
        body, name="dh_bwd_scatter" if n else "dh_bwd",
        grid=(ni, nk),
        in_specs=[pl.BlockSpec((tm, tkb), lambda i, k: (i, jnp.minimum(k, nkb - 1))),
                  pl.BlockSpec((tkb, D_MODEL), lambda i, k: (jnp.minimum(k, nkb - 1), 0)),
                  pl.BlockSpec((tm, tkg), lambda i, k: (i, jnp.maximum(k - nkb, 0))),
                  pl.BlockSpec((tkg, D_MODEL), lambda i, k: (jnp.maximum(k - nkb, 0), 0)),
                  pl.BlockSpec((tm, D_MODEL), lambda i, k: (i, 0)), pl.BlockSpec((1, D_MODEL), lambda i, k: (0, 0)),
                  pl.BlockSpec((tm, D_MODEL), lambda i, k: (i, 0))] + [ANY] * n,
        out_specs=[pl.BlockSpec((tm, D_MODEL), lambda i, k: (i, 0)), pl.BlockSpec((1, D_MODEL), lambda i, k: (0, 0))]
                  + [ANY] * n,
        out_shape=[jax.ShapeDtypeStruct((S, D_MODEL), F32), jax.ShapeDtypeStruct((1, D_MODEL), F32)]
                  + [jax.ShapeDtypeStruct(p.shape, p.dtype) for p in parts],
        scratch_shapes=[pltpu.VMEM((tm, D_MODEL), F32)] + (_dma_sems(3 * n, 3 * n, n) if n else []),
        compiler_params=_params(("arbitrary", "arbitrary")),
    )(dpb, wt, dpg, wgt, x, g, dxo, *parts)
    return res[0], res[1], list(res[2:])


def _dw_in(h, dpb, dpg, parts=()):
    S = h.shape[0]
    tk = 2048
    nk = S // tk
    n = len(parts)
    tmb = D_BRANCHES // 4
    ng = D_GATES // GATE_TILE

    def accumulate(a_ref, h_ref, o_ref, acc):
        kk = pl.program_id(1)

        @pl.when(kk == 0)
        def _():
            acc[...] = jnp.zeros_like(acc)

        acc[...] += _dot_tn(a_ref[...], h_ref[...])

        @pl.when(kk == nk - 1)
        def _():
            o_ref[...] = acc[...].astype(BF16)

    def branches(a_ref, h_ref, *rest):
        p_in, o_ref, p_out = rest[:n], rest[n], rest[n + 1:2 * n + 1]
        acc, sems = rest[2 * n + 1], rest[2 * n + 2:]
        i, kk = pl.program_id(0), pl.program_id(1)

        if n:
            @pl.when(jnp.logical_and(i == 0, kk == 0))
            def _():
                _comm_start(_rs_second(p_in, p_out, *sems))

        accumulate(a_ref, h_ref, o_ref, acc)

        if n:
            @pl.when(jnp.logical_and(i == 3, kk == nk - 1))
            def _():
                _comm_wait(_rs_second(p_in, p_out, *sems))

    def gates(a_ref, h_ref, dst_ref, o_ref, acc):
        accumulate(a_ref, h_ref, o_ref, acc)

    res = pl.pallas_call(
        branches, name="dw_in_branches_scatter" if n else "dw_in_branches",
        grid=(4, nk),
        in_specs=[pl.BlockSpec((tk, tmb), lambda i, k: (k, i)), pl.BlockSpec((tk, D_MODEL), lambda i, k: (k, 0))]
                 + [ANY] * n,
        out_specs=[pl.BlockSpec((tmb, D_MODEL), lambda i, k: (i, 0))] + [ANY] * n,
        out_shape=[jax.ShapeDtypeStruct((D_IN, D_MODEL), BF16)]
                  + [jax.ShapeDtypeStruct(p.shape, p.dtype) for p in parts],
        scratch_shapes=[pltpu.VMEM((tmb, D_MODEL), F32)] + (_dma_sems(3 * n, 3 * n, n) if n else []),
        compiler_params=_params(("arbitrary", "arbitrary")),
    )(dpb, h, *parts)
    dwt = pl.pallas_call(
        gates, name="dw_in_gates",
        grid=(ng, nk),
        in_specs=[pl.BlockSpec((tk, GATE_TILE), lambda i, k: (k, i)), pl.BlockSpec((tk, D_MODEL), lambda i, k: (k, 0)),
                  ANY],
        out_specs=pl.BlockSpec((GATE_TILE, D_MODEL), lambda i, k: (D_BRANCHES // GATE_TILE + i, 0)),
        out_shape=jax.ShapeDtypeStruct((D_IN, D_MODEL), BF16),
        input_output_aliases={2: 0},
        scratch_shapes=[pltpu.VMEM((GATE_TILE, D_MODEL), F32)],
        compiler_params=_params(("parallel", "arbitrary")),
    )(dpg, h, res[0])
    return dwt, list(res[1:])


def _row_tile(R, C, block_bytes=2 << 20):
    for cand in range(min(R, block_bytes // (C * 4)) // 8 * 8, 0, -8):
        if R % cand == 0:
            return cand
    return R


def _adamw_update(p_ref, w_ref, m_ref, v_ref, g_ref, d_ref, nm_ref, nv_ref):
    c1 = 1.0 / (1.0 - ADAM_B1 ** ADAM_STEP)
    c2 = 1.0 / (1.0 - ADAM_B2 ** ADAM_STEP)
    g = p_ref[0].astype(F32)
    for k in range(1, p_ref.shape[0]):
        g = g + p_ref[k].astype(F32)
    nm = ADAM_B1 * m_ref[...] + (1.0 - ADAM_B1) * g
    nv = ADAM_B2 * v_ref[...] + (1.0 - ADAM_B2) * (g * g)
    g_ref[...] = g
    nm_ref[...] = nm
    nv_ref[...] = nv
    d_ref[...] = -ADAM_LR * ((nm * c1) / (jnp.sqrt(nv * c2) + ADAM_EPS) + ADAM_WD * w_ref[...])


def _adamw(parts, w, m, v, name):
    P, R, C = parts.shape
    tr = _row_tile(R, C)

    def body(*refs):
        _adamw_update(*refs)

    spec = pl.BlockSpec((tr, C), lambda i: (i, 0))
    return pl.pallas_call(
        body, name=name,
        grid=(R // tr,),
        in_specs=[pl.BlockSpec((P, tr, C), lambda i: (0, i, 0)), spec, spec, spec],
        out_specs=[spec] * 4,
        out_shape=[jax.ShapeDtypeStruct((R, C), F32)] * 4,
        compiler_params=_params(("parallel",)),
    )(parts, w, m, v)


def _adamw_layers(parts, w, m, v, name):
    depth = len(parts)
    P, R, C = parts[0].shape
    tr = _row_tile(R, C, 1 << 20)

    def body(*refs):
        layer = pl.program_id(0)
        for k in range(depth):
            @pl.when(layer == k)
            def _(k=k):
                _adamw_update(refs[k], *refs[depth:])

    def part_spec(k):
        return pl.BlockSpec((P, tr, C), lambda l, i: (0, jnp.where(l == k, i, 0), 0))

    spec = pl.BlockSpec((None, tr, C), lambda l, i: (l, i, 0))
    return pl.pallas_call(
        body, name=name,
        grid=(depth, R // tr),
        in_specs=[part_spec(k) for k in range(depth)] + [spec] * 3,
        out_specs=[spec] * 4,
        out_shape=[jax.ShapeDtypeStruct((depth, R, C), F32)] * 4,
        compiler_params=_params(("arbitrary", "arbitrary")),
    )(*parts, w, m, v)


def _place():
    return lax.axis_index("x"), lax.axis_index("y"), lax.axis_index("c")


def _all_gather(shards):
    n = len(shards)

    def body(*refs):
        ins, outs = refs[:n], refs[n:2 * n]
        send1, recv1, local_sems, send2, recv2 = refs[2 * n:]
        first = _ag_first(ins, outs, send1, recv1, local_sems)
        second = _ag_second(outs, send2, recv2)
        _comm_start(first)
        for j in range(3):
            for a in range(n):
                first[2][4 * a + 1 + j].wait_recv()
            for a in range(n):
                second[1][3 * a + j].start()
        for a in range(n):
            first[2][4 * a].wait_recv()
        for cp in second[2]:
            cp.wait_recv()
        for cp in first[1] + second[1]:
            cp.wait_send()
        for cp in first[0]:
            cp.wait()

    return pl.pallas_call(
        body, name="weights_all_gather",
        in_specs=[ANY] * n, out_specs=[ANY] * n,
        out_shape=[jax.ShapeDtypeStruct((N_DEV,) + s.shape, s.dtype) for s in shards],
        scratch_shapes=_dma_sems(4 * n, 4 * n, n, 3 * n, 3 * n),
        compiler_params=pltpu.CompilerParams(has_side_effects=True),
    )(*shards)


N_BIG = 4


def _dev(p):
    return 4 * p[0] + 2 * p[1] + p[2]


def _other_chips(x, y):
    return [(1 - x, y), (x, 1 - y), (1 - x, 1 - y)]


def _remote(src, dst, send_sems, recv_sems, k, to):
    return pltpu.make_async_remote_copy(src_ref=src, dst_ref=dst, send_sem=send_sems.at[k], recv_sem=recv_sems.at[k],
                                        device_id=to, device_id_type=MESH)


def _ag_first(ins, outs, send_sems, recv_sems, local_sems):
    x, y, c = _place()
    me = (x, y, c)
    targets = [(x, y, 1 - c)] + [(*chip, c) for chip in _other_chips(x, y)]
    local, out, inc = [], [], []
    for a in range(len(ins)):
        local.append(pltpu.make_async_copy(ins[a], outs[a].at[_dev(me)], local_sems.at[a]))
        for k, to in enumerate(targets):
            out.append(_remote(ins[a], outs[a].at[_dev(me)], send_sems, recv_sems, 4 * a + k, to))
            inc.append(_remote(ins[a], outs[a].at[_dev(to)], send_sems, recv_sems, 4 * a + k, to))
    return local, out, inc


def _ag_second(bufs, send_sems, recv_sems):
    x, y, c = _place()
    out, inc = [], []
    for a in range(len(bufs)):
        for j, chip in enumerate(_other_chips(x, y)):
            mine, theirs = bufs[a].at[_dev((*chip, c))], bufs[a].at[_dev((*chip, 1 - c))]
            out.append(_remote(mine, mine, send_sems, recv_sems, 3 * a + j, (x, y, 1 - c)))
            inc.append(_remote(theirs, theirs, send_sems, recv_sems, 3 * a + j, (x, y, 1 - c)))
    return [], out, inc


def _rs_first(ins, outs, send_sems, recv_sems):
    x, y, c = _place()
    out = [_remote(ins[a].at[j, 1 - c], outs[a].at[j], send_sems, recv_sems, N_CHIP * a + j, (x, y, 1 - c))
           for a in range(len(ins)) for j in range(N_CHIP)]
    return [], out, out


def _rs_second(ins, outs, send_sems, recv_sems, local_sems):
    x, y, c = _place()
    my_chip = 2 * x + y
    local, out, inc = [], [], []
    for a in range(len(ins)):
        local.append(pltpu.make_async_copy(ins[a].at[my_chip], outs[a].at[my_chip], local_sems.at[a]))
        for k, (ox, oy) in enumerate(_other_chips(x, y)):
            out.append(_remote(ins[a].at[2 * ox + oy], outs[a].at[my_chip], send_sems, recv_sems, 3 * a + k, (ox, oy, c)))
            inc.append(_remote(ins[a].at[2 * ox + oy], outs[a].at[2 * ox + oy], send_sems, recv_sems, 3 * a + k,
                               (ox, oy, c)))
    return local, out, inc


def _comm_start(exchange):
    local, out, _ = exchange
    for cp in local + out:
        cp.start()


def _comm_wait(exchange):
    local, out, inc = exchange
    for cp in inc:
        cp.wait_recv()
    for cp in out:
        cp.wait_send()
    for cp in local:
        cp.wait()


def _dma_sems(*counts):
    return [pltpu.SemaphoreType.DMA((n,)) for n in counts]


def _rs_sibling(grads):
    n = len(grads)

    def body(*refs):
        ex = _rs_first(refs[:n], refs[n:2 * n], *refs[2 * n:])
        _comm_start(ex)
        _comm_wait(ex)

    return pl.pallas_call(
        body, name="grads_to_sibling",
        in_specs=[ANY] * n, out_specs=[ANY] * n,
        out_shape=[jax.ShapeDtypeStruct(g.shape[:1] + g.shape[2:], g.dtype) for g in grads],
        scratch_shapes=_dma_sems(N_CHIP * n, N_CHIP * n),
        compiler_params=pltpu.CompilerParams(has_side_effects=True),
    )(*grads)


def _pair_sum(grads, recvs):
    n = len(grads)

    def body(c_ref, *refs):
        for a in range(n):
            refs[2 * n + a][...] = (refs[a][...].astype(F32) + refs[n + a][...].astype(F32)).astype(BF16)

    def g_spec(g):
        return pl.BlockSpec((None, None) + g.shape[2:], lambda j, c_ref: (j, c_ref[0], 0, 0))

    def r_spec(r):
        return pl.BlockSpec((None,) + r.shape[1:], lambda j, c_ref: (j, 0, 0))

    return pl.pallas_call(
        body, name="pair_sum",
        grid_spec=pltpu.PrefetchScalarGridSpec(
            num_scalar_prefetch=1, grid=(N_CHIP,),
            in_specs=[g_spec(g) for g in grads] + [r_spec(r) for r in recvs],
            out_specs=[r_spec(r) for r in recvs]),
        out_shape=[jax.ShapeDtypeStruct(r.shape, BF16) for r in recvs],
        compiler_params=_params(("parallel",)),
    )(lax.axis_index("c").reshape(1).astype(jnp.int32), *grads, *recvs)


SMALL_ROWS = 544


def _all_reduce_small(buf, parts=()):
    n = len(parts)

    def body(in_ref, *rest):
        p_in, out_ref, p_out = rest[:n], rest[n], rest[n + 1:2 * n + 1]
        recv, acc, send1, recv1, send2, recv2 = rest[2 * n + 1:2 * n + 7]
        scatter_sems = rest[2 * n + 7:]
        x, y, c = _place()
        me = 4 * x + 2 * y + c
        peers = [(x ^ (r >> 2), y ^ ((r >> 1) & 1), c ^ (r & 1)) for r in range(1, N_DEV)]

        def idx(p):
            return 4 * p[0] + 2 * p[1] + p[2]

        if n:
            _comm_start(_rs_second(p_in, p_out, *scatter_sems))
        first = [pltpu.make_async_remote_copy(
            src_ref=in_ref.at[idx(p)], dst_ref=recv.at[me], send_sem=send1.at[r], recv_sem=recv1.at[r],
            device_id=p, device_id_type=MESH) for r, p in enumerate(peers)]
        for cp in first:
            cp.start()
        recv[me] = in_ref[me]
        for r, p in enumerate(peers):
            pltpu.make_async_remote_copy(
                src_ref=in_ref.at[idx(p)], dst_ref=recv.at[idx(p)], send_sem=send1.at[r], recv_sem=recv1.at[r],
                device_id=p, device_id_type=MESH).wait_recv()
        total = recv[0]
        for k in range(1, N_DEV):
            total = total + recv[k]
        acc[...] = total
        out_ref[me] = total
        second = [pltpu.make_async_remote_copy(
            src_ref=acc, dst_ref=out_ref.at[me], send_sem=send2.at[r], recv_sem=recv2.at[r],
            device_id=p, device_id_type=MESH) for r, p in enumerate(peers)]
        for cp in second:
            cp.start()
        for r, p in enumerate(peers):
            pltpu.make_async_remote_copy(
                src_ref=acc, dst_ref=out_ref.at[idx(p)], send_sem=send2.at[r], recv_sem=recv2.at[r],
                device_id=p, device_id_type=MESH).wait_recv()
        for cp in first + second:
            cp.wait_send()
        if n:
            _comm_wait(_rs_second(p_in, p_out, *scatter_sems))

    vm = pl.BlockSpec(memory_space=pltpu.VMEM)
    res = pl.pallas_call(
        body, name="small_grads_all_reduce",
        in_specs=[vm] + [ANY] * n, out_specs=[vm] + [ANY] * n,
        out_shape=[jax.ShapeDtypeStruct(buf.shape, F32)] + [jax.ShapeDtypeStruct(p.shape, p.dtype) for p in parts],
        scratch_shapes=[pltpu.VMEM(buf.shape, F32), pltpu.VMEM(buf.shape[1:], F32)] + _dma_sems(7, 7, 7, 7)
                       + (_dma_sems(3 * n, 3 * n, n) if n else []),
        compiler_params=pltpu.CompilerParams(has_side_effects=True, vmem_limit_bytes=VMEM_LIMIT),
    )(buf, *parts)
    return res[0], list(res[1:])


def _dilate(a, d):
    if d == 1:
        return a
    S, C = a.shape
    return a.reshape(S // d, d, C).transpose(1, 0, 2).reshape(S, C)


def _undilate(a, d):
    if d == 1:
        return a
    S, C = a.shape
    return a.reshape(d, S // d, C).transpose(1, 0, 2).reshape(S, C)


def _cols(a, cb, n=1):
    return a[:, cb * WIDTH:(cb + n) * WIDTH]


def _to_blocks(g, kind):
    if kind == "rows":
        C = g.shape[1]
        return g.reshape(N_CHIP, 2, -1, C)
    return g.reshape(4 * WIDTH, N_CHIP, 2, -1).transpose(1, 2, 0, 3)


SMALL = ("norm_g", "gm_ln_g", "gm_ln_b", "gm_ws", "gm_bs", "pool_w", "pool_scale", "mem_norm_g", "final_norm_g")


def _pack_small(tree):
    flat = jnp.concatenate([tree[k].reshape(-1, 128) for k in SMALL], axis=0)
    return jnp.pad(flat, ((0, N_DEV * SMALL_ROWS - flat.shape[0]), (0, 0)))


def _unpack_small(flat, like):
    out, at = {}, 0
    for k in SMALL:
        rows = like[k].size // 128
        out[k] = flat[at:at + rows].reshape(like[k].shape)
        at += rows
    return out


def _make_layer(wt, wkv, wb, wout, norm_g, mem_norm_g, ln_g, ln_b, gm_ws, gm_bs, pool_w, pool_scale):
    tril = jnp.tril(jnp.ones((CHUNK, CHUNK), bool))
    wsm = jnp.where(tril, gm_ws, 0.0).astype(BF16)
    pw = pool_w.astype(BF16)
    return dict(wt=wt, wgt=wt[D_BRANCHES:], wkv=wkv, wb=wb, wout=wout, g=norm_g[None], mg=mem_norm_g[None],
                ln_g=ln_g[None],
                ln_b=ln_b[None], wsm=wsm, wsm_t=wsm.transpose(0, 2, 1), pw=pw, pw_t=pw.transpose(0, 2, 1),
                ps=pool_scale[None], bias=jnp.repeat(gm_bs.T, HEAD, axis=1))


def _layer_fwd(xl, mem0, L, next_shards=()):
    S = xl.shape[0]
    proj, gates, h, half_gathered = _in_proj(xl, L["g"], L["wt"], L["wgt"], next_shards[:1])
    kv, mem_n = _mem_kv(mem0, L["mg"], L["wkv"])
    y4, cdf, gathered = _abm_fwd(proj, L["ln_g"], L["ln_b"], L["wsm"], L["bias"], L["pw"], L["ps"], kv, half_gathered)
    o_g, l_g = [], []
    for gi, d in enumerate(DILATIONS):
        if d == 1:
            o, lse = _attn_fwd(proj, CB_Q0, proj, CB_K, proj, CB_CV, S // CHUNK)
        else:
            o, lse = _attn_fwd_dilated(proj, CB_Q0 + gi, CB_K, CB_CV, d)
        o_g.append(o)
        l_g.append(lse)
    (xn, y4, oc, lse, z), rest = _merge_fwd(xl, y4, o_g, l_g, proj, gates, L["wb"], L["wout"], next_shards[1:])
    saved = dict(x=xl, proj=proj, gates=gates, h=h, kv=kv, mem_n=mem_n, y4=y4, cdf=cdf, oc=oc, lse=lse, z=z)
    return xn, saved, gathered + rest


def _place_cols(dst, piece, cb):
    return lax.dynamic_update_slice(dst, piece, (0, cb * WIDTH))


def _layer_bwd(dx, mem0, L, sv, later=()):
    S = dx.shape[0]
    proj = sv["proj"]
    (dy3, doc, delta, dcg, dgm, dwb, dwout), from_sibling = _merge_bwd(
        dx, sv["y4"], sv["oc"], sv["z"], proj, sv["gates"], L["wb"], L["wout"], later)
    pair = _pair_sum(later, from_sibling) if later else ()
    dpb, dm, dlng, dlnb, dws, dbias, dpw, dps, dkv = _abm_bwd(
        proj, sv["cdf"], dy3, L["ln_g"], L["ln_b"], L["wsm"], L["wsm_t"], L["bias"], L["pw"], L["pw_t"], L["ps"], sv["kv"])
    dk, dv = None, None
    for gi, d in enumerate(DILATIONS):
        if d == 1:
            r = _attn_bwd(proj, CB_Q0, proj, CB_K, proj, CB_CV, doc, sv["lse"], delta, S // CHUNK)
        else:
            r = _attn_bwd_dilated(proj, CB_Q0 + gi, CB_K, CB_CV, doc, sv["lse"], delta, d)
        dpb = _place_cols(dpb, r[0], CB_Q0 + gi)
        dkg, dvg = r[1].astype(F32), r[2].astype(F32)
        dk = dkg if dk is None else dk + dkg
        dv = dvg if dv is None else dv + dvg
    dpb = _place_cols(dpb, dk.astype(BF16), CB_K)
    dpb = _place_cols(dpb, dv.astype(BF16), CB_CV)
    dpb = _place_cols(dpb, dcg, CB_CGATE)
    dpb = _place_cols(dpb, dm, CB_MQ)
    dwkv, dmg = _mem_bwd(mem0, L["mg"], sv["mem_n"], L["wkv"], dkv)
    dwin_t, parts_rest = _dw_in(sv["h"], dpb, dgm, pair[1:])
    dxi, dng, parts = _dh_bwd(dpb, dgm, L["wt"], L["wgt"], sv["x"], L["g"], dx, pair[:1])
    parts = parts + parts_rest
    big = dict(w_in=dwin_t, w_mem_kv=dwkv, w_branch=dwb, w_out=dwout)
    small = dict(norm_g=dng[0], gm_ln_g=dlng[0], gm_ln_b=dlnb[0], gm_ws=dws,
                 gm_bs=_bias_reduce(dbias)[:, :N_HEAD].T, pool_w=dpw, pool_scale=dps[0], mem_norm_g=dmg[0])
    return dxi, big, small, parts


BIG = ("w_in", "w_mem_kv", "w_branch", "w_out")


def _blocked(big):
    return [_to_blocks(big["w_in"], "rows"), _to_blocks(big["w_mem_kv"], "rows"),
            _to_blocks(big["w_branch"], "branch"), _to_blocks(big["w_out"], "rows")]


def _full_weights(gathered):
    win_t, wkv, wb, wout = gathered
    return (win_t.reshape(D_IN, D_MODEL), wkv.reshape(D_MODEL, 2 * WIDTH),
            wb.reshape(N_DEV, 4, WIDTH, -1).transpose(1, 2, 0, 3).reshape(4, WIDTH, D_MODEL),
            wout.reshape(D_MODEL, D_MODEL))


def kernel(x, mem, norm_g, w_in, gm_ln_g, gm_ln_b, gm_ws, gm_bs, pool_w, pool_scale, mem_norm_g, w_mem_kv, w_branch, w_out, final_norm_g, loss_target, m_norm_g, m_w_in, m_gm_ln_g, m_gm_ln_b, m_gm_ws, m_gm_bs, m_pool_w, m_pool_scale, m_mem_norm_g, m_w_mem_kv, m_w_branch, m_w_out, m_final_norm_g, v_norm_g, v_w_in, v_gm_ln_g, v_gm_ln_b, v_gm_ws, v_gm_bs, v_pool_w, v_pool_scale, v_mem_norm_g, v_w_mem_kv, v_w_branch, v_w_out, v_final_norm_g):
    x0 = x[0]
    mem0 = mem[0]
    tgt = loss_target[0]
    S = x0.shape[0]

    shards = [[w_in[l].T.astype(BF16), w_mem_kv[l].astype(BF16), w_branch[l].astype(BF16).reshape(4 * WIDTH, -1),
               w_out[l].astype(BF16)] for l in range(DEPTH)]
    gathered = _all_gather(shards[0])
    layers, saved = [], []
    xl = x0
    for l in range(DEPTH):
        layers.append(_make_layer(*_full_weights(gathered), norm_g[l], mem_norm_g[l], gm_ln_g[l], gm_ln_b[l],
                                  gm_ws[l], gm_bs[l], pool_w[l], pool_scale[l]))
        xl, sv, gathered = _layer_fwd(xl, mem0, layers[l], shards[l + 1] if l + 1 < DEPTH else ())
        saved.append(sv)

    loss_part, dx, d_final = _loss_head(xl, final_norm_g[None], tgt)
    loss = lax.psum(loss_part[0, 0], ("x", "y", "c"))

    small = {k: [None] * DEPTH for k in SMALL if k != "final_norm_g"}
    parts = [None] * DEPTH
    later = ()
    for l in reversed(range(DEPTH)):
        dx, gb, gs, done = _layer_bwd(dx, mem0, layers[l], saved[l], later)
        if later:
            parts[l + 1] = done
        later = _blocked(gb)
        for k in gs:
            small[k][l] = gs[k]
    grad_x = dx[None]
    small_tree = {k: jnp.stack(small[k]) for k in small}
    small_tree["final_norm_g"] = d_final[0]
    reduced, parts[0] = _all_reduce_small(_pack_small(small_tree).reshape(N_DEV, SMALL_ROWS, 128),
                                          _pair_sum(later, _rs_sibling(later)))

    weights = dict(norm_g=norm_g, w_in=w_in, gm_ln_g=gm_ln_g, gm_ln_b=gm_ln_b, gm_ws=gm_ws, gm_bs=gm_bs,
                   pool_w=pool_w, pool_scale=pool_scale, mem_norm_g=mem_norm_g, w_mem_kv=w_mem_kv,
                   w_branch=w_branch, w_out=w_out, final_norm_g=final_norm_g)
    m_in = dict(norm_g=m_norm_g, w_in=m_w_in, gm_ln_g=m_gm_ln_g, gm_ln_b=m_gm_ln_b, gm_ws=m_gm_ws, gm_bs=m_gm_bs,
                pool_w=m_pool_w, pool_scale=m_pool_scale, mem_norm_g=m_mem_norm_g, w_mem_kv=m_w_mem_kv,
                w_branch=m_w_branch, w_out=m_w_out, final_norm_g=m_final_norm_g)
    v_in = dict(norm_g=v_norm_g, w_in=v_w_in, gm_ln_g=v_gm_ln_g, gm_ln_b=v_gm_ln_b, gm_ws=v_gm_ws, gm_bs=v_gm_bs,
                pool_w=v_pool_w, pool_scale=v_pool_scale, mem_norm_g=v_mem_norm_g, w_mem_kv=v_w_mem_kv,
                w_branch=v_w_branch, w_out=v_w_out, final_norm_g=v_final_norm_g)
    res = {}
    def view(k, arr):
        return arr.transpose(0, 2, 1) if k == "w_in" else arr

    for a, k in enumerate(BIG):
        shape = view(k, weights[k]).shape
        by_layer = [parts[l][a] for l in range(DEPTH)]
        lrc = (DEPTH,) + by_layer[0].shape[1:]
        outs = _adamw_layers(by_layer, view(k, weights[k]).reshape(lrc), view(k, m_in[k]).reshape(lrc),
                             view(k, v_in[k]).reshape(lrc), "adamw_" + k)
        res[k] = [view(k, o.reshape(shape)) for o in outs]
    outs = _adamw(reduced.reshape(1, N_DEV * SMALL_ROWS, 128), _pack_small(weights), _pack_small(m_in),
                  _pack_small(v_in), "adamw_small")
    unpacked = [_unpack_small(o, weights) for o in outs]
    for k in SMALL:
        res[k] = [u[k] for u in unpacked]

    order = ("norm_g", "w_in", "gm_ln_g", "gm_ln_b", "gm_ws", "gm_bs", "pool_w", "pool_scale", "mem_norm_g",
             "w_mem_kv", "w_branch", "w_out", "final_norm_g")
    return (loss, grad_x, *[res[k][0] for k in order], *[res[k][1] for k in order],
            *[res[k][2] for k in order], *[res[k][3] for k in order])
```

```python
import functools
import math

import jax
import jax.numpy as jnp
from jax import lax
from jax.experimental import pallas as pl
from jax.experimental.pallas import tpu as pltpu

F32 = jnp.float32
BF16 = jnp.bfloat16

D_MODEL = 1024
DEPTH = 4
WIDTH = 512
D_IN = 10752
HEAD = 128
N_HEAD = 4
CHUNK = 128
MEM_LEN = 256
POOL_WINDOWS = (2, 4, 8, 16)
DILATIONS = (1, 4, 16)
EPS = 1e-6
NEG = -1e30
ATT_SCALE = HEAD ** -0.5
N_DEV = 8
N_CHIP = 4

D_BRANCHES = 6656
D_GATES = D_IN - D_BRANCHES
CB_U, CB_V, CB_AGATE, CB_PIN, CB_PGATE = 0, 1, 2, 3, 4
CB_Q0, CB_K, CB_CV, CB_CGATE, CB_MQ, CB_MGATE = 5, 8, 9, 10, 11, 12

ADAM_LR = 0.001
ADAM_B1 = 0.9
ADAM_B2 = 0.999
ADAM_EPS = 1e-08
ADAM_WD = 0.01
ADAM_STEP = 10

VMEM_LIMIT = 56 * 1024 * 1024
MESH = pl.DeviceIdType.MESH
ANY = pl.BlockSpec(memory_space=pl.ANY)

NT = (((1,), (1,)), ((), ()))
TN = (((0,), (0,)), ((), ()))


def _dot(a, b):
    return jnp.dot(a, b, preferred_element_type=F32)


def _dot_nt(a, b):
    return lax.dot_general(a, b, NT, preferred_element_type=F32)


def _dot_tn(a, b):
    return lax.dot_general(a, b, TN, preferred_element_type=F32)


def _sigmoid(x):
    return 0.5 * jnp.tanh(0.5 * x) + 0.5


def _silu(x):
    return x * _sigmoid(x)


def _silu_and_grad(x):
    s = _sigmoid(x)
    return x * s, s * (1.0 + x * (1.0 - s))


def _normal_cdf(x):
    return 0.5 * (1.0 + lax.erf(x * (2.0 ** -0.5)))


def _gelu_and_grad(x, cdf):
    return x * cdf, cdf + x * jnp.exp(-0.5 * x * x) * (1.0 / math.sqrt(2.0 * math.pi))


def _col(blk, h):
    lane = lax.broadcasted_iota(jnp.int32, blk.shape, 1)
    return jnp.sum(jnp.where(lane == h, blk, 0.0), axis=1, keepdims=True)


def _put_cols(cols):
    rows = cols[0].shape[0]
    lane = lax.broadcasted_iota(jnp.int32, (rows, 128), 1)
    out = jnp.zeros((rows, 128), F32)
    for h, cv in enumerate(cols):
        out = jnp.where(lane == h, cv, out)
    return out


def _params(sem, vmem=VMEM_LIMIT):
    return pltpu.CompilerParams(dimension_semantics=sem, vmem_limit_bytes=vmem)


def _full(shape):
    nd = len(shape)
    return pl.BlockSpec(shape, lambda *_: (0,) * nd)


def _rows(tm, width, cb=0):
    return pl.BlockSpec((tm, width), lambda i: (i, cb))


GATE_TILE = 512


def _in_proj(x, g, wt, wgt, shards=()):
    S = x.shape[0]
    tm, tnb, tng = 1024, D_BRANCHES // 4, D_GATES // 4
    njb, njg = 4, 4
    n = len(shards)
    ni, nj = S // tm, njb + njg

    def body(x_ref, g_ref, wbr_ref, wg_ref, *rest):
        ins, (proj_ref, gates_ref, h_ref), outs = rest[:n], rest[n:n + 3], rest[n + 3:2 * n + 3]
        hs, sems = rest[2 * n + 3], rest[2 * n + 4:]
        i, j = pl.program_id(0), pl.program_id(1)

        if n:
            @pl.when(jnp.logical_and(i == 0, j == 0))
            def _():
                _comm_start(_ag_first(ins, outs, *sems))

        @pl.when(j == 0)
        def _():
            xf = x_ref[...]
            r = lax.rsqrt(jnp.mean(xf * xf, axis=-1, keepdims=True) + EPS)
            h = (xf * r * g_ref[...]).astype(BF16)
            hs[...] = h
            h_ref[...] = h

        @pl.when(j < njb)
        def _():
            proj_ref[...] = _dot_nt(hs[...], wbr_ref[...]).astype(BF16)

        @pl.when(j >= njb)
        def _():
            gates_ref[...] = _dot_nt(hs[...], wg_ref[...]).astype(BF16)

        if n:
            @pl.when(jnp.logical_and(i == ni - 1, j == nj - 1))
            def _():
                _comm_wait(_ag_first(ins, outs, *sems))

    def first(j):
        return jnp.minimum(j, njb - 1)

    def second(j):
        return jnp.maximum(j - njb, 0)

    res = pl.pallas_call(
        body, name="in_proj_gather" if n else "in_proj",
        grid=(ni, nj),
        in_specs=[pl.BlockSpec((tm, D_MODEL), lambda i, j: (i, 0)),
                  pl.BlockSpec((1, D_MODEL), lambda i, j: (0, 0)),
                  pl.BlockSpec((tnb, D_MODEL), lambda i, j: (first(j), 0)),
                  pl.BlockSpec((tng, D_MODEL), lambda i, j: (second(j), 0))]
                 + [ANY] * n,
        out_specs=[pl.BlockSpec((tm, tnb), lambda i, j: (i, first(j))),
                   pl.BlockSpec((tm, tng), lambda i, j: (i, second(j))),
                   pl.BlockSpec((tm, D_MODEL), lambda i, j: (i, 0))] + [ANY] * n,
        out_shape=[jax.ShapeDtypeStruct((S, D_BRANCHES), BF16), jax.ShapeDtypeStruct((S, D_GATES), BF16),
                   jax.ShapeDtypeStruct((S, D_MODEL), BF16)]
                  + [jax.ShapeDtypeStruct((N_DEV,) + s.shape, s.dtype) for s in shards],
        scratch_shapes=[pltpu.VMEM((tm, D_MODEL), BF16)] + (_dma_sems(4 * n, 4 * n, n) if n else []),
        compiler_params=_params(("arbitrary", "arbitrary")),
    )(x, g, wt, wgt, *shards)
    return res[0], res[1], res[2], list(res[3:])


def _mem_kv(mem, g, w):
    M = mem.shape[0]

    def body(m_ref, g_ref, w_ref, kv_ref, mn_ref):
        xf = m_ref[...]
        r = lax.rsqrt(jnp.mean(xf * xf, axis=-1, keepdims=True) + EPS)
        mn = (xf * r * g_ref[...]).astype(BF16)
        mn_ref[...] = mn
        kv_ref[...] = _dot(mn, w_ref[...]).astype(BF16)

    return pl.pallas_call(
        body, name="mem_kv",
        out_shape=[jax.ShapeDtypeStruct((M, 2 * WIDTH), BF16), jax.ShapeDtypeStruct((M, D_MODEL), BF16)],
        compiler_params=pltpu.CompilerParams(vmem_limit_bytes=VMEM_LIMIT),
    )(mem, g, w)


def _band_masks(win):
    t = lax.broadcasted_iota(jnp.int32, (CHUNK, CHUNK), 0)
    s = lax.broadcasted_iota(jnp.int32, (CHUNK, CHUNK), 1)
    cur = jnp.logical_and(t - s >= 0, t - s < win)
    prev = s > t + CHUNK - win
    return cur.astype(BF16), prev.astype(BF16)


def _inv_count(first_row, win):
    t = first_row + lax.broadcasted_iota(jnp.int32, (CHUNK, 1), 0)
    return 1.0 / jnp.minimum(t + 1, win).astype(F32)


def _layer_norm_fwd(v):
    mu = jnp.mean(v, axis=-1, keepdims=True)
    vc = v - mu
    var = jnp.mean(vc * vc, axis=-1, keepdims=True)
    rstd = lax.rsqrt(var + EPS)
    return vc * rstd, rstd


def _mem_softmax(q, kmem):
    s = _dot_nt(q, kmem) * ATT_SCALE
    m = jnp.max(s, axis=-1, keepdims=True)
    e = jnp.exp(s - m)
    return e * (1.0 / jnp.sum(e, axis=-1, keepdims=True))


def _abm_fwd(proj, ln_g, ln_b, wsm, bias_full, pool_w, pool_scale, kv, gathered=()):
    S = proj.shape[0]
    tm = 512
    nchunk = tm // CHUNK
    n = len(gathered)
    nsteps = S // tm

    def body(u_ref, v_ref, ag_ref, p_ref, ph_ref, pg_ref, mq_ref, mg_ref, lng_ref, lnb_ref, wsm_ref, bias_ref,
             pw_ref, ps_ref, kv_ref, *rest):
        y_ref, cdf_ref, bufs = rest[n], rest[n + 1], rest[n + 2:2 * n + 2]
        mix, sems = rest[2 * n + 2], rest[2 * n + 3:]
        i = pl.program_id(0)

        if n:
            @pl.when(i == 0)
            def _():
                _comm_start(_ag_second(bufs, *sems))

        au, av = u_ref[...].astype(F32), v_ref[...].astype(F32)
        cdf_u, cdf_v = _normal_cdf(au), _normal_cdf(av)
        cdf_ref[0] = cdf_u.astype(BF16)
        cdf_ref[1] = cdf_v.astype(BF16)
        u, v = au * cdf_u, av * cdf_v
        vhat, _ = _layer_norm_fwd(v)
        vln = (vhat * lng_ref[...] + lnb_ref[...]).astype(BF16)
        for c in range(nchunk):
            for h in range(N_HEAD):
                rs, cs = slice(c * CHUNK, (c + 1) * CHUNK), slice(h * HEAD, (h + 1) * HEAD)
                mix[rs, cs] = _dot(wsm_ref[h], vln[rs, cs]) + bias_ref[:, cs]
        y_ref[0] = (u * mix[...] * _silu(ag_ref[...].astype(F32))).astype(BF16)
        halo_ok = (i > 0).astype(F32)
        for c in range(nchunk):
            rs = slice(c * CHUNK, (c + 1) * CHUNK)
            for g, win in enumerate(POOL_WINDOWS):
                cs = slice(g * HEAD, (g + 1) * HEAD)
                bcur, bprev = _band_masks(win)
                cur = p_ref[rs, cs]
                if c == 0:
                    prev = (ph_ref[:, cs].astype(F32) * halo_ok).astype(BF16)
                else:
                    prev = p_ref[(c - 1) * CHUNK:c * CHUNK, cs]
                sums = _dot(bcur, cur) + _dot(bprev, prev)
                dm = sums * _inv_count(i * tm + c * CHUNK, win) - cur.astype(F32)
                mix[rs, cs] = _dot(dm.astype(BF16), pw_ref[g])
        y_ref[1] = (mix[...] * ps_ref[...] * _silu(pg_ref[...].astype(F32))).astype(BF16)
        for h in range(N_HEAD):
            cs = slice(h * HEAD, (h + 1) * HEAD)
            p = _mem_softmax(mq_ref[:, cs], kv_ref[:, cs])
            mix[:, cs] = _dot(p.astype(BF16), kv_ref[:, WIDTH + h * HEAD:WIDTH + (h + 1) * HEAD])
        y_ref[2] = (mix[...] * _silu(mg_ref[...].astype(F32))).astype(BF16)

        if n:
            @pl.when(i == nsteps - 1)
            def _():
                _comm_wait(_ag_second(bufs, *sems))

    blk = tm // CHUNK
    res = pl.pallas_call(
        body, name="abm_fwd_gather" if n else "abm_fwd",
        grid=(nsteps,),
        in_specs=[_rows(tm, WIDTH, CB_U), _rows(tm, WIDTH, CB_V), _rows(tm, WIDTH, CB_AGATE),
                  _rows(tm, WIDTH, CB_PIN),
                  pl.BlockSpec((CHUNK, WIDTH), lambda i: (jnp.maximum(i * blk - 1, 0), CB_PIN)),
                  _rows(tm, WIDTH, CB_PGATE), _rows(tm, WIDTH, CB_MQ), _rows(tm, WIDTH, CB_MGATE),
                  _full((1, WIDTH)), _full((1, WIDTH)), _full((N_HEAD, CHUNK, CHUNK)), _full((CHUNK, WIDTH)),
                  _full((4, HEAD, HEAD)), _full((1, WIDTH)), _full((MEM_LEN, 2 * WIDTH))] + [ANY] * n,
        out_specs=[pl.BlockSpec((3, tm, WIDTH), lambda i: (0, i, 0)), pl.BlockSpec((2, tm, WIDTH), lambda i: (0, i, 0))]
                  + [ANY] * n,
        out_shape=[jax.ShapeDtypeStruct((4, S, WIDTH), BF16),
                   jax.ShapeDtypeStruct((2, S, WIDTH), BF16)]
                  + [jax.ShapeDtypeStruct(b.shape, b.dtype) for b in gathered],
        input_output_aliases={15 + a: 2 + a for a in range(n)},
        scratch_shapes=[pltpu.VMEM((tm, WIDTH), F32)] + (_dma_sems(3 * n, 3 * n) if n else []),
        compiler_params=_params(("arbitrary",)),
    )(proj, proj, proj, proj, proj, proj, proj, proj, ln_g, ln_b, wsm, bias_full, pool_w, pool_scale, kv, *gathered)
    return res[0], res[1], list(res[2:])


ATT_TILE = 512


def _attn_fwd(q, qcb, k, kcb, v, vcb, bps):
    S = q.shape[0]
    tm = ATT_TILE
    nb = tm // CHUNK

    nblocks = nb * N_HEAD

    def body(q_ref, k_ref, v_ref, kh_ref, vh_ref, o_ref, l_ref, sc_s, sp_s, pc_s, pp_s):
        i = pl.program_id(0)

        def prev_kv(n, cs):
            if n == 0:
                return kh_ref[:, cs], vh_ref[:, cs]
            ps = slice((n - 1) * CHUNK, n * CHUNK)
            return k_ref[ps, cs], v_ref[ps, cs]

        pens = []
        for n in range(nb):
            rs = slice(n * CHUNK, (n + 1) * CHUNK)
            pens.append(jnp.full((N_HEAD * CHUNK, 1), jnp.where((i * nb + n) % bps != 0, 0.0, NEG), F32))
            for h in range(N_HEAD):
                cs = slice(h * HEAD, (h + 1) * HEAD)
                bs = slice((n * N_HEAD + h) * CHUNK, (n * N_HEAD + h + 1) * CHUNK)
                qh = q_ref[rs, cs]
                sc_s[bs, :] = _dot_nt(qh, k_ref[rs, cs])
                sp_s[bs, :] = _dot_nt(qh, prev_kv(n, cs)[0])
        row = lax.broadcasted_iota(jnp.int32, (nblocks * CHUNK, CHUNK), 0) & (CHUNK - 1)
        col = lax.broadcasted_iota(jnp.int32, (nblocks * CHUNK, CHUNK), 1)
        sc = jnp.where(col <= row, sc_s[...] * ATT_SCALE, NEG)
        sp = jnp.where(col >= row, sp_s[...] * ATT_SCALE, NEG) + jnp.concatenate(pens, axis=0)
        m = jnp.maximum(jnp.max(sc, axis=-1, keepdims=True), jnp.max(sp, axis=-1, keepdims=True))
        ec = jnp.exp(sc - m)
        ep = jnp.exp(sp - m)
        den = jnp.sum(ec, axis=-1, keepdims=True) + jnp.sum(ep, axis=-1, keepdims=True)
        inv = 1.0 / den
        pc_s[...] = (ec * inv).astype(BF16)
        pp_s[...] = (ep * inv).astype(BF16)
        lse = m + jnp.log(den)
        for n in range(nb):
            rs = slice(n * CHUNK, (n + 1) * CHUNK)
            for h in range(N_HEAD):
                cs = slice(h * HEAD, (h + 1) * HEAD)
                bs = slice((n * N_HEAD + h) * CHUNK, (n * N_HEAD + h + 1) * CHUNK)
                o = _dot(pc_s[bs, :], v_ref[rs, cs]) + _dot(pp_s[bs, :], prev_kv(n, cs)[1])
                o_ref[rs, cs] = o.astype(BF16)
            l_ref[rs, :] = _put_cols([lse[(n * N_HEAD + h) * CHUNK:(n * N_HEAD + h + 1) * CHUNK]
                                      for h in range(N_HEAD)])

    def halo(cb):
        return pl.BlockSpec((CHUNK, WIDTH), lambda i: (jnp.maximum(i * nb - 1, 0), cb))

    return pl.pallas_call(
        body, name=f"attn_fwd_{bps}",
        grid=(S // tm,),
        in_specs=[_rows(tm, WIDTH, qcb), _rows(tm, WIDTH, kcb), _rows(tm, WIDTH, vcb), halo(kcb), halo(vcb)],
        out_specs=[_rows(tm, WIDTH), _rows(tm, 128)],
        out_shape=[jax.ShapeDtypeStruct((S, WIDTH), BF16), jax.ShapeDtypeStruct((S, 128), F32)],
        scratch_shapes=[pltpu.VMEM((nblocks * CHUNK, CHUNK), F32), pltpu.VMEM((nblocks * CHUNK, CHUNK), F32),
                        pltpu.VMEM((nblocks * CHUNK, CHUNK), BF16), pltpu.VMEM((nblocks * CHUNK, CHUNK), BF16)],
        compiler_params=_params(("parallel",)),
    )(q, k, v, k, v)


def _gate_specs(tm):
    return [pl.BlockSpec((tm, D_MODEL), lambda i, b=b: (i, b)) for b in range(4)]


Y_SLOT = (0, 1, 3, 2)


def _merge_fwd(x, y4, o_g, l_g, proj, gates, wb, wout, shards=()):
    S = x.shape[0]
    tm = 256
    n = len(shards)
    nsteps = S // tm
    forward_at = nsteps - 4

    def body(x_ref, y_ref, o0, o1, o2, l0, l1, l2, cg_ref, *rest):
        gm = rest[:4]
        wb_ref, wo_ref = rest[4:6]
        s_in = rest[6:6 + n]
        xn_ref, yc_ref, oc_ref, lse_ref, z_ref = rest[6 + n:11 + n]
        s_out, ocs, sems = rest[11 + n:11 + 2 * n], rest[11 + 2 * n], rest[12 + 2 * n:]
        i = pl.program_id(0)

        if n:
            @pl.when(i == 0)
            def _():
                _comm_start(_ag_first(s_in, s_out, *sems[:3]))

            @pl.when(i == forward_at)
            def _():
                incoming = _ag_first(s_in, s_out, *sems[:3])[2]
                for a in range(n):
                    for k in range(1, 4):
                        incoming[4 * a + k].wait_recv()
                _comm_start(_ag_second(s_out, *sems[3:]))

        lcols = []
        for h in range(N_HEAD):
            cs = slice(h * HEAD, (h + 1) * HEAD)
            ls = [_col(l[...], h) for l in (l0, l1, l2)]
            m = jnp.maximum(jnp.maximum(ls[0], ls[1]), ls[2])
            tot = jnp.exp(ls[0] - m) + jnp.exp(ls[1] - m) + jnp.exp(ls[2] - m)
            lse = m + jnp.log(tot)
            ocs[:, cs] = sum(jnp.exp(lg - lse) * o[:, cs].astype(F32) for lg, o in zip(ls, (o0, o1, o2)))
            lcols.append(lse)
        lse_ref[...] = _put_cols(lcols)
        oc = ocs[...]
        oc_ref[...] = oc.astype(BF16)
        yc = (oc * _silu(cg_ref[...].astype(F32))).astype(BF16)
        yc_ref[...] = yc
        ys = (y_ref[0], y_ref[1], yc, y_ref[2])
        z = jnp.zeros((tm, D_MODEL), F32)
        for b in range(4):
            z = z + _sigmoid(gm[b][...].astype(F32)) * _dot(ys[b], wb_ref[b])
        zb = z.astype(BF16)
        z_ref[...] = zb
        xn_ref[...] = x_ref[...] + _dot(zb, wo_ref[...])

        if n:
            @pl.when(i == nsteps - 1)
            def _():
                local, out, incoming = _ag_first(s_in, s_out, *sems[:3])
                for a in range(n):
                    incoming[4 * a].wait_recv()
                _comm_wait(_ag_second(s_out, *sems[3:]))
                for cp in out:
                    cp.wait_send()
                for cp in local:
                    cp.wait()

    res = pl.pallas_call(
        body, name="merge_fwd_gather" if n else "merge_fwd",
        grid=(nsteps,),
        in_specs=[_rows(tm, D_MODEL), pl.BlockSpec((3, tm, WIDTH), lambda i: (0, i, 0)),
                  _rows(tm, WIDTH), _rows(tm, WIDTH), _rows(tm, WIDTH),
                  _rows(tm, 128), _rows(tm, 128), _rows(tm, 128),
                  _rows(tm, WIDTH, CB_CGATE)] + _gate_specs(tm)
                 + [_full((4, WIDTH, D_MODEL)), _full((D_MODEL, D_MODEL))] + [ANY] * n,
        out_specs=[_rows(tm, D_MODEL), pl.BlockSpec((None, tm, WIDTH), lambda i: (Y_SLOT[2], i, 0)),
                   _rows(tm, WIDTH), _rows(tm, 128), _rows(tm, D_MODEL)] + [ANY] * n,
        out_shape=[jax.ShapeDtypeStruct((S, D_MODEL), F32), jax.ShapeDtypeStruct(y4.shape, BF16),
                   jax.ShapeDtypeStruct((S, WIDTH), BF16), jax.ShapeDtypeStruct((S, 128), F32),
                   jax.ShapeDtypeStruct((S, D_MODEL), BF16)]
                  + [jax.ShapeDtypeStruct((N_DEV,) + s.shape, s.dtype) for s in shards],
        input_output_aliases={1: 1},
        scratch_shapes=[pltpu.VMEM((tm, WIDTH), F32)] + (_dma_sems(4 * n, 4 * n, n, 3 * n, 3 * n) if n else []),
        compiler_params=_params(("arbitrary",)),
    )(x, y4, *o_g, *l_g, proj, *([gates] * 4), wb, wout, *shards)
    return res[:5], list(res[5:])


def _loss_head(x, g, tgt):
    S = x.shape[0]
    tm = 512

    def body(x_ref, g_ref, t_ref, loss_ref, dx_ref, dg_ref):
        @pl.when(pl.program_id(0) == 0)
        def _():
            loss_ref[...] = jnp.zeros_like(loss_ref)
            dg_ref[...] = jnp.zeros_like(dg_ref)

        xf = x_ref[...]
        r = lax.rsqrt(jnp.mean(xf * xf, axis=-1, keepdims=True) + EPS)
        xhat = xf * r
        gv = g_ref[...]
        err = xhat * gv - t_ref[...]
        e2 = jnp.sum(err * err, axis=-1, keepdims=True)
        loss_ref[...] += (0.5 / D_MODEL) * jnp.sum(e2, axis=0, keepdims=True)
        dy = err * (1.0 / D_MODEL)
        dg_ref[...] += jnp.sum(dy * xhat, axis=0, keepdims=True)
        dxh = dy * gv
        dx_ref[...] = r * (dxh - xhat * jnp.mean(dxh * xhat, axis=-1, keepdims=True))

    return pl.pallas_call(
        body, name="loss_head",
        grid=(S // tm,),
        in_specs=[_rows(tm, D_MODEL), _full((1, D_MODEL)), _rows(tm, D_MODEL)],
        out_specs=[_full((1, 128)), _rows(tm, D_MODEL), _full((1, D_MODEL))],
        out_shape=[jax.ShapeDtypeStruct((1, 128), F32), jax.ShapeDtypeStruct((S, D_MODEL), F32),
                   jax.ShapeDtypeStruct((1, D_MODEL), F32)],
        compiler_params=_params(("arbitrary",)),
    )(x, g, tgt)


def _merge_bwd(dxo, y4, oc, z, proj, gates, wb, wout, grads=()):
    S = dxo.shape[0]
    tm = 256
    n = len(grads)
    nsteps = S // tm

    def body(dx_ref, y_ref, oc_ref, z_ref, cg_ref, *rest):
        gm = rest[:4]
        wb_ref, wo_ref = rest[4:6]
        g_in = rest[6:6 + n]
        dy_ref, doc_ref, delta_ref, dcg_ref, dgm_ref, dwb_ref, dwo_ref = rest[6 + n:13 + n]
        g_out = rest[13 + n:13 + 2 * n]
        acc_b, acc_o = rest[13 + 2 * n:15 + 2 * n]
        sems = rest[15 + 2 * n:]
        i = pl.program_id(0)

        @pl.when(i == 0)
        def _():
            acc_b[...] = jnp.zeros_like(acc_b)
            acc_o[...] = jnp.zeros_like(acc_o)
            if n:
                _comm_start(_rs_first(g_in, g_out, *sems))

        dxb = dx_ref[...].astype(BF16)
        acc_o[...] += _dot_tn(z_ref[...], dxb)
        dz = _dot_nt(dxb, wo_ref[...])
        for b in range(4):
            gate = _sigmoid(gm[b][...].astype(F32))
            yb = y_ref[Y_SLOT[b]]
            t = _dot(yb, wb_ref[b])
            dgm_ref[:, b * D_MODEL:(b + 1) * D_MODEL] = (dz * t * gate * (1.0 - gate)).astype(BF16)
            dt = (dz * gate).astype(BF16)
            acc_b[b] += _dot_tn(yb, dt)
            dyb = _dot_nt(dt, wb_ref[b])
            if b == 2:
                cg = cg_ref[...].astype(F32)
                oc = oc_ref[...].astype(F32)
                scg, dscg = _silu_and_grad(cg)
                doc = dyb * scg
                dcg_ref[...] = (dyb * oc * dscg).astype(BF16)
                doc_ref[...] = doc.astype(BF16)
                prod = doc * oc
                delta_ref[...] = _put_cols([jnp.sum(prod[:, h * HEAD:(h + 1) * HEAD], axis=1, keepdims=True)
                                            for h in range(N_HEAD)])
            else:
                dy_ref[b if b < 2 else 2] = dyb.astype(BF16)

        @pl.when(i == nsteps - 1)
        def _():
            dwb_ref[...] = acc_b[...].astype(BF16)
            dwo_ref[...] = acc_o[...].astype(BF16)
            if n:
                _comm_wait(_rs_first(g_in, g_out, *sems))

    def resident(shape):
        nd = len(shape)
        return pl.BlockSpec(shape, lambda i: (0,) * nd, pipeline_mode=pl.Buffered(1))

    res = pl.pallas_call(
        body, name="merge_bwd_scatter" if n else "merge_bwd",
        grid=(nsteps,),
        in_specs=[_rows(tm, D_MODEL), pl.BlockSpec((4, tm, WIDTH), lambda i: (0, i, 0)),
                  _rows(tm, WIDTH), _rows(tm, D_MODEL), _rows(tm, WIDTH, CB_CGATE)] + _gate_specs(tm)
                 + [resident((4, WIDTH, D_MODEL)), resident((D_MODEL, D_MODEL))] + [ANY] * n,
        out_specs=[pl.BlockSpec((3, tm, WIDTH), lambda i: (0, i, 0)), _rows(tm, WIDTH), _rows(tm, 128),
                   _rows(tm, WIDTH), _rows(tm, 4 * D_MODEL), _full((4, WIDTH, D_MODEL)), _full((D_MODEL, D_MODEL))]
                  + [ANY] * n,
        out_shape=[jax.ShapeDtypeStruct((3, S, WIDTH), BF16), jax.ShapeDtypeStruct((S, WIDTH), BF16),
                   jax.ShapeDtypeStruct((S, 128), F32), jax.ShapeDtypeStruct((S, WIDTH), BF16),
                   jax.ShapeDtypeStruct((S, 4 * D_MODEL), BF16), jax.ShapeDtypeStruct((4, WIDTH, D_MODEL), BF16),
                   jax.ShapeDtypeStruct((D_MODEL, D_MODEL), BF16)]
                  + [jax.ShapeDtypeStruct(g.shape[:1] + g.shape[2:], g.dtype) for g in grads],
        scratch_shapes=[pltpu.VMEM((4, WIDTH, D_MODEL), F32), pltpu.VMEM((D_MODEL, D_MODEL), F32)]
                       + (_dma_sems(N_CHIP * n, N_CHIP * n) if n else []),
        compiler_params=_params(("arbitrary",)),
    )(dxo, y4, oc, z, proj, *([gates] * 4), wb, wout, *grads)
    return res[:7], list(res[7:])


def _attn_bwd(q, qcb, k, kcb, v, vcb, do, lse, delta, bps, dst, dcb):
    S = q.shape[0]
    tm = ATT_TILE
    nb = tm // CHUNK
    nblk = S // CHUNK

    ncur = nb * N_HEAD
    nprev = (nb + 1) * N_HEAD

    def body(q_ref, k_ref, v_ref, do_ref, l_ref, d_ref, kh_ref, vh_ref, qn_ref, don_ref, ln_ref, dn_ref, dst_ref,
             dq_ref, dk_ref, dv_ref, sc_s, sp_s, dpc_s, dpp_s, pc_s, pp_s, dsc_s, dsp_s):
        i = pl.program_id(0)

        def rows_of(n):
            if n < nb:
                rs = slice(n * CHUNK, (n + 1) * CHUNK)
                return rs, q_ref, do_ref, l_ref, d_ref
            return slice(0, CHUNK), qn_ref, don_ref, ln_ref, dn_ref

        def prev_kv(n, cs):
            if n == 0:
                return kh_ref[:, cs], vh_ref[:, cs]
            ps = slice((n - 1) * CHUNK, n * CHUNK)
            return k_ref[ps, cs], v_ref[ps, cs]

        def blk(n, h):
            return slice((n * N_HEAD + h) * CHUNK, (n * N_HEAD + h + 1) * CHUNK)

        pens, lses, deltas = [], [], []
        for n in range(nb + 1):
            rs, qr, dor, lr, dr = rows_of(n)
            gb = i * nb + n
            pen = jnp.where(gb % bps != 0, 0.0, NEG)
            if n == nb:
                pen = pen + jnp.where(gb < nblk, 0.0, NEG)
            pens.append(jnp.full((N_HEAD * CHUNK, 1), pen, F32))
            lblk, dblk = lr[rs, :], dr[rs, :]
            for h in range(N_HEAD):
                cs = slice(h * HEAD, (h + 1) * HEAD)
                qh, doh = qr[rs, cs], dor[rs, cs]
                lses.append(_col(lblk, h))
                deltas.append(_col(dblk, h))
                kp, vp = prev_kv(n, cs)
                sp_s[blk(n, h), :] = _dot_nt(qh, kp)
                dpp_s[blk(n, h), :] = _dot_nt(doh, vp)
                if n < nb:
                    sc_s[blk(n, h), :] = _dot_nt(qh, k_ref[rs, cs])
                    dpc_s[blk(n, h), :] = _dot_nt(doh, v_ref[rs, cs])
        lse = jnp.concatenate(lses, axis=0)
        delta = jnp.concatenate(deltas, axis=0)
        row = lax.broadcasted_iota(jnp.int32, (nprev * CHUNK, CHUNK), 0) & (CHUNK - 1)
        col = lax.broadcasted_iota(jnp.int32, (nprev * CHUNK, CHUNK), 1)
        sp = jnp.where(col >= row, sp_s[...] * ATT_SCALE, NEG) + jnp.concatenate(pens, axis=0)
        pp = jnp.exp(sp - lse)
        pp_s[...] = pp.astype(BF16)
        dsp_s[...] = (pp * (dpp_s[...] - delta)).astype(BF16)
        nc = ncur * CHUNK
        sc = jnp.where(col[:nc] <= row[:nc], sc_s[...] * ATT_SCALE, NEG)
        pc = jnp.exp(sc - lse[:nc])
        pc_s[...] = pc.astype(BF16)
        dsc_s[...] = (pc * (dpc_s[...] - delta[:nc])).astype(BF16)
        for n in range(nb):
            rs, qr, dor, _, _ = rows_of(n)
            rn, qnr, donr, _, _ = rows_of(n + 1)
            for h in range(N_HEAD):
                cs = slice(h * HEAD, (h + 1) * HEAD)
                kp, _ = prev_kv(n, cs)
                dq = _dot(dsc_s[blk(n, h), :], k_ref[rs, cs]) + _dot(dsp_s[blk(n, h), :], kp)
                dq_ref[rs, cs] = (dq * ATT_SCALE).astype(BF16)
                dk = _dot_tn(dsc_s[blk(n, h), :], qr[rs, cs]) + _dot_tn(dsp_s[blk(n + 1, h), :], qnr[rn, cs])
                dk_ref[rs, cs] = (dk * ATT_SCALE).astype(BF16)
                dv = _dot_tn(pc_s[blk(n, h), :], dor[rs, cs]) + _dot_tn(pp_s[blk(n + 1, h), :], donr[rn, cs])
                dv_ref[rs, cs] = dv.astype(BF16)

    def prev_halo(cb):
        return pl.BlockSpec((CHUNK, WIDTH), lambda i: (jnp.maximum(i * nb - 1, 0), cb))

    def next_halo(width, cb=0):
        return pl.BlockSpec((CHUNK, width), lambda i: (jnp.minimum(i * nb + nb, nblk - 1), cb))

    return pl.pallas_call(
        body, name=f"attn_bwd_{bps}",
        grid=(S // tm,),
        in_specs=[_rows(tm, WIDTH, qcb), _rows(tm, WIDTH, kcb), _rows(tm, WIDTH, vcb), _rows(tm, WIDTH),
                  _rows(tm, 128), _rows(tm, 128), prev_halo(kcb), prev_halo(vcb),
                  next_halo(WIDTH, qcb), next_halo(WIDTH), next_halo(128), next_halo(128), ANY],
        out_specs=[_rows(tm, WIDTH, dcb), _rows(tm, WIDTH), _rows(tm, WIDTH)],
        out_shape=[jax.ShapeDtypeStruct(dst.shape, BF16)] + [jax.ShapeDtypeStruct((S, WIDTH), BF16)] * 2,
        input_output_aliases={12: 0},
        scratch_shapes=[pltpu.VMEM((ncur * CHUNK, CHUNK), F32), pltpu.VMEM((nprev * CHUNK, CHUNK), F32),
                        pltpu.VMEM((ncur * CHUNK, CHUNK), F32), pltpu.VMEM((nprev * CHUNK, CHUNK), F32),
                        pltpu.VMEM((ncur * CHUNK, CHUNK), BF16), pltpu.VMEM((nprev * CHUNK, CHUNK), BF16),
                        pltpu.VMEM((ncur * CHUNK, CHUNK), BF16), pltpu.VMEM((nprev * CHUNK, CHUNK), BF16)],
        compiler_params=_params(("parallel",)),
    )(q, k, v, do, lse, delta, k, v, q, do, lse, delta, dst)


def _dilated_split(d):
    hp = min(N_HEAD, 16 // d)
    return hp, N_HEAD // hp, HEAD * hp


def _by_class(src_ref, dst, d, hp):
    for j in range(hp):
        dst[j] = pltpu.einshape("(tr)l->(rt)l", src_ref[:, j * HEAD:(j + 1) * HEAD], r=d)


def _from_class(src, dst_ref, d, hp, add_ref=None):
    for j in range(hp):
        cs = slice(j * HEAD, (j + 1) * HEAD)
        val = pltpu.einshape("(rt)l->(tr)l", src[j].astype(BF16), r=d)
        if add_ref is not None:
            val = (val.astype(F32) + add_ref[:, cs].astype(F32)).astype(BF16)
        dst_ref[:, cs] = val


def _attn_fwd_dilated(proj, qcb, kcb, vcb, d):
    S = proj.shape[0]
    T = CHUNK * d
    hp, nh, cw = _dilated_split(d)
    nblocks = d * hp

    def body(q_ref, k_ref, v_ref, o_ref, l_ref, qf, kst, vst, of, lf, sc_s, sp_s, pc_s, pp_s):
        i, hh = pl.program_id(0), pl.program_id(1)
        kf, vf = kst.at[i % 2, hh], vst.at[i % 2, hh]
        kpf, vpf = kst.at[1 - i % 2, hh], vst.at[1 - i % 2, hh]

        @pl.when(i == 0)
        def _():
            kpf[...] = jnp.zeros_like(kpf)
            vpf[...] = jnp.zeros_like(vpf)

        _by_class(q_ref, qf, d, hp)
        _by_class(k_ref, kf, d, hp)
        _by_class(v_ref, vf, d, hp)

        def blk(ref, r, j):
            return ref[j, r * CHUNK:(r + 1) * CHUNK, :]

        def bs(r, j):
            return slice((r * hp + j) * CHUNK, (r * hp + j + 1) * CHUNK)

        for r in range(d):
            for j in range(hp):
                qb = blk(qf, r, j)
                sc_s[bs(r, j), :] = _dot_nt(qb, blk(kf, r, j))
                sp_s[bs(r, j), :] = _dot_nt(qb, blk(kpf, r, j))
        row = lax.broadcasted_iota(jnp.int32, (nblocks * CHUNK, CHUNK), 0) & (CHUNK - 1)
        col = lax.broadcasted_iota(jnp.int32, (nblocks * CHUNK, CHUNK), 1)
        sc = jnp.where(col <= row, sc_s[...] * ATT_SCALE, NEG)
        sp = jnp.where(col >= row, sp_s[...] * ATT_SCALE, NEG) + jnp.where(i > 0, 0.0, NEG)
        m = jnp.maximum(jnp.max(sc, axis=-1, keepdims=True), jnp.max(sp, axis=-1, keepdims=True))
        ec = jnp.exp(sc - m)
        ep = jnp.exp(sp - m)
        den = jnp.sum(ec, axis=-1, keepdims=True) + jnp.sum(ep, axis=-1, keepdims=True)
        inv = 1.0 / den
        pc_s[...] = (ec * inv).astype(BF16)
        pp_s[...] = (ep * inv).astype(BF16)
        lse = m + jnp.log(den)
        lane = lax.broadcasted_iota(jnp.int32, (CHUNK, 128), 1)
        for r in range(d):
            lblk = jnp.zeros((CHUNK, 128), F32)
            for j in range(hp):
                o = _dot(pc_s[bs(r, j), :], blk(vf, r, j)) + _dot(pp_s[bs(r, j), :], blk(vpf, r, j))
                of[j, r * CHUNK:(r + 1) * CHUNK, :] = o
                lblk = jnp.where(lane == hh * hp + j, lse[bs(r, j)], lblk)
            lf[r * CHUNK:(r + 1) * CHUNK, :] = lblk
        _from_class(of, o_ref, d, hp)
        lnat = pltpu.einshape("(rt)l->(tr)l", lf[...], r=d)

        @pl.when(hh == 0)
        def _():
            l_ref[...] = lnat

        @pl.when(hh > 0)
        def _():
            l_ref[...] += lnat

    def cols(cb):
        return pl.BlockSpec((T, cw), lambda i, hh: (i, cb * nh + hh))

    tile = pltpu.VMEM((hp, T, HEAD), BF16)
    return pl.pallas_call(
        body, name=f"attn_fwd_dilated_{d}",
        grid=(S // T, nh),
        in_specs=[cols(qcb), cols(kcb), cols(vcb)],
        out_specs=[cols(0), pl.BlockSpec((T, 128), lambda i, hh: (i, 0))],
        out_shape=[jax.ShapeDtypeStruct((S, WIDTH), BF16), jax.ShapeDtypeStruct((S, 128), F32)],
        scratch_shapes=[tile, pltpu.VMEM((2, nh, hp, T, HEAD), BF16), pltpu.VMEM((2, nh, hp, T, HEAD), BF16),
                        pltpu.VMEM((hp, T, HEAD), F32), pltpu.VMEM((T, 128), F32),
                        pltpu.VMEM((nblocks * CHUNK, CHUNK), F32), pltpu.VMEM((nblocks * CHUNK, CHUNK), F32),
                        pltpu.VMEM((nblocks * CHUNK, CHUNK), BF16), pltpu.VMEM((nblocks * CHUNK, CHUNK), BF16)],
        compiler_params=_params(("arbitrary", "arbitrary")),
    )(proj, proj, proj)


def _attn_bwd_dilated(proj, qcb, kcb, vcb, do, lse, delta, d, dk_in, dv_in, dst, dcb):
    S = proj.shape[0]
    T = CHUNK * d
    nt = S // T
    hp, nh, cw = _dilated_split(d)
    nblocks = d * hp

    def body(q_ref, k_ref, v_ref, do_ref, l_ref, d_ref, dki_ref, dvi_ref, dst_ref, dq_ref, dk_ref, dv_ref,
             qf, dof, kbuf, vbuf, dqf, gk, gv,
             sc_s, sp_s, dpc_s, dpp_s, pc_s, pp_s, dsc_s, dsp_s):
        hh, i = pl.program_id(0), pl.program_id(1)
        kf, vf, newk, newv = kbuf.at[i % 2], vbuf.at[i % 2], gk.at[i % 2], gv.at[i % 2]
        kpf, vpf, acck, accv = kbuf.at[1 - i % 2], vbuf.at[1 - i % 2], gk.at[1 - i % 2], gv.at[1 - i % 2]

        @pl.when(i == 0)
        def _():
            for ref in (kbuf, vbuf, gk, gv):
                ref[...] = jnp.zeros_like(ref)
            dk_ref[...] = jnp.zeros_like(dk_ref)
            dv_ref[...] = jnp.zeros_like(dv_ref)

        def blk(ref, r, j):
            return ref[j, r * CHUNK:(r + 1) * CHUNK, :]

        def bs(r, j):
            return slice((r * hp + j) * CHUNK, (r * hp + j + 1) * CHUNK)

        @pl.when(i < nt)
        def _():
            _by_class(q_ref, qf, d, hp)
            _by_class(do_ref, dof, d, hp)
            _by_class(k_ref, kf, d, hp)
            _by_class(v_ref, vf, d, hp)
            lses, deltas = [], []
            lcls = pltpu.einshape("(tr)l->(rt)l", l_ref[...], r=d)
            dcls = pltpu.einshape("(tr)l->(rt)l", d_ref[...], r=d)
            for r in range(d):
                lblk = lcls[r * CHUNK:(r + 1) * CHUNK]
                dblk = dcls[r * CHUNK:(r + 1) * CHUNK]
                for j in range(hp):
                    lses.append(_col(lblk, hh * hp + j))
                    deltas.append(_col(dblk, hh * hp + j))
                    qb, dob = blk(qf, r, j), blk(dof, r, j)
                    sc_s[bs(r, j), :] = _dot_nt(qb, blk(kf, r, j))
                    dpc_s[bs(r, j), :] = _dot_nt(dob, blk(vf, r, j))
                    sp_s[bs(r, j), :] = _dot_nt(qb, blk(kpf, r, j))
                    dpp_s[bs(r, j), :] = _dot_nt(dob, blk(vpf, r, j))
            lse = jnp.concatenate(lses, axis=0)
            delta = jnp.concatenate(deltas, axis=0)
            row = lax.broadcasted_iota(jnp.int32, (nblocks * CHUNK, CHUNK), 0) & (CHUNK - 1)
            col = lax.broadcasted_iota(jnp.int32, (nblocks * CHUNK, CHUNK), 1)
            sp = jnp.where(col >= row, sp_s[...] * ATT_SCALE, NEG) + jnp.where(i > 0, 0.0, NEG)
            pp = jnp.exp(sp - lse)
            pp_s[...] = pp.astype(BF16)
            dsp_s[...] = (pp * (dpp_s[...] - delta)).astype(BF16)
            sc = jnp.where(col <= row, sc_s[...] * ATT_SCALE, NEG)
            pc = jnp.exp(sc - lse)
            pc_s[...] = pc.astype(BF16)
            dsc_s[...] = (pc * (dpc_s[...] - delta)).astype(BF16)
            for r in range(d):
                rows = slice(r * CHUNK, (r + 1) * CHUNK)
                for j in range(hp):
                    qb, dob = blk(qf, r, j), blk(dof, r, j)
                    dsc, dsp = dsc_s[bs(r, j), :], dsp_s[bs(r, j), :]
                    dqf[j, rows, :] = (_dot(dsc, blk(kf, r, j)) + _dot(dsp, blk(kpf, r, j))) * ATT_SCALE
                    newk[j, rows, :] = _dot_tn(dsc, qb) * ATT_SCALE
                    newv[j, rows, :] = _dot_tn(pc_s[bs(r, j), :], dob)
                    acck[j, rows, :] += _dot_tn(dsp, qb) * ATT_SCALE
                    accv[j, rows, :] += _dot_tn(pp_s[bs(r, j), :], dob)
            _from_class(dqf, dq_ref, d, hp)

        @pl.when(i > 0)
        def _():
            _from_class(acck, dk_ref, d, hp, dki_ref)
            _from_class(accv, dv_ref, d, hp, dvi_ref)

    def cur(width, cb, nsplit):
        return pl.BlockSpec((T, width), lambda hh, i: (jnp.minimum(i, nt - 1), cb * nsplit + hh * (nsplit > 1)))

    def lag():
        return pl.BlockSpec((T, cw), lambda hh, i: (jnp.maximum(i - 1, 0), hh))

    tile = pltpu.VMEM((hp, T, HEAD), BF16)
    acc = pltpu.VMEM((hp, T, HEAD), F32)
    f32s = pltpu.VMEM((nblocks * CHUNK, CHUNK), F32)
    b16s = pltpu.VMEM((nblocks * CHUNK, CHUNK), BF16)
    return pl.pallas_call(
        body, name=f"attn_bwd_dilated_{d}",
        grid=(nh, nt + 1),
        in_specs=[cur(cw, qcb, nh), cur(cw, kcb, nh), cur(cw, vcb, nh), cur(cw, 0, nh), cur(128, 0, 1), cur(128, 0, 1),
                  lag(), lag(), ANY],
        out_specs=[cur(cw, dcb, nh), lag(), lag()],
        out_shape=[jax.ShapeDtypeStruct(dst.shape, BF16)] + [jax.ShapeDtypeStruct((S, WIDTH), BF16)] * 2,
        input_output_aliases={8: 0},
        scratch_shapes=[tile, tile, pltpu.VMEM((2, hp, T, HEAD), BF16), pltpu.VMEM((2, hp, T, HEAD), BF16), acc,
                        pltpu.VMEM((2, hp, T, HEAD), F32), pltpu.VMEM((2, hp, T, HEAD), F32)]
                       + [f32s] * 4 + [b16s] * 4,
        compiler_params=_params(("arbitrary", "arbitrary")),
    )(proj, proj, proj, do, lse, delta, dk_in, dv_in, dst)


def _abm_bwd(proj, cdf, dy3, ln_g, ln_b, wsm, wsm_t, bias_full, pool_w, pool_wt, pool_scale, kv):
    S = proj.shape[0]
    tm = 512
    nchunk = tm // CHUNK
    nblk = S // CHUNK

    def body(u_ref, v_ref, ag_ref, p_ref, ph_ref, pg_ref, pgn_ref, mq_ref, mg_ref, cdf_ref, dy_ref, dypn_ref,
             lng_ref, lnb_ref, wsm_ref, wsmt_ref, bias_ref, pw_ref, pwt_ref, ps_ref, kv_ref,
             dab_ref, dm_ref, dlng_ref, dlnb_ref, dws_ref, dbias_ref, dpw_ref, dps_ref, dkv_ref,
             mix, dvl, ddn):
        i = pl.program_id(0)

        @pl.when(i == 0)
        def _():
            for r in (dlng_ref, dlnb_ref, dws_ref, dbias_ref, dpw_ref, dps_ref, dkv_ref):
                r[...] = jnp.zeros_like(r)

        au = u_ref[...].astype(F32)
        av = v_ref[...].astype(F32)
        ag = ag_ref[...].astype(F32)
        u, du = _gelu_and_grad(au, cdf_ref[0].astype(F32))
        v, dgelu_v = _gelu_and_grad(av, cdf_ref[1].astype(F32))
        vhat, rstd = _layer_norm_fwd(v)
        vln = (vhat * lng_ref[...] + lnb_ref[...]).astype(BF16)
        for c in range(nchunk):
            for h in range(N_HEAD):
                rs, cs = slice(c * CHUNK, (c + 1) * CHUNK), slice(h * HEAD, (h + 1) * HEAD)
                mix[rs, cs] = _dot(wsm_ref[h], vln[rs, cs]) + bias_ref[:, cs]
        dya = dy_ref[0].astype(F32)
        sg, dsg = _silu_and_grad(ag)
        mixed = mix[...]
        dab_ref[:, 2 * WIDTH:3 * WIDTH] = (dya * u * mixed * dsg).astype(BF16)
        dab_ref[:, 0:WIDTH] = (dya * mixed * sg * du).astype(BF16)
        dmixed = dya * u * sg
        dmb = dmixed.astype(BF16)
        tril = (lax.broadcasted_iota(jnp.int32, (CHUNK, CHUNK), 1)
                <= lax.broadcasted_iota(jnp.int32, (CHUNK, CHUNK), 0))
        for c in range(nchunk):
            rs = slice(c * CHUNK, (c + 1) * CHUNK)
            dbias_ref[...] += dmixed[rs, :]
            for h in range(N_HEAD):
                cs = slice(h * HEAD, (h + 1) * HEAD)
                dvl[rs, cs] = _dot(wsmt_ref[h], dmb[rs, cs])
                dws_ref[h] += jnp.where(tril, _dot_nt(dmb[rs, cs], vln[rs, cs]), 0.0)
        dvln = dvl[...]
        dlng_ref[...] += jnp.sum(dvln * vhat, axis=0, keepdims=True)
        dlnb_ref[...] += jnp.sum(dvln, axis=0, keepdims=True)
        dvh = dvln * lng_ref[...]
        dv = rstd * (dvh - jnp.mean(dvh, axis=-1, keepdims=True)
                     - vhat * jnp.mean(dvh * vhat, axis=-1, keepdims=True))
        dab_ref[:, WIDTH:2 * WIDTH] = (dv * dgelu_v).astype(BF16)

        halo_ok = (i > 0).astype(F32)
        for c in range(nchunk):
            rs = slice(c * CHUNK, (c + 1) * CHUNK)
            for g, win in enumerate(POOL_WINDOWS):
                cs = slice(g * HEAD, (g + 1) * HEAD)
                bcur, bprev = _band_masks(win)
                cur = p_ref[rs, cs]
                if c == 0:
                    prev = (ph_ref[:, cs].astype(F32) * halo_ok).astype(BF16)
                else:
                    prev = p_ref[(c - 1) * CHUNK:c * CHUNK, cs]
                sums = _dot(bcur, cur) + _dot(bprev, prev)
                dvl[rs, cs] = sums * _inv_count(i * tm + c * CHUNK, win) - cur.astype(F32)
        dmat = dvl[...].astype(BF16)
        for g in range(4):
            cs = slice(g * HEAD, (g + 1) * HEAD)
            mix[:, cs] = _dot(dmat[:, cs], pw_ref[g])
        yg = mix[...]
        pg = pg_ref[...].astype(F32)
        dyp = dy_ref[1].astype(F32)
        spg, dspg = _silu_and_grad(pg)
        dyy = dyp * spg
        scale = ps_ref[...]
        dab_ref[:, 4 * WIDTH:5 * WIDTH] = (dyp * yg * scale * dspg).astype(BF16)
        dps_ref[...] += jnp.sum(dyy * yg, axis=0, keepdims=True)
        dyg = (dyy * scale).astype(BF16)
        for g in range(4):
            cs = slice(g * HEAD, (g + 1) * HEAD)
            dpw_ref[g] += _dot_tn(dmat[:, cs], dyg[:, cs])
            mix[:, cs] = _dot(dyg[:, cs], pwt_ref[g])
        next_ok = (i + 1 < S // tm).astype(F32)
        dygn = (dypn_ref[...].astype(F32) * _silu(pgn_ref[...].astype(F32)) * scale * next_ok).astype(BF16)
        for c in range(nchunk + 1):
            for g, win in enumerate(POOL_WINDOWS):
                cs = slice(g * HEAD, (g + 1) * HEAD)
                if c < nchunk:
                    dd = mix[c * CHUNK:(c + 1) * CHUNK, cs]
                else:
                    dd = _dot(dygn[:, cs], pwt_ref[g])
                ddn[c * CHUNK:(c + 1) * CHUNK, cs] = dd * _inv_count(i * tm + c * CHUNK, win)
        ddnb = ddn[...].astype(BF16)
        for c in range(nchunk):
            rs = slice(c * CHUNK, (c + 1) * CHUNK)
            ns = slice((c + 1) * CHUNK, (c + 2) * CHUNK)
            for g, win in enumerate(POOL_WINDOWS):
                cs = slice(g * HEAD, (g + 1) * HEAD)
                bcur, bprev = _band_masks(win)
                dp = _dot_tn(bcur, ddnb[rs, cs]) + _dot_tn(bprev, ddnb[ns, cs]) - mix[rs, cs]
                dab_ref[rs, 3 * WIDTH + g * HEAD:3 * WIDTH + (g + 1) * HEAD] = dp.astype(BF16)

        mg = mg_ref[...].astype(F32)
        dym = dy_ref[2].astype(F32)
        smg, dsmg = _silu_and_grad(mg)
        dob = (dym * smg).astype(BF16)
        for h in range(N_HEAD):
            cs = slice(h * HEAD, (h + 1) * HEAD)
            vs = slice(WIDTH + h * HEAD, WIDTH + (h + 1) * HEAD)
            qh = mq_ref[:, cs]
            p = _mem_softmax(qh, kv_ref[:, cs])
            pb = p.astype(BF16)
            mix[:, cs] = _dot(pb, kv_ref[:, vs])
            dp = _dot_nt(dob[:, cs], kv_ref[:, vs])
            ds = (p * (dp - jnp.sum(p * dp, axis=-1, keepdims=True))).astype(BF16)
            dm_ref[:, cs] = (_dot(ds, kv_ref[:, cs]) * ATT_SCALE).astype(BF16)
            dkv_ref[:, cs] += _dot_tn(ds, qh) * ATT_SCALE
            dkv_ref[:, vs] += _dot_tn(pb, dob[:, cs])
        dm_ref[:, WIDTH:2 * WIDTH] = (dym * mix[...] * dsmg).astype(BF16)

    blk = tm // CHUNK
    small = [_full((1, WIDTH)), _full((1, WIDTH)), _full((N_HEAD, CHUNK, CHUNK)), _full((CHUNK, WIDTH)),
             _full((4, HEAD, HEAD)), _full((1, WIDTH)), _full((MEM_LEN, 2 * WIDTH))]
    return pl.pallas_call(
        body, name="abm_bwd",
        grid=(S // tm,),
        in_specs=[_rows(tm, WIDTH, CB_U), _rows(tm, WIDTH, CB_V), _rows(tm, WIDTH, CB_AGATE),
                  _rows(tm, WIDTH, CB_PIN),
                  pl.BlockSpec((CHUNK, WIDTH), lambda i: (jnp.maximum(i * blk - 1, 0), CB_PIN)),
                  _rows(tm, WIDTH, CB_PGATE),
                  pl.BlockSpec((CHUNK, WIDTH), lambda i: (jnp.minimum(i * blk + blk, nblk - 1), CB_PGATE)),
                  _rows(tm, WIDTH, CB_MQ), _rows(tm, WIDTH, CB_MGATE),
                  pl.BlockSpec((2, tm, WIDTH), lambda i: (0, i, 0)),
                  pl.BlockSpec((3, tm, WIDTH), lambda i: (0, i, 0)),
                  pl.BlockSpec((None, CHUNK, WIDTH), lambda i: (1, jnp.minimum(i * blk + blk, nblk - 1), 0)),
                  _full((1, WIDTH)), _full((1, WIDTH)), _full((N_HEAD, CHUNK, CHUNK)), _full((N_HEAD, CHUNK, CHUNK)),
                  _full((CHUNK, WIDTH)), _full((4, HEAD, HEAD)), _full((4, HEAD, HEAD)), _full((1, WIDTH)),
                  _full((MEM_LEN, 2 * WIDTH))],
        out_specs=[_rows(tm, 5 * WIDTH), _rows(tm, 2 * WIDTH)] + small,
        out_shape=[jax.ShapeDtypeStruct((S, D_BRANCHES), BF16), jax.ShapeDtypeStruct((S, 2 * WIDTH), BF16),
                   jax.ShapeDtypeStruct((1, WIDTH), F32), jax.ShapeDtypeStruct((1, WIDTH), F32),
                   jax.ShapeDtypeStruct((N_HEAD, CHUNK, CHUNK), F32), jax.ShapeDtypeStruct((CHUNK, WIDTH), F32),
                   jax.ShapeDtypeStruct((4, HEAD, HEAD), F32), jax.ShapeDtypeStruct((1, WIDTH), F32),
                   jax.ShapeDtypeStruct((MEM_LEN, 2 * WIDTH), F32)],
        scratch_shapes=[pltpu.VMEM((tm, WIDTH), F32), pltpu.VMEM((tm, WIDTH), F32),
                        pltpu.VMEM((tm + CHUNK, WIDTH), F32)],
        compiler_params=_params(("arbitrary",)),
    )(proj, proj, proj, proj, proj, proj, proj, proj, proj, cdf, dy3, dy3,
      ln_g, ln_b, wsm, wsm_t, bias_full, pool_w, pool_wt, pool_scale, kv)


def _bias_reduce(dbias_full):
    def body(d_ref, o_ref):
        d = d_ref[...]
        o_ref[...] = _put_cols([jnp.sum(d[:, h * HEAD:(h + 1) * HEAD], axis=1, keepdims=True) for h in range(N_HEAD)])

    return pl.pallas_call(body, name="bias_reduce", out_shape=jax.ShapeDtypeStruct((CHUNK, 128), F32))(dbias_full)


def _mem_bwd(mem, g, mem_n, w, dkv):
    def body(m_ref, g_ref, mn_ref, w_ref, dkv_ref, dw_ref, dg_ref):
        dkvb = dkv_ref[...].astype(BF16)
        dw_ref[...] = _dot_tn(mn_ref[...], dkvb).astype(BF16)
        dmn = _dot_nt(dkvb, w_ref[...])
        xf = m_ref[...]
        r = lax.rsqrt(jnp.mean(xf * xf, axis=-1, keepdims=True) + EPS)
        dg_ref[...] = jnp.sum(dmn * xf * r, axis=0, keepdims=True)

    return pl.pallas_call(
        body, name="mem_bwd",
        out_shape=[jax.ShapeDtypeStruct((D_MODEL, 2 * WIDTH), BF16), jax.ShapeDtypeStruct((1, D_MODEL), F32)],
        compiler_params=pltpu.CompilerParams(vmem_limit_bytes=VMEM_LIMIT),
    )(mem, g, mem_n, w, dkv)


def _dh_bwd(dpb, dpg, wt, wgt, x, g, dxo, parts=()):
    S = x.shape[0]
    tm, tkb, tkg = 1024, D_BRANCHES // 4, D_GATES // 4
    nkb, nkg = 4, 4
    nk = nkb + nkg
    ni = S // tm
    n = len(parts)

    def body(dpb_ref, wbr_ref, dpg_ref, wg_ref, x_ref, g_ref, dxo_ref, *rest):
        p_in = rest[:n]
        dx_ref, dg_ref = rest[n:n + 2]
        p_out, acc, sems = rest[n + 2:2 * n + 2], rest[2 * n + 2], rest[2 * n + 3:]
        i, kk = pl.program_id(0), pl.program_id(1)

        @pl.when(jnp.logical_and(i == 0, kk == 0))
        def _():
            dg_ref[...] = jnp.zeros_like(dg_ref)
            if n:
                _comm_start(_rs_second(p_in, p_out, *sems))

        @pl.when(kk == 0)
        def _():
            acc[...] = jnp.zeros_like(acc)

        @pl.when(kk < nkb)
        def _():
            acc[...] += _dot(dpb_ref[...], wbr_ref[...])

        @pl.when(kk >= nkb)
        def _():
            acc[...] += _dot(dpg_ref[...], wg_ref[...])

        @pl.when(kk == nk - 1)
        def _():
            xf = x_ref[...]
            r = lax.rsqrt(jnp.mean(xf * xf, axis=-1, keepdims=True) + EPS)
            xhat = xf * r
            dh = acc[...]
            dg_ref[...] += jnp.sum(dh * xhat, axis=0, keepdims=True)
            dxh = dh * g_ref[...]
            dx_ref[...] = dxo_ref[...] + r * (dxh - xhat * jnp.mean(dxh * xhat, axis=-1, keepdims=True))

        if n:
            @pl.when(jnp.logical_and(i == ni - 1, kk == nk - 1))
            def _():
                _comm_wait(_rs_second(p_in, p_out, *sems))

    res = pl.pallas_call(
        body, name="dh_bwd_scatter" if n else "dh_bwd",
        grid=(ni, nk),
        in_specs=[pl.BlockSpec((tm, tkb), lambda i, k: (i, jnp.minimum(k, nkb - 1))),
                  pl.BlockSpec((tkb, D_MODEL), lambda i, k: (jnp.minimum(k, nkb - 1), 0)),
                  pl.BlockSpec((tm, tkg), lambda i, k: (i, jnp.maximum(k - nkb, 0))),
                  pl.BlockSpec((tkg, D_MODEL), lambda i, k: (jnp.maximum(k - nkb, 0), 0)),
                  pl.BlockSpec((tm, D_MODEL), lambda i, k: (i, 0)), pl.BlockSpec((1, D_MODEL), lambda i, k: (0, 0)),
                  pl.BlockSpec((tm, D_MODEL), lambda i, k: (i, 0))] + [ANY] * n,
        out_specs=[pl.BlockSpec((tm, D_MODEL), lambda i, k: (i, 0)), pl.BlockSpec((1, D_MODEL), lambda i, k: (0, 0))]
                  + [ANY] * n,
        out_shape=[jax.ShapeDtypeStruct((S, D_MODEL), F32), jax.ShapeDtypeStruct((1, D_MODEL), F32)]
                  + [jax.ShapeDtypeStruct(p.shape, p.dtype) for p in parts],
        scratch_shapes=[pltpu.VMEM((tm, D_MODEL), F32)] + (_dma_sems(3 * n, 3 * n, n) if n else []),
        compiler_params=_params(("arbitrary", "arbitrary")),
    )(dpb, wt, dpg, wgt, x, g, dxo, *parts)
    return res[0], res[1], list(res[2:])


def _dw_in(h, dpb, dpg, parts=()):
    S = h.shape[0]
    tk = 2048
    nk = S // tk
    n = len(parts)
    tmb = D_BRANCHES // 4
    ng = D_GATES // GATE_TILE

    def accumulate(a_ref, h_ref, o_ref, acc):
        kk = pl.program_id(1)

        @pl.when(kk == 0)
        def _():
            acc[...] = jnp.zeros_like(acc)

        acc[...] += _dot_tn(a_ref[...], h_ref[...])

        @pl.when(kk == nk - 1)
        def _():
            o_ref[...] = acc[...].astype(BF16)

    def branches(a_ref, h_ref, *rest):
        p_in, o_ref, p_out = rest[:n], rest[n], rest[n + 1:2 * n + 1]
        acc, sems = rest[2 * n + 1], rest[2 * n + 2:]
        i, kk = pl.program_id(0), pl.program_id(1)

        if n:
            @pl.when(jnp.logical_and(i == 0, kk == 0))
            def _():
                _comm_start(_rs_second(p_in, p_out, *sems))

        accumulate(a_ref, h_ref, o_ref, acc)

        if n:
            @pl.when(jnp.logical_and(i == 3, kk == nk - 1))
            def _():
                _comm_wait(_rs_second(p_in, p_out, *sems))

    def gates(a_ref, h_ref, dst_ref, o_ref, acc):
        accumulate(a_ref, h_ref, o_ref, acc)

    res = pl.pallas_call(
        branches, name="dw_in_branches_scatter" if n else "dw_in_branches",
        grid=(4, nk),
        in_specs=[pl.BlockSpec((tk, tmb), lambda i, k: (k, i)), pl.BlockSpec((tk, D_MODEL), lambda i, k: (k, 0))]
                 + [ANY] * n,
        out_specs=[pl.BlockSpec((tmb, D_MODEL), lambda i, k: (i, 0))] + [ANY] * n,
        out_shape=[jax.ShapeDtypeStruct((D_IN, D_MODEL), BF16)]
                  + [jax.ShapeDtypeStruct(p.shape, p.dtype) for p in parts],
        scratch_shapes=[pltpu.VMEM((tmb, D_MODEL), F32)] + (_dma_sems(3 * n, 3 * n, n) if n else []),
        compiler_params=_params(("arbitrary", "arbitrary")),
    )(dpb, h, *parts)
    dwt = pl.pallas_call(
        gates, name="dw_in_gates",
        grid=(ng, nk),
        in_specs=[pl.BlockSpec((tk, GATE_TILE), lambda i, k: (k, i)), pl.BlockSpec((tk, D_MODEL), lambda i, k: (k, 0)),
                  ANY],
        out_specs=pl.BlockSpec((GATE_TILE, D_MODEL), lambda i, k: (D_BRANCHES // GATE_TILE + i, 0)),
        out_shape=jax.ShapeDtypeStruct((D_IN, D_MODEL), BF16),
        input_output_aliases={2: 0},
        scratch_shapes=[pltpu.VMEM((GATE_TILE, D_MODEL), F32)],
        compiler_params=_params(("parallel", "arbitrary")),
    )(dpg, h, res[0])
    return dwt, list(res[1:])


def _row_tile(R, C, block_bytes=2 << 20):
    for cand in range(min(R, block_bytes // (C * 4)) // 8 * 8, 0, -8):
        if R % cand == 0:
            return cand
    return R


def _adamw_update(p_ref, w_ref, m_ref, v_ref, g_ref, d_ref, nm_ref, nv_ref):
    c1 = 1.0 / (1.0 - ADAM_B1 ** ADAM_STEP)
    c2 = 1.0 / (1.0 - ADAM_B2 ** ADAM_STEP)
    g = p_ref[0].astype(F32)
    for k in range(1, p_ref.shape[0]):
        g = g + p_ref[k].astype(F32)
    nm = ADAM_B1 * m_ref[...] + (1.0 - ADAM_B1) * g
    nv = ADAM_B2 * v_ref[...] + (1.0 - ADAM_B2) * (g * g)
    g_ref[...] = g
    nm_ref[...] = nm
    nv_ref[...] = nv
    d_ref[...] = -ADAM_LR * ((nm * c1) / (jnp.sqrt(nv * c2) + ADAM_EPS) + ADAM_WD * w_ref[...])


def _adamw(parts, w, m, v, name):
    P, R, C = parts.shape
    tr = _row_tile(R, C)

    def body(*refs):
        _adamw_update(*refs)

    spec = pl.BlockSpec((tr, C), lambda i: (i, 0))
    return pl.pallas_call(
        body, name=name,
        grid=(R // tr,),
        in_specs=[pl.BlockSpec((P, tr, C), lambda i: (0, i, 0)), spec, spec, spec],
        out_specs=[spec] * 4,
        out_shape=[jax.ShapeDtypeStruct((R, C), F32)] * 4,
        compiler_params=_params(("parallel",)),
    )(parts, w, m, v)


def _adamw_layers(parts, w, m, v, name):
    depth = len(parts)
    P, R, C = parts[0].shape
    tr = _row_tile(R, C, 1 << 20)

    def body(*refs):
        layer = pl.program_id(0)
        for k in range(depth):
            @pl.when(layer == k)
            def _(k=k):
                _adamw_update(refs[k], *refs[depth:])

    def part_spec(k):
        return pl.BlockSpec((P, tr, C), lambda l, i: (0, jnp.where(l == k, i, 0), 0))

    spec = pl.BlockSpec((None, tr, C), lambda l, i: (l, i, 0))
    return pl.pallas_call(
        body, name=name,
        grid=(depth, R // tr),
        in_specs=[part_spec(k) for k in range(depth)] + [spec] * 3,
        out_specs=[spec] * 4,
        out_shape=[jax.ShapeDtypeStruct((depth, R, C), F32)] * 4,
        compiler_params=_params(("arbitrary", "arbitrary")),
    )(*parts, w, m, v)


def _place():
    return lax.axis_index("x"), lax.axis_index("y"), lax.axis_index("c")


def _all_gather(shards):
    n = len(shards)

    def body(*refs):
        ins, outs = refs[:n], refs[n:2 * n]
        send1, recv1, local_sems, send2, recv2 = refs[2 * n:]
        first = _ag_first(ins, outs, send1, recv1, local_sems)
        second = _ag_second(outs, send2, recv2)
        _comm_start(first)
        for j in range(3):
            for a in range(n):
                first[2][4 * a + 1 + j].wait_recv()
            for a in range(n):
                second[1][3 * a + j].start()
        for a in range(n):
            first[2][4 * a].wait_recv()
        for cp in second[2]:
            cp.wait_recv()
        for cp in first[1] + second[1]:
            cp.wait_send()
        for cp in first[0]:
            cp.wait()

    return pl.pallas_call(
        body, name="weights_all_gather",
        in_specs=[ANY] * n, out_specs=[ANY] * n,
        out_shape=[jax.ShapeDtypeStruct((N_DEV,) + s.shape, s.dtype) for s in shards],
        scratch_shapes=_dma_sems(4 * n, 4 * n, n, 3 * n, 3 * n),
        compiler_params=pltpu.CompilerParams(has_side_effects=True),
    )(*shards)


N_BIG = 4


def _dev(p):
    return 4 * p[0] + 2 * p[1] + p[2]


def _other_chips(x, y):
    return [(1 - x, y), (x, 1 - y), (1 - x, 1 - y)]


def _remote(src, dst, send_sems, recv_sems, k, to):
    return pltpu.make_async_remote_copy(src_ref=src, dst_ref=dst, send_sem=send_sems.at[k], recv_sem=recv_sems.at[k],
                                        device_id=to, device_id_type=MESH)


def _ag_first(ins, outs, send_sems, recv_sems, local_sems):
    x, y, c = _place()
    me = (x, y, c)
    targets = [(x, y, 1 - c)] + [(*chip, c) for chip in _other_chips(x, y)]
    local, out, inc = [], [], []
    for a in range(len(ins)):
        local.append(pltpu.make_async_copy(ins[a], outs[a].at[_dev(me)], local_sems.at[a]))
        for k, to in enumerate(targets):
            out.append(_remote(ins[a], outs[a].at[_dev(me)], send_sems, recv_sems, 4 * a + k, to))
            inc.append(_remote(ins[a], outs[a].at[_dev(to)], send_sems, recv_sems, 4 * a + k, to))
    return local, out, inc


def _ag_second(bufs, send_sems, recv_sems):
    x, y, c = _place()
    out, inc = [], []
    for a in range(len(bufs)):
        for j, chip in enumerate(_other_chips(x, y)):
            mine, theirs = bufs[a].at[_dev((*chip, c))], bufs[a].at[_dev((*chip, 1 - c))]
            out.append(_remote(mine, mine, send_sems, recv_sems, 3 * a + j, (x, y, 1 - c)))
            inc.append(_remote(theirs, theirs, send_sems, recv_sems, 3 * a + j, (x, y, 1 - c)))
    return [], out, inc


def _rs_first(ins, outs, send_sems, recv_sems):
    x, y, c = _place()
    out = [_remote(ins[a].at[j, 1 - c], outs[a].at[j], send_sems, recv_sems, N_CHIP * a + j, (x, y, 1 - c))
           for a in range(len(ins)) for j in range(N_CHIP)]
    return [], out, out


def _rs_second(ins, outs, send_sems, recv_sems, local_sems):
    x, y, c = _place()
    my_chip = 2 * x + y
    local, out, inc = [], [], []
    for a in range(len(ins)):
        local.append(pltpu.make_async_copy(ins[a].at[my_chip], outs[a].at[my_chip], local_sems.at[a]))
        for k, (ox, oy) in enumerate(_other_chips(x, y)):
            out.append(_remote(ins[a].at[2 * ox + oy], outs[a].at[my_chip], send_sems, recv_sems, 3 * a + k, (ox, oy, c)))
            inc.append(_remote(ins[a].at[2 * ox + oy], outs[a].at[2 * ox + oy], send_sems, recv_sems, 3 * a + k,
                               (ox, oy, c)))
    return local, out, inc


def _comm_start(exchange):
    local, out, _ = exchange
    for cp in local + out:
        cp.start()


def _comm_wait(exchange):
    local, out, inc = exchange
    for cp in inc:
        cp.wait_recv()
    for cp in out:
        cp.wait_send()
    for cp in local:
        cp.wait()


def _dma_sems(*counts):
    return [pltpu.SemaphoreType.DMA((n,)) for n in counts]


def _rs_sibling(grads):
    n = len(grads)

    def body(*refs):
        ex = _rs_first(refs[:n], refs[n:2 * n], *refs[2 * n:])
        _comm_start(ex)
        _comm_wait(ex)

    return pl.pallas_call(
        body, name="grads_to_sibling",
        in_specs=[ANY] * n, out_specs=[ANY] * n,
        out_shape=[jax.ShapeDtypeStruct(g.shape[:1] + g.shape[2:], g.dtype) for g in grads],
        scratch_shapes=_dma_sems(N_CHIP * n, N_CHIP * n),
        compiler_params=pltpu.CompilerParams(has_side_effects=True),
    )(*grads)


def _pair_sum(grads, recvs):
    n = len(grads)

    def body(c_ref, *refs):
        for a in range(n):
            refs[2 * n + a][...] = (refs[a][...].astype(F32) + refs[n + a][...].astype(F32)).astype(BF16)

    def g_spec(g):
        return pl.BlockSpec((None, None) + g.shape[2:], lambda j, c_ref: (j, c_ref[0], 0, 0))

    def r_spec(r):
        return pl.BlockSpec((None,) + r.shape[1:], lambda j, c_ref: (j, 0, 0))

    return pl.pallas_call(
        body, name="pair_sum",
        grid_spec=pltpu.PrefetchScalarGridSpec(
            num_scalar_prefetch=1, grid=(N_CHIP,),
            in_specs=[g_spec(g) for g in grads] + [r_spec(r) for r in recvs],
            out_specs=[r_spec(r) for r in recvs]),
        out_shape=[jax.ShapeDtypeStruct(r.shape, BF16) for r in recvs],
        compiler_params=_params(("parallel",)),
    )(lax.axis_index("c").reshape(1).astype(jnp.int32), *grads, *recvs)


SMALL_ROWS = 544


def _all_reduce_small(buf, parts=()):
    n = len(parts)

    def body(in_ref, *rest):
        p_in, out_ref, p_out = rest[:n], rest[n], rest[n + 1:2 * n + 1]
        recv, acc, send1, recv1, send2, recv2 = rest[2 * n + 1:2 * n + 7]
        scatter_sems = rest[2 * n + 7:]
        x, y, c = _place()
        me = 4 * x + 2 * y + c
        peers = [(x ^ (r >> 2), y ^ ((r >> 1) & 1), c ^ (r & 1)) for r in range(1, N_DEV)]

        def idx(p):
            return 4 * p[0] + 2 * p[1] + p[2]

        if n:
            _comm_start(_rs_second(p_in, p_out, *scatter_sems))
        first = [pltpu.make_async_remote_copy(
            src_ref=in_ref.at[idx(p)], dst_ref=recv.at[me], send_sem=send1.at[r], recv_sem=recv1.at[r],
            device_id=p, device_id_type=MESH) for r, p in enumerate(peers)]
        for cp in first:
            cp.start()
        recv[me] = in_ref[me]
        for r, p in enumerate(peers):
            pltpu.make_async_remote_copy(
                src_ref=in_ref.at[idx(p)], dst_ref=recv.at[idx(p)], send_sem=send1.at[r], recv_sem=recv1.at[r],
                device_id=p, device_id_type=MESH).wait_recv()
        total = recv[0]
        for k in range(1, N_DEV):
            total = total + recv[k]
        acc[...] = total
        out_ref[me] = total
        second = [pltpu.make_async_remote_copy(
            src_ref=acc, dst_ref=out_ref.at[me], send_sem=send2.at[r], recv_sem=recv2.at[r],
            device_id=p, device_id_type=MESH) for r, p in enumerate(peers)]
        for cp in second:
            cp.start()
        for r, p in enumerate(peers):
            pltpu.make_async_remote_copy(
                src_ref=acc, dst_ref=out_ref.at[idx(p)], send_sem=send2.at[r], recv_sem=recv2.at[r],
                device_id=p, device_id_type=MESH).wait_recv()
        for cp in first + second:
            cp.wait_send()
        if n:
            _comm_wait(_rs_second(p_in, p_out, *scatter_sems))

    vm = pl.BlockSpec(memory_space=pltpu.VMEM)
    res = pl.pallas_call(
        body, name="small_grads_all_reduce",
        in_specs=[vm] + [ANY] * n, out_specs=[vm] + [ANY] * n,
        out_shape=[jax.ShapeDtypeStruct(buf.shape, F32)] + [jax.ShapeDtypeStruct(p.shape, p.dtype) for p in parts],
        scratch_shapes=[pltpu.VMEM(buf.shape, F32), pltpu.VMEM(buf.shape[1:], F32)] + _dma_sems(7, 7, 7, 7)
                       + (_dma_sems(3 * n, 3 * n, n) if n else []),
        compiler_params=pltpu.CompilerParams(has_side_effects=True, vmem_limit_bytes=VMEM_LIMIT),
    )(buf, *parts)
    return res[0], list(res[1:])


def _dilate(a, d):
    if d == 1:
        return a
    S, C = a.shape
    return a.reshape(S // d, d, C).transpose(1, 0, 2).reshape(S, C)


def _undilate(a, d):
    if d == 1:
        return a
    S, C = a.shape
    return a.reshape(d, S // d, C).transpose(1, 0, 2).reshape(S, C)


def _cols(a, cb, n=1):
    return a[:, cb * WIDTH:(cb + n) * WIDTH]


def _to_blocks(g, kind):
    if kind == "rows":
        C = g.shape[1]
        return g.reshape(N_CHIP, 2, -1, C)
    return g.reshape(4 * WIDTH, N_CHIP, 2, -1).transpose(1, 2, 0, 3)


SMALL = ("norm_g", "gm_ln_g", "gm_ln_b", "gm_ws", "gm_bs", "pool_w", "pool_scale", "mem_norm_g", "final_norm_g")


def _pack_small(tree):
    flat = jnp.concatenate([tree[k].reshape(-1, 128) for k in SMALL], axis=0)
    return jnp.pad(flat, ((0, N_DEV * SMALL_ROWS - flat.shape[0]), (0, 0)))


def _unpack_small(flat, like):
    out, at = {}, 0
    for k in SMALL:
        rows = like[k].size // 128
        out[k] = flat[at:at + rows].reshape(like[k].shape)
        at += rows
    return out


def _make_layer(wt, wkv, wb, wout, norm_g, mem_norm_g, ln_g, ln_b, gm_ws, gm_bs, pool_w, pool_scale):
    tril = jnp.tril(jnp.ones((CHUNK, CHUNK), bool))
    wsm = jnp.where(tril, gm_ws, 0.0).astype(BF16)
    pw = pool_w.astype(BF16)
    return dict(wt=wt, wgt=wt[D_BRANCHES:], wkv=wkv, wb=wb, wout=wout, g=norm_g[None], mg=mem_norm_g[None],
                ln_g=ln_g[None],
                ln_b=ln_b[None], wsm=wsm, wsm_t=wsm.transpose(0, 2, 1), pw=pw, pw_t=pw.transpose(0, 2, 1),
                ps=pool_scale[None], bias=jnp.repeat(gm_bs.T, HEAD, axis=1))


def _layer_fwd(xl, mem0, L, next_shards=()):
    S = xl.shape[0]
    proj, gates, h, half_gathered = _in_proj(xl, L["g"], L["wt"], L["wgt"], next_shards[:1])
    kv, mem_n = _mem_kv(mem0, L["mg"], L["wkv"])
    y4, cdf, gathered = _abm_fwd(proj, L["ln_g"], L["ln_b"], L["wsm"], L["bias"], L["pw"], L["ps"], kv, half_gathered)
    o_g, l_g = [], []
    for gi, d in enumerate(DILATIONS):
        if d == 1:
            o, lse = _attn_fwd(proj, CB_Q0, proj, CB_K, proj, CB_CV, S // CHUNK)
        else:
            o, lse = _attn_fwd_dilated(proj, CB_Q0 + gi, CB_K, CB_CV, d)
        o_g.append(o)
        l_g.append(lse)
    (xn, y4, oc, lse, z), rest = _merge_fwd(xl, y4, o_g, l_g, proj, gates, L["wb"], L["wout"], next_shards[1:])
    saved = dict(x=xl, proj=proj, gates=gates, h=h, kv=kv, mem_n=mem_n, y4=y4, cdf=cdf, oc=oc, lse=lse, z=z)
    return xn, saved, gathered + rest


def _place_cols(dst, piece, cb):
    return lax.dynamic_update_slice(dst, piece, (0, cb * WIDTH))


def _layer_bwd(dx, mem0, L, sv, later=()):
    S = dx.shape[0]
    proj = sv["proj"]
    (dy3, doc, delta, dcg, dgm, dwb, dwout), from_sibling = _merge_bwd(
        dx, sv["y4"], sv["oc"], sv["z"], proj, sv["gates"], L["wb"], L["wout"], later)
    pair = _pair_sum(later, from_sibling) if later else ()
    dpb, dm, dlng, dlnb, dws, dbias, dpw, dps, dkv = _abm_bwd(
        proj, sv["cdf"], dy3, L["ln_g"], L["ln_b"], L["wsm"], L["wsm_t"], L["bias"], L["pw"], L["pw_t"], L["ps"], sv["kv"])
    dk, dv = None, None
    for gi, d in enumerate(DILATIONS):
        if d == 1:
            dpb, dk, dv = _attn_bwd(proj, CB_Q0, proj, CB_K, proj, CB_CV, doc, sv["lse"], delta, S // CHUNK,
                                    dpb, CB_Q0)
        else:
            dpb, dk, dv = _attn_bwd_dilated(proj, CB_Q0 + gi, CB_K, CB_CV, doc, sv["lse"], delta, d, dk, dv,
                                            dpb, CB_Q0 + gi)
    dpb = _place_cols(dpb, dk, CB_K)
    dpb = _place_cols(dpb, dv, CB_CV)
    dpb = _place_cols(dpb, dcg, CB_CGATE)
    dpb = _place_cols(dpb, dm, CB_MQ)
    dwkv, dmg = _mem_bwd(mem0, L["mg"], sv["mem_n"], L["wkv"], dkv)
    dwin_t, parts_rest = _dw_in(sv["h"], dpb, dgm, pair[1:])
    dxi, dng, parts = _dh_bwd(dpb, dgm, L["wt"], L["wgt"], sv["x"], L["g"], dx, pair[:1])
    parts = parts + parts_rest
    big = dict(w_in=dwin_t, w_mem_kv=dwkv, w_branch=dwb, w_out=dwout)
    small = dict(norm_g=dng[0], gm_ln_g=dlng[0], gm_ln_b=dlnb[0], gm_ws=dws,
                 gm_bs=_bias_reduce(dbias)[:, :N_HEAD].T, pool_w=dpw, pool_scale=dps[0], mem_norm_g=dmg[0])
    return dxi, big, small, parts


BIG = ("w_in", "w_mem_kv", "w_branch", "w_out")


def _blocked(big):
    return [_to_blocks(big["w_in"], "rows"), _to_blocks(big["w_mem_kv"], "rows"),
            _to_blocks(big["w_branch"], "branch"), _to_blocks(big["w_out"], "rows")]


def _full_weights(gathered):
    win_t, wkv, wb, wout = gathered
    return (win_t.reshape(D_IN, D_MODEL), wkv.reshape(D_MODEL, 2 * WIDTH),
            wb.reshape(N_DEV, 4, WIDTH, -1).transpose(1, 2, 0, 3).reshape(4, WIDTH, D_MODEL),
            wout.reshape(D_MODEL, D_MODEL))


def kernel(x, mem, norm_g, w_in, gm_ln_g, gm_ln_b, gm_ws, gm_bs, pool_w, pool_scale, mem_norm_g, w_mem_kv, w_branch, w_out, final_norm_g, loss_target, m_norm_g, m_w_in, m_gm_ln_g, m_gm_ln_b, m_gm_ws, m_gm_bs, m_pool_w, m_pool_scale, m_mem_norm_g, m_w_mem_kv, m_w_branch, m_w_out, m_final_norm_g, v_norm_g, v_w_in, v_gm_ln_g, v_gm_ln_b, v_gm_ws, v_gm_bs, v_pool_w, v_pool_scale, v_mem_norm_g, v_w_mem_kv, v_w_branch, v_w_out, v_final_norm_g):
    x0 = x[0]
    mem0 = mem[0]
    tgt = loss_target[0]
    S = x0.shape[0]

    shards = [[w_in[l].T.astype(BF16), w_mem_kv[l].astype(BF16), w_branch[l].astype(BF16).reshape(4 * WIDTH, -1),
               w_out[l].astype(BF16)] for l in range(DEPTH)]
    gathered = _all_gather(shards[0])
    layers, saved = [], []
    xl = x0
    for l in range(DEPTH):
        layers.append(_make_layer(*_full_weights(gathered), norm_g[l], mem_norm_g[l], gm_ln_g[l], gm_ln_b[l],
                                  gm_ws[l], gm_bs[l], pool_w[l], pool_scale[l]))
        xl, sv, gathered = _layer_fwd(xl, mem0, layers[l], shards[l + 1] if l + 1 < DEPTH else ())
        saved.append(sv)

    loss_part, dx, d_final = _loss_head(xl, final_norm_g[None], tgt)
    loss = lax.psum(loss_part[0, 0], ("x", "y", "c"))

    small = {k: [None] * DEPTH for k in SMALL if k != "final_norm_g"}
    parts = [None] * DEPTH
    later = ()
    for l in reversed(range(DEPTH)):
        dx, gb, gs, done = _layer_bwd(dx, mem0, layers[l], saved[l], later)
        if later:
            parts[l + 1] = done
        later = _blocked(gb)
        for k in gs:
            small[k][l] = gs[k]
    grad_x = dx[None]
    small_tree = {k: jnp.stack(small[k]) for k in small}
    small_tree["final_norm_g"] = d_final[0]
    reduced, parts[0] = _all_reduce_small(_pack_small(small_tree).reshape(N_DEV, SMALL_ROWS, 128),
                                          _pair_sum(later, _rs_sibling(later)))

    weights = dict(norm_g=norm_g, w_in=w_in, gm_ln_g=gm_ln_g, gm_ln_b=gm_ln_b, gm_ws=gm_ws, gm_bs=gm_bs,
                   pool_w=pool_w, pool_scale=pool_scale, mem_norm_g=mem_norm_g, w_mem_kv=w_mem_kv,
                   w_branch=w_branch, w_out=w_out, final_norm_g=final_norm_g)
    m_in = dict(norm_g=m_norm_g, w_in=m_w_in, gm_ln_g=m_gm_ln_g, gm_ln_b=m_gm_ln_b, gm_ws=m_gm_ws, gm_bs=m_gm_bs,
                pool_w=m_pool_w, pool_scale=m_pool_scale, mem_norm_g=m_mem_norm_g, w_mem_kv=m_w_mem_kv,
                w_branch=m_w_branch, w_out=m_w_out, final_norm_g=m_final_norm_g)
    v_in = dict(norm_g=v_norm_g, w_in=v_w_in, gm_ln_g=v_gm_ln_g, gm_ln_b=v_gm_ln_b, gm_ws=v_gm_ws, gm_bs=v_gm_bs,
                pool_w=v_pool_w, pool_scale=v_pool_scale, mem_norm_g=v_mem_norm_g, w_mem_kv=v_w_mem_kv,
                w_branch=v_w_branch, w_out=v_w_out, final_norm_g=v_final_norm_g)
    res = {}
    def view(k, arr):
        return arr.transpose(0, 2, 1) if k == "w_in" else arr

    for a, k in enumerate(BIG):
        shape = view(k, weights[k]).shape
        by_layer = [parts[l][a] for l in range(DEPTH)]
        lrc = (DEPTH,) + by_layer[0].shape[1:]
        outs = _adamw_layers(by_layer, view(k, weights[k]).reshape(lrc), view(k, m_in[k]).reshape(lrc),
                             view(k, v_in[k]).reshape(lrc), "adamw_" + k)
        res[k] = [view(k, o.reshape(shape)) for o in outs]
    outs = _adamw(reduced.reshape(1, N_DEV * SMALL_ROWS, 128), _pack_small(weights), _pack_small(m_in),
                  _pack_small(v_in), "adamw_small")
    unpacked = [_unpack_small(o, weights) for o in outs]
    for k in SMALL:
        res[k] = [u[k] for u in unpacked]

    order = ("norm_g", "w_in", "gm_ln_g", "gm_ln_b", "gm_ws", "gm_bs", "pool_w", "pool_scale", "mem_norm_g",
             "w_mem_kv", "w_branch", "w_out", "final_norm_g")
    return (loss, grad_x, *[res[k][0] for k in order], *[res[k][1] for k in order],
            *[res[k][2] for k in order], *[res[k][3] for k in order])
```

```python
import functools
import math

import jax
import jax.numpy as jnp
from jax import lax
from jax.experimental import pallas as pl
from jax.experimental.pallas import tpu as pltpu

F32 = jnp.float32
BF16 = jnp.bfloat16

D_MODEL = 1024
DEPTH = 4
WIDTH = 512
D_IN = 10752
HEAD = 128
N_HEAD = 4
CHUNK = 128
MEM_LEN = 256
POOL_WINDOWS = (2, 4, 8, 16)
DILATIONS = (1, 4, 16)
EPS = 1e-6
NEG = -1e30
ATT_SCALE = HEAD ** -0.5
N_DEV = 8
N_CHIP = 4

D_BRANCHES = 6656
D_GATES = D_IN - D_BRANCHES
CB_U, CB_V, CB_AGATE, CB_PIN, CB_PGATE = 0, 1, 2, 3, 4
CB_Q0, CB_K, CB_CV, CB_CGATE, CB_MQ, CB_MGATE = 5, 8, 9, 10, 11, 12

ADAM_LR = 0.001
ADAM_B1 = 0.9
ADAM_B2 = 0.999
ADAM_EPS = 1e-08
ADAM_WD = 0.01
ADAM_STEP = 10

VMEM_LIMIT = 56 * 1024 * 1024
MESH = pl.DeviceIdType.MESH
ANY = pl.BlockSpec(memory_space=pl.ANY)

NT = (((1,), (1,)), ((), ()))
TN = (((0,), (0,)), ((), ()))


def _dot(a, b):
    return jnp.dot(a, b, preferred_element_type=F32)


def _dot_nt(a, b):
    return lax.dot_general(a, b, NT, preferred_element_type=F32)


def _dot_tn(a, b):
    return lax.dot_general(a, b, TN, preferred_element_type=F32)


def _sigmoid(x):
    return 0.5 * jnp.tanh(0.5 * x) + 0.5


def _silu(x):
    return x * _sigmoid(x)


def _silu_and_grad(x):
    s = _sigmoid(x)
    return x * s, s * (1.0 + x * (1.0 - s))


def _normal_cdf(x):
    return 0.5 * (1.0 + lax.erf(x * (2.0 ** -0.5)))


def _gelu_and_grad(x, cdf):
    return x * cdf, cdf + x * jnp.exp(-0.5 * x * x) * (1.0 / math.sqrt(2.0 * math.pi))


def _col(blk, h):
    lane = lax.broadcasted_iota(jnp.int32, blk.shape, 1)
    return jnp.sum(jnp.where(lane == h, blk, 0.0), axis=1, keepdims=True)


def _put_cols(cols):
    rows = cols[0].shape[0]
    lane = lax.broadcasted_iota(jnp.int32, (rows, 128), 1)
    out = jnp.zeros((rows, 128), F32)
    for h, cv in enumerate(cols):
        out = jnp.where(lane == h, cv, out)
    return out


def _params(sem, vmem=VMEM_LIMIT):
    return pltpu.CompilerParams(dimension_semantics=sem, vmem_limit_bytes=vmem)


def _full(shape):
    nd = len(shape)
    return pl.BlockSpec(shape, lambda *_: (0,) * nd)


def _rows(tm, width, cb=0):
    return pl.BlockSpec((tm, width), lambda i: (i, cb))


GATE_TILE = 512


def _in_proj(x, g, wt, wgt, shards=()):
    S = x.shape[0]
    tm, tnb, tng = 1024, D_BRANCHES // 4, D_GATES // 4
    njb, njg = 4, 4
    n = len(shards)
    ni, nj = S // tm, njb + njg

    def body(x_ref, g_ref, wbr_ref, wg_ref, *rest):
        ins, (proj_ref, gates_ref, h_ref), outs = rest[:n], rest[n:n + 3], rest[n + 3:2 * n + 3]
        hs, sems = rest[2 * n + 3], rest[2 * n + 4:]
        i, j = pl.program_id(0), pl.program_id(1)

        if n:
            @pl.when(jnp.logical_and(i == 0, j == 0))
            def _():
                _comm_start(_ag_first(ins, outs, *sems))

        @pl.when(j == 0)
        def _():
            xf = x_ref[...]
            r = lax.rsqrt(jnp.mean(xf * xf, axis=-1, keepdims=True) + EPS)
            h = (xf * r * g_ref[...]).astype(BF16)
            hs[...] = h
            h_ref[...] = h

        @pl.when(j < njb)
        def _():
            proj_ref[...] = _dot_nt(hs[...], wbr_ref[...]).astype(BF16)

        @pl.when(j >= njb)
        def _():
            gates_ref[...] = _dot_nt(hs[...], wg_ref[...]).astype(BF16)

        if n:
            @pl.when(jnp.logical_and(i == ni - 1, j == nj - 1))
            def _():
                _comm_wait(_ag_first(ins, outs, *sems))

    def first(j):
        return jnp.minimum(j, njb - 1)

    def second(j):
        return jnp.maximum(j - njb, 0)

    res = pl.pallas_call(
        body, name="in_proj_gather" if n else "in_proj",
        grid=(ni, nj),
        in_specs=[pl.BlockSpec((tm, D_MODEL), lambda i, j: (i, 0)),
                  pl.BlockSpec((1, D_MODEL), lambda i, j: (0, 0)),
                  pl.BlockSpec((tnb, D_MODEL), lambda i, j: (first(j), 0)),
                  pl.BlockSpec((tng, D_MODEL), lambda i, j: (second(j), 0))]
                 + [ANY] * n,
        out_specs=[pl.BlockSpec((tm, tnb), lambda i, j: (i, first(j))),
                   pl.BlockSpec((tm, tng), lambda i, j: (i, second(j))),
                   pl.BlockSpec((tm, D_MODEL), lambda i, j: (i, 0))] + [ANY] * n,
        out_shape=[jax.ShapeDtypeStruct((S, D_BRANCHES), BF16), jax.ShapeDtypeStruct((S, D_GATES), BF16),
                   jax.ShapeDtypeStruct((S, D_MODEL), BF16)]
                  + [jax.ShapeDtypeStruct((N_DEV,) + s.shape, s.dtype) for s in shards],
        scratch_shapes=[pltpu.VMEM((tm, D_MODEL), BF16)] + (_dma_sems(4 * n, 4 * n, n) if n else []),
        compiler_params=_params(("arbitrary", "arbitrary")),
    )(x, g, wt, wgt, *shards)
    return res[0], res[1], res[2], list(res[3:])


def _mem_kv(mem, g, w):
    M = mem.shape[0]

    def body(m_ref, g_ref, w_ref, kv_ref, mn_ref):
        xf = m_ref[...]
        r = lax.rsqrt(jnp.mean(xf * xf, axis=-1, keepdims=True) + EPS)
        mn = (xf * r * g_ref[...]).astype(BF16)
        mn_ref[...] = mn
        kv_ref[...] = _dot(mn, w_ref[...]).astype(BF16)

    return pl.pallas_call(
        body, name="mem_kv",
        out_shape=[jax.ShapeDtypeStruct((M, 2 * WIDTH), BF16), jax.ShapeDtypeStruct((M, D_MODEL), BF16)],
        compiler_params=pltpu.CompilerParams(vmem_limit_bytes=VMEM_LIMIT),
    )(mem, g, w)


def _band_masks(win):
    t = lax.broadcasted_iota(jnp.int32, (CHUNK, CHUNK), 0)
    s = lax.broadcasted_iota(jnp.int32, (CHUNK, CHUNK), 1)
    cur = jnp.logical_and(t - s >= 0, t - s < win)
    prev = s > t + CHUNK - win
    return cur.astype(BF16), prev.astype(BF16)


def _inv_count(first_row, win):
    t = first_row + lax.broadcasted_iota(jnp.int32, (CHUNK, 1), 0)
    return 1.0 / jnp.minimum(t + 1, win).astype(F32)


def _layer_norm_fwd(v):
    mu = jnp.mean(v, axis=-1, keepdims=True)
    vc = v - mu
    var = jnp.mean(vc * vc, axis=-1, keepdims=True)
    rstd = lax.rsqrt(var + EPS)
    return vc * rstd, rstd


def _mem_softmax(q, kmem):
    s = _dot_nt(q, kmem) * ATT_SCALE
    m = jnp.max(s, axis=-1, keepdims=True)
    e = jnp.exp(s - m)
    return e * (1.0 / jnp.sum(e, axis=-1, keepdims=True))


def _abm_fwd(proj, ln_g, ln_b, wsm, bias_full, pool_w, pool_scale, kv, gathered=()):
    S = proj.shape[0]
    tm = 512
    nchunk = tm // CHUNK
    n = len(gathered)
    nsteps = S // tm

    def body(u_ref, v_ref, ag_ref, p_ref, ph_ref, pg_ref, mq_ref, mg_ref, lng_ref, lnb_ref, wsm_ref, bias_ref,
             pw_ref, ps_ref, kv_ref, *rest):
        y_ref, cdf_ref, bufs = rest[n], rest[n + 1], rest[n + 2:2 * n + 2]
        mix, sems = rest[2 * n + 2], rest[2 * n + 3:]
        i = pl.program_id(0)

        if n:
            @pl.when(i == 0)
            def _():
                _comm_start(_ag_second(bufs, *sems))

        au, av = u_ref[...].astype(F32), v_ref[...].astype(F32)
        cdf_u, cdf_v = _normal_cdf(au), _normal_cdf(av)
        cdf_ref[0] = cdf_u.astype(BF16)
        cdf_ref[1] = cdf_v.astype(BF16)
        u, v = au * cdf_u, av * cdf_v
        vhat, _ = _layer_norm_fwd(v)
        vln = (vhat * lng_ref[...] + lnb_ref[...]).astype(BF16)
        for c in range(nchunk):
            for h in range(N_HEAD):
                rs, cs = slice(c * CHUNK, (c + 1) * CHUNK), slice(h * HEAD, (h + 1) * HEAD)
                mix[rs, cs] = _dot(wsm_ref[h], vln[rs, cs]) + bias_ref[:, cs]
        y_ref[0] = (u * mix[...] * _silu(ag_ref[...].astype(F32))).astype(BF16)
        halo_ok = (i > 0).astype(F32)
        for c in range(nchunk):
            rs = slice(c * CHUNK, (c + 1) * CHUNK)
            for g, win in enumerate(POOL_WINDOWS):
                cs = slice(g * HEAD, (g + 1) * HEAD)
                bcur, bprev = _band_masks(win)
                cur = p_ref[rs, cs]
                if c == 0:
                    prev = (ph_ref[:, cs].astype(F32) * halo_ok).astype(BF16)
                else:
                    prev = p_ref[(c - 1) * CHUNK:c * CHUNK, cs]
                sums = _dot(bcur, cur) + _dot(bprev, prev)
                dm = sums * _inv_count(i * tm + c * CHUNK, win) - cur.astype(F32)
                mix[rs, cs] = _dot(dm.astype(BF16), pw_ref[g])
        y_ref[1] = (mix[...] * ps_ref[...] * _silu(pg_ref[...].astype(F32))).astype(BF16)
        for h in range(N_HEAD):
            cs = slice(h * HEAD, (h + 1) * HEAD)
            p = _mem_softmax(mq_ref[:, cs], kv_ref[:, cs])
            mix[:, cs] = _dot(p.astype(BF16), kv_ref[:, WIDTH + h * HEAD:WIDTH + (h + 1) * HEAD])
        y_ref[2] = (mix[...] * _silu(mg_ref[...].astype(F32))).astype(BF16)

        if n:
            @pl.when(i == nsteps - 1)
            def _():
                _comm_wait(_ag_second(bufs, *sems))

    blk = tm // CHUNK
    res = pl.pallas_call(
        body, name="abm_fwd_gather" if n else "abm_fwd",
        grid=(nsteps,),
        in_specs=[_rows(tm, WIDTH, CB_U), _rows(tm, WIDTH, CB_V), _rows(tm, WIDTH, CB_AGATE),
                  _rows(tm, WIDTH, CB_PIN),
                  pl.BlockSpec((CHUNK, WIDTH), lambda i: (jnp.maximum(i * blk - 1, 0), CB_PIN)),
                  _rows(tm, WIDTH, CB_PGATE), _rows(tm, WIDTH, CB_MQ), _rows(tm, WIDTH, CB_MGATE),
                  _full((1, WIDTH)), _full((1, WIDTH)), _full((N_HEAD, CHUNK, CHUNK)), _full((CHUNK, WIDTH)),
                  _full((4, HEAD, HEAD)), _full((1, WIDTH)), _full((MEM_LEN, 2 * WIDTH))] + [ANY] * n,
        out_specs=[pl.BlockSpec((3, tm, WIDTH), lambda i: (0, i, 0)), pl.BlockSpec((2, tm, WIDTH), lambda i: (0, i, 0))]
                  + [ANY] * n,
        out_shape=[jax.ShapeDtypeStruct((4, S, WIDTH), BF16),
                   jax.ShapeDtypeStruct((2, S, WIDTH), BF16)]
                  + [jax.ShapeDtypeStruct(b.shape, b.dtype) for b in gathered],
        input_output_aliases={15 + a: 2 + a for a in range(n)},
        scratch_shapes=[pltpu.VMEM((tm, WIDTH), F32)] + (_dma_sems(3 * n, 3 * n) if n else []),
        compiler_params=_params(("arbitrary",)),
    )(proj, proj, proj, proj, proj, proj, proj, proj, ln_g, ln_b, wsm, bias_full, pool_w, pool_scale, kv, *gathered)
    return res[0], res[1], list(res[2:])


ATT_TILE = 512


def _attn_fwd(q, qcb, k, kcb, v, vcb, bps):
    S = q.shape[0]
    tm = ATT_TILE
    nb = tm // CHUNK

    nblocks = nb * N_HEAD

    def body(q_ref, k_ref, v_ref, kh_ref, vh_ref, o_ref, l_ref, sc_s, sp_s, pc_s, pp_s):
        i = pl.program_id(0)

        def prev_kv(n, cs):
            if n == 0:
                return kh_ref[:, cs], vh_ref[:, cs]
            ps = slice((n - 1) * CHUNK, n * CHUNK)
            return k_ref[ps, cs], v_ref[ps, cs]

        pens = []
        for n in range(nb):
            rs = slice(n * CHUNK, (n + 1) * CHUNK)
            pens.append(jnp.full((N_HEAD * CHUNK, 1), jnp.where((i * nb + n) % bps != 0, 0.0, NEG), F32))
            for h in range(N_HEAD):
                cs = slice(h * HEAD, (h + 1) * HEAD)
                bs = slice((n * N_HEAD + h) * CHUNK, (n * N_HEAD + h + 1) * CHUNK)
                qh = q_ref[rs, cs]
                sc_s[bs, :] = _dot_nt(qh, k_ref[rs, cs])
                sp_s[bs, :] = _dot_nt(qh, prev_kv(n, cs)[0])
        row = lax.broadcasted_iota(jnp.int32, (nblocks * CHUNK, CHUNK), 0) & (CHUNK - 1)
        col = lax.broadcasted_iota(jnp.int32, (nblocks * CHUNK, CHUNK), 1)
        sc = jnp.where(col <= row, sc_s[...] * ATT_SCALE, NEG)
        sp = jnp.where(col >= row, sp_s[...] * ATT_SCALE, NEG) + jnp.concatenate(pens, axis=0)
        m = jnp.maximum(jnp.max(sc, axis=-1, keepdims=True), jnp.max(sp, axis=-1, keepdims=True))
        ec = jnp.exp(sc - m)
        ep = jnp.exp(sp - m)
        den = jnp.sum(ec, axis=-1, keepdims=True) + jnp.sum(ep, axis=-1, keepdims=True)
        inv = 1.0 / den
        pc_s[...] = (ec * inv).astype(BF16)
        pp_s[...] = (ep * inv).astype(BF16)
        lse = m + jnp.log(den)
        for n in range(nb):
            rs = slice(n * CHUNK, (n + 1) * CHUNK)
            for h in range(N_HEAD):
                cs = slice(h * HEAD, (h + 1) * HEAD)
                bs = slice((n * N_HEAD + h) * CHUNK, (n * N_HEAD + h + 1) * CHUNK)
                o = _dot(pc_s[bs, :], v_ref[rs, cs]) + _dot(pp_s[bs, :], prev_kv(n, cs)[1])
                o_ref[rs, cs] = o.astype(BF16)
            l_ref[rs, :] = _put_cols([lse[(n * N_HEAD + h) * CHUNK:(n * N_HEAD + h + 1) * CHUNK]
                                      for h in range(N_HEAD)])

    def halo(cb):
        return pl.BlockSpec((CHUNK, WIDTH), lambda i: (jnp.maximum(i * nb - 1, 0), cb))

    return pl.pallas_call(
        body, name=f"attn_fwd_{bps}",
        grid=(S // tm,),
        in_specs=[_rows(tm, WIDTH, qcb), _rows(tm, WIDTH, kcb), _rows(tm, WIDTH, vcb), halo(kcb), halo(vcb)],
        out_specs=[_rows(tm, WIDTH), _rows(tm, 128)],
        out_shape=[jax.ShapeDtypeStruct((S, WIDTH), BF16), jax.ShapeDtypeStruct((S, 128), F32)],
        scratch_shapes=[pltpu.VMEM((nblocks * CHUNK, CHUNK), F32), pltpu.VMEM((nblocks * CHUNK, CHUNK), F32),
                        pltpu.VMEM((nblocks * CHUNK, CHUNK), BF16), pltpu.VMEM((nblocks * CHUNK, CHUNK), BF16)],
        compiler_params=_params(("parallel",)),
    )(q, k, v, k, v)


def _gate_specs(tm):
    return [pl.BlockSpec((tm, D_MODEL), lambda i, b=b: (i, b)) for b in range(4)]


Y_SLOT = (0, 1, 3, 2)


def _merge_fwd(x, y4, o_g, l_g, proj, gates, wb, wout, shards=()):
    S = x.shape[0]
    tm = 256
    n = len(shards)
    nsteps = S // tm
    forward_at = nsteps - 2

    def body(x_ref, y_ref, o0, o1, o2, l0, l1, l2, cg_ref, *rest):
        gm = rest[:4]
        wb_ref, wo_ref = rest[4:6]
        s_in = rest[6:6 + n]
        xn_ref, yc_ref, oc_ref, lse_ref, z_ref = rest[6 + n:11 + n]
        s_out, ocs, sems = rest[11 + n:11 + 2 * n], rest[11 + 2 * n], rest[12 + 2 * n:]
        i = pl.program_id(0)

        if n:
            @pl.when(i == 0)
            def _():
                _comm_start(_ag_first(s_in, s_out, *sems[:3]))

            @pl.when(i == forward_at)
            def _():
                incoming = _ag_first(s_in, s_out, *sems[:3])[2]
                for a in range(n):
                    for k in range(1, 4):
                        incoming[4 * a + k].wait_recv()
                _comm_start(_ag_second(s_out, *sems[3:]))

        lcols = []
        for h in range(N_HEAD):
            cs = slice(h * HEAD, (h + 1) * HEAD)
            ls = [_col(l[...], h) for l in (l0, l1, l2)]
            m = jnp.maximum(jnp.maximum(ls[0], ls[1]), ls[2])
            tot = jnp.exp(ls[0] - m) + jnp.exp(ls[1] - m) + jnp.exp(ls[2] - m)
            lse = m + jnp.log(tot)
            ocs[:, cs] = sum(jnp.exp(lg - lse) * o[:, cs].astype(F32) for lg, o in zip(ls, (o0, o1, o2)))
            lcols.append(lse)
        lse_ref[...] = _put_cols(lcols)
        oc = ocs[...]
        oc_ref[...] = oc.astype(BF16)
        yc = (oc * _silu(cg_ref[...].astype(F32))).astype(BF16)
        yc_ref[...] = yc
        ys = (y_ref[0], y_ref[1], yc, y_ref[2])
        z = jnp.zeros((tm, D_MODEL), F32)
        for b in range(4):
            z = z + _sigmoid(gm[b][...].astype(F32)) * _dot(ys[b], wb_ref[b])
        zb = z.astype(BF16)
        z_ref[...] = zb
        xn_ref[...] = x_ref[...] + _dot(zb, wo_ref[...])

        if n:
            @pl.when(i == nsteps - 1)
            def _():
                local, out, incoming = _ag_first(s_in, s_out, *sems[:3])
                for a in range(n):
                    incoming[4 * a].wait_recv()
                _comm_wait(_ag_second(s_out, *sems[3:]))
                for cp in out:
                    cp.wait_send()
                for cp in local:
                    cp.wait()

    res = pl.pallas_call(
        body, name="merge_fwd_gather" if n else "merge_fwd",
        grid=(nsteps,),
        in_specs=[_rows(tm, D_MODEL), pl.BlockSpec((3, tm, WIDTH), lambda i: (0, i, 0)),
                  _rows(tm, WIDTH), _rows(tm, WIDTH), _rows(tm, WIDTH),
                  _rows(tm, 128), _rows(tm, 128), _rows(tm, 128),
                  _rows(tm, WIDTH, CB_CGATE)] + _gate_specs(tm)
                 + [_full((4, WIDTH, D_MODEL)), _full((D_MODEL, D_MODEL))] + [ANY] * n,
        out_specs=[_rows(tm, D_MODEL), pl.BlockSpec((None, tm, WIDTH), lambda i: (Y_SLOT[2], i, 0)),
                   _rows(tm, WIDTH), _rows(tm, 128), _rows(tm, D_MODEL)] + [ANY] * n,
        out_shape=[jax.ShapeDtypeStruct((S, D_MODEL), F32), jax.ShapeDtypeStruct(y4.shape, BF16),
                   jax.ShapeDtypeStruct((S, WIDTH), BF16), jax.ShapeDtypeStruct((S, 128), F32),
                   jax.ShapeDtypeStruct((S, D_MODEL), BF16)]
                  + [jax.ShapeDtypeStruct((N_DEV,) + s.shape, s.dtype) for s in shards],
        input_output_aliases={1: 1},
        scratch_shapes=[pltpu.VMEM((tm, WIDTH), F32)] + (_dma_sems(4 * n, 4 * n, n, 3 * n, 3 * n) if n else []),
        compiler_params=_params(("arbitrary",)),
    )(x, y4, *o_g, *l_g, proj, *([gates] * 4), wb, wout, *shards)
    return res[:5], list(res[5:])


def _loss_head(x, g, tgt):
    S = x.shape[0]
    tm = 512

    def body(x_ref, g_ref, t_ref, loss_ref, dx_ref, dg_ref):
        @pl.when(pl.program_id(0) == 0)
        def _():
            loss_ref[...] = jnp.zeros_like(loss_ref)
            dg_ref[...] = jnp.zeros_like(dg_ref)

        xf = x_ref[...]
        r = lax.rsqrt(jnp.mean(xf * xf, axis=-1, keepdims=True) + EPS)
        xhat = xf * r
        gv = g_ref[...]
        err = xhat * gv - t_ref[...]
        e2 = jnp.sum(err * err, axis=-1, keepdims=True)
        loss_ref[...] += (0.5 / D_MODEL) * jnp.sum(e2, axis=0, keepdims=True)
        dy = err * (1.0 / D_MODEL)
        dg_ref[...] += jnp.sum(dy * xhat, axis=0, keepdims=True)
        dxh = dy * gv
        dx_ref[...] = r * (dxh - xhat * jnp.mean(dxh * xhat, axis=-1, keepdims=True))

    return pl.pallas_call(
        body, name="loss_head",
        grid=(S // tm,),
        in_specs=[_rows(tm, D_MODEL), _full((1, D_MODEL)), _rows(tm, D_MODEL)],
        out_specs=[_full((1, 128)), _rows(tm, D_MODEL), _full((1, D_MODEL))],
        out_shape=[jax.ShapeDtypeStruct((1, 128), F32), jax.ShapeDtypeStruct((S, D_MODEL), F32),
                   jax.ShapeDtypeStruct((1, D_MODEL), F32)],
        compiler_params=_params(("arbitrary",)),
    )(x, g, tgt)


def _merge_bwd(dxo, y4, oc, z, proj, gates, wb, wout, grads=()):
    S = dxo.shape[0]
    tm = 256
    n = len(grads)
    nsteps = S // tm

    def body(dx_ref, y_ref, oc_ref, z_ref, cg_ref, *rest):
        gm = rest[:4]
        wb_ref, wo_ref = rest[4:6]
        g_in = rest[6:6 + n]
        dy_ref, doc_ref, delta_ref, dcg_ref, dgm_ref, dwb_ref, dwo_ref = rest[6 + n:13 + n]
        g_out = rest[13 + n:13 + 2 * n]
        acc_b, acc_o = rest[13 + 2 * n:15 + 2 * n]
        sems = rest[15 + 2 * n:]
        i = pl.program_id(0)

        @pl.when(i == 0)
        def _():
            acc_b[...] = jnp.zeros_like(acc_b)
            acc_o[...] = jnp.zeros_like(acc_o)
            if n:
                _comm_start(_rs_first(g_in, g_out, *sems))

        dxb = dx_ref[...].astype(BF16)
        acc_o[...] += _dot_tn(z_ref[...], dxb)
        dz = _dot_nt(dxb, wo_ref[...])
        for b in range(4):
            gate = _sigmoid(gm[b][...].astype(F32))
            yb = y_ref[Y_SLOT[b]]
            t = _dot(yb, wb_ref[b])
            dgm_ref[:, b * D_MODEL:(b + 1) * D_MODEL] = (dz * t * gate * (1.0 - gate)).astype(BF16)
            dt = (dz * gate).astype(BF16)
            acc_b[b] += _dot_tn(yb, dt)
            dyb = _dot_nt(dt, wb_ref[b])
            if b == 2:
                cg = cg_ref[...].astype(F32)
                oc = oc_ref[...].astype(F32)
                scg, dscg = _silu_and_grad(cg)
                doc = dyb * scg
                dcg_ref[...] = (dyb * oc * dscg).astype(BF16)
                doc_ref[...] = doc.astype(BF16)
                prod = doc * oc
                delta_ref[...] = _put_cols([jnp.sum(prod[:, h * HEAD:(h + 1) * HEAD], axis=1, keepdims=True)
                                            for h in range(N_HEAD)])
            else:
                dy_ref[b if b < 2 else 2] = dyb.astype(BF16)

        @pl.when(i == nsteps - 1)
        def _():
            dwb_ref[...] = acc_b[...].astype(BF16)
            dwo_ref[...] = acc_o[...].astype(BF16)
            if n:
                _comm_wait(_rs_first(g_in, g_out, *sems))

    def resident(shape):
        nd = len(shape)
        return pl.BlockSpec(shape, lambda i: (0,) * nd, pipeline_mode=pl.Buffered(1))

    res = pl.pallas_call(
        body, name="merge_bwd_scatter" if n else "merge_bwd",
        grid=(nsteps,),
        in_specs=[_rows(tm, D_MODEL), pl.BlockSpec((4, tm, WIDTH), lambda i: (0, i, 0)),
                  _rows(tm, WIDTH), _rows(tm, D_MODEL), _rows(tm, WIDTH, CB_CGATE)] + _gate_specs(tm)
                 + [resident((4, WIDTH, D_MODEL)), resident((D_MODEL, D_MODEL))] + [ANY] * n,
        out_specs=[pl.BlockSpec((3, tm, WIDTH), lambda i: (0, i, 0)), _rows(tm, WIDTH), _rows(tm, 128),
                   _rows(tm, WIDTH), _rows(tm, 4 * D_MODEL), _full((4, WIDTH, D_MODEL)), _full((D_MODEL, D_MODEL))]
                  + [ANY] * n,
        out_shape=[jax.ShapeDtypeStruct((3, S, WIDTH), BF16), jax.ShapeDtypeStruct((S, WIDTH), BF16),
                   jax.ShapeDtypeStruct((S, 128), F32), jax.ShapeDtypeStruct((S, WIDTH), BF16),
                   jax.ShapeDtypeStruct((S, 4 * D_MODEL), BF16), jax.ShapeDtypeStruct((4, WIDTH, D_MODEL), BF16),
                   jax.ShapeDtypeStruct((D_MODEL, D_MODEL), BF16)]
                  + [jax.ShapeDtypeStruct(g.shape[:1] + g.shape[2:], g.dtype) for g in grads],
        scratch_shapes=[pltpu.VMEM((4, WIDTH, D_MODEL), F32), pltpu.VMEM((D_MODEL, D_MODEL), F32)]
                       + (_dma_sems(N_CHIP * n, N_CHIP * n) if n else []),
        compiler_params=_params(("arbitrary",)),
    )(dxo, y4, oc, z, proj, *([gates] * 4), wb, wout, *grads)
    return res[:7], list(res[7:])


def _attn_bwd(q, qcb, k, kcb, v, vcb, do, lse, delta, bps, dst, dcb):
    S = q.shape[0]
    tm = ATT_TILE
    nb = tm // CHUNK
    nblk = S // CHUNK

    ncur = nb * N_HEAD
    nprev = (nb + 1) * N_HEAD

    def body(q_ref, k_ref, v_ref, do_ref, l_ref, d_ref, kh_ref, vh_ref, qn_ref, don_ref, ln_ref, dn_ref, dst_ref,
             dq_ref, dk_ref, dv_ref, sc_s, sp_s, dpc_s, dpp_s, pc_s, pp_s, dsc_s, dsp_s):
        i = pl.program_id(0)

        def rows_of(n):
            if n < nb:
                rs = slice(n * CHUNK, (n + 1) * CHUNK)
                return rs, q_ref, do_ref, l_ref, d_ref
            return slice(0, CHUNK), qn_ref, don_ref, ln_ref, dn_ref

        def prev_kv(n, cs):
            if n == 0:
                return kh_ref[:, cs], vh_ref[:, cs]
            ps = slice((n - 1) * CHUNK, n * CHUNK)
            return k_ref[ps, cs], v_ref[ps, cs]

        def blk(n, h):
            return slice((n * N_HEAD + h) * CHUNK, (n * N_HEAD + h + 1) * CHUNK)

        pens, lses, deltas = [], [], []
        for n in range(nb + 1):
            rs, qr, dor, lr, dr = rows_of(n)
            gb = i * nb + n
            pen = jnp.where(gb % bps != 0, 0.0, NEG)
            if n == nb:
                pen = pen + jnp.where(gb < nblk, 0.0, NEG)
            pens.append(jnp.full((N_HEAD * CHUNK, 1), pen, F32))
            lblk, dblk = lr[rs, :], dr[rs, :]
            for h in range(N_HEAD):
                cs = slice(h * HEAD, (h + 1) * HEAD)
                qh, doh = qr[rs, cs], dor[rs, cs]
                lses.append(_col(lblk, h))
                deltas.append(_col(dblk, h))
                kp, vp = prev_kv(n, cs)
                sp_s[blk(n, h), :] = _dot_nt(qh, kp)
                dpp_s[blk(n, h), :] = _dot_nt(doh, vp)
                if n < nb:
                    sc_s[blk(n, h), :] = _dot_nt(qh, k_ref[rs, cs])
                    dpc_s[blk(n, h), :] = _dot_nt(doh, v_ref[rs, cs])
        lse = jnp.concatenate(lses, axis=0)
        delta = jnp.concatenate(deltas, axis=0)
        row = lax.broadcasted_iota(jnp.int32, (nprev * CHUNK, CHUNK), 0) & (CHUNK - 1)
        col = lax.broadcasted_iota(jnp.int32, (nprev * CHUNK, CHUNK), 1)
        sp = jnp.where(col >= row, sp_s[...] * ATT_SCALE, NEG) + jnp.concatenate(pens, axis=0)
        pp = jnp.exp(sp - lse)
        pp_s[...] = pp.astype(BF16)
        dsp_s[...] = (pp * (dpp_s[...] - delta)).astype(BF16)
        nc = ncur * CHUNK
        sc = jnp.where(col[:nc] <= row[:nc], sc_s[...] * ATT_SCALE, NEG)
        pc = jnp.exp(sc - lse[:nc])
        pc_s[...] = pc.astype(BF16)
        dsc_s[...] = (pc * (dpc_s[...] - delta[:nc])).astype(BF16)
        for n in range(nb):
            rs, qr, dor, _, _ = rows_of(n)
            rn, qnr, donr, _, _ = rows_of(n + 1)
            for h in range(N_HEAD):
                cs = slice(h * HEAD, (h + 1) * HEAD)
                kp, _ = prev_kv(n, cs)
                dq = _dot(dsc_s[blk(n, h), :], k_ref[rs, cs]) + _dot(dsp_s[blk(n, h), :], kp)
                dq_ref[rs, cs] = (dq * ATT_SCALE).astype(BF16)
                dk = _dot_tn(dsc_s[blk(n, h), :], qr[rs, cs]) + _dot_tn(dsp_s[blk(n + 1, h), :], qnr[rn, cs])
                dk_ref[rs, cs] = (dk * ATT_SCALE).astype(BF16)
                dv = _dot_tn(pc_s[blk(n, h), :], dor[rs, cs]) + _dot_tn(pp_s[blk(n + 1, h), :], donr[rn, cs])
                dv_ref[rs, cs] = dv.astype(BF16)

    def prev_halo(cb):
        return pl.BlockSpec((CHUNK, WIDTH), lambda i: (jnp.maximum(i * nb - 1, 0), cb))

    def next_halo(width, cb=0):
        return pl.BlockSpec((CHUNK, width), lambda i: (jnp.minimum(i * nb + nb, nblk - 1), cb))

    return pl.pallas_call(
        body, name=f"attn_bwd_{bps}",
        grid=(S // tm,),
        in_specs=[_rows(tm, WIDTH, qcb), _rows(tm, WIDTH, kcb), _rows(tm, WIDTH, vcb), _rows(tm, WIDTH),
                  _rows(tm, 128), _rows(tm, 128), prev_halo(kcb), prev_halo(vcb),
                  next_halo(WIDTH, qcb), next_halo(WIDTH), next_halo(128), next_halo(128), ANY],
        out_specs=[_rows(tm, WIDTH, dcb), _rows(tm, WIDTH), _rows(tm, WIDTH)],
        out_shape=[jax.ShapeDtypeStruct(dst.shape, BF16)] + [jax.ShapeDtypeStruct((S, WIDTH), BF16)] * 2,
        input_output_aliases={12: 0},
        scratch_shapes=[pltpu.VMEM((ncur * CHUNK, CHUNK), F32), pltpu.VMEM((nprev * CHUNK, CHUNK), F32),
                        pltpu.VMEM((ncur * CHUNK, CHUNK), F32), pltpu.VMEM((nprev * CHUNK, CHUNK), F32),
                        pltpu.VMEM((ncur * CHUNK, CHUNK), BF16), pltpu.VMEM((nprev * CHUNK, CHUNK), BF16),
                        pltpu.VMEM((ncur * CHUNK, CHUNK), BF16), pltpu.VMEM((nprev * CHUNK, CHUNK), BF16)],
        compiler_params=_params(("parallel",)),
    )(q, k, v, do, lse, delta, k, v, q, do, lse, delta, dst)


def _dilated_split(d):
    hp = min(N_HEAD, 16 // d)
    return hp, N_HEAD // hp, HEAD * hp


def _by_class(src_ref, dst, d, hp):
    for j in range(hp):
        dst[j] = pltpu.einshape("(tr)l->(rt)l", src_ref[:, j * HEAD:(j + 1) * HEAD], r=d)


def _from_class(src, dst_ref, d, hp, add_ref=None):
    for j in range(hp):
        cs = slice(j * HEAD, (j + 1) * HEAD)
        val = pltpu.einshape("(rt)l->(tr)l", src[j].astype(BF16), r=d)
        if add_ref is not None:
            val = (val.astype(F32) + add_ref[:, cs].astype(F32)).astype(BF16)
        dst_ref[:, cs] = val


def _attn_fwd_dilated(proj, qcb, kcb, vcb, d):
    S = proj.shape[0]
    T = CHUNK * d
    hp, nh, cw = _dilated_split(d)
    nblocks = d * hp

    def body(q_ref, k_ref, v_ref, o_ref, l_ref, qf, kst, vst, of, lf, sc_s, sp_s, pc_s, pp_s):
        i, hh = pl.program_id(0), pl.program_id(1)
        kf, vf = kst.at[i % 2, hh], vst.at[i % 2, hh]
        kpf, vpf = kst.at[1 - i % 2, hh], vst.at[1 - i % 2, hh]

        @pl.when(i == 0)
        def _():
            kpf[...] = jnp.zeros_like(kpf)
            vpf[...] = jnp.zeros_like(vpf)

        _by_class(q_ref, qf, d, hp)
        _by_class(k_ref, kf, d, hp)
        _by_class(v_ref, vf, d, hp)

        def blk(ref, r, j):
            return ref[j, r * CHUNK:(r + 1) * CHUNK, :]

        def bs(r, j):
            return slice((r * hp + j) * CHUNK, (r * hp + j + 1) * CHUNK)

        for r in range(d):
            for j in range(hp):
                qb = blk(qf, r, j)
                sc_s[bs(r, j), :] = _dot_nt(qb, blk(kf, r, j))
                sp_s[bs(r, j), :] = _dot_nt(qb, blk(kpf, r, j))
        row = lax.broadcasted_iota(jnp.int32, (nblocks * CHUNK, CHUNK), 0) & (CHUNK - 1)
        col = lax.broadcasted_iota(jnp.int32, (nblocks * CHUNK, CHUNK), 1)
        sc = jnp.where(col <= row, sc_s[...] * ATT_SCALE, NEG)
        sp = jnp.where(col >= row, sp_s[...] * ATT_SCALE, NEG) + jnp.where(i > 0, 0.0, NEG)
        m = jnp.maximum(jnp.max(sc, axis=-1, keepdims=True), jnp.max(sp, axis=-1, keepdims=True))
        ec = jnp.exp(sc - m)
        ep = jnp.exp(sp - m)
        den = jnp.sum(ec, axis=-1, keepdims=True) + jnp.sum(ep, axis=-1, keepdims=True)
        inv = 1.0 / den
        pc_s[...] = (ec * inv).astype(BF16)
        pp_s[...] = (ep * inv).astype(BF16)
        lse = m + jnp.log(den)
        lane = lax.broadcasted_iota(jnp.int32, (CHUNK, 128), 1)
        for r in range(d):
            lblk = jnp.zeros((CHUNK, 128), F32)
            for j in range(hp):
                o = _dot(pc_s[bs(r, j), :], blk(vf, r, j)) + _dot(pp_s[bs(r, j), :], blk(vpf, r, j))
                of[j, r * CHUNK:(r + 1) * CHUNK, :] = o
                lblk = jnp.where(lane == hh * hp + j, lse[bs(r, j)], lblk)
            lf[r * CHUNK:(r + 1) * CHUNK, :] = lblk
        _from_class(of, o_ref, d, hp)
        lnat = pltpu.einshape("(rt)l->(tr)l", lf[...], r=d)

        @pl.when(hh == 0)
        def _():
            l_ref[...] = lnat

        @pl.when(hh > 0)
        def _():
            l_ref[...] += lnat

    def cols(cb):
        return pl.BlockSpec((T, cw), lambda i, hh: (i, cb * nh + hh))

    tile = pltpu.VMEM((hp, T, HEAD), BF16)
    return pl.pallas_call(
        body, name=f"attn_fwd_dilated_{d}",
        grid=(S // T, nh),
        in_specs=[cols(qcb), cols(kcb), cols(vcb)],
        out_specs=[cols(0), pl.BlockSpec((T, 128), lambda i, hh: (i, 0))],
        out_shape=[jax.ShapeDtypeStruct((S, WIDTH), BF16), jax.ShapeDtypeStruct((S, 128), F32)],
        scratch_shapes=[tile, pltpu.VMEM((2, nh, hp, T, HEAD), BF16), pltpu.VMEM((2, nh, hp, T, HEAD), BF16),
                        pltpu.VMEM((hp, T, HEAD), F32), pltpu.VMEM((T, 128), F32),
                        pltpu.VMEM((nblocks * CHUNK, CHUNK), F32), pltpu.VMEM((nblocks * CHUNK, CHUNK), F32),
                        pltpu.VMEM((nblocks * CHUNK, CHUNK), BF16), pltpu.VMEM((nblocks * CHUNK, CHUNK), BF16)],
        compiler_params=_params(("arbitrary", "arbitrary")),
    )(proj, proj, proj)


def _attn_bwd_dilated(proj, qcb, kcb, vcb, do, lse, delta, d, dk_in, dv_in, dst, dcb):
    S = proj.shape[0]
    T = CHUNK * d
    nt = S // T
    hp, nh, cw = _dilated_split(d)
    nblocks = d * hp

    def body(q_ref, k_ref, v_ref, do_ref, l_ref, d_ref, dki_ref, dvi_ref, dst_ref, dq_ref, dk_ref, dv_ref,
             qf, dof, kbuf, vbuf, dqf, gk, gv,
             sc_s, sp_s, dpc_s, dpp_s, pc_s, pp_s, dsc_s, dsp_s):
        hh, i = pl.program_id(0), pl.program_id(1)
        kf, vf, newk, newv = kbuf.at[i % 2], vbuf.at[i % 2], gk.at[i % 2], gv.at[i % 2]
        kpf, vpf, acck, accv = kbuf.at[1 - i % 2], vbuf.at[1 - i % 2], gk.at[1 - i % 2], gv.at[1 - i % 2]

        @pl.when(i == 0)
        def _():
            for ref in (kbuf, vbuf, gk, gv):
                ref[...] = jnp.zeros_like(ref)
            dk_ref[...] = jnp.zeros_like(dk_ref)
            dv_ref[...] = jnp.zeros_like(dv_ref)

        def blk(ref, r, j):
            return ref[j, r * CHUNK:(r + 1) * CHUNK, :]

        def bs(r, j):
            return slice((r * hp + j) * CHUNK, (r * hp + j + 1) * CHUNK)

        @pl.when(i < nt)
        def _():
            _by_class(q_ref, qf, d, hp)
            _by_class(do_ref, dof, d, hp)
            _by_class(k_ref, kf, d, hp)
            _by_class(v_ref, vf, d, hp)
            lses, deltas = [], []
            lcls = pltpu.einshape("(tr)l->(rt)l", l_ref[...], r=d)
            dcls = pltpu.einshape("(tr)l->(rt)l", d_ref[...], r=d)
            for r in range(d):
                lblk = lcls[r * CHUNK:(r + 1) * CHUNK]
                dblk = dcls[r * CHUNK:(r + 1) * CHUNK]
                for j in range(hp):
                    lses.append(_col(lblk, hh * hp + j))
                    deltas.append(_col(dblk, hh * hp + j))
                    qb, dob = blk(qf, r, j), blk(dof, r, j)
                    sc_s[bs(r, j), :] = _dot_nt(qb, blk(kf, r, j))
                    dpc_s[bs(r, j), :] = _dot_nt(dob, blk(vf, r, j))
                    sp_s[bs(r, j), :] = _dot_nt(qb, blk(kpf, r, j))
                    dpp_s[bs(r, j), :] = _dot_nt(dob, blk(vpf, r, j))
            lse = jnp.concatenate(lses, axis=0)
            delta = jnp.concatenate(deltas, axis=0)
            row = lax.broadcasted_iota(jnp.int32, (nblocks * CHUNK, CHUNK), 0) & (CHUNK - 1)
            col = lax.broadcasted_iota(jnp.int32, (nblocks * CHUNK, CHUNK), 1)
            sp = jnp.where(col >= row, sp_s[...] * ATT_SCALE, NEG) + jnp.where(i > 0, 0.0, NEG)
            pp = jnp.exp(sp - lse)
            pp_s[...] = pp.astype(BF16)
            dsp_s[...] = (pp * (dpp_s[...] - delta)).astype(BF16)
            sc = jnp.where(col <= row, sc_s[...] * ATT_SCALE, NEG)
            pc = jnp.exp(sc - lse)
            pc_s[...] = pc.astype(BF16)
            dsc_s[...] = (pc * (dpc_s[...] - delta)).astype(BF16)
            for r in range(d):
                rows = slice(r * CHUNK, (r + 1) * CHUNK)
                for j in range(hp):
                    qb, dob = blk(qf, r, j), blk(dof, r, j)
                    dsc, dsp = dsc_s[bs(r, j), :], dsp_s[bs(r, j), :]
                    dqf[j, rows, :] = (_dot(dsc, blk(kf, r, j)) + _dot(dsp, blk(kpf, r, j))) * ATT_SCALE
                    newk[j, rows, :] = _dot_tn(dsc, qb) * ATT_SCALE
                    newv[j, rows, :] = _dot_tn(pc_s[bs(r, j), :], dob)
                    acck[j, rows, :] += _dot_tn(dsp, qb) * ATT_SCALE
                    accv[j, rows, :] += _dot_tn(pp_s[bs(r, j), :], dob)
            _from_class(dqf, dq_ref, d, hp)

        @pl.when(i > 0)
        def _():
            _from_class(acck, dk_ref, d, hp, dki_ref)
            _from_class(accv, dv_ref, d, hp, dvi_ref)

    def cur(width, cb, nsplit):
        return pl.BlockSpec((T, width), lambda hh, i: (jnp.minimum(i, nt - 1), cb * nsplit + hh * (nsplit > 1)))

    def lag():
        return pl.BlockSpec((T, cw), lambda hh, i: (jnp.maximum(i - 1, 0), hh))

    tile = pltpu.VMEM((hp, T, HEAD), BF16)
    acc = pltpu.VMEM((hp, T, HEAD), F32)
    f32s = pltpu.VMEM((nblocks * CHUNK, CHUNK), F32)
    b16s = pltpu.VMEM((nblocks * CHUNK, CHUNK), BF16)
    return pl.pallas_call(
        body, name=f"attn_bwd_dilated_{d}",
        grid=(nh, nt + 1),
        in_specs=[cur(cw, qcb, nh), cur(cw, kcb, nh), cur(cw, vcb, nh), cur(cw, 0, nh), cur(128, 0, 1), cur(128, 0, 1),
                  lag(), lag(), ANY],
        out_specs=[cur(cw, dcb, nh), lag(), lag()],
        out_shape=[jax.ShapeDtypeStruct(dst.shape, BF16)] + [jax.ShapeDtypeStruct((S, WIDTH), BF16)] * 2,
        input_output_aliases={8: 0},
        scratch_shapes=[tile, tile, pltpu.VMEM((2, hp, T, HEAD), BF16), pltpu.VMEM((2, hp, T, HEAD), BF16), acc,
                        pltpu.VMEM((2, hp, T, HEAD), F32), pltpu.VMEM((2, hp, T, HEAD), F32)]
                       + [f32s] * 4 + [b16s] * 4,
        compiler_params=_params(("arbitrary", "arbitrary")),
    )(proj, proj, proj, do, lse, delta, dk_in, dv_in, dst)


def _abm_bwd(proj, cdf, dy3, ln_g, ln_b, wsm, wsm_t, bias_full, pool_w, pool_wt, pool_scale, kv):
    S = proj.shape[0]
    tm = 512
    nchunk = tm // CHUNK
    nblk = S // CHUNK

    def body(u_ref, v_ref, ag_ref, p_ref, ph_ref, pg_ref, pgn_ref, mq_ref, mg_ref, cdf_ref, dy_ref, dypn_ref,
             lng_ref, lnb_ref, wsm_ref, wsmt_ref, bias_ref, pw_ref, pwt_ref, ps_ref, kv_ref,
             dab_ref, dm_ref, dlng_ref, dlnb_ref, dws_ref, dbias_ref, dpw_ref, dps_ref, dkv_ref,
             mix, dvl, ddn):
        i = pl.program_id(0)

        @pl.when(i == 0)
        def _():
            for r in (dlng_ref, dlnb_ref, dws_ref, dbias_ref, dpw_ref, dps_ref, dkv_ref):
                r[...] = jnp.zeros_like(r)

        au = u_ref[...].astype(F32)
        av = v_ref[...].astype(F32)
        ag = ag_ref[...].astype(F32)
        u, du = _gelu_and_grad(au, cdf_ref[0].astype(F32))
        v, dgelu_v = _gelu_and_grad(av, cdf_ref[1].astype(F32))
        vhat, rstd = _layer_norm_fwd(v)
        vln = (vhat * lng_ref[...] + lnb_ref[...]).astype(BF16)
        for c in range(nchunk):
            for h in range(N_HEAD):
                rs, cs = slice(c * CHUNK, (c + 1) * CHUNK), slice(h * HEAD, (h + 1) * HEAD)
                mix[rs, cs] = _dot(wsm_ref[h], vln[rs, cs]) + bias_ref[:, cs]
        dya = dy_ref[0].astype(F32)
        sg, dsg = _silu_and_grad(ag)
        mixed = mix[...]
        dab_ref[:, 2 * WIDTH:3 * WIDTH] = (dya * u * mixed * dsg).astype(BF16)
        dab_ref[:, 0:WIDTH] = (dya * mixed * sg * du).astype(BF16)
        dmixed = dya * u * sg
        dmb = dmixed.astype(BF16)
        tril = (lax.broadcasted_iota(jnp.int32, (CHUNK, CHUNK), 1)
                <= lax.broadcasted_iota(jnp.int32, (CHUNK, CHUNK), 0))
        for c in range(nchunk):
            rs = slice(c * CHUNK, (c + 1) * CHUNK)
            dbias_ref[...] += dmixed[rs, :]
            for h in range(N_HEAD):
                cs = slice(h * HEAD, (h + 1) * HEAD)
                dvl[rs, cs] = _dot(wsmt_ref[h], dmb[rs, cs])
                dws_ref[h] += jnp.where(tril, _dot_nt(dmb[rs, cs], vln[rs, cs]), 0.0)
        dvln = dvl[...]
        dlng_ref[...] += jnp.sum(dvln * vhat, axis=0, keepdims=True)
        dlnb_ref[...] += jnp.sum(dvln, axis=0, keepdims=True)
        dvh = dvln * lng_ref[...]
        dv = rstd * (dvh - jnp.mean(dvh, axis=-1, keepdims=True)
                     - vhat * jnp.mean(dvh * vhat, axis=-1, keepdims=True))
        dab_ref[:, WIDTH:2 * WIDTH] = (dv * dgelu_v).astype(BF16)

        halo_ok = (i > 0).astype(F32)
        for c in range(nchunk):
            rs = slice(c * CHUNK, (c + 1) * CHUNK)
            for g, win in enumerate(POOL_WINDOWS):
                cs = slice(g * HEAD, (g + 1) * HEAD)
                bcur, bprev = _band_masks(win)
                cur = p_ref[rs, cs]
                if c == 0:
                    prev = (ph_ref[:, cs].astype(F32) * halo_ok).astype(BF16)
                else:
                    prev = p_ref[(c - 1) * CHUNK:c * CHUNK, cs]
                sums = _dot(bcur, cur) + _dot(bprev, prev)
                dvl[rs, cs] = sums * _inv_count(i * tm + c * CHUNK, win) - cur.astype(F32)
        dmat = dvl[...].astype(BF16)
        for g in range(4):
            cs = slice(g * HEAD, (g + 1) * HEAD)
            mix[:, cs] = _dot(dmat[:, cs], pw_ref[g])
        yg = mix[...]
        pg = pg_ref[...].astype(F32)
        dyp = dy_ref[1].astype(F32)
        spg, dspg = _silu_and_grad(pg)
        dyy = dyp * spg
        scale = ps_ref[...]
        dab_ref[:, 4 * WIDTH:5 * WIDTH] = (dyp * yg * scale * dspg).astype(BF16)
        dps_ref[...] += jnp.sum(dyy * yg, axis=0, keepdims=True)
        dyg = (dyy * scale).astype(BF16)
        for g in range(4):
            cs = slice(g * HEAD, (g + 1) * HEAD)
            dpw_ref[g] += _dot_tn(dmat[:, cs], dyg[:, cs])
            mix[:, cs] = _dot(dyg[:, cs], pwt_ref[g])
        next_ok = (i + 1 < S // tm).astype(F32)
        dygn = (dypn_ref[...].astype(F32) * _silu(pgn_ref[...].astype(F32)) * scale * next_ok).astype(BF16)
        for c in range(nchunk + 1):
            for g, win in enumerate(POOL_WINDOWS):
                cs = slice(g * HEAD, (g + 1) * HEAD)
                if c < nchunk:
                    dd = mix[c * CHUNK:(c + 1) * CHUNK, cs]
                else:
                    dd = _dot(dygn[:, cs], pwt_ref[g])
                ddn[c * CHUNK:(c + 1) * CHUNK, cs] = dd * _inv_count(i * tm + c * CHUNK, win)
        ddnb = ddn[...].astype(BF16)
        for c in range(nchunk):
            rs = slice(c * CHUNK, (c + 1) * CHUNK)
            ns = slice((c + 1) * CHUNK, (c + 2) * CHUNK)
            for g, win in enumerate(POOL_WINDOWS):
                cs = slice(g * HEAD, (g + 1) * HEAD)
                bcur, bprev = _band_masks(win)
                dp = _dot_tn(bcur, ddnb[rs, cs]) + _dot_tn(bprev, ddnb[ns, cs]) - mix[rs, cs]
                dab_ref[rs, 3 * WIDTH + g * HEAD:3 * WIDTH + (g + 1) * HEAD] = dp.astype(BF16)

        mg = mg_ref[...].astype(F32)
        dym = dy_ref[2].astype(F32)
        smg, dsmg = _silu_and_grad(mg)
        dob = (dym * smg).astype(BF16)
        for h in range(N_HEAD):
            cs = slice(h * HEAD, (h + 1) * HEAD)
            vs = slice(WIDTH + h * HEAD, WIDTH + (h + 1) * HEAD)
            qh = mq_ref[:, cs]
            p = _mem_softmax(qh, kv_ref[:, cs])
            pb = p.astype(BF16)
            mix[:, cs] = _dot(pb, kv_ref[:, vs])
            dp = _dot_nt(dob[:, cs], kv_ref[:, vs])
            ds = (p * (dp - jnp.sum(p * dp, axis=-1, keepdims=True))).astype(BF16)
            dm_ref[:, cs] = (_dot(ds, kv_ref[:, cs]) * ATT_SCALE).astype(BF16)
            dkv_ref[:, cs] += _dot_tn(ds, qh) * ATT_SCALE
            dkv_ref[:, vs] += _dot_tn(pb, dob[:, cs])
        dm_ref[:, WIDTH:2 * WIDTH] = (dym * mix[...] * dsmg).astype(BF16)

    blk = tm // CHUNK
    small = [_full((1, WIDTH)), _full((1, WIDTH)), _full((N_HEAD, CHUNK, CHUNK)), _full((CHUNK, WIDTH)),
             _full((4, HEAD, HEAD)), _full((1, WIDTH)), _full((MEM_LEN, 2 * WIDTH))]
    return pl.pallas_call(
        body, name="abm_bwd",
        grid=(S // tm,),
        in_specs=[_rows(tm, WIDTH, CB_U), _rows(tm, WIDTH, CB_V), _rows(tm, WIDTH, CB_AGATE),
                  _rows(tm, WIDTH, CB_PIN),
                  pl.BlockSpec((CHUNK, WIDTH), lambda i: (jnp.maximum(i * blk - 1, 0), CB_PIN)),
                  _rows(tm, WIDTH, CB_PGATE),
                  pl.BlockSpec((CHUNK, WIDTH), lambda i: (jnp.minimum(i * blk + blk, nblk - 1), CB_PGATE)),
                  _rows(tm, WIDTH, CB_MQ), _rows(tm, WIDTH, CB_MGATE),
                  pl.BlockSpec((2, tm, WIDTH), lambda i: (0, i, 0)),
                  pl.BlockSpec((3, tm, WIDTH), lambda i: (0, i, 0)),
                  pl.BlockSpec((None, CHUNK, WIDTH), lambda i: (1, jnp.minimum(i * blk + blk, nblk - 1), 0)),
                  _full((1, WIDTH)), _full((1, WIDTH)), _full((N_HEAD, CHUNK, CHUNK)), _full((N_HEAD, CHUNK, CHUNK)),
                  _full((CHUNK, WIDTH)), _full((4, HEAD, HEAD)), _full((4, HEAD, HEAD)), _full((1, WIDTH)),
                  _full((MEM_LEN, 2 * WIDTH))],
        out_specs=[_rows(tm, 5 * WIDTH), _rows(tm, 2 * WIDTH)] + small,
        out_shape=[jax.ShapeDtypeStruct((S, D_BRANCHES), BF16), jax.ShapeDtypeStruct((S, 2 * WIDTH), BF16),
                   jax.ShapeDtypeStruct((1, WIDTH), F32), jax.ShapeDtypeStruct((1, WIDTH), F32),
                   jax.ShapeDtypeStruct((N_HEAD, CHUNK, CHUNK), F32), jax.ShapeDtypeStruct((CHUNK, WIDTH), F32),
                   jax.ShapeDtypeStruct((4, HEAD, HEAD), F32), jax.ShapeDtypeStruct((1, WIDTH), F32),
                   jax.ShapeDtypeStruct((MEM_LEN, 2 * WIDTH), F32)],
        scratch_shapes=[pltpu.VMEM((tm, WIDTH), F32), pltpu.VMEM((tm, WIDTH), F32),
                        pltpu.VMEM((tm + CHUNK, WIDTH), F32)],
        compiler_params=_params(("arbitrary",)),
    )(proj, proj, proj, proj, proj, proj, proj, proj, proj, cdf, dy3, dy3,
      ln_g, ln_b, wsm, wsm_t, bias_full, pool_w, pool_wt, pool_scale, kv)


def _bias_reduce(dbias_full):
    def body(d_ref, o_ref):
        d = d_ref[...]
        o_ref[...] = _put_cols([jnp.sum(d[:, h * HEAD:(h + 1) * HEAD], axis=1, keepdims=True) for h in range(N_HEAD)])

    return pl.pallas_call(body, name="bias_reduce", out_shape=jax.ShapeDtypeStruct((CHUNK, 128), F32))(dbias_full)


def _mem_bwd(mem, g, mem_n, w, dkv):
    def body(m_ref, g_ref, mn_ref, w_ref, dkv_ref, dw_ref, dg_ref):
        dkvb = dkv_ref[...].astype(BF16)
        dw_ref[...] = _dot_tn(mn_ref[...], dkvb).astype(BF16)
        dmn = _dot_nt(dkvb, w_ref[...])
        xf = m_ref[...]
        r = lax.rsqrt(jnp.mean(xf * xf, axis=-1, keepdims=True) + EPS)
        dg_ref[...] = jnp.sum(dmn * xf * r, axis=0, keepdims=True)

    return pl.pallas_call(
        body, name="mem_bwd",
        out_shape=[jax.ShapeDtypeStruct((D_MODEL, 2 * WIDTH), BF16), jax.ShapeDtypeStruct((1, D_MODEL), F32)],
        compiler_params=pltpu.CompilerParams(vmem_limit_bytes=VMEM_LIMIT),
    )(mem, g, mem_n, w, dkv)


def _dh_bwd(dpb, dpg, wt, wgt, x, g, dxo, parts=(), grads=()):
    S = x.shape[0]
    tm, tkb, tkg = 1024, D_BRANCHES // 4, D_GATES // 4
    nkb, nkg = 4, 4
    nk = nkb + nkg
    ni = S // tm
    n, m = len(parts), len(grads)

    def body(dpb_ref, wbr_ref, dpg_ref, wg_ref, x_ref, g_ref, dxo_ref, *rest):
        p_in, g_in = rest[:n], rest[n:n + m]
        dx_ref, dg_ref = rest[n + m:n + m + 2]
        p_out, g_out = rest[n + m + 2:2 * n + m + 2], rest[2 * n + m + 2:2 * (n + m) + 2]
        acc, sems = rest[2 * (n + m) + 2], rest[2 * (n + m) + 3:]
        second_sems, first_sems = (sems[:3] if n else ()), sems[3 if n else 0:]
        i, kk = pl.program_id(0), pl.program_id(1)

        @pl.when(jnp.logical_and(i == 0, kk == 0))
        def _():
            dg_ref[...] = jnp.zeros_like(dg_ref)
            if n:
                _comm_start(_rs_second(p_in, p_out, *second_sems))
            if m:
                _comm_start(_rs_first(g_in, g_out, *first_sems))

        @pl.when(kk == 0)
        def _():
            acc[...] = jnp.zeros_like(acc)

        @pl.when(kk < nkb)
        def _():
            acc[...] += _dot(dpb_ref[...], wbr_ref[...])

        @pl.when(kk >= nkb)
        def _():
            acc[...] += _dot(dpg_ref[...], wg_ref[...])

        @pl.when(kk == nk - 1)
        def _():
            xf = x_ref[...]
            r = lax.rsqrt(jnp.mean(xf * xf, axis=-1, keepdims=True) + EPS)
            xhat = xf * r
            dh = acc[...]
            dg_ref[...] += jnp.sum(dh * xhat, axis=0, keepdims=True)
            dxh = dh * g_ref[...]
            dx_ref[...] = dxo_ref[...] + r * (dxh - xhat * jnp.mean(dxh * xhat, axis=-1, keepdims=True))

        if n or m:
            @pl.when(jnp.logical_and(i == ni - 1, kk == nk - 1))
            def _():
                if n:
                    _comm_wait(_rs_second(p_in, p_out, *second_sems))
                if m:
                    _comm_wait(_rs_first(g_in, g_out, *first_sems))

    res = pl.pallas_call(
        body, name="dh_bwd_scatter" if (n or m) else "dh_bwd",
        grid=(ni, nk),
        in_specs=[pl.BlockSpec((tm, tkb), lambda i, k: (i, jnp.minimum(k, nkb - 1))),
                  pl.BlockSpec((tkb, D_MODEL), lambda i, k: (jnp.minimum(k, nkb - 1), 0)),
                  pl.BlockSpec((tm, tkg), lambda i, k: (i, jnp.maximum(k - nkb, 0))),
                  pl.BlockSpec((tkg, D_MODEL), lambda i, k: (jnp.maximum(k - nkb, 0), 0)),
                  pl.BlockSpec((tm, D_MODEL), lambda i, k: (i, 0)), pl.BlockSpec((1, D_MODEL), lambda i, k: (0, 0)),
                  pl.BlockSpec((tm, D_MODEL), lambda i, k: (i, 0))] + [ANY] * (n + m),
        out_specs=[pl.BlockSpec((tm, D_MODEL), lambda i, k: (i, 0)), pl.BlockSpec((1, D_MODEL), lambda i, k: (0, 0))]
                  + [ANY] * (n + m),
        out_shape=[jax.ShapeDtypeStruct((S, D_MODEL), F32), jax.ShapeDtypeStruct((1, D_MODEL), F32)]
                  + [jax.ShapeDtypeStruct(p.shape, p.dtype) for p in parts]
                  + [jax.ShapeDtypeStruct(gr.shape[:1] + gr.shape[2:], gr.dtype) for gr in grads],
        scratch_shapes=[pltpu.VMEM((tm, D_MODEL), F32)] + (_dma_sems(3 * n, 3 * n, n) if n else [])
                       + (_dma_sems(N_CHIP * m, N_CHIP * m) if m else []),
        compiler_params=_params(("arbitrary", "arbitrary")),
    )(dpb, wt, dpg, wgt, x, g, dxo, *parts, *grads)
    return res[0], res[1], list(res[2:2 + n]), list(res[2 + n:])


def _dw_in(h, dpb, dpg, parts=()):
    S = h.shape[0]
    tk = 2048
    nk = S // tk
    n = len(parts)
    tmb = D_BRANCHES // 4
    ng = D_GATES // GATE_TILE

    def accumulate(a_ref, h_ref, o_ref, acc):
        kk = pl.program_id(1)

        @pl.when(kk == 0)
        def _():
            acc[...] = jnp.zeros_like(acc)

        acc[...] += _dot_tn(a_ref[...], h_ref[...])

        @pl.when(kk == nk - 1)
        def _():
            o_ref[...] = acc[...].astype(BF16)

    def branches(a_ref, h_ref, *rest):
        p_in, o_ref, p_out = rest[:n], rest[n], rest[n + 1:2 * n + 1]
        acc, sems = rest[2 * n + 1], rest[2 * n + 2:]
        i, kk = pl.program_id(0), pl.program_id(1)

        if n:
            @pl.when(jnp.logical_and(i == 0, kk == 0))
            def _():
                _comm_start(_rs_second(p_in, p_out, *sems))

        accumulate(a_ref, h_ref, o_ref, acc)

        if n:
            @pl.when(jnp.logical_and(i == 3, kk == nk - 1))
            def _():
                _comm_wait(_rs_second(p_in, p_out, *sems))

    def gates(a_ref, h_ref, dst_ref, o_ref, acc):
        accumulate(a_ref, h_ref, o_ref, acc)

    res = pl.pallas_call(
        branches, name="dw_in_branches_scatter" if n else "dw_in_branches",
        grid=(4, nk),
        in_specs=[pl.BlockSpec((tk, tmb), lambda i, k: (k, i)), pl.BlockSpec((tk, D_MODEL), lambda i, k: (k, 0))]
                 + [ANY] * n,
        out_specs=[pl.BlockSpec((tmb, D_MODEL), lambda i, k: (i, 0))] + [ANY] * n,
        out_shape=[jax.ShapeDtypeStruct((D_IN, D_MODEL), BF16)]
                  + [jax.ShapeDtypeStruct(p.shape, p.dtype) for p in parts],
        scratch_shapes=[pltpu.VMEM((tmb, D_MODEL), F32)] + (_dma_sems(3 * n, 3 * n, n) if n else []),
        compiler_params=_params(("arbitrary", "arbitrary")),
    )(dpb, h, *parts)
    dwt = pl.pallas_call(
        gates, name="dw_in_gates",
        grid=(ng, nk),
        in_specs=[pl.BlockSpec((tk, GATE_TILE), lambda i, k: (k, i)), pl.BlockSpec((tk, D_MODEL), lambda i, k: (k, 0)),
                  ANY],
        out_specs=pl.BlockSpec((GATE_TILE, D_MODEL), lambda i, k: (D_BRANCHES // GATE_TILE + i, 0)),
        out_shape=jax.ShapeDtypeStruct((D_IN, D_MODEL), BF16),
        input_output_aliases={2: 0},
        scratch_shapes=[pltpu.VMEM((GATE_TILE, D_MODEL), F32)],
        compiler_params=_params(("parallel", "arbitrary")),
    )(dpg, h, res[0])
    return dwt, list(res[1:])


def _row_tile(R, C, block_bytes=2 << 20):
    for cand in range(min(R, block_bytes // (C * 4)) // 8 * 8, 0, -8):
        if R % cand == 0:
            return cand
    return R


def _adamw_update(p_ref, w_ref, m_ref, v_ref, g_ref, d_ref, nm_ref, nv_ref):
    c1 = 1.0 / (1.0 - ADAM_B1 ** ADAM_STEP)
    c2 = 1.0 / (1.0 - ADAM_B2 ** ADAM_STEP)
    g = p_ref[0].astype(F32)
    for k in range(1, p_ref.shape[0]):
        g = g + p_ref[k].astype(F32)
    nm = ADAM_B1 * m_ref[...] + (1.0 - ADAM_B1) * g
    nv = ADAM_B2 * v_ref[...] + (1.0 - ADAM_B2) * (g * g)
    g_ref[...] = g
    nm_ref[...] = nm
    nv_ref[...] = nv
    d_ref[...] = -ADAM_LR * ((nm * c1) / (jnp.sqrt(nv * c2) + ADAM_EPS) + ADAM_WD * w_ref[...])


def _adamw(parts, w, m, v, name):
    P, R, C = parts.shape
    tr = _row_tile(R, C)

    def body(*refs):
        _adamw_update(*refs)

    spec = pl.BlockSpec((tr, C), lambda i: (i, 0))
    return pl.pallas_call(
        body, name=name,
        grid=(R // tr,),
        in_specs=[pl.BlockSpec((P, tr, C), lambda i: (0, i, 0)), spec, spec, spec],
        out_specs=[spec] * 4,
        out_shape=[jax.ShapeDtypeStruct((R, C), F32)] * 4,
        compiler_params=_params(("parallel",)),
    )(parts, w, m, v)


def _adamw_layers(parts, w, m, v, name):
    depth = len(parts)
    P, R, C = parts[0].shape
    tr = _row_tile(R, C, 1 << 20)

    def body(*refs):
        layer = pl.program_id(0)
        for k in range(depth):
            @pl.when(layer == k)
            def _(k=k):
                _adamw_update(refs[k], *refs[depth:])

    def part_spec(k):
        return pl.BlockSpec((P, tr, C), lambda l, i: (0, jnp.where(l == k, i, 0), 0))

    spec = pl.BlockSpec((None, tr, C), lambda l, i: (l, i, 0))
    return pl.pallas_call(
        body, name=name,
        grid=(depth, R // tr),
        in_specs=[part_spec(k) for k in range(depth)] + [spec] * 3,
        out_specs=[spec] * 4,
        out_shape=[jax.ShapeDtypeStruct((depth, R, C), F32)] * 4,
        compiler_params=_params(("arbitrary", "arbitrary")),
    )(*parts, w, m, v)


def _place():
    return lax.axis_index("x"), lax.axis_index("y"), lax.axis_index("c")


def _all_gather(shards):
    n = len(shards)

    def body(*refs):
        ins, outs = refs[:n], refs[n:2 * n]
        send1, recv1, local_sems, send2, recv2 = refs[2 * n:]
        first = _ag_first(ins, outs, send1, recv1, local_sems)
        second = _ag_second(outs, send2, recv2)
        _comm_start(first)
        for j in range(3):
            for a in range(n):
                first[2][4 * a + 1 + j].wait_recv()
            for a in range(n):
                second[1][3 * a + j].start()
        for a in range(n):
            first[2][4 * a].wait_recv()
        for cp in second[2]:
            cp.wait_recv()
        for cp in first[1] + second[1]:
            cp.wait_send()
        for cp in first[0]:
            cp.wait()

    return pl.pallas_call(
        body, name="weights_all_gather",
        in_specs=[ANY] * n, out_specs=[ANY] * n,
        out_shape=[jax.ShapeDtypeStruct((N_DEV,) + s.shape, s.dtype) for s in shards],
        scratch_shapes=_dma_sems(4 * n, 4 * n, n, 3 * n, 3 * n),
        compiler_params=pltpu.CompilerParams(has_side_effects=True),
    )(*shards)


N_BIG = 4


def _dev(p):
    return 4 * p[0] + 2 * p[1] + p[2]


def _other_chips(x, y):
    return [(1 - x, y), (x, 1 - y), (1 - x, 1 - y)]


def _remote(src, dst, send_sems, recv_sems, k, to):
    return pltpu.make_async_remote_copy(src_ref=src, dst_ref=dst, send_sem=send_sems.at[k], recv_sem=recv_sems.at[k],
                                        device_id=to, device_id_type=MESH)


def _ag_first(ins, outs, send_sems, recv_sems, local_sems):
    x, y, c = _place()
    me = (x, y, c)
    targets = [(x, y, 1 - c)] + [(*chip, c) for chip in _other_chips(x, y)]
    local, out, inc = [], [], []
    for a in range(len(ins)):
        local.append(pltpu.make_async_copy(ins[a], outs[a].at[_dev(me)], local_sems.at[a]))
        for k, to in enumerate(targets):
            out.append(_remote(ins[a], outs[a].at[_dev(me)], send_sems, recv_sems, 4 * a + k, to))
            inc.append(_remote(ins[a], outs[a].at[_dev(to)], send_sems, recv_sems, 4 * a + k, to))
    return local, out, inc


def _ag_second(bufs, send_sems, recv_sems):
    x, y, c = _place()
    out, inc = [], []
    for a in range(len(bufs)):
        for j, chip in enumerate(_other_chips(x, y)):
            mine, theirs = bufs[a].at[_dev((*chip, c))], bufs[a].at[_dev((*chip, 1 - c))]
            out.append(_remote(mine, mine, send_sems, recv_sems, 3 * a + j, (x, y, 1 - c)))
            inc.append(_remote(theirs, theirs, send_sems, recv_sems, 3 * a + j, (x, y, 1 - c)))
    return [], out, inc


def _rs_first(ins, outs, send_sems, recv_sems):
    x, y, c = _place()
    out = [_remote(ins[a].at[j, 1 - c], outs[a].at[j], send_sems, recv_sems, N_CHIP * a + j, (x, y, 1 - c))
           for a in range(len(ins)) for j in range(N_CHIP)]
    return [], out, out


def _rs_second(ins, outs, send_sems, recv_sems, local_sems):
    x, y, c = _place()
    my_chip = 2 * x + y
    local, out, inc = [], [], []
    for a in range(len(ins)):
        local.append(pltpu.make_async_copy(ins[a].at[my_chip], outs[a].at[my_chip], local_sems.at[a]))
        for k, (ox, oy) in enumerate(_other_chips(x, y)):
            out.append(_remote(ins[a].at[2 * ox + oy], outs[a].at[my_chip], send_sems, recv_sems, 3 * a + k, (ox, oy, c)))
            inc.append(_remote(ins[a].at[2 * ox + oy], outs[a].at[2 * ox + oy], send_sems, recv_sems, 3 * a + k,
                               (ox, oy, c)))
    return local, out, inc


def _comm_start(exchange):
    local, out, _ = exchange
    for cp in local + out:
        cp.start()


def _comm_wait(exchange):
    local, out, inc = exchange
    for cp in inc:
        cp.wait_recv()
    for cp in out:
        cp.wait_send()
    for cp in local:
        cp.wait()


def _dma_sems(*counts):
    return [pltpu.SemaphoreType.DMA((n,)) for n in counts]


def _pair_sum(grads, recvs):
    n = len(grads)

    def body(c_ref, *refs):
        for a in range(n):
            refs[2 * n + a][...] = (refs[a][...].astype(F32) + refs[n + a][...].astype(F32)).astype(BF16)

    def g_spec(g):
        return pl.BlockSpec((None, None) + g.shape[2:], lambda j, c_ref: (j, c_ref[0], 0, 0))

    def r_spec(r):
        return pl.BlockSpec((None,) + r.shape[1:], lambda j, c_ref: (j, 0, 0))

    return pl.pallas_call(
        body, name="pair_sum",
        grid_spec=pltpu.PrefetchScalarGridSpec(
            num_scalar_prefetch=1, grid=(N_CHIP,),
            in_specs=[g_spec(g) for g in grads] + [r_spec(r) for r in recvs],
            out_specs=[r_spec(r) for r in recvs]),
        out_shape=[jax.ShapeDtypeStruct(r.shape, BF16) for r in recvs],
        compiler_params=_params(("parallel",)),
    )(lax.axis_index("c").reshape(1).astype(jnp.int32), *grads, *recvs)


SMALL_ROWS = 544


def _all_reduce_small(buf, parts=()):
    n = len(parts)

    def body(in_ref, *rest):
        p_in, out_ref, p_out = rest[:n], rest[n], rest[n + 1:2 * n + 1]
        recv, acc, send1, recv1, send2, recv2 = rest[2 * n + 1:2 * n + 7]
        scatter_sems = rest[2 * n + 7:]
        x, y, c = _place()
        me = 4 * x + 2 * y + c
        peers = [(x ^ (r >> 2), y ^ ((r >> 1) & 1), c ^ (r & 1)) for r in range(1, N_DEV)]

        def idx(p):
            return 4 * p[0] + 2 * p[1] + p[2]

        if n:
            _comm_start(_rs_second(p_in, p_out, *scatter_sems))
        first = [pltpu.make_async_remote_copy(
            src_ref=in_ref.at[idx(p)], dst_ref=recv.at[me], send_sem=send1.at[r], recv_sem=recv1.at[r],
            device_id=p, device_id_type=MESH) for r, p in enumerate(peers)]
        for cp in first:
            cp.start()
        recv[me] = in_ref[me]
        for r, p in enumerate(peers):
            pltpu.make_async_remote_copy(
                src_ref=in_ref.at[idx(p)], dst_ref=recv.at[idx(p)], send_sem=send1.at[r], recv_sem=recv1.at[r],
                device_id=p, device_id_type=MESH).wait_recv()
        total = recv[0]
        for k in range(1, N_DEV):
            total = total + recv[k]
        acc[...] = total
        out_ref[me] = total
        second = [pltpu.make_async_remote_copy(
            src_ref=acc, dst_ref=out_ref.at[me], send_sem=send2.at[r], recv_sem=recv2.at[r],
            device_id=p, device_id_type=MESH) for r, p in enumerate(peers)]
        for cp in second:
            cp.start()
        for r, p in enumerate(peers):
            pltpu.make_async_remote_copy(
                src_ref=acc, dst_ref=out_ref.at[idx(p)], send_sem=send2.at[r], recv_sem=recv2.at[r],
                device_id=p, device_id_type=MESH).wait_recv()
        for cp in first + second:
            cp.wait_send()
        if n:
            _comm_wait(_rs_second(p_in, p_out, *scatter_sems))

    vm = pl.BlockSpec(memory_space=pltpu.VMEM)
    res = pl.pallas_call(
        body, name="small_grads_all_reduce",
        in_specs=[vm] + [ANY] * n, out_specs=[vm] + [ANY] * n,
        out_shape=[jax.ShapeDtypeStruct(buf.shape, F32)] + [jax.ShapeDtypeStruct(p.shape, p.dtype) for p in parts],
        scratch_shapes=[pltpu.VMEM(buf.shape, F32), pltpu.VMEM(buf.shape[1:], F32)] + _dma_sems(7, 7, 7, 7)
                       + (_dma_sems(3 * n, 3 * n, n) if n else []),
        compiler_params=pltpu.CompilerParams(has_side_effects=True, vmem_limit_bytes=VMEM_LIMIT),
    )(buf, *parts)
    return res[0], list(res[1:])


def _dilate(a, d):
    if d == 1:
        return a
    S, C = a.shape
    return a.reshape(S // d, d, C).transpose(1, 0, 2).reshape(S, C)


def _undilate(a, d):
    if d == 1:
        return a
    S, C = a.shape
    return a.reshape(d, S // d, C).transpose(1, 0, 2).reshape(S, C)


def _cols(a, cb, n=1):
    return a[:, cb * WIDTH:(cb + n) * WIDTH]


def _to_blocks(g, kind):
    if kind == "rows":
        C = g.shape[1]
        return g.reshape(N_CHIP, 2, -1, C)
    return g.reshape(4 * WIDTH, N_CHIP, 2, -1).transpose(1, 2, 0, 3)


SMALL = ("norm_g", "gm_ln_g", "gm_ln_b", "gm_ws", "gm_bs", "pool_w", "pool_scale", "mem_norm_g", "final_norm_g")


def _pack_small(tree):
    flat = jnp.concatenate([tree[k].reshape(-1, 128) for k in SMALL], axis=0)
    return jnp.pad(flat, ((0, N_DEV * SMALL_ROWS - flat.shape[0]), (0, 0)))


def _unpack_small(flat, like):
    out, at = {}, 0
    for k in SMALL:
        rows = like[k].size // 128
        out[k] = flat[at:at + rows].reshape(like[k].shape)
        at += rows
    return out


def _make_layer(wt, wkv, wb, wout, norm_g, mem_norm_g, ln_g, ln_b, gm_ws, gm_bs, pool_w, pool_scale):
    tril = jnp.tril(jnp.ones((CHUNK, CHUNK), bool))
    wsm = jnp.where(tril, gm_ws, 0.0).astype(BF16)
    pw = pool_w.astype(BF16)
    return dict(wt=wt, wgt=wt[D_BRANCHES:], wkv=wkv, wb=wb, wout=wout, g=norm_g[None], mg=mem_norm_g[None],
                ln_g=ln_g[None],
                ln_b=ln_b[None], wsm=wsm, wsm_t=wsm.transpose(0, 2, 1), pw=pw, pw_t=pw.transpose(0, 2, 1),
                ps=pool_scale[None], bias=jnp.repeat(gm_bs.T, HEAD, axis=1))


def _layer_fwd(xl, mem0, L, next_shards=()):
    S = xl.shape[0]
    proj, gates, h, half_gathered = _in_proj(xl, L["g"], L["wt"], L["wgt"], next_shards[:1])
    kv, mem_n = _mem_kv(mem0, L["mg"], L["wkv"])
    y4, cdf, gathered = _abm_fwd(proj, L["ln_g"], L["ln_b"], L["wsm"], L["bias"], L["pw"], L["ps"], kv, half_gathered)
    o_g, l_g = [], []
    for gi, d in enumerate(DILATIONS):
        if d == 1:
            o, lse = _attn_fwd(proj, CB_Q0, proj, CB_K, proj, CB_CV, S // CHUNK)
        else:
            o, lse = _attn_fwd_dilated(proj, CB_Q0 + gi, CB_K, CB_CV, d)
        o_g.append(o)
        l_g.append(lse)
    (xn, y4, oc, lse, z), rest = _merge_fwd(xl, y4, o_g, l_g, proj, gates, L["wb"], L["wout"], next_shards[1:])
    saved = dict(x=xl, proj=proj, gates=gates, h=h, kv=kv, mem_n=mem_n, y4=y4, cdf=cdf, oc=oc, lse=lse, z=z)
    return xn, saved, gathered + rest


def _place_cols(dst, piece, cb):
    return lax.dynamic_update_slice(dst, piece, (0, cb * WIDTH))


def _layer_bwd(dx, mem0, L, sv, later=(), last=False):
    S = dx.shape[0]
    proj = sv["proj"]
    (dy3, doc, delta, dcg, dgm, dwb, dwout), from_sibling = _merge_bwd(
        dx, sv["y4"], sv["oc"], sv["z"], proj, sv["gates"], L["wb"], L["wout"], later)
    pair = _pair_sum(later, from_sibling) if later else ()
    dpb, dm, dlng, dlnb, dws, dbias, dpw, dps, dkv = _abm_bwd(
        proj, sv["cdf"], dy3, L["ln_g"], L["ln_b"], L["wsm"], L["wsm_t"], L["bias"], L["pw"], L["pw_t"], L["ps"], sv["kv"])
    dk, dv = None, None
    for gi, d in enumerate(DILATIONS):
        if d == 1:
            dpb, dk, dv = _attn_bwd(proj, CB_Q0, proj, CB_K, proj, CB_CV, doc, sv["lse"], delta, S // CHUNK,
                                    dpb, CB_Q0)
        else:
            dpb, dk, dv = _attn_bwd_dilated(proj, CB_Q0 + gi, CB_K, CB_CV, doc, sv["lse"], delta, d, dk, dv,
                                            dpb, CB_Q0 + gi)
    dpb = _place_cols(dpb, dk, CB_K)
    dpb = _place_cols(dpb, dv, CB_CV)
    dpb = _place_cols(dpb, dcg, CB_CGATE)
    dpb = _place_cols(dpb, dm, CB_MQ)
    dwkv, dmg = _mem_bwd(mem0, L["mg"], sv["mem_n"], L["wkv"], dkv)
    dwin_t, parts_rest = _dw_in(sv["h"], dpb, dgm, pair[1:])
    big = _blocked(dict(w_in=dwin_t, w_mem_kv=dwkv, w_branch=dwb, w_out=dwout))
    dxi, dng, parts, from_sibling = _dh_bwd(dpb, dgm, L["wt"], L["wgt"], sv["x"], L["g"], dx, pair[:1],
                                            big if last else ())
    parts = parts + parts_rest
    small = dict(norm_g=dng[0], gm_ln_g=dlng[0], gm_ln_b=dlnb[0], gm_ws=dws,
                 gm_bs=_bias_reduce(dbias)[:, :N_HEAD].T, pool_w=dpw, pool_scale=dps[0], mem_norm_g=dmg[0])
    return dxi, big, small, parts, from_sibling


BIG = ("w_in", "w_mem_kv", "w_branch", "w_out")


def _blocked(big):
    return [_to_blocks(big["w_in"], "rows"), _to_blocks(big["w_mem_kv"], "rows"),
            _to_blocks(big["w_branch"], "branch"), _to_blocks(big["w_out"], "rows")]


def _full_weights(gathered):
    win_t, wkv, wb, wout = gathered
    return (win_t.reshape(D_IN, D_MODEL), wkv.reshape(D_MODEL, 2 * WIDTH),
            wb.reshape(N_DEV, 4, WIDTH, -1).transpose(1, 2, 0, 3).reshape(4, WIDTH, D_MODEL),
            wout.reshape(D_MODEL, D_MODEL))


def kernel(x, mem, norm_g, w_in, gm_ln_g, gm_ln_b, gm_ws, gm_bs, pool_w, pool_scale, mem_norm_g, w_mem_kv, w_branch, w_out, final_norm_g, loss_target, m_norm_g, m_w_in, m_gm_ln_g, m_gm_ln_b, m_gm_ws, m_gm_bs, m_pool_w, m_pool_scale, m_mem_norm_g, m_w_mem_kv, m_w_branch, m_w_out, m_final_norm_g, v_norm_g, v_w_in, v_gm_ln_g, v_gm_ln_b, v_gm_ws, v_gm_bs, v_pool_w, v_pool_scale, v_mem_norm_g, v_w_mem_kv, v_w_branch, v_w_out, v_final_norm_g):
    x0 = x[0]
    mem0 = mem[0]
    tgt = loss_target[0]
    S = x0.shape[0]

    shards = [[w_in[l].T.astype(BF16), w_mem_kv[l].astype(BF16), w_branch[l].astype(BF16).reshape(4 * WIDTH, -1),
               w_out[l].astype(BF16)] for l in range(DEPTH)]
    gathered = _all_gather(shards[0])
    layers, saved = [], []
    xl = x0
    for l in range(DEPTH):
        layers.append(_make_layer(*_full_weights(gathered), norm_g[l], mem_norm_g[l], gm_ln_g[l], gm_ln_b[l],
                                  gm_ws[l], gm_bs[l], pool_w[l], pool_scale[l]))
        xl, sv, gathered = _layer_fwd(xl, mem0, layers[l], shards[l + 1] if l + 1 < DEPTH else ())
        saved.append(sv)

    loss_part, dx, d_final = _loss_head(xl, final_norm_g[None], tgt)
    loss = lax.psum(loss_part[0, 0], ("x", "y", "c"))

    small = {k: [None] * DEPTH for k in SMALL if k != "final_norm_g"}
    parts = [None] * DEPTH
    later = ()
    for l in reversed(range(DEPTH)):
        dx, gb, gs, done, from_sibling = _layer_bwd(dx, mem0, layers[l], saved[l], later, last=(l == 0))
        if later:
            parts[l + 1] = done
        later = gb
        for k in gs:
            small[k][l] = gs[k]
    grad_x = dx[None]
    small_tree = {k: jnp.stack(small[k]) for k in small}
    small_tree["final_norm_g"] = d_final[0]
    reduced, parts[0] = _all_reduce_small(_pack_small(small_tree).reshape(N_DEV, SMALL_ROWS, 128),
                                          _pair_sum(later, from_sibling))

    weights = dict(norm_g=norm_g, w_in=w_in, gm_ln_g=gm_ln_g, gm_ln_b=gm_ln_b, gm_ws=gm_ws, gm_bs=gm_bs,
                   pool_w=pool_w, pool_scale=pool_scale, mem_norm_g=mem_norm_g, w_mem_kv=w_mem_kv,
                   w_branch=w_branch, w_out=w_out, final_norm_g=final_norm_g)
    m_in = dict(norm_g=m_norm_g, w_in=m_w_in, gm_ln_g=m_gm_ln_g, gm_ln_b=m_gm_ln_b, gm_ws=m_gm_ws, gm_bs=m_gm_bs,
                pool_w=m_pool_w, pool_scale=m_pool_scale, mem_norm_g=m_mem_norm_g, w_mem_kv=m_w_mem_kv,
                w_branch=m_w_branch, w_out=m_w_out, final_norm_g=m_final_norm_g)
    v_in = dict(norm_g=v_norm_g, w_in=v_w_in, gm_ln_g=v_gm_ln_g, gm_ln_b=v_gm_ln_b, gm_ws=v_gm_ws, gm_bs=v_gm_bs,
                pool_w=v_pool_w, pool_scale=v_pool_scale, mem_norm_g=v_mem_norm_g, w_mem_kv=v_w_mem_kv,
                w_branch=v_w_branch, w_out=v_w_out, final_norm_g=v_final_norm_g)
    res = {}
    def view(k, arr):
        return arr.transpose(0, 2, 1) if k == "w_in" else arr

    for a, k in enumerate(BIG):
        shape = view(k, weights[k]).shape
        by_layer = [parts[l][a] for l in range(DEPTH)]
        lrc = (DEPTH,) + by_layer[0].shape[1:]
        outs = _adamw_layers(by_layer, view(k, weights[k]).reshape(lrc), view(k, m_in[k]).reshape(lrc),
                             view(k, v_in[k]).reshape(lrc), "adamw_" + k)
        res[k] = [view(k, o.reshape(shape)) for o in outs]
    outs = _adamw(reduced.reshape(1, N_DEV * SMALL_ROWS, 128), _pack_small(weights), _pack_small(m_in),
                  _pack_small(v_in), "adamw_small")
    unpacked = [_unpack_small(o, weights) for o in outs]
    for k in SMALL:
        res[k] = [u[k] for u in unpacked]

    order = ("norm_g", "w_in", "gm_ln_g", "gm_ln_b", "gm_ws", "gm_bs", "pool_w", "pool_scale", "mem_norm_g",
             "w_mem_kv", "w_branch", "w_out", "final_norm_g")
    return (loss, grad_x, *[res[k][0] for k in order], *[res[k][1] for k in order],
            *[res[k][2] for k in order], *[res[k][3] for k in order])
```

```python
import functools
import math

import jax
import jax.numpy as jnp
from jax import lax
from jax.experimental import pallas as pl
from jax.experimental.pallas import tpu as pltpu

F32 = jnp.float32
BF16 = jnp.bfloat16

D_MODEL = 1024
DEPTH = 4
WIDTH = 512
D_IN = 10752
HEAD = 128
N_HEAD = 4
CHUNK = 128
MEM_LEN = 256
POOL_WINDOWS = (2, 4, 8, 16)
DILATIONS = (1, 4, 16)
EPS = 1e-6
NEG = -1e30
ATT_SCALE = HEAD ** -0.5
N_DEV = 8
N_CHIP = 4

D_BRANCHES = 6656
D_GATES = D_IN - D_BRANCHES
CB_U, CB_V, CB_AGATE, CB_PIN, CB_PGATE = 0, 1, 2, 3, 4
CB_Q0, CB_K, CB_CV, CB_CGATE, CB_MQ, CB_MGATE = 5, 8, 9, 10, 11, 12

ADAM_LR = 0.001
ADAM_B1 = 0.9
ADAM_B2 = 0.999
ADAM_EPS = 1e-08
ADAM_WD = 0.01
ADAM_STEP = 10

VMEM_LIMIT = 56 * 1024 * 1024
MESH = pl.DeviceIdType.MESH
ANY = pl.BlockSpec(memory_space=pl.ANY)

NT = (((1,), (1,)), ((), ()))
TN = (((0,), (0,)), ((), ()))


def _dot(a, b):
    return jnp.dot(a, b, preferred_element_type=F32)


def _dot_nt(a, b):
    return lax.dot_general(a, b, NT, preferred_element_type=F32)


def _dot_tn(a, b):
    return lax.dot_general(a, b, TN, preferred_element_type=F32)


def _sigmoid(x):
    return 0.5 * jnp.tanh(0.5 * x) + 0.5


def _silu(x):
    return x * _sigmoid(x)


def _silu_and_grad(x):
    s = _sigmoid(x)
    return x * s, s * (1.0 + x * (1.0 - s))


def _normal_cdf(x):
    return 0.5 * (1.0 + lax.erf(x * (2.0 ** -0.5)))


def _gelu_and_grad(x, cdf):
    return x * cdf, cdf + x * jnp.exp(-0.5 * x * x) * (1.0 / math.sqrt(2.0 * math.pi))


def _col(blk, h):
    lane = lax.broadcasted_iota(jnp.int32, blk.shape, 1)
    return jnp.sum(jnp.where(lane == h, blk, 0.0), axis=1, keepdims=True)


def _put_cols(cols):
    rows = cols[0].shape[0]
    lane = lax.broadcasted_iota(jnp.int32, (rows, 128), 1)
    out = jnp.zeros((rows, 128), F32)
    for h, cv in enumerate(cols):
        out = jnp.where(lane == h, cv, out)
    return out


def _params(sem, vmem=VMEM_LIMIT):
    return pltpu.CompilerParams(dimension_semantics=sem, vmem_limit_bytes=vmem)


def _full(shape):
    nd = len(shape)
    return pl.BlockSpec(shape, lambda *_: (0,) * nd)


def _rows(tm, width, cb=0):
    return pl.BlockSpec((tm, width), lambda i: (i, cb))


GATE_TILE = 512


def _in_proj(x, g, wt, wgt, shards=()):
    S = x.shape[0]
    tm, tnb, tng = 1024, D_BRANCHES // 4, D_GATES // 4
    njb, njg = 4, 4
    n = len(shards)
    ni, nj = S // tm, njb + njg

    def body(x_ref, g_ref, wbr_ref, wg_ref, *rest):
        ins, (proj_ref, gates_ref, h_ref), outs = rest[:n], rest[n:n + 3], rest[n + 3:2 * n + 3]
        hs, sems = rest[2 * n + 3], rest[2 * n + 4:]
        i, j = pl.program_id(0), pl.program_id(1)

        if n:
            @pl.when(jnp.logical_and(i == 0, j == 0))
            def _():
                _comm_start(_ag_first(ins, outs, *sems))

        @pl.when(j == 0)
        def _():
            xf = x_ref[...]
            r = lax.rsqrt(jnp.mean(xf * xf, axis=-1, keepdims=True) + EPS)
            h = (xf * r * g_ref[...]).astype(BF16)
            hs[...] = h
            h_ref[...] = h

        @pl.when(j < njb)
        def _():
            proj_ref[...] = _dot_nt(hs[...], wbr_ref[...]).astype(BF16)

        @pl.when(j >= njb)
        def _():
            gates_ref[...] = _dot_nt(hs[...], wg_ref[...]).astype(BF16)

        if n:
            @pl.when(jnp.logical_and(i == ni - 1, j == nj - 1))
            def _():
                _comm_wait(_ag_first(ins, outs, *sems))

    def first(j):
        return jnp.minimum(j, njb - 1)

    def second(j):
        return jnp.maximum(j - njb, 0)

    res = pl.pallas_call(
        body, name="in_proj_gather" if n else "in_proj",
        grid=(ni, nj),
        in_specs=[pl.BlockSpec((tm, D_MODEL), lambda i, j: (i, 0)),
                  pl.BlockSpec((1, D_MODEL), lambda i, j: (0, 0)),
                  pl.BlockSpec((tnb, D_MODEL), lambda i, j: (first(j), 0)),
                  pl.BlockSpec((tng, D_MODEL), lambda i, j: (second(j), 0))]
                 + [ANY] * n,
        out_specs=[pl.BlockSpec((tm, tnb), lambda i, j: (i, first(j))),
                   pl.BlockSpec((tm, tng), lambda i, j: (i, second(j))),
                   pl.BlockSpec((tm, D_MODEL), lambda i, j: (i, 0))] + [ANY] * n,
        out_shape=[jax.ShapeDtypeStruct((S, D_BRANCHES), BF16), jax.ShapeDtypeStruct((S, D_GATES), BF16),
                   jax.ShapeDtypeStruct((S, D_MODEL), BF16)]
                  + [jax.ShapeDtypeStruct((N_DEV,) + s.shape, s.dtype) for s in shards],
        scratch_shapes=[pltpu.VMEM((tm, D_MODEL), BF16)] + (_dma_sems(4 * n, 4 * n, n) if n else []),
        compiler_params=_params(("arbitrary", "arbitrary")),
    )(x, g, wt, wgt, *shards)
    return res[0], res[1], res[2], list(res[3:])


def _mem_kv(mem, g, w):
    M = mem.shape[0]

    def body(m_ref, g_ref, w_ref, kv_ref, mn_ref):
        xf = m_ref[...]
        r = lax.rsqrt(jnp.mean(xf * xf, axis=-1, keepdims=True) + EPS)
        mn = (xf * r * g_ref[...]).astype(BF16)
        mn_ref[...] = mn
        kv_ref[...] = _dot(mn, w_ref[...]).astype(BF16)

    return pl.pallas_call(
        body, name="mem_kv",
        out_shape=[jax.ShapeDtypeStruct((M, 2 * WIDTH), BF16), jax.ShapeDtypeStruct((M, D_MODEL), BF16)],
        compiler_params=pltpu.CompilerParams(vmem_limit_bytes=VMEM_LIMIT),
    )(mem, g, w)


def _band_masks(win):
    t = lax.broadcasted_iota(jnp.int32, (CHUNK, CHUNK), 0)
    s = lax.broadcasted_iota(jnp.int32, (CHUNK, CHUNK), 1)
    cur = jnp.logical_and(t - s >= 0, t - s < win)
    prev = s > t + CHUNK - win
    return cur.astype(BF16), prev.astype(BF16)


def _inv_count(first_row, win):
    t = first_row + lax.broadcasted_iota(jnp.int32, (CHUNK, 1), 0)
    return 1.0 / jnp.minimum(t + 1, win).astype(F32)


def _layer_norm_fwd(v):
    mu = jnp.mean(v, axis=-1, keepdims=True)
    vc = v - mu
    var = jnp.mean(vc * vc, axis=-1, keepdims=True)
    rstd = lax.rsqrt(var + EPS)
    return vc * rstd, rstd


def _mem_softmax(q, kmem):
    s = _dot_nt(q, kmem) * ATT_SCALE
    m = jnp.max(s, axis=-1, keepdims=True)
    e = jnp.exp(s - m)
    return e * (1.0 / jnp.sum(e, axis=-1, keepdims=True))


def _abm_fwd(proj, ln_g, ln_b, wsm, bias_full, pool_w, pool_scale, kv, gathered=()):
    S = proj.shape[0]
    tm = 512
    nchunk = tm // CHUNK
    n = len(gathered)
    nsteps = S // tm

    def body(u_ref, v_ref, ag_ref, p_ref, ph_ref, pg_ref, mq_ref, mg_ref, lng_ref, lnb_ref, wsm_ref, bias_ref,
             pw_ref, ps_ref, kv_ref, *rest):
        y_ref, cdf_ref, bufs = rest[n], rest[n + 1], rest[n + 2:2 * n + 2]
        mix, sems = rest[2 * n + 2], rest[2 * n + 3:]
        i = pl.program_id(0)

        if n:
            @pl.when(i == 0)
            def _():
                _comm_start(_ag_second(bufs, *sems))

        au, av = u_ref[...].astype(F32), v_ref[...].astype(F32)
        cdf_u, cdf_v = _normal_cdf(au), _normal_cdf(av)
        cdf_ref[0] = cdf_u.astype(BF16)
        cdf_ref[1] = cdf_v.astype(BF16)
        u, v = au * cdf_u, av * cdf_v
        vhat, _ = _layer_norm_fwd(v)
        vln = (vhat * lng_ref[...] + lnb_ref[...]).astype(BF16)
        for c in range(nchunk):
            for h in range(N_HEAD):
                rs, cs = slice(c * CHUNK, (c + 1) * CHUNK), slice(h * HEAD, (h + 1) * HEAD)
                mix[rs, cs] = _dot(wsm_ref[h], vln[rs, cs]) + bias_ref[:, cs]
        y_ref[0] = (u * mix[...] * _silu(ag_ref[...].astype(F32))).astype(BF16)
        halo_ok = (i > 0).astype(F32)
        for c in range(nchunk):
            rs = slice(c * CHUNK, (c + 1) * CHUNK)
            for g, win in enumerate(POOL_WINDOWS):
                cs = slice(g * HEAD, (g + 1) * HEAD)
                bcur, bprev = _band_masks(win)
                cur = p_ref[rs, cs]
                if c == 0:
                    prev = (ph_ref[:, cs].astype(F32) * halo_ok).astype(BF16)
                else:
                    prev = p_ref[(c - 1) * CHUNK:c * CHUNK, cs]
                sums = _dot(bcur, cur) + _dot(bprev, prev)
                dm = sums * _inv_count(i * tm + c * CHUNK, win) - cur.astype(F32)
                mix[rs, cs] = _dot(dm.astype(BF16), pw_ref[g])
        y_ref[1] = (mix[...] * ps_ref[...] * _silu(pg_ref[...].astype(F32))).astype(BF16)
        for h in range(N_HEAD):
            cs = slice(h * HEAD, (h + 1) * HEAD)
            p = _mem_softmax(mq_ref[:, cs], kv_ref[:, cs])
            mix[:, cs] = _dot(p.astype(BF16), kv_ref[:, WIDTH + h * HEAD:WIDTH + (h + 1) * HEAD])
        y_ref[2] = (mix[...] * _silu(mg_ref[...].astype(F32))).astype(BF16)

        if n:
            @pl.when(i == nsteps - 1)
            def _():
                _comm_wait(_ag_second(bufs, *sems))

    blk = tm // CHUNK
    res = pl.pallas_call(
        body, name="abm_fwd_gather" if n else "abm_fwd",
        grid=(nsteps,),
        in_specs=[_rows(tm, WIDTH, CB_U), _rows(tm, WIDTH, CB_V), _rows(tm, WIDTH, CB_AGATE),
                  _rows(tm, WIDTH, CB_PIN),
                  pl.BlockSpec((CHUNK, WIDTH), lambda i: (jnp.maximum(i * blk - 1, 0), CB_PIN)),
                  _rows(tm, WIDTH, CB_PGATE), _rows(tm, WIDTH, CB_MQ), _rows(tm, WIDTH, CB_MGATE),
                  _full((1, WIDTH)), _full((1, WIDTH)), _full((N_HEAD, CHUNK, CHUNK)), _full((CHUNK, WIDTH)),
                  _full((4, HEAD, HEAD)), _full((1, WIDTH)), _full((MEM_LEN, 2 * WIDTH))] + [ANY] * n,
        out_specs=[pl.BlockSpec((3, tm, WIDTH), lambda i: (0, i, 0)), pl.BlockSpec((2, tm, WIDTH), lambda i: (0, i, 0))]
                  + [ANY] * n,
        out_shape=[jax.ShapeDtypeStruct((4, S, WIDTH), BF16),
                   jax.ShapeDtypeStruct((2, S, WIDTH), BF16)]
                  + [jax.ShapeDtypeStruct(b.shape, b.dtype) for b in gathered],
        input_output_aliases={15 + a: 2 + a for a in range(n)},
        scratch_shapes=[pltpu.VMEM((tm, WIDTH), F32)] + (_dma_sems(3 * n, 3 * n) if n else []),
        compiler_params=_params(("arbitrary",)),
    )(proj, proj, proj, proj, proj, proj, proj, proj, ln_g, ln_b, wsm, bias_full, pool_w, pool_scale, kv, *gathered)
    return res[0], res[1], list(res[2:])


ATT_TILE = 512


def _attn_fwd(q, qcb, k, kcb, v, vcb, bps):
    S = q.shape[0]
    tm = ATT_TILE
    nb = tm // CHUNK

    nblocks = nb * N_HEAD

    def body(q_ref, k_ref, v_ref, kh_ref, vh_ref, o_ref, l_ref, sc_s, sp_s, pc_s, pp_s):
        i = pl.program_id(0)

        def prev_kv(n, cs):
            if n == 0:
                return kh_ref[:, cs], vh_ref[:, cs]
            ps = slice((n - 1) * CHUNK, n * CHUNK)
            return k_ref[ps, cs], v_ref[ps, cs]

        pens = []
        for n in range(nb):
            rs = slice(n * CHUNK, (n + 1) * CHUNK)
            pens.append(jnp.full((N_HEAD * CHUNK, 1), jnp.where((i * nb + n) % bps != 0, 0.0, NEG), F32))
            for h in range(N_HEAD):
                cs = slice(h * HEAD, (h + 1) * HEAD)
                bs = slice((n * N_HEAD + h) * CHUNK, (n * N_HEAD + h + 1) * CHUNK)
                qh = q_ref[rs, cs]
                sc_s[bs, :] = _dot_nt(qh, k_ref[rs, cs])
                sp_s[bs, :] = _dot_nt(qh, prev_kv(n, cs)[0])
        row = lax.broadcasted_iota(jnp.int32, (nblocks * CHUNK, CHUNK), 0) & (CHUNK - 1)
        col = lax.broadcasted_iota(jnp.int32, (nblocks * CHUNK, CHUNK), 1)
        sc = jnp.where(col <= row, sc_s[...] * ATT_SCALE, NEG)
        sp = jnp.where(col >= row, sp_s[...] * ATT_SCALE, NEG) + jnp.concatenate(pens, axis=0)
        m = jnp.maximum(jnp.max(sc, axis=-1, keepdims=True), jnp.max(sp, axis=-1, keepdims=True))
        ec = jnp.exp(sc - m)
        ep = jnp.exp(sp - m)
        den = jnp.sum(ec, axis=-1, keepdims=True) + jnp.sum(ep, axis=-1, keepdims=True)
        inv = 1.0 / den
        pc_s[...] = (ec * inv).astype(BF16)
        pp_s[...] = (ep * inv).astype(BF16)
        lse = m + jnp.log(den)
        for n in range(nb):
            rs = slice(n * CHUNK, (n + 1) * CHUNK)
            for h in range(N_HEAD):
                cs = slice(h * HEAD, (h + 1) * HEAD)
                bs = slice((n * N_HEAD + h) * CHUNK, (n * N_HEAD + h + 1) * CHUNK)
                o = _dot(pc_s[bs, :], v_ref[rs, cs]) + _dot(pp_s[bs, :], prev_kv(n, cs)[1])
                o_ref[rs, cs] = o.astype(BF16)
            l_ref[rs, :] = _put_cols([lse[(n * N_HEAD + h) * CHUNK:(n * N_HEAD + h + 1) * CHUNK]
                                      for h in range(N_HEAD)])

    def halo(cb):
        return pl.BlockSpec((CHUNK, WIDTH), lambda i: (jnp.maximum(i * nb - 1, 0), cb))

    return pl.pallas_call(
        body, name=f"attn_fwd_{bps}",
        grid=(S // tm,),
        in_specs=[_rows(tm, WIDTH, qcb), _rows(tm, WIDTH, kcb), _rows(tm, WIDTH, vcb), halo(kcb), halo(vcb)],
        out_specs=[_rows(tm, WIDTH), _rows(tm, 128)],
        out_shape=[jax.ShapeDtypeStruct((S, WIDTH), BF16), jax.ShapeDtypeStruct((S, 128), F32)],
        scratch_shapes=[pltpu.VMEM((nblocks * CHUNK, CHUNK), F32), pltpu.VMEM((nblocks * CHUNK, CHUNK), F32),
                        pltpu.VMEM((nblocks * CHUNK, CHUNK), BF16), pltpu.VMEM((nblocks * CHUNK, CHUNK), BF16)],
        compiler_params=_params(("parallel",)),
    )(q, k, v, k, v)


def _gate_specs(tm):
    return [pl.BlockSpec((tm, D_MODEL), lambda i, b=b: (i, b)) for b in range(4)]


Y_SLOT = (0, 1, 3, 2)


def _merge_fwd(x, y4, o_g, l_g, proj, gates, wb, wout, shards=()):
    S = x.shape[0]
    tm = 256
    n = len(shards)
    nsteps = S // tm
    forward_at = nsteps - 2

    def body(x_ref, y_ref, o0, o1, o2, l0, l1, l2, cg_ref, *rest):
        gm = rest[:4]
        wb_ref, wo_ref = rest[4:6]
        s_in = rest[6:6 + n]
        xn_ref, yc_ref, oc_ref, lse_ref, z_ref = rest[6 + n:11 + n]
        s_out, ocs, sems = rest[11 + n:11 + 2 * n], rest[11 + 2 * n], rest[12 + 2 * n:]
        i = pl.program_id(0)

        if n:
            @pl.when(i == 0)
            def _():
                _comm_start(_ag_first(s_in, s_out, *sems[:3]))

            @pl.when(i == forward_at)
            def _():
                incoming = _ag_first(s_in, s_out, *sems[:3])[2]
                for a in range(n):
                    for k in range(1, 4):
                        incoming[4 * a + k].wait_recv()
                _comm_start(_ag_second(s_out, *sems[3:]))

        lcols = []
        for h in range(N_HEAD):
            cs = slice(h * HEAD, (h + 1) * HEAD)
            ls = [_col(l[...], h) for l in (l0, l1, l2)]
            m = jnp.maximum(jnp.maximum(ls[0], ls[1]), ls[2])
            tot = jnp.exp(ls[0] - m) + jnp.exp(ls[1] - m) + jnp.exp(ls[2] - m)
            lse = m + jnp.log(tot)
            ocs[:, cs] = sum(jnp.exp(lg - lse) * o[:, cs].astype(F32) for lg, o in zip(ls, (o0, o1, o2)))
            lcols.append(lse)
        lse_ref[...] = _put_cols(lcols)
        oc = ocs[...]
        oc_ref[...] = oc.astype(BF16)
        yc = (oc * _silu(cg_ref[...].astype(F32))).astype(BF16)
        yc_ref[...] = yc
        ys = (y_ref[0], y_ref[1], yc, y_ref[2])
        z = jnp.zeros((tm, D_MODEL), F32)
        for b in range(4):
            z = z + _sigmoid(gm[b][...].astype(F32)) * _dot(ys[b], wb_ref[b])
        zb = z.astype(BF16)
        z_ref[...] = zb
        xn_ref[...] = x_ref[...] + _dot(zb, wo_ref[...])

        if n:
            @pl.when(i == nsteps - 1)
            def _():
                local, out, incoming = _ag_first(s_in, s_out, *sems[:3])
                for a in range(n):
                    incoming[4 * a].wait_recv()
                _comm_wait(_ag_second(s_out, *sems[3:]))
                for cp in out:
                    cp.wait_send()
                for cp in local:
                    cp.wait()

    res = pl.pallas_call(
        body, name="merge_fwd_gather" if n else "merge_fwd",
        grid=(nsteps,),
        in_specs=[_rows(tm, D_MODEL), pl.BlockSpec((3, tm, WIDTH), lambda i: (0, i, 0)),
                  _rows(tm, WIDTH), _rows(tm, WIDTH), _rows(tm, WIDTH),
                  _rows(tm, 128), _rows(tm, 128), _rows(tm, 128),
                  _rows(tm, WIDTH, CB_CGATE)] + _gate_specs(tm)
                 + [_full((4, WIDTH, D_MODEL)), _full((D_MODEL, D_MODEL))] + [ANY] * n,
        out_specs=[_rows(tm, D_MODEL), pl.BlockSpec((None, tm, WIDTH), lambda i: (Y_SLOT[2], i, 0)),
                   _rows(tm, WIDTH), _rows(tm, 128), _rows(tm, D_MODEL)] + [ANY] * n,
        out_shape=[jax.ShapeDtypeStruct((S, D_MODEL), F32), jax.ShapeDtypeStruct(y4.shape, BF16),
                   jax.ShapeDtypeStruct((S, WIDTH), BF16), jax.ShapeDtypeStruct((S, 128), F32),
                   jax.ShapeDtypeStruct((S, D_MODEL), BF16)]
                  + [jax.ShapeDtypeStruct((N_DEV,) + s.shape, s.dtype) for s in shards],
        input_output_aliases={1: 1},
        scratch_shapes=[pltpu.VMEM((tm, WIDTH), F32)] + (_dma_sems(4 * n, 4 * n, n, 3 * n, 3 * n) if n else []),
        compiler_params=_params(("arbitrary",)),
    )(x, y4, *o_g, *l_g, proj, *([gates] * 4), wb, wout, *shards)
    return res[:5], list(res[5:])


def _loss_head(x, g, tgt):
    S = x.shape[0]
    tm = 512

    def body(x_ref, g_ref, t_ref, loss_ref, dx_ref, dg_ref):
        @pl.when(pl.program_id(0) == 0)
        def _():
            loss_ref[...] = jnp.zeros_like(loss_ref)
            dg_ref[...] = jnp.zeros_like(dg_ref)

        xf = x_ref[...]
        r = lax.rsqrt(jnp.mean(xf * xf, axis=-1, keepdims=True) + EPS)
        xhat = xf * r
        gv = g_ref[...]
        err = xhat * gv - t_ref[...]
        e2 = jnp.sum(err * err, axis=-1, keepdims=True)
        loss_ref[...] += (0.5 / D_MODEL) * jnp.sum(e2, axis=0, keepdims=True)
        dy = err * (1.0 / D_MODEL)
        dg_ref[...] += jnp.sum(dy * xhat, axis=0, keepdims=True)
        dxh = dy * gv
        dx_ref[...] = r * (dxh - xhat * jnp.mean(dxh * xhat, axis=-1, keepdims=True))

    return pl.pallas_call(
        body, name="loss_head",
        grid=(S // tm,),
        in_specs=[_rows(tm, D_MODEL), _full((1, D_MODEL)), _rows(tm, D_MODEL)],
        out_specs=[_full((1, 128)), _rows(tm, D_MODEL), _full((1, D_MODEL))],
        out_shape=[jax.ShapeDtypeStruct((1, 128), F32), jax.ShapeDtypeStruct((S, D_MODEL), F32),
                   jax.ShapeDtypeStruct((1, D_MODEL), F32)],
        compiler_params=_params(("arbitrary",)),
    )(x, g, tgt)


def _merge_bwd(dxo, y4, oc, z, proj, gates, wb, wout, grads=()):
    S = dxo.shape[0]
    tm = 256
    n = len(grads)
    nsteps = S // tm

    def body(dx_ref, y_ref, oc_ref, z_ref, cg_ref, *rest):
        gm = rest[:4]
        wb_ref, wo_ref = rest[4:6]
        g_in = rest[6:6 + n]
        dy_ref, doc_ref, delta_ref, dcg_ref, dgm_ref, dwb_ref, dwo_ref = rest[6 + n:13 + n]
        g_out = rest[13 + n:13 + 2 * n]
        acc_b, acc_o = rest[13 + 2 * n:15 + 2 * n]
        sems = rest[15 + 2 * n:]
        i = pl.program_id(0)

        @pl.when(i == 0)
        def _():
            acc_b[...] = jnp.zeros_like(acc_b)
            acc_o[...] = jnp.zeros_like(acc_o)
            if n:
                _comm_start(_rs_first(g_in, g_out, *sems))

        dxb = dx_ref[...].astype(BF16)
        acc_o[...] += _dot_tn(z_ref[...], dxb)
        dz = _dot_nt(dxb, wo_ref[...])
        for b in range(4):
            gate = _sigmoid(gm[b][...].astype(F32))
            yb = y_ref[Y_SLOT[b]]
            t = _dot(yb, wb_ref[b])
            dgm_ref[:, b * D_MODEL:(b + 1) * D_MODEL] = (dz * t * gate * (1.0 - gate)).astype(BF16)
            dt = (dz * gate).astype(BF16)
            acc_b[b] += _dot_tn(yb, dt)
            dyb = _dot_nt(dt, wb_ref[b])
            if b == 2:
                cg = cg_ref[...].astype(F32)
                oc = oc_ref[...].astype(F32)
                scg, dscg = _silu_and_grad(cg)
                doc = dyb * scg
                dcg_ref[...] = (dyb * oc * dscg).astype(BF16)
                doc_ref[...] = doc.astype(BF16)
                prod = doc * oc
                delta_ref[...] = _put_cols([jnp.sum(prod[:, h * HEAD:(h + 1) * HEAD], axis=1, keepdims=True)
                                            for h in range(N_HEAD)])
            else:
                dy_ref[b if b < 2 else 2] = dyb.astype(BF16)

        @pl.when(i == nsteps - 1)
        def _():
            dwb_ref[...] = acc_b[...].astype(BF16)
            dwo_ref[...] = acc_o[...].astype(BF16)
            if n:
                _comm_wait(_rs_first(g_in, g_out, *sems))

    def resident(shape):
        nd = len(shape)
        return pl.BlockSpec(shape, lambda i: (0,) * nd, pipeline_mode=pl.Buffered(1))

    res = pl.pallas_call(
        body, name="merge_bwd_scatter" if n else "merge_bwd",
        grid=(nsteps,),
        in_specs=[_rows(tm, D_MODEL), pl.BlockSpec((4, tm, WIDTH), lambda i: (0, i, 0)),
                  _rows(tm, WIDTH), _rows(tm, D_MODEL), _rows(tm, WIDTH, CB_CGATE)] + _gate_specs(tm)
                 + [resident((4, WIDTH, D_MODEL)), resident((D_MODEL, D_MODEL))] + [ANY] * n,
        out_specs=[pl.BlockSpec((3, tm, WIDTH), lambda i: (0, i, 0)), _rows(tm, WIDTH), _rows(tm, 128),
                   _rows(tm, WIDTH), _rows(tm, 4 * D_MODEL), _full((4, WIDTH, D_MODEL)), _full((D_MODEL, D_MODEL))]
                  + [ANY] * n,
        out_shape=[jax.ShapeDtypeStruct((3, S, WIDTH), BF16), jax.ShapeDtypeStruct((S, WIDTH), BF16),
                   jax.ShapeDtypeStruct((S, 128), F32), jax.ShapeDtypeStruct((S, WIDTH), BF16),
                   jax.ShapeDtypeStruct((S, 4 * D_MODEL), BF16), jax.ShapeDtypeStruct((4, WIDTH, D_MODEL), BF16),
                   jax.ShapeDtypeStruct((D_MODEL, D_MODEL), BF16)]
                  + [jax.ShapeDtypeStruct(g.shape[:1] + g.shape[2:], g.dtype) for g in grads],
        scratch_shapes=[pltpu.VMEM((4, WIDTH, D_MODEL), F32), pltpu.VMEM((D_MODEL, D_MODEL), F32)]
                       + (_dma_sems(N_CHIP * n, N_CHIP * n) if n else []),
        compiler_params=_params(("arbitrary",)),
    )(dxo, y4, oc, z, proj, *([gates] * 4), wb, wout, *grads)
    return res[:7], list(res[7:])


def _attn_bwd(q, qcb, k, kcb, v, vcb, do, lse, delta, bps, dst, dcb):
    S = q.shape[0]
    tm = ATT_TILE
    nb = tm // CHUNK
    nblk = S // CHUNK

    ncur = nb * N_HEAD
    nprev = (nb + 1) * N_HEAD

    def body(q_ref, k_ref, v_ref, do_ref, l_ref, d_ref, kh_ref, vh_ref, qn_ref, don_ref, ln_ref, dn_ref, dst_ref,
             dq_ref, dk_ref, dv_ref, sc_s, sp_s, dpc_s, dpp_s, pc_s, pp_s, dsc_s, dsp_s):
        i = pl.program_id(0)

        def rows_of(n):
            if n < nb:
                rs = slice(n * CHUNK, (n + 1) * CHUNK)
                return rs, q_ref, do_ref, l_ref, d_ref
            return slice(0, CHUNK), qn_ref, don_ref, ln_ref, dn_ref

        def prev_kv(n, cs):
            if n == 0:
                return kh_ref[:, cs], vh_ref[:, cs]
            ps = slice((n - 1) * CHUNK, n * CHUNK)
            return k_ref[ps, cs], v_ref[ps, cs]

        def blk(n, h):
            return slice((n * N_HEAD + h) * CHUNK, (n * N_HEAD + h + 1) * CHUNK)

        pens, lses, deltas = [], [], []
        for n in range(nb + 1):
            rs, qr, dor, lr, dr = rows_of(n)
            gb = i * nb + n
            pen = jnp.where(gb % bps != 0, 0.0, NEG)
            if n == nb:
                pen = pen + jnp.where(gb < nblk, 0.0, NEG)
            pens.append(jnp.full((N_HEAD * CHUNK, 1), pen, F32))
            lblk, dblk = lr[rs, :], dr[rs, :]
            for h in range(N_HEAD):
                cs = slice(h * HEAD, (h + 1) * HEAD)
                qh, doh = qr[rs, cs], dor[rs, cs]
                lses.append(_col(lblk, h))
                deltas.append(_col(dblk, h))
                kp, vp = prev_kv(n, cs)
                sp_s[blk(n, h), :] = _dot_nt(qh, kp)
                dpp_s[blk(n, h), :] = _dot_nt(doh, vp)
                if n < nb:
                    sc_s[blk(n, h), :] = _dot_nt(qh, k_ref[rs, cs])
                    dpc_s[blk(n, h), :] = _dot_nt(doh, v_ref[rs, cs])
        lse = jnp.concatenate(lses, axis=0)
        delta = jnp.concatenate(deltas, axis=0)
        row = lax.broadcasted_iota(jnp.int32, (nprev * CHUNK, CHUNK), 0) & (CHUNK - 1)
        col = lax.broadcasted_iota(jnp.int32, (nprev * CHUNK, CHUNK), 1)
        sp = jnp.where(col >= row, sp_s[...] * ATT_SCALE, NEG) + jnp.concatenate(pens, axis=0)
        pp = jnp.exp(sp - lse)
        pp_s[...] = pp.astype(BF16)
        dsp_s[...] = (pp * (dpp_s[...] - delta)).astype(BF16)
        nc = ncur * CHUNK
        sc = jnp.where(col[:nc] <= row[:nc], sc_s[...] * ATT_SCALE, NEG)
        pc = jnp.exp(sc - lse[:nc])
        pc_s[...] = pc.astype(BF16)
        dsc_s[...] = (pc * (dpc_s[...] - delta[:nc])).astype(BF16)
        for n in range(nb):
            rs, qr, dor, _, _ = rows_of(n)
            rn, qnr, donr, _, _ = rows_of(n + 1)
            for h in range(N_HEAD):
                cs = slice(h * HEAD, (h + 1) * HEAD)
                kp, _ = prev_kv(n, cs)
                dq = _dot(dsc_s[blk(n, h), :], k_ref[rs, cs]) + _dot(dsp_s[blk(n, h), :], kp)
                dq_ref[rs, cs] = (dq * ATT_SCALE).astype(BF16)
                dk = _dot_tn(dsc_s[blk(n, h), :], qr[rs, cs]) + _dot_tn(dsp_s[blk(n + 1, h), :], qnr[rn, cs])
                dk_ref[rs, cs] = (dk * ATT_SCALE).astype(BF16)
                dv = _dot_tn(pc_s[blk(n, h), :], dor[rs, cs]) + _dot_tn(pp_s[blk(n + 1, h), :], donr[rn, cs])
                dv_ref[rs, cs] = dv.astype(BF16)

    def prev_halo(cb):
        return pl.BlockSpec((CHUNK, WIDTH), lambda i: (jnp.maximum(i * nb - 1, 0), cb))

    def next_halo(width, cb=0):
        return pl.BlockSpec((CHUNK, width), lambda i: (jnp.minimum(i * nb + nb, nblk - 1), cb))

    return pl.pallas_call(
        body, name=f"attn_bwd_{bps}",
        grid=(S // tm,),
        in_specs=[_rows(tm, WIDTH, qcb), _rows(tm, WIDTH, kcb), _rows(tm, WIDTH, vcb), _rows(tm, WIDTH),
                  _rows(tm, 128), _rows(tm, 128), prev_halo(kcb), prev_halo(vcb),
                  next_halo(WIDTH, qcb), next_halo(WIDTH), next_halo(128), next_halo(128), ANY],
        out_specs=[_rows(tm, WIDTH, dcb), _rows(tm, WIDTH), _rows(tm, WIDTH)],
        out_shape=[jax.ShapeDtypeStruct(dst.shape, BF16)] + [jax.ShapeDtypeStruct((S, WIDTH), BF16)] * 2,
        input_output_aliases={12: 0},
        scratch_shapes=[pltpu.VMEM((ncur * CHUNK, CHUNK), F32), pltpu.VMEM((nprev * CHUNK, CHUNK), F32),
                        pltpu.VMEM((ncur * CHUNK, CHUNK), F32), pltpu.VMEM((nprev * CHUNK, CHUNK), F32),
                        pltpu.VMEM((ncur * CHUNK, CHUNK), BF16), pltpu.VMEM((nprev * CHUNK, CHUNK), BF16),
                        pltpu.VMEM((ncur * CHUNK, CHUNK), BF16), pltpu.VMEM((nprev * CHUNK, CHUNK), BF16)],
        compiler_params=_params(("parallel",)),
    )(q, k, v, do, lse, delta, k, v, q, do, lse, delta, dst)


def _dilated_split(d):
    hp = min(N_HEAD, 16 // d)
    return hp, N_HEAD // hp, HEAD * hp


def _strided_regroup(d):
    return d < 16


def _by_class(src_ref, dst, d, hp, nat):
    for j in range(hp):
        if _strided_regroup(d):
            nat[j] = src_ref[:, j * HEAD:(j + 1) * HEAD].astype(F32)
            for r in range(d):
                dst[j, r * CHUNK:(r + 1) * CHUNK, :] = nat.at[j][pl.ds(r, CHUNK, stride=d), :].astype(BF16)
        else:
            dst[j] = pltpu.einshape("(tr)l->(rt)l", src_ref[:, j * HEAD:(j + 1) * HEAD], r=d)


def _from_class(src, dst_ref, d, hp, nat, add_ref=None):
    for j in range(hp):
        cs = slice(j * HEAD, (j + 1) * HEAD)
        if _strided_regroup(d):
            for r in range(d):
                nat.at[j][pl.ds(r, CHUNK, stride=d), :] = src[j, r * CHUNK:(r + 1) * CHUNK, :]
            val = nat[j].astype(BF16)
        else:
            val = pltpu.einshape("(rt)l->(tr)l", src[j].astype(BF16), r=d)
        if add_ref is not None:
            val = (val.astype(F32) + add_ref[:, cs].astype(F32)).astype(BF16)
        dst_ref[:, cs] = val


def _attn_fwd_dilated(proj, qcb, kcb, vcb, d):
    S = proj.shape[0]
    T = CHUNK * d
    hp, nh, cw = _dilated_split(d)
    nblocks = d * hp

    def body(q_ref, k_ref, v_ref, o_ref, l_ref, qf, kst, vst, of, lf, nat, sc_s, sp_s, pc_s, pp_s):
        i, hh = pl.program_id(0), pl.program_id(1)
        kf, vf = kst.at[i % 2, hh], vst.at[i % 2, hh]
        kpf, vpf = kst.at[1 - i % 2, hh], vst.at[1 - i % 2, hh]

        @pl.when(i == 0)
        def _():
            kpf[...] = jnp.zeros_like(kpf)
            vpf[...] = jnp.zeros_like(vpf)

        _by_class(q_ref, qf, d, hp, nat)
        _by_class(k_ref, kf, d, hp, nat)
        _by_class(v_ref, vf, d, hp, nat)

        def blk(ref, r, j):
            return ref[j, r * CHUNK:(r + 1) * CHUNK, :]

        def bs(r, j):
            return slice((r * hp + j) * CHUNK, (r * hp + j + 1) * CHUNK)

        for r in range(d):
            for j in range(hp):
                qb = blk(qf, r, j)
                sc_s[bs(r, j), :] = _dot_nt(qb, blk(kf, r, j))
                sp_s[bs(r, j), :] = _dot_nt(qb, blk(kpf, r, j))
        row = lax.broadcasted_iota(jnp.int32, (nblocks * CHUNK, CHUNK), 0) & (CHUNK - 1)
        col = lax.broadcasted_iota(jnp.int32, (nblocks * CHUNK, CHUNK), 1)
        sc = jnp.where(col <= row, sc_s[...] * ATT_SCALE, NEG)
        sp = jnp.where(col >= row, sp_s[...] * ATT_SCALE, NEG) + jnp.where(i > 0, 0.0, NEG)
        m = jnp.maximum(jnp.max(sc, axis=-1, keepdims=True), jnp.max(sp, axis=-1, keepdims=True))
        ec = jnp.exp(sc - m)
        ep = jnp.exp(sp - m)
        den = jnp.sum(ec, axis=-1, keepdims=True) + jnp.sum(ep, axis=-1, keepdims=True)
        inv = 1.0 / den
        pc_s[...] = (ec * inv).astype(BF16)
        pp_s[...] = (ep * inv).astype(BF16)
        lse = m + jnp.log(den)
        lane = lax.broadcasted_iota(jnp.int32, (CHUNK, 128), 1)
        for r in range(d):
            lblk = jnp.zeros((CHUNK, 128), F32)
            for j in range(hp):
                o = _dot(pc_s[bs(r, j), :], blk(vf, r, j)) + _dot(pp_s[bs(r, j), :], blk(vpf, r, j))
                of[j, r * CHUNK:(r + 1) * CHUNK, :] = o
                lblk = jnp.where(lane == hh * hp + j, lse[bs(r, j)], lblk)
            lf[r * CHUNK:(r + 1) * CHUNK, :] = lblk
        _from_class(of, o_ref, d, hp, nat)
        lnat = pltpu.einshape("(rt)l->(tr)l", lf[...], r=d)

        @pl.when(hh == 0)
        def _():
            l_ref[...] = lnat

        @pl.when(hh > 0)
        def _():
            l_ref[...] += lnat

    def cols(cb):
        return pl.BlockSpec((T, cw), lambda i, hh: (i, cb * nh + hh))

    tile = pltpu.VMEM((hp, T, HEAD), BF16)
    staging = pltpu.VMEM((hp, T, HEAD) if _strided_regroup(d) else (1, 8, HEAD), F32)
    return pl.pallas_call(
        body, name=f"attn_fwd_dilated_{d}",
        grid=(S // T, nh),
        in_specs=[cols(qcb), cols(kcb), cols(vcb)],
        out_specs=[cols(0), pl.BlockSpec((T, 128), lambda i, hh: (i, 0))],
        out_shape=[jax.ShapeDtypeStruct((S, WIDTH), BF16), jax.ShapeDtypeStruct((S, 128), F32)],
        scratch_shapes=[tile, pltpu.VMEM((2, nh, hp, T, HEAD), BF16), pltpu.VMEM((2, nh, hp, T, HEAD), BF16),
                        pltpu.VMEM((hp, T, HEAD), F32), pltpu.VMEM((T, 128), F32), staging,
                        pltpu.VMEM((nblocks * CHUNK, CHUNK), F32), pltpu.VMEM((nblocks * CHUNK, CHUNK), F32),
                        pltpu.VMEM((nblocks * CHUNK, CHUNK), BF16), pltpu.VMEM((nblocks * CHUNK, CHUNK), BF16)],
        compiler_params=_params(("arbitrary", "arbitrary")),
    )(proj, proj, proj)


def _attn_bwd_dilated(proj, qcb, kcb, vcb, do, lse, delta, d, dk_in, dv_in, dst, dcb):
    S = proj.shape[0]
    T = CHUNK * d
    nt = S // T
    hp, nh, cw = _dilated_split(d)
    nblocks = d * hp

    def body(q_ref, k_ref, v_ref, do_ref, l_ref, d_ref, dki_ref, dvi_ref, dst_ref, dq_ref, dk_ref, dv_ref,
             qf, dof, kbuf, vbuf, dqf, gk, gv, nat,
             sc_s, sp_s, dpc_s, dpp_s, pc_s, pp_s, dsc_s, dsp_s):
        hh, i = pl.program_id(0), pl.program_id(1)
        kf, vf, newk, newv = kbuf.at[i % 2], vbuf.at[i % 2], gk.at[i % 2], gv.at[i % 2]
        kpf, vpf, acck, accv = kbuf.at[1 - i % 2], vbuf.at[1 - i % 2], gk.at[1 - i % 2], gv.at[1 - i % 2]

        @pl.when(i == 0)
        def _():
            for ref in (kbuf, vbuf, gk, gv):
                ref[...] = jnp.zeros_like(ref)
            dk_ref[...] = jnp.zeros_like(dk_ref)
            dv_ref[...] = jnp.zeros_like(dv_ref)

        def blk(ref, r, j):
            return ref[j, r * CHUNK:(r + 1) * CHUNK, :]

        def bs(r, j):
            return slice((r * hp + j) * CHUNK, (r * hp + j + 1) * CHUNK)

        @pl.when(i < nt)
        def _():
            _by_class(q_ref, qf, d, hp, nat)
            _by_class(do_ref, dof, d, hp, nat)
            _by_class(k_ref, kf, d, hp, nat)
            _by_class(v_ref, vf, d, hp, nat)
            lses, deltas = [], []
            lcls = pltpu.einshape("(tr)l->(rt)l", l_ref[...], r=d)
            dcls = pltpu.einshape("(tr)l->(rt)l", d_ref[...], r=d)
            for r in range(d):
                lblk = lcls[r * CHUNK:(r + 1) * CHUNK]
                dblk = dcls[r * CHUNK:(r + 1) * CHUNK]
                for j in range(hp):
                    lses.append(_col(lblk, hh * hp + j))
                    deltas.append(_col(dblk, hh * hp + j))
                    qb, dob = blk(qf, r, j), blk(dof, r, j)
                    sc_s[bs(r, j), :] = _dot_nt(qb, blk(kf, r, j))
                    dpc_s[bs(r, j), :] = _dot_nt(dob, blk(vf, r, j))
                    sp_s[bs(r, j), :] = _dot_nt(qb, blk(kpf, r, j))
                    dpp_s[bs(r, j), :] = _dot_nt(dob, blk(vpf, r, j))
            lse = jnp.concatenate(lses, axis=0)
            delta = jnp.concatenate(deltas, axis=0)
            row = lax.broadcasted_iota(jnp.int32, (nblocks * CHUNK, CHUNK), 0) & (CHUNK - 1)
            col = lax.broadcasted_iota(jnp.int32, (nblocks * CHUNK, CHUNK), 1)
            sp = jnp.where(col >= row, sp_s[...] * ATT_SCALE, NEG) + jnp.where(i > 0, 0.0, NEG)
            pp = jnp.exp(sp - lse)
            pp_s[...] = pp.astype(BF16)
            dsp_s[...] = (pp * (dpp_s[...] - delta)).astype(BF16)
            sc = jnp.where(col <= row, sc_s[...] * ATT_SCALE, NEG)
            pc = jnp.exp(sc - lse)
            pc_s[...] = pc.astype(BF16)
            dsc_s[...] = (pc * (dpc_s[...] - delta)).astype(BF16)
            for r in range(d):
                rows = slice(r * CHUNK, (r + 1) * CHUNK)
                for j in range(hp):
                    qb, dob = blk(qf, r, j), blk(dof, r, j)
                    dsc, dsp = dsc_s[bs(r, j), :], dsp_s[bs(r, j), :]
                    dqf[j, rows, :] = (_dot(dsc, blk(kf, r, j)) + _dot(dsp, blk(kpf, r, j))) * ATT_SCALE
                    newk[j, rows, :] = _dot_tn(dsc, qb) * ATT_SCALE
                    newv[j, rows, :] = _dot_tn(pc_s[bs(r, j), :], dob)
                    acck[j, rows, :] += _dot_tn(dsp, qb) * ATT_SCALE
                    accv[j, rows, :] += _dot_tn(pp_s[bs(r, j), :], dob)
            _from_class(dqf, dq_ref, d, hp, nat)

        @pl.when(i > 0)
        def _():
            _from_class(acck, dk_ref, d, hp, nat, dki_ref)
            _from_class(accv, dv_ref, d, hp, nat, dvi_ref)

    def cur(width, cb, nsplit):
        return pl.BlockSpec((T, width), lambda hh, i: (jnp.minimum(i, nt - 1), cb * nsplit + hh * (nsplit > 1)))

    def lag():
        return pl.BlockSpec((T, cw), lambda hh, i: (jnp.maximum(i - 1, 0), hh))

    tile = pltpu.VMEM((hp, T, HEAD), BF16)
    acc = pltpu.VMEM((hp, T, HEAD), F32)
    f32s = pltpu.VMEM((nblocks * CHUNK, CHUNK), F32)
    b16s = pltpu.VMEM((nblocks * CHUNK, CHUNK), BF16)
    return pl.pallas_call(
        body, name=f"attn_bwd_dilated_{d}",
        grid=(nh, nt + 1),
        in_specs=[cur(cw, qcb, nh), cur(cw, kcb, nh), cur(cw, vcb, nh), cur(cw, 0, nh), cur(128, 0, 1), cur(128, 0, 1),
                  lag(), lag(), ANY],
        out_specs=[cur(cw, dcb, nh), lag(), lag()],
        out_shape=[jax.ShapeDtypeStruct(dst.shape, BF16)] + [jax.ShapeDtypeStruct((S, WIDTH), BF16)] * 2,
        input_output_aliases={8: 0},
        scratch_shapes=[tile, tile, pltpu.VMEM((2, hp, T, HEAD), BF16), pltpu.VMEM((2, hp, T, HEAD), BF16), acc,
                        pltpu.VMEM((2, hp, T, HEAD), F32), pltpu.VMEM((2, hp, T, HEAD), F32),
                        acc if _strided_regroup(d) else pltpu.VMEM((1, 8, HEAD), F32)]
                       + [f32s] * 4 + [b16s] * 4,
        compiler_params=_params(("arbitrary", "arbitrary")),
    )(proj, proj, proj, do, lse, delta, dk_in, dv_in, dst)


def _abm_bwd(proj, cdf, dy3, ln_g, ln_b, wsm, wsm_t, bias_full, pool_w, pool_wt, pool_scale, kv):
    S = proj.shape[0]
    tm = 512
    nchunk = tm // CHUNK
    nblk = S // CHUNK

    def body(u_ref, v_ref, ag_ref, p_ref, ph_ref, pg_ref, pgn_ref, mq_ref, mg_ref, cdf_ref, dy_ref, dypn_ref,
             lng_ref, lnb_ref, wsm_ref, wsmt_ref, bias_ref, pw_ref, pwt_ref, ps_ref, kv_ref,
             dab_ref, dm_ref, dlng_ref, dlnb_ref, dws_ref, dbias_ref, dpw_ref, dps_ref, dkv_ref,
             mix, dvl, ddn):
        i = pl.program_id(0)

        @pl.when(i == 0)
        def _():
            for r in (dlng_ref, dlnb_ref, dws_ref, dbias_ref, dpw_ref, dps_ref, dkv_ref):
                r[...] = jnp.zeros_like(r)

        au = u_ref[...].astype(F32)
        av = v_ref[...].astype(F32)
        ag = ag_ref[...].astype(F32)
        u, du = _gelu_and_grad(au, cdf_ref[0].astype(F32))
        v, dgelu_v = _gelu_and_grad(av, cdf_ref[1].astype(F32))
        vhat, rstd = _layer_norm_fwd(v)
        vln = (vhat * lng_ref[...] + lnb_ref[...]).astype(BF16)
        for c in range(nchunk):
            for h in range(N_HEAD):
                rs, cs = slice(c * CHUNK, (c + 1) * CHUNK), slice(h * HEAD, (h + 1) * HEAD)
                mix[rs, cs] = _dot(wsm_ref[h], vln[rs, cs]) + bias_ref[:, cs]
        dya = dy_ref[0].astype(F32)
        sg, dsg = _silu_and_grad(ag)
        mixed = mix[...]
        dab_ref[:, 2 * WIDTH:3 * WIDTH] = (dya * u * mixed * dsg).astype(BF16)
        dab_ref[:, 0:WIDTH] = (dya * mixed * sg * du).astype(BF16)
        dmixed = dya * u * sg
        dmb = dmixed.astype(BF16)
        tril = (lax.broadcasted_iota(jnp.int32, (CHUNK, CHUNK), 1)
                <= lax.broadcasted_iota(jnp.int32, (CHUNK, CHUNK), 0))
        for c in range(nchunk):
            rs = slice(c * CHUNK, (c + 1) * CHUNK)
            dbias_ref[...] += dmixed[rs, :]
            for h in range(N_HEAD):
                cs = slice(h * HEAD, (h + 1) * HEAD)
                dvl[rs, cs] = _dot(wsmt_ref[h], dmb[rs, cs])
                dws_ref[h] += jnp.where(tril, _dot_nt(dmb[rs, cs], vln[rs, cs]), 0.0)
        dvln = dvl[...]
        dlng_ref[...] += jnp.sum(dvln * vhat, axis=0, keepdims=True)
        dlnb_ref[...] += jnp.sum(dvln, axis=0, keepdims=True)
        dvh = dvln * lng_ref[...]
        dv = rstd * (dvh - jnp.mean(dvh, axis=-1, keepdims=True)
                     - vhat * jnp.mean(dvh * vhat, axis=-1, keepdims=True))
        dab_ref[:, WIDTH:2 * WIDTH] = (dv * dgelu_v).astype(BF16)

        halo_ok = (i > 0).astype(F32)
        for c in range(nchunk):
            rs = slice(c * CHUNK, (c + 1) * CHUNK)
            for g, win in enumerate(POOL_WINDOWS):
                cs = slice(g * HEAD, (g + 1) * HEAD)
                bcur, bprev = _band_masks(win)
                cur = p_ref[rs, cs]
                if c == 0:
                    prev = (ph_ref[:, cs].astype(F32) * halo_ok).astype(BF16)
                else:
                    prev = p_ref[(c - 1) * CHUNK:c * CHUNK, cs]
                sums = _dot(bcur, cur) + _dot(bprev, prev)
                dvl[rs, cs] = sums * _inv_count(i * tm + c * CHUNK, win) - cur.astype(F32)
        dmat = dvl[...].astype(BF16)
        for g in range(4):
            cs = slice(g * HEAD, (g + 1) * HEAD)
            mix[:, cs] = _dot(dmat[:, cs], pw_ref[g])
        yg = mix[...]
        pg = pg_ref[...].astype(F32)
        dyp = dy_ref[1].astype(F32)
        spg, dspg = _silu_and_grad(pg)
        dyy = dyp * spg
        scale = ps_ref[...]
        dab_ref[:, 4 * WIDTH:5 * WIDTH] = (dyp * yg * scale * dspg).astype(BF16)
        dps_ref[...] += jnp.sum(dyy * yg, axis=0, keepdims=True)
        dyg = (dyy * scale).astype(BF16)
        for g in range(4):
            cs = slice(g * HEAD, (g + 1) * HEAD)
            dpw_ref[g] += _dot_tn(dmat[:, cs], dyg[:, cs])
            mix[:, cs] = _dot(dyg[:, cs], pwt_ref[g])
        next_ok = (i + 1 < S // tm).astype(F32)
        dygn = (dypn_ref[...].astype(F32) * _silu(pgn_ref[...].astype(F32)) * scale * next_ok).astype(BF16)
        for c in range(nchunk + 1):
            for g, win in enumerate(POOL_WINDOWS):
                cs = slice(g * HEAD, (g + 1) * HEAD)
                if c < nchunk:
                    dd = mix[c * CHUNK:(c + 1) * CHUNK, cs]
                else:
                    dd = _dot(dygn[:, cs], pwt_ref[g])
                ddn[c * CHUNK:(c + 1) * CHUNK, cs] = dd * _inv_count(i * tm + c * CHUNK, win)
        ddnb = ddn[...].astype(BF16)
        for c in range(nchunk):
            rs = slice(c * CHUNK, (c + 1) * CHUNK)
            ns = slice((c + 1) * CHUNK, (c + 2) * CHUNK)
            for g, win in enumerate(POOL_WINDOWS):
                cs = slice(g * HEAD, (g + 1) * HEAD)
                bcur, bprev = _band_masks(win)
                dp = _dot_tn(bcur, ddnb[rs, cs]) + _dot_tn(bprev, ddnb[ns, cs]) - mix[rs, cs]
                dab_ref[rs, 3 * WIDTH + g * HEAD:3 * WIDTH + (g + 1) * HEAD] = dp.astype(BF16)

        mg = mg_ref[...].astype(F32)
        dym = dy_ref[2].astype(F32)
        smg, dsmg = _silu_and_grad(mg)
        dob = (dym * smg).astype(BF16)
        for h in range(N_HEAD):
            cs = slice(h * HEAD, (h + 1) * HEAD)
            vs = slice(WIDTH + h * HEAD, WIDTH + (h + 1) * HEAD)
            qh = mq_ref[:, cs]
            p = _mem_softmax(qh, kv_ref[:, cs])
            pb = p.astype(BF16)
            mix[:, cs] = _dot(pb, kv_ref[:, vs])
            dp = _dot_nt(dob[:, cs], kv_ref[:, vs])
            ds = (p * (dp - jnp.sum(p * dp, axis=-1, keepdims=True))).astype(BF16)
            dm_ref[:, cs] = (_dot(ds, kv_ref[:, cs]) * ATT_SCALE).astype(BF16)
            dkv_ref[:, cs] += _dot_tn(ds, qh) * ATT_SCALE
            dkv_ref[:, vs] += _dot_tn(pb, dob[:, cs])
        dm_ref[:, WIDTH:2 * WIDTH] = (dym * mix[...] * dsmg).astype(BF16)

    blk = tm // CHUNK
    small = [_full((1, WIDTH)), _full((1, WIDTH)), _full((N_HEAD, CHUNK, CHUNK)), _full((CHUNK, WIDTH)),
             _full((4, HEAD, HEAD)), _full((1, WIDTH)), _full((MEM_LEN, 2 * WIDTH))]
    return pl.pallas_call(
        body, name="abm_bwd",
        grid=(S // tm,),
        in_specs=[_rows(tm, WIDTH, CB_U), _rows(tm, WIDTH, CB_V), _rows(tm, WIDTH, CB_AGATE),
                  _rows(tm, WIDTH, CB_PIN),
                  pl.BlockSpec((CHUNK, WIDTH), lambda i: (jnp.maximum(i * blk - 1, 0), CB_PIN)),
                  _rows(tm, WIDTH, CB_PGATE),
                  pl.BlockSpec((CHUNK, WIDTH), lambda i: (jnp.minimum(i * blk + blk, nblk - 1), CB_PGATE)),
                  _rows(tm, WIDTH, CB_MQ), _rows(tm, WIDTH, CB_MGATE),
                  pl.BlockSpec((2, tm, WIDTH), lambda i: (0, i, 0)),
                  pl.BlockSpec((3, tm, WIDTH), lambda i: (0, i, 0)),
                  pl.BlockSpec((None, CHUNK, WIDTH), lambda i: (1, jnp.minimum(i * blk + blk, nblk - 1), 0)),
                  _full((1, WIDTH)), _full((1, WIDTH)), _full((N_HEAD, CHUNK, CHUNK)), _full((N_HEAD, CHUNK, CHUNK)),
                  _full((CHUNK, WIDTH)), _full((4, HEAD, HEAD)), _full((4, HEAD, HEAD)), _full((1, WIDTH)),
                  _full((MEM_LEN, 2 * WIDTH))],
        out_specs=[_rows(tm, 5 * WIDTH), _rows(tm, 2 * WIDTH)] + small,
        out_shape=[jax.ShapeDtypeStruct((S, D_BRANCHES), BF16), jax.ShapeDtypeStruct((S, 2 * WIDTH), BF16),
                   jax.ShapeDtypeStruct((1, WIDTH), F32), jax.ShapeDtypeStruct((1, WIDTH), F32),
                   jax.ShapeDtypeStruct((N_HEAD, CHUNK, CHUNK), F32), jax.ShapeDtypeStruct((CHUNK, WIDTH), F32),
                   jax.ShapeDtypeStruct((4, HEAD, HEAD), F32), jax.ShapeDtypeStruct((1, WIDTH), F32),
                   jax.ShapeDtypeStruct((MEM_LEN, 2 * WIDTH), F32)],
        scratch_shapes=[pltpu.VMEM((tm, WIDTH), F32), pltpu.VMEM((tm, WIDTH), F32),
                        pltpu.VMEM((tm + CHUNK, WIDTH), F32)],
        compiler_params=_params(("arbitrary",)),
    )(proj, proj, proj, proj, proj, proj, proj, proj, proj, cdf, dy3, dy3,
      ln_g, ln_b, wsm, wsm_t, bias_full, pool_w, pool_wt, pool_scale, kv)


def _bias_reduce(dbias_full):
    def body(d_ref, o_ref):
        d = d_ref[...]
        o_ref[...] = _put_cols([jnp.sum(d[:, h * HEAD:(h + 1) * HEAD], axis=1, keepdims=True) for h in range(N_HEAD)])

    return pl.pallas_call(body, name="bias_reduce", out_shape=jax.ShapeDtypeStruct((CHUNK, 128), F32))(dbias_full)


def _mem_bwd(mem, g, mem_n, w, dkv):
    def body(m_ref, g_ref, mn_ref, w_ref, dkv_ref, dw_ref, dg_ref):
        dkvb = dkv_ref[...].astype(BF16)
        dw_ref[...] = _dot_tn(mn_ref[...], dkvb).astype(BF16)
        dmn = _dot_nt(dkvb, w_ref[...])
        xf = m_ref[...]
        r = lax.rsqrt(jnp.mean(xf * xf, axis=-1, keepdims=True) + EPS)
        dg_ref[...] = jnp.sum(dmn * xf * r, axis=0, keepdims=True)

    return pl.pallas_call(
        body, name="mem_bwd",
        out_shape=[jax.ShapeDtypeStruct((D_MODEL, 2 * WIDTH), BF16), jax.ShapeDtypeStruct((1, D_MODEL), F32)],
        compiler_params=pltpu.CompilerParams(vmem_limit_bytes=VMEM_LIMIT),
    )(mem, g, mem_n, w, dkv)


def _dh_bwd(dpb, dpg, wt, wgt, x, g, dxo, parts=(), grads=()):
    S = x.shape[0]
    tm, tkb, tkg = 1024, D_BRANCHES // 4, D_GATES // 4
    nkb, nkg = 4, 4
    nk = nkb + nkg
    ni = S // tm
    n, m = len(parts), len(grads)

    def body(dpb_ref, wbr_ref, dpg_ref, wg_ref, x_ref, g_ref, dxo_ref, *rest):
        p_in, g_in = rest[:n], rest[n:n + m]
        dx_ref, dg_ref = rest[n + m:n + m + 2]
        p_out, g_out = rest[n + m + 2:2 * n + m + 2], rest[2 * n + m + 2:2 * (n + m) + 2]
        acc, sems = rest[2 * (n + m) + 2], rest[2 * (n + m) + 3:]
        second_sems, first_sems = (sems[:3] if n else ()), sems[3 if n else 0:]
        i, kk = pl.program_id(0), pl.program_id(1)

        @pl.when(jnp.logical_and(i == 0, kk == 0))
        def _():
            dg_ref[...] = jnp.zeros_like(dg_ref)
            if n:
                _comm_start(_rs_second(p_in, p_out, *second_sems))
            if m:
                _comm_start(_rs_first(g_in, g_out, *first_sems))

        @pl.when(kk == 0)
        def _():
            acc[...] = jnp.zeros_like(acc)

        @pl.when(kk < nkb)
        def _():
            acc[...] += _dot(dpb_ref[...], wbr_ref[...])

        @pl.when(kk >= nkb)
        def _():
            acc[...] += _dot(dpg_ref[...], wg_ref[...])

        @pl.when(kk == nk - 1)
        def _():
            xf = x_ref[...]
            r = lax.rsqrt(jnp.mean(xf * xf, axis=-1, keepdims=True) + EPS)
            xhat = xf * r
            dh = acc[...]
            dg_ref[...] += jnp.sum(dh * xhat, axis=0, keepdims=True)
            dxh = dh * g_ref[...]
            dx_ref[...] = dxo_ref[...] + r * (dxh - xhat * jnp.mean(dxh * xhat, axis=-1, keepdims=True))

        if n or m:
            @pl.when(jnp.logical_and(i == ni - 1, kk == nk - 1))
            def _():
                if n:
                    _comm_wait(_rs_second(p_in, p_out, *second_sems))
                if m:
                    _comm_wait(_rs_first(g_in, g_out, *first_sems))

    res = pl.pallas_call(
        body, name="dh_bwd_scatter" if (n or m) else "dh_bwd",
        grid=(ni, nk),
        in_specs=[pl.BlockSpec((tm, tkb), lambda i, k: (i, jnp.minimum(k, nkb - 1))),
                  pl.BlockSpec((tkb, D_MODEL), lambda i, k: (jnp.minimum(k, nkb - 1), 0)),
                  pl.BlockSpec((tm, tkg), lambda i, k: (i, jnp.maximum(k - nkb, 0))),
                  pl.BlockSpec((tkg, D_MODEL), lambda i, k: (jnp.maximum(k - nkb, 0), 0)),
                  pl.BlockSpec((tm, D_MODEL), lambda i, k: (i, 0)), pl.BlockSpec((1, D_MODEL), lambda i, k: (0, 0)),
                  pl.BlockSpec((tm, D_MODEL), lambda i, k: (i, 0))] + [ANY] * (n + m),
        out_specs=[pl.BlockSpec((tm, D_MODEL), lambda i, k: (i, 0)), pl.BlockSpec((1, D_MODEL), lambda i, k: (0, 0))]
                  + [ANY] * (n + m),
        out_shape=[jax.ShapeDtypeStruct((S, D_MODEL), F32), jax.ShapeDtypeStruct((1, D_MODEL), F32)]
                  + [jax.ShapeDtypeStruct(p.shape, p.dtype) for p in parts]
                  + [jax.ShapeDtypeStruct(gr.shape[:1] + gr.shape[2:], gr.dtype) for gr in grads],
        scratch_shapes=[pltpu.VMEM((tm, D_MODEL), F32)] + (_dma_sems(3 * n, 3 * n, n) if n else [])
                       + (_dma_sems(N_CHIP * m, N_CHIP * m) if m else []),
        compiler_params=_params(("arbitrary", "arbitrary")),
    )(dpb, wt, dpg, wgt, x, g, dxo, *parts, *grads)
    return res[0], res[1], list(res[2:2 + n]), list(res[2 + n:])


def _dw_in(h, dpb, dpg, parts=()):
    S = h.shape[0]
    tk = 2048
    nk = S // tk
    n = len(parts)
    tmb = D_BRANCHES // 4
    ng = D_GATES // GATE_TILE

    def accumulate(a_ref, h_ref, o_ref, acc):
        kk = pl.program_id(1)

        @pl.when(kk == 0)
        def _():
            acc[...] = jnp.zeros_like(acc)

        acc[...] += _dot_tn(a_ref[...], h_ref[...])

        @pl.when(kk == nk - 1)
        def _():
            o_ref[...] = acc[...].astype(BF16)

    def branches(a_ref, h_ref, *rest):
        p_in, o_ref, p_out = rest[:n], rest[n], rest[n + 1:2 * n + 1]
        acc, sems = rest[2 * n + 1], rest[2 * n + 2:]
        i, kk = pl.program_id(0), pl.program_id(1)

        if n:
            @pl.when(jnp.logical_and(i == 0, kk == 0))
            def _():
                _comm_start(_rs_second(p_in, p_out, *sems))

        accumulate(a_ref, h_ref, o_ref, acc)

        if n:
            @pl.when(jnp.logical_and(i == 3, kk == nk - 1))
            def _():
                _comm_wait(_rs_second(p_in, p_out, *sems))

    def gates(a_ref, h_ref, dst_ref, o_ref, acc):
        accumulate(a_ref, h_ref, o_ref, acc)

    res = pl.pallas_call(
        branches, name="dw_in_branches_scatter" if n else "dw_in_branches",
        grid=(4, nk),
        in_specs=[pl.BlockSpec((tk, tmb), lambda i, k: (k, i)), pl.BlockSpec((tk, D_MODEL), lambda i, k: (k, 0))]
                 + [ANY] * n,
        out_specs=[pl.BlockSpec((tmb, D_MODEL), lambda i, k: (i, 0))] + [ANY] * n,
        out_shape=[jax.ShapeDtypeStruct((D_IN, D_MODEL), BF16)]
                  + [jax.ShapeDtypeStruct(p.shape, p.dtype) for p in parts],
        scratch_shapes=[pltpu.VMEM((tmb, D_MODEL), F32)] + (_dma_sems(3 * n, 3 * n, n) if n else []),
        compiler_params=_params(("arbitrary", "arbitrary")),
    )(dpb, h, *parts)
    dwt = pl.pallas_call(
        gates, name="dw_in_gates",
        grid=(ng, nk),
        in_specs=[pl.BlockSpec((tk, GATE_TILE), lambda i, k: (k, i)), pl.BlockSpec((tk, D_MODEL), lambda i, k: (k, 0)),
                  ANY],
        out_specs=pl.BlockSpec((GATE_TILE, D_MODEL), lambda i, k: (D_BRANCHES // GATE_TILE + i, 0)),
        out_shape=jax.ShapeDtypeStruct((D_IN, D_MODEL), BF16),
        input_output_aliases={2: 0},
        scratch_shapes=[pltpu.VMEM((GATE_TILE, D_MODEL), F32)],
        compiler_params=_params(("parallel", "arbitrary")),
    )(dpg, h, res[0])
    return dwt, list(res[1:])


def _row_tile(R, C, block_bytes=2 << 20):
    for cand in range(min(R, block_bytes // (C * 4)) // 8 * 8, 0, -8):
        if R % cand == 0:
            return cand
    return R


def _adamw_update(p_ref, w_ref, m_ref, v_ref, g_ref, d_ref, nm_ref, nv_ref):
    c1 = 1.0 / (1.0 - ADAM_B1 ** ADAM_STEP)
    c2 = 1.0 / (1.0 - ADAM_B2 ** ADAM_STEP)
    g = p_ref[0].astype(F32)
    for k in range(1, p_ref.shape[0]):
        g = g + p_ref[k].astype(F32)
    nm = ADAM_B1 * m_ref[...] + (1.0 - ADAM_B1) * g
    nv = ADAM_B2 * v_ref[...] + (1.0 - ADAM_B2) * (g * g)
    g_ref[...] = g
    nm_ref[...] = nm
    nv_ref[...] = nv
    d_ref[...] = -ADAM_LR * ((nm * c1) / (jnp.sqrt(nv * c2) + ADAM_EPS) + ADAM_WD * w_ref[...])


def _adamw(parts, w, m, v, name):
    P, R, C = parts.shape
    tr = _row_tile(R, C)

    def body(*refs):
        _adamw_update(*refs)

    spec = pl.BlockSpec((tr, C), lambda i: (i, 0))
    return pl.pallas_call(
        body, name=name,
        grid=(R // tr,),
        in_specs=[pl.BlockSpec((P, tr, C), lambda i: (0, i, 0)), spec, spec, spec],
        out_specs=[spec] * 4,
        out_shape=[jax.ShapeDtypeStruct((R, C), F32)] * 4,
        compiler_params=_params(("parallel",)),
    )(parts, w, m, v)


def _adamw_layers(parts, w, m, v, name):
    depth = len(parts)
    P, R, C = parts[0].shape
    tr = _row_tile(R, C, 1 << 20)

    def body(*refs):
        layer = pl.program_id(0)
        for k in range(depth):
            @pl.when(layer == k)
            def _(k=k):
                _adamw_update(refs[k], *refs[depth:])

    def part_spec(k):
        return pl.BlockSpec((P, tr, C), lambda l, i: (0, jnp.where(l == k, i, 0), 0))

    spec = pl.BlockSpec((None, tr, C), lambda l, i: (l, i, 0))
    return pl.pallas_call(
        body, name=name,
        grid=(depth, R // tr),
        in_specs=[part_spec(k) for k in range(depth)] + [spec] * 3,
        out_specs=[spec] * 4,
        out_shape=[jax.ShapeDtypeStruct((depth, R, C), F32)] * 4,
        compiler_params=_params(("arbitrary", "arbitrary")),
    )(*parts, w, m, v)


def _place():
    return lax.axis_index("x"), lax.axis_index("y"), lax.axis_index("c")


def _all_gather(shards):
    n = len(shards)

    def body(*refs):
        ins, outs = refs[:n], refs[n:2 * n]
        send1, recv1, local_sems, send2, recv2 = refs[2 * n:]
        first = _ag_first(ins, outs, send1, recv1, local_sems)
        second = _ag_second(outs, send2, recv2)
        _comm_start(first)
        for j in range(3):
            for a in range(n):
                first[2][4 * a + 1 + j].wait_recv()
            for a in range(n):
                second[1][3 * a + j].start()
        for a in range(n):
            first[2][4 * a].wait_recv()
        for cp in second[2]:
            cp.wait_recv()
        for cp in first[1] + second[1]:
            cp.wait_send()
        for cp in first[0]:
            cp.wait()

    return pl.pallas_call(
        body, name="weights_all_gather",
        in_specs=[ANY] * n, out_specs=[ANY] * n,
        out_shape=[jax.ShapeDtypeStruct((N_DEV,) + s.shape, s.dtype) for s in shards],
        scratch_shapes=_dma_sems(4 * n, 4 * n, n, 3 * n, 3 * n),
        compiler_params=pltpu.CompilerParams(has_side_effects=True),
    )(*shards)


N_BIG = 4


def _dev(p):
    return 4 * p[0] + 2 * p[1] + p[2]


def _other_chips(x, y):
    return [(1 - x, y), (x, 1 - y), (1 - x, 1 - y)]


def _remote(src, dst, send_sems, recv_sems, k, to):
    return pltpu.make_async_remote_copy(src_ref=src, dst_ref=dst, send_sem=send_sems.at[k], recv_sem=recv_sems.at[k],
                                        device_id=to, device_id_type=MESH)


def _ag_first(ins, outs, send_sems, recv_sems, local_sems):
    x, y, c = _place()
    me = (x, y, c)
    targets = [(x, y, 1 - c)] + [(*chip, c) for chip in _other_chips(x, y)]
    local, out, inc = [], [], []
    for a in range(len(ins)):
        local.append(pltpu.make_async_copy(ins[a], outs[a].at[_dev(me)], local_sems.at[a]))
        for k, to in enumerate(targets):
            out.append(_remote(ins[a], outs[a].at[_dev(me)], send_sems, recv_sems, 4 * a + k, to))
            inc.append(_remote(ins[a], outs[a].at[_dev(to)], send_sems, recv_sems, 4 * a + k, to))
    return local, out, inc


def _ag_second(bufs, send_sems, recv_sems):
    x, y, c = _place()
    out, inc = [], []
    for a in range(len(bufs)):
        for j, chip in enumerate(_other_chips(x, y)):
            mine, theirs = bufs[a].at[_dev((*chip, c))], bufs[a].at[_dev((*chip, 1 - c))]
            out.append(_remote(mine, mine, send_sems, recv_sems, 3 * a + j, (x, y, 1 - c)))
            inc.append(_remote(theirs, theirs, send_sems, recv_sems, 3 * a + j, (x, y, 1 - c)))
    return [], out, inc


def _rs_first(ins, outs, send_sems, recv_sems):
    x, y, c = _place()
    out = [_remote(ins[a].at[j, 1 - c], outs[a].at[j], send_sems, recv_sems, N_CHIP * a + j, (x, y, 1 - c))
           for a in range(len(ins)) for j in range(N_CHIP)]
    return [], out, out


def _rs_second(ins, outs, send_sems, recv_sems, local_sems):
    x, y, c = _place()
    my_chip = 2 * x + y
    local, out, inc = [], [], []
    for a in range(len(ins)):
        local.append(pltpu.make_async_copy(ins[a].at[my_chip], outs[a].at[my_chip], local_sems.at[a]))
        for k, (ox, oy) in enumerate(_other_chips(x, y)):
            out.append(_remote(ins[a].at[2 * ox + oy], outs[a].at[my_chip], send_sems, recv_sems, 3 * a + k, (ox, oy, c)))
            inc.append(_remote(ins[a].at[2 * ox + oy], outs[a].at[2 * ox + oy], send_sems, recv_sems, 3 * a + k,
                               (ox, oy, c)))
    return local, out, inc


def _comm_start(exchange):
    local, out, _ = exchange
    for cp in local + out:
        cp.start()


def _comm_wait(exchange):
    local, out, inc = exchange
    for cp in inc:
        cp.wait_recv()
    for cp in out:
        cp.wait_send()
    for cp in local:
        cp.wait()


def _dma_sems(*counts):
    return [pltpu.SemaphoreType.DMA((n,)) for n in counts]


def _pair_sum(grads, recvs):
    n = len(grads)

    def body(c_ref, *refs):
        for a in range(n):
            refs[2 * n + a][...] = (refs[a][...].astype(F32) + refs[n + a][...].astype(F32)).astype(BF16)

    def g_spec(g):
        return pl.BlockSpec((None, None) + g.shape[2:], lambda j, c_ref: (j, c_ref[0], 0, 0))

    def r_spec(r):
        return pl.BlockSpec((None,) + r.shape[1:], lambda j, c_ref: (j, 0, 0))

    return pl.pallas_call(
        body, name="pair_sum",
        grid_spec=pltpu.PrefetchScalarGridSpec(
            num_scalar_prefetch=1, grid=(N_CHIP,),
            in_specs=[g_spec(g) for g in grads] + [r_spec(r) for r in recvs],
            out_specs=[r_spec(r) for r in recvs]),
        out_shape=[jax.ShapeDtypeStruct(r.shape, BF16) for r in recvs],
        compiler_params=_params(("parallel",)),
    )(lax.axis_index("c").reshape(1).astype(jnp.int32), *grads, *recvs)


SMALL_ROWS = 544


def _all_reduce_small(buf, parts=()):
    n = len(parts)

    def body(in_ref, *rest):
        p_in, out_ref, p_out = rest[:n], rest[n], rest[n + 1:2 * n + 1]
        recv, acc, send1, recv1, send2, recv2 = rest[2 * n + 1:2 * n + 7]
        scatter_sems = rest[2 * n + 7:]
        x, y, c = _place()
        me = 4 * x + 2 * y + c
        peers = [(x ^ (r >> 2), y ^ ((r >> 1) & 1), c ^ (r & 1)) for r in range(1, N_DEV)]

        def idx(p):
            return 4 * p[0] + 2 * p[1] + p[2]

        if n:
            _comm_start(_rs_second(p_in, p_out, *scatter_sems))
        first = [pltpu.make_async_remote_copy(
            src_ref=in_ref.at[idx(p)], dst_ref=recv.at[me], send_sem=send1.at[r], recv_sem=recv1.at[r],
            device_id=p, device_id_type=MESH) for r, p in enumerate(peers)]
        for cp in first:
            cp.start()
        recv[me] = in_ref[me]
        for r, p in enumerate(peers):
            pltpu.make_async_remote_copy(
                src_ref=in_ref.at[idx(p)], dst_ref=recv.at[idx(p)], send_sem=send1.at[r], recv_sem=recv1.at[r],
                device_id=p, device_id_type=MESH).wait_recv()
        total = recv[0]
        for k in range(1, N_DEV):
            total = total + recv[k]
        acc[...] = total
        out_ref[me] = total
        second = [pltpu.make_async_remote_copy(
            src_ref=acc, dst_ref=out_ref.at[me], send_sem=send2.at[r], recv_sem=recv2.at[r],
            device_id=p, device_id_type=MESH) for r, p in enumerate(peers)]
        for cp in second:
            cp.start()
        for r, p in enumerate(peers):
            pltpu.make_async_remote_copy(
                src_ref=acc, dst_ref=out_ref.at[idx(p)], send_sem=send2.at[r], recv_sem=recv2.at[r],
                device_id=p, device_id_type=MESH).wait_recv()
        for cp in first + second:
            cp.wait_send()
        if n:
            _comm_wait(_rs_second(p_in, p_out, *scatter_sems))

    vm = pl.BlockSpec(memory_space=pltpu.VMEM)
    res = pl.pallas_call(
        body, name="small_grads_all_reduce",
        in_specs=[vm] + [ANY] * n, out_specs=[vm] + [ANY] * n,
        out_shape=[jax.ShapeDtypeStruct(buf.shape, F32)] + [jax.ShapeDtypeStruct(p.shape, p.dtype) for p in parts],
        scratch_shapes=[pltpu.VMEM(buf.shape, F32), pltpu.VMEM(buf.shape[1:], F32)] + _dma_sems(7, 7, 7, 7)
                       + (_dma_sems(3 * n, 3 * n, n) if n else []),
        compiler_params=pltpu.CompilerParams(has_side_effects=True, vmem_limit_bytes=VMEM_LIMIT),
    )(buf, *parts)
    return res[0], list(res[1:])


def _dilate(a, d):
    if d == 1:
        return a
    S, C = a.shape
    return a.reshape(S // d, d, C).transpose(1, 0, 2).reshape(S, C)


def _undilate(a, d):
    if d == 1:
        return a
    S, C = a.shape
    return a.reshape(d, S // d, C).transpose(1, 0, 2).reshape(S, C)


def _cols(a, cb, n=1):
    return a[:, cb * WIDTH:(cb + n) * WIDTH]


def _to_blocks(g, kind):
    if kind == "rows":
        C = g.shape[1]
        return g.reshape(N_CHIP, 2, -1, C)
    return g.reshape(4 * WIDTH, N_CHIP, 2, -1).transpose(1, 2, 0, 3)


SMALL = ("norm_g", "gm_ln_g", "gm_ln_b", "gm_ws", "gm_bs", "pool_w", "pool_scale", "mem_norm_g", "final_norm_g")


def _pack_small(tree):
    flat = jnp.concatenate([tree[k].reshape(-1, 128) for k in SMALL], axis=0)
    return jnp.pad(flat, ((0, N_DEV * SMALL_ROWS - flat.shape[0]), (0, 0)))


def _unpack_small(flat, like):
    out, at = {}, 0
    for k in SMALL:
        rows = like[k].size // 128
        out[k] = flat[at:at + rows].reshape(like[k].shape)
        at += rows
    return out


def _make_layer(wt, wkv, wb, wout, norm_g, mem_norm_g, ln_g, ln_b, gm_ws, gm_bs, pool_w, pool_scale):
    tril = jnp.tril(jnp.ones((CHUNK, CHUNK), bool))
    wsm = jnp.where(tril, gm_ws, 0.0).astype(BF16)
    pw = pool_w.astype(BF16)
    return dict(wt=wt, wgt=wt[D_BRANCHES:], wkv=wkv, wb=wb, wout=wout, g=norm_g[None], mg=mem_norm_g[None],
                ln_g=ln_g[None],
                ln_b=ln_b[None], wsm=wsm, wsm_t=wsm.transpose(0, 2, 1), pw=pw, pw_t=pw.transpose(0, 2, 1),
                ps=pool_scale[None], bias=jnp.repeat(gm_bs.T, HEAD, axis=1))


def _layer_fwd(xl, mem0, L, next_shards=()):
    S = xl.shape[0]
    proj, gates, h, half_gathered = _in_proj(xl, L["g"], L["wt"], L["wgt"], next_shards[:1])
    kv, mem_n = _mem_kv(mem0, L["mg"], L["wkv"])
    y4, cdf, gathered = _abm_fwd(proj, L["ln_g"], L["ln_b"], L["wsm"], L["bias"], L["pw"], L["ps"], kv, half_gathered)
    o_g, l_g = [], []
    for gi, d in enumerate(DILATIONS):
        if d == 1:
            o, lse = _attn_fwd(proj, CB_Q0, proj, CB_K, proj, CB_CV, S // CHUNK)
        else:
            o, lse = _attn_fwd_dilated(proj, CB_Q0 + gi, CB_K, CB_CV, d)
        o_g.append(o)
        l_g.append(lse)
    (xn, y4, oc, lse, z), rest = _merge_fwd(xl, y4, o_g, l_g, proj, gates, L["wb"], L["wout"], next_shards[1:])
    saved = dict(x=xl, proj=proj, gates=gates, h=h, kv=kv, mem_n=mem_n, y4=y4, cdf=cdf, oc=oc, lse=lse, z=z)
    return xn, saved, gathered + rest


def _place_cols(dst, piece, cb):
    return lax.dynamic_update_slice(dst, piece, (0, cb * WIDTH))


def _layer_bwd(dx, mem0, L, sv, later=(), last=False):
    S = dx.shape[0]
    proj = sv["proj"]
    (dy3, doc, delta, dcg, dgm, dwb, dwout), from_sibling = _merge_bwd(
        dx, sv["y4"], sv["oc"], sv["z"], proj, sv["gates"], L["wb"], L["wout"], later)
    pair = _pair_sum(later, from_sibling) if later else ()
    dpb, dm, dlng, dlnb, dws, dbias, dpw, dps, dkv = _abm_bwd(
        proj, sv["cdf"], dy3, L["ln_g"], L["ln_b"], L["wsm"], L["wsm_t"], L["bias"], L["pw"], L["pw_t"], L["ps"], sv["kv"])
    dk, dv = None, None
    for gi, d in enumerate(DILATIONS):
        if d == 1:
            dpb, dk, dv = _attn_bwd(proj, CB_Q0, proj, CB_K, proj, CB_CV, doc, sv["lse"], delta, S // CHUNK,
                                    dpb, CB_Q0)
        else:
            dpb, dk, dv = _attn_bwd_dilated(proj, CB_Q0 + gi, CB_K, CB_CV, doc, sv["lse"], delta, d, dk, dv,
                                            dpb, CB_Q0 + gi)
    dpb = _place_cols(dpb, dk, CB_K)
    dpb = _place_cols(dpb, dv, CB_CV)
    dpb = _place_cols(dpb, dcg, CB_CGATE)
    dpb = _place_cols(dpb, dm, CB_MQ)
    dwkv, dmg = _mem_bwd(mem0, L["mg"], sv["mem_n"], L["wkv"], dkv)
    dwin_t, parts_rest = _dw_in(sv["h"], dpb, dgm, pair[1:])
    big = _blocked(dict(w_in=dwin_t, w_mem_kv=dwkv, w_branch=dwb, w_out=dwout))
    dxi, dng, parts, from_sibling = _dh_bwd(dpb, dgm, L["wt"], L["wgt"], sv["x"], L["g"], dx, pair[:1],
                                            big if last else ())
    parts = parts + parts_rest
    small = dict(norm_g=dng[0], gm_ln_g=dlng[0], gm_ln_b=dlnb[0], gm_ws=dws,
                 gm_bs=_bias_reduce(dbias)[:, :N_HEAD].T, pool_w=dpw, pool_scale=dps[0], mem_norm_g=dmg[0])
    return dxi, big, small, parts, from_sibling


BIG = ("w_in", "w_mem_kv", "w_branch", "w_out")


def _blocked(big):
    return [_to_blocks(big["w_in"], "rows"), _to_blocks(big["w_mem_kv"], "rows"),
            _to_blocks(big["w_branch"], "branch"), _to_blocks(big["w_out"], "rows")]


def _full_weights(gathered):
    win_t, wkv, wb, wout = gathered
    return (win_t.reshape(D_IN, D_MODEL), wkv.reshape(D_MODEL, 2 * WIDTH),
            wb.reshape(N_DEV, 4, WIDTH, -1).transpose(1, 2, 0, 3).reshape(4, WIDTH, D_MODEL),
            wout.reshape(D_MODEL, D_MODEL))


def kernel(x, mem, norm_g, w_in, gm_ln_g, gm_ln_b, gm_ws, gm_bs, pool_w, pool_scale, mem_norm_g, w_mem_kv, w_branch, w_out, final_norm_g, loss_target, m_norm_g, m_w_in, m_gm_ln_g, m_gm_ln_b, m_gm_ws, m_gm_bs, m_pool_w, m_pool_scale, m_mem_norm_g, m_w_mem_kv, m_w_branch, m_w_out, m_final_norm_g, v_norm_g, v_w_in, v_gm_ln_g, v_gm_ln_b, v_gm_ws, v_gm_bs, v_pool_w, v_pool_scale, v_mem_norm_g, v_w_mem_kv, v_w_branch, v_w_out, v_final_norm_g):
    x0 = x[0]
    mem0 = mem[0]
    tgt = loss_target[0]
    S = x0.shape[0]

    shards = [[w_in[l].T.astype(BF16), w_mem_kv[l].astype(BF16), w_branch[l].astype(BF16).reshape(4 * WIDTH, -1),
               w_out[l].astype(BF16)] for l in range(DEPTH)]
    gathered = _all_gather(shards[0])
    layers, saved = [], []
    xl = x0
    for l in range(DEPTH):
        layers.append(_make_layer(*_full_weights(gathered), norm_g[l], mem_norm_g[l], gm_ln_g[l], gm_ln_b[l],
                                  gm_ws[l], gm_bs[l], pool_w[l], pool_scale[l]))
        xl, sv, gathered = _layer_fwd(xl, mem0, layers[l], shards[l + 1] if l + 1 < DEPTH else ())
        saved.append(sv)

    loss_part, dx, d_final = _loss_head(xl, final_norm_g[None], tgt)
    loss = lax.psum(loss_part[0, 0], ("x", "y", "c"))

    small = {k: [None] * DEPTH for k in SMALL if k != "final_norm_g"}
    parts = [None] * DEPTH
    later = ()
    for l in reversed(range(DEPTH)):
        dx, gb, gs, done, from_sibling = _layer_bwd(dx, mem0, layers[l], saved[l], later, last=(l == 0))
        if later:
            parts[l + 1] = done
        later = gb
        for k in gs:
            small[k][l] = gs[k]
    grad_x = dx[None]
    small_tree = {k: jnp.stack(small[k]) for k in small}
    small_tree["final_norm_g"] = d_final[0]
    reduced, parts[0] = _all_reduce_small(_pack_small(small_tree).reshape(N_DEV, SMALL_ROWS, 128),
                                          _pair_sum(later, from_sibling))

    weights = dict(norm_g=norm_g, w_in=w_in, gm_ln_g=gm_ln_g, gm_ln_b=gm_ln_b, gm_ws=gm_ws, gm_bs=gm_bs,
                   pool_w=pool_w, pool_scale=pool_scale, mem_norm_g=mem_norm_g, w_mem_kv=w_mem_kv,
                   w_branch=w_branch, w_out=w_out, final_norm_g=final_norm_g)
    m_in = dict(norm_g=m_norm_g, w_in=m_w_in, gm_ln_g=m_gm_ln_g, gm_ln_b=m_gm_ln_b, gm_ws=m_gm_ws, gm_bs=m_gm_bs,
                pool_w=m_pool_w, pool_scale=m_pool_scale, mem_norm_g=m_mem_norm_g, w_mem_kv=m_w_mem_kv,
                w_branch=m_w_branch, w_out=m_w_out, final_norm_g=m_final_norm_g)
    v_in = dict(norm_g=v_norm_g, w_in=v_w_in, gm_ln_g=v_gm_ln_g, gm_ln_b=v_gm_ln_b, gm_ws=v_gm_ws, gm_bs=v_gm_bs,
                pool_w=v_pool_w, pool_scale=v_pool_scale, mem_norm_g=v_mem_norm_g, w_mem_kv=v_w_mem_kv,
                w_branch=v_w_branch, w_out=v_w_out, final_norm_g=v_final_norm_g)
    res = {}
    def view(k, arr):
        return arr.transpose(0, 2, 1) if k == "w_in" else arr

    for a, k in enumerate(BIG):
        shape = view(k, weights[k]).shape
        by_layer = [parts[l][a] for l in range(DEPTH)]
        lrc = (DEPTH,) + by_layer[0].shape[1:]
        outs = _adamw_layers(by_layer, view(k, weights[k]).reshape(lrc), view(k, m_in[k]).reshape(lrc),
                             view(k, v_in[k]).reshape(lrc), "adamw_" + k)
        res[k] = [view(k, o.reshape(shape)) for o in outs]
    outs = _adamw(reduced.reshape(1, N_DEV * SMALL_ROWS, 128), _pack_small(weights), _pack_small(m_in),
                  _pack_small(v_in), "adamw_small")
    unpacked = [_unpack_small(o, weights) for o in outs]
    for k in SMALL:
        res[k] = [u[k] for u in unpacked]

    order = ("norm_g", "w_in", "gm_ln_g", "gm_ln_b", "gm_ws", "gm_bs", "pool_w", "pool_scale", "mem_norm_g",
             "w_mem_kv", "w_branch", "w_out", "final_norm_g")
    return (loss, grad_x, *[res[k][0] for k in order], *[res[k][1] for k in order],
            *[res[k][2] for k in order], *[res[k][3] for k in order])
```

```python
import functools
import math

import numpy as np

import jax
import jax.numpy as jnp
from jax import lax
from jax.experimental import pallas as pl
from jax.experimental.pallas import tpu as pltpu

F32 = jnp.float32
BF16 = jnp.bfloat16

D_MODEL = 1024
DEPTH = 4
WIDTH = 512
D_IN = 10752
HEAD = 128
N_HEAD = 4
CHUNK = 128
MEM_LEN = 256
POOL_WINDOWS = (2, 4, 8, 16)
DILATIONS = (1, 4, 16)
EPS = 1e-6
NEG = -1e30
ATT_SCALE = HEAD ** -0.5
N_DEV = 8
N_CHIP = 4

D_BRANCHES = 6656
D_GATES = D_IN - D_BRANCHES
CB_U, CB_V, CB_AGATE, CB_PIN, CB_PGATE = 0, 1, 2, 3, 4
CB_Q0, CB_K, CB_CV, CB_CGATE, CB_MQ, CB_MGATE = 5, 8, 9, 10, 11, 12

ADAM_LR = 0.001
ADAM_B1 = 0.9
ADAM_B2 = 0.999
ADAM_EPS = 1e-08
ADAM_WD = 0.01
ADAM_STEP = 10

VMEM_LIMIT = 56 * 1024 * 1024
MESH = pl.DeviceIdType.MESH
ANY = pl.BlockSpec(memory_space=pl.ANY)

NT = (((1,), (1,)), ((), ()))
TN = (((0,), (0,)), ((), ()))


def _dot(a, b):
    return jnp.dot(a, b, preferred_element_type=F32)


def _dot_nt(a, b):
    return lax.dot_general(a, b, NT, preferred_element_type=F32)


def _dot_tn(a, b):
    return lax.dot_general(a, b, TN, preferred_element_type=F32)


def _sigmoid(x):
    return 0.5 * jnp.tanh(0.5 * x) + 0.5


def _silu(x):
    return x * _sigmoid(x)


def _silu_and_grad(x):
    s = _sigmoid(x)
    return x * s, s * (1.0 + x * (1.0 - s))


def _normal_cdf(x):
    return 0.5 * (1.0 + lax.erf(x * (2.0 ** -0.5)))


def _gelu_and_grad(x, cdf):
    return x * cdf, cdf + x * jnp.exp(-0.5 * x * x) * (1.0 / math.sqrt(2.0 * math.pi))


def _col(blk, h):
    lane = lax.broadcasted_iota(jnp.int32, blk.shape, 1)
    return jnp.sum(jnp.where(lane == h, blk, 0.0), axis=1, keepdims=True)


def _put_cols(cols):
    rows = cols[0].shape[0]
    lane = lax.broadcasted_iota(jnp.int32, (rows, 128), 1)
    out = jnp.zeros((rows, 128), F32)
    for h, cv in enumerate(cols):
        out = jnp.where(lane == h, cv, out)
    return out


def _params(sem, vmem=VMEM_LIMIT):
    return pltpu.CompilerParams(dimension_semantics=sem, vmem_limit_bytes=vmem)


def _full(shape):
    nd = len(shape)
    return pl.BlockSpec(shape, lambda *_: (0,) * nd)


def _rows(tm, width, cb=0):
    return pl.BlockSpec((tm, width), lambda i: (i, cb))


GATE_TILE = 512


def _in_proj(x, g, wt, wgt, shards=()):
    S = x.shape[0]
    tm, tnb, tng = 1024, D_BRANCHES // 4, D_GATES // 4
    njb, njg = 4, 4
    n = len(shards)
    ni, nj = S // tm, njb + njg

    def body(x_ref, g_ref, wbr_ref, wg_ref, *rest):
        ins, (proj_ref, gates_ref, h_ref), outs = rest[:n], rest[n:n + 3], rest[n + 3:2 * n + 3]
        hs, sems = rest[2 * n + 3], rest[2 * n + 4:]
        i, j = pl.program_id(0), pl.program_id(1)

        if n:
            @pl.when(jnp.logical_and(i == 0, j == 0))
            def _():
                _comm_start(_ag_first(ins, outs, *sems))

        @pl.when(j == 0)
        def _():
            xf = x_ref[...]
            r = lax.rsqrt(jnp.mean(xf * xf, axis=-1, keepdims=True) + EPS)
            h = (xf * r * g_ref[...]).astype(BF16)
            hs[...] = h
            h_ref[...] = h

        @pl.when(j < njb)
        def _():
            proj_ref[...] = _dot_nt(hs[...], wbr_ref[...]).astype(BF16)

        @pl.when(j >= njb)
        def _():
            gates_ref[...] = _dot_nt(hs[...], wg_ref[...]).astype(BF16)

        if n:
            @pl.when(jnp.logical_and(i == ni - 1, j == nj - 1))
            def _():
                _comm_wait(_ag_first(ins, outs, *sems))

    def first(j):
        return jnp.minimum(j, njb - 1)

    def second(j):
        return jnp.maximum(j - njb, 0)

    res = pl.pallas_call(
        body, name="in_proj_gather" if n else "in_proj",
        grid=(ni, nj),
        in_specs=[pl.BlockSpec((tm, D_MODEL), lambda i, j: (i, 0)),
                  pl.BlockSpec((1, D_MODEL), lambda i, j: (0, 0)),
                  pl.BlockSpec((tnb, D_MODEL), lambda i, j: (first(j), 0)),
                  pl.BlockSpec((tng, D_MODEL), lambda i, j: (second(j), 0))]
                 + [ANY] * n,
        out_specs=[pl.BlockSpec((tm, tnb), lambda i, j: (i, first(j))),
                   pl.BlockSpec((tm, tng), lambda i, j: (i, second(j))),
                   pl.BlockSpec((tm, D_MODEL), lambda i, j: (i, 0))] + [ANY] * n,
        out_shape=[jax.ShapeDtypeStruct((S, D_BRANCHES), BF16), jax.ShapeDtypeStruct((S, D_GATES), BF16),
                   jax.ShapeDtypeStruct((S, D_MODEL), BF16)]
                  + [jax.ShapeDtypeStruct((N_DEV,) + s.shape, s.dtype) for s in shards],
        scratch_shapes=[pltpu.VMEM((tm, D_MODEL), BF16)] + (_dma_sems(4 * n, 4 * n, n) if n else []),
        compiler_params=_params(("arbitrary", "arbitrary")),
    )(x, g, wt, wgt, *shards)
    return res[0], res[1], res[2], list(res[3:])


def _mem_kv(mem, g, w):
    M = mem.shape[0]

    def body(m_ref, g_ref, w_ref, kv_ref, mn_ref):
        xf = m_ref[...]
        r = lax.rsqrt(jnp.mean(xf * xf, axis=-1, keepdims=True) + EPS)
        mn = (xf * r * g_ref[...]).astype(BF16)
        mn_ref[...] = mn
        kv_ref[...] = _dot(mn, w_ref[...]).astype(BF16)

    return pl.pallas_call(
        body, name="mem_kv",
        out_shape=[jax.ShapeDtypeStruct((M, 2 * WIDTH), BF16), jax.ShapeDtypeStruct((M, D_MODEL), BF16)],
        compiler_params=pltpu.CompilerParams(vmem_limit_bytes=VMEM_LIMIT),
    )(mem, g, w)


def _band_constants():
    t = np.arange(CHUNK)[:, None]
    s = np.arange(CHUNK)[None, :]
    bands = np.stack([np.stack([(t - s >= 0) & (t - s < win), s > t + CHUNK - win]) for win in POOL_WINDOWS])
    bands = bands.astype(np.float32)
    return jnp.asarray(bands, BF16), jnp.asarray(bands.transpose(0, 1, 3, 2), BF16)


def _inv_count(first_row, win):
    t = first_row + lax.broadcasted_iota(jnp.int32, (CHUNK, 1), 0)
    return 1.0 / jnp.minimum(t + 1, win).astype(F32)


def _layer_norm_fwd(v):
    mu = jnp.mean(v, axis=-1, keepdims=True)
    vc = v - mu
    var = jnp.mean(vc * vc, axis=-1, keepdims=True)
    rstd = lax.rsqrt(var + EPS)
    return vc * rstd, rstd


def _mem_softmax(q, kmem):
    s = _dot_nt(q, kmem) * ATT_SCALE
    m = jnp.max(s, axis=-1, keepdims=True)
    e = jnp.exp(s - m)
    return e * (1.0 / jnp.sum(e, axis=-1, keepdims=True))


def _abm_fwd(proj, ln_g, ln_b, wsm, bias_full, pool_w, pool_scale, kv, bands, gathered=()):
    S = proj.shape[0]
    tm = 512
    nchunk = tm // CHUNK
    n = len(gathered)
    nsteps = S // tm

    def body(u_ref, v_ref, ag_ref, p_ref, ph_ref, pg_ref, mq_ref, mg_ref, lng_ref, lnb_ref, wsm_ref, bias_ref,
             pw_ref, ps_ref, kv_ref, band_ref, *rest):
        y_ref, cdf_ref, bufs = rest[n], rest[n + 1], rest[n + 2:2 * n + 2]
        mix, sems = rest[2 * n + 2], rest[2 * n + 3:]
        i = pl.program_id(0)

        if n:
            @pl.when(i == 0)
            def _():
                _comm_start(_ag_second(bufs, *sems))

        au, av = u_ref[...].astype(F32), v_ref[...].astype(F32)
        cdf_u, cdf_v = _normal_cdf(au), _normal_cdf(av)
        cdf_ref[0] = cdf_u.astype(BF16)
        cdf_ref[1] = cdf_v.astype(BF16)
        u, v = au * cdf_u, av * cdf_v
        vhat, _ = _layer_norm_fwd(v)
        vln = (vhat * lng_ref[...] + lnb_ref[...]).astype(BF16)
        for c in range(nchunk):
            for h in range(N_HEAD):
                rs, cs = slice(c * CHUNK, (c + 1) * CHUNK), slice(h * HEAD, (h + 1) * HEAD)
                mix[rs, cs] = _dot(wsm_ref[h], vln[rs, cs]) + bias_ref[:, cs]
        y_ref[0] = (u * mix[...] * _silu(ag_ref[...].astype(F32))).astype(BF16)
        halo_ok = (i > 0).astype(F32)
        for c in range(nchunk):
            rs = slice(c * CHUNK, (c + 1) * CHUNK)
            for g, win in enumerate(POOL_WINDOWS):
                cs = slice(g * HEAD, (g + 1) * HEAD)
                cur = p_ref[rs, cs]
                if c == 0:
                    prev = (ph_ref[:, cs].astype(F32) * halo_ok).astype(BF16)
                else:
                    prev = p_ref[(c - 1) * CHUNK:c * CHUNK, cs]
                sums = _dot(band_ref[g, 0], cur) + _dot(band_ref[g, 1], prev)
                dm = sums * _inv_count(i * tm + c * CHUNK, win) - cur.astype(F32)
                mix[rs, cs] = _dot(dm.astype(BF16), pw_ref[g])
        y_ref[1] = (mix[...] * ps_ref[...] * _silu(pg_ref[...].astype(F32))).astype(BF16)
        for h in range(N_HEAD):
            cs = slice(h * HEAD, (h + 1) * HEAD)
            p = _mem_softmax(mq_ref[:, cs], kv_ref[:, cs])
            mix[:, cs] = _dot(p.astype(BF16), kv_ref[:, WIDTH + h * HEAD:WIDTH + (h + 1) * HEAD])
        y_ref[2] = (mix[...] * _silu(mg_ref[...].astype(F32))).astype(BF16)

        if n:
            @pl.when(i == nsteps - 1)
            def _():
                _comm_wait(_ag_second(bufs, *sems))

    blk = tm // CHUNK
    res = pl.pallas_call(
        body, name="abm_fwd_gather" if n else "abm_fwd",
        grid=(nsteps,),
        in_specs=[_rows(tm, WIDTH, CB_U), _rows(tm, WIDTH, CB_V), _rows(tm, WIDTH, CB_AGATE),
                  _rows(tm, WIDTH, CB_PIN),
                  pl.BlockSpec((CHUNK, WIDTH), lambda i: (jnp.maximum(i * blk - 1, 0), CB_PIN)),
                  _rows(tm, WIDTH, CB_PGATE), _rows(tm, WIDTH, CB_MQ), _rows(tm, WIDTH, CB_MGATE),
                  _full((1, WIDTH)), _full((1, WIDTH)), _full((N_HEAD, CHUNK, CHUNK)), _full((CHUNK, WIDTH)),
                  _full((4, HEAD, HEAD)), _full((1, WIDTH)), _full((MEM_LEN, 2 * WIDTH)),
                  _full((4, 2, CHUNK, CHUNK))] + [ANY] * n,
        out_specs=[pl.BlockSpec((3, tm, WIDTH), lambda i: (0, i, 0)), pl.BlockSpec((2, tm, WIDTH), lambda i: (0, i, 0))]
                  + [ANY] * n,
        out_shape=[jax.ShapeDtypeStruct((4, S, WIDTH), BF16),
                   jax.ShapeDtypeStruct((2, S, WIDTH), BF16)]
                  + [jax.ShapeDtypeStruct(b.shape, b.dtype) for b in gathered],
        input_output_aliases={16 + a: 2 + a for a in range(n)},
        scratch_shapes=[pltpu.VMEM((tm, WIDTH), F32)] + (_dma_sems(3 * n, 3 * n) if n else []),
        compiler_params=_params(("arbitrary",)),
    )(proj, proj, proj, proj, proj, proj, proj, proj, ln_g, ln_b, wsm, bias_full, pool_w, pool_scale, kv, bands,
      *gathered)
    return res[0], res[1], list(res[2:])


ATT_TILE = 512


def _attn_fwd(q, qcb, k, kcb, v, vcb, bps):
    S = q.shape[0]
    tm = ATT_TILE
    nb = tm // CHUNK

    nblocks = nb * N_HEAD

    def body(q_ref, k_ref, v_ref, kh_ref, vh_ref, o_ref, l_ref, sc_s, sp_s, pc_s, pp_s):
        i = pl.program_id(0)

        def prev_kv(n, cs):
            if n == 0:
                return kh_ref[:, cs], vh_ref[:, cs]
            ps = slice((n - 1) * CHUNK, n * CHUNK)
            return k_ref[ps, cs], v_ref[ps, cs]

        pens = []
        for n in range(nb):
            rs = slice(n * CHUNK, (n + 1) * CHUNK)
            pens.append(jnp.full((N_HEAD * CHUNK, 1), jnp.where((i * nb + n) % bps != 0, 0.0, NEG), F32))
            for h in range(N_HEAD):
                cs = slice(h * HEAD, (h + 1) * HEAD)
                bs = slice((n * N_HEAD + h) * CHUNK, (n * N_HEAD + h + 1) * CHUNK)
                qh = q_ref[rs, cs]
                sc_s[bs, :] = _dot_nt(qh, k_ref[rs, cs])
                sp_s[bs, :] = _dot_nt(qh, prev_kv(n, cs)[0])
        row = lax.broadcasted_iota(jnp.int32, (nblocks * CHUNK, CHUNK), 0) & (CHUNK - 1)
        col = lax.broadcasted_iota(jnp.int32, (nblocks * CHUNK, CHUNK), 1)
        sc = jnp.where(col <= row, sc_s[...] * ATT_SCALE, NEG)
        sp = jnp.where(col >= row, sp_s[...] * ATT_SCALE, NEG) + jnp.concatenate(pens, axis=0)
        m = jnp.maximum(jnp.max(sc, axis=-1, keepdims=True), jnp.max(sp, axis=-1, keepdims=True))
        ec = jnp.exp(sc - m)
        ep = jnp.exp(sp - m)
        den = jnp.sum(ec, axis=-1, keepdims=True) + jnp.sum(ep, axis=-1, keepdims=True)
        inv = 1.0 / den
        pc_s[...] = (ec * inv).astype(BF16)
        pp_s[...] = (ep * inv).astype(BF16)
        lse = m + jnp.log(den)
        for n in range(nb):
            rs = slice(n * CHUNK, (n + 1) * CHUNK)
            for h in range(N_HEAD):
                cs = slice(h * HEAD, (h + 1) * HEAD)
                bs = slice((n * N_HEAD + h) * CHUNK, (n * N_HEAD + h + 1) * CHUNK)
                o = _dot(pc_s[bs, :], v_ref[rs, cs]) + _dot(pp_s[bs, :], prev_kv(n, cs)[1])
                o_ref[rs, cs] = o.astype(BF16)
            l_ref[rs, :] = _put_cols([lse[(n * N_HEAD + h) * CHUNK:(n * N_HEAD + h + 1) * CHUNK]
                                      for h in range(N_HEAD)])

    def halo(cb):
        return pl.BlockSpec((CHUNK, WIDTH), lambda i: (jnp.maximum(i * nb - 1, 0), cb))

    return pl.pallas_call(
        body, name=f"attn_fwd_{bps}",
        grid=(S // tm,),
        in_specs=[_rows(tm, WIDTH, qcb), _rows(tm, WIDTH, kcb), _rows(tm, WIDTH, vcb), halo(kcb), halo(vcb)],
        out_specs=[_rows(tm, WIDTH), _rows(tm, 128)],
        out_shape=[jax.ShapeDtypeStruct((S, WIDTH), BF16), jax.ShapeDtypeStruct((S, 128), F32)],
        scratch_shapes=[pltpu.VMEM((nblocks * CHUNK, CHUNK), F32), pltpu.VMEM((nblocks * CHUNK, CHUNK), F32),
                        pltpu.VMEM((nblocks * CHUNK, CHUNK), BF16), pltpu.VMEM((nblocks * CHUNK, CHUNK), BF16)],
        compiler_params=_params(("parallel",)),
    )(q, k, v, k, v)


def _gate_specs(tm):
    return [pl.BlockSpec((tm, D_MODEL), lambda i, b=b: (i, b)) for b in range(4)]


Y_SLOT = (0, 1, 3, 2)


def _merge_fwd(x, y4, o_g, l_g, proj, gates, wb, wout, shards=()):
    S = x.shape[0]
    tm = 256
    n = len(shards)
    nsteps = S // tm
    forward_at = nsteps - 2

    def body(x_ref, y_ref, o0, o1, o2, l0, l1, l2, cg_ref, *rest):
        gm = rest[:4]
        wb_ref, wo_ref = rest[4:6]
        s_in = rest[6:6 + n]
        xn_ref, yc_ref, oc_ref, lse_ref, z_ref = rest[6 + n:11 + n]
        s_out, ocs, sems = rest[11 + n:11 + 2 * n], rest[11 + 2 * n], rest[12 + 2 * n:]
        i = pl.program_id(0)

        if n:
            @pl.when(i == 0)
            def _():
                _comm_start(_ag_first(s_in, s_out, *sems[:3]))

            @pl.when(i == forward_at)
            def _():
                incoming = _ag_first(s_in, s_out, *sems[:3])[2]
                for a in range(n):
                    for k in range(1, 4):
                        incoming[4 * a + k].wait_recv()
                _comm_start(_ag_second(s_out, *sems[3:]))

        lcols = []
        for h in range(N_HEAD):
            cs = slice(h * HEAD, (h + 1) * HEAD)
            ls = [_col(l[...], h) for l in (l0, l1, l2)]
            m = jnp.maximum(jnp.maximum(ls[0], ls[1]), ls[2])
            tot = jnp.exp(ls[0] - m) + jnp.exp(ls[1] - m) + jnp.exp(ls[2] - m)
            lse = m + jnp.log(tot)
            ocs[:, cs] = sum(jnp.exp(lg - lse) * o[:, cs].astype(F32) for lg, o in zip(ls, (o0, o1, o2)))
            lcols.append(lse)
        lse_ref[...] = _put_cols(lcols)
        oc = ocs[...]
        oc_ref[...] = oc.astype(BF16)
        yc = (oc * _silu(cg_ref[...].astype(F32))).astype(BF16)
        yc_ref[...] = yc
        ys = (y_ref[0], y_ref[1], yc, y_ref[2])
        z = jnp.zeros((tm, D_MODEL), F32)
        for b in range(4):
            z = z + _sigmoid(gm[b][...].astype(F32)) * _dot(ys[b], wb_ref[b])
        zb = z.astype(BF16)
        z_ref[...] = zb
        xn_ref[...] = x_ref[...] + _dot(zb, wo_ref[...])

        if n:
            @pl.when(i == nsteps - 1)
            def _():
                local, out, incoming = _ag_first(s_in, s_out, *sems[:3])
                for a in range(n):
                    incoming[4 * a].wait_recv()
                _comm_wait(_ag_second(s_out, *sems[3:]))
                for cp in out:
                    cp.wait_send()
                for cp in local:
                    cp.wait()

    res = pl.pallas_call(
        body, name="merge_fwd_gather" if n else "merge_fwd",
        grid=(nsteps,),
        in_specs=[_rows(tm, D_MODEL), pl.BlockSpec((3, tm, WIDTH), lambda i: (0, i, 0)),
                  _rows(tm, WIDTH), _rows(tm, WIDTH), _rows(tm, WIDTH),
                  _rows(tm, 128), _rows(tm, 128), _rows(tm, 128),
                  _rows(tm, WIDTH, CB_CGATE)] + _gate_specs(tm)
                 + [_full((4, WIDTH, D_MODEL)), _full((D_MODEL, D_MODEL))] + [ANY] * n,
        out_specs=[_rows(tm, D_MODEL), pl.BlockSpec((None, tm, WIDTH), lambda i: (Y_SLOT[2], i, 0)),
                   _rows(tm, WIDTH), _rows(tm, 128), _rows(tm, D_MODEL)] + [ANY] * n,
        out_shape=[jax.ShapeDtypeStruct((S, D_MODEL), F32), jax.ShapeDtypeStruct(y4.shape, BF16),
                   jax.ShapeDtypeStruct((S, WIDTH), BF16), jax.ShapeDtypeStruct((S, 128), F32),
                   jax.ShapeDtypeStruct((S, D_MODEL), BF16)]
                  + [jax.ShapeDtypeStruct((N_DEV,) + s.shape, s.dtype) for s in shards],
        input_output_aliases={1: 1},
        scratch_shapes=[pltpu.VMEM((tm, WIDTH), F32)] + (_dma_sems(4 * n, 4 * n, n, 3 * n, 3 * n) if n else []),
        compiler_params=_params(("arbitrary",)),
    )(x, y4, *o_g, *l_g, proj, *([gates] * 4), wb, wout, *shards)
    return res[:5], list(res[5:])


def _loss_head(x, g, tgt):
    S = x.shape[0]
    tm = 512

    def body(x_ref, g_ref, t_ref, loss_ref, dx_ref, dg_ref):
        @pl.when(pl.program_id(0) == 0)
        def _():
            loss_ref[...] = jnp.zeros_like(loss_ref)
            dg_ref[...] = jnp.zeros_like(dg_ref)

        xf = x_ref[...]
        r = lax.rsqrt(jnp.mean(xf * xf, axis=-1, keepdims=True) + EPS)
        xhat = xf * r
        gv = g_ref[...]
        err = xhat * gv - t_ref[...]
        e2 = jnp.sum(err * err, axis=-1, keepdims=True)
        loss_ref[...] += (0.5 / D_MODEL) * jnp.sum(e2, axis=0, keepdims=True)
        dy = err * (1.0 / D_MODEL)
        dg_ref[...] += jnp.sum(dy * xhat, axis=0, keepdims=True)
        dxh = dy * gv
        dx_ref[...] = r * (dxh - xhat * jnp.mean(dxh * xhat, axis=-1, keepdims=True))

    return pl.pallas_call(
        body, name="loss_head",
        grid=(S // tm,),
        in_specs=[_rows(tm, D_MODEL), _full((1, D_MODEL)), _rows(tm, D_MODEL)],
        out_specs=[_full((1, 128)), _rows(tm, D_MODEL), _full((1, D_MODEL))],
        out_shape=[jax.ShapeDtypeStruct((1, 128), F32), jax.ShapeDtypeStruct((S, D_MODEL), F32),
                   jax.ShapeDtypeStruct((1, D_MODEL), F32)],
        compiler_params=_params(("arbitrary",)),
    )(x, g, tgt)


def _merge_bwd(dxo, y4, oc, z, proj, gates, wb, wout, grads=()):
    S = dxo.shape[0]
    tm = 256
    n = len(grads)
    nsteps = S // tm

    def body(dx_ref, y_ref, oc_ref, z_ref, cg_ref, *rest):
        gm = rest[:4]
        wb_ref, wo_ref = rest[4:6]
        g_in = rest[6:6 + n]
        dy_ref, doc_ref, delta_ref, dcg_ref, dgm_ref, dwb_ref, dwo_ref = rest[6 + n:13 + n]
        g_out = rest[13 + n:13 + 2 * n]
        acc_b, acc_o = rest[13 + 2 * n:15 + 2 * n]
        sems = rest[15 + 2 * n:]
        i = pl.program_id(0)

        @pl.when(i == 0)
        def _():
            acc_b[...] = jnp.zeros_like(acc_b)
            acc_o[...] = jnp.zeros_like(acc_o)
            if n:
                _comm_start(_rs_first(g_in, g_out, *sems))

        dxb = dx_ref[...].astype(BF16)
        acc_o[...] += _dot_tn(z_ref[...], dxb)
        dz = _dot_nt(dxb, wo_ref[...])
        for b in range(4):
            gate = _sigmoid(gm[b][...].astype(F32))
            yb = y_ref[Y_SLOT[b]]
            t = _dot(yb, wb_ref[b])
            dgm_ref[:, b * D_MODEL:(b + 1) * D_MODEL] = (dz * t * gate * (1.0 - gate)).astype(BF16)
            dt = (dz * gate).astype(BF16)
            acc_b[b] += _dot_tn(yb, dt)
            dyb = _dot_nt(dt, wb_ref[b])
            if b == 2:
                cg = cg_ref[...].astype(F32)
                oc = oc_ref[...].astype(F32)
                scg, dscg = _silu_and_grad(cg)
                doc = dyb * scg
                dcg_ref[...] = (dyb * oc * dscg).astype(BF16)
                doc_ref[...] = doc.astype(BF16)
                prod = doc * oc
                delta_ref[...] = _put_cols([jnp.sum(prod[:, h * HEAD:(h + 1) * HEAD], axis=1, keepdims=True)
                                            for h in range(N_HEAD)])
            else:
                dy_ref[b if b < 2 else 2] = dyb.astype(BF16)

        @pl.when(i == nsteps - 1)
        def _():
            dwb_ref[...] = acc_b[...].astype(BF16)
            dwo_ref[...] = acc_o[...].astype(BF16)
            if n:
                _comm_wait(_rs_first(g_in, g_out, *sems))

    def resident(shape):
        nd = len(shape)
        return pl.BlockSpec(shape, lambda i: (0,) * nd, pipeline_mode=pl.Buffered(1))

    res = pl.pallas_call(
        body, name="merge_bwd_scatter" if n else "merge_bwd",
        grid=(nsteps,),
        in_specs=[_rows(tm, D_MODEL), pl.BlockSpec((4, tm, WIDTH), lambda i: (0, i, 0)),
                  _rows(tm, WIDTH), _rows(tm, D_MODEL), _rows(tm, WIDTH, CB_CGATE)] + _gate_specs(tm)
                 + [resident((4, WIDTH, D_MODEL)), resident((D_MODEL, D_MODEL))] + [ANY] * n,
        out_specs=[pl.BlockSpec((3, tm, WIDTH), lambda i: (0, i, 0)), _rows(tm, WIDTH), _rows(tm, 128),
                   _rows(tm, WIDTH), _rows(tm, 4 * D_MODEL), _full((4, WIDTH, D_MODEL)), _full((D_MODEL, D_MODEL))]
                  + [ANY] * n,
        out_shape=[jax.ShapeDtypeStruct((3, S, WIDTH), BF16), jax.ShapeDtypeStruct((S, WIDTH), BF16),
                   jax.ShapeDtypeStruct((S, 128), F32), jax.ShapeDtypeStruct((S, WIDTH), BF16),
                   jax.ShapeDtypeStruct((S, 4 * D_MODEL), BF16), jax.ShapeDtypeStruct((4, WIDTH, D_MODEL), BF16),
                   jax.ShapeDtypeStruct((D_MODEL, D_MODEL), BF16)]
                  + [jax.ShapeDtypeStruct(g.shape[:1] + g.shape[2:], g.dtype) for g in grads],
        scratch_shapes=[pltpu.VMEM((4, WIDTH, D_MODEL), F32), pltpu.VMEM((D_MODEL, D_MODEL), F32)]
                       + (_dma_sems(N_CHIP * n, N_CHIP * n) if n else []),
        compiler_params=_params(("arbitrary",)),
    )(dxo, y4, oc, z, proj, *([gates] * 4), wb, wout, *grads)
    return res[:7], list(res[7:])


def _attn_bwd(q, qcb, k, kcb, v, vcb, do, lse, delta, bps, dst, dcb):
    S = q.shape[0]
    tm = ATT_TILE
    nb = tm // CHUNK
    nblk = S // CHUNK

    ncur = nb * N_HEAD
    nprev = (nb + 1) * N_HEAD

    def body(q_ref, k_ref, v_ref, do_ref, l_ref, d_ref, kh_ref, vh_ref, qn_ref, don_ref, ln_ref, dn_ref, dst_ref,
             dq_ref, dk_ref, dv_ref, sc_s, sp_s, dpc_s, dpp_s, pc_s, pp_s, dsc_s, dsp_s):
        i = pl.program_id(0)

        def rows_of(n):
            if n < nb:
                rs = slice(n * CHUNK, (n + 1) * CHUNK)
                return rs, q_ref, do_ref, l_ref, d_ref
            return slice(0, CHUNK), qn_ref, don_ref, ln_ref, dn_ref

        def prev_kv(n, cs):
            if n == 0:
                return kh_ref[:, cs], vh_ref[:, cs]
            ps = slice((n - 1) * CHUNK, n * CHUNK)
            return k_ref[ps, cs], v_ref[ps, cs]

        def blk(n, h):
            return slice((n * N_HEAD + h) * CHUNK, (n * N_HEAD + h + 1) * CHUNK)

        pens, lses, deltas = [], [], []
        for n in range(nb + 1):
            rs, qr, dor, lr, dr = rows_of(n)
            gb = i * nb + n
            pen = jnp.where(gb % bps != 0, 0.0, NEG)
            if n == nb:
                pen = pen + jnp.where(gb < nblk, 0.0, NEG)
            pens.append(jnp.full((N_HEAD * CHUNK, 1), pen, F32))
            lblk, dblk = lr[rs, :], dr[rs, :]
            for h in range(N_HEAD):
                cs = slice(h * HEAD, (h + 1) * HEAD)
                qh, doh = qr[rs, cs], dor[rs, cs]
                lses.append(_col(lblk, h))
                deltas.append(_col(dblk, h))
                kp, vp = prev_kv(n, cs)
                sp_s[blk(n, h), :] = _dot_nt(qh, kp)
                dpp_s[blk(n, h), :] = _dot_nt(doh, vp)
                if n < nb:
                    sc_s[blk(n, h), :] = _dot_nt(qh, k_ref[rs, cs])
                    dpc_s[blk(n, h), :] = _dot_nt(doh, v_ref[rs, cs])
        lse = jnp.concatenate(lses, axis=0)
        delta = jnp.concatenate(deltas, axis=0)
        row = lax.broadcasted_iota(jnp.int32, (nprev * CHUNK, CHUNK), 0) & (CHUNK - 1)
        col = lax.broadcasted_iota(jnp.int32, (nprev * CHUNK, CHUNK), 1)
        sp = jnp.where(col >= row, sp_s[...] * ATT_SCALE, NEG) + jnp.concatenate(pens, axis=0)
        pp = jnp.exp(sp - lse)
        pp_s[...] = pp.astype(BF16)
        dsp_s[...] = (pp * (dpp_s[...] - delta)).astype(BF16)
        nc = ncur * CHUNK
        sc = jnp.where(col[:nc] <= row[:nc], sc_s[...] * ATT_SCALE, NEG)
        pc = jnp.exp(sc - lse[:nc])
        pc_s[...] = pc.astype(BF16)
        dsc_s[...] = (pc * (dpc_s[...] - delta[:nc])).astype(BF16)
        for n in range(nb):
            rs, qr, dor, _, _ = rows_of(n)
            rn, qnr, donr, _, _ = rows_of(n + 1)
            for h in range(N_HEAD):
                cs = slice(h * HEAD, (h + 1) * HEAD)
                kp, _ = prev_kv(n, cs)
                dq = _dot(dsc_s[blk(n, h), :], k_ref[rs, cs]) + _dot(dsp_s[blk(n, h), :], kp)
                dq_ref[rs, cs] = (dq * ATT_SCALE).astype(BF16)
                dk = _dot_tn(dsc_s[blk(n, h), :], qr[rs, cs]) + _dot_tn(dsp_s[blk(n + 1, h), :], qnr[rn, cs])
                dk_ref[rs, cs] = (dk * ATT_SCALE).astype(BF16)
                dv = _dot_tn(pc_s[blk(n, h), :], dor[rs, cs]) + _dot_tn(pp_s[blk(n + 1, h), :], donr[rn, cs])
                dv_ref[rs, cs] = dv.astype(BF16)

    def prev_halo(cb):
        return pl.BlockSpec((CHUNK, WIDTH), lambda i: (jnp.maximum(i * nb - 1, 0), cb))

    def next_halo(width, cb=0):
        return pl.BlockSpec((CHUNK, width), lambda i: (jnp.minimum(i * nb + nb, nblk - 1), cb))

    return pl.pallas_call(
        body, name=f"attn_bwd_{bps}",
        grid=(S // tm,),
        in_specs=[_rows(tm, WIDTH, qcb), _rows(tm, WIDTH, kcb), _rows(tm, WIDTH, vcb), _rows(tm, WIDTH),
                  _rows(tm, 128), _rows(tm, 128), prev_halo(kcb), prev_halo(vcb),
                  next_halo(WIDTH, qcb), next_halo(WIDTH), next_halo(128), next_halo(128), ANY],
        out_specs=[_rows(tm, WIDTH, dcb), _rows(tm, WIDTH), _rows(tm, WIDTH)],
        out_shape=[jax.ShapeDtypeStruct(dst.shape, BF16)] + [jax.ShapeDtypeStruct((S, WIDTH), BF16)] * 2,
        input_output_aliases={12: 0},
        scratch_shapes=[pltpu.VMEM((ncur * CHUNK, CHUNK), F32), pltpu.VMEM((nprev * CHUNK, CHUNK), F32),
                        pltpu.VMEM((ncur * CHUNK, CHUNK), F32), pltpu.VMEM((nprev * CHUNK, CHUNK), F32),
                        pltpu.VMEM((ncur * CHUNK, CHUNK), BF16), pltpu.VMEM((nprev * CHUNK, CHUNK), BF16),
                        pltpu.VMEM((ncur * CHUNK, CHUNK), BF16), pltpu.VMEM((nprev * CHUNK, CHUNK), BF16)],
        compiler_params=_params(("parallel",)),
    )(q, k, v, do, lse, delta, k, v, q, do, lse, delta, dst)


def _dilated_split(d):
    hp = min(N_HEAD, 16 // d)
    return hp, N_HEAD // hp, HEAD * hp


def _strided_regroup(d):
    return d < 16


def _by_class(src_ref, dst, d, hp, nat):
    for j in range(hp):
        if _strided_regroup(d):
            nat[j] = src_ref[:, j * HEAD:(j + 1) * HEAD].astype(F32)
            for r in range(d):
                dst[j, r * CHUNK:(r + 1) * CHUNK, :] = nat.at[j][pl.ds(r, CHUNK, stride=d), :].astype(BF16)
        else:
            dst[j] = pltpu.einshape("(tr)l->(rt)l", src_ref[:, j * HEAD:(j + 1) * HEAD], r=d)


def _from_class(src, dst_ref, d, hp, nat, add_ref=None):
    for j in range(hp):
        cs = slice(j * HEAD, (j + 1) * HEAD)
        if _strided_regroup(d):
            for r in range(d):
                nat.at[j][pl.ds(r, CHUNK, stride=d), :] = src[j, r * CHUNK:(r + 1) * CHUNK, :]
            val = nat[j].astype(BF16)
        else:
            val = pltpu.einshape("(rt)l->(tr)l", src[j].astype(BF16), r=d)
        if add_ref is not None:
            val = (val.astype(F32) + add_ref[:, cs].astype(F32)).astype(BF16)
        dst_ref[:, cs] = val


def _attn_fwd_dilated(proj, qcb, kcb, vcb, d):
    S = proj.shape[0]
    T = CHUNK * d
    hp, nh, cw = _dilated_split(d)
    nblocks = d * hp

    def body(q_ref, k_ref, v_ref, o_ref, l_ref, qf, kst, vst, of, lf, nat, sc_s, sp_s, pc_s, pp_s):
        i, hh = pl.program_id(0), pl.program_id(1)
        kf, vf = kst.at[i % 2, hh], vst.at[i % 2, hh]
        kpf, vpf = kst.at[1 - i % 2, hh], vst.at[1 - i % 2, hh]

        @pl.when(i == 0)
        def _():
            kpf[...] = jnp.zeros_like(kpf)
            vpf[...] = jnp.zeros_like(vpf)

        _by_class(q_ref, qf, d, hp, nat)
        _by_class(k_ref, kf, d, hp, nat)
        _by_class(v_ref, vf, d, hp, nat)

        def blk(ref, r, j):
            return ref[j, r * CHUNK:(r + 1) * CHUNK, :]

        def bs(r, j):
            return slice((r * hp + j) * CHUNK, (r * hp + j + 1) * CHUNK)

        for r in range(d):
            for j in range(hp):
                qb = blk(qf, r, j)
                sc_s[bs(r, j), :] = _dot_nt(qb, blk(kf, r, j))
                sp_s[bs(r, j), :] = _dot_nt(qb, blk(kpf, r, j))
        row = lax.broadcasted_iota(jnp.int32, (nblocks * CHUNK, CHUNK), 0) & (CHUNK - 1)
        col = lax.broadcasted_iota(jnp.int32, (nblocks * CHUNK, CHUNK), 1)
        sc = jnp.where(col <= row, sc_s[...] * ATT_SCALE, NEG)
        sp = jnp.where(col >= row, sp_s[...] * ATT_SCALE, NEG) + jnp.where(i > 0, 0.0, NEG)
        m = jnp.maximum(jnp.max(sc, axis=-1, keepdims=True), jnp.max(sp, axis=-1, keepdims=True))
        ec = jnp.exp(sc - m)
        ep = jnp.exp(sp - m)
        den = jnp.sum(ec, axis=-1, keepdims=True) + jnp.sum(ep, axis=-1, keepdims=True)
        inv = 1.0 / den
        pc_s[...] = (ec * inv).astype(BF16)
        pp_s[...] = (ep * inv).astype(BF16)
        lse = m + jnp.log(den)
        lane = lax.broadcasted_iota(jnp.int32, (CHUNK, 128), 1)
        for r in range(d):
            lblk = jnp.zeros((CHUNK, 128), F32)
            for j in range(hp):
                o = _dot(pc_s[bs(r, j), :], blk(vf, r, j)) + _dot(pp_s[bs(r, j), :], blk(vpf, r, j))
                of[j, r * CHUNK:(r + 1) * CHUNK, :] = o
                lblk = jnp.where(lane == hh * hp + j, lse[bs(r, j)], lblk)
            lf[r * CHUNK:(r + 1) * CHUNK, :] = lblk
        _from_class(of, o_ref, d, hp, nat)
        lnat = pltpu.einshape("(rt)l->(tr)l", lf[...], r=d)

        @pl.when(hh == 0)
        def _():
            l_ref[...] = lnat

        @pl.when(hh > 0)
        def _():
            l_ref[...] += lnat

    def cols(cb):
        return pl.BlockSpec((T, cw), lambda i, hh: (i, cb * nh + hh))

    tile = pltpu.VMEM((hp, T, HEAD), BF16)
    staging = pltpu.VMEM((hp, T, HEAD) if _strided_regroup(d) else (1, 8, HEAD), F32)
    return pl.pallas_call(
        body, name=f"attn_fwd_dilated_{d}",
        grid=(S // T, nh),
        in_specs=[cols(qcb), cols(kcb), cols(vcb)],
        out_specs=[cols(0), pl.BlockSpec((T, 128), lambda i, hh: (i, 0))],
        out_shape=[jax.ShapeDtypeStruct((S, WIDTH), BF16), jax.ShapeDtypeStruct((S, 128), F32)],
        scratch_shapes=[tile, pltpu.VMEM((2, nh, hp, T, HEAD), BF16), pltpu.VMEM((2, nh, hp, T, HEAD), BF16),
                        pltpu.VMEM((hp, T, HEAD), F32), pltpu.VMEM((T, 128), F32), staging,
                        pltpu.VMEM((nblocks * CHUNK, CHUNK), F32), pltpu.VMEM((nblocks * CHUNK, CHUNK), F32),
                        pltpu.VMEM((nblocks * CHUNK, CHUNK), BF16), pltpu.VMEM((nblocks * CHUNK, CHUNK), BF16)],
        compiler_params=_params(("arbitrary", "arbitrary")),
    )(proj, proj, proj)


def _attn_bwd_dilated(proj, qcb, kcb, vcb, do, lse, delta, d, dk_in, dv_in, dst, dcb):
    S = proj.shape[0]
    T = CHUNK * d
    nt = S // T
    hp, nh, cw = _dilated_split(d)
    nblocks = d * hp

    def body(q_ref, k_ref, v_ref, do_ref, l_ref, d_ref, dki_ref, dvi_ref, dst_ref, dq_ref, dk_ref, dv_ref,
             qf, dof, kbuf, vbuf, dqf, gk, gv, nat,
             sc_s, sp_s, dpc_s, dpp_s, pc_s, pp_s, dsc_s, dsp_s):
        hh, i = pl.program_id(0), pl.program_id(1)
        kf, vf, newk, newv = kbuf.at[i % 2], vbuf.at[i % 2], gk.at[i % 2], gv.at[i % 2]
        kpf, vpf, acck, accv = kbuf.at[1 - i % 2], vbuf.at[1 - i % 2], gk.at[1 - i % 2], gv.at[1 - i % 2]

        @pl.when(i == 0)
        def _():
            for ref in (kbuf, vbuf, gk, gv):
                ref[...] = jnp.zeros_like(ref)
            dk_ref[...] = jnp.zeros_like(dk_ref)
            dv_ref[...] = jnp.zeros_like(dv_ref)

        def blk(ref, r, j):
            return ref[j, r * CHUNK:(r + 1) * CHUNK, :]

        def bs(r, j):
            return slice((r * hp + j) * CHUNK, (r * hp + j + 1) * CHUNK)

        @pl.when(i < nt)
        def _():
            _by_class(q_ref, qf, d, hp, nat)
            _by_class(do_ref, dof, d, hp, nat)
            _by_class(k_ref, kf, d, hp, nat)
            _by_class(v_ref, vf, d, hp, nat)
            lses, deltas = [], []
            lcls = pltpu.einshape("(tr)l->(rt)l", l_ref[...], r=d)
            dcls = pltpu.einshape("(tr)l->(rt)l", d_ref[...], r=d)
            for r in range(d):
                lblk = lcls[r * CHUNK:(r + 1) * CHUNK]
                dblk = dcls[r * CHUNK:(r + 1) * CHUNK]
                for j in range(hp):
                    lses.append(_col(lblk, hh * hp + j))
                    deltas.append(_col(dblk, hh * hp + j))
                    qb, dob = blk(qf, r, j), blk(dof, r, j)
                    sc_s[bs(r, j), :] = _dot_nt(qb, blk(kf, r, j))
                    dpc_s[bs(r, j), :] = _dot_nt(dob, blk(vf, r, j))
                    sp_s[bs(r, j), :] = _dot_nt(qb, blk(kpf, r, j))
                    dpp_s[bs(r, j), :] = _dot_nt(dob, blk(vpf, r, j))
            lse = jnp.concatenate(lses, axis=0)
            delta = jnp.concatenate(deltas, axis=0)
            row = lax.broadcasted_iota(jnp.int32, (nblocks * CHUNK, CHUNK), 0) & (CHUNK - 1)
            col = lax.broadcasted_iota(jnp.int32, (nblocks * CHUNK, CHUNK), 1)
            sp = jnp.where(col >= row, sp_s[...] * ATT_SCALE, NEG) + jnp.where(i > 0, 0.0, NEG)
            pp = jnp.exp(sp - lse)
            pp_s[...] = pp.astype(BF16)
            dsp_s[...] = (pp * (dpp_s[...] - delta)).astype(BF16)
            sc = jnp.where(col <= row, sc_s[...] * ATT_SCALE, NEG)
            pc = jnp.exp(sc - lse)
            pc_s[...] = pc.astype(BF16)
            dsc_s[...] = (pc * (dpc_s[...] - delta)).astype(BF16)
            for r in range(d):
                rows = slice(r * CHUNK, (r + 1) * CHUNK)
                for j in range(hp):
                    qb, dob = blk(qf, r, j), blk(dof, r, j)
                    dsc, dsp = dsc_s[bs(r, j), :], dsp_s[bs(r, j), :]
                    dqf[j, rows, :] = (_dot(dsc, blk(kf, r, j)) + _dot(dsp, blk(kpf, r, j))) * ATT_SCALE
                    newk[j, rows, :] = _dot_tn(dsc, qb) * ATT_SCALE
                    newv[j, rows, :] = _dot_tn(pc_s[bs(r, j), :], dob)
                    acck[j, rows, :] += _dot_tn(dsp, qb) * ATT_SCALE
                    accv[j, rows, :] += _dot_tn(pp_s[bs(r, j), :], dob)
            _from_class(dqf, dq_ref, d, hp, nat)

        @pl.when(i > 0)
        def _():
            _from_class(acck, dk_ref, d, hp, nat, dki_ref)
            _from_class(accv, dv_ref, d, hp, nat, dvi_ref)

    def cur(width, cb, nsplit):
        return pl.BlockSpec((T, width), lambda hh, i: (jnp.minimum(i, nt - 1), cb * nsplit + hh * (nsplit > 1)))

    def lag():
        return pl.BlockSpec((T, cw), lambda hh, i: (jnp.maximum(i - 1, 0), hh))

    tile = pltpu.VMEM((hp, T, HEAD), BF16)
    acc = pltpu.VMEM((hp, T, HEAD), F32)
    f32s = pltpu.VMEM((nblocks * CHUNK, CHUNK), F32)
    b16s = pltpu.VMEM((nblocks * CHUNK, CHUNK), BF16)
    return pl.pallas_call(
        body, name=f"attn_bwd_dilated_{d}",
        grid=(nh, nt + 1),
        in_specs=[cur(cw, qcb, nh), cur(cw, kcb, nh), cur(cw, vcb, nh), cur(cw, 0, nh), cur(128, 0, 1), cur(128, 0, 1),
                  lag(), lag(), ANY],
        out_specs=[cur(cw, dcb, nh), lag(), lag()],
        out_shape=[jax.ShapeDtypeStruct(dst.shape, BF16)] + [jax.ShapeDtypeStruct((S, WIDTH), BF16)] * 2,
        input_output_aliases={8: 0},
        scratch_shapes=[tile, tile, pltpu.VMEM((2, hp, T, HEAD), BF16), pltpu.VMEM((2, hp, T, HEAD), BF16), acc,
                        pltpu.VMEM((2, hp, T, HEAD), F32), pltpu.VMEM((2, hp, T, HEAD), F32),
                        acc if _strided_regroup(d) else pltpu.VMEM((1, 8, HEAD), F32)]
                       + [f32s] * 4 + [b16s] * 4,
        compiler_params=_params(("arbitrary", "arbitrary")),
    )(proj, proj, proj, do, lse, delta, dk_in, dv_in, dst)


def _abm_bwd(proj, cdf, dy3, ln_g, ln_b, wsm, wsm_t, bias_full, pool_w, pool_wt, pool_scale, kv, bands, bands_t):
    S = proj.shape[0]
    tm = 512
    nchunk = tm // CHUNK
    nblk = S // CHUNK

    def body(u_ref, v_ref, ag_ref, p_ref, ph_ref, pg_ref, pgn_ref, mq_ref, mg_ref, cdf_ref, dy_ref, dypn_ref,
             lng_ref, lnb_ref, wsm_ref, wsmt_ref, bias_ref, pw_ref, pwt_ref, ps_ref, kv_ref, band_ref, bandt_ref,
             dab_ref, dm_ref, dlng_ref, dlnb_ref, dws_ref, dbias_ref, dpw_ref, dps_ref, dkv_ref,
             mix, dvl, ddn):
        i = pl.program_id(0)

        @pl.when(i == 0)
        def _():
            for r in (dlng_ref, dlnb_ref, dws_ref, dbias_ref, dpw_ref, dps_ref, dkv_ref):
                r[...] = jnp.zeros_like(r)

        au = u_ref[...].astype(F32)
        av = v_ref[...].astype(F32)
        ag = ag_ref[...].astype(F32)
        u, du = _gelu_and_grad(au, cdf_ref[0].astype(F32))
        v, dgelu_v = _gelu_and_grad(av, cdf_ref[1].astype(F32))
        vhat, rstd = _layer_norm_fwd(v)
        vln = (vhat * lng_ref[...] + lnb_ref[...]).astype(BF16)
        for c in range(nchunk):
            for h in range(N_HEAD):
                rs, cs = slice(c * CHUNK, (c + 1) * CHUNK), slice(h * HEAD, (h + 1) * HEAD)
                mix[rs, cs] = _dot(wsm_ref[h], vln[rs, cs]) + bias_ref[:, cs]
        dya = dy_ref[0].astype(F32)
        sg, dsg = _silu_and_grad(ag)
        mixed = mix[...]
        dab_ref[:, 2 * WIDTH:3 * WIDTH] = (dya * u * mixed * dsg).astype(BF16)
        dab_ref[:, 0:WIDTH] = (dya * mixed * sg * du).astype(BF16)
        dmixed = dya * u * sg
        dmb = dmixed.astype(BF16)
        tril = (lax.broadcasted_iota(jnp.int32, (CHUNK, CHUNK), 1)
                <= lax.broadcasted_iota(jnp.int32, (CHUNK, CHUNK), 0))
        for c in range(nchunk):
            rs = slice(c * CHUNK, (c + 1) * CHUNK)
            dbias_ref[...] += dmixed[rs, :]
            for h in range(N_HEAD):
                cs = slice(h * HEAD, (h + 1) * HEAD)
                dvl[rs, cs] = _dot(wsmt_ref[h], dmb[rs, cs])
                dws_ref[h] += jnp.where(tril, _dot_nt(dmb[rs, cs], vln[rs, cs]), 0.0)
        dvln = dvl[...]
        dlng_ref[...] += jnp.sum(dvln * vhat, axis=0, keepdims=True)
        dlnb_ref[...] += jnp.sum(dvln, axis=0, keepdims=True)
        dvh = dvln * lng_ref[...]
        dv = rstd * (dvh - jnp.mean(dvh, axis=-1, keepdims=True)
                     - vhat * jnp.mean(dvh * vhat, axis=-1, keepdims=True))
        dab_ref[:, WIDTH:2 * WIDTH] = (dv * dgelu_v).astype(BF16)

        halo_ok = (i > 0).astype(F32)
        for c in range(nchunk):
            rs = slice(c * CHUNK, (c + 1) * CHUNK)
            for g, win in enumerate(POOL_WINDOWS):
                cs = slice(g * HEAD, (g + 1) * HEAD)
                cur = p_ref[rs, cs]
                if c == 0:
                    prev = (ph_ref[:, cs].astype(F32) * halo_ok).astype(BF16)
                else:
                    prev = p_ref[(c - 1) * CHUNK:c * CHUNK, cs]
                sums = _dot(band_ref[g, 0], cur) + _dot(band_ref[g, 1], prev)
                dvl[rs, cs] = sums * _inv_count(i * tm + c * CHUNK, win) - cur.astype(F32)
        dmat = dvl[...].astype(BF16)
        for g in range(4):
            cs = slice(g * HEAD, (g + 1) * HEAD)
            mix[:, cs] = _dot(dmat[:, cs], pw_ref[g])
        yg = mix[...]
        pg = pg_ref[...].astype(F32)
        dyp = dy_ref[1].astype(F32)
        spg, dspg = _silu_and_grad(pg)
        dyy = dyp * spg
        scale = ps_ref[...]
        dab_ref[:, 4 * WIDTH:5 * WIDTH] = (dyp * yg * scale * dspg).astype(BF16)
        dps_ref[...] += jnp.sum(dyy * yg, axis=0, keepdims=True)
        dyg = (dyy * scale).astype(BF16)
        for g in range(4):
            cs = slice(g * HEAD, (g + 1) * HEAD)
            dpw_ref[g] += _dot_tn(dmat[:, cs], dyg[:, cs])
            mix[:, cs] = _dot(dyg[:, cs], pwt_ref[g])
        next_ok = (i + 1 < S // tm).astype(F32)
        dygn = (dypn_ref[...].astype(F32) * _silu(pgn_ref[...].astype(F32)) * scale * next_ok).astype(BF16)
        for c in range(nchunk + 1):
            for g, win in enumerate(POOL_WINDOWS):
                cs = slice(g * HEAD, (g + 1) * HEAD)
                if c < nchunk:
                    dd = mix[c * CHUNK:(c + 1) * CHUNK, cs]
                else:
                    dd = _dot(dygn[:, cs], pwt_ref[g])
                ddn[c * CHUNK:(c + 1) * CHUNK, cs] = dd * _inv_count(i * tm + c * CHUNK, win)
        ddnb = ddn[...].astype(BF16)
        for c in range(nchunk):
            rs = slice(c * CHUNK, (c + 1) * CHUNK)
            ns = slice((c + 1) * CHUNK, (c + 2) * CHUNK)
            for g, win in enumerate(POOL_WINDOWS):
                cs = slice(g * HEAD, (g + 1) * HEAD)
                dp = _dot(bandt_ref[g, 0], ddnb[rs, cs]) + _dot(bandt_ref[g, 1], ddnb[ns, cs]) - mix[rs, cs]
                dab_ref[rs, 3 * WIDTH + g * HEAD:3 * WIDTH + (g + 1) * HEAD] = dp.astype(BF16)

        mg = mg_ref[...].astype(F32)
        dym = dy_ref[2].astype(F32)
        smg, dsmg = _silu_and_grad(mg)
        dob = (dym * smg).astype(BF16)
        for h in range(N_HEAD):
            cs = slice(h * HEAD, (h + 1) * HEAD)
            vs = slice(WIDTH + h * HEAD, WIDTH + (h + 1) * HEAD)
            qh = mq_ref[:, cs]
            p = _mem_softmax(qh, kv_ref[:, cs])
            pb = p.astype(BF16)
            mix[:, cs] = _dot(pb, kv_ref[:, vs])
            dp = _dot_nt(dob[:, cs], kv_ref[:, vs])
            ds = (p * (dp - jnp.sum(p * dp, axis=-1, keepdims=True))).astype(BF16)
            dm_ref[:, cs] = (_dot(ds, kv_ref[:, cs]) * ATT_SCALE).astype(BF16)
            dkv_ref[:, cs] += _dot_tn(ds, qh) * ATT_SCALE
            dkv_ref[:, vs] += _dot_tn(pb, dob[:, cs])
        dm_ref[:, WIDTH:2 * WIDTH] = (dym * mix[...] * dsmg).astype(BF16)

    blk = tm // CHUNK
    small = [_full((1, WIDTH)), _full((1, WIDTH)), _full((N_HEAD, CHUNK, CHUNK)), _full((CHUNK, WIDTH)),
             _full((4, HEAD, HEAD)), _full((1, WIDTH)), _full((MEM_LEN, 2 * WIDTH))]
    return pl.pallas_call(
        body, name="abm_bwd",
        grid=(S // tm,),
        in_specs=[_rows(tm, WIDTH, CB_U), _rows(tm, WIDTH, CB_V), _rows(tm, WIDTH, CB_AGATE),
                  _rows(tm, WIDTH, CB_PIN),
                  pl.BlockSpec((CHUNK, WIDTH), lambda i: (jnp.maximum(i * blk - 1, 0), CB_PIN)),
                  _rows(tm, WIDTH, CB_PGATE),
                  pl.BlockSpec((CHUNK, WIDTH), lambda i: (jnp.minimum(i * blk + blk, nblk - 1), CB_PGATE)),
                  _rows(tm, WIDTH, CB_MQ), _rows(tm, WIDTH, CB_MGATE),
                  pl.BlockSpec((2, tm, WIDTH), lambda i: (0, i, 0)),
                  pl.BlockSpec((3, tm, WIDTH), lambda i: (0, i, 0)),
                  pl.BlockSpec((None, CHUNK, WIDTH), lambda i: (1, jnp.minimum(i * blk + blk, nblk - 1), 0)),
                  _full((1, WIDTH)), _full((1, WIDTH)), _full((N_HEAD, CHUNK, CHUNK)), _full((N_HEAD, CHUNK, CHUNK)),
                  _full((CHUNK, WIDTH)), _full((4, HEAD, HEAD)), _full((4, HEAD, HEAD)), _full((1, WIDTH)),
                  _full((MEM_LEN, 2 * WIDTH)), _full((4, 2, CHUNK, CHUNK)), _full((4, 2, CHUNK, CHUNK))],
        out_specs=[_rows(tm, 5 * WIDTH), _rows(tm, 2 * WIDTH)] + small,
        out_shape=[jax.ShapeDtypeStruct((S, D_BRANCHES), BF16), jax.ShapeDtypeStruct((S, 2 * WIDTH), BF16),
                   jax.ShapeDtypeStruct((1, WIDTH), F32), jax.ShapeDtypeStruct((1, WIDTH), F32),
                   jax.ShapeDtypeStruct((N_HEAD, CHUNK, CHUNK), F32), jax.ShapeDtypeStruct((CHUNK, WIDTH), F32),
                   jax.ShapeDtypeStruct((4, HEAD, HEAD), F32), jax.ShapeDtypeStruct((1, WIDTH), F32),
                   jax.ShapeDtypeStruct((MEM_LEN, 2 * WIDTH), F32)],
        scratch_shapes=[pltpu.VMEM((tm, WIDTH), F32), pltpu.VMEM((tm, WIDTH), F32),
                        pltpu.VMEM((tm + CHUNK, WIDTH), F32)],
        compiler_params=_params(("arbitrary",)),
    )(proj, proj, proj, proj, proj, proj, proj, proj, proj, cdf, dy3, dy3,
      ln_g, ln_b, wsm, wsm_t, bias_full, pool_w, pool_wt, pool_scale, kv, bands, bands_t)


def _bias_reduce(dbias_full):
    def body(d_ref, o_ref):
        d = d_ref[...]
        o_ref[...] = _put_cols([jnp.sum(d[:, h * HEAD:(h + 1) * HEAD], axis=1, keepdims=True) for h in range(N_HEAD)])

    return pl.pallas_call(body, name="bias_reduce", out_shape=jax.ShapeDtypeStruct((CHUNK, 128), F32))(dbias_full)


def _mem_bwd(mem, g, mem_n, w, dkv):
    def body(m_ref, g_ref, mn_ref, w_ref, dkv_ref, dw_ref, dg_ref):
        dkvb = dkv_ref[...].astype(BF16)
        dw_ref[...] = _dot_tn(mn_ref[...], dkvb).astype(BF16)
        dmn = _dot_nt(dkvb, w_ref[...])
        xf = m_ref[...]
        r = lax.rsqrt(jnp.mean(xf * xf, axis=-1, keepdims=True) + EPS)
        dg_ref[...] = jnp.sum(dmn * xf * r, axis=0, keepdims=True)

    return pl.pallas_call(
        body, name="mem_bwd",
        out_shape=[jax.ShapeDtypeStruct((D_MODEL, 2 * WIDTH), BF16), jax.ShapeDtypeStruct((1, D_MODEL), F32)],
        compiler_params=pltpu.CompilerParams(vmem_limit_bytes=VMEM_LIMIT),
    )(mem, g, mem_n, w, dkv)


def _dh_bwd(dpb, dpg, wt, wgt, x, g, dxo, parts=(), grads=()):
    S = x.shape[0]
    tm, tkb, tkg = 1024, D_BRANCHES // 4, D_GATES // 4
    nkb, nkg = 4, 4
    nk = nkb + nkg
    ni = S // tm
    n, m = len(parts), len(grads)

    def body(dpb_ref, wbr_ref, dpg_ref, wg_ref, x_ref, g_ref, dxo_ref, *rest):
        p_in, g_in = rest[:n], rest[n:n + m]
        dx_ref, dg_ref = rest[n + m:n + m + 2]
        p_out, g_out = rest[n + m + 2:2 * n + m + 2], rest[2 * n + m + 2:2 * (n + m) + 2]
        acc, sems = rest[2 * (n + m) + 2], rest[2 * (n + m) + 3:]
        second_sems, first_sems = (sems[:3] if n else ()), sems[3 if n else 0:]
        i, kk = pl.program_id(0), pl.program_id(1)

        @pl.when(jnp.logical_and(i == 0, kk == 0))
        def _():
            dg_ref[...] = jnp.zeros_like(dg_ref)
            if n:
                _comm_start(_rs_second(p_in, p_out, *second_sems))
            if m:
                _comm_start(_rs_first(g_in, g_out, *first_sems))

        @pl.when(kk == 0)
        def _():
            acc[...] = jnp.zeros_like(acc)

        @pl.when(kk < nkb)
        def _():
            acc[...] += _dot(dpb_ref[...], wbr_ref[...])

        @pl.when(kk >= nkb)
        def _():
            acc[...] += _dot(dpg_ref[...], wg_ref[...])

        @pl.when(kk == nk - 1)
        def _():
            xf = x_ref[...]
            r = lax.rsqrt(jnp.mean(xf * xf, axis=-1, keepdims=True) + EPS)
            xhat = xf * r
            dh = acc[...]
            dg_ref[...] += jnp.sum(dh * xhat, axis=0, keepdims=True)
            dxh = dh * g_ref[...]
            dx_ref[...] = dxo_ref[...] + r * (dxh - xhat * jnp.mean(dxh * xhat, axis=-1, keepdims=True))

        if n or m:
            @pl.when(jnp.logical_and(i == ni - 1, kk == nk - 1))
            def _():
                if n:
                    _comm_wait(_rs_second(p_in, p_out, *second_sems))
                if m:
                    _comm_wait(_rs_first(g_in, g_out, *first_sems))

    res = pl.pallas_call(
        body, name="dh_bwd_scatter" if (n or m) else "dh_bwd",
        grid=(ni, nk),
        in_specs=[pl.BlockSpec((tm, tkb), lambda i, k: (i, jnp.minimum(k, nkb - 1))),
                  pl.BlockSpec((tkb, D_MODEL), lambda i, k: (jnp.minimum(k, nkb - 1), 0)),
                  pl.BlockSpec((tm, tkg), lambda i, k: (i, jnp.maximum(k - nkb, 0))),
                  pl.BlockSpec((tkg, D_MODEL), lambda i, k: (jnp.maximum(k - nkb, 0), 0)),
                  pl.BlockSpec((tm, D_MODEL), lambda i, k: (i, 0)), pl.BlockSpec((1, D_MODEL), lambda i, k: (0, 0)),
                  pl.BlockSpec((tm, D_MODEL), lambda i, k: (i, 0))] + [ANY] * (n + m),
        out_specs=[pl.BlockSpec((tm, D_MODEL), lambda i, k: (i, 0)), pl.BlockSpec((1, D_MODEL), lambda i, k: (0, 0))]
                  + [ANY] * (n + m),
        out_shape=[jax.ShapeDtypeStruct((S, D_MODEL), F32), jax.ShapeDtypeStruct((1, D_MODEL), F32)]
                  + [jax.ShapeDtypeStruct(p.shape, p.dtype) for p in parts]
                  + [jax.ShapeDtypeStruct(gr.shape[:1] + gr.shape[2:], gr.dtype) for gr in grads],
        scratch_shapes=[pltpu.VMEM((tm, D_MODEL), F32)] + (_dma_sems(3 * n, 3 * n, n) if n else [])
                       + (_dma_sems(N_CHIP * m, N_CHIP * m) if m else []),
        compiler_params=_params(("arbitrary", "arbitrary")),
    )(dpb, wt, dpg, wgt, x, g, dxo, *parts, *grads)
    return res[0], res[1], list(res[2:2 + n]), list(res[2 + n:])


def _dw_in(h, dpb, dpg, parts=()):
    S = h.shape[0]
    tk = 2048
    nk = S // tk
    n = len(parts)
    tmb = D_BRANCHES // 4
    ng = D_GATES // GATE_TILE

    def accumulate(a_ref, h_ref, o_ref, acc):
        kk = pl.program_id(1)

        @pl.when(kk == 0)
        def _():
            acc[...] = jnp.zeros_like(acc)

        acc[...] += _dot_tn(a_ref[...], h_ref[...])

        @pl.when(kk == nk - 1)
        def _():
            o_ref[...] = acc[...].astype(BF16)

    def branches(a_ref, h_ref, *rest):
        p_in, o_ref, p_out = rest[:n], rest[n], rest[n + 1:2 * n + 1]
        acc, sems = rest[2 * n + 1], rest[2 * n + 2:]
        i, kk = pl.program_id(0), pl.program_id(1)

        if n:
            @pl.when(jnp.logical_and(i == 0, kk == 0))
            def _():
                _comm_start(_rs_second(p_in, p_out, *sems))

        accumulate(a_ref, h_ref, o_ref, acc)

        if n:
            @pl.when(jnp.logical_and(i == 3, kk == nk - 1))
            def _():
                _comm_wait(_rs_second(p_in, p_out, *sems))

    def gates(a_ref, h_ref, dst_ref, o_ref, acc):
        accumulate(a_ref, h_ref, o_ref, acc)

    res = pl.pallas_call(
        branches, name="dw_in_branches_scatter" if n else "dw_in_branches",
        grid=(4, nk),
        in_specs=[pl.BlockSpec((tk, tmb), lambda i, k: (k, i)), pl.BlockSpec((tk, D_MODEL), lambda i, k: (k, 0))]
                 + [ANY] * n,
        out_specs=[pl.BlockSpec((tmb, D_MODEL), lambda i, k: (i, 0))] + [ANY] * n,
        out_shape=[jax.ShapeDtypeStruct((D_IN, D_MODEL), BF16)]
                  + [jax.ShapeDtypeStruct(p.shape, p.dtype) for p in parts],
        scratch_shapes=[pltpu.VMEM((tmb, D_MODEL), F32)] + (_dma_sems(3 * n, 3 * n, n) if n else []),
        compiler_params=_params(("arbitrary", "arbitrary")),
    )(dpb, h, *parts)
    dwt = pl.pallas_call(
        gates, name="dw_in_gates",
        grid=(ng, nk),
        in_specs=[pl.BlockSpec((tk, GATE_TILE), lambda i, k: (k, i)), pl.BlockSpec((tk, D_MODEL), lambda i, k: (k, 0)),
                  ANY],
        out_specs=pl.BlockSpec((GATE_TILE, D_MODEL), lambda i, k: (D_BRANCHES // GATE_TILE + i, 0)),
        out_shape=jax.ShapeDtypeStruct((D_IN, D_MODEL), BF16),
        input_output_aliases={2: 0},
        scratch_shapes=[pltpu.VMEM((GATE_TILE, D_MODEL), F32)],
        compiler_params=_params(("parallel", "arbitrary")),
    )(dpg, h, res[0])
    return dwt, list(res[1:])


def _row_tile(R, C, block_bytes=2 << 20):
    for cand in range(min(R, block_bytes // (C * 4)) // 8 * 8, 0, -8):
        if R % cand == 0:
            return cand
    return R


def _adamw_update(p_ref, w_ref, m_ref, v_ref, g_ref, d_ref, nm_ref, nv_ref):
    c1 = 1.0 / (1.0 - ADAM_B1 ** ADAM_STEP)
    c2 = 1.0 / (1.0 - ADAM_B2 ** ADAM_STEP)
    g = p_ref[0].astype(F32)
    for k in range(1, p_ref.shape[0]):
        g = g + p_ref[k].astype(F32)
    nm = ADAM_B1 * m_ref[...] + (1.0 - ADAM_B1) * g
    nv = ADAM_B2 * v_ref[...] + (1.0 - ADAM_B2) * (g * g)
    g_ref[...] = g
    nm_ref[...] = nm
    nv_ref[...] = nv
    d_ref[...] = -ADAM_LR * ((nm * c1) / (jnp.sqrt(nv * c2) + ADAM_EPS) + ADAM_WD * w_ref[...])


def _adamw(parts, w, m, v, name):
    P, R, C = parts.shape
    tr = _row_tile(R, C)

    def body(*refs):
        _adamw_update(*refs)

    spec = pl.BlockSpec((tr, C), lambda i: (i, 0))
    return pl.pallas_call(
        body, name=name,
        grid=(R // tr,),
        in_specs=[pl.BlockSpec((P, tr, C), lambda i: (0, i, 0)), spec, spec, spec],
        out_specs=[spec] * 4,
        out_shape=[jax.ShapeDtypeStruct((R, C), F32)] * 4,
        compiler_params=_params(("parallel",)),
    )(parts, w, m, v)


def _adamw_layers(parts, w, m, v, name):
    depth = len(parts)
    P, R, C = parts[0].shape
    tr = _row_tile(R, C, 1 << 20)

    def body(*refs):
        layer = pl.program_id(0)
        for k in range(depth):
            @pl.when(layer == k)
            def _(k=k):
                _adamw_update(refs[k], *refs[depth:])

    def part_spec(k):
        return pl.BlockSpec((P, tr, C), lambda l, i: (0, jnp.where(l == k, i, 0), 0))

    spec = pl.BlockSpec((None, tr, C), lambda l, i: (l, i, 0))
    return pl.pallas_call(
        body, name=name,
        grid=(depth, R // tr),
        in_specs=[part_spec(k) for k in range(depth)] + [spec] * 3,
        out_specs=[spec] * 4,
        out_shape=[jax.ShapeDtypeStruct((depth, R, C), F32)] * 4,
        compiler_params=_params(("arbitrary", "arbitrary")),
    )(*parts, w, m, v)


def _place():
    return lax.axis_index("x"), lax.axis_index("y"), lax.axis_index("c")


def _all_gather(shards):
    n = len(shards)

    def body(*refs):
        ins, outs = refs[:n], refs[n:2 * n]
        send1, recv1, local_sems, send2, recv2 = refs[2 * n:]
        first = _ag_first(ins, outs, send1, recv1, local_sems)
        second = _ag_second(outs, send2, recv2)
        _comm_start(first)
        for j in range(3):
            for a in range(n):
                first[2][4 * a + 1 + j].wait_recv()
            for a in range(n):
                second[1][3 * a + j].start()
        for a in range(n):
            first[2][4 * a].wait_recv()
        for cp in second[2]:
            cp.wait_recv()
        for cp in first[1] + second[1]:
            cp.wait_send()
        for cp in first[0]:
            cp.wait()

    return pl.pallas_call(
        body, name="weights_all_gather",
        in_specs=[ANY] * n, out_specs=[ANY] * n,
        out_shape=[jax.ShapeDtypeStruct((N_DEV,) + s.shape, s.dtype) for s in shards],
        scratch_shapes=_dma_sems(4 * n, 4 * n, n, 3 * n, 3 * n),
        compiler_params=pltpu.CompilerParams(has_side_effects=True),
    )(*shards)


N_BIG = 4


def _dev(p):
    return 4 * p[0] + 2 * p[1] + p[2]


def _other_chips(x, y):
    return [(1 - x, y), (x, 1 - y), (1 - x, 1 - y)]


def _remote(src, dst, send_sems, recv_sems, k, to):
    return pltpu.make_async_remote_copy(src_ref=src, dst_ref=dst, send_sem=send_sems.at[k], recv_sem=recv_sems.at[k],
                                        device_id=to, device_id_type=MESH)


def _ag_first(ins, outs, send_sems, recv_sems, local_sems):
    x, y, c = _place()
    me = (x, y, c)
    targets = [(x, y, 1 - c)] + [(*chip, c) for chip in _other_chips(x, y)]
    local, out, inc = [], [], []
    for a in range(len(ins)):
        local.append(pltpu.make_async_copy(ins[a], outs[a].at[_dev(me)], local_sems.at[a]))
        for k, to in enumerate(targets):
            out.append(_remote(ins[a], outs[a].at[_dev(me)], send_sems, recv_sems, 4 * a + k, to))
            inc.append(_remote(ins[a], outs[a].at[_dev(to)], send_sems, recv_sems, 4 * a + k, to))
    return local, out, inc


def _ag_second(bufs, send_sems, recv_sems):
    x, y, c = _place()
    out, inc = [], []
    for a in range(len(bufs)):
        for j, chip in enumerate(_other_chips(x, y)):
            mine, theirs = bufs[a].at[_dev((*chip, c))], bufs[a].at[_dev((*chip, 1 - c))]
            out.append(_remote(mine, mine, send_sems, recv_sems, 3 * a + j, (x, y, 1 - c)))
            inc.append(_remote(theirs, theirs, send_sems, recv_sems, 3 * a + j, (x, y, 1 - c)))
    return [], out, inc


def _rs_first(ins, outs, send_sems, recv_sems):
    x, y, c = _place()
    out = [_remote(ins[a].at[j, 1 - c], outs[a].at[j], send_sems, recv_sems, N_CHIP * a + j, (x, y, 1 - c))
           for a in range(len(ins)) for j in range(N_CHIP)]
    return [], out, out


def _rs_second(ins, outs, send_sems, recv_sems, local_sems):
    x, y, c = _place()
    my_chip = 2 * x + y
    local, out, inc = [], [], []
    for a in range(len(ins)):
        local.append(pltpu.make_async_copy(ins[a].at[my_chip], outs[a].at[my_chip], local_sems.at[a]))
        for k, (ox, oy) in enumerate(_other_chips(x, y)):
            out.append(_remote(ins[a].at[2 * ox + oy], outs[a].at[my_chip], send_sems, recv_sems, 3 * a + k, (ox, oy, c)))
            inc.append(_remote(ins[a].at[2 * ox + oy], outs[a].at[2 * ox + oy], send_sems, recv_sems, 3 * a + k,
                               (ox, oy, c)))
    return local, out, inc


def _comm_start(exchange):
    local, out, _ = exchange
    for cp in local + out:
        cp.start()


def _comm_wait(exchange):
    local, out, inc = exchange
    for cp in inc:
        cp.wait_recv()
    for cp in out:
        cp.wait_send()
    for cp in local:
        cp.wait()


def _dma_sems(*counts):
    return [pltpu.SemaphoreType.DMA((n,)) for n in counts]


def _pair_sum(grads, recvs):
    n = len(grads)

    def body(c_ref, *refs):
        for a in range(n):
            refs[2 * n + a][...] = (refs[a][...].astype(F32) + refs[n + a][...].astype(F32)).astype(BF16)

    def g_spec(g):
        return pl.BlockSpec((None, None) + g.shape[2:], lambda j, c_ref: (j, c_ref[0], 0, 0))

    def r_spec(r):
        return pl.BlockSpec((None,) + r.shape[1:], lambda j, c_ref: (j, 0, 0))

    return pl.pallas_call(
        body, name="pair_sum",
        grid_spec=pltpu.PrefetchScalarGridSpec(
            num_scalar_prefetch=1, grid=(N_CHIP,),
            in_specs=[g_spec(g) for g in grads] + [r_spec(r) for r in recvs],
            out_specs=[r_spec(r) for r in recvs]),
        out_shape=[jax.ShapeDtypeStruct(r.shape, BF16) for r in recvs],
        compiler_params=_params(("parallel",)),
    )(lax.axis_index("c").reshape(1).astype(jnp.int32), *grads, *recvs)


SMALL_ROWS = 544


def _all_reduce_small(buf, parts=()):
    n = len(parts)

    def body(in_ref, *rest):
        p_in, out_ref, p_out = rest[:n], rest[n], rest[n + 1:2 * n + 1]
        recv, acc, send1, recv1, send2, recv2 = rest[2 * n + 1:2 * n + 7]
        scatter_sems = rest[2 * n + 7:]
        x, y, c = _place()
        me = 4 * x + 2 * y + c
        peers = [(x ^ (r >> 2), y ^ ((r >> 1) & 1), c ^ (r & 1)) for r in range(1, N_DEV)]

        def idx(p):
            return 4 * p[0] + 2 * p[1] + p[2]

        if n:
            _comm_start(_rs_second(p_in, p_out, *scatter_sems))
        first = [pltpu.make_async_remote_copy(
            src_ref=in_ref.at[idx(p)], dst_ref=recv.at[me], send_sem=send1.at[r], recv_sem=recv1.at[r],
            device_id=p, device_id_type=MESH) for r, p in enumerate(peers)]
        for cp in first:
            cp.start()
        recv[me] = in_ref[me]
        for r, p in enumerate(peers):
            pltpu.make_async_remote_copy(
                src_ref=in_ref.at[idx(p)], dst_ref=recv.at[idx(p)], send_sem=send1.at[r], recv_sem=recv1.at[r],
                device_id=p, device_id_type=MESH).wait_recv()
        total = recv[0]
        for k in range(1, N_DEV):
            total = total + recv[k]
        acc[...] = total
        out_ref[me] = total
        second = [pltpu.make_async_remote_copy(
            src_ref=acc, dst_ref=out_ref.at[me], send_sem=send2.at[r], recv_sem=recv2.at[r],
            device_id=p, device_id_type=MESH) for r, p in enumerate(peers)]
        for cp in second:
            cp.start()
        for r, p in enumerate(peers):
            pltpu.make_async_remote_copy(
                src_ref=acc, dst_ref=out_ref.at[idx(p)], send_sem=send2.at[r], recv_sem=recv2.at[r],
                device_id=p, device_id_type=MESH).wait_recv()
        for cp in first + second:
            cp.wait_send()
        if n:
            _comm_wait(_rs_second(p_in, p_out, *scatter_sems))

    vm = pl.BlockSpec(memory_space=pltpu.VMEM)
    res = pl.pallas_call(
        body, name="small_grads_all_reduce",
        in_specs=[vm] + [ANY] * n, out_specs=[vm] + [ANY] * n,
        out_shape=[jax.ShapeDtypeStruct(buf.shape, F32)] + [jax.ShapeDtypeStruct(p.shape, p.dtype) for p in parts],
        scratch_shapes=[pltpu.VMEM(buf.shape, F32), pltpu.VMEM(buf.shape[1:], F32)] + _dma_sems(7, 7, 7, 7)
                       + (_dma_sems(3 * n, 3 * n, n) if n else []),
        compiler_params=pltpu.CompilerParams(has_side_effects=True, vmem_limit_bytes=VMEM_LIMIT),
    )(buf, *parts)
    return res[0], list(res[1:])


def _dilate(a, d):
    if d == 1:
        return a
    S, C = a.shape
    return a.reshape(S // d, d, C).transpose(1, 0, 2).reshape(S, C)


def _undilate(a, d):
    if d == 1:
        return a
    S, C = a.shape
    return a.reshape(d, S // d, C).transpose(1, 0, 2).reshape(S, C)


def _cols(a, cb, n=1):
    return a[:, cb * WIDTH:(cb + n) * WIDTH]


def _to_blocks(g, kind):
    if kind == "rows":
        C = g.shape[1]
        return g.reshape(N_CHIP, 2, -1, C)
    return g.reshape(4 * WIDTH, N_CHIP, 2, -1).transpose(1, 2, 0, 3)


SMALL = ("norm_g", "gm_ln_g", "gm_ln_b", "gm_ws", "gm_bs", "pool_w", "pool_scale", "mem_norm_g", "final_norm_g")


def _pack_small(tree):
    flat = jnp.concatenate([tree[k].reshape(-1, 128) for k in SMALL], axis=0)
    return jnp.pad(flat, ((0, N_DEV * SMALL_ROWS - flat.shape[0]), (0, 0)))


def _unpack_small(flat, like):
    out, at = {}, 0
    for k in SMALL:
        rows = like[k].size // 128
        out[k] = flat[at:at + rows].reshape(like[k].shape)
        at += rows
    return out


def _make_layer(wt, wkv, wb, wout, norm_g, mem_norm_g, ln_g, ln_b, gm_ws, gm_bs, pool_w, pool_scale):
    tril = jnp.tril(jnp.ones((CHUNK, CHUNK), bool))
    wsm = jnp.where(tril, gm_ws, 0.0).astype(BF16)
    pw = pool_w.astype(BF16)
    bands, bands_t = _band_constants()
    return dict(wt=wt, wgt=wt[D_BRANCHES:], wkv=wkv, wb=wb, wout=wout, g=norm_g[None], mg=mem_norm_g[None],
                ln_g=ln_g[None],
                ln_b=ln_b[None], wsm=wsm, wsm_t=wsm.transpose(0, 2, 1), pw=pw, pw_t=pw.transpose(0, 2, 1),
                ps=pool_scale[None], bias=jnp.repeat(gm_bs.T, HEAD, axis=1), bands=bands, bands_t=bands_t)


def _layer_fwd(xl, mem0, L, next_shards=()):
    S = xl.shape[0]
    proj, gates, h, half_gathered = _in_proj(xl, L["g"], L["wt"], L["wgt"], next_shards[:1])
    kv, mem_n = _mem_kv(mem0, L["mg"], L["wkv"])
    y4, cdf, gathered = _abm_fwd(proj, L["ln_g"], L["ln_b"], L["wsm"], L["bias"], L["pw"], L["ps"], kv, L["bands"],
                                 half_gathered)
    o_g, l_g = [], []
    for gi, d in enumerate(DILATIONS):
        if d == 1:
            o, lse = _attn_fwd(proj, CB_Q0, proj, CB_K, proj, CB_CV, S // CHUNK)
        else:
            o, lse = _attn_fwd_dilated(proj, CB_Q0 + gi, CB_K, CB_CV, d)
        o_g.append(o)
        l_g.append(lse)
    (xn, y4, oc, lse, z), rest = _merge_fwd(xl, y4, o_g, l_g, proj, gates, L["wb"], L["wout"], next_shards[1:])
    saved = dict(x=xl, proj=proj, gates=gates, h=h, kv=kv, mem_n=mem_n, y4=y4, cdf=cdf, oc=oc, lse=lse, z=z)
    return xn, saved, gathered + rest


def _place_cols(dst, piece, cb):
    return lax.dynamic_update_slice(dst, piece, (0, cb * WIDTH))


def _layer_bwd(dx, mem0, L, sv, later=(), last=False):
    S = dx.shape[0]
    proj = sv["proj"]
    (dy3, doc, delta, dcg, dgm, dwb, dwout), from_sibling = _merge_bwd(
        dx, sv["y4"], sv["oc"], sv["z"], proj, sv["gates"], L["wb"], L["wout"], later)
    pair = _pair_sum(later, from_sibling) if later else ()
    dpb, dm, dlng, dlnb, dws, dbias, dpw, dps, dkv = _abm_bwd(
        proj, sv["cdf"], dy3, L["ln_g"], L["ln_b"], L["wsm"], L["wsm_t"], L["bias"], L["pw"], L["pw_t"], L["ps"], sv["kv"],
        L["bands"], L["bands_t"])
    dk, dv = None, None
    for gi, d in enumerate(DILATIONS):
        if d == 1:
            dpb, dk, dv = _attn_bwd(proj, CB_Q0, proj, CB_K, proj, CB_CV, doc, sv["lse"], delta, S // CHUNK,
                                    dpb, CB_Q0)
        else:
            dpb, dk, dv = _attn_bwd_dilated(proj, CB_Q0 + gi, CB_K, CB_CV, doc, sv["lse"], delta, d, dk, dv,
                                            dpb, CB_Q0 + gi)
    dpb = _place_cols(dpb, dk, CB_K)
    dpb = _place_cols(dpb, dv, CB_CV)
    dpb = _place_cols(dpb, dcg, CB_CGATE)
    dpb = _place_cols(dpb, dm, CB_MQ)
    dwkv, dmg = _mem_bwd(mem0, L["mg"], sv["mem_n"], L["wkv"], dkv)
    dwin_t, parts_rest = _dw_in(sv["h"], dpb, dgm, pair[1:])
    big = _blocked(dict(w_in=dwin_t, w_mem_kv=dwkv, w_branch=dwb, w_out=dwout))
    dxi, dng, parts, from_sibling = _dh_bwd(dpb, dgm, L["wt"], L["wgt"], sv["x"], L["g"], dx, pair[:1],
                                            big if last else ())
    parts = parts + parts_rest
    small = dict(norm_g=dng[0], gm_ln_g=dlng[0], gm_ln_b=dlnb[0], gm_ws=dws,
                 gm_bs=_bias_reduce(dbias)[:, :N_HEAD].T, pool_w=dpw, pool_scale=dps[0], mem_norm_g=dmg[0])
    return dxi, big, small, parts, from_sibling


BIG = ("w_in", "w_mem_kv", "w_branch", "w_out")


def _blocked(big):
    return [_to_blocks(big["w_in"], "rows"), _to_blocks(big["w_mem_kv"], "rows"),
            _to_blocks(big["w_branch"], "branch"), _to_blocks(big["w_out"], "rows")]


def _full_weights(gathered):
    win_t, wkv, wb, wout = gathered
    return (win_t.reshape(D_IN, D_MODEL), wkv.reshape(D_MODEL, 2 * WIDTH),
            wb.reshape(N_DEV, 4, WIDTH, -1).transpose(1, 2, 0, 3).reshape(4, WIDTH, D_MODEL),
            wout.reshape(D_MODEL, D_MODEL))


def kernel(x, mem, norm_g, w_in, gm_ln_g, gm_ln_b, gm_ws, gm_bs, pool_w, pool_scale, mem_norm_g, w_mem_kv, w_branch, w_out, final_norm_g, loss_target, m_norm_g, m_w_in, m_gm_ln_g, m_gm_ln_b, m_gm_ws, m_gm_bs, m_pool_w, m_pool_scale, m_mem_norm_g, m_w_mem_kv, m_w_branch, m_w_out, m_final_norm_g, v_norm_g, v_w_in, v_gm_ln_g, v_gm_ln_b, v_gm_ws, v_gm_bs, v_pool_w, v_pool_scale, v_mem_norm_g, v_w_mem_kv, v_w_branch, v_w_out, v_final_norm_g):
    x0 = x[0]
    mem0 = mem[0]
    tgt = loss_target[0]
    S = x0.shape[0]

    shards = [[w_in[l].T.astype(BF16), w_mem_kv[l].astype(BF16), w_branch[l].astype(BF16).reshape(4 * WIDTH, -1),
               w_out[l].astype(BF16)] for l in range(DEPTH)]
    gathered = _all_gather(shards[0])
    layers, saved = [], []
    xl = x0
    for l in range(DEPTH):
        layers.append(_make_layer(*_full_weights(gathered), norm_g[l], mem_norm_g[l], gm_ln_g[l], gm_ln_b[l],
                                  gm_ws[l], gm_bs[l], pool_w[l], pool_scale[l]))
        xl, sv, gathered = _layer_fwd(xl, mem0, layers[l], shards[l + 1] if l + 1 < DEPTH else ())
        saved.append(sv)

    loss_part, dx, d_final = _loss_head(xl, final_norm_g[None], tgt)
    loss = lax.psum(loss_part[0, 0], ("x", "y", "c"))

    small = {k: [None] * DEPTH for k in SMALL if k != "final_norm_g"}
    parts = [None] * DEPTH
    later = ()
    for l in reversed(range(DEPTH)):
        dx, gb, gs, done, from_sibling = _layer_bwd(dx, mem0, layers[l], saved[l], later, last=(l == 0))
        if later:
            parts[l + 1] = done
        later = gb
        for k in gs:
            small[k][l] = gs[k]
    grad_x = dx[None]
    small_tree = {k: jnp.stack(small[k]) for k in small}
    small_tree["final_norm_g"] = d_final[0]
    reduced, parts[0] = _all_reduce_small(_pack_small(small_tree).reshape(N_DEV, SMALL_ROWS, 128),
                                          _pair_sum(later, from_sibling))

    weights = dict(norm_g=norm_g, w_in=w_in, gm_ln_g=gm_ln_g, gm_ln_b=gm_ln_b, gm_ws=gm_ws, gm_bs=gm_bs,
                   pool_w=pool_w, pool_scale=pool_scale, mem_norm_g=mem_norm_g, w_mem_kv=w_mem_kv,
                   w_branch=w_branch, w_out=w_out, final_norm_g=final_norm_g)
    m_in = dict(norm_g=m_norm_g, w_in=m_w_in, gm_ln_g=m_gm_ln_g, gm_ln_b=m_gm_ln_b, gm_ws=m_gm_ws, gm_bs=m_gm_bs,
                pool_w=m_pool_w, pool_scale=m_pool_scale, mem_norm_g=m_mem_norm_g, w_mem_kv=m_w_mem_kv,
                w_branch=m_w_branch, w_out=m_w_out, final_norm_g=m_final_norm_g)
    v_in = dict(norm_g=v_norm_g, w_in=v_w_in, gm_ln_g=v_gm_ln_g, gm_ln_b=v_gm_ln_b, gm_ws=v_gm_ws, gm_bs=v_gm_bs,
                pool_w=v_pool_w, pool_scale=v_pool_scale, mem_norm_g=v_mem_norm_g, w_mem_kv=v_w_mem_kv,
                w_branch=v_w_branch, w_out=v_w_out, final_norm_g=v_final_norm_g)
    res = {}
    def view(k, arr):
        return arr.transpose(0, 2, 1) if k == "w_in" else arr

    for a, k in enumerate(BIG):
        shape = view(k, weights[k]).shape
        by_layer = [parts[l][a] for l in range(DEPTH)]
        lrc = (DEPTH,) + by_layer[0].shape[1:]
        outs = _adamw_layers(by_layer, view(k, weights[k]).reshape(lrc), view(k, m_in[k]).reshape(lrc),
                             view(k, v_in[k]).reshape(lrc), "adamw_" + k)
        res[k] = [view(k, o.reshape(shape)) for o in outs]
    outs = _adamw(reduced.reshape(1, N_DEV * SMALL_ROWS, 128), _pack_small(weights), _pack_small(m_in),
                  _pack_small(v_in), "adamw_small")
    unpacked = [_unpack_small(o, weights) for o in outs]
    for k in SMALL:
        res[k] = [u[k] for u in unpacked]

    order = ("norm_g", "w_in", "gm_ln_g", "gm_ln_b", "gm_ws", "gm_bs", "pool_w", "pool_scale", "mem_norm_g",
             "w_mem_kv", "w_branch", "w_out", "final_norm_g")
    return (loss, grad_x, *[res[k][0] for k in order], *[res[k][1] for k in order],
            *[res[k][2] for k in order], *[res[k][3] for k in order])
```

```python
import functools
import math

import numpy as np

import jax
import jax.numpy as jnp
from jax import lax
from jax.experimental import pallas as pl
from jax.experimental.pallas import tpu as pltpu

F32 = jnp.float32
BF16 = jnp.bfloat16

D_MODEL = 1024
DEPTH = 4
WIDTH = 512
D_IN = 10752
HEAD = 128
N_HEAD = 4
CHUNK = 128
MEM_LEN = 256
POOL_WINDOWS = (2, 4, 8, 16)
DILATIONS = (1, 4, 16)
EPS = 1e-6
NEG = -1e30
ATT_SCALE = HEAD ** -0.5
N_DEV = 8
N_CHIP = 4

D_BRANCHES = 6656
D_GATES = D_IN - D_BRANCHES
CB_U, CB_V, CB_AGATE, CB_PIN, CB_PGATE = 0, 1, 2, 3, 4
CB_Q0, CB_K, CB_CV, CB_CGATE, CB_MQ, CB_MGATE = 5, 8, 9, 10, 11, 12

ADAM_LR = 0.001
ADAM_B1 = 0.9
ADAM_B2 = 0.999
ADAM_EPS = 1e-08
ADAM_WD = 0.01
ADAM_STEP = 10

VMEM_LIMIT = 56 * 1024 * 1024
MESH = pl.DeviceIdType.MESH
ANY = pl.BlockSpec(memory_space=pl.ANY)

NT = (((1,), (1,)), ((), ()))
TN = (((0,), (0,)), ((), ()))


def _dot(a, b):
    return jnp.dot(a, b, preferred_element_type=F32)


def _dot_nt(a, b):
    return lax.dot_general(a, b, NT, preferred_element_type=F32)


def _dot_tn(a, b):
    return lax.dot_general(a, b, TN, preferred_element_type=F32)


def _sigmoid(x):
    return 0.5 * jnp.tanh(0.5 * x) + 0.5


def _silu(x):
    return x * _sigmoid(x)


def _silu_and_grad(x):
    s = _sigmoid(x)
    return x * s, s * (1.0 + x * (1.0 - s))


def _normal_cdf(x):
    return 0.5 * (1.0 + lax.erf(x * (2.0 ** -0.5)))


def _gelu_and_grad(x, cdf):
    return x * cdf, cdf + x * jnp.exp(-0.5 * x * x) * (1.0 / math.sqrt(2.0 * math.pi))


def _col(blk, h):
    lane = lax.broadcasted_iota(jnp.int32, blk.shape, 1)
    return jnp.sum(jnp.where(lane == h, blk, 0.0), axis=1, keepdims=True)


def _put_cols(cols):
    rows = cols[0].shape[0]
    lane = lax.broadcasted_iota(jnp.int32, (rows, 128), 1)
    out = jnp.zeros((rows, 128), F32)
    for h, cv in enumerate(cols):
        out = jnp.where(lane == h, cv, out)
    return out


def _params(sem, vmem=VMEM_LIMIT):
    return pltpu.CompilerParams(dimension_semantics=sem, vmem_limit_bytes=vmem)


def _full(shape):
    nd = len(shape)
    return pl.BlockSpec(shape, lambda *_: (0,) * nd)


def _resident(shape):
    nd = len(shape)
    return pl.BlockSpec(shape, lambda *_: (0,) * nd, pipeline_mode=pl.Buffered(1))


def _rows(tm, width, cb=0):
    return pl.BlockSpec((tm, width), lambda i: (i, cb))


GATE_TILE = 512


def _in_proj(x, g, wt, wgt, shards=()):
    S = x.shape[0]
    tm, tnb, tng = 1024, D_BRANCHES // 4, D_GATES // 4
    njb, njg = 4, 4
    n = len(shards)
    ni, nj = S // tm, njb + njg

    def body(x_ref, g_ref, wbr_ref, wg_ref, *rest):
        ins, (proj_ref, gates_ref, h_ref), outs = rest[:n], rest[n:n + 3], rest[n + 3:2 * n + 3]
        hs, sems = rest[2 * n + 3], rest[2 * n + 4:]
        i, j = pl.program_id(0), pl.program_id(1)

        if n:
            @pl.when(jnp.logical_and(i == 0, j == 0))
            def _():
                _comm_start(_ag_first(ins, outs, *sems))

        @pl.when(j == 0)
        def _():
            xf = x_ref[...]
            r = lax.rsqrt(jnp.mean(xf * xf, axis=-1, keepdims=True) + EPS)
            h = (xf * r * g_ref[...]).astype(BF16)
            hs[...] = h
            h_ref[...] = h

        @pl.when(j < njb)
        def _():
            proj_ref[...] = _dot_nt(hs[...], wbr_ref[...]).astype(BF16)

        @pl.when(j >= njb)
        def _():
            gates_ref[...] = _dot_nt(hs[...], wg_ref[...]).astype(BF16)

        if n:
            @pl.when(jnp.logical_and(i == ni - 1, j == nj - 1))
            def _():
                _comm_wait(_ag_first(ins, outs, *sems))

    def first(j):
        return jnp.minimum(j, njb - 1)

    def second(j):
        return jnp.maximum(j - njb, 0)

    res = pl.pallas_call(
        body, name="in_proj_gather" if n else "in_proj",
        grid=(ni, nj),
        in_specs=[pl.BlockSpec((tm, D_MODEL), lambda i, j: (i, 0)),
                  pl.BlockSpec((1, D_MODEL), lambda i, j: (0, 0)),
                  pl.BlockSpec((tnb, D_MODEL), lambda i, j: (first(j), 0)),
                  pl.BlockSpec((tng, D_MODEL), lambda i, j: (second(j), 0))]
                 + [ANY] * n,
        out_specs=[pl.BlockSpec((tm, tnb), lambda i, j: (i, first(j))),
                   pl.BlockSpec((tm, tng), lambda i, j: (i, second(j))),
                   pl.BlockSpec((tm, D_MODEL), lambda i, j: (i, 0))] + [ANY] * n,
        out_shape=[jax.ShapeDtypeStruct((S, D_BRANCHES), BF16), jax.ShapeDtypeStruct((S, D_GATES), BF16),
                   jax.ShapeDtypeStruct((S, D_MODEL), BF16)]
                  + [jax.ShapeDtypeStruct((N_DEV,) + s.shape, s.dtype) for s in shards],
        scratch_shapes=[pltpu.VMEM((tm, D_MODEL), BF16)] + (_dma_sems(4 * n, 4 * n, n) if n else []),
        compiler_params=_params(("arbitrary", "arbitrary")),
    )(x, g, wt, wgt, *shards)
    return res[0], res[1], res[2], list(res[3:])


def _mem_kv(mem, g, w):
    M = mem.shape[0]

    def body(m_ref, g_ref, w_ref, kv_ref, mn_ref):
        xf = m_ref[...]
        r = lax.rsqrt(jnp.mean(xf * xf, axis=-1, keepdims=True) + EPS)
        mn = (xf * r * g_ref[...]).astype(BF16)
        mn_ref[...] = mn
        kv_ref[...] = _dot(mn, w_ref[...]).astype(BF16)

    return pl.pallas_call(
        body, name="mem_kv",
        out_shape=[jax.ShapeDtypeStruct((M, 2 * WIDTH), BF16), jax.ShapeDtypeStruct((M, D_MODEL), BF16)],
        compiler_params=pltpu.CompilerParams(vmem_limit_bytes=VMEM_LIMIT),
    )(mem, g, w)


def _band_constants():
    t = np.arange(CHUNK)[:, None]
    s = np.arange(CHUNK)[None, :]
    bands = np.stack([np.stack([(t - s >= 0) & (t - s < win), s > t + CHUNK - win]) for win in POOL_WINDOWS])
    bands = bands.astype(np.float32)
    return jnp.asarray(bands, BF16), jnp.asarray(bands.transpose(0, 1, 3, 2), BF16)


def _inv_count(first_row, win):
    t = first_row + lax.broadcasted_iota(jnp.int32, (CHUNK, 1), 0)
    return 1.0 / jnp.minimum(t + 1, win).astype(F32)


def _layer_norm_fwd(v):
    mu = jnp.mean(v, axis=-1, keepdims=True)
    vc = v - mu
    var = jnp.mean(vc * vc, axis=-1, keepdims=True)
    rstd = lax.rsqrt(var + EPS)
    return vc * rstd, rstd


def _mem_softmax(q, kmem):
    s = _dot_nt(q, kmem) * ATT_SCALE
    m = jnp.max(s, axis=-1, keepdims=True)
    e = jnp.exp(s - m)
    return e * (1.0 / jnp.sum(e, axis=-1, keepdims=True))


def _abm_fwd(proj, ln_g, ln_b, wsm, bias_full, pool_w, pool_scale, kv, bands, gathered=()):
    S = proj.shape[0]
    tm = 512
    nchunk = tm // CHUNK
    n = len(gathered)
    nsteps = S // tm

    def body(u_ref, v_ref, ag_ref, p_ref, ph_ref, pg_ref, mq_ref, mg_ref, lng_ref, lnb_ref, wsm_ref, bias_ref,
             pw_ref, ps_ref, kv_ref, band_ref, *rest):
        y_ref, cdf_ref, bufs = rest[n], rest[n + 1], rest[n + 2:2 * n + 2]
        mix, sems = rest[2 * n + 2], rest[2 * n + 3:]
        i = pl.program_id(0)

        if n:
            @pl.when(i == 0)
            def _():
                _comm_start(_ag_second(bufs, *sems))

        au, av = u_ref[...].astype(F32), v_ref[...].astype(F32)
        cdf_u, cdf_v = _normal_cdf(au), _normal_cdf(av)
        cdf_ref[0] = cdf_u.astype(BF16)
        cdf_ref[1] = cdf_v.astype(BF16)
        u, v = au * cdf_u, av * cdf_v
        vhat, _ = _layer_norm_fwd(v)
        vln = (vhat * lng_ref[...] + lnb_ref[...]).astype(BF16)
        for c in range(nchunk):
            for h in range(N_HEAD):
                rs, cs = slice(c * CHUNK, (c + 1) * CHUNK), slice(h * HEAD, (h + 1) * HEAD)
                mix[rs, cs] = _dot(wsm_ref[h], vln[rs, cs]) + bias_ref[:, cs]
        y_ref[0] = (u * mix[...] * _silu(ag_ref[...].astype(F32))).astype(BF16)
        halo_ok = (i > 0).astype(F32)
        for c in range(nchunk):
            rs = slice(c * CHUNK, (c + 1) * CHUNK)
            for g, win in enumerate(POOL_WINDOWS):
                cs = slice(g * HEAD, (g + 1) * HEAD)
                cur = p_ref[rs, cs]
                if c == 0:
                    prev = (ph_ref[:, cs].astype(F32) * halo_ok).astype(BF16)
                else:
                    prev = p_ref[(c - 1) * CHUNK:c * CHUNK, cs]
                sums = _dot(band_ref[g, 0], cur) + _dot(band_ref[g, 1], prev)
                dm = sums * _inv_count(i * tm + c * CHUNK, win) - cur.astype(F32)
                mix[rs, cs] = _dot(dm.astype(BF16), pw_ref[g])
        y_ref[1] = (mix[...] * ps_ref[...] * _silu(pg_ref[...].astype(F32))).astype(BF16)
        for h in range(N_HEAD):
            cs = slice(h * HEAD, (h + 1) * HEAD)
            p = _mem_softmax(mq_ref[:, cs], kv_ref[:, cs])
            mix[:, cs] = _dot(p.astype(BF16), kv_ref[:, WIDTH + h * HEAD:WIDTH + (h + 1) * HEAD])
        y_ref[2] = (mix[...] * _silu(mg_ref[...].astype(F32))).astype(BF16)

        if n:
            @pl.when(i == nsteps - 1)
            def _():
                _comm_wait(_ag_second(bufs, *sems))

    blk = tm // CHUNK
    res = pl.pallas_call(
        body, name="abm_fwd_gather" if n else "abm_fwd",
        grid=(nsteps,),
        in_specs=[_rows(tm, WIDTH, CB_U), _rows(tm, WIDTH, CB_V), _rows(tm, WIDTH, CB_AGATE),
                  _rows(tm, WIDTH, CB_PIN),
                  pl.BlockSpec((CHUNK, WIDTH), lambda i: (jnp.maximum(i * blk - 1, 0), CB_PIN)),
                  _rows(tm, WIDTH, CB_PGATE), _rows(tm, WIDTH, CB_MQ), _rows(tm, WIDTH, CB_MGATE),
                  _full((1, WIDTH)), _full((1, WIDTH)), _full((N_HEAD, CHUNK, CHUNK)), _full((CHUNK, WIDTH)),
                  _full((4, HEAD, HEAD)), _full((1, WIDTH)), _full((MEM_LEN, 2 * WIDTH)),
                  _full((4, 2, CHUNK, CHUNK))] + [ANY] * n,
        out_specs=[pl.BlockSpec((3, tm, WIDTH), lambda i: (0, i, 0)), pl.BlockSpec((2, tm, WIDTH), lambda i: (0, i, 0))]
                  + [ANY] * n,
        out_shape=[jax.ShapeDtypeStruct((4, S, WIDTH), BF16),
                   jax.ShapeDtypeStruct((2, S, WIDTH), BF16)]
                  + [jax.ShapeDtypeStruct(b.shape, b.dtype) for b in gathered],
        input_output_aliases={16 + a: 2 + a for a in range(n)},
        scratch_shapes=[pltpu.VMEM((tm, WIDTH), F32)] + (_dma_sems(3 * n, 3 * n) if n else []),
        compiler_params=_params(("arbitrary",)),
    )(proj, proj, proj, proj, proj, proj, proj, proj, ln_g, ln_b, wsm, bias_full, pool_w, pool_scale, kv, bands,
      *gathered)
    return res[0], res[1], list(res[2:])


ATT_TILE = 512


def _attn_fwd(q, qcb, k, kcb, v, vcb, bps):
    S = q.shape[0]
    tm = ATT_TILE
    nb = tm // CHUNK

    nblocks = nb * N_HEAD

    def body(q_ref, k_ref, v_ref, kh_ref, vh_ref, o_ref, l_ref, sc_s, sp_s, pc_s, pp_s):
        i = pl.program_id(0)

        def prev_kv(n, cs):
            if n == 0:
                return kh_ref[:, cs], vh_ref[:, cs]
            ps = slice((n - 1) * CHUNK, n * CHUNK)
            return k_ref[ps, cs], v_ref[ps, cs]

        pens = []
        for n in range(nb):
            rs = slice(n * CHUNK, (n + 1) * CHUNK)
            pens.append(jnp.full((N_HEAD * CHUNK, 1), jnp.where((i * nb + n) % bps != 0, 0.0, NEG), F32))
            for h in range(N_HEAD):
                cs = slice(h * HEAD, (h + 1) * HEAD)
                bs = slice((n * N_HEAD + h) * CHUNK, (n * N_HEAD + h + 1) * CHUNK)
                qh = q_ref[rs, cs]
                sc_s[bs, :] = _dot_nt(qh, k_ref[rs, cs])
                sp_s[bs, :] = _dot_nt(qh, prev_kv(n, cs)[0])
        row = lax.broadcasted_iota(jnp.int32, (nblocks * CHUNK, CHUNK), 0) & (CHUNK - 1)
        col = lax.broadcasted_iota(jnp.int32, (nblocks * CHUNK, CHUNK), 1)
        sc = jnp.where(col <= row, sc_s[...] * ATT_SCALE, NEG)
        sp = jnp.where(col >= row, sp_s[...] * ATT_SCALE, NEG) + jnp.concatenate(pens, axis=0)
        m = jnp.maximum(jnp.max(sc, axis=-1, keepdims=True), jnp.max(sp, axis=-1, keepdims=True))
        ec = jnp.exp(sc - m)
        ep = jnp.exp(sp - m)
        den = jnp.sum(ec, axis=-1, keepdims=True) + jnp.sum(ep, axis=-1, keepdims=True)
        inv = 1.0 / den
        pc_s[...] = (ec * inv).astype(BF16)
        pp_s[...] = (ep * inv).astype(BF16)
        lse = m + jnp.log(den)
        for n in range(nb):
            rs = slice(n * CHUNK, (n + 1) * CHUNK)
            for h in range(N_HEAD):
                cs = slice(h * HEAD, (h + 1) * HEAD)
                bs = slice((n * N_HEAD + h) * CHUNK, (n * N_HEAD + h + 1) * CHUNK)
                o = _dot(pc_s[bs, :], v_ref[rs, cs]) + _dot(pp_s[bs, :], prev_kv(n, cs)[1])
                o_ref[rs, cs] = o.astype(BF16)
            l_ref[rs, :] = _put_cols([lse[(n * N_HEAD + h) * CHUNK:(n * N_HEAD + h + 1) * CHUNK]
                                      for h in range(N_HEAD)])

    def halo(cb):
        return pl.BlockSpec((CHUNK, WIDTH), lambda i: (jnp.maximum(i * nb - 1, 0), cb))

    return pl.pallas_call(
        body, name=f"attn_fwd_{bps}",
        grid=(S // tm,),
        in_specs=[_rows(tm, WIDTH, qcb), _rows(tm, WIDTH, kcb), _rows(tm, WIDTH, vcb), halo(kcb), halo(vcb)],
        out_specs=[_rows(tm, WIDTH), _rows(tm, 128)],
        out_shape=[jax.ShapeDtypeStruct((S, WIDTH), BF16), jax.ShapeDtypeStruct((S, 128), F32)],
        scratch_shapes=[pltpu.VMEM((nblocks * CHUNK, CHUNK), F32), pltpu.VMEM((nblocks * CHUNK, CHUNK), F32),
                        pltpu.VMEM((nblocks * CHUNK, CHUNK), BF16), pltpu.VMEM((nblocks * CHUNK, CHUNK), BF16)],
        compiler_params=_params(("parallel",)),
    )(q, k, v, k, v)


def _gate_specs(tm):
    return [pl.BlockSpec((tm, D_MODEL), lambda i, b=b: (i, b)) for b in range(4)]


Y_SLOT = (0, 1, 3, 2)


def _merge_fwd(x, y4, o_g, l_g, proj, gates, wb, wout, shards=()):
    S = x.shape[0]
    tm = 512
    n = len(shards)
    nsteps = S // tm
    forward_at = nsteps - 2

    def body(x_ref, y_ref, o0, o1, o2, l0, l1, l2, cg_ref, *rest):
        gm = rest[:4]
        wb_ref, wo_ref = rest[4:6]
        s_in = rest[6:6 + n]
        xn_ref, yc_ref, oc_ref, lse_ref, z_ref = rest[6 + n:11 + n]
        s_out, ocs, sems = rest[11 + n:11 + 2 * n], rest[11 + 2 * n], rest[12 + 2 * n:]
        i = pl.program_id(0)

        if n:
            @pl.when(i == 0)
            def _():
                _comm_start(_ag_first(s_in, s_out, *sems[:3]))

            @pl.when(i == forward_at)
            def _():
                incoming = _ag_first(s_in, s_out, *sems[:3])[2]
                for a in range(n):
                    for k in range(1, 4):
                        incoming[4 * a + k].wait_recv()
                _comm_start(_ag_second(s_out, *sems[3:]))

        lcols = []
        for h in range(N_HEAD):
            cs = slice(h * HEAD, (h + 1) * HEAD)
            ls = [_col(l[...], h) for l in (l0, l1, l2)]
            m = jnp.maximum(jnp.maximum(ls[0], ls[1]), ls[2])
            tot = jnp.exp(ls[0] - m) + jnp.exp(ls[1] - m) + jnp.exp(ls[2] - m)
            lse = m + jnp.log(tot)
            ocs[:, cs] = sum(jnp.exp(lg - lse) * o[:, cs].astype(F32) for lg, o in zip(ls, (o0, o1, o2)))
            lcols.append(lse)
        lse_ref[...] = _put_cols(lcols)
        oc = ocs[...]
        oc_ref[...] = oc.astype(BF16)
        yc = (oc * _silu(cg_ref[...].astype(F32))).astype(BF16)
        yc_ref[...] = yc
        ys = (y_ref[0], y_ref[1], yc, y_ref[2])
        z = jnp.zeros((tm, D_MODEL), F32)
        for b in range(4):
            z = z + _sigmoid(gm[b][...].astype(F32)) * _dot(ys[b], wb_ref[b])
        zb = z.astype(BF16)
        z_ref[...] = zb
        xn_ref[...] = x_ref[...] + _dot(zb, wo_ref[...])

        if n:
            @pl.when(i == nsteps - 1)
            def _():
                local, out, incoming = _ag_first(s_in, s_out, *sems[:3])
                for a in range(n):
                    incoming[4 * a].wait_recv()
                _comm_wait(_ag_second(s_out, *sems[3:]))
                for cp in out:
                    cp.wait_send()
                for cp in local:
                    cp.wait()

    res = pl.pallas_call(
        body, name="merge_fwd_gather" if n else "merge_fwd",
        grid=(nsteps,),
        in_specs=[_rows(tm, D_MODEL), pl.BlockSpec((3, tm, WIDTH), lambda i: (0, i, 0)),
                  _rows(tm, WIDTH), _rows(tm, WIDTH), _rows(tm, WIDTH),
                  _rows(tm, 128), _rows(tm, 128), _rows(tm, 128),
                  _rows(tm, WIDTH, CB_CGATE)] + _gate_specs(tm)
                 + [_resident((4, WIDTH, D_MODEL)), _resident((D_MODEL, D_MODEL))] + [ANY] * n,
        out_specs=[_rows(tm, D_MODEL), pl.BlockSpec((None, tm, WIDTH), lambda i: (Y_SLOT[2], i, 0)),
                   _rows(tm, WIDTH), _rows(tm, 128), _rows(tm, D_MODEL)] + [ANY] * n,
        out_shape=[jax.ShapeDtypeStruct((S, D_MODEL), F32), jax.ShapeDtypeStruct(y4.shape, BF16),
                   jax.ShapeDtypeStruct((S, WIDTH), BF16), jax.ShapeDtypeStruct((S, 128), F32),
                   jax.ShapeDtypeStruct((S, D_MODEL), BF16)]
                  + [jax.ShapeDtypeStruct((N_DEV,) + s.shape, s.dtype) for s in shards],
        input_output_aliases={1: 1},
        scratch_shapes=[pltpu.VMEM((tm, WIDTH), F32)] + (_dma_sems(4 * n, 4 * n, n, 3 * n, 3 * n) if n else []),
        compiler_params=_params(("arbitrary",)),
    )(x, y4, *o_g, *l_g, proj, *([gates] * 4), wb, wout, *shards)
    return res[:5], list(res[5:])


def _loss_head(x, g, tgt):
    S = x.shape[0]
    tm = 512

    def body(x_ref, g_ref, t_ref, loss_ref, dx_ref, dg_ref):
        @pl.when(pl.program_id(0) == 0)
        def _():
            loss_ref[...] = jnp.zeros_like(loss_ref)
            dg_ref[...] = jnp.zeros_like(dg_ref)

        xf = x_ref[...]
        r = lax.rsqrt(jnp.mean(xf * xf, axis=-1, keepdims=True) + EPS)
        xhat = xf * r
        gv = g_ref[...]
        err = xhat * gv - t_ref[...]
        e2 = jnp.sum(err * err, axis=-1, keepdims=True)
        loss_ref[...] += (0.5 / D_MODEL) * jnp.sum(e2, axis=0, keepdims=True)
        dy = err * (1.0 / D_MODEL)
        dg_ref[...] += jnp.sum(dy * xhat, axis=0, keepdims=True)
        dxh = dy * gv
        dx_ref[...] = r * (dxh - xhat * jnp.mean(dxh * xhat, axis=-1, keepdims=True))

    return pl.pallas_call(
        body, name="loss_head",
        grid=(S // tm,),
        in_specs=[_rows(tm, D_MODEL), _full((1, D_MODEL)), _rows(tm, D_MODEL)],
        out_specs=[_full((1, 128)), _rows(tm, D_MODEL), _full((1, D_MODEL))],
        out_shape=[jax.ShapeDtypeStruct((1, 128), F32), jax.ShapeDtypeStruct((S, D_MODEL), F32),
                   jax.ShapeDtypeStruct((1, D_MODEL), F32)],
        compiler_params=_params(("arbitrary",)),
    )(x, g, tgt)


def _merge_bwd(dxo, y4, oc, z, proj, gates, wb, wout, grads=()):
    S = dxo.shape[0]
    tm = 256
    n = len(grads)
    nsteps = S // tm

    def body(dx_ref, y_ref, oc_ref, z_ref, cg_ref, *rest):
        gm = rest[:4]
        wb_ref, wo_ref = rest[4:6]
        g_in = rest[6:6 + n]
        dy_ref, doc_ref, delta_ref, dcg_ref, dgm_ref, dwb_ref, dwo_ref = rest[6 + n:13 + n]
        g_out = rest[13 + n:13 + 2 * n]
        acc_b, acc_o = rest[13 + 2 * n:15 + 2 * n]
        sems = rest[15 + 2 * n:]
        i = pl.program_id(0)

        @pl.when(i == 0)
        def _():
            acc_b[...] = jnp.zeros_like(acc_b)
            acc_o[...] = jnp.zeros_like(acc_o)
            if n:
                _comm_start(_rs_first(g_in, g_out, *sems))

        dxb = dx_ref[...].astype(BF16)
        acc_o[...] += _dot_tn(z_ref[...], dxb)
        dz = _dot_nt(dxb, wo_ref[...])
        for b in range(4):
            gate = _sigmoid(gm[b][...].astype(F32))
            yb = y_ref[Y_SLOT[b]]
            t = _dot(yb, wb_ref[b])
            dgm_ref[:, b * D_MODEL:(b + 1) * D_MODEL] = (dz * t * gate * (1.0 - gate)).astype(BF16)
            dt = (dz * gate).astype(BF16)
            acc_b[b] += _dot_tn(yb, dt)
            dyb = _dot_nt(dt, wb_ref[b])
            if b == 2:
                cg = cg_ref[...].astype(F32)
                oc = oc_ref[...].astype(F32)
                scg, dscg = _silu_and_grad(cg)
                doc = dyb * scg
                dcg_ref[...] = (dyb * oc * dscg).astype(BF16)
                doc_ref[...] = doc.astype(BF16)
                prod = doc * oc
                delta_ref[...] = _put_cols([jnp.sum(prod[:, h * HEAD:(h + 1) * HEAD], axis=1, keepdims=True)
                                            for h in range(N_HEAD)])
            else:
                dy_ref[b if b < 2 else 2] = dyb.astype(BF16)

        @pl.when(i == nsteps - 1)
        def _():
            dwb_ref[...] = acc_b[...].astype(BF16)
            dwo_ref[...] = acc_o[...].astype(BF16)
            if n:
                _comm_wait(_rs_first(g_in, g_out, *sems))

    res = pl.pallas_call(
        body, name="merge_bwd_scatter" if n else "merge_bwd",
        grid=(nsteps,),
        in_specs=[_rows(tm, D_MODEL), pl.BlockSpec((4, tm, WIDTH), lambda i: (0, i, 0)),
                  _rows(tm, WIDTH), _rows(tm, D_MODEL), _rows(tm, WIDTH, CB_CGATE)] + _gate_specs(tm)
                 + [_resident((4, WIDTH, D_MODEL)), _resident((D_MODEL, D_MODEL))] + [ANY] * n,
        out_specs=[pl.BlockSpec((3, tm, WIDTH), lambda i: (0, i, 0)), _rows(tm, WIDTH), _rows(tm, 128),
                   _rows(tm, WIDTH), _rows(tm, 4 * D_MODEL), _full((4, WIDTH, D_MODEL)), _full((D_MODEL, D_MODEL))]
                  + [ANY] * n,
        out_shape=[jax.ShapeDtypeStruct((3, S, WIDTH), BF16), jax.ShapeDtypeStruct((S, WIDTH), BF16),
                   jax.ShapeDtypeStruct((S, 128), F32), jax.ShapeDtypeStruct((S, WIDTH), BF16),
                   jax.ShapeDtypeStruct((S, 4 * D_MODEL), BF16), jax.ShapeDtypeStruct((4, WIDTH, D_MODEL), BF16),
                   jax.ShapeDtypeStruct((D_MODEL, D_MODEL), BF16)]
                  + [jax.ShapeDtypeStruct(g.shape[:1] + g.shape[2:], g.dtype) for g in grads],
        scratch_shapes=[pltpu.VMEM((4, WIDTH, D_MODEL), F32), pltpu.VMEM((D_MODEL, D_MODEL), F32)]
                       + (_dma_sems(N_CHIP * n, N_CHIP * n) if n else []),
        compiler_params=_params(("arbitrary",)),
    )(dxo, y4, oc, z, proj, *([gates] * 4), wb, wout, *grads)
    return res[:7], list(res[7:])


def _attn_bwd(q, qcb, k, kcb, v, vcb, do, lse, delta, bps, dst, dcb):
    S = q.shape[0]
    tm = ATT_TILE
    nb = tm // CHUNK
    nblk = S // CHUNK

    ncur = nb * N_HEAD
    nprev = (nb + 1) * N_HEAD

    def body(q_ref, k_ref, v_ref, do_ref, l_ref, d_ref, kh_ref, vh_ref, qn_ref, don_ref, ln_ref, dn_ref, dst_ref,
             dq_ref, dk_ref, dv_ref, sc_s, sp_s, dpc_s, dpp_s, pc_s, pp_s, dsc_s, dsp_s):
        i = pl.program_id(0)

        def rows_of(n):
            if n < nb:
                rs = slice(n * CHUNK, (n + 1) * CHUNK)
                return rs, q_ref, do_ref, l_ref, d_ref
            return slice(0, CHUNK), qn_ref, don_ref, ln_ref, dn_ref

        def prev_kv(n, cs):
            if n == 0:
                return kh_ref[:, cs], vh_ref[:, cs]
            ps = slice((n - 1) * CHUNK, n * CHUNK)
            return k_ref[ps, cs], v_ref[ps, cs]

        def blk(n, h):
            return slice((n * N_HEAD + h) * CHUNK, (n * N_HEAD + h + 1) * CHUNK)

        pens, lses, deltas = [], [], []
        for n in range(nb + 1):
            rs, qr, dor, lr, dr = rows_of(n)
            gb = i * nb + n
            pen = jnp.where(gb % bps != 0, 0.0, NEG)
            if n == nb:
                pen = pen + jnp.where(gb < nblk, 0.0, NEG)
            pens.append(jnp.full((N_HEAD * CHUNK, 1), pen, F32))
            lblk, dblk = lr[rs, :], dr[rs, :]
            for h in range(N_HEAD):
                cs = slice(h * HEAD, (h + 1) * HEAD)
                qh, doh = qr[rs, cs], dor[rs, cs]
                lses.append(_col(lblk, h))
                deltas.append(_col(dblk, h))
                kp, vp = prev_kv(n, cs)
                sp_s[blk(n, h), :] = _dot_nt(qh, kp)
                dpp_s[blk(n, h), :] = _dot_nt(doh, vp)
                if n < nb:
                    sc_s[blk(n, h), :] = _dot_nt(qh, k_ref[rs, cs])
                    dpc_s[blk(n, h), :] = _dot_nt(doh, v_ref[rs, cs])
        lse = jnp.concatenate(lses, axis=0)
        delta = jnp.concatenate(deltas, axis=0)
        row = lax.broadcasted_iota(jnp.int32, (nprev * CHUNK, CHUNK), 0) & (CHUNK - 1)
        col = lax.broadcasted_iota(jnp.int32, (nprev * CHUNK, CHUNK), 1)
        sp = jnp.where(col >= row, sp_s[...] * ATT_SCALE, NEG) + jnp.concatenate(pens, axis=0)
        pp = jnp.exp(sp - lse)
        pp_s[...] = pp.astype(BF16)
        dsp_s[...] = (pp * (dpp_s[...] - delta)).astype(BF16)
        nc = ncur * CHUNK
        sc = jnp.where(col[:nc] <= row[:nc], sc_s[...] * ATT_SCALE, NEG)
        pc = jnp.exp(sc - lse[:nc])
        pc_s[...] = pc.astype(BF16)
        dsc_s[...] = (pc * (dpc_s[...] - delta[:nc])).astype(BF16)
        for n in range(nb):
            rs, qr, dor, _, _ = rows_of(n)
            rn, qnr, donr, _, _ = rows_of(n + 1)
            for h in range(N_HEAD):
                cs = slice(h * HEAD, (h + 1) * HEAD)
                kp, _ = prev_kv(n, cs)
                dq = _dot(dsc_s[blk(n, h), :], k_ref[rs, cs]) + _dot(dsp_s[blk(n, h), :], kp)
                dq_ref[rs, cs] = (dq * ATT_SCALE).astype(BF16)
                dk = _dot_tn(dsc_s[blk(n, h), :], qr[rs, cs]) + _dot_tn(dsp_s[blk(n + 1, h), :], qnr[rn, cs])
                dk_ref[rs, cs] = (dk * ATT_SCALE).astype(BF16)
                dv = _dot_tn(pc_s[blk(n, h), :], dor[rs, cs]) + _dot_tn(pp_s[blk(n + 1, h), :], donr[rn, cs])
                dv_ref[rs, cs] = dv.astype(BF16)

    def prev_halo(cb):
        return pl.BlockSpec((CHUNK, WIDTH), lambda i: (jnp.maximum(i * nb - 1, 0), cb))

    def next_halo(width, cb=0):
        return pl.BlockSpec((CHUNK, width), lambda i: (jnp.minimum(i * nb + nb, nblk - 1), cb))

    return pl.pallas_call(
        body, name=f"attn_bwd_{bps}",
        grid=(S // tm,),
        in_specs=[_rows(tm, WIDTH, qcb), _rows(tm, WIDTH, kcb), _rows(tm, WIDTH, vcb), _rows(tm, WIDTH),
                  _rows(tm, 128), _rows(tm, 128), prev_halo(kcb), prev_halo(vcb),
                  next_halo(WIDTH, qcb), next_halo(WIDTH), next_halo(128), next_halo(128), ANY],
        out_specs=[_rows(tm, WIDTH, dcb), _rows(tm, WIDTH), _rows(tm, WIDTH)],
        out_shape=[jax.ShapeDtypeStruct(dst.shape, BF16)] + [jax.ShapeDtypeStruct((S, WIDTH), BF16)] * 2,
        input_output_aliases={12: 0},
        scratch_shapes=[pltpu.VMEM((ncur * CHUNK, CHUNK), F32), pltpu.VMEM((nprev * CHUNK, CHUNK), F32),
                        pltpu.VMEM((ncur * CHUNK, CHUNK), F32), pltpu.VMEM((nprev * CHUNK, CHUNK), F32),
                        pltpu.VMEM((ncur * CHUNK, CHUNK), BF16), pltpu.VMEM((nprev * CHUNK, CHUNK), BF16),
                        pltpu.VMEM((ncur * CHUNK, CHUNK), BF16), pltpu.VMEM((nprev * CHUNK, CHUNK), BF16)],
        compiler_params=_params(("parallel",)),
    )(q, k, v, do, lse, delta, k, v, q, do, lse, delta, dst)


def _dilated_split(d):
    hp = min(N_HEAD, 16 // d)
    return hp, N_HEAD // hp, HEAD * hp


def _strided_regroup(d):
    return d < 16


def _by_class(src_ref, dst, d, hp, nat):
    for j in range(hp):
        if _strided_regroup(d):
            nat[j] = src_ref[:, j * HEAD:(j + 1) * HEAD].astype(F32)
            for r in range(d):
                dst[j, r * CHUNK:(r + 1) * CHUNK, :] = nat.at[j][pl.ds(r, CHUNK, stride=d), :].astype(BF16)
        else:
            dst[j] = pltpu.einshape("(tr)l->(rt)l", src_ref[:, j * HEAD:(j + 1) * HEAD], r=d)


def _from_class(src, dst_ref, d, hp, nat, add_ref=None):
    for j in range(hp):
        cs = slice(j * HEAD, (j + 1) * HEAD)
        if _strided_regroup(d):
            for r in range(d):
                nat.at[j][pl.ds(r, CHUNK, stride=d), :] = src[j, r * CHUNK:(r + 1) * CHUNK, :]
            val = nat[j].astype(BF16)
        else:
            val = pltpu.einshape("(rt)l->(tr)l", src[j].astype(BF16), r=d)
        if add_ref is not None:
            val = (val.astype(F32) + add_ref[:, cs].astype(F32)).astype(BF16)
        dst_ref[:, cs] = val


def _attn_fwd_dilated(proj, qcb, kcb, vcb, d):
    S = proj.shape[0]
    T = CHUNK * d
    hp, nh, cw = _dilated_split(d)
    nblocks = d * hp

    def body(q_ref, k_ref, v_ref, o_ref, l_ref, qf, kst, vst, of, lf, nat, sc_s, sp_s, pc_s, pp_s):
        i, hh = pl.program_id(0), pl.program_id(1)
        kf, vf = kst.at[i % 2, hh], vst.at[i % 2, hh]
        kpf, vpf = kst.at[1 - i % 2, hh], vst.at[1 - i % 2, hh]

        @pl.when(i == 0)
        def _():
            kpf[...] = jnp.zeros_like(kpf)
            vpf[...] = jnp.zeros_like(vpf)

        _by_class(q_ref, qf, d, hp, nat)
        _by_class(k_ref, kf, d, hp, nat)
        _by_class(v_ref, vf, d, hp, nat)

        def blk(ref, r, j):
            return ref[j, r * CHUNK:(r + 1) * CHUNK, :]

        def bs(r, j):
            return slice((r * hp + j) * CHUNK, (r * hp + j + 1) * CHUNK)

        for r in range(d):
            for j in range(hp):
                qb = blk(qf, r, j)
                sc_s[bs(r, j), :] = _dot_nt(qb, blk(kf, r, j))
                sp_s[bs(r, j), :] = _dot_nt(qb, blk(kpf, r, j))
        row = lax.broadcasted_iota(jnp.int32, (nblocks * CHUNK, CHUNK), 0) & (CHUNK - 1)
        col = lax.broadcasted_iota(jnp.int32, (nblocks * CHUNK, CHUNK), 1)
        sc = jnp.where(col <= row, sc_s[...] * ATT_SCALE, NEG)
        sp = jnp.where(col >= row, sp_s[...] * ATT_SCALE, NEG) + jnp.where(i > 0, 0.0, NEG)
        m = jnp.maximum(jnp.max(sc, axis=-1, keepdims=True), jnp.max(sp, axis=-1, keepdims=True))
        ec = jnp.exp(sc - m)
        ep = jnp.exp(sp - m)
        den = jnp.sum(ec, axis=-1, keepdims=True) + jnp.sum(ep, axis=-1, keepdims=True)
        inv = 1.0 / den
        pc_s[...] = (ec * inv).astype(BF16)
        pp_s[...] = (ep * inv).astype(BF16)
        lse = m + jnp.log(den)
        lane = lax.broadcasted_iota(jnp.int32, (CHUNK, 128), 1)
        for r in range(d):
            lblk = jnp.zeros((CHUNK, 128), F32)
            for j in range(hp):
                o = _dot(pc_s[bs(r, j), :], blk(vf, r, j)) + _dot(pp_s[bs(r, j), :], blk(vpf, r, j))
                of[j, r * CHUNK:(r + 1) * CHUNK, :] = o
                lblk = jnp.where(lane == hh * hp + j, lse[bs(r, j)], lblk)
            lf[r * CHUNK:(r + 1) * CHUNK, :] = lblk
        _from_class(of, o_ref, d, hp, nat)
        lnat = pltpu.einshape("(rt)l->(tr)l", lf[...], r=d)

        @pl.when(hh == 0)
        def _():
            l_ref[...] = lnat

        @pl.when(hh > 0)
        def _():
            l_ref[...] += lnat

    def cols(cb):
        return pl.BlockSpec((T, cw), lambda i, hh: (i, cb * nh + hh))

    tile = pltpu.VMEM((hp, T, HEAD), BF16)
    staging = pltpu.VMEM((hp, T, HEAD) if _strided_regroup(d) else (1, 8, HEAD), F32)
    return pl.pallas_call(
        body, name=f"attn_fwd_dilated_{d}",
        grid=(S // T, nh),
        in_specs=[cols(qcb), cols(kcb), cols(vcb)],
        out_specs=[cols(0), pl.BlockSpec((T, 128), lambda i, hh: (i, 0))],
        out_shape=[jax.ShapeDtypeStruct((S, WIDTH), BF16), jax.ShapeDtypeStruct((S, 128), F32)],
        scratch_shapes=[tile, pltpu.VMEM((2, nh, hp, T, HEAD), BF16), pltpu.VMEM((2, nh, hp, T, HEAD), BF16),
                        pltpu.VMEM((hp, T, HEAD), F32), pltpu.VMEM((T, 128), F32), staging,
                        pltpu.VMEM((nblocks * CHUNK, CHUNK), F32), pltpu.VMEM((nblocks * CHUNK, CHUNK), F32),
                        pltpu.VMEM((nblocks * CHUNK, CHUNK), BF16), pltpu.VMEM((nblocks * CHUNK, CHUNK), BF16)],
        compiler_params=_params(("arbitrary", "arbitrary")),
    )(proj, proj, proj)


def _attn_bwd_dilated(proj, qcb, kcb, vcb, do, lse, delta, d, dk_in, dv_in, dst, dcb):
    S = proj.shape[0]
    T = CHUNK * d
    nt = S // T
    hp, nh, cw = _dilated_split(d)
    nblocks = d * hp

    def body(q_ref, k_ref, v_ref, do_ref, l_ref, d_ref, dki_ref, dvi_ref, dst_ref, dq_ref, dk_ref, dv_ref,
             qf, dof, kbuf, vbuf, dqf, gk, gv, nat,
             sc_s, sp_s, dpc_s, dpp_s, pc_s, pp_s, dsc_s, dsp_s):
        hh, i = pl.program_id(0), pl.program_id(1)
        kf, vf, newk, newv = kbuf.at[i % 2], vbuf.at[i % 2], gk.at[i % 2], gv.at[i % 2]
        kpf, vpf, acck, accv = kbuf.at[1 - i % 2], vbuf.at[1 - i % 2], gk.at[1 - i % 2], gv.at[1 - i % 2]

        @pl.when(i == 0)
        def _():
            for ref in (kbuf, vbuf, gk, gv):
                ref[...] = jnp.zeros_like(ref)
            dk_ref[...] = jnp.zeros_like(dk_ref)
            dv_ref[...] = jnp.zeros_like(dv_ref)

        def blk(ref, r, j):
            return ref[j, r * CHUNK:(r + 1) * CHUNK, :]

        def bs(r, j):
            return slice((r * hp + j) * CHUNK, (r * hp + j + 1) * CHUNK)

        @pl.when(i < nt)
        def _():
            _by_class(q_ref, qf, d, hp, nat)
            _by_class(do_ref, dof, d, hp, nat)
            _by_class(k_ref, kf, d, hp, nat)
            _by_class(v_ref, vf, d, hp, nat)
            lses, deltas = [], []
            lcls = pltpu.einshape("(tr)l->(rt)l", l_ref[...], r=d)
            dcls = pltpu.einshape("(tr)l->(rt)l", d_ref[...], r=d)
            for r in range(d):
                lblk = lcls[r * CHUNK:(r + 1) * CHUNK]
                dblk = dcls[r * CHUNK:(r + 1) * CHUNK]
                for j in range(hp):
                    lses.append(_col(lblk, hh * hp + j))
                    deltas.append(_col(dblk, hh * hp + j))
                    qb, dob = blk(qf, r, j), blk(dof, r, j)
                    sc_s[bs(r, j), :] = _dot_nt(qb, blk(kf, r, j))
                    dpc_s[bs(r, j), :] = _dot_nt(dob, blk(vf, r, j))
                    sp_s[bs(r, j), :] = _dot_nt(qb, blk(kpf, r, j))
                    dpp_s[bs(r, j), :] = _dot_nt(dob, blk(vpf, r, j))
            lse = jnp.concatenate(lses, axis=0)
            delta = jnp.concatenate(deltas, axis=0)
            row = lax.broadcasted_iota(jnp.int32, (nblocks * CHUNK, CHUNK), 0) & (CHUNK - 1)
            col = lax.broadcasted_iota(jnp.int32, (nblocks * CHUNK, CHUNK), 1)
            sp = jnp.where(col >= row, sp_s[...] * ATT_SCALE, NEG) + jnp.where(i > 0, 0.0, NEG)
            pp = jnp.exp(sp - lse)
            pp_s[...] = pp.astype(BF16)
            dsp_s[...] = (pp * (dpp_s[...] - delta)).astype(BF16)
            sc = jnp.where(col <= row, sc_s[...] * ATT_SCALE, NEG)
            pc = jnp.exp(sc - lse)
            pc_s[...] = pc.astype(BF16)
            dsc_s[...] = (pc * (dpc_s[...] - delta)).astype(BF16)
            for r in range(d):
                rows = slice(r * CHUNK, (r + 1) * CHUNK)
                for j in range(hp):
                    qb, dob = blk(qf, r, j), blk(dof, r, j)
                    dsc, dsp = dsc_s[bs(r, j), :], dsp_s[bs(r, j), :]
                    dqf[j, rows, :] = (_dot(dsc, blk(kf, r, j)) + _dot(dsp, blk(kpf, r, j))) * ATT_SCALE
                    newk[j, rows, :] = _dot_tn(dsc, qb) * ATT_SCALE
                    newv[j, rows, :] = _dot_tn(pc_s[bs(r, j), :], dob)
                    acck[j, rows, :] += _dot_tn(dsp, qb) * ATT_SCALE
                    accv[j, rows, :] += _dot_tn(pp_s[bs(r, j), :], dob)
            _from_class(dqf, dq_ref, d, hp, nat)

        @pl.when(i > 0)
        def _():
            _from_class(acck, dk_ref, d, hp, nat, dki_ref)
            _from_class(accv, dv_ref, d, hp, nat, dvi_ref)

    def cur(width, cb, nsplit):
        return pl.BlockSpec((T, width), lambda hh, i: (jnp.minimum(i, nt - 1), cb * nsplit + hh * (nsplit > 1)))

    def lag():
        return pl.BlockSpec((T, cw), lambda hh, i: (jnp.maximum(i - 1, 0), hh))

    tile = pltpu.VMEM((hp, T, HEAD), BF16)
    acc = pltpu.VMEM((hp, T, HEAD), F32)
    f32s = pltpu.VMEM((nblocks * CHUNK, CHUNK), F32)
    b16s = pltpu.VMEM((nblocks * CHUNK, CHUNK), BF16)
    return pl.pallas_call(
        body, name=f"attn_bwd_dilated_{d}",
        grid=(nh, nt + 1),
        in_specs=[cur(cw, qcb, nh), cur(cw, kcb, nh), cur(cw, vcb, nh), cur(cw, 0, nh), cur(128, 0, 1), cur(128, 0, 1),
                  lag(), lag(), ANY],
        out_specs=[cur(cw, dcb, nh), lag(), lag()],
        out_shape=[jax.ShapeDtypeStruct(dst.shape, BF16)] + [jax.ShapeDtypeStruct((S, WIDTH), BF16)] * 2,
        input_output_aliases={8: 0},
        scratch_shapes=[tile, tile, pltpu.VMEM((2, hp, T, HEAD), BF16), pltpu.VMEM((2, hp, T, HEAD), BF16), acc,
                        pltpu.VMEM((2, hp, T, HEAD), F32), pltpu.VMEM((2, hp, T, HEAD), F32),
                        acc if _strided_regroup(d) else pltpu.VMEM((1, 8, HEAD), F32)]
                       + [f32s] * 4 + [b16s] * 4,
        compiler_params=_params(("arbitrary", "arbitrary")),
    )(proj, proj, proj, do, lse, delta, dk_in, dv_in, dst)


def _abm_bwd(proj, cdf, dy3, ln_g, ln_b, wsm, wsm_t, bias_full, pool_w, pool_wt, pool_scale, kv, bands, bands_t):
    S = proj.shape[0]
    tm = 512
    nchunk = tm // CHUNK
    nblk = S // CHUNK

    def body(u_ref, v_ref, ag_ref, p_ref, ph_ref, pg_ref, pgn_ref, mq_ref, mg_ref, cdf_ref, dy_ref, dypn_ref,
             lng_ref, lnb_ref, wsm_ref, wsmt_ref, bias_ref, pw_ref, pwt_ref, ps_ref, kv_ref, band_ref, bandt_ref,
             dab_ref, dm_ref, dlng_ref, dlnb_ref, dws_ref, dbias_ref, dpw_ref, dps_ref, dkv_ref,
             mix, dvl, ddn):
        i = pl.program_id(0)

        @pl.when(i == 0)
        def _():
            for r in (dlng_ref, dlnb_ref, dws_ref, dbias_ref, dpw_ref, dps_ref, dkv_ref):
                r[...] = jnp.zeros_like(r)

        au = u_ref[...].astype(F32)
        av = v_ref[...].astype(F32)
        ag = ag_ref[...].astype(F32)
        u, du = _gelu_and_grad(au, cdf_ref[0].astype(F32))
        v, dgelu_v = _gelu_and_grad(av, cdf_ref[1].astype(F32))
        vhat, rstd = _layer_norm_fwd(v)
        vln = (vhat * lng_ref[...] + lnb_ref[...]).astype(BF16)
        for c in range(nchunk):
            for h in range(N_HEAD):
                rs, cs = slice(c * CHUNK, (c + 1) * CHUNK), slice(h * HEAD, (h + 1) * HEAD)
                mix[rs, cs] = _dot(wsm_ref[h], vln[rs, cs]) + bias_ref[:, cs]
        dya = dy_ref[0].astype(F32)
        sg, dsg = _silu_and_grad(ag)
        mixed = mix[...]
        dab_ref[:, 2 * WIDTH:3 * WIDTH] = (dya * u * mixed * dsg).astype(BF16)
        dab_ref[:, 0:WIDTH] = (dya * mixed * sg * du).astype(BF16)
        dmixed = dya * u * sg
        dmb = dmixed.astype(BF16)
        tril = (lax.broadcasted_iota(jnp.int32, (CHUNK, CHUNK), 1)
                <= lax.broadcasted_iota(jnp.int32, (CHUNK, CHUNK), 0))
        for c in range(nchunk):
            rs = slice(c * CHUNK, (c + 1) * CHUNK)
            dbias_ref[...] += dmixed[rs, :]
            for h in range(N_HEAD):
                cs = slice(h * HEAD, (h + 1) * HEAD)
                dvl[rs, cs] = _dot(wsmt_ref[h], dmb[rs, cs])
                dws_ref[h] += jnp.where(tril, _dot_nt(dmb[rs, cs], vln[rs, cs]), 0.0)
        dvln = dvl[...]
        dlng_ref[...] += jnp.sum(dvln * vhat, axis=0, keepdims=True)
        dlnb_ref[...] += jnp.sum(dvln, axis=0, keepdims=True)
        dvh = dvln * lng_ref[...]
        dv = rstd * (dvh - jnp.mean(dvh, axis=-1, keepdims=True)
                     - vhat * jnp.mean(dvh * vhat, axis=-1, keepdims=True))
        dab_ref[:, WIDTH:2 * WIDTH] = (dv * dgelu_v).astype(BF16)

        halo_ok = (i > 0).astype(F32)
        for c in range(nchunk):
            rs = slice(c * CHUNK, (c + 1) * CHUNK)
            for g, win in enumerate(POOL_WINDOWS):
                cs = slice(g * HEAD, (g + 1) * HEAD)
                cur = p_ref[rs, cs]
                if c == 0:
                    prev = (ph_ref[:, cs].astype(F32) * halo_ok).astype(BF16)
                else:
                    prev = p_ref[(c - 1) * CHUNK:c * CHUNK, cs]
                sums = _dot(band_ref[g, 0], cur) + _dot(band_ref[g, 1], prev)
                dvl[rs, cs] = sums * _inv_count(i * tm + c * CHUNK, win) - cur.astype(F32)
        dmat = dvl[...].astype(BF16)
        for g in range(4):
            cs = slice(g * HEAD, (g + 1) * HEAD)
            mix[:, cs] = _dot(dmat[:, cs], pw_ref[g])
        yg = mix[...]
        pg = pg_ref[...].astype(F32)
        dyp = dy_ref[1].astype(F32)
        spg, dspg = _silu_and_grad(pg)
        dyy = dyp * spg
        scale = ps_ref[...]
        dab_ref[:, 4 * WIDTH:5 * WIDTH] = (dyp * yg * scale * dspg).astype(BF16)
        dps_ref[...] += jnp.sum(dyy * yg, axis=0, keepdims=True)
        dyg = (dyy * scale).astype(BF16)
        for g in range(4):
            cs = slice(g * HEAD, (g + 1) * HEAD)
            dpw_ref[g] += _dot_tn(dmat[:, cs], dyg[:, cs])
            mix[:, cs] = _dot(dyg[:, cs], pwt_ref[g])
        next_ok = (i + 1 < S // tm).astype(F32)
        dygn = (dypn_ref[...].astype(F32) * _silu(pgn_ref[...].astype(F32)) * scale * next_ok).astype(BF16)
        for c in range(nchunk + 1):
            for g, win in enumerate(POOL_WINDOWS):
                cs = slice(g * HEAD, (g + 1) * HEAD)
                if c < nchunk:
                    dd = mix[c * CHUNK:(c + 1) * CHUNK, cs]
                else:
                    dd = _dot(dygn[:, cs], pwt_ref[g])
                ddn[c * CHUNK:(c + 1) * CHUNK, cs] = dd * _inv_count(i * tm + c * CHUNK, win)
        ddnb = ddn[...].astype(BF16)
        for c in range(nchunk):
            rs = slice(c * CHUNK, (c + 1) * CHUNK)
            ns = slice((c + 1) * CHUNK, (c + 2) * CHUNK)
            for g, win in enumerate(POOL_WINDOWS):
                cs = slice(g * HEAD, (g + 1) * HEAD)
                dp = _dot(bandt_ref[g, 0], ddnb[rs, cs]) + _dot(bandt_ref[g, 1], ddnb[ns, cs]) - mix[rs, cs]
                dab_ref[rs, 3 * WIDTH + g * HEAD:3 * WIDTH + (g + 1) * HEAD] = dp.astype(BF16)

        mg = mg_ref[...].astype(F32)
        dym = dy_ref[2].astype(F32)
        smg, dsmg = _silu_and_grad(mg)
        dob = (dym * smg).astype(BF16)
        for h in range(N_HEAD):
            cs = slice(h * HEAD, (h + 1) * HEAD)
            vs = slice(WIDTH + h * HEAD, WIDTH + (h + 1) * HEAD)
            qh = mq_ref[:, cs]
            p = _mem_softmax(qh, kv_ref[:, cs])
            pb = p.astype(BF16)
            mix[:, cs] = _dot(pb, kv_ref[:, vs])
            dp = _dot_nt(dob[:, cs], kv_ref[:, vs])
            ds = (p * (dp - jnp.sum(p * dp, axis=-1, keepdims=True))).astype(BF16)
            dm_ref[:, cs] = (_dot(ds, kv_ref[:, cs]) * ATT_SCALE).astype(BF16)
            dkv_ref[:, cs] += _dot_tn(ds, qh) * ATT_SCALE
            dkv_ref[:, vs] += _dot_tn(pb, dob[:, cs])
        dm_ref[:, WIDTH:2 * WIDTH] = (dym * mix[...] * dsmg).astype(BF16)

    blk = tm // CHUNK
    small = [_full((1, WIDTH)), _full((1, WIDTH)), _full((N_HEAD, CHUNK, CHUNK)), _full((CHUNK, WIDTH)),
             _full((4, HEAD, HEAD)), _full((1, WIDTH)), _full((MEM_LEN, 2 * WIDTH))]
    return pl.pallas_call(
        body, name="abm_bwd",
        grid=(S // tm,),
        in_specs=[_rows(tm, WIDTH, CB_U), _rows(tm, WIDTH, CB_V), _rows(tm, WIDTH, CB_AGATE),
                  _rows(tm, WIDTH, CB_PIN),
                  pl.BlockSpec((CHUNK, WIDTH), lambda i: (jnp.maximum(i * blk - 1, 0), CB_PIN)),
                  _rows(tm, WIDTH, CB_PGATE),
                  pl.BlockSpec((CHUNK, WIDTH), lambda i: (jnp.minimum(i * blk + blk, nblk - 1), CB_PGATE)),
                  _rows(tm, WIDTH, CB_MQ), _rows(tm, WIDTH, CB_MGATE),
                  pl.BlockSpec((2, tm, WIDTH), lambda i: (0, i, 0)),
                  pl.BlockSpec((3, tm, WIDTH), lambda i: (0, i, 0)),
                  pl.BlockSpec((None, CHUNK, WIDTH), lambda i: (1, jnp.minimum(i * blk + blk, nblk - 1), 0)),
                  _full((1, WIDTH)), _full((1, WIDTH)), _full((N_HEAD, CHUNK, CHUNK)), _full((N_HEAD, CHUNK, CHUNK)),
                  _full((CHUNK, WIDTH)), _full((4, HEAD, HEAD)), _full((4, HEAD, HEAD)), _full((1, WIDTH)),
                  _full((MEM_LEN, 2 * WIDTH)), _full((4, 2, CHUNK, CHUNK)), _full((4, 2, CHUNK, CHUNK))],
        out_specs=[_rows(tm, 5 * WIDTH), _rows(tm, 2 * WIDTH)] + small,
        out_shape=[jax.ShapeDtypeStruct((S, D_BRANCHES), BF16), jax.ShapeDtypeStruct((S, 2 * WIDTH), BF16),
                   jax.ShapeDtypeStruct((1, WIDTH), F32), jax.ShapeDtypeStruct((1, WIDTH), F32),
                   jax.ShapeDtypeStruct((N_HEAD, CHUNK, CHUNK), F32), jax.ShapeDtypeStruct((CHUNK, WIDTH), F32),
                   jax.ShapeDtypeStruct((4, HEAD, HEAD), F32), jax.ShapeDtypeStruct((1, WIDTH), F32),
                   jax.ShapeDtypeStruct((MEM_LEN, 2 * WIDTH), F32)],
        scratch_shapes=[pltpu.VMEM((tm, WIDTH), F32), pltpu.VMEM((tm, WIDTH), F32),
                        pltpu.VMEM((tm + CHUNK, WIDTH), F32)],
        compiler_params=_params(("arbitrary",)),
    )(proj, proj, proj, proj, proj, proj, proj, proj, proj, cdf, dy3, dy3,
      ln_g, ln_b, wsm, wsm_t, bias_full, pool_w, pool_wt, pool_scale, kv, bands, bands_t)


def _bias_reduce(dbias_full):
    def body(d_ref, o_ref):
        d = d_ref[...]
        o_ref[...] = _put_cols([jnp.sum(d[:, h * HEAD:(h + 1) * HEAD], axis=1, keepdims=True) for h in range(N_HEAD)])

    return pl.pallas_call(body, name="bias_reduce", out_shape=jax.ShapeDtypeStruct((CHUNK, 128), F32))(dbias_full)


def _mem_bwd(mem, g, mem_n, w, dkv):
    def body(m_ref, g_ref, mn_ref, w_ref, dkv_ref, dw_ref, dg_ref):
        dkvb = dkv_ref[...].astype(BF16)
        dw_ref[...] = _dot_tn(mn_ref[...], dkvb).astype(BF16)
        dmn = _dot_nt(dkvb, w_ref[...])
        xf = m_ref[...]
        r = lax.rsqrt(jnp.mean(xf * xf, axis=-1, keepdims=True) + EPS)
        dg_ref[...] = jnp.sum(dmn * xf * r, axis=0, keepdims=True)

    return pl.pallas_call(
        body, name="mem_bwd",
        out_shape=[jax.ShapeDtypeStruct((D_MODEL, 2 * WIDTH), BF16), jax.ShapeDtypeStruct((1, D_MODEL), F32)],
        compiler_params=pltpu.CompilerParams(vmem_limit_bytes=VMEM_LIMIT),
    )(mem, g, mem_n, w, dkv)


def _dh_bwd(dpb, dpg, wt, wgt, x, g, dxo, parts=(), grads=()):
    S = x.shape[0]
    tm, tkb, tkg = 1024, D_BRANCHES // 4, D_GATES // 4
    nkb, nkg = 4, 4
    nk = nkb + nkg
    ni = S // tm
    n, m = len(parts), len(grads)

    def body(dpb_ref, wbr_ref, dpg_ref, wg_ref, x_ref, g_ref, dxo_ref, *rest):
        p_in, g_in = rest[:n], rest[n:n + m]
        dx_ref, dg_ref = rest[n + m:n + m + 2]
        p_out, g_out = rest[n + m + 2:2 * n + m + 2], rest[2 * n + m + 2:2 * (n + m) + 2]
        acc, sems = rest[2 * (n + m) + 2], rest[2 * (n + m) + 3:]
        second_sems, first_sems = (sems[:3] if n else ()), sems[3 if n else 0:]
        i, kk = pl.program_id(0), pl.program_id(1)

        @pl.when(jnp.logical_and(i == 0, kk == 0))
        def _():
            dg_ref[...] = jnp.zeros_like(dg_ref)
            if n:
                _comm_start(_rs_second(p_in, p_out, *second_sems))
            if m:
                _comm_start(_rs_first(g_in, g_out, *first_sems))

        @pl.when(kk == 0)
        def _():
            acc[...] = jnp.zeros_like(acc)

        @pl.when(kk < nkb)
        def _():
            acc[...] += _dot(dpb_ref[...], wbr_ref[...])

        @pl.when(kk >= nkb)
        def _():
            acc[...] += _dot(dpg_ref[...], wg_ref[...])

        @pl.when(kk == nk - 1)
        def _():
            xf = x_ref[...]
            r = lax.rsqrt(jnp.mean(xf * xf, axis=-1, keepdims=True) + EPS)
            xhat = xf * r
            dh = acc[...]
            dg_ref[...] += jnp.sum(dh * xhat, axis=0, keepdims=True)
            dxh = dh * g_ref[...]
            dx_ref[...] = dxo_ref[...] + r * (dxh - xhat * jnp.mean(dxh * xhat, axis=-1, keepdims=True))

        if n or m:
            @pl.when(jnp.logical_and(i == ni - 1, kk == nk - 1))
            def _():
                if n:
                    _comm_wait(_rs_second(p_in, p_out, *second_sems))
                if m:
                    _comm_wait(_rs_first(g_in, g_out, *first_sems))

    res = pl.pallas_call(
        body, name="dh_bwd_scatter" if (n or m) else "dh_bwd",
        grid=(ni, nk),
        in_specs=[pl.BlockSpec((tm, tkb), lambda i, k: (i, jnp.minimum(k, nkb - 1))),
                  pl.BlockSpec((tkb, D_MODEL), lambda i, k: (jnp.minimum(k, nkb - 1), 0)),
                  pl.BlockSpec((tm, tkg), lambda i, k: (i, jnp.maximum(k - nkb, 0))),
                  pl.BlockSpec((tkg, D_MODEL), lambda i, k: (jnp.maximum(k - nkb, 0), 0)),
                  pl.BlockSpec((tm, D_MODEL), lambda i, k: (i, 0)), pl.BlockSpec((1, D_MODEL), lambda i, k: (0, 0)),
                  pl.BlockSpec((tm, D_MODEL), lambda i, k: (i, 0))] + [ANY] * (n + m),
        out_specs=[pl.BlockSpec((tm, D_MODEL), lambda i, k: (i, 0)), pl.BlockSpec((1, D_MODEL), lambda i, k: (0, 0))]
                  + [ANY] * (n + m),
        out_shape=[jax.ShapeDtypeStruct((S, D_MODEL), F32), jax.ShapeDtypeStruct((1, D_MODEL), F32)]
                  + [jax.ShapeDtypeStruct(p.shape, p.dtype) for p in parts]
                  + [jax.ShapeDtypeStruct(gr.shape[:1] + gr.shape[2:], gr.dtype) for gr in grads],
        scratch_shapes=[pltpu.VMEM((tm, D_MODEL), F32)] + (_dma_sems(3 * n, 3 * n, n) if n else [])
                       + (_dma_sems(N_CHIP * m, N_CHIP * m) if m else []),
        compiler_params=_params(("arbitrary", "arbitrary")),
    )(dpb, wt, dpg, wgt, x, g, dxo, *parts, *grads)
    return res[0], res[1], list(res[2:2 + n]), list(res[2 + n:])


def _dw_in(h, dpb, dpg, parts=()):
    S = h.shape[0]
    tk = 2048
    nk = S // tk
    n = len(parts)
    tmb = D_BRANCHES // 4
    ng = D_GATES // GATE_TILE

    def accumulate(a_ref, h_ref, o_ref, acc):
        kk = pl.program_id(1)

        @pl.when(kk == 0)
        def _():
            acc[...] = jnp.zeros_like(acc)

        acc[...] += _dot_tn(a_ref[...], h_ref[...])

        @pl.when(kk == nk - 1)
        def _():
            o_ref[...] = acc[...].astype(BF16)

    def branches(a_ref, h_ref, *rest):
        p_in, o_ref, p_out = rest[:n], rest[n], rest[n + 1:2 * n + 1]
        acc, sems = rest[2 * n + 1], rest[2 * n + 2:]
        i, kk = pl.program_id(0), pl.program_id(1)

        if n:
            @pl.when(jnp.logical_and(i == 0, kk == 0))
            def _():
                _comm_start(_rs_second(p_in, p_out, *sems))

        accumulate(a_ref, h_ref, o_ref, acc)

        if n:
            @pl.when(jnp.logical_and(i == 3, kk == nk - 1))
            def _():
                _comm_wait(_rs_second(p_in, p_out, *sems))

    def gates(a_ref, h_ref, dst_ref, o_ref, acc):
        accumulate(a_ref, h_ref, o_ref, acc)

    res = pl.pallas_call(
        branches, name="dw_in_branches_scatter" if n else "dw_in_branches",
        grid=(4, nk),
        in_specs=[pl.BlockSpec((tk, tmb), lambda i, k: (k, i)), pl.BlockSpec((tk, D_MODEL), lambda i, k: (k, 0))]
                 + [ANY] * n,
        out_specs=[pl.BlockSpec((tmb, D_MODEL), lambda i, k: (i, 0))] + [ANY] * n,
        out_shape=[jax.ShapeDtypeStruct((D_IN, D_MODEL), BF16)]
                  + [jax.ShapeDtypeStruct(p.shape, p.dtype) for p in parts],
        scratch_shapes=[pltpu.VMEM((tmb, D_MODEL), F32)] + (_dma_sems(3 * n, 3 * n, n) if n else []),
        compiler_params=_params(("arbitrary", "arbitrary")),
    )(dpb, h, *parts)
    dwt = pl.pallas_call(
        gates, name="dw_in_gates",
        grid=(ng, nk),
        in_specs=[pl.BlockSpec((tk, GATE_TILE), lambda i, k: (k, i)), pl.BlockSpec((tk, D_MODEL), lambda i, k: (k, 0)),
                  ANY],
        out_specs=pl.BlockSpec((GATE_TILE, D_MODEL), lambda i, k: (D_BRANCHES // GATE_TILE + i, 0)),
        out_shape=jax.ShapeDtypeStruct((D_IN, D_MODEL), BF16),
        input_output_aliases={2: 0},
        scratch_shapes=[pltpu.VMEM((GATE_TILE, D_MODEL), F32)],
        compiler_params=_params(("parallel", "arbitrary")),
    )(dpg, h, res[0])
    return dwt, list(res[1:])


def _row_tile(R, C, block_bytes=2 << 20):
    for cand in range(min(R, block_bytes // (C * 4)) // 8 * 8, 0, -8):
        if R % cand == 0:
            return cand
    return R


def _adamw_update(p_ref, w_ref, m_ref, v_ref, g_ref, d_ref, nm_ref, nv_ref):
    c1 = 1.0 / (1.0 - ADAM_B1 ** ADAM_STEP)
    c2 = 1.0 / (1.0 - ADAM_B2 ** ADAM_STEP)
    g = p_ref[0].astype(F32)
    for k in range(1, p_ref.shape[0]):
        g = g + p_ref[k].astype(F32)
    nm = ADAM_B1 * m_ref[...] + (1.0 - ADAM_B1) * g
    nv = ADAM_B2 * v_ref[...] + (1.0 - ADAM_B2) * (g * g)
    g_ref[...] = g
    nm_ref[...] = nm
    nv_ref[...] = nv
    d_ref[...] = -ADAM_LR * ((nm * c1) / (jnp.sqrt(nv * c2) + ADAM_EPS) + ADAM_WD * w_ref[...])


def _adamw(parts, w, m, v, name):
    P, R, C = parts.shape
    tr = _row_tile(R, C)

    def body(*refs):
        _adamw_update(*refs)

    spec = pl.BlockSpec((tr, C), lambda i: (i, 0))
    return pl.pallas_call(
        body, name=name,
        grid=(R // tr,),
        in_specs=[pl.BlockSpec((P, tr, C), lambda i: (0, i, 0)), spec, spec, spec],
        out_specs=[spec] * 4,
        out_shape=[jax.ShapeDtypeStruct((R, C), F32)] * 4,
        compiler_params=_params(("parallel",)),
    )(parts, w, m, v)


def _adamw_layers(parts, w, m, v, name):
    depth = len(parts)
    P, R, C = parts[0].shape
    tr = _row_tile(R, C, 1 << 20)

    def body(*refs):
        layer = pl.program_id(0)
        for k in range(depth):
            @pl.when(layer == k)
            def _(k=k):
                _adamw_update(refs[k], *refs[depth:])

    def part_spec(k):
        return pl.BlockSpec((P, tr, C), lambda l, i: (0, jnp.where(l == k, i, 0), 0))

    spec = pl.BlockSpec((None, tr, C), lambda l, i: (l, i, 0))
    return pl.pallas_call(
        body, name=name,
        grid=(depth, R // tr),
        in_specs=[part_spec(k) for k in range(depth)] + [spec] * 3,
        out_specs=[spec] * 4,
        out_shape=[jax.ShapeDtypeStruct((depth, R, C), F32)] * 4,
        compiler_params=_params(("arbitrary", "arbitrary")),
    )(*parts, w, m, v)


def _place():
    return lax.axis_index("x"), lax.axis_index("y"), lax.axis_index("c")


def _all_gather(shards):
    n = len(shards)

    def body(*refs):
        ins, outs = refs[:n], refs[n:2 * n]
        send1, recv1, local_sems, send2, recv2 = refs[2 * n:]
        first = _ag_first(ins, outs, send1, recv1, local_sems)
        second = _ag_second(outs, send2, recv2)
        _comm_start(first)
        for j in range(3):
            for a in range(n):
                first[2][4 * a + 1 + j].wait_recv()
            for a in range(n):
                second[1][3 * a + j].start()
        for a in range(n):
            first[2][4 * a].wait_recv()
        for cp in second[2]:
            cp.wait_recv()
        for cp in first[1] + second[1]:
            cp.wait_send()
        for cp in first[0]:
            cp.wait()

    return pl.pallas_call(
        body, name="weights_all_gather",
        in_specs=[ANY] * n, out_specs=[ANY] * n,
        out_shape=[jax.ShapeDtypeStruct((N_DEV,) + s.shape, s.dtype) for s in shards],
        scratch_shapes=_dma_sems(4 * n, 4 * n, n, 3 * n, 3 * n),
        compiler_params=pltpu.CompilerParams(has_side_effects=True),
    )(*shards)


N_BIG = 4


def _dev(p):
    return 4 * p[0] + 2 * p[1] + p[2]


def _other_chips(x, y):
    return [(1 - x, y), (x, 1 - y), (1 - x, 1 - y)]


def _remote(src, dst, send_sems, recv_sems, k, to):
    return pltpu.make_async_remote_copy(src_ref=src, dst_ref=dst, send_sem=send_sems.at[k], recv_sem=recv_sems.at[k],
                                        device_id=to, device_id_type=MESH)


def _ag_first(ins, outs, send_sems, recv_sems, local_sems):
    x, y, c = _place()
    me = (x, y, c)
    targets = [(x, y, 1 - c)] + [(*chip, c) for chip in _other_chips(x, y)]
    local, out, inc = [], [], []
    for a in range(len(ins)):
        local.append(pltpu.make_async_copy(ins[a], outs[a].at[_dev(me)], local_sems.at[a]))
        for k, to in enumerate(targets):
            out.append(_remote(ins[a], outs[a].at[_dev(me)], send_sems, recv_sems, 4 * a + k, to))
            inc.append(_remote(ins[a], outs[a].at[_dev(to)], send_sems, recv_sems, 4 * a + k, to))
    return local, out, inc


def _ag_second(bufs, send_sems, recv_sems):
    x, y, c = _place()
    out, inc = [], []
    for a in range(len(bufs)):
        for j, chip in enumerate(_other_chips(x, y)):
            mine, theirs = bufs[a].at[_dev((*chip, c))], bufs[a].at[_dev((*chip, 1 - c))]
            out.append(_remote(mine, mine, send_sems, recv_sems, 3 * a + j, (x, y, 1 - c)))
            inc.append(_remote(theirs, theirs, send_sems, recv_sems, 3 * a + j, (x, y, 1 - c)))
    return [], out, inc


def _rs_first(ins, outs, send_sems, recv_sems):
    x, y, c = _place()
    out = [_remote(ins[a].at[j, 1 - c], outs[a].at[j], send_sems, recv_sems, N_CHIP * a + j, (x, y, 1 - c))
           for a in range(len(ins)) for j in range(N_CHIP)]
    return [], out, out


def _rs_second(ins, outs, send_sems, recv_sems, local_sems):
    x, y, c = _place()
    my_chip = 2 * x + y
    local, out, inc = [], [], []
    for a in range(len(ins)):
        local.append(pltpu.make_async_copy(ins[a].at[my_chip], outs[a].at[my_chip], local_sems.at[a]))
        for k, (ox, oy) in enumerate(_other_chips(x, y)):
            out.append(_remote(ins[a].at[2 * ox + oy], outs[a].at[my_chip], send_sems, recv_sems, 3 * a + k, (ox, oy, c)))
            inc.append(_remote(ins[a].at[2 * ox + oy], outs[a].at[2 * ox + oy], send_sems, recv_sems, 3 * a + k,
                               (ox, oy, c)))
    return local, out, inc


def _comm_start(exchange):
    local, out, _ = exchange
    for cp in local + out:
        cp.start()


def _comm_wait(exchange):
    local, out, inc = exchange
    for cp in inc:
        cp.wait_recv()
    for cp in out:
        cp.wait_send()
    for cp in local:
        cp.wait()


def _dma_sems(*counts):
    return [pltpu.SemaphoreType.DMA((n,)) for n in counts]


def _pair_sum(grads, recvs):
    n = len(grads)

    def body(c_ref, *refs):
        for a in range(n):
            refs[2 * n + a][...] = (refs[a][...].astype(F32) + refs[n + a][...].astype(F32)).astype(BF16)

    def g_spec(g):
        return pl.BlockSpec((None, None) + g.shape[2:], lambda j, c_ref: (j, c_ref[0], 0, 0))

    def r_spec(r):
        return pl.BlockSpec((None,) + r.shape[1:], lambda j, c_ref: (j, 0, 0))

    return pl.pallas_call(
        body, name="pair_sum",
        grid_spec=pltpu.PrefetchScalarGridSpec(
            num_scalar_prefetch=1, grid=(N_CHIP,),
            in_specs=[g_spec(g) for g in grads] + [r_spec(r) for r in recvs],
            out_specs=[r_spec(r) for r in recvs]),
        out_shape=[jax.ShapeDtypeStruct(r.shape, BF16) for r in recvs],
        compiler_params=_params(("parallel",)),
    )(lax.axis_index("c").reshape(1).astype(jnp.int32), *grads, *recvs)


SMALL_ROWS = 544


def _all_reduce_small(buf, parts=()):
    n = len(parts)

    def body(in_ref, *rest):
        p_in, out_ref, p_out = rest[:n], rest[n], rest[n + 1:2 * n + 1]
        recv, acc, send1, recv1, send2, recv2 = rest[2 * n + 1:2 * n + 7]
        scatter_sems = rest[2 * n + 7:]
        x, y, c = _place()
        me = 4 * x + 2 * y + c
        peers = [(x ^ (r >> 2), y ^ ((r >> 1) & 1), c ^ (r & 1)) for r in range(1, N_DEV)]

        def idx(p):
            return 4 * p[0] + 2 * p[1] + p[2]

        if n:
            _comm_start(_rs_second(p_in, p_out, *scatter_sems))
        first = [pltpu.make_async_remote_copy(
            src_ref=in_ref.at[idx(p)], dst_ref=recv.at[me], send_sem=send1.at[r], recv_sem=recv1.at[r],
            device_id=p, device_id_type=MESH) for r, p in enumerate(peers)]
        for cp in first:
            cp.start()
        recv[me] = in_ref[me]
        for r, p in enumerate(peers):
            pltpu.make_async_remote_copy(
                src_ref=in_ref.at[idx(p)], dst_ref=recv.at[idx(p)], send_sem=send1.at[r], recv_sem=recv1.at[r],
                device_id=p, device_id_type=MESH).wait_recv()
        total = recv[0]
        for k in range(1, N_DEV):
            total = total + recv[k]
        acc[...] = total
        out_ref[me] = total
        second = [pltpu.make_async_remote_copy(
            src_ref=acc, dst_ref=out_ref.at[me], send_sem=send2.at[r], recv_sem=recv2.at[r],
            device_id=p, device_id_type=MESH) for r, p in enumerate(peers)]
        for cp in second:
            cp.start()
        for r, p in enumerate(peers):
            pltpu.make_async_remote_copy(
                src_ref=acc, dst_ref=out_ref.at[idx(p)], send_sem=send2.at[r], recv_sem=recv2.at[r],
                device_id=p, device_id_type=MESH).wait_recv()
        for cp in first + second:
            cp.wait_send()
        if n:
            _comm_wait(_rs_second(p_in, p_out, *scatter_sems))

    vm = pl.BlockSpec(memory_space=pltpu.VMEM)
    res = pl.pallas_call(
        body, name="small_grads_all_reduce",
        in_specs=[vm] + [ANY] * n, out_specs=[vm] + [ANY] * n,
        out_shape=[jax.ShapeDtypeStruct(buf.shape, F32)] + [jax.ShapeDtypeStruct(p.shape, p.dtype) for p in parts],
        scratch_shapes=[pltpu.VMEM(buf.shape, F32), pltpu.VMEM(buf.shape[1:], F32)] + _dma_sems(7, 7, 7, 7)
                       + (_dma_sems(3 * n, 3 * n, n) if n else []),
        compiler_params=pltpu.CompilerParams(has_side_effects=True, vmem_limit_bytes=VMEM_LIMIT),
    )(buf, *parts)
    return res[0], list(res[1:])


def _dilate(a, d):
    if d == 1:
        return a
    S, C = a.shape
    return a.reshape(S // d, d, C).transpose(1, 0, 2).reshape(S, C)


def _undilate(a, d):
    if d == 1:
        return a
    S, C = a.shape
    return a.reshape(d, S // d, C).transpose(1, 0, 2).reshape(S, C)


def _cols(a, cb, n=1):
    return a[:, cb * WIDTH:(cb + n) * WIDTH]


def _to_blocks(g, kind):
    if kind == "rows":
        C = g.shape[1]
        return g.reshape(N_CHIP, 2, -1, C)
    return g.reshape(4 * WIDTH, N_CHIP, 2, -1).transpose(1, 2, 0, 3)


SMALL = ("norm_g", "gm_ln_g", "gm_ln_b", "gm_ws", "gm_bs", "pool_w", "pool_scale", "mem_norm_g", "final_norm_g")


def _pack_small(tree):
    flat = jnp.concatenate([tree[k].reshape(-1, 128) for k in SMALL], axis=0)
    return jnp.pad(flat, ((0, N_DEV * SMALL_ROWS - flat.shape[0]), (0, 0)))


def _unpack_small(flat, like):
    out, at = {}, 0
    for k in SMALL:
        rows = like[k].size // 128
        out[k] = flat[at:at + rows].reshape(like[k].shape)
        at += rows
    return out


def _make_layer(wt, wkv, wb, wout, norm_g, mem_norm_g, ln_g, ln_b, gm_ws, gm_bs, pool_w, pool_scale):
    tril = jnp.tril(jnp.ones((CHUNK, CHUNK), bool))
    wsm = jnp.where(tril, gm_ws, 0.0).astype(BF16)
    pw = pool_w.astype(BF16)
    bands, bands_t = _band_constants()
    return dict(wt=wt, wgt=wt[D_BRANCHES:], wkv=wkv, wb=wb, wout=wout, g=norm_g[None], mg=mem_norm_g[None],
                ln_g=ln_g[None],
                ln_b=ln_b[None], wsm=wsm, wsm_t=wsm.transpose(0, 2, 1), pw=pw, pw_t=pw.transpose(0, 2, 1),
                ps=pool_scale[None], bias=jnp.repeat(gm_bs.T, HEAD, axis=1), bands=bands, bands_t=bands_t)


def _layer_fwd(xl, mem0, L, next_shards=()):
    S = xl.shape[0]
    proj, gates, h, half_gathered = _in_proj(xl, L["g"], L["wt"], L["wgt"], next_shards[:1])
    kv, mem_n = _mem_kv(mem0, L["mg"], L["wkv"])
    y4, cdf, gathered = _abm_fwd(proj, L["ln_g"], L["ln_b"], L["wsm"], L["bias"], L["pw"], L["ps"], kv, L["bands"],
                                 half_gathered)
    o_g, l_g = [], []
    for gi, d in enumerate(DILATIONS):
        if d == 1:
            o, lse = _attn_fwd(proj, CB_Q0, proj, CB_K, proj, CB_CV, S // CHUNK)
        else:
            o, lse = _attn_fwd_dilated(proj, CB_Q0 + gi, CB_K, CB_CV, d)
        o_g.append(o)
        l_g.append(lse)
    (xn, y4, oc, lse, z), rest = _merge_fwd(xl, y4, o_g, l_g, proj, gates, L["wb"], L["wout"], next_shards[1:])
    saved = dict(x=xl, proj=proj, gates=gates, h=h, kv=kv, mem_n=mem_n, y4=y4, cdf=cdf, oc=oc, lse=lse, z=z)
    return xn, saved, gathered + rest


def _place_cols(dst, piece, cb):
    return lax.dynamic_update_slice(dst, piece, (0, cb * WIDTH))


def _layer_bwd(dx, mem0, L, sv, later=(), last=False):
    S = dx.shape[0]
    proj = sv["proj"]
    (dy3, doc, delta, dcg, dgm, dwb, dwout), from_sibling = _merge_bwd(
        dx, sv["y4"], sv["oc"], sv["z"], proj, sv["gates"], L["wb"], L["wout"], later)
    pair = _pair_sum(later, from_sibling) if later else ()
    dpb, dm, dlng, dlnb, dws, dbias, dpw, dps, dkv = _abm_bwd(
        proj, sv["cdf"], dy3, L["ln_g"], L["ln_b"], L["wsm"], L["wsm_t"], L["bias"], L["pw"], L["pw_t"], L["ps"], sv["kv"],
        L["bands"], L["bands_t"])
    dk, dv = None, None
    for gi, d in enumerate(DILATIONS):
        if d == 1:
            dpb, dk, dv = _attn_bwd(proj, CB_Q0, proj, CB_K, proj, CB_CV, doc, sv["lse"], delta, S // CHUNK,
                                    dpb, CB_Q0)
        else:
            dpb, dk, dv = _attn_bwd_dilated(proj, CB_Q0 + gi, CB_K, CB_CV, doc, sv["lse"], delta, d, dk, dv,
                                            dpb, CB_Q0 + gi)
    dpb = _place_cols(dpb, dk, CB_K)
    dpb = _place_cols(dpb, dv, CB_CV)
    dpb = _place_cols(dpb, dcg, CB_CGATE)
    dpb = _place_cols(dpb, dm, CB_MQ)
    dwkv, dmg = _mem_bwd(mem0, L["mg"], sv["mem_n"], L["wkv"], dkv)
    dwin_t, parts_rest = _dw_in(sv["h"], dpb, dgm, pair[1:])
    big = _blocked(dict(w_in=dwin_t, w_mem_kv=dwkv, w_branch=dwb, w_out=dwout))
    dxi, dng, parts, from_sibling = _dh_bwd(dpb, dgm, L["wt"], L["wgt"], sv["x"], L["g"], dx, pair[:1],
                                            big if last else ())
    parts = parts + parts_rest
    small = dict(norm_g=dng[0], gm_ln_g=dlng[0], gm_ln_b=dlnb[0], gm_ws=dws,
                 gm_bs=_bias_reduce(dbias)[:, :N_HEAD].T, pool_w=dpw, pool_scale=dps[0], mem_norm_g=dmg[0])
    return dxi, big, small, parts, from_sibling


BIG = ("w_in", "w_mem_kv", "w_branch", "w_out")


def _blocked(big):
    return [_to_blocks(big["w_in"], "rows"), _to_blocks(big["w_mem_kv"], "rows"),
            _to_blocks(big["w_branch"], "branch"), _to_blocks(big["w_out"], "rows")]


def _full_weights(gathered):
    win_t, wkv, wb, wout = gathered
    return (win_t.reshape(D_IN, D_MODEL), wkv.reshape(D_MODEL, 2 * WIDTH),
            wb.reshape(N_DEV, 4, WIDTH, -1).transpose(1, 2, 0, 3).reshape(4, WIDTH, D_MODEL),
            wout.reshape(D_MODEL, D_MODEL))


def kernel(x, mem, norm_g, w_in, gm_ln_g, gm_ln_b, gm_ws, gm_bs, pool_w, pool_scale, mem_norm_g, w_mem_kv, w_branch, w_out, final_norm_g, loss_target, m_norm_g, m_w_in, m_gm_ln_g, m_gm_ln_b, m_gm_ws, m_gm_bs, m_pool_w, m_pool_scale, m_mem_norm_g, m_w_mem_kv, m_w_branch, m_w_out, m_final_norm_g, v_norm_g, v_w_in, v_gm_ln_g, v_gm_ln_b, v_gm_ws, v_gm_bs, v_pool_w, v_pool_scale, v_mem_norm_g, v_w_mem_kv, v_w_branch, v_w_out, v_final_norm_g):
    x0 = x[0]
    mem0 = mem[0]
    tgt = loss_target[0]
    S = x0.shape[0]

    shards = [[w_in[l].T.astype(BF16), w_mem_kv[l].astype(BF16), w_branch[l].astype(BF16).reshape(4 * WIDTH, -1),
               w_out[l].astype(BF16)] for l in range(DEPTH)]
    gathered = _all_gather(shards[0])
    layers, saved = [], []
    xl = x0
    for l in range(DEPTH):
        layers.append(_make_layer(*_full_weights(gathered), norm_g[l], mem_norm_g[l], gm_ln_g[l], gm_ln_b[l],
                                  gm_ws[l], gm_bs[l], pool_w[l], pool_scale[l]))
        xl, sv, gathered = _layer_fwd(xl, mem0, layers[l], shards[l + 1] if l + 1 < DEPTH else ())
        saved.append(sv)

    loss_part, dx, d_final = _loss_head(xl, final_norm_g[None], tgt)
    loss = lax.psum(loss_part[0, 0], ("x", "y", "c"))

    small = {k: [None] * DEPTH for k in SMALL if k != "final_norm_g"}
    parts = [None] * DEPTH
    later = ()
    for l in reversed(range(DEPTH)):
        dx, gb, gs, done, from_sibling = _layer_bwd(dx, mem0, layers[l], saved[l], later, last=(l == 0))
        if later:
            parts[l + 1] = done
        later = gb
        for k in gs:
            small[k][l] = gs[k]
    grad_x = dx[None]
    small_tree = {k: jnp.stack(small[k]) for k in small}
    small_tree["final_norm_g"] = d_final[0]
    reduced, parts[0] = _all_reduce_small(_pack_small(small_tree).reshape(N_DEV, SMALL_ROWS, 128),
                                          _pair_sum(later, from_sibling))

    weights = dict(norm_g=norm_g, w_in=w_in, gm_ln_g=gm_ln_g, gm_ln_b=gm_ln_b, gm_ws=gm_ws, gm_bs=gm_bs,
                   pool_w=pool_w, pool_scale=pool_scale, mem_norm_g=mem_norm_g, w_mem_kv=w_mem_kv,
                   w_branch=w_branch, w_out=w_out, final_norm_g=final_norm_g)
    m_in = dict(norm_g=m_norm_g, w_in=m_w_in, gm_ln_g=m_gm_ln_g, gm_ln_b=m_gm_ln_b, gm_ws=m_gm_ws, gm_bs=m_gm_bs,
                pool_w=m_pool_w, pool_scale=m_pool_scale, mem_norm_g=m_mem_norm_g, w_mem_kv=m_w_mem_kv,
                w_branch=m_w_branch, w_out=m_w_out, final_norm_g=m_final_norm_g)
    v_in = dict(norm_g=v_norm_g, w_in=v_w_in, gm_ln_g=v_gm_ln_g, gm_ln_b=v_gm_ln_b, gm_ws=v_gm_ws, gm_bs=v_gm_bs,
                pool_w=v_pool_w, pool_scale=v_pool_scale, mem_norm_g=v_mem_norm_g, w_mem_kv=v_w_mem_kv,
                w_branch=v_w_branch, w_out=v_w_out, final_norm_g=v_final_norm_g)
    res = {}
    def view(k, arr):
        return arr.transpose(0, 2, 1) if k == "w_in" else arr

    for a, k in enumerate(BIG):
        shape = view(k, weights[k]).shape
        by_layer = [parts[l][a] for l in range(DEPTH)]
        lrc = (DEPTH,) + by_layer[0].shape[1:]
        outs = _adamw_layers(by_layer, view(k, weights[k]).reshape(lrc), view(k, m_in[k]).reshape(lrc),
                             view(k, v_in[k]).reshape(lrc), "adamw_" + k)
        res[k] = [view(k, o.reshape(shape)) for o in outs]
    outs = _adamw(reduced.reshape(1, N_DEV * SMALL_ROWS, 128), _pack_small(weights), _pack_small(m_in),
                  _pack_small(v_in), "adamw_small")
    unpacked = [_unpack_small(o, weights) for o in outs]
    for k in SMALL:
        res[k] = [u[k] for u in unpacked]

    order = ("norm_g", "w_in", "gm_ln_g", "gm_ln_b", "gm_ws", "gm_bs", "pool_w", "pool_scale", "mem_norm_g",
             "w_mem_kv", "w_branch", "w_out", "final_norm_g")
    return (loss, grad_x, *[res[k][0] for k in order], *[res[k][1] for k in order],
            *[res[k][2] for k in order], *[res[k][3] for k in order])
```

```python
import functools
import math

import numpy as np

import jax
import jax.numpy as jnp
from jax import lax
from jax.experimental import pallas as pl
from jax.experimental.pallas import tpu as pltpu

F32 = jnp.float32
BF16 = jnp.bfloat16

D_MODEL = 1024
DEPTH = 4
WIDTH = 512
D_IN = 10752
HEAD = 128
N_HEAD = 4
CHUNK = 128
MEM_LEN = 256
POOL_WINDOWS = (2, 4, 8, 16)
DILATIONS = (1, 4, 16)
EPS = 1e-6
NEG = -1e30
ATT_SCALE = HEAD ** -0.5
N_DEV = 8
N_CHIP = 4

D_BRANCHES = 6656
D_GATES = D_IN - D_BRANCHES
CB_U, CB_V, CB_AGATE, CB_PIN, CB_PGATE = 0, 1, 2, 3, 4
CB_Q0, CB_K, CB_CV, CB_CGATE, CB_MQ, CB_MGATE = 5, 8, 9, 10, 11, 12

ADAM_LR = 0.001
ADAM_B1 = 0.9
ADAM_B2 = 0.999
ADAM_EPS = 1e-08
ADAM_WD = 0.01
ADAM_STEP = 10

VMEM_LIMIT = 56 * 1024 * 1024
MESH = pl.DeviceIdType.MESH
ANY = pl.BlockSpec(memory_space=pl.ANY)

NT = (((1,), (1,)), ((), ()))
TN = (((0,), (0,)), ((), ()))


def _dot(a, b):
    return jnp.dot(a, b, preferred_element_type=F32)


def _dot_nt(a, b):
    return lax.dot_general(a, b, NT, preferred_element_type=F32)


def _dot_tn(a, b):
    return lax.dot_general(a, b, TN, preferred_element_type=F32)


def _sigmoid(x):
    return 0.5 * jnp.tanh(0.5 * x) + 0.5


def _silu(x):
    return x * _sigmoid(x)


def _silu_and_grad(x):
    s = _sigmoid(x)
    return x * s, s * (1.0 + x * (1.0 - s))


def _normal_cdf(x):
    return 0.5 * (1.0 + lax.erf(x * (2.0 ** -0.5)))


def _gelu_and_grad(x, cdf):
    return x * cdf, cdf + x * jnp.exp(-0.5 * x * x) * (1.0 / math.sqrt(2.0 * math.pi))


def _col(blk, h):
    lane = lax.broadcasted_iota(jnp.int32, blk.shape, 1)
    return jnp.sum(jnp.where(lane == h, blk, 0.0), axis=1, keepdims=True)


def _put_cols(cols):
    rows = cols[0].shape[0]
    lane = lax.broadcasted_iota(jnp.int32, (rows, 128), 1)
    out = jnp.zeros((rows, 128), F32)
    for h, cv in enumerate(cols):
        out = jnp.where(lane == h, cv, out)
    return out


def _params(sem, vmem=VMEM_LIMIT):
    return pltpu.CompilerParams(dimension_semantics=sem, vmem_limit_bytes=vmem)


def _full(shape):
    nd = len(shape)
    return pl.BlockSpec(shape, lambda *_: (0,) * nd)


def _resident(shape):
    nd = len(shape)
    return pl.BlockSpec(shape, lambda *_: (0,) * nd, pipeline_mode=pl.Buffered(1))


def _rows(tm, width, cb=0):
    return pl.BlockSpec((tm, width), lambda i: (i, cb))


GATE_TILE = 512


def _in_proj(x, g, wt, wgt, shards=()):
    S = x.shape[0]
    tm, tnb, tng = 1024, D_BRANCHES // 4, D_GATES // 4
    njb, njg = 4, 4
    n = len(shards)
    ni, nj = S // tm, njb + njg

    def body(x_ref, g_ref, wbr_ref, wg_ref, *rest):
        ins, (proj_ref, gates_ref, h_ref), outs = rest[:n], rest[n:n + 3], rest[n + 3:2 * n + 3]
        hs, sems = rest[2 * n + 3], rest[2 * n + 4:]
        i, j = pl.program_id(0), pl.program_id(1)

        if n:
            @pl.when(jnp.logical_and(i == 0, j == 0))
            def _():
                _comm_start(_ag_first(ins, outs, *sems))

        @pl.when(j == 0)
        def _():
            xf = x_ref[...]
            r = lax.rsqrt(jnp.mean(xf * xf, axis=-1, keepdims=True) + EPS)
            h = (xf * r * g_ref[...]).astype(BF16)
            hs[...] = h
            h_ref[...] = h

        @pl.when(j < njb)
        def _():
            proj_ref[...] = _dot_nt(hs[...], wbr_ref[...]).astype(BF16)

        @pl.when(j >= njb)
        def _():
            gates_ref[...] = _dot_nt(hs[...], wg_ref[...]).astype(BF16)

        if n:
            @pl.when(jnp.logical_and(i == ni - 1, j == nj - 1))
            def _():
                _comm_wait(_ag_first(ins, outs, *sems))

    def first(j):
        return jnp.minimum(j, njb - 1)

    def second(j):
        return jnp.maximum(j - njb, 0)

    res = pl.pallas_call(
        body, name="in_proj_gather" if n else "in_proj",
        grid=(ni, nj),
        in_specs=[pl.BlockSpec((tm, D_MODEL), lambda i, j: (i, 0)),
                  pl.BlockSpec((1, D_MODEL), lambda i, j: (0, 0)),
                  pl.BlockSpec((tnb, D_MODEL), lambda i, j: (first(j), 0)),
                  pl.BlockSpec((tng, D_MODEL), lambda i, j: (second(j), 0))]
                 + [ANY] * n,
        out_specs=[pl.BlockSpec((tm, tnb), lambda i, j: (i, first(j))),
                   pl.BlockSpec((tm, tng), lambda i, j: (i, second(j))),
                   pl.BlockSpec((tm, D_MODEL), lambda i, j: (i, 0))] + [ANY] * n,
        out_shape=[jax.ShapeDtypeStruct((S, D_BRANCHES), BF16), jax.ShapeDtypeStruct((S, D_GATES), BF16),
                   jax.ShapeDtypeStruct((S, D_MODEL), BF16)]
                  + [jax.ShapeDtypeStruct((N_DEV,) + s.shape, s.dtype) for s in shards],
        scratch_shapes=[pltpu.VMEM((tm, D_MODEL), BF16)] + (_dma_sems(4 * n, 4 * n, n) if n else []),
        compiler_params=_params(("arbitrary", "arbitrary")),
    )(x, g, wt, wgt, *shards)
    return res[0], res[1], res[2], list(res[3:])


def _mem_kv(mem, g, w):
    M = mem.shape[0]

    def body(m_ref, g_ref, w_ref, kv_ref, mn_ref):
        xf = m_ref[...]
        r = lax.rsqrt(jnp.mean(xf * xf, axis=-1, keepdims=True) + EPS)
        mn = (xf * r * g_ref[...]).astype(BF16)
        mn_ref[...] = mn
        kv_ref[...] = _dot(mn, w_ref[...]).astype(BF16)

    return pl.pallas_call(
        body, name="mem_kv",
        out_shape=[jax.ShapeDtypeStruct((M, 2 * WIDTH), BF16), jax.ShapeDtypeStruct((M, D_MODEL), BF16)],
        compiler_params=pltpu.CompilerParams(vmem_limit_bytes=VMEM_LIMIT),
    )(mem, g, w)


def _band_constants():
    t = np.arange(CHUNK)[:, None]
    s = np.arange(CHUNK)[None, :]
    bands = np.stack([np.stack([(t - s >= 0) & (t - s < win), s > t + CHUNK - win]) for win in POOL_WINDOWS])
    bands = bands.astype(np.float32)
    return jnp.asarray(bands, BF16), jnp.asarray(bands.transpose(0, 1, 3, 2), BF16)


def _inv_count(first_row, win):
    t = first_row + lax.broadcasted_iota(jnp.int32, (CHUNK, 1), 0)
    return 1.0 / jnp.minimum(t + 1, win).astype(F32)


def _layer_norm_fwd(v):
    mu = jnp.mean(v, axis=-1, keepdims=True)
    vc = v - mu
    var = jnp.mean(vc * vc, axis=-1, keepdims=True)
    rstd = lax.rsqrt(var + EPS)
    return vc * rstd, rstd


def _mem_softmax(q, kmem):
    s = _dot_nt(q, kmem) * ATT_SCALE
    m = jnp.max(s, axis=-1, keepdims=True)
    e = jnp.exp(s - m)
    return e * (1.0 / jnp.sum(e, axis=-1, keepdims=True))


def _abm_fwd(proj, ln_g, ln_b, wsm, bias_full, pool_w, pool_scale, kv, bands, gathered=()):
    S = proj.shape[0]
    tm = 512
    nchunk = tm // CHUNK
    n = len(gathered)
    nsteps = S // tm

    def body(u_ref, v_ref, ag_ref, p_ref, ph_ref, pg_ref, mq_ref, mg_ref, lng_ref, lnb_ref, wsm_ref, bias_ref,
             pw_ref, ps_ref, kv_ref, band_ref, *rest):
        y_ref, cdf_ref, bufs = rest[n], rest[n + 1], rest[n + 2:2 * n + 2]
        mix, sems = rest[2 * n + 2], rest[2 * n + 3:]
        i = pl.program_id(0)

        if n:
            @pl.when(i == 0)
            def _():
                _comm_start(_ag_second(bufs, *sems))

        au, av = u_ref[...].astype(F32), v_ref[...].astype(F32)
        cdf_u, cdf_v = _normal_cdf(au), _normal_cdf(av)
        cdf_ref[0] = cdf_u.astype(BF16)
        cdf_ref[1] = cdf_v.astype(BF16)
        u, v = au * cdf_u, av * cdf_v
        vhat, _ = _layer_norm_fwd(v)
        vln = (vhat * lng_ref[...] + lnb_ref[...]).astype(BF16)
        for c in range(nchunk):
            for h in range(N_HEAD):
                rs, cs = slice(c * CHUNK, (c + 1) * CHUNK), slice(h * HEAD, (h + 1) * HEAD)
                mix[rs, cs] = _dot(wsm_ref[h], vln[rs, cs]) + bias_ref[:, cs]
        y_ref[0] = (u * mix[...] * _silu(ag_ref[...].astype(F32))).astype(BF16)
        halo_ok = (i > 0).astype(F32)
        for c in range(nchunk):
            rs = slice(c * CHUNK, (c + 1) * CHUNK)
            for g, win in enumerate(POOL_WINDOWS):
                cs = slice(g * HEAD, (g + 1) * HEAD)
                cur = p_ref[rs, cs]
                if c == 0:
                    prev = (ph_ref[:, cs].astype(F32) * halo_ok).astype(BF16)
                else:
                    prev = p_ref[(c - 1) * CHUNK:c * CHUNK, cs]
                sums = _dot(band_ref[g, 0], cur) + _dot(band_ref[g, 1], prev)
                dm = sums * _inv_count(i * tm + c * CHUNK, win) - cur.astype(F32)
                mix[rs, cs] = _dot(dm.astype(BF16), pw_ref[g])
        y_ref[1] = (mix[...] * ps_ref[...] * _silu(pg_ref[...].astype(F32))).astype(BF16)
        for h in range(N_HEAD):
            cs = slice(h * HEAD, (h + 1) * HEAD)
            p = _mem_softmax(mq_ref[:, cs], kv_ref[:, cs])
            mix[:, cs] = _dot(p.astype(BF16), kv_ref[:, WIDTH + h * HEAD:WIDTH + (h + 1) * HEAD])
        y_ref[2] = (mix[...] * _silu(mg_ref[...].astype(F32))).astype(BF16)

        if n:
            @pl.when(i == nsteps - 1)
            def _():
                _comm_wait(_ag_second(bufs, *sems))

    blk = tm // CHUNK
    res = pl.pallas_call(
        body, name="abm_fwd_gather" if n else "abm_fwd",
        grid=(nsteps,),
        in_specs=[_rows(tm, WIDTH, CB_U), _rows(tm, WIDTH, CB_V), _rows(tm, WIDTH, CB_AGATE),
                  _rows(tm, WIDTH, CB_PIN),
                  pl.BlockSpec((CHUNK, WIDTH), lambda i: (jnp.maximum(i * blk - 1, 0), CB_PIN)),
                  _rows(tm, WIDTH, CB_PGATE), _rows(tm, WIDTH, CB_MQ), _rows(tm, WIDTH, CB_MGATE),
                  _full((1, WIDTH)), _full((1, WIDTH)), _full((N_HEAD, CHUNK, CHUNK)), _full((CHUNK, WIDTH)),
                  _full((4, HEAD, HEAD)), _full((1, WIDTH)), _full((MEM_LEN, 2 * WIDTH)),
                  _full((4, 2, CHUNK, CHUNK))] + [ANY] * n,
        out_specs=[pl.BlockSpec((3, tm, WIDTH), lambda i: (0, i, 0)), pl.BlockSpec((2, tm, WIDTH), lambda i: (0, i, 0))]
                  + [ANY] * n,
        out_shape=[jax.ShapeDtypeStruct((4, S, WIDTH), BF16),
                   jax.ShapeDtypeStruct((2, S, WIDTH), BF16)]
                  + [jax.ShapeDtypeStruct(b.shape, b.dtype) for b in gathered],
        input_output_aliases={16 + a: 2 + a for a in range(n)},
        scratch_shapes=[pltpu.VMEM((tm, WIDTH), F32)] + (_dma_sems(3 * n, 3 * n) if n else []),
        compiler_params=_params(("arbitrary",)),
    )(proj, proj, proj, proj, proj, proj, proj, proj, ln_g, ln_b, wsm, bias_full, pool_w, pool_scale, kv, bands,
      *gathered)
    return res[0], res[1], list(res[2:])


ATT_TILE = 512


def _attn_fwd(q, qcb, k, kcb, v, vcb, bps):
    S = q.shape[0]
    tm = ATT_TILE
    nb = tm // CHUNK

    nblocks = nb * N_HEAD

    def body(q_ref, k_ref, v_ref, kh_ref, vh_ref, o_ref, l_ref, sc_s, sp_s, pc_s, pp_s):
        i = pl.program_id(0)

        def prev_kv(n, cs):
            if n == 0:
                return kh_ref[:, cs], vh_ref[:, cs]
            ps = slice((n - 1) * CHUNK, n * CHUNK)
            return k_ref[ps, cs], v_ref[ps, cs]

        pens = []
        for n in range(nb):
            rs = slice(n * CHUNK, (n + 1) * CHUNK)
            pens.append(jnp.full((N_HEAD * CHUNK, 1), jnp.where((i * nb + n) % bps != 0, 0.0, NEG), F32))
            for h in range(N_HEAD):
                cs = slice(h * HEAD, (h + 1) * HEAD)
                bs = slice((n * N_HEAD + h) * CHUNK, (n * N_HEAD + h + 1) * CHUNK)
                qh = q_ref[rs, cs]
                sc_s[bs, :] = _dot_nt(qh, k_ref[rs, cs])
                sp_s[bs, :] = _dot_nt(qh, prev_kv(n, cs)[0])
        row = lax.broadcasted_iota(jnp.int32, (nblocks * CHUNK, CHUNK), 0) & (CHUNK - 1)
        col = lax.broadcasted_iota(jnp.int32, (nblocks * CHUNK, CHUNK), 1)
        sc = jnp.where(col <= row, sc_s[...] * ATT_SCALE, NEG)
        sp = jnp.where(col >= row, sp_s[...] * ATT_SCALE, NEG) + jnp.concatenate(pens, axis=0)
        m = jnp.maximum(jnp.max(sc, axis=-1, keepdims=True), jnp.max(sp, axis=-1, keepdims=True))
        ec = jnp.exp(sc - m)
        ep = jnp.exp(sp - m)
        den = jnp.sum(ec, axis=-1, keepdims=True) + jnp.sum(ep, axis=-1, keepdims=True)
        inv = 1.0 / den
        pc_s[...] = (ec * inv).astype(BF16)
        pp_s[...] = (ep * inv).astype(BF16)
        lse = m + jnp.log(den)
        for n in range(nb):
            rs = slice(n * CHUNK, (n + 1) * CHUNK)
            for h in range(N_HEAD):
                cs = slice(h * HEAD, (h + 1) * HEAD)
                bs = slice((n * N_HEAD + h) * CHUNK, (n * N_HEAD + h + 1) * CHUNK)
                o = _dot(pc_s[bs, :], v_ref[rs, cs]) + _dot(pp_s[bs, :], prev_kv(n, cs)[1])
                o_ref[rs, cs] = o.astype(BF16)
            l_ref[rs, :] = _put_cols([lse[(n * N_HEAD + h) * CHUNK:(n * N_HEAD + h + 1) * CHUNK]
                                      for h in range(N_HEAD)])

    def halo(cb):
        return pl.BlockSpec((CHUNK, WIDTH), lambda i: (jnp.maximum(i * nb - 1, 0), cb))

    return pl.pallas_call(
        body, name=f"attn_fwd_{bps}",
        grid=(S // tm,),
        in_specs=[_rows(tm, WIDTH, qcb), _rows(tm, WIDTH, kcb), _rows(tm, WIDTH, vcb), halo(kcb), halo(vcb)],
        out_specs=[_rows(tm, WIDTH), _rows(tm, 128)],
        out_shape=[jax.ShapeDtypeStruct((S, WIDTH), BF16), jax.ShapeDtypeStruct((S, 128), F32)],
        scratch_shapes=[pltpu.VMEM((nblocks * CHUNK, CHUNK), F32), pltpu.VMEM((nblocks * CHUNK, CHUNK), F32),
                        pltpu.VMEM((nblocks * CHUNK, CHUNK), BF16), pltpu.VMEM((nblocks * CHUNK, CHUNK), BF16)],
        compiler_params=_params(("parallel",)),
    )(q, k, v, k, v)


def _gate_specs(tm):
    return [pl.BlockSpec((tm, D_MODEL), lambda i, b=b: (i, b)) for b in range(4)]


Y_SLOT = (0, 1, 3, 2)


def _merge_fwd(x, y4, o_g, l_g, proj, gates, wb, wout, shards=()):
    S = x.shape[0]
    tm = 512
    n = len(shards)
    nsteps = S // tm
    forward_at = nsteps - 1

    def body(x_ref, y_ref, o0, o1, o2, l0, l1, l2, cg_ref, *rest):
        gm = rest[:4]
        wb_ref, wo_ref = rest[4:6]
        s_in = rest[6:6 + n]
        xn_ref, yc_ref, oc_ref, lse_ref, z_ref = rest[6 + n:11 + n]
        s_out, ocs, sems = rest[11 + n:11 + 2 * n], rest[11 + 2 * n], rest[12 + 2 * n:]
        i = pl.program_id(0)

        if n:
            @pl.when(i == 0)
            def _():
                _comm_start(_ag_first(s_in, s_out, *sems[:3]))

            @pl.when(i == forward_at)
            def _():
                incoming = _ag_first(s_in, s_out, *sems[:3])[2]
                for a in range(n):
                    for k in range(1, 4):
                        incoming[4 * a + k].wait_recv()
                _comm_start(_ag_second(s_out, *sems[3:]))

        lcols = []
        for h in range(N_HEAD):
            cs = slice(h * HEAD, (h + 1) * HEAD)
            ls = [_col(l[...], h) for l in (l0, l1, l2)]
            m = jnp.maximum(jnp.maximum(ls[0], ls[1]), ls[2])
            tot = jnp.exp(ls[0] - m) + jnp.exp(ls[1] - m) + jnp.exp(ls[2] - m)
            lse = m + jnp.log(tot)
            ocs[:, cs] = sum(jnp.exp(lg - lse) * o[:, cs].astype(F32) for lg, o in zip(ls, (o0, o1, o2)))
            lcols.append(lse)
        lse_ref[...] = _put_cols(lcols)
        oc = ocs[...]
        oc_ref[...] = oc.astype(BF16)
        yc = (oc * _silu(cg_ref[...].astype(F32))).astype(BF16)
        yc_ref[...] = yc
        ys = (y_ref[0], y_ref[1], yc, y_ref[2])
        z = jnp.zeros((tm, D_MODEL), F32)
        for b in range(4):
            z = z + _sigmoid(gm[b][...].astype(F32)) * _dot(ys[b], wb_ref[b])
        zb = z.astype(BF16)
        z_ref[...] = zb
        xn_ref[...] = x_ref[...] + _dot(zb, wo_ref[...])

        if n:
            @pl.when(i == nsteps - 1)
            def _():
                local, out, incoming = _ag_first(s_in, s_out, *sems[:3])
                for a in range(n):
                    incoming[4 * a].wait_recv()
                _comm_wait(_ag_second(s_out, *sems[3:]))
                for cp in out:
                    cp.wait_send()
                for cp in local:
                    cp.wait()

    res = pl.pallas_call(
        body, name="merge_fwd_gather" if n else "merge_fwd",
        grid=(nsteps,),
        in_specs=[_rows(tm, D_MODEL), pl.BlockSpec((3, tm, WIDTH), lambda i: (0, i, 0)),
                  _rows(tm, WIDTH), _rows(tm, WIDTH), _rows(tm, WIDTH),
                  _rows(tm, 128), _rows(tm, 128), _rows(tm, 128),
                  _rows(tm, WIDTH, CB_CGATE)] + _gate_specs(tm)
                 + [_resident((4, WIDTH, D_MODEL)), _resident((D_MODEL, D_MODEL))] + [ANY] * n,
        out_specs=[_rows(tm, D_MODEL), pl.BlockSpec((None, tm, WIDTH), lambda i: (Y_SLOT[2], i, 0)),
                   _rows(tm, WIDTH), _rows(tm, 128), _rows(tm, D_MODEL)] + [ANY] * n,
        out_shape=[jax.ShapeDtypeStruct((S, D_MODEL), F32), jax.ShapeDtypeStruct(y4.shape, BF16),
                   jax.ShapeDtypeStruct((S, WIDTH), BF16), jax.ShapeDtypeStruct((S, 128), F32),
                   jax.ShapeDtypeStruct((S, D_MODEL), BF16)]
                  + [jax.ShapeDtypeStruct((N_DEV,) + s.shape, s.dtype) for s in shards],
        input_output_aliases={1: 1},
        scratch_shapes=[pltpu.VMEM((tm, WIDTH), F32)] + (_dma_sems(4 * n, 4 * n, n, 3 * n, 3 * n) if n else []),
        compiler_params=_params(("arbitrary",)),
    )(x, y4, *o_g, *l_g, proj, *([gates] * 4), wb, wout, *shards)
    return res[:5], list(res[5:])


def _loss_head(x, g, tgt):
    S = x.shape[0]
    tm = 512

    def body(x_ref, g_ref, t_ref, loss_ref, dx_ref, dg_ref):
        @pl.when(pl.program_id(0) == 0)
        def _():
            loss_ref[...] = jnp.zeros_like(loss_ref)
            dg_ref[...] = jnp.zeros_like(dg_ref)

        xf = x_ref[...]
        r = lax.rsqrt(jnp.mean(xf * xf, axis=-1, keepdims=True) + EPS)
        xhat = xf * r
        gv = g_ref[...]
        err = xhat * gv - t_ref[...]
        e2 = jnp.sum(err * err, axis=-1, keepdims=True)
        loss_ref[...] += (0.5 / D_MODEL) * jnp.sum(e2, axis=0, keepdims=True)
        dy = err * (1.0 / D_MODEL)
        dg_ref[...] += jnp.sum(dy * xhat, axis=0, keepdims=True)
        dxh = dy * gv
        dx_ref[...] = r * (dxh - xhat * jnp.mean(dxh * xhat, axis=-1, keepdims=True))

    return pl.pallas_call(
        body, name="loss_head",
        grid=(S // tm,),
        in_specs=[_rows(tm, D_MODEL), _full((1, D_MODEL)), _rows(tm, D_MODEL)],
        out_specs=[_full((1, 128)), _rows(tm, D_MODEL), _full((1, D_MODEL))],
        out_shape=[jax.ShapeDtypeStruct((1, 128), F32), jax.ShapeDtypeStruct((S, D_MODEL), F32),
                   jax.ShapeDtypeStruct((1, D_MODEL), F32)],
        compiler_params=_params(("arbitrary",)),
    )(x, g, tgt)


def _merge_bwd(dxo, y4, oc, z, proj, gates, wb, wout, grads=()):
    S = dxo.shape[0]
    tm = 256
    n = len(grads)
    nsteps = S // tm

    def body(dx_ref, y_ref, oc_ref, z_ref, cg_ref, *rest):
        gm = rest[:4]
        wb_ref, wo_ref = rest[4:6]
        g_in = rest[6:6 + n]
        dy_ref, doc_ref, delta_ref, dcg_ref, dgm_ref, dwb_ref, dwo_ref = rest[6 + n:13 + n]
        g_out = rest[13 + n:13 + 2 * n]
        acc_b, acc_o = rest[13 + 2 * n:15 + 2 * n]
        sems = rest[15 + 2 * n:]
        i = pl.program_id(0)

        @pl.when(i == 0)
        def _():
            acc_b[...] = jnp.zeros_like(acc_b)
            acc_o[...] = jnp.zeros_like(acc_o)
            if n:
                _comm_start(_rs_first(g_in, g_out, *sems))

        dxb = dx_ref[...].astype(BF16)
        acc_o[...] += _dot_tn(z_ref[...], dxb)
        dz = _dot_nt(dxb, wo_ref[...])
        for b in range(4):
            gate = _sigmoid(gm[b][...].astype(F32))
            yb = y_ref[Y_SLOT[b]]
            t = _dot(yb, wb_ref[b])
            dgm_ref[:, b * D_MODEL:(b + 1) * D_MODEL] = (dz * t * gate * (1.0 - gate)).astype(BF16)
            dt = (dz * gate).astype(BF16)
            acc_b[b] += _dot_tn(yb, dt)
            dyb = _dot_nt(dt, wb_ref[b])
            if b == 2:
                cg = cg_ref[...].astype(F32)
                oc = oc_ref[...].astype(F32)
                scg, dscg = _silu_and_grad(cg)
                doc = dyb * scg
                dcg_ref[...] = (dyb * oc * dscg).astype(BF16)
                doc_ref[...] = doc.astype(BF16)
                prod = doc * oc
                delta_ref[...] = _put_cols([jnp.sum(prod[:, h * HEAD:(h + 1) * HEAD], axis=1, keepdims=True)
                                            for h in range(N_HEAD)])
            else:
                dy_ref[b if b < 2 else 2] = dyb.astype(BF16)

        @pl.when(i == nsteps - 1)
        def _():
            dwb_ref[...] = acc_b[...].astype(BF16)
            dwo_ref[...] = acc_o[...].astype(BF16)
            if n:
                _comm_wait(_rs_first(g_in, g_out, *sems))

    res = pl.pallas_call(
        body, name="merge_bwd_scatter" if n else "merge_bwd",
        grid=(nsteps,),
        in_specs=[_rows(tm, D_MODEL), pl.BlockSpec((4, tm, WIDTH), lambda i: (0, i, 0)),
                  _rows(tm, WIDTH), _rows(tm, D_MODEL), _rows(tm, WIDTH, CB_CGATE)] + _gate_specs(tm)
                 + [_resident((4, WIDTH, D_MODEL)), _resident((D_MODEL, D_MODEL))] + [ANY] * n,
        out_specs=[pl.BlockSpec((3, tm, WIDTH), lambda i: (0, i, 0)), _rows(tm, WIDTH), _rows(tm, 128),
                   _rows(tm, WIDTH), _rows(tm, 4 * D_MODEL), _full((4, WIDTH, D_MODEL)), _full((D_MODEL, D_MODEL))]
                  + [ANY] * n,
        out_shape=[jax.ShapeDtypeStruct((3, S, WIDTH), BF16), jax.ShapeDtypeStruct((S, WIDTH), BF16),
                   jax.ShapeDtypeStruct((S, 128), F32), jax.ShapeDtypeStruct((S, WIDTH), BF16),
                   jax.ShapeDtypeStruct((S, 4 * D_MODEL), BF16), jax.ShapeDtypeStruct((4, WIDTH, D_MODEL), BF16),
                   jax.ShapeDtypeStruct((D_MODEL, D_MODEL), BF16)]
                  + [jax.ShapeDtypeStruct(g.shape[:1] + g.shape[2:], g.dtype) for g in grads],
        scratch_shapes=[pltpu.VMEM((4, WIDTH, D_MODEL), F32), pltpu.VMEM((D_MODEL, D_MODEL), F32)]
                       + (_dma_sems(N_CHIP * n, N_CHIP * n) if n else []),
        compiler_params=_params(("arbitrary",)),
    )(dxo, y4, oc, z, proj, *([gates] * 4), wb, wout, *grads)
    return res[:7], list(res[7:])


def _attn_bwd(q, qcb, k, kcb, v, vcb, do, lse, delta, bps, dst, dcb):
    S = q.shape[0]
    tm = ATT_TILE
    nb = tm // CHUNK
    nblk = S // CHUNK

    ncur = nb * N_HEAD
    nprev = (nb + 1) * N_HEAD

    def body(q_ref, k_ref, v_ref, do_ref, l_ref, d_ref, kh_ref, vh_ref, qn_ref, don_ref, ln_ref, dn_ref, dst_ref,
             dq_ref, dk_ref, dv_ref, sc_s, sp_s, dpc_s, dpp_s, pc_s, pp_s, dsc_s, dsp_s):
        i = pl.program_id(0)

        def rows_of(n):
            if n < nb:
                rs = slice(n * CHUNK, (n + 1) * CHUNK)
                return rs, q_ref, do_ref, l_ref, d_ref
            return slice(0, CHUNK), qn_ref, don_ref, ln_ref, dn_ref

        def prev_kv(n, cs):
            if n == 0:
                return kh_ref[:, cs], vh_ref[:, cs]
            ps = slice((n - 1) * CHUNK, n * CHUNK)
            return k_ref[ps, cs], v_ref[ps, cs]

        def blk(n, h):
            return slice((n * N_HEAD + h) * CHUNK, (n * N_HEAD + h + 1) * CHUNK)

        pens, lses, deltas = [], [], []
        for n in range(nb + 1):
            rs, qr, dor, lr, dr = rows_of(n)
            gb = i * nb + n
            pen = jnp.where(gb % bps != 0, 0.0, NEG)
            if n == nb:
                pen = pen + jnp.where(gb < nblk, 0.0, NEG)
            pens.append(jnp.full((N_HEAD * CHUNK, 1), pen, F32))
            lblk, dblk = lr[rs, :], dr[rs, :]
            for h in range(N_HEAD):
                cs = slice(h * HEAD, (h + 1) * HEAD)
                qh, doh = qr[rs, cs], dor[rs, cs]
                lses.append(_col(lblk, h))
                deltas.append(_col(dblk, h))
                kp, vp = prev_kv(n, cs)
                sp_s[blk(n, h), :] = _dot_nt(qh, kp)
                dpp_s[blk(n, h), :] = _dot_nt(doh, vp)
                if n < nb:
                    sc_s[blk(n, h), :] = _dot_nt(qh, k_ref[rs, cs])
                    dpc_s[blk(n, h), :] = _dot_nt(doh, v_ref[rs, cs])
        lse = jnp.concatenate(lses, axis=0)
        delta = jnp.concatenate(deltas, axis=0)
        row = lax.broadcasted_iota(jnp.int32, (nprev * CHUNK, CHUNK), 0) & (CHUNK - 1)
        col = lax.broadcasted_iota(jnp.int32, (nprev * CHUNK, CHUNK), 1)
        sp = jnp.where(col >= row, sp_s[...] * ATT_SCALE, NEG) + jnp.concatenate(pens, axis=0)
        pp = jnp.exp(sp - lse)
        pp_s[...] = pp.astype(BF16)
        dsp_s[...] = (pp * (dpp_s[...] - delta)).astype(BF16)
        nc = ncur * CHUNK
        sc = jnp.where(col[:nc] <= row[:nc], sc_s[...] * ATT_SCALE, NEG)
        pc = jnp.exp(sc - lse[:nc])
        pc_s[...] = pc.astype(BF16)
        dsc_s[...] = (pc * (dpc_s[...] - delta[:nc])).astype(BF16)
        for n in range(nb):
            rs, qr, dor, _, _ = rows_of(n)
            rn, qnr, donr, _, _ = rows_of(n + 1)
            for h in range(N_HEAD):
                cs = slice(h * HEAD, (h + 1) * HEAD)
                kp, _ = prev_kv(n, cs)
                dq = _dot(dsc_s[blk(n, h), :], k_ref[rs, cs]) + _dot(dsp_s[blk(n, h), :], kp)
                dq_ref[rs, cs] = (dq * ATT_SCALE).astype(BF16)
                dk = _dot_tn(dsc_s[blk(n, h), :], qr[rs, cs]) + _dot_tn(dsp_s[blk(n + 1, h), :], qnr[rn, cs])
                dk_ref[rs, cs] = (dk * ATT_SCALE).astype(BF16)
                dv = _dot_tn(pc_s[blk(n, h), :], dor[rs, cs]) + _dot_tn(pp_s[blk(n + 1, h), :], donr[rn, cs])
                dv_ref[rs, cs] = dv.astype(BF16)

    def prev_halo(cb):
        return pl.BlockSpec((CHUNK, WIDTH), lambda i: (jnp.maximum(i * nb - 1, 0), cb))

    def next_halo(width, cb=0):
        return pl.BlockSpec((CHUNK, width), lambda i: (jnp.minimum(i * nb + nb, nblk - 1), cb))

    return pl.pallas_call(
        body, name=f"attn_bwd_{bps}",
        grid=(S // tm,),
        in_specs=[_rows(tm, WIDTH, qcb), _rows(tm, WIDTH, kcb), _rows(tm, WIDTH, vcb), _rows(tm, WIDTH),
                  _rows(tm, 128), _rows(tm, 128), prev_halo(kcb), prev_halo(vcb),
                  next_halo(WIDTH, qcb), next_halo(WIDTH), next_halo(128), next_halo(128), ANY],
        out_specs=[_rows(tm, WIDTH, dcb), _rows(tm, WIDTH), _rows(tm, WIDTH)],
        out_shape=[jax.ShapeDtypeStruct(dst.shape, BF16)] + [jax.ShapeDtypeStruct((S, WIDTH), BF16)] * 2,
        input_output_aliases={12: 0},
        scratch_shapes=[pltpu.VMEM((ncur * CHUNK, CHUNK), F32), pltpu.VMEM((nprev * CHUNK, CHUNK), F32),
                        pltpu.VMEM((ncur * CHUNK, CHUNK), F32), pltpu.VMEM((nprev * CHUNK, CHUNK), F32),
                        pltpu.VMEM((ncur * CHUNK, CHUNK), BF16), pltpu.VMEM((nprev * CHUNK, CHUNK), BF16),
                        pltpu.VMEM((ncur * CHUNK, CHUNK), BF16), pltpu.VMEM((nprev * CHUNK, CHUNK), BF16)],
        compiler_params=_params(("parallel",)),
    )(q, k, v, do, lse, delta, k, v, q, do, lse, delta, dst)


def _dilated_split(d):
    hp = min(N_HEAD, 16 // d)
    return hp, N_HEAD // hp, HEAD * hp


def _strided_regroup(d):
    return d < 16


def _by_class(src_ref, dst, d, hp, nat):
    for j in range(hp):
        if _strided_regroup(d):
            nat[j] = src_ref[:, j * HEAD:(j + 1) * HEAD].astype(F32)
            for r in range(d):
                dst[j, r * CHUNK:(r + 1) * CHUNK, :] = nat.at[j][pl.ds(r, CHUNK, stride=d), :].astype(BF16)
        else:
            dst[j] = pltpu.einshape("(tr)l->(rt)l", src_ref[:, j * HEAD:(j + 1) * HEAD], r=d)


def _from_class(src, dst_ref, d, hp, nat, add_ref=None):
    for j in range(hp):
        cs = slice(j * HEAD, (j + 1) * HEAD)
        if _strided_regroup(d):
            for r in range(d):
                nat.at[j][pl.ds(r, CHUNK, stride=d), :] = src[j, r * CHUNK:(r + 1) * CHUNK, :]
            val = nat[j].astype(BF16)
        else:
            val = pltpu.einshape("(rt)l->(tr)l", src[j].astype(BF16), r=d)
        if add_ref is not None:
            val = (val.astype(F32) + add_ref[:, cs].astype(F32)).astype(BF16)
        dst_ref[:, cs] = val


def _attn_fwd_dilated(proj, qcb, kcb, vcb, d):
    S = proj.shape[0]
    T = CHUNK * d
    hp, nh, cw = _dilated_split(d)
    nblocks = d * hp

    def body(q_ref, k_ref, v_ref, o_ref, l_ref, qf, kst, vst, of, lf, nat, sc_s, sp_s, pc_s, pp_s):
        i, hh = pl.program_id(0), pl.program_id(1)
        kf, vf = kst.at[i % 2, hh], vst.at[i % 2, hh]
        kpf, vpf = kst.at[1 - i % 2, hh], vst.at[1 - i % 2, hh]

        @pl.when(i == 0)
        def _():
            kpf[...] = jnp.zeros_like(kpf)
            vpf[...] = jnp.zeros_like(vpf)

        _by_class(q_ref, qf, d, hp, nat)
        _by_class(k_ref, kf, d, hp, nat)
        _by_class(v_ref, vf, d, hp, nat)

        def blk(ref, r, j):
            return ref[j, r * CHUNK:(r + 1) * CHUNK, :]

        def bs(r, j):
            return slice((r * hp + j) * CHUNK, (r * hp + j + 1) * CHUNK)

        for r in range(d):
            for j in range(hp):
                qb = blk(qf, r, j)
                sc_s[bs(r, j), :] = _dot_nt(qb, blk(kf, r, j))
                sp_s[bs(r, j), :] = _dot_nt(qb, blk(kpf, r, j))
        row = lax.broadcasted_iota(jnp.int32, (nblocks * CHUNK, CHUNK), 0) & (CHUNK - 1)
        col = lax.broadcasted_iota(jnp.int32, (nblocks * CHUNK, CHUNK), 1)
        sc = jnp.where(col <= row, sc_s[...] * ATT_SCALE, NEG)
        sp = jnp.where(col >= row, sp_s[...] * ATT_SCALE, NEG) + jnp.where(i > 0, 0.0, NEG)
        m = jnp.maximum(jnp.max(sc, axis=-1, keepdims=True), jnp.max(sp, axis=-1, keepdims=True))
        ec = jnp.exp(sc - m)
        ep = jnp.exp(sp - m)
        den = jnp.sum(ec, axis=-1, keepdims=True) + jnp.sum(ep, axis=-1, keepdims=True)
        inv = 1.0 / den
        pc_s[...] = (ec * inv).astype(BF16)
        pp_s[...] = (ep * inv).astype(BF16)
        lse = m + jnp.log(den)
        lane = lax.broadcasted_iota(jnp.int32, (CHUNK, 128), 1)
        for r in range(d):
            lblk = jnp.zeros((CHUNK, 128), F32)
            for j in range(hp):
                o = _dot(pc_s[bs(r, j), :], blk(vf, r, j)) + _dot(pp_s[bs(r, j), :], blk(vpf, r, j))
                of[j, r * CHUNK:(r + 1) * CHUNK, :] = o
                lblk = jnp.where(lane == hh * hp + j, lse[bs(r, j)], lblk)
            lf[r * CHUNK:(r + 1) * CHUNK, :] = lblk
        _from_class(of, o_ref, d, hp, nat)
        lnat = pltpu.einshape("(rt)l->(tr)l", lf[...], r=d)

        @pl.when(hh == 0)
        def _():
            l_ref[...] = lnat

        @pl.when(hh > 0)
        def _():
            l_ref[...] += lnat

    def cols(cb):
        return pl.BlockSpec((T, cw), lambda i, hh: (i, cb * nh + hh))

    tile = pltpu.VMEM((hp, T, HEAD), BF16)
    staging = pltpu.VMEM((hp, T, HEAD) if _strided_regroup(d) else (1, 8, HEAD), F32)
    return pl.pallas_call(
        body, name=f"attn_fwd_dilated_{d}",
        grid=(S // T, nh),
        in_specs=[cols(qcb), cols(kcb), cols(vcb)],
        out_specs=[cols(0), pl.BlockSpec((T, 128), lambda i, hh: (i, 0))],
        out_shape=[jax.ShapeDtypeStruct((S, WIDTH), BF16), jax.ShapeDtypeStruct((S, 128), F32)],
        scratch_shapes=[tile, pltpu.VMEM((2, nh, hp, T, HEAD), BF16), pltpu.VMEM((2, nh, hp, T, HEAD), BF16),
                        pltpu.VMEM((hp, T, HEAD), F32), pltpu.VMEM((T, 128), F32), staging,
                        pltpu.VMEM((nblocks * CHUNK, CHUNK), F32), pltpu.VMEM((nblocks * CHUNK, CHUNK), F32),
                        pltpu.VMEM((nblocks * CHUNK, CHUNK), BF16), pltpu.VMEM((nblocks * CHUNK, CHUNK), BF16)],
        compiler_params=_params(("arbitrary", "arbitrary")),
    )(proj, proj, proj)


def _attn_bwd_dilated(proj, qcb, kcb, vcb, do, lse, delta, d, dk_in, dv_in, dst, dcb):
    S = proj.shape[0]
    T = CHUNK * d
    nt = S // T
    hp, nh, cw = _dilated_split(d)
    nblocks = d * hp

    def body(q_ref, k_ref, v_ref, do_ref, l_ref, d_ref, dki_ref, dvi_ref, dst_ref, dq_ref, dk_ref, dv_ref,
             qf, dof, kbuf, vbuf, dqf, gk, gv, nat,
             sc_s, sp_s, dpc_s, dpp_s, pc_s, pp_s, dsc_s, dsp_s):
        hh, i = pl.program_id(0), pl.program_id(1)
        kf, vf, newk, newv = kbuf.at[i % 2], vbuf.at[i % 2], gk.at[i % 2], gv.at[i % 2]
        kpf, vpf, acck, accv = kbuf.at[1 - i % 2], vbuf.at[1 - i % 2], gk.at[1 - i % 2], gv.at[1 - i % 2]

        @pl.when(i == 0)
        def _():
            for ref in (kbuf, vbuf, gk, gv):
                ref[...] = jnp.zeros_like(ref)
            dk_ref[...] = jnp.zeros_like(dk_ref)
            dv_ref[...] = jnp.zeros_like(dv_ref)

        def blk(ref, r, j):
            return ref[j, r * CHUNK:(r + 1) * CHUNK, :]

        def bs(r, j):
            return slice((r * hp + j) * CHUNK, (r * hp + j + 1) * CHUNK)

        @pl.when(i < nt)
        def _():
            _by_class(q_ref, qf, d, hp, nat)
            _by_class(do_ref, dof, d, hp, nat)
            _by_class(k_ref, kf, d, hp, nat)
            _by_class(v_ref, vf, d, hp, nat)
            lses, deltas = [], []
            lcls = pltpu.einshape("(tr)l->(rt)l", l_ref[...], r=d)
            dcls = pltpu.einshape("(tr)l->(rt)l", d_ref[...], r=d)
            for r in range(d):
                lblk = lcls[r * CHUNK:(r + 1) * CHUNK]
                dblk = dcls[r * CHUNK:(r + 1) * CHUNK]
                for j in range(hp):
                    lses.append(_col(lblk, hh * hp + j))
                    deltas.append(_col(dblk, hh * hp + j))
                    qb, dob = blk(qf, r, j), blk(dof, r, j)
                    sc_s[bs(r, j), :] = _dot_nt(qb, blk(kf, r, j))
                    dpc_s[bs(r, j), :] = _dot_nt(dob, blk(vf, r, j))
                    sp_s[bs(r, j), :] = _dot_nt(qb, blk(kpf, r, j))
                    dpp_s[bs(r, j), :] = _dot_nt(dob, blk(vpf, r, j))
            lse = jnp.concatenate(lses, axis=0)
            delta = jnp.concatenate(deltas, axis=0)
            row = lax.broadcasted_iota(jnp.int32, (nblocks * CHUNK, CHUNK), 0) & (CHUNK - 1)
            col = lax.broadcasted_iota(jnp.int32, (nblocks * CHUNK, CHUNK), 1)
            sp = jnp.where(col >= row, sp_s[...] * ATT_SCALE, NEG) + jnp.where(i > 0, 0.0, NEG)
            pp = jnp.exp(sp - lse)
            pp_s[...] = pp.astype(BF16)
            dsp_s[...] = (pp * (dpp_s[...] - delta)).astype(BF16)
            sc = jnp.where(col <= row, sc_s[...] * ATT_SCALE, NEG)
            pc = jnp.exp(sc - lse)
            pc_s[...] = pc.astype(BF16)
            dsc_s[...] = (pc * (dpc_s[...] - delta)).astype(BF16)
            for r in range(d):
                rows = slice(r * CHUNK, (r + 1) * CHUNK)
                for j in range(hp):
                    qb, dob = blk(qf, r, j), blk(dof, r, j)
                    dsc, dsp = dsc_s[bs(r, j), :], dsp_s[bs(r, j), :]
                    dqf[j, rows, :] = (_dot(dsc, blk(kf, r, j)) + _dot(dsp, blk(kpf, r, j))) * ATT_SCALE
                    newk[j, rows, :] = _dot_tn(dsc, qb) * ATT_SCALE
                    newv[j, rows, :] = _dot_tn(pc_s[bs(r, j), :], dob)
                    acck[j, rows, :] += _dot_tn(dsp, qb) * ATT_SCALE
                    accv[j, rows, :] += _dot_tn(pp_s[bs(r, j), :], dob)
            _from_class(dqf, dq_ref, d, hp, nat)

        @pl.when(i > 0)
        def _():
            _from_class(acck, dk_ref, d, hp, nat, dki_ref)
            _from_class(accv, dv_ref, d, hp, nat, dvi_ref)

    def cur(width, cb, nsplit):
        return pl.BlockSpec((T, width), lambda hh, i: (jnp.minimum(i, nt - 1), cb * nsplit + hh * (nsplit > 1)))

    def lag():
        return pl.BlockSpec((T, cw), lambda hh, i: (jnp.maximum(i - 1, 0), hh))

    tile = pltpu.VMEM((hp, T, HEAD), BF16)
    acc = pltpu.VMEM((hp, T, HEAD), F32)
    f32s = pltpu.VMEM((nblocks * CHUNK, CHUNK), F32)
    b16s = pltpu.VMEM((nblocks * CHUNK, CHUNK), BF16)
    return pl.pallas_call(
        body, name=f"attn_bwd_dilated_{d}",
        grid=(nh, nt + 1),
        in_specs=[cur(cw, qcb, nh), cur(cw, kcb, nh), cur(cw, vcb, nh), cur(cw, 0, nh), cur(128, 0, 1), cur(128, 0, 1),
                  lag(), lag(), ANY],
        out_specs=[cur(cw, dcb, nh), lag(), lag()],
        out_shape=[jax.ShapeDtypeStruct(dst.shape, BF16)] + [jax.ShapeDtypeStruct((S, WIDTH), BF16)] * 2,
        input_output_aliases={8: 0},
        scratch_shapes=[tile, tile, pltpu.VMEM((2, hp, T, HEAD), BF16), pltpu.VMEM((2, hp, T, HEAD), BF16), acc,
                        pltpu.VMEM((2, hp, T, HEAD), F32), pltpu.VMEM((2, hp, T, HEAD), F32),
                        acc if _strided_regroup(d) else pltpu.VMEM((1, 8, HEAD), F32)]
                       + [f32s] * 4 + [b16s] * 4,
        compiler_params=_params(("arbitrary", "arbitrary")),
    )(proj, proj, proj, do, lse, delta, dk_in, dv_in, dst)


def _abm_bwd(proj, cdf, dy3, ln_g, ln_b, wsm, wsm_t, bias_full, pool_w, pool_wt, pool_scale, kv, bands, bands_t):
    S = proj.shape[0]
    tm = 512
    nchunk = tm // CHUNK
    nblk = S // CHUNK

    def body(u_ref, v_ref, ag_ref, p_ref, ph_ref, pg_ref, pgn_ref, mq_ref, mg_ref, cdf_ref, dy_ref, dypn_ref,
             lng_ref, lnb_ref, wsm_ref, wsmt_ref, bias_ref, pw_ref, pwt_ref, ps_ref, kv_ref, band_ref, bandt_ref,
             dab_ref, dm_ref, dlng_ref, dlnb_ref, dws_ref, dbias_ref, dpw_ref, dps_ref, dkv_ref,
             mix, dvl, ddn):
        i = pl.program_id(0)

        @pl.when(i == 0)
        def _():
            for r in (dlng_ref, dlnb_ref, dws_ref, dbias_ref, dpw_ref, dps_ref, dkv_ref):
                r[...] = jnp.zeros_like(r)

        au = u_ref[...].astype(F32)
        av = v_ref[...].astype(F32)
        ag = ag_ref[...].astype(F32)
        u, du = _gelu_and_grad(au, cdf_ref[0].astype(F32))
        v, dgelu_v = _gelu_and_grad(av, cdf_ref[1].astype(F32))
        vhat, rstd = _layer_norm_fwd(v)
        vln = (vhat * lng_ref[...] + lnb_ref[...]).astype(BF16)
        for c in range(nchunk):
            for h in range(N_HEAD):
                rs, cs = slice(c * CHUNK, (c + 1) * CHUNK), slice(h * HEAD, (h + 1) * HEAD)
                mix[rs, cs] = _dot(wsm_ref[h], vln[rs, cs]) + bias_ref[:, cs]
        dya = dy_ref[0].astype(F32)
        sg, dsg = _silu_and_grad(ag)
        mixed = mix[...]
        dab_ref[:, 2 * WIDTH:3 * WIDTH] = (dya * u * mixed * dsg).astype(BF16)
        dab_ref[:, 0:WIDTH] = (dya * mixed * sg * du).astype(BF16)
        dmixed = dya * u * sg
        dmb = dmixed.astype(BF16)
        tril = (lax.broadcasted_iota(jnp.int32, (CHUNK, CHUNK), 1)
                <= lax.broadcasted_iota(jnp.int32, (CHUNK, CHUNK), 0))
        for c in range(nchunk):
            rs = slice(c * CHUNK, (c + 1) * CHUNK)
            dbias_ref[...] += dmixed[rs, :]
            for h in range(N_HEAD):
                cs = slice(h * HEAD, (h + 1) * HEAD)
                dvl[rs, cs] = _dot(wsmt_ref[h], dmb[rs, cs])
                dws_ref[h] += jnp.where(tril, _dot_nt(dmb[rs, cs], vln[rs, cs]), 0.0)
        dvln = dvl[...]
        dlng_ref[...] += jnp.sum(dvln * vhat, axis=0, keepdims=True)
        dlnb_ref[...] += jnp.sum(dvln, axis=0, keepdims=True)
        dvh = dvln * lng_ref[...]
        dv = rstd * (dvh - jnp.mean(dvh, axis=-1, keepdims=True)
                     - vhat * jnp.mean(dvh * vhat, axis=-1, keepdims=True))
        dab_ref[:, WIDTH:2 * WIDTH] = (dv * dgelu_v).astype(BF16)

        halo_ok = (i > 0).astype(F32)
        for c in range(nchunk):
            rs = slice(c * CHUNK, (c + 1) * CHUNK)
            for g, win in enumerate(POOL_WINDOWS):
                cs = slice(g * HEAD, (g + 1) * HEAD)
                cur = p_ref[rs, cs]
                if c == 0:
                    prev = (ph_ref[:, cs].astype(F32) * halo_ok).astype(BF16)
                else:
                    prev = p_ref[(c - 1) * CHUNK:c * CHUNK, cs]
                sums = _dot(band_ref[g, 0], cur) + _dot(band_ref[g, 1], prev)
                dvl[rs, cs] = sums * _inv_count(i * tm + c * CHUNK, win) - cur.astype(F32)
        dmat = dvl[...].astype(BF16)
        for g in range(4):
            cs = slice(g * HEAD, (g + 1) * HEAD)
            mix[:, cs] = _dot(dmat[:, cs], pw_ref[g])
        yg = mix[...]
        pg = pg_ref[...].astype(F32)
        dyp = dy_ref[1].astype(F32)
        spg, dspg = _silu_and_grad(pg)
        dyy = dyp * spg
        scale = ps_ref[...]
        dab_ref[:, 4 * WIDTH:5 * WIDTH] = (dyp * yg * scale * dspg).astype(BF16)
        dps_ref[...] += jnp.sum(dyy * yg, axis=0, keepdims=True)
        dyg = (dyy * scale).astype(BF16)
        for g in range(4):
            cs = slice(g * HEAD, (g + 1) * HEAD)
            dpw_ref[g] += _dot_tn(dmat[:, cs], dyg[:, cs])
            mix[:, cs] = _dot(dyg[:, cs], pwt_ref[g])
        next_ok = (i + 1 < S // tm).astype(F32)
        dygn = (dypn_ref[...].astype(F32) * _silu(pgn_ref[...].astype(F32)) * scale * next_ok).astype(BF16)
        for c in range(nchunk + 1):
            for g, win in enumerate(POOL_WINDOWS):
                cs = slice(g * HEAD, (g + 1) * HEAD)
                if c < nchunk:
                    dd = mix[c * CHUNK:(c + 1) * CHUNK, cs]
                else:
                    dd = _dot(dygn[:, cs], pwt_ref[g])
                ddn[c * CHUNK:(c + 1) * CHUNK, cs] = dd * _inv_count(i * tm + c * CHUNK, win)
        ddnb = ddn[...].astype(BF16)
        for c in range(nchunk):
            rs = slice(c * CHUNK, (c + 1) * CHUNK)
            ns = slice((c + 1) * CHUNK, (c + 2) * CHUNK)
            for g, win in enumerate(POOL_WINDOWS):
                cs = slice(g * HEAD, (g + 1) * HEAD)
                dp = _dot(bandt_ref[g, 0], ddnb[rs, cs]) + _dot(bandt_ref[g, 1], ddnb[ns, cs]) - mix[rs, cs]
                dab_ref[rs, 3 * WIDTH + g * HEAD:3 * WIDTH + (g + 1) * HEAD] = dp.astype(BF16)

        mg = mg_ref[...].astype(F32)
        dym = dy_ref[2].astype(F32)
        smg, dsmg = _silu_and_grad(mg)
        dob = (dym * smg).astype(BF16)
        for h in range(N_HEAD):
            cs = slice(h * HEAD, (h + 1) * HEAD)
            vs = slice(WIDTH + h * HEAD, WIDTH + (h + 1) * HEAD)
            qh = mq_ref[:, cs]
            p = _mem_softmax(qh, kv_ref[:, cs])
            pb = p.astype(BF16)
            mix[:, cs] = _dot(pb, kv_ref[:, vs])
            dp = _dot_nt(dob[:, cs], kv_ref[:, vs])
            ds = (p * (dp - jnp.sum(p * dp, axis=-1, keepdims=True))).astype(BF16)
            dm_ref[:, cs] = (_dot(ds, kv_ref[:, cs]) * ATT_SCALE).astype(BF16)
            dkv_ref[:, cs] += _dot_tn(ds, qh) * ATT_SCALE
            dkv_ref[:, vs] += _dot_tn(pb, dob[:, cs])
        dm_ref[:, WIDTH:2 * WIDTH] = (dym * mix[...] * dsmg).astype(BF16)

    blk = tm // CHUNK
    small = [_full((1, WIDTH)), _full((1, WIDTH)), _full((N_HEAD, CHUNK, CHUNK)), _full((CHUNK, WIDTH)),
             _full((4, HEAD, HEAD)), _full((1, WIDTH)), _full((MEM_LEN, 2 * WIDTH))]
    return pl.pallas_call(
        body, name="abm_bwd",
        grid=(S // tm,),
        in_specs=[_rows(tm, WIDTH, CB_U), _rows(tm, WIDTH, CB_V), _rows(tm, WIDTH, CB_AGATE),
                  _rows(tm, WIDTH, CB_PIN),
                  pl.BlockSpec((CHUNK, WIDTH), lambda i: (jnp.maximum(i * blk - 1, 0), CB_PIN)),
                  _rows(tm, WIDTH, CB_PGATE),
                  pl.BlockSpec((CHUNK, WIDTH), lambda i: (jnp.minimum(i * blk + blk, nblk - 1), CB_PGATE)),
                  _rows(tm, WIDTH, CB_MQ), _rows(tm, WIDTH, CB_MGATE),
                  pl.BlockSpec((2, tm, WIDTH), lambda i: (0, i, 0)),
                  pl.BlockSpec((3, tm, WIDTH), lambda i: (0, i, 0)),
                  pl.BlockSpec((None, CHUNK, WIDTH), lambda i: (1, jnp.minimum(i * blk + blk, nblk - 1), 0)),
                  _full((1, WIDTH)), _full((1, WIDTH)), _full((N_HEAD, CHUNK, CHUNK)), _full((N_HEAD, CHUNK, CHUNK)),
                  _full((CHUNK, WIDTH)), _full((4, HEAD, HEAD)), _full((4, HEAD, HEAD)), _full((1, WIDTH)),
                  _full((MEM_LEN, 2 * WIDTH)), _full((4, 2, CHUNK, CHUNK)), _full((4, 2, CHUNK, CHUNK))],
        out_specs=[_rows(tm, 5 * WIDTH), _rows(tm, 2 * WIDTH)] + small,
        out_shape=[jax.ShapeDtypeStruct((S, D_BRANCHES), BF16), jax.ShapeDtypeStruct((S, 2 * WIDTH), BF16),
                   jax.ShapeDtypeStruct((1, WIDTH), F32), jax.ShapeDtypeStruct((1, WIDTH), F32),
                   jax.ShapeDtypeStruct((N_HEAD, CHUNK, CHUNK), F32), jax.ShapeDtypeStruct((CHUNK, WIDTH), F32),
                   jax.ShapeDtypeStruct((4, HEAD, HEAD), F32), jax.ShapeDtypeStruct((1, WIDTH), F32),
                   jax.ShapeDtypeStruct((MEM_LEN, 2 * WIDTH), F32)],
        scratch_shapes=[pltpu.VMEM((tm, WIDTH), F32), pltpu.VMEM((tm, WIDTH), F32),
                        pltpu.VMEM((tm + CHUNK, WIDTH), F32)],
        compiler_params=_params(("arbitrary",)),
    )(proj, proj, proj, proj, proj, proj, proj, proj, proj, cdf, dy3, dy3,
      ln_g, ln_b, wsm, wsm_t, bias_full, pool_w, pool_wt, pool_scale, kv, bands, bands_t)


def _bias_reduce(dbias_full):
    def body(d_ref, o_ref):
        d = d_ref[...]
        o_ref[...] = _put_cols([jnp.sum(d[:, h * HEAD:(h + 1) * HEAD], axis=1, keepdims=True) for h in range(N_HEAD)])

    return pl.pallas_call(body, name="bias_reduce", out_shape=jax.ShapeDtypeStruct((CHUNK, 128), F32))(dbias_full)


def _mem_bwd(mem, g, mem_n, w, dkv):
    def body(m_ref, g_ref, mn_ref, w_ref, dkv_ref, dw_ref, dg_ref):
        dkvb = dkv_ref[...].astype(BF16)
        dw_ref[...] = _dot_tn(mn_ref[...], dkvb).astype(BF16)
        dmn = _dot_nt(dkvb, w_ref[...])
        xf = m_ref[...]
        r = lax.rsqrt(jnp.mean(xf * xf, axis=-1, keepdims=True) + EPS)
        dg_ref[...] = jnp.sum(dmn * xf * r, axis=0, keepdims=True)

    return pl.pallas_call(
        body, name="mem_bwd",
        out_shape=[jax.ShapeDtypeStruct((D_MODEL, 2 * WIDTH), BF16), jax.ShapeDtypeStruct((1, D_MODEL), F32)],
        compiler_params=pltpu.CompilerParams(vmem_limit_bytes=VMEM_LIMIT),
    )(mem, g, mem_n, w, dkv)


def _dh_bwd(dpb, dpg, wt, wgt, x, g, dxo, parts=(), grads=()):
    S = x.shape[0]
    tm, tkb, tkg = 1024, D_BRANCHES // 4, D_GATES // 4
    nkb, nkg = 4, 4
    nk = nkb + nkg
    ni = S // tm
    n, m = len(parts), len(grads)

    def body(dpb_ref, wbr_ref, dpg_ref, wg_ref, x_ref, g_ref, dxo_ref, *rest):
        p_in, g_in = rest[:n], rest[n:n + m]
        dx_ref, dg_ref = rest[n + m:n + m + 2]
        p_out, g_out = rest[n + m + 2:2 * n + m + 2], rest[2 * n + m + 2:2 * (n + m) + 2]
        acc, sems = rest[2 * (n + m) + 2], rest[2 * (n + m) + 3:]
        second_sems, first_sems = (sems[:3] if n else ()), sems[3 if n else 0:]
        i, kk = pl.program_id(0), pl.program_id(1)

        @pl.when(jnp.logical_and(i == 0, kk == 0))
        def _():
            dg_ref[...] = jnp.zeros_like(dg_ref)
            if n:
                _comm_start(_rs_second(p_in, p_out, *second_sems))
            if m:
                _comm_start(_rs_first(g_in, g_out, *first_sems))

        @pl.when(kk == 0)
        def _():
            acc[...] = jnp.zeros_like(acc)

        @pl.when(kk < nkb)
        def _():
            acc[...] += _dot(dpb_ref[...], wbr_ref[...])

        @pl.when(kk >= nkb)
        def _():
            acc[...] += _dot(dpg_ref[...], wg_ref[...])

        @pl.when(kk == nk - 1)
        def _():
            xf = x_ref[...]
            r = lax.rsqrt(jnp.mean(xf * xf, axis=-1, keepdims=True) + EPS)
            xhat = xf * r
            dh = acc[...]
            dg_ref[...] += jnp.sum(dh * xhat, axis=0, keepdims=True)
            dxh = dh * g_ref[...]
            dx_ref[...] = dxo_ref[...] + r * (dxh - xhat * jnp.mean(dxh * xhat, axis=-1, keepdims=True))

        if n or m:
            @pl.when(jnp.logical_and(i == ni - 1, kk == nk - 1))
            def _():
                if n:
                    _comm_wait(_rs_second(p_in, p_out, *second_sems))
                if m:
                    _comm_wait(_rs_first(g_in, g_out, *first_sems))

    res = pl.pallas_call(
        body, name="dh_bwd_scatter" if (n or m) else "dh_bwd",
        grid=(ni, nk),
        in_specs=[pl.BlockSpec((tm, tkb), lambda i, k: (i, jnp.minimum(k, nkb - 1))),
                  pl.BlockSpec((tkb, D_MODEL), lambda i, k: (jnp.minimum(k, nkb - 1), 0)),
                  pl.BlockSpec((tm, tkg), lambda i, k: (i, jnp.maximum(k - nkb, 0))),
                  pl.BlockSpec((tkg, D_MODEL), lambda i, k: (jnp.maximum(k - nkb, 0), 0)),
                  pl.BlockSpec((tm, D_MODEL), lambda i, k: (i, 0)), pl.BlockSpec((1, D_MODEL), lambda i, k: (0, 0)),
                  pl.BlockSpec((tm, D_MODEL), lambda i, k: (i, 0))] + [ANY] * (n + m),
        out_specs=[pl.BlockSpec((tm, D_MODEL), lambda i, k: (i, 0)), pl.BlockSpec((1, D_MODEL), lambda i, k: (0, 0))]
                  + [ANY] * (n + m),
        out_shape=[jax.ShapeDtypeStruct((S, D_MODEL), F32), jax.ShapeDtypeStruct((1, D_MODEL), F32)]
                  + [jax.ShapeDtypeStruct(p.shape, p.dtype) for p in parts]
                  + [jax.ShapeDtypeStruct(gr.shape[:1] + gr.shape[2:], gr.dtype) for gr in grads],
        scratch_shapes=[pltpu.VMEM((tm, D_MODEL), F32)] + (_dma_sems(3 * n, 3 * n, n) if n else [])
                       + (_dma_sems(N_CHIP * m, N_CHIP * m) if m else []),
        compiler_params=_params(("arbitrary", "arbitrary")),
    )(dpb, wt, dpg, wgt, x, g, dxo, *parts, *grads)
    return res[0], res[1], list(res[2:2 + n]), list(res[2 + n:])


def _dw_in(h, dpb, dpg, parts=()):
    S = h.shape[0]
    tk = 2048
    nk = S // tk
    n = len(parts)
    tmb = D_BRANCHES // 4
    ng = D_GATES // GATE_TILE

    def accumulate(a_ref, h_ref, o_ref, acc):
        kk = pl.program_id(1)

        @pl.when(kk == 0)
        def _():
            acc[...] = jnp.zeros_like(acc)

        acc[...] += _dot_tn(a_ref[...], h_ref[...])

        @pl.when(kk == nk - 1)
        def _():
            o_ref[...] = acc[...].astype(BF16)

    def branches(a_ref, h_ref, *rest):
        p_in, o_ref, p_out = rest[:n], rest[n], rest[n + 1:2 * n + 1]
        acc, sems = rest[2 * n + 1], rest[2 * n + 2:]
        i, kk = pl.program_id(0), pl.program_id(1)

        if n:
            @pl.when(jnp.logical_and(i == 0, kk == 0))
            def _():
                _comm_start(_rs_second(p_in, p_out, *sems))

        accumulate(a_ref, h_ref, o_ref, acc)

        if n:
            @pl.when(jnp.logical_and(i == 3, kk == nk - 1))
            def _():
                _comm_wait(_rs_second(p_in, p_out, *sems))

    def gates(a_ref, h_ref, dst_ref, o_ref, acc):
        accumulate(a_ref, h_ref, o_ref, acc)

    res = pl.pallas_call(
        branches, name="dw_in_branches_scatter" if n else "dw_in_branches",
        grid=(4, nk),
        in_specs=[pl.BlockSpec((tk, tmb), lambda i, k: (k, i)), pl.BlockSpec((tk, D_MODEL), lambda i, k: (k, 0))]
                 + [ANY] * n,
        out_specs=[pl.BlockSpec((tmb, D_MODEL), lambda i, k: (i, 0))] + [ANY] * n,
        out_shape=[jax.ShapeDtypeStruct((D_IN, D_MODEL), BF16)]
                  + [jax.ShapeDtypeStruct(p.shape, p.dtype) for p in parts],
        scratch_shapes=[pltpu.VMEM((tmb, D_MODEL), F32)] + (_dma_sems(3 * n, 3 * n, n) if n else []),
        compiler_params=_params(("arbitrary", "arbitrary")),
    )(dpb, h, *parts)
    dwt = pl.pallas_call(
        gates, name="dw_in_gates",
        grid=(ng, nk),
        in_specs=[pl.BlockSpec((tk, GATE_TILE), lambda i, k: (k, i)), pl.BlockSpec((tk, D_MODEL), lambda i, k: (k, 0)),
                  ANY],
        out_specs=pl.BlockSpec((GATE_TILE, D_MODEL), lambda i, k: (D_BRANCHES // GATE_TILE + i, 0)),
        out_shape=jax.ShapeDtypeStruct((D_IN, D_MODEL), BF16),
        input_output_aliases={2: 0},
        scratch_shapes=[pltpu.VMEM((GATE_TILE, D_MODEL), F32)],
        compiler_params=_params(("parallel", "arbitrary")),
    )(dpg, h, res[0])
    return dwt, list(res[1:])


def _row_tile(R, C, block_bytes=2 << 20):
    for cand in range(min(R, block_bytes // (C * 4)) // 8 * 8, 0, -8):
        if R % cand == 0:
            return cand
    return R


def _adamw_update(p_ref, w_ref, m_ref, v_ref, g_ref, d_ref, nm_ref, nv_ref):
    c1 = 1.0 / (1.0 - ADAM_B1 ** ADAM_STEP)
    c2 = 1.0 / (1.0 - ADAM_B2 ** ADAM_STEP)
    g = p_ref[0].astype(F32)
    for k in range(1, p_ref.shape[0]):
        g = g + p_ref[k].astype(F32)
    nm = ADAM_B1 * m_ref[...] + (1.0 - ADAM_B1) * g
    nv = ADAM_B2 * v_ref[...] + (1.0 - ADAM_B2) * (g * g)
    g_ref[...] = g
    nm_ref[...] = nm
    nv_ref[...] = nv
    d_ref[...] = -ADAM_LR * ((nm * c1) / (jnp.sqrt(nv * c2) + ADAM_EPS) + ADAM_WD * w_ref[...])


def _adamw(parts, w, m, v, name):
    P, R, C = parts.shape
    tr = _row_tile(R, C)

    def body(*refs):
        _adamw_update(*refs)

    spec = pl.BlockSpec((tr, C), lambda i: (i, 0))
    return pl.pallas_call(
        body, name=name,
        grid=(R // tr,),
        in_specs=[pl.BlockSpec((P, tr, C), lambda i: (0, i, 0)), spec, spec, spec],
        out_specs=[spec] * 4,
        out_shape=[jax.ShapeDtypeStruct((R, C), F32)] * 4,
        compiler_params=_params(("parallel",)),
    )(parts, w, m, v)


def _adamw_layers(parts, w, m, v, name):
    depth = len(parts)
    P, R, C = parts[0].shape
    tr = _row_tile(R, C, 1 << 20)

    def body(*refs):
        layer = pl.program_id(0)
        for k in range(depth):
            @pl.when(layer == k)
            def _(k=k):
                _adamw_update(refs[k], *refs[depth:])

    def part_spec(k):
        return pl.BlockSpec((P, tr, C), lambda l, i: (0, jnp.where(l == k, i, 0), 0))

    spec = pl.BlockSpec((None, tr, C), lambda l, i: (l, i, 0))
    return pl.pallas_call(
        body, name=name,
        grid=(depth, R // tr),
        in_specs=[part_spec(k) for k in range(depth)] + [spec] * 3,
        out_specs=[spec] * 4,
        out_shape=[jax.ShapeDtypeStruct((depth, R, C), F32)] * 4,
        compiler_params=_params(("arbitrary", "arbitrary")),
    )(*parts, w, m, v)


def _place():
    return lax.axis_index("x"), lax.axis_index("y"), lax.axis_index("c")


def _all_gather(shards):
    n = len(shards)

    def body(*refs):
        ins, outs = refs[:n], refs[n:2 * n]
        send1, recv1, local_sems, send2, recv2 = refs[2 * n:]
        first = _ag_first(ins, outs, send1, recv1, local_sems)
        second = _ag_second(outs, send2, recv2)
        _comm_start(first)
        for j in range(3):
            for a in range(n):
                first[2][4 * a + 1 + j].wait_recv()
            for a in range(n):
                second[1][3 * a + j].start()
        for a in range(n):
            first[2][4 * a].wait_recv()
        for cp in second[2]:
            cp.wait_recv()
        for cp in first[1] + second[1]:
            cp.wait_send()
        for cp in first[0]:
            cp.wait()

    return pl.pallas_call(
        body, name="weights_all_gather",
        in_specs=[ANY] * n, out_specs=[ANY] * n,
        out_shape=[jax.ShapeDtypeStruct((N_DEV,) + s.shape, s.dtype) for s in shards],
        scratch_shapes=_dma_sems(4 * n, 4 * n, n, 3 * n, 3 * n),
        compiler_params=pltpu.CompilerParams(has_side_effects=True),
    )(*shards)


N_BIG = 4


def _dev(p):
    return 4 * p[0] + 2 * p[1] + p[2]


def _other_chips(x, y):
    return [(1 - x, y), (x, 1 - y), (1 - x, 1 - y)]


def _remote(src, dst, send_sems, recv_sems, k, to):
    return pltpu.make_async_remote_copy(src_ref=src, dst_ref=dst, send_sem=send_sems.at[k], recv_sem=recv_sems.at[k],
                                        device_id=to, device_id_type=MESH)


def _ag_first(ins, outs, send_sems, recv_sems, local_sems):
    x, y, c = _place()
    me = (x, y, c)
    targets = [(x, y, 1 - c)] + [(*chip, c) for chip in _other_chips(x, y)]
    local, out, inc = [], [], []
    for a in range(len(ins)):
        local.append(pltpu.make_async_copy(ins[a], outs[a].at[_dev(me)], local_sems.at[a]))
        for k, to in enumerate(targets):
            out.append(_remote(ins[a], outs[a].at[_dev(me)], send_sems, recv_sems, 4 * a + k, to))
            inc.append(_remote(ins[a], outs[a].at[_dev(to)], send_sems, recv_sems, 4 * a + k, to))
    return local, out, inc


def _ag_second(bufs, send_sems, recv_sems):
    x, y, c = _place()
    out, inc = [], []
    for a in range(len(bufs)):
        for j, chip in enumerate(_other_chips(x, y)):
            mine, theirs = bufs[a].at[_dev((*chip, c))], bufs[a].at[_dev((*chip, 1 - c))]
            out.append(_remote(mine, mine, send_sems, recv_sems, 3 * a + j, (x, y, 1 - c)))
            inc.append(_remote(theirs, theirs, send_sems, recv_sems, 3 * a + j, (x, y, 1 - c)))
    return [], out, inc


def _rs_first(ins, outs, send_sems, recv_sems):
    x, y, c = _place()
    out = [_remote(ins[a].at[j, 1 - c], outs[a].at[j], send_sems, recv_sems, N_CHIP * a + j, (x, y, 1 - c))
           for a in range(len(ins)) for j in range(N_CHIP)]
    return [], out, out


def _rs_second(ins, outs, send_sems, recv_sems, local_sems):
    x, y, c = _place()
    my_chip = 2 * x + y
    local, out, inc = [], [], []
    for a in range(len(ins)):
        local.append(pltpu.make_async_copy(ins[a].at[my_chip], outs[a].at[my_chip], local_sems.at[a]))
        for k, (ox, oy) in enumerate(_other_chips(x, y)):
            out.append(_remote(ins[a].at[2 * ox + oy], outs[a].at[my_chip], send_sems, recv_sems, 3 * a + k, (ox, oy, c)))
            inc.append(_remote(ins[a].at[2 * ox + oy], outs[a].at[2 * ox + oy], send_sems, recv_sems, 3 * a + k,
                               (ox, oy, c)))
    return local, out, inc


def _comm_start(exchange):
    local, out, _ = exchange
    for cp in local + out:
        cp.start()


def _comm_wait(exchange):
    local, out, inc = exchange
    for cp in inc:
        cp.wait_recv()
    for cp in out:
        cp.wait_send()
    for cp in local:
        cp.wait()


def _dma_sems(*counts):
    return [pltpu.SemaphoreType.DMA((n,)) for n in counts]


def _pair_sum(grads, recvs):
    n = len(grads)

    def body(c_ref, *refs):
        for a in range(n):
            refs[2 * n + a][...] = (refs[a][...].astype(F32) + refs[n + a][...].astype(F32)).astype(BF16)

    def g_spec(g):
        return pl.BlockSpec((None, None) + g.shape[2:], lambda j, c_ref: (j, c_ref[0], 0, 0))

    def r_spec(r):
        return pl.BlockSpec((None,) + r.shape[1:], lambda j, c_ref: (j, 0, 0))

    return pl.pallas_call(
        body, name="pair_sum",
        grid_spec=pltpu.PrefetchScalarGridSpec(
            num_scalar_prefetch=1, grid=(N_CHIP,),
            in_specs=[g_spec(g) for g in grads] + [r_spec(r) for r in recvs],
            out_specs=[r_spec(r) for r in recvs]),
        out_shape=[jax.ShapeDtypeStruct(r.shape, BF16) for r in recvs],
        compiler_params=_params(("parallel",)),
    )(lax.axis_index("c").reshape(1).astype(jnp.int32), *grads, *recvs)


SMALL_ROWS = 544


def _all_reduce_small(buf, parts=()):
    n = len(parts)

    def body(in_ref, *rest):
        p_in, out_ref, p_out = rest[:n], rest[n], rest[n + 1:2 * n + 1]
        recv, acc, send1, recv1, send2, recv2 = rest[2 * n + 1:2 * n + 7]
        scatter_sems = rest[2 * n + 7:]
        x, y, c = _place()
        me = 4 * x + 2 * y + c
        peers = [(x ^ (r >> 2), y ^ ((r >> 1) & 1), c ^ (r & 1)) for r in range(1, N_DEV)]

        def idx(p):
            return 4 * p[0] + 2 * p[1] + p[2]

        if n:
            _comm_start(_rs_second(p_in, p_out, *scatter_sems))
        first = [pltpu.make_async_remote_copy(
            src_ref=in_ref.at[idx(p)], dst_ref=recv.at[me], send_sem=send1.at[r], recv_sem=recv1.at[r],
            device_id=p, device_id_type=MESH) for r, p in enumerate(peers)]
        for cp in first:
            cp.start()
        recv[me] = in_ref[me]
        for r, p in enumerate(peers):
            pltpu.make_async_remote_copy(
                src_ref=in_ref.at[idx(p)], dst_ref=recv.at[idx(p)], send_sem=send1.at[r], recv_sem=recv1.at[r],
                device_id=p, device_id_type=MESH).wait_recv()
        total = recv[0]
        for k in range(1, N_DEV):
            total = total + recv[k]
        acc[...] = total
        out_ref[me] = total
        second = [pltpu.make_async_remote_copy(
            src_ref=acc, dst_ref=out_ref.at[me], send_sem=send2.at[r], recv_sem=recv2.at[r],
            device_id=p, device_id_type=MESH) for r, p in enumerate(peers)]
        for cp in second:
            cp.start()
        for r, p in enumerate(peers):
            pltpu.make_async_remote_copy(
                src_ref=acc, dst_ref=out_ref.at[idx(p)], send_sem=send2.at[r], recv_sem=recv2.at[r],
                device_id=p, device_id_type=MESH).wait_recv()
        for cp in first + second:
            cp.wait_send()
        if n:
            _comm_wait(_rs_second(p_in, p_out, *scatter_sems))

    vm = pl.BlockSpec(memory_space=pltpu.VMEM)
    res = pl.pallas_call(
        body, name="small_grads_all_reduce",
        in_specs=[vm] + [ANY] * n, out_specs=[vm] + [ANY] * n,
        out_shape=[jax.ShapeDtypeStruct(buf.shape, F32)] + [jax.ShapeDtypeStruct(p.shape, p.dtype) for p in parts],
        scratch_shapes=[pltpu.VMEM(buf.shape, F32), pltpu.VMEM(buf.shape[1:], F32)] + _dma_sems(7, 7, 7, 7)
                       + (_dma_sems(3 * n, 3 * n, n) if n else []),
        compiler_params=pltpu.CompilerParams(has_side_effects=True, vmem_limit_bytes=VMEM_LIMIT),
    )(buf, *parts)
    return res[0], list(res[1:])


def _dilate(a, d):
    if d == 1:
        return a
    S, C = a.shape
    return a.reshape(S // d, d, C).transpose(1, 0, 2).reshape(S, C)


def _undilate(a, d):
    if d == 1:
        return a
    S, C = a.shape
    return a.reshape(d, S // d, C).transpose(1, 0, 2).reshape(S, C)


def _cols(a, cb, n=1):
    return a[:, cb * WIDTH:(cb + n) * WIDTH]


def _to_blocks(g, kind):
    if kind == "rows":
        C = g.shape[1]
        return g.reshape(N_CHIP, 2, -1, C)
    return g.reshape(4 * WIDTH, N_CHIP, 2, -1).transpose(1, 2, 0, 3)


SMALL = ("norm_g", "gm_ln_g", "gm_ln_b", "gm_ws", "gm_bs", "pool_w", "pool_scale", "mem_norm_g", "final_norm_g")


def _pack_small(tree):
    flat = jnp.concatenate([tree[k].reshape(-1, 128) for k in SMALL], axis=0)
    return jnp.pad(flat, ((0, N_DEV * SMALL_ROWS - flat.shape[0]), (0, 0)))


def _unpack_small(flat, like):
    out, at = {}, 0
    for k in SMALL:
        rows = like[k].size // 128
        out[k] = flat[at:at + rows].reshape(like[k].shape)
        at += rows
    return out


def _make_layer(wt, wkv, wb, wout, norm_g, mem_norm_g, ln_g, ln_b, gm_ws, gm_bs, pool_w, pool_scale):
    tril = jnp.tril(jnp.ones((CHUNK, CHUNK), bool))
    wsm = jnp.where(tril, gm_ws, 0.0).astype(BF16)
    pw = pool_w.astype(BF16)
    bands, bands_t = _band_constants()
    return dict(wt=wt, wgt=wt[D_BRANCHES:], wkv=wkv, wb=wb, wout=wout, g=norm_g[None], mg=mem_norm_g[None],
                ln_g=ln_g[None],
                ln_b=ln_b[None], wsm=wsm, wsm_t=wsm.transpose(0, 2, 1), pw=pw, pw_t=pw.transpose(0, 2, 1),
                ps=pool_scale[None], bias=jnp.repeat(gm_bs.T, HEAD, axis=1), bands=bands, bands_t=bands_t)


def _layer_fwd(xl, mem0, L, next_shards=()):
    S = xl.shape[0]
    proj, gates, h, half_gathered = _in_proj(xl, L["g"], L["wt"], L["wgt"], next_shards[:1])
    kv, mem_n = _mem_kv(mem0, L["mg"], L["wkv"])
    y4, cdf, gathered = _abm_fwd(proj, L["ln_g"], L["ln_b"], L["wsm"], L["bias"], L["pw"], L["ps"], kv, L["bands"],
                                 half_gathered)
    o_g, l_g = [], []
    for gi, d in enumerate(DILATIONS):
        if d == 1:
            o, lse = _attn_fwd(proj, CB_Q0, proj, CB_K, proj, CB_CV, S // CHUNK)
        else:
            o, lse = _attn_fwd_dilated(proj, CB_Q0 + gi, CB_K, CB_CV, d)
        o_g.append(o)
        l_g.append(lse)
    (xn, y4, oc, lse, z), rest = _merge_fwd(xl, y4, o_g, l_g, proj, gates, L["wb"], L["wout"], next_shards[1:])
    saved = dict(x=xl, proj=proj, gates=gates, h=h, kv=kv, mem_n=mem_n, y4=y4, cdf=cdf, oc=oc, lse=lse, z=z)
    return xn, saved, gathered + rest


def _place_cols(dst, piece, cb):
    return lax.dynamic_update_slice(dst, piece, (0, cb * WIDTH))


def _layer_bwd(dx, mem0, L, sv, later=(), last=False):
    S = dx.shape[0]
    proj = sv["proj"]
    (dy3, doc, delta, dcg, dgm, dwb, dwout), from_sibling = _merge_bwd(
        dx, sv["y4"], sv["oc"], sv["z"], proj, sv["gates"], L["wb"], L["wout"], later)
    pair = _pair_sum(later, from_sibling) if later else ()
    dpb, dm, dlng, dlnb, dws, dbias, dpw, dps, dkv = _abm_bwd(
        proj, sv["cdf"], dy3, L["ln_g"], L["ln_b"], L["wsm"], L["wsm_t"], L["bias"], L["pw"], L["pw_t"], L["ps"], sv["kv"],
        L["bands"], L["bands_t"])
    dk, dv = None, None
    for gi, d in enumerate(DILATIONS):
        if d == 1:
            dpb, dk, dv = _attn_bwd(proj, CB_Q0, proj, CB_K, proj, CB_CV, doc, sv["lse"], delta, S // CHUNK,
                                    dpb, CB_Q0)
        else:
            dpb, dk, dv = _attn_bwd_dilated(proj, CB_Q0 + gi, CB_K, CB_CV, doc, sv["lse"], delta, d, dk, dv,
                                            dpb, CB_Q0 + gi)
    dpb = _place_cols(dpb, dk, CB_K)
    dpb = _place_cols(dpb, dv, CB_CV)
    dpb = _place_cols(dpb, dcg, CB_CGATE)
    dpb = _place_cols(dpb, dm, CB_MQ)
    dwkv, dmg = _mem_bwd(mem0, L["mg"], sv["mem_n"], L["wkv"], dkv)
    dwin_t, parts_rest = _dw_in(sv["h"], dpb, dgm, pair[1:])
    big = _blocked(dict(w_in=dwin_t, w_mem_kv=dwkv, w_branch=dwb, w_out=dwout))
    dxi, dng, parts, from_sibling = _dh_bwd(dpb, dgm, L["wt"], L["wgt"], sv["x"], L["g"], dx, pair[:1],
                                            big if last else ())
    parts = parts + parts_rest
    small = dict(norm_g=dng[0], gm_ln_g=dlng[0], gm_ln_b=dlnb[0], gm_ws=dws,
                 gm_bs=_bias_reduce(dbias)[:, :N_HEAD].T, pool_w=dpw, pool_scale=dps[0], mem_norm_g=dmg[0])
    return dxi, big, small, parts, from_sibling


BIG = ("w_in", "w_mem_kv", "w_branch", "w_out")


def _blocked(big):
    return [_to_blocks(big["w_in"], "rows"), _to_blocks(big["w_mem_kv"], "rows"),
            _to_blocks(big["w_branch"], "branch"), _to_blocks(big["w_out"], "rows")]


def _full_weights(gathered):
    win_t, wkv, wb, wout = gathered
    return (win_t.reshape(D_IN, D_MODEL), wkv.reshape(D_MODEL, 2 * WIDTH),
            wb.reshape(N_DEV, 4, WIDTH, -1).transpose(1, 2, 0, 3).reshape(4, WIDTH, D_MODEL),
            wout.reshape(D_MODEL, D_MODEL))


def kernel(x, mem, norm_g, w_in, gm_ln_g, gm_ln_b, gm_ws, gm_bs, pool_w, pool_scale, mem_norm_g, w_mem_kv, w_branch, w_out, final_norm_g, loss_target, m_norm_g, m_w_in, m_gm_ln_g, m_gm_ln_b, m_gm_ws, m_gm_bs, m_pool_w, m_pool_scale, m_mem_norm_g, m_w_mem_kv, m_w_branch, m_w_out, m_final_norm_g, v_norm_g, v_w_in, v_gm_ln_g, v_gm_ln_b, v_gm_ws, v_gm_bs, v_pool_w, v_pool_scale, v_mem_norm_g, v_w_mem_kv, v_w_branch, v_w_out, v_final_norm_g):
    x0 = x[0]
    mem0 = mem[0]
    tgt = loss_target[0]
    S = x0.shape[0]

    shards = [[w_in[l].T.astype(BF16), w_mem_kv[l].astype(BF16), w_branch[l].astype(BF16).reshape(4 * WIDTH, -1),
               w_out[l].astype(BF16)] for l in range(DEPTH)]
    gathered = _all_gather(shards[0])
    layers, saved = [], []
    xl = x0
    for l in range(DEPTH):
        layers.append(_make_layer(*_full_weights(gathered), norm_g[l], mem_norm_g[l], gm_ln_g[l], gm_ln_b[l],
                                  gm_ws[l], gm_bs[l], pool_w[l], pool_scale[l]))
        xl, sv, gathered = _layer_fwd(xl, mem0, layers[l], shards[l + 1] if l + 1 < DEPTH else ())
        saved.append(sv)

    loss_part, dx, d_final = _loss_head(xl, final_norm_g[None], tgt)
    loss = lax.psum(loss_part[0, 0], ("x", "y", "c"))

    small = {k: [None] * DEPTH for k in SMALL if k != "final_norm_g"}
    parts = [None] * DEPTH
    later = ()
    for l in reversed(range(DEPTH)):
        dx, gb, gs, done, from_sibling = _layer_bwd(dx, mem0, layers[l], saved[l], later, last=(l == 0))
        if later:
            parts[l + 1] = done
        later = gb
        for k in gs:
            small[k][l] = gs[k]
    grad_x = dx[None]
    small_tree = {k: jnp.stack(small[k]) for k in small}
    small_tree["final_norm_g"] = d_final[0]
    reduced, parts[0] = _all_reduce_small(_pack_small(small_tree).reshape(N_DEV, SMALL_ROWS, 128),
                                          _pair_sum(later, from_sibling))

    weights = dict(norm_g=norm_g, w_in=w_in, gm_ln_g=gm_ln_g, gm_ln_b=gm_ln_b, gm_ws=gm_ws, gm_bs=gm_bs,
                   pool_w=pool_w, pool_scale=pool_scale, mem_norm_g=mem_norm_g, w_mem_kv=w_mem_kv,
                   w_branch=w_branch, w_out=w_out, final_norm_g=final_norm_g)
    m_in = dict(norm_g=m_norm_g, w_in=m_w_in, gm_ln_g=m_gm_ln_g, gm_ln_b=m_gm_ln_b, gm_ws=m_gm_ws, gm_bs=m_gm_bs,
                pool_w=m_pool_w, pool_scale=m_pool_scale, mem_norm_g=m_mem_norm_g, w_mem_kv=m_w_mem_kv,
                w_branch=m_w_branch, w_out=m_w_out, final_norm_g=m_final_norm_g)
    v_in = dict(norm_g=v_norm_g, w_in=v_w_in, gm_ln_g=v_gm_ln_g, gm_ln_b=v_gm_ln_b, gm_ws=v_gm_ws, gm_bs=v_gm_bs,
                pool_w=v_pool_w, pool_scale=v_pool_scale, mem_norm_g=v_mem_norm_g, w_mem_kv=v_w_mem_kv,
                w_branch=v_w_branch, w_out=v_w_out, final_norm_g=v_final_norm_g)
    res = {}
    def view(k, arr):
        return arr.transpose(0, 2, 1) if k == "w_in" else arr

    for a, k in enumerate(BIG):
        shape = view(k, weights[k]).shape
        by_layer = [parts[l][a] for l in range(DEPTH)]
        lrc = (DEPTH,) + by_layer[0].shape[1:]
        outs = _adamw_layers(by_layer, view(k, weights[k]).reshape(lrc), view(k, m_in[k]).reshape(lrc),
                             view(k, v_in[k]).reshape(lrc), "adamw_" + k)
        res[k] = [view(k, o.reshape(shape)) for o in outs]
    outs = _adamw(reduced.reshape(1, N_DEV * SMALL_ROWS, 128), _pack_small(weights), _pack_small(m_in),
                  _pack_small(v_in), "adamw_small")
    unpacked = [_unpack_small(o, weights) for o in outs]
    for k in SMALL:
        res[k] = [u[k] for u in unpacked]

    order = ("norm_g", "w_in", "gm_ln_g", "gm_ln_b", "gm_ws", "gm_bs", "pool_w", "pool_scale", "mem_norm_g",
             "w_mem_kv", "w_branch", "w_out", "final_norm_g")
    return (loss, grad_x, *[res[k][0] for k in order], *[res[k][1] for k in order],
            *[res[k][2] for k in order], *[res[k][3] for k in order])
```

```python
import functools
import math

import numpy as np

import jax
import jax.numpy as jnp
from jax import lax
from jax.experimental import pallas as pl
from jax.experimental.pallas import tpu as pltpu

F32 = jnp.float32
BF16 = jnp.bfloat16

D_MODEL = 1024
DEPTH = 4
WIDTH = 512
D_IN = 10752
HEAD = 128
N_HEAD = 4
CHUNK = 128
MEM_LEN = 256
POOL_WINDOWS = (2, 4, 8, 16)
DILATIONS = (1, 4, 16)
EPS = 1e-6
NEG = -1e30
ATT_SCALE = HEAD ** -0.5
N_DEV = 8
N_CHIP = 4

D_BRANCHES = 6656
D_GATES = D_IN - D_BRANCHES
CB_U, CB_V, CB_AGATE, CB_PIN, CB_PGATE = 0, 1, 2, 3, 4
CB_Q0, CB_K, CB_CV, CB_CGATE, CB_MQ, CB_MGATE = 5, 8, 9, 10, 11, 12

ADAM_LR = 0.001
ADAM_B1 = 0.9
ADAM_B2 = 0.999
ADAM_EPS = 1e-08
ADAM_WD = 0.01
ADAM_STEP = 10

VMEM_LIMIT = 56 * 1024 * 1024
MESH = pl.DeviceIdType.MESH
ANY = pl.BlockSpec(memory_space=pl.ANY)

NT = (((1,), (1,)), ((), ()))
TN = (((0,), (0,)), ((), ()))


def _dot(a, b):
    return jnp.dot(a, b, preferred_element_type=F32)


def _dot_nt(a, b):
    return lax.dot_general(a, b, NT, preferred_element_type=F32)


def _dot_tn(a, b):
    return lax.dot_general(a, b, TN, preferred_element_type=F32)


def _sigmoid(x):
    return 0.5 * jnp.tanh(0.5 * x) + 0.5


def _silu(x):
    return x * _sigmoid(x)


def _silu_and_grad(x):
    s = _sigmoid(x)
    return x * s, s * (1.0 + x * (1.0 - s))


def _normal_cdf(x):
    return 0.5 * (1.0 + lax.erf(x * (2.0 ** -0.5)))


def _gelu_and_grad(x, cdf):
    return x * cdf, cdf + x * jnp.exp(-0.5 * x * x) * (1.0 / math.sqrt(2.0 * math.pi))


def _col(blk, h):
    lane = lax.broadcasted_iota(jnp.int32, blk.shape, 1)
    return jnp.sum(jnp.where(lane == h, blk, 0.0), axis=1, keepdims=True)


def _put_cols(cols):
    rows = cols[0].shape[0]
    lane = lax.broadcasted_iota(jnp.int32, (rows, 128), 1)
    out = jnp.zeros((rows, 128), F32)
    for h, cv in enumerate(cols):
        out = jnp.where(lane == h, cv, out)
    return out


def _params(sem, vmem=VMEM_LIMIT):
    return pltpu.CompilerParams(dimension_semantics=sem, vmem_limit_bytes=vmem)


def _full(shape):
    nd = len(shape)
    return pl.BlockSpec(shape, lambda *_: (0,) * nd)


def _resident(shape):
    nd = len(shape)
    return pl.BlockSpec(shape, lambda *_: (0,) * nd, pipeline_mode=pl.Buffered(1))


def _rows(tm, width, cb=0):
    return pl.BlockSpec((tm, width), lambda i: (i, cb))


GATE_TILE = 512


def _in_proj(x, g, wt, wgt, shards=()):
    S = x.shape[0]
    tm, tnb, tng = 1024, D_BRANCHES // 4, D_GATES // 4
    njb, njg = 4, 4
    n = len(shards)
    ni, nj = S // tm, njb + njg

    def body(x_ref, g_ref, wbr_ref, wg_ref, *rest):
        ins, (proj_ref, gates_ref, h_ref), outs = rest[:n], rest[n:n + 3], rest[n + 3:2 * n + 3]
        hs, sems = rest[2 * n + 3], rest[2 * n + 4:]
        i, j = pl.program_id(0), pl.program_id(1)

        if n:
            @pl.when(jnp.logical_and(i == 0, j == 0))
            def _():
                _comm_start(_ag_first(ins, outs, *sems))

        @pl.when(j == 0)
        def _():
            xf = x_ref[...]
            r = lax.rsqrt(jnp.mean(xf * xf, axis=-1, keepdims=True) + EPS)
            h = (xf * r * g_ref[...]).astype(BF16)
            hs[...] = h
            h_ref[...] = h

        @pl.when(j < njb)
        def _():
            proj_ref[...] = _dot_nt(hs[...], wbr_ref[...]).astype(BF16)

        @pl.when(j >= njb)
        def _():
            gates_ref[...] = _dot_nt(hs[...], wg_ref[...]).astype(BF16)

        if n:
            @pl.when(jnp.logical_and(i == ni - 1, j == nj - 1))
            def _():
                _comm_wait(_ag_first(ins, outs, *sems))

    def first(j):
        return jnp.minimum(j, njb - 1)

    def second(j):
        return jnp.maximum(j - njb, 0)

    res = pl.pallas_call(
        body, name="in_proj_gather" if n else "in_proj",
        grid=(ni, nj),
        in_specs=[pl.BlockSpec((tm, D_MODEL), lambda i, j: (i, 0)),
                  pl.BlockSpec((1, D_MODEL), lambda i, j: (0, 0)),
                  pl.BlockSpec((tnb, D_MODEL), lambda i, j: (first(j), 0)),
                  pl.BlockSpec((tng, D_MODEL), lambda i, j: (second(j), 0))]
                 + [ANY] * n,
        out_specs=[pl.BlockSpec((tm, tnb), lambda i, j: (i, first(j))),
                   pl.BlockSpec((tm, tng), lambda i, j: (i, second(j))),
                   pl.BlockSpec((tm, D_MODEL), lambda i, j: (i, 0))] + [ANY] * n,
        out_shape=[jax.ShapeDtypeStruct((S, D_BRANCHES), BF16), jax.ShapeDtypeStruct((S, D_GATES), BF16),
                   jax.ShapeDtypeStruct((S, D_MODEL), BF16)]
                  + [jax.ShapeDtypeStruct((N_DEV,) + s.shape, s.dtype) for s in shards],
        scratch_shapes=[pltpu.VMEM((tm, D_MODEL), BF16)] + (_dma_sems(4 * n, 4 * n, n) if n else []),
        compiler_params=_params(("arbitrary", "arbitrary")),
    )(x, g, wt, wgt, *shards)
    return res[0], res[1], res[2], list(res[3:])


def _mem_kv(mem, g, w):
    M = mem.shape[0]

    def body(m_ref, g_ref, w_ref, kv_ref, mn_ref):
        xf = m_ref[...]
        r = lax.rsqrt(jnp.mean(xf * xf, axis=-1, keepdims=True) + EPS)
        mn = (xf * r * g_ref[...]).astype(BF16)
        mn_ref[...] = mn
        kv_ref[...] = _dot(mn, w_ref[...]).astype(BF16)

    return pl.pallas_call(
        body, name="mem_kv",
        out_shape=[jax.ShapeDtypeStruct((M, 2 * WIDTH), BF16), jax.ShapeDtypeStruct((M, D_MODEL), BF16)],
        compiler_params=pltpu.CompilerParams(vmem_limit_bytes=VMEM_LIMIT),
    )(mem, g, w)


def _band_constants():
    t = np.arange(CHUNK)[:, None]
    s = np.arange(CHUNK)[None, :]
    bands = np.stack([np.stack([(t - s >= 0) & (t - s < win), s > t + CHUNK - win]) for win in POOL_WINDOWS])
    bands = bands.astype(np.float32)
    return jnp.asarray(bands, BF16), jnp.asarray(bands.transpose(0, 1, 3, 2), BF16)


def _inv_count(first_row, win):
    t = first_row + lax.broadcasted_iota(jnp.int32, (CHUNK, 1), 0)
    return 1.0 / jnp.minimum(t + 1, win).astype(F32)


def _layer_norm_fwd(v):
    mu = jnp.mean(v, axis=-1, keepdims=True)
    vc = v - mu
    var = jnp.mean(vc * vc, axis=-1, keepdims=True)
    rstd = lax.rsqrt(var + EPS)
    return vc * rstd, rstd


def _mem_softmax(q, kmem):
    s = _dot_nt(q, kmem) * ATT_SCALE
    m = jnp.max(s, axis=-1, keepdims=True)
    e = jnp.exp(s - m)
    return e * (1.0 / jnp.sum(e, axis=-1, keepdims=True))


def _abm_fwd(proj, ln_g, ln_b, wsm, bias_full, pool_w, pool_scale, kv, bands, gathered=(), shards=()):
    S = proj.shape[0]
    tm = 512
    nchunk = tm // CHUNK
    n, m = len(gathered), len(shards)
    nsteps = S // tm

    def body(u_ref, v_ref, ag_ref, p_ref, ph_ref, pg_ref, mq_ref, mg_ref, lng_ref, lnb_ref, wsm_ref, bias_ref,
             pw_ref, ps_ref, kv_ref, band_ref, *rest):
        s_in = rest[n:n + m]
        y_ref, cdf_ref = rest[n + m], rest[n + m + 1]
        bufs, s_out = rest[n + m + 2:2 * n + m + 2], rest[2 * n + m + 2:2 * (n + m) + 2]
        mix, sems = rest[2 * (n + m) + 2], rest[2 * (n + m) + 3:]
        second_sems, first_sems = (sems[:2] if n else ()), sems[2 if n else 0:]
        i = pl.program_id(0)

        if n or m:
            @pl.when(i == 0)
            def _():
                if n:
                    _comm_start(_ag_second(bufs, *second_sems))
                if m:
                    _comm_start(_ag_first(s_in, s_out, *first_sems))

        au, av = u_ref[...].astype(F32), v_ref[...].astype(F32)
        cdf_u, cdf_v = _normal_cdf(au), _normal_cdf(av)
        cdf_ref[0] = cdf_u.astype(BF16)
        cdf_ref[1] = cdf_v.astype(BF16)
        u, v = au * cdf_u, av * cdf_v
        vhat, _ = _layer_norm_fwd(v)
        vln = (vhat * lng_ref[...] + lnb_ref[...]).astype(BF16)
        for c in range(nchunk):
            for h in range(N_HEAD):
                rs, cs = slice(c * CHUNK, (c + 1) * CHUNK), slice(h * HEAD, (h + 1) * HEAD)
                mix[rs, cs] = _dot(wsm_ref[h], vln[rs, cs]) + bias_ref[:, cs]
        y_ref[0] = (u * mix[...] * _silu(ag_ref[...].astype(F32))).astype(BF16)
        halo_ok = (i > 0).astype(F32)
        for c in range(nchunk):
            rs = slice(c * CHUNK, (c + 1) * CHUNK)
            for g, win in enumerate(POOL_WINDOWS):
                cs = slice(g * HEAD, (g + 1) * HEAD)
                cur = p_ref[rs, cs]
                if c == 0:
                    prev = (ph_ref[:, cs].astype(F32) * halo_ok).astype(BF16)
                else:
                    prev = p_ref[(c - 1) * CHUNK:c * CHUNK, cs]
                sums = _dot(band_ref[g, 0], cur) + _dot(band_ref[g, 1], prev)
                dm = sums * _inv_count(i * tm + c * CHUNK, win) - cur.astype(F32)
                mix[rs, cs] = _dot(dm.astype(BF16), pw_ref[g])
        y_ref[1] = (mix[...] * ps_ref[...] * _silu(pg_ref[...].astype(F32))).astype(BF16)
        for h in range(N_HEAD):
            cs = slice(h * HEAD, (h + 1) * HEAD)
            p = _mem_softmax(mq_ref[:, cs], kv_ref[:, cs])
            mix[:, cs] = _dot(p.astype(BF16), kv_ref[:, WIDTH + h * HEAD:WIDTH + (h + 1) * HEAD])
        y_ref[2] = (mix[...] * _silu(mg_ref[...].astype(F32))).astype(BF16)

        if n or m:
            @pl.when(i == nsteps - 1)
            def _():
                if n:
                    _comm_wait(_ag_second(bufs, *second_sems))
                if m:
                    _comm_wait(_ag_first(s_in, s_out, *first_sems))

    blk = tm // CHUNK
    res = pl.pallas_call(
        body, name="abm_fwd_gather" if n or m else "abm_fwd",
        grid=(nsteps,),
        in_specs=[_rows(tm, WIDTH, CB_U), _rows(tm, WIDTH, CB_V), _rows(tm, WIDTH, CB_AGATE),
                  _rows(tm, WIDTH, CB_PIN),
                  pl.BlockSpec((CHUNK, WIDTH), lambda i: (jnp.maximum(i * blk - 1, 0), CB_PIN)),
                  _rows(tm, WIDTH, CB_PGATE), _rows(tm, WIDTH, CB_MQ), _rows(tm, WIDTH, CB_MGATE),
                  _full((1, WIDTH)), _full((1, WIDTH)), _full((N_HEAD, CHUNK, CHUNK)), _full((CHUNK, WIDTH)),
                  _full((4, HEAD, HEAD)), _full((1, WIDTH)), _full((MEM_LEN, 2 * WIDTH)),
                  _full((4, 2, CHUNK, CHUNK))] + [ANY] * (n + m),
        out_specs=[pl.BlockSpec((3, tm, WIDTH), lambda i: (0, i, 0)), pl.BlockSpec((2, tm, WIDTH), lambda i: (0, i, 0))]
                  + [ANY] * (n + m),
        out_shape=[jax.ShapeDtypeStruct((4, S, WIDTH), BF16),
                   jax.ShapeDtypeStruct((2, S, WIDTH), BF16)]
                  + [jax.ShapeDtypeStruct(b.shape, b.dtype) for b in gathered]
                  + [jax.ShapeDtypeStruct((N_DEV,) + s.shape, s.dtype) for s in shards],
        input_output_aliases={16 + a: 2 + a for a in range(n)},
        scratch_shapes=[pltpu.VMEM((tm, WIDTH), F32)] + (_dma_sems(3 * n, 3 * n) if n else [])
                       + (_dma_sems(4 * m, 4 * m, m) if m else []),
        compiler_params=_params(("arbitrary",)),
    )(proj, proj, proj, proj, proj, proj, proj, proj, ln_g, ln_b, wsm, bias_full, pool_w, pool_scale, kv, bands,
      *gathered, *shards)
    return res[0], res[1], list(res[2:2 + n]), list(res[2 + n:])


ATT_TILE = 512


def _attn_fwd(q, qcb, k, kcb, v, vcb, bps):
    S = q.shape[0]
    tm = ATT_TILE
    nb = tm // CHUNK

    nblocks = nb * N_HEAD

    def body(q_ref, k_ref, v_ref, kh_ref, vh_ref, o_ref, l_ref, sc_s, sp_s, pc_s, pp_s):
        i = pl.program_id(0)

        def prev_kv(n, cs):
            if n == 0:
                return kh_ref[:, cs], vh_ref[:, cs]
            ps = slice((n - 1) * CHUNK, n * CHUNK)
            return k_ref[ps, cs], v_ref[ps, cs]

        pens = []
        for n in range(nb):
            rs = slice(n * CHUNK, (n + 1) * CHUNK)
            pens.append(jnp.full((N_HEAD * CHUNK, 1), jnp.where((i * nb + n) % bps != 0, 0.0, NEG), F32))
            for h in range(N_HEAD):
                cs = slice(h * HEAD, (h + 1) * HEAD)
                bs = slice((n * N_HEAD + h) * CHUNK, (n * N_HEAD + h + 1) * CHUNK)
                qh = q_ref[rs, cs]
                sc_s[bs, :] = _dot_nt(qh, k_ref[rs, cs])
                sp_s[bs, :] = _dot_nt(qh, prev_kv(n, cs)[0])
        row = lax.broadcasted_iota(jnp.int32, (nblocks * CHUNK, CHUNK), 0) & (CHUNK - 1)
        col = lax.broadcasted_iota(jnp.int32, (nblocks * CHUNK, CHUNK), 1)
        sc = jnp.where(col <= row, sc_s[...] * ATT_SCALE, NEG)
        sp = jnp.where(col >= row, sp_s[...] * ATT_SCALE, NEG) + jnp.concatenate(pens, axis=0)
        m = jnp.maximum(jnp.max(sc, axis=-1, keepdims=True), jnp.max(sp, axis=-1, keepdims=True))
        ec = jnp.exp(sc - m)
        ep = jnp.exp(sp - m)
        den = jnp.sum(ec, axis=-1, keepdims=True) + jnp.sum(ep, axis=-1, keepdims=True)
        inv = 1.0 / den
        pc_s[...] = (ec * inv).astype(BF16)
        pp_s[...] = (ep * inv).astype(BF16)
        lse = m + jnp.log(den)
        for n in range(nb):
            rs = slice(n * CHUNK, (n + 1) * CHUNK)
            for h in range(N_HEAD):
                cs = slice(h * HEAD, (h + 1) * HEAD)
                bs = slice((n * N_HEAD + h) * CHUNK, (n * N_HEAD + h + 1) * CHUNK)
                o = _dot(pc_s[bs, :], v_ref[rs, cs]) + _dot(pp_s[bs, :], prev_kv(n, cs)[1])
                o_ref[rs, cs] = o.astype(BF16)
            l_ref[rs, :] = _put_cols([lse[(n * N_HEAD + h) * CHUNK:(n * N_HEAD + h + 1) * CHUNK]
                                      for h in range(N_HEAD)])

    def halo(cb):
        return pl.BlockSpec((CHUNK, WIDTH), lambda i: (jnp.maximum(i * nb - 1, 0), cb))

    return pl.pallas_call(
        body, name=f"attn_fwd_{bps}",
        grid=(S // tm,),
        in_specs=[_rows(tm, WIDTH, qcb), _rows(tm, WIDTH, kcb), _rows(tm, WIDTH, vcb), halo(kcb), halo(vcb)],
        out_specs=[_rows(tm, WIDTH), _rows(tm, 128)],
        out_shape=[jax.ShapeDtypeStruct((S, WIDTH), BF16), jax.ShapeDtypeStruct((S, 128), F32)],
        scratch_shapes=[pltpu.VMEM((nblocks * CHUNK, CHUNK), F32), pltpu.VMEM((nblocks * CHUNK, CHUNK), F32),
                        pltpu.VMEM((nblocks * CHUNK, CHUNK), BF16), pltpu.VMEM((nblocks * CHUNK, CHUNK), BF16)],
        compiler_params=_params(("parallel",)),
    )(q, k, v, k, v)


def _gate_specs(tm):
    return [pl.BlockSpec((tm, D_MODEL), lambda i, b=b: (i, b)) for b in range(4)]


Y_SLOT = (0, 1, 3, 2)


def _merge_fwd(x, y4, o_g, l_g, proj, gates, wb, wout, shards=(), half=()):
    S = x.shape[0]
    tm = 512
    n, nh = len(shards), len(half)
    nsteps = S // tm
    forward_at = nsteps // 2

    def body(x_ref, y_ref, o0, o1, o2, l0, l1, l2, cg_ref, *rest):
        gm = rest[:4]
        wb_ref, wo_ref = rest[4:6]
        s_in = rest[6:6 + n]
        rest = rest[6 + n + nh:]
        xn_ref, yc_ref, oc_ref, lse_ref, z_ref = rest[:5]
        s_out, bufs, ocs, sems = rest[5:5 + n], rest[5:5 + n + nh], rest[5 + n + nh], rest[6 + n + nh:]
        i = pl.program_id(0)

        if n:
            @pl.when(i == 0)
            def _():
                _comm_start(_ag_first(s_in, s_out, *sems[:3]))

            @pl.when(i == forward_at)
            def _():
                incoming = _ag_first(s_in, s_out, *sems[:3])[2]
                for a in range(n):
                    for k in range(1, 4):
                        incoming[4 * a + k].wait_recv()
                _comm_start(_ag_second(bufs, *sems[3:]))

        lcols = []
        for h in range(N_HEAD):
            cs = slice(h * HEAD, (h + 1) * HEAD)
            ls = [_col(l[...], h) for l in (l0, l1, l2)]
            m = jnp.maximum(jnp.maximum(ls[0], ls[1]), ls[2])
            tot = jnp.exp(ls[0] - m) + jnp.exp(ls[1] - m) + jnp.exp(ls[2] - m)
            lse = m + jnp.log(tot)
            ocs[:, cs] = sum(jnp.exp(lg - lse) * o[:, cs].astype(F32) for lg, o in zip(ls, (o0, o1, o2)))
            lcols.append(lse)
        lse_ref[...] = _put_cols(lcols)
        oc = ocs[...]
        oc_ref[...] = oc.astype(BF16)
        yc = (oc * _silu(cg_ref[...].astype(F32))).astype(BF16)
        yc_ref[...] = yc
        ys = (y_ref[0], y_ref[1], yc, y_ref[2])
        z = jnp.zeros((tm, D_MODEL), F32)
        for b in range(4):
            z = z + _sigmoid(gm[b][...].astype(F32)) * _dot(ys[b], wb_ref[b])
        zb = z.astype(BF16)
        z_ref[...] = zb
        xn_ref[...] = x_ref[...] + _dot(zb, wo_ref[...])

        if n:
            @pl.when(i == nsteps - 1)
            def _():
                local, out, incoming = _ag_first(s_in, s_out, *sems[:3])
                for a in range(n):
                    incoming[4 * a].wait_recv()
                _comm_wait(_ag_second(bufs, *sems[3:]))
                for cp in out:
                    cp.wait_send()
                for cp in local:
                    cp.wait()

    res = pl.pallas_call(
        body, name="merge_fwd_gather" if n else "merge_fwd",
        grid=(nsteps,),
        in_specs=[_rows(tm, D_MODEL), pl.BlockSpec((3, tm, WIDTH), lambda i: (0, i, 0)),
                  _rows(tm, WIDTH), _rows(tm, WIDTH), _rows(tm, WIDTH),
                  _rows(tm, 128), _rows(tm, 128), _rows(tm, 128),
                  _rows(tm, WIDTH, CB_CGATE)] + _gate_specs(tm)
                 + [_resident((4, WIDTH, D_MODEL)), _resident((D_MODEL, D_MODEL))] + [ANY] * (n + nh),
        out_specs=[_rows(tm, D_MODEL), pl.BlockSpec((None, tm, WIDTH), lambda i: (Y_SLOT[2], i, 0)),
                   _rows(tm, WIDTH), _rows(tm, 128), _rows(tm, D_MODEL)] + [ANY] * (n + nh),
        out_shape=[jax.ShapeDtypeStruct((S, D_MODEL), F32), jax.ShapeDtypeStruct(y4.shape, BF16),
                   jax.ShapeDtypeStruct((S, WIDTH), BF16), jax.ShapeDtypeStruct((S, 128), F32),
                   jax.ShapeDtypeStruct((S, D_MODEL), BF16)]
                  + [jax.ShapeDtypeStruct((N_DEV,) + s.shape, s.dtype) for s in shards]
                  + [jax.ShapeDtypeStruct(b.shape, b.dtype) for b in half],
        input_output_aliases={1: 1, **{15 + n + a: 5 + n + a for a in range(nh)}},
        scratch_shapes=[pltpu.VMEM((tm, WIDTH), F32)]
                       + (_dma_sems(4 * n, 4 * n, n, 3 * (n + nh), 3 * (n + nh)) if n else []),
        compiler_params=_params(("arbitrary",)),
    )(x, y4, *o_g, *l_g, proj, *([gates] * 4), wb, wout, *shards, *half)
    return res[:5], list(res[5:])


def _loss_head(x, g, tgt):
    S = x.shape[0]
    tm = 512

    def body(x_ref, g_ref, t_ref, loss_ref, dx_ref, dg_ref):
        @pl.when(pl.program_id(0) == 0)
        def _():
            loss_ref[...] = jnp.zeros_like(loss_ref)
            dg_ref[...] = jnp.zeros_like(dg_ref)

        xf = x_ref[...]
        r = lax.rsqrt(jnp.mean(xf * xf, axis=-1, keepdims=True) + EPS)
        xhat = xf * r
        gv = g_ref[...]
        err = xhat * gv - t_ref[...]
        e2 = jnp.sum(err * err, axis=-1, keepdims=True)
        loss_ref[...] += (0.5 / D_MODEL) * jnp.sum(e2, axis=0, keepdims=True)
        dy = err * (1.0 / D_MODEL)
        dg_ref[...] += jnp.sum(dy * xhat, axis=0, keepdims=True)
        dxh = dy * gv
        dx_ref[...] = r * (dxh - xhat * jnp.mean(dxh * xhat, axis=-1, keepdims=True))

    return pl.pallas_call(
        body, name="loss_head",
        grid=(S // tm,),
        in_specs=[_rows(tm, D_MODEL), _full((1, D_MODEL)), _rows(tm, D_MODEL)],
        out_specs=[_full((1, 128)), _rows(tm, D_MODEL), _full((1, D_MODEL))],
        out_shape=[jax.ShapeDtypeStruct((1, 128), F32), jax.ShapeDtypeStruct((S, D_MODEL), F32),
                   jax.ShapeDtypeStruct((1, D_MODEL), F32)],
        compiler_params=_params(("arbitrary",)),
    )(x, g, tgt)


def _merge_bwd(dxo, y4, oc, z, proj, gates, wb, wout, grads=()):
    S = dxo.shape[0]
    tm = 256
    n = len(grads)
    nsteps = S // tm

    def body(dx_ref, y_ref, oc_ref, z_ref, cg_ref, *rest):
        gm = rest[:4]
        wb_ref, wo_ref = rest[4:6]
        g_in = rest[6:6 + n]
        dy_ref, doc_ref, delta_ref, dcg_ref, dgm_ref, dwb_ref, dwo_ref = rest[6 + n:13 + n]
        g_out = rest[13 + n:13 + 2 * n]
        acc_b, acc_o = rest[13 + 2 * n:15 + 2 * n]
        sems = rest[15 + 2 * n:]
        i = pl.program_id(0)

        @pl.when(i == 0)
        def _():
            acc_b[...] = jnp.zeros_like(acc_b)
            acc_o[...] = jnp.zeros_like(acc_o)
            if n:
                _comm_start(_rs_first(g_in, g_out, *sems))

        dxb = dx_ref[...].astype(BF16)
        acc_o[...] += _dot_tn(z_ref[...], dxb)
        dz = _dot_nt(dxb, wo_ref[...])
        for b in range(4):
            gate = _sigmoid(gm[b][...].astype(F32))
            yb = y_ref[Y_SLOT[b]]
            t = _dot(yb, wb_ref[b])
            dgm_ref[:, b * D_MODEL:(b + 1) * D_MODEL] = (dz * t * gate * (1.0 - gate)).astype(BF16)
            dt = (dz * gate).astype(BF16)
            acc_b[b] += _dot_tn(yb, dt)
            dyb = _dot_nt(dt, wb_ref[b])
            if b == 2:
                cg = cg_ref[...].astype(F32)
                oc = oc_ref[...].astype(F32)
                scg, dscg = _silu_and_grad(cg)
                doc = dyb * scg
                dcg_ref[...] = (dyb * oc * dscg).astype(BF16)
                doc_ref[...] = doc.astype(BF16)
                prod = doc * oc
                delta_ref[...] = _put_cols([jnp.sum(prod[:, h * HEAD:(h + 1) * HEAD], axis=1, keepdims=True)
                                            for h in range(N_HEAD)])
            else:
                dy_ref[b if b < 2 else 2] = dyb.astype(BF16)

        @pl.when(i == nsteps - 1)
        def _():
            dwb_ref[...] = acc_b[...].astype(BF16)
            dwo_ref[...] = acc_o[...].astype(BF16)
            if n:
                _comm_wait(_rs_first(g_in, g_out, *sems))

    res = pl.pallas_call(
        body, name="merge_bwd_scatter" if n else "merge_bwd",
        grid=(nsteps,),
        in_specs=[_rows(tm, D_MODEL), pl.BlockSpec((4, tm, WIDTH), lambda i: (0, i, 0)),
                  _rows(tm, WIDTH), _rows(tm, D_MODEL), _rows(tm, WIDTH, CB_CGATE)] + _gate_specs(tm)
                 + [_resident((4, WIDTH, D_MODEL)), _resident((D_MODEL, D_MODEL))] + [ANY] * n,
        out_specs=[pl.BlockSpec((3, tm, WIDTH), lambda i: (0, i, 0)), _rows(tm, WIDTH), _rows(tm, 128),
                   _rows(tm, WIDTH), _rows(tm, 4 * D_MODEL), _full((4, WIDTH, D_MODEL)), _full((D_MODEL, D_MODEL))]
                  + [ANY] * n,
        out_shape=[jax.ShapeDtypeStruct((3, S, WIDTH), BF16), jax.ShapeDtypeStruct((S, WIDTH), BF16),
                   jax.ShapeDtypeStruct((S, 128), F32), jax.ShapeDtypeStruct((S, WIDTH), BF16),
                   jax.ShapeDtypeStruct((S, 4 * D_MODEL), BF16), jax.ShapeDtypeStruct((4, WIDTH, D_MODEL), BF16),
                   jax.ShapeDtypeStruct((D_MODEL, D_MODEL), BF16)]
                  + [jax.ShapeDtypeStruct(g.shape[:1] + g.shape[2:], g.dtype) for g in grads],
        scratch_shapes=[pltpu.VMEM((4, WIDTH, D_MODEL), F32), pltpu.VMEM((D_MODEL, D_MODEL), F32)]
                       + (_dma_sems(N_CHIP * n, N_CHIP * n) if n else []),
        compiler_params=_params(("arbitrary",)),
    )(dxo, y4, oc, z, proj, *([gates] * 4), wb, wout, *grads)
    return res[:7], list(res[7:])


def _attn_bwd(q, qcb, k, kcb, v, vcb, do, lse, delta, bps, dst, dcb):
    S = q.shape[0]
    tm = ATT_TILE
    nb = tm // CHUNK
    nblk = S // CHUNK

    ncur = nb * N_HEAD
    nprev = (nb + 1) * N_HEAD

    def body(q_ref, k_ref, v_ref, do_ref, l_ref, d_ref, kh_ref, vh_ref, qn_ref, don_ref, ln_ref, dn_ref, dst_ref,
             dq_ref, dk_ref, dv_ref, sc_s, sp_s, dpc_s, dpp_s, pc_s, pp_s, dsc_s, dsp_s):
        i = pl.program_id(0)

        def rows_of(n):
            if n < nb:
                rs = slice(n * CHUNK, (n + 1) * CHUNK)
                return rs, q_ref, do_ref, l_ref, d_ref
            return slice(0, CHUNK), qn_ref, don_ref, ln_ref, dn_ref

        def prev_kv(n, cs):
            if n == 0:
                return kh_ref[:, cs], vh_ref[:, cs]
            ps = slice((n - 1) * CHUNK, n * CHUNK)
            return k_ref[ps, cs], v_ref[ps, cs]

        def blk(n, h):
            return slice((n * N_HEAD + h) * CHUNK, (n * N_HEAD + h + 1) * CHUNK)

        pens, lses, deltas = [], [], []
        for n in range(nb + 1):
            rs, qr, dor, lr, dr = rows_of(n)
            gb = i * nb + n
            pen = jnp.where(gb % bps != 0, 0.0, NEG)
            if n == nb:
                pen = pen + jnp.where(gb < nblk, 0.0, NEG)
            pens.append(jnp.full((N_HEAD * CHUNK, 1), pen, F32))
            lblk, dblk = lr[rs, :], dr[rs, :]
            for h in range(N_HEAD):
                cs = slice(h * HEAD, (h + 1) * HEAD)
                qh, doh = qr[rs, cs], dor[rs, cs]
                lses.append(_col(lblk, h))
                deltas.append(_col(dblk, h))
                kp, vp = prev_kv(n, cs)
                sp_s[blk(n, h), :] = _dot_nt(qh, kp)
                dpp_s[blk(n, h), :] = _dot_nt(doh, vp)
                if n < nb:
                    sc_s[blk(n, h), :] = _dot_nt(qh, k_ref[rs, cs])
                    dpc_s[blk(n, h), :] = _dot_nt(doh, v_ref[rs, cs])
        lse = jnp.concatenate(lses, axis=0)
        delta = jnp.concatenate(deltas, axis=0)
        row = lax.broadcasted_iota(jnp.int32, (nprev * CHUNK, CHUNK), 0) & (CHUNK - 1)
        col = lax.broadcasted_iota(jnp.int32, (nprev * CHUNK, CHUNK), 1)
        sp = jnp.where(col >= row, sp_s[...] * ATT_SCALE, NEG) + jnp.concatenate(pens, axis=0)
        pp = jnp.exp(sp - lse)
        pp_s[...] = pp.astype(BF16)
        dsp_s[...] = (pp * (dpp_s[...] - delta)).astype(BF16)
        nc = ncur * CHUNK
        sc = jnp.where(col[:nc] <= row[:nc], sc_s[...] * ATT_SCALE, NEG)
        pc = jnp.exp(sc - lse[:nc])
        pc_s[...] = pc.astype(BF16)
        dsc_s[...] = (pc * (dpc_s[...] - delta[:nc])).astype(BF16)
        for n in range(nb):
            rs, qr, dor, _, _ = rows_of(n)
            rn, qnr, donr, _, _ = rows_of(n + 1)
            for h in range(N_HEAD):
                cs = slice(h * HEAD, (h + 1) * HEAD)
                kp, _ = prev_kv(n, cs)
                dq = _dot(dsc_s[blk(n, h), :], k_ref[rs, cs]) + _dot(dsp_s[blk(n, h), :], kp)
                dq_ref[rs, cs] = (dq * ATT_SCALE).astype(BF16)
                dk = _dot_tn(dsc_s[blk(n, h), :], qr[rs, cs]) + _dot_tn(dsp_s[blk(n + 1, h), :], qnr[rn, cs])
                dk_ref[rs, cs] = (dk * ATT_SCALE).astype(BF16)
                dv = _dot_tn(pc_s[blk(n, h), :], dor[rs, cs]) + _dot_tn(pp_s[blk(n + 1, h), :], donr[rn, cs])
                dv_ref[rs, cs] = dv.astype(BF16)

    def prev_halo(cb):
        return pl.BlockSpec((CHUNK, WIDTH), lambda i: (jnp.maximum(i * nb - 1, 0), cb))

    def next_halo(width, cb=0):
        return pl.BlockSpec((CHUNK, width), lambda i: (jnp.minimum(i * nb + nb, nblk - 1), cb))

    return pl.pallas_call(
        body, name=f"attn_bwd_{bps}",
        grid=(S // tm,),
        in_specs=[_rows(tm, WIDTH, qcb), _rows(tm, WIDTH, kcb), _rows(tm, WIDTH, vcb), _rows(tm, WIDTH),
                  _rows(tm, 128), _rows(tm, 128), prev_halo(kcb), prev_halo(vcb),
                  next_halo(WIDTH, qcb), next_halo(WIDTH), next_halo(128), next_halo(128), ANY],
        out_specs=[_rows(tm, WIDTH, dcb), _rows(tm, WIDTH), _rows(tm, WIDTH)],
        out_shape=[jax.ShapeDtypeStruct(dst.shape, BF16)] + [jax.ShapeDtypeStruct((S, WIDTH), BF16)] * 2,
        input_output_aliases={12: 0},
        scratch_shapes=[pltpu.VMEM((ncur * CHUNK, CHUNK), F32), pltpu.VMEM((nprev * CHUNK, CHUNK), F32),
                        pltpu.VMEM((ncur * CHUNK, CHUNK), F32), pltpu.VMEM((nprev * CHUNK, CHUNK), F32),
                        pltpu.VMEM((ncur * CHUNK, CHUNK), BF16), pltpu.VMEM((nprev * CHUNK, CHUNK), BF16),
                        pltpu.VMEM((ncur * CHUNK, CHUNK), BF16), pltpu.VMEM((nprev * CHUNK, CHUNK), BF16)],
        compiler_params=_params(("parallel",)),
    )(q, k, v, do, lse, delta, k, v, q, do, lse, delta, dst)


def _dilated_split(d):
    hp = min(N_HEAD, 16 // d)
    return hp, N_HEAD // hp, HEAD * hp


def _strided_regroup(d):
    return d < 16


def _by_class(src_ref, dst, d, hp, nat):
    for j in range(hp):
        if _strided_regroup(d):
            nat[j] = src_ref[:, j * HEAD:(j + 1) * HEAD].astype(F32)
            for r in range(d):
                dst[j, r * CHUNK:(r + 1) * CHUNK, :] = nat.at[j][pl.ds(r, CHUNK, stride=d), :].astype(BF16)
        else:
            dst[j] = pltpu.einshape("(tr)l->(rt)l", src_ref[:, j * HEAD:(j + 1) * HEAD], r=d)


def _from_class(src, dst_ref, d, hp, nat, add_ref=None):
    for j in range(hp):
        cs = slice(j * HEAD, (j + 1) * HEAD)
        if _strided_regroup(d):
            for r in range(d):
                nat.at[j][pl.ds(r, CHUNK, stride=d), :] = src[j, r * CHUNK:(r + 1) * CHUNK, :]
            val = nat[j].astype(BF16)
        else:
            val = pltpu.einshape("(rt)l->(tr)l", src[j].astype(BF16), r=d)
        if add_ref is not None:
            val = (val.astype(F32) + add_ref[:, cs].astype(F32)).astype(BF16)
        dst_ref[:, cs] = val


def _attn_fwd_dilated(proj, qcb, kcb, vcb, d):
    S = proj.shape[0]
    T = CHUNK * d
    hp, nh, cw = _dilated_split(d)
    nblocks = d * hp

    def body(q_ref, k_ref, v_ref, o_ref, l_ref, qf, kst, vst, of, lf, nat, sc_s, sp_s, pc_s, pp_s):
        i, hh = pl.program_id(0), pl.program_id(1)
        kf, vf = kst.at[i % 2, hh], vst.at[i % 2, hh]
        kpf, vpf = kst.at[1 - i % 2, hh], vst.at[1 - i % 2, hh]

        @pl.when(i == 0)
        def _():
            kpf[...] = jnp.zeros_like(kpf)
            vpf[...] = jnp.zeros_like(vpf)

        _by_class(q_ref, qf, d, hp, nat)
        _by_class(k_ref, kf, d, hp, nat)
        _by_class(v_ref, vf, d, hp, nat)

        def blk(ref, r, j):
            return ref[j, r * CHUNK:(r + 1) * CHUNK, :]

        def bs(r, j):
            return slice((r * hp + j) * CHUNK, (r * hp + j + 1) * CHUNK)

        for r in range(d):
            for j in range(hp):
                qb = blk(qf, r, j)
                sc_s[bs(r, j), :] = _dot_nt(qb, blk(kf, r, j))
                sp_s[bs(r, j), :] = _dot_nt(qb, blk(kpf, r, j))
        row = lax.broadcasted_iota(jnp.int32, (nblocks * CHUNK, CHUNK), 0) & (CHUNK - 1)
        col = lax.broadcasted_iota(jnp.int32, (nblocks * CHUNK, CHUNK), 1)
        sc = jnp.where(col <= row, sc_s[...] * ATT_SCALE, NEG)
        sp = jnp.where(col >= row, sp_s[...] * ATT_SCALE, NEG) + jnp.where(i > 0, 0.0, NEG)
        m = jnp.maximum(jnp.max(sc, axis=-1, keepdims=True), jnp.max(sp, axis=-1, keepdims=True))
        ec = jnp.exp(sc - m)
        ep = jnp.exp(sp - m)
        den = jnp.sum(ec, axis=-1, keepdims=True) + jnp.sum(ep, axis=-1, keepdims=True)
        inv = 1.0 / den
        pc_s[...] = (ec * inv).astype(BF16)
        pp_s[...] = (ep * inv).astype(BF16)
        lse = m + jnp.log(den)
        lane = lax.broadcasted_iota(jnp.int32, (CHUNK, 128), 1)
        for r in range(d):
            lblk = jnp.zeros((CHUNK, 128), F32)
            for j in range(hp):
                o = _dot(pc_s[bs(r, j), :], blk(vf, r, j)) + _dot(pp_s[bs(r, j), :], blk(vpf, r, j))
                of[j, r * CHUNK:(r + 1) * CHUNK, :] = o
                lblk = jnp.where(lane == hh * hp + j, lse[bs(r, j)], lblk)
            lf[r * CHUNK:(r + 1) * CHUNK, :] = lblk
        _from_class(of, o_ref, d, hp, nat)
        lnat = pltpu.einshape("(rt)l->(tr)l", lf[...], r=d)

        @pl.when(hh == 0)
        def _():
            l_ref[...] = lnat

        @pl.when(hh > 0)
        def _():
            l_ref[...] += lnat

    def cols(cb):
        return pl.BlockSpec((T, cw), lambda i, hh: (i, cb * nh + hh))

    tile = pltpu.VMEM((hp, T, HEAD), BF16)
    staging = pltpu.VMEM((hp, T, HEAD) if _strided_regroup(d) else (1, 8, HEAD), F32)
    return pl.pallas_call(
        body, name=f"attn_fwd_dilated_{d}",
        grid=(S // T, nh),
        in_specs=[cols(qcb), cols(kcb), cols(vcb)],
        out_specs=[cols(0), pl.BlockSpec((T, 128), lambda i, hh: (i, 0))],
        out_shape=[jax.ShapeDtypeStruct((S, WIDTH), BF16), jax.ShapeDtypeStruct((S, 128), F32)],
        scratch_shapes=[tile, pltpu.VMEM((2, nh, hp, T, HEAD), BF16), pltpu.VMEM((2, nh, hp, T, HEAD), BF16),
                        pltpu.VMEM((hp, T, HEAD), F32), pltpu.VMEM((T, 128), F32), staging,
                        pltpu.VMEM((nblocks * CHUNK, CHUNK), F32), pltpu.VMEM((nblocks * CHUNK, CHUNK), F32),
                        pltpu.VMEM((nblocks * CHUNK, CHUNK), BF16), pltpu.VMEM((nblocks * CHUNK, CHUNK), BF16)],
        compiler_params=_params(("arbitrary", "arbitrary")),
    )(proj, proj, proj)


def _attn_bwd_dilated(proj, qcb, kcb, vcb, do, lse, delta, d, dk_in, dv_in, dst, dcb):
    S = proj.shape[0]
    T = CHUNK * d
    nt = S // T
    hp, nh, cw = _dilated_split(d)
    nblocks = d * hp

    def body(q_ref, k_ref, v_ref, do_ref, l_ref, d_ref, dki_ref, dvi_ref, dst_ref, dq_ref, dk_ref, dv_ref,
             qf, dof, kbuf, vbuf, dqf, gk, gv, nat,
             sc_s, sp_s, dpc_s, dpp_s, pc_s, pp_s, dsc_s, dsp_s):
        hh, i = pl.program_id(0), pl.program_id(1)
        kf, vf, newk, newv = kbuf.at[i % 2], vbuf.at[i % 2], gk.at[i % 2], gv.at[i % 2]
        kpf, vpf, acck, accv = kbuf.at[1 - i % 2], vbuf.at[1 - i % 2], gk.at[1 - i % 2], gv.at[1 - i % 2]

        @pl.when(i == 0)
        def _():
            for ref in (kbuf, vbuf, gk, gv):
                ref[...] = jnp.zeros_like(ref)
            dk_ref[...] = jnp.zeros_like(dk_ref)
            dv_ref[...] = jnp.zeros_like(dv_ref)

        def blk(ref, r, j):
            return ref[j, r * CHUNK:(r + 1) * CHUNK, :]

        def bs(r, j):
            return slice((r * hp + j) * CHUNK, (r * hp + j + 1) * CHUNK)

        @pl.when(i < nt)
        def _():
            _by_class(q_ref, qf, d, hp, nat)
            _by_class(do_ref, dof, d, hp, nat)
            _by_class(k_ref, kf, d, hp, nat)
            _by_class(v_ref, vf, d, hp, nat)
            lses, deltas = [], []
            lcls = pltpu.einshape("(tr)l->(rt)l", l_ref[...], r=d)
            dcls = pltpu.einshape("(tr)l->(rt)l", d_ref[...], r=d)
            for r in range(d):
                lblk = lcls[r * CHUNK:(r + 1) * CHUNK]
                dblk = dcls[r * CHUNK:(r + 1) * CHUNK]
                for j in range(hp):
                    lses.append(_col(lblk, hh * hp + j))
                    deltas.append(_col(dblk, hh * hp + j))
                    qb, dob = blk(qf, r, j), blk(dof, r, j)
                    sc_s[bs(r, j), :] = _dot_nt(qb, blk(kf, r, j))
                    dpc_s[bs(r, j), :] = _dot_nt(dob, blk(vf, r, j))
                    sp_s[bs(r, j), :] = _dot_nt(qb, blk(kpf, r, j))
                    dpp_s[bs(r, j), :] = _dot_nt(dob, blk(vpf, r, j))
            lse = jnp.concatenate(lses, axis=0)
            delta = jnp.concatenate(deltas, axis=0)
            row = lax.broadcasted_iota(jnp.int32, (nblocks * CHUNK, CHUNK), 0) & (CHUNK - 1)
            col = lax.broadcasted_iota(jnp.int32, (nblocks * CHUNK, CHUNK), 1)
            sp = jnp.where(col >= row, sp_s[...] * ATT_SCALE, NEG) + jnp.where(i > 0, 0.0, NEG)
            pp = jnp.exp(sp - lse)
            pp_s[...] = pp.astype(BF16)
            dsp_s[...] = (pp * (dpp_s[...] - delta)).astype(BF16)
            sc = jnp.where(col <= row, sc_s[...] * ATT_SCALE, NEG)
            pc = jnp.exp(sc - lse)
            pc_s[...] = pc.astype(BF16)
            dsc_s[...] = (pc * (dpc_s[...] - delta)).astype(BF16)
            for r in range(d):
                rows = slice(r * CHUNK, (r + 1) * CHUNK)
                for j in range(hp):
                    qb, dob = blk(qf, r, j), blk(dof, r, j)
                    dsc, dsp = dsc_s[bs(r, j), :], dsp_s[bs(r, j), :]
                    dqf[j, rows, :] = (_dot(dsc, blk(kf, r, j)) + _dot(dsp, blk(kpf, r, j))) * ATT_SCALE
                    newk[j, rows, :] = _dot_tn(dsc, qb) * ATT_SCALE
                    newv[j, rows, :] = _dot_tn(pc_s[bs(r, j), :], dob)
                    acck[j, rows, :] += _dot_tn(dsp, qb) * ATT_SCALE
                    accv[j, rows, :] += _dot_tn(pp_s[bs(r, j), :], dob)
            _from_class(dqf, dq_ref, d, hp, nat)

        @pl.when(i > 0)
        def _():
            _from_class(acck, dk_ref, d, hp, nat, dki_ref)
            _from_class(accv, dv_ref, d, hp, nat, dvi_ref)

    def cur(width, cb, nsplit):
        return pl.BlockSpec((T, width), lambda hh, i: (jnp.minimum(i, nt - 1), cb * nsplit + hh * (nsplit > 1)))

    def lag():
        return pl.BlockSpec((T, cw), lambda hh, i: (jnp.maximum(i - 1, 0), hh))

    tile = pltpu.VMEM((hp, T, HEAD), BF16)
    acc = pltpu.VMEM((hp, T, HEAD), F32)
    f32s = pltpu.VMEM((nblocks * CHUNK, CHUNK), F32)
    b16s = pltpu.VMEM((nblocks * CHUNK, CHUNK), BF16)
    return pl.pallas_call(
        body, name=f"attn_bwd_dilated_{d}",
        grid=(nh, nt + 1),
        in_specs=[cur(cw, qcb, nh), cur(cw, kcb, nh), cur(cw, vcb, nh), cur(cw, 0, nh), cur(128, 0, 1), cur(128, 0, 1),
                  lag(), lag(), ANY],
        out_specs=[cur(cw, dcb, nh), lag(), lag()],
        out_shape=[jax.ShapeDtypeStruct(dst.shape, BF16)] + [jax.ShapeDtypeStruct((S, WIDTH), BF16)] * 2,
        input_output_aliases={8: 0},
        scratch_shapes=[tile, tile, pltpu.VMEM((2, hp, T, HEAD), BF16), pltpu.VMEM((2, hp, T, HEAD), BF16), acc,
                        pltpu.VMEM((2, hp, T, HEAD), F32), pltpu.VMEM((2, hp, T, HEAD), F32),
                        acc if _strided_regroup(d) else pltpu.VMEM((1, 8, HEAD), F32)]
                       + [f32s] * 4 + [b16s] * 4,
        compiler_params=_params(("arbitrary", "arbitrary")),
    )(proj, proj, proj, do, lse, delta, dk_in, dv_in, dst)


def _abm_bwd(proj, cdf, dy3, ln_g, ln_b, wsm, wsm_t, bias_full, pool_w, pool_wt, pool_scale, kv, bands, bands_t):
    S = proj.shape[0]
    tm = 512
    nchunk = tm // CHUNK
    nblk = S // CHUNK

    def body(u_ref, v_ref, ag_ref, p_ref, ph_ref, pg_ref, pgn_ref, mq_ref, mg_ref, cdf_ref, dy_ref, dypn_ref,
             lng_ref, lnb_ref, wsm_ref, wsmt_ref, bias_ref, pw_ref, pwt_ref, ps_ref, kv_ref, band_ref, bandt_ref,
             dab_ref, dm_ref, dlng_ref, dlnb_ref, dws_ref, dbias_ref, dpw_ref, dps_ref, dkv_ref,
             mix, dvl, ddn):
        i = pl.program_id(0)

        @pl.when(i == 0)
        def _():
            for r in (dlng_ref, dlnb_ref, dws_ref, dbias_ref, dpw_ref, dps_ref, dkv_ref):
                r[...] = jnp.zeros_like(r)

        au = u_ref[...].astype(F32)
        av = v_ref[...].astype(F32)
        ag = ag_ref[...].astype(F32)
        u, du = _gelu_and_grad(au, cdf_ref[0].astype(F32))
        v, dgelu_v = _gelu_and_grad(av, cdf_ref[1].astype(F32))
        vhat, rstd = _layer_norm_fwd(v)
        vln = (vhat * lng_ref[...] + lnb_ref[...]).astype(BF16)
        for c in range(nchunk):
            for h in range(N_HEAD):
                rs, cs = slice(c * CHUNK, (c + 1) * CHUNK), slice(h * HEAD, (h + 1) * HEAD)
                mix[rs, cs] = _dot(wsm_ref[h], vln[rs, cs]) + bias_ref[:, cs]
        dya = dy_ref[0].astype(F32)
        sg, dsg = _silu_and_grad(ag)
        mixed = mix[...]
        dab_ref[:, 2 * WIDTH:3 * WIDTH] = (dya * u * mixed * dsg).astype(BF16)
        dab_ref[:, 0:WIDTH] = (dya * mixed * sg * du).astype(BF16)
        dmixed = dya * u * sg
        dmb = dmixed.astype(BF16)
        tril = (lax.broadcasted_iota(jnp.int32, (CHUNK, CHUNK), 1)
                <= lax.broadcasted_iota(jnp.int32, (CHUNK, CHUNK), 0))
        for c in range(nchunk):
            rs = slice(c * CHUNK, (c + 1) * CHUNK)
            dbias_ref[...] += dmixed[rs, :]
            for h in range(N_HEAD):
                cs = slice(h * HEAD, (h + 1) * HEAD)
                dvl[rs, cs] = _dot(wsmt_ref[h], dmb[rs, cs])
                dws_ref[h] += jnp.where(tril, _dot_nt(dmb[rs, cs], vln[rs, cs]), 0.0)
        dvln = dvl[...]
        dlng_ref[...] += jnp.sum(dvln * vhat, axis=0, keepdims=True)
        dlnb_ref[...] += jnp.sum(dvln, axis=0, keepdims=True)
        dvh = dvln * lng_ref[...]
        dv = rstd * (dvh - jnp.mean(dvh, axis=-1, keepdims=True)
                     - vhat * jnp.mean(dvh * vhat, axis=-1, keepdims=True))
        dab_ref[:, WIDTH:2 * WIDTH] = (dv * dgelu_v).astype(BF16)

        halo_ok = (i > 0).astype(F32)
        for c in range(nchunk):
            rs = slice(c * CHUNK, (c + 1) * CHUNK)
            for g, win in enumerate(POOL_WINDOWS):
                cs = slice(g * HEAD, (g + 1) * HEAD)
                cur = p_ref[rs, cs]
                if c == 0:
                    prev = (ph_ref[:, cs].astype(F32) * halo_ok).astype(BF16)
                else:
                    prev = p_ref[(c - 1) * CHUNK:c * CHUNK, cs]
                sums = _dot(band_ref[g, 0], cur) + _dot(band_ref[g, 1], prev)
                dvl[rs, cs] = sums * _inv_count(i * tm + c * CHUNK, win) - cur.astype(F32)
        dmat = dvl[...].astype(BF16)
        for g in range(4):
            cs = slice(g * HEAD, (g + 1) * HEAD)
            mix[:, cs] = _dot(dmat[:, cs], pw_ref[g])
        yg = mix[...]
        pg = pg_ref[...].astype(F32)
        dyp = dy_ref[1].astype(F32)
        spg, dspg = _silu_and_grad(pg)
        dyy = dyp * spg
        scale = ps_ref[...]
        dab_ref[:, 4 * WIDTH:5 * WIDTH] = (dyp * yg * scale * dspg).astype(BF16)
        dps_ref[...] += jnp.sum(dyy * yg, axis=0, keepdims=True)
        dyg = (dyy * scale).astype(BF16)
        for g in range(4):
            cs = slice(g * HEAD, (g + 1) * HEAD)
            dpw_ref[g] += _dot_tn(dmat[:, cs], dyg[:, cs])
            mix[:, cs] = _dot(dyg[:, cs], pwt_ref[g])
        next_ok = (i + 1 < S // tm).astype(F32)
        dygn = (dypn_ref[...].astype(F32) * _silu(pgn_ref[...].astype(F32)) * scale * next_ok).astype(BF16)
        for c in range(nchunk + 1):
            for g, win in enumerate(POOL_WINDOWS):
                cs = slice(g * HEAD, (g + 1) * HEAD)
                if c < nchunk:
                    dd = mix[c * CHUNK:(c + 1) * CHUNK, cs]
                else:
                    dd = _dot(dygn[:, cs], pwt_ref[g])
                ddn[c * CHUNK:(c + 1) * CHUNK, cs] = dd * _inv_count(i * tm + c * CHUNK, win)
        ddnb = ddn[...].astype(BF16)
        for c in range(nchunk):
            rs = slice(c * CHUNK, (c + 1) * CHUNK)
            ns = slice((c + 1) * CHUNK, (c + 2) * CHUNK)
            for g, win in enumerate(POOL_WINDOWS):
                cs = slice(g * HEAD, (g + 1) * HEAD)
                dp = _dot(bandt_ref[g, 0], ddnb[rs, cs]) + _dot(bandt_ref[g, 1], ddnb[ns, cs]) - mix[rs, cs]
                dab_ref[rs, 3 * WIDTH + g * HEAD:3 * WIDTH + (g + 1) * HEAD] = dp.astype(BF16)

        mg = mg_ref[...].astype(F32)
        dym = dy_ref[2].astype(F32)
        smg, dsmg = _silu_and_grad(mg)
        dob = (dym * smg).astype(BF16)
        for h in range(N_HEAD):
            cs = slice(h * HEAD, (h + 1) * HEAD)
            vs = slice(WIDTH + h * HEAD, WIDTH + (h + 1) * HEAD)
            qh = mq_ref[:, cs]
            p = _mem_softmax(qh, kv_ref[:, cs])
            pb = p.astype(BF16)
            mix[:, cs] = _dot(pb, kv_ref[:, vs])
            dp = _dot_nt(dob[:, cs], kv_ref[:, vs])
            ds = (p * (dp - jnp.sum(p * dp, axis=-1, keepdims=True))).astype(BF16)
            dm_ref[:, cs] = (_dot(ds, kv_ref[:, cs]) * ATT_SCALE).astype(BF16)
            dkv_ref[:, cs] += _dot_tn(ds, qh) * ATT_SCALE
            dkv_ref[:, vs] += _dot_tn(pb, dob[:, cs])
        dm_ref[:, WIDTH:2 * WIDTH] = (dym * mix[...] * dsmg).astype(BF16)

    blk = tm // CHUNK
    small = [_full((1, WIDTH)), _full((1, WIDTH)), _full((N_HEAD, CHUNK, CHUNK)), _full((CHUNK, WIDTH)),
             _full((4, HEAD, HEAD)), _full((1, WIDTH)), _full((MEM_LEN, 2 * WIDTH))]
    return pl.pallas_call(
        body, name="abm_bwd",
        grid=(S // tm,),
        in_specs=[_rows(tm, WIDTH, CB_U), _rows(tm, WIDTH, CB_V), _rows(tm, WIDTH, CB_AGATE),
                  _rows(tm, WIDTH, CB_PIN),
                  pl.BlockSpec((CHUNK, WIDTH), lambda i: (jnp.maximum(i * blk - 1, 0), CB_PIN)),
                  _rows(tm, WIDTH, CB_PGATE),
                  pl.BlockSpec((CHUNK, WIDTH), lambda i: (jnp.minimum(i * blk + blk, nblk - 1), CB_PGATE)),
                  _rows(tm, WIDTH, CB_MQ), _rows(tm, WIDTH, CB_MGATE),
                  pl.BlockSpec((2, tm, WIDTH), lambda i: (0, i, 0)),
                  pl.BlockSpec((3, tm, WIDTH), lambda i: (0, i, 0)),
                  pl.BlockSpec((None, CHUNK, WIDTH), lambda i: (1, jnp.minimum(i * blk + blk, nblk - 1), 0)),
                  _full((1, WIDTH)), _full((1, WIDTH)), _full((N_HEAD, CHUNK, CHUNK)), _full((N_HEAD, CHUNK, CHUNK)),
                  _full((CHUNK, WIDTH)), _full((4, HEAD, HEAD)), _full((4, HEAD, HEAD)), _full((1, WIDTH)),
                  _full((MEM_LEN, 2 * WIDTH)), _full((4, 2, CHUNK, CHUNK)), _full((4, 2, CHUNK, CHUNK))],
        out_specs=[_rows(tm, 5 * WIDTH), _rows(tm, 2 * WIDTH)] + small,
        out_shape=[jax.ShapeDtypeStruct((S, D_BRANCHES), BF16), jax.ShapeDtypeStruct((S, 2 * WIDTH), BF16),
                   jax.ShapeDtypeStruct((1, WIDTH), F32), jax.ShapeDtypeStruct((1, WIDTH), F32),
                   jax.ShapeDtypeStruct((N_HEAD, CHUNK, CHUNK), F32), jax.ShapeDtypeStruct((CHUNK, WIDTH), F32),
                   jax.ShapeDtypeStruct((4, HEAD, HEAD), F32), jax.ShapeDtypeStruct((1, WIDTH), F32),
                   jax.ShapeDtypeStruct((MEM_LEN, 2 * WIDTH), F32)],
        scratch_shapes=[pltpu.VMEM((tm, WIDTH), F32), pltpu.VMEM((tm, WIDTH), F32),
                        pltpu.VMEM((tm + CHUNK, WIDTH), F32)],
        compiler_params=_params(("arbitrary",)),
    )(proj, proj, proj, proj, proj, proj, proj, proj, proj, cdf, dy3, dy3,
      ln_g, ln_b, wsm, wsm_t, bias_full, pool_w, pool_wt, pool_scale, kv, bands, bands_t)


def _bias_reduce(dbias_full):
    def body(d_ref, o_ref):
        d = d_ref[...]
        o_ref[...] = _put_cols([jnp.sum(d[:, h * HEAD:(h + 1) * HEAD], axis=1, keepdims=True) for h in range(N_HEAD)])

    return pl.pallas_call(body, name="bias_reduce", out_shape=jax.ShapeDtypeStruct((CHUNK, 128), F32))(dbias_full)


def _mem_bwd(mem, g, mem_n, w, dkv):
    def body(m_ref, g_ref, mn_ref, w_ref, dkv_ref, dw_ref, dg_ref):
        dkvb = dkv_ref[...].astype(BF16)
        dw_ref[...] = _dot_tn(mn_ref[...], dkvb).astype(BF16)
        dmn = _dot_nt(dkvb, w_ref[...])
        xf = m_ref[...]
        r = lax.rsqrt(jnp.mean(xf * xf, axis=-1, keepdims=True) + EPS)
        dg_ref[...] = jnp.sum(dmn * xf * r, axis=0, keepdims=True)

    return pl.pallas_call(
        body, name="mem_bwd",
        out_shape=[jax.ShapeDtypeStruct((D_MODEL, 2 * WIDTH), BF16), jax.ShapeDtypeStruct((1, D_MODEL), F32)],
        compiler_params=pltpu.CompilerParams(vmem_limit_bytes=VMEM_LIMIT),
    )(mem, g, mem_n, w, dkv)


def _dh_bwd(dpb, dpg, wt, wgt, x, g, dxo, parts=(), grads=()):
    S = x.shape[0]
    tm, tkb, tkg = 1024, D_BRANCHES // 4, D_GATES // 4
    nkb, nkg = 4, 4
    nk = nkb + nkg
    ni = S // tm
    n, m = len(parts), len(grads)

    def body(dpb_ref, wbr_ref, dpg_ref, wg_ref, x_ref, g_ref, dxo_ref, *rest):
        p_in, g_in = rest[:n], rest[n:n + m]
        dx_ref, dg_ref = rest[n + m:n + m + 2]
        p_out, g_out = rest[n + m + 2:2 * n + m + 2], rest[2 * n + m + 2:2 * (n + m) + 2]
        acc, sems = rest[2 * (n + m) + 2], rest[2 * (n + m) + 3:]
        second_sems, first_sems = (sems[:3] if n else ()), sems[3 if n else 0:]
        i, kk = pl.program_id(0), pl.program_id(1)

        @pl.when(jnp.logical_and(i == 0, kk == 0))
        def _():
            dg_ref[...] = jnp.zeros_like(dg_ref)
            if n:
                _comm_start(_rs_second(p_in, p_out, *second_sems))
            if m:
                _comm_start(_rs_first(g_in, g_out, *first_sems))

        @pl.when(kk == 0)
        def _():
            acc[...] = jnp.zeros_like(acc)

        @pl.when(kk < nkb)
        def _():
            acc[...] += _dot(dpb_ref[...], wbr_ref[...])

        @pl.when(kk >= nkb)
        def _():
            acc[...] += _dot(dpg_ref[...], wg_ref[...])

        @pl.when(kk == nk - 1)
        def _():
            xf = x_ref[...]
            r = lax.rsqrt(jnp.mean(xf * xf, axis=-1, keepdims=True) + EPS)
            xhat = xf * r
            dh = acc[...]
            dg_ref[...] += jnp.sum(dh * xhat, axis=0, keepdims=True)
            dxh = dh * g_ref[...]
            dx_ref[...] = dxo_ref[...] + r * (dxh - xhat * jnp.mean(dxh * xhat, axis=-1, keepdims=True))

        if n or m:
            @pl.when(jnp.logical_and(i == ni - 1, kk == nk - 1))
            def _():
                if n:
                    _comm_wait(_rs_second(p_in, p_out, *second_sems))
                if m:
                    _comm_wait(_rs_first(g_in, g_out, *first_sems))

    res = pl.pallas_call(
        body, name="dh_bwd_scatter" if (n or m) else "dh_bwd",
        grid=(ni, nk),
        in_specs=[pl.BlockSpec((tm, tkb), lambda i, k: (i, jnp.minimum(k, nkb - 1))),
                  pl.BlockSpec((tkb, D_MODEL), lambda i, k: (jnp.minimum(k, nkb - 1), 0)),
                  pl.BlockSpec((tm, tkg), lambda i, k: (i, jnp.maximum(k - nkb, 0))),
                  pl.BlockSpec((tkg, D_MODEL), lambda i, k: (jnp.maximum(k - nkb, 0), 0)),
                  pl.BlockSpec((tm, D_MODEL), lambda i, k: (i, 0)), pl.BlockSpec((1, D_MODEL), lambda i, k: (0, 0)),
                  pl.BlockSpec((tm, D_MODEL), lambda i, k: (i, 0))] + [ANY] * (n + m),
        out_specs=[pl.BlockSpec((tm, D_MODEL), lambda i, k: (i, 0)), pl.BlockSpec((1, D_MODEL), lambda i, k: (0, 0))]
                  + [ANY] * (n + m),
        out_shape=[jax.ShapeDtypeStruct((S, D_MODEL), F32), jax.ShapeDtypeStruct((1, D_MODEL), F32)]
                  + [jax.ShapeDtypeStruct(p.shape, p.dtype) for p in parts]
                  + [jax.ShapeDtypeStruct(gr.shape[:1] + gr.shape[2:], gr.dtype) for gr in grads],
        scratch_shapes=[pltpu.VMEM((tm, D_MODEL), F32)] + (_dma_sems(3 * n, 3 * n, n) if n else [])
                       + (_dma_sems(N_CHIP * m, N_CHIP * m) if m else []),
        compiler_params=_params(("arbitrary", "arbitrary")),
    )(dpb, wt, dpg, wgt, x, g, dxo, *parts, *grads)
    return res[0], res[1], list(res[2:2 + n]), list(res[2 + n:])


def _dw_in(h, dpb, dpg, parts=()):
    S = h.shape[0]
    tk = 2048
    nk = S // tk
    n = len(parts)
    tmb = D_BRANCHES // 4
    ng = D_GATES // GATE_TILE

    def accumulate(a_ref, h_ref, o_ref, acc):
        kk = pl.program_id(1)

        @pl.when(kk == 0)
        def _():
            acc[...] = jnp.zeros_like(acc)

        acc[...] += _dot_tn(a_ref[...], h_ref[...])

        @pl.when(kk == nk - 1)
        def _():
            o_ref[...] = acc[...].astype(BF16)

    def branches(a_ref, h_ref, *rest):
        p_in, o_ref, p_out = rest[:n], rest[n], rest[n + 1:2 * n + 1]
        acc, sems = rest[2 * n + 1], rest[2 * n + 2:]
        i, kk = pl.program_id(0), pl.program_id(1)

        if n:
            @pl.when(jnp.logical_and(i == 0, kk == 0))
            def _():
                _comm_start(_rs_second(p_in, p_out, *sems))

        accumulate(a_ref, h_ref, o_ref, acc)

        if n:
            @pl.when(jnp.logical_and(i == 3, kk == nk - 1))
            def _():
                _comm_wait(_rs_second(p_in, p_out, *sems))

    def gates(a_ref, h_ref, dst_ref, o_ref, acc):
        accumulate(a_ref, h_ref, o_ref, acc)

    res = pl.pallas_call(
        branches, name="dw_in_branches_scatter" if n else "dw_in_branches",
        grid=(4, nk),
        in_specs=[pl.BlockSpec((tk, tmb), lambda i, k: (k, i)), pl.BlockSpec((tk, D_MODEL), lambda i, k: (k, 0))]
                 + [ANY] * n,
        out_specs=[pl.BlockSpec((tmb, D_MODEL), lambda i, k: (i, 0))] + [ANY] * n,
        out_shape=[jax.ShapeDtypeStruct((D_IN, D_MODEL), BF16)]
                  + [jax.ShapeDtypeStruct(p.shape, p.dtype) for p in parts],
        scratch_shapes=[pltpu.VMEM((tmb, D_MODEL), F32)] + (_dma_sems(3 * n, 3 * n, n) if n else []),
        compiler_params=_params(("arbitrary", "arbitrary")),
    )(dpb, h, *parts)
    dwt = pl.pallas_call(
        gates, name="dw_in_gates",
        grid=(ng, nk),
        in_specs=[pl.BlockSpec((tk, GATE_TILE), lambda i, k: (k, i)), pl.BlockSpec((tk, D_MODEL), lambda i, k: (k, 0)),
                  ANY],
        out_specs=pl.BlockSpec((GATE_TILE, D_MODEL), lambda i, k: (D_BRANCHES // GATE_TILE + i, 0)),
        out_shape=jax.ShapeDtypeStruct((D_IN, D_MODEL), BF16),
        input_output_aliases={2: 0},
        scratch_shapes=[pltpu.VMEM((GATE_TILE, D_MODEL), F32)],
        compiler_params=_params(("parallel", "arbitrary")),
    )(dpg, h, res[0])
    return dwt, list(res[1:])


def _row_tile(R, C, block_bytes=2 << 20):
    for cand in range(min(R, block_bytes // (C * 4)) // 8 * 8, 0, -8):
        if R % cand == 0:
            return cand
    return R


def _adamw_update(p_ref, w_ref, m_ref, v_ref, g_ref, d_ref, nm_ref, nv_ref):
    c1 = 1.0 / (1.0 - ADAM_B1 ** ADAM_STEP)
    c2 = 1.0 / (1.0 - ADAM_B2 ** ADAM_STEP)
    g = p_ref[0].astype(F32)
    for k in range(1, p_ref.shape[0]):
        g = g + p_ref[k].astype(F32)
    nm = ADAM_B1 * m_ref[...] + (1.0 - ADAM_B1) * g
    nv = ADAM_B2 * v_ref[...] + (1.0 - ADAM_B2) * (g * g)
    g_ref[...] = g
    nm_ref[...] = nm
    nv_ref[...] = nv
    d_ref[...] = -ADAM_LR * ((nm * c1) / (jnp.sqrt(nv * c2) + ADAM_EPS) + ADAM_WD * w_ref[...])


def _adamw(parts, w, m, v, name):
    P, R, C = parts.shape
    tr = _row_tile(R, C)

    def body(*refs):
        _adamw_update(*refs)

    spec = pl.BlockSpec((tr, C), lambda i: (i, 0))
    return pl.pallas_call(
        body, name=name,
        grid=(R // tr,),
        in_specs=[pl.BlockSpec((P, tr, C), lambda i: (0, i, 0)), spec, spec, spec],
        out_specs=[spec] * 4,
        out_shape=[jax.ShapeDtypeStruct((R, C), F32)] * 4,
        compiler_params=_params(("parallel",)),
    )(parts, w, m, v)


def _adamw_layers(parts, w, m, v, name):
    depth = len(parts)
    P, R, C = parts[0].shape
    tr = _row_tile(R, C, 1 << 20)

    def body(*refs):
        layer = pl.program_id(0)
        for k in range(depth):
            @pl.when(layer == k)
            def _(k=k):
                _adamw_update(refs[k], *refs[depth:])

    def part_spec(k):
        return pl.BlockSpec((P, tr, C), lambda l, i: (0, jnp.where(l == k, i, 0), 0))

    spec = pl.BlockSpec((None, tr, C), lambda l, i: (l, i, 0))
    return pl.pallas_call(
        body, name=name,
        grid=(depth, R // tr),
        in_specs=[part_spec(k) for k in range(depth)] + [spec] * 3,
        out_specs=[spec] * 4,
        out_shape=[jax.ShapeDtypeStruct((depth, R, C), F32)] * 4,
        compiler_params=_params(("arbitrary", "arbitrary")),
    )(*parts, w, m, v)


def _place():
    return lax.axis_index("x"), lax.axis_index("y"), lax.axis_index("c")


def _all_gather(shards):
    n = len(shards)

    def body(*refs):
        ins, outs = refs[:n], refs[n:2 * n]
        send1, recv1, local_sems, send2, recv2 = refs[2 * n:]
        first = _ag_first(ins, outs, send1, recv1, local_sems)
        second = _ag_second(outs, send2, recv2)
        _comm_start(first)
        for j in range(3):
            for a in range(n):
                first[2][4 * a + 1 + j].wait_recv()
            for a in range(n):
                second[1][3 * a + j].start()
        for a in range(n):
            first[2][4 * a].wait_recv()
        for cp in second[2]:
            cp.wait_recv()
        for cp in first[1] + second[1]:
            cp.wait_send()
        for cp in first[0]:
            cp.wait()

    return pl.pallas_call(
        body, name="weights_all_gather",
        in_specs=[ANY] * n, out_specs=[ANY] * n,
        out_shape=[jax.ShapeDtypeStruct((N_DEV,) + s.shape, s.dtype) for s in shards],
        scratch_shapes=_dma_sems(4 * n, 4 * n, n, 3 * n, 3 * n),
        compiler_params=pltpu.CompilerParams(has_side_effects=True),
    )(*shards)


N_BIG = 4


def _dev(p):
    return 4 * p[0] + 2 * p[1] + p[2]


def _other_chips(x, y):
    return [(1 - x, y), (x, 1 - y), (1 - x, 1 - y)]


def _remote(src, dst, send_sems, recv_sems, k, to):
    return pltpu.make_async_remote_copy(src_ref=src, dst_ref=dst, send_sem=send_sems.at[k], recv_sem=recv_sems.at[k],
                                        device_id=to, device_id_type=MESH)


def _ag_first(ins, outs, send_sems, recv_sems, local_sems):
    x, y, c = _place()
    me = (x, y, c)
    targets = [(x, y, 1 - c)] + [(*chip, c) for chip in _other_chips(x, y)]
    local, out, inc = [], [], []
    for a in range(len(ins)):
        local.append(pltpu.make_async_copy(ins[a], outs[a].at[_dev(me)], local_sems.at[a]))
        for k, to in enumerate(targets):
            out.append(_remote(ins[a], outs[a].at[_dev(me)], send_sems, recv_sems, 4 * a + k, to))
            inc.append(_remote(ins[a], outs[a].at[_dev(to)], send_sems, recv_sems, 4 * a + k, to))
    return local, out, inc


def _ag_second(bufs, send_sems, recv_sems):
    x, y, c = _place()
    out, inc = [], []
    for a in range(len(bufs)):
        for j, chip in enumerate(_other_chips(x, y)):
            mine, theirs = bufs[a].at[_dev((*chip, c))], bufs[a].at[_dev((*chip, 1 - c))]
            out.append(_remote(mine, mine, send_sems, recv_sems, 3 * a + j, (x, y, 1 - c)))
            inc.append(_remote(theirs, theirs, send_sems, recv_sems, 3 * a + j, (x, y, 1 - c)))
    return [], out, inc


def _rs_first(ins, outs, send_sems, recv_sems):
    x, y, c = _place()
    out = [_remote(ins[a].at[j, 1 - c], outs[a].at[j], send_sems, recv_sems, N_CHIP * a + j, (x, y, 1 - c))
           for a in range(len(ins)) for j in range(N_CHIP)]
    return [], out, out


def _rs_second(ins, outs, send_sems, recv_sems, local_sems):
    x, y, c = _place()
    my_chip = 2 * x + y
    local, out, inc = [], [], []
    for a in range(len(ins)):
        local.append(pltpu.make_async_copy(ins[a].at[my_chip], outs[a].at[my_chip], local_sems.at[a]))
        for k, (ox, oy) in enumerate(_other_chips(x, y)):
            out.append(_remote(ins[a].at[2 * ox + oy], outs[a].at[my_chip], send_sems, recv_sems, 3 * a + k, (ox, oy, c)))
            inc.append(_remote(ins[a].at[2 * ox + oy], outs[a].at[2 * ox + oy], send_sems, recv_sems, 3 * a + k,
                               (ox, oy, c)))
    return local, out, inc


def _comm_start(exchange):
    local, out, _ = exchange
    for cp in local + out:
        cp.start()


def _comm_wait(exchange):
    local, out, inc = exchange
    for cp in inc:
        cp.wait_recv()
    for cp in out:
        cp.wait_send()
    for cp in local:
        cp.wait()


def _dma_sems(*counts):
    return [pltpu.SemaphoreType.DMA((n,)) for n in counts]


def _pair_sum(grads, recvs):
    n = len(grads)

    def body(c_ref, *refs):
        for a in range(n):
            refs[2 * n + a][...] = (refs[a][...].astype(F32) + refs[n + a][...].astype(F32)).astype(BF16)

    def g_spec(g):
        return pl.BlockSpec((None, None) + g.shape[2:], lambda j, c_ref: (j, c_ref[0], 0, 0))

    def r_spec(r):
        return pl.BlockSpec((None,) + r.shape[1:], lambda j, c_ref: (j, 0, 0))

    return pl.pallas_call(
        body, name="pair_sum",
        grid_spec=pltpu.PrefetchScalarGridSpec(
            num_scalar_prefetch=1, grid=(N_CHIP,),
            in_specs=[g_spec(g) for g in grads] + [r_spec(r) for r in recvs],
            out_specs=[r_spec(r) for r in recvs]),
        out_shape=[jax.ShapeDtypeStruct(r.shape, BF16) for r in recvs],
        compiler_params=_params(("parallel",)),
    )(lax.axis_index("c").reshape(1).astype(jnp.int32), *grads, *recvs)


SMALL_ROWS = 544


def _all_reduce_small(buf, parts=()):
    n = len(parts)

    def body(in_ref, *rest):
        p_in, out_ref, p_out = rest[:n], rest[n], rest[n + 1:2 * n + 1]
        recv, acc, send1, recv1, send2, recv2 = rest[2 * n + 1:2 * n + 7]
        scatter_sems = rest[2 * n + 7:]
        x, y, c = _place()
        me = 4 * x + 2 * y + c
        peers = [(x ^ (r >> 2), y ^ ((r >> 1) & 1), c ^ (r & 1)) for r in range(1, N_DEV)]

        def idx(p):
            return 4 * p[0] + 2 * p[1] + p[2]

        if n:
            _comm_start(_rs_second(p_in, p_out, *scatter_sems))
        first = [pltpu.make_async_remote_copy(
            src_ref=in_ref.at[idx(p)], dst_ref=recv.at[me], send_sem=send1.at[r], recv_sem=recv1.at[r],
            device_id=p, device_id_type=MESH) for r, p in enumerate(peers)]
        for cp in first:
            cp.start()
        recv[me] = in_ref[me]
        for r, p in enumerate(peers):
            pltpu.make_async_remote_copy(
                src_ref=in_ref.at[idx(p)], dst_ref=recv.at[idx(p)], send_sem=send1.at[r], recv_sem=recv1.at[r],
                device_id=p, device_id_type=MESH).wait_recv()
        total = recv[0]
        for k in range(1, N_DEV):
            total = total + recv[k]
        acc[...] = total
        out_ref[me] = total
        second = [pltpu.make_async_remote_copy(
            src_ref=acc, dst_ref=out_ref.at[me], send_sem=send2.at[r], recv_sem=recv2.at[r],
            device_id=p, device_id_type=MESH) for r, p in enumerate(peers)]
        for cp in second:
            cp.start()
        for r, p in enumerate(peers):
            pltpu.make_async_remote_copy(
                src_ref=acc, dst_ref=out_ref.at[idx(p)], send_sem=send2.at[r], recv_sem=recv2.at[r],
                device_id=p, device_id_type=MESH).wait_recv()
        for cp in first + second:
            cp.wait_send()
        if n:
            _comm_wait(_rs_second(p_in, p_out, *scatter_sems))

    vm = pl.BlockSpec(memory_space=pltpu.VMEM)
    res = pl.pallas_call(
        body, name="small_grads_all_reduce",
        in_specs=[vm] + [ANY] * n, out_specs=[vm] + [ANY] * n,
        out_shape=[jax.ShapeDtypeStruct(buf.shape, F32)] + [jax.ShapeDtypeStruct(p.shape, p.dtype) for p in parts],
        scratch_shapes=[pltpu.VMEM(buf.shape, F32), pltpu.VMEM(buf.shape[1:], F32)] + _dma_sems(7, 7, 7, 7)
                       + (_dma_sems(3 * n, 3 * n, n) if n else []),
        compiler_params=pltpu.CompilerParams(has_side_effects=True, vmem_limit_bytes=VMEM_LIMIT),
    )(buf, *parts)
    return res[0], list(res[1:])


def _dilate(a, d):
    if d == 1:
        return a
    S, C = a.shape
    return a.reshape(S // d, d, C).transpose(1, 0, 2).reshape(S, C)


def _undilate(a, d):
    if d == 1:
        return a
    S, C = a.shape
    return a.reshape(d, S // d, C).transpose(1, 0, 2).reshape(S, C)


def _cols(a, cb, n=1):
    return a[:, cb * WIDTH:(cb + n) * WIDTH]


def _to_blocks(g, kind):
    if kind == "rows":
        C = g.shape[1]
        return g.reshape(N_CHIP, 2, -1, C)
    return g.reshape(4 * WIDTH, N_CHIP, 2, -1).transpose(1, 2, 0, 3)


SMALL = ("norm_g", "gm_ln_g", "gm_ln_b", "gm_ws", "gm_bs", "pool_w", "pool_scale", "mem_norm_g", "final_norm_g")


def _pack_small(tree):
    flat = jnp.concatenate([tree[k].reshape(-1, 128) for k in SMALL], axis=0)
    return jnp.pad(flat, ((0, N_DEV * SMALL_ROWS - flat.shape[0]), (0, 0)))


def _unpack_small(flat, like):
    out, at = {}, 0
    for k in SMALL:
        rows = like[k].size // 128
        out[k] = flat[at:at + rows].reshape(like[k].shape)
        at += rows
    return out


def _make_layer(wt, wkv, wb, wout, norm_g, mem_norm_g, ln_g, ln_b, gm_ws, gm_bs, pool_w, pool_scale):
    tril = jnp.tril(jnp.ones((CHUNK, CHUNK), bool))
    wsm = jnp.where(tril, gm_ws, 0.0).astype(BF16)
    pw = pool_w.astype(BF16)
    bands, bands_t = _band_constants()
    return dict(wt=wt, wgt=wt[D_BRANCHES:], wkv=wkv, wb=wb, wout=wout, g=norm_g[None], mg=mem_norm_g[None],
                ln_g=ln_g[None],
                ln_b=ln_b[None], wsm=wsm, wsm_t=wsm.transpose(0, 2, 1), pw=pw, pw_t=pw.transpose(0, 2, 1),
                ps=pool_scale[None], bias=jnp.repeat(gm_bs.T, HEAD, axis=1), bands=bands, bands_t=bands_t)


def _layer_fwd(xl, mem0, L, next_shards=()):
    S = xl.shape[0]
    proj, gates, h, half_win = _in_proj(xl, L["g"], L["wt"], L["wgt"], next_shards[:1])
    kv, mem_n = _mem_kv(mem0, L["mg"], L["wkv"])
    y4, cdf, win, half_small = _abm_fwd(proj, L["ln_g"], L["ln_b"], L["wsm"], L["bias"], L["pw"], L["ps"], kv,
                                        L["bands"], half_win, next_shards[2:])
    o_g, l_g = [], []
    for gi, d in enumerate(DILATIONS):
        if d == 1:
            o, lse = _attn_fwd(proj, CB_Q0, proj, CB_K, proj, CB_CV, S // CHUNK)
        else:
            o, lse = _attn_fwd_dilated(proj, CB_Q0 + gi, CB_K, CB_CV, d)
        o_g.append(o)
        l_g.append(lse)
    (xn, y4, oc, lse, z), small = _merge_fwd(xl, y4, o_g, l_g, proj, gates, L["wb"], L["wout"], next_shards[1:2],
                                             half_small)
    saved = dict(x=xl, proj=proj, gates=gates, h=h, kv=kv, mem_n=mem_n, y4=y4, cdf=cdf, oc=oc, lse=lse, z=z)
    return xn, saved, win + small


def _place_cols(dst, piece, cb):
    return lax.dynamic_update_slice(dst, piece, (0, cb * WIDTH))


def _layer_bwd(dx, mem0, L, sv, later=(), last=False):
    S = dx.shape[0]
    proj = sv["proj"]
    (dy3, doc, delta, dcg, dgm, dwb, dwout), from_sibling = _merge_bwd(
        dx, sv["y4"], sv["oc"], sv["z"], proj, sv["gates"], L["wb"], L["wout"], later)
    pair = _pair_sum(later, from_sibling) if later else ()
    dpb, dm, dlng, dlnb, dws, dbias, dpw, dps, dkv = _abm_bwd(
        proj, sv["cdf"], dy3, L["ln_g"], L["ln_b"], L["wsm"], L["wsm_t"], L["bias"], L["pw"], L["pw_t"], L["ps"], sv["kv"],
        L["bands"], L["bands_t"])
    dk, dv = None, None
    for gi, d in enumerate(DILATIONS):
        if d == 1:
            dpb, dk, dv = _attn_bwd(proj, CB_Q0, proj, CB_K, proj, CB_CV, doc, sv["lse"], delta, S // CHUNK,
                                    dpb, CB_Q0)
        else:
            dpb, dk, dv = _attn_bwd_dilated(proj, CB_Q0 + gi, CB_K, CB_CV, doc, sv["lse"], delta, d, dk, dv,
                                            dpb, CB_Q0 + gi)
    dpb = _place_cols(dpb, dk, CB_K)
    dpb = _place_cols(dpb, dv, CB_CV)
    dpb = _place_cols(dpb, dcg, CB_CGATE)
    dpb = _place_cols(dpb, dm, CB_MQ)
    dwkv, dmg = _mem_bwd(mem0, L["mg"], sv["mem_n"], L["wkv"], dkv)
    dwin_t, parts_rest = _dw_in(sv["h"], dpb, dgm, pair[1:])
    big = _blocked(dict(w_in=dwin_t, w_mem_kv=dwkv, w_branch=dwb, w_out=dwout))
    dxi, dng, parts, from_sibling = _dh_bwd(dpb, dgm, L["wt"], L["wgt"], sv["x"], L["g"], dx, pair[:1],
                                            big if last else ())
    parts = parts + parts_rest
    small = dict(norm_g=dng[0], gm_ln_g=dlng[0], gm_ln_b=dlnb[0], gm_ws=dws,
                 gm_bs=_bias_reduce(dbias)[:, :N_HEAD].T, pool_w=dpw, pool_scale=dps[0], mem_norm_g=dmg[0])
    return dxi, big, small, parts, from_sibling


BIG = ("w_in", "w_mem_kv", "w_branch", "w_out")


def _blocked(big):
    return [_to_blocks(big["w_in"], "rows"), _to_blocks(big["w_mem_kv"], "rows"),
            _to_blocks(big["w_branch"], "branch"), _to_blocks(big["w_out"], "rows")]


def _full_weights(gathered):
    win_t, wkv, wb, wout = gathered
    return (win_t.reshape(D_IN, D_MODEL), wkv.reshape(D_MODEL, 2 * WIDTH),
            wb.reshape(N_DEV, 4, WIDTH, -1).transpose(1, 2, 0, 3).reshape(4, WIDTH, D_MODEL),
            wout.reshape(D_MODEL, D_MODEL))


def kernel(x, mem, norm_g, w_in, gm_ln_g, gm_ln_b, gm_ws, gm_bs, pool_w, pool_scale, mem_norm_g, w_mem_kv, w_branch, w_out, final_norm_g, loss_target, m_norm_g, m_w_in, m_gm_ln_g, m_gm_ln_b, m_gm_ws, m_gm_bs, m_pool_w, m_pool_scale, m_mem_norm_g, m_w_mem_kv, m_w_branch, m_w_out, m_final_norm_g, v_norm_g, v_w_in, v_gm_ln_g, v_gm_ln_b, v_gm_ws, v_gm_bs, v_pool_w, v_pool_scale, v_mem_norm_g, v_w_mem_kv, v_w_branch, v_w_out, v_final_norm_g):
    x0 = x[0]
    mem0 = mem[0]
    tgt = loss_target[0]
    S = x0.shape[0]

    shards = [[w_in[l].T.astype(BF16), w_mem_kv[l].astype(BF16), w_branch[l].astype(BF16).reshape(4 * WIDTH, -1),
               w_out[l].astype(BF16)] for l in range(DEPTH)]
    gathered = _all_gather(shards[0])
    layers, saved = [], []
    xl = x0
    for l in range(DEPTH):
        layers.append(_make_layer(*_full_weights(gathered), norm_g[l], mem_norm_g[l], gm_ln_g[l], gm_ln_b[l],
                                  gm_ws[l], gm_bs[l], pool_w[l], pool_scale[l]))
        xl, sv, gathered = _layer_fwd(xl, mem0, layers[l], shards[l + 1] if l + 1 < DEPTH else ())
        saved.append(sv)

    loss_part, dx, d_final = _loss_head(xl, final_norm_g[None], tgt)
    loss = lax.psum(loss_part[0, 0], ("x", "y", "c"))

    small = {k: [None] * DEPTH for k in SMALL if k != "final_norm_g"}
    parts = [None] * DEPTH
    later = ()
    for l in reversed(range(DEPTH)):
        dx, gb, gs, done, from_sibling = _layer_bwd(dx, mem0, layers[l], saved[l], later, last=(l == 0))
        if later:
            parts[l + 1] = done
        later = gb
        for k in gs:
            small[k][l] = gs[k]
    grad_x = dx[None]
    small_tree = {k: jnp.stack(small[k]) for k in small}
    small_tree["final_norm_g"] = d_final[0]
    reduced, parts[0] = _all_reduce_small(_pack_small(small_tree).reshape(N_DEV, SMALL_ROWS, 128),
                                          _pair_sum(later, from_sibling))

    weights = dict(norm_g=norm_g, w_in=w_in, gm_ln_g=gm_ln_g, gm_ln_b=gm_ln_b, gm_ws=gm_ws, gm_bs=gm_bs,
                   pool_w=pool_w, pool_scale=pool_scale, mem_norm_g=mem_norm_g, w_mem_kv=w_mem_kv,
                   w_branch=w_branch, w_out=w_out, final_norm_g=final_norm_g)
    m_in = dict(norm_g=m_norm_g, w_in=m_w_in, gm_ln_g=m_gm_ln_g, gm_ln_b=m_gm_ln_b, gm_ws=m_gm_ws, gm_bs=m_gm_bs,
                pool_w=m_pool_w, pool_scale=m_pool_scale, mem_norm_g=m_mem_norm_g, w_mem_kv=m_w_mem_kv,
                w_branch=m_w_branch, w_out=m_w_out, final_norm_g=m_final_norm_g)
    v_in = dict(norm_g=v_norm_g, w_in=v_w_in, gm_ln_g=v_gm_ln_g, gm_ln_b=v_gm_ln_b, gm_ws=v_gm_ws, gm_bs=v_gm_bs,
                pool_w=v_pool_w, pool_scale=v_pool_scale, mem_norm_g=v_mem_norm_g, w_mem_kv=v_w_mem_kv,
                w_branch=v_w_branch, w_out=v_w_out, final_norm_g=v_final_norm_g)
    res = {}
    def view(k, arr):
        return arr.transpose(0, 2, 1) if k == "w_in" else arr

    for a, k in enumerate(BIG):
        shape = view(k, weights[k]).shape
        by_layer = [parts[l][a] for l in range(DEPTH)]
        lrc = (DEPTH,) + by_layer[0].shape[1:]
        outs = _adamw_layers(by_layer, view(k, weights[k]).reshape(lrc), view(k, m_in[k]).reshape(lrc),
                             view(k, v_in[k]).reshape(lrc), "adamw_" + k)
        res[k] = [view(k, o.reshape(shape)) for o in outs]
    outs = _adamw(reduced.reshape(1, N_DEV * SMALL_ROWS, 128), _pack_small(weights), _pack_small(m_in),
                  _pack_small(v_in), "adamw_small")
    unpacked = [_unpack_small(o, weights) for o in outs]
    for k in SMALL:
        res[k] = [u[k] for u in unpacked]

    order = ("norm_g", "w_in", "gm_ln_g", "gm_ln_b", "gm_ws", "gm_bs", "pool_w", "pool_scale", "mem_norm_g",
             "w_mem_kv", "w_branch", "w_out", "final_norm_g")
    return (loss, grad_x, *[res[k][0] for k in order], *[res[k][1] for k in order],
            *[res[k][2] for k in order], *[res[k][3] for k in order])
```

```python
import functools
import math

import numpy as np

import jax
import jax.numpy as jnp
from jax import lax
from jax.experimental import pallas as pl
from jax.experimental.pallas import tpu as pltpu

F32 = jnp.float32
BF16 = jnp.bfloat16

D_MODEL = 1024
DEPTH = 4
WIDTH = 512
D_IN = 10752
HEAD = 128
N_HEAD = 4
CHUNK = 128
MEM_LEN = 256
POOL_WINDOWS = (2, 4, 8, 16)
DILATIONS = (1, 4, 16)
EPS = 1e-6
NEG = -1e30
ATT_SCALE = HEAD ** -0.5
N_DEV = 8
N_CHIP = 4

D_BRANCHES = 6656
D_GATES = D_IN - D_BRANCHES
CB_U, CB_V, CB_AGATE, CB_PIN, CB_PGATE = 0, 1, 2, 3, 4
CB_Q0, CB_K, CB_CV, CB_CGATE, CB_MQ, CB_MGATE = 5, 8, 9, 10, 11, 12

ADAM_LR = 0.001
ADAM_B1 = 0.9
ADAM_B2 = 0.999
ADAM_EPS = 1e-08
ADAM_WD = 0.01
ADAM_STEP = 10

VMEM_LIMIT = 56 * 1024 * 1024
MESH = pl.DeviceIdType.MESH
ANY = pl.BlockSpec(memory_space=pl.ANY)

NT = (((1,), (1,)), ((), ()))
TN = (((0,), (0,)), ((), ()))


def _dot(a, b):
    return jnp.dot(a, b, preferred_element_type=F32)


def _dot_nt(a, b):
    return lax.dot_general(a, b, NT, preferred_element_type=F32)


def _dot_tn(a, b):
    return lax.dot_general(a, b, TN, preferred_element_type=F32)


def _sigmoid(x):
    return 0.5 * jnp.tanh(0.5 * x) + 0.5


def _silu(x):
    return x * _sigmoid(x)


def _silu_and_grad(x):
    s = _sigmoid(x)
    return x * s, s * (1.0 + x * (1.0 - s))


def _normal_cdf(x):
    return 0.5 * (1.0 + lax.erf(x * (2.0 ** -0.5)))


def _gelu_and_grad(x, cdf):
    return x * cdf, cdf + x * jnp.exp(-0.5 * x * x) * (1.0 / math.sqrt(2.0 * math.pi))


def _col(blk, h):
    lane = lax.broadcasted_iota(jnp.int32, blk.shape, 1)
    return jnp.sum(jnp.where(lane == h, blk, 0.0), axis=1, keepdims=True)


def _put_cols(cols):
    rows = cols[0].shape[0]
    lane = lax.broadcasted_iota(jnp.int32, (rows, 128), 1)
    out = jnp.zeros((rows, 128), F32)
    for h, cv in enumerate(cols):
        out = jnp.where(lane == h, cv, out)
    return out


def _params(sem, vmem=VMEM_LIMIT):
    return pltpu.CompilerParams(dimension_semantics=sem, vmem_limit_bytes=vmem)


def _full(shape):
    nd = len(shape)
    return pl.BlockSpec(shape, lambda *_: (0,) * nd)


def _resident(shape):
    nd = len(shape)
    return pl.BlockSpec(shape, lambda *_: (0,) * nd, pipeline_mode=pl.Buffered(1))


def _rows(tm, width, cb=0):
    return pl.BlockSpec((tm, width), lambda i: (i, cb))


GATE_TILE = 512


def _in_proj(x, g, wt, wgt, shards=()):
    S = x.shape[0]
    tm, tnb, tng = 1024, D_BRANCHES // 4, D_GATES // 4
    njb, njg = 4, 4
    n = len(shards)
    ni, nj = S // tm, njb + njg

    def body(x_ref, g_ref, wbr_ref, wg_ref, *rest):
        ins, (proj_ref, gates_ref, h_ref), outs = rest[:n], rest[n:n + 3], rest[n + 3:2 * n + 3]
        hs, sems = rest[2 * n + 3], rest[2 * n + 4:]
        i, j = pl.program_id(0), pl.program_id(1)

        if n:
            @pl.when(jnp.logical_and(i == 0, j == 0))
            def _():
                _comm_start(_ag_first(ins, outs, *sems))

        @pl.when(j == 0)
        def _():
            xf = x_ref[...]
            r = lax.rsqrt(jnp.mean(xf * xf, axis=-1, keepdims=True) + EPS)
            h = (xf * r * g_ref[...]).astype(BF16)
            hs[...] = h
            h_ref[...] = h

        @pl.when(j < njb)
        def _():
            proj_ref[...] = _dot_nt(hs[...], wbr_ref[...]).astype(BF16)

        @pl.when(j >= njb)
        def _():
            gates_ref[...] = _dot_nt(hs[...], wg_ref[...]).astype(BF16)

        if n:
            @pl.when(jnp.logical_and(i == ni - 1, j == nj - 1))
            def _():
                _comm_wait(_ag_first(ins, outs, *sems))

    def first(j):
        return jnp.minimum(j, njb - 1)

    def second(j):
        return jnp.maximum(j - njb, 0)

    res = pl.pallas_call(
        body, name="in_proj_gather" if n else "in_proj",
        grid=(ni, nj),
        in_specs=[pl.BlockSpec((tm, D_MODEL), lambda i, j: (i, 0)),
                  pl.BlockSpec((1, D_MODEL), lambda i, j: (0, 0)),
                  pl.BlockSpec((tnb, D_MODEL), lambda i, j: (first(j), 0)),
                  pl.BlockSpec((tng, D_MODEL), lambda i, j: (second(j), 0))]
                 + [ANY] * n,
        out_specs=[pl.BlockSpec((tm, tnb), lambda i, j: (i, first(j))),
                   pl.BlockSpec((tm, tng), lambda i, j: (i, second(j))),
                   pl.BlockSpec((tm, D_MODEL), lambda i, j: (i, 0))] + [ANY] * n,
        out_shape=[jax.ShapeDtypeStruct((S, D_BRANCHES), BF16), jax.ShapeDtypeStruct((S, D_GATES), BF16),
                   jax.ShapeDtypeStruct((S, D_MODEL), BF16)]
                  + [jax.ShapeDtypeStruct((N_DEV,) + s.shape, s.dtype) for s in shards],
        scratch_shapes=[pltpu.VMEM((tm, D_MODEL), BF16)] + (_dma_sems(4 * n, 4 * n, n) if n else []),
        compiler_params=_params(("arbitrary", "arbitrary")),
    )(x, g, wt, wgt, *shards)
    return res[0], res[1], res[2], list(res[3:])


def _mem_kv(mem, g, w):
    M = mem.shape[0]

    def body(m_ref, g_ref, w_ref, kv_ref, mn_ref):
        xf = m_ref[...]
        r = lax.rsqrt(jnp.mean(xf * xf, axis=-1, keepdims=True) + EPS)
        mn = (xf * r * g_ref[...]).astype(BF16)
        mn_ref[...] = mn
        kv_ref[...] = _dot(mn, w_ref[...]).astype(BF16)

    return pl.pallas_call(
        body, name="mem_kv",
        out_shape=[jax.ShapeDtypeStruct((M, 2 * WIDTH), BF16), jax.ShapeDtypeStruct((M, D_MODEL), BF16)],
        compiler_params=pltpu.CompilerParams(vmem_limit_bytes=VMEM_LIMIT),
    )(mem, g, w)


def _band_constants():
    t = np.arange(CHUNK)[:, None]
    s = np.arange(CHUNK)[None, :]
    bands = np.stack([np.stack([(t - s >= 0) & (t - s < win), s > t + CHUNK - win]) for win in POOL_WINDOWS])
    bands = bands.astype(np.float32)
    return jnp.asarray(bands, BF16), jnp.asarray(bands.transpose(0, 1, 3, 2), BF16)


def _inv_count(first_row, win):
    t = first_row + lax.broadcasted_iota(jnp.int32, (CHUNK, 1), 0)
    return 1.0 / jnp.minimum(t + 1, win).astype(F32)


def _layer_norm_fwd(v):
    mu = jnp.mean(v, axis=-1, keepdims=True)
    vc = v - mu
    var = jnp.mean(vc * vc, axis=-1, keepdims=True)
    rstd = lax.rsqrt(var + EPS)
    return vc * rstd, rstd


def _mem_softmax(q, kmem):
    s = _dot_nt(q, kmem) * ATT_SCALE
    m = jnp.max(s, axis=-1, keepdims=True)
    e = jnp.exp(s - m)
    return e * (1.0 / jnp.sum(e, axis=-1, keepdims=True))


def _abm_fwd(proj, ln_g, ln_b, wsm, bias_full, pool_w, pool_scale, kv, bands, gathered=(), shards=()):
    S = proj.shape[0]
    tm = 512
    nchunk = tm // CHUNK
    n, m = len(gathered), len(shards)
    nsteps = S // tm

    def body(u_ref, v_ref, ag_ref, p_ref, ph_ref, pg_ref, mq_ref, mg_ref, lng_ref, lnb_ref, wsm_ref, bias_ref,
             pw_ref, ps_ref, kv_ref, band_ref, *rest):
        s_in = rest[n:n + m]
        y_ref, cdf_ref = rest[n + m], rest[n + m + 1]
        bufs, s_out = rest[n + m + 2:2 * n + m + 2], rest[2 * n + m + 2:2 * (n + m) + 2]
        mix, sems = rest[2 * (n + m) + 2], rest[2 * (n + m) + 3:]
        second_sems, first_sems = (sems[:2] if n else ()), sems[2 if n else 0:]
        i = pl.program_id(0)

        if n or m:
            @pl.when(i == 0)
            def _():
                if n:
                    _comm_start(_ag_second(bufs, *second_sems))
                if m:
                    _comm_start(_ag_first(s_in, s_out, *first_sems))

        au, av = u_ref[...].astype(F32), v_ref[...].astype(F32)
        cdf_u, cdf_v = _normal_cdf(au), _normal_cdf(av)
        cdf_ref[0] = cdf_u.astype(BF16)
        cdf_ref[1] = cdf_v.astype(BF16)
        u, v = au * cdf_u, av * cdf_v
        vhat, _ = _layer_norm_fwd(v)
        vln = (vhat * lng_ref[...] + lnb_ref[...]).astype(BF16)
        for c in range(nchunk):
            for h in range(N_HEAD):
                rs, cs = slice(c * CHUNK, (c + 1) * CHUNK), slice(h * HEAD, (h + 1) * HEAD)
                mix[rs, cs] = _dot(wsm_ref[h], vln[rs, cs]) + bias_ref[:, cs]
        y_ref[0] = (u * mix[...] * _silu(ag_ref[...].astype(F32))).astype(BF16)
        halo_ok = (i > 0).astype(F32)
        for c in range(nchunk):
            rs = slice(c * CHUNK, (c + 1) * CHUNK)
            for g, win in enumerate(POOL_WINDOWS):
                cs = slice(g * HEAD, (g + 1) * HEAD)
                cur = p_ref[rs, cs]
                if c == 0:
                    prev = (ph_ref[:, cs].astype(F32) * halo_ok).astype(BF16)
                else:
                    prev = p_ref[(c - 1) * CHUNK:c * CHUNK, cs]
                sums = _dot(band_ref[g, 0], cur) + _dot(band_ref[g, 1], prev)
                dm = sums * _inv_count(i * tm + c * CHUNK, win) - cur.astype(F32)
                mix[rs, cs] = _dot(dm.astype(BF16), pw_ref[g])
        y_ref[1] = (mix[...] * ps_ref[...] * _silu(pg_ref[...].astype(F32))).astype(BF16)
        for h in range(N_HEAD):
            cs = slice(h * HEAD, (h + 1) * HEAD)
            p = _mem_softmax(mq_ref[:, cs], kv_ref[:, cs])
            mix[:, cs] = _dot(p.astype(BF16), kv_ref[:, WIDTH + h * HEAD:WIDTH + (h + 1) * HEAD])
        y_ref[2] = (mix[...] * _silu(mg_ref[...].astype(F32))).astype(BF16)

        if n or m:
            @pl.when(i == nsteps - 1)
            def _():
                if n:
                    _comm_wait(_ag_second(bufs, *second_sems))
                if m:
                    _comm_wait(_ag_first(s_in, s_out, *first_sems))

    blk = tm // CHUNK
    res = pl.pallas_call(
        body, name="abm_fwd_gather" if n or m else "abm_fwd",
        grid=(nsteps,),
        in_specs=[_rows(tm, WIDTH, CB_U), _rows(tm, WIDTH, CB_V), _rows(tm, WIDTH, CB_AGATE),
                  _rows(tm, WIDTH, CB_PIN),
                  pl.BlockSpec((CHUNK, WIDTH), lambda i: (jnp.maximum(i * blk - 1, 0), CB_PIN)),
                  _rows(tm, WIDTH, CB_PGATE), _rows(tm, WIDTH, CB_MQ), _rows(tm, WIDTH, CB_MGATE),
                  _full((1, WIDTH)), _full((1, WIDTH)), _full((N_HEAD, CHUNK, CHUNK)), _full((CHUNK, WIDTH)),
                  _full((4, HEAD, HEAD)), _full((1, WIDTH)), _full((MEM_LEN, 2 * WIDTH)),
                  _full((4, 2, CHUNK, CHUNK))] + [ANY] * (n + m),
        out_specs=[pl.BlockSpec((3, tm, WIDTH), lambda i: (0, i, 0)), pl.BlockSpec((2, tm, WIDTH), lambda i: (0, i, 0))]
                  + [ANY] * (n + m),
        out_shape=[jax.ShapeDtypeStruct((4, S, WIDTH), BF16),
                   jax.ShapeDtypeStruct((2, S, WIDTH), BF16)]
                  + [jax.ShapeDtypeStruct(b.shape, b.dtype) for b in gathered]
                  + [jax.ShapeDtypeStruct((N_DEV,) + s.shape, s.dtype) for s in shards],
        input_output_aliases={16 + a: 2 + a for a in range(n)},
        scratch_shapes=[pltpu.VMEM((tm, WIDTH), F32)] + (_dma_sems(3 * n, 3 * n) if n else [])
                       + (_dma_sems(4 * m, 4 * m, m) if m else []),
        compiler_params=_params(("arbitrary",)),
    )(proj, proj, proj, proj, proj, proj, proj, proj, ln_g, ln_b, wsm, bias_full, pool_w, pool_scale, kv, bands,
      *gathered, *shards)
    return res[0], res[1], list(res[2:2 + n]), list(res[2 + n:])


ATT_TILE = 512


def _attn_fwd(q, qcb, k, kcb, v, vcb, bps):
    S = q.shape[0]
    tm = ATT_TILE
    nb = tm // CHUNK

    nblocks = nb * N_HEAD

    def body(q_ref, k_ref, v_ref, kh_ref, vh_ref, o_ref, l_ref, sc_s, sp_s, pc_s, pp_s):
        i = pl.program_id(0)

        def prev_kv(n, cs):
            if n == 0:
                return kh_ref[:, cs], vh_ref[:, cs]
            ps = slice((n - 1) * CHUNK, n * CHUNK)
            return k_ref[ps, cs], v_ref[ps, cs]

        pens = []
        for n in range(nb):
            rs = slice(n * CHUNK, (n + 1) * CHUNK)
            pens.append(jnp.full((N_HEAD * CHUNK, 1), jnp.where((i * nb + n) % bps != 0, 0.0, NEG), F32))
            for h in range(N_HEAD):
                cs = slice(h * HEAD, (h + 1) * HEAD)
                bs = slice((n * N_HEAD + h) * CHUNK, (n * N_HEAD + h + 1) * CHUNK)
                qh = q_ref[rs, cs]
                sc_s[bs, :] = _dot_nt(qh, k_ref[rs, cs])
                sp_s[bs, :] = _dot_nt(qh, prev_kv(n, cs)[0])
        row = lax.broadcasted_iota(jnp.int32, (nblocks * CHUNK, CHUNK), 0) & (CHUNK - 1)
        col = lax.broadcasted_iota(jnp.int32, (nblocks * CHUNK, CHUNK), 1)
        sc = jnp.where(col <= row, sc_s[...] * ATT_SCALE, NEG)
        sp = jnp.where(col >= row, sp_s[...] * ATT_SCALE, NEG) + jnp.concatenate(pens, axis=0)
        m = jnp.maximum(jnp.max(sc, axis=-1, keepdims=True), jnp.max(sp, axis=-1, keepdims=True))
        ec = jnp.exp(sc - m)
        ep = jnp.exp(sp - m)
        den = jnp.sum(ec, axis=-1, keepdims=True) + jnp.sum(ep, axis=-1, keepdims=True)
        inv = 1.0 / den
        pc_s[...] = (ec * inv).astype(BF16)
        pp_s[...] = (ep * inv).astype(BF16)
        lse = m + jnp.log(den)
        for n in range(nb):
            rs = slice(n * CHUNK, (n + 1) * CHUNK)
            for h in range(N_HEAD):
                cs = slice(h * HEAD, (h + 1) * HEAD)
                bs = slice((n * N_HEAD + h) * CHUNK, (n * N_HEAD + h + 1) * CHUNK)
                o = _dot(pc_s[bs, :], v_ref[rs, cs]) + _dot(pp_s[bs, :], prev_kv(n, cs)[1])
                o_ref[rs, cs] = o.astype(BF16)
            l_ref[rs, :] = _put_cols([lse[(n * N_HEAD + h) * CHUNK:(n * N_HEAD + h + 1) * CHUNK]
                                      for h in range(N_HEAD)])

    def halo(cb):
        return pl.BlockSpec((CHUNK, WIDTH), lambda i: (jnp.maximum(i * nb - 1, 0), cb))

    return pl.pallas_call(
        body, name=f"attn_fwd_{bps}",
        grid=(S // tm,),
        in_specs=[_rows(tm, WIDTH, qcb), _rows(tm, WIDTH, kcb), _rows(tm, WIDTH, vcb), halo(kcb), halo(vcb)],
        out_specs=[_rows(tm, WIDTH), _rows(tm, 128)],
        out_shape=[jax.ShapeDtypeStruct((S, WIDTH), BF16), jax.ShapeDtypeStruct((S, 128), F32)],
        scratch_shapes=[pltpu.VMEM((nblocks * CHUNK, CHUNK), F32), pltpu.VMEM((nblocks * CHUNK, CHUNK), F32),
                        pltpu.VMEM((nblocks * CHUNK, CHUNK), BF16), pltpu.VMEM((nblocks * CHUNK, CHUNK), BF16)],
        compiler_params=_params(("parallel",)),
    )(q, k, v, k, v)


def _gate_specs(tm):
    return [pl.BlockSpec((tm, D_MODEL), lambda i, b=b: (i, b)) for b in range(4)]


Y_SLOT = (0, 1, 3, 2)


def _merge_fwd(x, y4, o_g, l_g, proj, gates, wb, wout, shards=(), half=()):
    S = x.shape[0]
    tm = 512
    n, nh = len(shards), len(half)
    nsteps = S // tm
    forward_at = nsteps // 2

    def body(x_ref, y_ref, o0, o1, o2, l0, l1, l2, cg_ref, *rest):
        gm = rest[:4]
        wb_ref, wo_ref = rest[4:6]
        s_in = rest[6:6 + n]
        rest = rest[6 + n + nh:]
        xn_ref, yc_ref, oc_ref, lse_ref, z_ref = rest[:5]
        s_out, bufs, ocs, sems = rest[5:5 + n], rest[5:5 + n + nh], rest[5 + n + nh], rest[6 + n + nh:]
        i = pl.program_id(0)

        if n:
            @pl.when(i == 0)
            def _():
                _comm_start(_ag_first(s_in, s_out, *sems[:3]))

            @pl.when(i == forward_at)
            def _():
                incoming = _ag_first(s_in, s_out, *sems[:3])[2]
                for a in range(n):
                    for k in range(1, 4):
                        incoming[4 * a + k].wait_recv()
                _comm_start(_ag_second(bufs, *sems[3:]))

        lcols = []
        for h in range(N_HEAD):
            cs = slice(h * HEAD, (h + 1) * HEAD)
            ls = [_col(l[...], h) for l in (l0, l1, l2)]
            m = jnp.maximum(jnp.maximum(ls[0], ls[1]), ls[2])
            tot = jnp.exp(ls[0] - m) + jnp.exp(ls[1] - m) + jnp.exp(ls[2] - m)
            lse = m + jnp.log(tot)
            ocs[:, cs] = sum(jnp.exp(lg - lse) * o[:, cs].astype(F32) for lg, o in zip(ls, (o0, o1, o2)))
            lcols.append(lse)
        lse_ref[...] = _put_cols(lcols)
        oc = ocs[...]
        oc_ref[...] = oc.astype(BF16)
        yc = (oc * _silu(cg_ref[...].astype(F32))).astype(BF16)
        yc_ref[...] = yc
        ys = (y_ref[0], y_ref[1], yc, y_ref[2])
        z = jnp.zeros((tm, D_MODEL), F32)
        for b in range(4):
            z = z + _sigmoid(gm[b][...].astype(F32)) * _dot(ys[b], wb_ref[b])
        zb = z.astype(BF16)
        z_ref[...] = zb
        xn_ref[...] = x_ref[...] + _dot(zb, wo_ref[...])

        if n:
            @pl.when(i == nsteps - 1)
            def _():
                local, out, incoming = _ag_first(s_in, s_out, *sems[:3])
                for a in range(n):
                    incoming[4 * a].wait_recv()
                _comm_wait(_ag_second(bufs, *sems[3:]))
                for cp in out:
                    cp.wait_send()
                for cp in local:
                    cp.wait()

    res = pl.pallas_call(
        body, name="merge_fwd_gather" if n else "merge_fwd",
        grid=(nsteps,),
        in_specs=[_rows(tm, D_MODEL), pl.BlockSpec((3, tm, WIDTH), lambda i: (0, i, 0)),
                  _rows(tm, WIDTH), _rows(tm, WIDTH), _rows(tm, WIDTH),
                  _rows(tm, 128), _rows(tm, 128), _rows(tm, 128),
                  _rows(tm, WIDTH, CB_CGATE)] + _gate_specs(tm)
                 + [_resident((4, WIDTH, D_MODEL)), _resident((D_MODEL, D_MODEL))] + [ANY] * (n + nh),
        out_specs=[_rows(tm, D_MODEL), pl.BlockSpec((None, tm, WIDTH), lambda i: (Y_SLOT[2], i, 0)),
                   _rows(tm, WIDTH), _rows(tm, 128), _rows(tm, D_MODEL)] + [ANY] * (n + nh),
        out_shape=[jax.ShapeDtypeStruct((S, D_MODEL), F32), jax.ShapeDtypeStruct(y4.shape, BF16),
                   jax.ShapeDtypeStruct((S, WIDTH), BF16), jax.ShapeDtypeStruct((S, 128), F32),
                   jax.ShapeDtypeStruct((S, D_MODEL), BF16)]
                  + [jax.ShapeDtypeStruct((N_DEV,) + s.shape, s.dtype) for s in shards]
                  + [jax.ShapeDtypeStruct(b.shape, b.dtype) for b in half],
        input_output_aliases={1: 1, **{15 + n + a: 5 + n + a for a in range(nh)}},
        scratch_shapes=[pltpu.VMEM((tm, WIDTH), F32)]
                       + (_dma_sems(4 * n, 4 * n, n, 3 * (n + nh), 3 * (n + nh)) if n else []),
        compiler_params=_params(("arbitrary",)),
    )(x, y4, *o_g, *l_g, proj, *([gates] * 4), wb, wout, *shards, *half)
    return res[:5], list(res[5:])


def _loss_head(x, g, tgt):
    S = x.shape[0]
    tm = 512

    def body(x_ref, g_ref, t_ref, loss_ref, dx_ref, dg_ref):
        @pl.when(pl.program_id(0) == 0)
        def _():
            loss_ref[...] = jnp.zeros_like(loss_ref)
            dg_ref[...] = jnp.zeros_like(dg_ref)

        xf = x_ref[...]
        r = lax.rsqrt(jnp.mean(xf * xf, axis=-1, keepdims=True) + EPS)
        xhat = xf * r
        gv = g_ref[...]
        err = xhat * gv - t_ref[...]
        e2 = jnp.sum(err * err, axis=-1, keepdims=True)
        loss_ref[...] += (0.5 / D_MODEL) * jnp.sum(e2, axis=0, keepdims=True)
        dy = err * (1.0 / D_MODEL)
        dg_ref[...] += jnp.sum(dy * xhat, axis=0, keepdims=True)
        dxh = dy * gv
        dx_ref[...] = r * (dxh - xhat * jnp.mean(dxh * xhat, axis=-1, keepdims=True))

    return pl.pallas_call(
        body, name="loss_head",
        grid=(S // tm,),
        in_specs=[_rows(tm, D_MODEL), _full((1, D_MODEL)), _rows(tm, D_MODEL)],
        out_specs=[_full((1, 128)), _rows(tm, D_MODEL), _full((1, D_MODEL))],
        out_shape=[jax.ShapeDtypeStruct((1, 128), F32), jax.ShapeDtypeStruct((S, D_MODEL), F32),
                   jax.ShapeDtypeStruct((1, D_MODEL), F32)],
        compiler_params=_params(("arbitrary",)),
    )(x, g, tgt)


def _merge_bwd(dxo, y4, oc, z, proj, gates, wb, wout, grads=()):
    S = dxo.shape[0]
    tm = 256
    n = len(grads)
    nsteps = S // tm

    def body(dx_ref, y_ref, oc_ref, z_ref, cg_ref, *rest):
        gm = rest[:4]
        wb_ref, wo_ref = rest[4:6]
        g_in = rest[6:6 + n]
        dy_ref, doc_ref, delta_ref, dcg_ref, dgm_ref, dwb_ref, dwo_ref = rest[6 + n:13 + n]
        g_out = rest[13 + n:13 + 2 * n]
        acc_b, acc_o = rest[13 + 2 * n:15 + 2 * n]
        sems = rest[15 + 2 * n:]
        i = pl.program_id(0)

        @pl.when(i == 0)
        def _():
            acc_b[...] = jnp.zeros_like(acc_b)
            acc_o[...] = jnp.zeros_like(acc_o)
            if n:
                _comm_start(_rs_first(g_in, g_out, *sems))

        dxb = dx_ref[...].astype(BF16)
        acc_o[...] += _dot_tn(z_ref[...], dxb)
        dz = _dot_nt(dxb, wo_ref[...])
        for b in range(4):
            gate = _sigmoid(gm[b][...].astype(F32))
            yb = y_ref[Y_SLOT[b]]
            t = _dot(yb, wb_ref[b])
            dgm_ref[:, b * D_MODEL:(b + 1) * D_MODEL] = (dz * t * gate * (1.0 - gate)).astype(BF16)
            dt = (dz * gate).astype(BF16)
            acc_b[b] += _dot_tn(yb, dt)
            dyb = _dot_nt(dt, wb_ref[b])
            if b == 2:
                cg = cg_ref[...].astype(F32)
                oc = oc_ref[...].astype(F32)
                scg, dscg = _silu_and_grad(cg)
                doc = dyb * scg
                dcg_ref[...] = (dyb * oc * dscg).astype(BF16)
                doc_ref[...] = doc.astype(BF16)
                prod = doc * oc
                delta_ref[...] = _put_cols([jnp.sum(prod[:, h * HEAD:(h + 1) * HEAD], axis=1, keepdims=True)
                                            for h in range(N_HEAD)])
            else:
                dy_ref[b if b < 2 else 2] = dyb.astype(BF16)

        @pl.when(i == nsteps - 1)
        def _():
            dwb_ref[...] = acc_b[...].astype(BF16)
            dwo_ref[...] = acc_o[...].astype(BF16)
            if n:
                _comm_wait(_rs_first(g_in, g_out, *sems))

    res = pl.pallas_call(
        body, name="merge_bwd_scatter" if n else "merge_bwd",
        grid=(nsteps,),
        in_specs=[_rows(tm, D_MODEL), pl.BlockSpec((4, tm, WIDTH), lambda i: (0, i, 0)),
                  _rows(tm, WIDTH), _rows(tm, D_MODEL), _rows(tm, WIDTH, CB_CGATE)] + _gate_specs(tm)
                 + [_resident((4, WIDTH, D_MODEL)), _resident((D_MODEL, D_MODEL))] + [ANY] * n,
        out_specs=[pl.BlockSpec((3, tm, WIDTH), lambda i: (0, i, 0)), _rows(tm, WIDTH), _rows(tm, 128),
                   _rows(tm, WIDTH), _rows(tm, 4 * D_MODEL), _full((4, WIDTH, D_MODEL)), _full((D_MODEL, D_MODEL))]
                  + [ANY] * n,
        out_shape=[jax.ShapeDtypeStruct((3, S, WIDTH), BF16), jax.ShapeDtypeStruct((S, WIDTH), BF16),
                   jax.ShapeDtypeStruct((S, 128), F32), jax.ShapeDtypeStruct((S, WIDTH), BF16),
                   jax.ShapeDtypeStruct((S, 4 * D_MODEL), BF16), jax.ShapeDtypeStruct((4, WIDTH, D_MODEL), BF16),
                   jax.ShapeDtypeStruct((D_MODEL, D_MODEL), BF16)]
                  + [jax.ShapeDtypeStruct(g.shape[:1] + g.shape[2:], g.dtype) for g in grads],
        scratch_shapes=[pltpu.VMEM((4, WIDTH, D_MODEL), F32), pltpu.VMEM((D_MODEL, D_MODEL), F32)]
                       + (_dma_sems(N_CHIP * n, N_CHIP * n) if n else []),
        compiler_params=_params(("arbitrary",)),
    )(dxo, y4, oc, z, proj, *([gates] * 4), wb, wout, *grads)
    return res[:7], list(res[7:])


def _attn_bwd(q, qcb, k, kcb, v, vcb, do, lse, delta, bps, dst, dcb):
    S = q.shape[0]
    tm = ATT_TILE
    nb = tm // CHUNK
    nblk = S // CHUNK

    ncur = nb * N_HEAD
    nprev = (nb + 1) * N_HEAD

    def body(q_ref, k_ref, v_ref, do_ref, l_ref, d_ref, kh_ref, vh_ref, qn_ref, don_ref, ln_ref, dn_ref, dst_ref,
             dq_ref, dk_ref, dv_ref, sc_s, sp_s, dpc_s, dpp_s, pc_s, pp_s, dsc_s, dsp_s):
        i = pl.program_id(0)

        def rows_of(n):
            if n < nb:
                rs = slice(n * CHUNK, (n + 1) * CHUNK)
                return rs, q_ref, do_ref, l_ref, d_ref
            return slice(0, CHUNK), qn_ref, don_ref, ln_ref, dn_ref

        def prev_kv(n, cs):
            if n == 0:
                return kh_ref[:, cs], vh_ref[:, cs]
            ps = slice((n - 1) * CHUNK, n * CHUNK)
            return k_ref[ps, cs], v_ref[ps, cs]

        def blk(n, h):
            return slice((n * N_HEAD + h) * CHUNK, (n * N_HEAD + h + 1) * CHUNK)

        pens, lses, deltas = [], [], []
        for n in range(nb + 1):
            rs, qr, dor, lr, dr = rows_of(n)
            gb = i * nb + n
            pen = jnp.where(gb % bps != 0, 0.0, NEG)
            if n == nb:
                pen = pen + jnp.where(gb < nblk, 0.0, NEG)
            pens.append(jnp.full((N_HEAD * CHUNK, 1), pen, F32))
            lblk, dblk = lr[rs, :], dr[rs, :]
            for h in range(N_HEAD):
                cs = slice(h * HEAD, (h + 1) * HEAD)
                qh, doh = qr[rs, cs], dor[rs, cs]
                lses.append(_col(lblk, h))
                deltas.append(_col(dblk, h))
                kp, vp = prev_kv(n, cs)
                sp_s[blk(n, h), :] = _dot_nt(qh, kp)
                dpp_s[blk(n, h), :] = _dot_nt(doh, vp)
                if n < nb:
                    sc_s[blk(n, h), :] = _dot_nt(qh, k_ref[rs, cs])
                    dpc_s[blk(n, h), :] = _dot_nt(doh, v_ref[rs, cs])
        lse = jnp.concatenate(lses, axis=0)
        delta = jnp.concatenate(deltas, axis=0)
        row = lax.broadcasted_iota(jnp.int32, (nprev * CHUNK, CHUNK), 0) & (CHUNK - 1)
        col = lax.broadcasted_iota(jnp.int32, (nprev * CHUNK, CHUNK), 1)
        sp = jnp.where(col >= row, sp_s[...] * ATT_SCALE, NEG) + jnp.concatenate(pens, axis=0)
        pp = jnp.exp(sp - lse)
        pp_s[...] = pp.astype(BF16)
        dsp_s[...] = (pp * (dpp_s[...] - delta)).astype(BF16)
        nc = ncur * CHUNK
        sc = jnp.where(col[:nc] <= row[:nc], sc_s[...] * ATT_SCALE, NEG)
        pc = jnp.exp(sc - lse[:nc])
        pc_s[...] = pc.astype(BF16)
        dsc_s[...] = (pc * (dpc_s[...] - delta[:nc])).astype(BF16)
        for n in range(nb):
            rs, qr, dor, _, _ = rows_of(n)
            rn, qnr, donr, _, _ = rows_of(n + 1)
            for h in range(N_HEAD):
                cs = slice(h * HEAD, (h + 1) * HEAD)
                kp, _ = prev_kv(n, cs)
                dq = _dot(dsc_s[blk(n, h), :], k_ref[rs, cs]) + _dot(dsp_s[blk(n, h), :], kp)
                dq_ref[rs, cs] = (dq * ATT_SCALE).astype(BF16)
                dk = _dot_tn(dsc_s[blk(n, h), :], qr[rs, cs]) + _dot_tn(dsp_s[blk(n + 1, h), :], qnr[rn, cs])
                dk_ref[rs, cs] = (dk * ATT_SCALE).astype(BF16)
                dv = _dot_tn(pc_s[blk(n, h), :], dor[rs, cs]) + _dot_tn(pp_s[blk(n + 1, h), :], donr[rn, cs])
                dv_ref[rs, cs] = dv.astype(BF16)

    def prev_halo(cb):
        return pl.BlockSpec((CHUNK, WIDTH), lambda i: (jnp.maximum(i * nb - 1, 0), cb))

    def next_halo(width, cb=0):
        return pl.BlockSpec((CHUNK, width), lambda i: (jnp.minimum(i * nb + nb, nblk - 1), cb))

    return pl.pallas_call(
        body, name=f"attn_bwd_{bps}",
        grid=(S // tm,),
        in_specs=[_rows(tm, WIDTH, qcb), _rows(tm, WIDTH, kcb), _rows(tm, WIDTH, vcb), _rows(tm, WIDTH),
                  _rows(tm, 128), _rows(tm, 128), prev_halo(kcb), prev_halo(vcb),
                  next_halo(WIDTH, qcb), next_halo(WIDTH), next_halo(128), next_halo(128), ANY],
        out_specs=[_rows(tm, WIDTH, dcb), _rows(tm, WIDTH), _rows(tm, WIDTH)],
        out_shape=[jax.ShapeDtypeStruct(dst.shape, BF16)] + [jax.ShapeDtypeStruct((S, WIDTH), BF16)] * 2,
        input_output_aliases={12: 0},
        scratch_shapes=[pltpu.VMEM((ncur * CHUNK, CHUNK), F32), pltpu.VMEM((nprev * CHUNK, CHUNK), F32),
                        pltpu.VMEM((ncur * CHUNK, CHUNK), F32), pltpu.VMEM((nprev * CHUNK, CHUNK), F32),
                        pltpu.VMEM((ncur * CHUNK, CHUNK), BF16), pltpu.VMEM((nprev * CHUNK, CHUNK), BF16),
                        pltpu.VMEM((ncur * CHUNK, CHUNK), BF16), pltpu.VMEM((nprev * CHUNK, CHUNK), BF16)],
        compiler_params=_params(("parallel",)),
    )(q, k, v, do, lse, delta, k, v, q, do, lse, delta, dst)


def _dilated_split(d):
    hp = min(N_HEAD, 16 // d)
    return hp, N_HEAD // hp, HEAD * hp


def _strided_regroup(d):
    return d < 16


def _by_class(src_ref, dst, d, hp, nat):
    for j in range(hp):
        if _strided_regroup(d):
            nat[j] = src_ref[:, j * HEAD:(j + 1) * HEAD].astype(F32)
            for r in range(d):
                dst[j, r * CHUNK:(r + 1) * CHUNK, :] = nat.at[j][pl.ds(r, CHUNK, stride=d), :].astype(BF16)
        else:
            dst[j] = pltpu.einshape("(tr)l->(rt)l", src_ref[:, j * HEAD:(j + 1) * HEAD], r=d)


def _from_class(src, dst_ref, d, hp, nat, add_ref=None):
    for j in range(hp):
        cs = slice(j * HEAD, (j + 1) * HEAD)
        if _strided_regroup(d):
            for r in range(d):
                nat.at[j][pl.ds(r, CHUNK, stride=d), :] = src[j, r * CHUNK:(r + 1) * CHUNK, :]
            val = nat[j].astype(BF16)
        else:
            val = pltpu.einshape("(rt)l->(tr)l", src[j].astype(BF16), r=d)
        if add_ref is not None:
            val = (val.astype(F32) + add_ref[:, cs].astype(F32)).astype(BF16)
        dst_ref[:, cs] = val


def _attn_fwd_dilated(proj, qcb, kcb, vcb, d):
    S = proj.shape[0]
    T = CHUNK * d
    hp, nh, cw = _dilated_split(d)
    nblocks = d * hp

    def body(q_ref, k_ref, v_ref, o_ref, l_ref, qf, kst, vst, of, lf, nat, sc_s, sp_s, pc_s, pp_s):
        i, hh = pl.program_id(0), pl.program_id(1)
        kf, vf = kst.at[i % 2, hh], vst.at[i % 2, hh]
        kpf, vpf = kst.at[1 - i % 2, hh], vst.at[1 - i % 2, hh]

        @pl.when(i == 0)
        def _():
            kpf[...] = jnp.zeros_like(kpf)
            vpf[...] = jnp.zeros_like(vpf)

        _by_class(q_ref, qf, d, hp, nat)
        _by_class(k_ref, kf, d, hp, nat)
        _by_class(v_ref, vf, d, hp, nat)

        def blk(ref, r, j):
            return ref[j, r * CHUNK:(r + 1) * CHUNK, :]

        def bs(r, j):
            return slice((r * hp + j) * CHUNK, (r * hp + j + 1) * CHUNK)

        for r in range(d):
            for j in range(hp):
                qb = blk(qf, r, j)
                sc_s[bs(r, j), :] = _dot_nt(qb, blk(kf, r, j))
                sp_s[bs(r, j), :] = _dot_nt(qb, blk(kpf, r, j))
        row = lax.broadcasted_iota(jnp.int32, (nblocks * CHUNK, CHUNK), 0) & (CHUNK - 1)
        col = lax.broadcasted_iota(jnp.int32, (nblocks * CHUNK, CHUNK), 1)
        sc = jnp.where(col <= row, sc_s[...] * ATT_SCALE, NEG)
        sp = jnp.where(col >= row, sp_s[...] * ATT_SCALE, NEG) + jnp.where(i > 0, 0.0, NEG)
        m = jnp.maximum(jnp.max(sc, axis=-1, keepdims=True), jnp.max(sp, axis=-1, keepdims=True))
        ec = jnp.exp(sc - m)
        ep = jnp.exp(sp - m)
        den = jnp.sum(ec, axis=-1, keepdims=True) + jnp.sum(ep, axis=-1, keepdims=True)
        inv = 1.0 / den
        pc_s[...] = (ec * inv).astype(BF16)
        pp_s[...] = (ep * inv).astype(BF16)
        lse = m + jnp.log(den)
        lane = lax.broadcasted_iota(jnp.int32, (CHUNK, 128), 1)
        for r in range(d):
            lblk = jnp.zeros((CHUNK, 128), F32)
            for j in range(hp):
                o = _dot(pc_s[bs(r, j), :], blk(vf, r, j)) + _dot(pp_s[bs(r, j), :], blk(vpf, r, j))
                of[j, r * CHUNK:(r + 1) * CHUNK, :] = o
                lblk = jnp.where(lane == hh * hp + j, lse[bs(r, j)], lblk)
            lf[r * CHUNK:(r + 1) * CHUNK, :] = lblk
        _from_class(of, o_ref, d, hp, nat)
        lnat = pltpu.einshape("(rt)l->(tr)l", lf[...], r=d)

        @pl.when(hh == 0)
        def _():
            l_ref[...] = lnat

        @pl.when(hh > 0)
        def _():
            l_ref[...] += lnat

    def cols(cb):
        return pl.BlockSpec((T, cw), lambda i, hh: (i, cb * nh + hh))

    tile = pltpu.VMEM((hp, T, HEAD), BF16)
    staging = pltpu.VMEM((hp, T, HEAD) if _strided_regroup(d) else (1, 8, HEAD), F32)
    return pl.pallas_call(
        body, name=f"attn_fwd_dilated_{d}",
        grid=(S // T, nh),
        in_specs=[cols(qcb), cols(kcb), cols(vcb)],
        out_specs=[cols(0), pl.BlockSpec((T, 128), lambda i, hh: (i, 0))],
        out_shape=[jax.ShapeDtypeStruct((S, WIDTH), BF16), jax.ShapeDtypeStruct((S, 128), F32)],
        scratch_shapes=[tile, pltpu.VMEM((2, nh, hp, T, HEAD), BF16), pltpu.VMEM((2, nh, hp, T, HEAD), BF16),
                        pltpu.VMEM((hp, T, HEAD), F32), pltpu.VMEM((T, 128), F32), staging,
                        pltpu.VMEM((nblocks * CHUNK, CHUNK), F32), pltpu.VMEM((nblocks * CHUNK, CHUNK), F32),
                        pltpu.VMEM((nblocks * CHUNK, CHUNK), BF16), pltpu.VMEM((nblocks * CHUNK, CHUNK), BF16)],
        compiler_params=_params(("arbitrary", "arbitrary")),
    )(proj, proj, proj)


def _attn_bwd_dilated(proj, qcb, kcb, vcb, do, lse, delta, d, dk_in, dv_in, dst, dcb):
    S = proj.shape[0]
    T = CHUNK * d
    nt = S // T
    hp, nh, cw = _dilated_split(d)
    nblocks = d * hp

    def body(q_ref, k_ref, v_ref, do_ref, l_ref, d_ref, dki_ref, dvi_ref, dst_ref, dq_ref, dk_ref, dv_ref,
             qf, dof, kbuf, vbuf, dqf, gk, gv, nat,
             sc_s, sp_s, dpc_s, dpp_s, pc_s, pp_s, dsc_s, dsp_s):
        hh, i = pl.program_id(0), pl.program_id(1)
        kf, vf, newk, newv = kbuf.at[i % 2], vbuf.at[i % 2], gk.at[i % 2], gv.at[i % 2]
        kpf, vpf, acck, accv = kbuf.at[1 - i % 2], vbuf.at[1 - i % 2], gk.at[1 - i % 2], gv.at[1 - i % 2]

        @pl.when(i == 0)
        def _():
            for ref in (kbuf, vbuf, gk, gv):
                ref[...] = jnp.zeros_like(ref)
            dk_ref[...] = jnp.zeros_like(dk_ref)
            dv_ref[...] = jnp.zeros_like(dv_ref)

        def blk(ref, r, j):
            return ref[j, r * CHUNK:(r + 1) * CHUNK, :]

        def bs(r, j):
            return slice((r * hp + j) * CHUNK, (r * hp + j + 1) * CHUNK)

        @pl.when(i < nt)
        def _():
            _by_class(q_ref, qf, d, hp, nat)
            _by_class(do_ref, dof, d, hp, nat)
            _by_class(k_ref, kf, d, hp, nat)
            _by_class(v_ref, vf, d, hp, nat)
            lses, deltas = [], []
            lcls = pltpu.einshape("(tr)l->(rt)l", l_ref[...], r=d)
            dcls = pltpu.einshape("(tr)l->(rt)l", d_ref[...], r=d)
            for r in range(d):
                lblk = lcls[r * CHUNK:(r + 1) * CHUNK]
                dblk = dcls[r * CHUNK:(r + 1) * CHUNK]
                for j in range(hp):
                    lses.append(_col(lblk, hh * hp + j))
                    deltas.append(_col(dblk, hh * hp + j))
                    qb, dob = blk(qf, r, j), blk(dof, r, j)
                    sc_s[bs(r, j), :] = _dot_nt(qb, blk(kf, r, j))
                    dpc_s[bs(r, j), :] = _dot_nt(dob, blk(vf, r, j))
                    sp_s[bs(r, j), :] = _dot_nt(qb, blk(kpf, r, j))
                    dpp_s[bs(r, j), :] = _dot_nt(dob, blk(vpf, r, j))
            lse = jnp.concatenate(lses, axis=0)
            delta = jnp.concatenate(deltas, axis=0)
            row = lax.broadcasted_iota(jnp.int32, (nblocks * CHUNK, CHUNK), 0) & (CHUNK - 1)
            col = lax.broadcasted_iota(jnp.int32, (nblocks * CHUNK, CHUNK), 1)
            sp = jnp.where(col >= row, sp_s[...] * ATT_SCALE, NEG) + jnp.where(i > 0, 0.0, NEG)
            pp = jnp.exp(sp - lse)
            pp_s[...] = pp.astype(BF16)
            dsp_s[...] = (pp * (dpp_s[...] - delta)).astype(BF16)
            sc = jnp.where(col <= row, sc_s[...] * ATT_SCALE, NEG)
            pc = jnp.exp(sc - lse)
            pc_s[...] = pc.astype(BF16)
            dsc_s[...] = (pc * (dpc_s[...] - delta)).astype(BF16)
            for r in range(d):
                rows = slice(r * CHUNK, (r + 1) * CHUNK)
                for j in range(hp):
                    qb, dob = blk(qf, r, j), blk(dof, r, j)
                    dsc, dsp = dsc_s[bs(r, j), :], dsp_s[bs(r, j), :]
                    dqf[j, rows, :] = (_dot(dsc, blk(kf, r, j)) + _dot(dsp, blk(kpf, r, j))) * ATT_SCALE
                    newk[j, rows, :] = _dot_tn(dsc, qb) * ATT_SCALE
                    newv[j, rows, :] = _dot_tn(pc_s[bs(r, j), :], dob)
                    acck[j, rows, :] += _dot_tn(dsp, qb) * ATT_SCALE
                    accv[j, rows, :] += _dot_tn(pp_s[bs(r, j), :], dob)
            _from_class(dqf, dq_ref, d, hp, nat)

        @pl.when(i > 0)
        def _():
            _from_class(acck, dk_ref, d, hp, nat, dki_ref)
            _from_class(accv, dv_ref, d, hp, nat, dvi_ref)

    def cur(width, cb, nsplit):
        return pl.BlockSpec((T, width), lambda hh, i: (jnp.minimum(i, nt - 1), cb * nsplit + hh * (nsplit > 1)))

    def lag():
        return pl.BlockSpec((T, cw), lambda hh, i: (jnp.maximum(i - 1, 0), hh))

    tile = pltpu.VMEM((hp, T, HEAD), BF16)
    acc = pltpu.VMEM((hp, T, HEAD), F32)
    f32s = pltpu.VMEM((nblocks * CHUNK, CHUNK), F32)
    b16s = pltpu.VMEM((nblocks * CHUNK, CHUNK), BF16)
    return pl.pallas_call(
        body, name=f"attn_bwd_dilated_{d}",
        grid=(nh, nt + 1),
        in_specs=[cur(cw, qcb, nh), cur(cw, kcb, nh), cur(cw, vcb, nh), cur(cw, 0, nh), cur(128, 0, 1), cur(128, 0, 1),
                  lag(), lag(), ANY],
        out_specs=[cur(cw, dcb, nh), lag(), lag()],
        out_shape=[jax.ShapeDtypeStruct(dst.shape, BF16)] + [jax.ShapeDtypeStruct((S, WIDTH), BF16)] * 2,
        input_output_aliases={8: 0},
        scratch_shapes=[tile, tile, pltpu.VMEM((2, hp, T, HEAD), BF16), pltpu.VMEM((2, hp, T, HEAD), BF16), acc,
                        pltpu.VMEM((2, hp, T, HEAD), F32), pltpu.VMEM((2, hp, T, HEAD), F32),
                        acc if _strided_regroup(d) else pltpu.VMEM((1, 8, HEAD), F32)]
                       + [f32s] * 4 + [b16s] * 4,
        compiler_params=_params(("arbitrary", "arbitrary")),
    )(proj, proj, proj, do, lse, delta, dk_in, dv_in, dst)


def _abm_bwd(proj, cdf, dy3, ln_g, ln_b, wsm, wsm_t, bias_full, pool_w, pool_wt, pool_scale, kv, bands, bands_t):
    S = proj.shape[0]
    tm = 512
    nchunk = tm // CHUNK
    nblk = S // CHUNK

    def body(u_ref, v_ref, ag_ref, p_ref, ph_ref, pg_ref, pgn_ref, mq_ref, mg_ref, cdf_ref, dy_ref, dypn_ref,
             lng_ref, lnb_ref, wsm_ref, wsmt_ref, bias_ref, pw_ref, pwt_ref, ps_ref, kv_ref, band_ref, bandt_ref,
             dab_ref, dm_ref, dlng_ref, dlnb_ref, dws_ref, dbias_ref, dpw_ref, dps_ref, dkv_ref,
             mix, dvl, ddn):
        i = pl.program_id(0)

        @pl.when(i == 0)
        def _():
            for r in (dlng_ref, dlnb_ref, dws_ref, dbias_ref, dpw_ref, dps_ref, dkv_ref):
                r[...] = jnp.zeros_like(r)

        au = u_ref[...].astype(F32)
        av = v_ref[...].astype(F32)
        ag = ag_ref[...].astype(F32)
        u, du = _gelu_and_grad(au, cdf_ref[0].astype(F32))
        v, dgelu_v = _gelu_and_grad(av, cdf_ref[1].astype(F32))
        vhat, rstd = _layer_norm_fwd(v)
        vln = (vhat * lng_ref[...] + lnb_ref[...]).astype(BF16)
        for c in range(nchunk):
            for h in range(N_HEAD):
                rs, cs = slice(c * CHUNK, (c + 1) * CHUNK), slice(h * HEAD, (h + 1) * HEAD)
                mix[rs, cs] = _dot(wsm_ref[h], vln[rs, cs]) + bias_ref[:, cs]
        dya = dy_ref[0].astype(F32)
        sg, dsg = _silu_and_grad(ag)
        mixed = mix[...]
        dab_ref[:, 2 * WIDTH:3 * WIDTH] = (dya * u * mixed * dsg).astype(BF16)
        dab_ref[:, 0:WIDTH] = (dya * mixed * sg * du).astype(BF16)
        dmixed = dya * u * sg
        dmb = dmixed.astype(BF16)
        tril = (lax.broadcasted_iota(jnp.int32, (CHUNK, CHUNK), 1)
                <= lax.broadcasted_iota(jnp.int32, (CHUNK, CHUNK), 0))
        for c in range(nchunk):
            rs = slice(c * CHUNK, (c + 1) * CHUNK)
            dbias_ref[...] += dmixed[rs, :]
            for h in range(N_HEAD):
                cs = slice(h * HEAD, (h + 1) * HEAD)
                dvl[rs, cs] = _dot(wsmt_ref[h], dmb[rs, cs])
                dws_ref[h] += jnp.where(tril, _dot_nt(dmb[rs, cs], vln[rs, cs]), 0.0)
        dvln = dvl[...]
        dlng_ref[...] += jnp.sum(dvln * vhat, axis=0, keepdims=True)
        dlnb_ref[...] += jnp.sum(dvln, axis=0, keepdims=True)
        dvh = dvln * lng_ref[...]
        dv = rstd * (dvh - jnp.mean(dvh, axis=-1, keepdims=True)
                     - vhat * jnp.mean(dvh * vhat, axis=-1, keepdims=True))
        dab_ref[:, WIDTH:2 * WIDTH] = (dv * dgelu_v).astype(BF16)

        halo_ok = (i > 0).astype(F32)
        for c in range(nchunk):
            rs = slice(c * CHUNK, (c + 1) * CHUNK)
            for g, win in enumerate(POOL_WINDOWS):
                cs = slice(g * HEAD, (g + 1) * HEAD)
                cur = p_ref[rs, cs]
                if c == 0:
                    prev = (ph_ref[:, cs].astype(F32) * halo_ok).astype(BF16)
                else:
                    prev = p_ref[(c - 1) * CHUNK:c * CHUNK, cs]
                sums = _dot(band_ref[g, 0], cur) + _dot(band_ref[g, 1], prev)
                dvl[rs, cs] = sums * _inv_count(i * tm + c * CHUNK, win) - cur.astype(F32)
        dmat = dvl[...].astype(BF16)
        for g in range(4):
            cs = slice(g * HEAD, (g + 1) * HEAD)
            mix[:, cs] = _dot(dmat[:, cs], pw_ref[g])
        yg = mix[...]
        pg = pg_ref[...].astype(F32)
        dyp = dy_ref[1].astype(F32)
        spg, dspg = _silu_and_grad(pg)
        dyy = dyp * spg
        scale = ps_ref[...]
        dab_ref[:, 4 * WIDTH:5 * WIDTH] = (dyp * yg * scale * dspg).astype(BF16)
        dps_ref[...] += jnp.sum(dyy * yg, axis=0, keepdims=True)
        dyg = (dyy * scale).astype(BF16)
        for g in range(4):
            cs = slice(g * HEAD, (g + 1) * HEAD)
            dpw_ref[g] += _dot_tn(dmat[:, cs], dyg[:, cs])
            mix[:, cs] = _dot(dyg[:, cs], pwt_ref[g])
        next_ok = (i + 1 < S // tm).astype(F32)
        dygn = (dypn_ref[...].astype(F32) * _silu(pgn_ref[...].astype(F32)) * scale * next_ok).astype(BF16)
        for c in range(nchunk + 1):
            for g, win in enumerate(POOL_WINDOWS):
                cs = slice(g * HEAD, (g + 1) * HEAD)
                if c < nchunk:
                    dd = mix[c * CHUNK:(c + 1) * CHUNK, cs]
                else:
                    dd = _dot(dygn[:, cs], pwt_ref[g])
                ddn[c * CHUNK:(c + 1) * CHUNK, cs] = dd * _inv_count(i * tm + c * CHUNK, win)
        ddnb = ddn[...].astype(BF16)
        for c in range(nchunk):
            rs = slice(c * CHUNK, (c + 1) * CHUNK)
            ns = slice((c + 1) * CHUNK, (c + 2) * CHUNK)
            for g, win in enumerate(POOL_WINDOWS):
                cs = slice(g * HEAD, (g + 1) * HEAD)
                dp = _dot(bandt_ref[g, 0], ddnb[rs, cs]) + _dot(bandt_ref[g, 1], ddnb[ns, cs]) - mix[rs, cs]
                dab_ref[rs, 3 * WIDTH + g * HEAD:3 * WIDTH + (g + 1) * HEAD] = dp.astype(BF16)

        mg = mg_ref[...].astype(F32)
        dym = dy_ref[2].astype(F32)
        smg, dsmg = _silu_and_grad(mg)
        dob = (dym * smg).astype(BF16)
        for h in range(N_HEAD):
            cs = slice(h * HEAD, (h + 1) * HEAD)
            vs = slice(WIDTH + h * HEAD, WIDTH + (h + 1) * HEAD)
            qh = mq_ref[:, cs]
            p = _mem_softmax(qh, kv_ref[:, cs])
            pb = p.astype(BF16)
            mix[:, cs] = _dot(pb, kv_ref[:, vs])
            dp = _dot_nt(dob[:, cs], kv_ref[:, vs])
            ds = (p * (dp - jnp.sum(p * dp, axis=-1, keepdims=True))).astype(BF16)
            dm_ref[:, cs] = (_dot(ds, kv_ref[:, cs]) * ATT_SCALE).astype(BF16)
            dkv_ref[:, cs] += _dot_tn(ds, qh) * ATT_SCALE
            dkv_ref[:, vs] += _dot_tn(pb, dob[:, cs])
        dm_ref[:, WIDTH:2 * WIDTH] = (dym * mix[...] * dsmg).astype(BF16)

    blk = tm // CHUNK
    small = [_full((1, WIDTH)), _full((1, WIDTH)), _full((N_HEAD, CHUNK, CHUNK)), _full((CHUNK, WIDTH)),
             _full((4, HEAD, HEAD)), _full((1, WIDTH)), _full((MEM_LEN, 2 * WIDTH))]
    return pl.pallas_call(
        body, name="abm_bwd",
        grid=(S // tm,),
        in_specs=[_rows(tm, WIDTH, CB_U), _rows(tm, WIDTH, CB_V), _rows(tm, WIDTH, CB_AGATE),
                  _rows(tm, WIDTH, CB_PIN),
                  pl.BlockSpec((CHUNK, WIDTH), lambda i: (jnp.maximum(i * blk - 1, 0), CB_PIN)),
                  _rows(tm, WIDTH, CB_PGATE),
                  pl.BlockSpec((CHUNK, WIDTH), lambda i: (jnp.minimum(i * blk + blk, nblk - 1), CB_PGATE)),
                  _rows(tm, WIDTH, CB_MQ), _rows(tm, WIDTH, CB_MGATE),
                  pl.BlockSpec((2, tm, WIDTH), lambda i: (0, i, 0)),
                  pl.BlockSpec((3, tm, WIDTH), lambda i: (0, i, 0)),
                  pl.BlockSpec((None, CHUNK, WIDTH), lambda i: (1, jnp.minimum(i * blk + blk, nblk - 1), 0)),
                  _full((1, WIDTH)), _full((1, WIDTH)), _full((N_HEAD, CHUNK, CHUNK)), _full((N_HEAD, CHUNK, CHUNK)),
                  _full((CHUNK, WIDTH)), _full((4, HEAD, HEAD)), _full((4, HEAD, HEAD)), _full((1, WIDTH)),
                  _full((MEM_LEN, 2 * WIDTH)), _full((4, 2, CHUNK, CHUNK)), _full((4, 2, CHUNK, CHUNK))],
        out_specs=[_rows(tm, 5 * WIDTH), _rows(tm, 2 * WIDTH)] + small,
        out_shape=[jax.ShapeDtypeStruct((S, D_BRANCHES), BF16), jax.ShapeDtypeStruct((S, 2 * WIDTH), BF16),
                   jax.ShapeDtypeStruct((1, WIDTH), F32), jax.ShapeDtypeStruct((1, WIDTH), F32),
                   jax.ShapeDtypeStruct((N_HEAD, CHUNK, CHUNK), F32), jax.ShapeDtypeStruct((CHUNK, WIDTH), F32),
                   jax.ShapeDtypeStruct((4, HEAD, HEAD), F32), jax.ShapeDtypeStruct((1, WIDTH), F32),
                   jax.ShapeDtypeStruct((MEM_LEN, 2 * WIDTH), F32)],
        scratch_shapes=[pltpu.VMEM((tm, WIDTH), F32), pltpu.VMEM((tm, WIDTH), F32),
                        pltpu.VMEM((tm + CHUNK, WIDTH), F32)],
        compiler_params=_params(("arbitrary",)),
    )(proj, proj, proj, proj, proj, proj, proj, proj, proj, cdf, dy3, dy3,
      ln_g, ln_b, wsm, wsm_t, bias_full, pool_w, pool_wt, pool_scale, kv, bands, bands_t)


def _bias_reduce(dbias_full):
    def body(d_ref, o_ref):
        d = d_ref[...]
        o_ref[...] = _put_cols([jnp.sum(d[:, h * HEAD:(h + 1) * HEAD], axis=1, keepdims=True) for h in range(N_HEAD)])

    return pl.pallas_call(body, name="bias_reduce", out_shape=jax.ShapeDtypeStruct((CHUNK, 128), F32))(dbias_full)


def _mem_bwd(mem, g, mem_n, w, dkv):
    def body(m_ref, g_ref, mn_ref, w_ref, dkv_ref, dw_ref, dg_ref):
        dkvb = dkv_ref[...].astype(BF16)
        dw_ref[...] = _dot_tn(mn_ref[...], dkvb).astype(BF16)
        dmn = _dot_nt(dkvb, w_ref[...])
        xf = m_ref[...]
        r = lax.rsqrt(jnp.mean(xf * xf, axis=-1, keepdims=True) + EPS)
        dg_ref[...] = jnp.sum(dmn * xf * r, axis=0, keepdims=True)

    return pl.pallas_call(
        body, name="mem_bwd",
        out_shape=[jax.ShapeDtypeStruct((D_MODEL, 2 * WIDTH), BF16), jax.ShapeDtypeStruct((1, D_MODEL), F32)],
        compiler_params=pltpu.CompilerParams(vmem_limit_bytes=VMEM_LIMIT),
    )(mem, g, mem_n, w, dkv)


def _dh_bwd(dpb, dpg, wt, wgt, x, g, dxo, parts=(), grads=()):
    S = x.shape[0]
    tm, tkb, tkg = 1024, D_BRANCHES // 4, D_GATES // 4
    nkb, nkg = 4, 4
    nk = nkb + nkg
    ni = S // tm
    n, m = len(parts), len(grads)

    def body(dpb_ref, wbr_ref, dpg_ref, wg_ref, x_ref, g_ref, dxo_ref, *rest):
        p_in, g_in = rest[:n], rest[n:n + m]
        dx_ref, dg_ref = rest[n + m:n + m + 2]
        p_out, g_out = rest[n + m + 2:2 * n + m + 2], rest[2 * n + m + 2:2 * (n + m) + 2]
        acc, sems = rest[2 * (n + m) + 2], rest[2 * (n + m) + 3:]
        second_sems, first_sems = (sems[:3] if n else ()), sems[3 if n else 0:]
        i, kk = pl.program_id(0), pl.program_id(1)

        @pl.when(jnp.logical_and(i == 0, kk == 0))
        def _():
            dg_ref[...] = jnp.zeros_like(dg_ref)
            if n:
                _comm_start(_rs_second(p_in, p_out, *second_sems))
            if m:
                _comm_start(_rs_first(g_in, g_out, *first_sems))

        @pl.when(kk == 0)
        def _():
            acc[...] = jnp.zeros_like(acc)

        @pl.when(kk < nkb)
        def _():
            acc[...] += _dot(dpb_ref[...], wbr_ref[...])

        @pl.when(kk >= nkb)
        def _():
            acc[...] += _dot(dpg_ref[...], wg_ref[...])

        @pl.when(kk == nk - 1)
        def _():
            xf = x_ref[...]
            r = lax.rsqrt(jnp.mean(xf * xf, axis=-1, keepdims=True) + EPS)
            xhat = xf * r
            dh = acc[...]
            dg_ref[...] += jnp.sum(dh * xhat, axis=0, keepdims=True)
            dxh = dh * g_ref[...]
            dx_ref[...] = dxo_ref[...] + r * (dxh - xhat * jnp.mean(dxh * xhat, axis=-1, keepdims=True))

        if n or m:
            @pl.when(jnp.logical_and(i == ni - 1, kk == nk - 1))
            def _():
                if n:
                    _comm_wait(_rs_second(p_in, p_out, *second_sems))
                if m:
                    _comm_wait(_rs_first(g_in, g_out, *first_sems))

    res = pl.pallas_call(
        body, name="dh_bwd_scatter" if (n or m) else "dh_bwd",
        grid=(ni, nk),
        in_specs=[pl.BlockSpec((tm, tkb), lambda i, k: (i, jnp.minimum(k, nkb - 1))),
                  pl.BlockSpec((tkb, D_MODEL), lambda i, k: (jnp.minimum(k, nkb - 1), 0)),
                  pl.BlockSpec((tm, tkg), lambda i, k: (i, jnp.maximum(k - nkb, 0))),
                  pl.BlockSpec((tkg, D_MODEL), lambda i, k: (jnp.maximum(k - nkb, 0), 0)),
                  pl.BlockSpec((tm, D_MODEL), lambda i, k: (i, 0)), pl.BlockSpec((1, D_MODEL), lambda i, k: (0, 0)),
                  pl.BlockSpec((tm, D_MODEL), lambda i, k: (i, 0))] + [ANY] * (n + m),
        out_specs=[pl.BlockSpec((tm, D_MODEL), lambda i, k: (i, 0)), pl.BlockSpec((1, D_MODEL), lambda i, k: (0, 0))]
                  + [ANY] * (n + m),
        out_shape=[jax.ShapeDtypeStruct((S, D_MODEL), F32), jax.ShapeDtypeStruct((1, D_MODEL), F32)]
                  + [jax.ShapeDtypeStruct(p.shape, p.dtype) for p in parts]
                  + [jax.ShapeDtypeStruct(gr.shape[:1] + gr.shape[2:], gr.dtype) for gr in grads],
        scratch_shapes=[pltpu.VMEM((tm, D_MODEL), F32)] + (_dma_sems(3 * n, 3 * n, n) if n else [])
                       + (_dma_sems(N_CHIP * m, N_CHIP * m) if m else []),
        compiler_params=_params(("arbitrary", "arbitrary")),
    )(dpb, wt, dpg, wgt, x, g, dxo, *parts, *grads)
    return res[0], res[1], list(res[2:2 + n]), list(res[2 + n:])


def _dw_in(h, dpb, dpg, parts=()):
    S = h.shape[0]
    tk = 2048
    nk = S // tk
    n = len(parts)
    tmb = D_BRANCHES // 4
    ng = D_GATES // GATE_TILE

    def accumulate(a_ref, h_ref, o_ref, acc):
        kk = pl.program_id(1)

        @pl.when(kk == 0)
        def _():
            acc[...] = jnp.zeros_like(acc)

        acc[...] += _dot_tn(a_ref[...], h_ref[...])

        @pl.when(kk == nk - 1)
        def _():
            o_ref[...] = acc[...].astype(BF16)

    def branches(a_ref, h_ref, *rest):
        p_in, o_ref, p_out = rest[:n], rest[n], rest[n + 1:2 * n + 1]
        acc, sems = rest[2 * n + 1], rest[2 * n + 2:]
        i, kk = pl.program_id(0), pl.program_id(1)

        if n:
            @pl.when(jnp.logical_and(i == 0, kk == 0))
            def _():
                _comm_start(_rs_second(p_in, p_out, *sems))

        accumulate(a_ref, h_ref, o_ref, acc)

        if n:
            @pl.when(jnp.logical_and(i == 3, kk == nk - 1))
            def _():
                _comm_wait(_rs_second(p_in, p_out, *sems))

    def gates(a_ref, h_ref, dst_ref, o_ref, acc):
        accumulate(a_ref, h_ref, o_ref, acc)

    res = pl.pallas_call(
        branches, name="dw_in_branches_scatter" if n else "dw_in_branches",
        grid=(4, nk),
        in_specs=[pl.BlockSpec((tk, tmb), lambda i, k: (k, i)), pl.BlockSpec((tk, D_MODEL), lambda i, k: (k, 0))]
                 + [ANY] * n,
        out_specs=[pl.BlockSpec((tmb, D_MODEL), lambda i, k: (i, 0))] + [ANY] * n,
        out_shape=[jax.ShapeDtypeStruct((D_IN, D_MODEL), BF16)]
                  + [jax.ShapeDtypeStruct(p.shape, p.dtype) for p in parts],
        scratch_shapes=[pltpu.VMEM((tmb, D_MODEL), F32)] + (_dma_sems(3 * n, 3 * n, n) if n else []),
        compiler_params=_params(("arbitrary", "arbitrary")),
    )(dpb, h, *parts)
    dwt = pl.pallas_call(
        gates, name="dw_in_gates",
        grid=(ng, nk),
        in_specs=[pl.BlockSpec((tk, GATE_TILE), lambda i, k: (k, i)), pl.BlockSpec((tk, D_MODEL), lambda i, k: (k, 0)),
                  ANY],
        out_specs=pl.BlockSpec((GATE_TILE, D_MODEL), lambda i, k: (D_BRANCHES // GATE_TILE + i, 0)),
        out_shape=jax.ShapeDtypeStruct((D_IN, D_MODEL), BF16),
        input_output_aliases={2: 0},
        scratch_shapes=[pltpu.VMEM((GATE_TILE, D_MODEL), F32)],
        compiler_params=_params(("parallel", "arbitrary")),
    )(dpg, h, res[0])
    return dwt, list(res[1:])


def _row_tile(R, C, block_bytes=2 << 20):
    for cand in range(min(R, block_bytes // (C * 4)) // 8 * 8, 0, -8):
        if R % cand == 0:
            return cand
    return R


def _adamw_update(p_ref, w_ref, m_ref, v_ref, g_ref, d_ref, nm_ref, nv_ref):
    c1 = 1.0 / (1.0 - ADAM_B1 ** ADAM_STEP)
    c2 = 1.0 / (1.0 - ADAM_B2 ** ADAM_STEP)
    g = p_ref[0].astype(F32)
    for k in range(1, p_ref.shape[0]):
        g = g + p_ref[k].astype(F32)
    nm = ADAM_B1 * m_ref[...] + (1.0 - ADAM_B1) * g
    nv = ADAM_B2 * v_ref[...] + (1.0 - ADAM_B2) * (g * g)
    g_ref[...] = g
    nm_ref[...] = nm
    nv_ref[...] = nv
    d_ref[...] = -ADAM_LR * ((nm * c1) / (jnp.sqrt(nv * c2) + ADAM_EPS) + ADAM_WD * w_ref[...])


def _adamw(parts, w, m, v, name):
    P, R, C = parts.shape
    tr = _row_tile(R, C)

    def body(*refs):
        _adamw_update(*refs)

    spec = pl.BlockSpec((tr, C), lambda i: (i, 0))
    return pl.pallas_call(
        body, name=name,
        grid=(R // tr,),
        in_specs=[pl.BlockSpec((P, tr, C), lambda i: (0, i, 0)), spec, spec, spec],
        out_specs=[spec] * 4,
        out_shape=[jax.ShapeDtypeStruct((R, C), F32)] * 4,
        compiler_params=_params(("parallel",)),
    )(parts, w, m, v)


def _adamw_layers(parts, w, m, v, name):
    depth = len(parts)
    P, R, C = parts[0].shape
    tr = _row_tile(R, C, 1 << 20)

    def body(*refs):
        layer = pl.program_id(0)
        for k in range(depth):
            @pl.when(layer == k)
            def _(k=k):
                _adamw_update(refs[k], *refs[depth:])

    def part_spec(k):
        return pl.BlockSpec((P, tr, C), lambda l, i: (0, jnp.where(l == k, i, 0), 0))

    spec = pl.BlockSpec((None, tr, C), lambda l, i: (l, i, 0))
    return pl.pallas_call(
        body, name=name,
        grid=(depth, R // tr),
        in_specs=[part_spec(k) for k in range(depth)] + [spec] * 3,
        out_specs=[spec] * 4,
        out_shape=[jax.ShapeDtypeStruct((depth, R, C), F32)] * 4,
        compiler_params=_params(("arbitrary", "arbitrary")),
    )(*parts, w, m, v)


def _place():
    return lax.axis_index("x"), lax.axis_index("y"), lax.axis_index("c")


def _all_gather(shards):
    n = len(shards)

    def body(*refs):
        ins, outs = refs[:n], refs[n:2 * n]
        send1, recv1, local_sems, relay_send, relay_recv, send2, recv2 = refs[2 * n:]
        x, y, c = _place()
        me, sibling = (x, y, c), (x, y, 1 - c)
        x_nbr, y_nbr, diagonal = [(*chip, c) for chip in _other_chips(x, y)]
        local, first_out, first_in, relay_out, relay_in = [], [], [], [], []
        for a in range(n):
            local.append(pltpu.make_async_copy(ins[a], outs[a].at[_dev(me)], local_sems.at[a]))
            for k, to in enumerate((sibling, x_nbr, y_nbr)):
                first_out.append(_remote(ins[a], outs[a].at[_dev(me)], send1, recv1, 3 * a + k, to))
                first_in.append(_remote(ins[a], outs[a].at[_dev(to)], send1, recv1, 3 * a + k, to))
            half = shards[a].shape[0] // 2
            for k, (src, to, rows) in enumerate(((x_nbr, y_nbr, pl.ds(0, half)), (y_nbr, x_nbr, pl.ds(half, half)))):
                passed, got = outs[a].at[_dev(src)].at[rows], outs[a].at[_dev(diagonal)].at[rows]
                relay_out.append(_remote(passed, passed, relay_send, relay_recv, 2 * a + k, to))
                relay_in.append(_remote(got, got, relay_send, relay_recv, 2 * a + k, to))
        second = _ag_second(outs, send2, recv2)
        for cp in local + first_out:
            cp.start()
        for a in range(n):
            for k in range(2):
                first_in[3 * a + 1 + k].wait_recv()
                relay_out[2 * a + k].start()
                second[1][3 * a + k].start()
        for a in range(n):
            for k in range(2):
                relay_in[2 * a + k].wait_recv()
            second[1][3 * a + 2].start()
        for a in range(n):
            first_in[3 * a].wait_recv()
        for cp in second[2]:
            cp.wait_recv()
        for cp in first_out + relay_out + second[1]:
            cp.wait_send()
        for cp in local:
            cp.wait()

    assert all(s.shape[0] % 32 == 0 for s in shards)
    return pl.pallas_call(
        body, name="weights_all_gather",
        in_specs=[ANY] * n, out_specs=[ANY] * n,
        out_shape=[jax.ShapeDtypeStruct((N_DEV,) + s.shape, s.dtype) for s in shards],
        scratch_shapes=_dma_sems(3 * n, 3 * n, n, 2 * n, 2 * n, 3 * n, 3 * n),
        compiler_params=pltpu.CompilerParams(has_side_effects=True),
    )(*shards)


N_BIG = 4


def _dev(p):
    return 4 * p[0] + 2 * p[1] + p[2]


def _other_chips(x, y):
    return [(1 - x, y), (x, 1 - y), (1 - x, 1 - y)]


def _remote(src, dst, send_sems, recv_sems, k, to):
    return pltpu.make_async_remote_copy(src_ref=src, dst_ref=dst, send_sem=send_sems.at[k], recv_sem=recv_sems.at[k],
                                        device_id=to, device_id_type=MESH)


def _ag_first(ins, outs, send_sems, recv_sems, local_sems):
    x, y, c = _place()
    me = (x, y, c)
    targets = [(x, y, 1 - c)] + [(*chip, c) for chip in _other_chips(x, y)]
    local, out, inc = [], [], []
    for a in range(len(ins)):
        local.append(pltpu.make_async_copy(ins[a], outs[a].at[_dev(me)], local_sems.at[a]))
        for k, to in enumerate(targets):
            out.append(_remote(ins[a], outs[a].at[_dev(me)], send_sems, recv_sems, 4 * a + k, to))
            inc.append(_remote(ins[a], outs[a].at[_dev(to)], send_sems, recv_sems, 4 * a + k, to))
    return local, out, inc


def _ag_second(bufs, send_sems, recv_sems):
    x, y, c = _place()
    out, inc = [], []
    for a in range(len(bufs)):
        for j, chip in enumerate(_other_chips(x, y)):
            mine, theirs = bufs[a].at[_dev((*chip, c))], bufs[a].at[_dev((*chip, 1 - c))]
            out.append(_remote(mine, mine, send_sems, recv_sems, 3 * a + j, (x, y, 1 - c)))
            inc.append(_remote(theirs, theirs, send_sems, recv_sems, 3 * a + j, (x, y, 1 - c)))
    return [], out, inc


def _rs_first(ins, outs, send_sems, recv_sems):
    x, y, c = _place()
    out = [_remote(ins[a].at[j, 1 - c], outs[a].at[j], send_sems, recv_sems, N_CHIP * a + j, (x, y, 1 - c))
           for a in range(len(ins)) for j in range(N_CHIP)]
    return [], out, out


def _rs_second(ins, outs, send_sems, recv_sems, local_sems):
    x, y, c = _place()
    my_chip = 2 * x + y
    local, out, inc = [], [], []
    for a in range(len(ins)):
        local.append(pltpu.make_async_copy(ins[a].at[my_chip], outs[a].at[my_chip], local_sems.at[a]))
        for k, (ox, oy) in enumerate(_other_chips(x, y)):
            out.append(_remote(ins[a].at[2 * ox + oy], outs[a].at[my_chip], send_sems, recv_sems, 3 * a + k, (ox, oy, c)))
            inc.append(_remote(ins[a].at[2 * ox + oy], outs[a].at[2 * ox + oy], send_sems, recv_sems, 3 * a + k,
                               (ox, oy, c)))
    return local, out, inc


def _comm_start(exchange):
    local, out, _ = exchange
    for cp in local + out:
        cp.start()


def _comm_wait(exchange):
    local, out, inc = exchange
    for cp in inc:
        cp.wait_recv()
    for cp in out:
        cp.wait_send()
    for cp in local:
        cp.wait()


def _dma_sems(*counts):
    return [pltpu.SemaphoreType.DMA((n,)) for n in counts]


def _pair_sum(grads, recvs):
    n = len(grads)

    def body(c_ref, *refs):
        for a in range(n):
            refs[2 * n + a][...] = (refs[a][...].astype(F32) + refs[n + a][...].astype(F32)).astype(BF16)

    def g_spec(g):
        return pl.BlockSpec((None, None) + g.shape[2:], lambda j, c_ref: (j, c_ref[0], 0, 0))

    def r_spec(r):
        return pl.BlockSpec((None,) + r.shape[1:], lambda j, c_ref: (j, 0, 0))

    return pl.pallas_call(
        body, name="pair_sum",
        grid_spec=pltpu.PrefetchScalarGridSpec(
            num_scalar_prefetch=1, grid=(N_CHIP,),
            in_specs=[g_spec(g) for g in grads] + [r_spec(r) for r in recvs],
            out_specs=[r_spec(r) for r in recvs]),
        out_shape=[jax.ShapeDtypeStruct(r.shape, BF16) for r in recvs],
        compiler_params=_params(("parallel",)),
    )(lax.axis_index("c").reshape(1).astype(jnp.int32), *grads, *recvs)


SMALL_ROWS = 544


def _all_reduce_small(buf, parts=()):
    n = len(parts)

    def body(in_ref, *rest):
        p_in, out_ref, p_out = rest[:n], rest[n], rest[n + 1:2 * n + 1]
        recv, acc, send1, recv1, send2, recv2 = rest[2 * n + 1:2 * n + 7]
        scatter_sems = rest[2 * n + 7:]
        x, y, c = _place()
        me = 4 * x + 2 * y + c
        peers = [(x ^ (r >> 2), y ^ ((r >> 1) & 1), c ^ (r & 1)) for r in range(1, N_DEV)]

        def idx(p):
            return 4 * p[0] + 2 * p[1] + p[2]

        if n:
            _comm_start(_rs_second(p_in, p_out, *scatter_sems))
        first = [pltpu.make_async_remote_copy(
            src_ref=in_ref.at[idx(p)], dst_ref=recv.at[me], send_sem=send1.at[r], recv_sem=recv1.at[r],
            device_id=p, device_id_type=MESH) for r, p in enumerate(peers)]
        for cp in first:
            cp.start()
        recv[me] = in_ref[me]
        for r, p in enumerate(peers):
            pltpu.make_async_remote_copy(
                src_ref=in_ref.at[idx(p)], dst_ref=recv.at[idx(p)], send_sem=send1.at[r], recv_sem=recv1.at[r],
                device_id=p, device_id_type=MESH).wait_recv()
        total = recv[0]
        for k in range(1, N_DEV):
            total = total + recv[k]
        acc[...] = total
        out_ref[me] = total
        second = [pltpu.make_async_remote_copy(
            src_ref=acc, dst_ref=out_ref.at[me], send_sem=send2.at[r], recv_sem=recv2.at[r],
            device_id=p, device_id_type=MESH) for r, p in enumerate(peers)]
        for cp in second:
            cp.start()
        for r, p in enumerate(peers):
            pltpu.make_async_remote_copy(
                src_ref=acc, dst_ref=out_ref.at[idx(p)], send_sem=send2.at[r], recv_sem=recv2.at[r],
                device_id=p, device_id_type=MESH).wait_recv()
        for cp in first + second:
            cp.wait_send()
        if n:
            _comm_wait(_rs_second(p_in, p_out, *scatter_sems))

    vm = pl.BlockSpec(memory_space=pltpu.VMEM)
    res = pl.pallas_call(
        body, name="small_grads_all_reduce",
        in_specs=[vm] + [ANY] * n, out_specs=[vm] + [ANY] * n,
        out_shape=[jax.ShapeDtypeStruct(buf.shape, F32)] + [jax.ShapeDtypeStruct(p.shape, p.dtype) for p in parts],
        scratch_shapes=[pltpu.VMEM(buf.shape, F32), pltpu.VMEM(buf.shape[1:], F32)] + _dma_sems(7, 7, 7, 7)
                       + (_dma_sems(3 * n, 3 * n, n) if n else []),
        compiler_params=pltpu.CompilerParams(has_side_effects=True, vmem_limit_bytes=VMEM_LIMIT),
    )(buf, *parts)
    return res[0], list(res[1:])


def _dilate(a, d):
    if d == 1:
        return a
    S, C = a.shape
    return a.reshape(S // d, d, C).transpose(1, 0, 2).reshape(S, C)


def _undilate(a, d):
    if d == 1:
        return a
    S, C = a.shape
    return a.reshape(d, S // d, C).transpose(1, 0, 2).reshape(S, C)


def _cols(a, cb, n=1):
    return a[:, cb * WIDTH:(cb + n) * WIDTH]


def _to_blocks(g, kind):
    if kind == "rows":
        C = g.shape[1]
        return g.reshape(N_CHIP, 2, -1, C)
    return g.reshape(4 * WIDTH, N_CHIP, 2, -1).transpose(1, 2, 0, 3)


SMALL = ("norm_g", "gm_ln_g", "gm_ln_b", "gm_ws", "gm_bs", "pool_w", "pool_scale", "mem_norm_g", "final_norm_g")


def _pack_small(tree):
    flat = jnp.concatenate([tree[k].reshape(-1, 128) for k in SMALL], axis=0)
    return jnp.pad(flat, ((0, N_DEV * SMALL_ROWS - flat.shape[0]), (0, 0)))


def _unpack_small(flat, like):
    out, at = {}, 0
    for k in SMALL:
        rows = like[k].size // 128
        out[k] = flat[at:at + rows].reshape(like[k].shape)
        at += rows
    return out


def _make_layer(wt, wkv, wb, wout, norm_g, mem_norm_g, ln_g, ln_b, gm_ws, gm_bs, pool_w, pool_scale):
    tril = jnp.tril(jnp.ones((CHUNK, CHUNK), bool))
    wsm = jnp.where(tril, gm_ws, 0.0).astype(BF16)
    pw = pool_w.astype(BF16)
    bands, bands_t = _band_constants()
    return dict(wt=wt, wgt=wt[D_BRANCHES:], wkv=wkv, wb=wb, wout=wout, g=norm_g[None], mg=mem_norm_g[None],
                ln_g=ln_g[None],
                ln_b=ln_b[None], wsm=wsm, wsm_t=wsm.transpose(0, 2, 1), pw=pw, pw_t=pw.transpose(0, 2, 1),
                ps=pool_scale[None], bias=jnp.repeat(gm_bs.T, HEAD, axis=1), bands=bands, bands_t=bands_t)


def _layer_fwd(xl, mem0, L, next_shards=()):
    S = xl.shape[0]
    proj, gates, h, half_win = _in_proj(xl, L["g"], L["wt"], L["wgt"], next_shards[:1])
    kv, mem_n = _mem_kv(mem0, L["mg"], L["wkv"])
    y4, cdf, win, half_small = _abm_fwd(proj, L["ln_g"], L["ln_b"], L["wsm"], L["bias"], L["pw"], L["ps"], kv,
                                        L["bands"], half_win, next_shards[2:])
    o_g, l_g = [], []
    for gi, d in enumerate(DILATIONS):
        if d == 1:
            o, lse = _attn_fwd(proj, CB_Q0, proj, CB_K, proj, CB_CV, S // CHUNK)
        else:
            o, lse = _attn_fwd_dilated(proj, CB_Q0 + gi, CB_K, CB_CV, d)
        o_g.append(o)
        l_g.append(lse)
    (xn, y4, oc, lse, z), small = _merge_fwd(xl, y4, o_g, l_g, proj, gates, L["wb"], L["wout"], next_shards[1:2],
                                             half_small)
    saved = dict(x=xl, proj=proj, gates=gates, h=h, kv=kv, mem_n=mem_n, y4=y4, cdf=cdf, oc=oc, lse=lse, z=z)
    return xn, saved, win + small


def _place_cols(dst, piece, cb):
    return lax.dynamic_update_slice(dst, piece, (0, cb * WIDTH))


def _layer_bwd(dx, mem0, L, sv, later=(), last=False):
    S = dx.shape[0]
    proj = sv["proj"]
    (dy3, doc, delta, dcg, dgm, dwb, dwout), from_sibling = _merge_bwd(
        dx, sv["y4"], sv["oc"], sv["z"], proj, sv["gates"], L["wb"], L["wout"], later)
    pair = _pair_sum(later, from_sibling) if later else ()
    dpb, dm, dlng, dlnb, dws, dbias, dpw, dps, dkv = _abm_bwd(
        proj, sv["cdf"], dy3, L["ln_g"], L["ln_b"], L["wsm"], L["wsm_t"], L["bias"], L["pw"], L["pw_t"], L["ps"], sv["kv"],
        L["bands"], L["bands_t"])
    dk, dv = None, None
    for gi, d in enumerate(DILATIONS):
        if d == 1:
            dpb, dk, dv = _attn_bwd(proj, CB_Q0, proj, CB_K, proj, CB_CV, doc, sv["lse"], delta, S // CHUNK,
                                    dpb, CB_Q0)
        else:
            dpb, dk, dv = _attn_bwd_dilated(proj, CB_Q0 + gi, CB_K, CB_CV, doc, sv["lse"], delta, d, dk, dv,
                                            dpb, CB_Q0 + gi)
    dpb = _place_cols(dpb, dk, CB_K)
    dpb = _place_cols(dpb, dv, CB_CV)
    dpb = _place_cols(dpb, dcg, CB_CGATE)
    dpb = _place_cols(dpb, dm, CB_MQ)
    dwkv, dmg = _mem_bwd(mem0, L["mg"], sv["mem_n"], L["wkv"], dkv)
    dwin_t, parts_rest = _dw_in(sv["h"], dpb, dgm, pair[1:])
    big = _blocked(dict(w_in=dwin_t, w_mem_kv=dwkv, w_branch=dwb, w_out=dwout))
    dxi, dng, parts, from_sibling = _dh_bwd(dpb, dgm, L["wt"], L["wgt"], sv["x"], L["g"], dx, pair[:1],
                                            big if last else ())
    parts = parts + parts_rest
    small = dict(norm_g=dng[0], gm_ln_g=dlng[0], gm_ln_b=dlnb[0], gm_ws=dws,
                 gm_bs=_bias_reduce(dbias)[:, :N_HEAD].T, pool_w=dpw, pool_scale=dps[0], mem_norm_g=dmg[0])
    return dxi, big, small, parts, from_sibling


BIG = ("w_in", "w_mem_kv", "w_branch", "w_out")


def _blocked(big):
    return [_to_blocks(big["w_in"], "rows"), _to_blocks(big["w_mem_kv"], "rows"),
            _to_blocks(big["w_branch"], "branch"), _to_blocks(big["w_out"], "rows")]


def _full_weights(gathered):
    win_t, wkv, wb, wout = gathered
    return (win_t.reshape(D_IN, D_MODEL), wkv.reshape(D_MODEL, 2 * WIDTH),
            wb.reshape(N_DEV, 4, WIDTH, -1).transpose(1, 2, 0, 3).reshape(4, WIDTH, D_MODEL),
            wout.reshape(D_MODEL, D_MODEL))


def kernel(x, mem, norm_g, w_in, gm_ln_g, gm_ln_b, gm_ws, gm_bs, pool_w, pool_scale, mem_norm_g, w_mem_kv, w_branch, w_out, final_norm_g, loss_target, m_norm_g, m_w_in, m_gm_ln_g, m_gm_ln_b, m_gm_ws, m_gm_bs, m_pool_w, m_pool_scale, m_mem_norm_g, m_w_mem_kv, m_w_branch, m_w_out, m_final_norm_g, v_norm_g, v_w_in, v_gm_ln_g, v_gm_ln_b, v_gm_ws, v_gm_bs, v_pool_w, v_pool_scale, v_mem_norm_g, v_w_mem_kv, v_w_branch, v_w_out, v_final_norm_g):
    x0 = x[0]
    mem0 = mem[0]
    tgt = loss_target[0]
    S = x0.shape[0]

    shards = [[w_in[l].T.astype(BF16), w_mem_kv[l].astype(BF16), w_branch[l].astype(BF16).reshape(4 * WIDTH, -1),
               w_out[l].astype(BF16)] for l in range(DEPTH)]
    gathered = _all_gather(shards[0])
    layers, saved = [], []
    xl = x0
    for l in range(DEPTH):
        layers.append(_make_layer(*_full_weights(gathered), norm_g[l], mem_norm_g[l], gm_ln_g[l], gm_ln_b[l],
                                  gm_ws[l], gm_bs[l], pool_w[l], pool_scale[l]))
        xl, sv, gathered = _layer_fwd(xl, mem0, layers[l], shards[l + 1] if l + 1 < DEPTH else ())
        saved.append(sv)

    loss_part, dx, d_final = _loss_head(xl, final_norm_g[None], tgt)
    loss = lax.psum(loss_part[0, 0], ("x", "y", "c"))

    small = {k: [None] * DEPTH for k in SMALL if k != "final_norm_g"}
    parts = [None] * DEPTH
    later = ()
    for l in reversed(range(DEPTH)):
        dx, gb, gs, done, from_sibling = _layer_bwd(dx, mem0, layers[l], saved[l], later, last=(l == 0))
        if later:
            parts[l + 1] = done
        later = gb
        for k in gs:
            small[k][l] = gs[k]
    grad_x = dx[None]
    small_tree = {k: jnp.stack(small[k]) for k in small}
    small_tree["final_norm_g"] = d_final[0]
    reduced, parts[0] = _all_reduce_small(_pack_small(small_tree).reshape(N_DEV, SMALL_ROWS, 128),
                                          _pair_sum(later, from_sibling))

    weights = dict(norm_g=norm_g, w_in=w_in, gm_ln_g=gm_ln_g, gm_ln_b=gm_ln_b, gm_ws=gm_ws, gm_bs=gm_bs,
                   pool_w=pool_w, pool_scale=pool_scale, mem_norm_g=mem_norm_g, w_mem_kv=w_mem_kv,
                   w_branch=w_branch, w_out=w_out, final_norm_g=final_norm_g)
    m_in = dict(norm_g=m_norm_g, w_in=m_w_in, gm_ln_g=m_gm_ln_g, gm_ln_b=m_gm_ln_b, gm_ws=m_gm_ws, gm_bs=m_gm_bs,
                pool_w=m_pool_w, pool_scale=m_pool_scale, mem_norm_g=m_mem_norm_g, w_mem_kv=m_w_mem_kv,
                w_branch=m_w_branch, w_out=m_w_out, final_norm_g=m_final_norm_g)
    v_in = dict(norm_g=v_norm_g, w_in=v_w_in, gm_ln_g=v_gm_ln_g, gm_ln_b=v_gm_ln_b, gm_ws=v_gm_ws, gm_bs=v_gm_bs,
                pool_w=v_pool_w, pool_scale=v_pool_scale, mem_norm_g=v_mem_norm_g, w_mem_kv=v_w_mem_kv,
                w_branch=v_w_branch, w_out=v_w_out, final_norm_g=v_final_norm_g)
    res = {}
    def view(k, arr):
        return arr.transpose(0, 2, 1) if k == "w_in" else arr

    for a, k in enumerate(BIG):
        shape = view(k, weights[k]).shape
        by_layer = [parts[l][a] for l in range(DEPTH)]
        lrc = (DEPTH,) + by_layer[0].shape[1:]
        outs = _adamw_layers(by_layer, view(k, weights[k]).reshape(lrc), view(k, m_in[k]).reshape(lrc),
                             view(k, v_in[k]).reshape(lrc), "adamw_" + k)
        res[k] = [view(k, o.reshape(shape)) for o in outs]
    outs = _adamw(reduced.reshape(1, N_DEV * SMALL_ROWS, 128), _pack_small(weights), _pack_small(m_in),
                  _pack_small(v_in), "adamw_small")
    unpacked = [_unpack_small(o, weights) for o in outs]
    for k in SMALL:
        res[k] = [u[k] for u in unpacked]

    order = ("norm_g", "w_in", "gm_ln_g", "gm_ln_b", "gm_ws", "gm_bs", "pool_w", "pool_scale", "mem_norm_g",
             "w_mem_kv", "w_branch", "w_out", "final_norm_g")
    return (loss, grad_x, *[res[k][0] for k in order], *[res[k][1] for k in order],
            *[res[k][2] for k in order], *[res[k][3] for k in order])
```

```python
import functools
import math

import numpy as np

import jax
import jax.numpy as jnp
from jax import lax
from jax.experimental import pallas as pl
from jax.experimental.pallas import tpu as pltpu

F32 = jnp.float32
BF16 = jnp.bfloat16

D_MODEL = 1024
DEPTH = 4
WIDTH = 512
D_IN = 10752
HEAD = 128
N_HEAD = 4
CHUNK = 128
MEM_LEN = 256
POOL_WINDOWS = (2, 4, 8, 16)
DILATIONS = (1, 4, 16)
EPS = 1e-6
NEG = -1e30
ATT_SCALE = HEAD ** -0.5
N_DEV = 8
N_CHIP = 4

D_BRANCHES = 6656
D_GATES = D_IN - D_BRANCHES
CB_U, CB_V, CB_AGATE, CB_PIN, CB_PGATE = 0, 1, 2, 3, 4
CB_Q0, CB_K, CB_CV, CB_CGATE, CB_MQ, CB_MGATE = 5, 8, 9, 10, 11, 12

ADAM_LR = 0.001
ADAM_B1 = 0.9
ADAM_B2 = 0.999
ADAM_EPS = 1e-08
ADAM_WD = 0.01
ADAM_STEP = 10

VMEM_LIMIT = 56 * 1024 * 1024
MESH = pl.DeviceIdType.MESH
ANY = pl.BlockSpec(memory_space=pl.ANY)

NT = (((1,), (1,)), ((), ()))
TN = (((0,), (0,)), ((), ()))


def _dot(a, b):
    return jnp.dot(a, b, preferred_element_type=F32)


def _dot_nt(a, b):
    return lax.dot_general(a, b, NT, preferred_element_type=F32)


def _dot_tn(a, b):
    return lax.dot_general(a, b, TN, preferred_element_type=F32)


def _sigmoid(x):
    return 0.5 * jnp.tanh(0.5 * x) + 0.5


def _silu(x):
    return x * _sigmoid(x)


def _silu_and_grad(x):
    s = _sigmoid(x)
    return x * s, s * (1.0 + x * (1.0 - s))


def _normal_cdf(x):
    return 0.5 * (1.0 + lax.erf(x * (2.0 ** -0.5)))


def _gelu_and_grad(x, cdf):
    return x * cdf, cdf + x * jnp.exp(-0.5 * x * x) * (1.0 / math.sqrt(2.0 * math.pi))


def _col(blk, h):
    lane = lax.broadcasted_iota(jnp.int32, blk.shape, 1)
    return jnp.sum(jnp.where(lane == h, blk, 0.0), axis=1, keepdims=True)


def _put_cols(cols):
    rows = cols[0].shape[0]
    lane = lax.broadcasted_iota(jnp.int32, (rows, 128), 1)
    out = jnp.zeros((rows, 128), F32)
    for h, cv in enumerate(cols):
        out = jnp.where(lane == h, cv, out)
    return out


def _params(sem, vmem=VMEM_LIMIT):
    return pltpu.CompilerParams(dimension_semantics=sem, vmem_limit_bytes=vmem)


def _full(shape):
    nd = len(shape)
    return pl.BlockSpec(shape, lambda *_: (0,) * nd)


def _resident(shape):
    nd = len(shape)
    return pl.BlockSpec(shape, lambda *_: (0,) * nd, pipeline_mode=pl.Buffered(1))


def _rows(tm, width, cb=0):
    return pl.BlockSpec((tm, width), lambda i: (i, cb))


GATE_TILE = 512


def _in_proj(x, g, wt, wgt, shards=()):
    S = x.shape[0]
    tm, tnb, tng = 1024, D_BRANCHES // 4, D_GATES // 4
    njb, njg = 4, 4
    n = len(shards)
    ni, nj = S // tm, njb + njg

    def body(x_ref, g_ref, wbr_ref, wg_ref, *rest):
        ins, (proj_ref, gates_ref, h_ref), outs = rest[:n], rest[n:n + 3], rest[n + 3:2 * n + 3]
        hs, sems = rest[2 * n + 3], rest[2 * n + 4:]
        i, j = pl.program_id(0), pl.program_id(1)

        if n:
            @pl.when(jnp.logical_and(i == 0, j == 0))
            def _():
                _comm_start(_ag_first(ins, outs, *sems))

        @pl.when(j == 0)
        def _():
            xf = x_ref[...]
            r = lax.rsqrt(jnp.mean(xf * xf, axis=-1, keepdims=True) + EPS)
            h = (xf * r * g_ref[...]).astype(BF16)
            hs[...] = h
            h_ref[...] = h

        @pl.when(j < njb)
        def _():
            proj_ref[...] = _dot_nt(hs[...], wbr_ref[...]).astype(BF16)

        @pl.when(j >= njb)
        def _():
            gates_ref[...] = _dot_nt(hs[...], wg_ref[...]).astype(BF16)

        if n:
            @pl.when(jnp.logical_and(i == ni - 1, j == nj - 1))
            def _():
                _comm_wait(_ag_first(ins, outs, *sems))

    def first(j):
        return jnp.minimum(j, njb - 1)

    def second(j):
        return jnp.maximum(j - njb, 0)

    res = pl.pallas_call(
        body, name="in_proj_gather" if n else "in_proj",
        grid=(ni, nj),
        in_specs=[pl.BlockSpec((tm, D_MODEL), lambda i, j: (i, 0)),
                  pl.BlockSpec((1, D_MODEL), lambda i, j: (0, 0)),
                  pl.BlockSpec((tnb, D_MODEL), lambda i, j: (first(j), 0)),
                  pl.BlockSpec((tng, D_MODEL), lambda i, j: (second(j), 0))]
                 + [ANY] * n,
        out_specs=[pl.BlockSpec((tm, tnb), lambda i, j: (i, first(j))),
                   pl.BlockSpec((tm, tng), lambda i, j: (i, second(j))),
                   pl.BlockSpec((tm, D_MODEL), lambda i, j: (i, 0))] + [ANY] * n,
        out_shape=[jax.ShapeDtypeStruct((S, D_BRANCHES), BF16), jax.ShapeDtypeStruct((S, D_GATES), BF16),
                   jax.ShapeDtypeStruct((S, D_MODEL), BF16)]
                  + [jax.ShapeDtypeStruct((N_DEV,) + s.shape, s.dtype) for s in shards],
        scratch_shapes=[pltpu.VMEM((tm, D_MODEL), BF16)] + (_dma_sems(4 * n, 4 * n, n) if n else []),
        compiler_params=_params(("arbitrary", "arbitrary")),
    )(x, g, wt, wgt, *shards)
    return res[0], res[1], res[2], list(res[3:])


def _mem_kv(mem, g, w):
    M = mem.shape[0]

    def body(m_ref, g_ref, w_ref, kv_ref, mn_ref):
        xf = m_ref[...]
        r = lax.rsqrt(jnp.mean(xf * xf, axis=-1, keepdims=True) + EPS)
        mn = (xf * r * g_ref[...]).astype(BF16)
        mn_ref[...] = mn
        kv_ref[...] = _dot(mn, w_ref[...]).astype(BF16)

    return pl.pallas_call(
        body, name="mem_kv",
        out_shape=[jax.ShapeDtypeStruct((M, 2 * WIDTH), BF16), jax.ShapeDtypeStruct((M, D_MODEL), BF16)],
        compiler_params=pltpu.CompilerParams(vmem_limit_bytes=VMEM_LIMIT),
    )(mem, g, w)


def _band_constants():
    t = np.arange(CHUNK)[:, None]
    s = np.arange(CHUNK)[None, :]
    bands = np.stack([np.stack([(t - s >= 0) & (t - s < win), s > t + CHUNK - win]) for win in POOL_WINDOWS])
    bands = bands.astype(np.float32)
    return jnp.asarray(bands, BF16), jnp.asarray(bands.transpose(0, 1, 3, 2), BF16)


def _inv_count(first_row, win):
    t = first_row + lax.broadcasted_iota(jnp.int32, (CHUNK, 1), 0)
    return 1.0 / jnp.minimum(t + 1, win).astype(F32)


def _layer_norm_fwd(v):
    mu = jnp.mean(v, axis=-1, keepdims=True)
    vc = v - mu
    var = jnp.mean(vc * vc, axis=-1, keepdims=True)
    rstd = lax.rsqrt(var + EPS)
    return vc * rstd, rstd


def _mem_softmax(q, kmem):
    s = _dot_nt(q, kmem) * ATT_SCALE
    m = jnp.max(s, axis=-1, keepdims=True)
    e = jnp.exp(s - m)
    return e * (1.0 / jnp.sum(e, axis=-1, keepdims=True))


def _abm_fwd(proj, ln_g, ln_b, wsm, bias_full, pool_w, pool_scale, kv, bands, gathered=(), shards=()):
    S = proj.shape[0]
    tm = 512
    nchunk = tm // CHUNK
    n, m = len(gathered), len(shards)
    nsteps = S // tm

    def body(u_ref, v_ref, ag_ref, p_ref, ph_ref, pg_ref, mq_ref, mg_ref, lng_ref, lnb_ref, wsm_ref, bias_ref,
             pw_ref, ps_ref, kv_ref, band_ref, *rest):
        s_in = rest[n:n + m]
        y_ref, cdf_ref = rest[n + m], rest[n + m + 1]
        bufs, s_out = rest[n + m + 2:2 * n + m + 2], rest[2 * n + m + 2:2 * (n + m) + 2]
        mix, sems = rest[2 * (n + m) + 2], rest[2 * (n + m) + 3:]
        second_sems, first_sems = (sems[:2] if n else ()), sems[2 if n else 0:]
        i = pl.program_id(0)

        if n or m:
            @pl.when(i == 0)
            def _():
                if n:
                    _comm_start(_ag_second(bufs, *second_sems))
                if m:
                    _comm_start(_ag_first(s_in, s_out, *first_sems))

        au, av = u_ref[...].astype(F32), v_ref[...].astype(F32)
        cdf_u, cdf_v = _normal_cdf(au), _normal_cdf(av)
        cdf_ref[0] = cdf_u.astype(BF16)
        cdf_ref[1] = cdf_v.astype(BF16)
        u, v = au * cdf_u, av * cdf_v
        vhat, _ = _layer_norm_fwd(v)
        vln = (vhat * lng_ref[...] + lnb_ref[...]).astype(BF16)
        for c in range(nchunk):
            for h in range(N_HEAD):
                rs, cs = slice(c * CHUNK, (c + 1) * CHUNK), slice(h * HEAD, (h + 1) * HEAD)
                mix[rs, cs] = _dot(wsm_ref[h], vln[rs, cs]) + bias_ref[:, cs]
        y_ref[0] = (u * mix[...] * _silu(ag_ref[...].astype(F32))).astype(BF16)
        halo_ok = (i > 0).astype(F32)
        for c in range(nchunk):
            rs = slice(c * CHUNK, (c + 1) * CHUNK)
            for g, win in enumerate(POOL_WINDOWS):
                cs = slice(g * HEAD, (g + 1) * HEAD)
                cur = p_ref[rs, cs]
                if c == 0:
                    prev = (ph_ref[:, cs].astype(F32) * halo_ok).astype(BF16)
                else:
                    prev = p_ref[(c - 1) * CHUNK:c * CHUNK, cs]
                sums = _dot(band_ref[g, 0], cur) + _dot(band_ref[g, 1], prev)
                dm = sums * _inv_count(i * tm + c * CHUNK, win) - cur.astype(F32)
                mix[rs, cs] = _dot(dm.astype(BF16), pw_ref[g])
        y_ref[1] = (mix[...] * ps_ref[...] * _silu(pg_ref[...].astype(F32))).astype(BF16)
        for h in range(N_HEAD):
            cs = slice(h * HEAD, (h + 1) * HEAD)
            p = _mem_softmax(mq_ref[:, cs], kv_ref[:, cs])
            mix[:, cs] = _dot(p.astype(BF16), kv_ref[:, WIDTH + h * HEAD:WIDTH + (h + 1) * HEAD])
        y_ref[2] = (mix[...] * _silu(mg_ref[...].astype(F32))).astype(BF16)

        if n or m:
            @pl.when(i == nsteps - 1)
            def _():
                if n:
                    _comm_wait(_ag_second(bufs, *second_sems))
                if m:
                    _comm_wait(_ag_first(s_in, s_out, *first_sems))

    blk = tm // CHUNK
    res = pl.pallas_call(
        body, name="abm_fwd_gather" if n or m else "abm_fwd",
        grid=(nsteps,),
        in_specs=[_rows(tm, WIDTH, CB_U), _rows(tm, WIDTH, CB_V), _rows(tm, WIDTH, CB_AGATE),
                  _rows(tm, WIDTH, CB_PIN),
                  pl.BlockSpec((CHUNK, WIDTH), lambda i: (jnp.maximum(i * blk - 1, 0), CB_PIN)),
                  _rows(tm, WIDTH, CB_PGATE), _rows(tm, WIDTH, CB_MQ), _rows(tm, WIDTH, CB_MGATE),
                  _full((1, WIDTH)), _full((1, WIDTH)), _full((N_HEAD, CHUNK, CHUNK)), _full((CHUNK, WIDTH)),
                  _full((4, HEAD, HEAD)), _full((1, WIDTH)), _full((MEM_LEN, 2 * WIDTH)),
                  _full((4, 2, CHUNK, CHUNK))] + [ANY] * (n + m),
        out_specs=[pl.BlockSpec((3, tm, WIDTH), lambda i: (0, i, 0)), pl.BlockSpec((2, tm, WIDTH), lambda i: (0, i, 0))]
                  + [ANY] * (n + m),
        out_shape=[jax.ShapeDtypeStruct((4, S, WIDTH), BF16),
                   jax.ShapeDtypeStruct((2, S, WIDTH), BF16)]
                  + [jax.ShapeDtypeStruct(b.shape, b.dtype) for b in gathered]
                  + [jax.ShapeDtypeStruct((N_DEV,) + s.shape, s.dtype) for s in shards],
        input_output_aliases={16 + a: 2 + a for a in range(n)},
        scratch_shapes=[pltpu.VMEM((tm, WIDTH), F32)] + (_dma_sems(3 * n, 3 * n) if n else [])
                       + (_dma_sems(4 * m, 4 * m, m) if m else []),
        compiler_params=_params(("arbitrary",)),
    )(proj, proj, proj, proj, proj, proj, proj, proj, ln_g, ln_b, wsm, bias_full, pool_w, pool_scale, kv, bands,
      *gathered, *shards)
    return res[0], res[1], list(res[2:2 + n]), list(res[2 + n:])


ATT_TILE = 512


def _attn_fwd(q, qcb, k, kcb, v, vcb, bps):
    S = q.shape[0]
    tm = ATT_TILE
    nb = tm // CHUNK

    nblocks = nb * N_HEAD

    def body(q_ref, k_ref, v_ref, kh_ref, vh_ref, o_ref, l_ref, sc_s, sp_s, pc_s, pp_s):
        i = pl.program_id(0)

        def prev_kv(n, cs):
            if n == 0:
                return kh_ref[:, cs], vh_ref[:, cs]
            ps = slice((n - 1) * CHUNK, n * CHUNK)
            return k_ref[ps, cs], v_ref[ps, cs]

        pens = []
        for n in range(nb):
            rs = slice(n * CHUNK, (n + 1) * CHUNK)
            pens.append(jnp.full((N_HEAD * CHUNK, 1), jnp.where((i * nb + n) % bps != 0, 0.0, NEG), F32))
            for h in range(N_HEAD):
                cs = slice(h * HEAD, (h + 1) * HEAD)
                bs = slice((n * N_HEAD + h) * CHUNK, (n * N_HEAD + h + 1) * CHUNK)
                qh = q_ref[rs, cs]
                sc_s[bs, :] = _dot_nt(qh, k_ref[rs, cs])
                sp_s[bs, :] = _dot_nt(qh, prev_kv(n, cs)[0])
        row = lax.broadcasted_iota(jnp.int32, (nblocks * CHUNK, CHUNK), 0) & (CHUNK - 1)
        col = lax.broadcasted_iota(jnp.int32, (nblocks * CHUNK, CHUNK), 1)
        sc = jnp.where(col <= row, sc_s[...] * ATT_SCALE, NEG)
        sp = jnp.where(col >= row, sp_s[...] * ATT_SCALE, NEG) + jnp.concatenate(pens, axis=0)
        m = jnp.maximum(jnp.max(sc, axis=-1, keepdims=True), jnp.max(sp, axis=-1, keepdims=True))
        ec = jnp.exp(sc - m)
        ep = jnp.exp(sp - m)
        den = jnp.sum(ec, axis=-1, keepdims=True) + jnp.sum(ep, axis=-1, keepdims=True)
        inv = 1.0 / den
        pc_s[...] = (ec * inv).astype(BF16)
        pp_s[...] = (ep * inv).astype(BF16)
        lse = m + jnp.log(den)
        for n in range(nb):
            rs = slice(n * CHUNK, (n + 1) * CHUNK)
            for h in range(N_HEAD):
                cs = slice(h * HEAD, (h + 1) * HEAD)
                bs = slice((n * N_HEAD + h) * CHUNK, (n * N_HEAD + h + 1) * CHUNK)
                o = _dot(pc_s[bs, :], v_ref[rs, cs]) + _dot(pp_s[bs, :], prev_kv(n, cs)[1])
                o_ref[rs, cs] = o.astype(BF16)
            l_ref[rs, :] = _put_cols([lse[(n * N_HEAD + h) * CHUNK:(n * N_HEAD + h + 1) * CHUNK]
                                      for h in range(N_HEAD)])

    def halo(cb):
        return pl.BlockSpec((CHUNK, WIDTH), lambda i: (jnp.maximum(i * nb - 1, 0), cb))

    return pl.pallas_call(
        body, name=f"attn_fwd_{bps}",
        grid=(S // tm,),
        in_specs=[_rows(tm, WIDTH, qcb), _rows(tm, WIDTH, kcb), _rows(tm, WIDTH, vcb), halo(kcb), halo(vcb)],
        out_specs=[_rows(tm, WIDTH), _rows(tm, 128)],
        out_shape=[jax.ShapeDtypeStruct((S, WIDTH), BF16), jax.ShapeDtypeStruct((S, 128), F32)],
        scratch_shapes=[pltpu.VMEM((nblocks * CHUNK, CHUNK), F32), pltpu.VMEM((nblocks * CHUNK, CHUNK), F32),
                        pltpu.VMEM((nblocks * CHUNK, CHUNK), BF16), pltpu.VMEM((nblocks * CHUNK, CHUNK), BF16)],
        compiler_params=_params(("parallel",)),
    )(q, k, v, k, v)


def _gate_specs(tm):
    return [pl.BlockSpec((tm, D_MODEL), lambda i, b=b: (i, b)) for b in range(4)]


Y_SLOT = (0, 1, 3, 2)


def _merge_fwd(x, y4, o_g, l_g, proj, gates, wb, wout, shards=(), half=()):
    S = x.shape[0]
    tm = 512
    n, nh = len(shards), len(half)
    nsteps = S // tm
    forward_at = nsteps // 2

    def body(x_ref, y_ref, o0, o1, o2, l0, l1, l2, cg_ref, *rest):
        gm = rest[:4]
        wb_ref, wo_ref = rest[4:6]
        s_in = rest[6:6 + n]
        rest = rest[6 + n + nh:]
        xn_ref, yc_ref, oc_ref, lse_ref, z_ref = rest[:5]
        s_out, bufs, ocs, sems = rest[5:5 + n], rest[5:5 + n + nh], rest[5 + n + nh], rest[6 + n + nh:]
        i = pl.program_id(0)

        if n:
            @pl.when(i == 0)
            def _():
                _comm_start(_ag_first(s_in, s_out, *sems[:3]))

            @pl.when(i == forward_at)
            def _():
                incoming = _ag_first(s_in, s_out, *sems[:3])[2]
                for a in range(n):
                    for k in range(1, 4):
                        incoming[4 * a + k].wait_recv()
                _comm_start(_ag_second(bufs, *sems[3:]))

        lcols = []
        for h in range(N_HEAD):
            cs = slice(h * HEAD, (h + 1) * HEAD)
            ls = [_col(l[...], h) for l in (l0, l1, l2)]
            m = jnp.maximum(jnp.maximum(ls[0], ls[1]), ls[2])
            tot = jnp.exp(ls[0] - m) + jnp.exp(ls[1] - m) + jnp.exp(ls[2] - m)
            lse = m + jnp.log(tot)
            ocs[:, cs] = sum(jnp.exp(lg - lse) * o[:, cs].astype(F32) for lg, o in zip(ls, (o0, o1, o2)))
            lcols.append(lse)
        lse_ref[...] = _put_cols(lcols)
        oc = ocs[...]
        oc_ref[...] = oc.astype(BF16)
        yc = (oc * _silu(cg_ref[...].astype(F32))).astype(BF16)
        yc_ref[...] = yc
        ys = (y_ref[0], y_ref[1], yc, y_ref[2])
        z = jnp.zeros((tm, D_MODEL), F32)
        for b in range(4):
            z = z + _sigmoid(gm[b][...].astype(F32)) * _dot(ys[b], wb_ref[b])
        zb = z.astype(BF16)
        z_ref[...] = zb
        xn_ref[...] = x_ref[...] + _dot(zb, wo_ref[...])

        if n:
            @pl.when(i == nsteps - 1)
            def _():
                local, out, incoming = _ag_first(s_in, s_out, *sems[:3])
                for a in range(n):
                    incoming[4 * a].wait_recv()
                _comm_wait(_ag_second(bufs, *sems[3:]))
                for cp in out:
                    cp.wait_send()
                for cp in local:
                    cp.wait()

    res = pl.pallas_call(
        body, name="merge_fwd_gather" if n else "merge_fwd",
        grid=(nsteps,),
        in_specs=[_rows(tm, D_MODEL), pl.BlockSpec((3, tm, WIDTH), lambda i: (0, i, 0)),
                  _rows(tm, WIDTH), _rows(tm, WIDTH), _rows(tm, WIDTH),
                  _rows(tm, 128), _rows(tm, 128), _rows(tm, 128),
                  _rows(tm, WIDTH, CB_CGATE)] + _gate_specs(tm)
                 + [_resident((4, WIDTH, D_MODEL)), _resident((D_MODEL, D_MODEL))] + [ANY] * (n + nh),
        out_specs=[_rows(tm, D_MODEL), pl.BlockSpec((None, tm, WIDTH), lambda i: (Y_SLOT[2], i, 0)),
                   _rows(tm, WIDTH), _rows(tm, 128), _rows(tm, D_MODEL)] + [ANY] * (n + nh),
        out_shape=[jax.ShapeDtypeStruct((S, D_MODEL), F32), jax.ShapeDtypeStruct(y4.shape, BF16),
                   jax.ShapeDtypeStruct((S, WIDTH), BF16), jax.ShapeDtypeStruct((S, 128), F32),
                   jax.ShapeDtypeStruct((S, D_MODEL), BF16)]
                  + [jax.ShapeDtypeStruct((N_DEV,) + s.shape, s.dtype) for s in shards]
                  + [jax.ShapeDtypeStruct(b.shape, b.dtype) for b in half],
        input_output_aliases={1: 1, **{15 + n + a: 5 + n + a for a in range(nh)}},
        scratch_shapes=[pltpu.VMEM((tm, WIDTH), F32)]
                       + (_dma_sems(4 * n, 4 * n, n, 3 * (n + nh), 3 * (n + nh)) if n else []),
        compiler_params=_params(("arbitrary",)),
    )(x, y4, *o_g, *l_g, proj, *([gates] * 4), wb, wout, *shards, *half)
    return res[:5], list(res[5:])


def _loss_head(x, g, tgt):
    S = x.shape[0]
    tm = 512

    def body(x_ref, g_ref, t_ref, loss_ref, dx_ref, dg_ref):
        @pl.when(pl.program_id(0) == 0)
        def _():
            loss_ref[...] = jnp.zeros_like(loss_ref)
            dg_ref[...] = jnp.zeros_like(dg_ref)

        xf = x_ref[...]
        r = lax.rsqrt(jnp.mean(xf * xf, axis=-1, keepdims=True) + EPS)
        xhat = xf * r
        gv = g_ref[...]
        err = xhat * gv - t_ref[...]
        e2 = jnp.sum(err * err, axis=-1, keepdims=True)
        loss_ref[...] += (0.5 / D_MODEL) * jnp.sum(e2, axis=0, keepdims=True)
        dy = err * (1.0 / D_MODEL)
        dg_ref[...] += jnp.sum(dy * xhat, axis=0, keepdims=True)
        dxh = dy * gv
        dx_ref[...] = r * (dxh - xhat * jnp.mean(dxh * xhat, axis=-1, keepdims=True))

    return pl.pallas_call(
        body, name="loss_head",
        grid=(S // tm,),
        in_specs=[_rows(tm, D_MODEL), _full((1, D_MODEL)), _rows(tm, D_MODEL)],
        out_specs=[_full((1, 128)), _rows(tm, D_MODEL), _full((1, D_MODEL))],
        out_shape=[jax.ShapeDtypeStruct((1, 128), F32), jax.ShapeDtypeStruct((S, D_MODEL), F32),
                   jax.ShapeDtypeStruct((1, D_MODEL), F32)],
        compiler_params=_params(("arbitrary",)),
    )(x, g, tgt)


def _merge_bwd(dxo, y4, oc, z, proj, gates, wb, wout, grads=()):
    S = dxo.shape[0]
    tm = 256
    n = len(grads)
    nsteps = S // tm

    def body(dx_ref, y_ref, oc_ref, z_ref, cg_ref, *rest):
        gm = rest[:4]
        wb_ref, wo_ref = rest[4:6]
        g_in = rest[6:6 + n]
        dy_ref, doc_ref, delta_ref, dcg_ref, dgm_ref, dwb_ref, dwo_ref = rest[6 + n:13 + n]
        g_out = rest[13 + n:13 + 2 * n]
        acc_b, acc_o = rest[13 + 2 * n:15 + 2 * n]
        sems = rest[15 + 2 * n:]
        i = pl.program_id(0)

        @pl.when(i == 0)
        def _():
            acc_b[...] = jnp.zeros_like(acc_b)
            acc_o[...] = jnp.zeros_like(acc_o)
            if n:
                _comm_start(_rs_first(g_in, g_out, *sems))

        dxb = dx_ref[...].astype(BF16)
        acc_o[...] += _dot_tn(z_ref[...], dxb)
        dz = _dot_nt(dxb, wo_ref[...])
        for b in range(4):
            gate = _sigmoid(gm[b][...].astype(F32))
            yb = y_ref[Y_SLOT[b]]
            t = _dot(yb, wb_ref[b])
            dgm_ref[:, b * D_MODEL:(b + 1) * D_MODEL] = (dz * t * gate * (1.0 - gate)).astype(BF16)
            dt = (dz * gate).astype(BF16)
            acc_b[b] += _dot_tn(yb, dt)
            dyb = _dot_nt(dt, wb_ref[b])
            if b == 2:
                cg = cg_ref[...].astype(F32)
                oc = oc_ref[...].astype(F32)
                scg, dscg = _silu_and_grad(cg)
                doc = dyb * scg
                dcg_ref[...] = (dyb * oc * dscg).astype(BF16)
                doc_ref[...] = doc.astype(BF16)
                prod = doc * oc
                delta_ref[...] = _put_cols([jnp.sum(prod[:, h * HEAD:(h + 1) * HEAD], axis=1, keepdims=True)
                                            for h in range(N_HEAD)])
            else:
                dy_ref[b if b < 2 else 2] = dyb.astype(BF16)

        @pl.when(i == nsteps - 1)
        def _():
            dwb_ref[...] = acc_b[...].astype(BF16)
            dwo_ref[...] = acc_o[...].astype(BF16)
            if n:
                _comm_wait(_rs_first(g_in, g_out, *sems))

    res = pl.pallas_call(
        body, name="merge_bwd_scatter" if n else "merge_bwd",
        grid=(nsteps,),
        in_specs=[_rows(tm, D_MODEL), pl.BlockSpec((4, tm, WIDTH), lambda i: (0, i, 0)),
                  _rows(tm, WIDTH), _rows(tm, D_MODEL), _rows(tm, WIDTH, CB_CGATE)] + _gate_specs(tm)
                 + [_resident((4, WIDTH, D_MODEL)), _resident((D_MODEL, D_MODEL))] + [ANY] * n,
        out_specs=[pl.BlockSpec((3, tm, WIDTH), lambda i: (0, i, 0)), _rows(tm, WIDTH), _rows(tm, 128),
                   _rows(tm, WIDTH), _rows(tm, 4 * D_MODEL), _full((4, WIDTH, D_MODEL)), _full((D_MODEL, D_MODEL))]
                  + [ANY] * n,
        out_shape=[jax.ShapeDtypeStruct((3, S, WIDTH), BF16), jax.ShapeDtypeStruct((S, WIDTH), BF16),
                   jax.ShapeDtypeStruct((S, 128), F32), jax.ShapeDtypeStruct((S, WIDTH), BF16),
                   jax.ShapeDtypeStruct((S, 4 * D_MODEL), BF16), jax.ShapeDtypeStruct((4, WIDTH, D_MODEL), BF16),
                   jax.ShapeDtypeStruct((D_MODEL, D_MODEL), BF16)]
                  + [jax.ShapeDtypeStruct(g.shape[:1] + g.shape[2:], g.dtype) for g in grads],
        scratch_shapes=[pltpu.VMEM((4, WIDTH, D_MODEL), F32), pltpu.VMEM((D_MODEL, D_MODEL), F32)]
                       + (_dma_sems(N_CHIP * n, N_CHIP * n) if n else []),
        compiler_params=_params(("arbitrary",)),
    )(dxo, y4, oc, z, proj, *([gates] * 4), wb, wout, *grads)
    return res[:7], list(res[7:])


def _attn_bwd(q, qcb, k, kcb, v, vcb, do, lse, delta, bps, dst, dcb):
    S = q.shape[0]
    tm = ATT_TILE
    nb = tm // CHUNK
    nblk = S // CHUNK

    ncur = nb * N_HEAD
    nprev = (nb + 1) * N_HEAD

    def body(q_ref, k_ref, v_ref, do_ref, l_ref, d_ref, kh_ref, vh_ref, qn_ref, don_ref, ln_ref, dn_ref, dst_ref,
             dq_ref, dk_ref, dv_ref, sc_s, sp_s, dpc_s, dpp_s, pc_s, pp_s, dsc_s, dsp_s):
        i = pl.program_id(0)

        def rows_of(n):
            if n < nb:
                rs = slice(n * CHUNK, (n + 1) * CHUNK)
                return rs, q_ref, do_ref, l_ref, d_ref
            return slice(0, CHUNK), qn_ref, don_ref, ln_ref, dn_ref

        def prev_kv(n, cs):
            if n == 0:
                return kh_ref[:, cs], vh_ref[:, cs]
            ps = slice((n - 1) * CHUNK, n * CHUNK)
            return k_ref[ps, cs], v_ref[ps, cs]

        def blk(n, h):
            return slice((n * N_HEAD + h) * CHUNK, (n * N_HEAD + h + 1) * CHUNK)

        pens, lses, deltas = [], [], []
        for n in range(nb + 1):
            rs, qr, dor, lr, dr = rows_of(n)
            gb = i * nb + n
            pen = jnp.where(gb % bps != 0, 0.0, NEG)
            if n == nb:
                pen = pen + jnp.where(gb < nblk, 0.0, NEG)
            pens.append(jnp.full((N_HEAD * CHUNK, 1), pen, F32))
            lblk, dblk = lr[rs, :], dr[rs, :]
            for h in range(N_HEAD):
                cs = slice(h * HEAD, (h + 1) * HEAD)
                qh, doh = qr[rs, cs], dor[rs, cs]
                lses.append(_col(lblk, h))
                deltas.append(_col(dblk, h))
                kp, vp = prev_kv(n, cs)
                sp_s[blk(n, h), :] = _dot_nt(qh, kp)
                dpp_s[blk(n, h), :] = _dot_nt(doh, vp)
                if n < nb:
                    sc_s[blk(n, h), :] = _dot_nt(qh, k_ref[rs, cs])
                    dpc_s[blk(n, h), :] = _dot_nt(doh, v_ref[rs, cs])
        lse = jnp.concatenate(lses, axis=0)
        delta = jnp.concatenate(deltas, axis=0)
        row = lax.broadcasted_iota(jnp.int32, (nprev * CHUNK, CHUNK), 0) & (CHUNK - 1)
        col = lax.broadcasted_iota(jnp.int32, (nprev * CHUNK, CHUNK), 1)
        sp = jnp.where(col >= row, sp_s[...] * ATT_SCALE, NEG) + jnp.concatenate(pens, axis=0)
        pp = jnp.exp(sp - lse)
        pp_s[...] = pp.astype(BF16)
        dsp_s[...] = (pp * (dpp_s[...] - delta)).astype(BF16)
        nc = ncur * CHUNK
        sc = jnp.where(col[:nc] <= row[:nc], sc_s[...] * ATT_SCALE, NEG)
        pc = jnp.exp(sc - lse[:nc])
        pc_s[...] = pc.astype(BF16)
        dsc_s[...] = (pc * (dpc_s[...] - delta[:nc])).astype(BF16)
        for n in range(nb):
            rs, qr, dor, _, _ = rows_of(n)
            rn, qnr, donr, _, _ = rows_of(n + 1)
            for h in range(N_HEAD):
                cs = slice(h * HEAD, (h + 1) * HEAD)
                kp, _ = prev_kv(n, cs)
                dq = _dot(dsc_s[blk(n, h), :], k_ref[rs, cs]) + _dot(dsp_s[blk(n, h), :], kp)
                dq_ref[rs, cs] = (dq * ATT_SCALE).astype(BF16)
                dk = _dot_tn(dsc_s[blk(n, h), :], qr[rs, cs]) + _dot_tn(dsp_s[blk(n + 1, h), :], qnr[rn, cs])
                dk_ref[rs, cs] = (dk * ATT_SCALE).astype(BF16)
                dv = _dot_tn(pc_s[blk(n, h), :], dor[rs, cs]) + _dot_tn(pp_s[blk(n + 1, h), :], donr[rn, cs])
                dv_ref[rs, cs] = dv.astype(BF16)

    def prev_halo(cb):
        return pl.BlockSpec((CHUNK, WIDTH), lambda i: (jnp.maximum(i * nb - 1, 0), cb))

    def next_halo(width, cb=0):
        return pl.BlockSpec((CHUNK, width), lambda i: (jnp.minimum(i * nb + nb, nblk - 1), cb))

    return pl.pallas_call(
        body, name=f"attn_bwd_{bps}",
        grid=(S // tm,),
        in_specs=[_rows(tm, WIDTH, qcb), _rows(tm, WIDTH, kcb), _rows(tm, WIDTH, vcb), _rows(tm, WIDTH),
                  _rows(tm, 128), _rows(tm, 128), prev_halo(kcb), prev_halo(vcb),
                  next_halo(WIDTH, qcb), next_halo(WIDTH), next_halo(128), next_halo(128), ANY],
        out_specs=[_rows(tm, WIDTH, dcb), _rows(tm, WIDTH), _rows(tm, WIDTH)],
        out_shape=[jax.ShapeDtypeStruct(dst.shape, BF16)] + [jax.ShapeDtypeStruct((S, WIDTH), BF16)] * 2,
        input_output_aliases={12: 0},
        scratch_shapes=[pltpu.VMEM((ncur * CHUNK, CHUNK), F32), pltpu.VMEM((nprev * CHUNK, CHUNK), F32),
                        pltpu.VMEM((ncur * CHUNK, CHUNK), F32), pltpu.VMEM((nprev * CHUNK, CHUNK), F32),
                        pltpu.VMEM((ncur * CHUNK, CHUNK), BF16), pltpu.VMEM((nprev * CHUNK, CHUNK), BF16),
                        pltpu.VMEM((ncur * CHUNK, CHUNK), BF16), pltpu.VMEM((nprev * CHUNK, CHUNK), BF16)],
        compiler_params=_params(("parallel",)),
    )(q, k, v, do, lse, delta, k, v, q, do, lse, delta, dst)


def _dilated_split(d):
    hp = min(N_HEAD, 16 // d)
    return hp, N_HEAD // hp, HEAD * hp


def _strided_regroup(d):
    return d < 16


def _by_class(src_ref, dst, d, hp, nat):
    for j in range(hp):
        if _strided_regroup(d):
            nat[j] = src_ref[:, j * HEAD:(j + 1) * HEAD].astype(F32)
            for r in range(d):
                dst[j, r * CHUNK:(r + 1) * CHUNK, :] = nat.at[j][pl.ds(r, CHUNK, stride=d), :].astype(BF16)
        else:
            dst[j] = pltpu.einshape("(tr)l->(rt)l", src_ref[:, j * HEAD:(j + 1) * HEAD], r=d)


def _from_class(src, dst_ref, d, hp, nat, add_ref=None):
    for j in range(hp):
        cs = slice(j * HEAD, (j + 1) * HEAD)
        if _strided_regroup(d):
            for r in range(d):
                nat.at[j][pl.ds(r, CHUNK, stride=d), :] = src[j, r * CHUNK:(r + 1) * CHUNK, :]
            val = nat[j].astype(BF16)
        else:
            val = pltpu.einshape("(rt)l->(tr)l", src[j].astype(BF16), r=d)
        if add_ref is not None:
            val = (val.astype(F32) + add_ref[:, cs].astype(F32)).astype(BF16)
        dst_ref[:, cs] = val


def _attn_fwd_dilated(proj, qcb, kcb, vcb, d):
    S = proj.shape[0]
    T = CHUNK * d
    hp, nh, cw = _dilated_split(d)
    nblocks = d * hp

    def body(q_ref, k_ref, v_ref, o_ref, l_ref, qf, kst, vst, of, lf, nat, sc_s, sp_s, pc_s, pp_s):
        i, hh = pl.program_id(0), pl.program_id(1)
        kf, vf = kst.at[i % 2, hh], vst.at[i % 2, hh]
        kpf, vpf = kst.at[1 - i % 2, hh], vst.at[1 - i % 2, hh]

        @pl.when(i == 0)
        def _():
            kpf[...] = jnp.zeros_like(kpf)
            vpf[...] = jnp.zeros_like(vpf)

        _by_class(q_ref, qf, d, hp, nat)
        _by_class(k_ref, kf, d, hp, nat)
        _by_class(v_ref, vf, d, hp, nat)

        def blk(ref, r, j):
            return ref[j, r * CHUNK:(r + 1) * CHUNK, :]

        def bs(r, j):
            return slice((r * hp + j) * CHUNK, (r * hp + j + 1) * CHUNK)

        for r in range(d):
            for j in range(hp):
                qb = blk(qf, r, j)
                sc_s[bs(r, j), :] = _dot_nt(qb, blk(kf, r, j))
                sp_s[bs(r, j), :] = _dot_nt(qb, blk(kpf, r, j))
        row = lax.broadcasted_iota(jnp.int32, (nblocks * CHUNK, CHUNK), 0) & (CHUNK - 1)
        col = lax.broadcasted_iota(jnp.int32, (nblocks * CHUNK, CHUNK), 1)
        sc = jnp.where(col <= row, sc_s[...] * ATT_SCALE, NEG)
        sp = jnp.where(col >= row, sp_s[...] * ATT_SCALE, NEG) + jnp.where(i > 0, 0.0, NEG)
        m = jnp.maximum(jnp.max(sc, axis=-1, keepdims=True), jnp.max(sp, axis=-1, keepdims=True))
        ec = jnp.exp(sc - m)
        ep = jnp.exp(sp - m)
        den = jnp.sum(ec, axis=-1, keepdims=True) + jnp.sum(ep, axis=-1, keepdims=True)
        inv = 1.0 / den
        pc_s[...] = (ec * inv).astype(BF16)
        pp_s[...] = (ep * inv).astype(BF16)
        lse = m + jnp.log(den)
        lane = lax.broadcasted_iota(jnp.int32, (CHUNK, 128), 1)
        for r in range(d):
            lblk = jnp.zeros((CHUNK, 128), F32)
            for j in range(hp):
                o = _dot(pc_s[bs(r, j), :], blk(vf, r, j)) + _dot(pp_s[bs(r, j), :], blk(vpf, r, j))
                of[j, r * CHUNK:(r + 1) * CHUNK, :] = o
                lblk = jnp.where(lane == hh * hp + j, lse[bs(r, j)], lblk)
            lf[r * CHUNK:(r + 1) * CHUNK, :] = lblk
        _from_class(of, o_ref, d, hp, nat)
        lnat = pltpu.einshape("(rt)l->(tr)l", lf[...], r=d)

        @pl.when(hh == 0)
        def _():
            l_ref[...] = lnat

        @pl.when(hh > 0)
        def _():
            l_ref[...] += lnat

    def cols(cb):
        return pl.BlockSpec((T, cw), lambda i, hh: (i, cb * nh + hh))

    tile = pltpu.VMEM((hp, T, HEAD), BF16)
    staging = pltpu.VMEM((hp, T, HEAD) if _strided_regroup(d) else (1, 8, HEAD), F32)
    return pl.pallas_call(
        body, name=f"attn_fwd_dilated_{d}",
        grid=(S // T, nh),
        in_specs=[cols(qcb), cols(kcb), cols(vcb)],
        out_specs=[cols(0), pl.BlockSpec((T, 128), lambda i, hh: (i, 0))],
        out_shape=[jax.ShapeDtypeStruct((S, WIDTH), BF16), jax.ShapeDtypeStruct((S, 128), F32)],
        scratch_shapes=[tile, pltpu.VMEM((2, nh, hp, T, HEAD), BF16), pltpu.VMEM((2, nh, hp, T, HEAD), BF16),
                        pltpu.VMEM((hp, T, HEAD), F32), pltpu.VMEM((T, 128), F32), staging,
                        pltpu.VMEM((nblocks * CHUNK, CHUNK), F32), pltpu.VMEM((nblocks * CHUNK, CHUNK), F32),
                        pltpu.VMEM((nblocks * CHUNK, CHUNK), BF16), pltpu.VMEM((nblocks * CHUNK, CHUNK), BF16)],
        compiler_params=_params(("arbitrary", "arbitrary")),
    )(proj, proj, proj)


def _attn_bwd_dilated(proj, qcb, kcb, vcb, do, lse, delta, d, dk_in, dv_in, dst, dcb):
    S = proj.shape[0]
    T = CHUNK * d
    nt = S // T
    hp, nh, cw = _dilated_split(d)
    nblocks = d * hp

    def body(q_ref, k_ref, v_ref, do_ref, l_ref, d_ref, dki_ref, dvi_ref, dst_ref, dq_ref, dk_ref, dv_ref,
             qf, dof, kbuf, vbuf, dqf, gk, gv, nat,
             sc_s, sp_s, dpc_s, dpp_s, pc_s, pp_s, dsc_s, dsp_s):
        hh, i = pl.program_id(0), pl.program_id(1)
        kf, vf, newk, newv = kbuf.at[i % 2], vbuf.at[i % 2], gk.at[i % 2], gv.at[i % 2]
        kpf, vpf, acck, accv = kbuf.at[1 - i % 2], vbuf.at[1 - i % 2], gk.at[1 - i % 2], gv.at[1 - i % 2]

        @pl.when(i == 0)
        def _():
            for ref in (kbuf, vbuf, gk, gv):
                ref[...] = jnp.zeros_like(ref)
            dk_ref[...] = jnp.zeros_like(dk_ref)
            dv_ref[...] = jnp.zeros_like(dv_ref)

        def blk(ref, r, j):
            return ref[j, r * CHUNK:(r + 1) * CHUNK, :]

        def bs(r, j):
            return slice((r * hp + j) * CHUNK, (r * hp + j + 1) * CHUNK)

        @pl.when(i < nt)
        def _():
            _by_class(q_ref, qf, d, hp, nat)
            _by_class(do_ref, dof, d, hp, nat)
            _by_class(k_ref, kf, d, hp, nat)
            _by_class(v_ref, vf, d, hp, nat)
            lses, deltas = [], []
            lcls = pltpu.einshape("(tr)l->(rt)l", l_ref[...], r=d)
            dcls = pltpu.einshape("(tr)l->(rt)l", d_ref[...], r=d)
            for r in range(d):
                lblk = lcls[r * CHUNK:(r + 1) * CHUNK]
                dblk = dcls[r * CHUNK:(r + 1) * CHUNK]
                for j in range(hp):
                    lses.append(_col(lblk, hh * hp + j))
                    deltas.append(_col(dblk, hh * hp + j))
                    qb, dob = blk(qf, r, j), blk(dof, r, j)
                    sc_s[bs(r, j), :] = _dot_nt(qb, blk(kf, r, j))
                    dpc_s[bs(r, j), :] = _dot_nt(dob, blk(vf, r, j))
                    sp_s[bs(r, j), :] = _dot_nt(qb, blk(kpf, r, j))
                    dpp_s[bs(r, j), :] = _dot_nt(dob, blk(vpf, r, j))
            lse = jnp.concatenate(lses, axis=0)
            delta = jnp.concatenate(deltas, axis=0)
            row = lax.broadcasted_iota(jnp.int32, (nblocks * CHUNK, CHUNK), 0) & (CHUNK - 1)
            col = lax.broadcasted_iota(jnp.int32, (nblocks * CHUNK, CHUNK), 1)
            sp = jnp.where(col >= row, sp_s[...] * ATT_SCALE, NEG) + jnp.where(i > 0, 0.0, NEG)
            pp = jnp.exp(sp - lse)
            pp_s[...] = pp.astype(BF16)
            dsp_s[...] = (pp * (dpp_s[...] - delta)).astype(BF16)
            sc = jnp.where(col <= row, sc_s[...] * ATT_SCALE, NEG)
            pc = jnp.exp(sc - lse)
            pc_s[...] = pc.astype(BF16)
            dsc_s[...] = (pc * (dpc_s[...] - delta)).astype(BF16)
            for r in range(d):
                rows = slice(r * CHUNK, (r + 1) * CHUNK)
                for j in range(hp):
                    qb, dob = blk(qf, r, j), blk(dof, r, j)
                    dsc, dsp = dsc_s[bs(r, j), :], dsp_s[bs(r, j), :]
                    dqf[j, rows, :] = (_dot(dsc, blk(kf, r, j)) + _dot(dsp, blk(kpf, r, j))) * ATT_SCALE
                    newk[j, rows, :] = _dot_tn(dsc, qb) * ATT_SCALE
                    newv[j, rows, :] = _dot_tn(pc_s[bs(r, j), :], dob)
                    acck[j, rows, :] += _dot_tn(dsp, qb) * ATT_SCALE
                    accv[j, rows, :] += _dot_tn(pp_s[bs(r, j), :], dob)
            _from_class(dqf, dq_ref, d, hp, nat)

        @pl.when(i > 0)
        def _():
            _from_class(acck, dk_ref, d, hp, nat, dki_ref)
            _from_class(accv, dv_ref, d, hp, nat, dvi_ref)

    def cur(width, cb, nsplit):
        return pl.BlockSpec((T, width), lambda hh, i: (jnp.minimum(i, nt - 1), cb * nsplit + hh * (nsplit > 1)))

    def lag():
        return pl.BlockSpec((T, cw), lambda hh, i: (jnp.maximum(i - 1, 0), hh))

    tile = pltpu.VMEM((hp, T, HEAD), BF16)
    acc = pltpu.VMEM((hp, T, HEAD), F32)
    f32s = pltpu.VMEM((nblocks * CHUNK, CHUNK), F32)
    b16s = pltpu.VMEM((nblocks * CHUNK, CHUNK), BF16)
    return pl.pallas_call(
        body, name=f"attn_bwd_dilated_{d}",
        grid=(nh, nt + 1),
        in_specs=[cur(cw, qcb, nh), cur(cw, kcb, nh), cur(cw, vcb, nh), cur(cw, 0, nh), cur(128, 0, 1), cur(128, 0, 1),
                  lag(), lag(), ANY],
        out_specs=[cur(cw, dcb, nh), lag(), lag()],
        out_shape=[jax.ShapeDtypeStruct(dst.shape, BF16)] + [jax.ShapeDtypeStruct((S, WIDTH), BF16)] * 2,
        input_output_aliases={8: 0},
        scratch_shapes=[tile, tile, pltpu.VMEM((2, hp, T, HEAD), BF16), pltpu.VMEM((2, hp, T, HEAD), BF16), acc,
                        pltpu.VMEM((2, hp, T, HEAD), F32), pltpu.VMEM((2, hp, T, HEAD), F32),
                        acc if _strided_regroup(d) else pltpu.VMEM((1, 8, HEAD), F32)]
                       + [f32s] * 4 + [b16s] * 4,
        compiler_params=_params(("arbitrary", "arbitrary")),
    )(proj, proj, proj, do, lse, delta, dk_in, dv_in, dst)


def _abm_bwd(proj, cdf, dy3, ln_g, ln_b, wsm, wsm_t, bias_full, pool_w, pool_wt, pool_scale, kv, bands, bands_t):
    S = proj.shape[0]
    tm = 512
    nchunk = tm // CHUNK
    nblk = S // CHUNK

    def body(u_ref, v_ref, ag_ref, p_ref, ph_ref, pg_ref, pgn_ref, mq_ref, mg_ref, cdf_ref, dy_ref, dypn_ref,
             lng_ref, lnb_ref, wsm_ref, wsmt_ref, bias_ref, pw_ref, pwt_ref, ps_ref, kv_ref, band_ref, bandt_ref,
             dab_ref, dm_ref, dlng_ref, dlnb_ref, dws_ref, dbias_ref, dpw_ref, dps_ref, dkv_ref,
             mix, dvl, ddn):
        i = pl.program_id(0)

        @pl.when(i == 0)
        def _():
            for r in (dlng_ref, dlnb_ref, dws_ref, dbias_ref, dpw_ref, dps_ref, dkv_ref):
                r[...] = jnp.zeros_like(r)

        au = u_ref[...].astype(F32)
        av = v_ref[...].astype(F32)
        ag = ag_ref[...].astype(F32)
        u, du = _gelu_and_grad(au, cdf_ref[0].astype(F32))
        v, dgelu_v = _gelu_and_grad(av, cdf_ref[1].astype(F32))
        vhat, rstd = _layer_norm_fwd(v)
        vln = (vhat * lng_ref[...] + lnb_ref[...]).astype(BF16)
        for c in range(nchunk):
            for h in range(N_HEAD):
                rs, cs = slice(c * CHUNK, (c + 1) * CHUNK), slice(h * HEAD, (h + 1) * HEAD)
                mix[rs, cs] = _dot(wsm_ref[h], vln[rs, cs]) + bias_ref[:, cs]
        dya = dy_ref[0].astype(F32)
        sg, dsg = _silu_and_grad(ag)
        mixed = mix[...]
        dab_ref[:, 2 * WIDTH:3 * WIDTH] = (dya * u * mixed * dsg).astype(BF16)
        dab_ref[:, 0:WIDTH] = (dya * mixed * sg * du).astype(BF16)
        dmixed = dya * u * sg
        dmb = dmixed.astype(BF16)
        tril = (lax.broadcasted_iota(jnp.int32, (CHUNK, CHUNK), 1)
                <= lax.broadcasted_iota(jnp.int32, (CHUNK, CHUNK), 0))
        for c in range(nchunk):
            rs = slice(c * CHUNK, (c + 1) * CHUNK)
            dbias_ref[...] += dmixed[rs, :]
            for h in range(N_HEAD):
                cs = slice(h * HEAD, (h + 1) * HEAD)
                dvl[rs, cs] = _dot(wsmt_ref[h], dmb[rs, cs])
                dws_ref[h] += jnp.where(tril, _dot_nt(dmb[rs, cs], vln[rs, cs]), 0.0)
        dvln = dvl[...]
        dlng_ref[...] += jnp.sum(dvln * vhat, axis=0, keepdims=True)
        dlnb_ref[...] += jnp.sum(dvln, axis=0, keepdims=True)
        dvh = dvln * lng_ref[...]
        dv = rstd * (dvh - jnp.mean(dvh, axis=-1, keepdims=True)
                     - vhat * jnp.mean(dvh * vhat, axis=-1, keepdims=True))
        dab_ref[:, WIDTH:2 * WIDTH] = (dv * dgelu_v).astype(BF16)

        halo_ok = (i > 0).astype(F32)
        for c in range(nchunk):
            rs = slice(c * CHUNK, (c + 1) * CHUNK)
            for g, win in enumerate(POOL_WINDOWS):
                cs = slice(g * HEAD, (g + 1) * HEAD)
                cur = p_ref[rs, cs]
                if c == 0:
                    prev = (ph_ref[:, cs].astype(F32) * halo_ok).astype(BF16)
                else:
                    prev = p_ref[(c - 1) * CHUNK:c * CHUNK, cs]
                sums = _dot(band_ref[g, 0], cur) + _dot(band_ref[g, 1], prev)
                dvl[rs, cs] = sums * _inv_count(i * tm + c * CHUNK, win) - cur.astype(F32)
        dmat = dvl[...].astype(BF16)
        for g in range(4):
            cs = slice(g * HEAD, (g + 1) * HEAD)
            mix[:, cs] = _dot(dmat[:, cs], pw_ref[g])
        yg = mix[...]
        pg = pg_ref[...].astype(F32)
        dyp = dy_ref[1].astype(F32)
        spg, dspg = _silu_and_grad(pg)
        dyy = dyp * spg
        scale = ps_ref[...]
        dab_ref[:, 4 * WIDTH:5 * WIDTH] = (dyp * yg * scale * dspg).astype(BF16)
        dps_ref[...] += jnp.sum(dyy * yg, axis=0, keepdims=True)
        dyg = (dyy * scale).astype(BF16)
        for g in range(4):
            cs = slice(g * HEAD, (g + 1) * HEAD)
            dpw_ref[g] += _dot_tn(dmat[:, cs], dyg[:, cs])
            mix[:, cs] = _dot(dyg[:, cs], pwt_ref[g])
        next_ok = (i + 1 < S // tm).astype(F32)
        dygn = (dypn_ref[...].astype(F32) * _silu(pgn_ref[...].astype(F32)) * scale * next_ok).astype(BF16)
        for c in range(nchunk + 1):
            for g, win in enumerate(POOL_WINDOWS):
                cs = slice(g * HEAD, (g + 1) * HEAD)
                if c < nchunk:
                    dd = mix[c * CHUNK:(c + 1) * CHUNK, cs]
                else:
                    dd = _dot(dygn[:, cs], pwt_ref[g])
                ddn[c * CHUNK:(c + 1) * CHUNK, cs] = dd * _inv_count(i * tm + c * CHUNK, win)
        ddnb = ddn[...].astype(BF16)
        for c in range(nchunk):
            rs = slice(c * CHUNK, (c + 1) * CHUNK)
            ns = slice((c + 1) * CHUNK, (c + 2) * CHUNK)
            for g, win in enumerate(POOL_WINDOWS):
                cs = slice(g * HEAD, (g + 1) * HEAD)
                dp = _dot(bandt_ref[g, 0], ddnb[rs, cs]) + _dot(bandt_ref[g, 1], ddnb[ns, cs]) - mix[rs, cs]
                dab_ref[rs, 3 * WIDTH + g * HEAD:3 * WIDTH + (g + 1) * HEAD] = dp.astype(BF16)

        mg = mg_ref[...].astype(F32)
        dym = dy_ref[2].astype(F32)
        smg, dsmg = _silu_and_grad(mg)
        dob = (dym * smg).astype(BF16)
        for h in range(N_HEAD):
            cs = slice(h * HEAD, (h + 1) * HEAD)
            vs = slice(WIDTH + h * HEAD, WIDTH + (h + 1) * HEAD)
            qh = mq_ref[:, cs]
            p = _mem_softmax(qh, kv_ref[:, cs])
            pb = p.astype(BF16)
            mix[:, cs] = _dot(pb, kv_ref[:, vs])
            dp = _dot_nt(dob[:, cs], kv_ref[:, vs])
            ds = (p * (dp - jnp.sum(p * dp, axis=-1, keepdims=True))).astype(BF16)
            dm_ref[:, cs] = (_dot(ds, kv_ref[:, cs]) * ATT_SCALE).astype(BF16)
            dkv_ref[:, cs] += _dot_tn(ds, qh) * ATT_SCALE
            dkv_ref[:, vs] += _dot_tn(pb, dob[:, cs])
        dm_ref[:, WIDTH:2 * WIDTH] = (dym * mix[...] * dsmg).astype(BF16)

    blk = tm // CHUNK
    small = [_full((1, WIDTH)), _full((1, WIDTH)), _full((N_HEAD, CHUNK, CHUNK)), _full((CHUNK, WIDTH)),
             _full((4, HEAD, HEAD)), _full((1, WIDTH)), _full((MEM_LEN, 2 * WIDTH))]
    return pl.pallas_call(
        body, name="abm_bwd",
        grid=(S // tm,),
        in_specs=[_rows(tm, WIDTH, CB_U), _rows(tm, WIDTH, CB_V), _rows(tm, WIDTH, CB_AGATE),
                  _rows(tm, WIDTH, CB_PIN),
                  pl.BlockSpec((CHUNK, WIDTH), lambda i: (jnp.maximum(i * blk - 1, 0), CB_PIN)),
                  _rows(tm, WIDTH, CB_PGATE),
                  pl.BlockSpec((CHUNK, WIDTH), lambda i: (jnp.minimum(i * blk + blk, nblk - 1), CB_PGATE)),
                  _rows(tm, WIDTH, CB_MQ), _rows(tm, WIDTH, CB_MGATE),
                  pl.BlockSpec((2, tm, WIDTH), lambda i: (0, i, 0)),
                  pl.BlockSpec((3, tm, WIDTH), lambda i: (0, i, 0)),
                  pl.BlockSpec((None, CHUNK, WIDTH), lambda i: (1, jnp.minimum(i * blk + blk, nblk - 1), 0)),
                  _full((1, WIDTH)), _full((1, WIDTH)), _full((N_HEAD, CHUNK, CHUNK)), _full((N_HEAD, CHUNK, CHUNK)),
                  _full((CHUNK, WIDTH)), _full((4, HEAD, HEAD)), _full((4, HEAD, HEAD)), _full((1, WIDTH)),
                  _full((MEM_LEN, 2 * WIDTH)), _full((4, 2, CHUNK, CHUNK)), _full((4, 2, CHUNK, CHUNK))],
        out_specs=[_rows(tm, 5 * WIDTH), _rows(tm, 2 * WIDTH)] + small,
        out_shape=[jax.ShapeDtypeStruct((S, D_BRANCHES), BF16), jax.ShapeDtypeStruct((S, 2 * WIDTH), BF16),
                   jax.ShapeDtypeStruct((1, WIDTH), F32), jax.ShapeDtypeStruct((1, WIDTH), F32),
                   jax.ShapeDtypeStruct((N_HEAD, CHUNK, CHUNK), F32), jax.ShapeDtypeStruct((CHUNK, WIDTH), F32),
                   jax.ShapeDtypeStruct((4, HEAD, HEAD), F32), jax.ShapeDtypeStruct((1, WIDTH), F32),
                   jax.ShapeDtypeStruct((MEM_LEN, 2 * WIDTH), F32)],
        scratch_shapes=[pltpu.VMEM((tm, WIDTH), F32), pltpu.VMEM((tm, WIDTH), F32),
                        pltpu.VMEM((tm + CHUNK, WIDTH), F32)],
        compiler_params=_params(("arbitrary",)),
    )(proj, proj, proj, proj, proj, proj, proj, proj, proj, cdf, dy3, dy3,
      ln_g, ln_b, wsm, wsm_t, bias_full, pool_w, pool_wt, pool_scale, kv, bands, bands_t)


def _bias_reduce(dbias_full):
    def body(d_ref, o_ref):
        d = d_ref[...]
        o_ref[...] = _put_cols([jnp.sum(d[:, h * HEAD:(h + 1) * HEAD], axis=1, keepdims=True) for h in range(N_HEAD)])

    return pl.pallas_call(body, name="bias_reduce", out_shape=jax.ShapeDtypeStruct((CHUNK, 128), F32))(dbias_full)


def _mem_bwd(mem, g, mem_n, w, dkv):
    def body(m_ref, g_ref, mn_ref, w_ref, dkv_ref, dw_ref, dg_ref):
        dkvb = dkv_ref[...].astype(BF16)
        dw_ref[...] = _dot_tn(mn_ref[...], dkvb).astype(BF16)
        dmn = _dot_nt(dkvb, w_ref[...])
        xf = m_ref[...]
        r = lax.rsqrt(jnp.mean(xf * xf, axis=-1, keepdims=True) + EPS)
        dg_ref[...] = jnp.sum(dmn * xf * r, axis=0, keepdims=True)

    return pl.pallas_call(
        body, name="mem_bwd",
        out_shape=[jax.ShapeDtypeStruct((D_MODEL, 2 * WIDTH), BF16), jax.ShapeDtypeStruct((1, D_MODEL), F32)],
        compiler_params=pltpu.CompilerParams(vmem_limit_bytes=VMEM_LIMIT),
    )(mem, g, mem_n, w, dkv)


def _dh_bwd(dpb, dpg, wt, wgt, x, g, dxo, parts=(), grads=()):
    S = x.shape[0]
    tm, tkb, tkg = 1024, D_BRANCHES // 4, D_GATES // 4
    nkb, nkg = 4, 4
    nk = nkb + nkg
    ni = S // tm
    n, m = len(parts), len(grads)

    def body(dpb_ref, wbr_ref, dpg_ref, wg_ref, x_ref, g_ref, dxo_ref, *rest):
        p_in, g_in = rest[:n], rest[n:n + m]
        dx_ref, dg_ref = rest[n + m:n + m + 2]
        p_out, g_out = rest[n + m + 2:2 * n + m + 2], rest[2 * n + m + 2:2 * (n + m) + 2]
        acc, sems = rest[2 * (n + m) + 2], rest[2 * (n + m) + 3:]
        second_sems, first_sems = (sems[:3] if n else ()), sems[3 if n else 0:]
        i, kk = pl.program_id(0), pl.program_id(1)

        @pl.when(jnp.logical_and(i == 0, kk == 0))
        def _():
            dg_ref[...] = jnp.zeros_like(dg_ref)
            if n:
                _comm_start(_rs_second(p_in, p_out, *second_sems))
            if m:
                _comm_start(_rs_first(g_in, g_out, *first_sems))

        @pl.when(kk == 0)
        def _():
            acc[...] = jnp.zeros_like(acc)

        @pl.when(kk < nkb)
        def _():
            acc[...] += _dot(dpb_ref[...], wbr_ref[...])

        @pl.when(kk >= nkb)
        def _():
            acc[...] += _dot(dpg_ref[...], wg_ref[...])

        @pl.when(kk == nk - 1)
        def _():
            xf = x_ref[...]
            r = lax.rsqrt(jnp.mean(xf * xf, axis=-1, keepdims=True) + EPS)
            xhat = xf * r
            dh = acc[...]
            dg_ref[...] += jnp.sum(dh * xhat, axis=0, keepdims=True)
            dxh = dh * g_ref[...]
            dx_ref[...] = dxo_ref[...] + r * (dxh - xhat * jnp.mean(dxh * xhat, axis=-1, keepdims=True))

        if n or m:
            @pl.when(jnp.logical_and(i == ni - 1, kk == nk - 1))
            def _():
                if n:
                    _comm_wait(_rs_second(p_in, p_out, *second_sems))
                if m:
                    _comm_wait(_rs_first(g_in, g_out, *first_sems))

    res = pl.pallas_call(
        body, name="dh_bwd_scatter" if (n or m) else "dh_bwd",
        grid=(ni, nk),
        in_specs=[pl.BlockSpec((tm, tkb), lambda i, k: (i, jnp.minimum(k, nkb - 1))),
                  pl.BlockSpec((tkb, D_MODEL), lambda i, k: (jnp.minimum(k, nkb - 1), 0)),
                  pl.BlockSpec((tm, tkg), lambda i, k: (i, jnp.maximum(k - nkb, 0))),
                  pl.BlockSpec((tkg, D_MODEL), lambda i, k: (jnp.maximum(k - nkb, 0), 0)),
                  pl.BlockSpec((tm, D_MODEL), lambda i, k: (i, 0)), pl.BlockSpec((1, D_MODEL), lambda i, k: (0, 0)),
                  pl.BlockSpec((tm, D_MODEL), lambda i, k: (i, 0))] + [ANY] * (n + m),
        out_specs=[pl.BlockSpec((tm, D_MODEL), lambda i, k: (i, 0)), pl.BlockSpec((1, D_MODEL), lambda i, k: (0, 0))]
                  + [ANY] * (n + m),
        out_shape=[jax.ShapeDtypeStruct((S, D_MODEL), F32), jax.ShapeDtypeStruct((1, D_MODEL), F32)]
                  + [jax.ShapeDtypeStruct(p.shape, p.dtype) for p in parts]
                  + [jax.ShapeDtypeStruct(gr.shape[:1] + gr.shape[2:], gr.dtype) for gr in grads],
        scratch_shapes=[pltpu.VMEM((tm, D_MODEL), F32)] + (_dma_sems(3 * n, 3 * n, n) if n else [])
                       + (_dma_sems(N_CHIP * m, N_CHIP * m) if m else []),
        compiler_params=_params(("arbitrary", "arbitrary")),
    )(dpb, wt, dpg, wgt, x, g, dxo, *parts, *grads)
    return res[0], res[1], list(res[2:2 + n]), list(res[2 + n:])


def _dw_in(h, dpb, dpg, parts=(), grads=()):
    S = h.shape[0]
    tk = 2048
    nk = S // tk
    n, m = len(parts), len(grads)
    tmb = D_BRANCHES // 4
    ng = D_GATES // GATE_TILE

    def accumulate(a_ref, h_ref, o_ref, acc):
        kk = pl.program_id(1)

        @pl.when(kk == 0)
        def _():
            acc[...] = jnp.zeros_like(acc)

        acc[...] += _dot_tn(a_ref[...], h_ref[...])

        @pl.when(kk == nk - 1)
        def _():
            o_ref[...] = acc[...].astype(BF16)

    def branches(a_ref, h_ref, *rest):
        p_in, g_in, o_ref = rest[:n], rest[n:n + m], rest[n + m]
        p_out, g_out = rest[n + m + 1:2 * n + m + 1], rest[2 * n + m + 1:2 * (n + m) + 1]
        acc, sems = rest[2 * (n + m) + 1], rest[2 * (n + m) + 2:]
        second_sems, first_sems = (sems[:3] if n else ()), sems[3 if n else 0:]
        i, kk = pl.program_id(0), pl.program_id(1)

        if n or m:
            @pl.when(jnp.logical_and(i == 0, kk == 0))
            def _():
                if n:
                    _comm_start(_rs_second(p_in, p_out, *second_sems))
                if m:
                    _comm_start(_rs_first(g_in, g_out, *first_sems))

        accumulate(a_ref, h_ref, o_ref, acc)

        if n or m:
            @pl.when(jnp.logical_and(i == 3, kk == nk - 1))
            def _():
                if n:
                    _comm_wait(_rs_second(p_in, p_out, *second_sems))
                if m:
                    _comm_wait(_rs_first(g_in, g_out, *first_sems))

    def gates(a_ref, h_ref, dst_ref, *rest):
        q_in, o_ref, q_out, acc, sems = rest[:m], rest[m], rest[m + 1:2 * m + 1], rest[2 * m + 1], rest[2 * m + 2:]
        i, kk = pl.program_id(0), pl.program_id(1)

        if m:
            @pl.when(jnp.logical_and(i == 0, kk == 0))
            def _():
                _comm_start(_rs_second(q_in, q_out, *sems))

        accumulate(a_ref, h_ref, o_ref, acc)

        if m:
            @pl.when(jnp.logical_and(i == ng - 1, kk == nk - 1))
            def _():
                _comm_wait(_rs_second(q_in, q_out, *sems))

    res = pl.pallas_call(
        branches, name="dw_in_branches_scatter" if (n or m) else "dw_in_branches",
        grid=(4, nk),
        in_specs=[pl.BlockSpec((tk, tmb), lambda i, k: (k, i)), pl.BlockSpec((tk, D_MODEL), lambda i, k: (k, 0))]
                 + [ANY] * (n + m),
        out_specs=[pl.BlockSpec((tmb, D_MODEL), lambda i, k: (i, 0))] + [ANY] * (n + m),
        out_shape=[jax.ShapeDtypeStruct((D_IN, D_MODEL), BF16)]
                  + [jax.ShapeDtypeStruct(p.shape, p.dtype) for p in parts]
                  + [jax.ShapeDtypeStruct(gr.shape[:1] + gr.shape[2:], gr.dtype) for gr in grads],
        scratch_shapes=[pltpu.VMEM((tmb, D_MODEL), F32)] + (_dma_sems(3 * n, 3 * n, n) if n else [])
                       + (_dma_sems(N_CHIP * m, N_CHIP * m) if m else []),
        compiler_params=_params(("arbitrary", "arbitrary")),
    )(dpb, h, *parts, *grads)
    pair = _pair_sum(grads, res[1 + n:]) if m else []
    res2 = pl.pallas_call(
        gates, name="dw_in_gates_scatter" if m else "dw_in_gates",
        grid=(ng, nk),
        in_specs=[pl.BlockSpec((tk, GATE_TILE), lambda i, k: (k, i)), pl.BlockSpec((tk, D_MODEL), lambda i, k: (k, 0)),
                  ANY] + [ANY] * m,
        out_specs=[pl.BlockSpec((GATE_TILE, D_MODEL), lambda i, k: (D_BRANCHES // GATE_TILE + i, 0))] + [ANY] * m,
        out_shape=[jax.ShapeDtypeStruct((D_IN, D_MODEL), BF16)] + [jax.ShapeDtypeStruct(p.shape, p.dtype) for p in pair],
        input_output_aliases={2: 0},
        scratch_shapes=[pltpu.VMEM((GATE_TILE, D_MODEL), F32)] + (_dma_sems(3 * m, 3 * m, m) if m else []),
        compiler_params=_params(("arbitrary", "arbitrary")),
    )(dpg, h, res[0], *pair)
    return res2[0], list(res[1:1 + n]), list(res2[1:])


def _row_tile(R, C, block_bytes=2 << 20):
    for cand in range(min(R, block_bytes // (C * 4)) // 8 * 8, 0, -8):
        if R % cand == 0:
            return cand
    return R


def _adamw_update(p_ref, w_ref, m_ref, v_ref, g_ref, d_ref, nm_ref, nv_ref):
    c1 = 1.0 / (1.0 - ADAM_B1 ** ADAM_STEP)
    c2 = 1.0 / (1.0 - ADAM_B2 ** ADAM_STEP)
    g = p_ref[0].astype(F32)
    for k in range(1, p_ref.shape[0]):
        g = g + p_ref[k].astype(F32)
    nm = ADAM_B1 * m_ref[...] + (1.0 - ADAM_B1) * g
    nv = ADAM_B2 * v_ref[...] + (1.0 - ADAM_B2) * (g * g)
    g_ref[...] = g
    nm_ref[...] = nm
    nv_ref[...] = nv
    d_ref[...] = -ADAM_LR * ((nm * c1) / (jnp.sqrt(nv * c2) + ADAM_EPS) + ADAM_WD * w_ref[...])


def _adamw(parts, w, m, v, name):
    P, R, C = parts.shape
    tr = _row_tile(R, C)

    def body(*refs):
        _adamw_update(*refs)

    spec = pl.BlockSpec((tr, C), lambda i: (i, 0))
    return pl.pallas_call(
        body, name=name,
        grid=(R // tr,),
        in_specs=[pl.BlockSpec((P, tr, C), lambda i: (0, i, 0)), spec, spec, spec],
        out_specs=[spec] * 4,
        out_shape=[jax.ShapeDtypeStruct((R, C), F32)] * 4,
        compiler_params=_params(("parallel",)),
    )(parts, w, m, v)


def _adamw_layers(parts, w, m, v, name):
    depth = len(parts)
    P, R, C = parts[0].shape
    tr = _row_tile(R, C, 1 << 20)

    def body(*refs):
        layer = pl.program_id(0)
        for k in range(depth):
            @pl.when(layer == k)
            def _(k=k):
                _adamw_update(refs[k], *refs[depth:])

    def part_spec(k):
        return pl.BlockSpec((P, tr, C), lambda l, i: (0, jnp.where(l == k, i, 0), 0))

    spec = pl.BlockSpec((None, tr, C), lambda l, i: (l, i, 0))
    return pl.pallas_call(
        body, name=name,
        grid=(depth, R // tr),
        in_specs=[part_spec(k) for k in range(depth)] + [spec] * 3,
        out_specs=[spec] * 4,
        out_shape=[jax.ShapeDtypeStruct((depth, R, C), F32)] * 4,
        compiler_params=_params(("arbitrary", "arbitrary")),
    )(*parts, w, m, v)


def _place():
    return lax.axis_index("x"), lax.axis_index("y"), lax.axis_index("c")


def _all_gather(shards):
    n = len(shards)

    def body(*refs):
        ins, outs = refs[:n], refs[n:2 * n]
        send1, recv1, local_sems, relay_send, relay_recv, send2, recv2 = refs[2 * n:]
        x, y, c = _place()
        me, sibling = (x, y, c), (x, y, 1 - c)
        x_nbr, y_nbr, diagonal = [(*chip, c) for chip in _other_chips(x, y)]
        local, first_out, first_in, relay_out, relay_in = [], [], [], [], []
        for a in range(n):
            local.append(pltpu.make_async_copy(ins[a], outs[a].at[_dev(me)], local_sems.at[a]))
            for k, to in enumerate((sibling, x_nbr, y_nbr)):
                first_out.append(_remote(ins[a], outs[a].at[_dev(me)], send1, recv1, 3 * a + k, to))
                first_in.append(_remote(ins[a], outs[a].at[_dev(to)], send1, recv1, 3 * a + k, to))
            half = shards[a].shape[0] // 2
            for k, (src, to, rows) in enumerate(((x_nbr, y_nbr, pl.ds(0, half)), (y_nbr, x_nbr, pl.ds(half, half)))):
                passed, got = outs[a].at[_dev(src)].at[rows], outs[a].at[_dev(diagonal)].at[rows]
                relay_out.append(_remote(passed, passed, relay_send, relay_recv, 2 * a + k, to))
                relay_in.append(_remote(got, got, relay_send, relay_recv, 2 * a + k, to))
        second = _ag_second(outs, send2, recv2)
        for cp in local + first_out:
            cp.start()
        for a in range(n):
            for k in range(2):
                first_in[3 * a + 1 + k].wait_recv()
                relay_out[2 * a + k].start()
                second[1][3 * a + k].start()
        for a in range(n):
            for k in range(2):
                relay_in[2 * a + k].wait_recv()
            second[1][3 * a + 2].start()
        for a in range(n):
            first_in[3 * a].wait_recv()
        for cp in second[2]:
            cp.wait_recv()
        for cp in first_out + relay_out + second[1]:
            cp.wait_send()
        for cp in local:
            cp.wait()

    assert all(s.shape[0] % 32 == 0 for s in shards)
    return pl.pallas_call(
        body, name="weights_all_gather",
        in_specs=[ANY] * n, out_specs=[ANY] * n,
        out_shape=[jax.ShapeDtypeStruct((N_DEV,) + s.shape, s.dtype) for s in shards],
        scratch_shapes=_dma_sems(3 * n, 3 * n, n, 2 * n, 2 * n, 3 * n, 3 * n),
        compiler_params=pltpu.CompilerParams(has_side_effects=True),
    )(*shards)


N_BIG = 4


def _dev(p):
    return 4 * p[0] + 2 * p[1] + p[2]


def _other_chips(x, y):
    return [(1 - x, y), (x, 1 - y), (1 - x, 1 - y)]


def _remote(src, dst, send_sems, recv_sems, k, to):
    return pltpu.make_async_remote_copy(src_ref=src, dst_ref=dst, send_sem=send_sems.at[k], recv_sem=recv_sems.at[k],
                                        device_id=to, device_id_type=MESH)


def _ag_first(ins, outs, send_sems, recv_sems, local_sems):
    x, y, c = _place()
    me = (x, y, c)
    targets = [(x, y, 1 - c)] + [(*chip, c) for chip in _other_chips(x, y)]
    local, out, inc = [], [], []
    for a in range(len(ins)):
        local.append(pltpu.make_async_copy(ins[a], outs[a].at[_dev(me)], local_sems.at[a]))
        for k, to in enumerate(targets):
            out.append(_remote(ins[a], outs[a].at[_dev(me)], send_sems, recv_sems, 4 * a + k, to))
            inc.append(_remote(ins[a], outs[a].at[_dev(to)], send_sems, recv_sems, 4 * a + k, to))
    return local, out, inc


def _ag_second(bufs, send_sems, recv_sems):
    x, y, c = _place()
    out, inc = [], []
    for a in range(len(bufs)):
        for j, chip in enumerate(_other_chips(x, y)):
            mine, theirs = bufs[a].at[_dev((*chip, c))], bufs[a].at[_dev((*chip, 1 - c))]
            out.append(_remote(mine, mine, send_sems, recv_sems, 3 * a + j, (x, y, 1 - c)))
            inc.append(_remote(theirs, theirs, send_sems, recv_sems, 3 * a + j, (x, y, 1 - c)))
    return [], out, inc


def _rs_first(ins, outs, send_sems, recv_sems):
    x, y, c = _place()
    out = [_remote(ins[a].at[j, 1 - c], outs[a].at[j], send_sems, recv_sems, N_CHIP * a + j, (x, y, 1 - c))
           for a in range(len(ins)) for j in range(N_CHIP)]
    return [], out, out


def _rs_second(ins, outs, send_sems, recv_sems, local_sems):
    x, y, c = _place()
    my_chip = 2 * x + y
    local, out, inc = [], [], []
    for a in range(len(ins)):
        local.append(pltpu.make_async_copy(ins[a].at[my_chip], outs[a].at[my_chip], local_sems.at[a]))
        for k, (ox, oy) in enumerate(_other_chips(x, y)):
            out.append(_remote(ins[a].at[2 * ox + oy], outs[a].at[my_chip], send_sems, recv_sems, 3 * a + k, (ox, oy, c)))
            inc.append(_remote(ins[a].at[2 * ox + oy], outs[a].at[2 * ox + oy], send_sems, recv_sems, 3 * a + k,
                               (ox, oy, c)))
    return local, out, inc


def _comm_start(exchange):
    local, out, _ = exchange
    for cp in local + out:
        cp.start()


def _comm_wait(exchange):
    local, out, inc = exchange
    for cp in inc:
        cp.wait_recv()
    for cp in out:
        cp.wait_send()
    for cp in local:
        cp.wait()


def _dma_sems(*counts):
    return [pltpu.SemaphoreType.DMA((n,)) for n in counts]


def _pair_sum(grads, recvs):
    n = len(grads)

    def body(c_ref, *refs):
        for a in range(n):
            refs[2 * n + a][...] = (refs[a][...].astype(F32) + refs[n + a][...].astype(F32)).astype(BF16)

    def g_spec(g):
        return pl.BlockSpec((None, None) + g.shape[2:], lambda j, c_ref: (j, c_ref[0], 0, 0))

    def r_spec(r):
        return pl.BlockSpec((None,) + r.shape[1:], lambda j, c_ref: (j, 0, 0))

    return pl.pallas_call(
        body, name="pair_sum",
        grid_spec=pltpu.PrefetchScalarGridSpec(
            num_scalar_prefetch=1, grid=(N_CHIP,),
            in_specs=[g_spec(g) for g in grads] + [r_spec(r) for r in recvs],
            out_specs=[r_spec(r) for r in recvs]),
        out_shape=[jax.ShapeDtypeStruct(r.shape, BF16) for r in recvs],
        compiler_params=_params(("parallel",)),
    )(lax.axis_index("c").reshape(1).astype(jnp.int32), *grads, *recvs)


SMALL_ROWS = 544


def _all_reduce_small(buf, parts=()):
    n = len(parts)

    def body(in_ref, *rest):
        p_in, out_ref, p_out = rest[:n], rest[n], rest[n + 1:2 * n + 1]
        recv, acc, send1, recv1, send2, recv2 = rest[2 * n + 1:2 * n + 7]
        scatter_sems = rest[2 * n + 7:]
        x, y, c = _place()
        me = 4 * x + 2 * y + c
        peers = [(x ^ (r >> 2), y ^ ((r >> 1) & 1), c ^ (r & 1)) for r in range(1, N_DEV)]

        def idx(p):
            return 4 * p[0] + 2 * p[1] + p[2]

        if n:
            _comm_start(_rs_second(p_in, p_out, *scatter_sems))
        first = [pltpu.make_async_remote_copy(
            src_ref=in_ref.at[idx(p)], dst_ref=recv.at[me], send_sem=send1.at[r], recv_sem=recv1.at[r],
            device_id=p, device_id_type=MESH) for r, p in enumerate(peers)]
        for cp in first:
            cp.start()
        recv[me] = in_ref[me]
        for r, p in enumerate(peers):
            pltpu.make_async_remote_copy(
                src_ref=in_ref.at[idx(p)], dst_ref=recv.at[idx(p)], send_sem=send1.at[r], recv_sem=recv1.at[r],
                device_id=p, device_id_type=MESH).wait_recv()
        total = recv[0]
        for k in range(1, N_DEV):
            total = total + recv[k]
        acc[...] = total
        out_ref[me] = total
        second = [pltpu.make_async_remote_copy(
            src_ref=acc, dst_ref=out_ref.at[me], send_sem=send2.at[r], recv_sem=recv2.at[r],
            device_id=p, device_id_type=MESH) for r, p in enumerate(peers)]
        for cp in second:
            cp.start()
        for r, p in enumerate(peers):
            pltpu.make_async_remote_copy(
                src_ref=acc, dst_ref=out_ref.at[idx(p)], send_sem=send2.at[r], recv_sem=recv2.at[r],
                device_id=p, device_id_type=MESH).wait_recv()
        for cp in first + second:
            cp.wait_send()
        if n:
            _comm_wait(_rs_second(p_in, p_out, *scatter_sems))

    vm = pl.BlockSpec(memory_space=pltpu.VMEM)
    res = pl.pallas_call(
        body, name="small_grads_all_reduce",
        in_specs=[vm] + [ANY] * n, out_specs=[vm] + [ANY] * n,
        out_shape=[jax.ShapeDtypeStruct(buf.shape, F32)] + [jax.ShapeDtypeStruct(p.shape, p.dtype) for p in parts],
        scratch_shapes=[pltpu.VMEM(buf.shape, F32), pltpu.VMEM(buf.shape[1:], F32)] + _dma_sems(7, 7, 7, 7)
                       + (_dma_sems(3 * n, 3 * n, n) if n else []),
        compiler_params=pltpu.CompilerParams(has_side_effects=True, vmem_limit_bytes=VMEM_LIMIT),
    )(buf, *parts)
    return res[0], list(res[1:])


def _dilate(a, d):
    if d == 1:
        return a
    S, C = a.shape
    return a.reshape(S // d, d, C).transpose(1, 0, 2).reshape(S, C)


def _undilate(a, d):
    if d == 1:
        return a
    S, C = a.shape
    return a.reshape(d, S // d, C).transpose(1, 0, 2).reshape(S, C)


def _cols(a, cb, n=1):
    return a[:, cb * WIDTH:(cb + n) * WIDTH]


def _to_blocks(g, kind):
    if kind == "rows":
        C = g.shape[1]
        return g.reshape(N_CHIP, 2, -1, C)
    return g.reshape(4 * WIDTH, N_CHIP, 2, -1).transpose(1, 2, 0, 3)


SMALL = ("norm_g", "gm_ln_g", "gm_ln_b", "gm_ws", "gm_bs", "pool_w", "pool_scale", "mem_norm_g", "final_norm_g")


def _pack_small(tree):
    flat = jnp.concatenate([tree[k].reshape(-1, 128) for k in SMALL], axis=0)
    return jnp.pad(flat, ((0, N_DEV * SMALL_ROWS - flat.shape[0]), (0, 0)))


def _unpack_small(flat, like):
    out, at = {}, 0
    for k in SMALL:
        rows = like[k].size // 128
        out[k] = flat[at:at + rows].reshape(like[k].shape)
        at += rows
    return out


def _make_layer(wt, wkv, wb, wout, norm_g, mem_norm_g, ln_g, ln_b, gm_ws, gm_bs, pool_w, pool_scale):
    tril = jnp.tril(jnp.ones((CHUNK, CHUNK), bool))
    wsm = jnp.where(tril, gm_ws, 0.0).astype(BF16)
    pw = pool_w.astype(BF16)
    bands, bands_t = _band_constants()
    return dict(wt=wt, wgt=wt[D_BRANCHES:], wkv=wkv, wb=wb, wout=wout, g=norm_g[None], mg=mem_norm_g[None],
                ln_g=ln_g[None],
                ln_b=ln_b[None], wsm=wsm, wsm_t=wsm.transpose(0, 2, 1), pw=pw, pw_t=pw.transpose(0, 2, 1),
                ps=pool_scale[None], bias=jnp.repeat(gm_bs.T, HEAD, axis=1), bands=bands, bands_t=bands_t)


def _layer_fwd(xl, mem0, L, next_shards=()):
    S = xl.shape[0]
    proj, gates, h, half_win = _in_proj(xl, L["g"], L["wt"], L["wgt"], next_shards[:1])
    kv, mem_n = _mem_kv(mem0, L["mg"], L["wkv"])
    y4, cdf, win, half_small = _abm_fwd(proj, L["ln_g"], L["ln_b"], L["wsm"], L["bias"], L["pw"], L["ps"], kv,
                                        L["bands"], half_win, next_shards[2:])
    o_g, l_g = [], []
    for gi, d in enumerate(DILATIONS):
        if d == 1:
            o, lse = _attn_fwd(proj, CB_Q0, proj, CB_K, proj, CB_CV, S // CHUNK)
        else:
            o, lse = _attn_fwd_dilated(proj, CB_Q0 + gi, CB_K, CB_CV, d)
        o_g.append(o)
        l_g.append(lse)
    (xn, y4, oc, lse, z), small = _merge_fwd(xl, y4, o_g, l_g, proj, gates, L["wb"], L["wout"], next_shards[1:2],
                                             half_small)
    saved = dict(x=xl, proj=proj, gates=gates, h=h, kv=kv, mem_n=mem_n, y4=y4, cdf=cdf, oc=oc, lse=lse, z=z)
    return xn, saved, win + small


def _place_cols(dst, piece, cb):
    return lax.dynamic_update_slice(dst, piece, (0, cb * WIDTH))


def _layer_bwd(dx, mem0, L, sv, later=(), last=False):
    S = dx.shape[0]
    proj = sv["proj"]
    (dy3, doc, delta, dcg, dgm, dwb, dwout), from_sibling = _merge_bwd(
        dx, sv["y4"], sv["oc"], sv["z"], proj, sv["gates"], L["wb"], L["wout"], later)
    pair = _pair_sum(later, from_sibling) if later else ()
    dpb, dm, dlng, dlnb, dws, dbias, dpw, dps, dkv = _abm_bwd(
        proj, sv["cdf"], dy3, L["ln_g"], L["ln_b"], L["wsm"], L["wsm_t"], L["bias"], L["pw"], L["pw_t"], L["ps"], sv["kv"],
        L["bands"], L["bands_t"])
    dk, dv = None, None
    for gi, d in enumerate(DILATIONS):
        if d == 1:
            dpb, dk, dv = _attn_bwd(proj, CB_Q0, proj, CB_K, proj, CB_CV, doc, sv["lse"], delta, S // CHUNK,
                                    dpb, CB_Q0)
        else:
            dpb, dk, dv = _attn_bwd_dilated(proj, CB_Q0 + gi, CB_K, CB_CV, doc, sv["lse"], delta, d, dk, dv,
                                            dpb, CB_Q0 + gi)
    dpb = _place_cols(dpb, dk, CB_K)
    dpb = _place_cols(dpb, dv, CB_CV)
    dpb = _place_cols(dpb, dcg, CB_CGATE)
    dpb = _place_cols(dpb, dm, CB_MQ)
    dwkv, dmg = _mem_bwd(mem0, L["mg"], sv["mem_n"], L["wkv"], dkv)
    ready = _blocked_rest(dict(w_mem_kv=dwkv, w_branch=dwb, w_out=dwout))
    dwin_t, parts_rest, own = _dw_in(sv["h"], dpb, dgm, pair[1:], ready if last else ())
    big = [_to_blocks(dwin_t, "rows")] + ready
    dxi, dng, parts, from_sibling = _dh_bwd(dpb, dgm, L["wt"], L["wgt"], sv["x"], L["g"], dx, pair[:1],
                                            big[:1] if last else ())
    parts = parts + parts_rest
    small = dict(norm_g=dng[0], gm_ln_g=dlng[0], gm_ln_b=dlnb[0], gm_ws=dws,
                 gm_bs=_bias_reduce(dbias)[:, :N_HEAD].T, pool_w=dpw, pool_scale=dps[0], mem_norm_g=dmg[0])
    return dxi, big, small, parts, own, from_sibling


BIG = ("w_in", "w_mem_kv", "w_branch", "w_out")


def _blocked_rest(big):
    return [_to_blocks(big["w_mem_kv"], "rows"), _to_blocks(big["w_branch"], "branch"), _to_blocks(big["w_out"], "rows")]


def _full_weights(gathered):
    win_t, wkv, wb, wout = gathered
    return (win_t.reshape(D_IN, D_MODEL), wkv.reshape(D_MODEL, 2 * WIDTH),
            wb.reshape(N_DEV, 4, WIDTH, -1).transpose(1, 2, 0, 3).reshape(4, WIDTH, D_MODEL),
            wout.reshape(D_MODEL, D_MODEL))


def kernel(x, mem, norm_g, w_in, gm_ln_g, gm_ln_b, gm_ws, gm_bs, pool_w, pool_scale, mem_norm_g, w_mem_kv, w_branch, w_out, final_norm_g, loss_target, m_norm_g, m_w_in, m_gm_ln_g, m_gm_ln_b, m_gm_ws, m_gm_bs, m_pool_w, m_pool_scale, m_mem_norm_g, m_w_mem_kv, m_w_branch, m_w_out, m_final_norm_g, v_norm_g, v_w_in, v_gm_ln_g, v_gm_ln_b, v_gm_ws, v_gm_bs, v_pool_w, v_pool_scale, v_mem_norm_g, v_w_mem_kv, v_w_branch, v_w_out, v_final_norm_g):
    x0 = x[0]
    mem0 = mem[0]
    tgt = loss_target[0]
    S = x0.shape[0]

    shards = [[w_in[l].T.astype(BF16), w_mem_kv[l].astype(BF16), w_branch[l].astype(BF16).reshape(4 * WIDTH, -1),
               w_out[l].astype(BF16)] for l in range(DEPTH)]
    gathered = _all_gather(shards[0])
    layers, saved = [], []
    xl = x0
    for l in range(DEPTH):
        layers.append(_make_layer(*_full_weights(gathered), norm_g[l], mem_norm_g[l], gm_ln_g[l], gm_ln_b[l],
                                  gm_ws[l], gm_bs[l], pool_w[l], pool_scale[l]))
        xl, sv, gathered = _layer_fwd(xl, mem0, layers[l], shards[l + 1] if l + 1 < DEPTH else ())
        saved.append(sv)

    loss_part, dx, d_final = _loss_head(xl, final_norm_g[None], tgt)
    loss = lax.psum(loss_part[0, 0], ("x", "y", "c"))

    small = {k: [None] * DEPTH for k in SMALL if k != "final_norm_g"}
    parts = [None] * DEPTH
    later = ()
    for l in reversed(range(DEPTH)):
        dx, gb, gs, done, own, from_sibling = _layer_bwd(dx, mem0, layers[l], saved[l], later, last=(l == 0))
        if later:
            parts[l + 1] = done
        later = gb
        for k in gs:
            small[k][l] = gs[k]
    grad_x = dx[None]
    small_tree = {k: jnp.stack(small[k]) for k in small}
    small_tree["final_norm_g"] = d_final[0]
    reduced, win_parts = _all_reduce_small(_pack_small(small_tree).reshape(N_DEV, SMALL_ROWS, 128),
                                           _pair_sum(later[:1], from_sibling))
    parts[0] = win_parts + own

    weights = dict(norm_g=norm_g, w_in=w_in, gm_ln_g=gm_ln_g, gm_ln_b=gm_ln_b, gm_ws=gm_ws, gm_bs=gm_bs,
                   pool_w=pool_w, pool_scale=pool_scale, mem_norm_g=mem_norm_g, w_mem_kv=w_mem_kv,
                   w_branch=w_branch, w_out=w_out, final_norm_g=final_norm_g)
    m_in = dict(norm_g=m_norm_g, w_in=m_w_in, gm_ln_g=m_gm_ln_g, gm_ln_b=m_gm_ln_b, gm_ws=m_gm_ws, gm_bs=m_gm_bs,
                pool_w=m_pool_w, pool_scale=m_pool_scale, mem_norm_g=m_mem_norm_g, w_mem_kv=m_w_mem_kv,
                w_branch=m_w_branch, w_out=m_w_out, final_norm_g=m_final_norm_g)
    v_in = dict(norm_g=v_norm_g, w_in=v_w_in, gm_ln_g=v_gm_ln_g, gm_ln_b=v_gm_ln_b, gm_ws=v_gm_ws, gm_bs=v_gm_bs,
                pool_w=v_pool_w, pool_scale=v_pool_scale, mem_norm_g=v_mem_norm_g, w_mem_kv=v_w_mem_kv,
                w_branch=v_w_branch, w_out=v_w_out, final_norm_g=v_final_norm_g)
    res = {}
    def view(k, arr):
        return arr.transpose(0, 2, 1) if k == "w_in" else arr

    for a, k in enumerate(BIG):
        shape = view(k, weights[k]).shape
        by_layer = [parts[l][a] for l in range(DEPTH)]
        lrc = (DEPTH,) + by_layer[0].shape[1:]
        outs = _adamw_layers(by_layer, view(k, weights[k]).reshape(lrc), view(k, m_in[k]).reshape(lrc),
                             view(k, v_in[k]).reshape(lrc), "adamw_" + k)
        res[k] = [view(k, o.reshape(shape)) for o in outs]
    outs = _adamw(reduced.reshape(1, N_DEV * SMALL_ROWS, 128), _pack_small(weights), _pack_small(m_in),
                  _pack_small(v_in), "adamw_small")
    unpacked = [_unpack_small(o, weights) for o in outs]
    for k in SMALL:
        res[k] = [u[k] for u in unpacked]

    order = ("norm_g", "w_in", "gm_ln_g", "gm_ln_b", "gm_ws", "gm_bs", "pool_w", "pool_scale", "mem_norm_g",
             "w_mem_kv", "w_branch", "w_out", "final_norm_g")
    return (loss, grad_x, *[res[k][0] for k in order], *[res[k][1] for k in order],
            *[res[k][2] for k in order], *[res[k][3] for k in order])
```

```python
import functools
import math

import numpy as np

import jax
import jax.numpy as jnp
from jax import lax
from jax.experimental import pallas as pl
from jax.experimental.pallas import tpu as pltpu

F32 = jnp.float32
BF16 = jnp.bfloat16

D_MODEL = 1024
DEPTH = 4
WIDTH = 512
D_IN = 10752
HEAD = 128
N_HEAD = 4
CHUNK = 128
MEM_LEN = 256
POOL_WINDOWS = (2, 4, 8, 16)
DILATIONS = (1, 4, 16)
EPS = 1e-6
NEG = -1e30
ATT_SCALE = HEAD ** -0.5
N_DEV = 8
N_CHIP = 4

D_BRANCHES = 6656
D_GATES = D_IN - D_BRANCHES
CB_U, CB_V, CB_AGATE, CB_PIN, CB_PGATE = 0, 1, 2, 3, 4
CB_Q0, CB_K, CB_CV, CB_CGATE, CB_MQ, CB_MGATE = 5, 8, 9, 10, 11, 12

ADAM_LR = 0.001
ADAM_B1 = 0.9
ADAM_B2 = 0.999
ADAM_EPS = 1e-08
ADAM_WD = 0.01
ADAM_STEP = 10

VMEM_LIMIT = 56 * 1024 * 1024
MESH = pl.DeviceIdType.MESH
ANY = pl.BlockSpec(memory_space=pl.ANY)

NT = (((1,), (1,)), ((), ()))
TN = (((0,), (0,)), ((), ()))


def _dot(a, b):
    return jnp.dot(a, b, preferred_element_type=F32)


def _dot_nt(a, b):
    return lax.dot_general(a, b, NT, preferred_element_type=F32)


def _dot_tn(a, b):
    return lax.dot_general(a, b, TN, preferred_element_type=F32)


def _sigmoid(x):
    return 0.5 * jnp.tanh(0.5 * x) + 0.5


def _silu(x):
    return x * _sigmoid(x)


def _silu_and_grad(x):
    s = _sigmoid(x)
    return x * s, s * (1.0 + x * (1.0 - s))


def _normal_cdf(x):
    return 0.5 * (1.0 + lax.erf(x * (2.0 ** -0.5)))


def _gelu_and_grad(x, cdf):
    return x * cdf, cdf + x * jnp.exp(-0.5 * x * x) * (1.0 / math.sqrt(2.0 * math.pi))


def _col(blk, h):
    lane = lax.broadcasted_iota(jnp.int32, blk.shape, 1)
    return jnp.sum(jnp.where(lane == h, blk, 0.0), axis=1, keepdims=True)


def _put_cols(cols):
    rows = cols[0].shape[0]
    lane = lax.broadcasted_iota(jnp.int32, (rows, 128), 1)
    out = jnp.zeros((rows, 128), F32)
    for h, cv in enumerate(cols):
        out = jnp.where(lane == h, cv, out)
    return out


def _params(sem, vmem=VMEM_LIMIT):
    return pltpu.CompilerParams(dimension_semantics=sem, vmem_limit_bytes=vmem)


def _full(shape):
    nd = len(shape)
    return pl.BlockSpec(shape, lambda *_: (0,) * nd)


def _resident(shape):
    nd = len(shape)
    return pl.BlockSpec(shape, lambda *_: (0,) * nd, pipeline_mode=pl.Buffered(1))


def _rows(tm, width, cb=0):
    return pl.BlockSpec((tm, width), lambda i: (i, cb))


GATE_TILE = 512


def _in_proj(x, g, wt, shards=()):
    S = x.shape[0]
    tm, tnb, tng = 1024, D_BRANCHES // 4, D_GATES // 4
    njb, njg = 4, 4
    n = len(shards)
    ni, nj = S // tm, njb + njg

    def body(x_ref, g_ref, wbr_ref, wg0_ref, wg1_ref, *rest):
        ins, (proj_ref, gates_ref, h_ref), outs = rest[:n], rest[n:n + 3], rest[n + 3:2 * n + 3]
        hs, sems = rest[2 * n + 3], rest[2 * n + 4:]
        i, j = pl.program_id(0), pl.program_id(1)

        if n:
            @pl.when(jnp.logical_and(i == 0, j == 0))
            def _():
                _comm_start(_ag_first(ins, outs, *sems))

        @pl.when(j == 0)
        def _():
            xf = x_ref[...]
            r = lax.rsqrt(jnp.mean(xf * xf, axis=-1, keepdims=True) + EPS)
            h = (xf * r * g_ref[...]).astype(BF16)
            hs[...] = h
            h_ref[...] = h

        @pl.when(j < njb)
        def _():
            proj_ref[...] = _dot_nt(hs[...], wbr_ref[...]).astype(BF16)

        @pl.when(j >= njb)
        def _():
            gates_ref[:, :tng // 2] = _dot_nt(hs[...], wg0_ref[...]).astype(BF16)
            gates_ref[:, tng // 2:] = _dot_nt(hs[...], wg1_ref[...]).astype(BF16)

        if n:
            @pl.when(jnp.logical_and(i == ni - 1, j == nj - 1))
            def _():
                _comm_wait(_ag_first(ins, outs, *sems))

    def first(j):
        return jnp.minimum(j, njb - 1)

    def second(j):
        return jnp.maximum(j - njb, 0)

    def gate_rows(half):
        return pl.BlockSpec((tng // 2, D_MODEL), lambda i, j: (D_BRANCHES // (tng // 2) + 2 * second(j) + half, 0))

    res = pl.pallas_call(
        body, name="in_proj_gather" if n else "in_proj",
        grid=(ni, nj),
        in_specs=[pl.BlockSpec((tm, D_MODEL), lambda i, j: (i, 0)),
                  pl.BlockSpec((1, D_MODEL), lambda i, j: (0, 0)),
                  pl.BlockSpec((tnb, D_MODEL), lambda i, j: (first(j), 0)), gate_rows(0), gate_rows(1)]
                 + [ANY] * n,
        out_specs=[pl.BlockSpec((tm, tnb), lambda i, j: (i, first(j))),
                   pl.BlockSpec((tm, tng), lambda i, j: (i, second(j))),
                   pl.BlockSpec((tm, D_MODEL), lambda i, j: (i, 0))] + [ANY] * n,
        out_shape=[jax.ShapeDtypeStruct((S, D_BRANCHES), BF16), jax.ShapeDtypeStruct((S, D_GATES), BF16),
                   jax.ShapeDtypeStruct((S, D_MODEL), BF16)]
                  + [jax.ShapeDtypeStruct((N_DEV,) + s.shape, s.dtype) for s in shards],
        scratch_shapes=[pltpu.VMEM((tm, D_MODEL), BF16)] + (_dma_sems(4 * n, 4 * n, n) if n else []),
        compiler_params=_params(("arbitrary", "arbitrary")),
    )(x, g, wt, wt, wt, *shards)
    return res[0], res[1], res[2], list(res[3:])


def _mem_kv(mem, g, w):
    M = mem.shape[0]

    def body(m_ref, g_ref, w_ref, kv_ref, mn_ref):
        xf = m_ref[...]
        r = lax.rsqrt(jnp.mean(xf * xf, axis=-1, keepdims=True) + EPS)
        mn = (xf * r * g_ref[...]).astype(BF16)
        mn_ref[...] = mn
        kv_ref[...] = _dot(mn, w_ref[...]).astype(BF16)

    return pl.pallas_call(
        body, name="mem_kv",
        out_shape=[jax.ShapeDtypeStruct((M, 2 * WIDTH), BF16), jax.ShapeDtypeStruct((M, D_MODEL), BF16)],
        compiler_params=pltpu.CompilerParams(vmem_limit_bytes=VMEM_LIMIT),
    )(mem, g, w)


def _band_constants():
    t = np.arange(CHUNK)[:, None]
    s = np.arange(CHUNK)[None, :]
    bands = np.stack([np.stack([(t - s >= 0) & (t - s < win), s > t + CHUNK - win]) for win in POOL_WINDOWS])
    bands = bands.astype(np.float32)
    return jnp.asarray(bands, BF16), jnp.asarray(bands.transpose(0, 1, 3, 2), BF16)


def _inv_count(first_row, win):
    t = first_row + lax.broadcasted_iota(jnp.int32, (CHUNK, 1), 0)
    return 1.0 / jnp.minimum(t + 1, win).astype(F32)


def _layer_norm_fwd(v):
    mu = jnp.mean(v, axis=-1, keepdims=True)
    vc = v - mu
    var = jnp.mean(vc * vc, axis=-1, keepdims=True)
    rstd = lax.rsqrt(var + EPS)
    return vc * rstd, rstd


def _mem_softmax(q, kmem):
    s = _dot_nt(q, kmem) * ATT_SCALE
    m = jnp.max(s, axis=-1, keepdims=True)
    e = jnp.exp(s - m)
    return e * (1.0 / jnp.sum(e, axis=-1, keepdims=True))


def _abm_fwd(proj, ln_g, ln_b, wsm, bias_full, pool_w, pool_scale, kv, bands, gathered=(), shards=()):
    S = proj.shape[0]
    tm = 512
    nchunk = tm // CHUNK
    n, m = len(gathered), len(shards)
    nsteps = S // tm

    def body(u_ref, v_ref, ag_ref, p_ref, ph_ref, pg_ref, mq_ref, mg_ref, lng_ref, lnb_ref, wsm_ref, bias_ref,
             pw_ref, ps_ref, kv_ref, band_ref, *rest):
        s_in = rest[n:n + m]
        y_ref, cdf_ref = rest[n + m], rest[n + m + 1]
        bufs, s_out = rest[n + m + 2:2 * n + m + 2], rest[2 * n + m + 2:2 * (n + m) + 2]
        mix, sems = rest[2 * (n + m) + 2], rest[2 * (n + m) + 3:]
        second_sems, first_sems = (sems[:2] if n else ()), sems[2 if n else 0:]
        i = pl.program_id(0)

        if n or m:
            @pl.when(i == 0)
            def _():
                if n:
                    _comm_start(_ag_second(bufs, *second_sems))
                if m:
                    _comm_start(_ag_first(s_in, s_out, *first_sems))

        au, av = u_ref[...].astype(F32), v_ref[...].astype(F32)
        cdf_u, cdf_v = _normal_cdf(au), _normal_cdf(av)
        cdf_ref[0] = cdf_u.astype(BF16)
        cdf_ref[1] = cdf_v.astype(BF16)
        u, v = au * cdf_u, av * cdf_v
        vhat, _ = _layer_norm_fwd(v)
        vln = (vhat * lng_ref[...] + lnb_ref[...]).astype(BF16)
        for c in range(nchunk):
            for h in range(N_HEAD):
                rs, cs = slice(c * CHUNK, (c + 1) * CHUNK), slice(h * HEAD, (h + 1) * HEAD)
                mix[rs, cs] = _dot(wsm_ref[h], vln[rs, cs]) + bias_ref[:, cs]
        y_ref[0] = (u * mix[...] * _silu(ag_ref[...].astype(F32))).astype(BF16)
        halo_ok = (i > 0).astype(F32)
        for c in range(nchunk):
            rs = slice(c * CHUNK, (c + 1) * CHUNK)
            for g, win in enumerate(POOL_WINDOWS):
                cs = slice(g * HEAD, (g + 1) * HEAD)
                cur = p_ref[rs, cs]
                if c == 0:
                    prev = (ph_ref[:, cs].astype(F32) * halo_ok).astype(BF16)
                else:
                    prev = p_ref[(c - 1) * CHUNK:c * CHUNK, cs]
                sums = _dot(band_ref[g, 0], cur) + _dot(band_ref[g, 1], prev)
                dm = sums * _inv_count(i * tm + c * CHUNK, win) - cur.astype(F32)
                mix[rs, cs] = _dot(dm.astype(BF16), pw_ref[g])
        y_ref[1] = (mix[...] * ps_ref[...] * _silu(pg_ref[...].astype(F32))).astype(BF16)
        for h in range(N_HEAD):
            cs = slice(h * HEAD, (h + 1) * HEAD)
            p = _mem_softmax(mq_ref[:, cs], kv_ref[:, cs])
            mix[:, cs] = _dot(p.astype(BF16), kv_ref[:, WIDTH + h * HEAD:WIDTH + (h + 1) * HEAD])
        y_ref[2] = (mix[...] * _silu(mg_ref[...].astype(F32))).astype(BF16)

        if n or m:
            @pl.when(i == nsteps - 1)
            def _():
                if n:
                    _comm_wait(_ag_second(bufs, *second_sems))
                if m:
                    _comm_wait(_ag_first(s_in, s_out, *first_sems))

    blk = tm // CHUNK
    res = pl.pallas_call(
        body, name="abm_fwd_gather" if n or m else "abm_fwd",
        grid=(nsteps,),
        in_specs=[_rows(tm, WIDTH, CB_U), _rows(tm, WIDTH, CB_V), _rows(tm, WIDTH, CB_AGATE),
                  _rows(tm, WIDTH, CB_PIN),
                  pl.BlockSpec((CHUNK, WIDTH), lambda i: (jnp.maximum(i * blk - 1, 0), CB_PIN)),
                  _rows(tm, WIDTH, CB_PGATE), _rows(tm, WIDTH, CB_MQ), _rows(tm, WIDTH, CB_MGATE),
                  _full((1, WIDTH)), _full((1, WIDTH)), _full((N_HEAD, CHUNK, CHUNK)), _full((CHUNK, WIDTH)),
                  _full((4, HEAD, HEAD)), _full((1, WIDTH)), _full((MEM_LEN, 2 * WIDTH)),
                  _full((4, 2, CHUNK, CHUNK))] + [ANY] * (n + m),
        out_specs=[pl.BlockSpec((3, tm, WIDTH), lambda i: (0, i, 0)), pl.BlockSpec((2, tm, WIDTH), lambda i: (0, i, 0))]
                  + [ANY] * (n + m),
        out_shape=[jax.ShapeDtypeStruct((4, S, WIDTH), BF16),
                   jax.ShapeDtypeStruct((2, S, WIDTH), BF16)]
                  + [jax.ShapeDtypeStruct(b.shape, b.dtype) for b in gathered]
                  + [jax.ShapeDtypeStruct((N_DEV,) + s.shape, s.dtype) for s in shards],
        input_output_aliases={16 + a: 2 + a for a in range(n)},
        scratch_shapes=[pltpu.VMEM((tm, WIDTH), F32)] + (_dma_sems(3 * n, 3 * n) if n else [])
                       + (_dma_sems(4 * m, 4 * m, m) if m else []),
        compiler_params=_params(("arbitrary",)),
    )(proj, proj, proj, proj, proj, proj, proj, proj, ln_g, ln_b, wsm, bias_full, pool_w, pool_scale, kv, bands,
      *gathered, *shards)
    return res[0], res[1], list(res[2:2 + n]), list(res[2 + n:])


ATT_TILE = 512


def _attn_fwd(q, qcb, k, kcb, v, vcb, bps):
    S = q.shape[0]
    tm = ATT_TILE
    nb = tm // CHUNK

    nblocks = nb * N_HEAD

    def body(q_ref, k_ref, v_ref, kh_ref, vh_ref, o_ref, l_ref, sc_s, sp_s, pc_s, pp_s):
        i = pl.program_id(0)

        def prev_kv(n, cs):
            if n == 0:
                return kh_ref[:, cs], vh_ref[:, cs]
            ps = slice((n - 1) * CHUNK, n * CHUNK)
            return k_ref[ps, cs], v_ref[ps, cs]

        pens = []
        for n in range(nb):
            rs = slice(n * CHUNK, (n + 1) * CHUNK)
            pens.append(jnp.full((N_HEAD * CHUNK, 1), jnp.where((i * nb + n) % bps != 0, 0.0, NEG), F32))
            for h in range(N_HEAD):
                cs = slice(h * HEAD, (h + 1) * HEAD)
                bs = slice((n * N_HEAD + h) * CHUNK, (n * N_HEAD + h + 1) * CHUNK)
                qh = q_ref[rs, cs]
                sc_s[bs, :] = _dot_nt(qh, k_ref[rs, cs])
                sp_s[bs, :] = _dot_nt(qh, prev_kv(n, cs)[0])
        row = lax.broadcasted_iota(jnp.int32, (nblocks * CHUNK, CHUNK), 0) & (CHUNK - 1)
        col = lax.broadcasted_iota(jnp.int32, (nblocks * CHUNK, CHUNK), 1)
        sc = jnp.where(col <= row, sc_s[...] * ATT_SCALE, NEG)
        sp = jnp.where(col >= row, sp_s[...] * ATT_SCALE, NEG) + jnp.concatenate(pens, axis=0)
        m = jnp.maximum(jnp.max(sc, axis=-1, keepdims=True), jnp.max(sp, axis=-1, keepdims=True))
        ec = jnp.exp(sc - m)
        ep = jnp.exp(sp - m)
        den = jnp.sum(ec, axis=-1, keepdims=True) + jnp.sum(ep, axis=-1, keepdims=True)
        inv = 1.0 / den
        pc_s[...] = (ec * inv).astype(BF16)
        pp_s[...] = (ep * inv).astype(BF16)
        lse = m + jnp.log(den)
        for n in range(nb):
            rs = slice(n * CHUNK, (n + 1) * CHUNK)
            for h in range(N_HEAD):
                cs = slice(h * HEAD, (h + 1) * HEAD)
                bs = slice((n * N_HEAD + h) * CHUNK, (n * N_HEAD + h + 1) * CHUNK)
                o = _dot(pc_s[bs, :], v_ref[rs, cs]) + _dot(pp_s[bs, :], prev_kv(n, cs)[1])
                o_ref[rs, cs] = o.astype(BF16)
            l_ref[rs, :] = _put_cols([lse[(n * N_HEAD + h) * CHUNK:(n * N_HEAD + h + 1) * CHUNK]
                                      for h in range(N_HEAD)])

    def halo(cb):
        return pl.BlockSpec((CHUNK, WIDTH), lambda i: (jnp.maximum(i * nb - 1, 0), cb))

    return pl.pallas_call(
        body, name=f"attn_fwd_{bps}",
        grid=(S // tm,),
        in_specs=[_rows(tm, WIDTH, qcb), _rows(tm, WIDTH, kcb), _rows(tm, WIDTH, vcb), halo(kcb), halo(vcb)],
        out_specs=[_rows(tm, WIDTH), _rows(tm, 128)],
        out_shape=[jax.ShapeDtypeStruct((S, WIDTH), BF16), jax.ShapeDtypeStruct((S, 128), F32)],
        scratch_shapes=[pltpu.VMEM((nblocks * CHUNK, CHUNK), F32), pltpu.VMEM((nblocks * CHUNK, CHUNK), F32),
                        pltpu.VMEM((nblocks * CHUNK, CHUNK), BF16), pltpu.VMEM((nblocks * CHUNK, CHUNK), BF16)],
        compiler_params=_params(("parallel",)),
    )(q, k, v, k, v)


def _gate_specs(tm):
    return [pl.BlockSpec((tm, D_MODEL), lambda i, b=b: (i, b)) for b in range(4)]


Y_SLOT = (0, 1, 3, 2)


def _merge_fwd(x, y4, o_g, l_g, proj, gates, wb, wout, shards=(), half=()):
    S = x.shape[0]
    tm = 512
    n, nh = len(shards), len(half)
    nsteps = S // tm
    forward_at = nsteps // 2

    def body(x_ref, y_ref, o0, o1, o2, l0, l1, l2, cg_ref, *rest):
        gm = rest[:4]
        wb_ref, wo_ref = rest[4:6]
        s_in = rest[6:6 + n]
        rest = rest[6 + n + nh:]
        xn_ref, yc_ref, oc_ref, lse_ref, z_ref = rest[:5]
        s_out, bufs, ocs, sems = rest[5:5 + n], rest[5:5 + n + nh], rest[5 + n + nh], rest[6 + n + nh:]
        i = pl.program_id(0)

        if n:
            @pl.when(i == 0)
            def _():
                _comm_start(_ag_first(s_in, s_out, *sems[:3]))

            @pl.when(i == forward_at)
            def _():
                incoming = _ag_first(s_in, s_out, *sems[:3])[2]
                for a in range(n):
                    for k in range(1, 4):
                        incoming[4 * a + k].wait_recv()
                _comm_start(_ag_second(bufs, *sems[3:]))

        lcols = []
        for h in range(N_HEAD):
            cs = slice(h * HEAD, (h + 1) * HEAD)
            ls = [_col(l[...], h) for l in (l0, l1, l2)]
            m = jnp.maximum(jnp.maximum(ls[0], ls[1]), ls[2])
            tot = jnp.exp(ls[0] - m) + jnp.exp(ls[1] - m) + jnp.exp(ls[2] - m)
            lse = m + jnp.log(tot)
            ocs[:, cs] = sum(jnp.exp(lg - lse) * o[:, cs].astype(F32) for lg, o in zip(ls, (o0, o1, o2)))
            lcols.append(lse)
        lse_ref[...] = _put_cols(lcols)
        oc = ocs[...]
        oc_ref[...] = oc.astype(BF16)
        yc = (oc * _silu(cg_ref[...].astype(F32))).astype(BF16)
        yc_ref[...] = yc
        ys = (y_ref[0], y_ref[1], yc, y_ref[2])
        z = jnp.zeros((tm, D_MODEL), F32)
        for b in range(4):
            z = z + _sigmoid(gm[b][...].astype(F32)) * _dot(ys[b], wb_ref[b])
        zb = z.astype(BF16)
        z_ref[...] = zb
        xn_ref[...] = x_ref[...] + _dot(zb, wo_ref[...])

        if n:
            @pl.when(i == nsteps - 1)
            def _():
                local, out, incoming = _ag_first(s_in, s_out, *sems[:3])
                for a in range(n):
                    incoming[4 * a].wait_recv()
                _comm_wait(_ag_second(bufs, *sems[3:]))
                for cp in out:
                    cp.wait_send()
                for cp in local:
                    cp.wait()

    res = pl.pallas_call(
        body, name="merge_fwd_gather" if n else "merge_fwd",
        grid=(nsteps,),
        in_specs=[_rows(tm, D_MODEL), pl.BlockSpec((3, tm, WIDTH), lambda i: (0, i, 0)),
                  _rows(tm, WIDTH), _rows(tm, WIDTH), _rows(tm, WIDTH),
                  _rows(tm, 128), _rows(tm, 128), _rows(tm, 128),
                  _rows(tm, WIDTH, CB_CGATE)] + _gate_specs(tm)
                 + [_resident((4, WIDTH, D_MODEL)), _resident((D_MODEL, D_MODEL))] + [ANY] * (n + nh),
        out_specs=[_rows(tm, D_MODEL), pl.BlockSpec((None, tm, WIDTH), lambda i: (Y_SLOT[2], i, 0)),
                   _rows(tm, WIDTH), _rows(tm, 128), _rows(tm, D_MODEL)] + [ANY] * (n + nh),
        out_shape=[jax.ShapeDtypeStruct((S, D_MODEL), F32), jax.ShapeDtypeStruct(y4.shape, BF16),
                   jax.ShapeDtypeStruct((S, WIDTH), BF16), jax.ShapeDtypeStruct((S, 128), F32),
                   jax.ShapeDtypeStruct((S, D_MODEL), BF16)]
                  + [jax.ShapeDtypeStruct((N_DEV,) + s.shape, s.dtype) for s in shards]
                  + [jax.ShapeDtypeStruct(b.shape, b.dtype) for b in half],
        input_output_aliases={1: 1, **{15 + n + a: 5 + n + a for a in range(nh)}},
        scratch_shapes=[pltpu.VMEM((tm, WIDTH), F32)]
                       + (_dma_sems(4 * n, 4 * n, n, 3 * (n + nh), 3 * (n + nh)) if n else []),
        compiler_params=_params(("arbitrary",)),
    )(x, y4, *o_g, *l_g, proj, *([gates] * 4), wb, wout, *shards, *half)
    return res[:5], list(res[5:])


def _loss_head(x, g, tgt):
    S = x.shape[0]
    tm = 512

    def body(x_ref, g_ref, t_ref, loss_ref, dx_ref, dg_ref):
        @pl.when(pl.program_id(0) == 0)
        def _():
            loss_ref[...] = jnp.zeros_like(loss_ref)
            dg_ref[...] = jnp.zeros_like(dg_ref)

        xf = x_ref[...]
        r = lax.rsqrt(jnp.mean(xf * xf, axis=-1, keepdims=True) + EPS)
        xhat = xf * r
        gv = g_ref[...]
        err = xhat * gv - t_ref[...]
        e2 = jnp.sum(err * err, axis=-1, keepdims=True)
        loss_ref[...] += (0.5 / D_MODEL) * jnp.sum(e2, axis=0, keepdims=True)
        dy = err * (1.0 / D_MODEL)
        dg_ref[...] += jnp.sum(dy * xhat, axis=0, keepdims=True)
        dxh = dy * gv
        dx_ref[...] = r * (dxh - xhat * jnp.mean(dxh * xhat, axis=-1, keepdims=True))

    return pl.pallas_call(
        body, name="loss_head",
        grid=(S // tm,),
        in_specs=[_rows(tm, D_MODEL), _full((1, D_MODEL)), _rows(tm, D_MODEL)],
        out_specs=[_full((1, 128)), _rows(tm, D_MODEL), _full((1, D_MODEL))],
        out_shape=[jax.ShapeDtypeStruct((1, 128), F32), jax.ShapeDtypeStruct((S, D_MODEL), F32),
                   jax.ShapeDtypeStruct((1, D_MODEL), F32)],
        compiler_params=_params(("arbitrary",)),
    )(x, g, tgt)


def _merge_bwd(dxo, y4, oc, z, proj, gates, wb, wout, grads=()):
    S = dxo.shape[0]
    tm = 256
    n = len(grads)
    nsteps = S // tm

    def body(dx_ref, y_ref, oc_ref, z_ref, cg_ref, *rest):
        gm = rest[:4]
        wb_ref, wo_ref = rest[4:6]
        g_in = rest[6:6 + n]
        dy_ref, doc_ref, delta_ref, dcg_ref, dgm_ref, dwb_ref, dwo_ref = rest[6 + n:13 + n]
        g_out = rest[13 + n:13 + 2 * n]
        acc_b, acc_o = rest[13 + 2 * n:15 + 2 * n]
        sems = rest[15 + 2 * n:]
        i = pl.program_id(0)

        @pl.when(i == 0)
        def _():
            acc_b[...] = jnp.zeros_like(acc_b)
            acc_o[...] = jnp.zeros_like(acc_o)
            if n:
                _comm_start(_rs_first(g_in, g_out, *sems))

        dxb = dx_ref[...].astype(BF16)
        acc_o[...] += _dot_tn(z_ref[...], dxb)
        dz = _dot_nt(dxb, wo_ref[...])
        for b in range(4):
            gate = _sigmoid(gm[b][...].astype(F32))
            yb = y_ref[Y_SLOT[b]]
            t = _dot(yb, wb_ref[b])
            dgm_ref[:, b * D_MODEL:(b + 1) * D_MODEL] = (dz * t * gate * (1.0 - gate)).astype(BF16)
            dt = (dz * gate).astype(BF16)
            acc_b[b] += _dot_tn(yb, dt)
            dyb = _dot_nt(dt, wb_ref[b])
            if b == 2:
                cg = cg_ref[...].astype(F32)
                oc = oc_ref[...].astype(F32)
                scg, dscg = _silu_and_grad(cg)
                doc = dyb * scg
                dcg_ref[...] = (dyb * oc * dscg).astype(BF16)
                doc_ref[...] = doc.astype(BF16)
                prod = doc * oc
                delta_ref[...] = _put_cols([jnp.sum(prod[:, h * HEAD:(h + 1) * HEAD], axis=1, keepdims=True)
                                            for h in range(N_HEAD)])
            else:
                dy_ref[b if b < 2 else 2] = dyb.astype(BF16)

        @pl.when(i == nsteps - 1)
        def _():
            for k in range(N_DEV):
                dwb_ref[k] = acc_b[:, :, k * (D_MODEL // N_DEV):(k + 1) * (D_MODEL // N_DEV)].astype(BF16)
            dwo_ref[...] = acc_o[...].astype(BF16)
            if n:
                _comm_wait(_rs_first(g_in, g_out, *sems))

    res = pl.pallas_call(
        body, name="merge_bwd_scatter" if n else "merge_bwd",
        grid=(nsteps,),
        in_specs=[_rows(tm, D_MODEL), pl.BlockSpec((4, tm, WIDTH), lambda i: (0, i, 0)),
                  _rows(tm, WIDTH), _rows(tm, D_MODEL), _rows(tm, WIDTH, CB_CGATE)] + _gate_specs(tm)
                 + [_resident((4, WIDTH, D_MODEL)), _resident((D_MODEL, D_MODEL))] + [ANY] * n,
        out_specs=[pl.BlockSpec((3, tm, WIDTH), lambda i: (0, i, 0)), _rows(tm, WIDTH), _rows(tm, 128),
                   _rows(tm, WIDTH), _rows(tm, 4 * D_MODEL), _full((N_DEV, 4, WIDTH, D_MODEL // N_DEV)),
                   _full((D_MODEL, D_MODEL))]
                  + [ANY] * n,
        out_shape=[jax.ShapeDtypeStruct((3, S, WIDTH), BF16), jax.ShapeDtypeStruct((S, WIDTH), BF16),
                   jax.ShapeDtypeStruct((S, 128), F32), jax.ShapeDtypeStruct((S, WIDTH), BF16),
                   jax.ShapeDtypeStruct((S, 4 * D_MODEL), BF16),
                   jax.ShapeDtypeStruct((N_DEV, 4, WIDTH, D_MODEL // N_DEV), BF16),
                   jax.ShapeDtypeStruct((D_MODEL, D_MODEL), BF16)]
                  + [jax.ShapeDtypeStruct(g.shape[:1] + g.shape[2:], g.dtype) for g in grads],
        scratch_shapes=[pltpu.VMEM((4, WIDTH, D_MODEL), F32), pltpu.VMEM((D_MODEL, D_MODEL), F32)]
                       + (_dma_sems(N_CHIP * n, N_CHIP * n) if n else []),
        compiler_params=_params(("arbitrary",)),
    )(dxo, y4, oc, z, proj, *([gates] * 4), wb, wout, *grads)
    return res[:7], list(res[7:])


def _attn_bwd(q, qcb, k, kcb, v, vcb, do, lse, delta, bps, dst, dcb):
    S = q.shape[0]
    tm = ATT_TILE
    nb = tm // CHUNK
    nblk = S // CHUNK

    ncur = nb * N_HEAD
    nprev = (nb + 1) * N_HEAD

    def body(q_ref, k_ref, v_ref, do_ref, l_ref, d_ref, kh_ref, vh_ref, qn_ref, don_ref, ln_ref, dn_ref, dst_ref,
             dq_ref, dk_ref, dv_ref, sc_s, sp_s, dpc_s, dpp_s, pc_s, pp_s, dsc_s, dsp_s):
        i = pl.program_id(0)

        def rows_of(n):
            if n < nb:
                rs = slice(n * CHUNK, (n + 1) * CHUNK)
                return rs, q_ref, do_ref, l_ref, d_ref
            return slice(0, CHUNK), qn_ref, don_ref, ln_ref, dn_ref

        def prev_kv(n, cs):
            if n == 0:
                return kh_ref[:, cs], vh_ref[:, cs]
            ps = slice((n - 1) * CHUNK, n * CHUNK)
            return k_ref[ps, cs], v_ref[ps, cs]

        def blk(n, h):
            return slice((n * N_HEAD + h) * CHUNK, (n * N_HEAD + h + 1) * CHUNK)

        pens, lses, deltas = [], [], []
        for n in range(nb + 1):
            rs, qr, dor, lr, dr = rows_of(n)
            gb = i * nb + n
            pen = jnp.where(gb % bps != 0, 0.0, NEG)
            if n == nb:
                pen = pen + jnp.where(gb < nblk, 0.0, NEG)
            pens.append(jnp.full((N_HEAD * CHUNK, 1), pen, F32))
            lblk, dblk = lr[rs, :], dr[rs, :]
            for h in range(N_HEAD):
                cs = slice(h * HEAD, (h + 1) * HEAD)
                qh, doh = qr[rs, cs], dor[rs, cs]
                lses.append(_col(lblk, h))
                deltas.append(_col(dblk, h))
                kp, vp = prev_kv(n, cs)
                sp_s[blk(n, h), :] = _dot_nt(qh, kp)
                dpp_s[blk(n, h), :] = _dot_nt(doh, vp)
                if n < nb:
                    sc_s[blk(n, h), :] = _dot_nt(qh, k_ref[rs, cs])
                    dpc_s[blk(n, h), :] = _dot_nt(doh, v_ref[rs, cs])
        lse = jnp.concatenate(lses, axis=0)
        delta = jnp.concatenate(deltas, axis=0)
        row = lax.broadcasted_iota(jnp.int32, (nprev * CHUNK, CHUNK), 0) & (CHUNK - 1)
        col = lax.broadcasted_iota(jnp.int32, (nprev * CHUNK, CHUNK), 1)
        sp = jnp.where(col >= row, sp_s[...] * ATT_SCALE, NEG) + jnp.concatenate(pens, axis=0)
        pp = jnp.exp(sp - lse)
        pp_s[...] = pp.astype(BF16)
        dsp_s[...] = (pp * (dpp_s[...] - delta)).astype(BF16)
        nc = ncur * CHUNK
        sc = jnp.where(col[:nc] <= row[:nc], sc_s[...] * ATT_SCALE, NEG)
        pc = jnp.exp(sc - lse[:nc])
        pc_s[...] = pc.astype(BF16)
        dsc_s[...] = (pc * (dpc_s[...] - delta[:nc])).astype(BF16)
        for n in range(nb):
            rs, qr, dor, _, _ = rows_of(n)
            rn, qnr, donr, _, _ = rows_of(n + 1)
            for h in range(N_HEAD):
                cs = slice(h * HEAD, (h + 1) * HEAD)
                kp, _ = prev_kv(n, cs)
                dq = _dot(dsc_s[blk(n, h), :], k_ref[rs, cs]) + _dot(dsp_s[blk(n, h), :], kp)
                dq_ref[rs, cs] = (dq * ATT_SCALE).astype(BF16)
                dk = _dot_tn(dsc_s[blk(n, h), :], qr[rs, cs]) + _dot_tn(dsp_s[blk(n + 1, h), :], qnr[rn, cs])
                dk_ref[rs, cs] = (dk * ATT_SCALE).astype(BF16)
                dv = _dot_tn(pc_s[blk(n, h), :], dor[rs, cs]) + _dot_tn(pp_s[blk(n + 1, h), :], donr[rn, cs])
                dv_ref[rs, cs] = dv.astype(BF16)

    def prev_halo(cb):
        return pl.BlockSpec((CHUNK, WIDTH), lambda i: (jnp.maximum(i * nb - 1, 0), cb))

    def next_halo(width, cb=0):
        return pl.BlockSpec((CHUNK, width), lambda i: (jnp.minimum(i * nb + nb, nblk - 1), cb))

    return pl.pallas_call(
        body, name=f"attn_bwd_{bps}",
        grid=(S // tm,),
        in_specs=[_rows(tm, WIDTH, qcb), _rows(tm, WIDTH, kcb), _rows(tm, WIDTH, vcb), _rows(tm, WIDTH),
                  _rows(tm, 128), _rows(tm, 128), prev_halo(kcb), prev_halo(vcb),
                  next_halo(WIDTH, qcb), next_halo(WIDTH), next_halo(128), next_halo(128), ANY],
        out_specs=[_rows(tm, WIDTH, dcb), _rows(tm, WIDTH), _rows(tm, WIDTH)],
        out_shape=[jax.ShapeDtypeStruct(dst.shape, BF16)] + [jax.ShapeDtypeStruct((S, WIDTH), BF16)] * 2,
        input_output_aliases={12: 0},
        scratch_shapes=[pltpu.VMEM((ncur * CHUNK, CHUNK), F32), pltpu.VMEM((nprev * CHUNK, CHUNK), F32),
                        pltpu.VMEM((ncur * CHUNK, CHUNK), F32), pltpu.VMEM((nprev * CHUNK, CHUNK), F32),
                        pltpu.VMEM((ncur * CHUNK, CHUNK), BF16), pltpu.VMEM((nprev * CHUNK, CHUNK), BF16),
                        pltpu.VMEM((ncur * CHUNK, CHUNK), BF16), pltpu.VMEM((nprev * CHUNK, CHUNK), BF16)],
        compiler_params=_params(("parallel",)),
    )(q, k, v, do, lse, delta, k, v, q, do, lse, delta, dst)


def _dilated_split(d):
    hp = min(N_HEAD, 16 // d)
    return hp, N_HEAD // hp, HEAD * hp


def _strided_regroup(d):
    return d < 16


def _by_class(src_ref, dst, d, hp, nat):
    for j in range(hp):
        if _strided_regroup(d):
            nat[j] = src_ref[:, j * HEAD:(j + 1) * HEAD].astype(F32)
            for r in range(d):
                dst[j, r * CHUNK:(r + 1) * CHUNK, :] = nat.at[j][pl.ds(r, CHUNK, stride=d), :].astype(BF16)
        else:
            dst[j] = pltpu.einshape("(tr)l->(rt)l", src_ref[:, j * HEAD:(j + 1) * HEAD], r=d)


def _from_class(src, dst_ref, d, hp, nat, add_ref=None):
    for j in range(hp):
        cs = slice(j * HEAD, (j + 1) * HEAD)
        if _strided_regroup(d):
            for r in range(d):
                nat.at[j][pl.ds(r, CHUNK, stride=d), :] = src[j, r * CHUNK:(r + 1) * CHUNK, :]
            val = nat[j].astype(BF16)
        else:
            val = pltpu.einshape("(rt)l->(tr)l", src[j].astype(BF16), r=d)
        if add_ref is not None:
            val = (val.astype(F32) + add_ref[:, cs].astype(F32)).astype(BF16)
        dst_ref[:, cs] = val


def _attn_fwd_dilated(proj, qcb, kcb, vcb, d):
    S = proj.shape[0]
    T = CHUNK * d
    hp, nh, cw = _dilated_split(d)
    nblocks = d * hp

    def body(q_ref, k_ref, v_ref, o_ref, l_ref, qf, kst, vst, of, lf, nat, sc_s, sp_s, pc_s, pp_s):
        i, hh = pl.program_id(0), pl.program_id(1)
        kf, vf = kst.at[i % 2, hh], vst.at[i % 2, hh]
        kpf, vpf = kst.at[1 - i % 2, hh], vst.at[1 - i % 2, hh]

        @pl.when(i == 0)
        def _():
            kpf[...] = jnp.zeros_like(kpf)
            vpf[...] = jnp.zeros_like(vpf)

        _by_class(q_ref, qf, d, hp, nat)
        _by_class(k_ref, kf, d, hp, nat)
        _by_class(v_ref, vf, d, hp, nat)

        def blk(ref, r, j):
            return ref[j, r * CHUNK:(r + 1) * CHUNK, :]

        def bs(r, j):
            return slice((r * hp + j) * CHUNK, (r * hp + j + 1) * CHUNK)

        for r in range(d):
            for j in range(hp):
                qb = blk(qf, r, j)
                sc_s[bs(r, j), :] = _dot_nt(qb, blk(kf, r, j))
                sp_s[bs(r, j), :] = _dot_nt(qb, blk(kpf, r, j))
        row = lax.broadcasted_iota(jnp.int32, (nblocks * CHUNK, CHUNK), 0) & (CHUNK - 1)
        col = lax.broadcasted_iota(jnp.int32, (nblocks * CHUNK, CHUNK), 1)
        sc = jnp.where(col <= row, sc_s[...] * ATT_SCALE, NEG)
        sp = jnp.where(col >= row, sp_s[...] * ATT_SCALE, NEG) + jnp.where(i > 0, 0.0, NEG)
        m = jnp.maximum(jnp.max(sc, axis=-1, keepdims=True), jnp.max(sp, axis=-1, keepdims=True))
        ec = jnp.exp(sc - m)
        ep = jnp.exp(sp - m)
        den = jnp.sum(ec, axis=-1, keepdims=True) + jnp.sum(ep, axis=-1, keepdims=True)
        inv = 1.0 / den
        pc_s[...] = (ec * inv).astype(BF16)
        pp_s[...] = (ep * inv).astype(BF16)
        lse = m + jnp.log(den)
        lane = lax.broadcasted_iota(jnp.int32, (CHUNK, 128), 1)
        for r in range(d):
            lblk = jnp.zeros((CHUNK, 128), F32)
            for j in range(hp):
                o = _dot(pc_s[bs(r, j), :], blk(vf, r, j)) + _dot(pp_s[bs(r, j), :], blk(vpf, r, j))
                of[j, r * CHUNK:(r + 1) * CHUNK, :] = o
                lblk = jnp.where(lane == hh * hp + j, lse[bs(r, j)], lblk)
            lf[r * CHUNK:(r + 1) * CHUNK, :] = lblk
        _from_class(of, o_ref, d, hp, nat)
        lnat = pltpu.einshape("(rt)l->(tr)l", lf[...], r=d)

        @pl.when(hh == 0)
        def _():
            l_ref[...] = lnat

        @pl.when(hh > 0)
        def _():
            l_ref[...] += lnat

    def cols(cb):
        return pl.BlockSpec((T, cw), lambda i, hh: (i, cb * nh + hh))

    tile = pltpu.VMEM((hp, T, HEAD), BF16)
    staging = pltpu.VMEM((hp, T, HEAD) if _strided_regroup(d) else (1, 8, HEAD), F32)
    return pl.pallas_call(
        body, name=f"attn_fwd_dilated_{d}",
        grid=(S // T, nh),
        in_specs=[cols(qcb), cols(kcb), cols(vcb)],
        out_specs=[cols(0), pl.BlockSpec((T, 128), lambda i, hh: (i, 0))],
        out_shape=[jax.ShapeDtypeStruct((S, WIDTH), BF16), jax.ShapeDtypeStruct((S, 128), F32)],
        scratch_shapes=[tile, pltpu.VMEM((2, nh, hp, T, HEAD), BF16), pltpu.VMEM((2, nh, hp, T, HEAD), BF16),
                        pltpu.VMEM((hp, T, HEAD), F32), pltpu.VMEM((T, 128), F32), staging,
                        pltpu.VMEM((nblocks * CHUNK, CHUNK), F32), pltpu.VMEM((nblocks * CHUNK, CHUNK), F32),
                        pltpu.VMEM((nblocks * CHUNK, CHUNK), BF16), pltpu.VMEM((nblocks * CHUNK, CHUNK), BF16)],
        compiler_params=_params(("arbitrary", "arbitrary")),
    )(proj, proj, proj)


def _attn_bwd_dilated(proj, qcb, kcb, vcb, do, lse, delta, d, dk_in, dv_in, dst, dcb):
    S = proj.shape[0]
    T = CHUNK * d
    nt = S // T
    hp, nh, cw = _dilated_split(d)
    nblocks = d * hp

    def body(q_ref, k_ref, v_ref, do_ref, l_ref, d_ref, dki_ref, dvi_ref, dst_ref, dq_ref, dk_ref, dv_ref,
             qf, dof, kbuf, vbuf, dqf, gk, gv, nat,
             sc_s, sp_s, dpc_s, dpp_s, pc_s, pp_s, dsc_s, dsp_s):
        hh, i = pl.program_id(0), pl.program_id(1)
        kf, vf, newk, newv = kbuf.at[i % 2], vbuf.at[i % 2], gk.at[i % 2], gv.at[i % 2]
        kpf, vpf, acck, accv = kbuf.at[1 - i % 2], vbuf.at[1 - i % 2], gk.at[1 - i % 2], gv.at[1 - i % 2]

        @pl.when(i == 0)
        def _():
            for ref in (kbuf, vbuf, gk, gv):
                ref[...] = jnp.zeros_like(ref)
            dk_ref[...] = jnp.zeros_like(dk_ref)
            dv_ref[...] = jnp.zeros_like(dv_ref)

        def blk(ref, r, j):
            return ref[j, r * CHUNK:(r + 1) * CHUNK, :]

        def bs(r, j):
            return slice((r * hp + j) * CHUNK, (r * hp + j + 1) * CHUNK)

        @pl.when(i < nt)
        def _():
            _by_class(q_ref, qf, d, hp, nat)
            _by_class(do_ref, dof, d, hp, nat)
            _by_class(k_ref, kf, d, hp, nat)
            _by_class(v_ref, vf, d, hp, nat)
            lses, deltas = [], []
            lcls = pltpu.einshape("(tr)l->(rt)l", l_ref[...], r=d)
            dcls = pltpu.einshape("(tr)l->(rt)l", d_ref[...], r=d)
            for r in range(d):
                lblk = lcls[r * CHUNK:(r + 1) * CHUNK]
                dblk = dcls[r * CHUNK:(r + 1) * CHUNK]
                for j in range(hp):
                    lses.append(_col(lblk, hh * hp + j))
                    deltas.append(_col(dblk, hh * hp + j))
                    qb, dob = blk(qf, r, j), blk(dof, r, j)
                    sc_s[bs(r, j), :] = _dot_nt(qb, blk(kf, r, j))
                    dpc_s[bs(r, j), :] = _dot_nt(dob, blk(vf, r, j))
                    sp_s[bs(r, j), :] = _dot_nt(qb, blk(kpf, r, j))
                    dpp_s[bs(r, j), :] = _dot_nt(dob, blk(vpf, r, j))
            lse = jnp.concatenate(lses, axis=0)
            delta = jnp.concatenate(deltas, axis=0)
            row = lax.broadcasted_iota(jnp.int32, (nblocks * CHUNK, CHUNK), 0) & (CHUNK - 1)
            col = lax.broadcasted_iota(jnp.int32, (nblocks * CHUNK, CHUNK), 1)
            sp = jnp.where(col >= row, sp_s[...] * ATT_SCALE, NEG) + jnp.where(i > 0, 0.0, NEG)
            pp = jnp.exp(sp - lse)
            pp_s[...] = pp.astype(BF16)
            dsp_s[...] = (pp * (dpp_s[...] - delta)).astype(BF16)
            sc = jnp.where(col <= row, sc_s[...] * ATT_SCALE, NEG)
            pc = jnp.exp(sc - lse)
            pc_s[...] = pc.astype(BF16)
            dsc_s[...] = (pc * (dpc_s[...] - delta)).astype(BF16)
            for r in range(d):
                rows = slice(r * CHUNK, (r + 1) * CHUNK)
                for j in range(hp):
                    qb, dob = blk(qf, r, j), blk(dof, r, j)
                    dsc, dsp = dsc_s[bs(r, j), :], dsp_s[bs(r, j), :]
                    dqf[j, rows, :] = (_dot(dsc, blk(kf, r, j)) + _dot(dsp, blk(kpf, r, j))) * ATT_SCALE
                    newk[j, rows, :] = _dot_tn(dsc, qb) * ATT_SCALE
                    newv[j, rows, :] = _dot_tn(pc_s[bs(r, j), :], dob)
                    acck[j, rows, :] += _dot_tn(dsp, qb) * ATT_SCALE
                    accv[j, rows, :] += _dot_tn(pp_s[bs(r, j), :], dob)
            _from_class(dqf, dq_ref, d, hp, nat)

        @pl.when(i > 0)
        def _():
            _from_class(acck, dk_ref, d, hp, nat, dki_ref)
            _from_class(accv, dv_ref, d, hp, nat, dvi_ref)

    def cur(width, cb, nsplit):
        return pl.BlockSpec((T, width), lambda hh, i: (jnp.minimum(i, nt - 1), cb * nsplit + hh * (nsplit > 1)))

    def lag():
        return pl.BlockSpec((T, cw), lambda hh, i: (jnp.maximum(i - 1, 0), hh))

    tile = pltpu.VMEM((hp, T, HEAD), BF16)
    acc = pltpu.VMEM((hp, T, HEAD), F32)
    f32s = pltpu.VMEM((nblocks * CHUNK, CHUNK), F32)
    b16s = pltpu.VMEM((nblocks * CHUNK, CHUNK), BF16)
    return pl.pallas_call(
        body, name=f"attn_bwd_dilated_{d}",
        grid=(nh, nt + 1),
        in_specs=[cur(cw, qcb, nh), cur(cw, kcb, nh), cur(cw, vcb, nh), cur(cw, 0, nh), cur(128, 0, 1), cur(128, 0, 1),
                  lag(), lag(), ANY],
        out_specs=[cur(cw, dcb, nh), lag(), lag()],
        out_shape=[jax.ShapeDtypeStruct(dst.shape, BF16)] + [jax.ShapeDtypeStruct((S, WIDTH), BF16)] * 2,
        input_output_aliases={8: 0},
        scratch_shapes=[tile, tile, pltpu.VMEM((2, hp, T, HEAD), BF16), pltpu.VMEM((2, hp, T, HEAD), BF16), acc,
                        pltpu.VMEM((2, hp, T, HEAD), F32), pltpu.VMEM((2, hp, T, HEAD), F32),
                        acc if _strided_regroup(d) else pltpu.VMEM((1, 8, HEAD), F32)]
                       + [f32s] * 4 + [b16s] * 4,
        compiler_params=_params(("arbitrary", "arbitrary")),
    )(proj, proj, proj, do, lse, delta, dk_in, dv_in, dst)


def _abm_bwd(proj, cdf, dy3, ln_g, ln_b, wsm, wsm_t, bias_full, pool_w, pool_wt, pool_scale, kv, bands, bands_t):
    S = proj.shape[0]
    tm = 512
    nchunk = tm // CHUNK
    nblk = S // CHUNK

    def body(u_ref, v_ref, ag_ref, p_ref, ph_ref, pg_ref, pgn_ref, mq_ref, mg_ref, cdf_ref, dy_ref, dypn_ref,
             lng_ref, lnb_ref, wsm_ref, wsmt_ref, bias_ref, pw_ref, pwt_ref, ps_ref, kv_ref, band_ref, bandt_ref,
             dab_ref, dm_ref, dlng_ref, dlnb_ref, dws_ref, dbias_ref, dpw_ref, dps_ref, dkv_ref,
             mix, dvl, ddn):
        i = pl.program_id(0)

        @pl.when(i == 0)
        def _():
            for r in (dlng_ref, dlnb_ref, dws_ref, dbias_ref, dpw_ref, dps_ref, dkv_ref):
                r[...] = jnp.zeros_like(r)

        au = u_ref[...].astype(F32)
        av = v_ref[...].astype(F32)
        ag = ag_ref[...].astype(F32)
        u, du = _gelu_and_grad(au, cdf_ref[0].astype(F32))
        v, dgelu_v = _gelu_and_grad(av, cdf_ref[1].astype(F32))
        vhat, rstd = _layer_norm_fwd(v)
        vln = (vhat * lng_ref[...] + lnb_ref[...]).astype(BF16)
        for c in range(nchunk):
            for h in range(N_HEAD):
                rs, cs = slice(c * CHUNK, (c + 1) * CHUNK), slice(h * HEAD, (h + 1) * HEAD)
                mix[rs, cs] = _dot(wsm_ref[h], vln[rs, cs]) + bias_ref[:, cs]
        dya = dy_ref[0].astype(F32)
        sg, dsg = _silu_and_grad(ag)
        mixed = mix[...]
        dab_ref[:, 2 * WIDTH:3 * WIDTH] = (dya * u * mixed * dsg).astype(BF16)
        dab_ref[:, 0:WIDTH] = (dya * mixed * sg * du).astype(BF16)
        dmixed = dya * u * sg
        dmb = dmixed.astype(BF16)
        tril = (lax.broadcasted_iota(jnp.int32, (CHUNK, CHUNK), 1)
                <= lax.broadcasted_iota(jnp.int32, (CHUNK, CHUNK), 0))
        for c in range(nchunk):
            rs = slice(c * CHUNK, (c + 1) * CHUNK)
            dbias_ref[...] += dmixed[rs, :]
            for h in range(N_HEAD):
                cs = slice(h * HEAD, (h + 1) * HEAD)
                dvl[rs, cs] = _dot(wsmt_ref[h], dmb[rs, cs])
                dws_ref[h] += jnp.where(tril, _dot_nt(dmb[rs, cs], vln[rs, cs]), 0.0)
        dvln = dvl[...]
        dlng_ref[...] += jnp.sum(dvln * vhat, axis=0, keepdims=True)
        dlnb_ref[...] += jnp.sum(dvln, axis=0, keepdims=True)
        dvh = dvln * lng_ref[...]
        dv = rstd * (dvh - jnp.mean(dvh, axis=-1, keepdims=True)
                     - vhat * jnp.mean(dvh * vhat, axis=-1, keepdims=True))
        dab_ref[:, WIDTH:2 * WIDTH] = (dv * dgelu_v).astype(BF16)

        halo_ok = (i > 0).astype(F32)
        for c in range(nchunk):
            rs = slice(c * CHUNK, (c + 1) * CHUNK)
            for g, win in enumerate(POOL_WINDOWS):
                cs = slice(g * HEAD, (g + 1) * HEAD)
                cur = p_ref[rs, cs]
                if c == 0:
                    prev = (ph_ref[:, cs].astype(F32) * halo_ok).astype(BF16)
                else:
                    prev = p_ref[(c - 1) * CHUNK:c * CHUNK, cs]
                sums = _dot(band_ref[g, 0], cur) + _dot(band_ref[g, 1], prev)
                dvl[rs, cs] = sums * _inv_count(i * tm + c * CHUNK, win) - cur.astype(F32)
        dmat = dvl[...].astype(BF16)
        for g in range(4):
            cs = slice(g * HEAD, (g + 1) * HEAD)
            mix[:, cs] = _dot(dmat[:, cs], pw_ref[g])
        yg = mix[...]
        pg = pg_ref[...].astype(F32)
        dyp = dy_ref[1].astype(F32)
        spg, dspg = _silu_and_grad(pg)
        dyy = dyp * spg
        scale = ps_ref[...]
        dab_ref[:, 4 * WIDTH:5 * WIDTH] = (dyp * yg * scale * dspg).astype(BF16)
        dps_ref[...] += jnp.sum(dyy * yg, axis=0, keepdims=True)
        dyg = (dyy * scale).astype(BF16)
        for g in range(4):
            cs = slice(g * HEAD, (g + 1) * HEAD)
            dpw_ref[g] += _dot_tn(dmat[:, cs], dyg[:, cs])
            mix[:, cs] = _dot(dyg[:, cs], pwt_ref[g])
        next_ok = (i + 1 < S // tm).astype(F32)
        dygn = (dypn_ref[...].astype(F32) * _silu(pgn_ref[...].astype(F32)) * scale * next_ok).astype(BF16)
        for c in range(nchunk + 1):
            for g, win in enumerate(POOL_WINDOWS):
                cs = slice(g * HEAD, (g + 1) * HEAD)
                if c < nchunk:
                    dd = mix[c * CHUNK:(c + 1) * CHUNK, cs]
                else:
                    dd = _dot(dygn[:, cs], pwt_ref[g])
                ddn[c * CHUNK:(c + 1) * CHUNK, cs] = dd * _inv_count(i * tm + c * CHUNK, win)
        ddnb = ddn[...].astype(BF16)
        for c in range(nchunk):
            rs = slice(c * CHUNK, (c + 1) * CHUNK)
            ns = slice((c + 1) * CHUNK, (c + 2) * CHUNK)
            for g, win in enumerate(POOL_WINDOWS):
                cs = slice(g * HEAD, (g + 1) * HEAD)
                dp = _dot(bandt_ref[g, 0], ddnb[rs, cs]) + _dot(bandt_ref[g, 1], ddnb[ns, cs]) - mix[rs, cs]
                dab_ref[rs, 3 * WIDTH + g * HEAD:3 * WIDTH + (g + 1) * HEAD] = dp.astype(BF16)

        mg = mg_ref[...].astype(F32)
        dym = dy_ref[2].astype(F32)
        smg, dsmg = _silu_and_grad(mg)
        dob = (dym * smg).astype(BF16)
        for h in range(N_HEAD):
            cs = slice(h * HEAD, (h + 1) * HEAD)
            vs = slice(WIDTH + h * HEAD, WIDTH + (h + 1) * HEAD)
            qh = mq_ref[:, cs]
            p = _mem_softmax(qh, kv_ref[:, cs])
            pb = p.astype(BF16)
            mix[:, cs] = _dot(pb, kv_ref[:, vs])
            dp = _dot_nt(dob[:, cs], kv_ref[:, vs])
            ds = (p * (dp - jnp.sum(p * dp, axis=-1, keepdims=True))).astype(BF16)
            dm_ref[:, cs] = (_dot(ds, kv_ref[:, cs]) * ATT_SCALE).astype(BF16)
            dkv_ref[:, cs] += _dot_tn(ds, qh) * ATT_SCALE
            dkv_ref[:, vs] += _dot_tn(pb, dob[:, cs])
        dm_ref[:, WIDTH:2 * WIDTH] = (dym * mix[...] * dsmg).astype(BF16)

    blk = tm // CHUNK
    small = [_full((1, WIDTH)), _full((1, WIDTH)), _full((N_HEAD, CHUNK, CHUNK)), _full((CHUNK, WIDTH)),
             _full((4, HEAD, HEAD)), _full((1, WIDTH)), _full((MEM_LEN, 2 * WIDTH))]
    return pl.pallas_call(
        body, name="abm_bwd",
        grid=(S // tm,),
        in_specs=[_rows(tm, WIDTH, CB_U), _rows(tm, WIDTH, CB_V), _rows(tm, WIDTH, CB_AGATE),
                  _rows(tm, WIDTH, CB_PIN),
                  pl.BlockSpec((CHUNK, WIDTH), lambda i: (jnp.maximum(i * blk - 1, 0), CB_PIN)),
                  _rows(tm, WIDTH, CB_PGATE),
                  pl.BlockSpec((CHUNK, WIDTH), lambda i: (jnp.minimum(i * blk + blk, nblk - 1), CB_PGATE)),
                  _rows(tm, WIDTH, CB_MQ), _rows(tm, WIDTH, CB_MGATE),
                  pl.BlockSpec((2, tm, WIDTH), lambda i: (0, i, 0)),
                  pl.BlockSpec((3, tm, WIDTH), lambda i: (0, i, 0)),
                  pl.BlockSpec((None, CHUNK, WIDTH), lambda i: (1, jnp.minimum(i * blk + blk, nblk - 1), 0)),
                  _full((1, WIDTH)), _full((1, WIDTH)), _full((N_HEAD, CHUNK, CHUNK)), _full((N_HEAD, CHUNK, CHUNK)),
                  _full((CHUNK, WIDTH)), _full((4, HEAD, HEAD)), _full((4, HEAD, HEAD)), _full((1, WIDTH)),
                  _full((MEM_LEN, 2 * WIDTH)), _full((4, 2, CHUNK, CHUNK)), _full((4, 2, CHUNK, CHUNK))],
        out_specs=[_rows(tm, 5 * WIDTH), _rows(tm, 2 * WIDTH)] + small,
        out_shape=[jax.ShapeDtypeStruct((S, D_BRANCHES), BF16), jax.ShapeDtypeStruct((S, 2 * WIDTH), BF16),
                   jax.ShapeDtypeStruct((1, WIDTH), F32), jax.ShapeDtypeStruct((1, WIDTH), F32),
                   jax.ShapeDtypeStruct((N_HEAD, CHUNK, CHUNK), F32), jax.ShapeDtypeStruct((CHUNK, WIDTH), F32),
                   jax.ShapeDtypeStruct((4, HEAD, HEAD), F32), jax.ShapeDtypeStruct((1, WIDTH), F32),
                   jax.ShapeDtypeStruct((MEM_LEN, 2 * WIDTH), F32)],
        scratch_shapes=[pltpu.VMEM((tm, WIDTH), F32), pltpu.VMEM((tm, WIDTH), F32),
                        pltpu.VMEM((tm + CHUNK, WIDTH), F32)],
        compiler_params=_params(("arbitrary",)),
    )(proj, proj, proj, proj, proj, proj, proj, proj, proj, cdf, dy3, dy3,
      ln_g, ln_b, wsm, wsm_t, bias_full, pool_w, pool_wt, pool_scale, kv, bands, bands_t)


def _bias_reduce(dbias_full):
    def body(d_ref, o_ref):
        d = d_ref[...]
        o_ref[...] = _put_cols([jnp.sum(d[:, h * HEAD:(h + 1) * HEAD], axis=1, keepdims=True) for h in range(N_HEAD)])

    return pl.pallas_call(body, name="bias_reduce", out_shape=jax.ShapeDtypeStruct((CHUNK, 128), F32))(dbias_full)


def _mem_bwd(mem, g, mem_n, w, dkv):
    def body(m_ref, g_ref, mn_ref, w_ref, dkv_ref, dw_ref, dg_ref):
        dkvb = dkv_ref[...].astype(BF16)
        dw_ref[...] = _dot_tn(mn_ref[...], dkvb).astype(BF16)
        dmn = _dot_nt(dkvb, w_ref[...])
        xf = m_ref[...]
        r = lax.rsqrt(jnp.mean(xf * xf, axis=-1, keepdims=True) + EPS)
        dg_ref[...] = jnp.sum(dmn * xf * r, axis=0, keepdims=True)

    return pl.pallas_call(
        body, name="mem_bwd",
        out_shape=[jax.ShapeDtypeStruct((D_MODEL, 2 * WIDTH), BF16), jax.ShapeDtypeStruct((1, D_MODEL), F32)],
        compiler_params=pltpu.CompilerParams(vmem_limit_bytes=VMEM_LIMIT),
    )(mem, g, mem_n, w, dkv)


def _dh_bwd(dpb, dpg, wt, x, g, dxo, parts=(), grads=()):
    S = x.shape[0]
    tm, tkb, tkg = 1024, D_BRANCHES // 4, D_GATES // 4
    nkb, nkg = 4, 4
    nk = nkb + nkg
    ni = S // tm
    n, m = len(parts), len(grads)

    def body(dpb_ref, wbr_ref, dpg_ref, wg0_ref, wg1_ref, x_ref, g_ref, dxo_ref, *rest):
        p_in, g_in = rest[:n], rest[n:n + m]
        dx_ref, dg_ref = rest[n + m:n + m + 2]
        p_out, g_out = rest[n + m + 2:2 * n + m + 2], rest[2 * n + m + 2:2 * (n + m) + 2]
        acc, sems = rest[2 * (n + m) + 2], rest[2 * (n + m) + 3:]
        second_sems, first_sems = (sems[:3] if n else ()), sems[3 if n else 0:]
        i, kk = pl.program_id(0), pl.program_id(1)

        @pl.when(jnp.logical_and(i == 0, kk == 0))
        def _():
            dg_ref[...] = jnp.zeros_like(dg_ref)
            if n:
                _comm_start(_rs_second(p_in, p_out, *second_sems))
            if m:
                _comm_start(_rs_first(g_in, g_out, *first_sems))

        @pl.when(kk == 0)
        def _():
            acc[...] = jnp.zeros_like(acc)

        @pl.when(kk < nkb)
        def _():
            acc[...] += _dot(dpb_ref[...], wbr_ref[...])

        @pl.when(kk >= nkb)
        def _():
            acc[...] += _dot(dpg_ref[:, :tkg // 2], wg0_ref[...]) + _dot(dpg_ref[:, tkg // 2:], wg1_ref[...])

        @pl.when(kk == nk - 1)
        def _():
            xf = x_ref[...]
            r = lax.rsqrt(jnp.mean(xf * xf, axis=-1, keepdims=True) + EPS)
            xhat = xf * r
            dh = acc[...]
            dg_ref[...] += jnp.sum(dh * xhat, axis=0, keepdims=True)
            dxh = dh * g_ref[...]
            dx_ref[...] = dxo_ref[...] + r * (dxh - xhat * jnp.mean(dxh * xhat, axis=-1, keepdims=True))

        if n or m:
            @pl.when(jnp.logical_and(i == ni - 1, kk == nk - 1))
            def _():
                if n:
                    _comm_wait(_rs_second(p_in, p_out, *second_sems))
                if m:
                    _comm_wait(_rs_first(g_in, g_out, *first_sems))

    def gate_rows(half):
        return pl.BlockSpec((tkg // 2, D_MODEL),
                            lambda i, k: (D_BRANCHES // (tkg // 2) + 2 * jnp.maximum(k - nkb, 0) + half, 0))

    res = pl.pallas_call(
        body, name="dh_bwd_scatter" if (n or m) else "dh_bwd",
        grid=(ni, nk),
        in_specs=[pl.BlockSpec((tm, tkb), lambda i, k: (i, jnp.minimum(k, nkb - 1))),
                  pl.BlockSpec((tkb, D_MODEL), lambda i, k: (jnp.minimum(k, nkb - 1), 0)),
                  pl.BlockSpec((tm, tkg), lambda i, k: (i, jnp.maximum(k - nkb, 0))),
                  gate_rows(0), gate_rows(1),
                  pl.BlockSpec((tm, D_MODEL), lambda i, k: (i, 0)), pl.BlockSpec((1, D_MODEL), lambda i, k: (0, 0)),
                  pl.BlockSpec((tm, D_MODEL), lambda i, k: (i, 0))] + [ANY] * (n + m),
        out_specs=[pl.BlockSpec((tm, D_MODEL), lambda i, k: (i, 0)), pl.BlockSpec((1, D_MODEL), lambda i, k: (0, 0))]
                  + [ANY] * (n + m),
        out_shape=[jax.ShapeDtypeStruct((S, D_MODEL), F32), jax.ShapeDtypeStruct((1, D_MODEL), F32)]
                  + [jax.ShapeDtypeStruct(p.shape, p.dtype) for p in parts]
                  + [jax.ShapeDtypeStruct(gr.shape[:1] + gr.shape[2:], gr.dtype) for gr in grads],
        scratch_shapes=[pltpu.VMEM((tm, D_MODEL), F32)] + (_dma_sems(3 * n, 3 * n, n) if n else [])
                       + (_dma_sems(N_CHIP * m, N_CHIP * m) if m else []),
        compiler_params=_params(("arbitrary", "arbitrary")),
    )(dpb, wt, dpg, wt, wt, x, g, dxo, *parts, *grads)
    return res[0], res[1], list(res[2:2 + n]), list(res[2 + n:])


def _dw_in(h, dpb, dpg, parts=(), grads=()):
    S = h.shape[0]
    tk = 2048
    nk = S // tk
    n, m = len(parts), len(grads)
    tmb = D_BRANCHES // 4
    ng = D_GATES // GATE_TILE

    def accumulate(a_ref, h_ref, o_ref, acc):
        kk = pl.program_id(1)

        @pl.when(kk == 0)
        def _():
            acc[...] = jnp.zeros_like(acc)

        acc[...] += _dot_tn(a_ref[...], h_ref[...])

        @pl.when(kk == nk - 1)
        def _():
            o_ref[...] = acc[...].astype(BF16)

    def branches(a_ref, h_ref, *rest):
        p_in, g_in, o_ref = rest[:n], rest[n:n + m], rest[n + m]
        p_out, g_out = rest[n + m + 1:2 * n + m + 1], rest[2 * n + m + 1:2 * (n + m) + 1]
        acc, sems = rest[2 * (n + m) + 1], rest[2 * (n + m) + 2:]
        second_sems, first_sems = (sems[:3] if n else ()), sems[3 if n else 0:]
        i, kk = pl.program_id(0), pl.program_id(1)

        if n or m:
            @pl.when(jnp.logical_and(i == 0, kk == 0))
            def _():
                if n:
                    _comm_start(_rs_second(p_in, p_out, *second_sems))
                if m:
                    _comm_start(_rs_first(g_in, g_out, *first_sems))

        accumulate(a_ref, h_ref, o_ref, acc)

        if n or m:
            @pl.when(jnp.logical_and(i == 3, kk == nk - 1))
            def _():
                if n:
                    _comm_wait(_rs_second(p_in, p_out, *second_sems))
                if m:
                    _comm_wait(_rs_first(g_in, g_out, *first_sems))

    def gates(a_ref, h_ref, dst_ref, *rest):
        q_in, o_ref, q_out, acc, sems = rest[:m], rest[m], rest[m + 1:2 * m + 1], rest[2 * m + 1], rest[2 * m + 2:]
        i, kk = pl.program_id(0), pl.program_id(1)

        if m:
            @pl.when(jnp.logical_and(i == 0, kk == 0))
            def _():
                _comm_start(_rs_second(q_in, q_out, *sems))

        accumulate(a_ref, h_ref, o_ref, acc)

        if m:
            @pl.when(jnp.logical_and(i == ng - 1, kk == nk - 1))
            def _():
                _comm_wait(_rs_second(q_in, q_out, *sems))

    res = pl.pallas_call(
        branches, name="dw_in_branches_scatter" if (n or m) else "dw_in_branches",
        grid=(4, nk),
        in_specs=[pl.BlockSpec((tk, tmb), lambda i, k: (k, i)), pl.BlockSpec((tk, D_MODEL), lambda i, k: (k, 0))]
                 + [ANY] * (n + m),
        out_specs=[pl.BlockSpec((tmb, D_MODEL), lambda i, k: (i, 0))] + [ANY] * (n + m),
        out_shape=[jax.ShapeDtypeStruct((D_IN, D_MODEL), BF16)]
                  + [jax.ShapeDtypeStruct(p.shape, p.dtype) for p in parts]
                  + [jax.ShapeDtypeStruct(gr.shape[:1] + gr.shape[2:], gr.dtype) for gr in grads],
        scratch_shapes=[pltpu.VMEM((tmb, D_MODEL), F32)] + (_dma_sems(3 * n, 3 * n, n) if n else [])
                       + (_dma_sems(N_CHIP * m, N_CHIP * m) if m else []),
        compiler_params=_params(("arbitrary", "arbitrary")),
    )(dpb, h, *parts, *grads)
    pair = _pair_sum(grads, res[1 + n:]) if m else []
    res2 = pl.pallas_call(
        gates, name="dw_in_gates_scatter" if m else "dw_in_gates",
        grid=(ng, nk),
        in_specs=[pl.BlockSpec((tk, GATE_TILE), lambda i, k: (k, i)), pl.BlockSpec((tk, D_MODEL), lambda i, k: (k, 0)),
                  ANY] + [ANY] * m,
        out_specs=[pl.BlockSpec((GATE_TILE, D_MODEL), lambda i, k: (D_BRANCHES // GATE_TILE + i, 0))] + [ANY] * m,
        out_shape=[jax.ShapeDtypeStruct((D_IN, D_MODEL), BF16)] + [jax.ShapeDtypeStruct(p.shape, p.dtype) for p in pair],
        input_output_aliases={2: 0},
        scratch_shapes=[pltpu.VMEM((GATE_TILE, D_MODEL), F32)] + (_dma_sems(3 * m, 3 * m, m) if m else []),
        compiler_params=_params(("arbitrary", "arbitrary")),
    )(dpg, h, res[0], *pair)
    return res2[0], list(res[1:1 + n]), list(res2[1:])


def _row_tile(R, C, block_bytes=2 << 20):
    for cand in range(min(R, block_bytes // (C * 4)) // 8 * 8, 0, -8):
        if R % cand == 0:
            return cand
    return R


def _adamw_update(p_ref, w_ref, m_ref, v_ref, g_ref, d_ref, nm_ref, nv_ref):
    c1 = 1.0 / (1.0 - ADAM_B1 ** ADAM_STEP)
    c2 = 1.0 / (1.0 - ADAM_B2 ** ADAM_STEP)
    g = p_ref[0].astype(F32)
    for k in range(1, p_ref.shape[0]):
        g = g + p_ref[k].astype(F32)
    nm = ADAM_B1 * m_ref[...] + (1.0 - ADAM_B1) * g
    nv = ADAM_B2 * v_ref[...] + (1.0 - ADAM_B2) * (g * g)
    g_ref[...] = g
    nm_ref[...] = nm
    nv_ref[...] = nv
    d_ref[...] = -ADAM_LR * ((nm * c1) / (jnp.sqrt(nv * c2) + ADAM_EPS) + ADAM_WD * w_ref[...])


def _adamw(parts, w, m, v, name):
    P, R, C = parts.shape
    tr = _row_tile(R, C)

    def body(*refs):
        _adamw_update(*refs)

    spec = pl.BlockSpec((tr, C), lambda i: (i, 0))
    return pl.pallas_call(
        body, name=name,
        grid=(R // tr,),
        in_specs=[pl.BlockSpec((P, tr, C), lambda i: (0, i, 0)), spec, spec, spec],
        out_specs=[spec] * 4,
        out_shape=[jax.ShapeDtypeStruct((R, C), F32)] * 4,
        compiler_params=_params(("parallel",)),
    )(parts, w, m, v)


def _adamw_layers(parts, w, m, v, name):
    depth = len(parts)
    P, R, C = parts[0].shape
    tr = _row_tile(R, C, 1 << 20)

    def body(*refs):
        layer = pl.program_id(0)
        for k in range(depth):
            @pl.when(layer == k)
            def _(k=k):
                _adamw_update(refs[k], *refs[depth:])

    def part_spec(k):
        return pl.BlockSpec((P, tr, C), lambda l, i: (0, jnp.where(l == k, i, 0), 0))

    spec = pl.BlockSpec((None, tr, C), lambda l, i: (l, i, 0))
    return pl.pallas_call(
        body, name=name,
        grid=(depth, R // tr),
        in_specs=[part_spec(k) for k in range(depth)] + [spec] * 3,
        out_specs=[spec] * 4,
        out_shape=[jax.ShapeDtypeStruct((depth, R, C), F32)] * 4,
        compiler_params=_params(("arbitrary", "arbitrary")),
    )(*parts, w, m, v)


def _place():
    return lax.axis_index("x"), lax.axis_index("y"), lax.axis_index("c")


def _all_gather(shards):
    n = len(shards)

    def body(*refs):
        ins, outs = refs[:n], refs[n:2 * n]
        send1, recv1, local_sems, relay_send, relay_recv, send2, recv2 = refs[2 * n:]
        x, y, c = _place()
        me, sibling = (x, y, c), (x, y, 1 - c)
        x_nbr, y_nbr, diagonal = [(*chip, c) for chip in _other_chips(x, y)]
        local, first_out, first_in, relay_out, relay_in = [], [], [], [], []
        for a in range(n):
            local.append(pltpu.make_async_copy(ins[a], outs[a].at[_dev(me)], local_sems.at[a]))
            for k, to in enumerate((sibling, x_nbr, y_nbr)):
                first_out.append(_remote(ins[a], outs[a].at[_dev(me)], send1, recv1, 3 * a + k, to))
                first_in.append(_remote(ins[a], outs[a].at[_dev(to)], send1, recv1, 3 * a + k, to))
            half = shards[a].shape[0] // 2
            for k, (src, to, rows) in enumerate(((x_nbr, y_nbr, pl.ds(0, half)), (y_nbr, x_nbr, pl.ds(half, half)))):
                passed, got = outs[a].at[_dev(src)].at[rows], outs[a].at[_dev(diagonal)].at[rows]
                relay_out.append(_remote(passed, passed, relay_send, relay_recv, 2 * a + k, to))
                relay_in.append(_remote(got, got, relay_send, relay_recv, 2 * a + k, to))
        second = _ag_second(outs, send2, recv2)
        for cp in local + first_out:
            cp.start()
        for a in range(n):
            for k in range(2):
                first_in[3 * a + 1 + k].wait_recv()
                relay_out[2 * a + k].start()
                second[1][3 * a + k].start()
        for a in range(n):
            for k in range(2):
                relay_in[2 * a + k].wait_recv()
            second[1][3 * a + 2].start()
        for a in range(n):
            first_in[3 * a].wait_recv()
        for cp in second[2]:
            cp.wait_recv()
        for cp in first_out + relay_out + second[1]:
            cp.wait_send()
        for cp in local:
            cp.wait()

    assert all(s.shape[0] % 32 == 0 for s in shards)
    return pl.pallas_call(
        body, name="weights_all_gather",
        in_specs=[ANY] * n, out_specs=[ANY] * n,
        out_shape=[jax.ShapeDtypeStruct((N_DEV,) + s.shape, s.dtype) for s in shards],
        scratch_shapes=_dma_sems(3 * n, 3 * n, n, 2 * n, 2 * n, 3 * n, 3 * n),
        compiler_params=pltpu.CompilerParams(has_side_effects=True),
    )(*shards)


N_BIG = 4


def _dev(p):
    return 4 * p[0] + 2 * p[1] + p[2]


def _other_chips(x, y):
    return [(1 - x, y), (x, 1 - y), (1 - x, 1 - y)]


def _remote(src, dst, send_sems, recv_sems, k, to):
    return pltpu.make_async_remote_copy(src_ref=src, dst_ref=dst, send_sem=send_sems.at[k], recv_sem=recv_sems.at[k],
                                        device_id=to, device_id_type=MESH)


def _ag_first(ins, outs, send_sems, recv_sems, local_sems):
    x, y, c = _place()
    me = (x, y, c)
    targets = [(x, y, 1 - c)] + [(*chip, c) for chip in _other_chips(x, y)]
    local, out, inc = [], [], []
    for a in range(len(ins)):
        local.append(pltpu.make_async_copy(ins[a], outs[a].at[_dev(me)], local_sems.at[a]))
        for k, to in enumerate(targets):
            out.append(_remote(ins[a], outs[a].at[_dev(me)], send_sems, recv_sems, 4 * a + k, to))
            inc.append(_remote(ins[a], outs[a].at[_dev(to)], send_sems, recv_sems, 4 * a + k, to))
    return local, out, inc


def _ag_second(bufs, send_sems, recv_sems):
    x, y, c = _place()
    out, inc = [], []
    for a in range(len(bufs)):
        for j, chip in enumerate(_other_chips(x, y)):
            mine, theirs = bufs[a].at[_dev((*chip, c))], bufs[a].at[_dev((*chip, 1 - c))]
            out.append(_remote(mine, mine, send_sems, recv_sems, 3 * a + j, (x, y, 1 - c)))
            inc.append(_remote(theirs, theirs, send_sems, recv_sems, 3 * a + j, (x, y, 1 - c)))
    return [], out, inc


def _rs_first(ins, outs, send_sems, recv_sems):
    x, y, c = _place()
    out = [_remote(ins[a].at[j, 1 - c], outs[a].at[j], send_sems, recv_sems, N_CHIP * a + j, (x, y, 1 - c))
           for a in range(len(ins)) for j in range(N_CHIP)]
    return [], out, out


def _rs_second(ins, outs, send_sems, recv_sems, local_sems):
    x, y, c = _place()
    my_chip = 2 * x + y
    local, out, inc = [], [], []
    for a in range(len(ins)):
        local.append(pltpu.make_async_copy(ins[a].at[my_chip], outs[a].at[my_chip], local_sems.at[a]))
        for k, (ox, oy) in enumerate(_other_chips(x, y)):
            out.append(_remote(ins[a].at[2 * ox + oy], outs[a].at[my_chip], send_sems, recv_sems, 3 * a + k, (ox, oy, c)))
            inc.append(_remote(ins[a].at[2 * ox + oy], outs[a].at[2 * ox + oy], send_sems, recv_sems, 3 * a + k,
                               (ox, oy, c)))
    return local, out, inc


def _comm_start(exchange):
    local, out, _ = exchange
    for cp in local + out:
        cp.start()


def _comm_wait(exchange):
    local, out, inc = exchange
    for cp in inc:
        cp.wait_recv()
    for cp in out:
        cp.wait_send()
    for cp in local:
        cp.wait()


def _dma_sems(*counts):
    return [pltpu.SemaphoreType.DMA((n,)) for n in counts]


def _pair_sum(grads, recvs):
    n = len(grads)

    def body(c_ref, *refs):
        for a in range(n):
            refs[2 * n + a][...] = (refs[a][...].astype(F32) + refs[n + a][...].astype(F32)).astype(BF16)

    def g_spec(g):
        return pl.BlockSpec((None, None) + g.shape[2:], lambda j, c_ref: (j, c_ref[0], 0, 0))

    def r_spec(r):
        return pl.BlockSpec((None,) + r.shape[1:], lambda j, c_ref: (j, 0, 0))

    return pl.pallas_call(
        body, name="pair_sum",
        grid_spec=pltpu.PrefetchScalarGridSpec(
            num_scalar_prefetch=1, grid=(N_CHIP,),
            in_specs=[g_spec(g) for g in grads] + [r_spec(r) for r in recvs],
            out_specs=[r_spec(r) for r in recvs]),
        out_shape=[jax.ShapeDtypeStruct(r.shape, BF16) for r in recvs],
        compiler_params=_params(("parallel",)),
    )(lax.axis_index("c").reshape(1).astype(jnp.int32), *grads, *recvs)


SMALL_ROWS = 544


def _all_reduce_small(buf, parts=()):
    n = len(parts)

    def body(in_ref, *rest):
        p_in, out_ref, p_out = rest[:n], rest[n], rest[n + 1:2 * n + 1]
        recv, acc, send1, recv1, send2, recv2 = rest[2 * n + 1:2 * n + 7]
        scatter_sems = rest[2 * n + 7:]
        x, y, c = _place()
        me = 4 * x + 2 * y + c
        peers = [(x ^ (r >> 2), y ^ ((r >> 1) & 1), c ^ (r & 1)) for r in range(1, N_DEV)]

        def idx(p):
            return 4 * p[0] + 2 * p[1] + p[2]

        if n:
            _comm_start(_rs_second(p_in, p_out, *scatter_sems))
        first = [pltpu.make_async_remote_copy(
            src_ref=in_ref.at[idx(p)], dst_ref=recv.at[me], send_sem=send1.at[r], recv_sem=recv1.at[r],
            device_id=p, device_id_type=MESH) for r, p in enumerate(peers)]
        for cp in first:
            cp.start()
        recv[me] = in_ref[me]
        for r, p in enumerate(peers):
            pltpu.make_async_remote_copy(
                src_ref=in_ref.at[idx(p)], dst_ref=recv.at[idx(p)], send_sem=send1.at[r], recv_sem=recv1.at[r],
                device_id=p, device_id_type=MESH).wait_recv()
        total = recv[0]
        for k in range(1, N_DEV):
            total = total + recv[k]
        acc[...] = total
        out_ref[me] = total
        second = [pltpu.make_async_remote_copy(
            src_ref=acc, dst_ref=out_ref.at[me], send_sem=send2.at[r], recv_sem=recv2.at[r],
            device_id=p, device_id_type=MESH) for r, p in enumerate(peers)]
        for cp in second:
            cp.start()
        for r, p in enumerate(peers):
            pltpu.make_async_remote_copy(
                src_ref=acc, dst_ref=out_ref.at[idx(p)], send_sem=send2.at[r], recv_sem=recv2.at[r],
                device_id=p, device_id_type=MESH).wait_recv()
        for cp in first + second:
            cp.wait_send()
        if n:
            _comm_wait(_rs_second(p_in, p_out, *scatter_sems))

    vm = pl.BlockSpec(memory_space=pltpu.VMEM)
    res = pl.pallas_call(
        body, name="small_grads_all_reduce",
        in_specs=[vm] + [ANY] * n, out_specs=[vm] + [ANY] * n,
        out_shape=[jax.ShapeDtypeStruct(buf.shape, F32)] + [jax.ShapeDtypeStruct(p.shape, p.dtype) for p in parts],
        scratch_shapes=[pltpu.VMEM(buf.shape, F32), pltpu.VMEM(buf.shape[1:], F32)] + _dma_sems(7, 7, 7, 7)
                       + (_dma_sems(3 * n, 3 * n, n) if n else []),
        compiler_params=pltpu.CompilerParams(has_side_effects=True, vmem_limit_bytes=VMEM_LIMIT),
    )(buf, *parts)
    return res[0], list(res[1:])


def _dilate(a, d):
    if d == 1:
        return a
    S, C = a.shape
    return a.reshape(S // d, d, C).transpose(1, 0, 2).reshape(S, C)


def _undilate(a, d):
    if d == 1:
        return a
    S, C = a.shape
    return a.reshape(d, S // d, C).transpose(1, 0, 2).reshape(S, C)


def _cols(a, cb, n=1):
    return a[:, cb * WIDTH:(cb + n) * WIDTH]


def _to_blocks(g, kind):
    if kind == "rows":
        C = g.shape[1]
        return g.reshape(N_CHIP, 2, -1, C)
    return g.reshape(N_CHIP, 2, 4 * WIDTH, -1)


SMALL = ("norm_g", "gm_ln_g", "gm_ln_b", "gm_ws", "gm_bs", "pool_w", "pool_scale", "mem_norm_g", "final_norm_g")


def _pack_small(tree):
    flat = jnp.concatenate([tree[k].reshape(-1, 128) for k in SMALL], axis=0)
    return jnp.pad(flat, ((0, N_DEV * SMALL_ROWS - flat.shape[0]), (0, 0)))


def _unpack_small(flat, like):
    out, at = {}, 0
    for k in SMALL:
        rows = like[k].size // 128
        out[k] = flat[at:at + rows].reshape(like[k].shape)
        at += rows
    return out


def _make_layer(wt, wkv, wb, wout, norm_g, mem_norm_g, ln_g, ln_b, gm_ws, gm_bs, pool_w, pool_scale):
    tril = jnp.tril(jnp.ones((CHUNK, CHUNK), bool))
    wsm = jnp.where(tril, gm_ws, 0.0).astype(BF16)
    pw = pool_w.astype(BF16)
    bands, bands_t = _band_constants()
    return dict(wt=wt, wkv=wkv, wb=wb, wout=wout, g=norm_g[None], mg=mem_norm_g[None],
                ln_g=ln_g[None],
                ln_b=ln_b[None], wsm=wsm, wsm_t=wsm.transpose(0, 2, 1), pw=pw, pw_t=pw.transpose(0, 2, 1),
                ps=pool_scale[None], bias=jnp.repeat(gm_bs.T, HEAD, axis=1), bands=bands, bands_t=bands_t)


def _layer_fwd(xl, mem0, L, next_shards=()):
    S = xl.shape[0]
    proj, gates, h, half_win = _in_proj(xl, L["g"], L["wt"], next_shards[:1])
    kv, mem_n = _mem_kv(mem0, L["mg"], L["wkv"])
    y4, cdf, win, half_small = _abm_fwd(proj, L["ln_g"], L["ln_b"], L["wsm"], L["bias"], L["pw"], L["ps"], kv,
                                        L["bands"], half_win, next_shards[2:])
    o_g, l_g = [], []
    for gi, d in enumerate(DILATIONS):
        if d == 1:
            o, lse = _attn_fwd(proj, CB_Q0, proj, CB_K, proj, CB_CV, S // CHUNK)
        else:
            o, lse = _attn_fwd_dilated(proj, CB_Q0 + gi, CB_K, CB_CV, d)
        o_g.append(o)
        l_g.append(lse)
    (xn, y4, oc, lse, z), small = _merge_fwd(xl, y4, o_g, l_g, proj, gates, L["wb"], L["wout"], next_shards[1:2],
                                             half_small)
    saved = dict(x=xl, proj=proj, gates=gates, h=h, kv=kv, mem_n=mem_n, y4=y4, cdf=cdf, oc=oc, lse=lse, z=z)
    return xn, saved, win + small


def _place_cols(dst, piece, cb):
    return lax.dynamic_update_slice(dst, piece, (0, cb * WIDTH))


def _layer_bwd(dx, mem0, L, sv, later=(), last=False):
    S = dx.shape[0]
    proj = sv["proj"]
    (dy3, doc, delta, dcg, dgm, dwb, dwout), from_sibling = _merge_bwd(
        dx, sv["y4"], sv["oc"], sv["z"], proj, sv["gates"], L["wb"], L["wout"], later)
    pair = _pair_sum(later, from_sibling) if later else ()
    dpb, dm, dlng, dlnb, dws, dbias, dpw, dps, dkv = _abm_bwd(
        proj, sv["cdf"], dy3, L["ln_g"], L["ln_b"], L["wsm"], L["wsm_t"], L["bias"], L["pw"], L["pw_t"], L["ps"], sv["kv"],
        L["bands"], L["bands_t"])
    dk, dv = None, None
    for gi, d in enumerate(DILATIONS):
        if d == 1:
            dpb, dk, dv = _attn_bwd(proj, CB_Q0, proj, CB_K, proj, CB_CV, doc, sv["lse"], delta, S // CHUNK,
                                    dpb, CB_Q0)
        else:
            dpb, dk, dv = _attn_bwd_dilated(proj, CB_Q0 + gi, CB_K, CB_CV, doc, sv["lse"], delta, d, dk, dv,
                                            dpb, CB_Q0 + gi)
    dpb = _place_cols(dpb, dk, CB_K)
    dpb = _place_cols(dpb, dv, CB_CV)
    dpb = _place_cols(dpb, dcg, CB_CGATE)
    dpb = _place_cols(dpb, dm, CB_MQ)
    dwkv, dmg = _mem_bwd(mem0, L["mg"], sv["mem_n"], L["wkv"], dkv)
    ready = _blocked_rest(dict(w_mem_kv=dwkv, w_branch=dwb, w_out=dwout))
    dwin_t, parts_rest, own = _dw_in(sv["h"], dpb, dgm, pair[1:], ready if last else ())
    big = [_to_blocks(dwin_t, "rows")] + ready
    dxi, dng, parts, from_sibling = _dh_bwd(dpb, dgm, L["wt"], sv["x"], L["g"], dx, pair[:1],
                                            big[:1] if last else ())
    parts = parts + parts_rest
    small = dict(norm_g=dng[0], gm_ln_g=dlng[0], gm_ln_b=dlnb[0], gm_ws=dws,
                 gm_bs=_bias_reduce(dbias)[:, :N_HEAD].T, pool_w=dpw, pool_scale=dps[0], mem_norm_g=dmg[0])
    return dxi, big, small, parts, own, from_sibling


BIG = ("w_in", "w_mem_kv", "w_branch", "w_out")


def _blocked_rest(big):
    return [_to_blocks(big["w_mem_kv"], "rows"), _to_blocks(big["w_branch"], "branch"), _to_blocks(big["w_out"], "rows")]


def _full_weights(gathered):
    win_t, wkv, wb, wout = gathered
    return (win_t.reshape(D_IN, D_MODEL), wkv.reshape(D_MODEL, 2 * WIDTH),
            wb.reshape(N_DEV, 4, WIDTH, -1).transpose(1, 2, 0, 3).reshape(4, WIDTH, D_MODEL),
            wout.reshape(D_MODEL, D_MODEL))


def kernel(x, mem, norm_g, w_in, gm_ln_g, gm_ln_b, gm_ws, gm_bs, pool_w, pool_scale, mem_norm_g, w_mem_kv, w_branch, w_out, final_norm_g, loss_target, m_norm_g, m_w_in, m_gm_ln_g, m_gm_ln_b, m_gm_ws, m_gm_bs, m_pool_w, m_pool_scale, m_mem_norm_g, m_w_mem_kv, m_w_branch, m_w_out, m_final_norm_g, v_norm_g, v_w_in, v_gm_ln_g, v_gm_ln_b, v_gm_ws, v_gm_bs, v_pool_w, v_pool_scale, v_mem_norm_g, v_w_mem_kv, v_w_branch, v_w_out, v_final_norm_g):
    x0 = x[0]
    mem0 = mem[0]
    tgt = loss_target[0]
    S = x0.shape[0]

    shards = [[w_in[l].T.astype(BF16), w_mem_kv[l].astype(BF16), w_branch[l].astype(BF16).reshape(4 * WIDTH, -1),
               w_out[l].astype(BF16)] for l in range(DEPTH)]
    gathered = _all_gather(shards[0])
    layers, saved = [], []
    xl = x0
    for l in range(DEPTH):
        layers.append(_make_layer(*_full_weights(gathered), norm_g[l], mem_norm_g[l], gm_ln_g[l], gm_ln_b[l],
                                  gm_ws[l], gm_bs[l], pool_w[l], pool_scale[l]))
        xl, sv, gathered = _layer_fwd(xl, mem0, layers[l], shards[l + 1] if l + 1 < DEPTH else ())
        saved.append(sv)

    loss_part, dx, d_final = _loss_head(xl, final_norm_g[None], tgt)
    loss = lax.psum(loss_part[0, 0], ("x", "y", "c"))

    small = {k: [None] * DEPTH for k in SMALL if k != "final_norm_g"}
    parts = [None] * DEPTH
    later = ()
    for l in reversed(range(DEPTH)):
        dx, gb, gs, done, own, from_sibling = _layer_bwd(dx, mem0, layers[l], saved[l], later, last=(l == 0))
        if later:
            parts[l + 1] = done
        later = gb
        for k in gs:
            small[k][l] = gs[k]
    grad_x = dx[None]
    small_tree = {k: jnp.stack(small[k]) for k in small}
    small_tree["final_norm_g"] = d_final[0]
    reduced, win_parts = _all_reduce_small(_pack_small(small_tree).reshape(N_DEV, SMALL_ROWS, 128),
                                           _pair_sum(later[:1], from_sibling))
    parts[0] = win_parts + own

    weights = dict(norm_g=norm_g, w_in=w_in, gm_ln_g=gm_ln_g, gm_ln_b=gm_ln_b, gm_ws=gm_ws, gm_bs=gm_bs,
                   pool_w=pool_w, pool_scale=pool_scale, mem_norm_g=mem_norm_g, w_mem_kv=w_mem_kv,
                   w_branch=w_branch, w_out=w_out, final_norm_g=final_norm_g)
    m_in = dict(norm_g=m_norm_g, w_in=m_w_in, gm_ln_g=m_gm_ln_g, gm_ln_b=m_gm_ln_b, gm_ws=m_gm_ws, gm_bs=m_gm_bs,
                pool_w=m_pool_w, pool_scale=m_pool_scale, mem_norm_g=m_mem_norm_g, w_mem_kv=m_w_mem_kv,
                w_branch=m_w_branch, w_out=m_w_out, final_norm_g=m_final_norm_g)
    v_in = dict(norm_g=v_norm_g, w_in=v_w_in, gm_ln_g=v_gm_ln_g, gm_ln_b=v_gm_ln_b, gm_ws=v_gm_ws, gm_bs=v_gm_bs,
                pool_w=v_pool_w, pool_scale=v_pool_scale, mem_norm_g=v_mem_norm_g, w_mem_kv=v_w_mem_kv,
                w_branch=v_w_branch, w_out=v_w_out, final_norm_g=v_final_norm_g)
    res = {}
    def view(k, arr):
        return arr.transpose(0, 2, 1) if k == "w_in" else arr

    for a, k in enumerate(BIG):
        shape = view(k, weights[k]).shape
        by_layer = [parts[l][a] for l in range(DEPTH)]
        lrc = (DEPTH,) + by_layer[0].shape[1:]
        outs = _adamw_layers(by_layer, view(k, weights[k]).reshape(lrc), view(k, m_in[k]).reshape(lrc),
                             view(k, v_in[k]).reshape(lrc), "adamw_" + k)
        res[k] = [view(k, o.reshape(shape)) for o in outs]
    outs = _adamw(reduced.reshape(1, N_DEV * SMALL_ROWS, 128), _pack_small(weights), _pack_small(m_in),
                  _pack_small(v_in), "adamw_small")
    unpacked = [_unpack_small(o, weights) for o in outs]
    for k in SMALL:
        res[k] = [u[k] for u in unpacked]

    order = ("norm_g", "w_in", "gm_ln_g", "gm_ln_b", "gm_ws", "gm_bs", "pool_w", "pool_scale", "mem_norm_g",
             "w_mem_kv", "w_branch", "w_out", "final_norm_g")
    return (loss, grad_x, *[res[k][0] for k in order], *[res[k][1] for k in order],
            *[res[k][2] for k in order], *[res[k][3] for k in order])
```

```python
import functools
import math

import numpy as np

import jax
import jax.numpy as jnp
from jax import lax
from jax.experimental import pallas as pl
from jax.experimental.pallas import tpu as pltpu

F32 = jnp.float32
BF16 = jnp.bfloat16

D_MODEL = 1024
DEPTH = 4
WIDTH = 512
D_IN = 10752
HEAD = 128
N_HEAD = 4
CHUNK = 128
MEM_LEN = 256
POOL_WINDOWS = (2, 4, 8, 16)
DILATIONS = (1, 4, 16)
EPS = 1e-6
NEG = -1e30
ATT_SCALE = HEAD ** -0.5
N_DEV = 8
N_CHIP = 4

D_BRANCHES = 6656
D_GATES = D_IN - D_BRANCHES
CB_U, CB_V, CB_AGATE, CB_PIN, CB_PGATE = 0, 1, 2, 3, 4
CB_Q0, CB_K, CB_CV, CB_CGATE, CB_MQ, CB_MGATE = 5, 8, 9, 10, 11, 12

ADAM_LR = 0.001
ADAM_B1 = 0.9
ADAM_B2 = 0.999
ADAM_EPS = 1e-08
ADAM_WD = 0.01
ADAM_STEP = 10

VMEM_LIMIT = 56 * 1024 * 1024
MESH = pl.DeviceIdType.MESH
ANY = pl.BlockSpec(memory_space=pl.ANY)

NT = (((1,), (1,)), ((), ()))
TN = (((0,), (0,)), ((), ()))


def _dot(a, b):
    return jnp.dot(a, b, preferred_element_type=F32)


def _dot_nt(a, b):
    return lax.dot_general(a, b, NT, preferred_element_type=F32)


def _dot_tn(a, b):
    return lax.dot_general(a, b, TN, preferred_element_type=F32)


def _sigmoid(x):
    return 0.5 * jnp.tanh(0.5 * x) + 0.5


def _silu(x):
    return x * _sigmoid(x)


def _silu_and_grad(x):
    s = _sigmoid(x)
    return x * s, s * (1.0 + x * (1.0 - s))


def _normal_cdf(x):
    return 0.5 * (1.0 + lax.erf(x * (2.0 ** -0.5)))


def _gelu_and_grad(x, cdf):
    return x * cdf, cdf + x * jnp.exp(-0.5 * x * x) * (1.0 / math.sqrt(2.0 * math.pi))


def _col(blk, h):
    lane = lax.broadcasted_iota(jnp.int32, blk.shape, 1)
    return jnp.sum(jnp.where(lane == h, blk, 0.0), axis=1, keepdims=True)


def _put_cols(cols):
    rows = cols[0].shape[0]
    lane = lax.broadcasted_iota(jnp.int32, (rows, 128), 1)
    out = jnp.zeros((rows, 128), F32)
    for h, cv in enumerate(cols):
        out = jnp.where(lane == h, cv, out)
    return out


def _params(sem, vmem=VMEM_LIMIT):
    return pltpu.CompilerParams(dimension_semantics=sem, vmem_limit_bytes=vmem)


def _full(shape):
    nd = len(shape)
    return pl.BlockSpec(shape, lambda *_: (0,) * nd)


def _resident(shape):
    nd = len(shape)
    return pl.BlockSpec(shape, lambda *_: (0,) * nd, pipeline_mode=pl.Buffered(1))


def _rows(tm, width, cb=0):
    return pl.BlockSpec((tm, width), lambda i: (i, cb))


GATE_TILE = 512


def _in_proj(x, g, wt, shards=()):
    S = x.shape[0]
    tm, tnb, tng = 1024, D_BRANCHES // 4, D_GATES // 4
    njb, njg = 4, 4
    n = len(shards)
    ni, nj = S // tm, njb + njg

    def body(x_ref, g_ref, wbr_ref, wg0_ref, wg1_ref, *rest):
        ins, (proj_ref, gates_ref, h_ref), outs = rest[:n], rest[n:n + 3], rest[n + 3:2 * n + 3]
        hs, sems = rest[2 * n + 3], rest[2 * n + 4:]
        i, j = pl.program_id(0), pl.program_id(1)

        if n:
            @pl.when(jnp.logical_and(i == 0, j == 0))
            def _():
                _comm_start(_ag_first(ins, outs, *sems))

        @pl.when(j == 0)
        def _():
            xf = x_ref[...]
            r = lax.rsqrt(jnp.mean(xf * xf, axis=-1, keepdims=True) + EPS)
            h = (xf * r * g_ref[...]).astype(BF16)
            hs[...] = h
            h_ref[...] = h

        @pl.when(j < njb)
        def _():
            proj_ref[...] = _dot_nt(hs[...], wbr_ref[...]).astype(BF16)

        @pl.when(j >= njb)
        def _():
            gates_ref[:, :tng // 2] = _dot_nt(hs[...], wg0_ref[...]).astype(BF16)
            gates_ref[:, tng // 2:] = _dot_nt(hs[...], wg1_ref[...]).astype(BF16)

        if n:
            @pl.when(jnp.logical_and(i == ni - 1, j == nj - 1))
            def _():
                _comm_wait(_ag_first(ins, outs, *sems))

    def first(j):
        return jnp.minimum(j, njb - 1)

    def second(j):
        return jnp.maximum(j - njb, 0)

    def gate_rows(half):
        return pl.BlockSpec((tng // 2, D_MODEL), lambda i, j: (D_BRANCHES // (tng // 2) + 2 * second(j) + half, 0))

    res = pl.pallas_call(
        body, name="in_proj_gather" if n else "in_proj",
        grid=(ni, nj),
        in_specs=[pl.BlockSpec((tm, D_MODEL), lambda i, j: (i, 0)),
                  pl.BlockSpec((1, D_MODEL), lambda i, j: (0, 0)),
                  pl.BlockSpec((tnb, D_MODEL), lambda i, j: (first(j), 0)), gate_rows(0), gate_rows(1)]
                 + [ANY] * n,
        out_specs=[pl.BlockSpec((tm, tnb), lambda i, j: (i, first(j))),
                   pl.BlockSpec((tm, tng), lambda i, j: (i, second(j))),
                   pl.BlockSpec((tm, D_MODEL), lambda i, j: (i, 0))] + [ANY] * n,
        out_shape=[jax.ShapeDtypeStruct((S, D_BRANCHES), BF16), jax.ShapeDtypeStruct((S, D_GATES), BF16),
                   jax.ShapeDtypeStruct((S, D_MODEL), BF16)]
                  + [jax.ShapeDtypeStruct((N_DEV,) + s.shape, s.dtype) for s in shards],
        scratch_shapes=[pltpu.VMEM((tm, D_MODEL), BF16)] + (_dma_sems(4 * n, 4 * n, n) if n else []),
        compiler_params=_params(("arbitrary", "arbitrary")),
    )(x, g, wt, wt, wt, *shards)
    return res[0], res[1], res[2], list(res[3:])


def _mem_kv(mem, g, w):
    M = mem.shape[0]

    def body(m_ref, g_ref, w_ref, kv_ref, mn_ref):
        xf = m_ref[...]
        r = lax.rsqrt(jnp.mean(xf * xf, axis=-1, keepdims=True) + EPS)
        mn = (xf * r * g_ref[...]).astype(BF16)
        mn_ref[...] = mn
        kv_ref[...] = _dot(mn, w_ref[...]).astype(BF16)

    return pl.pallas_call(
        body, name="mem_kv",
        out_shape=[jax.ShapeDtypeStruct((M, 2 * WIDTH), BF16), jax.ShapeDtypeStruct((M, D_MODEL), BF16)],
        compiler_params=pltpu.CompilerParams(vmem_limit_bytes=VMEM_LIMIT),
    )(mem, g, w)


def _band_constants():
    t = np.arange(CHUNK)[:, None]
    s = np.arange(CHUNK)[None, :]
    bands = np.stack([np.stack([(t - s >= 0) & (t - s < win), s > t + CHUNK - win]) for win in POOL_WINDOWS])
    bands = bands.astype(np.float32)
    return jnp.asarray(bands, BF16), jnp.asarray(bands.transpose(0, 1, 3, 2), BF16)


def _inv_count(first_row, win):
    t = first_row + lax.broadcasted_iota(jnp.int32, (CHUNK, 1), 0)
    return 1.0 / jnp.minimum(t + 1, win).astype(F32)


def _layer_norm_fwd(v):
    mu = jnp.mean(v, axis=-1, keepdims=True)
    vc = v - mu
    var = jnp.mean(vc * vc, axis=-1, keepdims=True)
    rstd = lax.rsqrt(var + EPS)
    return vc * rstd, rstd


def _mem_softmax(q, kmem):
    s = _dot_nt(q, kmem) * ATT_SCALE
    m = jnp.max(s, axis=-1, keepdims=True)
    e = jnp.exp(s - m)
    return e * (1.0 / jnp.sum(e, axis=-1, keepdims=True))


def _abm_fwd(proj, ln_g, ln_b, wsm, bias_full, pool_w, pool_scale, kv, bands, gathered=(), shards=()):
    S = proj.shape[0]
    tm = 512
    nchunk = tm // CHUNK
    n, m = len(gathered), len(shards)
    nsteps = S // tm

    def body(u_ref, v_ref, ag_ref, p_ref, ph_ref, pg_ref, mq_ref, mg_ref, lng_ref, lnb_ref, wsm_ref, bias_ref,
             pw_ref, ps_ref, kv_ref, band_ref, *rest):
        s_in = rest[n:n + m]
        y_ref, cdf_ref = rest[n + m], rest[n + m + 1]
        bufs, s_out = rest[n + m + 2:2 * n + m + 2], rest[2 * n + m + 2:2 * (n + m) + 2]
        mix, sems = rest[2 * (n + m) + 2], rest[2 * (n + m) + 3:]
        second_sems, first_sems = (sems[:2] if n else ()), sems[2 if n else 0:]
        i = pl.program_id(0)

        if n or m:
            @pl.when(i == 0)
            def _():
                if n:
                    _comm_start(_ag_second(bufs, *second_sems))
                if m:
                    _comm_start(_ag_first(s_in, s_out, *first_sems))

        au, av = u_ref[...].astype(F32), v_ref[...].astype(F32)
        cdf_u, cdf_v = _normal_cdf(au), _normal_cdf(av)
        cdf_ref[0] = cdf_u.astype(BF16)
        cdf_ref[1] = cdf_v.astype(BF16)
        u, v = au * cdf_u, av * cdf_v
        vhat, _ = _layer_norm_fwd(v)
        vln = (vhat * lng_ref[...] + lnb_ref[...]).astype(BF16)
        for c in range(nchunk):
            for h in range(N_HEAD):
                rs, cs = slice(c * CHUNK, (c + 1) * CHUNK), slice(h * HEAD, (h + 1) * HEAD)
                mix[rs, cs] = _dot(wsm_ref[h], vln[rs, cs]) + bias_ref[:, cs]
        y_ref[0] = (u * mix[...] * _silu(ag_ref[...].astype(F32))).astype(BF16)
        halo_ok = (i > 0).astype(F32)
        for c in range(nchunk):
            rs = slice(c * CHUNK, (c + 1) * CHUNK)
            for g, win in enumerate(POOL_WINDOWS):
                cs = slice(g * HEAD, (g + 1) * HEAD)
                cur = p_ref[rs, cs]
                if c == 0:
                    prev = (ph_ref[:, cs].astype(F32) * halo_ok).astype(BF16)
                else:
                    prev = p_ref[(c - 1) * CHUNK:c * CHUNK, cs]
                sums = _dot(band_ref[g, 0], cur) + _dot(band_ref[g, 1], prev)
                dm = sums * _inv_count(i * tm + c * CHUNK, win) - cur.astype(F32)
                mix[rs, cs] = _dot(dm.astype(BF16), pw_ref[g])
        y_ref[1] = (mix[...] * ps_ref[...] * _silu(pg_ref[...].astype(F32))).astype(BF16)
        for h in range(N_HEAD):
            cs = slice(h * HEAD, (h + 1) * HEAD)
            p = _mem_softmax(mq_ref[:, cs], kv_ref[:, cs])
            mix[:, cs] = _dot(p.astype(BF16), kv_ref[:, WIDTH + h * HEAD:WIDTH + (h + 1) * HEAD])
        y_ref[2] = (mix[...] * _silu(mg_ref[...].astype(F32))).astype(BF16)

        if n or m:
            @pl.when(i == nsteps - 1)
            def _():
                if n:
                    _comm_wait(_ag_second(bufs, *second_sems))
                if m:
                    _comm_wait(_ag_first(s_in, s_out, *first_sems))

    blk = tm // CHUNK
    res = pl.pallas_call(
        body, name="abm_fwd_gather" if n or m else "abm_fwd",
        grid=(nsteps,),
        in_specs=[_rows(tm, WIDTH, CB_U), _rows(tm, WIDTH, CB_V), _rows(tm, WIDTH, CB_AGATE),
                  _rows(tm, WIDTH, CB_PIN),
                  pl.BlockSpec((CHUNK, WIDTH), lambda i: (jnp.maximum(i * blk - 1, 0), CB_PIN)),
                  _rows(tm, WIDTH, CB_PGATE), _rows(tm, WIDTH, CB_MQ), _rows(tm, WIDTH, CB_MGATE),
                  _full((1, WIDTH)), _full((1, WIDTH)), _full((N_HEAD, CHUNK, CHUNK)), _full((CHUNK, WIDTH)),
                  _full((4, HEAD, HEAD)), _full((1, WIDTH)), _full((MEM_LEN, 2 * WIDTH)),
                  _full((4, 2, CHUNK, CHUNK))] + [ANY] * (n + m),
        out_specs=[pl.BlockSpec((3, tm, WIDTH), lambda i: (0, i, 0)), pl.BlockSpec((2, tm, WIDTH), lambda i: (0, i, 0))]
                  + [ANY] * (n + m),
        out_shape=[jax.ShapeDtypeStruct((4, S, WIDTH), BF16),
                   jax.ShapeDtypeStruct((2, S, WIDTH), BF16)]
                  + [jax.ShapeDtypeStruct(b.shape, b.dtype) for b in gathered]
                  + [jax.ShapeDtypeStruct((N_DEV,) + s.shape, s.dtype) for s in shards],
        input_output_aliases={16 + a: 2 + a for a in range(n)},
        scratch_shapes=[pltpu.VMEM((tm, WIDTH), F32)] + (_dma_sems(3 * n, 3 * n) if n else [])
                       + (_dma_sems(4 * m, 4 * m, m) if m else []),
        compiler_params=_params(("arbitrary",)),
    )(proj, proj, proj, proj, proj, proj, proj, proj, ln_g, ln_b, wsm, bias_full, pool_w, pool_scale, kv, bands,
      *gathered, *shards)
    return res[0], res[1], list(res[2:2 + n]), list(res[2 + n:])


ATT_TILE = 512


def _attn_fwd(q, qcb, k, kcb, v, vcb, bps):
    S = q.shape[0]
    tm = ATT_TILE
    nb = tm // CHUNK

    nblocks = nb * N_HEAD

    def body(q_ref, k_ref, v_ref, kh_ref, vh_ref, o_ref, l_ref, sc_s, sp_s, pc_s, pp_s):
        i = pl.program_id(0)

        def prev_kv(n, cs):
            if n == 0:
                return kh_ref[:, cs], vh_ref[:, cs]
            ps = slice((n - 1) * CHUNK, n * CHUNK)
            return k_ref[ps, cs], v_ref[ps, cs]

        pens = []
        for n in range(nb):
            rs = slice(n * CHUNK, (n + 1) * CHUNK)
            pens.append(jnp.full((N_HEAD * CHUNK, 1), jnp.where((i * nb + n) % bps != 0, 0.0, NEG), F32))
            for h in range(N_HEAD):
                cs = slice(h * HEAD, (h + 1) * HEAD)
                bs = slice((n * N_HEAD + h) * CHUNK, (n * N_HEAD + h + 1) * CHUNK)
                qh = q_ref[rs, cs]
                sc_s[bs, :] = _dot_nt(qh, k_ref[rs, cs])
                sp_s[bs, :] = _dot_nt(qh, prev_kv(n, cs)[0])
        row = lax.broadcasted_iota(jnp.int32, (nblocks * CHUNK, CHUNK), 0) & (CHUNK - 1)
        col = lax.broadcasted_iota(jnp.int32, (nblocks * CHUNK, CHUNK), 1)
        sc = jnp.where(col <= row, sc_s[...] * ATT_SCALE, NEG)
        sp = jnp.where(col >= row, sp_s[...] * ATT_SCALE, NEG) + jnp.concatenate(pens, axis=0)
        m = jnp.maximum(jnp.max(sc, axis=-1, keepdims=True), jnp.max(sp, axis=-1, keepdims=True))
        ec = jnp.exp(sc - m)
        ep = jnp.exp(sp - m)
        den = jnp.sum(ec, axis=-1, keepdims=True) + jnp.sum(ep, axis=-1, keepdims=True)
        inv = 1.0 / den
        pc_s[...] = (ec * inv).astype(BF16)
        pp_s[...] = (ep * inv).astype(BF16)
        lse = m + jnp.log(den)
        for n in range(nb):
            rs = slice(n * CHUNK, (n + 1) * CHUNK)
            for h in range(N_HEAD):
                cs = slice(h * HEAD, (h + 1) * HEAD)
                bs = slice((n * N_HEAD + h) * CHUNK, (n * N_HEAD + h + 1) * CHUNK)
                o = _dot(pc_s[bs, :], v_ref[rs, cs]) + _dot(pp_s[bs, :], prev_kv(n, cs)[1])
                o_ref[rs, cs] = o.astype(BF16)
            l_ref[rs, :] = _put_cols([lse[(n * N_HEAD + h) * CHUNK:(n * N_HEAD + h + 1) * CHUNK]
                                      for h in range(N_HEAD)])

    def halo(cb):
        return pl.BlockSpec((CHUNK, WIDTH), lambda i: (jnp.maximum(i * nb - 1, 0), cb))

    return pl.pallas_call(
        body, name=f"attn_fwd_{bps}",
        grid=(S // tm,),
        in_specs=[_rows(tm, WIDTH, qcb), _rows(tm, WIDTH, kcb), _rows(tm, WIDTH, vcb), halo(kcb), halo(vcb)],
        out_specs=[_rows(tm, WIDTH), _rows(tm, 128)],
        out_shape=[jax.ShapeDtypeStruct((S, WIDTH), BF16), jax.ShapeDtypeStruct((S, 128), F32)],
        scratch_shapes=[pltpu.VMEM((nblocks * CHUNK, CHUNK), F32), pltpu.VMEM((nblocks * CHUNK, CHUNK), F32),
                        pltpu.VMEM((nblocks * CHUNK, CHUNK), BF16), pltpu.VMEM((nblocks * CHUNK, CHUNK), BF16)],
        compiler_params=_params(("parallel",)),
    )(q, k, v, k, v)


def _gate_specs(tm):
    return [pl.BlockSpec((tm, D_MODEL), lambda i, b=b: (i, b)) for b in range(4)]


Y_SLOT = (0, 1, 3, 2)


def _w_branch_from_blocks(src_ref, dst):
    cols = D_MODEL // N_DEV
    for k in range(N_DEV):
        for b in range(4):
            dst[b, :, k * cols:(k + 1) * cols] = src_ref[k, b * WIDTH:(b + 1) * WIDTH, :]


def _merge_fwd(x, y4, o_g, l_g, proj, gates, wb, wout, shards=(), half=()):
    S = x.shape[0]
    tm = 512
    n, nh = len(shards), len(half)
    nsteps = S // tm
    forward_at = nsteps // 2

    def body(x_ref, y_ref, o0, o1, o2, l0, l1, l2, cg_ref, *rest):
        gm = rest[:4]
        wb_ref, wo_ref = rest[4:6]
        s_in = rest[6:6 + n]
        rest = rest[6 + n + nh:]
        xn_ref, yc_ref, oc_ref, lse_ref, z_ref = rest[:5]
        s_out, bufs = rest[5:5 + n], rest[5:5 + n + nh]
        ocs, wbs, sems = rest[5 + n + nh], rest[6 + n + nh], rest[7 + n + nh:]
        i = pl.program_id(0)

        @pl.when(i == 0)
        def _():
            if n:
                _comm_start(_ag_first(s_in, s_out, *sems[:3]))
            _w_branch_from_blocks(wb_ref, wbs)

        if n:
            @pl.when(i == forward_at)
            def _():
                incoming = _ag_first(s_in, s_out, *sems[:3])[2]
                for a in range(n):
                    for k in range(1, 4):
                        incoming[4 * a + k].wait_recv()
                _comm_start(_ag_second(bufs, *sems[3:]))

        lcols = []
        for h in range(N_HEAD):
            cs = slice(h * HEAD, (h + 1) * HEAD)
            ls = [_col(l[...], h) for l in (l0, l1, l2)]
            m = jnp.maximum(jnp.maximum(ls[0], ls[1]), ls[2])
            tot = jnp.exp(ls[0] - m) + jnp.exp(ls[1] - m) + jnp.exp(ls[2] - m)
            lse = m + jnp.log(tot)
            ocs[:, cs] = sum(jnp.exp(lg - lse) * o[:, cs].astype(F32) for lg, o in zip(ls, (o0, o1, o2)))
            lcols.append(lse)
        lse_ref[...] = _put_cols(lcols)
        oc = ocs[...]
        oc_ref[...] = oc.astype(BF16)
        yc = (oc * _silu(cg_ref[...].astype(F32))).astype(BF16)
        yc_ref[...] = yc
        ys = (y_ref[0], y_ref[1], yc, y_ref[2])
        z = jnp.zeros((tm, D_MODEL), F32)
        for b in range(4):
            z = z + _sigmoid(gm[b][...].astype(F32)) * _dot(ys[b], wbs[b])
        zb = z.astype(BF16)
        z_ref[...] = zb
        xn_ref[...] = x_ref[...] + _dot(zb, wo_ref[...])

        if n:
            @pl.when(i == nsteps - 1)
            def _():
                local, out, incoming = _ag_first(s_in, s_out, *sems[:3])
                for a in range(n):
                    incoming[4 * a].wait_recv()
                _comm_wait(_ag_second(bufs, *sems[3:]))
                for cp in out:
                    cp.wait_send()
                for cp in local:
                    cp.wait()

    res = pl.pallas_call(
        body, name="merge_fwd_gather" if n else "merge_fwd",
        grid=(nsteps,),
        in_specs=[_rows(tm, D_MODEL), pl.BlockSpec((3, tm, WIDTH), lambda i: (0, i, 0)),
                  _rows(tm, WIDTH), _rows(tm, WIDTH), _rows(tm, WIDTH),
                  _rows(tm, 128), _rows(tm, 128), _rows(tm, 128),
                  _rows(tm, WIDTH, CB_CGATE)] + _gate_specs(tm)
                 + [_resident(wb.shape), _resident((D_MODEL, D_MODEL))] + [ANY] * (n + nh),
        out_specs=[_rows(tm, D_MODEL), pl.BlockSpec((None, tm, WIDTH), lambda i: (Y_SLOT[2], i, 0)),
                   _rows(tm, WIDTH), _rows(tm, 128), _rows(tm, D_MODEL)] + [ANY] * (n + nh),
        out_shape=[jax.ShapeDtypeStruct((S, D_MODEL), F32), jax.ShapeDtypeStruct(y4.shape, BF16),
                   jax.ShapeDtypeStruct((S, WIDTH), BF16), jax.ShapeDtypeStruct((S, 128), F32),
                   jax.ShapeDtypeStruct((S, D_MODEL), BF16)]
                  + [jax.ShapeDtypeStruct((N_DEV,) + s.shape, s.dtype) for s in shards]
                  + [jax.ShapeDtypeStruct(b.shape, b.dtype) for b in half],
        input_output_aliases={1: 1, **{15 + n + a: 5 + n + a for a in range(nh)}},
        scratch_shapes=[pltpu.VMEM((tm, WIDTH), F32), pltpu.VMEM((4, WIDTH, D_MODEL), BF16)]
                       + (_dma_sems(4 * n, 4 * n, n, 3 * (n + nh), 3 * (n + nh)) if n else []),
        compiler_params=_params(("arbitrary",)),
    )(x, y4, *o_g, *l_g, proj, *([gates] * 4), wb, wout, *shards, *half)
    return res[:5], list(res[5:])


def _loss_head(x, g, tgt):
    S = x.shape[0]
    tm = 512

    def body(x_ref, g_ref, t_ref, loss_ref, dx_ref, dg_ref):
        @pl.when(pl.program_id(0) == 0)
        def _():
            loss_ref[...] = jnp.zeros_like(loss_ref)
            dg_ref[...] = jnp.zeros_like(dg_ref)

        xf = x_ref[...]
        r = lax.rsqrt(jnp.mean(xf * xf, axis=-1, keepdims=True) + EPS)
        xhat = xf * r
        gv = g_ref[...]
        err = xhat * gv - t_ref[...]
        e2 = jnp.sum(err * err, axis=-1, keepdims=True)
        loss_ref[...] += (0.5 / D_MODEL) * jnp.sum(e2, axis=0, keepdims=True)
        dy = err * (1.0 / D_MODEL)
        dg_ref[...] += jnp.sum(dy * xhat, axis=0, keepdims=True)
        dxh = dy * gv
        dx_ref[...] = r * (dxh - xhat * jnp.mean(dxh * xhat, axis=-1, keepdims=True))

    return pl.pallas_call(
        body, name="loss_head",
        grid=(S // tm,),
        in_specs=[_rows(tm, D_MODEL), _full((1, D_MODEL)), _rows(tm, D_MODEL)],
        out_specs=[_full((1, 128)), _rows(tm, D_MODEL), _full((1, D_MODEL))],
        out_shape=[jax.ShapeDtypeStruct((1, 128), F32), jax.ShapeDtypeStruct((S, D_MODEL), F32),
                   jax.ShapeDtypeStruct((1, D_MODEL), F32)],
        compiler_params=_params(("arbitrary",)),
    )(x, g, tgt)


def _merge_bwd(dxo, y4, oc, z, proj, gates, wb, wout, grads=()):
    S = dxo.shape[0]
    tm = 256
    n = len(grads)
    nsteps = S // tm

    def body(dx_ref, y_ref, oc_ref, z_ref, cg_ref, *rest):
        gm = rest[:4]
        wb_ref, wo_ref = rest[4:6]
        g_in = rest[6:6 + n]
        dy_ref, doc_ref, delta_ref, dcg_ref, dgm_ref, dwb_ref, dwo_ref = rest[6 + n:13 + n]
        g_out = rest[13 + n:13 + 2 * n]
        acc_b, acc_o, wbs = rest[13 + 2 * n:16 + 2 * n]
        sems = rest[16 + 2 * n:]
        i = pl.program_id(0)

        @pl.when(i == 0)
        def _():
            acc_b[...] = jnp.zeros_like(acc_b)
            acc_o[...] = jnp.zeros_like(acc_o)
            _w_branch_from_blocks(wb_ref, wbs)
            if n:
                _comm_start(_rs_first(g_in, g_out, *sems))

        dxb = dx_ref[...].astype(BF16)
        acc_o[...] += _dot_tn(z_ref[...], dxb)
        dz = _dot_nt(dxb, wo_ref[...])
        for b in range(4):
            gate = _sigmoid(gm[b][...].astype(F32))
            yb = y_ref[Y_SLOT[b]]
            t = _dot(yb, wbs[b])
            dgm_ref[:, b * D_MODEL:(b + 1) * D_MODEL] = (dz * t * gate * (1.0 - gate)).astype(BF16)
            dt = (dz * gate).astype(BF16)
            acc_b[b] += _dot_tn(yb, dt)
            dyb = _dot_nt(dt, wbs[b])
            if b == 2:
                cg = cg_ref[...].astype(F32)
                oc = oc_ref[...].astype(F32)
                scg, dscg = _silu_and_grad(cg)
                doc = dyb * scg
                dcg_ref[...] = (dyb * oc * dscg).astype(BF16)
                doc_ref[...] = doc.astype(BF16)
                prod = doc * oc
                delta_ref[...] = _put_cols([jnp.sum(prod[:, h * HEAD:(h + 1) * HEAD], axis=1, keepdims=True)
                                            for h in range(N_HEAD)])
            else:
                dy_ref[b if b < 2 else 2] = dyb.astype(BF16)

        @pl.when(i == nsteps - 1)
        def _():
            for k in range(N_DEV):
                dwb_ref[k] = acc_b[:, :, k * (D_MODEL // N_DEV):(k + 1) * (D_MODEL // N_DEV)].astype(BF16)
            dwo_ref[...] = acc_o[...].astype(BF16)
            if n:
                _comm_wait(_rs_first(g_in, g_out, *sems))

    res = pl.pallas_call(
        body, name="merge_bwd_scatter" if n else "merge_bwd",
        grid=(nsteps,),
        in_specs=[_rows(tm, D_MODEL), pl.BlockSpec((4, tm, WIDTH), lambda i: (0, i, 0)),
                  _rows(tm, WIDTH), _rows(tm, D_MODEL), _rows(tm, WIDTH, CB_CGATE)] + _gate_specs(tm)
                 + [_resident(wb.shape), _resident((D_MODEL, D_MODEL))] + [ANY] * n,
        out_specs=[pl.BlockSpec((3, tm, WIDTH), lambda i: (0, i, 0)), _rows(tm, WIDTH), _rows(tm, 128),
                   _rows(tm, WIDTH, CB_CGATE), _rows(tm, 4 * D_MODEL), _full((N_DEV, 4, WIDTH, D_MODEL // N_DEV)),
                   _full((D_MODEL, D_MODEL))]
                  + [ANY] * n,
        out_shape=[jax.ShapeDtypeStruct((3, S, WIDTH), BF16), jax.ShapeDtypeStruct((S, WIDTH), BF16),
                   jax.ShapeDtypeStruct((S, 128), F32), jax.ShapeDtypeStruct((S, D_BRANCHES), BF16),
                   jax.ShapeDtypeStruct((S, 4 * D_MODEL), BF16),
                   jax.ShapeDtypeStruct((N_DEV, 4, WIDTH, D_MODEL // N_DEV), BF16),
                   jax.ShapeDtypeStruct((D_MODEL, D_MODEL), BF16)]
                  + [jax.ShapeDtypeStruct(g.shape[:1] + g.shape[2:], g.dtype) for g in grads],
        scratch_shapes=[pltpu.VMEM((4, WIDTH, D_MODEL), F32), pltpu.VMEM((D_MODEL, D_MODEL), F32),
                        pltpu.VMEM((4, WIDTH, D_MODEL), BF16)]
                       + (_dma_sems(N_CHIP * n, N_CHIP * n) if n else []),
        compiler_params=_params(("arbitrary",)),
    )(dxo, y4, oc, z, proj, *([gates] * 4), wb, wout, *grads)
    return res[:7], list(res[7:])


def _attn_bwd(q, qcb, k, kcb, v, vcb, do, lse, delta, bps, dst, dcb):
    S = q.shape[0]
    tm = ATT_TILE
    nb = tm // CHUNK
    nblk = S // CHUNK

    ncur = nb * N_HEAD
    nprev = (nb + 1) * N_HEAD

    def body(q_ref, k_ref, v_ref, do_ref, l_ref, d_ref, kh_ref, vh_ref, qn_ref, don_ref, ln_ref, dn_ref, dst_ref,
             dq_ref, dk_ref, dv_ref, sc_s, sp_s, dpc_s, dpp_s, pc_s, pp_s, dsc_s, dsp_s):
        i = pl.program_id(0)

        def rows_of(n):
            if n < nb:
                rs = slice(n * CHUNK, (n + 1) * CHUNK)
                return rs, q_ref, do_ref, l_ref, d_ref
            return slice(0, CHUNK), qn_ref, don_ref, ln_ref, dn_ref

        def prev_kv(n, cs):
            if n == 0:
                return kh_ref[:, cs], vh_ref[:, cs]
            ps = slice((n - 1) * CHUNK, n * CHUNK)
            return k_ref[ps, cs], v_ref[ps, cs]

        def blk(n, h):
            return slice((n * N_HEAD + h) * CHUNK, (n * N_HEAD + h + 1) * CHUNK)

        pens, lses, deltas = [], [], []
        for n in range(nb + 1):
            rs, qr, dor, lr, dr = rows_of(n)
            gb = i * nb + n
            pen = jnp.where(gb % bps != 0, 0.0, NEG)
            if n == nb:
                pen = pen + jnp.where(gb < nblk, 0.0, NEG)
            pens.append(jnp.full((N_HEAD * CHUNK, 1), pen, F32))
            lblk, dblk = lr[rs, :], dr[rs, :]
            for h in range(N_HEAD):
                cs = slice(h * HEAD, (h + 1) * HEAD)
                qh, doh = qr[rs, cs], dor[rs, cs]
                lses.append(_col(lblk, h))
                deltas.append(_col(dblk, h))
                kp, vp = prev_kv(n, cs)
                sp_s[blk(n, h), :] = _dot_nt(qh, kp)
                dpp_s[blk(n, h), :] = _dot_nt(doh, vp)
                if n < nb:
                    sc_s[blk(n, h), :] = _dot_nt(qh, k_ref[rs, cs])
                    dpc_s[blk(n, h), :] = _dot_nt(doh, v_ref[rs, cs])
        lse = jnp.concatenate(lses, axis=0)
        delta = jnp.concatenate(deltas, axis=0)
        row = lax.broadcasted_iota(jnp.int32, (nprev * CHUNK, CHUNK), 0) & (CHUNK - 1)
        col = lax.broadcasted_iota(jnp.int32, (nprev * CHUNK, CHUNK), 1)
        sp = jnp.where(col >= row, sp_s[...] * ATT_SCALE, NEG) + jnp.concatenate(pens, axis=0)
        pp = jnp.exp(sp - lse)
        pp_s[...] = pp.astype(BF16)
        dsp_s[...] = (pp * (dpp_s[...] - delta)).astype(BF16)
        nc = ncur * CHUNK
        sc = jnp.where(col[:nc] <= row[:nc], sc_s[...] * ATT_SCALE, NEG)
        pc = jnp.exp(sc - lse[:nc])
        pc_s[...] = pc.astype(BF16)
        dsc_s[...] = (pc * (dpc_s[...] - delta[:nc])).astype(BF16)
        for n in range(nb):
            rs, qr, dor, _, _ = rows_of(n)
            rn, qnr, donr, _, _ = rows_of(n + 1)
            for h in range(N_HEAD):
                cs = slice(h * HEAD, (h + 1) * HEAD)
                kp, _ = prev_kv(n, cs)
                dq = _dot(dsc_s[blk(n, h), :], k_ref[rs, cs]) + _dot(dsp_s[blk(n, h), :], kp)
                dq_ref[rs, cs] = (dq * ATT_SCALE).astype(BF16)
                dk = _dot_tn(dsc_s[blk(n, h), :], qr[rs, cs]) + _dot_tn(dsp_s[blk(n + 1, h), :], qnr[rn, cs])
                dk_ref[rs, cs] = (dk * ATT_SCALE).astype(BF16)
                dv = _dot_tn(pc_s[blk(n, h), :], dor[rs, cs]) + _dot_tn(pp_s[blk(n + 1, h), :], donr[rn, cs])
                dv_ref[rs, cs] = dv.astype(BF16)

    def prev_halo(cb):
        return pl.BlockSpec((CHUNK, WIDTH), lambda i: (jnp.maximum(i * nb - 1, 0), cb))

    def next_halo(width, cb=0):
        return pl.BlockSpec((CHUNK, width), lambda i: (jnp.minimum(i * nb + nb, nblk - 1), cb))

    return pl.pallas_call(
        body, name=f"attn_bwd_{bps}",
        grid=(S // tm,),
        in_specs=[_rows(tm, WIDTH, qcb), _rows(tm, WIDTH, kcb), _rows(tm, WIDTH, vcb), _rows(tm, WIDTH),
                  _rows(tm, 128), _rows(tm, 128), prev_halo(kcb), prev_halo(vcb),
                  next_halo(WIDTH, qcb), next_halo(WIDTH), next_halo(128), next_halo(128), ANY],
        out_specs=[_rows(tm, WIDTH, dcb), _rows(tm, WIDTH), _rows(tm, WIDTH)],
        out_shape=[jax.ShapeDtypeStruct(dst.shape, BF16)] + [jax.ShapeDtypeStruct((S, WIDTH), BF16)] * 2,
        input_output_aliases={12: 0},
        scratch_shapes=[pltpu.VMEM((ncur * CHUNK, CHUNK), F32), pltpu.VMEM((nprev * CHUNK, CHUNK), F32),
                        pltpu.VMEM((ncur * CHUNK, CHUNK), F32), pltpu.VMEM((nprev * CHUNK, CHUNK), F32),
                        pltpu.VMEM((ncur * CHUNK, CHUNK), BF16), pltpu.VMEM((nprev * CHUNK, CHUNK), BF16),
                        pltpu.VMEM((ncur * CHUNK, CHUNK), BF16), pltpu.VMEM((nprev * CHUNK, CHUNK), BF16)],
        compiler_params=_params(("parallel",)),
    )(q, k, v, do, lse, delta, k, v, q, do, lse, delta, dst)


def _dilated_split(d):
    hp = min(N_HEAD, 16 // d)
    return hp, N_HEAD // hp, HEAD * hp


def _strided_regroup(d):
    return d < 16


def _by_class(src_ref, dst, d, hp, nat):
    for j in range(hp):
        if _strided_regroup(d):
            nat[j] = src_ref[:, j * HEAD:(j + 1) * HEAD].astype(F32)
            for r in range(d):
                dst[j, r * CHUNK:(r + 1) * CHUNK, :] = nat.at[j][pl.ds(r, CHUNK, stride=d), :].astype(BF16)
        else:
            dst[j] = pltpu.einshape("(tr)l->(rt)l", src_ref[:, j * HEAD:(j + 1) * HEAD], r=d)


def _from_class(src, dst_ref, d, hp, nat, add_ref=None):
    for j in range(hp):
        cs = slice(j * HEAD, (j + 1) * HEAD)
        if _strided_regroup(d):
            for r in range(d):
                nat.at[j][pl.ds(r, CHUNK, stride=d), :] = src[j, r * CHUNK:(r + 1) * CHUNK, :]
            val = nat[j].astype(BF16)
        else:
            val = pltpu.einshape("(rt)l->(tr)l", src[j].astype(BF16), r=d)
        if add_ref is not None:
            val = (val.astype(F32) + add_ref[:, cs].astype(F32)).astype(BF16)
        dst_ref[:, cs] = val


def _attn_fwd_dilated(proj, qcb, kcb, vcb, d):
    S = proj.shape[0]
    T = CHUNK * d
    hp, nh, cw = _dilated_split(d)
    nblocks = d * hp

    def body(q_ref, k_ref, v_ref, o_ref, l_ref, qf, kst, vst, of, lf, nat, sc_s, sp_s, pc_s, pp_s):
        i, hh = pl.program_id(0), pl.program_id(1)
        kf, vf = kst.at[i % 2, hh], vst.at[i % 2, hh]
        kpf, vpf = kst.at[1 - i % 2, hh], vst.at[1 - i % 2, hh]

        @pl.when(i == 0)
        def _():
            kpf[...] = jnp.zeros_like(kpf)
            vpf[...] = jnp.zeros_like(vpf)

        _by_class(q_ref, qf, d, hp, nat)
        _by_class(k_ref, kf, d, hp, nat)
        _by_class(v_ref, vf, d, hp, nat)

        def blk(ref, r, j):
            return ref[j, r * CHUNK:(r + 1) * CHUNK, :]

        def bs(r, j):
            return slice((r * hp + j) * CHUNK, (r * hp + j + 1) * CHUNK)

        for r in range(d):
            for j in range(hp):
                qb = blk(qf, r, j)
                sc_s[bs(r, j), :] = _dot_nt(qb, blk(kf, r, j))
                sp_s[bs(r, j), :] = _dot_nt(qb, blk(kpf, r, j))
        row = lax.broadcasted_iota(jnp.int32, (nblocks * CHUNK, CHUNK), 0) & (CHUNK - 1)
        col = lax.broadcasted_iota(jnp.int32, (nblocks * CHUNK, CHUNK), 1)
        sc = jnp.where(col <= row, sc_s[...] * ATT_SCALE, NEG)
        sp = jnp.where(col >= row, sp_s[...] * ATT_SCALE, NEG) + jnp.where(i > 0, 0.0, NEG)
        m = jnp.maximum(jnp.max(sc, axis=-1, keepdims=True), jnp.max(sp, axis=-1, keepdims=True))
        ec = jnp.exp(sc - m)
        ep = jnp.exp(sp - m)
        den = jnp.sum(ec, axis=-1, keepdims=True) + jnp.sum(ep, axis=-1, keepdims=True)
        inv = 1.0 / den
        pc_s[...] = (ec * inv).astype(BF16)
        pp_s[...] = (ep * inv).astype(BF16)
        lse = m + jnp.log(den)
        lane = lax.broadcasted_iota(jnp.int32, (CHUNK, 128), 1)
        for r in range(d):
            lblk = jnp.zeros((CHUNK, 128), F32)
            for j in range(hp):
                o = _dot(pc_s[bs(r, j), :], blk(vf, r, j)) + _dot(pp_s[bs(r, j), :], blk(vpf, r, j))
                of[j, r * CHUNK:(r + 1) * CHUNK, :] = o
                lblk = jnp.where(lane == hh * hp + j, lse[bs(r, j)], lblk)
            lf[r * CHUNK:(r + 1) * CHUNK, :] = lblk
        _from_class(of, o_ref, d, hp, nat)
        lnat = pltpu.einshape("(rt)l->(tr)l", lf[...], r=d)

        @pl.when(hh == 0)
        def _():
            l_ref[...] = lnat

        @pl.when(hh > 0)
        def _():
            l_ref[...] += lnat

    def cols(cb):
        return pl.BlockSpec((T, cw), lambda i, hh: (i, cb * nh + hh))

    tile = pltpu.VMEM((hp, T, HEAD), BF16)
    staging = pltpu.VMEM((hp, T, HEAD) if _strided_regroup(d) else (1, 8, HEAD), F32)
    return pl.pallas_call(
        body, name=f"attn_fwd_dilated_{d}",
        grid=(S // T, nh),
        in_specs=[cols(qcb), cols(kcb), cols(vcb)],
        out_specs=[cols(0), pl.BlockSpec((T, 128), lambda i, hh: (i, 0))],
        out_shape=[jax.ShapeDtypeStruct((S, WIDTH), BF16), jax.ShapeDtypeStruct((S, 128), F32)],
        scratch_shapes=[tile, pltpu.VMEM((2, nh, hp, T, HEAD), BF16), pltpu.VMEM((2, nh, hp, T, HEAD), BF16),
                        pltpu.VMEM((hp, T, HEAD), F32), pltpu.VMEM((T, 128), F32), staging,
                        pltpu.VMEM((nblocks * CHUNK, CHUNK), F32), pltpu.VMEM((nblocks * CHUNK, CHUNK), F32),
                        pltpu.VMEM((nblocks * CHUNK, CHUNK), BF16), pltpu.VMEM((nblocks * CHUNK, CHUNK), BF16)],
        compiler_params=_params(("arbitrary", "arbitrary")),
    )(proj, proj, proj)


def _attn_bwd_dilated(proj, qcb, kcb, vcb, do, lse, delta, d, dk_in, dv_in, dst, dcb):
    S = proj.shape[0]
    T = CHUNK * d
    nt = S // T
    hp, nh, cw = _dilated_split(d)
    nblocks = d * hp

    def body(q_ref, k_ref, v_ref, do_ref, l_ref, d_ref, dki_ref, dvi_ref, dst_ref, dq_ref, dk_ref, dv_ref,
             qf, dof, kbuf, vbuf, dqf, gk, gv, nat,
             sc_s, sp_s, dpc_s, dpp_s, pc_s, pp_s, dsc_s, dsp_s):
        hh, i = pl.program_id(0), pl.program_id(1)
        kf, vf, newk, newv = kbuf.at[i % 2], vbuf.at[i % 2], gk.at[i % 2], gv.at[i % 2]
        kpf, vpf, acck, accv = kbuf.at[1 - i % 2], vbuf.at[1 - i % 2], gk.at[1 - i % 2], gv.at[1 - i % 2]

        @pl.when(i == 0)
        def _():
            for ref in (kbuf, vbuf, gk, gv):
                ref[...] = jnp.zeros_like(ref)
            dk_ref[...] = jnp.zeros_like(dk_ref)
            dv_ref[...] = jnp.zeros_like(dv_ref)

        def blk(ref, r, j):
            return ref[j, r * CHUNK:(r + 1) * CHUNK, :]

        def bs(r, j):
            return slice((r * hp + j) * CHUNK, (r * hp + j + 1) * CHUNK)

        @pl.when(i < nt)
        def _():
            _by_class(q_ref, qf, d, hp, nat)
            _by_class(do_ref, dof, d, hp, nat)
            _by_class(k_ref, kf, d, hp, nat)
            _by_class(v_ref, vf, d, hp, nat)
            lses, deltas = [], []
            lcls = pltpu.einshape("(tr)l->(rt)l", l_ref[...], r=d)
            dcls = pltpu.einshape("(tr)l->(rt)l", d_ref[...], r=d)
            for r in range(d):
                lblk = lcls[r * CHUNK:(r + 1) * CHUNK]
                dblk = dcls[r * CHUNK:(r + 1) * CHUNK]
                for j in range(hp):
                    lses.append(_col(lblk, hh * hp + j))
                    deltas.append(_col(dblk, hh * hp + j))
                    qb, dob = blk(qf, r, j), blk(dof, r, j)
                    sc_s[bs(r, j), :] = _dot_nt(qb, blk(kf, r, j))
                    dpc_s[bs(r, j), :] = _dot_nt(dob, blk(vf, r, j))
                    sp_s[bs(r, j), :] = _dot_nt(qb, blk(kpf, r, j))
                    dpp_s[bs(r, j), :] = _dot_nt(dob, blk(vpf, r, j))
            lse = jnp.concatenate(lses, axis=0)
            delta = jnp.concatenate(deltas, axis=0)
            row = lax.broadcasted_iota(jnp.int32, (nblocks * CHUNK, CHUNK), 0) & (CHUNK - 1)
            col = lax.broadcasted_iota(jnp.int32, (nblocks * CHUNK, CHUNK), 1)
            sp = jnp.where(col >= row, sp_s[...] * ATT_SCALE, NEG) + jnp.where(i > 0, 0.0, NEG)
            pp = jnp.exp(sp - lse)
            pp_s[...] = pp.astype(BF16)
            dsp_s[...] = (pp * (dpp_s[...] - delta)).astype(BF16)
            sc = jnp.where(col <= row, sc_s[...] * ATT_SCALE, NEG)
            pc = jnp.exp(sc - lse)
            pc_s[...] = pc.astype(BF16)
            dsc_s[...] = (pc * (dpc_s[...] - delta)).astype(BF16)
            for r in range(d):
                rows = slice(r * CHUNK, (r + 1) * CHUNK)
                for j in range(hp):
                    qb, dob = blk(qf, r, j), blk(dof, r, j)
                    dsc, dsp = dsc_s[bs(r, j), :], dsp_s[bs(r, j), :]
                    dqf[j, rows, :] = (_dot(dsc, blk(kf, r, j)) + _dot(dsp, blk(kpf, r, j))) * ATT_SCALE
                    newk[j, rows, :] = _dot_tn(dsc, qb) * ATT_SCALE
                    newv[j, rows, :] = _dot_tn(pc_s[bs(r, j), :], dob)
                    acck[j, rows, :] += _dot_tn(dsp, qb) * ATT_SCALE
                    accv[j, rows, :] += _dot_tn(pp_s[bs(r, j), :], dob)
            _from_class(dqf, dq_ref, d, hp, nat)

        @pl.when(i > 0)
        def _():
            _from_class(acck, dk_ref, d, hp, nat, dki_ref)
            _from_class(accv, dv_ref, d, hp, nat, dvi_ref)

    def cur(width, cb, nsplit):
        return pl.BlockSpec((T, width), lambda hh, i: (jnp.minimum(i, nt - 1), cb * nsplit + hh * (nsplit > 1)))

    def lag():
        return pl.BlockSpec((T, cw), lambda hh, i: (jnp.maximum(i - 1, 0), hh))

    tile = pltpu.VMEM((hp, T, HEAD), BF16)
    acc = pltpu.VMEM((hp, T, HEAD), F32)
    f32s = pltpu.VMEM((nblocks * CHUNK, CHUNK), F32)
    b16s = pltpu.VMEM((nblocks * CHUNK, CHUNK), BF16)
    return pl.pallas_call(
        body, name=f"attn_bwd_dilated_{d}",
        grid=(nh, nt + 1),
        in_specs=[cur(cw, qcb, nh), cur(cw, kcb, nh), cur(cw, vcb, nh), cur(cw, 0, nh), cur(128, 0, 1), cur(128, 0, 1),
                  lag(), lag(), ANY],
        out_specs=[cur(cw, dcb, nh), lag(), lag()],
        out_shape=[jax.ShapeDtypeStruct(dst.shape, BF16)] + [jax.ShapeDtypeStruct((S, WIDTH), BF16)] * 2,
        input_output_aliases={8: 0},
        scratch_shapes=[tile, tile, pltpu.VMEM((2, hp, T, HEAD), BF16), pltpu.VMEM((2, hp, T, HEAD), BF16), acc,
                        pltpu.VMEM((2, hp, T, HEAD), F32), pltpu.VMEM((2, hp, T, HEAD), F32),
                        acc if _strided_regroup(d) else pltpu.VMEM((1, 8, HEAD), F32)]
                       + [f32s] * 4 + [b16s] * 4,
        compiler_params=_params(("arbitrary", "arbitrary")),
    )(proj, proj, proj, do, lse, delta, dk_in, dv_in, dst)


def _abm_bwd(proj, cdf, dy3, ln_g, ln_b, wsm, wsm_t, bias_full, pool_w, pool_wt, pool_scale, kv, bands, bands_t, dst):
    S = proj.shape[0]
    tm = 512
    nchunk = tm // CHUNK
    nblk = S // CHUNK

    def body(u_ref, v_ref, ag_ref, p_ref, ph_ref, pg_ref, pgn_ref, mq_ref, mg_ref, cdf_ref, dy_ref, dypn_ref,
             lng_ref, lnb_ref, wsm_ref, wsmt_ref, bias_ref, pw_ref, pwt_ref, ps_ref, kv_ref, band_ref, bandt_ref,
             dst_ref, dab_ref, dm_ref, dlng_ref, dlnb_ref, dws_ref, dbias_ref, dpw_ref, dps_ref, dkv_ref,
             mix, dvl, ddn):
        i = pl.program_id(0)

        @pl.when(i == 0)
        def _():
            for r in (dlng_ref, dlnb_ref, dws_ref, dbias_ref, dpw_ref, dps_ref, dkv_ref):
                r[...] = jnp.zeros_like(r)

        au = u_ref[...].astype(F32)
        av = v_ref[...].astype(F32)
        ag = ag_ref[...].astype(F32)
        u, du = _gelu_and_grad(au, cdf_ref[0].astype(F32))
        v, dgelu_v = _gelu_and_grad(av, cdf_ref[1].astype(F32))
        vhat, rstd = _layer_norm_fwd(v)
        vln = (vhat * lng_ref[...] + lnb_ref[...]).astype(BF16)
        for c in range(nchunk):
            for h in range(N_HEAD):
                rs, cs = slice(c * CHUNK, (c + 1) * CHUNK), slice(h * HEAD, (h + 1) * HEAD)
                mix[rs, cs] = _dot(wsm_ref[h], vln[rs, cs]) + bias_ref[:, cs]
        dya = dy_ref[0].astype(F32)
        sg, dsg = _silu_and_grad(ag)
        mixed = mix[...]
        dab_ref[:, 2 * WIDTH:3 * WIDTH] = (dya * u * mixed * dsg).astype(BF16)
        dab_ref[:, 0:WIDTH] = (dya * mixed * sg * du).astype(BF16)
        dmixed = dya * u * sg
        dmb = dmixed.astype(BF16)
        tril = (lax.broadcasted_iota(jnp.int32, (CHUNK, CHUNK), 1)
                <= lax.broadcasted_iota(jnp.int32, (CHUNK, CHUNK), 0))
        for c in range(nchunk):
            rs = slice(c * CHUNK, (c + 1) * CHUNK)
            dbias_ref[...] += dmixed[rs, :]
            for h in range(N_HEAD):
                cs = slice(h * HEAD, (h + 1) * HEAD)
                dvl[rs, cs] = _dot(wsmt_ref[h], dmb[rs, cs])
                dws_ref[h] += jnp.where(tril, _dot_nt(dmb[rs, cs], vln[rs, cs]), 0.0)
        dvln = dvl[...]
        dlng_ref[...] += jnp.sum(dvln * vhat, axis=0, keepdims=True)
        dlnb_ref[...] += jnp.sum(dvln, axis=0, keepdims=True)
        dvh = dvln * lng_ref[...]
        dv = rstd * (dvh - jnp.mean(dvh, axis=-1, keepdims=True)
                     - vhat * jnp.mean(dvh * vhat, axis=-1, keepdims=True))
        dab_ref[:, WIDTH:2 * WIDTH] = (dv * dgelu_v).astype(BF16)

        halo_ok = (i > 0).astype(F32)
        for c in range(nchunk):
            rs = slice(c * CHUNK, (c + 1) * CHUNK)
            for g, win in enumerate(POOL_WINDOWS):
                cs = slice(g * HEAD, (g + 1) * HEAD)
                cur = p_ref[rs, cs]
                if c == 0:
                    prev = (ph_ref[:, cs].astype(F32) * halo_ok).astype(BF16)
                else:
                    prev = p_ref[(c - 1) * CHUNK:c * CHUNK, cs]
                sums = _dot(band_ref[g, 0], cur) + _dot(band_ref[g, 1], prev)
                dvl[rs, cs] = sums * _inv_count(i * tm + c * CHUNK, win) - cur.astype(F32)
        dmat = dvl[...].astype(BF16)
        for g in range(4):
            cs = slice(g * HEAD, (g + 1) * HEAD)
            mix[:, cs] = _dot(dmat[:, cs], pw_ref[g])
        yg = mix[...]
        pg = pg_ref[...].astype(F32)
        dyp = dy_ref[1].astype(F32)
        spg, dspg = _silu_and_grad(pg)
        dyy = dyp * spg
        scale = ps_ref[...]
        dab_ref[:, 4 * WIDTH:5 * WIDTH] = (dyp * yg * scale * dspg).astype(BF16)
        dps_ref[...] += jnp.sum(dyy * yg, axis=0, keepdims=True)
        dyg = (dyy * scale).astype(BF16)
        for g in range(4):
            cs = slice(g * HEAD, (g + 1) * HEAD)
            dpw_ref[g] += _dot_tn(dmat[:, cs], dyg[:, cs])
            mix[:, cs] = _dot(dyg[:, cs], pwt_ref[g])
        next_ok = (i + 1 < S // tm).astype(F32)
        dygn = (dypn_ref[...].astype(F32) * _silu(pgn_ref[...].astype(F32)) * scale * next_ok).astype(BF16)
        for c in range(nchunk + 1):
            for g, win in enumerate(POOL_WINDOWS):
                cs = slice(g * HEAD, (g + 1) * HEAD)
                if c < nchunk:
                    dd = mix[c * CHUNK:(c + 1) * CHUNK, cs]
                else:
                    dd = _dot(dygn[:, cs], pwt_ref[g])
                ddn[c * CHUNK:(c + 1) * CHUNK, cs] = dd * _inv_count(i * tm + c * CHUNK, win)
        ddnb = ddn[...].astype(BF16)
        for c in range(nchunk):
            rs = slice(c * CHUNK, (c + 1) * CHUNK)
            ns = slice((c + 1) * CHUNK, (c + 2) * CHUNK)
            for g, win in enumerate(POOL_WINDOWS):
                cs = slice(g * HEAD, (g + 1) * HEAD)
                dp = _dot(bandt_ref[g, 0], ddnb[rs, cs]) + _dot(bandt_ref[g, 1], ddnb[ns, cs]) - mix[rs, cs]
                dab_ref[rs, 3 * WIDTH + g * HEAD:3 * WIDTH + (g + 1) * HEAD] = dp.astype(BF16)

        mg = mg_ref[...].astype(F32)
        dym = dy_ref[2].astype(F32)
        smg, dsmg = _silu_and_grad(mg)
        dob = (dym * smg).astype(BF16)
        for h in range(N_HEAD):
            cs = slice(h * HEAD, (h + 1) * HEAD)
            vs = slice(WIDTH + h * HEAD, WIDTH + (h + 1) * HEAD)
            qh = mq_ref[:, cs]
            p = _mem_softmax(qh, kv_ref[:, cs])
            pb = p.astype(BF16)
            mix[:, cs] = _dot(pb, kv_ref[:, vs])
            dp = _dot_nt(dob[:, cs], kv_ref[:, vs])
            ds = (p * (dp - jnp.sum(p * dp, axis=-1, keepdims=True))).astype(BF16)
            dm_ref[:, cs] = (_dot(ds, kv_ref[:, cs]) * ATT_SCALE).astype(BF16)
            dkv_ref[:, cs] += _dot_tn(ds, qh) * ATT_SCALE
            dkv_ref[:, vs] += _dot_tn(pb, dob[:, cs])
        dm_ref[:, WIDTH:2 * WIDTH] = (dym * mix[...] * dsmg).astype(BF16)

    blk = tm // CHUNK
    small = [_full((1, WIDTH)), _full((1, WIDTH)), _full((N_HEAD, CHUNK, CHUNK)), _full((CHUNK, WIDTH)),
             _full((4, HEAD, HEAD)), _full((1, WIDTH)), _full((MEM_LEN, 2 * WIDTH))]
    return pl.pallas_call(
        body, name="abm_bwd",
        grid=(S // tm,),
        in_specs=[_rows(tm, WIDTH, CB_U), _rows(tm, WIDTH, CB_V), _rows(tm, WIDTH, CB_AGATE),
                  _rows(tm, WIDTH, CB_PIN),
                  pl.BlockSpec((CHUNK, WIDTH), lambda i: (jnp.maximum(i * blk - 1, 0), CB_PIN)),
                  _rows(tm, WIDTH, CB_PGATE),
                  pl.BlockSpec((CHUNK, WIDTH), lambda i: (jnp.minimum(i * blk + blk, nblk - 1), CB_PGATE)),
                  _rows(tm, WIDTH, CB_MQ), _rows(tm, WIDTH, CB_MGATE),
                  pl.BlockSpec((2, tm, WIDTH), lambda i: (0, i, 0)),
                  pl.BlockSpec((3, tm, WIDTH), lambda i: (0, i, 0)),
                  pl.BlockSpec((None, CHUNK, WIDTH), lambda i: (1, jnp.minimum(i * blk + blk, nblk - 1), 0)),
                  _full((1, WIDTH)), _full((1, WIDTH)), _full((N_HEAD, CHUNK, CHUNK)), _full((N_HEAD, CHUNK, CHUNK)),
                  _full((CHUNK, WIDTH)), _full((4, HEAD, HEAD)), _full((4, HEAD, HEAD)), _full((1, WIDTH)),
                  _full((MEM_LEN, 2 * WIDTH)), _full((4, 2, CHUNK, CHUNK)), _full((4, 2, CHUNK, CHUNK)), ANY],
        out_specs=[_rows(tm, 5 * WIDTH), _rows(tm, 2 * WIDTH)] + small,
        out_shape=[jax.ShapeDtypeStruct(dst.shape, BF16), jax.ShapeDtypeStruct((S, 2 * WIDTH), BF16),
                   jax.ShapeDtypeStruct((1, WIDTH), F32), jax.ShapeDtypeStruct((1, WIDTH), F32),
                   jax.ShapeDtypeStruct((N_HEAD, CHUNK, CHUNK), F32), jax.ShapeDtypeStruct((CHUNK, WIDTH), F32),
                   jax.ShapeDtypeStruct((4, HEAD, HEAD), F32), jax.ShapeDtypeStruct((1, WIDTH), F32),
                   jax.ShapeDtypeStruct((MEM_LEN, 2 * WIDTH), F32)],
        input_output_aliases={23: 0},
        scratch_shapes=[pltpu.VMEM((tm, WIDTH), F32), pltpu.VMEM((tm, WIDTH), F32),
                        pltpu.VMEM((tm + CHUNK, WIDTH), F32)],
        compiler_params=_params(("arbitrary",)),
    )(proj, proj, proj, proj, proj, proj, proj, proj, proj, cdf, dy3, dy3,
      ln_g, ln_b, wsm, wsm_t, bias_full, pool_w, pool_wt, pool_scale, kv, bands, bands_t, dst)


def _bias_reduce(dbias_full):
    def body(d_ref, o_ref):
        d = d_ref[...]
        o_ref[...] = _put_cols([jnp.sum(d[:, h * HEAD:(h + 1) * HEAD], axis=1, keepdims=True) for h in range(N_HEAD)])

    return pl.pallas_call(body, name="bias_reduce", out_shape=jax.ShapeDtypeStruct((CHUNK, 128), F32))(dbias_full)


def _mem_bwd(mem, g, mem_n, w, dkv):
    def body(m_ref, g_ref, mn_ref, w_ref, dkv_ref, dw_ref, dg_ref):
        dkvb = dkv_ref[...].astype(BF16)
        dw_ref[...] = _dot_tn(mn_ref[...], dkvb).astype(BF16)
        dmn = _dot_nt(dkvb, w_ref[...])
        xf = m_ref[...]
        r = lax.rsqrt(jnp.mean(xf * xf, axis=-1, keepdims=True) + EPS)
        dg_ref[...] = jnp.sum(dmn * xf * r, axis=0, keepdims=True)

    return pl.pallas_call(
        body, name="mem_bwd",
        out_shape=[jax.ShapeDtypeStruct((D_MODEL, 2 * WIDTH), BF16), jax.ShapeDtypeStruct((1, D_MODEL), F32)],
        compiler_params=pltpu.CompilerParams(vmem_limit_bytes=VMEM_LIMIT),
    )(mem, g, mem_n, w, dkv)


def _dh_bwd(dpb, dpg, wt, x, g, dxo, parts=(), grads=()):
    S = x.shape[0]
    tm, tkb, tkg = 1024, D_BRANCHES // 4, D_GATES // 4
    nkb, nkg = 4, 4
    nk = nkb + nkg
    ni = S // tm
    n, m = len(parts), len(grads)

    def body(dpb_ref, wbr_ref, dpg_ref, wg0_ref, wg1_ref, x_ref, g_ref, dxo_ref, *rest):
        p_in, g_in = rest[:n], rest[n:n + m]
        dx_ref, dg_ref = rest[n + m:n + m + 2]
        p_out, g_out = rest[n + m + 2:2 * n + m + 2], rest[2 * n + m + 2:2 * (n + m) + 2]
        acc, sems = rest[2 * (n + m) + 2], rest[2 * (n + m) + 3:]
        second_sems, first_sems = (sems[:3] if n else ()), sems[3 if n else 0:]
        i, kk = pl.program_id(0), pl.program_id(1)

        @pl.when(jnp.logical_and(i == 0, kk == 0))
        def _():
            dg_ref[...] = jnp.zeros_like(dg_ref)
            if n:
                _comm_start(_rs_second(p_in, p_out, *second_sems))
            if m:
                _comm_start(_rs_first(g_in, g_out, *first_sems))

        @pl.when(kk == 0)
        def _():
            acc[...] = jnp.zeros_like(acc)

        @pl.when(kk < nkb)
        def _():
            acc[...] += _dot(dpb_ref[...], wbr_ref[...])

        @pl.when(kk >= nkb)
        def _():
            acc[...] += _dot(dpg_ref[:, :tkg // 2], wg0_ref[...]) + _dot(dpg_ref[:, tkg // 2:], wg1_ref[...])

        @pl.when(kk == nk - 1)
        def _():
            xf = x_ref[...]
            r = lax.rsqrt(jnp.mean(xf * xf, axis=-1, keepdims=True) + EPS)
            xhat = xf * r
            dh = acc[...]
            dg_ref[...] += jnp.sum(dh * xhat, axis=0, keepdims=True)
            dxh = dh * g_ref[...]
            dx_ref[...] = dxo_ref[...] + r * (dxh - xhat * jnp.mean(dxh * xhat, axis=-1, keepdims=True))

        if n or m:
            @pl.when(jnp.logical_and(i == ni - 1, kk == nk - 1))
            def _():
                if n:
                    _comm_wait(_rs_second(p_in, p_out, *second_sems))
                if m:
                    _comm_wait(_rs_first(g_in, g_out, *first_sems))

    def gate_rows(half):
        return pl.BlockSpec((tkg // 2, D_MODEL),
                            lambda i, k: (D_BRANCHES // (tkg // 2) + 2 * jnp.maximum(k - nkb, 0) + half, 0))

    res = pl.pallas_call(
        body, name="dh_bwd_scatter" if (n or m) else "dh_bwd",
        grid=(ni, nk),
        in_specs=[pl.BlockSpec((tm, tkb), lambda i, k: (i, jnp.minimum(k, nkb - 1))),
                  pl.BlockSpec((tkb, D_MODEL), lambda i, k: (jnp.minimum(k, nkb - 1), 0)),
                  pl.BlockSpec((tm, tkg), lambda i, k: (i, jnp.maximum(k - nkb, 0))),
                  gate_rows(0), gate_rows(1),
                  pl.BlockSpec((tm, D_MODEL), lambda i, k: (i, 0)), pl.BlockSpec((1, D_MODEL), lambda i, k: (0, 0)),
                  pl.BlockSpec((tm, D_MODEL), lambda i, k: (i, 0))] + [ANY] * (n + m),
        out_specs=[pl.BlockSpec((tm, D_MODEL), lambda i, k: (i, 0)), pl.BlockSpec((1, D_MODEL), lambda i, k: (0, 0))]
                  + [ANY] * (n + m),
        out_shape=[jax.ShapeDtypeStruct((S, D_MODEL), F32), jax.ShapeDtypeStruct((1, D_MODEL), F32)]
                  + [jax.ShapeDtypeStruct(p.shape, p.dtype) for p in parts]
                  + [jax.ShapeDtypeStruct(gr.shape[:1] + gr.shape[2:], gr.dtype) for gr in grads],
        scratch_shapes=[pltpu.VMEM((tm, D_MODEL), F32)] + (_dma_sems(3 * n, 3 * n, n) if n else [])
                       + (_dma_sems(N_CHIP * m, N_CHIP * m) if m else []),
        compiler_params=_params(("arbitrary", "arbitrary")),
    )(dpb, wt, dpg, wt, wt, x, g, dxo, *parts, *grads)
    return res[0], res[1], list(res[2:2 + n]), list(res[2 + n:])


def _dw_in(h, dpb, dpg, parts=(), grads=()):
    S = h.shape[0]
    tk = 2048
    nk = S // tk
    n, m = len(parts), len(grads)
    tmb = D_BRANCHES // 4
    ng = D_GATES // GATE_TILE

    def accumulate(a_ref, h_ref, o_ref, acc):
        kk = pl.program_id(1)

        @pl.when(kk == 0)
        def _():
            acc[...] = jnp.zeros_like(acc)

        acc[...] += _dot_tn(a_ref[...], h_ref[...])

        @pl.when(kk == nk - 1)
        def _():
            o_ref[...] = acc[...].astype(BF16)

    def branches(a_ref, h_ref, *rest):
        p_in, g_in, o_ref = rest[:n], rest[n:n + m], rest[n + m]
        p_out, g_out = rest[n + m + 1:2 * n + m + 1], rest[2 * n + m + 1:2 * (n + m) + 1]
        acc, sems = rest[2 * (n + m) + 1], rest[2 * (n + m) + 2:]
        second_sems, first_sems = (sems[:3] if n else ()), sems[3 if n else 0:]
        i, kk = pl.program_id(0), pl.program_id(1)

        if n or m:
            @pl.when(jnp.logical_and(i == 0, kk == 0))
            def _():
                if n:
                    _comm_start(_rs_second(p_in, p_out, *second_sems))
                if m:
                    _comm_start(_rs_first(g_in, g_out, *first_sems))

        accumulate(a_ref, h_ref, o_ref, acc)

        if n or m:
            @pl.when(jnp.logical_and(i == 3, kk == nk - 1))
            def _():
                if n:
                    _comm_wait(_rs_second(p_in, p_out, *second_sems))
                if m:
                    _comm_wait(_rs_first(g_in, g_out, *first_sems))

    def gates(a_ref, h_ref, dst_ref, *rest):
        q_in, o_ref, q_out, acc, sems = rest[:m], rest[m], rest[m + 1:2 * m + 1], rest[2 * m + 1], rest[2 * m + 2:]
        i, kk = pl.program_id(0), pl.program_id(1)

        if m:
            @pl.when(jnp.logical_and(i == 0, kk == 0))
            def _():
                _comm_start(_rs_second(q_in, q_out, *sems))

        accumulate(a_ref, h_ref, o_ref, acc)

        if m:
            @pl.when(jnp.logical_and(i == ng - 1, kk == nk - 1))
            def _():
                _comm_wait(_rs_second(q_in, q_out, *sems))

    res = pl.pallas_call(
        branches, name="dw_in_branches_scatter" if (n or m) else "dw_in_branches",
        grid=(4, nk),
        in_specs=[pl.BlockSpec((tk, tmb), lambda i, k: (k, i)), pl.BlockSpec((tk, D_MODEL), lambda i, k: (k, 0))]
                 + [ANY] * (n + m),
        out_specs=[pl.BlockSpec((tmb, D_MODEL), lambda i, k: (i, 0))] + [ANY] * (n + m),
        out_shape=[jax.ShapeDtypeStruct((D_IN, D_MODEL), BF16)]
                  + [jax.ShapeDtypeStruct(p.shape, p.dtype) for p in parts]
                  + [jax.ShapeDtypeStruct(gr.shape[:1] + gr.shape[2:], gr.dtype) for gr in grads],
        scratch_shapes=[pltpu.VMEM((tmb, D_MODEL), F32)] + (_dma_sems(3 * n, 3 * n, n) if n else [])
                       + (_dma_sems(N_CHIP * m, N_CHIP * m) if m else []),
        compiler_params=_params(("arbitrary", "arbitrary")),
    )(dpb, h, *parts, *grads)
    pair = _pair_sum(grads, res[1 + n:]) if m else []
    res2 = pl.pallas_call(
        gates, name="dw_in_gates_scatter" if m else "dw_in_gates",
        grid=(ng, nk),
        in_specs=[pl.BlockSpec((tk, GATE_TILE), lambda i, k: (k, i)), pl.BlockSpec((tk, D_MODEL), lambda i, k: (k, 0)),
                  ANY] + [ANY] * m,
        out_specs=[pl.BlockSpec((GATE_TILE, D_MODEL), lambda i, k: (D_BRANCHES // GATE_TILE + i, 0))] + [ANY] * m,
        out_shape=[jax.ShapeDtypeStruct((D_IN, D_MODEL), BF16)] + [jax.ShapeDtypeStruct(p.shape, p.dtype) for p in pair],
        input_output_aliases={2: 0},
        scratch_shapes=[pltpu.VMEM((GATE_TILE, D_MODEL), F32)] + (_dma_sems(3 * m, 3 * m, m) if m else []),
        compiler_params=_params(("arbitrary", "arbitrary")),
    )(dpg, h, res[0], *pair)
    return res2[0], list(res[1:1 + n]), list(res2[1:])


def _row_tile(R, C, block_bytes=2 << 20):
    for cand in range(min(R, block_bytes // (C * 4)) // 8 * 8, 0, -8):
        if R % cand == 0:
            return cand
    return R


def _adamw_update(p_ref, w_ref, m_ref, v_ref, g_ref, d_ref, nm_ref, nv_ref):
    c1 = 1.0 / (1.0 - ADAM_B1 ** ADAM_STEP)
    c2 = 1.0 / (1.0 - ADAM_B2 ** ADAM_STEP)
    g = p_ref[0].astype(F32)
    for k in range(1, p_ref.shape[0]):
        g = g + p_ref[k].astype(F32)
    nm = ADAM_B1 * m_ref[...] + (1.0 - ADAM_B1) * g
    nv = ADAM_B2 * v_ref[...] + (1.0 - ADAM_B2) * (g * g)
    g_ref[...] = g
    nm_ref[...] = nm
    nv_ref[...] = nv
    d_ref[...] = -ADAM_LR * ((nm * c1) / (jnp.sqrt(nv * c2) + ADAM_EPS) + ADAM_WD * w_ref[...])


def _adamw(parts, w, m, v, name):
    P, R, C = parts.shape
    tr = _row_tile(R, C)

    def body(*refs):
        _adamw_update(*refs)

    spec = pl.BlockSpec((tr, C), lambda i: (i, 0))
    return pl.pallas_call(
        body, name=name,
        grid=(R // tr,),
        in_specs=[pl.BlockSpec((P, tr, C), lambda i: (0, i, 0)), spec, spec, spec],
        out_specs=[spec] * 4,
        out_shape=[jax.ShapeDtypeStruct((R, C), F32)] * 4,
        compiler_params=_params(("parallel",)),
    )(parts, w, m, v)


def _adamw_layers(parts, w, m, v, name):
    depth = len(parts)
    P, R, C = parts[0].shape
    tr = _row_tile(R, C, 1 << 20)

    def body(*refs):
        layer = pl.program_id(0)
        for k in range(depth):
            @pl.when(layer == k)
            def _(k=k):
                _adamw_update(refs[k], *refs[depth:])

    def part_spec(k):
        return pl.BlockSpec((P, tr, C), lambda l, i: (0, jnp.where(l == k, i, 0), 0))

    spec = pl.BlockSpec((None, tr, C), lambda l, i: (l, i, 0))
    return pl.pallas_call(
        body, name=name,
        grid=(depth, R // tr),
        in_specs=[part_spec(k) for k in range(depth)] + [spec] * 3,
        out_specs=[spec] * 4,
        out_shape=[jax.ShapeDtypeStruct((depth, R, C), F32)] * 4,
        compiler_params=_params(("arbitrary", "arbitrary")),
    )(*parts, w, m, v)


def _place():
    return lax.axis_index("x"), lax.axis_index("y"), lax.axis_index("c")


def _all_gather(shards):
    n = len(shards)

    def body(*refs):
        ins, outs = refs[:n], refs[n:2 * n]
        send1, recv1, local_sems, relay_send, relay_recv, send2, recv2 = refs[2 * n:]
        x, y, c = _place()
        me, sibling = (x, y, c), (x, y, 1 - c)
        x_nbr, y_nbr, diagonal = [(*chip, c) for chip in _other_chips(x, y)]
        local, first_out, first_in, relay_out, relay_in = [], [], [], [], []
        for a in range(n):
            local.append(pltpu.make_async_copy(ins[a], outs[a].at[_dev(me)], local_sems.at[a]))
            for k, to in enumerate((sibling, x_nbr, y_nbr)):
                first_out.append(_remote(ins[a], outs[a].at[_dev(me)], send1, recv1, 3 * a + k, to))
                first_in.append(_remote(ins[a], outs[a].at[_dev(to)], send1, recv1, 3 * a + k, to))
            half = shards[a].shape[0] // 2
            for k, (src, to, rows) in enumerate(((x_nbr, y_nbr, pl.ds(0, half)), (y_nbr, x_nbr, pl.ds(half, half)))):
                passed, got = outs[a].at[_dev(src)].at[rows], outs[a].at[_dev(diagonal)].at[rows]
                relay_out.append(_remote(passed, passed, relay_send, relay_recv, 2 * a + k, to))
                relay_in.append(_remote(got, got, relay_send, relay_recv, 2 * a + k, to))
        second = _ag_second(outs, send2, recv2)
        for cp in local + first_out:
            cp.start()
        for a in range(n):
            for k in range(2):
                first_in[3 * a + 1 + k].wait_recv()
                relay_out[2 * a + k].start()
                second[1][3 * a + k].start()
        for a in range(n):
            for k in range(2):
                relay_in[2 * a + k].wait_recv()
            second[1][3 * a + 2].start()
        for a in range(n):
            first_in[3 * a].wait_recv()
        for cp in second[2]:
            cp.wait_recv()
        for cp in first_out + relay_out + second[1]:
            cp.wait_send()
        for cp in local:
            cp.wait()

    assert all(s.shape[0] % 32 == 0 for s in shards)
    return pl.pallas_call(
        body, name="weights_all_gather",
        in_specs=[ANY] * n, out_specs=[ANY] * n,
        out_shape=[jax.ShapeDtypeStruct((N_DEV,) + s.shape, s.dtype) for s in shards],
        scratch_shapes=_dma_sems(3 * n, 3 * n, n, 2 * n, 2 * n, 3 * n, 3 * n),
        compiler_params=pltpu.CompilerParams(has_side_effects=True),
    )(*shards)


N_BIG = 4


def _dev(p):
    return 4 * p[0] + 2 * p[1] + p[2]


def _other_chips(x, y):
    return [(1 - x, y), (x, 1 - y), (1 - x, 1 - y)]


def _remote(src, dst, send_sems, recv_sems, k, to):
    return pltpu.make_async_remote_copy(src_ref=src, dst_ref=dst, send_sem=send_sems.at[k], recv_sem=recv_sems.at[k],
                                        device_id=to, device_id_type=MESH)


def _ag_first(ins, outs, send_sems, recv_sems, local_sems):
    x, y, c = _place()
    me = (x, y, c)
    targets = [(x, y, 1 - c)] + [(*chip, c) for chip in _other_chips(x, y)]
    local, out, inc = [], [], []
    for a in range(len(ins)):
        local.append(pltpu.make_async_copy(ins[a], outs[a].at[_dev(me)], local_sems.at[a]))
        for k, to in enumerate(targets):
            out.append(_remote(ins[a], outs[a].at[_dev(me)], send_sems, recv_sems, 4 * a + k, to))
            inc.append(_remote(ins[a], outs[a].at[_dev(to)], send_sems, recv_sems, 4 * a + k, to))
    return local, out, inc


def _ag_second(bufs, send_sems, recv_sems):
    x, y, c = _place()
    out, inc = [], []
    for a in range(len(bufs)):
        for j, chip in enumerate(_other_chips(x, y)):
            mine, theirs = bufs[a].at[_dev((*chip, c))], bufs[a].at[_dev((*chip, 1 - c))]
            out.append(_remote(mine, mine, send_sems, recv_sems, 3 * a + j, (x, y, 1 - c)))
            inc.append(_remote(theirs, theirs, send_sems, recv_sems, 3 * a + j, (x, y, 1 - c)))
    return [], out, inc


def _rs_first(ins, outs, send_sems, recv_sems):
    x, y, c = _place()
    out = [_remote(ins[a].at[j, 1 - c], outs[a].at[j], send_sems, recv_sems, N_CHIP * a + j, (x, y, 1 - c))
           for a in range(len(ins)) for j in range(N_CHIP)]
    return [], out, out


def _rs_second(ins, outs, send_sems, recv_sems, local_sems):
    x, y, c = _place()
    my_chip = 2 * x + y
    local, out, inc = [], [], []
    for a in range(len(ins)):
        local.append(pltpu.make_async_copy(ins[a].at[my_chip], outs[a].at[my_chip], local_sems.at[a]))
        for k, (ox, oy) in enumerate(_other_chips(x, y)):
            out.append(_remote(ins[a].at[2 * ox + oy], outs[a].at[my_chip], send_sems, recv_sems, 3 * a + k, (ox, oy, c)))
            inc.append(_remote(ins[a].at[2 * ox + oy], outs[a].at[2 * ox + oy], send_sems, recv_sems, 3 * a + k,
                               (ox, oy, c)))
    return local, out, inc


def _comm_start(exchange):
    local, out, _ = exchange
    for cp in local + out:
        cp.start()


def _comm_wait(exchange):
    local, out, inc = exchange
    for cp in inc:
        cp.wait_recv()
    for cp in out:
        cp.wait_send()
    for cp in local:
        cp.wait()


def _dma_sems(*counts):
    return [pltpu.SemaphoreType.DMA((n,)) for n in counts]


def _pair_sum(grads, recvs):
    n = len(grads)

    def body(c_ref, *refs):
        for a in range(n):
            refs[2 * n + a][...] = (refs[a][...].astype(F32) + refs[n + a][...].astype(F32)).astype(BF16)

    def g_spec(g):
        return pl.BlockSpec((None, None) + g.shape[2:], lambda j, c_ref: (j, c_ref[0], 0, 0))

    def r_spec(r):
        return pl.BlockSpec((None,) + r.shape[1:], lambda j, c_ref: (j, 0, 0))

    return pl.pallas_call(
        body, name="pair_sum",
        grid_spec=pltpu.PrefetchScalarGridSpec(
            num_scalar_prefetch=1, grid=(N_CHIP,),
            in_specs=[g_spec(g) for g in grads] + [r_spec(r) for r in recvs],
            out_specs=[r_spec(r) for r in recvs]),
        out_shape=[jax.ShapeDtypeStruct(r.shape, BF16) for r in recvs],
        compiler_params=_params(("parallel",)),
    )(lax.axis_index("c").reshape(1).astype(jnp.int32), *grads, *recvs)


SMALL_ROWS = 544


def _all_reduce_small(buf, parts=()):
    n = len(parts)

    def body(in_ref, *rest):
        p_in, out_ref, p_out = rest[:n], rest[n], rest[n + 1:2 * n + 1]
        recv, acc, send1, recv1, send2, recv2 = rest[2 * n + 1:2 * n + 7]
        scatter_sems = rest[2 * n + 7:]
        x, y, c = _place()
        me = 4 * x + 2 * y + c
        peers = [(x ^ (r >> 2), y ^ ((r >> 1) & 1), c ^ (r & 1)) for r in range(1, N_DEV)]

        def idx(p):
            return 4 * p[0] + 2 * p[1] + p[2]

        if n:
            _comm_start(_rs_second(p_in, p_out, *scatter_sems))
        first = [pltpu.make_async_remote_copy(
            src_ref=in_ref.at[idx(p)], dst_ref=recv.at[me], send_sem=send1.at[r], recv_sem=recv1.at[r],
            device_id=p, device_id_type=MESH) for r, p in enumerate(peers)]
        for cp in first:
            cp.start()
        recv[me] = in_ref[me]
        for r, p in enumerate(peers):
            pltpu.make_async_remote_copy(
                src_ref=in_ref.at[idx(p)], dst_ref=recv.at[idx(p)], send_sem=send1.at[r], recv_sem=recv1.at[r],
                device_id=p, device_id_type=MESH).wait_recv()
        total = recv[0]
        for k in range(1, N_DEV):
            total = total + recv[k]
        acc[...] = total
        out_ref[me] = total
        second = [pltpu.make_async_remote_copy(
            src_ref=acc, dst_ref=out_ref.at[me], send_sem=send2.at[r], recv_sem=recv2.at[r],
            device_id=p, device_id_type=MESH) for r, p in enumerate(peers)]
        for cp in second:
            cp.start()
        for r, p in enumerate(peers):
            pltpu.make_async_remote_copy(
                src_ref=acc, dst_ref=out_ref.at[idx(p)], send_sem=send2.at[r], recv_sem=recv2.at[r],
                device_id=p, device_id_type=MESH).wait_recv()
        for cp in first + second:
            cp.wait_send()
        if n:
            _comm_wait(_rs_second(p_in, p_out, *scatter_sems))

    vm = pl.BlockSpec(memory_space=pltpu.VMEM)
    res = pl.pallas_call(
        body, name="small_grads_all_reduce",
        in_specs=[vm] + [ANY] * n, out_specs=[vm] + [ANY] * n,
        out_shape=[jax.ShapeDtypeStruct(buf.shape, F32)] + [jax.ShapeDtypeStruct(p.shape, p.dtype) for p in parts],
        scratch_shapes=[pltpu.VMEM(buf.shape, F32), pltpu.VMEM(buf.shape[1:], F32)] + _dma_sems(7, 7, 7, 7)
                       + (_dma_sems(3 * n, 3 * n, n) if n else []),
        compiler_params=pltpu.CompilerParams(has_side_effects=True, vmem_limit_bytes=VMEM_LIMIT),
    )(buf, *parts)
    return res[0], list(res[1:])


def _dilate(a, d):
    if d == 1:
        return a
    S, C = a.shape
    return a.reshape(S // d, d, C).transpose(1, 0, 2).reshape(S, C)


def _undilate(a, d):
    if d == 1:
        return a
    S, C = a.shape
    return a.reshape(d, S // d, C).transpose(1, 0, 2).reshape(S, C)


def _cols(a, cb, n=1):
    return a[:, cb * WIDTH:(cb + n) * WIDTH]


def _to_blocks(g, kind):
    if kind == "rows":
        C = g.shape[1]
        return g.reshape(N_CHIP, 2, -1, C)
    return g.reshape(N_CHIP, 2, 4 * WIDTH, -1)


SMALL = ("norm_g", "gm_ln_g", "gm_ln_b", "gm_ws", "gm_bs", "pool_w", "pool_scale", "mem_norm_g", "final_norm_g")


def _pack_small(tree):
    flat = jnp.concatenate([tree[k].reshape(-1, 128) for k in SMALL], axis=0)
    return jnp.pad(flat, ((0, N_DEV * SMALL_ROWS - flat.shape[0]), (0, 0)))


def _unpack_small(flat, like):
    out, at = {}, 0
    for k in SMALL:
        rows = like[k].size // 128
        out[k] = flat[at:at + rows].reshape(like[k].shape)
        at += rows
    return out


def _make_layer(wt, wkv, wb, wout, norm_g, mem_norm_g, ln_g, ln_b, gm_ws, gm_bs, pool_w, pool_scale):
    tril = jnp.tril(jnp.ones((CHUNK, CHUNK), bool))
    wsm = jnp.where(tril, gm_ws, 0.0).astype(BF16)
    pw = pool_w.astype(BF16)
    bands, bands_t = _band_constants()
    return dict(wt=wt, wkv=wkv, wb=wb, wout=wout, g=norm_g[None], mg=mem_norm_g[None],
                ln_g=ln_g[None],
                ln_b=ln_b[None], wsm=wsm, wsm_t=wsm.transpose(0, 2, 1), pw=pw, pw_t=pw.transpose(0, 2, 1),
                ps=pool_scale[None], bias=jnp.repeat(gm_bs.T, HEAD, axis=1), bands=bands, bands_t=bands_t)


def _layer_fwd(xl, mem0, L, next_shards=()):
    S = xl.shape[0]
    proj, gates, h, half_win = _in_proj(xl, L["g"], L["wt"], next_shards[:1])
    kv, mem_n = _mem_kv(mem0, L["mg"], L["wkv"])
    y4, cdf, win, half_small = _abm_fwd(proj, L["ln_g"], L["ln_b"], L["wsm"], L["bias"], L["pw"], L["ps"], kv,
                                        L["bands"], half_win, next_shards[2:])
    o_g, l_g = [], []
    for gi, d in enumerate(DILATIONS):
        if d == 1:
            o, lse = _attn_fwd(proj, CB_Q0, proj, CB_K, proj, CB_CV, S // CHUNK)
        else:
            o, lse = _attn_fwd_dilated(proj, CB_Q0 + gi, CB_K, CB_CV, d)
        o_g.append(o)
        l_g.append(lse)
    (xn, y4, oc, lse, z), small = _merge_fwd(xl, y4, o_g, l_g, proj, gates, L["wb"], L["wout"], next_shards[1:2],
                                             half_small)
    saved = dict(x=xl, proj=proj, gates=gates, h=h, kv=kv, mem_n=mem_n, y4=y4, cdf=cdf, oc=oc, lse=lse, z=z)
    return xn, saved, win + small


def _place_cols(dst, piece, cb):
    return lax.dynamic_update_slice(dst, piece, (0, cb * WIDTH))


def _layer_bwd(dx, mem0, L, sv, later=(), last=False):
    S = dx.shape[0]
    proj = sv["proj"]
    (dy3, doc, delta, dpb, dgm, dwb, dwout), from_sibling = _merge_bwd(
        dx, sv["y4"], sv["oc"], sv["z"], proj, sv["gates"], L["wb"], L["wout"], later)
    pair = _pair_sum(later, from_sibling) if later else ()
    dpb, dm, dlng, dlnb, dws, dbias, dpw, dps, dkv = _abm_bwd(
        proj, sv["cdf"], dy3, L["ln_g"], L["ln_b"], L["wsm"], L["wsm_t"], L["bias"], L["pw"], L["pw_t"], L["ps"], sv["kv"],
        L["bands"], L["bands_t"], dpb)
    dk, dv = None, None
    for gi, d in enumerate(DILATIONS):
        if d == 1:
            dpb, dk, dv = _attn_bwd(proj, CB_Q0, proj, CB_K, proj, CB_CV, doc, sv["lse"], delta, S // CHUNK,
                                    dpb, CB_Q0)
        else:
            dpb, dk, dv = _attn_bwd_dilated(proj, CB_Q0 + gi, CB_K, CB_CV, doc, sv["lse"], delta, d, dk, dv,
                                            dpb, CB_Q0 + gi)
    dpb = _place_cols(dpb, dk, CB_K)
    dpb = _place_cols(dpb, dv, CB_CV)
    dpb = _place_cols(dpb, dm, CB_MQ)
    dwkv, dmg = _mem_bwd(mem0, L["mg"], sv["mem_n"], L["wkv"], dkv)
    ready = _blocked_rest(dict(w_mem_kv=dwkv, w_branch=dwb, w_out=dwout))
    dwin_t, parts_rest, own = _dw_in(sv["h"], dpb, dgm, pair[1:], ready if last else ())
    big = [_to_blocks(dwin_t, "rows")] + ready
    dxi, dng, parts, from_sibling = _dh_bwd(dpb, dgm, L["wt"], sv["x"], L["g"], dx, pair[:1],
                                            big[:1] if last else ())
    parts = parts + parts_rest
    small = dict(norm_g=dng[0], gm_ln_g=dlng[0], gm_ln_b=dlnb[0], gm_ws=dws,
                 gm_bs=_bias_reduce(dbias)[:, :N_HEAD].T, pool_w=dpw, pool_scale=dps[0], mem_norm_g=dmg[0])
    return dxi, big, small, parts, own, from_sibling


BIG = ("w_in", "w_mem_kv", "w_branch", "w_out")


def _blocked_rest(big):
    return [_to_blocks(big["w_mem_kv"], "rows"), _to_blocks(big["w_branch"], "branch"), _to_blocks(big["w_out"], "rows")]


def _full_weights(gathered):
    win_t, wkv, wb, wout = gathered
    return win_t.reshape(D_IN, D_MODEL), wkv.reshape(D_MODEL, 2 * WIDTH), wb, wout.reshape(D_MODEL, D_MODEL)


def kernel(x, mem, norm_g, w_in, gm_ln_g, gm_ln_b, gm_ws, gm_bs, pool_w, pool_scale, mem_norm_g, w_mem_kv, w_branch, w_out, final_norm_g, loss_target, m_norm_g, m_w_in, m_gm_ln_g, m_gm_ln_b, m_gm_ws, m_gm_bs, m_pool_w, m_pool_scale, m_mem_norm_g, m_w_mem_kv, m_w_branch, m_w_out, m_final_norm_g, v_norm_g, v_w_in, v_gm_ln_g, v_gm_ln_b, v_gm_ws, v_gm_bs, v_pool_w, v_pool_scale, v_mem_norm_g, v_w_mem_kv, v_w_branch, v_w_out, v_final_norm_g):
    x0 = x[0]
    mem0 = mem[0]
    tgt = loss_target[0]
    S = x0.shape[0]

    shards = [[w_in[l].T.astype(BF16), w_mem_kv[l].astype(BF16), w_branch[l].astype(BF16).reshape(4 * WIDTH, -1),
               w_out[l].astype(BF16)] for l in range(DEPTH)]
    gathered = _all_gather(shards[0])
    layers, saved = [], []
    xl = x0
    for l in range(DEPTH):
        layers.append(_make_layer(*_full_weights(gathered), norm_g[l], mem_norm_g[l], gm_ln_g[l], gm_ln_b[l],
                                  gm_ws[l], gm_bs[l], pool_w[l], pool_scale[l]))
        xl, sv, gathered = _layer_fwd(xl, mem0, layers[l], shards[l + 1] if l + 1 < DEPTH else ())
        saved.append(sv)

    loss_part, dx, d_final = _loss_head(xl, final_norm_g[None], tgt)
    loss = lax.psum(loss_part[0, 0], ("x", "y", "c"))

    small = {k: [None] * DEPTH for k in SMALL if k != "final_norm_g"}
    parts = [None] * DEPTH
    later = ()
    for l in reversed(range(DEPTH)):
        dx, gb, gs, done, own, from_sibling = _layer_bwd(dx, mem0, layers[l], saved[l], later, last=(l == 0))
        if later:
            parts[l + 1] = done
        later = gb
        for k in gs:
            small[k][l] = gs[k]
    grad_x = dx[None]
    small_tree = {k: jnp.stack(small[k]) for k in small}
    small_tree["final_norm_g"] = d_final[0]
    reduced, win_parts = _all_reduce_small(_pack_small(small_tree).reshape(N_DEV, SMALL_ROWS, 128),
                                           _pair_sum(later[:1], from_sibling))
    parts[0] = win_parts + own

    weights = dict(norm_g=norm_g, w_in=w_in, gm_ln_g=gm_ln_g, gm_ln_b=gm_ln_b, gm_ws=gm_ws, gm_bs=gm_bs,
                   pool_w=pool_w, pool_scale=pool_scale, mem_norm_g=mem_norm_g, w_mem_kv=w_mem_kv,
                   w_branch=w_branch, w_out=w_out, final_norm_g=final_norm_g)
    m_in = dict(norm_g=m_norm_g, w_in=m_w_in, gm_ln_g=m_gm_ln_g, gm_ln_b=m_gm_ln_b, gm_ws=m_gm_ws, gm_bs=m_gm_bs,
                pool_w=m_pool_w, pool_scale=m_pool_scale, mem_norm_g=m_mem_norm_g, w_mem_kv=m_w_mem_kv,
                w_branch=m_w_branch, w_out=m_w_out, final_norm_g=m_final_norm_g)
    v_in = dict(norm_g=v_norm_g, w_in=v_w_in, gm_ln_g=v_gm_ln_g, gm_ln_b=v_gm_ln_b, gm_ws=v_gm_ws, gm_bs=v_gm_bs,
                pool_w=v_pool_w, pool_scale=v_pool_scale, mem_norm_g=v_mem_norm_g, w_mem_kv=v_w_mem_kv,
                w_branch=v_w_branch, w_out=v_w_out, final_norm_g=v_final_norm_g)
    res = {}
    def view(k, arr):
        return arr.transpose(0, 2, 1) if k == "w_in" else arr

    for a, k in enumerate(BIG):
        shape = view(k, weights[k]).shape
        by_layer = [parts[l][a] for l in range(DEPTH)]
        lrc = (DEPTH,) + by_layer[0].shape[1:]
        outs = _adamw_layers(by_layer, view(k, weights[k]).reshape(lrc), view(k, m_in[k]).reshape(lrc),
                             view(k, v_in[k]).reshape(lrc), "adamw_" + k)
        res[k] = [view(k, o.reshape(shape)) for o in outs]
    outs = _adamw(reduced.reshape(1, N_DEV * SMALL_ROWS, 128), _pack_small(weights), _pack_small(m_in),
                  _pack_small(v_in), "adamw_small")
    unpacked = [_unpack_small(o, weights) for o in outs]
    for k in SMALL:
        res[k] = [u[k] for u in unpacked]

    order = ("norm_g", "w_in", "gm_ln_g", "gm_ln_b", "gm_ws", "gm_bs", "pool_w", "pool_scale", "mem_norm_g",
             "w_mem_kv", "w_branch", "w_out", "final_norm_g")
    return (loss, grad_x, *[res[k][0] for k in order], *[res[k][1] for k in order],
            *[res[k][2] for k in order], *[res[k][3] for k in order])
```

```python
import functools
import math

import numpy as np

import jax
import jax.numpy as jnp
from jax import lax
from jax.experimental import pallas as pl
from jax.experimental.pallas import tpu as pltpu

F32 = jnp.float32
BF16 = jnp.bfloat16

D_MODEL = 1024
DEPTH = 4
WIDTH = 512
D_IN = 10752
HEAD = 128
N_HEAD = 4
CHUNK = 128
MEM_LEN = 256
POOL_WINDOWS = (2, 4, 8, 16)
DILATIONS = (1, 4, 16)
EPS = 1e-6
NEG = -1e30
ATT_SCALE = HEAD ** -0.5
N_DEV = 8
N_CHIP = 4

D_BRANCHES = 6656
D_GATES = D_IN - D_BRANCHES
CB_U, CB_V, CB_AGATE, CB_PIN, CB_PGATE = 0, 1, 2, 3, 4
CB_Q0, CB_K, CB_CV, CB_CGATE, CB_MQ, CB_MGATE = 5, 8, 9, 10, 11, 12

ADAM_LR = 0.001
ADAM_B1 = 0.9
ADAM_B2 = 0.999
ADAM_EPS = 1e-08
ADAM_WD = 0.01
ADAM_STEP = 10

VMEM_LIMIT = 56 * 1024 * 1024
MESH = pl.DeviceIdType.MESH
ANY = pl.BlockSpec(memory_space=pl.ANY)

NT = (((1,), (1,)), ((), ()))
TN = (((0,), (0,)), ((), ()))


def _dot(a, b):
    return jnp.dot(a, b, preferred_element_type=F32)


def _dot_nt(a, b):
    return lax.dot_general(a, b, NT, preferred_element_type=F32)


def _dot_tn(a, b):
    return lax.dot_general(a, b, TN, preferred_element_type=F32)


def _sigmoid(x):
    return 0.5 * jnp.tanh(0.5 * x) + 0.5


def _silu(x):
    return x * _sigmoid(x)


def _silu_and_grad(x):
    s = _sigmoid(x)
    return x * s, s * (1.0 + x * (1.0 - s))


def _normal_cdf(x):
    return 0.5 * (1.0 + lax.erf(x * (2.0 ** -0.5)))


def _gelu_and_grad(x, cdf):
    return x * cdf, cdf + x * jnp.exp(-0.5 * x * x) * (1.0 / math.sqrt(2.0 * math.pi))


def _col(blk, h):
    lane = lax.broadcasted_iota(jnp.int32, blk.shape, 1)
    return jnp.sum(jnp.where(lane == h, blk, 0.0), axis=1, keepdims=True)


def _put_cols(cols):
    rows = cols[0].shape[0]
    lane = lax.broadcasted_iota(jnp.int32, (rows, 128), 1)
    out = jnp.zeros((rows, 128), F32)
    for h, cv in enumerate(cols):
        out = jnp.where(lane == h, cv, out)
    return out


def _params(sem, vmem=VMEM_LIMIT):
    return pltpu.CompilerParams(dimension_semantics=sem, vmem_limit_bytes=vmem)


def _full(shape):
    nd = len(shape)
    return pl.BlockSpec(shape, lambda *_: (0,) * nd)


def _resident(shape):
    nd = len(shape)
    return pl.BlockSpec(shape, lambda *_: (0,) * nd, pipeline_mode=pl.Buffered(1))


def _rows(tm, width, cb=0):
    return pl.BlockSpec((tm, width), lambda i: (i, cb))


GATE_TILE = 512


def _in_proj(x, g, wt, shards=()):
    S = x.shape[0]
    tm, tnb, tng = 1024, D_BRANCHES // 4, D_GATES // 4
    njb, njg = 4, 4
    n = len(shards)
    ni, nj = S // tm, njb + njg

    def body(x_ref, g_ref, wbr_ref, wg0_ref, wg1_ref, *rest):
        ins, (proj_ref, gates_ref, h_ref), outs = rest[:n], rest[n:n + 3], rest[n + 3:2 * n + 3]
        hs, sems = rest[2 * n + 3], rest[2 * n + 4:]
        i, j = pl.program_id(0), pl.program_id(1)

        if n:
            @pl.when(jnp.logical_and(i == 0, j == 0))
            def _():
                _comm_start(_ag_first(ins, outs, *sems))

        @pl.when(j == 0)
        def _():
            xf = x_ref[...]
            r = lax.rsqrt(jnp.mean(xf * xf, axis=-1, keepdims=True) + EPS)
            h = (xf * r * g_ref[...]).astype(BF16)
            hs[...] = h
            h_ref[...] = h

        @pl.when(j < njb)
        def _():
            proj_ref[...] = _dot_nt(hs[...], wbr_ref[...]).astype(BF16)

        @pl.when(j >= njb)
        def _():
            gates_ref[:, :tng // 2] = _dot_nt(hs[...], wg0_ref[...]).astype(BF16)
            gates_ref[:, tng // 2:] = _dot_nt(hs[...], wg1_ref[...]).astype(BF16)

        if n:
            @pl.when(jnp.logical_and(i == ni - 1, j == nj - 1))
            def _():
                _comm_wait(_ag_first(ins, outs, *sems))

    def first(j):
        return jnp.minimum(j, njb - 1)

    def second(j):
        return jnp.maximum(j - njb, 0)

    def gate_rows(half):
        return pl.BlockSpec((tng // 2, D_MODEL), lambda i, j: (D_BRANCHES // (tng // 2) + 2 * second(j) + half, 0))

    res = pl.pallas_call(
        body, name="in_proj_gather" if n else "in_proj",
        grid=(ni, nj),
        in_specs=[pl.BlockSpec((tm, D_MODEL), lambda i, j: (i, 0)),
                  pl.BlockSpec((1, D_MODEL), lambda i, j: (0, 0)),
                  pl.BlockSpec((tnb, D_MODEL), lambda i, j: (first(j), 0)), gate_rows(0), gate_rows(1)]
                 + [ANY] * n,
        out_specs=[pl.BlockSpec((tm, tnb), lambda i, j: (i, first(j))),
                   pl.BlockSpec((tm, tng), lambda i, j: (i, second(j))),
                   pl.BlockSpec((tm, D_MODEL), lambda i, j: (i, 0))] + [ANY] * n,
        out_shape=[jax.ShapeDtypeStruct((S, D_BRANCHES), BF16), jax.ShapeDtypeStruct((S, D_GATES), BF16),
                   jax.ShapeDtypeStruct((S, D_MODEL), BF16)]
                  + [jax.ShapeDtypeStruct((N_DEV,) + s.shape, s.dtype) for s in shards],
        scratch_shapes=[pltpu.VMEM((tm, D_MODEL), BF16)] + (_dma_sems(4 * n, 4 * n, n) if n else []),
        compiler_params=_params(("arbitrary", "arbitrary")),
    )(x, g, wt, wt, wt, *shards)
    return res[0], res[1], res[2], list(res[3:])


def _mem_kv(mem, g, w):
    M = mem.shape[0]

    def body(m_ref, g_ref, w_ref, kv_ref, mn_ref):
        xf = m_ref[...]
        r = lax.rsqrt(jnp.mean(xf * xf, axis=-1, keepdims=True) + EPS)
        mn = (xf * r * g_ref[...]).astype(BF16)
        mn_ref[...] = mn
        kv_ref[...] = _dot(mn, w_ref[...]).astype(BF16)

    return pl.pallas_call(
        body, name="mem_kv",
        out_shape=[jax.ShapeDtypeStruct((M, 2 * WIDTH), BF16), jax.ShapeDtypeStruct((M, D_MODEL), BF16)],
        compiler_params=pltpu.CompilerParams(vmem_limit_bytes=VMEM_LIMIT),
    )(mem, g, w)


def _band_constants():
    t = np.arange(CHUNK)[:, None]
    s = np.arange(CHUNK)[None, :]
    bands = np.stack([np.stack([(t - s >= 0) & (t - s < win), s > t + CHUNK - win]) for win in POOL_WINDOWS])
    bands = bands.astype(np.float32)
    return jnp.asarray(bands, BF16), jnp.asarray(bands.transpose(0, 1, 3, 2), BF16)


def _inv_count(first_row, win):
    t = first_row + lax.broadcasted_iota(jnp.int32, (CHUNK, 1), 0)
    return 1.0 / jnp.minimum(t + 1, win).astype(F32)


def _layer_norm_fwd(v):
    mu = jnp.mean(v, axis=-1, keepdims=True)
    vc = v - mu
    var = jnp.mean(vc * vc, axis=-1, keepdims=True)
    rstd = lax.rsqrt(var + EPS)
    return vc * rstd, rstd


def _mem_softmax(q, kmem):
    s = _dot_nt(q, kmem) * ATT_SCALE
    m = jnp.max(s, axis=-1, keepdims=True)
    e = jnp.exp(s - m)
    return e * (1.0 / jnp.sum(e, axis=-1, keepdims=True))


def _abm_fwd(proj, ln_g, ln_b, wsm, bias_full, pool_w, pool_scale, kv, bands, gathered=(), shards=()):
    S = proj.shape[0]
    tm = 512
    nchunk = tm // CHUNK
    n, m = len(gathered), len(shards)
    nsteps = S // tm

    def body(u_ref, v_ref, ag_ref, p_ref, ph_ref, pg_ref, mq_ref, mg_ref, lng_ref, lnb_ref, wsm_ref, bias_ref,
             pw_ref, ps_ref, kv_ref, band_ref, *rest):
        s_in = rest[n:n + m]
        y_ref, cdf_ref = rest[n + m], rest[n + m + 1]
        bufs, s_out = rest[n + m + 2:2 * n + m + 2], rest[2 * n + m + 2:2 * (n + m) + 2]
        mix, sems = rest[2 * (n + m) + 2], rest[2 * (n + m) + 3:]
        second_sems, first_sems = (sems[:2] if n else ()), sems[2 if n else 0:]
        i = pl.program_id(0)

        if n or m:
            @pl.when(i == 0)
            def _():
                if n:
                    _comm_start(_ag_second(bufs, *second_sems))
                if m:
                    _comm_start(_ag_first(s_in, s_out, *first_sems))

        au, av = u_ref[...].astype(F32), v_ref[...].astype(F32)
        cdf_u, cdf_v = _normal_cdf(au), _normal_cdf(av)
        cdf_ref[0] = cdf_u.astype(BF16)
        cdf_ref[1] = cdf_v.astype(BF16)
        u, v = au * cdf_u, av * cdf_v
        vhat, _ = _layer_norm_fwd(v)
        vln = (vhat * lng_ref[...] + lnb_ref[...]).astype(BF16)
        for c in range(nchunk):
            for h in range(N_HEAD):
                rs, cs = slice(c * CHUNK, (c + 1) * CHUNK), slice(h * HEAD, (h + 1) * HEAD)
                mix[rs, cs] = _dot(wsm_ref[h], vln[rs, cs]) + bias_ref[:, cs]
        y_ref[0] = (u * mix[...] * _silu(ag_ref[...].astype(F32))).astype(BF16)
        halo_ok = (i > 0).astype(F32)
        for c in range(nchunk):
            rs = slice(c * CHUNK, (c + 1) * CHUNK)
            for g, win in enumerate(POOL_WINDOWS):
                cs = slice(g * HEAD, (g + 1) * HEAD)
                cur = p_ref[rs, cs]
                if c == 0:
                    prev = (ph_ref[:, cs].astype(F32) * halo_ok).astype(BF16)
                else:
                    prev = p_ref[(c - 1) * CHUNK:c * CHUNK, cs]
                sums = _dot(band_ref[g, 0], cur) + _dot(band_ref[g, 1], prev)
                dm = sums * _inv_count(i * tm + c * CHUNK, win) - cur.astype(F32)
                mix[rs, cs] = _dot(dm.astype(BF16), pw_ref[g])
        y_ref[1] = (mix[...] * ps_ref[...] * _silu(pg_ref[...].astype(F32))).astype(BF16)
        for h in range(N_HEAD):
            cs = slice(h * HEAD, (h + 1) * HEAD)
            p = _mem_softmax(mq_ref[:, cs], kv_ref[:, cs])
            mix[:, cs] = _dot(p.astype(BF16), kv_ref[:, WIDTH + h * HEAD:WIDTH + (h + 1) * HEAD])
        y_ref[2] = (mix[...] * _silu(mg_ref[...].astype(F32))).astype(BF16)

        if n or m:
            @pl.when(i == nsteps - 1)
            def _():
                if n:
                    _comm_wait(_ag_second(bufs, *second_sems))
                if m:
                    _comm_wait(_ag_first(s_in, s_out, *first_sems))

    blk = tm // CHUNK
    res = pl.pallas_call(
        body, name="abm_fwd_gather" if n or m else "abm_fwd",
        grid=(nsteps,),
        in_specs=[_rows(tm, WIDTH, CB_U), _rows(tm, WIDTH, CB_V), _rows(tm, WIDTH, CB_AGATE),
                  _rows(tm, WIDTH, CB_PIN),
                  pl.BlockSpec((CHUNK, WIDTH), lambda i: (jnp.maximum(i * blk - 1, 0), CB_PIN)),
                  _rows(tm, WIDTH, CB_PGATE), _rows(tm, WIDTH, CB_MQ), _rows(tm, WIDTH, CB_MGATE),
                  _full((1, WIDTH)), _full((1, WIDTH)), _full((N_HEAD, CHUNK, CHUNK)), _full((CHUNK, WIDTH)),
                  _full((4, HEAD, HEAD)), _full((1, WIDTH)), _full((MEM_LEN, 2 * WIDTH)),
                  _full((4, 2, CHUNK, CHUNK))] + [ANY] * (n + m),
        out_specs=[pl.BlockSpec((3, tm, WIDTH), lambda i: (0, i, 0)), pl.BlockSpec((2, tm, WIDTH), lambda i: (0, i, 0))]
                  + [ANY] * (n + m),
        out_shape=[jax.ShapeDtypeStruct((4, S, WIDTH), BF16),
                   jax.ShapeDtypeStruct((2, S, WIDTH), BF16)]
                  + [jax.ShapeDtypeStruct(b.shape, b.dtype) for b in gathered]
                  + [jax.ShapeDtypeStruct((N_DEV,) + s.shape, s.dtype) for s in shards],
        input_output_aliases={16 + a: 2 + a for a in range(n)},
        scratch_shapes=[pltpu.VMEM((tm, WIDTH), F32)] + (_dma_sems(3 * n, 3 * n) if n else [])
                       + (_dma_sems(4 * m, 4 * m, m) if m else []),
        compiler_params=_params(("arbitrary",)),
    )(proj, proj, proj, proj, proj, proj, proj, proj, ln_g, ln_b, wsm, bias_full, pool_w, pool_scale, kv, bands,
      *gathered, *shards)
    return res[0], res[1], list(res[2:2 + n]), list(res[2 + n:])


ATT_TILE = 512


def _attn_fwd(q, qcb, k, kcb, v, vcb, bps):
    S = q.shape[0]
    tm = ATT_TILE
    nb = tm // CHUNK

    nblocks = nb * N_HEAD

    def body(q_ref, k_ref, v_ref, kh_ref, vh_ref, o_ref, l_ref, sc_s, sp_s, pc_s, pp_s):
        i = pl.program_id(0)

        def prev_kv(n, cs):
            if n == 0:
                return kh_ref[:, cs], vh_ref[:, cs]
            ps = slice((n - 1) * CHUNK, n * CHUNK)
            return k_ref[ps, cs], v_ref[ps, cs]

        pens = []
        for n in range(nb):
            rs = slice(n * CHUNK, (n + 1) * CHUNK)
            pens.append(jnp.full((N_HEAD * CHUNK, 1), jnp.where((i * nb + n) % bps != 0, 0.0, NEG), F32))
            for h in range(N_HEAD):
                cs = slice(h * HEAD, (h + 1) * HEAD)
                bs = slice((n * N_HEAD + h) * CHUNK, (n * N_HEAD + h + 1) * CHUNK)
                qh = q_ref[rs, cs]
                sc_s[bs, :] = _dot_nt(qh, k_ref[rs, cs])
                sp_s[bs, :] = _dot_nt(qh, prev_kv(n, cs)[0])
        row = lax.broadcasted_iota(jnp.int32, (nblocks * CHUNK, CHUNK), 0) & (CHUNK - 1)
        col = lax.broadcasted_iota(jnp.int32, (nblocks * CHUNK, CHUNK), 1)
        sc = jnp.where(col <= row, sc_s[...] * ATT_SCALE, NEG)
        sp = jnp.where(col >= row, sp_s[...] * ATT_SCALE, NEG) + jnp.concatenate(pens, axis=0)
        m = jnp.maximum(jnp.max(sc, axis=-1, keepdims=True), jnp.max(sp, axis=-1, keepdims=True))
        ec = jnp.exp(sc - m)
        ep = jnp.exp(sp - m)
        den = jnp.sum(ec, axis=-1, keepdims=True) + jnp.sum(ep, axis=-1, keepdims=True)
        inv = 1.0 / den
        pc_s[...] = (ec * inv).astype(BF16)
        pp_s[...] = (ep * inv).astype(BF16)
        lse = m + jnp.log(den)
        for n in range(nb):
            rs = slice(n * CHUNK, (n + 1) * CHUNK)
            for h in range(N_HEAD):
                cs = slice(h * HEAD, (h + 1) * HEAD)
                bs = slice((n * N_HEAD + h) * CHUNK, (n * N_HEAD + h + 1) * CHUNK)
                o = _dot(pc_s[bs, :], v_ref[rs, cs]) + _dot(pp_s[bs, :], prev_kv(n, cs)[1])
                o_ref[rs, cs] = o.astype(BF16)
            l_ref[rs, :] = _put_cols([lse[(n * N_HEAD + h) * CHUNK:(n * N_HEAD + h + 1) * CHUNK]
                                      for h in range(N_HEAD)])

    def halo(cb):
        return pl.BlockSpec((CHUNK, WIDTH), lambda i: (jnp.maximum(i * nb - 1, 0), cb))

    return pl.pallas_call(
        body, name=f"attn_fwd_{bps}",
        grid=(S // tm,),
        in_specs=[_rows(tm, WIDTH, qcb), _rows(tm, WIDTH, kcb), _rows(tm, WIDTH, vcb), halo(kcb), halo(vcb)],
        out_specs=[_rows(tm, WIDTH), _rows(tm, 128)],
        out_shape=[jax.ShapeDtypeStruct((S, WIDTH), BF16), jax.ShapeDtypeStruct((S, 128), F32)],
        scratch_shapes=[pltpu.VMEM((nblocks * CHUNK, CHUNK), F32), pltpu.VMEM((nblocks * CHUNK, CHUNK), F32),
                        pltpu.VMEM((nblocks * CHUNK, CHUNK), BF16), pltpu.VMEM((nblocks * CHUNK, CHUNK), BF16)],
        compiler_params=_params(("parallel",)),
    )(q, k, v, k, v)


def _gate_specs(tm):
    return [pl.BlockSpec((tm, D_MODEL), lambda i, b=b: (i, b)) for b in range(4)]


Y_SLOT = (0, 1, 3, 2)


def _w_branch_from_blocks(src_ref, dst):
    cols = D_MODEL // N_DEV
    for k in range(N_DEV):
        for b in range(4):
            dst[b, :, k * cols:(k + 1) * cols] = src_ref[k, b * WIDTH:(b + 1) * WIDTH, :]


def _merge_fwd(x, y4, o_g, l_g, proj, gates, wb, wout, shards=(), half=()):
    S = x.shape[0]
    tm = 512
    n, nh = len(shards), len(half)
    nsteps = S // tm
    forward_at = nsteps // 2

    def body(x_ref, y_ref, o0, o1, o2, l0, l1, l2, cg_ref, *rest):
        gm = rest[:4]
        wb_ref, wo_ref = rest[4:6]
        s_in = rest[6:6 + n]
        rest = rest[6 + n + nh:]
        xn_ref, yc_ref, oc_ref, lse_ref, z_ref = rest[:5]
        s_out, bufs = rest[5:5 + n], rest[5:5 + n + nh]
        ocs, wbs, sems = rest[5 + n + nh], rest[6 + n + nh], rest[7 + n + nh:]
        i = pl.program_id(0)

        @pl.when(i == 0)
        def _():
            if n:
                _comm_start(_ag_first(s_in, s_out, *sems[:3]))
            _w_branch_from_blocks(wb_ref, wbs)

        if n:
            @pl.when(i == forward_at)
            def _():
                incoming = _ag_first(s_in, s_out, *sems[:3])[2]
                for a in range(n):
                    for k in range(1, 4):
                        incoming[4 * a + k].wait_recv()
                _comm_start(_ag_second(bufs, *sems[3:]))

        lcols = []
        for h in range(N_HEAD):
            cs = slice(h * HEAD, (h + 1) * HEAD)
            ls = [_col(l[...], h) for l in (l0, l1, l2)]
            m = jnp.maximum(jnp.maximum(ls[0], ls[1]), ls[2])
            tot = jnp.exp(ls[0] - m) + jnp.exp(ls[1] - m) + jnp.exp(ls[2] - m)
            lse = m + jnp.log(tot)
            ocs[:, cs] = sum(jnp.exp(lg - lse) * o[:, cs].astype(F32) for lg, o in zip(ls, (o0, o1, o2)))
            lcols.append(lse)
        lse_ref[...] = _put_cols(lcols)
        oc = ocs[...]
        oc_ref[...] = oc.astype(BF16)
        yc = (oc * _silu(cg_ref[...].astype(F32))).astype(BF16)
        yc_ref[...] = yc
        ys = (y_ref[0], y_ref[1], yc, y_ref[2])
        z = jnp.zeros((tm, D_MODEL), F32)
        for b in range(4):
            z = z + _sigmoid(gm[b][...].astype(F32)) * _dot(ys[b], wbs[b])
        zb = z.astype(BF16)
        z_ref[...] = zb
        xn_ref[...] = x_ref[...] + _dot(zb, wo_ref[...])

        if n:
            @pl.when(i == nsteps - 1)
            def _():
                local, out, incoming = _ag_first(s_in, s_out, *sems[:3])
                for a in range(n):
                    incoming[4 * a].wait_recv()
                _comm_wait(_ag_second(bufs, *sems[3:]))
                for cp in out:
                    cp.wait_send()
                for cp in local:
                    cp.wait()

    res = pl.pallas_call(
        body, name="merge_fwd_gather" if n else "merge_fwd",
        grid=(nsteps,),
        in_specs=[_rows(tm, D_MODEL), pl.BlockSpec((3, tm, WIDTH), lambda i: (0, i, 0)),
                  _rows(tm, WIDTH), _rows(tm, WIDTH), _rows(tm, WIDTH),
                  _rows(tm, 128), _rows(tm, 128), _rows(tm, 128),
                  _rows(tm, WIDTH, CB_CGATE)] + _gate_specs(tm)
                 + [_resident(wb.shape), _resident((D_MODEL, D_MODEL))] + [ANY] * (n + nh),
        out_specs=[_rows(tm, D_MODEL), pl.BlockSpec((None, tm, WIDTH), lambda i: (Y_SLOT[2], i, 0)),
                   _rows(tm, WIDTH), _rows(tm, 128), _rows(tm, D_MODEL)] + [ANY] * (n + nh),
        out_shape=[jax.ShapeDtypeStruct((S, D_MODEL), F32), jax.ShapeDtypeStruct(y4.shape, BF16),
                   jax.ShapeDtypeStruct((S, WIDTH), BF16), jax.ShapeDtypeStruct((S, 128), F32),
                   jax.ShapeDtypeStruct((S, D_MODEL), BF16)]
                  + [jax.ShapeDtypeStruct((N_DEV,) + s.shape, s.dtype) for s in shards]
                  + [jax.ShapeDtypeStruct(b.shape, b.dtype) for b in half],
        input_output_aliases={1: 1, **{15 + n + a: 5 + n + a for a in range(nh)}},
        scratch_shapes=[pltpu.VMEM((tm, WIDTH), F32), pltpu.VMEM((4, WIDTH, D_MODEL), BF16)]
                       + (_dma_sems(4 * n, 4 * n, n, 3 * (n + nh), 3 * (n + nh)) if n else []),
        compiler_params=_params(("arbitrary",)),
    )(x, y4, *o_g, *l_g, proj, *([gates] * 4), wb, wout, *shards, *half)
    return res[:5], list(res[5:])


def _loss_head(x, g, tgt):
    S = x.shape[0]
    tm = 512

    def body(x_ref, g_ref, t_ref, loss_ref, dx_ref, dg_ref):
        @pl.when(pl.program_id(0) == 0)
        def _():
            loss_ref[...] = jnp.zeros_like(loss_ref)
            dg_ref[...] = jnp.zeros_like(dg_ref)

        xf = x_ref[...]
        r = lax.rsqrt(jnp.mean(xf * xf, axis=-1, keepdims=True) + EPS)
        xhat = xf * r
        gv = g_ref[...]
        err = xhat * gv - t_ref[...]
        e2 = jnp.sum(err * err, axis=-1, keepdims=True)
        loss_ref[...] += (0.5 / D_MODEL) * jnp.sum(e2, axis=0, keepdims=True)
        dy = err * (1.0 / D_MODEL)
        dg_ref[...] += jnp.sum(dy * xhat, axis=0, keepdims=True)
        dxh = dy * gv
        dx_ref[...] = r * (dxh - xhat * jnp.mean(dxh * xhat, axis=-1, keepdims=True))

    return pl.pallas_call(
        body, name="loss_head",
        grid=(S // tm,),
        in_specs=[_rows(tm, D_MODEL), _full((1, D_MODEL)), _rows(tm, D_MODEL)],
        out_specs=[_full((1, 128)), _rows(tm, D_MODEL), _full((1, D_MODEL))],
        out_shape=[jax.ShapeDtypeStruct((1, 128), F32), jax.ShapeDtypeStruct((S, D_MODEL), F32),
                   jax.ShapeDtypeStruct((1, D_MODEL), F32)],
        compiler_params=_params(("arbitrary",)),
    )(x, g, tgt)


def _merge_bwd(dxo, y4, oc, z, proj, gates, wb, wout, grads=()):
    S = dxo.shape[0]
    tm = 256
    n = len(grads)
    nsteps = S // tm

    def body(dx_ref, y_ref, oc_ref, z_ref, cg_ref, *rest):
        gm = rest[:4]
        wb_ref, wo_ref = rest[4:6]
        g_in = rest[6:6 + n]
        dy_ref, doc_ref, delta_ref, dcg_ref, dgm_ref, dwb_ref, dwo_ref = rest[6 + n:13 + n]
        g_out = rest[13 + n:13 + 2 * n]
        acc_b, acc_o, wbs = rest[13 + 2 * n:16 + 2 * n]
        sems = rest[16 + 2 * n:]
        i = pl.program_id(0)

        @pl.when(i == 0)
        def _():
            acc_b[...] = jnp.zeros_like(acc_b)
            acc_o[...] = jnp.zeros_like(acc_o)
            _w_branch_from_blocks(wb_ref, wbs)
            if n:
                _comm_start(_rs_first(g_in, g_out, *sems))

        dxb = dx_ref[...].astype(BF16)
        acc_o[...] += _dot_tn(z_ref[...], dxb)
        dz = _dot_nt(dxb, wo_ref[...])
        for b in range(4):
            gate = _sigmoid(gm[b][...].astype(F32))
            yb = y_ref[Y_SLOT[b]]
            t = _dot(yb, wbs[b])
            dgm_ref[:, b * D_MODEL:(b + 1) * D_MODEL] = (dz * t * gate * (1.0 - gate)).astype(BF16)
            dt = (dz * gate).astype(BF16)
            acc_b[b] += _dot_tn(yb, dt)
            dyb = _dot_nt(dt, wbs[b])
            if b == 2:
                cg = cg_ref[...].astype(F32)
                oc = oc_ref[...].astype(F32)
                scg, dscg = _silu_and_grad(cg)
                doc = dyb * scg
                dcg_ref[...] = (dyb * oc * dscg).astype(BF16)
                doc_ref[...] = doc.astype(BF16)
                prod = doc * oc
                delta_ref[...] = _put_cols([jnp.sum(prod[:, h * HEAD:(h + 1) * HEAD], axis=1, keepdims=True)
                                            for h in range(N_HEAD)])
            else:
                dy_ref[b if b < 2 else 2] = dyb.astype(BF16)

        @pl.when(i == nsteps - 1)
        def _():
            for k in range(N_DEV):
                dwb_ref[k] = acc_b[:, :, k * (D_MODEL // N_DEV):(k + 1) * (D_MODEL // N_DEV)].astype(BF16)
            dwo_ref[...] = acc_o[...].astype(BF16)
            if n:
                _comm_wait(_rs_first(g_in, g_out, *sems))

    res = pl.pallas_call(
        body, name="merge_bwd_scatter" if n else "merge_bwd",
        grid=(nsteps,),
        in_specs=[_rows(tm, D_MODEL), pl.BlockSpec((4, tm, WIDTH), lambda i: (0, i, 0)),
                  _rows(tm, WIDTH), _rows(tm, D_MODEL), _rows(tm, WIDTH, CB_CGATE)] + _gate_specs(tm)
                 + [_resident(wb.shape), _resident((D_MODEL, D_MODEL))] + [ANY] * n,
        out_specs=[pl.BlockSpec((3, tm, WIDTH), lambda i: (0, i, 0)), _rows(tm, WIDTH), _rows(tm, 128),
                   _rows(tm, WIDTH, CB_CGATE), _rows(tm, 4 * D_MODEL), _full((N_DEV, 4, WIDTH, D_MODEL // N_DEV)),
                   _full((D_MODEL, D_MODEL))]
                  + [ANY] * n,
        out_shape=[jax.ShapeDtypeStruct((3, S, WIDTH), BF16), jax.ShapeDtypeStruct((S, WIDTH), BF16),
                   jax.ShapeDtypeStruct((S, 128), F32), jax.ShapeDtypeStruct((S, D_BRANCHES), BF16),
                   jax.ShapeDtypeStruct((S, 4 * D_MODEL), BF16),
                   jax.ShapeDtypeStruct((N_DEV, 4, WIDTH, D_MODEL // N_DEV), BF16),
                   jax.ShapeDtypeStruct((D_MODEL, D_MODEL), BF16)]
                  + [jax.ShapeDtypeStruct(g.shape[:1] + g.shape[2:], g.dtype) for g in grads],
        scratch_shapes=[pltpu.VMEM((4, WIDTH, D_MODEL), F32), pltpu.VMEM((D_MODEL, D_MODEL), F32),
                        pltpu.VMEM((4, WIDTH, D_MODEL), BF16)]
                       + (_dma_sems(N_CHIP * n, N_CHIP * n) if n else []),
        compiler_params=_params(("arbitrary",)),
    )(dxo, y4, oc, z, proj, *([gates] * 4), wb, wout, *grads)
    return res[:7], list(res[7:])


def _attn_bwd(q, qcb, k, kcb, v, vcb, do, lse, delta, bps, dq_others, dk_in, dv_in, dst):
    assert (qcb, kcb, vcb) == (CB_Q0, CB_K, CB_CV) and CB_Q0 % 5 == 0
    S = q.shape[0]
    tm = ATT_TILE
    nb = tm // CHUNK
    nblk = S // CHUNK

    ncur = nb * N_HEAD
    nprev = (nb + 1) * N_HEAD

    def body(q_ref, k_ref, v_ref, do_ref, l_ref, d_ref, kh_ref, vh_ref, qn_ref, don_ref, ln_ref, dn_ref,
             dq1_ref, dq2_ref, dki_ref, dvi_ref, dst_ref,
             out_ref, sc_s, sp_s, dpc_s, dpp_s, pc_s, pp_s, dsc_s, dsp_s):
        i = pl.program_id(0)
        out_ref[:, WIDTH:2 * WIDTH] = dq1_ref[...]
        out_ref[:, 2 * WIDTH:3 * WIDTH] = dq2_ref[...]

        def rows_of(n):
            if n < nb:
                rs = slice(n * CHUNK, (n + 1) * CHUNK)
                return rs, q_ref, do_ref, l_ref, d_ref
            return slice(0, CHUNK), qn_ref, don_ref, ln_ref, dn_ref

        def prev_kv(n, cs):
            if n == 0:
                return kh_ref[:, cs], vh_ref[:, cs]
            ps = slice((n - 1) * CHUNK, n * CHUNK)
            return k_ref[ps, cs], v_ref[ps, cs]

        def blk(n, h):
            return slice((n * N_HEAD + h) * CHUNK, (n * N_HEAD + h + 1) * CHUNK)

        pens, lses, deltas = [], [], []
        for n in range(nb + 1):
            rs, qr, dor, lr, dr = rows_of(n)
            gb = i * nb + n
            pen = jnp.where(gb % bps != 0, 0.0, NEG)
            if n == nb:
                pen = pen + jnp.where(gb < nblk, 0.0, NEG)
            pens.append(jnp.full((N_HEAD * CHUNK, 1), pen, F32))
            lblk, dblk = lr[rs, :], dr[rs, :]
            for h in range(N_HEAD):
                cs = slice(h * HEAD, (h + 1) * HEAD)
                qh, doh = qr[rs, cs], dor[rs, cs]
                lses.append(_col(lblk, h))
                deltas.append(_col(dblk, h))
                kp, vp = prev_kv(n, cs)
                sp_s[blk(n, h), :] = _dot_nt(qh, kp)
                dpp_s[blk(n, h), :] = _dot_nt(doh, vp)
                if n < nb:
                    sc_s[blk(n, h), :] = _dot_nt(qh, k_ref[rs, cs])
                    dpc_s[blk(n, h), :] = _dot_nt(doh, v_ref[rs, cs])
        lse = jnp.concatenate(lses, axis=0)
        delta = jnp.concatenate(deltas, axis=0)
        row = lax.broadcasted_iota(jnp.int32, (nprev * CHUNK, CHUNK), 0) & (CHUNK - 1)
        col = lax.broadcasted_iota(jnp.int32, (nprev * CHUNK, CHUNK), 1)
        sp = jnp.where(col >= row, sp_s[...] * ATT_SCALE, NEG) + jnp.concatenate(pens, axis=0)
        pp = jnp.exp(sp - lse)
        pp_s[...] = pp.astype(BF16)
        dsp_s[...] = (pp * (dpp_s[...] - delta)).astype(BF16)
        nc = ncur * CHUNK
        sc = jnp.where(col[:nc] <= row[:nc], sc_s[...] * ATT_SCALE, NEG)
        pc = jnp.exp(sc - lse[:nc])
        pc_s[...] = pc.astype(BF16)
        dsc_s[...] = (pc * (dpc_s[...] - delta[:nc])).astype(BF16)
        for n in range(nb):
            rs, qr, dor, _, _ = rows_of(n)
            rn, qnr, donr, _, _ = rows_of(n + 1)
            for h in range(N_HEAD):
                cs = slice(h * HEAD, (h + 1) * HEAD)
                kp, _ = prev_kv(n, cs)
                dq = _dot(dsc_s[blk(n, h), :], k_ref[rs, cs]) + _dot(dsp_s[blk(n, h), :], kp)
                out_ref[rs, cs] = (dq * ATT_SCALE).astype(BF16)
                dk = _dot_tn(dsc_s[blk(n, h), :], qr[rs, cs]) + _dot_tn(dsp_s[blk(n + 1, h), :], qnr[rn, cs])
                kcs = slice(3 * WIDTH + h * HEAD, 3 * WIDTH + (h + 1) * HEAD)
                out_ref[rs, kcs] = (dk * ATT_SCALE + dki_ref[rs, cs].astype(F32)).astype(BF16)
                dv = _dot_tn(pc_s[blk(n, h), :], dor[rs, cs]) + _dot_tn(pp_s[blk(n + 1, h), :], donr[rn, cs])
                vcs = slice(4 * WIDTH + h * HEAD, 4 * WIDTH + (h + 1) * HEAD)
                out_ref[rs, vcs] = (dv + dvi_ref[rs, cs].astype(F32)).astype(BF16)

    def prev_halo(cb):
        return pl.BlockSpec((CHUNK, WIDTH), lambda i: (jnp.maximum(i * nb - 1, 0), cb))

    def next_halo(width, cb=0):
        return pl.BlockSpec((CHUNK, width), lambda i: (jnp.minimum(i * nb + nb, nblk - 1), cb))

    return pl.pallas_call(
        body, name=f"attn_bwd_{bps}",
        grid=(S // tm,),
        in_specs=[_rows(tm, WIDTH, qcb), _rows(tm, WIDTH, kcb), _rows(tm, WIDTH, vcb), _rows(tm, WIDTH),
                  _rows(tm, 128), _rows(tm, 128), prev_halo(kcb), prev_halo(vcb),
                  next_halo(WIDTH, qcb), next_halo(WIDTH), next_halo(128), next_halo(128),
                  _rows(tm, WIDTH), _rows(tm, WIDTH), _rows(tm, WIDTH), _rows(tm, WIDTH), ANY],
        out_specs=_rows(tm, 5 * WIDTH, CB_Q0 // 5),
        out_shape=jax.ShapeDtypeStruct(dst.shape, BF16),
        input_output_aliases={16: 0},
        scratch_shapes=[pltpu.VMEM((ncur * CHUNK, CHUNK), F32), pltpu.VMEM((nprev * CHUNK, CHUNK), F32),
                        pltpu.VMEM((ncur * CHUNK, CHUNK), F32), pltpu.VMEM((nprev * CHUNK, CHUNK), F32),
                        pltpu.VMEM((ncur * CHUNK, CHUNK), BF16), pltpu.VMEM((nprev * CHUNK, CHUNK), BF16),
                        pltpu.VMEM((ncur * CHUNK, CHUNK), BF16), pltpu.VMEM((nprev * CHUNK, CHUNK), BF16)],
        compiler_params=_params(("parallel",)),
    )(q, k, v, do, lse, delta, k, v, q, do, lse, delta, *dq_others, dk_in, dv_in, dst)


def _dilated_split(d):
    hp = min(N_HEAD, 16 // d)
    return hp, N_HEAD // hp, HEAD * hp


def _strided_regroup(d):
    return d < 16


def _by_class(src_ref, dst, d, hp, nat):
    for j in range(hp):
        if _strided_regroup(d):
            nat[j] = src_ref[:, j * HEAD:(j + 1) * HEAD].astype(F32)
            for r in range(d):
                dst[j, r * CHUNK:(r + 1) * CHUNK, :] = nat.at[j][pl.ds(r, CHUNK, stride=d), :].astype(BF16)
        else:
            dst[j] = pltpu.einshape("(tr)l->(rt)l", src_ref[:, j * HEAD:(j + 1) * HEAD], r=d)


def _from_class(src, dst_ref, d, hp, nat, add_ref=None):
    for j in range(hp):
        cs = slice(j * HEAD, (j + 1) * HEAD)
        if _strided_regroup(d):
            for r in range(d):
                nat.at[j][pl.ds(r, CHUNK, stride=d), :] = src[j, r * CHUNK:(r + 1) * CHUNK, :]
            val = nat[j].astype(BF16)
        else:
            val = pltpu.einshape("(rt)l->(tr)l", src[j].astype(BF16), r=d)
        if add_ref is not None:
            val = (val.astype(F32) + add_ref[:, cs].astype(F32)).astype(BF16)
        dst_ref[:, cs] = val


def _attn_fwd_dilated(proj, qcb, kcb, vcb, d):
    S = proj.shape[0]
    T = CHUNK * d
    hp, nh, cw = _dilated_split(d)
    nblocks = d * hp

    def body(q_ref, k_ref, v_ref, o_ref, l_ref, qf, kst, vst, of, lf, nat, sc_s, sp_s, pc_s, pp_s):
        i, hh = pl.program_id(0), pl.program_id(1)
        kf, vf = kst.at[i % 2, hh], vst.at[i % 2, hh]
        kpf, vpf = kst.at[1 - i % 2, hh], vst.at[1 - i % 2, hh]

        @pl.when(i == 0)
        def _():
            kpf[...] = jnp.zeros_like(kpf)
            vpf[...] = jnp.zeros_like(vpf)

        _by_class(q_ref, qf, d, hp, nat)
        _by_class(k_ref, kf, d, hp, nat)
        _by_class(v_ref, vf, d, hp, nat)

        def blk(ref, r, j):
            return ref[j, r * CHUNK:(r + 1) * CHUNK, :]

        def bs(r, j):
            return slice((r * hp + j) * CHUNK, (r * hp + j + 1) * CHUNK)

        for r in range(d):
            for j in range(hp):
                qb = blk(qf, r, j)
                sc_s[bs(r, j), :] = _dot_nt(qb, blk(kf, r, j))
                sp_s[bs(r, j), :] = _dot_nt(qb, blk(kpf, r, j))
        row = lax.broadcasted_iota(jnp.int32, (nblocks * CHUNK, CHUNK), 0) & (CHUNK - 1)
        col = lax.broadcasted_iota(jnp.int32, (nblocks * CHUNK, CHUNK), 1)
        sc = jnp.where(col <= row, sc_s[...] * ATT_SCALE, NEG)
        sp = jnp.where(col >= row, sp_s[...] * ATT_SCALE, NEG) + jnp.where(i > 0, 0.0, NEG)
        m = jnp.maximum(jnp.max(sc, axis=-1, keepdims=True), jnp.max(sp, axis=-1, keepdims=True))
        ec = jnp.exp(sc - m)
        ep = jnp.exp(sp - m)
        den = jnp.sum(ec, axis=-1, keepdims=True) + jnp.sum(ep, axis=-1, keepdims=True)
        inv = 1.0 / den
        pc_s[...] = (ec * inv).astype(BF16)
        pp_s[...] = (ep * inv).astype(BF16)
        lse = m + jnp.log(den)
        lane = lax.broadcasted_iota(jnp.int32, (CHUNK, 128), 1)
        for r in range(d):
            lblk = jnp.zeros((CHUNK, 128), F32)
            for j in range(hp):
                o = _dot(pc_s[bs(r, j), :], blk(vf, r, j)) + _dot(pp_s[bs(r, j), :], blk(vpf, r, j))
                of[j, r * CHUNK:(r + 1) * CHUNK, :] = o
                lblk = jnp.where(lane == hh * hp + j, lse[bs(r, j)], lblk)
            lf[r * CHUNK:(r + 1) * CHUNK, :] = lblk
        _from_class(of, o_ref, d, hp, nat)
        lnat = pltpu.einshape("(rt)l->(tr)l", lf[...], r=d)

        @pl.when(hh == 0)
        def _():
            l_ref[...] = lnat

        @pl.when(hh > 0)
        def _():
            l_ref[...] += lnat

    def cols(cb):
        return pl.BlockSpec((T, cw), lambda i, hh: (i, cb * nh + hh))

    tile = pltpu.VMEM((hp, T, HEAD), BF16)
    staging = pltpu.VMEM((hp, T, HEAD) if _strided_regroup(d) else (1, 8, HEAD), F32)
    return pl.pallas_call(
        body, name=f"attn_fwd_dilated_{d}",
        grid=(S // T, nh),
        in_specs=[cols(qcb), cols(kcb), cols(vcb)],
        out_specs=[cols(0), pl.BlockSpec((T, 128), lambda i, hh: (i, 0))],
        out_shape=[jax.ShapeDtypeStruct((S, WIDTH), BF16), jax.ShapeDtypeStruct((S, 128), F32)],
        scratch_shapes=[tile, pltpu.VMEM((2, nh, hp, T, HEAD), BF16), pltpu.VMEM((2, nh, hp, T, HEAD), BF16),
                        pltpu.VMEM((hp, T, HEAD), F32), pltpu.VMEM((T, 128), F32), staging,
                        pltpu.VMEM((nblocks * CHUNK, CHUNK), F32), pltpu.VMEM((nblocks * CHUNK, CHUNK), F32),
                        pltpu.VMEM((nblocks * CHUNK, CHUNK), BF16), pltpu.VMEM((nblocks * CHUNK, CHUNK), BF16)],
        compiler_params=_params(("arbitrary", "arbitrary")),
    )(proj, proj, proj)


def _attn_bwd_dilated(proj, qcb, kcb, vcb, do, lse, delta, d, dk_in=None, dv_in=None):
    S = proj.shape[0]
    T = CHUNK * d
    nt = S // T
    hp, nh, cw = _dilated_split(d)
    nblocks = d * hp

    chained = dk_in is not None

    def body(q_ref, k_ref, v_ref, do_ref, l_ref, d_ref, *rest):
        dki_ref, dvi_ref = rest[:2] if chained else (None, None)
        (dq_ref, dk_ref, dv_ref, qf, dof, kbuf, vbuf, dqf, gk, gv, nat,
         sc_s, sp_s, dpc_s, dpp_s, pc_s, pp_s, dsc_s, dsp_s) = rest[2 * chained:]
        hh, i = pl.program_id(0), pl.program_id(1)
        kf, vf, newk, newv = kbuf.at[i % 2], vbuf.at[i % 2], gk.at[i % 2], gv.at[i % 2]
        kpf, vpf, acck, accv = kbuf.at[1 - i % 2], vbuf.at[1 - i % 2], gk.at[1 - i % 2], gv.at[1 - i % 2]

        @pl.when(i == 0)
        def _():
            for ref in (kbuf, vbuf, gk, gv):
                ref[...] = jnp.zeros_like(ref)
            dk_ref[...] = jnp.zeros_like(dk_ref)
            dv_ref[...] = jnp.zeros_like(dv_ref)

        def blk(ref, r, j):
            return ref[j, r * CHUNK:(r + 1) * CHUNK, :]

        def bs(r, j):
            return slice((r * hp + j) * CHUNK, (r * hp + j + 1) * CHUNK)

        @pl.when(i < nt)
        def _():
            _by_class(q_ref, qf, d, hp, nat)
            _by_class(do_ref, dof, d, hp, nat)
            _by_class(k_ref, kf, d, hp, nat)
            _by_class(v_ref, vf, d, hp, nat)
            lses, deltas = [], []
            lcls = pltpu.einshape("(tr)l->(rt)l", l_ref[...], r=d)
            dcls = pltpu.einshape("(tr)l->(rt)l", d_ref[...], r=d)
            for r in range(d):
                lblk = lcls[r * CHUNK:(r + 1) * CHUNK]
                dblk = dcls[r * CHUNK:(r + 1) * CHUNK]
                for j in range(hp):
                    lses.append(_col(lblk, hh * hp + j))
                    deltas.append(_col(dblk, hh * hp + j))
                    qb, dob = blk(qf, r, j), blk(dof, r, j)
                    sc_s[bs(r, j), :] = _dot_nt(qb, blk(kf, r, j))
                    dpc_s[bs(r, j), :] = _dot_nt(dob, blk(vf, r, j))
                    sp_s[bs(r, j), :] = _dot_nt(qb, blk(kpf, r, j))
                    dpp_s[bs(r, j), :] = _dot_nt(dob, blk(vpf, r, j))
            lse = jnp.concatenate(lses, axis=0)
            delta = jnp.concatenate(deltas, axis=0)
            row = lax.broadcasted_iota(jnp.int32, (nblocks * CHUNK, CHUNK), 0) & (CHUNK - 1)
            col = lax.broadcasted_iota(jnp.int32, (nblocks * CHUNK, CHUNK), 1)
            sp = jnp.where(col >= row, sp_s[...] * ATT_SCALE, NEG) + jnp.where(i > 0, 0.0, NEG)
            pp = jnp.exp(sp - lse)
            pp_s[...] = pp.astype(BF16)
            dsp_s[...] = (pp * (dpp_s[...] - delta)).astype(BF16)
            sc = jnp.where(col <= row, sc_s[...] * ATT_SCALE, NEG)
            pc = jnp.exp(sc - lse)
            pc_s[...] = pc.astype(BF16)
            dsc_s[...] = (pc * (dpc_s[...] - delta)).astype(BF16)
            for r in range(d):
                rows = slice(r * CHUNK, (r + 1) * CHUNK)
                for j in range(hp):
                    qb, dob = blk(qf, r, j), blk(dof, r, j)
                    dsc, dsp = dsc_s[bs(r, j), :], dsp_s[bs(r, j), :]
                    dqf[j, rows, :] = (_dot(dsc, blk(kf, r, j)) + _dot(dsp, blk(kpf, r, j))) * ATT_SCALE
                    newk[j, rows, :] = _dot_tn(dsc, qb) * ATT_SCALE
                    newv[j, rows, :] = _dot_tn(pc_s[bs(r, j), :], dob)
                    acck[j, rows, :] += _dot_tn(dsp, qb) * ATT_SCALE
                    accv[j, rows, :] += _dot_tn(pp_s[bs(r, j), :], dob)
            _from_class(dqf, dq_ref, d, hp, nat)

        @pl.when(i > 0)
        def _():
            _from_class(acck, dk_ref, d, hp, nat, dki_ref)
            _from_class(accv, dv_ref, d, hp, nat, dvi_ref)

    def cur(width, cb, nsplit):
        return pl.BlockSpec((T, width), lambda hh, i: (jnp.minimum(i, nt - 1), cb * nsplit + hh * (nsplit > 1)))

    def lag():
        return pl.BlockSpec((T, cw), lambda hh, i: (jnp.maximum(i - 1, 0), hh))

    tile = pltpu.VMEM((hp, T, HEAD), BF16)
    acc = pltpu.VMEM((hp, T, HEAD), F32)
    f32s = pltpu.VMEM((nblocks * CHUNK, CHUNK), F32)
    b16s = pltpu.VMEM((nblocks * CHUNK, CHUNK), BF16)
    return pl.pallas_call(
        body, name=f"attn_bwd_dilated_{d}",
        grid=(nh, nt + 1),
        in_specs=[cur(cw, qcb, nh), cur(cw, kcb, nh), cur(cw, vcb, nh), cur(cw, 0, nh), cur(128, 0, 1), cur(128, 0, 1)]
                 + [lag(), lag()] * chained,
        out_specs=[cur(cw, 0, nh), lag(), lag()],
        out_shape=[jax.ShapeDtypeStruct((S, WIDTH), BF16)] * 3,
        scratch_shapes=[tile, tile, pltpu.VMEM((2, hp, T, HEAD), BF16), pltpu.VMEM((2, hp, T, HEAD), BF16), acc,
                        pltpu.VMEM((2, hp, T, HEAD), F32), pltpu.VMEM((2, hp, T, HEAD), F32),
                        acc if _strided_regroup(d) else pltpu.VMEM((1, 8, HEAD), F32)]
                       + [f32s] * 4 + [b16s] * 4,
        compiler_params=_params(("arbitrary", "arbitrary")),
    )(proj, proj, proj, do, lse, delta, *((dk_in, dv_in) if chained else ()))


def _abm_bwd(proj, cdf, dy3, ln_g, ln_b, wsm, wsm_t, bias_full, pool_w, pool_wt, pool_scale, kv, bands, bands_t, dst):
    S = proj.shape[0]
    tm = 512
    nchunk = tm // CHUNK
    nblk = S // CHUNK

    def body(u_ref, v_ref, ag_ref, p_ref, ph_ref, pg_ref, pgn_ref, mq_ref, mg_ref, cdf_ref, dy_ref, dypn_ref,
             lng_ref, lnb_ref, wsm_ref, wsmt_ref, bias_ref, pw_ref, pwt_ref, ps_ref, kv_ref, band_ref, bandt_ref,
             dst_ref, dab_ref, dm_ref, dlng_ref, dlnb_ref, dws_ref, dbias_ref, dpw_ref, dps_ref, dkv_ref,
             mix, dvl, ddn):
        i = pl.program_id(0)

        @pl.when(i == 0)
        def _():
            for r in (dlng_ref, dlnb_ref, dws_ref, dbias_ref, dpw_ref, dps_ref, dkv_ref):
                r[...] = jnp.zeros_like(r)

        au = u_ref[...].astype(F32)
        av = v_ref[...].astype(F32)
        ag = ag_ref[...].astype(F32)
        u, du = _gelu_and_grad(au, cdf_ref[0].astype(F32))
        v, dgelu_v = _gelu_and_grad(av, cdf_ref[1].astype(F32))
        vhat, rstd = _layer_norm_fwd(v)
        vln = (vhat * lng_ref[...] + lnb_ref[...]).astype(BF16)
        for c in range(nchunk):
            for h in range(N_HEAD):
                rs, cs = slice(c * CHUNK, (c + 1) * CHUNK), slice(h * HEAD, (h + 1) * HEAD)
                mix[rs, cs] = _dot(wsm_ref[h], vln[rs, cs]) + bias_ref[:, cs]
        dya = dy_ref[0].astype(F32)
        sg, dsg = _silu_and_grad(ag)
        mixed = mix[...]
        dab_ref[:, 2 * WIDTH:3 * WIDTH] = (dya * u * mixed * dsg).astype(BF16)
        dab_ref[:, 0:WIDTH] = (dya * mixed * sg * du).astype(BF16)
        dmixed = dya * u * sg
        dmb = dmixed.astype(BF16)
        tril = (lax.broadcasted_iota(jnp.int32, (CHUNK, CHUNK), 1)
                <= lax.broadcasted_iota(jnp.int32, (CHUNK, CHUNK), 0))
        for c in range(nchunk):
            rs = slice(c * CHUNK, (c + 1) * CHUNK)
            dbias_ref[...] += dmixed[rs, :]
            for h in range(N_HEAD):
                cs = slice(h * HEAD, (h + 1) * HEAD)
                dvl[rs, cs] = _dot(wsmt_ref[h], dmb[rs, cs])
                dws_ref[h] += jnp.where(tril, _dot_nt(dmb[rs, cs], vln[rs, cs]), 0.0)
        dvln = dvl[...]
        dlng_ref[...] += jnp.sum(dvln * vhat, axis=0, keepdims=True)
        dlnb_ref[...] += jnp.sum(dvln, axis=0, keepdims=True)
        dvh = dvln * lng_ref[...]
        dv = rstd * (dvh - jnp.mean(dvh, axis=-1, keepdims=True)
                     - vhat * jnp.mean(dvh * vhat, axis=-1, keepdims=True))
        dab_ref[:, WIDTH:2 * WIDTH] = (dv * dgelu_v).astype(BF16)

        halo_ok = (i > 0).astype(F32)
        for c in range(nchunk):
            rs = slice(c * CHUNK, (c + 1) * CHUNK)
            for g, win in enumerate(POOL_WINDOWS):
                cs = slice(g * HEAD, (g + 1) * HEAD)
                cur = p_ref[rs, cs]
                if c == 0:
                    prev = (ph_ref[:, cs].astype(F32) * halo_ok).astype(BF16)
                else:
                    prev = p_ref[(c - 1) * CHUNK:c * CHUNK, cs]
                sums = _dot(band_ref[g, 0], cur) + _dot(band_ref[g, 1], prev)
                dvl[rs, cs] = sums * _inv_count(i * tm + c * CHUNK, win) - cur.astype(F32)
        dmat = dvl[...].astype(BF16)
        for g in range(4):
            cs = slice(g * HEAD, (g + 1) * HEAD)
            mix[:, cs] = _dot(dmat[:, cs], pw_ref[g])
        yg = mix[...]
        pg = pg_ref[...].astype(F32)
        dyp = dy_ref[1].astype(F32)
        spg, dspg = _silu_and_grad(pg)
        dyy = dyp * spg
        scale = ps_ref[...]
        dab_ref[:, 4 * WIDTH:5 * WIDTH] = (dyp * yg * scale * dspg).astype(BF16)
        dps_ref[...] += jnp.sum(dyy * yg, axis=0, keepdims=True)
        dyg = (dyy * scale).astype(BF16)
        for g in range(4):
            cs = slice(g * HEAD, (g + 1) * HEAD)
            dpw_ref[g] += _dot_tn(dmat[:, cs], dyg[:, cs])
            mix[:, cs] = _dot(dyg[:, cs], pwt_ref[g])
        next_ok = (i + 1 < S // tm).astype(F32)
        dygn = (dypn_ref[...].astype(F32) * _silu(pgn_ref[...].astype(F32)) * scale * next_ok).astype(BF16)
        for c in range(nchunk + 1):
            for g, win in enumerate(POOL_WINDOWS):
                cs = slice(g * HEAD, (g + 1) * HEAD)
                if c < nchunk:
                    dd = mix[c * CHUNK:(c + 1) * CHUNK, cs]
                else:
                    dd = _dot(dygn[:, cs], pwt_ref[g])
                ddn[c * CHUNK:(c + 1) * CHUNK, cs] = dd * _inv_count(i * tm + c * CHUNK, win)
        ddnb = ddn[...].astype(BF16)
        for c in range(nchunk):
            rs = slice(c * CHUNK, (c + 1) * CHUNK)
            ns = slice((c + 1) * CHUNK, (c + 2) * CHUNK)
            for g, win in enumerate(POOL_WINDOWS):
                cs = slice(g * HEAD, (g + 1) * HEAD)
                dp = _dot(bandt_ref[g, 0], ddnb[rs, cs]) + _dot(bandt_ref[g, 1], ddnb[ns, cs]) - mix[rs, cs]
                dab_ref[rs, 3 * WIDTH + g * HEAD:3 * WIDTH + (g + 1) * HEAD] = dp.astype(BF16)

        mg = mg_ref[...].astype(F32)
        dym = dy_ref[2].astype(F32)
        smg, dsmg = _silu_and_grad(mg)
        dob = (dym * smg).astype(BF16)
        for h in range(N_HEAD):
            cs = slice(h * HEAD, (h + 1) * HEAD)
            vs = slice(WIDTH + h * HEAD, WIDTH + (h + 1) * HEAD)
            qh = mq_ref[:, cs]
            p = _mem_softmax(qh, kv_ref[:, cs])
            pb = p.astype(BF16)
            mix[:, cs] = _dot(pb, kv_ref[:, vs])
            dp = _dot_nt(dob[:, cs], kv_ref[:, vs])
            ds = (p * (dp - jnp.sum(p * dp, axis=-1, keepdims=True))).astype(BF16)
            dm_ref[:, cs] = (_dot(ds, kv_ref[:, cs]) * ATT_SCALE).astype(BF16)
            dkv_ref[:, cs] += _dot_tn(ds, qh) * ATT_SCALE
            dkv_ref[:, vs] += _dot_tn(pb, dob[:, cs])
        dm_ref[:, WIDTH:2 * WIDTH] = (dym * mix[...] * dsmg).astype(BF16)

    blk = tm // CHUNK
    small = [_full((1, WIDTH)), _full((1, WIDTH)), _full((N_HEAD, CHUNK, CHUNK)), _full((CHUNK, WIDTH)),
             _full((4, HEAD, HEAD)), _full((1, WIDTH)), _full((MEM_LEN, 2 * WIDTH))]
    return pl.pallas_call(
        body, name="abm_bwd",
        grid=(S // tm,),
        in_specs=[_rows(tm, WIDTH, CB_U), _rows(tm, WIDTH, CB_V), _rows(tm, WIDTH, CB_AGATE),
                  _rows(tm, WIDTH, CB_PIN),
                  pl.BlockSpec((CHUNK, WIDTH), lambda i: (jnp.maximum(i * blk - 1, 0), CB_PIN)),
                  _rows(tm, WIDTH, CB_PGATE),
                  pl.BlockSpec((CHUNK, WIDTH), lambda i: (jnp.minimum(i * blk + blk, nblk - 1), CB_PGATE)),
                  _rows(tm, WIDTH, CB_MQ), _rows(tm, WIDTH, CB_MGATE),
                  pl.BlockSpec((2, tm, WIDTH), lambda i: (0, i, 0)),
                  pl.BlockSpec((3, tm, WIDTH), lambda i: (0, i, 0)),
                  pl.BlockSpec((None, CHUNK, WIDTH), lambda i: (1, jnp.minimum(i * blk + blk, nblk - 1), 0)),
                  _full((1, WIDTH)), _full((1, WIDTH)), _full((N_HEAD, CHUNK, CHUNK)), _full((N_HEAD, CHUNK, CHUNK)),
                  _full((CHUNK, WIDTH)), _full((4, HEAD, HEAD)), _full((4, HEAD, HEAD)), _full((1, WIDTH)),
                  _full((MEM_LEN, 2 * WIDTH)), _full((4, 2, CHUNK, CHUNK)), _full((4, 2, CHUNK, CHUNK)), ANY],
        out_specs=[_rows(tm, 5 * WIDTH), _rows(tm, 2 * WIDTH)] + small,
        out_shape=[jax.ShapeDtypeStruct(dst.shape, BF16), jax.ShapeDtypeStruct((S, 2 * WIDTH), BF16),
                   jax.ShapeDtypeStruct((1, WIDTH), F32), jax.ShapeDtypeStruct((1, WIDTH), F32),
                   jax.ShapeDtypeStruct((N_HEAD, CHUNK, CHUNK), F32), jax.ShapeDtypeStruct((CHUNK, WIDTH), F32),
                   jax.ShapeDtypeStruct((4, HEAD, HEAD), F32), jax.ShapeDtypeStruct((1, WIDTH), F32),
                   jax.ShapeDtypeStruct((MEM_LEN, 2 * WIDTH), F32)],
        input_output_aliases={23: 0},
        scratch_shapes=[pltpu.VMEM((tm, WIDTH), F32), pltpu.VMEM((tm, WIDTH), F32),
                        pltpu.VMEM((tm + CHUNK, WIDTH), F32)],
        compiler_params=_params(("arbitrary",)),
    )(proj, proj, proj, proj, proj, proj, proj, proj, proj, cdf, dy3, dy3,
      ln_g, ln_b, wsm, wsm_t, bias_full, pool_w, pool_wt, pool_scale, kv, bands, bands_t, dst)


def _bias_reduce(dbias_full):
    def body(d_ref, o_ref):
        d = d_ref[...]
        o_ref[...] = _put_cols([jnp.sum(d[:, h * HEAD:(h + 1) * HEAD], axis=1, keepdims=True) for h in range(N_HEAD)])

    return pl.pallas_call(body, name="bias_reduce", out_shape=jax.ShapeDtypeStruct((CHUNK, 128), F32))(dbias_full)


def _mem_bwd(mem, g, mem_n, w, dkv):
    def body(m_ref, g_ref, mn_ref, w_ref, dkv_ref, dw_ref, dg_ref):
        dkvb = dkv_ref[...].astype(BF16)
        dw_ref[...] = _dot_tn(mn_ref[...], dkvb).astype(BF16)
        dmn = _dot_nt(dkvb, w_ref[...])
        xf = m_ref[...]
        r = lax.rsqrt(jnp.mean(xf * xf, axis=-1, keepdims=True) + EPS)
        dg_ref[...] = jnp.sum(dmn * xf * r, axis=0, keepdims=True)

    return pl.pallas_call(
        body, name="mem_bwd",
        out_shape=[jax.ShapeDtypeStruct((D_MODEL, 2 * WIDTH), BF16), jax.ShapeDtypeStruct((1, D_MODEL), F32)],
        compiler_params=pltpu.CompilerParams(vmem_limit_bytes=VMEM_LIMIT),
    )(mem, g, mem_n, w, dkv)


def _dh_bwd(dpb, dpg, wt, x, g, dxo, parts=(), grads=()):
    S = x.shape[0]
    tm, tkb, tkg = 1024, D_BRANCHES // 4, D_GATES // 4
    nkb, nkg = 4, 4
    nk = nkb + nkg
    ni = S // tm
    n, m = len(parts), len(grads)

    def body(dpb_ref, wbr_ref, dpg_ref, wg0_ref, wg1_ref, x_ref, g_ref, dxo_ref, *rest):
        p_in, g_in = rest[:n], rest[n:n + m]
        dx_ref, dg_ref = rest[n + m:n + m + 2]
        p_out, g_out = rest[n + m + 2:2 * n + m + 2], rest[2 * n + m + 2:2 * (n + m) + 2]
        acc, sems = rest[2 * (n + m) + 2], rest[2 * (n + m) + 3:]
        second_sems, first_sems = (sems[:3] if n else ()), sems[3 if n else 0:]
        i, kk = pl.program_id(0), pl.program_id(1)

        @pl.when(jnp.logical_and(i == 0, kk == 0))
        def _():
            dg_ref[...] = jnp.zeros_like(dg_ref)
            if n:
                _comm_start(_rs_second(p_in, p_out, *second_sems))
            if m:
                _comm_start(_rs_first(g_in, g_out, *first_sems))

        @pl.when(kk == 0)
        def _():
            acc[...] = jnp.zeros_like(acc)

        @pl.when(kk < nkb)
        def _():
            acc[...] += _dot(dpb_ref[...], wbr_ref[...])

        @pl.when(kk >= nkb)
        def _():
            acc[...] += _dot(dpg_ref[:, :tkg // 2], wg0_ref[...]) + _dot(dpg_ref[:, tkg // 2:], wg1_ref[...])

        @pl.when(kk == nk - 1)
        def _():
            xf = x_ref[...]
            r = lax.rsqrt(jnp.mean(xf * xf, axis=-1, keepdims=True) + EPS)
            xhat = xf * r
            dh = acc[...]
            dg_ref[...] += jnp.sum(dh * xhat, axis=0, keepdims=True)
            dxh = dh * g_ref[...]
            dx_ref[...] = dxo_ref[...] + r * (dxh - xhat * jnp.mean(dxh * xhat, axis=-1, keepdims=True))

        if n or m:
            @pl.when(jnp.logical_and(i == ni - 1, kk == nk - 1))
            def _():
                if n:
                    _comm_wait(_rs_second(p_in, p_out, *second_sems))
                if m:
                    _comm_wait(_rs_first(g_in, g_out, *first_sems))

    def gate_rows(half):
        return pl.BlockSpec((tkg // 2, D_MODEL),
                            lambda i, k: (D_BRANCHES // (tkg // 2) + 2 * jnp.maximum(k - nkb, 0) + half, 0))

    res = pl.pallas_call(
        body, name="dh_bwd_scatter" if (n or m) else "dh_bwd",
        grid=(ni, nk),
        in_specs=[pl.BlockSpec((tm, tkb), lambda i, k: (i, jnp.minimum(k, nkb - 1))),
                  pl.BlockSpec((tkb, D_MODEL), lambda i, k: (jnp.minimum(k, nkb - 1), 0)),
                  pl.BlockSpec((tm, tkg), lambda i, k: (i, jnp.maximum(k - nkb, 0))),
                  gate_rows(0), gate_rows(1),
                  pl.BlockSpec((tm, D_MODEL), lambda i, k: (i, 0)), pl.BlockSpec((1, D_MODEL), lambda i, k: (0, 0)),
                  pl.BlockSpec((tm, D_MODEL), lambda i, k: (i, 0))] + [ANY] * (n + m),
        out_specs=[pl.BlockSpec((tm, D_MODEL), lambda i, k: (i, 0)), pl.BlockSpec((1, D_MODEL), lambda i, k: (0, 0))]
                  + [ANY] * (n + m),
        out_shape=[jax.ShapeDtypeStruct((S, D_MODEL), F32), jax.ShapeDtypeStruct((1, D_MODEL), F32)]
                  + [jax.ShapeDtypeStruct(p.shape, p.dtype) for p in parts]
                  + [jax.ShapeDtypeStruct(gr.shape[:1] + gr.shape[2:], gr.dtype) for gr in grads],
        scratch_shapes=[pltpu.VMEM((tm, D_MODEL), F32)] + (_dma_sems(3 * n, 3 * n, n) if n else [])
                       + (_dma_sems(N_CHIP * m, N_CHIP * m) if m else []),
        compiler_params=_params(("arbitrary", "arbitrary")),
    )(dpb, wt, dpg, wt, wt, x, g, dxo, *parts, *grads)
    return res[0], res[1], list(res[2:2 + n]), list(res[2 + n:])


def _dw_in(h, dpb, dpg, parts=(), grads=()):
    S = h.shape[0]
    tk = 2048
    nk = S // tk
    n, m = len(parts), len(grads)
    tmb = D_BRANCHES // 4
    ng = D_GATES // GATE_TILE

    def accumulate(a_ref, h_ref, o_ref, acc):
        kk = pl.program_id(1)

        @pl.when(kk == 0)
        def _():
            acc[...] = jnp.zeros_like(acc)

        acc[...] += _dot_tn(a_ref[...], h_ref[...])

        @pl.when(kk == nk - 1)
        def _():
            o_ref[...] = acc[...].astype(BF16)

    def branches(a_ref, h_ref, *rest):
        p_in, g_in, o_ref = rest[:n], rest[n:n + m], rest[n + m]
        p_out, g_out = rest[n + m + 1:2 * n + m + 1], rest[2 * n + m + 1:2 * (n + m) + 1]
        acc, sems = rest[2 * (n + m) + 1], rest[2 * (n + m) + 2:]
        second_sems, first_sems = (sems[:3] if n else ()), sems[3 if n else 0:]
        i, kk = pl.program_id(0), pl.program_id(1)

        if n or m:
            @pl.when(jnp.logical_and(i == 0, kk == 0))
            def _():
                if n:
                    _comm_start(_rs_second(p_in, p_out, *second_sems))
                if m:
                    _comm_start(_rs_first(g_in, g_out, *first_sems))

        accumulate(a_ref, h_ref, o_ref, acc)

        if n or m:
            @pl.when(jnp.logical_and(i == 3, kk == nk - 1))
            def _():
                if n:
                    _comm_wait(_rs_second(p_in, p_out, *second_sems))
                if m:
                    _comm_wait(_rs_first(g_in, g_out, *first_sems))

    def gates(a_ref, h_ref, dst_ref, *rest):
        q_in, o_ref, q_out, acc, sems = rest[:m], rest[m], rest[m + 1:2 * m + 1], rest[2 * m + 1], rest[2 * m + 2:]
        i, kk = pl.program_id(0), pl.program_id(1)

        if m:
            @pl.when(jnp.logical_and(i == 0, kk == 0))
            def _():
                _comm_start(_rs_second(q_in, q_out, *sems))

        accumulate(a_ref, h_ref, o_ref, acc)

        if m:
            @pl.when(jnp.logical_and(i == ng - 1, kk == nk - 1))
            def _():
                _comm_wait(_rs_second(q_in, q_out, *sems))

    res = pl.pallas_call(
        branches, name="dw_in_branches_scatter" if (n or m) else "dw_in_branches",
        grid=(4, nk),
        in_specs=[pl.BlockSpec((tk, tmb), lambda i, k: (k, i)), pl.BlockSpec((tk, D_MODEL), lambda i, k: (k, 0))]
                 + [ANY] * (n + m),
        out_specs=[pl.BlockSpec((tmb, D_MODEL), lambda i, k: (i, 0))] + [ANY] * (n + m),
        out_shape=[jax.ShapeDtypeStruct((D_IN, D_MODEL), BF16)]
                  + [jax.ShapeDtypeStruct(p.shape, p.dtype) for p in parts]
                  + [jax.ShapeDtypeStruct(gr.shape[:1] + gr.shape[2:], gr.dtype) for gr in grads],
        scratch_shapes=[pltpu.VMEM((tmb, D_MODEL), F32)] + (_dma_sems(3 * n, 3 * n, n) if n else [])
                       + (_dma_sems(N_CHIP * m, N_CHIP * m) if m else []),
        compiler_params=_params(("arbitrary", "arbitrary")),
    )(dpb, h, *parts, *grads)
    pair = _pair_sum(grads, res[1 + n:]) if m else []
    res2 = pl.pallas_call(
        gates, name="dw_in_gates_scatter" if m else "dw_in_gates",
        grid=(ng, nk),
        in_specs=[pl.BlockSpec((tk, GATE_TILE), lambda i, k: (k, i)), pl.BlockSpec((tk, D_MODEL), lambda i, k: (k, 0)),
                  ANY] + [ANY] * m,
        out_specs=[pl.BlockSpec((GATE_TILE, D_MODEL), lambda i, k: (D_BRANCHES // GATE_TILE + i, 0))] + [ANY] * m,
        out_shape=[jax.ShapeDtypeStruct((D_IN, D_MODEL), BF16)] + [jax.ShapeDtypeStruct(p.shape, p.dtype) for p in pair],
        input_output_aliases={2: 0},
        scratch_shapes=[pltpu.VMEM((GATE_TILE, D_MODEL), F32)] + (_dma_sems(3 * m, 3 * m, m) if m else []),
        compiler_params=_params(("arbitrary", "arbitrary")),
    )(dpg, h, res[0], *pair)
    return res2[0], list(res[1:1 + n]), list(res2[1:])


def _row_tile(R, C, block_bytes=2 << 20):
    for cand in range(min(R, block_bytes // (C * 4)) // 8 * 8, 0, -8):
        if R % cand == 0:
            return cand
    return R


def _adamw_update(p_ref, w_ref, m_ref, v_ref, g_ref, d_ref, nm_ref, nv_ref):
    c1 = 1.0 / (1.0 - ADAM_B1 ** ADAM_STEP)
    c2 = 1.0 / (1.0 - ADAM_B2 ** ADAM_STEP)
    g = p_ref[0].astype(F32)
    for k in range(1, p_ref.shape[0]):
        g = g + p_ref[k].astype(F32)
    nm = ADAM_B1 * m_ref[...] + (1.0 - ADAM_B1) * g
    nv = ADAM_B2 * v_ref[...] + (1.0 - ADAM_B2) * (g * g)
    g_ref[...] = g
    nm_ref[...] = nm
    nv_ref[...] = nv
    d_ref[...] = -ADAM_LR * ((nm * c1) / (jnp.sqrt(nv * c2) + ADAM_EPS) + ADAM_WD * w_ref[...])


def _adamw(parts, w, m, v, name):
    P, R, C = parts.shape
    tr = _row_tile(R, C)

    def body(*refs):
        _adamw_update(*refs)

    spec = pl.BlockSpec((tr, C), lambda i: (i, 0))
    return pl.pallas_call(
        body, name=name,
        grid=(R // tr,),
        in_specs=[pl.BlockSpec((P, tr, C), lambda i: (0, i, 0)), spec, spec, spec],
        out_specs=[spec] * 4,
        out_shape=[jax.ShapeDtypeStruct((R, C), F32)] * 4,
        compiler_params=_params(("parallel",)),
    )(parts, w, m, v)


def _adamw_layers(parts, w, m, v, name):
    depth = len(parts)
    P, R, C = parts[0].shape
    tr = _row_tile(R, C, 1 << 20)

    def body(*refs):
        layer = pl.program_id(0)
        for k in range(depth):
            @pl.when(layer == k)
            def _(k=k):
                _adamw_update(refs[k], *refs[depth:])

    def part_spec(k):
        return pl.BlockSpec((P, tr, C), lambda l, i: (0, jnp.where(l == k, i, 0), 0))

    spec = pl.BlockSpec((None, tr, C), lambda l, i: (l, i, 0))
    return pl.pallas_call(
        body, name=name,
        grid=(depth, R // tr),
        in_specs=[part_spec(k) for k in range(depth)] + [spec] * 3,
        out_specs=[spec] * 4,
        out_shape=[jax.ShapeDtypeStruct((depth, R, C), F32)] * 4,
        compiler_params=_params(("arbitrary", "arbitrary")),
    )(*parts, w, m, v)


def _place():
    return lax.axis_index("x"), lax.axis_index("y"), lax.axis_index("c")


def _all_gather(shards):
    n = len(shards)

    def body(*refs):
        ins, outs = refs[:n], refs[n:2 * n]
        send1, recv1, local_sems, relay_send, relay_recv, send2, recv2 = refs[2 * n:]
        x, y, c = _place()
        me, sibling = (x, y, c), (x, y, 1 - c)
        x_nbr, y_nbr, diagonal = [(*chip, c) for chip in _other_chips(x, y)]
        local, first_out, first_in, relay_out, relay_in = [], [], [], [], []
        for a in range(n):
            local.append(pltpu.make_async_copy(ins[a], outs[a].at[_dev(me)], local_sems.at[a]))
            for k, to in enumerate((sibling, x_nbr, y_nbr)):
                first_out.append(_remote(ins[a], outs[a].at[_dev(me)], send1, recv1, 3 * a + k, to))
                first_in.append(_remote(ins[a], outs[a].at[_dev(to)], send1, recv1, 3 * a + k, to))
            half = shards[a].shape[0] // 2
            for k, (src, to, rows) in enumerate(((x_nbr, y_nbr, pl.ds(0, half)), (y_nbr, x_nbr, pl.ds(half, half)))):
                passed, got = outs[a].at[_dev(src)].at[rows], outs[a].at[_dev(diagonal)].at[rows]
                relay_out.append(_remote(passed, passed, relay_send, relay_recv, 2 * a + k, to))
                relay_in.append(_remote(got, got, relay_send, relay_recv, 2 * a + k, to))
        second = _ag_second(outs, send2, recv2)
        for cp in local + first_out:
            cp.start()
        for a in range(n):
            for k in range(2):
                first_in[3 * a + 1 + k].wait_recv()
                relay_out[2 * a + k].start()
                second[1][3 * a + k].start()
        for a in range(n):
            for k in range(2):
                relay_in[2 * a + k].wait_recv()
            second[1][3 * a + 2].start()
        for a in range(n):
            first_in[3 * a].wait_recv()
        for cp in second[2]:
            cp.wait_recv()
        for cp in first_out + relay_out + second[1]:
            cp.wait_send()
        for cp in local:
            cp.wait()

    assert all(s.shape[0] % 32 == 0 for s in shards)
    return pl.pallas_call(
        body, name="weights_all_gather",
        in_specs=[ANY] * n, out_specs=[ANY] * n,
        out_shape=[jax.ShapeDtypeStruct((N_DEV,) + s.shape, s.dtype) for s in shards],
        scratch_shapes=_dma_sems(3 * n, 3 * n, n, 2 * n, 2 * n, 3 * n, 3 * n),
        compiler_params=pltpu.CompilerParams(has_side_effects=True),
    )(*shards)


N_BIG = 4


def _dev(p):
    return 4 * p[0] + 2 * p[1] + p[2]


def _other_chips(x, y):
    return [(1 - x, y), (x, 1 - y), (1 - x, 1 - y)]


def _remote(src, dst, send_sems, recv_sems, k, to):
    return pltpu.make_async_remote_copy(src_ref=src, dst_ref=dst, send_sem=send_sems.at[k], recv_sem=recv_sems.at[k],
                                        device_id=to, device_id_type=MESH)


def _ag_first(ins, outs, send_sems, recv_sems, local_sems):
    x, y, c = _place()
    me = (x, y, c)
    targets = [(x, y, 1 - c)] + [(*chip, c) for chip in _other_chips(x, y)]
    local, out, inc = [], [], []
    for a in range(len(ins)):
        local.append(pltpu.make_async_copy(ins[a], outs[a].at[_dev(me)], local_sems.at[a]))
        for k, to in enumerate(targets):
            out.append(_remote(ins[a], outs[a].at[_dev(me)], send_sems, recv_sems, 4 * a + k, to))
            inc.append(_remote(ins[a], outs[a].at[_dev(to)], send_sems, recv_sems, 4 * a + k, to))
    return local, out, inc


def _ag_second(bufs, send_sems, recv_sems):
    x, y, c = _place()
    out, inc = [], []
    for a in range(len(bufs)):
        for j, chip in enumerate(_other_chips(x, y)):
            mine, theirs = bufs[a].at[_dev((*chip, c))], bufs[a].at[_dev((*chip, 1 - c))]
            out.append(_remote(mine, mine, send_sems, recv_sems, 3 * a + j, (x, y, 1 - c)))
            inc.append(_remote(theirs, theirs, send_sems, recv_sems, 3 * a + j, (x, y, 1 - c)))
    return [], out, inc


def _rs_first(ins, outs, send_sems, recv_sems):
    x, y, c = _place()
    out = [_remote(ins[a].at[j, 1 - c], outs[a].at[j], send_sems, recv_sems, N_CHIP * a + j, (x, y, 1 - c))
           for a in range(len(ins)) for j in range(N_CHIP)]
    return [], out, out


def _rs_second(ins, outs, send_sems, recv_sems, local_sems):
    x, y, c = _place()
    my_chip = 2 * x + y
    local, out, inc = [], [], []
    for a in range(len(ins)):
        local.append(pltpu.make_async_copy(ins[a].at[my_chip], outs[a].at[my_chip], local_sems.at[a]))
        for k, (ox, oy) in enumerate(_other_chips(x, y)):
            out.append(_remote(ins[a].at[2 * ox + oy], outs[a].at[my_chip], send_sems, recv_sems, 3 * a + k, (ox, oy, c)))
            inc.append(_remote(ins[a].at[2 * ox + oy], outs[a].at[2 * ox + oy], send_sems, recv_sems, 3 * a + k,
                               (ox, oy, c)))
    return local, out, inc


def _comm_start(exchange):
    local, out, _ = exchange
    for cp in local + out:
        cp.start()


def _comm_wait(exchange):
    local, out, inc = exchange
    for cp in inc:
        cp.wait_recv()
    for cp in out:
        cp.wait_send()
    for cp in local:
        cp.wait()


def _dma_sems(*counts):
    return [pltpu.SemaphoreType.DMA((n,)) for n in counts]


def _pair_sum(grads, recvs):
    n = len(grads)

    def body(c_ref, *refs):
        for a in range(n):
            refs[2 * n + a][...] = (refs[a][...].astype(F32) + refs[n + a][...].astype(F32)).astype(BF16)

    def g_spec(g):
        return pl.BlockSpec((None, None) + g.shape[2:], lambda j, c_ref: (j, c_ref[0], 0, 0))

    def r_spec(r):
        return pl.BlockSpec((None,) + r.shape[1:], lambda j, c_ref: (j, 0, 0))

    return pl.pallas_call(
        body, name="pair_sum",
        grid_spec=pltpu.PrefetchScalarGridSpec(
            num_scalar_prefetch=1, grid=(N_CHIP,),
            in_specs=[g_spec(g) for g in grads] + [r_spec(r) for r in recvs],
            out_specs=[r_spec(r) for r in recvs]),
        out_shape=[jax.ShapeDtypeStruct(r.shape, BF16) for r in recvs],
        compiler_params=_params(("parallel",)),
    )(lax.axis_index("c").reshape(1).astype(jnp.int32), *grads, *recvs)


SMALL_ROWS = 544


def _all_reduce_small(buf, parts=()):
    n = len(parts)

    def body(in_ref, *rest):
        p_in, out_ref, p_out = rest[:n], rest[n], rest[n + 1:2 * n + 1]
        recv, acc, send1, recv1, send2, recv2 = rest[2 * n + 1:2 * n + 7]
        scatter_sems = rest[2 * n + 7:]
        x, y, c = _place()
        me = 4 * x + 2 * y + c
        peers = [(x ^ (r >> 2), y ^ ((r >> 1) & 1), c ^ (r & 1)) for r in range(1, N_DEV)]

        def idx(p):
            return 4 * p[0] + 2 * p[1] + p[2]

        if n:
            _comm_start(_rs_second(p_in, p_out, *scatter_sems))
        first = [pltpu.make_async_remote_copy(
            src_ref=in_ref.at[idx(p)], dst_ref=recv.at[me], send_sem=send1.at[r], recv_sem=recv1.at[r],
            device_id=p, device_id_type=MESH) for r, p in enumerate(peers)]
        for cp in first:
            cp.start()
        recv[me] = in_ref[me]
        for r, p in enumerate(peers):
            pltpu.make_async_remote_copy(
                src_ref=in_ref.at[idx(p)], dst_ref=recv.at[idx(p)], send_sem=send1.at[r], recv_sem=recv1.at[r],
                device_id=p, device_id_type=MESH).wait_recv()
        total = recv[0]
        for k in range(1, N_DEV):
            total = total + recv[k]
        acc[...] = total
        out_ref[me] = total
        second = [pltpu.make_async_remote_copy(
            src_ref=acc, dst_ref=out_ref.at[me], send_sem=send2.at[r], recv_sem=recv2.at[r],
            device_id=p, device_id_type=MESH) for r, p in enumerate(peers)]
        for cp in second:
            cp.start()
        for r, p in enumerate(peers):
            pltpu.make_async_remote_copy(
                src_ref=acc, dst_ref=out_ref.at[idx(p)], send_sem=send2.at[r], recv_sem=recv2.at[r],
                device_id=p, device_id_type=MESH).wait_recv()
        for cp in first + second:
            cp.wait_send()
        if n:
            _comm_wait(_rs_second(p_in, p_out, *scatter_sems))

    vm = pl.BlockSpec(memory_space=pltpu.VMEM)
    res = pl.pallas_call(
        body, name="small_grads_all_reduce",
        in_specs=[vm] + [ANY] * n, out_specs=[vm] + [ANY] * n,
        out_shape=[jax.ShapeDtypeStruct(buf.shape, F32)] + [jax.ShapeDtypeStruct(p.shape, p.dtype) for p in parts],
        scratch_shapes=[pltpu.VMEM(buf.shape, F32), pltpu.VMEM(buf.shape[1:], F32)] + _dma_sems(7, 7, 7, 7)
                       + (_dma_sems(3 * n, 3 * n, n) if n else []),
        compiler_params=pltpu.CompilerParams(has_side_effects=True, vmem_limit_bytes=VMEM_LIMIT),
    )(buf, *parts)
    return res[0], list(res[1:])


def _dilate(a, d):
    if d == 1:
        return a
    S, C = a.shape
    return a.reshape(S // d, d, C).transpose(1, 0, 2).reshape(S, C)


def _undilate(a, d):
    if d == 1:
        return a
    S, C = a.shape
    return a.reshape(d, S // d, C).transpose(1, 0, 2).reshape(S, C)


def _cols(a, cb, n=1):
    return a[:, cb * WIDTH:(cb + n) * WIDTH]


def _to_blocks(g, kind):
    if kind == "rows":
        C = g.shape[1]
        return g.reshape(N_CHIP, 2, -1, C)
    return g.reshape(N_CHIP, 2, 4 * WIDTH, -1)


SMALL = ("norm_g", "gm_ln_g", "gm_ln_b", "gm_ws", "gm_bs", "pool_w", "pool_scale", "mem_norm_g", "final_norm_g")


def _pack_small(tree):
    flat = jnp.concatenate([tree[k].reshape(-1, 128) for k in SMALL], axis=0)
    return jnp.pad(flat, ((0, N_DEV * SMALL_ROWS - flat.shape[0]), (0, 0)))


def _unpack_small(flat, like):
    out, at = {}, 0
    for k in SMALL:
        rows = like[k].size // 128
        out[k] = flat[at:at + rows].reshape(like[k].shape)
        at += rows
    return out


def _make_layer(wt, wkv, wb, wout, norm_g, mem_norm_g, ln_g, ln_b, gm_ws, gm_bs, pool_w, pool_scale):
    tril = jnp.tril(jnp.ones((CHUNK, CHUNK), bool))
    wsm = jnp.where(tril, gm_ws, 0.0).astype(BF16)
    pw = pool_w.astype(BF16)
    bands, bands_t = _band_constants()
    return dict(wt=wt, wkv=wkv, wb=wb, wout=wout, g=norm_g[None], mg=mem_norm_g[None],
                ln_g=ln_g[None],
                ln_b=ln_b[None], wsm=wsm, wsm_t=wsm.transpose(0, 2, 1), pw=pw, pw_t=pw.transpose(0, 2, 1),
                ps=pool_scale[None], bias=jnp.repeat(gm_bs.T, HEAD, axis=1), bands=bands, bands_t=bands_t)


def _layer_fwd(xl, mem0, L, next_shards=()):
    S = xl.shape[0]
    proj, gates, h, half_win = _in_proj(xl, L["g"], L["wt"], next_shards[:1])
    kv, mem_n = _mem_kv(mem0, L["mg"], L["wkv"])
    y4, cdf, win, half_small = _abm_fwd(proj, L["ln_g"], L["ln_b"], L["wsm"], L["bias"], L["pw"], L["ps"], kv,
                                        L["bands"], half_win, next_shards[2:])
    o_g, l_g = [], []
    for gi, d in enumerate(DILATIONS):
        if d == 1:
            o, lse = _attn_fwd(proj, CB_Q0, proj, CB_K, proj, CB_CV, S // CHUNK)
        else:
            o, lse = _attn_fwd_dilated(proj, CB_Q0 + gi, CB_K, CB_CV, d)
        o_g.append(o)
        l_g.append(lse)
    (xn, y4, oc, lse, z), small = _merge_fwd(xl, y4, o_g, l_g, proj, gates, L["wb"], L["wout"], next_shards[1:2],
                                             half_small)
    saved = dict(x=xl, proj=proj, gates=gates, h=h, kv=kv, mem_n=mem_n, y4=y4, cdf=cdf, oc=oc, lse=lse, z=z)
    return xn, saved, win + small


def _place_cols(dst, piece, cb):
    return lax.dynamic_update_slice(dst, piece, (0, cb * WIDTH))


def _layer_bwd(dx, mem0, L, sv, later=(), last=False):
    S = dx.shape[0]
    proj = sv["proj"]
    (dy3, doc, delta, dpb, dgm, dwb, dwout), from_sibling = _merge_bwd(
        dx, sv["y4"], sv["oc"], sv["z"], proj, sv["gates"], L["wb"], L["wout"], later)
    pair = _pair_sum(later, from_sibling) if later else ()
    dpb, dm, dlng, dlnb, dws, dbias, dpw, dps, dkv = _abm_bwd(
        proj, sv["cdf"], dy3, L["ln_g"], L["ln_b"], L["wsm"], L["wsm_t"], L["bias"], L["pw"], L["pw_t"], L["ps"], sv["kv"],
        L["bands"], L["bands_t"], dpb)
    assert DILATIONS[0] == 1
    dqs, dk, dv = [], None, None
    for gi, d in enumerate(DILATIONS[1:], start=1):
        dq, dk, dv = _attn_bwd_dilated(proj, CB_Q0 + gi, CB_K, CB_CV, doc, sv["lse"], delta, d, dk, dv)
        dqs.append(dq)
    dpb = _attn_bwd(proj, CB_Q0, proj, CB_K, proj, CB_CV, doc, sv["lse"], delta, S // CHUNK, dqs, dk, dv, dpb)
    dpb = _place_cols(dpb, dm, CB_MQ)
    dwkv, dmg = _mem_bwd(mem0, L["mg"], sv["mem_n"], L["wkv"], dkv)
    ready = _blocked_rest(dict(w_mem_kv=dwkv, w_branch=dwb, w_out=dwout))
    dwin_t, parts_rest, own = _dw_in(sv["h"], dpb, dgm, pair[1:], ready if last else ())
    big = [_to_blocks(dwin_t, "rows")] + ready
    dxi, dng, parts, from_sibling = _dh_bwd(dpb, dgm, L["wt"], sv["x"], L["g"], dx, pair[:1],
                                            big[:1] if last else ())
    parts = parts + parts_rest
    small = dict(norm_g=dng[0], gm_ln_g=dlng[0], gm_ln_b=dlnb[0], gm_ws=dws,
                 gm_bs=_bias_reduce(dbias)[:, :N_HEAD].T, pool_w=dpw, pool_scale=dps[0], mem_norm_g=dmg[0])
    return dxi, big, small, parts, own, from_sibling


BIG = ("w_in", "w_mem_kv", "w_branch", "w_out")


def _blocked_rest(big):
    return [_to_blocks(big["w_mem_kv"], "rows"), _to_blocks(big["w_branch"], "branch"), _to_blocks(big["w_out"], "rows")]


def _full_weights(gathered):
    win_t, wkv, wb, wout = gathered
    return win_t.reshape(D_IN, D_MODEL), wkv.reshape(D_MODEL, 2 * WIDTH), wb, wout.reshape(D_MODEL, D_MODEL)


def kernel(x, mem, norm_g, w_in, gm_ln_g, gm_ln_b, gm_ws, gm_bs, pool_w, pool_scale, mem_norm_g, w_mem_kv, w_branch, w_out, final_norm_g, loss_target, m_norm_g, m_w_in, m_gm_ln_g, m_gm_ln_b, m_gm_ws, m_gm_bs, m_pool_w, m_pool_scale, m_mem_norm_g, m_w_mem_kv, m_w_branch, m_w_out, m_final_norm_g, v_norm_g, v_w_in, v_gm_ln_g, v_gm_ln_b, v_gm_ws, v_gm_bs, v_pool_w, v_pool_scale, v_mem_norm_g, v_w_mem_kv, v_w_branch, v_w_out, v_final_norm_g):
    x0 = x[0]
    mem0 = mem[0]
    tgt = loss_target[0]
    S = x0.shape[0]

    shards = [[w_in[l].T.astype(BF16), w_mem_kv[l].astype(BF16), w_branch[l].astype(BF16).reshape(4 * WIDTH, -1),
               w_out[l].astype(BF16)] for l in range(DEPTH)]
    gathered = _all_gather(shards[0])
    layers, saved = [], []
    xl = x0
    for l in range(DEPTH):
        layers.append(_make_layer(*_full_weights(gathered), norm_g[l], mem_norm_g[l], gm_ln_g[l], gm_ln_b[l],
                                  gm_ws[l], gm_bs[l], pool_w[l], pool_scale[l]))
        xl, sv, gathered = _layer_fwd(xl, mem0, layers[l], shards[l + 1] if l + 1 < DEPTH else ())
        saved.append(sv)

    loss_part, dx, d_final = _loss_head(xl, final_norm_g[None], tgt)
    loss = lax.psum(loss_part[0, 0], ("x", "y", "c"))

    small = {k: [None] * DEPTH for k in SMALL if k != "final_norm_g"}
    parts = [None] * DEPTH
    later = ()
    for l in reversed(range(DEPTH)):
        dx, gb, gs, done, own, from_sibling = _layer_bwd(dx, mem0, layers[l], saved[l], later, last=(l == 0))
        if later:
            parts[l + 1] = done
        later = gb
        for k in gs:
            small[k][l] = gs[k]
    grad_x = dx[None]
    small_tree = {k: jnp.stack(small[k]) for k in small}
    small_tree["final_norm_g"] = d_final[0]
    reduced, win_parts = _all_reduce_small(_pack_small(small_tree).reshape(N_DEV, SMALL_ROWS, 128),
                                           _pair_sum(later[:1], from_sibling))
    parts[0] = win_parts + own

    weights = dict(norm_g=norm_g, w_in=w_in, gm_ln_g=gm_ln_g, gm_ln_b=gm_ln_b, gm_ws=gm_ws, gm_bs=gm_bs,
                   pool_w=pool_w, pool_scale=pool_scale, mem_norm_g=mem_norm_g, w_mem_kv=w_mem_kv,
                   w_branch=w_branch, w_out=w_out, final_norm_g=final_norm_g)
    m_in = dict(norm_g=m_norm_g, w_in=m_w_in, gm_ln_g=m_gm_ln_g, gm_ln_b=m_gm_ln_b, gm_ws=m_gm_ws, gm_bs=m_gm_bs,
                pool_w=m_pool_w, pool_scale=m_pool_scale, mem_norm_g=m_mem_norm_g, w_mem_kv=m_w_mem_kv,
                w_branch=m_w_branch, w_out=m_w_out, final_norm_g=m_final_norm_g)
    v_in = dict(norm_g=v_norm_g, w_in=v_w_in, gm_ln_g=v_gm_ln_g, gm_ln_b=v_gm_ln_b, gm_ws=v_gm_ws, gm_bs=v_gm_bs,
                pool_w=v_pool_w, pool_scale=v_pool_scale, mem_norm_g=v_mem_norm_g, w_mem_kv=v_w_mem_kv,
                w_branch=v_w_branch, w_out=v_w_out, final_norm_g=v_final_norm_g)
    res = {}
    def view(k, arr):
        return arr.transpose(0, 2, 1) if k == "w_in" else arr

    for a, k in enumerate(BIG):
        shape = view(k, weights[k]).shape
        by_layer = [parts[l][a] for l in range(DEPTH)]
        lrc = (DEPTH,) + by_layer[0].shape[1:]
        outs = _adamw_layers(by_layer, view(k, weights[k]).reshape(lrc), view(k, m_in[k]).reshape(lrc),
                             view(k, v_in[k]).reshape(lrc), "adamw_" + k)
        res[k] = [view(k, o.reshape(shape)) for o in outs]
    outs = _adamw(reduced.reshape(1, N_DEV * SMALL_ROWS, 128), _pack_small(weights), _pack_small(m_in),
                  _pack_small(v_in), "adamw_small")
    unpacked = [_unpack_small(o, weights) for o in outs]
    for k in SMALL:
        res[k] = [u[k] for u in unpacked]

    order = ("norm_g", "w_in", "gm_ln_g", "gm_ln_b", "gm_ws", "gm_bs", "pool_w", "pool_scale", "mem_norm_g",
             "w_mem_kv", "w_branch", "w_out", "final_norm_g")
    return (loss, grad_x, *[res[k][0] for k in order], *[res[k][1] for k in order],
            *[res[k][2] for k in order], *[res[k][3] for k in order])
```

```python
import functools
import math

import numpy as np

import jax
import jax.numpy as jnp
from jax import lax
from jax.experimental import pallas as pl
from jax.experimental.pallas import tpu as pltpu

F32 = jnp.float32
BF16 = jnp.bfloat16

D_MODEL = 1024
DEPTH = 4
WIDTH = 512
D_IN = 10752
HEAD = 128
N_HEAD = 4
CHUNK = 128
MEM_LEN = 256
POOL_WINDOWS = (2, 4, 8, 16)
DILATIONS = (1, 4, 16)
EPS = 1e-6
NEG = -1e30
ATT_SCALE = HEAD ** -0.5
N_DEV = 8
N_CHIP = 4

D_BRANCHES = 6656
D_GATES = D_IN - D_BRANCHES
CB_U, CB_V, CB_AGATE, CB_PIN, CB_PGATE = 0, 1, 2, 3, 4
CB_Q0, CB_K, CB_CV, CB_CGATE, CB_MQ, CB_MGATE = 5, 8, 9, 10, 11, 12
D_REST = CB_MQ * WIDTH

ADAM_LR = 0.001
ADAM_B1 = 0.9
ADAM_B2 = 0.999
ADAM_EPS = 1e-08
ADAM_WD = 0.01
ADAM_STEP = 10

VMEM_LIMIT = 56 * 1024 * 1024
MESH = pl.DeviceIdType.MESH
ANY = pl.BlockSpec(memory_space=pl.ANY)

NT = (((1,), (1,)), ((), ()))
TN = (((0,), (0,)), ((), ()))


def _dot(a, b):
    return jnp.dot(a, b, preferred_element_type=F32)


def _dot_nt(a, b):
    return lax.dot_general(a, b, NT, preferred_element_type=F32)


def _dot_tn(a, b):
    return lax.dot_general(a, b, TN, preferred_element_type=F32)


def _sigmoid(x):
    return 0.5 * jnp.tanh(0.5 * x) + 0.5


def _silu(x):
    return x * _sigmoid(x)


def _silu_and_grad(x):
    s = _sigmoid(x)
    return x * s, s * (1.0 + x * (1.0 - s))


def _normal_cdf(x):
    return 0.5 * (1.0 + lax.erf(x * (2.0 ** -0.5)))


def _gelu_and_grad(x, cdf):
    return x * cdf, cdf + x * jnp.exp(-0.5 * x * x) * (1.0 / math.sqrt(2.0 * math.pi))


def _col(blk, h):
    lane = lax.broadcasted_iota(jnp.int32, blk.shape, 1)
    return jnp.sum(jnp.where(lane == h, blk, 0.0), axis=1, keepdims=True)


def _put_cols(cols):
    rows = cols[0].shape[0]
    lane = lax.broadcasted_iota(jnp.int32, (rows, 128), 1)
    out = jnp.zeros((rows, 128), F32)
    for h, cv in enumerate(cols):
        out = jnp.where(lane == h, cv, out)
    return out


def _params(sem, vmem=VMEM_LIMIT):
    return pltpu.CompilerParams(dimension_semantics=sem, vmem_limit_bytes=vmem)


def _full(shape):
    nd = len(shape)
    return pl.BlockSpec(shape, lambda *_: (0,) * nd)


def _resident(shape):
    nd = len(shape)
    return pl.BlockSpec(shape, lambda *_: (0,) * nd, pipeline_mode=pl.Buffered(1))


def _rows(tm, width, cb=0):
    return pl.BlockSpec((tm, width), lambda i: (i, cb))


GATE_TILE = 512


def _in_proj(x, g, wt, shards=()):
    S = x.shape[0]
    tm, tnb, tng = 1024, D_BRANCHES // 4, D_GATES // 4
    njb, njg = 4, 4
    n = len(shards)
    ni, nj = S // tm, njb + njg

    def body(x_ref, g_ref, wbr_ref, wg0_ref, wg1_ref, *rest):
        ins, (proj_ref, gates_ref, h_ref), outs = rest[:n], rest[n:n + 3], rest[n + 3:2 * n + 3]
        hs, sems = rest[2 * n + 3], rest[2 * n + 4:]
        i, j = pl.program_id(0), pl.program_id(1)

        if n:
            @pl.when(jnp.logical_and(i == 0, j == 0))
            def _():
                _comm_start(_ag_first(ins, outs, *sems))

        @pl.when(j == 0)
        def _():
            xf = x_ref[...]
            r = lax.rsqrt(jnp.mean(xf * xf, axis=-1, keepdims=True) + EPS)
            h = (xf * r * g_ref[...]).astype(BF16)
            hs[...] = h
            h_ref[...] = h

        @pl.when(j < njb)
        def _():
            proj_ref[...] = _dot_nt(hs[...], wbr_ref[...]).astype(BF16)

        @pl.when(j >= njb)
        def _():
            gates_ref[:, :tng // 2] = _dot_nt(hs[...], wg0_ref[...]).astype(BF16)
            gates_ref[:, tng // 2:] = _dot_nt(hs[...], wg1_ref[...]).astype(BF16)

        if n:
            @pl.when(jnp.logical_and(i == ni - 1, j == nj - 1))
            def _():
                _comm_wait(_ag_first(ins, outs, *sems))

    def first(j):
        return jnp.minimum(j, njb - 1)

    def second(j):
        return jnp.maximum(j - njb, 0)

    def gate_rows(half):
        return pl.BlockSpec((tng // 2, D_MODEL), lambda i, j: (D_BRANCHES // (tng // 2) + 2 * second(j) + half, 0))

    res = pl.pallas_call(
        body, name="in_proj_gather" if n else "in_proj",
        grid=(ni, nj),
        in_specs=[pl.BlockSpec((tm, D_MODEL), lambda i, j: (i, 0)),
                  pl.BlockSpec((1, D_MODEL), lambda i, j: (0, 0)),
                  pl.BlockSpec((tnb, D_MODEL), lambda i, j: (first(j), 0)), gate_rows(0), gate_rows(1)]
                 + [ANY] * n,
        out_specs=[pl.BlockSpec((tm, tnb), lambda i, j: (i, first(j))),
                   pl.BlockSpec((tm, tng), lambda i, j: (i, second(j))),
                   pl.BlockSpec((tm, D_MODEL), lambda i, j: (i, 0))] + [ANY] * n,
        out_shape=[jax.ShapeDtypeStruct((S, D_BRANCHES), BF16), jax.ShapeDtypeStruct((S, D_GATES), BF16),
                   jax.ShapeDtypeStruct((S, D_MODEL), BF16)]
                  + [jax.ShapeDtypeStruct((N_DEV,) + s.shape, s.dtype) for s in shards],
        scratch_shapes=[pltpu.VMEM((tm, D_MODEL), BF16)] + (_dma_sems(4 * n, 4 * n, n) if n else []),
        compiler_params=_params(("arbitrary", "arbitrary")),
    )(x, g, wt, wt, wt, *shards)
    return res[0], res[1], res[2], list(res[3:])


def _mem_kv(mem, g, w):
    M = mem.shape[0]

    def body(m_ref, g_ref, w_ref, kv_ref, mn_ref):
        xf = m_ref[...]
        r = lax.rsqrt(jnp.mean(xf * xf, axis=-1, keepdims=True) + EPS)
        mn = (xf * r * g_ref[...]).astype(BF16)
        mn_ref[...] = mn
        kv_ref[...] = _dot(mn, w_ref[...]).astype(BF16)

    return pl.pallas_call(
        body, name="mem_kv",
        out_shape=[jax.ShapeDtypeStruct((M, 2 * WIDTH), BF16), jax.ShapeDtypeStruct((M, D_MODEL), BF16)],
        compiler_params=pltpu.CompilerParams(vmem_limit_bytes=VMEM_LIMIT),
    )(mem, g, w)


def _band_constants():
    t = np.arange(CHUNK)[:, None]
    s = np.arange(CHUNK)[None, :]
    bands = np.stack([np.stack([(t - s >= 0) & (t - s < win), s > t + CHUNK - win]) for win in POOL_WINDOWS])
    bands = bands.astype(np.float32)
    return jnp.asarray(bands, BF16), jnp.asarray(bands.transpose(0, 1, 3, 2), BF16)


def _inv_count(first_row, win):
    t = first_row + lax.broadcasted_iota(jnp.int32, (CHUNK, 1), 0)
    return 1.0 / jnp.minimum(t + 1, win).astype(F32)


def _layer_norm_fwd(v):
    mu = jnp.mean(v, axis=-1, keepdims=True)
    vc = v - mu
    var = jnp.mean(vc * vc, axis=-1, keepdims=True)
    rstd = lax.rsqrt(var + EPS)
    return vc * rstd, rstd


def _mem_softmax(q, kmem):
    s = _dot_nt(q, kmem) * ATT_SCALE
    m = jnp.max(s, axis=-1, keepdims=True)
    e = jnp.exp(s - m)
    return e * (1.0 / jnp.sum(e, axis=-1, keepdims=True))


def _abm_fwd(proj, ln_g, ln_b, wsm, bias_full, pool_w, pool_scale, kv, bands, gathered=(), shards=()):
    S = proj.shape[0]
    tm = 512
    nchunk = tm // CHUNK
    n, m = len(gathered), len(shards)
    nsteps = S // tm

    def body(u_ref, v_ref, ag_ref, p_ref, ph_ref, pg_ref, mq_ref, mg_ref, lng_ref, lnb_ref, wsm_ref, bias_ref,
             pw_ref, ps_ref, kv_ref, band_ref, *rest):
        s_in = rest[n:n + m]
        y_ref, cdf_ref = rest[n + m], rest[n + m + 1]
        bufs, s_out = rest[n + m + 2:2 * n + m + 2], rest[2 * n + m + 2:2 * (n + m) + 2]
        mix, sems = rest[2 * (n + m) + 2], rest[2 * (n + m) + 3:]
        second_sems, first_sems = (sems[:2] if n else ()), sems[2 if n else 0:]
        i = pl.program_id(0)

        if n or m:
            @pl.when(i == 0)
            def _():
                if n:
                    _comm_start(_ag_second(bufs, *second_sems))
                if m:
                    _comm_start(_ag_first(s_in, s_out, *first_sems))

        au, av = u_ref[...].astype(F32), v_ref[...].astype(F32)
        cdf_u, cdf_v = _normal_cdf(au), _normal_cdf(av)
        cdf_ref[0] = cdf_u.astype(BF16)
        cdf_ref[1] = cdf_v.astype(BF16)
        u, v = au * cdf_u, av * cdf_v
        vhat, _ = _layer_norm_fwd(v)
        vln = (vhat * lng_ref[...] + lnb_ref[...]).astype(BF16)
        for c in range(nchunk):
            for h in range(N_HEAD):
                rs, cs = slice(c * CHUNK, (c + 1) * CHUNK), slice(h * HEAD, (h + 1) * HEAD)
                mix[rs, cs] = _dot(wsm_ref[h], vln[rs, cs]) + bias_ref[:, cs]
        y_ref[0] = (u * mix[...] * _silu(ag_ref[...].astype(F32))).astype(BF16)
        halo_ok = (i > 0).astype(F32)
        for c in range(nchunk):
            rs = slice(c * CHUNK, (c + 1) * CHUNK)
            for g, win in enumerate(POOL_WINDOWS):
                cs = slice(g * HEAD, (g + 1) * HEAD)
                cur = p_ref[rs, cs]
                if c == 0:
                    prev = (ph_ref[:, cs].astype(F32) * halo_ok).astype(BF16)
                else:
                    prev = p_ref[(c - 1) * CHUNK:c * CHUNK, cs]
                sums = _dot(band_ref[g, 0], cur) + _dot(band_ref[g, 1], prev)
                dm = sums * _inv_count(i * tm + c * CHUNK, win) - cur.astype(F32)
                mix[rs, cs] = _dot(dm.astype(BF16), pw_ref[g])
        y_ref[1] = (mix[...] * ps_ref[...] * _silu(pg_ref[...].astype(F32))).astype(BF16)
        for h in range(N_HEAD):
            cs = slice(h * HEAD, (h + 1) * HEAD)
            p = _mem_softmax(mq_ref[:, cs], kv_ref[:, cs])
            mix[:, cs] = _dot(p.astype(BF16), kv_ref[:, WIDTH + h * HEAD:WIDTH + (h + 1) * HEAD])
        y_ref[2] = (mix[...] * _silu(mg_ref[...].astype(F32))).astype(BF16)

        if n or m:
            @pl.when(i == nsteps - 1)
            def _():
                if n:
                    _comm_wait(_ag_second(bufs, *second_sems))
                if m:
                    _comm_wait(_ag_first(s_in, s_out, *first_sems))

    blk = tm // CHUNK
    res = pl.pallas_call(
        body, name="abm_fwd_gather" if n or m else "abm_fwd",
        grid=(nsteps,),
        in_specs=[_rows(tm, WIDTH, CB_U), _rows(tm, WIDTH, CB_V), _rows(tm, WIDTH, CB_AGATE),
                  _rows(tm, WIDTH, CB_PIN),
                  pl.BlockSpec((CHUNK, WIDTH), lambda i: (jnp.maximum(i * blk - 1, 0), CB_PIN)),
                  _rows(tm, WIDTH, CB_PGATE), _rows(tm, WIDTH, CB_MQ), _rows(tm, WIDTH, CB_MGATE),
                  _full((1, WIDTH)), _full((1, WIDTH)), _full((N_HEAD, CHUNK, CHUNK)), _full((CHUNK, WIDTH)),
                  _full((4, HEAD, HEAD)), _full((1, WIDTH)), _full((MEM_LEN, 2 * WIDTH)),
                  _full((4, 2, CHUNK, CHUNK))] + [ANY] * (n + m),
        out_specs=[pl.BlockSpec((3, tm, WIDTH), lambda i: (0, i, 0)), pl.BlockSpec((2, tm, WIDTH), lambda i: (0, i, 0))]
                  + [ANY] * (n + m),
        out_shape=[jax.ShapeDtypeStruct((4, S, WIDTH), BF16),
                   jax.ShapeDtypeStruct((2, S, WIDTH), BF16)]
                  + [jax.ShapeDtypeStruct(b.shape, b.dtype) for b in gathered]
                  + [jax.ShapeDtypeStruct((N_DEV,) + s.shape, s.dtype) for s in shards],
        input_output_aliases={16 + a: 2 + a for a in range(n)},
        scratch_shapes=[pltpu.VMEM((tm, WIDTH), F32)] + (_dma_sems(3 * n, 3 * n) if n else [])
                       + (_dma_sems(4 * m, 4 * m, m) if m else []),
        compiler_params=_params(("arbitrary",)),
    )(proj, proj, proj, proj, proj, proj, proj, proj, ln_g, ln_b, wsm, bias_full, pool_w, pool_scale, kv, bands,
      *gathered, *shards)
    return res[0], res[1], list(res[2:2 + n]), list(res[2 + n:])


ATT_TILE = 512


def _attn_fwd(q, qcb, k, kcb, v, vcb, bps):
    S = q.shape[0]
    tm = ATT_TILE
    nb = tm // CHUNK

    nblocks = nb * N_HEAD

    def body(q_ref, k_ref, v_ref, kh_ref, vh_ref, o_ref, l_ref, sc_s, sp_s, pc_s, pp_s):
        i = pl.program_id(0)

        def prev_kv(n, cs):
            if n == 0:
                return kh_ref[:, cs], vh_ref[:, cs]
            ps = slice((n - 1) * CHUNK, n * CHUNK)
            return k_ref[ps, cs], v_ref[ps, cs]

        pens = []
        for n in range(nb):
            rs = slice(n * CHUNK, (n + 1) * CHUNK)
            pens.append(jnp.full((N_HEAD * CHUNK, 1), jnp.where((i * nb + n) % bps != 0, 0.0, NEG), F32))
            for h in range(N_HEAD):
                cs = slice(h * HEAD, (h + 1) * HEAD)
                bs = slice((n * N_HEAD + h) * CHUNK, (n * N_HEAD + h + 1) * CHUNK)
                qh = q_ref[rs, cs]
                sc_s[bs, :] = _dot_nt(qh, k_ref[rs, cs])
                sp_s[bs, :] = _dot_nt(qh, prev_kv(n, cs)[0])
        row = lax.broadcasted_iota(jnp.int32, (nblocks * CHUNK, CHUNK), 0) & (CHUNK - 1)
        col = lax.broadcasted_iota(jnp.int32, (nblocks * CHUNK, CHUNK), 1)
        sc = jnp.where(col <= row, sc_s[...] * ATT_SCALE, NEG)
        sp = jnp.where(col >= row, sp_s[...] * ATT_SCALE, NEG) + jnp.concatenate(pens, axis=0)
        m = jnp.maximum(jnp.max(sc, axis=-1, keepdims=True), jnp.max(sp, axis=-1, keepdims=True))
        ec = jnp.exp(sc - m)
        ep = jnp.exp(sp - m)
        den = jnp.sum(ec, axis=-1, keepdims=True) + jnp.sum(ep, axis=-1, keepdims=True)
        inv = 1.0 / den
        pc_s[...] = (ec * inv).astype(BF16)
        pp_s[...] = (ep * inv).astype(BF16)
        lse = m + jnp.log(den)
        for n in range(nb):
            rs = slice(n * CHUNK, (n + 1) * CHUNK)
            for h in range(N_HEAD):
                cs = slice(h * HEAD, (h + 1) * HEAD)
                bs = slice((n * N_HEAD + h) * CHUNK, (n * N_HEAD + h + 1) * CHUNK)
                o = _dot(pc_s[bs, :], v_ref[rs, cs]) + _dot(pp_s[bs, :], prev_kv(n, cs)[1])
                o_ref[rs, cs] = o.astype(BF16)
            l_ref[rs, :] = _put_cols([lse[(n * N_HEAD + h) * CHUNK:(n * N_HEAD + h + 1) * CHUNK]
                                      for h in range(N_HEAD)])

    def halo(cb):
        return pl.BlockSpec((CHUNK, WIDTH), lambda i: (jnp.maximum(i * nb - 1, 0), cb))

    return pl.pallas_call(
        body, name=f"attn_fwd_{bps}",
        grid=(S // tm,),
        in_specs=[_rows(tm, WIDTH, qcb), _rows(tm, WIDTH, kcb), _rows(tm, WIDTH, vcb), halo(kcb), halo(vcb)],
        out_specs=[_rows(tm, WIDTH), _rows(tm, 128)],
        out_shape=[jax.ShapeDtypeStruct((S, WIDTH), BF16), jax.ShapeDtypeStruct((S, 128), F32)],
        scratch_shapes=[pltpu.VMEM((nblocks * CHUNK, CHUNK), F32), pltpu.VMEM((nblocks * CHUNK, CHUNK), F32),
                        pltpu.VMEM((nblocks * CHUNK, CHUNK), BF16), pltpu.VMEM((nblocks * CHUNK, CHUNK), BF16)],
        compiler_params=_params(("parallel",)),
    )(q, k, v, k, v)


def _gate_specs(tm):
    return [pl.BlockSpec((tm, D_MODEL), lambda i, b=b: (i, b)) for b in range(4)]


Y_SLOT = (0, 1, 3, 2)


def _w_branch_from_blocks(src_ref, dst):
    cols = D_MODEL // N_DEV
    for k in range(N_DEV):
        for b in range(4):
            dst[b, :, k * cols:(k + 1) * cols] = src_ref[k, b * WIDTH:(b + 1) * WIDTH, :]


def _merge_fwd(x, y4, o_g, l_g, proj, gates, wb, wout, shards=(), half=()):
    S = x.shape[0]
    tm = 512
    n, nh = len(shards), len(half)
    nsteps = S // tm
    forward_at = nsteps // 2

    def body(x_ref, y_ref, o0, o1, o2, l0, l1, l2, cg_ref, *rest):
        gm = rest[:4]
        wb_ref, wo_ref = rest[4:6]
        s_in = rest[6:6 + n]
        rest = rest[6 + n + nh:]
        xn_ref, yc_ref, oc_ref, lse_ref, z_ref = rest[:5]
        s_out, bufs = rest[5:5 + n], rest[5:5 + n + nh]
        ocs, wbs, sems = rest[5 + n + nh], rest[6 + n + nh], rest[7 + n + nh:]
        i = pl.program_id(0)

        @pl.when(i == 0)
        def _():
            if n:
                _comm_start(_ag_first(s_in, s_out, *sems[:3]))
            _w_branch_from_blocks(wb_ref, wbs)

        if n:
            @pl.when(i == forward_at)
            def _():
                incoming = _ag_first(s_in, s_out, *sems[:3])[2]
                for a in range(n):
                    for k in range(1, 4):
                        incoming[4 * a + k].wait_recv()
                _comm_start(_ag_second(bufs, *sems[3:]))

        lcols = []
        for h in range(N_HEAD):
            cs = slice(h * HEAD, (h + 1) * HEAD)
            ls = [_col(l[...], h) for l in (l0, l1, l2)]
            m = jnp.maximum(jnp.maximum(ls[0], ls[1]), ls[2])
            tot = jnp.exp(ls[0] - m) + jnp.exp(ls[1] - m) + jnp.exp(ls[2] - m)
            lse = m + jnp.log(tot)
            ocs[:, cs] = sum(jnp.exp(lg - lse) * o[:, cs].astype(F32) for lg, o in zip(ls, (o0, o1, o2)))
            lcols.append(lse)
        lse_ref[...] = _put_cols(lcols)
        oc = ocs[...]
        oc_ref[...] = oc.astype(BF16)
        yc = (oc * _silu(cg_ref[...].astype(F32))).astype(BF16)
        yc_ref[...] = yc
        ys = (y_ref[0], y_ref[1], yc, y_ref[2])
        z = jnp.zeros((tm, D_MODEL), F32)
        for b in range(4):
            z = z + _sigmoid(gm[b][...].astype(F32)) * _dot(ys[b], wbs[b])
        zb = z.astype(BF16)
        z_ref[...] = zb
        xn_ref[...] = x_ref[...] + _dot(zb, wo_ref[...])

        if n:
            @pl.when(i == nsteps - 1)
            def _():
                local, out, incoming = _ag_first(s_in, s_out, *sems[:3])
                for a in range(n):
                    incoming[4 * a].wait_recv()
                _comm_wait(_ag_second(bufs, *sems[3:]))
                for cp in out:
                    cp.wait_send()
                for cp in local:
                    cp.wait()

    res = pl.pallas_call(
        body, name="merge_fwd_gather" if n else "merge_fwd",
        grid=(nsteps,),
        in_specs=[_rows(tm, D_MODEL), pl.BlockSpec((3, tm, WIDTH), lambda i: (0, i, 0)),
                  _rows(tm, WIDTH), _rows(tm, WIDTH), _rows(tm, WIDTH),
                  _rows(tm, 128), _rows(tm, 128), _rows(tm, 128),
                  _rows(tm, WIDTH, CB_CGATE)] + _gate_specs(tm)
                 + [_resident(wb.shape), _resident((D_MODEL, D_MODEL))] + [ANY] * (n + nh),
        out_specs=[_rows(tm, D_MODEL), pl.BlockSpec((None, tm, WIDTH), lambda i: (Y_SLOT[2], i, 0)),
                   _rows(tm, WIDTH), _rows(tm, 128), _rows(tm, D_MODEL)] + [ANY] * (n + nh),
        out_shape=[jax.ShapeDtypeStruct((S, D_MODEL), F32), jax.ShapeDtypeStruct(y4.shape, BF16),
                   jax.ShapeDtypeStruct((S, WIDTH), BF16), jax.ShapeDtypeStruct((S, 128), F32),
                   jax.ShapeDtypeStruct((S, D_MODEL), BF16)]
                  + [jax.ShapeDtypeStruct((N_DEV,) + s.shape, s.dtype) for s in shards]
                  + [jax.ShapeDtypeStruct(b.shape, b.dtype) for b in half],
        input_output_aliases={1: 1, **{15 + n + a: 5 + n + a for a in range(nh)}},
        scratch_shapes=[pltpu.VMEM((tm, WIDTH), F32), pltpu.VMEM((4, WIDTH, D_MODEL), BF16)]
                       + (_dma_sems(4 * n, 4 * n, n, 3 * (n + nh), 3 * (n + nh)) if n else []),
        compiler_params=_params(("arbitrary",)),
    )(x, y4, *o_g, *l_g, proj, *([gates] * 4), wb, wout, *shards, *half)
    return res[:5], list(res[5:])


def _loss_head(x, g, tgt):
    S = x.shape[0]
    tm = 512

    def body(x_ref, g_ref, t_ref, loss_ref, dx_ref, dg_ref):
        @pl.when(pl.program_id(0) == 0)
        def _():
            loss_ref[...] = jnp.zeros_like(loss_ref)
            dg_ref[...] = jnp.zeros_like(dg_ref)

        xf = x_ref[...]
        r = lax.rsqrt(jnp.mean(xf * xf, axis=-1, keepdims=True) + EPS)
        xhat = xf * r
        gv = g_ref[...]
        err = xhat * gv - t_ref[...]
        e2 = jnp.sum(err * err, axis=-1, keepdims=True)
        loss_ref[...] += (0.5 / D_MODEL) * jnp.sum(e2, axis=0, keepdims=True)
        dy = err * (1.0 / D_MODEL)
        dg_ref[...] += jnp.sum(dy * xhat, axis=0, keepdims=True)
        dxh = dy * gv
        dx_ref[...] = r * (dxh - xhat * jnp.mean(dxh * xhat, axis=-1, keepdims=True))

    return pl.pallas_call(
        body, name="loss_head",
        grid=(S // tm,),
        in_specs=[_rows(tm, D_MODEL), _full((1, D_MODEL)), _rows(tm, D_MODEL)],
        out_specs=[_full((1, 128)), _rows(tm, D_MODEL), _full((1, D_MODEL))],
        out_shape=[jax.ShapeDtypeStruct((1, 128), F32), jax.ShapeDtypeStruct((S, D_MODEL), F32),
                   jax.ShapeDtypeStruct((1, D_MODEL), F32)],
        compiler_params=_params(("arbitrary",)),
    )(x, g, tgt)


def _merge_bwd(dxo, y4, oc, z, proj, gates, wb, wout, grads=()):
    S = dxo.shape[0]
    tm = 256
    n = len(grads)
    nsteps = S // tm

    def body(dx_ref, y_ref, oc_ref, z_ref, cg_ref, *rest):
        gm = rest[:4]
        wb_ref, wo_ref = rest[4:6]
        g_in = rest[6:6 + n]
        dy_ref, doc_ref, delta_ref, dcg_ref, dgm_ref, dwb_ref, dwo_ref = rest[6 + n:13 + n]
        g_out = rest[13 + n:13 + 2 * n]
        acc_b, acc_o, wbs = rest[13 + 2 * n:16 + 2 * n]
        sems = rest[16 + 2 * n:]
        i = pl.program_id(0)

        @pl.when(i == 0)
        def _():
            acc_b[...] = jnp.zeros_like(acc_b)
            acc_o[...] = jnp.zeros_like(acc_o)
            _w_branch_from_blocks(wb_ref, wbs)
            if n:
                _comm_start(_rs_first(g_in, g_out, *sems))

        dxb = dx_ref[...].astype(BF16)
        acc_o[...] += _dot_tn(z_ref[...], dxb)
        dz = _dot_nt(dxb, wo_ref[...])
        for b in range(4):
            gate = _sigmoid(gm[b][...].astype(F32))
            yb = y_ref[Y_SLOT[b]]
            t = _dot(yb, wbs[b])
            dgm_ref[:, b * D_MODEL:(b + 1) * D_MODEL] = (dz * t * gate * (1.0 - gate)).astype(BF16)
            dt = (dz * gate).astype(BF16)
            acc_b[b] += _dot_tn(yb, dt)
            dyb = _dot_nt(dt, wbs[b])
            if b == 2:
                cg = cg_ref[...].astype(F32)
                oc = oc_ref[...].astype(F32)
                scg, dscg = _silu_and_grad(cg)
                doc = dyb * scg
                dcg_ref[...] = (dyb * oc * dscg).astype(BF16)
                doc_ref[...] = doc.astype(BF16)
                prod = doc * oc
                delta_ref[...] = _put_cols([jnp.sum(prod[:, h * HEAD:(h + 1) * HEAD], axis=1, keepdims=True)
                                            for h in range(N_HEAD)])
            else:
                dy_ref[b if b < 2 else 2] = dyb.astype(BF16)

        @pl.when(i == nsteps - 1)
        def _():
            for k in range(N_DEV):
                dwb_ref[k] = acc_b[:, :, k * (D_MODEL // N_DEV):(k + 1) * (D_MODEL // N_DEV)].astype(BF16)
            dwo_ref[...] = acc_o[...].astype(BF16)
            if n:
                _comm_wait(_rs_first(g_in, g_out, *sems))

    res = pl.pallas_call(
        body, name="merge_bwd_scatter" if n else "merge_bwd",
        grid=(nsteps,),
        in_specs=[_rows(tm, D_MODEL), pl.BlockSpec((4, tm, WIDTH), lambda i: (0, i, 0)),
                  _rows(tm, WIDTH), _rows(tm, D_MODEL), _rows(tm, WIDTH, CB_CGATE)] + _gate_specs(tm)
                 + [_resident(wb.shape), _resident((D_MODEL, D_MODEL))] + [ANY] * n,
        out_specs=[pl.BlockSpec((3, tm, WIDTH), lambda i: (0, i, 0)), _rows(tm, WIDTH), _rows(tm, 128),
                   _rows(tm, WIDTH, CB_CGATE), _rows(tm, 4 * D_MODEL), _full((N_DEV, 4, WIDTH, D_MODEL // N_DEV)),
                   _full((D_MODEL, D_MODEL))]
                  + [ANY] * n,
        out_shape=[jax.ShapeDtypeStruct((3, S, WIDTH), BF16), jax.ShapeDtypeStruct((S, WIDTH), BF16),
                   jax.ShapeDtypeStruct((S, 128), F32), jax.ShapeDtypeStruct((S, D_BRANCHES), BF16),
                   jax.ShapeDtypeStruct((S, 4 * D_MODEL), BF16),
                   jax.ShapeDtypeStruct((N_DEV, 4, WIDTH, D_MODEL // N_DEV), BF16),
                   jax.ShapeDtypeStruct((D_MODEL, D_MODEL), BF16)]
                  + [jax.ShapeDtypeStruct(g.shape[:1] + g.shape[2:], g.dtype) for g in grads],
        scratch_shapes=[pltpu.VMEM((4, WIDTH, D_MODEL), F32), pltpu.VMEM((D_MODEL, D_MODEL), F32),
                        pltpu.VMEM((4, WIDTH, D_MODEL), BF16)]
                       + (_dma_sems(N_CHIP * n, N_CHIP * n) if n else []),
        compiler_params=_params(("arbitrary",)),
    )(dxo, y4, oc, z, proj, *([gates] * 4), wb, wout, *grads)
    return res[:7], list(res[7:])


def _attn_bwd(q, qcb, k, kcb, v, vcb, do, lse, delta, bps, dq_others, dk_in, dv_in, dst):
    assert (qcb, kcb, vcb) == (CB_Q0, CB_K, CB_CV) and CB_Q0 % 5 == 0
    S = q.shape[0]
    tm = ATT_TILE
    nb = tm // CHUNK
    nblk = S // CHUNK

    ncur = nb * N_HEAD
    nprev = (nb + 1) * N_HEAD

    def body(q_ref, k_ref, v_ref, do_ref, l_ref, d_ref, kh_ref, vh_ref, qn_ref, don_ref, ln_ref, dn_ref,
             dq1_ref, dq2_ref, dki_ref, dvi_ref, dst_ref,
             out_ref, sc_s, sp_s, dpc_s, dpp_s, pc_s, pp_s, dsc_s, dsp_s):
        i = pl.program_id(0)
        out_ref[:, WIDTH:2 * WIDTH] = dq1_ref[...]
        out_ref[:, 2 * WIDTH:3 * WIDTH] = dq2_ref[...]

        def rows_of(n):
            if n < nb:
                rs = slice(n * CHUNK, (n + 1) * CHUNK)
                return rs, q_ref, do_ref, l_ref, d_ref
            return slice(0, CHUNK), qn_ref, don_ref, ln_ref, dn_ref

        def prev_kv(n, cs):
            if n == 0:
                return kh_ref[:, cs], vh_ref[:, cs]
            ps = slice((n - 1) * CHUNK, n * CHUNK)
            return k_ref[ps, cs], v_ref[ps, cs]

        def blk(n, h):
            return slice((n * N_HEAD + h) * CHUNK, (n * N_HEAD + h + 1) * CHUNK)

        pens, lses, deltas = [], [], []
        for n in range(nb + 1):
            rs, qr, dor, lr, dr = rows_of(n)
            gb = i * nb + n
            pen = jnp.where(gb % bps != 0, 0.0, NEG)
            if n == nb:
                pen = pen + jnp.where(gb < nblk, 0.0, NEG)
            pens.append(jnp.full((N_HEAD * CHUNK, 1), pen, F32))
            lblk, dblk = lr[rs, :], dr[rs, :]
            for h in range(N_HEAD):
                cs = slice(h * HEAD, (h + 1) * HEAD)
                qh, doh = qr[rs, cs], dor[rs, cs]
                lses.append(_col(lblk, h))
                deltas.append(_col(dblk, h))
                kp, vp = prev_kv(n, cs)
                sp_s[blk(n, h), :] = _dot_nt(qh, kp)
                dpp_s[blk(n, h), :] = _dot_nt(doh, vp)
                if n < nb:
                    sc_s[blk(n, h), :] = _dot_nt(qh, k_ref[rs, cs])
                    dpc_s[blk(n, h), :] = _dot_nt(doh, v_ref[rs, cs])
        lse = jnp.concatenate(lses, axis=0)
        delta = jnp.concatenate(deltas, axis=0)
        row = lax.broadcasted_iota(jnp.int32, (nprev * CHUNK, CHUNK), 0) & (CHUNK - 1)
        col = lax.broadcasted_iota(jnp.int32, (nprev * CHUNK, CHUNK), 1)
        sp = jnp.where(col >= row, sp_s[...] * ATT_SCALE, NEG) + jnp.concatenate(pens, axis=0)
        pp = jnp.exp(sp - lse)
        pp_s[...] = pp.astype(BF16)
        dsp_s[...] = (pp * (dpp_s[...] - delta)).astype(BF16)
        nc = ncur * CHUNK
        sc = jnp.where(col[:nc] <= row[:nc], sc_s[...] * ATT_SCALE, NEG)
        pc = jnp.exp(sc - lse[:nc])
        pc_s[...] = pc.astype(BF16)
        dsc_s[...] = (pc * (dpc_s[...] - delta[:nc])).astype(BF16)
        for n in range(nb):
            rs, qr, dor, _, _ = rows_of(n)
            rn, qnr, donr, _, _ = rows_of(n + 1)
            for h in range(N_HEAD):
                cs = slice(h * HEAD, (h + 1) * HEAD)
                kp, _ = prev_kv(n, cs)
                dq = _dot(dsc_s[blk(n, h), :], k_ref[rs, cs]) + _dot(dsp_s[blk(n, h), :], kp)
                out_ref[rs, cs] = (dq * ATT_SCALE).astype(BF16)
                dk = _dot_tn(dsc_s[blk(n, h), :], qr[rs, cs]) + _dot_tn(dsp_s[blk(n + 1, h), :], qnr[rn, cs])
                kcs = slice(3 * WIDTH + h * HEAD, 3 * WIDTH + (h + 1) * HEAD)
                out_ref[rs, kcs] = (dk * ATT_SCALE + dki_ref[rs, cs].astype(F32)).astype(BF16)
                dv = _dot_tn(pc_s[blk(n, h), :], dor[rs, cs]) + _dot_tn(pp_s[blk(n + 1, h), :], donr[rn, cs])
                vcs = slice(4 * WIDTH + h * HEAD, 4 * WIDTH + (h + 1) * HEAD)
                out_ref[rs, vcs] = (dv + dvi_ref[rs, cs].astype(F32)).astype(BF16)

    def prev_halo(cb):
        return pl.BlockSpec((CHUNK, WIDTH), lambda i: (jnp.maximum(i * nb - 1, 0), cb))

    def next_halo(width, cb=0):
        return pl.BlockSpec((CHUNK, width), lambda i: (jnp.minimum(i * nb + nb, nblk - 1), cb))

    return pl.pallas_call(
        body, name=f"attn_bwd_{bps}",
        grid=(S // tm,),
        in_specs=[_rows(tm, WIDTH, qcb), _rows(tm, WIDTH, kcb), _rows(tm, WIDTH, vcb), _rows(tm, WIDTH),
                  _rows(tm, 128), _rows(tm, 128), prev_halo(kcb), prev_halo(vcb),
                  next_halo(WIDTH, qcb), next_halo(WIDTH), next_halo(128), next_halo(128),
                  _rows(tm, WIDTH), _rows(tm, WIDTH), _rows(tm, WIDTH), _rows(tm, WIDTH), ANY],
        out_specs=_rows(tm, 5 * WIDTH, CB_Q0 // 5),
        out_shape=jax.ShapeDtypeStruct(dst.shape, BF16),
        input_output_aliases={16: 0},
        scratch_shapes=[pltpu.VMEM((ncur * CHUNK, CHUNK), F32), pltpu.VMEM((nprev * CHUNK, CHUNK), F32),
                        pltpu.VMEM((ncur * CHUNK, CHUNK), F32), pltpu.VMEM((nprev * CHUNK, CHUNK), F32),
                        pltpu.VMEM((ncur * CHUNK, CHUNK), BF16), pltpu.VMEM((nprev * CHUNK, CHUNK), BF16),
                        pltpu.VMEM((ncur * CHUNK, CHUNK), BF16), pltpu.VMEM((nprev * CHUNK, CHUNK), BF16)],
        compiler_params=_params(("parallel",)),
    )(q, k, v, do, lse, delta, k, v, q, do, lse, delta, *dq_others, dk_in, dv_in, dst)


def _dilated_split(d):
    hp = min(N_HEAD, 16 // d)
    return hp, N_HEAD // hp, HEAD * hp


def _strided_regroup(d):
    return d < 16


def _by_class(src_ref, dst, d, hp, nat):
    for j in range(hp):
        if _strided_regroup(d):
            nat[j] = src_ref[:, j * HEAD:(j + 1) * HEAD].astype(F32)
            for r in range(d):
                dst[j, r * CHUNK:(r + 1) * CHUNK, :] = nat.at[j][pl.ds(r, CHUNK, stride=d), :].astype(BF16)
        else:
            dst[j] = pltpu.einshape("(tr)l->(rt)l", src_ref[:, j * HEAD:(j + 1) * HEAD], r=d)


def _from_class(src, dst_ref, d, hp, nat, add_ref=None):
    for j in range(hp):
        cs = slice(j * HEAD, (j + 1) * HEAD)
        if _strided_regroup(d):
            for r in range(d):
                nat.at[j][pl.ds(r, CHUNK, stride=d), :] = src[j, r * CHUNK:(r + 1) * CHUNK, :]
            val = nat[j].astype(BF16)
        else:
            val = pltpu.einshape("(rt)l->(tr)l", src[j].astype(BF16), r=d)
        if add_ref is not None:
            val = (val.astype(F32) + add_ref[:, cs].astype(F32)).astype(BF16)
        dst_ref[:, cs] = val


def _attn_fwd_dilated(proj, qcb, kcb, vcb, d):
    S = proj.shape[0]
    T = CHUNK * d
    hp, nh, cw = _dilated_split(d)
    nblocks = d * hp

    def body(q_ref, k_ref, v_ref, o_ref, l_ref, qf, kst, vst, of, lf, nat, sc_s, sp_s, pc_s, pp_s):
        i, hh = pl.program_id(0), pl.program_id(1)
        kf, vf = kst.at[i % 2, hh], vst.at[i % 2, hh]
        kpf, vpf = kst.at[1 - i % 2, hh], vst.at[1 - i % 2, hh]

        @pl.when(i == 0)
        def _():
            kpf[...] = jnp.zeros_like(kpf)
            vpf[...] = jnp.zeros_like(vpf)

        _by_class(q_ref, qf, d, hp, nat)
        _by_class(k_ref, kf, d, hp, nat)
        _by_class(v_ref, vf, d, hp, nat)

        def blk(ref, r, j):
            return ref[j, r * CHUNK:(r + 1) * CHUNK, :]

        def bs(r, j):
            return slice((r * hp + j) * CHUNK, (r * hp + j + 1) * CHUNK)

        for r in range(d):
            for j in range(hp):
                qb = blk(qf, r, j)
                sc_s[bs(r, j), :] = _dot_nt(qb, blk(kf, r, j))
                sp_s[bs(r, j), :] = _dot_nt(qb, blk(kpf, r, j))
        row = lax.broadcasted_iota(jnp.int32, (nblocks * CHUNK, CHUNK), 0) & (CHUNK - 1)
        col = lax.broadcasted_iota(jnp.int32, (nblocks * CHUNK, CHUNK), 1)
        sc = jnp.where(col <= row, sc_s[...] * ATT_SCALE, NEG)
        sp = jnp.where(col >= row, sp_s[...] * ATT_SCALE, NEG) + jnp.where(i > 0, 0.0, NEG)
        m = jnp.maximum(jnp.max(sc, axis=-1, keepdims=True), jnp.max(sp, axis=-1, keepdims=True))
        ec = jnp.exp(sc - m)
        ep = jnp.exp(sp - m)
        den = jnp.sum(ec, axis=-1, keepdims=True) + jnp.sum(ep, axis=-1, keepdims=True)
        inv = 1.0 / den
        pc_s[...] = (ec * inv).astype(BF16)
        pp_s[...] = (ep * inv).astype(BF16)
        lse = m + jnp.log(den)
        lane = lax.broadcasted_iota(jnp.int32, (CHUNK, 128), 1)
        for r in range(d):
            lblk = jnp.zeros((CHUNK, 128), F32)
            for j in range(hp):
                o = _dot(pc_s[bs(r, j), :], blk(vf, r, j)) + _dot(pp_s[bs(r, j), :], blk(vpf, r, j))
                of[j, r * CHUNK:(r + 1) * CHUNK, :] = o
                lblk = jnp.where(lane == hh * hp + j, lse[bs(r, j)], lblk)
            lf[r * CHUNK:(r + 1) * CHUNK, :] = lblk
        _from_class(of, o_ref, d, hp, nat)
        lnat = pltpu.einshape("(rt)l->(tr)l", lf[...], r=d)

        @pl.when(hh == 0)
        def _():
            l_ref[...] = lnat

        @pl.when(hh > 0)
        def _():
            l_ref[...] += lnat

    def cols(cb):
        return pl.BlockSpec((T, cw), lambda i, hh: (i, cb * nh + hh))

    tile = pltpu.VMEM((hp, T, HEAD), BF16)
    staging = pltpu.VMEM((hp, T, HEAD) if _strided_regroup(d) else (1, 8, HEAD), F32)
    return pl.pallas_call(
        body, name=f"attn_fwd_dilated_{d}",
        grid=(S // T, nh),
        in_specs=[cols(qcb), cols(kcb), cols(vcb)],
        out_specs=[cols(0), pl.BlockSpec((T, 128), lambda i, hh: (i, 0))],
        out_shape=[jax.ShapeDtypeStruct((S, WIDTH), BF16), jax.ShapeDtypeStruct((S, 128), F32)],
        scratch_shapes=[tile, pltpu.VMEM((2, nh, hp, T, HEAD), BF16), pltpu.VMEM((2, nh, hp, T, HEAD), BF16),
                        pltpu.VMEM((hp, T, HEAD), F32), pltpu.VMEM((T, 128), F32), staging,
                        pltpu.VMEM((nblocks * CHUNK, CHUNK), F32), pltpu.VMEM((nblocks * CHUNK, CHUNK), F32),
                        pltpu.VMEM((nblocks * CHUNK, CHUNK), BF16), pltpu.VMEM((nblocks * CHUNK, CHUNK), BF16)],
        compiler_params=_params(("arbitrary", "arbitrary")),
    )(proj, proj, proj)


def _attn_bwd_dilated(proj, qcb, kcb, vcb, do, lse, delta, d, dk_in=None, dv_in=None):
    S = proj.shape[0]
    T = CHUNK * d
    nt = S // T
    hp, nh, cw = _dilated_split(d)
    nblocks = d * hp

    chained = dk_in is not None

    def body(q_ref, k_ref, v_ref, do_ref, l_ref, d_ref, *rest):
        dki_ref, dvi_ref = rest[:2] if chained else (None, None)
        (dq_ref, dk_ref, dv_ref, qf, dof, kbuf, vbuf, dqf, gk, gv, nat,
         sc_s, sp_s, dpc_s, dpp_s, pc_s, pp_s, dsc_s, dsp_s) = rest[2 * chained:]
        hh, i = pl.program_id(0), pl.program_id(1)
        kf, vf, newk, newv = kbuf.at[i % 2], vbuf.at[i % 2], gk.at[i % 2], gv.at[i % 2]
        kpf, vpf, acck, accv = kbuf.at[1 - i % 2], vbuf.at[1 - i % 2], gk.at[1 - i % 2], gv.at[1 - i % 2]

        @pl.when(i == 0)
        def _():
            for ref in (kbuf, vbuf, gk, gv):
                ref[...] = jnp.zeros_like(ref)
            dk_ref[...] = jnp.zeros_like(dk_ref)
            dv_ref[...] = jnp.zeros_like(dv_ref)

        def blk(ref, r, j):
            return ref[j, r * CHUNK:(r + 1) * CHUNK, :]

        def bs(r, j):
            return slice((r * hp + j) * CHUNK, (r * hp + j + 1) * CHUNK)

        @pl.when(i < nt)
        def _():
            _by_class(q_ref, qf, d, hp, nat)
            _by_class(do_ref, dof, d, hp, nat)
            _by_class(k_ref, kf, d, hp, nat)
            _by_class(v_ref, vf, d, hp, nat)
            lses, deltas = [], []
            lcls = pltpu.einshape("(tr)l->(rt)l", l_ref[...], r=d)
            dcls = pltpu.einshape("(tr)l->(rt)l", d_ref[...], r=d)
            for r in range(d):
                lblk = lcls[r * CHUNK:(r + 1) * CHUNK]
                dblk = dcls[r * CHUNK:(r + 1) * CHUNK]
                for j in range(hp):
                    lses.append(_col(lblk, hh * hp + j))
                    deltas.append(_col(dblk, hh * hp + j))
                    qb, dob = blk(qf, r, j), blk(dof, r, j)
                    sc_s[bs(r, j), :] = _dot_nt(qb, blk(kf, r, j))
                    dpc_s[bs(r, j), :] = _dot_nt(dob, blk(vf, r, j))
                    sp_s[bs(r, j), :] = _dot_nt(qb, blk(kpf, r, j))
                    dpp_s[bs(r, j), :] = _dot_nt(dob, blk(vpf, r, j))
            lse = jnp.concatenate(lses, axis=0)
            delta = jnp.concatenate(deltas, axis=0)
            row = lax.broadcasted_iota(jnp.int32, (nblocks * CHUNK, CHUNK), 0) & (CHUNK - 1)
            col = lax.broadcasted_iota(jnp.int32, (nblocks * CHUNK, CHUNK), 1)
            sp = jnp.where(col >= row, sp_s[...] * ATT_SCALE, NEG) + jnp.where(i > 0, 0.0, NEG)
            pp = jnp.exp(sp - lse)
            pp_s[...] = pp.astype(BF16)
            dsp_s[...] = (pp * (dpp_s[...] - delta)).astype(BF16)
            sc = jnp.where(col <= row, sc_s[...] * ATT_SCALE, NEG)
            pc = jnp.exp(sc - lse)
            pc_s[...] = pc.astype(BF16)
            dsc_s[...] = (pc * (dpc_s[...] - delta)).astype(BF16)
            for r in range(d):
                rows = slice(r * CHUNK, (r + 1) * CHUNK)
                for j in range(hp):
                    qb, dob = blk(qf, r, j), blk(dof, r, j)
                    dsc, dsp = dsc_s[bs(r, j), :], dsp_s[bs(r, j), :]
                    dqf[j, rows, :] = (_dot(dsc, blk(kf, r, j)) + _dot(dsp, blk(kpf, r, j))) * ATT_SCALE
                    newk[j, rows, :] = _dot_tn(dsc, qb) * ATT_SCALE
                    newv[j, rows, :] = _dot_tn(pc_s[bs(r, j), :], dob)
                    acck[j, rows, :] += _dot_tn(dsp, qb) * ATT_SCALE
                    accv[j, rows, :] += _dot_tn(pp_s[bs(r, j), :], dob)
            _from_class(dqf, dq_ref, d, hp, nat)

        @pl.when(i > 0)
        def _():
            _from_class(acck, dk_ref, d, hp, nat, dki_ref)
            _from_class(accv, dv_ref, d, hp, nat, dvi_ref)

    def cur(width, cb, nsplit):
        return pl.BlockSpec((T, width), lambda hh, i: (jnp.minimum(i, nt - 1), cb * nsplit + hh * (nsplit > 1)))

    def lag():
        return pl.BlockSpec((T, cw), lambda hh, i: (jnp.maximum(i - 1, 0), hh))

    tile = pltpu.VMEM((hp, T, HEAD), BF16)
    acc = pltpu.VMEM((hp, T, HEAD), F32)
    f32s = pltpu.VMEM((nblocks * CHUNK, CHUNK), F32)
    b16s = pltpu.VMEM((nblocks * CHUNK, CHUNK), BF16)
    return pl.pallas_call(
        body, name=f"attn_bwd_dilated_{d}",
        grid=(nh, nt + 1),
        in_specs=[cur(cw, qcb, nh), cur(cw, kcb, nh), cur(cw, vcb, nh), cur(cw, 0, nh), cur(128, 0, 1), cur(128, 0, 1)]
                 + [lag(), lag()] * chained,
        out_specs=[cur(cw, 0, nh), lag(), lag()],
        out_shape=[jax.ShapeDtypeStruct((S, WIDTH), BF16)] * 3,
        scratch_shapes=[tile, tile, pltpu.VMEM((2, hp, T, HEAD), BF16), pltpu.VMEM((2, hp, T, HEAD), BF16), acc,
                        pltpu.VMEM((2, hp, T, HEAD), F32), pltpu.VMEM((2, hp, T, HEAD), F32),
                        acc if _strided_regroup(d) else pltpu.VMEM((1, 8, HEAD), F32)]
                       + [f32s] * 4 + [b16s] * 4,
        compiler_params=_params(("arbitrary", "arbitrary")),
    )(proj, proj, proj, do, lse, delta, *((dk_in, dv_in) if chained else ()))


def _abm_bwd(proj, cdf, dy3, ln_g, ln_b, wsm, wsm_t, bias_full, pool_w, pool_wt, pool_scale, kv, bands, bands_t, dst):
    S = proj.shape[0]
    tm = 512
    nchunk = tm // CHUNK
    nblk = S // CHUNK

    def body(u_ref, v_ref, ag_ref, p_ref, ph_ref, pg_ref, pgn_ref, mq_ref, mg_ref, cdf_ref, dy_ref, dypn_ref,
             lng_ref, lnb_ref, wsm_ref, wsmt_ref, bias_ref, pw_ref, pwt_ref, ps_ref, kv_ref, band_ref, bandt_ref,
             dst_ref, dab_ref, dm_ref, dlng_ref, dlnb_ref, dws_ref, dbias_ref, dpw_ref, dps_ref, dkv_ref,
             mix, dvl, ddn):
        i = pl.program_id(0)

        @pl.when(i == 0)
        def _():
            for r in (dlng_ref, dlnb_ref, dws_ref, dbias_ref, dpw_ref, dps_ref, dkv_ref):
                r[...] = jnp.zeros_like(r)

        au = u_ref[...].astype(F32)
        av = v_ref[...].astype(F32)
        ag = ag_ref[...].astype(F32)
        u, du = _gelu_and_grad(au, cdf_ref[0].astype(F32))
        v, dgelu_v = _gelu_and_grad(av, cdf_ref[1].astype(F32))
        vhat, rstd = _layer_norm_fwd(v)
        vln = (vhat * lng_ref[...] + lnb_ref[...]).astype(BF16)
        for c in range(nchunk):
            for h in range(N_HEAD):
                rs, cs = slice(c * CHUNK, (c + 1) * CHUNK), slice(h * HEAD, (h + 1) * HEAD)
                mix[rs, cs] = _dot(wsm_ref[h], vln[rs, cs]) + bias_ref[:, cs]
        dya = dy_ref[0].astype(F32)
        sg, dsg = _silu_and_grad(ag)
        mixed = mix[...]
        dab_ref[:, 2 * WIDTH:3 * WIDTH] = (dya * u * mixed * dsg).astype(BF16)
        dab_ref[:, 0:WIDTH] = (dya * mixed * sg * du).astype(BF16)
        dmixed = dya * u * sg
        dmb = dmixed.astype(BF16)
        tril = (lax.broadcasted_iota(jnp.int32, (CHUNK, CHUNK), 1)
                <= lax.broadcasted_iota(jnp.int32, (CHUNK, CHUNK), 0))
        for c in range(nchunk):
            rs = slice(c * CHUNK, (c + 1) * CHUNK)
            dbias_ref[...] += dmixed[rs, :]
            for h in range(N_HEAD):
                cs = slice(h * HEAD, (h + 1) * HEAD)
                dvl[rs, cs] = _dot(wsmt_ref[h], dmb[rs, cs])
                dws_ref[h] += jnp.where(tril, _dot_nt(dmb[rs, cs], vln[rs, cs]), 0.0)
        dvln = dvl[...]
        dlng_ref[...] += jnp.sum(dvln * vhat, axis=0, keepdims=True)
        dlnb_ref[...] += jnp.sum(dvln, axis=0, keepdims=True)
        dvh = dvln * lng_ref[...]
        dv = rstd * (dvh - jnp.mean(dvh, axis=-1, keepdims=True)
                     - vhat * jnp.mean(dvh * vhat, axis=-1, keepdims=True))
        dab_ref[:, WIDTH:2 * WIDTH] = (dv * dgelu_v).astype(BF16)

        halo_ok = (i > 0).astype(F32)
        for c in range(nchunk):
            rs = slice(c * CHUNK, (c + 1) * CHUNK)
            for g, win in enumerate(POOL_WINDOWS):
                cs = slice(g * HEAD, (g + 1) * HEAD)
                cur = p_ref[rs, cs]
                if c == 0:
                    prev = (ph_ref[:, cs].astype(F32) * halo_ok).astype(BF16)
                else:
                    prev = p_ref[(c - 1) * CHUNK:c * CHUNK, cs]
                sums = _dot(band_ref[g, 0], cur) + _dot(band_ref[g, 1], prev)
                dvl[rs, cs] = sums * _inv_count(i * tm + c * CHUNK, win) - cur.astype(F32)
        dmat = dvl[...].astype(BF16)
        for g in range(4):
            cs = slice(g * HEAD, (g + 1) * HEAD)
            mix[:, cs] = _dot(dmat[:, cs], pw_ref[g])
        yg = mix[...]
        pg = pg_ref[...].astype(F32)
        dyp = dy_ref[1].astype(F32)
        spg, dspg = _silu_and_grad(pg)
        dyy = dyp * spg
        scale = ps_ref[...]
        dab_ref[:, 4 * WIDTH:5 * WIDTH] = (dyp * yg * scale * dspg).astype(BF16)
        dps_ref[...] += jnp.sum(dyy * yg, axis=0, keepdims=True)
        dyg = (dyy * scale).astype(BF16)
        for g in range(4):
            cs = slice(g * HEAD, (g + 1) * HEAD)
            dpw_ref[g] += _dot_tn(dmat[:, cs], dyg[:, cs])
            mix[:, cs] = _dot(dyg[:, cs], pwt_ref[g])
        next_ok = (i + 1 < S // tm).astype(F32)
        dygn = (dypn_ref[...].astype(F32) * _silu(pgn_ref[...].astype(F32)) * scale * next_ok).astype(BF16)
        for c in range(nchunk + 1):
            for g, win in enumerate(POOL_WINDOWS):
                cs = slice(g * HEAD, (g + 1) * HEAD)
                if c < nchunk:
                    dd = mix[c * CHUNK:(c + 1) * CHUNK, cs]
                else:
                    dd = _dot(dygn[:, cs], pwt_ref[g])
                ddn[c * CHUNK:(c + 1) * CHUNK, cs] = dd * _inv_count(i * tm + c * CHUNK, win)
        ddnb = ddn[...].astype(BF16)
        for c in range(nchunk):
            rs = slice(c * CHUNK, (c + 1) * CHUNK)
            ns = slice((c + 1) * CHUNK, (c + 2) * CHUNK)
            for g, win in enumerate(POOL_WINDOWS):
                cs = slice(g * HEAD, (g + 1) * HEAD)
                dp = _dot(bandt_ref[g, 0], ddnb[rs, cs]) + _dot(bandt_ref[g, 1], ddnb[ns, cs]) - mix[rs, cs]
                dab_ref[rs, 3 * WIDTH + g * HEAD:3 * WIDTH + (g + 1) * HEAD] = dp.astype(BF16)

        mg = mg_ref[...].astype(F32)
        dym = dy_ref[2].astype(F32)
        smg, dsmg = _silu_and_grad(mg)
        dob = (dym * smg).astype(BF16)
        for h in range(N_HEAD):
            cs = slice(h * HEAD, (h + 1) * HEAD)
            vs = slice(WIDTH + h * HEAD, WIDTH + (h + 1) * HEAD)
            qh = mq_ref[:, cs]
            p = _mem_softmax(qh, kv_ref[:, cs])
            pb = p.astype(BF16)
            mix[:, cs] = _dot(pb, kv_ref[:, vs])
            dp = _dot_nt(dob[:, cs], kv_ref[:, vs])
            ds = (p * (dp - jnp.sum(p * dp, axis=-1, keepdims=True))).astype(BF16)
            dm_ref[:, cs] = (_dot(ds, kv_ref[:, cs]) * ATT_SCALE).astype(BF16)
            dkv_ref[:, cs] += _dot_tn(ds, qh) * ATT_SCALE
            dkv_ref[:, vs] += _dot_tn(pb, dob[:, cs])
        dm_ref[:, WIDTH:2 * WIDTH] = (dym * mix[...] * dsmg).astype(BF16)

    blk = tm // CHUNK
    small = [_full((1, WIDTH)), _full((1, WIDTH)), _full((N_HEAD, CHUNK, CHUNK)), _full((CHUNK, WIDTH)),
             _full((4, HEAD, HEAD)), _full((1, WIDTH)), _full((MEM_LEN, 2 * WIDTH))]
    return pl.pallas_call(
        body, name="abm_bwd",
        grid=(S // tm,),
        in_specs=[_rows(tm, WIDTH, CB_U), _rows(tm, WIDTH, CB_V), _rows(tm, WIDTH, CB_AGATE),
                  _rows(tm, WIDTH, CB_PIN),
                  pl.BlockSpec((CHUNK, WIDTH), lambda i: (jnp.maximum(i * blk - 1, 0), CB_PIN)),
                  _rows(tm, WIDTH, CB_PGATE),
                  pl.BlockSpec((CHUNK, WIDTH), lambda i: (jnp.minimum(i * blk + blk, nblk - 1), CB_PGATE)),
                  _rows(tm, WIDTH, CB_MQ), _rows(tm, WIDTH, CB_MGATE),
                  pl.BlockSpec((2, tm, WIDTH), lambda i: (0, i, 0)),
                  pl.BlockSpec((3, tm, WIDTH), lambda i: (0, i, 0)),
                  pl.BlockSpec((None, CHUNK, WIDTH), lambda i: (1, jnp.minimum(i * blk + blk, nblk - 1), 0)),
                  _full((1, WIDTH)), _full((1, WIDTH)), _full((N_HEAD, CHUNK, CHUNK)), _full((N_HEAD, CHUNK, CHUNK)),
                  _full((CHUNK, WIDTH)), _full((4, HEAD, HEAD)), _full((4, HEAD, HEAD)), _full((1, WIDTH)),
                  _full((MEM_LEN, 2 * WIDTH)), _full((4, 2, CHUNK, CHUNK)), _full((4, 2, CHUNK, CHUNK)), ANY],
        out_specs=[_rows(tm, 5 * WIDTH), _rows(tm, 2 * WIDTH)] + small,
        out_shape=[jax.ShapeDtypeStruct(dst.shape, BF16), jax.ShapeDtypeStruct((S, 2 * WIDTH), BF16),
                   jax.ShapeDtypeStruct((1, WIDTH), F32), jax.ShapeDtypeStruct((1, WIDTH), F32),
                   jax.ShapeDtypeStruct((N_HEAD, CHUNK, CHUNK), F32), jax.ShapeDtypeStruct((CHUNK, WIDTH), F32),
                   jax.ShapeDtypeStruct((4, HEAD, HEAD), F32), jax.ShapeDtypeStruct((1, WIDTH), F32),
                   jax.ShapeDtypeStruct((MEM_LEN, 2 * WIDTH), F32)],
        input_output_aliases={23: 0},
        scratch_shapes=[pltpu.VMEM((tm, WIDTH), F32), pltpu.VMEM((tm, WIDTH), F32),
                        pltpu.VMEM((tm + CHUNK, WIDTH), F32)],
        compiler_params=_params(("arbitrary",)),
    )(proj, proj, proj, proj, proj, proj, proj, proj, proj, cdf, dy3, dy3,
      ln_g, ln_b, wsm, wsm_t, bias_full, pool_w, pool_wt, pool_scale, kv, bands, bands_t, dst)


def _bias_reduce(dbias_full):
    def body(d_ref, o_ref):
        d = d_ref[...]
        o_ref[...] = _put_cols([jnp.sum(d[:, h * HEAD:(h + 1) * HEAD], axis=1, keepdims=True) for h in range(N_HEAD)])

    return pl.pallas_call(body, name="bias_reduce", out_shape=jax.ShapeDtypeStruct((CHUNK, 128), F32))(dbias_full)


def _mem_bwd(mem, g, mem_n, w, dkv):
    def body(m_ref, g_ref, mn_ref, w_ref, dkv_ref, dw_ref, dg_ref):
        dkvb = dkv_ref[...].astype(BF16)
        dw_ref[...] = _dot_tn(mn_ref[...], dkvb).astype(BF16)
        dmn = _dot_nt(dkvb, w_ref[...])
        xf = m_ref[...]
        r = lax.rsqrt(jnp.mean(xf * xf, axis=-1, keepdims=True) + EPS)
        dg_ref[...] = jnp.sum(dmn * xf * r, axis=0, keepdims=True)

    return pl.pallas_call(
        body, name="mem_bwd",
        out_shape=[jax.ShapeDtypeStruct((D_MODEL, 2 * WIDTH), BF16), jax.ShapeDtypeStruct((1, D_MODEL), F32)],
        compiler_params=pltpu.CompilerParams(vmem_limit_bytes=VMEM_LIMIT),
    )(mem, g, mem_n, w, dkv)


def _dh_bwd(dpb, dm, dpg, wt, x, g, dxo, parts=(), grads=()):
    S = x.shape[0]
    tm, tkb, tkg = 1024, D_REST // 4, D_GATES // 4
    nkb, nkg = 4, 4
    nk = nkb + 1 + nkg
    ni = S // tm
    n, m = len(parts), len(grads)

    def body(dpb_ref, wbr_ref, dm_ref, wm0_ref, wm1_ref, dpg_ref, wg0_ref, wg1_ref, x_ref, g_ref, dxo_ref, *rest):
        p_in, g_in = rest[:n], rest[n:n + m]
        dx_ref, dg_ref = rest[n + m:n + m + 2]
        p_out, g_out = rest[n + m + 2:2 * n + m + 2], rest[2 * n + m + 2:2 * (n + m) + 2]
        acc, sems = rest[2 * (n + m) + 2], rest[2 * (n + m) + 3:]
        second_sems, first_sems = (sems[:3] if n else ()), sems[3 if n else 0:]
        i, kk = pl.program_id(0), pl.program_id(1)

        @pl.when(jnp.logical_and(i == 0, kk == 0))
        def _():
            dg_ref[...] = jnp.zeros_like(dg_ref)
            if n:
                _comm_start(_rs_second(p_in, p_out, *second_sems))
            if m:
                _comm_start(_rs_first(g_in, g_out, *first_sems))

        @pl.when(kk == 0)
        def _():
            acc[...] = jnp.zeros_like(acc)

        @pl.when(kk < nkb)
        def _():
            acc[...] += _dot(dpb_ref[...], wbr_ref[...])

        @pl.when(kk == nkb)
        def _():
            acc[...] += _dot(dm_ref[:, :WIDTH], wm0_ref[...]) + _dot(dm_ref[:, WIDTH:], wm1_ref[...])

        @pl.when(kk > nkb)
        def _():
            acc[...] += _dot(dpg_ref[:, :tkg // 2], wg0_ref[...]) + _dot(dpg_ref[:, tkg // 2:], wg1_ref[...])

        @pl.when(kk == nk - 1)
        def _():
            xf = x_ref[...]
            r = lax.rsqrt(jnp.mean(xf * xf, axis=-1, keepdims=True) + EPS)
            xhat = xf * r
            dh = acc[...]
            dg_ref[...] += jnp.sum(dh * xhat, axis=0, keepdims=True)
            dxh = dh * g_ref[...]
            dx_ref[...] = dxo_ref[...] + r * (dxh - xhat * jnp.mean(dxh * xhat, axis=-1, keepdims=True))

        if n or m:
            @pl.when(jnp.logical_and(i == ni - 1, kk == nk - 1))
            def _():
                if n:
                    _comm_wait(_rs_second(p_in, p_out, *second_sems))
                if m:
                    _comm_wait(_rs_first(g_in, g_out, *first_sems))

    def gate_rows(half):
        return pl.BlockSpec((tkg // 2, D_MODEL),
                            lambda i, k: (D_BRANCHES // (tkg // 2) + 2 * jnp.maximum(k - nkb - 1, 0) + half, 0))

    res = pl.pallas_call(
        body, name="dh_bwd_scatter" if (n or m) else "dh_bwd",
        grid=(ni, nk),
        in_specs=[pl.BlockSpec((tm, tkb), lambda i, k: (i, jnp.minimum(k, nkb - 1))),
                  pl.BlockSpec((tkb, D_MODEL), lambda i, k: (jnp.minimum(k, nkb - 1), 0)),
                  pl.BlockSpec((tm, 2 * WIDTH), lambda i, k: (i, 0), pipeline_mode=pl.Buffered(1)),
                  pl.BlockSpec((WIDTH, D_MODEL), lambda i, k: (CB_MQ, 0), pipeline_mode=pl.Buffered(1)),
                  pl.BlockSpec((WIDTH, D_MODEL), lambda i, k: (CB_MGATE, 0), pipeline_mode=pl.Buffered(1)),
                  pl.BlockSpec((tm, tkg), lambda i, k: (i, jnp.maximum(k - nkb - 1, 0))),
                  gate_rows(0), gate_rows(1),
                  pl.BlockSpec((tm, D_MODEL), lambda i, k: (i, 0)), pl.BlockSpec((1, D_MODEL), lambda i, k: (0, 0)),
                  pl.BlockSpec((tm, D_MODEL), lambda i, k: (i, 0))] + [ANY] * (n + m),
        out_specs=[pl.BlockSpec((tm, D_MODEL), lambda i, k: (i, 0)), pl.BlockSpec((1, D_MODEL), lambda i, k: (0, 0))]
                  + [ANY] * (n + m),
        out_shape=[jax.ShapeDtypeStruct((S, D_MODEL), F32), jax.ShapeDtypeStruct((1, D_MODEL), F32)]
                  + [jax.ShapeDtypeStruct(p.shape, p.dtype) for p in parts]
                  + [jax.ShapeDtypeStruct(gr.shape[:1] + gr.shape[2:], gr.dtype) for gr in grads],
        scratch_shapes=[pltpu.VMEM((tm, D_MODEL), F32)] + (_dma_sems(3 * n, 3 * n, n) if n else [])
                       + (_dma_sems(N_CHIP * m, N_CHIP * m) if m else []),
        compiler_params=_params(("arbitrary", "arbitrary"), vmem=60 * 1024 * 1024),
    )(dpb, wt, dm, wt, wt, dpg, wt, wt, x, g, dxo, *parts, *grads)
    return res[0], res[1], list(res[2:2 + n]), list(res[2 + n:])


def _dw_in(h, dpb, dm, dpg, parts=(), grads=()):
    S = h.shape[0]
    tk = 2048
    nk = S // tk
    n, m = len(parts), len(grads)
    tmb = D_REST // 4
    nm = 2 * WIDTH // GATE_TILE
    ng = nm + D_GATES // GATE_TILE

    def accumulate(a_ref, h_ref, o_ref, acc):
        kk = pl.program_id(1)

        @pl.when(kk == 0)
        def _():
            acc[...] = jnp.zeros_like(acc)

        acc[...] += _dot_tn(a_ref[...], h_ref[...])

        @pl.when(kk == nk - 1)
        def _():
            o_ref[...] = acc[...].astype(BF16)

    def branches(a_ref, h_ref, *rest):
        p_in, g_in, o_ref = rest[:n], rest[n:n + m], rest[n + m]
        p_out, g_out = rest[n + m + 1:2 * n + m + 1], rest[2 * n + m + 1:2 * (n + m) + 1]
        acc, sems = rest[2 * (n + m) + 1], rest[2 * (n + m) + 2:]
        second_sems, first_sems = (sems[:3] if n else ()), sems[3 if n else 0:]
        i, kk = pl.program_id(0), pl.program_id(1)

        if n or m:
            @pl.when(jnp.logical_and(i == 0, kk == 0))
            def _():
                if n:
                    _comm_start(_rs_second(p_in, p_out, *second_sems))
                if m:
                    _comm_start(_rs_first(g_in, g_out, *first_sems))

        accumulate(a_ref, h_ref, o_ref, acc)

        if n or m:
            @pl.when(jnp.logical_and(i == 3, kk == nk - 1))
            def _():
                if n:
                    _comm_wait(_rs_second(p_in, p_out, *second_sems))
                if m:
                    _comm_wait(_rs_first(g_in, g_out, *first_sems))

    def gates(m_ref, a_ref, h_ref, dst_ref, *rest):
        q_in, o_ref, q_out, acc, sems = rest[:m], rest[m], rest[m + 1:2 * m + 1], rest[2 * m + 1], rest[2 * m + 2:]
        i, kk = pl.program_id(0), pl.program_id(1)

        if m:
            @pl.when(jnp.logical_and(i == 0, kk == 0))
            def _():
                _comm_start(_rs_second(q_in, q_out, *sems))

        @pl.when(kk == 0)
        def _():
            acc[...] = jnp.zeros_like(acc)

        @pl.when(i < nm)
        def _():
            acc[...] += _dot_tn(m_ref[...], h_ref[...])

        @pl.when(i >= nm)
        def _():
            acc[...] += _dot_tn(a_ref[...], h_ref[...])

        @pl.when(kk == nk - 1)
        def _():
            o_ref[...] = acc[...].astype(BF16)

        if m:
            @pl.when(jnp.logical_and(i == ng - 1, kk == nk - 1))
            def _():
                _comm_wait(_rs_second(q_in, q_out, *sems))

    res = pl.pallas_call(
        branches, name="dw_in_branches_scatter" if (n or m) else "dw_in_branches",
        grid=(4, nk),
        in_specs=[pl.BlockSpec((tk, tmb), lambda i, k: (k, i)), pl.BlockSpec((tk, D_MODEL), lambda i, k: (k, 0))]
                 + [ANY] * (n + m),
        out_specs=[pl.BlockSpec((tmb, D_MODEL), lambda i, k: (i, 0))] + [ANY] * (n + m),
        out_shape=[jax.ShapeDtypeStruct((D_IN, D_MODEL), BF16)]
                  + [jax.ShapeDtypeStruct(p.shape, p.dtype) for p in parts]
                  + [jax.ShapeDtypeStruct(gr.shape[:1] + gr.shape[2:], gr.dtype) for gr in grads],
        scratch_shapes=[pltpu.VMEM((tmb, D_MODEL), F32)] + (_dma_sems(3 * n, 3 * n, n) if n else [])
                       + (_dma_sems(N_CHIP * m, N_CHIP * m) if m else []),
        compiler_params=_params(("arbitrary", "arbitrary")),
    )(dpb, h, *parts, *grads)
    pair = _pair_sum(grads, res[1 + n:]) if m else []
    res2 = pl.pallas_call(
        gates, name="dw_in_gates_scatter" if m else "dw_in_gates",
        grid=(ng, nk),
        in_specs=[pl.BlockSpec((tk, GATE_TILE), lambda i, k: (k, jnp.minimum(i, nm - 1))),
                  pl.BlockSpec((tk, GATE_TILE), lambda i, k: (k, jnp.maximum(i - nm, 0))),
                  pl.BlockSpec((tk, D_MODEL), lambda i, k: (k, 0)), ANY] + [ANY] * m,
        out_specs=[pl.BlockSpec((GATE_TILE, D_MODEL), lambda i, k: (D_REST // GATE_TILE + i, 0))] + [ANY] * m,
        out_shape=[jax.ShapeDtypeStruct((D_IN, D_MODEL), BF16)] + [jax.ShapeDtypeStruct(p.shape, p.dtype) for p in pair],
        input_output_aliases={3: 0},
        scratch_shapes=[pltpu.VMEM((GATE_TILE, D_MODEL), F32)] + (_dma_sems(3 * m, 3 * m, m) if m else []),
        compiler_params=_params(("arbitrary", "arbitrary")),
    )(dm, dpg, h, res[0], *pair)
    return res2[0], list(res[1:1 + n]), list(res2[1:])


def _row_tile(R, C, block_bytes=2 << 20):
    for cand in range(min(R, block_bytes // (C * 4)) // 8 * 8, 0, -8):
        if R % cand == 0:
            return cand
    return R


def _adamw_update(p_ref, w_ref, m_ref, v_ref, g_ref, d_ref, nm_ref, nv_ref):
    c1 = 1.0 / (1.0 - ADAM_B1 ** ADAM_STEP)
    c2 = 1.0 / (1.0 - ADAM_B2 ** ADAM_STEP)
    g = p_ref[0].astype(F32)
    for k in range(1, p_ref.shape[0]):
        g = g + p_ref[k].astype(F32)
    nm = ADAM_B1 * m_ref[...] + (1.0 - ADAM_B1) * g
    nv = ADAM_B2 * v_ref[...] + (1.0 - ADAM_B2) * (g * g)
    g_ref[...] = g
    nm_ref[...] = nm
    nv_ref[...] = nv
    d_ref[...] = -ADAM_LR * ((nm * c1) / (jnp.sqrt(nv * c2) + ADAM_EPS) + ADAM_WD * w_ref[...])


def _adamw(parts, w, m, v, name):
    P, R, C = parts.shape
    tr = _row_tile(R, C)

    def body(*refs):
        _adamw_update(*refs)

    spec = pl.BlockSpec((tr, C), lambda i: (i, 0))
    return pl.pallas_call(
        body, name=name,
        grid=(R // tr,),
        in_specs=[pl.BlockSpec((P, tr, C), lambda i: (0, i, 0)), spec, spec, spec],
        out_specs=[spec] * 4,
        out_shape=[jax.ShapeDtypeStruct((R, C), F32)] * 4,
        compiler_params=_params(("parallel",)),
    )(parts, w, m, v)


def _adamw_layers(parts, w, m, v, name):
    depth = len(parts)
    P, R, C = parts[0].shape
    tr = _row_tile(R, C, 1 << 20)

    def body(*refs):
        layer = pl.program_id(0)
        for k in range(depth):
            @pl.when(layer == k)
            def _(k=k):
                _adamw_update(refs[k], *refs[depth:])

    def part_spec(k):
        return pl.BlockSpec((P, tr, C), lambda l, i: (0, jnp.where(l == k, i, 0), 0))

    spec = pl.BlockSpec((None, tr, C), lambda l, i: (l, i, 0))
    return pl.pallas_call(
        body, name=name,
        grid=(depth, R // tr),
        in_specs=[part_spec(k) for k in range(depth)] + [spec] * 3,
        out_specs=[spec] * 4,
        out_shape=[jax.ShapeDtypeStruct((depth, R, C), F32)] * 4,
        compiler_params=_params(("arbitrary", "arbitrary")),
    )(*parts, w, m, v)


def _place():
    return lax.axis_index("x"), lax.axis_index("y"), lax.axis_index("c")


def _all_gather(shards):
    n = len(shards)

    def body(*refs):
        ins, outs = refs[:n], refs[n:2 * n]
        send1, recv1, local_sems, relay_send, relay_recv, send2, recv2 = refs[2 * n:]
        x, y, c = _place()
        me, sibling = (x, y, c), (x, y, 1 - c)
        x_nbr, y_nbr, diagonal = [(*chip, c) for chip in _other_chips(x, y)]
        local, first_out, first_in, relay_out, relay_in = [], [], [], [], []
        for a in range(n):
            local.append(pltpu.make_async_copy(ins[a], outs[a].at[_dev(me)], local_sems.at[a]))
            for k, to in enumerate((sibling, x_nbr, y_nbr)):
                first_out.append(_remote(ins[a], outs[a].at[_dev(me)], send1, recv1, 3 * a + k, to))
                first_in.append(_remote(ins[a], outs[a].at[_dev(to)], send1, recv1, 3 * a + k, to))
            half = shards[a].shape[0] // 2
            for k, (src, to, rows) in enumerate(((x_nbr, y_nbr, pl.ds(0, half)), (y_nbr, x_nbr, pl.ds(half, half)))):
                passed, got = outs[a].at[_dev(src)].at[rows], outs[a].at[_dev(diagonal)].at[rows]
                relay_out.append(_remote(passed, passed, relay_send, relay_recv, 2 * a + k, to))
                relay_in.append(_remote(got, got, relay_send, relay_recv, 2 * a + k, to))
        second = _ag_second(outs, send2, recv2)
        for cp in local + first_out:
            cp.start()
        for a in range(n):
            for k in range(2):
                first_in[3 * a + 1 + k].wait_recv()
                relay_out[2 * a + k].start()
                second[1][3 * a + k].start()
        for a in range(n):
            for k in range(2):
                relay_in[2 * a + k].wait_recv()
            second[1][3 * a + 2].start()
        for a in range(n):
            first_in[3 * a].wait_recv()
        for cp in second[2]:
            cp.wait_recv()
        for cp in first_out + relay_out + second[1]:
            cp.wait_send()
        for cp in local:
            cp.wait()

    assert all(s.shape[0] % 32 == 0 for s in shards)
    return pl.pallas_call(
        body, name="weights_all_gather",
        in_specs=[ANY] * n, out_specs=[ANY] * n,
        out_shape=[jax.ShapeDtypeStruct((N_DEV,) + s.shape, s.dtype) for s in shards],
        scratch_shapes=_dma_sems(3 * n, 3 * n, n, 2 * n, 2 * n, 3 * n, 3 * n),
        compiler_params=pltpu.CompilerParams(has_side_effects=True),
    )(*shards)


N_BIG = 4


def _dev(p):
    return 4 * p[0] + 2 * p[1] + p[2]


def _other_chips(x, y):
    return [(1 - x, y), (x, 1 - y), (1 - x, 1 - y)]


def _remote(src, dst, send_sems, recv_sems, k, to):
    return pltpu.make_async_remote_copy(src_ref=src, dst_ref=dst, send_sem=send_sems.at[k], recv_sem=recv_sems.at[k],
                                        device_id=to, device_id_type=MESH)


def _ag_first(ins, outs, send_sems, recv_sems, local_sems):
    x, y, c = _place()
    me = (x, y, c)
    targets = [(x, y, 1 - c)] + [(*chip, c) for chip in _other_chips(x, y)]
    local, out, inc = [], [], []
    for a in range(len(ins)):
        local.append(pltpu.make_async_copy(ins[a], outs[a].at[_dev(me)], local_sems.at[a]))
        for k, to in enumerate(targets):
            out.append(_remote(ins[a], outs[a].at[_dev(me)], send_sems, recv_sems, 4 * a + k, to))
            inc.append(_remote(ins[a], outs[a].at[_dev(to)], send_sems, recv_sems, 4 * a + k, to))
    return local, out, inc


def _ag_second(bufs, send_sems, recv_sems):
    x, y, c = _place()
    out, inc = [], []
    for a in range(len(bufs)):
        for j, chip in enumerate(_other_chips(x, y)):
            mine, theirs = bufs[a].at[_dev((*chip, c))], bufs[a].at[_dev((*chip, 1 - c))]
            out.append(_remote(mine, mine, send_sems, recv_sems, 3 * a + j, (x, y, 1 - c)))
            inc.append(_remote(theirs, theirs, send_sems, recv_sems, 3 * a + j, (x, y, 1 - c)))
    return [], out, inc


def _rs_first(ins, outs, send_sems, recv_sems):
    x, y, c = _place()
    out = [_remote(ins[a].at[j, 1 - c], outs[a].at[j], send_sems, recv_sems, N_CHIP * a + j, (x, y, 1 - c))
           for a in range(len(ins)) for j in range(N_CHIP)]
    return [], out, out


def _rs_second(ins, outs, send_sems, recv_sems, local_sems):
    x, y, c = _place()
    my_chip = 2 * x + y
    local, out, inc = [], [], []
    for a in range(len(ins)):
        local.append(pltpu.make_async_copy(ins[a].at[my_chip], outs[a].at[my_chip], local_sems.at[a]))
        for k, (ox, oy) in enumerate(_other_chips(x, y)):
            out.append(_remote(ins[a].at[2 * ox + oy], outs[a].at[my_chip], send_sems, recv_sems, 3 * a + k, (ox, oy, c)))
            inc.append(_remote(ins[a].at[2 * ox + oy], outs[a].at[2 * ox + oy], send_sems, recv_sems, 3 * a + k,
                               (ox, oy, c)))
    return local, out, inc


def _comm_start(exchange):
    local, out, _ = exchange
    for cp in local + out:
        cp.start()


def _comm_wait(exchange):
    local, out, inc = exchange
    for cp in inc:
        cp.wait_recv()
    for cp in out:
        cp.wait_send()
    for cp in local:
        cp.wait()


def _dma_sems(*counts):
    return [pltpu.SemaphoreType.DMA((n,)) for n in counts]


def _pair_sum(grads, recvs):
    n = len(grads)

    def body(c_ref, *refs):
        for a in range(n):
            refs[2 * n + a][...] = (refs[a][...].astype(F32) + refs[n + a][...].astype(F32)).astype(BF16)

    def g_spec(g):
        return pl.BlockSpec((None, None) + g.shape[2:], lambda j, c_ref: (j, c_ref[0], 0, 0))

    def r_spec(r):
        return pl.BlockSpec((None,) + r.shape[1:], lambda j, c_ref: (j, 0, 0))

    return pl.pallas_call(
        body, name="pair_sum",
        grid_spec=pltpu.PrefetchScalarGridSpec(
            num_scalar_prefetch=1, grid=(N_CHIP,),
            in_specs=[g_spec(g) for g in grads] + [r_spec(r) for r in recvs],
            out_specs=[r_spec(r) for r in recvs]),
        out_shape=[jax.ShapeDtypeStruct(r.shape, BF16) for r in recvs],
        compiler_params=_params(("parallel",)),
    )(lax.axis_index("c").reshape(1).astype(jnp.int32), *grads, *recvs)


SMALL_ROWS = 544


def _all_reduce_small(buf, parts=()):
    n = len(parts)

    def body(in_ref, *rest):
        p_in, out_ref, p_out = rest[:n], rest[n], rest[n + 1:2 * n + 1]
        recv, acc, send1, recv1, send2, recv2 = rest[2 * n + 1:2 * n + 7]
        scatter_sems = rest[2 * n + 7:]
        x, y, c = _place()
        me = 4 * x + 2 * y + c
        peers = [(x ^ (r >> 2), y ^ ((r >> 1) & 1), c ^ (r & 1)) for r in range(1, N_DEV)]

        def idx(p):
            return 4 * p[0] + 2 * p[1] + p[2]

        if n:
            _comm_start(_rs_second(p_in, p_out, *scatter_sems))
        first = [pltpu.make_async_remote_copy(
            src_ref=in_ref.at[idx(p)], dst_ref=recv.at[me], send_sem=send1.at[r], recv_sem=recv1.at[r],
            device_id=p, device_id_type=MESH) for r, p in enumerate(peers)]
        for cp in first:
            cp.start()
        recv[me] = in_ref[me]
        for r, p in enumerate(peers):
            pltpu.make_async_remote_copy(
                src_ref=in_ref.at[idx(p)], dst_ref=recv.at[idx(p)], send_sem=send1.at[r], recv_sem=recv1.at[r],
                device_id=p, device_id_type=MESH).wait_recv()
        total = recv[0]
        for k in range(1, N_DEV):
            total = total + recv[k]
        acc[...] = total
        out_ref[me] = total
        second = [pltpu.make_async_remote_copy(
            src_ref=acc, dst_ref=out_ref.at[me], send_sem=send2.at[r], recv_sem=recv2.at[r],
            device_id=p, device_id_type=MESH) for r, p in enumerate(peers)]
        for cp in second:
            cp.start()
        for r, p in enumerate(peers):
            pltpu.make_async_remote_copy(
                src_ref=acc, dst_ref=out_ref.at[idx(p)], send_sem=send2.at[r], recv_sem=recv2.at[r],
                device_id=p, device_id_type=MESH).wait_recv()
        for cp in first + second:
            cp.wait_send()
        if n:
            _comm_wait(_rs_second(p_in, p_out, *scatter_sems))

    vm = pl.BlockSpec(memory_space=pltpu.VMEM)
    res = pl.pallas_call(
        body, name="small_grads_all_reduce",
        in_specs=[vm] + [ANY] * n, out_specs=[vm] + [ANY] * n,
        out_shape=[jax.ShapeDtypeStruct(buf.shape, F32)] + [jax.ShapeDtypeStruct(p.shape, p.dtype) for p in parts],
        scratch_shapes=[pltpu.VMEM(buf.shape, F32), pltpu.VMEM(buf.shape[1:], F32)] + _dma_sems(7, 7, 7, 7)
                       + (_dma_sems(3 * n, 3 * n, n) if n else []),
        compiler_params=pltpu.CompilerParams(has_side_effects=True, vmem_limit_bytes=VMEM_LIMIT),
    )(buf, *parts)
    return res[0], list(res[1:])


def _dilate(a, d):
    if d == 1:
        return a
    S, C = a.shape
    return a.reshape(S // d, d, C).transpose(1, 0, 2).reshape(S, C)


def _undilate(a, d):
    if d == 1:
        return a
    S, C = a.shape
    return a.reshape(d, S // d, C).transpose(1, 0, 2).reshape(S, C)


def _cols(a, cb, n=1):
    return a[:, cb * WIDTH:(cb + n) * WIDTH]


def _to_blocks(g, kind):
    if kind == "rows":
        C = g.shape[1]
        return g.reshape(N_CHIP, 2, -1, C)
    return g.reshape(N_CHIP, 2, 4 * WIDTH, -1)


SMALL = ("norm_g", "gm_ln_g", "gm_ln_b", "gm_ws", "gm_bs", "pool_w", "pool_scale", "mem_norm_g", "final_norm_g")


def _pack_small(tree):
    flat = jnp.concatenate([tree[k].reshape(-1, 128) for k in SMALL], axis=0)
    return jnp.pad(flat, ((0, N_DEV * SMALL_ROWS - flat.shape[0]), (0, 0)))


def _unpack_small(flat, like):
    out, at = {}, 0
    for k in SMALL:
        rows = like[k].size // 128
        out[k] = flat[at:at + rows].reshape(like[k].shape)
        at += rows
    return out


def _make_layer(wt, wkv, wb, wout, norm_g, mem_norm_g, ln_g, ln_b, gm_ws, gm_bs, pool_w, pool_scale):
    tril = jnp.tril(jnp.ones((CHUNK, CHUNK), bool))
    wsm = jnp.where(tril, gm_ws, 0.0).astype(BF16)
    pw = pool_w.astype(BF16)
    bands, bands_t = _band_constants()
    return dict(wt=wt, wkv=wkv, wb=wb, wout=wout, g=norm_g[None], mg=mem_norm_g[None],
                ln_g=ln_g[None],
                ln_b=ln_b[None], wsm=wsm, wsm_t=wsm.transpose(0, 2, 1), pw=pw, pw_t=pw.transpose(0, 2, 1),
                ps=pool_scale[None], bias=jnp.repeat(gm_bs.T, HEAD, axis=1), bands=bands, bands_t=bands_t)


def _layer_fwd(xl, mem0, L, next_shards=()):
    S = xl.shape[0]
    proj, gates, h, half_win = _in_proj(xl, L["g"], L["wt"], next_shards[:1])
    kv, mem_n = _mem_kv(mem0, L["mg"], L["wkv"])
    y4, cdf, win, half_small = _abm_fwd(proj, L["ln_g"], L["ln_b"], L["wsm"], L["bias"], L["pw"], L["ps"], kv,
                                        L["bands"], half_win, next_shards[2:])
    o_g, l_g = [], []
    for gi, d in enumerate(DILATIONS):
        if d == 1:
            o, lse = _attn_fwd(proj, CB_Q0, proj, CB_K, proj, CB_CV, S // CHUNK)
        else:
            o, lse = _attn_fwd_dilated(proj, CB_Q0 + gi, CB_K, CB_CV, d)
        o_g.append(o)
        l_g.append(lse)
    (xn, y4, oc, lse, z), small = _merge_fwd(xl, y4, o_g, l_g, proj, gates, L["wb"], L["wout"], next_shards[1:2],
                                             half_small)
    saved = dict(x=xl, proj=proj, gates=gates, h=h, kv=kv, mem_n=mem_n, y4=y4, cdf=cdf, oc=oc, lse=lse, z=z)
    return xn, saved, win + small


def _layer_bwd(dx, mem0, L, sv, later=(), last=False):
    S = dx.shape[0]
    proj = sv["proj"]
    (dy3, doc, delta, dpb, dgm, dwb, dwout), from_sibling = _merge_bwd(
        dx, sv["y4"], sv["oc"], sv["z"], proj, sv["gates"], L["wb"], L["wout"], later)
    pair = _pair_sum(later, from_sibling) if later else ()
    dpb, dm, dlng, dlnb, dws, dbias, dpw, dps, dkv = _abm_bwd(
        proj, sv["cdf"], dy3, L["ln_g"], L["ln_b"], L["wsm"], L["wsm_t"], L["bias"], L["pw"], L["pw_t"], L["ps"], sv["kv"],
        L["bands"], L["bands_t"], dpb)
    assert DILATIONS[0] == 1
    dqs, dk, dv = [], None, None
    for gi, d in enumerate(DILATIONS[1:], start=1):
        dq, dk, dv = _attn_bwd_dilated(proj, CB_Q0 + gi, CB_K, CB_CV, doc, sv["lse"], delta, d, dk, dv)
        dqs.append(dq)
    dpb = _attn_bwd(proj, CB_Q0, proj, CB_K, proj, CB_CV, doc, sv["lse"], delta, S // CHUNK, dqs, dk, dv, dpb)
    dwkv, dmg = _mem_bwd(mem0, L["mg"], sv["mem_n"], L["wkv"], dkv)
    ready = _blocked_rest(dict(w_mem_kv=dwkv, w_branch=dwb, w_out=dwout))
    dwin_t, parts_rest, own = _dw_in(sv["h"], dpb, dm, dgm, pair[1:], ready if last else ())
    big = [_to_blocks(dwin_t, "rows")] + ready
    dxi, dng, parts, from_sibling = _dh_bwd(dpb, dm, dgm, L["wt"], sv["x"], L["g"], dx, pair[:1],
                                            big[:1] if last else ())
    parts = parts + parts_rest
    small = dict(norm_g=dng[0], gm_ln_g=dlng[0], gm_ln_b=dlnb[0], gm_ws=dws,
                 gm_bs=_bias_reduce(dbias)[:, :N_HEAD].T, pool_w=dpw, pool_scale=dps[0], mem_norm_g=dmg[0])
    return dxi, big, small, parts, own, from_sibling


BIG = ("w_in", "w_mem_kv", "w_branch", "w_out")


def _blocked_rest(big):
    return [_to_blocks(big["w_mem_kv"], "rows"), _to_blocks(big["w_branch"], "branch"), _to_blocks(big["w_out"], "rows")]


def _full_weights(gathered):
    win_t, wkv, wb, wout = gathered
    return win_t.reshape(D_IN, D_MODEL), wkv.reshape(D_MODEL, 2 * WIDTH), wb, wout.reshape(D_MODEL, D_MODEL)


def kernel(x, mem, norm_g, w_in, gm_ln_g, gm_ln_b, gm_ws, gm_bs, pool_w, pool_scale, mem_norm_g, w_mem_kv, w_branch, w_out, final_norm_g, loss_target, m_norm_g, m_w_in, m_gm_ln_g, m_gm_ln_b, m_gm_ws, m_gm_bs, m_pool_w, m_pool_scale, m_mem_norm_g, m_w_mem_kv, m_w_branch, m_w_out, m_final_norm_g, v_norm_g, v_w_in, v_gm_ln_g, v_gm_ln_b, v_gm_ws, v_gm_bs, v_pool_w, v_pool_scale, v_mem_norm_g, v_w_mem_kv, v_w_branch, v_w_out, v_final_norm_g):
    x0 = x[0]
    mem0 = mem[0]
    tgt = loss_target[0]
    S = x0.shape[0]

    shards = [[w_in[l].T.astype(BF16), w_mem_kv[l].astype(BF16), w_branch[l].astype(BF16).reshape(4 * WIDTH, -1),
               w_out[l].astype(BF16)] for l in range(DEPTH)]
    gathered = _all_gather(shards[0])
    layers, saved = [], []
    xl = x0
    for l in range(DEPTH):
        layers.append(_make_layer(*_full_weights(gathered), norm_g[l], mem_norm_g[l], gm_ln_g[l], gm_ln_b[l],
                                  gm_ws[l], gm_bs[l], pool_w[l], pool_scale[l]))
        xl, sv, gathered = _layer_fwd(xl, mem0, layers[l], shards[l + 1] if l + 1 < DEPTH else ())
        saved.append(sv)

    loss_part, dx, d_final = _loss_head(xl, final_norm_g[None], tgt)
    loss = lax.psum(loss_part[0, 0], ("x", "y", "c"))

    small = {k: [None] * DEPTH for k in SMALL if k != "final_norm_g"}
    parts = [None] * DEPTH
    later = ()
    for l in reversed(range(DEPTH)):
        dx, gb, gs, done, own, from_sibling = _layer_bwd(dx, mem0, layers[l], saved[l], later, last=(l == 0))
        if later:
            parts[l + 1] = done
        later = gb
        for k in gs:
            small[k][l] = gs[k]
    grad_x = dx[None]
    small_tree = {k: jnp.stack(small[k]) for k in small}
    small_tree["final_norm_g"] = d_final[0]
    reduced, win_parts = _all_reduce_small(_pack_small(small_tree).reshape(N_DEV, SMALL_ROWS, 128),
                                           _pair_sum(later[:1], from_sibling))
    parts[0] = win_parts + own

    weights = dict(norm_g=norm_g, w_in=w_in, gm_ln_g=gm_ln_g, gm_ln_b=gm_ln_b, gm_ws=gm_ws, gm_bs=gm_bs,
                   pool_w=pool_w, pool_scale=pool_scale, mem_norm_g=mem_norm_g, w_mem_kv=w_mem_kv,
                   w_branch=w_branch, w_out=w_out, final_norm_g=final_norm_g)
    m_in = dict(norm_g=m_norm_g, w_in=m_w_in, gm_ln_g=m_gm_ln_g, gm_ln_b=m_gm_ln_b, gm_ws=m_gm_ws, gm_bs=m_gm_bs,
                pool_w=m_pool_w, pool_scale=m_pool_scale, mem_norm_g=m_mem_norm_g, w_mem_kv=m_w_mem_kv,
                w_branch=m_w_branch, w_out=m_w_out, final_norm_g=m_final_norm_g)
    v_in = dict(norm_g=v_norm_g, w_in=v_w_in, gm_ln_g=v_gm_ln_g, gm_ln_b=v_gm_ln_b, gm_ws=v_gm_ws, gm_bs=v_gm_bs,
                pool_w=v_pool_w, pool_scale=v_pool_scale, mem_norm_g=v_mem_norm_g, w_mem_kv=v_w_mem_kv,
                w_branch=v_w_branch, w_out=v_w_out, final_norm_g=v_final_norm_g)
    res = {}
    def view(k, arr):
        return arr.transpose(0, 2, 1) if k == "w_in" else arr

    for a, k in enumerate(BIG):
        shape = view(k, weights[k]).shape
        by_layer = [parts[l][a] for l in range(DEPTH)]
        lrc = (DEPTH,) + by_layer[0].shape[1:]
        outs = _adamw_layers(by_layer, view(k, weights[k]).reshape(lrc), view(k, m_in[k]).reshape(lrc),
                             view(k, v_in[k]).reshape(lrc), "adamw_" + k)
        res[k] = [view(k, o.reshape(shape)) for o in outs]
    outs = _adamw(reduced.reshape(1, N_DEV * SMALL_ROWS, 128), _pack_small(weights), _pack_small(m_in),
                  _pack_small(v_in), "adamw_small")
    unpacked = [_unpack_small(o, weights) for o in outs]
    for k in SMALL:
        res[k] = [u[k] for u in unpacked]

    order = ("norm_g", "w_in", "gm_ln_g", "gm_ln_b", "gm_ws", "gm_bs", "pool_w", "pool_scale", "mem_norm_g",
             "w_mem_kv", "w_branch", "w_out", "final_norm_g")
    return (loss, grad_x, *[res[k][0] for k in order], *[res[k][1] for k in order],
            *[res[k][2] for k in order], *[res[k][3] for k in order])
```
